```python
import math
import jax, jax.numpy as jnp
from jax import lax
import numpy as np

D_MODEL = 1024
BATCH = 8
SEQ = 8192
DEPTH = 1

CHUNK = 64
A_HEADS = 8
A_DK = 128
A_DV = 128
A_CONV = 4
A_W = A_HEADS * A_DV
B_HEADS = 16
B_DH = 64
B_W = B_HEADS * B_DH
B_PREV_CHUNKS = 8
B_MAX_REL = 256
B_REL_SIZE = CHUNK - 1 + B_MAX_REL + 1
D_FF = 2816
FFN_CONV = 3
N_BRANCHES = 2
IN_SPLITS = (3 * A_W, 4 * A_W, 4 * A_W + A_HEADS, 4 * A_W + 2 * A_HEADS,
             4 * A_W + 2 * A_HEADS + B_W, 4 * A_W + 2 * A_HEADS + 2 * B_W,
             4 * A_W + 2 * A_HEADS + 3 * B_W)
IN_COLS = 4 * A_W + 2 * A_HEADS + 3 * B_W + N_BRANCHES * D_MODEL
DEEPNORM_ALPHA = (2.0 * DEPTH) ** 0.25
DEEPNORM_BETA = (8.0 * DEPTH) ** -0.25
LN_EPS = 1e-5
RMS_EPS = 1e-6
L2_EPS = 1e-6
NEG_INF = -1e30

kernel_name = "hybrid_deltanet_bandattn_convffn_deepnorm_adaln"


def layernorm(x, g, b):
    xf = x.astype(jnp.float32)
    mu = jnp.mean(xf, axis=-1, keepdims=True)
    var = jnp.mean(jnp.square(xf - mu), axis=-1, keepdims=True)
    y = (xf - mu) * lax.rsqrt(var + LN_EPS) * g.astype(jnp.float32) + b.astype(jnp.float32)
    return y.astype(x.dtype)


def causal_dwconv(x, w):
    k_width, ch = w.shape
    return lax.conv_general_dilated(
        x, w[:, None, :].astype(x.dtype), window_strides=(1,), padding=[(k_width - 1, 0)],
        dimension_numbers=("NWC", "WIO", "NWC"), feature_group_count=ch)


def l2norm(x):
    return x * lax.rsqrt(jnp.sum(jnp.square(x), axis=-1, keepdims=True) + L2_EPS)


def chunk_gated_delta_rule(q, k, v, g, beta):
    b_, s_, h_, dk = q.shape
    dv = v.shape[-1]
    n_chunks = s_ // CHUNK

    def to_chunks(t):
        return t.reshape(b_, n_chunks, CHUNK, h_, -1).transpose(0, 1, 3, 2, 4)

    q, k, v = to_chunks(q), to_chunks(k), to_chunks(v)
    g = g.reshape(b_, n_chunks, CHUNK, h_).transpose(0, 1, 3, 2)
    beta = beta.reshape(b_, n_chunks, CHUNK, h_).transpose(0, 1, 3, 2)
    g = jnp.cumsum(g, axis=-1)

    causal = jnp.tril(jnp.ones((CHUNK, CHUNK), dtype=bool))
    strict = jnp.tril(jnp.ones((CHUNK, CHUNK), dtype=bool), k=-1)
    diff = g[..., :, None] - g[..., None, :]
    decay = jnp.where(causal, jnp.exp(jnp.where(causal, diff, 0.0)), 0.0)

    k_beta = k * beta[..., None]
    v_beta = v * beta[..., None]
    a_low = jnp.where(strict, jnp.einsum("bnhid,bnhjd->bnhij", k_beta, k) * decay, 0.0)
    eye = jnp.eye(CHUNK, dtype=jnp.float32)
    rhs = jnp.concatenate([v_beta, k_beta * jnp.exp(g)[..., None]], axis=-1)
    sol = lax.linalg.triangular_solve(a_low + eye, rhs, left_side=True, lower=True,
                                      unit_diagonal=True)
    u = sol[..., :dv]
    w = sol[..., dv:]
    qk = jnp.where(causal, jnp.einsum("bnhid,bnhjd->bnhij", q, k) * decay, 0.0)

    def step(state, inp):
        q_n, k_n, u_n, w_n, qk_n, g_n = inp
        v_new = u_n - jnp.einsum("bhck,bhkv->bhcv", w_n, state)
        o_n = (jnp.einsum("bhck,bhkv->bhcv", q_n * jnp.exp(g_n)[..., None], state)
               + jnp.einsum("bhij,bhjv->bhiv", qk_n, v_new))
        g_last = g_n[..., -1]
        k_dec = k_n * jnp.exp(g_last[..., None] - g_n)[..., None]
        state = state * jnp.exp(g_last)[..., None, None] + jnp.einsum("bhck,bhcv->bhkv", k_dec, v_new)
        return state, o_n

    xs = tuple(jnp.moveaxis(t, 1, 0) for t in (q, k, u, w, qk, g))
    state0 = jnp.zeros((b_, h_, dk, dv), jnp.float32)
    _, o = lax.scan(step, state0, xs)
    return o.transpose(1, 0, 3, 2, 4).reshape(b_, s_, h_, dv)


def gated_deltanet(qkv, z, beta_raw, a_raw, conv_w, a_log, dt_bias, norm_w):
    b_, s_, _ = qkv.shape
    qkv = jax.nn.silu(causal_dwconv(qkv, conv_w))
    q, k, v = jnp.split(qkv.astype(jnp.float32), 3, axis=-1)
    q = l2norm(q.reshape(b_, s_, A_HEADS, A_DK)) * (A_DK ** -0.5)
    k = l2norm(k.reshape(b_, s_, A_HEADS, A_DK))
    v = v.reshape(b_, s_, A_HEADS, A_DV)
    beta = jax.nn.sigmoid(beta_raw.astype(jnp.float32))
    g = -jnp.exp(a_log.astype(jnp.float32)) * jax.nn.softplus(
        a_raw.astype(jnp.float32) + dt_bias.astype(jnp.float32))
    o = chunk_gated_delta_rule(q, k, v, g, beta)
    o = o * lax.rsqrt(jnp.mean(jnp.square(o), axis=-1, keepdims=True) + RMS_EPS)
    o = o * norm_w.astype(jnp.float32) * jax.nn.silu(z.astype(jnp.float32).reshape(b_, s_, A_HEADS, A_DV))
    return o.reshape(b_, s_, A_W).astype(qkv.dtype)


def chunk_band_attention(q, k, v, rel_bias):
    b_, s_, h_, dh = q.shape
    n_chunks = s_ // CHUNK
    pad = B_PREV_CHUNKS * CHUNK
    band = (B_PREV_CHUNKS + 1) * CHUNK
    k_pad = jnp.pad(k, ((0, 0), (pad, 0), (0, 0), (0, 0)))
    v_pad = jnp.pad(v, ((0, 0), (pad, 0), (0, 0), (0, 0)))
    qi = np.arange(CHUNK)[:, None]
    kj = np.arange(band)[None, :]
    dist = pad + qi - kj
    idx = np.clip(dist, -(CHUNK - 1), B_MAX_REL) + (CHUNK - 1)
    bias = rel_bias.astype(jnp.float32)[:, idx]
    key_chunk = jnp.arange(band) // CHUNK
    scale = dh ** -0.5

    def one_chunk(n):
        q_n = lax.dynamic_slice_in_dim(q, n * CHUNK, CHUNK, axis=1)
        k_n = lax.dynamic_slice_in_dim(k_pad, n * CHUNK, band, axis=1)
        v_n = lax.dynamic_slice_in_dim(v_pad, n * CHUNK, band, axis=1)
        valid = (n - B_PREV_CHUNKS + key_chunk) >= 0
        s = jnp.einsum("bqhd,bkhd->bhqk", q_n, k_n).astype(jnp.float32) * scale + bias
        s = jnp.where(valid, s, NEG_INF)
        p = jax.nn.softmax(s, axis=-1).astype(v.dtype)
        return jnp.einsum("bhqk,bkhd->bqhd", p, v_n)

    out = lax.map(one_chunk, jnp.arange(n_chunks))
    return out.transpose(1, 0, 2, 3, 4).reshape(b_, s_, h_ * dh)


def _fwd_setup_inputs(seed: int = 0) -> dict:
    key = jax.random.key(seed)
    ks = jax.random.split(key, 24)

    def nrm(k, shape, scale):
        return jax.random.normal(k, shape, jnp.float32) * scale

    L = DEPTH
    x = nrm(ks[0], (BATCH, SEQ, D_MODEL), 1.0)
    c = nrm(ks[1], (BATCH, D_MODEL), 1.0)
    w_ada = nrm(ks[2], (L, D_MODEL, 6 * D_MODEL), D_MODEL ** -0.5)
    b_ada = nrm(ks[3], (L, 6 * D_MODEL), 0.02)
    w_in = nrm(ks[4], (L, D_MODEL, IN_COLS), D_MODEL ** -0.5)
    b_gate = nrm(ks[5], (L, N_BRANCHES * D_MODEL), 0.1)
    conv_a = nrm(ks[6], (L, A_CONV, 3 * A_W), A_CONV ** -0.5)
    a_log = jnp.log(jax.random.uniform(ks[7], (L, A_HEADS), jnp.float32, minval=1.0, maxval=16.0))
    dt = jnp.exp(jax.random.uniform(ks[8], (L, A_HEADS), jnp.float32,
                                    minval=math.log(1e-3), maxval=math.log(1e-1)))
    dt_bias = dt + jnp.log(-jnp.expm1(-dt))
    norm_a = 1.0 + nrm(ks[9], (L, A_DV), 0.05)
    rel_bias = nrm(ks[10], (L, B_HEADS, B_REL_SIZE), 0.2)
    w_branch_a = nrm(ks[11], (L, A_W, D_MODEL), A_W ** -0.5)
    w_branch_b = nrm(ks[12], (L, B_W, D_MODEL), B_W ** -0.5)
    w_o = nrm(ks[13], (L, D_MODEL, D_MODEL), DEEPNORM_BETA * D_MODEL ** -0.5)
    ln1_g = 1.0 + nrm(ks[14], (L, D_MODEL), 0.05)
    ln1_b = nrm(ks[15], (L, D_MODEL), 0.02)
    w_up = nrm(ks[16], (L, D_MODEL, 2 * D_FF), D_MODEL ** -0.5)
    conv_ffn = nrm(ks[17], (L, FFN_CONV, 2 * D_FF), FFN_CONV ** -0.5)
    b_conv_ffn = nrm(ks[18], (L, 2 * D_FF), 0.02)
    w_down = nrm(ks[19], (L, D_FF, D_MODEL), DEEPNORM_BETA * D_FF ** -0.5)
    ln2_g = 1.0 + nrm(ks[20], (L, D_MODEL), 0.05)
    ln2_b = nrm(ks[21], (L, D_MODEL), 0.02)
    return {"x": x, "c": c, "w_ada": w_ada, "b_ada": b_ada, "w_in": w_in, "b_gate": b_gate,
            "conv_a": conv_a, "a_log": a_log, "dt_bias": dt_bias, "norm_a": norm_a,
            "rel_bias": rel_bias, "w_branch_a": w_branch_a, "w_branch_b": w_branch_b, "w_o": w_o,
            "ln1_g": ln1_g, "ln1_b": ln1_b, "w_up": w_up, "conv_ffn": conv_ffn,
            "b_conv_ffn": b_conv_ffn, "w_down": w_down, "ln2_g": ln2_g, "ln2_b": ln2_b}


def _fwd_reference(x, c, w_ada, b_ada, w_in, b_gate, conv_a, a_log, dt_bias, norm_a, rel_bias,
              w_branch_a, w_branch_b, w_o, ln1_g, ln1_b, w_up, conv_ffn, b_conv_ffn, w_down,
              ln2_g, ln2_b):
    b_, s_, d_ = x.shape
    c_act = jax.nn.silu(c)
    for l in range(DEPTH):
        mod = (c_act @ w_ada[l] + b_ada[l])[:, None, :]
        shift_t, scale_t, gate_t, shift_f, scale_f, gate_f = jnp.split(mod, 6, axis=-1)

        h = x * (1.0 + scale_t) + shift_t
        proj = h @ w_in[l]
        qkv_a, z_a, beta_raw, a_raw, q_b, k_b, v_b, gates = jnp.split(proj, IN_SPLITS, axis=-1)

        o_a = gated_deltanet(qkv_a, z_a, beta_raw, a_raw, conv_a[l], a_log[l], dt_bias[l], norm_a[l])
        o_b = chunk_band_attention(q_b.reshape(b_, s_, B_HEADS, B_DH),
                                   k_b.reshape(b_, s_, B_HEADS, B_DH),
                                   v_b.reshape(b_, s_, B_HEADS, B_DH), rel_bias[l])

        gate_a, gate_b = jnp.split(jax.nn.sigmoid(gates + b_gate[l]), N_BRANCHES, axis=-1)
        merged = gate_a * (o_a @ w_branch_a[l]) + gate_b * (o_b @ w_branch_b[l])
        mix = merged @ w_o[l]
        x = layernorm(DEEPNORM_ALPHA * x + gate_t * mix, ln1_g[l], ln1_b[l])

        h = x * (1.0 + scale_f) + shift_f
        u = causal_dwconv(h @ w_up[l], conv_ffn[l]) + b_conv_ffn[l]
        u_gate, u_val = jnp.split(u, 2, axis=-1)
        ffn = (jax.nn.silu(u_gate) * u_val) @ w_down[l]
        x = layernorm(DEEPNORM_ALPHA * x + gate_f * ffn, ln2_g[l], ln2_b[l])
    return x


import jax as _jax
import jax.numpy as _jnp

TWIN_FORMAT = 'train_step'
FWD_PARAMS = ['x', 'c', 'w_ada', 'b_ada', 'w_in', 'b_gate', 'conv_a', 'a_log', 'dt_bias', 'norm_a', 'rel_bias', 'w_branch_a', 'w_branch_b', 'w_o', 'ln1_g', 'ln1_b', 'w_up', 'conv_ffn', 'b_conv_ffn', 'w_down', 'ln2_g', 'ln2_b']
TWIN_WEIGHTS = ['w_ada', 'b_ada', 'w_in', 'b_gate', 'conv_a', 'a_log', 'dt_bias', 'norm_a', 'rel_bias', 'w_branch_a', 'w_branch_b', 'w_o', 'ln1_g', 'ln1_b', 'w_up', 'conv_ffn', 'b_conv_ffn', 'w_down', 'ln2_g', 'ln2_b']
TWIN_DIFF_INPUT = 'x'
TWIN_INPUTS = ['x', 'c', 'w_ada', 'b_ada', 'w_in', 'b_gate', 'conv_a', 'a_log', 'dt_bias', 'norm_a', 'rel_bias', 'w_branch_a', 'w_branch_b', 'w_o', 'ln1_g', 'ln1_b', 'w_up', 'conv_ffn', 'b_conv_ffn', 'w_down', 'ln2_g', 'ln2_b', 'loss_target', 'm_w_ada', 'm_b_ada', 'm_w_in', 'm_b_gate', 'm_conv_a', 'm_a_log', 'm_dt_bias', 'm_norm_a', 'm_rel_bias', 'm_w_branch_a', 'm_w_branch_b', 'm_w_o', 'm_ln1_g', 'm_ln1_b', 'm_w_up', 'm_conv_ffn', 'm_b_conv_ffn', 'm_w_down', 'm_ln2_g', 'm_ln2_b', 'v_w_ada', 'v_b_ada', 'v_w_in', 'v_b_gate', 'v_conv_a', 'v_a_log', 'v_dt_bias', 'v_norm_a', 'v_rel_bias', 'v_w_branch_a', 'v_w_branch_b', 'v_w_o', 'v_ln1_g', 'v_ln1_b', 'v_w_up', 'v_conv_ffn', 'v_b_conv_ffn', 'v_w_down', 'v_ln2_g', 'v_ln2_b']
TWIN_OUTPUTS = ['loss', 'grad_x', 'grad_w_ada', 'grad_b_ada', 'grad_w_in', 'grad_b_gate', 'grad_conv_a', 'grad_a_log', 'grad_dt_bias', 'grad_norm_a', 'grad_rel_bias', 'grad_w_branch_a', 'grad_w_branch_b', 'grad_w_o', 'grad_ln1_g', 'grad_ln1_b', 'grad_w_up', 'grad_conv_ffn', 'grad_b_conv_ffn', 'grad_w_down', 'grad_ln2_g', 'grad_ln2_b', 'delta_w_ada', 'delta_b_ada', 'delta_w_in', 'delta_b_gate', 'delta_conv_a', 'delta_a_log', 'delta_dt_bias', 'delta_norm_a', 'delta_rel_bias', 'delta_w_branch_a', 'delta_w_branch_b', 'delta_w_o', 'delta_ln1_g', 'delta_ln1_b', 'delta_w_up', 'delta_conv_ffn', 'delta_b_conv_ffn', 'delta_w_down', 'delta_ln2_g', 'delta_ln2_b', 'new_m_w_ada', 'new_m_b_ada', 'new_m_w_in', 'new_m_b_gate', 'new_m_conv_a', 'new_m_a_log', 'new_m_dt_bias', 'new_m_norm_a', 'new_m_rel_bias', 'new_m_w_branch_a', 'new_m_w_branch_b', 'new_m_w_o', 'new_m_ln1_g', 'new_m_ln1_b', 'new_m_w_up', 'new_m_conv_ffn', 'new_m_b_conv_ffn', 'new_m_w_down', 'new_m_ln2_g', 'new_m_ln2_b', 'new_v_w_ada', 'new_v_b_ada', 'new_v_w_in', 'new_v_b_gate', 'new_v_conv_a', 'new_v_a_log', 'new_v_dt_bias', 'new_v_norm_a', 'new_v_rel_bias', 'new_v_w_branch_a', 'new_v_w_branch_b', 'new_v_w_o', 'new_v_ln1_g', 'new_v_ln1_b', 'new_v_w_up', 'new_v_conv_ffn', 'new_v_b_conv_ffn', 'new_v_w_down', 'new_v_ln2_g', 'new_v_ln2_b']
TWIN_LEAF_KINDS = {'loss': 'loss', 'grad_x': 'grad_x', 'grad_w_ada': 'grad_w', 'grad_b_ada': 'grad_w', 'grad_w_in': 'grad_w', 'grad_b_gate': 'grad_w', 'grad_conv_a': 'grad_w', 'grad_a_log': 'grad_w', 'grad_dt_bias': 'grad_w', 'grad_norm_a': 'grad_w', 'grad_rel_bias': 'grad_w', 'grad_w_branch_a': 'grad_w', 'grad_w_branch_b': 'grad_w', 'grad_w_o': 'grad_w', 'grad_ln1_g': 'grad_w', 'grad_ln1_b': 'grad_w', 'grad_w_up': 'grad_w', 'grad_conv_ffn': 'grad_w', 'grad_b_conv_ffn': 'grad_w', 'grad_w_down': 'grad_w', 'grad_ln2_g': 'grad_w', 'grad_ln2_b': 'grad_w', 'delta_w_ada': 'delta_w', 'delta_b_ada': 'delta_w', 'delta_w_in': 'delta_w', 'delta_b_gate': 'delta_w', 'delta_conv_a': 'delta_w', 'delta_a_log': 'delta_w', 'delta_dt_bias': 'delta_w', 'delta_norm_a': 'delta_w', 'delta_rel_bias': 'delta_w', 'delta_w_branch_a': 'delta_w', 'delta_w_branch_b': 'delta_w', 'delta_w_o': 'delta_w', 'delta_ln1_g': 'delta_w', 'delta_ln1_b': 'delta_w', 'delta_w_up': 'delta_w', 'delta_conv_ffn': 'delta_w', 'delta_b_conv_ffn': 'delta_w', 'delta_w_down': 'delta_w', 'delta_ln2_g': 'delta_w', 'delta_ln2_b': 'delta_w', 'new_m_w_ada': 'new_m', 'new_m_b_ada': 'new_m', 'new_m_w_in': 'new_m', 'new_m_b_gate': 'new_m', 'new_m_conv_a': 'new_m', 'new_m_a_log': 'new_m', 'new_m_dt_bias': 'new_m', 'new_m_norm_a': 'new_m', 'new_m_rel_bias': 'new_m', 'new_m_w_branch_a': 'new_m', 'new_m_w_branch_b': 'new_m', 'new_m_w_o': 'new_m', 'new_m_ln1_g': 'new_m', 'new_m_ln1_b': 'new_m', 'new_m_w_up': 'new_m', 'new_m_conv_ffn': 'new_m', 'new_m_b_conv_ffn': 'new_m', 'new_m_w_down': 'new_m', 'new_m_ln2_g': 'new_m', 'new_m_ln2_b': 'new_m', 'new_v_w_ada': 'new_v', 'new_v_b_ada': 'new_v', 'new_v_w_in': 'new_v', 'new_v_b_gate': 'new_v', 'new_v_conv_a': 'new_v', 'new_v_a_log': 'new_v', 'new_v_dt_bias': 'new_v', 'new_v_norm_a': 'new_v', 'new_v_rel_bias': 'new_v', 'new_v_w_branch_a': 'new_v', 'new_v_w_branch_b': 'new_v', 'new_v_w_o': 'new_v', 'new_v_ln1_g': 'new_v', 'new_v_ln1_b': 'new_v', 'new_v_w_up': 'new_v', 'new_v_conv_ffn': 'new_v', 'new_v_b_conv_ffn': 'new_v', 'new_v_w_down': 'new_v', 'new_v_ln2_g': 'new_v', 'new_v_ln2_b': 'new_v'}


def _forward(args):
    return _fwd_reference(*[args[k] for k in FWD_PARAMS])


def _output_shape():
    def fwd():
        inp = _fwd_setup_inputs(0)
        return _fwd_reference(*[inp[k] for k in FWD_PARAMS])
    out = _jax.eval_shape(fwd)
    return out.shape, out.dtype

N_MICROBATCH = 1
ADAM_LR = 0.001
ADAM_B1 = 0.9
ADAM_B2 = 0.999
ADAM_EPS = 1e-08
ADAM_WD = 0.01
ADAM_STEP = 10
PER_EXAMPLE_BATCH_AXIS = {'x': 0, 'c': 0, 'loss_target': 0}
SHARED_INPUTS = []
_WEIGHT_DTYPES = {'w_ada': _jnp.float32, 'b_ada': _jnp.float32, 'w_in': _jnp.float32, 'b_gate': _jnp.float32, 'conv_a': _jnp.float32, 'a_log': _jnp.float32, 'dt_bias': _jnp.float32, 'norm_a': _jnp.float32, 'rel_bias': _jnp.float32, 'w_branch_a': _jnp.float32, 'w_branch_b': _jnp.float32, 'w_o': _jnp.float32, 'ln1_g': _jnp.float32, 'ln1_b': _jnp.float32, 'w_up': _jnp.float32, 'conv_ffn': _jnp.float32, 'b_conv_ffn': _jnp.float32, 'w_down': _jnp.float32, 'ln2_g': _jnp.float32, 'ln2_b': _jnp.float32}
MOMENT_SCALE = {'w_ada': 8.334643e-02, 'b_ada': 1.668746e-01, 'w_in': 2.551565e-02, 'b_gate': 1.596083e-02, 'conv_a': 2.714080e-02, 'a_log': 2.425893e-01, 'dt_bias': 2.136369e-01, 'norm_a': 1.113546e-01, 'rel_bias': 4.680259e-03, 'w_branch_a': 3.988187e-02, 'w_branch_b': 3.958062e-02, 'w_o': 9.524529e-02, 'ln1_g': 5.382071e+00, 'ln1_b': 1.027184e+00, 'w_up': 5.291975e-02, 'conv_ffn': 5.510150e-02, 'b_conv_ffn': 4.711894e-02, 'w_down': 1.469136e-01, 'ln2_g': 6.454013e+01, 'ln2_b': 3.263591e+00}


def _to_microbatches(a, axis):
    t = _jnp.moveaxis(a, axis, 0)
    t = t.reshape((N_MICROBATCH, t.shape[0] // N_MICROBATCH) + t.shape[1:])
    return _jnp.moveaxis(t, 1, axis + 1)


def setup_inputs(seed: int = 0) -> dict:
    inp = _fwd_setup_inputs(seed)
    key = _jax.random.fold_in(_jax.random.key(seed), 7919)
    shape, _ = _output_shape()
    out = dict(inp)
    out["loss_target"] = _jax.random.normal(_jax.random.fold_in(key, 0), shape, _jnp.float32)
    for i, name in enumerate(TWIN_WEIGHTS):
        w = inp[name].astype(_jnp.float32)
        if MOMENT_SCALE is None:
            s = _jnp.sqrt(_jnp.mean(_jnp.square(w)) + 1e-30)
        else:
            s = MOMENT_SCALE[name]
        km, kv = _jax.random.split(_jax.random.fold_in(key, i + 1))
        out[name] = w
        out["m_" + name] = s * _jax.random.normal(km, w.shape, _jnp.float32)
        out["v_" + name] = (s * s) * _jax.random.uniform(kv, w.shape, _jnp.float32, 0.5, 1.5)
    if N_MICROBATCH > 1:
        for name, axis in PER_EXAMPLE_BATCH_AXIS.items():
            out[name] = _to_microbatches(out[name], axis)
    return {'x': out['x'], 'c': out['c'], 'w_ada': out['w_ada'], 'b_ada': out['b_ada'], 'w_in': out['w_in'], 'b_gate': out['b_gate'], 'conv_a': out['conv_a'], 'a_log': out['a_log'], 'dt_bias': out['dt_bias'], 'norm_a': out['norm_a'], 'rel_bias': out['rel_bias'], 'w_branch_a': out['w_branch_a'], 'w_branch_b': out['w_branch_b'], 'w_o': out['w_o'], 'ln1_g': out['ln1_g'], 'ln1_b': out['ln1_b'], 'w_up': out['w_up'], 'conv_ffn': out['conv_ffn'], 'b_conv_ffn': out['b_conv_ffn'], 'w_down': out['w_down'], 'ln2_g': out['ln2_g'], 'ln2_b': out['ln2_b'], 'loss_target': out['loss_target'], 'm_w_ada': out['m_w_ada'], 'm_b_ada': out['m_b_ada'], 'm_w_in': out['m_w_in'], 'm_b_gate': out['m_b_gate'], 'm_conv_a': out['m_conv_a'], 'm_a_log': out['m_a_log'], 'm_dt_bias': out['m_dt_bias'], 'm_norm_a': out['m_norm_a'], 'm_rel_bias': out['m_rel_bias'], 'm_w_branch_a': out['m_w_branch_a'], 'm_w_branch_b': out['m_w_branch_b'], 'm_w_o': out['m_w_o'], 'm_ln1_g': out['m_ln1_g'], 'm_ln1_b': out['m_ln1_b'], 'm_w_up': out['m_w_up'], 'm_conv_ffn': out['m_conv_ffn'], 'm_b_conv_ffn': out['m_b_conv_ffn'], 'm_w_down': out['m_w_down'], 'm_ln2_g': out['m_ln2_g'], 'm_ln2_b': out['m_ln2_b'], 'v_w_ada': out['v_w_ada'], 'v_b_ada': out['v_b_ada'], 'v_w_in': out['v_w_in'], 'v_b_gate': out['v_b_gate'], 'v_conv_a': out['v_conv_a'], 'v_a_log': out['v_a_log'], 'v_dt_bias': out['v_dt_bias'], 'v_norm_a': out['v_norm_a'], 'v_rel_bias': out['v_rel_bias'], 'v_w_branch_a': out['v_w_branch_a'], 'v_w_branch_b': out['v_w_branch_b'], 'v_w_o': out['v_w_o'], 'v_ln1_g': out['v_ln1_g'], 'v_ln1_b': out['v_ln1_b'], 'v_w_up': out['v_w_up'], 'v_conv_ffn': out['v_conv_ffn'], 'v_b_conv_ffn': out['v_b_conv_ffn'], 'v_w_down': out['v_w_down'], 'v_ln2_g': out['v_ln2_g'], 'v_ln2_b': out['v_ln2_b']}


def _loss(weights, diff, rest, loss_target):
    with _jax.named_scope("forward"):
        args = {**rest, TWIN_DIFF_INPUT: diff, **{k: w.astype(_WEIGHT_DTYPES[k]) for k, w in weights.items()}}
        y = _forward(args)
    with _jax.named_scope("loss_head"):
        err = _jnp.square(y.astype(_jnp.float32) - loss_target)
        return 0.5 * _jnp.sum(_jnp.mean(err, axis=-1)) if err.ndim else 0.5 * err


def _adamw(w, g, m, v):
    m = ADAM_B1 * m + (1.0 - ADAM_B1) * g
    v = ADAM_B2 * v + (1.0 - ADAM_B2) * _jnp.square(g)
    m_hat = m / (1.0 - ADAM_B1 ** ADAM_STEP)
    v_hat = v / (1.0 - ADAM_B2 ** ADAM_STEP)
    delta = -ADAM_LR * (m_hat / (_jnp.sqrt(v_hat) + ADAM_EPS) + ADAM_WD * w)
    return delta, m, v


def reference(x, c, w_ada, b_ada, w_in, b_gate, conv_a, a_log, dt_bias, norm_a, rel_bias, w_branch_a, w_branch_b, w_o, ln1_g, ln1_b, w_up, conv_ffn, b_conv_ffn, w_down, ln2_g, ln2_b, loss_target, m_w_ada, m_b_ada, m_w_in, m_b_gate, m_conv_a, m_a_log, m_dt_bias, m_norm_a, m_rel_bias, m_w_branch_a, m_w_branch_b, m_w_o, m_ln1_g, m_ln1_b, m_w_up, m_conv_ffn, m_b_conv_ffn, m_w_down, m_ln2_g, m_ln2_b, v_w_ada, v_b_ada, v_w_in, v_b_gate, v_conv_a, v_a_log, v_dt_bias, v_norm_a, v_rel_bias, v_w_branch_a, v_w_branch_b, v_w_o, v_ln1_g, v_ln1_b, v_w_up, v_conv_ffn, v_b_conv_ffn, v_w_down, v_ln2_g, v_ln2_b):
    given = dict(x=x, c=c, w_ada=w_ada, b_ada=b_ada, w_in=w_in, b_gate=b_gate, conv_a=conv_a, a_log=a_log, dt_bias=dt_bias, norm_a=norm_a, rel_bias=rel_bias, w_branch_a=w_branch_a, w_branch_b=w_branch_b, w_o=w_o, ln1_g=ln1_g, ln1_b=ln1_b, w_up=w_up, conv_ffn=conv_ffn, b_conv_ffn=b_conv_ffn, w_down=w_down, ln2_g=ln2_g, ln2_b=ln2_b, loss_target=loss_target, m_w_ada=m_w_ada, m_b_ada=m_b_ada, m_w_in=m_w_in, m_b_gate=m_b_gate, m_conv_a=m_conv_a, m_a_log=m_a_log, m_dt_bias=m_dt_bias, m_norm_a=m_norm_a, m_rel_bias=m_rel_bias, m_w_branch_a=m_w_branch_a, m_w_branch_b=m_w_branch_b, m_w_o=m_w_o, m_ln1_g=m_ln1_g, m_ln1_b=m_ln1_b, m_w_up=m_w_up, m_conv_ffn=m_conv_ffn, m_b_conv_ffn=m_b_conv_ffn, m_w_down=m_w_down, m_ln2_g=m_ln2_g, m_ln2_b=m_ln2_b, v_w_ada=v_w_ada, v_b_ada=v_b_ada, v_w_in=v_w_in, v_b_gate=v_b_gate, v_conv_a=v_conv_a, v_a_log=v_a_log, v_dt_bias=v_dt_bias, v_norm_a=v_norm_a, v_rel_bias=v_rel_bias, v_w_branch_a=v_w_branch_a, v_w_branch_b=v_w_branch_b, v_w_o=v_w_o, v_ln1_g=v_ln1_g, v_ln1_b=v_ln1_b, v_w_up=v_w_up, v_conv_ffn=v_conv_ffn, v_b_conv_ffn=v_b_conv_ffn, v_w_down=v_w_down, v_ln2_g=v_ln2_g, v_ln2_b=v_ln2_b)
    weights = {n: given[n] for n in TWIN_WEIGHTS}
    shared = {n: given[n] for n in SHARED_INPUTS}
    per_example = {n: given[n] for n in ['x', 'c']}
    grad_fn = _jax.value_and_grad(_loss, argnums=(0, 1))

    def one_microbatch(ex, loss_target):
        ex = dict(ex)
        diff = ex.pop(TWIN_DIFF_INPUT)
        return grad_fn(weights, diff, {**shared, **ex}, loss_target)

    if N_MICROBATCH == 1:
        loss, (grad_w, grad_x) = one_microbatch(per_example, given["loss_target"])
    else:
        def body(carry, xs):
            loss_sum, grad_sum = carry
            l_k, (gw_k, gx_k) = one_microbatch(xs[0], xs[1])
            with _jax.named_scope("update"):
                return (loss_sum + l_k, _jax.tree.map(_jnp.add, grad_sum, gw_k)), gx_k

        init = (_jnp.zeros((), _jnp.float32), _jax.tree.map(_jnp.zeros_like, weights))
        (loss, grad_w), grad_x = _jax.lax.scan(body, init, (per_example, given["loss_target"]))
    with _jax.named_scope("update"):
        delta_w, new_m, new_v = {}, {}, {}
        for n in TWIN_WEIGHTS:
            delta_w[n], new_m[n], new_v[n] = _adamw(weights[n], grad_w[n], given["m_" + n], given["v_" + n])
    return (loss, grad_x, *[grad_w[n] for n in TWIN_WEIGHTS], *[delta_w[n] for n in TWIN_WEIGHTS],
            *[new_m[n] for n in TWIN_WEIGHTS], *[new_v[n] for n in TWIN_WEIGHTS])
```

```python
import functools
import math

import jax
import jax.numpy as jnp
from jax import lax
from jax.experimental import pallas as pl
from jax.experimental.pallas import tpu as pltpu

F32 = jnp.float32
BF16 = jnp.bfloat16

D_MODEL = 1024
CHUNK = 64
A_HEADS = 8
A_DK = 128
A_W = A_HEADS * A_DK
A_CONV = 4
B_HEADS = 16
B_DH = 64
B_W = B_HEADS * B_DH
B_PREV = 8
B_BAND = (B_PREV + 1) * CHUNK
B_MAX_REL = 256
B_REL = CHUNK - 1 + B_MAX_REL + 1
D_FF = 2816
FFN_CONV = 3
IN_COLS = 4 * A_W + 2 * A_HEADS + 3 * B_W + 2 * D_MODEL
ALPHA = 2.0 ** 0.25
LN_EPS = 1e-5
RMS_EPS = 1e-6
L2_EPS = 1e-6
NEG_INF = -1e30
ADAM_LR, ADAM_B1, ADAM_B2, ADAM_EPS, ADAM_WD, ADAM_STEP = 0.001, 0.9, 0.999, 1e-08, 0.01, 10
N_CHIPS = 4
N_DEV = 8
VMEM_LIMIT = 56 * 1024 * 1024


def _cparams(sem=None):
    return pltpu.CompilerParams(dimension_semantics=sem, vmem_limit_bytes=VMEM_LIMIT)


_DIMS = {"nn": (((1,), (0,)), ((), ())), "nt": (((1,), (1,)), ((), ())), "tn": (((0,), (0,)), ((), ()))}


def mm(a, b, *, mode, out_dtype, name, tm=1024, tn=512, tk=1024, acc_in=None):
    if mode == "nn":
        (M, K), (K2, N) = a.shape, b.shape
    elif mode == "nt":
        (M, K), (N, K2) = a.shape, b.shape
    else:
        (K, M), (K2, N) = a.shape, b.shape
    assert K == K2, (a.shape, b.shape, mode)
    tm, tn, tk = min(tm, M), min(tn, N), min(tk, K)
    assert M % tm == 0 and N % tn == 0 and K % tk == 0, (M, N, K, tm, tn, tk)
    nk = K // tk

    def body(*refs):
        if acc_in is None:
            a_ref, b_ref, o_ref, acc_ref = refs
        else:
            a_ref, b_ref, c_ref, o_ref, acc_ref = refs
        k = pl.program_id(2)

        @pl.when(k == 0)
        def _():
            if acc_in is None:
                acc_ref[...] = jnp.zeros_like(acc_ref)
            else:
                acc_ref[...] = c_ref[...]

        acc_ref[...] += lax.dot_general(a_ref[...].astype(BF16), b_ref[...].astype(BF16), _DIMS[mode],
                                        preferred_element_type=F32)

        @pl.when(k == nk - 1)
        def _():
            o_ref[...] = acc_ref[...].astype(out_dtype)

    a_spec = pl.BlockSpec((tk, tm), lambda i, j, k: (k, i)) if mode == "tn" else pl.BlockSpec((tm, tk), lambda i, j, k: (i, k))
    b_spec = pl.BlockSpec((tn, tk), lambda i, j, k: (j, k)) if mode == "nt" else pl.BlockSpec((tk, tn), lambda i, j, k: (k, j))
    o_spec = pl.BlockSpec((tm, tn), lambda i, j, k: (i, j))
    ins, in_specs, aliases = [a, b], [a_spec, b_spec], {}
    if acc_in is not None:
        assert acc_in.shape == (M, N) and acc_in.dtype == F32 and out_dtype == F32
        ins.append(acc_in)
        in_specs.append(o_spec)
        aliases = {2: 0}
    return pl.pallas_call(
        body, name=name, grid=(M // tm, N // tn, nk), in_specs=in_specs, out_specs=o_spec,
        out_shape=jax.ShapeDtypeStruct((M, N), out_dtype), scratch_shapes=[pltpu.VMEM((tm, tn), F32)],
        input_output_aliases=aliases, compiler_params=_cparams(("parallel", "parallel", "arbitrary")),
    )(*ins)


def rowcall(body, *, name, S, ts, ins, outs):
    assert S % ts == 0 and ts % 8 == 0
    nsteps, r8 = S // ts, ts // 8
    last8 = S // 8 - 1
    in_specs, arrays = [], []
    for arr, kind in ins:
        arrays.append(arr)
        if kind == "row":
            in_specs.append(pl.BlockSpec((ts, arr.shape[1]), lambda i: (i, 0)))
        elif kind == "prev":
            in_specs.append(pl.BlockSpec((8, arr.shape[1]), lambda i: (jnp.maximum(i * r8 - 1, 0), 0)))
        elif kind == "next":
            in_specs.append(pl.BlockSpec((8, arr.shape[1]), lambda i: (jnp.minimum((i + 1) * r8, last8), 0)))
        else:
            nd = arr.ndim
            in_specs.append(pl.BlockSpec(arr.shape, lambda i, nd=nd: (0,) * nd))
    out_specs, out_shapes, acc_idx = [], [], []
    for n, (shape, dtype, kind) in enumerate(outs):
        out_shapes.append(jax.ShapeDtypeStruct(shape, dtype))
        if kind == "row":
            out_specs.append(pl.BlockSpec((ts, shape[1]), lambda i: (i, 0)))
        else:
            nd = len(shape)
            out_specs.append(pl.BlockSpec(shape, lambda i, nd=nd: (0,) * nd))
            acc_idx.append(n)
    n_in = len(arrays)

    def wrapped(*refs):
        @pl.when(pl.program_id(0) == 0)
        def _():
            for n in acc_idx:
                refs[n_in + n][...] = jnp.zeros_like(refs[n_in + n])

        body(*refs)

    res = pl.pallas_call(
        wrapped, name=name, grid=(nsteps,), in_specs=in_specs, out_specs=out_specs, out_shape=out_shapes,
        compiler_params=_cparams(("arbitrary",) if acc_idx else ("parallel",)),
    )(*arrays)
    return res


def _shift_down(cur, prev8, k):
    if k == 0:
        return cur
    rolled = pltpu.roll(cur, k, axis=0)
    fix = pltpu.roll(prev8, k, axis=0)
    row = lax.broadcasted_iota(jnp.int32, (8, 1), 0)
    top = jnp.where(row < k, fix, rolled[0:8])
    if cur.shape[0] == 8:
        return top
    return jnp.concatenate([top, rolled[8:]], axis=0)


def _shift_up(cur, next8, k):
    if k == 0:
        return cur
    n = cur.shape[0]
    rolled = pltpu.roll(cur, n - k, axis=0)
    fix = pltpu.roll(next8, 8 - k, axis=0)
    row = lax.broadcasted_iota(jnp.int32, (8, 1), 0)
    bot = jnp.where(row >= 8 - k, fix, rolled[n - 8:n])
    return jnp.concatenate([rolled[:n - 8], bot], axis=0)


def _sigmoid(x):
    return 1.0 / (1.0 + jnp.exp(-x))


def _silu(x):
    return x * _sigmoid(x)


def _dsilu(x):
    s = _sigmoid(x)
    return s * (1.0 + x * (1.0 - s))


def _softplus(x):
    return jnp.maximum(x, 0.0) + jnp.log1p(jnp.exp(-jnp.abs(x)))


def _split2(x):
    hi = x.astype(BF16)
    return hi, (x - hi.astype(F32)).astype(BF16)


def _dot1(a, b, mode):
    return lax.dot_general(a.astype(BF16), b.astype(BF16), _DIMS[mode], preferred_element_type=F32)


def _dot3(a, b, mode):
    ah, al = _split2(a)
    bh, bl = _split2(b)
    d = lambda p, q: lax.dot_general(p, q, _DIMS[mode], preferred_element_type=F32)
    return d(ah, bh) + (d(ah, bl) + d(al, bh))


def ada_fwd(c_all, w_sh, b_sh):
    n = w_sh.shape[1]
    tn = 512

    def body(c_ref, w_ref, b_ref, o_ref):
        o_ref[...] = _dot1(_silu(c_ref[...]), w_ref[...], "nn") + b_ref[...]

    return pl.pallas_call(
        body, name="ada_fwd", grid=(n // tn,),
        in_specs=[pl.BlockSpec((N_DEV, D_MODEL), lambda j: (0, 0)), pl.BlockSpec((D_MODEL, tn), lambda j: (0, j)),
                  pl.BlockSpec((1, tn), lambda j: (0, j))],
        out_specs=pl.BlockSpec((N_DEV, tn), lambda j: (0, j)), out_shape=jax.ShapeDtypeStruct((N_DEV, n), F32),
        compiler_params=_cparams(("parallel",)),
    )(c_all, w_sh, b_sh)


def ada_bwd(c_all, dmod_sh):
    n = dmod_sh.shape[1]
    tn = 512

    def body(c_ref, d_ref, o_ref):
        o_ref[...] = _dot1(_silu(c_ref[...]), d_ref[...], "tn")

    return pl.pallas_call(
        body, name="ada_bwd", grid=(n // tn,),
        in_specs=[pl.BlockSpec((N_DEV, D_MODEL), lambda j: (0, 0)), pl.BlockSpec((N_DEV, tn), lambda j: (0, j))],
        out_specs=pl.BlockSpec((D_MODEL, tn), lambda j: (0, j)), out_shape=jax.ShapeDtypeStruct((D_MODEL, n), F32),
        compiler_params=_cparams(("parallel",)),
    )(c_all, dmod_sh)


SHIFT_T, SCALE_T, GATE_T, SHIFT_F, SCALE_F, GATE_F = range(6)


def modulate(x, mod, shift_row, scale_row, name):
    S = x.shape[0]

    def body(x_ref, m_ref, o_ref):
        m = m_ref[...]
        o_ref[...] = (x_ref[...] * (1.0 + m[scale_row:scale_row + 1]) + m[shift_row:shift_row + 1]).astype(BF16)

    return rowcall(body, name=name, S=S, ts=512, ins=[(x, "row"), (mod, "vec")], outs=[((S, D_MODEL), BF16, "row")])[0]


def _conv_fwd(cur, prev, w, width):
    y = cur * w[width - 1:width]
    for j in range(width - 1):
        y = y + _shift_down(cur, prev, width - 1 - j) * w[j:j + 1]
    return y


def _prep_a_core(cur, prev, w):
    pre = _conv_fwd(cur, prev, w, A_CONV)
    y = _silu(pre)
    return pre, y


def prep_a_fwd(qkv_raw, ba, conv_a, a_log, dt_bias):
    S = qkv_raw.shape[0]

    def body(x_ref, xp_ref, ba_ref, w_ref, al_ref, dt_ref, q_ref, k_ref, v_ref, beta_ref, g_ref):
        first = (pl.program_id(0) > 0).astype(F32)
        _, y = _prep_a_core(x_ref[...], xp_ref[...] * first, w_ref[...])
        for h in range(A_HEADS):
            sl = slice(h * A_DK, (h + 1) * A_DK)
            qh = y[:, sl]
            kh = y[:, A_W + h * A_DK:A_W + (h + 1) * A_DK]
            q_ref[:, sl] = qh * (lax.rsqrt(jnp.sum(qh * qh, axis=-1, keepdims=True) + L2_EPS) * (A_DK ** -0.5))
            k_ref[:, sl] = kh * lax.rsqrt(jnp.sum(kh * kh, axis=-1, keepdims=True) + L2_EPS)
        v_ref[...] = y[:, 2 * A_W:3 * A_W]
        bav = ba_ref[...]
        beta_ref[...] = _sigmoid(bav[:, 0:A_HEADS])
        g_ref[...] = -jnp.exp(al_ref[...]) * _softplus(bav[:, A_HEADS:2 * A_HEADS] + dt_ref[...])

    return rowcall(
        body, name="prep_a_fwd", S=S, ts=256,
        ins=[(qkv_raw, "row"), (qkv_raw, "prev"), (ba, "row"), (conv_a, "vec"), (a_log, "vec"), (dt_bias, "vec")],
        outs=[((S, A_W), F32, "row")] * 3 + [((S, A_HEADS), F32, "row")] * 2)


def _tri_masks():
    row = lax.broadcasted_iota(jnp.int32, (CHUNK, CHUNK), 0)
    col = lax.broadcasted_iota(jnp.int32, (CHUNK, CHUNK), 1)
    return row >= col, row > col, row == col


def _col_to_row(colv, eye):
    return jnp.sum(jnp.where(eye, colv, 0.0), axis=0, keepdims=True)


def _row_to_col(rowv, eye):
    return jnp.sum(jnp.where(eye, rowv, 0.0), axis=1, keepdims=True)


def _tri_inv(a, eye):
    rb = lax.broadcasted_iota(jnp.int32, (CHUNK, CHUNK), 0)
    cb = lax.broadcasted_iota(jnp.int32, (CHUNK, CHUNK), 1)
    t = jnp.where(eye, 1.0, 0.0) - jnp.where((rb >> 1) == (cb >> 1), a, 0.0)
    for lvl in range(1, int(math.log2(CHUNK))):
        rs, cs = rb >> lvl, cb >> lvl
        off = jnp.where(((rs & 1) == 1) & (cs == rs - 1), a, 0.0)
        t = t - _dot3(_dot3(t, off, "nn"), t, "nn")
    return t


def _delta_local(q, k, v, beta, g, masks):
    causal, strict, eye = masks
    g_row = _col_to_row(g, eye)
    gc = jnp.sum(jnp.where(causal, g_row, 0.0), axis=1, keepdims=True)
    gc_row = _col_to_row(gc, eye)
    decay = jnp.where(causal, jnp.exp(jnp.where(causal, gc - gc_row, 0.0)), 0.0)
    gam = jnp.exp(gc)
    kb = k * beta
    vb = v * beta
    y = kb * gam
    mkk = _dot1(kb, k, "nt")
    a = jnp.where(strict, mkk * decay, 0.0)
    nqk = _dot1(q, k, "nt")
    p = jnp.where(causal, nqk * decay, 0.0)
    gl = gc[CHUNK - 1:CHUNK, :]
    kd = k * jnp.exp(gl - gc)
    return dict(gc=gc, decay=decay, gam=gam, kb=kb, vb=vb, y=y, mkk=mkk, a=a, nqk=nqk, p=p, gl=gl, kd=kd)


def delta_fwd(q, k, v, beta, g):
    S = q.shape[0]
    n_chunks = S // CHUNK

    def body(q_ref, k_ref, v_ref, beta_ref, g_ref, o_ref, sprev_ref, t_ref, state_ref):
        @pl.when(pl.program_id(0) == 0)
        def _():
            state_ref[...] = jnp.zeros_like(state_ref)

        masks = _tri_masks()
        betav, gv = beta_ref[...], g_ref[...]
        for h in range(A_HEADS):
            sl = slice(h * A_DK, (h + 1) * A_DK)
            qh, kh, vh = q_ref[:, sl], k_ref[:, sl], v_ref[:, sl]
            loc = _delta_local(qh, kh, vh, betav[:, h:h + 1], gv[:, h:h + 1], masks)
            tinv = _tri_inv(loc["a"], masks[2])
            uw = _dot3(tinv, jnp.concatenate([loc["vb"], loc["y"]], axis=1), "nn")
            s0 = state_ref[h]
            vn = uw[:, :A_DK] - _dot1(uw[:, A_DK:], s0, "nn")
            o_ref[:, sl] = _dot1(qh * loc["gam"], s0, "nn") + _dot1(loc["p"], vn, "nn")
            sprev_ref[0, h] = s0
            t_ref[:, h * CHUNK:(h + 1) * CHUNK] = tinv
            state_ref[h] = s0 * jnp.exp(loc["gl"]) + _dot1(loc["kd"], vn, "tn")

    tile = pl.BlockSpec((CHUNK, A_W), lambda n: (n, 0))
    small = pl.BlockSpec((CHUNK, A_HEADS), lambda n: (n, 0))
    return pl.pallas_call(
        body, name="delta_fwd", grid=(n_chunks,), in_specs=[tile, tile, tile, small, small],
        out_specs=[tile, pl.BlockSpec((1, A_HEADS, A_DK, A_DK), lambda n: (n, 0, 0, 0)),
                   pl.BlockSpec((CHUNK, A_HEADS * CHUNK), lambda n: (n, 0))],
        out_shape=[jax.ShapeDtypeStruct((S, A_W), F32), jax.ShapeDtypeStruct((n_chunks, A_HEADS, A_DK, A_DK), F32),
                   jax.ShapeDtypeStruct((S, A_HEADS * CHUNK), F32)],
        scratch_shapes=[pltpu.VMEM((A_HEADS, A_DK, A_DK), F32)],
        compiler_params=_cparams(("arbitrary",)),
    )(q, k, v, beta, g)


def gate_a_fwd(o_pre, z, norm_w):
    S = o_pre.shape[0]

    def body(o_ref, z_ref, nw_ref, out_ref):
        nw = nw_ref[...]
        for h in range(A_HEADS):
            sl = slice(h * A_DK, (h + 1) * A_DK)
            oh = o_ref[:, sl]
            r = lax.rsqrt(jnp.mean(oh * oh, axis=-1, keepdims=True) + RMS_EPS)
            out_ref[:, sl] = (oh * r * nw * _silu(z_ref[:, sl])).astype(BF16)

    return rowcall(body, name="gate_a_fwd", S=S, ts=512, ins=[(o_pre, "row"), (z, "row"), (norm_w, "vec")],
                   outs=[((S, A_W), BF16, "row")])[0]


HEADS_PER_GROUP = 2
GROUP_W = HEADS_PER_GROUP * B_DH
N_GROUPS = B_HEADS // HEADS_PER_GROUP
PAD_ROWS = B_PREV * CHUNK


def _band_probs(qh, kh, bias, valid):
    s = _dot1(qh, kh, "nt") * (B_DH ** -0.5) + bias
    s = jnp.where(valid, s, NEG_INF)
    e = jnp.exp(s - jnp.max(s, axis=-1, keepdims=True))
    return e / jnp.sum(e, axis=-1, keepdims=True)


def attn_fwd(qkv_pad, bias):
    S = qkv_pad.shape[0] - PAD_ROWS
    n_chunks = S // CHUNK
    n_cb = B_W // GROUP_W

    def body(q_ref, k_ref, v_ref, b_ref, o_ref):
        n = pl.program_id(1)
        start = pl.multiple_of(n * CHUNK, CHUNK)
        kb = k_ref[pl.ds(start, B_BAND), :]
        vb = v_ref[pl.ds(start, B_BAND), :]
        qv = q_ref[...]
        valid = lax.broadcasted_iota(jnp.int32, (CHUNK, B_BAND), 1) >= (B_PREV - n) * CHUNK
        outs = []
        for hh in range(HEADS_PER_GROUP):
            sl = slice(hh * B_DH, (hh + 1) * B_DH)
            p = _band_probs(qv[:, sl], kb[:, sl], b_ref[hh], valid)
            outs.append(_dot1(p, vb[:, sl], "nn"))
        o_ref[...] = jnp.concatenate(outs, axis=1).astype(BF16)

    return pl.pallas_call(
        body, name="attn_fwd", grid=(N_GROUPS, n_chunks),
        in_specs=[pl.BlockSpec((CHUNK, GROUP_W), lambda g, n: (n + B_PREV, g)),
                  pl.BlockSpec((PAD_ROWS + S, GROUP_W), lambda g, n: (0, n_cb + g)),
                  pl.BlockSpec((PAD_ROWS + S, GROUP_W), lambda g, n: (0, 2 * n_cb + g)),
                  pl.BlockSpec((HEADS_PER_GROUP, CHUNK, B_BAND), lambda g, n: (g, 0, 0))],
        out_specs=pl.BlockSpec((CHUNK, GROUP_W), lambda g, n: (n, g)),
        out_shape=jax.ShapeDtypeStruct((S, B_W), BF16),
        compiler_params=_cparams(("parallel", "arbitrary")),
    )(qkv_pad, qkv_pad, qkv_pad, bias)


def _rel_onehot(i):
    kj = lax.broadcasted_iota(jnp.int32, (B_BAND, B_REL), 0)
    r = lax.broadcasted_iota(jnp.int32, (B_BAND, B_REL), 1)
    idx = jnp.clip(PAD_ROWS + i - kj, -(CHUNK - 1), B_MAX_REL) + (CHUNK - 1)
    return jnp.where(idx == r, 1.0, 0.0)


def bias_expand(rel_bias):
    def body(rb_ref, o_ref):
        i = pl.program_id(0)
        o_ref[0] = _dot3(rb_ref[...], _rel_onehot(i), "nt")

    return pl.pallas_call(
        body, name="bias_expand", grid=(CHUNK,),
        in_specs=[pl.BlockSpec((B_HEADS, B_REL), lambda i: (0, 0))],
        out_specs=pl.BlockSpec((1, B_HEADS, B_BAND), lambda i: (i, 0, 0)),
        out_shape=jax.ShapeDtypeStruct((CHUNK, B_HEADS, B_BAND), F32),
        compiler_params=_cparams(("parallel",)),
    )(rel_bias)


def bias_reduce(dbias):
    def body(d_ref, o_ref):
        i = pl.program_id(0)

        @pl.when(i == 0)
        def _():
            o_ref[...] = jnp.zeros_like(o_ref)

        o_ref[...] += _dot3(d_ref[0], _rel_onehot(i), "nn")

    return pl.pallas_call(
        body, name="bias_reduce", grid=(CHUNK,),
        in_specs=[pl.BlockSpec((1, B_HEADS, B_BAND), lambda i: (i, 0, 0))],
        out_specs=pl.BlockSpec((B_HEADS, B_REL), lambda i: (0, 0)),
        out_shape=jax.ShapeDtypeStruct((B_HEADS, B_REL), F32),
        compiler_params=_cparams(("arbitrary",)),
    )(dbias)


def merge_fwd(gates_raw, b_gate, ya, yb):
    S = ya.shape[0]

    def body(g_ref, b_ref, ya_ref, yb_ref, o_ref):
        gt = _sigmoid(g_ref[...] + b_ref[...])
        o_ref[...] = (gt[:, :D_MODEL] * ya_ref[...] + gt[:, D_MODEL:] * yb_ref[...]).astype(BF16)

    return rowcall(body, name="merge_fwd", S=S, ts=512,
                   ins=[(gates_raw, "row"), (b_gate, "vec"), (ya, "row"), (yb, "row")],
                   outs=[((S, D_MODEL), BF16, "row")])[0]


def _ln_stats(xpre):
    mu = jnp.mean(xpre, axis=-1, keepdims=True)
    xc = xpre - mu
    rstd = lax.rsqrt(jnp.mean(xc * xc, axis=-1, keepdims=True) + LN_EPS)
    return xc * rstd, rstd


def ln1_fwd(x, mix, mod, ln_g, ln_b):
    S = x.shape[0]

    def body(x_ref, mix_ref, m_ref, g_ref, b_ref, xpre_ref, x1_ref, h2_ref):
        m = m_ref[...]
        xpre = ALPHA * x_ref[...] + m[GATE_T:GATE_T + 1] * mix_ref[...]
        xhat, _ = _ln_stats(xpre)
        x1 = xhat * g_ref[...] + b_ref[...]
        xpre_ref[...] = xpre
        x1_ref[...] = x1
        h2_ref[...] = (x1 * (1.0 + m[SCALE_F:SCALE_F + 1]) + m[SHIFT_F:SHIFT_F + 1]).astype(BF16)

    return rowcall(body, name="ln1_fwd", S=S, ts=512,
                   ins=[(x, "row"), (mix, "row"), (mod, "vec"), (ln_g, "vec"), (ln_b, "vec")],
                   outs=[((S, D_MODEL), F32, "row"), ((S, D_MODEL), F32, "row"), ((S, D_MODEL), BF16, "row")])


def ffn_act_fwd(up, conv_w, conv_b):
    S = up.shape[0]

    def body(u_ref, up_ref, w_ref, b_ref, o_ref):
        first = (pl.program_id(0) > 0).astype(F32)
        uc = _conv_fwd(u_ref[...], up_ref[...] * first, w_ref[...], FFN_CONV) + b_ref[...]
        o_ref[...] = (_silu(uc[:, :D_FF]) * uc[:, D_FF:]).astype(BF16)

    return rowcall(body, name="ffn_act_fwd", S=S, ts=128,
                   ins=[(up, "row"), (up, "prev"), (conv_w, "vec"), (conv_b, "vec")],
                   outs=[((S, D_FF), BF16, "row")])[0]


def final_fwd_bwd(x1, ffn, target, mod, ln_g, ln_b):
    S = x1.shape[0]

    def body(x1_ref, f_ref, t_ref, m_ref, g_ref, b_ref, dxpre_ref, dffn_ref, loss_ref, dgate_ref, dg_ref, db_ref):
        gate = m_ref[...][GATE_F:GATE_F + 1]
        ffn_v = f_ref[...]
        xpre = ALPHA * x1_ref[...] + gate * ffn_v
        xhat, rstd = _ln_stats(xpre)
        err = xhat * g_ref[...] + b_ref[...] - t_ref[...]
        loss_ref[...] += 0.5 * jnp.sum(jnp.mean(err * err, axis=-1, keepdims=True), axis=0, keepdims=True)
        dy = err * (1.0 / D_MODEL)
        dg_ref[...] += jnp.sum(dy * xhat, axis=0, keepdims=True)
        db_ref[...] += jnp.sum(dy, axis=0, keepdims=True)
        dyg = dy * g_ref[...]
        dxpre = rstd * (dyg - jnp.mean(dyg, axis=-1, keepdims=True) - xhat * jnp.mean(dyg * xhat, axis=-1, keepdims=True))
        dxpre_ref[...] = dxpre
        dffn_ref[...] = (gate * dxpre).astype(BF16)
        dgate_ref[...] += jnp.sum(dxpre * ffn_v, axis=0, keepdims=True)

    vec = ((1, D_MODEL), F32, "acc")
    return rowcall(body, name="final_fwd_bwd", S=S, ts=512,
                   ins=[(x1, "row"), (ffn, "row"), (target, "row"), (mod, "vec"), (ln_g, "vec"), (ln_b, "vec")],
                   outs=[((S, D_MODEL), F32, "row"), ((S, D_MODEL), BF16, "row"), ((1, 1), F32, "acc"), vec, vec, vec])


def _ffn_duc(dact, uc):
    ug, uv = uc[:, :D_FF], uc[:, D_FF:]
    return jnp.concatenate([dact * uv * _dsilu(ug), dact * _silu(ug)], axis=1)


def ffn_act_bwd(dact, up, conv_w, conv_b):
    S = up.shape[0]
    ts = 128

    def body(d_ref, dn_ref, u_ref, up_ref, un_ref, w_ref, b_ref, dup_ref, dw_ref, db_ref):
        i = pl.program_id(0)
        first = (i > 0).astype(F32)
        last = (i < pl.num_programs(0) - 1).astype(F32)
        w, b = w_ref[...], b_ref[...]
        cur, prev = u_ref[...], up_ref[...] * first
        shifted = [_shift_down(cur, prev, FFN_CONV - 1 - j) for j in range(FFN_CONV)]
        uc = b + sum(shifted[j] * w[j:j + 1] for j in range(FFN_CONV))
        duc = _ffn_duc(d_ref[...], uc)
        uc_n = _conv_fwd(un_ref[...], cur[ts - 8:ts], w, FFN_CONV) + b
        duc_n = _ffn_duc(dn_ref[...], uc_n) * last
        db_ref[...] += jnp.sum(duc, axis=0, keepdims=True)
        for j in range(FFN_CONV):
            dw_ref[j:j + 1, :] += jnp.sum(duc * shifted[j], axis=0, keepdims=True)
        dup = duc * w[FFN_CONV - 1:FFN_CONV]
        for j in range(FFN_CONV - 1):
            dup = dup + _shift_up(duc, duc_n, FFN_CONV - 1 - j) * w[j:j + 1]
        dup_ref[...] = dup.astype(BF16)

    return rowcall(body, name="ffn_act_bwd", S=S, ts=ts,
                   ins=[(dact, "row"), (dact, "next"), (up, "row"), (up, "prev"), (up, "next"), (conv_w, "vec"), (conv_b, "vec")],
                   outs=[((S, 2 * D_FF), BF16, "row"), ((FFN_CONV, 2 * D_FF), F32, "acc"), ((1, 2 * D_FF), F32, "acc")])


def ln1_bwd(dxpre2, dh2, xpre1, mix, mod, ln_g, ln_b):
    S = xpre1.shape[0]

    def body(d2_ref, dh_ref, xp_ref, mix_ref, m_ref, g_ref, b_ref, dxpre_ref, dmix_ref,
             dscale_ref, dshift_ref, dgate_ref, dg_ref, db_ref):
        m = m_ref[...]
        xhat, rstd = _ln_stats(xp_ref[...])
        x1 = xhat * g_ref[...] + b_ref[...]
        dh = dh_ref[...]
        dx1 = ALPHA * d2_ref[...] + dh * (1.0 + m[SCALE_F:SCALE_F + 1])
        dscale_ref[...] += jnp.sum(dh * x1, axis=0, keepdims=True)
        dshift_ref[...] += jnp.sum(dh, axis=0, keepdims=True)
        dg_ref[...] += jnp.sum(dx1 * xhat, axis=0, keepdims=True)
        db_ref[...] += jnp.sum(dx1, axis=0, keepdims=True)
        dyg = dx1 * g_ref[...]
        dxpre = rstd * (dyg - jnp.mean(dyg, axis=-1, keepdims=True) - xhat * jnp.mean(dyg * xhat, axis=-1, keepdims=True))
        dxpre_ref[...] = dxpre
        dmix_ref[...] = (m[GATE_T:GATE_T + 1] * dxpre).astype(BF16)
        dgate_ref[...] += jnp.sum(dxpre * mix_ref[...], axis=0, keepdims=True)

    vec = ((1, D_MODEL), F32, "acc")
    return rowcall(body, name="ln1_bwd", S=S, ts=512,
                   ins=[(dxpre2, "row"), (dh2, "row"), (xpre1, "row"), (mix, "row"), (mod, "vec"), (ln_g, "vec"), (ln_b, "vec")],
                   outs=[((S, D_MODEL), F32, "row"), ((S, D_MODEL), BF16, "row"), vec, vec, vec, vec, vec])


def merge_bwd(dmerged, gates_raw, b_gate, ya, yb):
    S = ya.shape[0]

    def body(d_ref, g_ref, b_ref, ya_ref, yb_ref, dya_ref, dyb_ref, dg_ref, dbg_ref):
        gt = _sigmoid(g_ref[...] + b_ref[...])
        d = d_ref[...]
        ga, gb = gt[:, :D_MODEL], gt[:, D_MODEL:]
        dya_ref[...] = (d * ga).astype(BF16)
        dyb_ref[...] = (d * gb).astype(BF16)
        dgr = jnp.concatenate([d * ya_ref[...] * ga * (1.0 - ga), d * yb_ref[...] * gb * (1.0 - gb)], axis=1)
        dg_ref[...] = dgr.astype(BF16)
        dbg_ref[...] += jnp.sum(dgr, axis=0, keepdims=True)

    return rowcall(body, name="merge_bwd", S=S, ts=512,
                   ins=[(dmerged, "row"), (gates_raw, "row"), (b_gate, "vec"), (ya, "row"), (yb, "row")],
                   outs=[((S, D_MODEL), BF16, "row"), ((S, D_MODEL), BF16, "row"), ((S, 2 * D_MODEL), BF16, "row"),
                         ((1, 2 * D_MODEL), F32, "acc")])


def attn_bwd(qkv_pad, bias, do_b):
    S = qkv_pad.shape[0] - PAD_ROWS
    n_chunks = S // CHUNK
    n_cb = B_W // GROUP_W

    def body(q_ref, k_ref, v_ref, b_ref, do_ref, dq_ref, dk_ref, dv_ref, db_ref):
        n = pl.program_id(1)

        @pl.when(n == 0)
        def _():
            dk_ref[...] = jnp.zeros_like(dk_ref)
            dv_ref[...] = jnp.zeros_like(dv_ref)
            db_ref[...] = jnp.zeros_like(db_ref)

        start = pl.multiple_of(n * CHUNK, CHUNK)
        kb = k_ref[pl.ds(start, B_BAND), :]
        vb = v_ref[pl.ds(start, B_BAND), :]
        qv, dov = q_ref[...], do_ref[...]
        valid = lax.broadcasted_iota(jnp.int32, (CHUNK, B_BAND), 1) >= (B_PREV - n) * CHUNK
        dqs, dks, dvs = [], [], []
        for hh in range(HEADS_PER_GROUP):
            sl = slice(hh * B_DH, (hh + 1) * B_DH)
            p = _band_probs(qv[:, sl], kb[:, sl], b_ref[hh], valid)
            dp = _dot1(dov[:, sl], vb[:, sl], "nt")
            ds = p * (dp - jnp.sum(dp * p, axis=-1, keepdims=True))
            db_ref[hh] += ds
            dsq = ds * (B_DH ** -0.5)
            dqs.append(_dot1(dsq, kb[:, sl], "nn"))
            dks.append(_dot1(dsq, qv[:, sl], "tn"))
            dvs.append(_dot1(p, dov[:, sl], "tn"))
        dq_ref[...] = jnp.concatenate(dqs, axis=1).astype(BF16)
        dk_ref[pl.ds(start, B_BAND), :] += jnp.concatenate(dks, axis=1)
        dv_ref[pl.ds(start, B_BAND), :] += jnp.concatenate(dvs, axis=1)

    col = pl.BlockSpec((PAD_ROWS + S, GROUP_W), lambda g, n: (0, g))
    return pl.pallas_call(
        body, name="attn_bwd", grid=(N_GROUPS, n_chunks),
        in_specs=[pl.BlockSpec((CHUNK, GROUP_W), lambda g, n: (n + B_PREV, g)),
                  pl.BlockSpec((PAD_ROWS + S, GROUP_W), lambda g, n: (0, n_cb + g)),
                  pl.BlockSpec((PAD_ROWS + S, GROUP_W), lambda g, n: (0, 2 * n_cb + g)),
                  pl.BlockSpec((HEADS_PER_GROUP, CHUNK, B_BAND), lambda g, n: (g, 0, 0)),
                  pl.BlockSpec((CHUNK, GROUP_W), lambda g, n: (n, g))],
        out_specs=[pl.BlockSpec((CHUNK, GROUP_W), lambda g, n: (n, g)), col, col,
                   pl.BlockSpec((HEADS_PER_GROUP, CHUNK, B_BAND), lambda g, n: (g, 0, 0))],
        out_shape=[jax.ShapeDtypeStruct((S, B_W), BF16), jax.ShapeDtypeStruct((PAD_ROWS + S, B_W), F32),
                   jax.ShapeDtypeStruct((PAD_ROWS + S, B_W), F32), jax.ShapeDtypeStruct((B_HEADS, CHUNK, B_BAND), F32)],
        compiler_params=_cparams(("parallel", "arbitrary")),
    )(qkv_pad, qkv_pad, qkv_pad, bias, do_b)


def gate_a_bwd(do_a, o_pre, z, norm_w):
    S = o_pre.shape[0]

    def body(d_ref, o_ref, z_ref, nw_ref, dop_ref, dz_ref, dnw_ref):
        nw = nw_ref[...]
        acc = jnp.zeros((1, A_DK), F32)
        for h in range(A_HEADS):
            sl = slice(h * A_DK, (h + 1) * A_DK)
            oh, zh, dh = o_ref[:, sl], z_ref[:, sl], d_ref[:, sl]
            r = lax.rsqrt(jnp.mean(oh * oh, axis=-1, keepdims=True) + RMS_EPS)
            sz = _silu(zh)
            dz_ref[:, sl] = (dh * oh * r * nw * _dsilu(zh)).astype(BF16)
            acc = acc + jnp.sum(dh * oh * r * sz, axis=0, keepdims=True)
            t = dh * nw * sz
            dop_ref[:, sl] = r * t - oh * (r * r * r) * jnp.mean(t * oh, axis=-1, keepdims=True)
        dnw_ref[...] += acc

    return rowcall(body, name="gate_a_bwd", S=S, ts=512,
                   ins=[(do_a, "row"), (o_pre, "row"), (z, "row"), (norm_w, "vec")],
                   outs=[((S, A_W), F32, "row"), ((S, A_W), BF16, "row"), ((1, A_DK), F32, "acc")])


def delta_bwd(q, k, v, beta, g, sprev, tinv, do):
    S = q.shape[0]
    n_chunks = S // CHUNK

    def body(q_ref, k_ref, v_ref, beta_ref, g_ref, sprev_ref, t_ref, do_ref,
             dq_ref, dk_ref, dv_ref, dbeta_ref, dg_ref, dstate_ref):
        @pl.when(pl.program_id(0) == 0)
        def _():
            dstate_ref[...] = jnp.zeros_like(dstate_ref)

        masks = _tri_masks()
        causal, strict, eye = masks
        row = lax.broadcasted_iota(jnp.int32, (CHUNK, 1), 0)
        upper = lax.broadcasted_iota(jnp.int32, (CHUNK, CHUNK), 1) >= lax.broadcasted_iota(jnp.int32, (CHUNK, CHUNK), 0)
        lane = lax.broadcasted_iota(jnp.int32, (CHUNK, A_HEADS), 1)
        betav, gv = beta_ref[...], g_ref[...]
        dbeta_t = jnp.zeros((CHUNK, A_HEADS), F32)
        dg_t = jnp.zeros((CHUNK, A_HEADS), F32)
        for h in range(A_HEADS):
            sl = slice(h * A_DK, (h + 1) * A_DK)
            qh, kh, vh, doh = q_ref[:, sl], k_ref[:, sl], v_ref[:, sl], do_ref[:, sl]
            bh = betav[:, h:h + 1]
            loc = _delta_local(qh, kh, vh, bh, gv[:, h:h + 1], masks)
            gam, decay, kd, gl, gc = loc["gam"], loc["decay"], loc["kd"], loc["gl"], loc["gc"]
            tinv_h = t_ref[:, h * CHUNK:(h + 1) * CHUNK]
            rhs = jnp.concatenate([loc["vb"], loc["y"]], axis=1)
            uw = _dot3(tinv_h, rhs, "nn")
            w = uw[:, A_DK:]
            s0 = sprev_ref[0, h]
            ds1 = dstate_ref[h]
            vn = uw[:, :A_DK] - _dot1(w, s0, "nn")
            qg = qh * gam
            egl = jnp.exp(gl)
            dvn = _dot1(loc["p"], doh, "tn") + _dot1(kd, ds1, "nn")
            dp = jnp.where(causal, _dot1(doh, vn, "nt"), 0.0)
            dqg = _dot1(doh, s0, "nt")
            dq = dqg * gam
            dgc = jnp.sum(dqg * qg, axis=-1, keepdims=True)
            dstate_ref[h] = _dot1(qg, doh, "tn") + egl * ds1 - _dot1(w, dvn, "tn")
            dkd = _dot1(vn, ds1, "nt")
            dk = dkd * jnp.exp(gl - gc)
            t1 = jnp.sum(dkd * kd, axis=-1, keepdims=True)
            dgc = dgc - t1
            dgl = jnp.sum(t1, axis=0, keepdims=True) + jnp.sum(jnp.sum(ds1 * s0, axis=-1, keepdims=True), axis=0, keepdims=True) * egl
            duw = jnp.concatenate([dvn, -_dot1(dvn, s0, "nt")], axis=1)
            dvby = _dot3(tinv_h, duw, "tn")
            dt = _dot3(duw, rhs, "nt")
            da = jnp.where(strict, -_dot3(_dot3(tinv_h, dt, "tn"), tinv_h, "nt"), 0.0)
            dm = da * decay
            dn = dp * decay
            e = da * loc["a"] + dp * loc["p"]
            dgc = dgc + jnp.sum(e, axis=1, keepdims=True) - _row_to_col(jnp.sum(e, axis=0, keepdims=True), eye)
            dkb = _dot1(dm, kh, "nn")
            dk = dk + _dot1(dm, loc["kb"], "tn")
            dq = dq + _dot1(dn, kh, "nn")
            dk = dk + _dot1(dn, qh, "tn")
            dy = dvby[:, A_DK:]
            dvb = dvby[:, :A_DK]
            dkb = dkb + dy * gam
            dgc = dgc + jnp.sum(dy * loc["y"], axis=-1, keepdims=True)
            dk = dk + dkb * bh
            dbeta = jnp.sum(dkb * kh, axis=-1, keepdims=True) + jnp.sum(dvb * vh, axis=-1, keepdims=True)
            dgc = dgc + jnp.where(row == CHUNK - 1, dgl, 0.0)
            dgh = jnp.sum(jnp.where(upper, _col_to_row(dgc, eye), 0.0), axis=1, keepdims=True)
            dq_ref[:, sl] = dq
            dk_ref[:, sl] = dk
            dv_ref[:, sl] = dvb * bh
            dbeta_t = dbeta_t + jnp.where(lane == h, dbeta, 0.0)
            dg_t = dg_t + jnp.where(lane == h, dgh, 0.0)
        dbeta_ref[...] = dbeta_t
        dg_ref[...] = dg_t

    rev = lambda n: (n_chunks - 1 - n, 0)
    tile = pl.BlockSpec((CHUNK, A_W), rev)
    small = pl.BlockSpec((CHUNK, A_HEADS), rev)
    return pl.pallas_call(
        body, name="delta_bwd", grid=(n_chunks,),
        in_specs=[tile, tile, tile, small, small,
                  pl.BlockSpec((1, A_HEADS, A_DK, A_DK), lambda n: (n_chunks - 1 - n, 0, 0, 0)),
                  pl.BlockSpec((CHUNK, A_HEADS * CHUNK), rev), tile],
        out_specs=[tile, tile, tile, small, small],
        out_shape=[jax.ShapeDtypeStruct((S, A_W), F32)] * 3 + [jax.ShapeDtypeStruct((S, A_HEADS), F32)] * 2,
        scratch_shapes=[pltpu.VMEM((A_HEADS, A_DK, A_DK), F32)],
        compiler_params=_cparams(("arbitrary",)),
    )(q, k, v, beta, g, sprev, tinv, do)


def _prep_a_dpre(raw, raw_prev, w, dq, dk, dv):
    pre, y = _prep_a_core(raw, raw_prev, w)
    parts = []
    for h in range(A_HEADS):
        yq = y[:, h * A_DK:(h + 1) * A_DK]
        dqh = dq[:, h * A_DK:(h + 1) * A_DK]
        rq = lax.rsqrt(jnp.sum(yq * yq, axis=-1, keepdims=True) + L2_EPS)
        parts.append((A_DK ** -0.5) * (rq * dqh - yq * (rq * rq * rq) * jnp.sum(dqh * yq, axis=-1, keepdims=True)))
    for h in range(A_HEADS):
        yk = y[:, A_W + h * A_DK:A_W + (h + 1) * A_DK]
        dkh = dk[:, h * A_DK:(h + 1) * A_DK]
        rk = lax.rsqrt(jnp.sum(yk * yk, axis=-1, keepdims=True) + L2_EPS)
        parts.append(rk * dkh - yk * (rk * rk * rk) * jnp.sum(dkh * yk, axis=-1, keepdims=True))
    parts.append(dv)
    return jnp.concatenate(parts, axis=1) * _dsilu(pre)


def prep_a_bwd(qkv_raw, ba, conv_a, a_log, dt_bias, dq, dk, dv, dbeta, dg):
    S = qkv_raw.shape[0]
    ts = 256

    def body(x_ref, xp_ref, xn_ref, ba_ref, w_ref, al_ref, dt_ref, dq_ref, dqn_ref, dk_ref, dkn_ref, dv_ref, dvn_ref,
             dbeta_ref, dg_ref, draw_ref, dba_ref, dw_ref, dal_ref, ddt_ref):
        i = pl.program_id(0)
        first = (i > 0).astype(F32)
        last = (i < pl.num_programs(0) - 1).astype(F32)
        w = w_ref[...]
        cur, prev = x_ref[...], xp_ref[...] * first
        dpre = _prep_a_dpre(cur, prev, w, dq_ref[...], dk_ref[...], dv_ref[...])
        dpre_n = _prep_a_dpre(xn_ref[...], cur[ts - 8:ts], w, dqn_ref[...], dkn_ref[...], dvn_ref[...]) * last
        for j in range(A_CONV):
            dw_ref[j:j + 1, :] += jnp.sum(dpre * _shift_down(cur, prev, A_CONV - 1 - j), axis=0, keepdims=True)
        draw = dpre * w[A_CONV - 1:A_CONV]
        for j in range(A_CONV - 1):
            draw = draw + _shift_up(dpre, dpre_n, A_CONV - 1 - j) * w[j:j + 1]
        draw_ref[...] = draw.astype(BF16)
        bav = ba_ref[...]
        beta = _sigmoid(bav[:, 0:A_HEADS])
        xa = bav[:, A_HEADS:2 * A_HEADS] + dt_ref[...]
        nexp = -jnp.exp(al_ref[...])
        dgv = dg_ref[...]
        da = dgv * nexp * _sigmoid(xa)
        dba_ref[:, 0:A_HEADS] = dbeta_ref[...] * beta * (1.0 - beta)
        dba_ref[:, A_HEADS:2 * A_HEADS] = da
        dal_ref[...] += jnp.sum(dgv * nexp * _softplus(xa), axis=0, keepdims=True)
        ddt_ref[...] += jnp.sum(da, axis=0, keepdims=True)

    return rowcall(
        body, name="prep_a_bwd", S=S, ts=ts,
        ins=[(qkv_raw, "row"), (qkv_raw, "prev"), (qkv_raw, "next"), (ba, "row"), (conv_a, "vec"), (a_log, "vec"),
             (dt_bias, "vec"), (dq, "row"), (dq, "next"), (dk, "row"), (dk, "next"), (dv, "row"), (dv, "next"),
             (dbeta, "row"), (dg, "row")],
        outs=[((S, 3 * A_W), BF16, "row"), ((S, 2 * A_HEADS), F32, "row"), ((A_CONV, 3 * A_W), F32, "acc"),
              ((1, A_HEADS), F32, "acc"), ((1, A_HEADS), F32, "acc")])


def grad_x_final(dh1, x, dxpre1, mod):
    S = x.shape[0]

    def body(dh_ref, x_ref, dx_ref, m_ref, gx_ref, dscale_ref, dshift_ref):
        dh = dh_ref[...]
        gx_ref[...] = ALPHA * dx_ref[...] + dh * (1.0 + m_ref[...][SCALE_T:SCALE_T + 1])
        dscale_ref[...] += jnp.sum(dh * x_ref[...], axis=0, keepdims=True)
        dshift_ref[...] += jnp.sum(dh, axis=0, keepdims=True)

    vec = ((1, D_MODEL), F32, "acc")
    return rowcall(body, name="grad_x_final", S=S, ts=512, ins=[(dh1, "row"), (x, "row"), (dxpre1, "row"), (mod, "vec")],
                   outs=[((S, D_MODEL), F32, "row"), vec, vec])


_C_QKV, _C_Z, _C_BA, _C_QKVB, _C_G = 0, 3 * A_W, 4 * A_W, 4 * A_W + 2 * A_HEADS, 4 * A_W + 2 * A_HEADS + 3 * B_W
BA_PAD = 128


def split_w_in(w_in):
    ba = jnp.pad(w_in[:, _C_BA:_C_QKVB], ((0, 0), (0, BA_PAD - 2 * A_HEADS)))
    return dict(qkv=w_in[:, _C_QKV:_C_Z], z=w_in[:, _C_Z:_C_BA], ba=ba, qkvb=w_in[:, _C_QKVB:_C_G], g=w_in[:, _C_G:])


def join_w_in(p):
    return jnp.concatenate([p["qkv"], p["z"], p["ba"][:, :2 * A_HEADS], p["qkvb"], p["g"]], axis=1)


def forward_local(x, target, mod, w, sm):
    h1 = modulate(x, mod, SHIFT_T, SCALE_T, "mod_t")
    qkv_raw = mm(h1, w["qkv"], mode="nn", out_dtype=F32, name="proj_qkv")
    z = mm(h1, w["z"], mode="nn", out_dtype=F32, name="proj_z")
    ba = mm(h1, w["ba"], mode="nn", out_dtype=F32, name="proj_ba")
    qkvb = mm(h1, w["qkvb"], mode="nn", out_dtype=BF16, name="proj_qkvb")
    gates_raw = mm(h1, w["g"], mode="nn", out_dtype=F32, name="proj_g")
    q, k, v, beta, g = prep_a_fwd(qkv_raw, ba, sm["conv_a"], sm["a_log"], sm["dt_bias"])
    o_pre, sprev, tinv = delta_fwd(q, k, v, beta, g)
    o_a = gate_a_fwd(o_pre, z, sm["norm_a"])
    qkv_pad = jnp.pad(qkvb, ((PAD_ROWS, 0), (0, 0)))
    bias = jnp.transpose(bias_expand(sm["rel_bias"]), (1, 0, 2))
    o_b = attn_fwd(qkv_pad, bias)
    ya = mm(o_a, w["branch_a"], mode="nn", out_dtype=F32, name="branch_a")
    yb = mm(o_b, w["branch_b"], mode="nn", out_dtype=F32, name="branch_b")
    merged = merge_fwd(gates_raw, sm["b_gate"], ya, yb)
    mix = mm(merged, w["o"], mode="nn", out_dtype=F32, name="mix")
    xpre1, x1, h2 = ln1_fwd(x, mix, mod, sm["ln1_g"], sm["ln1_b"])
    up = mm(h2, w["up"], mode="nn", out_dtype=F32, name="ffn_up")
    act = ffn_act_fwd(up, sm["conv_ffn"], sm["b_conv_ffn"])
    ffn = mm(act, w["down"], mode="nn", out_dtype=F32, name="ffn_down", tk=D_FF // 2)
    dxpre2, dffn, loss, dgate_f, dln2_g, dln2_b = final_fwd_bwd(x1, ffn, target, mod, sm["ln2_g"], sm["ln2_b"])
    saved = dict(h1=h1, qkv_raw=qkv_raw, z=z, ba=ba, gates_raw=gates_raw, q=q, k=k, v=v, beta=beta, g=g,
                 o_pre=o_pre, sprev=sprev, tinv=tinv, o_a=o_a, qkv_pad=qkv_pad, bias=bias, o_b=o_b, ya=ya, yb=yb,
                 merged=merged, mix=mix, xpre1=xpre1, x1=x1, h2=h2, up=up, act=act, ffn=ffn)
    return loss, dxpre2, dffn, dict(gate_f=dgate_f, ln2_g=dln2_g, ln2_b=dln2_b), saved


def backward_local(x, mod, w, sm, dxpre2, dffn, fin, sv):
    half_ff = D_FF // 2
    dact = mm(dffn, w["down"], mode="nt", out_dtype=F32, name="d_act", tn=half_ff)
    gw_down = mm(sv["act"], dffn, mode="tn", out_dtype=BF16, name="gw_down", tm=half_ff)
    dup, dconv_ffn, db_conv_ffn = ffn_act_bwd(dact, sv["up"], sm["conv_ffn"], sm["b_conv_ffn"])
    dh2 = mm(dup, w["up"], mode="nt", out_dtype=F32, name="d_h2", tk=half_ff)
    gw_up = mm(sv["h2"], dup, mode="tn", out_dtype=BF16, name="gw_up")
    dxpre1, dmix, dsc_f, dsh_f, dgate_t, dln1_g, dln1_b = ln1_bwd(
        dxpre2, dh2, sv["xpre1"], sv["mix"], mod, sm["ln1_g"], sm["ln1_b"])
    dmerged = mm(dmix, w["o"], mode="nt", out_dtype=F32, name="d_merged")
    gw_o = mm(sv["merged"], dmix, mode="tn", out_dtype=BF16, name="gw_o")
    dya, dyb, dgates, db_gate = merge_bwd(dmerged, sv["gates_raw"], sm["b_gate"], sv["ya"], sv["yb"])
    do_a = mm(dya, w["branch_a"], mode="nt", out_dtype=F32, name="d_oa")
    gw_branch_a = mm(sv["o_a"], dya, mode="tn", out_dtype=BF16, name="gw_branch_a")
    do_b = mm(dyb, w["branch_b"], mode="nt", out_dtype=BF16, name="d_ob")
    gw_branch_b = mm(sv["o_b"], dyb, mode="tn", out_dtype=BF16, name="gw_branch_b")
    dq_b, dk_pad, dv_pad, dbias = attn_bwd(sv["qkv_pad"], sv["bias"], do_b)
    dqkvb = jnp.concatenate([dq_b, dk_pad[PAD_ROWS:].astype(BF16), dv_pad[PAD_ROWS:].astype(BF16)], axis=1)
    drel_bias = bias_reduce(jnp.transpose(dbias, (1, 0, 2)))
    do_pre, dz, dnorm_a = gate_a_bwd(do_a, sv["o_pre"], sv["z"], sm["norm_a"])
    dq, dk, dv, dbeta, dg = delta_bwd(sv["q"], sv["k"], sv["v"], sv["beta"], sv["g"], sv["sprev"], sv["tinv"], do_pre)
    dqkv_raw, dba16, dconv_a, da_log, ddt_bias = prep_a_bwd(
        sv["qkv_raw"], sv["ba"], sm["conv_a"], sm["a_log"], sm["dt_bias"], dq, dk, dv, dbeta, dg)
    dba = jnp.pad(dba16, ((0, 0), (0, BA_PAD - 2 * A_HEADS))).astype(BF16)
    pieces = dict(qkv=dqkv_raw, z=dz, ba=dba, qkvb=dqkvb, g=dgates)
    dh1 = None
    gw_in = {}
    for key, dpiece in pieces.items():
        dh1 = mm(dpiece, w[key], mode="nt", out_dtype=F32, name="d_h1_" + key, acc_in=dh1)
        gw_in[key] = mm(sv["h1"], dpiece, mode="tn", out_dtype=BF16, name="gw_in_" + key)
    grad_x, dsc_t, dsh_t = grad_x_final(dh1, x, dxpre1, mod)
    dmod = jnp.concatenate([dsh_t, dsc_t, dgate_t, dsh_f, dsc_f, fin["gate_f"]], axis=0)
    gw = dict(w_in=join_w_in(gw_in), w_branch_a=gw_branch_a, w_branch_b=gw_branch_b, w_o=gw_o, w_up=gw_up, w_down=gw_down)
    gs = dict(b_gate=db_gate, conv_a=dconv_a, a_log=da_log, dt_bias=ddt_bias, norm_a=dnorm_a, rel_bias=drel_bias,
              ln1_g=dln1_g, ln1_b=dln1_b, conv_ffn=dconv_ffn, b_conv_ffn=db_conv_ffn, ln2_g=fin["ln2_g"], ln2_b=fin["ln2_b"])
    return grad_x, dmod, gw, gs


MESH = pl.DeviceIdType.MESH
ANY = pl.BlockSpec(memory_space=pl.ANY)
WHOLE_VMEM = pl.BlockSpec(memory_space=pltpu.VMEM)


def _place():
    return lax.axis_index("x"), lax.axis_index("y"), lax.axis_index("c")


def allgather8(blk, name):
    m_per, n = blk.shape

    def body(x_ref, out_ref, send_sems, recv_sems, local_sem):
        x, y, c = _place()
        me, sibling = (x, y, c), (x, y, 1 - c)
        chips = [(1 - x, y), (x, 1 - y), (1 - x, 1 - y)]

        def rows(px, py, pc):
            return out_ref.at[pl.ds((4 * px + 2 * py + pc) * m_per, m_per), :]

        def copy(k, block, to, src=None):
            return pltpu.make_async_remote_copy(
                src_ref=rows(*block) if src is None else src, dst_ref=rows(*block),
                send_sem=send_sems.at[k], recv_sem=recv_sems.at[k], device_id=to, device_id_type=MESH)

        mine = pltpu.make_async_copy(x_ref, rows(*me), local_sem)
        mine.start()
        first = [copy(0, me, sibling, src=x_ref)]
        first += [copy(1 + j, me, (*chip, c), src=x_ref) for j, chip in enumerate(chips)]
        for cp in first:
            cp.start()
        passed = [copy(4 + j, (*chip, c), sibling) for j, chip in enumerate(chips)]
        for j, chip in enumerate(chips):
            copy(1 + j, (*chip, c), me).wait_recv()
            passed[j].start()
        copy(0, sibling, me).wait_recv()
        for j, chip in enumerate(chips):
            copy(4 + j, (*chip, 1 - c), me).wait_recv()
        for cp in first + passed:
            cp.wait_send()
        mine.wait()

    return pl.pallas_call(
        body, name=name, out_shape=jax.ShapeDtypeStruct((N_DEV * m_per, n), blk.dtype),
        in_specs=[WHOLE_VMEM], out_specs=WHOLE_VMEM,
        scratch_shapes=[pltpu.SemaphoreType.DMA((7,)), pltpu.SemaphoreType.DMA((7,)), pltpu.SemaphoreType.DMA],
    )(blk)


def _chip_peers(x, y):
    return [(1 - x, y), (x, 1 - y), (1 - x, 1 - y)]


def gather_chips(shards, name):
    n = len(shards)

    def body(*refs):
        ins, outs = refs[:n], refs[n:2 * n]
        send_sems, recv_sems, local_sems = refs[2 * n:]
        x, y, c = _place()
        me = 2 * x + y
        peers = _chip_peers(x, y)
        sends, local = [], []
        for a in range(n):
            lc = pltpu.make_async_copy(ins[a], outs[a].at[me], local_sems.at[a])
            lc.start()
            local.append(lc)
            for j, (px, py) in enumerate(peers):
                cp = pltpu.make_async_remote_copy(
                    src_ref=ins[a], dst_ref=outs[a].at[me], send_sem=send_sems.at[3 * a + j],
                    recv_sem=recv_sems.at[3 * a + j], device_id=(px, py, c), device_id_type=MESH)
                cp.start()
                sends.append(cp)
        for a in range(n):
            for j, (px, py) in enumerate(peers):
                pltpu.make_async_remote_copy(
                    src_ref=ins[a], dst_ref=outs[a].at[2 * px + py], send_sem=send_sems.at[3 * a + j],
                    recv_sem=recv_sems.at[3 * a + j], device_id=(px, py, c), device_id_type=MESH).wait_recv()
        for cp in sends:
            cp.wait_send()
        for lc in local:
            lc.wait()

    return pl.pallas_call(
        body, name=name, out_shape=[jax.ShapeDtypeStruct((N_CHIPS,) + s.shape, s.dtype) for s in shards],
        in_specs=[ANY] * n, out_specs=[ANY] * n,
        scratch_shapes=[pltpu.SemaphoreType.DMA((3 * n,)), pltpu.SemaphoreType.DMA((3 * n,)), pltpu.SemaphoreType.DMA((n,))],
    )(*shards)


def sibling_exchange(arrs, name):
    n = len(arrs)

    def body(*refs):
        ins, outs = refs[:n], refs[n:2 * n]
        send_sems, recv_sems = refs[2 * n:]
        x, y, c = _place()
        cps = [pltpu.make_async_remote_copy(src_ref=ins[a], dst_ref=outs[a], send_sem=send_sems.at[a],
                                            recv_sem=recv_sems.at[a], device_id=(x, y, 1 - c), device_id_type=MESH)
               for a in range(n)]
        for cp in cps:
            cp.start()
        for cp in cps:
            cp.wait()

    return pl.pallas_call(
        body, name=name, out_shape=[jax.ShapeDtypeStruct(a.shape, a.dtype) for a in arrs],
        in_specs=[ANY] * n, out_specs=[ANY] * n,
        scratch_shapes=[pltpu.SemaphoreType.DMA((n,)), pltpu.SemaphoreType.DMA((n,))],
    )(*arrs)


def scatter_chips(parts, name):
    n = len(parts)

    def body(*refs):
        ins, outs = refs[:n], refs[n:2 * n]
        send_sems, recv_sems, local_sems = refs[2 * n:]
        x, y, c = _place()
        me = 2 * x + y
        peers = _chip_peers(x, y)
        sends, local = [], []
        for a in range(n):
            lc = pltpu.make_async_copy(ins[a].at[me], outs[a].at[me], local_sems.at[a])
            lc.start()
            local.append(lc)
            for j, (px, py) in enumerate(peers):
                cp = pltpu.make_async_remote_copy(
                    src_ref=ins[a].at[2 * px + py], dst_ref=outs[a].at[me], send_sem=send_sems.at[3 * a + j],
                    recv_sem=recv_sems.at[3 * a + j], device_id=(px, py, c), device_id_type=MESH)
                cp.start()
                sends.append(cp)
        for a in range(n):
            for j, (px, py) in enumerate(peers):
                pltpu.make_async_remote_copy(
                    src_ref=ins[a].at[me], dst_ref=outs[a].at[2 * px + py], send_sem=send_sems.at[3 * a + j],
                    recv_sem=recv_sems.at[3 * a + j], device_id=(px, py, c), device_id_type=MESH).wait_recv()
        for cp in sends:
            cp.wait_send()
        for lc in local:
            lc.wait()

    return pl.pallas_call(
        body, name=name, out_shape=[jax.ShapeDtypeStruct(p.shape, p.dtype) for p in parts],
        in_specs=[ANY] * n, out_specs=[ANY] * n,
        scratch_shapes=[pltpu.SemaphoreType.DMA((3 * n,)), pltpu.SemaphoreType.DMA((3 * n,)), pltpu.SemaphoreType.DMA((n,))],
    )(*parts)


TILE_BYTES = 2 * 1024 * 1024


def _row_tile(rows, row_bytes):
    if rows * row_bytes <= TILE_BYTES or rows % 8:
        return rows
    best = 8
    for t in range(8, rows + 1, 8):
        if rows % t == 0 and t * row_bytes <= TILE_BYTES:
            best = t
    return best


def pair_add(a, b, name):
    R, C = a.shape
    tr = _row_tile(R, C * 4)

    def body(a_ref, b_ref, o_ref):
        o_ref[...] = (a_ref[...].astype(F32) + b_ref[...].astype(F32)).astype(BF16)

    spec = pl.BlockSpec((tr, C), lambda i: (i, 0))
    return pl.pallas_call(body, name=name, grid=(R // tr,), in_specs=[spec, spec], out_specs=spec,
                          out_shape=jax.ShapeDtypeStruct((R, C), BF16), compiler_params=_cparams(("parallel",)))(a, b)


def sum_lead(parts, name):
    K, R, C = parts.shape
    tr = _row_tile(R, C * 4)

    def body(p_ref, o_ref):
        acc = p_ref[0].astype(F32)
        for j in range(1, K):
            acc = acc + p_ref[j].astype(F32)
        o_ref[...] = acc

    return pl.pallas_call(
        body, name=name, grid=(R // tr,), in_specs=[pl.BlockSpec((K, tr, C), lambda i: (0, i, 0))],
        out_specs=pl.BlockSpec((tr, C), lambda i: (i, 0)), out_shape=jax.ShapeDtypeStruct((R, C), F32),
        compiler_params=_cparams(("parallel",)))(parts)


def adamw(w, g, m, v, name):
    R, C = w.shape
    tr = _row_tile(R, C * 4)

    def body(w_ref, g_ref, m_ref, v_ref, d_ref, mo_ref, vo_ref):
        gv = g_ref[...]
        m2 = ADAM_B1 * m_ref[...] + (1.0 - ADAM_B1) * gv
        v2 = ADAM_B2 * v_ref[...] + (1.0 - ADAM_B2) * (gv * gv)
        m_hat = m2 / (1.0 - ADAM_B1 ** ADAM_STEP)
        v_hat = v2 / (1.0 - ADAM_B2 ** ADAM_STEP)
        d_ref[...] = -ADAM_LR * (m_hat / (jnp.sqrt(v_hat) + ADAM_EPS) + ADAM_WD * w_ref[...])
        mo_ref[...] = m2
        vo_ref[...] = v2

    spec = pl.BlockSpec((tr, C), lambda i: (i, 0))
    return pl.pallas_call(body, name=name, grid=(R // tr,), in_specs=[spec] * 4, out_specs=[spec] * 3,
                          out_shape=[jax.ShapeDtypeStruct((R, C), F32)] * 3, compiler_params=_cparams(("parallel",)))(w, g, m, v)


LANES = 1024


def _pack(arrs, rows):
    out, offs, r = [], [], 0
    for a in arrs:
        flat = a.reshape(-1)
        nr = -(-flat.shape[0] // LANES)
        out.append(jnp.pad(flat, (0, nr * LANES - flat.shape[0])))
        offs.append(r)
        r += nr
    assert r <= rows, (r, rows)
    out.append(jnp.zeros(((rows - r) * LANES,), F32))
    return jnp.concatenate(out).reshape(rows, LANES), offs


def _unpack(packed, offs, shapes):
    flat = packed.reshape(-1)
    return [flat[o * LANES:o * LANES + math.prod(s)].reshape(s) for o, s in zip(offs, shapes)]


WEIGHTS = ["w_ada", "b_ada", "w_in", "b_gate", "conv_a", "a_log", "dt_bias", "norm_a", "rel_bias", "w_branch_a",
           "w_branch_b", "w_o", "ln1_g", "ln1_b", "w_up", "conv_ffn", "b_conv_ffn", "w_down", "ln2_g", "ln2_b"]
BIG = ["w_in", "w_branch_a", "w_branch_b", "w_o", "w_up", "w_down"]
COL_SHARDED = {"w_in", "w_up"}
SMALL_SHARDED = {"conv_a": 3 * A_W // N_CHIPS, "rel_bias": B_REL // N_CHIPS, "conv_ffn": 2 * D_FF // N_CHIPS}
SMALL = [n for n in WEIGHTS if n not in BIG and n != "w_ada"]


def _to_full(g4, name):
    if name in COL_SHARDED:
        return jnp.transpose(g4, (1, 0, 2)).reshape(g4.shape[1], -1)
    return g4.reshape(-1, g4.shape[2])


def _to_shards(full, name):
    if name in COL_SHARDED:
        return jnp.transpose(full.reshape(full.shape[0], N_CHIPS, -1), (1, 0, 2))
    return full.reshape(N_CHIPS, -1, full.shape[1])


def kernel(x, c, w_ada, b_ada, w_in, b_gate, conv_a, a_log, dt_bias, norm_a, rel_bias, w_branch_a, w_branch_b, w_o, ln1_g, ln1_b, w_up, conv_ffn, b_conv_ffn, w_down, ln2_g, ln2_b, loss_target, m_w_ada, m_b_ada, m_w_in, m_b_gate, m_conv_a, m_a_log, m_dt_bias, m_norm_a, m_rel_bias, m_w_branch_a, m_w_branch_b, m_w_o, m_ln1_g, m_ln1_b, m_w_up, m_conv_ffn, m_b_conv_ffn, m_w_down, m_ln2_g, m_ln2_b, v_w_ada, v_b_ada, v_w_in, v_b_gate, v_conv_a, v_a_log, v_dt_bias, v_norm_a, v_rel_bias, v_w_branch_a, v_w_branch_b, v_w_o, v_ln1_g, v_ln1_b, v_w_up, v_conv_ffn, v_b_conv_ffn, v_w_down, v_ln2_g, v_ln2_b):
    args = dict(locals())
    wts = {n: args[n] for n in WEIGHTS}
    moms = {n: args["m_" + n] for n in WEIGHTS}
    vars_ = {n: args["v_" + n] for n in WEIGHTS}
    xi, yi, ci = _place()
    chip = 2 * xi + yi
    dev = 4 * xi + 2 * yi + ci
    ada_cols = w_ada.shape[2]

    c_all = allgather8(jnp.pad(c, ((0, 7), (0, 0))), "gather_c").reshape(N_DEV, 8, D_MODEL)[:, 0]
    b_ada_sh = lax.dynamic_slice(b_ada, (0, chip * ada_cols), (1, ada_cols))
    mod_sh = ada_fwd(c_all, w_ada[0], b_ada_sh)
    mod_g = allgather8(mod_sh, "gather_mod").reshape(N_CHIPS, 2, N_DEV, ada_cols)[:, 0]
    mod = lax.dynamic_slice(mod_g, (0, dev, 0), (N_CHIPS, 1, ada_cols)).reshape(6, D_MODEL)

    big_full = gather_chips([wts[n][0].astype(BF16) for n in BIG], "gather_weights")
    w = {n: _to_full(g4, n) for n, g4 in zip(BIG, big_full)}
    wd = dict(split_w_in(w["w_in"]), branch_a=w["w_branch_a"], branch_b=w["w_branch_b"], o=w["w_o"], up=w["w_up"], down=w["w_down"])
    sshapes = [wts[n].shape[1:] for n in SMALL_SHARDED]
    spack, soffs = _pack([wts[n][0] for n in SMALL_SHARDED], 16)
    sg = allgather8(spack, "gather_small_w").reshape(N_CHIPS, 2, 16, LANES)[:, 0]
    sparts = [_unpack(sg[j], soffs, sshapes) for j in range(N_CHIPS)]
    sm = {n: wts[n] for n in SMALL if n not in SMALL_SHARDED and n != "b_ada"}
    for i, n in enumerate(SMALL_SHARDED):
        sm[n] = jnp.concatenate([sparts[j][i] for j in range(N_CHIPS)], axis=-1)

    loss, dxpre2, dffn, fin, sv = forward_local(x[0], loss_target[0], mod, wd, sm)
    grad_x, dmod, gw, gs = backward_local(x[0], mod, wd, sm, dxpre2, dffn, fin, sv)

    gnames = [n for n in SMALL if n != "b_ada"]
    vec, voffs = _pack([dmod] + [gs[n] for n in gnames] + [loss], 56)
    gathered = allgather8(vec, "gather_small_g").reshape(N_DEV, 56, LANES)
    summed = sum_lead(gathered, "sum_small_g")
    full_shapes = [(6, D_MODEL)] + [gs[n].shape for n in gnames] + [(1, 1)]
    parts = _unpack(summed, voffs, full_shapes)
    grads = {"b_ada": parts[0].reshape(1, -1)}
    for n, p in zip(gnames, parts[1:-1]):
        if n in SMALL_SHARDED:
            p = lax.dynamic_slice_in_dim(p, chip * SMALL_SHARDED[n], SMALL_SHARDED[n], axis=1)
        grads[n] = p.reshape(wts[n].shape)
    loss_total = parts[-1].reshape(())
    dmod_all = gathered[:, 0:6, :].reshape(N_DEV, 6 * D_MODEL)
    grads["w_ada"] = ada_bwd(c_all, lax.dynamic_slice(dmod_all, (0, chip * ada_cols), (N_DEV, ada_cols)))[None]

    mine = [gw[n] for n in BIG]
    theirs = sibling_exchange(mine, "grad_sibling")
    chip_sums = [_to_shards(pair_add(a, b, "grad_pair_" + n), n) for n, a, b in zip(BIG, mine, theirs)]
    received = scatter_chips(chip_sums, "grad_scatter")
    for n, r in zip(BIG, received):
        grads[n] = sum_lead(r, "grad_sum_" + n)[None]

    delta, new_m, new_v = {}, {}, {}
    for n in ["w_ada"] + BIG:
        d, m2, v2 = adamw(wts[n][0], grads[n][0], moms[n][0], vars_[n][0], "adamw_" + n)
        delta[n], new_m[n], new_v[n] = d[None], m2[None], v2[None]
    shapes = [wts[n].shape for n in SMALL]
    packs = [_pack([t[n] for n in SMALL], 32) for t in (wts, grads, moms, vars_)]
    outs = adamw(*[p[0] for p in packs], "adamw_small")
    for res, o in zip((delta, new_m, new_v), outs):
        for n, a in zip(SMALL, _unpack(o, packs[0][1], shapes)):
            res[n] = a
    return (loss_total, grad_x[None], *[grads[n] for n in WEIGHTS], *[delta[n] for n in WEIGHTS],
            *[new_m[n] for n in WEIGHTS], *[new_v[n] for n in WEIGHTS])
```

```python
import functools
import math

import jax
import jax.numpy as jnp
from jax import lax
from jax.experimental import pallas as pl
from jax.experimental.pallas import tpu as pltpu

F32 = jnp.float32
BF16 = jnp.bfloat16

D_MODEL = 1024
CHUNK = 64
A_HEADS = 8
A_DK = 128
A_W = A_HEADS * A_DK
A_CONV = 4
B_HEADS = 16
B_DH = 64
B_W = B_HEADS * B_DH
B_PREV = 8
B_BAND = (B_PREV + 1) * CHUNK
B_MAX_REL = 256
B_REL = CHUNK - 1 + B_MAX_REL + 1
D_FF = 2816
FFN_CONV = 3
IN_COLS = 4 * A_W + 2 * A_HEADS + 3 * B_W + 2 * D_MODEL
ALPHA = 2.0 ** 0.25
LN_EPS = 1e-5
RMS_EPS = 1e-6
L2_EPS = 1e-6
NEG_INF = -1e30
ADAM_LR, ADAM_B1, ADAM_B2, ADAM_EPS, ADAM_WD, ADAM_STEP = 0.001, 0.9, 0.999, 1e-08, 0.01, 10
N_CHIPS = 4
N_DEV = 8
VMEM_LIMIT = 56 * 1024 * 1024


def _cparams(sem=None):
    return pltpu.CompilerParams(dimension_semantics=sem, vmem_limit_bytes=VMEM_LIMIT)


_DIMS = {"nn": (((1,), (0,)), ((), ())), "nt": (((1,), (1,)), ((), ())), "tn": (((0,), (0,)), ((), ()))}


def mm(a, b, *, mode, out_dtype, name, tm=1024, tn=512, tk=1024, acc_in=None):
    if mode == "nn":
        (M, K), (K2, N) = a.shape, b.shape
    elif mode == "nt":
        (M, K), (N, K2) = a.shape, b.shape
    else:
        (K, M), (K2, N) = a.shape, b.shape
    assert K == K2, (a.shape, b.shape, mode)
    tm, tn, tk = min(tm, M), min(tn, N), min(tk, K)
    assert M % tm == 0 and N % tn == 0 and K % tk == 0, (M, N, K, tm, tn, tk)
    nk = K // tk

    def body(*refs):
        if acc_in is None:
            a_ref, b_ref, o_ref, acc_ref = refs
        else:
            a_ref, b_ref, c_ref, o_ref, acc_ref = refs
        k = pl.program_id(2)

        @pl.when(k == 0)
        def _():
            if acc_in is None:
                acc_ref[...] = jnp.zeros_like(acc_ref)
            else:
                acc_ref[...] = c_ref[...]

        acc_ref[...] += lax.dot_general(a_ref[...].astype(BF16), b_ref[...].astype(BF16), _DIMS[mode],
                                        preferred_element_type=F32)

        @pl.when(k == nk - 1)
        def _():
            o_ref[...] = acc_ref[...].astype(out_dtype)

    a_spec = pl.BlockSpec((tk, tm), lambda i, j, k: (k, i)) if mode == "tn" else pl.BlockSpec((tm, tk), lambda i, j, k: (i, k))
    b_spec = pl.BlockSpec((tn, tk), lambda i, j, k: (j, k)) if mode == "nt" else pl.BlockSpec((tk, tn), lambda i, j, k: (k, j))
    o_spec = pl.BlockSpec((tm, tn), lambda i, j, k: (i, j))
    ins, in_specs, aliases = [a, b], [a_spec, b_spec], {}
    if acc_in is not None:
        assert acc_in.shape == (M, N) and acc_in.dtype == F32 and out_dtype == F32
        ins.append(acc_in)
        in_specs.append(o_spec)
        aliases = {2: 0}
    return pl.pallas_call(
        body, name=name, grid=(M // tm, N // tn, nk), in_specs=in_specs, out_specs=o_spec,
        out_shape=jax.ShapeDtypeStruct((M, N), out_dtype), scratch_shapes=[pltpu.VMEM((tm, tn), F32)],
        input_output_aliases=aliases, compiler_params=_cparams(("parallel", "parallel", "arbitrary")),
    )(*ins)


def rowcall(body, *, name, S, ts, ins, outs):
    assert S % ts == 0 and ts % 8 == 0
    nsteps, r8 = S // ts, ts // 8
    last8 = S // 8 - 1
    in_specs, arrays = [], []
    for arr, kind in ins:
        arrays.append(arr)
        if kind == "row":
            in_specs.append(pl.BlockSpec((ts, arr.shape[1]), lambda i: (i, 0)))
        elif kind == "prev":
            in_specs.append(pl.BlockSpec((8, arr.shape[1]), lambda i: (jnp.maximum(i * r8 - 1, 0), 0)))
        elif kind == "next":
            in_specs.append(pl.BlockSpec((8, arr.shape[1]), lambda i: (jnp.minimum((i + 1) * r8, last8), 0)))
        else:
            nd = arr.ndim
            in_specs.append(pl.BlockSpec(arr.shape, lambda i, nd=nd: (0,) * nd))
    out_specs, out_shapes, acc_idx = [], [], []
    for n, (shape, dtype, kind) in enumerate(outs):
        out_shapes.append(jax.ShapeDtypeStruct(shape, dtype))
        if kind == "row":
            out_specs.append(pl.BlockSpec((ts, shape[1]), lambda i: (i, 0)))
        else:
            nd = len(shape)
            out_specs.append(pl.BlockSpec(shape, lambda i, nd=nd: (0,) * nd))
            acc_idx.append(n)
    n_in = len(arrays)

    def wrapped(*refs):
        @pl.when(pl.program_id(0) == 0)
        def _():
            for n in acc_idx:
                refs[n_in + n][...] = jnp.zeros_like(refs[n_in + n])

        body(*refs)

    res = pl.pallas_call(
        wrapped, name=name, grid=(nsteps,), in_specs=in_specs, out_specs=out_specs, out_shape=out_shapes,
        compiler_params=_cparams(("arbitrary",) if acc_idx else ("parallel",)),
    )(*arrays)
    return res


def _shift_down(cur, prev8, k):
    if k == 0:
        return cur
    rolled = pltpu.roll(cur, k, axis=0)
    fix = pltpu.roll(prev8, k, axis=0)
    row = lax.broadcasted_iota(jnp.int32, (8, 1), 0)
    top = jnp.where(row < k, fix, rolled[0:8])
    if cur.shape[0] == 8:
        return top
    return jnp.concatenate([top, rolled[8:]], axis=0)


def _shift_up(cur, next8, k):
    if k == 0:
        return cur
    n = cur.shape[0]
    rolled = pltpu.roll(cur, n - k, axis=0)
    fix = pltpu.roll(next8, 8 - k, axis=0)
    row = lax.broadcasted_iota(jnp.int32, (8, 1), 0)
    bot = jnp.where(row >= 8 - k, fix, rolled[n - 8:n])
    return jnp.concatenate([rolled[:n - 8], bot], axis=0)


def _sigmoid(x):
    return 1.0 / (1.0 + jnp.exp(-x))


def _silu(x):
    return x * _sigmoid(x)


def _dsilu(x):
    s = _sigmoid(x)
    return s * (1.0 + x * (1.0 - s))


def _softplus(x):
    return jnp.maximum(x, 0.0) + jnp.log1p(jnp.exp(-jnp.abs(x)))


def _split2(x):
    hi = x.astype(BF16)
    return hi, (x - hi.astype(F32)).astype(BF16)


def _dot1(a, b, mode):
    return lax.dot_general(a.astype(BF16), b.astype(BF16), _DIMS[mode], preferred_element_type=F32)


def _dot3(a, b, mode):
    ah, al = _split2(a)
    bh, bl = _split2(b)
    d = lambda p, q: lax.dot_general(p, q, _DIMS[mode], preferred_element_type=F32)
    return d(ah, bh) + (d(ah, bl) + d(al, bh))


def ada_fwd(c_all, w_sh, b_sh):
    n = w_sh.shape[1]
    tn = 512

    def body(c_ref, w_ref, b_ref, o_ref):
        o_ref[...] = _dot1(_silu(c_ref[...]), w_ref[...], "nn") + b_ref[...]

    return pl.pallas_call(
        body, name="ada_fwd", grid=(n // tn,),
        in_specs=[pl.BlockSpec((N_DEV, D_MODEL), lambda j: (0, 0)), pl.BlockSpec((D_MODEL, tn), lambda j: (0, j)),
                  pl.BlockSpec((1, tn), lambda j: (0, j))],
        out_specs=pl.BlockSpec((N_DEV, tn), lambda j: (0, j)), out_shape=jax.ShapeDtypeStruct((N_DEV, n), F32),
        compiler_params=_cparams(("parallel",)),
    )(c_all, w_sh, b_sh)


def ada_bwd(c_all, dmod_sh):
    n = dmod_sh.shape[1]
    tn = 512

    def body(c_ref, d_ref, o_ref):
        o_ref[...] = _dot1(_silu(c_ref[...]), d_ref[...], "tn")

    return pl.pallas_call(
        body, name="ada_bwd", grid=(n // tn,),
        in_specs=[pl.BlockSpec((N_DEV, D_MODEL), lambda j: (0, 0)), pl.BlockSpec((N_DEV, tn), lambda j: (0, j))],
        out_specs=pl.BlockSpec((D_MODEL, tn), lambda j: (0, j)), out_shape=jax.ShapeDtypeStruct((D_MODEL, n), F32),
        compiler_params=_cparams(("parallel",)),
    )(c_all, dmod_sh)


SHIFT_T, SCALE_T, GATE_T, SHIFT_F, SCALE_F, GATE_F = range(6)


def modulate(x, mod, shift_row, scale_row, name):
    S = x.shape[0]

    def body(x_ref, m_ref, o_ref):
        m = m_ref[...]
        o_ref[...] = (x_ref[...] * (1.0 + m[scale_row:scale_row + 1]) + m[shift_row:shift_row + 1]).astype(BF16)

    return rowcall(body, name=name, S=S, ts=512, ins=[(x, "row"), (mod, "vec")], outs=[((S, D_MODEL), BF16, "row")])[0]


def _conv_fwd(cur, prev, w, width):
    y = cur * w[width - 1:width]
    for j in range(width - 1):
        y = y + _shift_down(cur, prev, width - 1 - j) * w[j:j + 1]
    return y


def _prep_a_core(cur, prev, w):
    pre = _conv_fwd(cur, prev, w, A_CONV)
    y = _silu(pre)
    return pre, y


def prep_a_fwd(qkv_raw, ba, conv_a, a_log, dt_bias):
    S = qkv_raw.shape[0]

    def body(x_ref, xp_ref, ba_ref, w_ref, al_ref, dt_ref, q_ref, k_ref, v_ref, beta_ref, g_ref):
        first = (pl.program_id(0) > 0).astype(F32)
        _, y = _prep_a_core(x_ref[...], xp_ref[...] * first, w_ref[...])
        for h in range(A_HEADS):
            sl = slice(h * A_DK, (h + 1) * A_DK)
            qh = y[:, sl]
            kh = y[:, A_W + h * A_DK:A_W + (h + 1) * A_DK]
            q_ref[:, sl] = qh * (lax.rsqrt(jnp.sum(qh * qh, axis=-1, keepdims=True) + L2_EPS) * (A_DK ** -0.5))
            k_ref[:, sl] = kh * lax.rsqrt(jnp.sum(kh * kh, axis=-1, keepdims=True) + L2_EPS)
        v_ref[...] = y[:, 2 * A_W:3 * A_W]
        bav = ba_ref[...]
        beta_ref[...] = _sigmoid(bav[:, 0:A_HEADS])
        g_ref[...] = -jnp.exp(al_ref[...]) * _softplus(bav[:, A_HEADS:2 * A_HEADS] + dt_ref[...])

    return rowcall(
        body, name="prep_a_fwd", S=S, ts=256,
        ins=[(qkv_raw, "row"), (qkv_raw, "prev"), (ba, "row"), (conv_a, "vec"), (a_log, "vec"), (dt_bias, "vec")],
        outs=[((S, A_W), F32, "row")] * 3 + [((S, A_HEADS), F32, "row")] * 2)


HEAD_GROUP = 4
GROUP_ROWS = HEAD_GROUP * CHUNK
N_HEAD_GROUPS = A_HEADS // HEAD_GROUP
LOG_CHUNK = int(math.log2(CHUNK))


def _tri_masks():
    rb = lax.broadcasted_iota(jnp.int32, (GROUP_ROWS, GROUP_ROWS), 0)
    cb = lax.broadcasted_iota(jnp.int32, (GROUP_ROWS, GROUP_ROWS), 1)
    same = (rb >> LOG_CHUNK) == (cb >> LOG_CHUNK)
    return dict(causal=same & (rb >= cb), strict=same & (rb > cb), eye=rb == cb, upper=same & (cb >= rb),
                last=cb == (rb | (CHUNK - 1)), rb=rb, cb=cb)


def _col_to_row(colv, eye):
    return jnp.sum(jnp.where(eye, colv, 0.0), axis=0, keepdims=True)


def _row_to_col(rowv, eye):
    return jnp.sum(jnp.where(eye, rowv, 0.0), axis=1, keepdims=True)


def _tri_inv(a, mk):
    rb, cb = mk["rb"], mk["cb"]
    t = jnp.where(mk["eye"], 1.0, 0.0) - jnp.where((rb >> 1) == (cb >> 1), a, 0.0)
    for lvl in range(1, LOG_CHUNK):
        rs, cs = rb >> lvl, cb >> lvl
        off = jnp.where(((rs & 1) == 1) & (cs == rs - 1), a, 0.0)
        t = t - _dot3(_dot3(t, off, "nn"), t, "nn")
    return t


def _stack_heads(ref, grp):
    return jnp.concatenate([ref[:, (grp * HEAD_GROUP + j) * A_DK:(grp * HEAD_GROUP + j + 1) * A_DK]
                            for j in range(HEAD_GROUP)], axis=0)


def _stack_cols(tile, grp):
    return jnp.concatenate([tile[:, grp * HEAD_GROUP + j:grp * HEAD_GROUP + j + 1] for j in range(HEAD_GROUP)], axis=0)


def _delta_local(q, k, v, beta, g, mk):
    causal, strict, eye = mk["causal"], mk["strict"], mk["eye"]
    g_row = _col_to_row(g, eye)
    gc = jnp.sum(jnp.where(causal, g_row, 0.0), axis=1, keepdims=True)
    gc_row = _col_to_row(gc, eye)
    decay = jnp.where(causal, jnp.exp(jnp.where(causal, gc - gc_row, 0.0)), 0.0)
    gam = jnp.exp(gc)
    kb = k * beta
    vb = v * beta
    y = kb * gam
    a = jnp.where(strict, _dot1(kb, k, "nt") * decay, 0.0)
    p = jnp.where(causal, _dot1(q, k, "nt") * decay, 0.0)
    gl = jnp.sum(jnp.where(mk["last"], gc_row, 0.0), axis=1, keepdims=True)
    kd = k * jnp.exp(gl - gc)
    return dict(gc=gc, decay=decay, gam=gam, kb=kb, vb=vb, y=y, a=a, p=p, gl=gl, kd=kd)


def _head_rows(x, j):
    return x[j * CHUNK:(j + 1) * CHUNK]


def delta_fwd(q, k, v, beta, g):
    S = q.shape[0]
    n_chunks = S // CHUNK

    def body(q_ref, k_ref, v_ref, beta_ref, g_ref, o_ref, sprev_ref, t_ref, state_ref):
        @pl.when(pl.program_id(0) == 0)
        def _():
            state_ref[...] = jnp.zeros_like(state_ref)

        mk = _tri_masks()
        betav, gv = beta_ref[...], g_ref[...]
        for grp in range(N_HEAD_GROUPS):
            qs, ks, vs = _stack_heads(q_ref, grp), _stack_heads(k_ref, grp), _stack_heads(v_ref, grp)
            loc = _delta_local(qs, ks, vs, _stack_cols(betav, grp), _stack_cols(gv, grp), mk)
            tinv = _tri_inv(loc["a"], mk)
            t_ref[0, grp] = tinv
            uw = _dot3(tinv, jnp.concatenate([loc["vb"], loc["y"]], axis=1), "nn")
            qg = qs * loc["gam"]
            egl = jnp.exp(loc["gl"])
            vns, o_state = [], []
            for j in range(HEAD_GROUP):
                h = grp * HEAD_GROUP + j
                s0 = state_ref[h]
                sprev_ref[0, h] = s0
                uw_h = _head_rows(uw, j)
                vn = uw_h[:, :A_DK] - _dot1(uw_h[:, A_DK:], s0, "nn")
                vns.append(vn)
                o_state.append(_dot1(_head_rows(qg, j), s0, "nn"))
                state_ref[h] = s0 * egl[(j + 1) * CHUNK - 1:(j + 1) * CHUNK] + _dot1(_head_rows(loc["kd"], j), vn, "tn")
            o_local = _dot1(loc["p"], jnp.concatenate(vns, axis=0), "nn")
            for j in range(HEAD_GROUP):
                h = grp * HEAD_GROUP + j
                o_ref[:, h * A_DK:(h + 1) * A_DK] = o_state[j] + _head_rows(o_local, j)

    tile = pl.BlockSpec((CHUNK, A_W), lambda n: (n, 0))
    small = pl.BlockSpec((CHUNK, A_HEADS), lambda n: (n, 0))
    return pl.pallas_call(
        body, name="delta_fwd", grid=(n_chunks,), in_specs=[tile, tile, tile, small, small],
        out_specs=[tile, pl.BlockSpec((1, A_HEADS, A_DK, A_DK), lambda n: (n, 0, 0, 0)),
                   pl.BlockSpec((1, N_HEAD_GROUPS, GROUP_ROWS, GROUP_ROWS), lambda n: (n, 0, 0, 0))],
        out_shape=[jax.ShapeDtypeStruct((S, A_W), F32), jax.ShapeDtypeStruct((n_chunks, A_HEADS, A_DK, A_DK), F32),
                   jax.ShapeDtypeStruct((n_chunks, N_HEAD_GROUPS, GROUP_ROWS, GROUP_ROWS), F32)],
        scratch_shapes=[pltpu.VMEM((A_HEADS, A_DK, A_DK), F32)],
        compiler_params=_cparams(("arbitrary",)),
    )(q, k, v, beta, g)


def gate_a_fwd(o_pre, z, norm_w):
    S = o_pre.shape[0]

    def body(o_ref, z_ref, nw_ref, out_ref):
        nw = nw_ref[...]
        for h in range(A_HEADS):
            sl = slice(h * A_DK, (h + 1) * A_DK)
            oh = o_ref[:, sl]
            r = lax.rsqrt(jnp.mean(oh * oh, axis=-1, keepdims=True) + RMS_EPS)
            out_ref[:, sl] = (oh * r * nw * _silu(z_ref[:, sl])).astype(BF16)

    return rowcall(body, name="gate_a_fwd", S=S, ts=512, ins=[(o_pre, "row"), (z, "row"), (norm_w, "vec")],
                   outs=[((S, A_W), BF16, "row")])[0]


HEADS_PER_GROUP = 2
GROUP_W = HEADS_PER_GROUP * B_DH
N_GROUPS = B_HEADS // HEADS_PER_GROUP
PAD_ROWS = B_PREV * CHUNK


Q_TILE = 256
Q_CHUNKS = Q_TILE // CHUNK
KEY_WIN = (B_PREV + Q_CHUNKS) * CHUNK


def bias_tiles(bias):
    rows = [jnp.pad(bias, ((0, 0), (0, 0), (qc * CHUNK, (Q_CHUNKS - 1 - qc) * CHUNK)), constant_values=NEG_INF)
            for qc in range(Q_CHUNKS)]
    return jnp.concatenate(rows, axis=1)


def _band_probs(qh, kh, bias, valid):
    s = _dot1(qh, kh, "nt") * (B_DH ** -0.5) + bias
    s = jnp.where(valid, s, NEG_INF)
    e = jnp.exp(s - jnp.max(s, axis=-1, keepdims=True))
    return e / jnp.sum(e, axis=-1, keepdims=True)


def _attn_specs(S):
    n_cb = B_W // GROUP_W
    return [pl.BlockSpec((Q_TILE, GROUP_W), lambda g, n: (n + PAD_ROWS // Q_TILE, g)),
            pl.BlockSpec((PAD_ROWS + S, GROUP_W), lambda g, n: (0, n_cb + g)),
            pl.BlockSpec((PAD_ROWS + S, GROUP_W), lambda g, n: (0, 2 * n_cb + g)),
            pl.BlockSpec((HEADS_PER_GROUP, Q_TILE, KEY_WIN), lambda g, n: (g, 0, 0))]


def _key_valid(n):
    return lax.broadcasted_iota(jnp.int32, (Q_TILE, KEY_WIN), 1) >= PAD_ROWS - n * Q_TILE


def attn_fwd(qkv_pad, btile):
    S = qkv_pad.shape[0] - PAD_ROWS

    def body(q_ref, k_ref, v_ref, b_ref, o_ref):
        n = pl.program_id(1)
        start = pl.multiple_of(n * Q_TILE, Q_TILE)
        kb = k_ref[pl.ds(start, KEY_WIN), :]
        vb = v_ref[pl.ds(start, KEY_WIN), :]
        qv = q_ref[...]
        valid = _key_valid(n)
        outs = []
        for hh in range(HEADS_PER_GROUP):
            sl = slice(hh * B_DH, (hh + 1) * B_DH)
            p = _band_probs(qv[:, sl], kb[:, sl], b_ref[hh], valid)
            outs.append(_dot1(p, vb[:, sl], "nn"))
        o_ref[...] = jnp.concatenate(outs, axis=1).astype(BF16)

    return pl.pallas_call(
        body, name="attn_fwd", grid=(N_GROUPS, S // Q_TILE), in_specs=_attn_specs(S),
        out_specs=pl.BlockSpec((Q_TILE, GROUP_W), lambda g, n: (n, g)),
        out_shape=jax.ShapeDtypeStruct((S, B_W), BF16),
        compiler_params=_cparams(("parallel", "arbitrary")),
    )(qkv_pad, qkv_pad, qkv_pad, btile)


def _rel_onehot(i):
    kj = lax.broadcasted_iota(jnp.int32, (B_BAND, B_REL), 0)
    r = lax.broadcasted_iota(jnp.int32, (B_BAND, B_REL), 1)
    idx = jnp.clip(PAD_ROWS + i - kj, -(CHUNK - 1), B_MAX_REL) + (CHUNK - 1)
    return jnp.where(idx == r, 1.0, 0.0)


def bias_expand(rel_bias):
    def body(rb_ref, o_ref):
        i = pl.program_id(0)
        o_ref[0] = _dot3(rb_ref[...], _rel_onehot(i), "nt")

    return pl.pallas_call(
        body, name="bias_expand", grid=(CHUNK,),
        in_specs=[pl.BlockSpec((B_HEADS, B_REL), lambda i: (0, 0))],
        out_specs=pl.BlockSpec((1, B_HEADS, B_BAND), lambda i: (i, 0, 0)),
        out_shape=jax.ShapeDtypeStruct((CHUNK, B_HEADS, B_BAND), F32),
        compiler_params=_cparams(("parallel",)),
    )(rel_bias)


def bias_reduce(dbias):
    def body(d_ref, o_ref):
        i = pl.program_id(0)

        @pl.when(i == 0)
        def _():
            o_ref[...] = jnp.zeros_like(o_ref)

        o_ref[...] += _dot3(d_ref[0], _rel_onehot(i), "nn")

    return pl.pallas_call(
        body, name="bias_reduce", grid=(CHUNK,),
        in_specs=[pl.BlockSpec((1, B_HEADS, B_BAND), lambda i: (i, 0, 0))],
        out_specs=pl.BlockSpec((B_HEADS, B_REL), lambda i: (0, 0)),
        out_shape=jax.ShapeDtypeStruct((B_HEADS, B_REL), F32),
        compiler_params=_cparams(("arbitrary",)),
    )(dbias)


def merge_fwd(gates_raw, b_gate, ya, yb):
    S = ya.shape[0]

    def body(g_ref, b_ref, ya_ref, yb_ref, o_ref):
        gt = _sigmoid(g_ref[...] + b_ref[...])
        o_ref[...] = (gt[:, :D_MODEL] * ya_ref[...] + gt[:, D_MODEL:] * yb_ref[...]).astype(BF16)

    return rowcall(body, name="merge_fwd", S=S, ts=512,
                   ins=[(gates_raw, "row"), (b_gate, "vec"), (ya, "row"), (yb, "row")],
                   outs=[((S, D_MODEL), BF16, "row")])[0]


def _ln_stats(xpre):
    mu = jnp.mean(xpre, axis=-1, keepdims=True)
    xc = xpre - mu
    rstd = lax.rsqrt(jnp.mean(xc * xc, axis=-1, keepdims=True) + LN_EPS)
    return xc * rstd, rstd


def ln1_fwd(x, mix, mod, ln_g, ln_b):
    S = x.shape[0]

    def body(x_ref, mix_ref, m_ref, g_ref, b_ref, xpre_ref, x1_ref, h2_ref):
        m = m_ref[...]
        xpre = ALPHA * x_ref[...] + m[GATE_T:GATE_T + 1] * mix_ref[...]
        xhat, _ = _ln_stats(xpre)
        x1 = xhat * g_ref[...] + b_ref[...]
        xpre_ref[...] = xpre
        x1_ref[...] = x1
        h2_ref[...] = (x1 * (1.0 + m[SCALE_F:SCALE_F + 1]) + m[SHIFT_F:SHIFT_F + 1]).astype(BF16)

    return rowcall(body, name="ln1_fwd", S=S, ts=512,
                   ins=[(x, "row"), (mix, "row"), (mod, "vec"), (ln_g, "vec"), (ln_b, "vec")],
                   outs=[((S, D_MODEL), F32, "row"), ((S, D_MODEL), F32, "row"), ((S, D_MODEL), BF16, "row")])


def ffn_act_fwd(up, conv_w, conv_b):
    S = up.shape[0]

    def body(u_ref, up_ref, w_ref, b_ref, o_ref):
        first = (pl.program_id(0) > 0).astype(F32)
        uc = _conv_fwd(u_ref[...], up_ref[...] * first, w_ref[...], FFN_CONV) + b_ref[...]
        o_ref[...] = (_silu(uc[:, :D_FF]) * uc[:, D_FF:]).astype(BF16)

    return rowcall(body, name="ffn_act_fwd", S=S, ts=128,
                   ins=[(up, "row"), (up, "prev"), (conv_w, "vec"), (conv_b, "vec")],
                   outs=[((S, D_FF), BF16, "row")])[0]


def final_fwd_bwd(x1, ffn, target, mod, ln_g, ln_b):
    S = x1.shape[0]

    def body(x1_ref, f_ref, t_ref, m_ref, g_ref, b_ref, dxpre_ref, dffn_ref, loss_ref, dgate_ref, dg_ref, db_ref):
        gate = m_ref[...][GATE_F:GATE_F + 1]
        ffn_v = f_ref[...]
        xpre = ALPHA * x1_ref[...] + gate * ffn_v
        xhat, rstd = _ln_stats(xpre)
        err = xhat * g_ref[...] + b_ref[...] - t_ref[...]
        loss_ref[...] += 0.5 * jnp.sum(jnp.mean(err * err, axis=-1, keepdims=True), axis=0, keepdims=True)
        dy = err * (1.0 / D_MODEL)
        dg_ref[...] += jnp.sum(dy * xhat, axis=0, keepdims=True)
        db_ref[...] += jnp.sum(dy, axis=0, keepdims=True)
        dyg = dy * g_ref[...]
        dxpre = rstd * (dyg - jnp.mean(dyg, axis=-1, keepdims=True) - xhat * jnp.mean(dyg * xhat, axis=-1, keepdims=True))
        dxpre_ref[...] = dxpre
        dffn_ref[...] = (gate * dxpre).astype(BF16)
        dgate_ref[...] += jnp.sum(dxpre * ffn_v, axis=0, keepdims=True)

    vec = ((1, D_MODEL), F32, "acc")
    return rowcall(body, name="final_fwd_bwd", S=S, ts=512,
                   ins=[(x1, "row"), (ffn, "row"), (target, "row"), (mod, "vec"), (ln_g, "vec"), (ln_b, "vec")],
                   outs=[((S, D_MODEL), F32, "row"), ((S, D_MODEL), BF16, "row"), ((1, 1), F32, "acc"), vec, vec, vec])


def _ffn_duc(dact, uc):
    ug, uv = uc[:, :D_FF], uc[:, D_FF:]
    return jnp.concatenate([dact * uv * _dsilu(ug), dact * _silu(ug)], axis=1)


def ffn_act_bwd(dact, up, conv_w, conv_b):
    S = up.shape[0]
    ts = 128

    def body(d_ref, dn_ref, u_ref, up_ref, un_ref, w_ref, b_ref, dup_ref, dw_ref, db_ref):
        i = pl.program_id(0)
        first = (i > 0).astype(F32)
        last = (i < pl.num_programs(0) - 1).astype(F32)
        w, b = w_ref[...], b_ref[...]
        cur, prev = u_ref[...], up_ref[...] * first
        shifted = [_shift_down(cur, prev, FFN_CONV - 1 - j) for j in range(FFN_CONV)]
        uc = b + sum(shifted[j] * w[j:j + 1] for j in range(FFN_CONV))
        duc = _ffn_duc(d_ref[...], uc)
        uc_n = _conv_fwd(un_ref[...], cur[ts - 8:ts], w, FFN_CONV) + b
        duc_n = _ffn_duc(dn_ref[...], uc_n) * last
        db_ref[...] += jnp.sum(duc, axis=0, keepdims=True)
        for j in range(FFN_CONV):
            dw_ref[j:j + 1, :] += jnp.sum(duc * shifted[j], axis=0, keepdims=True)
        dup = duc * w[FFN_CONV - 1:FFN_CONV]
        for j in range(FFN_CONV - 1):
            dup = dup + _shift_up(duc, duc_n, FFN_CONV - 1 - j) * w[j:j + 1]
        dup_ref[...] = dup.astype(BF16)

    return rowcall(body, name="ffn_act_bwd", S=S, ts=ts,
                   ins=[(dact, "row"), (dact, "next"), (up, "row"), (up, "prev"), (up, "next"), (conv_w, "vec"), (conv_b, "vec")],
                   outs=[((S, 2 * D_FF), BF16, "row"), ((FFN_CONV, 2 * D_FF), F32, "acc"), ((1, 2 * D_FF), F32, "acc")])


def ln1_bwd(dxpre2, dh2, xpre1, mix, mod, ln_g, ln_b):
    S = xpre1.shape[0]

    def body(d2_ref, dh_ref, xp_ref, mix_ref, m_ref, g_ref, b_ref, dxpre_ref, dmix_ref,
             dscale_ref, dshift_ref, dgate_ref, dg_ref, db_ref):
        m = m_ref[...]
        xhat, rstd = _ln_stats(xp_ref[...])
        x1 = xhat * g_ref[...] + b_ref[...]
        dh = dh_ref[...]
        dx1 = ALPHA * d2_ref[...] + dh * (1.0 + m[SCALE_F:SCALE_F + 1])
        dscale_ref[...] += jnp.sum(dh * x1, axis=0, keepdims=True)
        dshift_ref[...] += jnp.sum(dh, axis=0, keepdims=True)
        dg_ref[...] += jnp.sum(dx1 * xhat, axis=0, keepdims=True)
        db_ref[...] += jnp.sum(dx1, axis=0, keepdims=True)
        dyg = dx1 * g_ref[...]
        dxpre = rstd * (dyg - jnp.mean(dyg, axis=-1, keepdims=True) - xhat * jnp.mean(dyg * xhat, axis=-1, keepdims=True))
        dxpre_ref[...] = dxpre
        dmix_ref[...] = (m[GATE_T:GATE_T + 1] * dxpre).astype(BF16)
        dgate_ref[...] += jnp.sum(dxpre * mix_ref[...], axis=0, keepdims=True)

    vec = ((1, D_MODEL), F32, "acc")
    return rowcall(body, name="ln1_bwd", S=S, ts=512,
                   ins=[(dxpre2, "row"), (dh2, "row"), (xpre1, "row"), (mix, "row"), (mod, "vec"), (ln_g, "vec"), (ln_b, "vec")],
                   outs=[((S, D_MODEL), F32, "row"), ((S, D_MODEL), BF16, "row"), vec, vec, vec, vec, vec])


def merge_bwd(dmerged, gates_raw, b_gate, ya, yb):
    S = ya.shape[0]

    def body(d_ref, g_ref, b_ref, ya_ref, yb_ref, dya_ref, dyb_ref, dg_ref, dbg_ref):
        gt = _sigmoid(g_ref[...] + b_ref[...])
        d = d_ref[...]
        ga, gb = gt[:, :D_MODEL], gt[:, D_MODEL:]
        dya_ref[...] = (d * ga).astype(BF16)
        dyb_ref[...] = (d * gb).astype(BF16)
        dgr = jnp.concatenate([d * ya_ref[...] * ga * (1.0 - ga), d * yb_ref[...] * gb * (1.0 - gb)], axis=1)
        dg_ref[...] = dgr.astype(BF16)
        dbg_ref[...] += jnp.sum(dgr, axis=0, keepdims=True)

    return rowcall(body, name="merge_bwd", S=S, ts=512,
                   ins=[(dmerged, "row"), (gates_raw, "row"), (b_gate, "vec"), (ya, "row"), (yb, "row")],
                   outs=[((S, D_MODEL), BF16, "row"), ((S, D_MODEL), BF16, "row"), ((S, 2 * D_MODEL), BF16, "row"),
                         ((1, 2 * D_MODEL), F32, "acc")])


def attn_bwd(qkv_pad, btile, do_b):
    S = qkv_pad.shape[0] - PAD_ROWS

    def body(q_ref, k_ref, v_ref, b_ref, do_ref, dq_ref, dk_ref, dv_ref, db_ref):
        n = pl.program_id(1)

        @pl.when(n == 0)
        def _():
            dk_ref[...] = jnp.zeros_like(dk_ref)
            dv_ref[...] = jnp.zeros_like(dv_ref)
            db_ref[...] = jnp.zeros_like(db_ref)

        start = pl.multiple_of(n * Q_TILE, Q_TILE)
        kb = k_ref[pl.ds(start, KEY_WIN), :]
        vb = v_ref[pl.ds(start, KEY_WIN), :]
        qv, dov = q_ref[...], do_ref[...]
        valid = _key_valid(n)
        dqs, dks, dvs = [], [], []
        for hh in range(HEADS_PER_GROUP):
            sl = slice(hh * B_DH, (hh + 1) * B_DH)
            p = _band_probs(qv[:, sl], kb[:, sl], b_ref[hh], valid)
            dp = _dot1(dov[:, sl], vb[:, sl], "nt")
            ds = p * (dp - jnp.sum(dp * p, axis=-1, keepdims=True))
            dbh = ds[0:CHUNK, 0:B_BAND]
            for qc in range(1, Q_CHUNKS):
                dbh = dbh + ds[qc * CHUNK:(qc + 1) * CHUNK, qc * CHUNK:qc * CHUNK + B_BAND]
            db_ref[hh] += dbh
            dsq = ds * (B_DH ** -0.5)
            dqs.append(_dot1(dsq, kb[:, sl], "nn"))
            dks.append(_dot1(dsq, qv[:, sl], "tn"))
            dvs.append(_dot1(p, dov[:, sl], "tn"))
        dq_ref[...] = jnp.concatenate(dqs, axis=1).astype(BF16)
        dk_ref[pl.ds(start, KEY_WIN), :] += jnp.concatenate(dks, axis=1)
        dv_ref[pl.ds(start, KEY_WIN), :] += jnp.concatenate(dvs, axis=1)

    col = pl.BlockSpec((PAD_ROWS + S, GROUP_W), lambda g, n: (0, g))
    tile = pl.BlockSpec((Q_TILE, GROUP_W), lambda g, n: (n, g))
    return pl.pallas_call(
        body, name="attn_bwd", grid=(N_GROUPS, S // Q_TILE), in_specs=_attn_specs(S) + [tile],
        out_specs=[tile, col, col, pl.BlockSpec((HEADS_PER_GROUP, CHUNK, B_BAND), lambda g, n: (g, 0, 0))],
        out_shape=[jax.ShapeDtypeStruct((S, B_W), BF16), jax.ShapeDtypeStruct((PAD_ROWS + S, B_W), F32),
                   jax.ShapeDtypeStruct((PAD_ROWS + S, B_W), F32), jax.ShapeDtypeStruct((B_HEADS, CHUNK, B_BAND), F32)],
        compiler_params=_cparams(("parallel", "arbitrary")),
    )(qkv_pad, qkv_pad, qkv_pad, btile, do_b)


def gate_a_bwd(do_a, o_pre, z, norm_w):
    S = o_pre.shape[0]

    def body(d_ref, o_ref, z_ref, nw_ref, dop_ref, dz_ref, dnw_ref):
        nw = nw_ref[...]
        acc = jnp.zeros((1, A_DK), F32)
        for h in range(A_HEADS):
            sl = slice(h * A_DK, (h + 1) * A_DK)
            oh, zh, dh = o_ref[:, sl], z_ref[:, sl], d_ref[:, sl]
            r = lax.rsqrt(jnp.mean(oh * oh, axis=-1, keepdims=True) + RMS_EPS)
            sz = _silu(zh)
            dz_ref[:, sl] = (dh * oh * r * nw * _dsilu(zh)).astype(BF16)
            acc = acc + jnp.sum(dh * oh * r * sz, axis=0, keepdims=True)
            t = dh * nw * sz
            dop_ref[:, sl] = r * t - oh * (r * r * r) * jnp.mean(t * oh, axis=-1, keepdims=True)
        dnw_ref[...] += acc

    return rowcall(body, name="gate_a_bwd", S=S, ts=512,
                   ins=[(do_a, "row"), (o_pre, "row"), (z, "row"), (norm_w, "vec")],
                   outs=[((S, A_W), F32, "row"), ((S, A_W), BF16, "row"), ((1, A_DK), F32, "acc")])


def delta_bwd(q, k, v, beta, g, sprev, tinv, do):
    S = q.shape[0]
    n_chunks = S // CHUNK

    def body(q_ref, k_ref, v_ref, beta_ref, g_ref, sprev_ref, t_ref, do_ref,
             dq_ref, dk_ref, dv_ref, dbeta_ref, dg_ref, dstate_ref):
        @pl.when(pl.program_id(0) == 0)
        def _():
            dstate_ref[...] = jnp.zeros_like(dstate_ref)

        mk = _tri_masks()
        causal, strict, eye = mk["causal"], mk["strict"], mk["eye"]
        blk_end = (lax.broadcasted_iota(jnp.int32, (GROUP_ROWS, 1), 0) & (CHUNK - 1)) == CHUNK - 1
        lane = lax.broadcasted_iota(jnp.int32, (CHUNK, A_HEADS), 1)
        betav, gv = beta_ref[...], g_ref[...]
        dbeta_t = jnp.zeros((CHUNK, A_HEADS), F32)
        dg_t = jnp.zeros((CHUNK, A_HEADS), F32)
        for grp in range(N_HEAD_GROUPS):
            qs, ks, vs = _stack_heads(q_ref, grp), _stack_heads(k_ref, grp), _stack_heads(v_ref, grp)
            dos = _stack_heads(do_ref, grp)
            bs = _stack_cols(betav, grp)
            loc = _delta_local(qs, ks, vs, bs, _stack_cols(gv, grp), mk)
            gam, decay, kd, gl, gc = loc["gam"], loc["decay"], loc["kd"], loc["gl"], loc["gc"]
            tinv_g = t_ref[0, grp]
            rhs = jnp.concatenate([loc["vb"], loc["y"]], axis=1)
            uw = _dot3(tinv_g, rhs, "nn")
            qg = qs * gam
            egl = jnp.exp(gl)
            heads = range(HEAD_GROUP)
            hid = [grp * HEAD_GROUP + j for j in heads]
            s0 = [sprev_ref[0, h] for h in hid]
            ds1 = [dstate_ref[h] for h in hid]
            w = [_head_rows(uw, j)[:, A_DK:] for j in heads]
            vn = [_head_rows(uw, j)[:, :A_DK] - _dot1(w[j], s0[j], "nn") for j in heads]
            vns = jnp.concatenate(vn, axis=0)
            dvn_local = _dot1(loc["p"], dos, "tn")
            dvn = [_head_rows(dvn_local, j) + _dot1(_head_rows(kd, j), ds1[j], "nn") for j in heads]
            dvns = jnp.concatenate(dvn, axis=0)
            dp = jnp.where(causal, _dot1(dos, vns, "nt"), 0.0)
            dqg = jnp.concatenate([_dot1(_head_rows(dos, j), s0[j], "nt") for j in heads], axis=0)
            dq = dqg * gam
            dgc = jnp.sum(dqg * qg, axis=-1, keepdims=True)
            for j in heads:
                dstate_ref[hid[j]] = (_dot1(_head_rows(qg, j), _head_rows(dos, j), "tn")
                                      + egl[(j + 1) * CHUNK - 1:(j + 1) * CHUNK] * ds1[j] - _dot1(w[j], dvn[j], "tn"))
            dkd = jnp.concatenate([_dot1(vn[j], ds1[j], "nt") for j in heads], axis=0)
            dk = dkd * jnp.exp(gl - gc)
            t1 = jnp.sum(dkd * kd, axis=-1, keepdims=True)
            dgc = dgc - t1
            dgl = jnp.concatenate(
                [jnp.broadcast_to(jnp.sum(_head_rows(t1, j), axis=0, keepdims=True)
                                  + jnp.sum(jnp.sum(ds1[j] * s0[j], axis=-1, keepdims=True), axis=0, keepdims=True)
                                  * egl[(j + 1) * CHUNK - 1:(j + 1) * CHUNK], (CHUNK, 1)) for j in heads], axis=0)
            dgc = dgc + jnp.where(blk_end, dgl, 0.0)
            duw = jnp.concatenate([dvns, jnp.concatenate([-_dot1(dvn[j], s0[j], "nt") for j in heads], axis=0)], axis=1)
            dvby = _dot3(tinv_g, duw, "tn")
            dt = _dot3(duw, rhs, "nt")
            da = jnp.where(strict, -_dot3(_dot3(tinv_g, dt, "tn"), tinv_g, "nt"), 0.0)
            dm = da * decay
            dn = dp * decay
            e = da * loc["a"] + dp * loc["p"]
            dgc = dgc + jnp.sum(e, axis=1, keepdims=True) - _row_to_col(jnp.sum(e, axis=0, keepdims=True), eye)
            dkb = _dot1(dm, ks, "nn")
            dk = dk + _dot1(dm, loc["kb"], "tn")
            dq = dq + _dot1(dn, ks, "nn")
            dk = dk + _dot1(dn, qs, "tn")
            dy = dvby[:, A_DK:]
            dvb = dvby[:, :A_DK]
            dkb = dkb + dy * gam
            dgc = dgc + jnp.sum(dy * loc["y"], axis=-1, keepdims=True)
            dk = dk + dkb * bs
            dbeta = jnp.sum(dkb * ks, axis=-1, keepdims=True) + jnp.sum(dvb * vs, axis=-1, keepdims=True)
            dv = dvb * bs
            dgs = jnp.sum(jnp.where(mk["upper"], _col_to_row(dgc, eye), 0.0), axis=1, keepdims=True)
            for j in heads:
                sl = slice(hid[j] * A_DK, (hid[j] + 1) * A_DK)
                dq_ref[:, sl] = _head_rows(dq, j)
                dk_ref[:, sl] = _head_rows(dk, j)
                dv_ref[:, sl] = _head_rows(dv, j)
                dbeta_t = dbeta_t + jnp.where(lane == hid[j], _head_rows(dbeta, j), 0.0)
                dg_t = dg_t + jnp.where(lane == hid[j], _head_rows(dgs, j), 0.0)
        dbeta_ref[...] = dbeta_t
        dg_ref[...] = dg_t

    rev = lambda n: (n_chunks - 1 - n, 0)
    rev4 = lambda n: (n_chunks - 1 - n, 0, 0, 0)
    tile = pl.BlockSpec((CHUNK, A_W), rev)
    small = pl.BlockSpec((CHUNK, A_HEADS), rev)
    return pl.pallas_call(
        body, name="delta_bwd", grid=(n_chunks,),
        in_specs=[tile, tile, tile, small, small, pl.BlockSpec((1, A_HEADS, A_DK, A_DK), rev4),
                  pl.BlockSpec((1, N_HEAD_GROUPS, GROUP_ROWS, GROUP_ROWS), rev4), tile],
        out_specs=[tile, tile, tile, small, small],
        out_shape=[jax.ShapeDtypeStruct((S, A_W), F32)] * 3 + [jax.ShapeDtypeStruct((S, A_HEADS), F32)] * 2,
        scratch_shapes=[pltpu.VMEM((A_HEADS, A_DK, A_DK), F32)],
        compiler_params=_cparams(("arbitrary",)),
    )(q, k, v, beta, g, sprev, tinv, do)


def _prep_a_dpre(raw, raw_prev, w, dq, dk, dv):
    pre, y = _prep_a_core(raw, raw_prev, w)
    parts = []
    for h in range(A_HEADS):
        yq = y[:, h * A_DK:(h + 1) * A_DK]
        dqh = dq[:, h * A_DK:(h + 1) * A_DK]
        rq = lax.rsqrt(jnp.sum(yq * yq, axis=-1, keepdims=True) + L2_EPS)
        parts.append((A_DK ** -0.5) * (rq * dqh - yq * (rq * rq * rq) * jnp.sum(dqh * yq, axis=-1, keepdims=True)))
    for h in range(A_HEADS):
        yk = y[:, A_W + h * A_DK:A_W + (h + 1) * A_DK]
        dkh = dk[:, h * A_DK:(h + 1) * A_DK]
        rk = lax.rsqrt(jnp.sum(yk * yk, axis=-1, keepdims=True) + L2_EPS)
        parts.append(rk * dkh - yk * (rk * rk * rk) * jnp.sum(dkh * yk, axis=-1, keepdims=True))
    parts.append(dv)
    return jnp.concatenate(parts, axis=1) * _dsilu(pre)


def prep_a_bwd(qkv_raw, ba, conv_a, a_log, dt_bias, dq, dk, dv, dbeta, dg):
    S = qkv_raw.shape[0]
    ts = 256

    def body(x_ref, xp_ref, xn_ref, ba_ref, w_ref, al_ref, dt_ref, dq_ref, dqn_ref, dk_ref, dkn_ref, dv_ref, dvn_ref,
             dbeta_ref, dg_ref, draw_ref, dba_ref, dw_ref, dal_ref, ddt_ref):
        i = pl.program_id(0)
        first = (i > 0).astype(F32)
        last = (i < pl.num_programs(0) - 1).astype(F32)
        w = w_ref[...]
        cur, prev = x_ref[...], xp_ref[...] * first
        dpre = _prep_a_dpre(cur, prev, w, dq_ref[...], dk_ref[...], dv_ref[...])
        dpre_n = _prep_a_dpre(xn_ref[...], cur[ts - 8:ts], w, dqn_ref[...], dkn_ref[...], dvn_ref[...]) * last
        for j in range(A_CONV):
            dw_ref[j:j + 1, :] += jnp.sum(dpre * _shift_down(cur, prev, A_CONV - 1 - j), axis=0, keepdims=True)
        draw = dpre * w[A_CONV - 1:A_CONV]
        for j in range(A_CONV - 1):
            draw = draw + _shift_up(dpre, dpre_n, A_CONV - 1 - j) * w[j:j + 1]
        draw_ref[...] = draw.astype(BF16)
        bav = ba_ref[...]
        beta = _sigmoid(bav[:, 0:A_HEADS])
        xa = bav[:, A_HEADS:2 * A_HEADS] + dt_ref[...]
        nexp = -jnp.exp(al_ref[...])
        dgv = dg_ref[...]
        da = dgv * nexp * _sigmoid(xa)
        dba_ref[:, 0:A_HEADS] = dbeta_ref[...] * beta * (1.0 - beta)
        dba_ref[:, A_HEADS:2 * A_HEADS] = da
        dal_ref[...] += jnp.sum(dgv * nexp * _softplus(xa), axis=0, keepdims=True)
        ddt_ref[...] += jnp.sum(da, axis=0, keepdims=True)

    return rowcall(
        body, name="prep_a_bwd", S=S, ts=ts,
        ins=[(qkv_raw, "row"), (qkv_raw, "prev"), (qkv_raw, "next"), (ba, "row"), (conv_a, "vec"), (a_log, "vec"),
             (dt_bias, "vec"), (dq, "row"), (dq, "next"), (dk, "row"), (dk, "next"), (dv, "row"), (dv, "next"),
             (dbeta, "row"), (dg, "row")],
        outs=[((S, 3 * A_W), BF16, "row"), ((S, 2 * A_HEADS), F32, "row"), ((A_CONV, 3 * A_W), F32, "acc"),
              ((1, A_HEADS), F32, "acc"), ((1, A_HEADS), F32, "acc")])


def grad_x_final(dh1, x, dxpre1, mod):
    S = x.shape[0]

    def body(dh_ref, x_ref, dx_ref, m_ref, gx_ref, dscale_ref, dshift_ref):
        dh = dh_ref[...]
        gx_ref[...] = ALPHA * dx_ref[...] + dh * (1.0 + m_ref[...][SCALE_T:SCALE_T + 1])
        dscale_ref[...] += jnp.sum(dh * x_ref[...], axis=0, keepdims=True)
        dshift_ref[...] += jnp.sum(dh, axis=0, keepdims=True)

    vec = ((1, D_MODEL), F32, "acc")
    return rowcall(body, name="grad_x_final", S=S, ts=512, ins=[(dh1, "row"), (x, "row"), (dxpre1, "row"), (mod, "vec")],
                   outs=[((S, D_MODEL), F32, "row"), vec, vec])


_C_QKV, _C_Z, _C_BA, _C_QKVB, _C_G = 0, 3 * A_W, 4 * A_W, 4 * A_W + 2 * A_HEADS, 4 * A_W + 2 * A_HEADS + 3 * B_W
BA_PAD = 128


def split_w_in(w_in):
    ba = jnp.pad(w_in[:, _C_BA:_C_QKVB], ((0, 0), (0, BA_PAD - 2 * A_HEADS)))
    return dict(qkv=w_in[:, _C_QKV:_C_Z], z=w_in[:, _C_Z:_C_BA], ba=ba, qkvb=w_in[:, _C_QKVB:_C_G], g=w_in[:, _C_G:])


def join_w_in(p):
    return jnp.concatenate([p["qkv"], p["z"], p["ba"][:, :2 * A_HEADS], p["qkvb"], p["g"]], axis=1)


def forward_local(x, target, mod, w, sm):
    h1 = modulate(x, mod, SHIFT_T, SCALE_T, "mod_t")
    qkv_raw = mm(h1, w["qkv"], mode="nn", out_dtype=F32, name="proj_qkv")
    z = mm(h1, w["z"], mode="nn", out_dtype=F32, name="proj_z")
    ba = mm(h1, w["ba"], mode="nn", out_dtype=F32, name="proj_ba")
    qkvb = mm(h1, w["qkvb"], mode="nn", out_dtype=BF16, name="proj_qkvb")
    gates_raw = mm(h1, w["g"], mode="nn", out_dtype=F32, name="proj_g")
    q, k, v, beta, g = prep_a_fwd(qkv_raw, ba, sm["conv_a"], sm["a_log"], sm["dt_bias"])
    o_pre, sprev, tinv = delta_fwd(q, k, v, beta, g)
    o_a = gate_a_fwd(o_pre, z, sm["norm_a"])
    qkv_pad = jnp.pad(qkvb, ((PAD_ROWS, 0), (0, 0)))
    bias = bias_tiles(jnp.transpose(bias_expand(sm["rel_bias"]), (1, 0, 2)))
    o_b = attn_fwd(qkv_pad, bias)
    ya = mm(o_a, w["branch_a"], mode="nn", out_dtype=F32, name="branch_a")
    yb = mm(o_b, w["branch_b"], mode="nn", out_dtype=F32, name="branch_b")
    merged = merge_fwd(gates_raw, sm["b_gate"], ya, yb)
    mix = mm(merged, w["o"], mode="nn", out_dtype=F32, name="mix")
    xpre1, x1, h2 = ln1_fwd(x, mix, mod, sm["ln1_g"], sm["ln1_b"])
    up = mm(h2, w["up"], mode="nn", out_dtype=F32, name="ffn_up")
    act = ffn_act_fwd(up, sm["conv_ffn"], sm["b_conv_ffn"])
    ffn = mm(act, w["down"], mode="nn", out_dtype=F32, name="ffn_down", tk=D_FF // 2)
    dxpre2, dffn, loss, dgate_f, dln2_g, dln2_b = final_fwd_bwd(x1, ffn, target, mod, sm["ln2_g"], sm["ln2_b"])
    saved = dict(h1=h1, qkv_raw=qkv_raw, z=z, ba=ba, gates_raw=gates_raw, q=q, k=k, v=v, beta=beta, g=g,
                 o_pre=o_pre, sprev=sprev, tinv=tinv, o_a=o_a, qkv_pad=qkv_pad, bias=bias, o_b=o_b, ya=ya, yb=yb,
                 merged=merged, mix=mix, xpre1=xpre1, x1=x1, h2=h2, up=up, act=act, ffn=ffn)
    return loss, dxpre2, dffn, dict(gate_f=dgate_f, ln2_g=dln2_g, ln2_b=dln2_b), saved


def backward_local(x, mod, w, sm, dxpre2, dffn, fin, sv):
    half_ff = D_FF // 2
    dact = mm(dffn, w["down"], mode="nt", out_dtype=F32, name="d_act", tn=half_ff)
    gw_down = mm(sv["act"], dffn, mode="tn", out_dtype=BF16, name="gw_down", tm=half_ff)
    dup, dconv_ffn, db_conv_ffn = ffn_act_bwd(dact, sv["up"], sm["conv_ffn"], sm["b_conv_ffn"])
    dh2 = mm(dup, w["up"], mode="nt", out_dtype=F32, name="d_h2", tk=half_ff)
    gw_up = mm(sv["h2"], dup, mode="tn", out_dtype=BF16, name="gw_up")
    dxpre1, dmix, dsc_f, dsh_f, dgate_t, dln1_g, dln1_b = ln1_bwd(
        dxpre2, dh2, sv["xpre1"], sv["mix"], mod, sm["ln1_g"], sm["ln1_b"])
    dmerged = mm(dmix, w["o"], mode="nt", out_dtype=F32, name="d_merged")
    gw_o = mm(sv["merged"], dmix, mode="tn", out_dtype=BF16, name="gw_o")
    dya, dyb, dgates, db_gate = merge_bwd(dmerged, sv["gates_raw"], sm["b_gate"], sv["ya"], sv["yb"])
    do_a = mm(dya, w["branch_a"], mode="nt", out_dtype=F32, name="d_oa")
    gw_branch_a = mm(sv["o_a"], dya, mode="tn", out_dtype=BF16, name="gw_branch_a")
    do_b = mm(dyb, w["branch_b"], mode="nt", out_dtype=BF16, name="d_ob")
    gw_branch_b = mm(sv["o_b"], dyb, mode="tn", out_dtype=BF16, name="gw_branch_b")
    dq_b, dk_pad, dv_pad, dbias = attn_bwd(sv["qkv_pad"], sv["bias"], do_b)
    dqkvb = jnp.concatenate([dq_b, dk_pad[PAD_ROWS:].astype(BF16), dv_pad[PAD_ROWS:].astype(BF16)], axis=1)
    drel_bias = bias_reduce(jnp.transpose(dbias, (1, 0, 2)))
    do_pre, dz, dnorm_a = gate_a_bwd(do_a, sv["o_pre"], sv["z"], sm["norm_a"])
    dq, dk, dv, dbeta, dg = delta_bwd(sv["q"], sv["k"], sv["v"], sv["beta"], sv["g"], sv["sprev"], sv["tinv"], do_pre)
    dqkv_raw, dba16, dconv_a, da_log, ddt_bias = prep_a_bwd(
        sv["qkv_raw"], sv["ba"], sm["conv_a"], sm["a_log"], sm["dt_bias"], dq, dk, dv, dbeta, dg)
    dba = jnp.pad(dba16, ((0, 0), (0, BA_PAD - 2 * A_HEADS))).astype(BF16)
    pieces = dict(qkv=dqkv_raw, z=dz, ba=dba, qkvb=dqkvb, g=dgates)
    dh1 = None
    gw_in = {}
    for key, dpiece in pieces.items():
        dh1 = mm(dpiece, w[key], mode="nt", out_dtype=F32, name="d_h1_" + key, acc_in=dh1)
        gw_in[key] = mm(sv["h1"], dpiece, mode="tn", out_dtype=BF16, name="gw_in_" + key)
    grad_x, dsc_t, dsh_t = grad_x_final(dh1, x, dxpre1, mod)
    dmod = jnp.concatenate([dsh_t, dsc_t, dgate_t, dsh_f, dsc_f, fin["gate_f"]], axis=0)
    gw = dict(w_in=join_w_in(gw_in), w_branch_a=gw_branch_a, w_branch_b=gw_branch_b, w_o=gw_o, w_up=gw_up, w_down=gw_down)
    gs = dict(b_gate=db_gate, conv_a=dconv_a, a_log=da_log, dt_bias=ddt_bias, norm_a=dnorm_a, rel_bias=drel_bias,
              ln1_g=dln1_g, ln1_b=dln1_b, conv_ffn=dconv_ffn, b_conv_ffn=db_conv_ffn, ln2_g=fin["ln2_g"], ln2_b=fin["ln2_b"])
    return grad_x, dmod, gw, gs


MESH = pl.DeviceIdType.MESH
ANY = pl.BlockSpec(memory_space=pl.ANY)
WHOLE_VMEM = pl.BlockSpec(memory_space=pltpu.VMEM)


def _place():
    return lax.axis_index("x"), lax.axis_index("y"), lax.axis_index("c")


def allgather8(blk, name):
    m_per, n = blk.shape

    def body(x_ref, out_ref, send_sems, recv_sems, local_sem):
        x, y, c = _place()
        me, sibling = (x, y, c), (x, y, 1 - c)
        chips = [(1 - x, y), (x, 1 - y), (1 - x, 1 - y)]

        def rows(px, py, pc):
            return out_ref.at[pl.ds((4 * px + 2 * py + pc) * m_per, m_per), :]

        def copy(k, block, to, src=None):
            return pltpu.make_async_remote_copy(
                src_ref=rows(*block) if src is None else src, dst_ref=rows(*block),
                send_sem=send_sems.at[k], recv_sem=recv_sems.at[k], device_id=to, device_id_type=MESH)

        mine = pltpu.make_async_copy(x_ref, rows(*me), local_sem)
        mine.start()
        first = [copy(0, me, sibling, src=x_ref)]
        first += [copy(1 + j, me, (*chip, c), src=x_ref) for j, chip in enumerate(chips)]
        for cp in first:
            cp.start()
        passed = [copy(4 + j, (*chip, c), sibling) for j, chip in enumerate(chips)]
        for j, chip in enumerate(chips):
            copy(1 + j, (*chip, c), me).wait_recv()
            passed[j].start()
        copy(0, sibling, me).wait_recv()
        for j, chip in enumerate(chips):
            copy(4 + j, (*chip, 1 - c), me).wait_recv()
        for cp in first + passed:
            cp.wait_send()
        mine.wait()

    return pl.pallas_call(
        body, name=name, out_shape=jax.ShapeDtypeStruct((N_DEV * m_per, n), blk.dtype),
        in_specs=[WHOLE_VMEM], out_specs=WHOLE_VMEM,
        scratch_shapes=[pltpu.SemaphoreType.DMA((7,)), pltpu.SemaphoreType.DMA((7,)), pltpu.SemaphoreType.DMA],
    )(blk)


def _chip_peers(x, y):
    return [(1 - x, y), (x, 1 - y), (1 - x, 1 - y)]


def gather_chips(shards, name):
    n = len(shards)

    def body(*refs):
        ins, outs = refs[:n], refs[n:2 * n]
        send_sems, recv_sems, local_sems = refs[2 * n:]
        x, y, c = _place()
        me = 2 * x + y
        peers = _chip_peers(x, y)
        sends, local = [], []
        for a in range(n):
            lc = pltpu.make_async_copy(ins[a], outs[a].at[me], local_sems.at[a])
            lc.start()
            local.append(lc)
            for j, (px, py) in enumerate(peers):
                cp = pltpu.make_async_remote_copy(
                    src_ref=ins[a], dst_ref=outs[a].at[me], send_sem=send_sems.at[3 * a + j],
                    recv_sem=recv_sems.at[3 * a + j], device_id=(px, py, c), device_id_type=MESH)
                cp.start()
                sends.append(cp)
        for a in range(n):
            for j, (px, py) in enumerate(peers):
                pltpu.make_async_remote_copy(
                    src_ref=ins[a], dst_ref=outs[a].at[2 * px + py], send_sem=send_sems.at[3 * a + j],
                    recv_sem=recv_sems.at[3 * a + j], device_id=(px, py, c), device_id_type=MESH).wait_recv()
        for cp in sends:
            cp.wait_send()
        for lc in local:
            lc.wait()

    return pl.pallas_call(
        body, name=name, out_shape=[jax.ShapeDtypeStruct((N_CHIPS,) + s.shape, s.dtype) for s in shards],
        in_specs=[ANY] * n, out_specs=[ANY] * n,
        scratch_shapes=[pltpu.SemaphoreType.DMA((3 * n,)), pltpu.SemaphoreType.DMA((3 * n,)), pltpu.SemaphoreType.DMA((n,))],
    )(*shards)


def sibling_exchange(arrs, name):
    n = len(arrs)

    def body(*refs):
        ins, outs = refs[:n], refs[n:2 * n]
        send_sems, recv_sems = refs[2 * n:]
        x, y, c = _place()
        cps = [pltpu.make_async_remote_copy(src_ref=ins[a], dst_ref=outs[a], send_sem=send_sems.at[a],
                                            recv_sem=recv_sems.at[a], device_id=(x, y, 1 - c), device_id_type=MESH)
               for a in range(n)]
        for cp in cps:
            cp.start()
        for cp in cps:
            cp.wait()

    return pl.pallas_call(
        body, name=name, out_shape=[jax.ShapeDtypeStruct(a.shape, a.dtype) for a in arrs],
        in_specs=[ANY] * n, out_specs=[ANY] * n,
        scratch_shapes=[pltpu.SemaphoreType.DMA((n,)), pltpu.SemaphoreType.DMA((n,))],
    )(*arrs)


def scatter_chips(parts, name):
    n = len(parts)

    def body(*refs):
        ins, outs = refs[:n], refs[n:2 * n]
        send_sems, recv_sems, local_sems = refs[2 * n:]
        x, y, c = _place()
        me = 2 * x + y
        peers = _chip_peers(x, y)
        sends, local = [], []
        for a in range(n):
            lc = pltpu.make_async_copy(ins[a].at[me], outs[a].at[me], local_sems.at[a])
            lc.start()
            local.append(lc)
            for j, (px, py) in enumerate(peers):
                cp = pltpu.make_async_remote_copy(
                    src_ref=ins[a].at[2 * px + py], dst_ref=outs[a].at[me], send_sem=send_sems.at[3 * a + j],
                    recv_sem=recv_sems.at[3 * a + j], device_id=(px, py, c), device_id_type=MESH)
                cp.start()
                sends.append(cp)
        for a in range(n):
            for j, (px, py) in enumerate(peers):
                pltpu.make_async_remote_copy(
                    src_ref=ins[a].at[me], dst_ref=outs[a].at[2 * px + py], send_sem=send_sems.at[3 * a + j],
                    recv_sem=recv_sems.at[3 * a + j], device_id=(px, py, c), device_id_type=MESH).wait_recv()
        for cp in sends:
            cp.wait_send()
        for lc in local:
            lc.wait()

    return pl.pallas_call(
        body, name=name, out_shape=[jax.ShapeDtypeStruct(p.shape, p.dtype) for p in parts],
        in_specs=[ANY] * n, out_specs=[ANY] * n,
        scratch_shapes=[pltpu.SemaphoreType.DMA((3 * n,)), pltpu.SemaphoreType.DMA((3 * n,)), pltpu.SemaphoreType.DMA((n,))],
    )(*parts)


TILE_BYTES = 2 * 1024 * 1024


def _row_tile(rows, row_bytes):
    if rows * row_bytes <= TILE_BYTES or rows % 8:
        return rows
    best = 8
    for t in range(8, rows + 1, 8):
        if rows % t == 0 and t * row_bytes <= TILE_BYTES:
            best = t
    return best


def pair_add(a, b, name):
    R, C = a.shape
    tr = _row_tile(R, C * 4)

    def body(a_ref, b_ref, o_ref):
        o_ref[...] = (a_ref[...].astype(F32) + b_ref[...].astype(F32)).astype(BF16)

    spec = pl.BlockSpec((tr, C), lambda i: (i, 0))
    return pl.pallas_call(body, name=name, grid=(R // tr,), in_specs=[spec, spec], out_specs=spec,
                          out_shape=jax.ShapeDtypeStruct((R, C), BF16), compiler_params=_cparams(("parallel",)))(a, b)


def sum_lead(parts, name):
    K, R, C = parts.shape
    tr = _row_tile(R, C * 4)

    def body(p_ref, o_ref):
        acc = p_ref[0].astype(F32)
        for j in range(1, K):
            acc = acc + p_ref[j].astype(F32)
        o_ref[...] = acc

    return pl.pallas_call(
        body, name=name, grid=(R // tr,), in_specs=[pl.BlockSpec((K, tr, C), lambda i: (0, i, 0))],
        out_specs=pl.BlockSpec((tr, C), lambda i: (i, 0)), out_shape=jax.ShapeDtypeStruct((R, C), F32),
        compiler_params=_cparams(("parallel",)))(parts)


def adamw(w, g, m, v, name):
    R, C = w.shape
    tr = _row_tile(R, C * 4)

    def body(w_ref, g_ref, m_ref, v_ref, d_ref, mo_ref, vo_ref):
        gv = g_ref[...]
        m2 = ADAM_B1 * m_ref[...] + (1.0 - ADAM_B1) * gv
        v2 = ADAM_B2 * v_ref[...] + (1.0 - ADAM_B2) * (gv * gv)
        m_hat = m2 / (1.0 - ADAM_B1 ** ADAM_STEP)
        v_hat = v2 / (1.0 - ADAM_B2 ** ADAM_STEP)
        d_ref[...] = -ADAM_LR * (m_hat / (jnp.sqrt(v_hat) + ADAM_EPS) + ADAM_WD * w_ref[...])
        mo_ref[...] = m2
        vo_ref[...] = v2

    spec = pl.BlockSpec((tr, C), lambda i: (i, 0))
    return pl.pallas_call(body, name=name, grid=(R // tr,), in_specs=[spec] * 4, out_specs=[spec] * 3,
                          out_shape=[jax.ShapeDtypeStruct((R, C), F32)] * 3, compiler_params=_cparams(("parallel",)))(w, g, m, v)


LANES = 1024


def _pack(arrs, rows):
    out, offs, r = [], [], 0
    for a in arrs:
        flat = a.reshape(-1)
        nr = -(-flat.shape[0] // LANES)
        out.append(jnp.pad(flat, (0, nr * LANES - flat.shape[0])))
        offs.append(r)
        r += nr
    assert r <= rows, (r, rows)
    out.append(jnp.zeros(((rows - r) * LANES,), F32))
    return jnp.concatenate(out).reshape(rows, LANES), offs


def _unpack(packed, offs, shapes):
    flat = packed.reshape(-1)
    return [flat[o * LANES:o * LANES + math.prod(s)].reshape(s) for o, s in zip(offs, shapes)]


WEIGHTS = ["w_ada", "b_ada", "w_in", "b_gate", "conv_a", "a_log", "dt_bias", "norm_a", "rel_bias", "w_branch_a",
           "w_branch_b", "w_o", "ln1_g", "ln1_b", "w_up", "conv_ffn", "b_conv_ffn", "w_down", "ln2_g", "ln2_b"]
BIG = ["w_in", "w_branch_a", "w_branch_b", "w_o", "w_up", "w_down"]
COL_SHARDED = {"w_in", "w_up"}
SMALL_SHARDED = {"conv_a": 3 * A_W // N_CHIPS, "rel_bias": B_REL // N_CHIPS, "conv_ffn": 2 * D_FF // N_CHIPS}
SMALL = [n for n in WEIGHTS if n not in BIG and n != "w_ada"]


def _to_full(g4, name):
    if name in COL_SHARDED:
        return jnp.transpose(g4, (1, 0, 2)).reshape(g4.shape[1], -1)
    return g4.reshape(-1, g4.shape[2])


def _to_shards(full, name):
    if name in COL_SHARDED:
        return jnp.transpose(full.reshape(full.shape[0], N_CHIPS, -1), (1, 0, 2))
    return full.reshape(N_CHIPS, -1, full.shape[1])


def kernel(x, c, w_ada, b_ada, w_in, b_gate, conv_a, a_log, dt_bias, norm_a, rel_bias, w_branch_a, w_branch_b, w_o, ln1_g, ln1_b, w_up, conv_ffn, b_conv_ffn, w_down, ln2_g, ln2_b, loss_target, m_w_ada, m_b_ada, m_w_in, m_b_gate, m_conv_a, m_a_log, m_dt_bias, m_norm_a, m_rel_bias, m_w_branch_a, m_w_branch_b, m_w_o, m_ln1_g, m_ln1_b, m_w_up, m_conv_ffn, m_b_conv_ffn, m_w_down, m_ln2_g, m_ln2_b, v_w_ada, v_b_ada, v_w_in, v_b_gate, v_conv_a, v_a_log, v_dt_bias, v_norm_a, v_rel_bias, v_w_branch_a, v_w_branch_b, v_w_o, v_ln1_g, v_ln1_b, v_w_up, v_conv_ffn, v_b_conv_ffn, v_w_down, v_ln2_g, v_ln2_b):
    args = dict(locals())
    wts = {n: args[n] for n in WEIGHTS}
    moms = {n: args["m_" + n] for n in WEIGHTS}
    vars_ = {n: args["v_" + n] for n in WEIGHTS}
    xi, yi, ci = _place()
    chip = 2 * xi + yi
    dev = 4 * xi + 2 * yi + ci
    ada_cols = w_ada.shape[2]

    c_all = allgather8(jnp.pad(c, ((0, 7), (0, 0))), "gather_c").reshape(N_DEV, 8, D_MODEL)[:, 0]
    b_ada_sh = lax.dynamic_slice(b_ada, (0, chip * ada_cols), (1, ada_cols))
    mod_sh = ada_fwd(c_all, w_ada[0], b_ada_sh)
    mod_g = allgather8(mod_sh, "gather_mod").reshape(N_CHIPS, 2, N_DEV, ada_cols)[:, 0]
    mod = lax.dynamic_slice(mod_g, (0, dev, 0), (N_CHIPS, 1, ada_cols)).reshape(6, D_MODEL)

    big_full = gather_chips([wts[n][0].astype(BF16) for n in BIG], "gather_weights")
    w = {n: _to_full(g4, n) for n, g4 in zip(BIG, big_full)}
    wd = dict(split_w_in(w["w_in"]), branch_a=w["w_branch_a"], branch_b=w["w_branch_b"], o=w["w_o"], up=w["w_up"], down=w["w_down"])
    sshapes = [wts[n].shape[1:] for n in SMALL_SHARDED]
    spack, soffs = _pack([wts[n][0] for n in SMALL_SHARDED], 16)
    sg = allgather8(spack, "gather_small_w").reshape(N_CHIPS, 2, 16, LANES)[:, 0]
    sparts = [_unpack(sg[j], soffs, sshapes) for j in range(N_CHIPS)]
    sm = {n: wts[n] for n in SMALL if n not in SMALL_SHARDED and n != "b_ada"}
    for i, n in enumerate(SMALL_SHARDED):
        sm[n] = jnp.concatenate([sparts[j][i] for j in range(N_CHIPS)], axis=-1)

    loss, dxpre2, dffn, fin, sv = forward_local(x[0], loss_target[0], mod, wd, sm)
    grad_x, dmod, gw, gs = backward_local(x[0], mod, wd, sm, dxpre2, dffn, fin, sv)

    gnames = [n for n in SMALL if n != "b_ada"]
    vec, voffs = _pack([dmod] + [gs[n] for n in gnames] + [loss], 56)
    gathered = allgather8(vec, "gather_small_g").reshape(N_DEV, 56, LANES)
    summed = sum_lead(gathered, "sum_small_g")
    full_shapes = [(6, D_MODEL)] + [gs[n].shape for n in gnames] + [(1, 1)]
    parts = _unpack(summed, voffs, full_shapes)
    grads = {"b_ada": parts[0].reshape(1, -1)}
    for n, p in zip(gnames, parts[1:-1]):
        if n in SMALL_SHARDED:
            p = lax.dynamic_slice_in_dim(p, chip * SMALL_SHARDED[n], SMALL_SHARDED[n], axis=1)
        grads[n] = p.reshape(wts[n].shape)
    loss_total = parts[-1].reshape(())
    dmod_all = gathered[:, 0:6, :].reshape(N_DEV, 6 * D_MODEL)
    grads["w_ada"] = ada_bwd(c_all, lax.dynamic_slice(dmod_all, (0, chip * ada_cols), (N_DEV, ada_cols)))[None]

    mine = [gw[n] for n in BIG]
    theirs = sibling_exchange(mine, "grad_sibling")
    chip_sums = [_to_shards(pair_add(a, b, "grad_pair_" + n), n) for n, a, b in zip(BIG, mine, theirs)]
    received = scatter_chips(chip_sums, "grad_scatter")
    for n, r in zip(BIG, received):
        grads[n] = sum_lead(r, "grad_sum_" + n)[None]

    delta, new_m, new_v = {}, {}, {}
    for n in ["w_ada"] + BIG:
        d, m2, v2 = adamw(wts[n][0], grads[n][0], moms[n][0], vars_[n][0], "adamw_" + n)
        delta[n], new_m[n], new_v[n] = d[None], m2[None], v2[None]
    shapes = [wts[n].shape for n in SMALL]
    packs = [_pack([t[n] for n in SMALL], 32) for t in (wts, grads, moms, vars_)]
    outs = adamw(*[p[0] for p in packs], "adamw_small")
    for res, o in zip((delta, new_m, new_v), outs):
        for n, a in zip(SMALL, _unpack(o, packs[0][1], shapes)):
            res[n] = a
    return (loss_total, grad_x[None], *[grads[n] for n in WEIGHTS], *[delta[n] for n in WEIGHTS],
            *[new_m[n] for n in WEIGHTS], *[new_v[n] for n in WEIGHTS])
```

```python
import functools
import math

import jax
import jax.numpy as jnp
from jax import lax
from jax.experimental import pallas as pl
from jax.experimental.pallas import tpu as pltpu

F32 = jnp.float32
BF16 = jnp.bfloat16

D_MODEL = 1024
CHUNK = 64
A_HEADS = 8
A_DK = 128
A_W = A_HEADS * A_DK
A_CONV = 4
B_HEADS = 16
B_DH = 64
B_W = B_HEADS * B_DH
B_PREV = 8
B_BAND = (B_PREV + 1) * CHUNK
B_MAX_REL = 256
B_REL = CHUNK - 1 + B_MAX_REL + 1
D_FF = 2816
FFN_CONV = 3
IN_COLS = 4 * A_W + 2 * A_HEADS + 3 * B_W + 2 * D_MODEL
ALPHA = 2.0 ** 0.25
LN_EPS = 1e-5
RMS_EPS = 1e-6
L2_EPS = 1e-6
NEG_INF = -1e30
ADAM_LR, ADAM_B1, ADAM_B2, ADAM_EPS, ADAM_WD, ADAM_STEP = 0.001, 0.9, 0.999, 1e-08, 0.01, 10
N_CHIPS = 4
N_DEV = 8
VMEM_LIMIT = 56 * 1024 * 1024


def _cparams(sem=None):
    return pltpu.CompilerParams(dimension_semantics=sem, vmem_limit_bytes=VMEM_LIMIT)


_DIMS = {"nn": (((1,), (0,)), ((), ())), "nt": (((1,), (1,)), ((), ())), "tn": (((0,), (0,)), ((), ()))}


def mm(a, b, *, mode, out_dtype, name, tm=1024, tn=512, tk=1024, acc_in=None):
    if mode == "nn":
        (M, K), (K2, N) = a.shape, b.shape
    elif mode == "nt":
        (M, K), (N, K2) = a.shape, b.shape
    else:
        (K, M), (K2, N) = a.shape, b.shape
    assert K == K2, (a.shape, b.shape, mode)
    tm, tn, tk = min(tm, M), min(tn, N), min(tk, K)
    assert M % tm == 0 and N % tn == 0 and K % tk == 0, (M, N, K, tm, tn, tk)
    nk = K // tk

    def body(*refs):
        if acc_in is None:
            a_ref, b_ref, o_ref, acc_ref = refs
        else:
            a_ref, b_ref, c_ref, o_ref, acc_ref = refs
        k = pl.program_id(2)

        @pl.when(k == 0)
        def _():
            if acc_in is None:
                acc_ref[...] = jnp.zeros_like(acc_ref)
            else:
                acc_ref[...] = c_ref[...]

        acc_ref[...] += lax.dot_general(a_ref[...].astype(BF16), b_ref[...].astype(BF16), _DIMS[mode],
                                        preferred_element_type=F32)

        @pl.when(k == nk - 1)
        def _():
            o_ref[...] = acc_ref[...].astype(out_dtype)

    a_spec = pl.BlockSpec((tk, tm), lambda i, j, k: (k, i)) if mode == "tn" else pl.BlockSpec((tm, tk), lambda i, j, k: (i, k))
    b_spec = pl.BlockSpec((tn, tk), lambda i, j, k: (j, k)) if mode == "nt" else pl.BlockSpec((tk, tn), lambda i, j, k: (k, j))
    o_spec = pl.BlockSpec((tm, tn), lambda i, j, k: (i, j))
    ins, in_specs, aliases = [a, b], [a_spec, b_spec], {}
    if acc_in is not None:
        assert acc_in.shape == (M, N) and acc_in.dtype == F32 and out_dtype == F32
        ins.append(acc_in)
        in_specs.append(o_spec)
        aliases = {2: 0}
    return pl.pallas_call(
        body, name=name, grid=(M // tm, N // tn, nk), in_specs=in_specs, out_specs=o_spec,
        out_shape=jax.ShapeDtypeStruct((M, N), out_dtype), scratch_shapes=[pltpu.VMEM((tm, tn), F32)],
        input_output_aliases=aliases, compiler_params=_cparams(("parallel", "parallel", "arbitrary")),
    )(*ins)


def rowcall(body, *, name, S, ts, ins, outs):
    assert S % ts == 0 and ts % 8 == 0
    nsteps, r8 = S // ts, ts // 8
    last8 = S // 8 - 1
    in_specs, arrays = [], []
    for arr, kind in ins:
        arrays.append(arr)
        if kind == "row":
            in_specs.append(pl.BlockSpec((ts, arr.shape[1]), lambda i: (i, 0)))
        elif kind == "prev":
            in_specs.append(pl.BlockSpec((8, arr.shape[1]), lambda i: (jnp.maximum(i * r8 - 1, 0), 0)))
        elif kind == "next":
            in_specs.append(pl.BlockSpec((8, arr.shape[1]), lambda i: (jnp.minimum((i + 1) * r8, last8), 0)))
        else:
            nd = arr.ndim
            in_specs.append(pl.BlockSpec(arr.shape, lambda i, nd=nd: (0,) * nd))
    out_specs, out_shapes, acc_idx = [], [], []
    for n, (shape, dtype, kind) in enumerate(outs):
        out_shapes.append(jax.ShapeDtypeStruct(shape, dtype))
        if kind == "row":
            out_specs.append(pl.BlockSpec((ts, shape[1]), lambda i: (i, 0)))
        else:
            nd = len(shape)
            out_specs.append(pl.BlockSpec(shape, lambda i, nd=nd: (0,) * nd))
            acc_idx.append(n)
    n_in = len(arrays)

    def wrapped(*refs):
        @pl.when(pl.program_id(0) == 0)
        def _():
            for n in acc_idx:
                refs[n_in + n][...] = jnp.zeros_like(refs[n_in + n])

        body(*refs)

    res = pl.pallas_call(
        wrapped, name=name, grid=(nsteps,), in_specs=in_specs, out_specs=out_specs, out_shape=out_shapes,
        compiler_params=_cparams(("arbitrary",) if acc_idx else ("parallel",)),
    )(*arrays)
    return res


def _shift_down(cur, prev8, k):
    if k == 0:
        return cur
    rolled = pltpu.roll(cur, k, axis=0)
    fix = pltpu.roll(prev8, k, axis=0)
    row = lax.broadcasted_iota(jnp.int32, (8, 1), 0)
    top = jnp.where(row < k, fix, rolled[0:8])
    if cur.shape[0] == 8:
        return top
    return jnp.concatenate([top, rolled[8:]], axis=0)


def _shift_up(cur, next8, k):
    if k == 0:
        return cur
    n = cur.shape[0]
    rolled = pltpu.roll(cur, n - k, axis=0)
    fix = pltpu.roll(next8, 8 - k, axis=0)
    row = lax.broadcasted_iota(jnp.int32, (8, 1), 0)
    bot = jnp.where(row >= 8 - k, fix, rolled[n - 8:n])
    return jnp.concatenate([rolled[:n - 8], bot], axis=0)


def _sigmoid(x):
    return 1.0 / (1.0 + jnp.exp(-x))


def _silu(x):
    return x * _sigmoid(x)


def _silu_and_grad(x):
    s = _sigmoid(x)
    return x * s, s * (1.0 + x * (1.0 - s))


def _softplus(x):
    return jnp.maximum(x, 0.0) + jnp.log1p(jnp.exp(-jnp.abs(x)))


def _split2(x):
    hi = x.astype(BF16)
    return hi, (x - hi.astype(F32)).astype(BF16)


def _dot1(a, b, mode):
    return lax.dot_general(a.astype(BF16), b.astype(BF16), _DIMS[mode], preferred_element_type=F32)


def _dot3(a, b, mode):
    ah, al = _split2(a)
    bh, bl = _split2(b)
    d = lambda p, q: lax.dot_general(p, q, _DIMS[mode], preferred_element_type=F32)
    return d(ah, bh) + (d(ah, bl) + d(al, bh))


def ada_fwd(c_all, w_sh, b_sh):
    n = w_sh.shape[1]
    tn = 512

    def body(c_ref, w_ref, b_ref, o_ref):
        o_ref[...] = _dot1(_silu(c_ref[...]), w_ref[...], "nn") + b_ref[...]

    return pl.pallas_call(
        body, name="ada_fwd", grid=(n // tn,),
        in_specs=[pl.BlockSpec((N_DEV, D_MODEL), lambda j: (0, 0)), pl.BlockSpec((D_MODEL, tn), lambda j: (0, j)),
                  pl.BlockSpec((1, tn), lambda j: (0, j))],
        out_specs=pl.BlockSpec((N_DEV, tn), lambda j: (0, j)), out_shape=jax.ShapeDtypeStruct((N_DEV, n), F32),
        compiler_params=_cparams(("parallel",)),
    )(c_all, w_sh, b_sh)


def ada_bwd(c_all, dmod_sh):
    n = dmod_sh.shape[1]
    tn = 512

    def body(c_ref, d_ref, o_ref):
        o_ref[...] = _dot1(_silu(c_ref[...]), d_ref[...], "tn")

    return pl.pallas_call(
        body, name="ada_bwd", grid=(n // tn,),
        in_specs=[pl.BlockSpec((N_DEV, D_MODEL), lambda j: (0, 0)), pl.BlockSpec((N_DEV, tn), lambda j: (0, j))],
        out_specs=pl.BlockSpec((D_MODEL, tn), lambda j: (0, j)), out_shape=jax.ShapeDtypeStruct((D_MODEL, n), F32),
        compiler_params=_cparams(("parallel",)),
    )(c_all, dmod_sh)


SHIFT_T, SCALE_T, GATE_T, SHIFT_F, SCALE_F, GATE_F = range(6)


def modulate(x, mod, shift_row, scale_row, name):
    S = x.shape[0]

    def body(x_ref, m_ref, o_ref):
        m = m_ref[...]
        o_ref[...] = (x_ref[...] * (1.0 + m[scale_row:scale_row + 1]) + m[shift_row:shift_row + 1]).astype(BF16)

    return rowcall(body, name=name, S=S, ts=512, ins=[(x, "row"), (mod, "vec")], outs=[((S, D_MODEL), BF16, "row")])[0]


def _conv_fwd(cur, prev, w, width):
    y = cur * w[width - 1:width]
    for j in range(width - 1):
        y = y + _shift_down(cur, prev, width - 1 - j) * w[j:j + 1]
    return y


def _prep_a_core(cur, prev, w):
    return _silu_and_grad(_conv_fwd(cur, prev, w, A_CONV))


def prep_a_fwd(qkv_raw, ba, conv_a, a_log, dt_bias):
    S = qkv_raw.shape[0]

    def body(x_ref, xp_ref, ba_ref, w_ref, al_ref, dt_ref, q_ref, k_ref, v_ref, beta_ref, g_ref):
        first = (pl.program_id(0) > 0).astype(F32)
        y, _ = _prep_a_core(x_ref[...], xp_ref[...] * first, w_ref[...])
        for h in range(A_HEADS):
            sl = slice(h * A_DK, (h + 1) * A_DK)
            qh = y[:, sl]
            kh = y[:, A_W + h * A_DK:A_W + (h + 1) * A_DK]
            q_ref[:, sl] = qh * (lax.rsqrt(jnp.sum(qh * qh, axis=-1, keepdims=True) + L2_EPS) * (A_DK ** -0.5))
            k_ref[:, sl] = kh * lax.rsqrt(jnp.sum(kh * kh, axis=-1, keepdims=True) + L2_EPS)
        v_ref[...] = y[:, 2 * A_W:3 * A_W]
        bav = ba_ref[...]
        beta_ref[...] = _sigmoid(bav[:, 0:A_HEADS])
        g_ref[...] = -jnp.exp(al_ref[...]) * _softplus(bav[:, A_HEADS:2 * A_HEADS] + dt_ref[...])

    return rowcall(
        body, name="prep_a_fwd", S=S, ts=256,
        ins=[(qkv_raw, "row"), (qkv_raw, "prev"), (ba, "row"), (conv_a, "vec"), (a_log, "vec"), (dt_bias, "vec")],
        outs=[((S, A_W), F32, "row")] * 3 + [((S, A_HEADS), F32, "row")] * 2)


HEAD_GROUP = 4
GROUP_ROWS = HEAD_GROUP * CHUNK
N_HEAD_GROUPS = A_HEADS // HEAD_GROUP
LOG_CHUNK = int(math.log2(CHUNK))


def _tri_masks():
    rb = lax.broadcasted_iota(jnp.int32, (GROUP_ROWS, GROUP_ROWS), 0)
    cb = lax.broadcasted_iota(jnp.int32, (GROUP_ROWS, GROUP_ROWS), 1)
    same = (rb >> LOG_CHUNK) == (cb >> LOG_CHUNK)
    return dict(causal=same & (rb >= cb), strict=same & (rb > cb), eye=rb == cb, upper=same & (cb >= rb),
                last=cb == (rb | (CHUNK - 1)), rb=rb, cb=cb)


def _col_to_row(colv, eye):
    return jnp.sum(jnp.where(eye, colv, 0.0), axis=0, keepdims=True)


def _row_to_col(rowv, eye):
    return jnp.sum(jnp.where(eye, rowv, 0.0), axis=1, keepdims=True)


def _tri_inv(a_list, mk):
    rb, cb = mk["rb"], mk["cb"]
    ts = [jnp.where(mk["eye"], 1.0, 0.0) - jnp.where((rb >> 1) == (cb >> 1), a, 0.0) for a in a_list]
    for lvl in range(1, LOG_CHUNK):
        rs, cs = rb >> lvl, cb >> lvl
        sel = ((rs & 1) == 1) & (cs == rs - 1)
        inner = [_dot3(t, jnp.where(sel, a, 0.0), "nn") for t, a in zip(ts, a_list)]
        ts = [t - _dot3(i, t, "nn") for i, t in zip(inner, ts)]
    return ts


def _stack_heads(ref, grp):
    return jnp.concatenate([ref[:, (grp * HEAD_GROUP + j) * A_DK:(grp * HEAD_GROUP + j + 1) * A_DK]
                            for j in range(HEAD_GROUP)], axis=0)


def _stack_cols(tile, grp):
    return jnp.concatenate([tile[:, grp * HEAD_GROUP + j:grp * HEAD_GROUP + j + 1] for j in range(HEAD_GROUP)], axis=0)


def _delta_local(q, k, v, beta, g, mk):
    causal, strict, eye = mk["causal"], mk["strict"], mk["eye"]
    g_row = _col_to_row(g, eye)
    gc = jnp.sum(jnp.where(causal, g_row, 0.0), axis=1, keepdims=True)
    gc_row = _col_to_row(gc, eye)
    decay = jnp.where(causal, jnp.exp(jnp.where(causal, gc - gc_row, 0.0)), 0.0)
    gam = jnp.exp(gc)
    kb = k * beta
    vb = v * beta
    y = kb * gam
    a = jnp.where(strict, _dot1(kb, k, "nt") * decay, 0.0)
    p = jnp.where(causal, _dot1(q, k, "nt") * decay, 0.0)
    gl = jnp.sum(jnp.where(mk["last"], gc_row, 0.0), axis=1, keepdims=True)
    kd = k * jnp.exp(gl - gc)
    return dict(gc=gc, decay=decay, gam=gam, kb=kb, vb=vb, y=y, a=a, p=p, gl=gl, kd=kd)


def _head_rows(x, j):
    return x[j * CHUNK:(j + 1) * CHUNK]


def delta_fwd(q, k, v, beta, g):
    S = q.shape[0]
    n_chunks = S // CHUNK

    def body(q_ref, k_ref, v_ref, beta_ref, g_ref, o_ref, sprev_ref, t_ref, state_ref):
        @pl.when(pl.program_id(0) == 0)
        def _():
            state_ref[...] = jnp.zeros_like(state_ref)

        mk = _tri_masks()
        betav, gv = beta_ref[...], g_ref[...]
        groups = range(N_HEAD_GROUPS)
        q_all = [_stack_heads(q_ref, grp) for grp in groups]
        locs = [_delta_local(q_all[grp], _stack_heads(k_ref, grp), _stack_heads(v_ref, grp),
                             _stack_cols(betav, grp), _stack_cols(gv, grp), mk) for grp in groups]
        tinvs = _tri_inv([loc["a"] for loc in locs], mk)
        uws = [_dot3(tinvs[grp], jnp.concatenate([locs[grp]["vb"], locs[grp]["y"]], axis=1), "nn") for grp in groups]
        for grp in groups:
            loc, uw = locs[grp], uws[grp]
            t_ref[0, grp] = tinvs[grp]
            qg = q_all[grp] * loc["gam"]
            egl = jnp.exp(loc["gl"])
            vns, o_state = [], []
            for j in range(HEAD_GROUP):
                h = grp * HEAD_GROUP + j
                s0 = state_ref[h]
                sprev_ref[0, h] = s0
                uw_h = _head_rows(uw, j)
                vn = uw_h[:, :A_DK] - _dot1(uw_h[:, A_DK:], s0, "nn")
                vns.append(vn)
                o_state.append(_dot1(_head_rows(qg, j), s0, "nn"))
                state_ref[h] = s0 * egl[(j + 1) * CHUNK - 1:(j + 1) * CHUNK] + _dot1(_head_rows(loc["kd"], j), vn, "tn")
            o_local = _dot1(loc["p"], jnp.concatenate(vns, axis=0), "nn")
            for j in range(HEAD_GROUP):
                h = grp * HEAD_GROUP + j
                o_ref[:, h * A_DK:(h + 1) * A_DK] = o_state[j] + _head_rows(o_local, j)

    tile = pl.BlockSpec((CHUNK, A_W), lambda n: (n, 0))
    small = pl.BlockSpec((CHUNK, A_HEADS), lambda n: (n, 0))
    return pl.pallas_call(
        body, name="delta_fwd", grid=(n_chunks,), in_specs=[tile, tile, tile, small, small],
        out_specs=[tile, pl.BlockSpec((1, A_HEADS, A_DK, A_DK), lambda n: (n, 0, 0, 0)),
                   pl.BlockSpec((1, N_HEAD_GROUPS, GROUP_ROWS, GROUP_ROWS), lambda n: (n, 0, 0, 0))],
        out_shape=[jax.ShapeDtypeStruct((S, A_W), F32), jax.ShapeDtypeStruct((n_chunks, A_HEADS, A_DK, A_DK), F32),
                   jax.ShapeDtypeStruct((n_chunks, N_HEAD_GROUPS, GROUP_ROWS, GROUP_ROWS), F32)],
        scratch_shapes=[pltpu.VMEM((A_HEADS, A_DK, A_DK), F32)],
        compiler_params=_cparams(("arbitrary",)),
    )(q, k, v, beta, g)


def gate_a_fwd(o_pre, z, norm_w):
    S = o_pre.shape[0]

    def body(o_ref, z_ref, nw_ref, out_ref):
        nw = nw_ref[...]
        for h in range(A_HEADS):
            sl = slice(h * A_DK, (h + 1) * A_DK)
            oh = o_ref[:, sl]
            r = lax.rsqrt(jnp.mean(oh * oh, axis=-1, keepdims=True) + RMS_EPS)
            out_ref[:, sl] = (oh * r * nw * _silu(z_ref[:, sl])).astype(BF16)

    return rowcall(body, name="gate_a_fwd", S=S, ts=512, ins=[(o_pre, "row"), (z, "row"), (norm_w, "vec")],
                   outs=[((S, A_W), BF16, "row")])[0]


HEADS_PER_GROUP = 2
GROUP_W = HEADS_PER_GROUP * B_DH
N_GROUPS = B_HEADS // HEADS_PER_GROUP
PAD_ROWS = B_PREV * CHUNK


Q_TILE = 256
Q_CHUNKS = Q_TILE // CHUNK
KEY_WIN = (B_PREV + Q_CHUNKS) * CHUNK


def bias_tiles(bias):
    rows = [jnp.pad(bias, ((0, 0), (0, 0), (qc * CHUNK, (Q_CHUNKS - 1 - qc) * CHUNK)), constant_values=NEG_INF)
            for qc in range(Q_CHUNKS)]
    return jnp.concatenate(rows, axis=1)


def _band_probs(qh, kh, bias, valid):
    s = _dot1(qh, kh, "nt") * (B_DH ** -0.5) + bias
    s = jnp.where(valid, s, NEG_INF)
    e = jnp.exp(s - jnp.max(s, axis=-1, keepdims=True))
    return e / jnp.sum(e, axis=-1, keepdims=True)


def _attn_specs(S):
    n_cb = B_W // GROUP_W
    return [pl.BlockSpec((Q_TILE, GROUP_W), lambda g, n: (n + PAD_ROWS // Q_TILE, g)),
            pl.BlockSpec((PAD_ROWS + S, GROUP_W), lambda g, n: (0, n_cb + g)),
            pl.BlockSpec((PAD_ROWS + S, GROUP_W), lambda g, n: (0, 2 * n_cb + g)),
            pl.BlockSpec((HEADS_PER_GROUP, Q_TILE, KEY_WIN), lambda g, n: (g, 0, 0))]


def _key_valid(n):
    return lax.broadcasted_iota(jnp.int32, (Q_TILE, KEY_WIN), 1) >= PAD_ROWS - n * Q_TILE


def attn_fwd(qkv_pad, btile):
    S = qkv_pad.shape[0] - PAD_ROWS

    def body(q_ref, k_ref, v_ref, b_ref, o_ref):
        n = pl.program_id(1)
        start = pl.multiple_of(n * Q_TILE, Q_TILE)
        kb = k_ref[pl.ds(start, KEY_WIN), :]
        vb = v_ref[pl.ds(start, KEY_WIN), :]
        qv = q_ref[...]
        valid = _key_valid(n)
        outs = []
        for hh in range(HEADS_PER_GROUP):
            sl = slice(hh * B_DH, (hh + 1) * B_DH)
            p = _band_probs(qv[:, sl], kb[:, sl], b_ref[hh], valid)
            outs.append(_dot1(p, vb[:, sl], "nn"))
        o_ref[...] = jnp.concatenate(outs, axis=1).astype(BF16)

    return pl.pallas_call(
        body, name="attn_fwd", grid=(N_GROUPS, S // Q_TILE), in_specs=_attn_specs(S),
        out_specs=pl.BlockSpec((Q_TILE, GROUP_W), lambda g, n: (n, g)),
        out_shape=jax.ShapeDtypeStruct((S, B_W), BF16),
        compiler_params=_cparams(("parallel", "arbitrary")),
    )(qkv_pad, qkv_pad, qkv_pad, btile)


def _rel_onehot(i):
    kj = lax.broadcasted_iota(jnp.int32, (B_BAND, B_REL), 0)
    r = lax.broadcasted_iota(jnp.int32, (B_BAND, B_REL), 1)
    idx = jnp.clip(PAD_ROWS + i - kj, -(CHUNK - 1), B_MAX_REL) + (CHUNK - 1)
    return jnp.where(idx == r, 1.0, 0.0)


def bias_expand(rel_bias):
    def body(rb_ref, o_ref):
        i = pl.program_id(0)
        o_ref[0] = _dot3(rb_ref[...], _rel_onehot(i), "nt")

    return pl.pallas_call(
        body, name="bias_expand", grid=(CHUNK,),
        in_specs=[pl.BlockSpec((B_HEADS, B_REL), lambda i: (0, 0))],
        out_specs=pl.BlockSpec((1, B_HEADS, B_BAND), lambda i: (i, 0, 0)),
        out_shape=jax.ShapeDtypeStruct((CHUNK, B_HEADS, B_BAND), F32),
        compiler_params=_cparams(("parallel",)),
    )(rel_bias)


def bias_reduce(dbias):
    def body(d_ref, o_ref):
        i = pl.program_id(0)

        @pl.when(i == 0)
        def _():
            o_ref[...] = jnp.zeros_like(o_ref)

        o_ref[...] += _dot3(d_ref[0], _rel_onehot(i), "nn")

    return pl.pallas_call(
        body, name="bias_reduce", grid=(CHUNK,),
        in_specs=[pl.BlockSpec((1, B_HEADS, B_BAND), lambda i: (i, 0, 0))],
        out_specs=pl.BlockSpec((B_HEADS, B_REL), lambda i: (0, 0)),
        out_shape=jax.ShapeDtypeStruct((B_HEADS, B_REL), F32),
        compiler_params=_cparams(("arbitrary",)),
    )(dbias)


def merge_fwd(gates_raw, b_gate, ya, yb):
    S = ya.shape[0]

    def body(g_ref, b_ref, ya_ref, yb_ref, o_ref):
        gt = _sigmoid(g_ref[...] + b_ref[...])
        o_ref[...] = (gt[:, :D_MODEL] * ya_ref[...] + gt[:, D_MODEL:] * yb_ref[...]).astype(BF16)

    return rowcall(body, name="merge_fwd", S=S, ts=512,
                   ins=[(gates_raw, "row"), (b_gate, "vec"), (ya, "row"), (yb, "row")],
                   outs=[((S, D_MODEL), BF16, "row")])[0]


def _ln_stats(xpre):
    mu = jnp.mean(xpre, axis=-1, keepdims=True)
    xc = xpre - mu
    rstd = lax.rsqrt(jnp.mean(xc * xc, axis=-1, keepdims=True) + LN_EPS)
    return xc * rstd, rstd


def ln1_fwd(x, mix, mod, ln_g, ln_b):
    S = x.shape[0]

    def body(x_ref, mix_ref, m_ref, g_ref, b_ref, xpre_ref, x1_ref, h2_ref):
        m = m_ref[...]
        xpre = ALPHA * x_ref[...] + m[GATE_T:GATE_T + 1] * mix_ref[...]
        xhat, _ = _ln_stats(xpre)
        x1 = xhat * g_ref[...] + b_ref[...]
        xpre_ref[...] = xpre
        x1_ref[...] = x1
        h2_ref[...] = (x1 * (1.0 + m[SCALE_F:SCALE_F + 1]) + m[SHIFT_F:SHIFT_F + 1]).astype(BF16)

    return rowcall(body, name="ln1_fwd", S=S, ts=512,
                   ins=[(x, "row"), (mix, "row"), (mod, "vec"), (ln_g, "vec"), (ln_b, "vec")],
                   outs=[((S, D_MODEL), F32, "row"), ((S, D_MODEL), F32, "row"), ((S, D_MODEL), BF16, "row")])


def ffn_act_fwd(up, conv_w, conv_b):
    S = up.shape[0]

    def body(u_ref, up_ref, w_ref, b_ref, o_ref):
        first = (pl.program_id(0) > 0).astype(F32)
        uc = _conv_fwd(u_ref[...], up_ref[...] * first, w_ref[...], FFN_CONV) + b_ref[...]
        o_ref[...] = (_silu(uc[:, :D_FF]) * uc[:, D_FF:]).astype(BF16)

    return rowcall(body, name="ffn_act_fwd", S=S, ts=128,
                   ins=[(up, "row"), (up, "prev"), (conv_w, "vec"), (conv_b, "vec")],
                   outs=[((S, D_FF), BF16, "row")])[0]


def final_fwd_bwd(x1, ffn, target, mod, ln_g, ln_b):
    S = x1.shape[0]

    def body(x1_ref, f_ref, t_ref, m_ref, g_ref, b_ref, dxpre_ref, dffn_ref, loss_ref, dgate_ref, dg_ref, db_ref):
        gate = m_ref[...][GATE_F:GATE_F + 1]
        ffn_v = f_ref[...]
        xpre = ALPHA * x1_ref[...] + gate * ffn_v
        xhat, rstd = _ln_stats(xpre)
        err = xhat * g_ref[...] + b_ref[...] - t_ref[...]
        loss_ref[...] += 0.5 * jnp.sum(jnp.mean(err * err, axis=-1, keepdims=True), axis=0, keepdims=True)
        dy = err * (1.0 / D_MODEL)
        dg_ref[...] += jnp.sum(dy * xhat, axis=0, keepdims=True)
        db_ref[...] += jnp.sum(dy, axis=0, keepdims=True)
        dyg = dy * g_ref[...]
        dxpre = rstd * (dyg - jnp.mean(dyg, axis=-1, keepdims=True) - xhat * jnp.mean(dyg * xhat, axis=-1, keepdims=True))
        dxpre_ref[...] = dxpre
        dffn_ref[...] = (gate * dxpre).astype(BF16)
        dgate_ref[...] += jnp.sum(dxpre * ffn_v, axis=0, keepdims=True)

    vec = ((1, D_MODEL), F32, "acc")
    return rowcall(body, name="final_fwd_bwd", S=S, ts=512,
                   ins=[(x1, "row"), (ffn, "row"), (target, "row"), (mod, "vec"), (ln_g, "vec"), (ln_b, "vec")],
                   outs=[((S, D_MODEL), F32, "row"), ((S, D_MODEL), BF16, "row"), ((1, 1), F32, "acc"), vec, vec, vec])


def _ffn_duc(dact, uc):
    ug, uv = uc[:, :D_FF], uc[:, D_FF:]
    sg, dsg = _silu_and_grad(ug)
    return jnp.concatenate([dact * uv * dsg, dact * sg], axis=1)


def ffn_act_bwd(dact, up, conv_w, conv_b):
    S = up.shape[0]
    ts = 128

    def body(d_ref, dn_ref, u_ref, up_ref, un_ref, w_ref, b_ref, dup_ref, dw_ref, db_ref):
        i = pl.program_id(0)
        first = (i > 0).astype(F32)
        last = (i < pl.num_programs(0) - 1).astype(F32)
        w, b = w_ref[...], b_ref[...]
        cur, prev = u_ref[...], up_ref[...] * first
        shifted = [_shift_down(cur, prev, FFN_CONV - 1 - j) for j in range(FFN_CONV)]
        uc = b + sum(shifted[j] * w[j:j + 1] for j in range(FFN_CONV))
        duc = _ffn_duc(d_ref[...], uc)
        uc_n = _conv_fwd(un_ref[...], cur[ts - 8:ts], w, FFN_CONV) + b
        duc_n = _ffn_duc(dn_ref[...], uc_n) * last
        db_ref[...] += jnp.sum(duc, axis=0, keepdims=True)
        for j in range(FFN_CONV):
            dw_ref[j:j + 1, :] += jnp.sum(duc * shifted[j], axis=0, keepdims=True)
        dup = duc * w[FFN_CONV - 1:FFN_CONV]
        for j in range(FFN_CONV - 1):
            dup = dup + _shift_up(duc, duc_n, FFN_CONV - 1 - j) * w[j:j + 1]
        dup_ref[...] = dup.astype(BF16)

    return rowcall(body, name="ffn_act_bwd", S=S, ts=ts,
                   ins=[(dact, "row"), (dact, "next"), (up, "row"), (up, "prev"), (up, "next"), (conv_w, "vec"), (conv_b, "vec")],
                   outs=[((S, 2 * D_FF), BF16, "row"), ((FFN_CONV, 2 * D_FF), F32, "acc"), ((1, 2 * D_FF), F32, "acc")])


def ln1_bwd(dxpre2, dh2, xpre1, mix, mod, ln_g, ln_b):
    S = xpre1.shape[0]

    def body(d2_ref, dh_ref, xp_ref, mix_ref, m_ref, g_ref, b_ref, dxpre_ref, dmix_ref,
             dscale_ref, dshift_ref, dgate_ref, dg_ref, db_ref):
        m = m_ref[...]
        xhat, rstd = _ln_stats(xp_ref[...])
        x1 = xhat * g_ref[...] + b_ref[...]
        dh = dh_ref[...]
        dx1 = ALPHA * d2_ref[...] + dh * (1.0 + m[SCALE_F:SCALE_F + 1])
        dscale_ref[...] += jnp.sum(dh * x1, axis=0, keepdims=True)
        dshift_ref[...] += jnp.sum(dh, axis=0, keepdims=True)
        dg_ref[...] += jnp.sum(dx1 * xhat, axis=0, keepdims=True)
        db_ref[...] += jnp.sum(dx1, axis=0, keepdims=True)
        dyg = dx1 * g_ref[...]
        dxpre = rstd * (dyg - jnp.mean(dyg, axis=-1, keepdims=True) - xhat * jnp.mean(dyg * xhat, axis=-1, keepdims=True))
        dxpre_ref[...] = dxpre
        dmix_ref[...] = (m[GATE_T:GATE_T + 1] * dxpre).astype(BF16)
        dgate_ref[...] += jnp.sum(dxpre * mix_ref[...], axis=0, keepdims=True)

    vec = ((1, D_MODEL), F32, "acc")
    return rowcall(body, name="ln1_bwd", S=S, ts=512,
                   ins=[(dxpre2, "row"), (dh2, "row"), (xpre1, "row"), (mix, "row"), (mod, "vec"), (ln_g, "vec"), (ln_b, "vec")],
                   outs=[((S, D_MODEL), F32, "row"), ((S, D_MODEL), BF16, "row"), vec, vec, vec, vec, vec])


def merge_bwd(dmerged, gates_raw, b_gate, ya, yb):
    S = ya.shape[0]

    def body(d_ref, g_ref, b_ref, ya_ref, yb_ref, dya_ref, dyb_ref, dg_ref, dbg_ref):
        gt = _sigmoid(g_ref[...] + b_ref[...])
        d = d_ref[...]
        ga, gb = gt[:, :D_MODEL], gt[:, D_MODEL:]
        dya_ref[...] = (d * ga).astype(BF16)
        dyb_ref[...] = (d * gb).astype(BF16)
        dgr = jnp.concatenate([d * ya_ref[...] * ga * (1.0 - ga), d * yb_ref[...] * gb * (1.0 - gb)], axis=1)
        dg_ref[...] = dgr.astype(BF16)
        dbg_ref[...] += jnp.sum(dgr, axis=0, keepdims=True)

    return rowcall(body, name="merge_bwd", S=S, ts=512,
                   ins=[(dmerged, "row"), (gates_raw, "row"), (b_gate, "vec"), (ya, "row"), (yb, "row")],
                   outs=[((S, D_MODEL), BF16, "row"), ((S, D_MODEL), BF16, "row"), ((S, 2 * D_MODEL), BF16, "row"),
                         ((1, 2 * D_MODEL), F32, "acc")])


def attn_bwd(qkv_pad, btile, do_b):
    S = qkv_pad.shape[0] - PAD_ROWS

    def body(q_ref, k_ref, v_ref, b_ref, do_ref, dq_ref, dk_ref, dv_ref, db_ref):
        n = pl.program_id(1)

        @pl.when(n == 0)
        def _():
            dk_ref[...] = jnp.zeros_like(dk_ref)
            dv_ref[...] = jnp.zeros_like(dv_ref)
            db_ref[...] = jnp.zeros_like(db_ref)

        start = pl.multiple_of(n * Q_TILE, Q_TILE)
        kb = k_ref[pl.ds(start, KEY_WIN), :]
        vb = v_ref[pl.ds(start, KEY_WIN), :]
        qv, dov = q_ref[...], do_ref[...]
        valid = _key_valid(n)
        dqs, dks, dvs = [], [], []
        for hh in range(HEADS_PER_GROUP):
            sl = slice(hh * B_DH, (hh + 1) * B_DH)
            p = _band_probs(qv[:, sl], kb[:, sl], b_ref[hh], valid)
            dp = _dot1(dov[:, sl], vb[:, sl], "nt")
            ds = p * (dp - jnp.sum(dp * p, axis=-1, keepdims=True))
            dbh = ds[0:CHUNK, 0:B_BAND]
            for qc in range(1, Q_CHUNKS):
                dbh = dbh + ds[qc * CHUNK:(qc + 1) * CHUNK, qc * CHUNK:qc * CHUNK + B_BAND]
            db_ref[hh] += dbh
            dsq = ds * (B_DH ** -0.5)
            dqs.append(_dot1(dsq, kb[:, sl], "nn"))
            dks.append(_dot1(dsq, qv[:, sl], "tn"))
            dvs.append(_dot1(p, dov[:, sl], "tn"))
        dq_ref[...] = jnp.concatenate(dqs, axis=1).astype(BF16)
        dk_ref[pl.ds(start, KEY_WIN), :] += jnp.concatenate(dks, axis=1)
        dv_ref[pl.ds(start, KEY_WIN), :] += jnp.concatenate(dvs, axis=1)

    col = pl.BlockSpec((PAD_ROWS + S, GROUP_W), lambda g, n: (0, g))
    tile = pl.BlockSpec((Q_TILE, GROUP_W), lambda g, n: (n, g))
    return pl.pallas_call(
        body, name="attn_bwd", grid=(N_GROUPS, S // Q_TILE), in_specs=_attn_specs(S) + [tile],
        out_specs=[tile, col, col, pl.BlockSpec((HEADS_PER_GROUP, CHUNK, B_BAND), lambda g, n: (g, 0, 0))],
        out_shape=[jax.ShapeDtypeStruct((S, B_W), BF16), jax.ShapeDtypeStruct((PAD_ROWS + S, B_W), F32),
                   jax.ShapeDtypeStruct((PAD_ROWS + S, B_W), F32), jax.ShapeDtypeStruct((B_HEADS, CHUNK, B_BAND), F32)],
        compiler_params=_cparams(("parallel", "arbitrary")),
    )(qkv_pad, qkv_pad, qkv_pad, btile, do_b)


def gate_a_bwd(do_a, o_pre, z, norm_w):
    S = o_pre.shape[0]

    def body(d_ref, o_ref, z_ref, nw_ref, dop_ref, dz_ref, dnw_ref):
        nw = nw_ref[...]
        acc = jnp.zeros((1, A_DK), F32)
        for h in range(A_HEADS):
            sl = slice(h * A_DK, (h + 1) * A_DK)
            oh, zh, dh = o_ref[:, sl], z_ref[:, sl], d_ref[:, sl]
            r = lax.rsqrt(jnp.mean(oh * oh, axis=-1, keepdims=True) + RMS_EPS)
            sz, dsz = _silu_and_grad(zh)
            dz_ref[:, sl] = (dh * oh * r * nw * dsz).astype(BF16)
            acc = acc + jnp.sum(dh * oh * r * sz, axis=0, keepdims=True)
            t = dh * nw * sz
            dop_ref[:, sl] = r * t - oh * (r * r * r) * jnp.mean(t * oh, axis=-1, keepdims=True)
        dnw_ref[...] += acc

    return rowcall(body, name="gate_a_bwd", S=S, ts=512,
                   ins=[(do_a, "row"), (o_pre, "row"), (z, "row"), (norm_w, "vec")],
                   outs=[((S, A_W), F32, "row"), ((S, A_W), BF16, "row"), ((1, A_DK), F32, "acc")])


def delta_bwd(q, k, v, beta, g, sprev, tinv, do):
    S = q.shape[0]
    n_chunks = S // CHUNK

    def body(q_ref, k_ref, v_ref, beta_ref, g_ref, sprev_ref, t_ref, do_ref,
             dq_ref, dk_ref, dv_ref, dbeta_ref, dg_ref, dstate_ref):
        @pl.when(pl.program_id(0) == 0)
        def _():
            dstate_ref[...] = jnp.zeros_like(dstate_ref)

        mk = _tri_masks()
        causal, strict, eye = mk["causal"], mk["strict"], mk["eye"]
        blk_end = (lax.broadcasted_iota(jnp.int32, (GROUP_ROWS, 1), 0) & (CHUNK - 1)) == CHUNK - 1
        lane = lax.broadcasted_iota(jnp.int32, (CHUNK, A_HEADS), 1)
        betav, gv = beta_ref[...], g_ref[...]
        dbeta_t = jnp.zeros((CHUNK, A_HEADS), F32)
        dg_t = jnp.zeros((CHUNK, A_HEADS), F32)
        for grp in range(N_HEAD_GROUPS):
            qs, ks, vs = _stack_heads(q_ref, grp), _stack_heads(k_ref, grp), _stack_heads(v_ref, grp)
            dos = _stack_heads(do_ref, grp)
            bs = _stack_cols(betav, grp)
            loc = _delta_local(qs, ks, vs, bs, _stack_cols(gv, grp), mk)
            gam, decay, kd, gl, gc = loc["gam"], loc["decay"], loc["kd"], loc["gl"], loc["gc"]
            tinv_g = t_ref[0, grp]
            rhs = jnp.concatenate([loc["vb"], loc["y"]], axis=1)
            uw = _dot3(tinv_g, rhs, "nn")
            qg = qs * gam
            egl = jnp.exp(gl)
            heads = range(HEAD_GROUP)
            hid = [grp * HEAD_GROUP + j for j in heads]
            s0 = [sprev_ref[0, h] for h in hid]
            ds1 = [dstate_ref[h] for h in hid]
            w = [_head_rows(uw, j)[:, A_DK:] for j in heads]
            vn = [_head_rows(uw, j)[:, :A_DK] - _dot1(w[j], s0[j], "nn") for j in heads]
            vns = jnp.concatenate(vn, axis=0)
            dvn_local = _dot1(loc["p"], dos, "tn")
            dvn = [_head_rows(dvn_local, j) + _dot1(_head_rows(kd, j), ds1[j], "nn") for j in heads]
            dvns = jnp.concatenate(dvn, axis=0)
            dp = jnp.where(causal, _dot1(dos, vns, "nt"), 0.0)
            dqg = jnp.concatenate([_dot1(_head_rows(dos, j), s0[j], "nt") for j in heads], axis=0)
            dq = dqg * gam
            dgc = jnp.sum(dqg * qg, axis=-1, keepdims=True)
            for j in heads:
                dstate_ref[hid[j]] = (_dot1(_head_rows(qg, j), _head_rows(dos, j), "tn")
                                      + egl[(j + 1) * CHUNK - 1:(j + 1) * CHUNK] * ds1[j] - _dot1(w[j], dvn[j], "tn"))
            dkd = jnp.concatenate([_dot1(vn[j], ds1[j], "nt") for j in heads], axis=0)
            dk = dkd * jnp.exp(gl - gc)
            t1 = jnp.sum(dkd * kd, axis=-1, keepdims=True)
            dgc = dgc - t1
            dgl = jnp.concatenate(
                [jnp.broadcast_to(jnp.sum(_head_rows(t1, j), axis=0, keepdims=True)
                                  + jnp.sum(jnp.sum(ds1[j] * s0[j], axis=-1, keepdims=True), axis=0, keepdims=True)
                                  * egl[(j + 1) * CHUNK - 1:(j + 1) * CHUNK], (CHUNK, 1)) for j in heads], axis=0)
            dgc = dgc + jnp.where(blk_end, dgl, 0.0)
            duw = jnp.concatenate([dvns, jnp.concatenate([-_dot1(dvn[j], s0[j], "nt") for j in heads], axis=0)], axis=1)
            dvby = _dot3(tinv_g, duw, "tn")
            dt = _dot3(duw, rhs, "nt")
            da = jnp.where(strict, -_dot3(_dot3(tinv_g, dt, "tn"), tinv_g, "nt"), 0.0)
            dm = da * decay
            dn = dp * decay
            e = da * loc["a"] + dp * loc["p"]
            dgc = dgc + jnp.sum(e, axis=1, keepdims=True) - _row_to_col(jnp.sum(e, axis=0, keepdims=True), eye)
            dkb = _dot1(dm, ks, "nn")
            dk = dk + _dot1(dm, loc["kb"], "tn")
            dq = dq + _dot1(dn, ks, "nn")
            dk = dk + _dot1(dn, qs, "tn")
            dy = dvby[:, A_DK:]
            dvb = dvby[:, :A_DK]
            dkb = dkb + dy * gam
            dgc = dgc + jnp.sum(dy * loc["y"], axis=-1, keepdims=True)
            dk = dk + dkb * bs
            dbeta = jnp.sum(dkb * ks, axis=-1, keepdims=True) + jnp.sum(dvb * vs, axis=-1, keepdims=True)
            dv = dvb * bs
            dgs = jnp.sum(jnp.where(mk["upper"], _col_to_row(dgc, eye), 0.0), axis=1, keepdims=True)
            for j in heads:
                sl = slice(hid[j] * A_DK, (hid[j] + 1) * A_DK)
                dq_ref[:, sl] = _head_rows(dq, j)
                dk_ref[:, sl] = _head_rows(dk, j)
                dv_ref[:, sl] = _head_rows(dv, j)
                dbeta_t = dbeta_t + jnp.where(lane == hid[j], _head_rows(dbeta, j), 0.0)
                dg_t = dg_t + jnp.where(lane == hid[j], _head_rows(dgs, j), 0.0)
        dbeta_ref[...] = dbeta_t
        dg_ref[...] = dg_t

    rev = lambda n: (n_chunks - 1 - n, 0)
    rev4 = lambda n: (n_chunks - 1 - n, 0, 0, 0)
    tile = pl.BlockSpec((CHUNK, A_W), rev)
    small = pl.BlockSpec((CHUNK, A_HEADS), rev)
    return pl.pallas_call(
        body, name="delta_bwd", grid=(n_chunks,),
        in_specs=[tile, tile, tile, small, small, pl.BlockSpec((1, A_HEADS, A_DK, A_DK), rev4),
                  pl.BlockSpec((1, N_HEAD_GROUPS, GROUP_ROWS, GROUP_ROWS), rev4), tile],
        out_specs=[tile, tile, tile, small, small],
        out_shape=[jax.ShapeDtypeStruct((S, A_W), F32)] * 3 + [jax.ShapeDtypeStruct((S, A_HEADS), F32)] * 2,
        scratch_shapes=[pltpu.VMEM((A_HEADS, A_DK, A_DK), F32)],
        compiler_params=_cparams(("arbitrary",)),
    )(q, k, v, beta, g, sprev, tinv, do)


def _prep_a_dpre(raw, raw_prev, w, dq, dk, dv):
    y, dy_dpre = _prep_a_core(raw, raw_prev, w)
    parts = []
    for h in range(A_HEADS):
        yq = y[:, h * A_DK:(h + 1) * A_DK]
        dqh = dq[:, h * A_DK:(h + 1) * A_DK]
        rq = lax.rsqrt(jnp.sum(yq * yq, axis=-1, keepdims=True) + L2_EPS)
        parts.append((A_DK ** -0.5) * (rq * dqh - yq * (rq * rq * rq) * jnp.sum(dqh * yq, axis=-1, keepdims=True)))
    for h in range(A_HEADS):
        yk = y[:, A_W + h * A_DK:A_W + (h + 1) * A_DK]
        dkh = dk[:, h * A_DK:(h + 1) * A_DK]
        rk = lax.rsqrt(jnp.sum(yk * yk, axis=-1, keepdims=True) + L2_EPS)
        parts.append(rk * dkh - yk * (rk * rk * rk) * jnp.sum(dkh * yk, axis=-1, keepdims=True))
    parts.append(dv)
    return jnp.concatenate(parts, axis=1) * dy_dpre


def prep_a_bwd(qkv_raw, ba, conv_a, a_log, dt_bias, dq, dk, dv, dbeta, dg):
    S = qkv_raw.shape[0]
    ts = 256

    def body(x_ref, xp_ref, xn_ref, ba_ref, w_ref, al_ref, dt_ref, dq_ref, dqn_ref, dk_ref, dkn_ref, dv_ref, dvn_ref,
             dbeta_ref, dg_ref, draw_ref, dba_ref, dw_ref, dal_ref, ddt_ref):
        i = pl.program_id(0)
        first = (i > 0).astype(F32)
        last = (i < pl.num_programs(0) - 1).astype(F32)
        w = w_ref[...]
        cur, prev = x_ref[...], xp_ref[...] * first
        dpre = _prep_a_dpre(cur, prev, w, dq_ref[...], dk_ref[...], dv_ref[...])
        dpre_n = _prep_a_dpre(xn_ref[...], cur[ts - 8:ts], w, dqn_ref[...], dkn_ref[...], dvn_ref[...]) * last
        for j in range(A_CONV):
            dw_ref[j:j + 1, :] += jnp.sum(dpre * _shift_down(cur, prev, A_CONV - 1 - j), axis=0, keepdims=True)
        draw = dpre * w[A_CONV - 1:A_CONV]
        for j in range(A_CONV - 1):
            draw = draw + _shift_up(dpre, dpre_n, A_CONV - 1 - j) * w[j:j + 1]
        draw_ref[...] = draw.astype(BF16)
        bav = ba_ref[...]
        beta = _sigmoid(bav[:, 0:A_HEADS])
        xa = bav[:, A_HEADS:2 * A_HEADS] + dt_ref[...]
        nexp = -jnp.exp(al_ref[...])
        dgv = dg_ref[...]
        da = dgv * nexp * _sigmoid(xa)
        dba_ref[:, 0:A_HEADS] = dbeta_ref[...] * beta * (1.0 - beta)
        dba_ref[:, A_HEADS:2 * A_HEADS] = da
        dal_ref[...] += jnp.sum(dgv * nexp * _softplus(xa), axis=0, keepdims=True)
        ddt_ref[...] += jnp.sum(da, axis=0, keepdims=True)

    return rowcall(
        body, name="prep_a_bwd", S=S, ts=ts,
        ins=[(qkv_raw, "row"), (qkv_raw, "prev"), (qkv_raw, "next"), (ba, "row"), (conv_a, "vec"), (a_log, "vec"),
             (dt_bias, "vec"), (dq, "row"), (dq, "next"), (dk, "row"), (dk, "next"), (dv, "row"), (dv, "next"),
             (dbeta, "row"), (dg, "row")],
        outs=[((S, 3 * A_W), BF16, "row"), ((S, 2 * A_HEADS), F32, "row"), ((A_CONV, 3 * A_W), F32, "acc"),
              ((1, A_HEADS), F32, "acc"), ((1, A_HEADS), F32, "acc")])


def grad_x_final(dh1, x, dxpre1, mod):
    S = x.shape[0]

    def body(dh_ref, x_ref, dx_ref, m_ref, gx_ref, dscale_ref, dshift_ref):
        dh = dh_ref[...]
        gx_ref[...] = ALPHA * dx_ref[...] + dh * (1.0 + m_ref[...][SCALE_T:SCALE_T + 1])
        dscale_ref[...] += jnp.sum(dh * x_ref[...], axis=0, keepdims=True)
        dshift_ref[...] += jnp.sum(dh, axis=0, keepdims=True)

    vec = ((1, D_MODEL), F32, "acc")
    return rowcall(body, name="grad_x_final", S=S, ts=512, ins=[(dh1, "row"), (x, "row"), (dxpre1, "row"), (mod, "vec")],
                   outs=[((S, D_MODEL), F32, "row"), vec, vec])


_C_QKV, _C_Z, _C_BA, _C_QKVB, _C_G = 0, 3 * A_W, 4 * A_W, 4 * A_W + 2 * A_HEADS, 4 * A_W + 2 * A_HEADS + 3 * B_W
BA_PAD = 128


def split_w_in(w_in):
    ba = jnp.pad(w_in[:, _C_BA:_C_QKVB], ((0, 0), (0, BA_PAD - 2 * A_HEADS)))
    return dict(qkv=w_in[:, _C_QKV:_C_Z], z=w_in[:, _C_Z:_C_BA], ba=ba, qkvb=w_in[:, _C_QKVB:_C_G], g=w_in[:, _C_G:])


def join_w_in(p):
    return jnp.concatenate([p["qkv"], p["z"], p["ba"][:, :2 * A_HEADS], p["qkvb"], p["g"]], axis=1)


def forward_local(x, target, mod, w, sm):
    h1 = modulate(x, mod, SHIFT_T, SCALE_T, "mod_t")
    qkv_raw = mm(h1, w["qkv"], mode="nn", out_dtype=F32, name="proj_qkv")
    z = mm(h1, w["z"], mode="nn", out_dtype=F32, name="proj_z")
    ba = mm(h1, w["ba"], mode="nn", out_dtype=F32, name="proj_ba")
    qkvb = mm(h1, w["qkvb"], mode="nn", out_dtype=BF16, name="proj_qkvb")
    gates_raw = mm(h1, w["g"], mode="nn", out_dtype=F32, name="proj_g")
    q, k, v, beta, g = prep_a_fwd(qkv_raw, ba, sm["conv_a"], sm["a_log"], sm["dt_bias"])
    o_pre, sprev, tinv = delta_fwd(q, k, v, beta, g)
    o_a = gate_a_fwd(o_pre, z, sm["norm_a"])
    qkv_pad = jnp.pad(qkvb, ((PAD_ROWS, 0), (0, 0)))
    bias = bias_tiles(jnp.transpose(bias_expand(sm["rel_bias"]), (1, 0, 2)))
    o_b = attn_fwd(qkv_pad, bias)
    ya = mm(o_a, w["branch_a"], mode="nn", out_dtype=F32, name="branch_a")
    yb = mm(o_b, w["branch_b"], mode="nn", out_dtype=F32, name="branch_b")
    merged = merge_fwd(gates_raw, sm["b_gate"], ya, yb)
    mix = mm(merged, w["o"], mode="nn", out_dtype=F32, name="mix")
    xpre1, x1, h2 = ln1_fwd(x, mix, mod, sm["ln1_g"], sm["ln1_b"])
    up = mm(h2, w["up"], mode="nn", out_dtype=F32, name="ffn_up")
    act = ffn_act_fwd(up, sm["conv_ffn"], sm["b_conv_ffn"])
    ffn = mm(act, w["down"], mode="nn", out_dtype=F32, name="ffn_down", tk=D_FF // 2)
    dxpre2, dffn, loss, dgate_f, dln2_g, dln2_b = final_fwd_bwd(x1, ffn, target, mod, sm["ln2_g"], sm["ln2_b"])
    saved = dict(h1=h1, qkv_raw=qkv_raw, z=z, ba=ba, gates_raw=gates_raw, q=q, k=k, v=v, beta=beta, g=g,
                 o_pre=o_pre, sprev=sprev, tinv=tinv, o_a=o_a, qkv_pad=qkv_pad, bias=bias, o_b=o_b, ya=ya, yb=yb,
                 merged=merged, mix=mix, xpre1=xpre1, x1=x1, h2=h2, up=up, act=act, ffn=ffn)
    return loss, dxpre2, dffn, dict(gate_f=dgate_f, ln2_g=dln2_g, ln2_b=dln2_b), saved


def backward_local(x, mod, w, sm, dxpre2, dffn, fin, sv):
    half_ff = D_FF // 2
    dact = mm(dffn, w["down"], mode="nt", out_dtype=F32, name="d_act", tn=half_ff)
    gw_down = mm(sv["act"], dffn, mode="tn", out_dtype=BF16, name="gw_down", tm=half_ff)
    dup, dconv_ffn, db_conv_ffn = ffn_act_bwd(dact, sv["up"], sm["conv_ffn"], sm["b_conv_ffn"])
    dh2 = mm(dup, w["up"], mode="nt", out_dtype=F32, name="d_h2", tk=half_ff)
    gw_up = mm(sv["h2"], dup, mode="tn", out_dtype=BF16, name="gw_up")
    dxpre1, dmix, dsc_f, dsh_f, dgate_t, dln1_g, dln1_b = ln1_bwd(
        dxpre2, dh2, sv["xpre1"], sv["mix"], mod, sm["ln1_g"], sm["ln1_b"])
    dmerged = mm(dmix, w["o"], mode="nt", out_dtype=F32, name="d_merged")
    gw_o = mm(sv["merged"], dmix, mode="tn", out_dtype=BF16, name="gw_o")
    dya, dyb, dgates, db_gate = merge_bwd(dmerged, sv["gates_raw"], sm["b_gate"], sv["ya"], sv["yb"])
    do_a = mm(dya, w["branch_a"], mode="nt", out_dtype=F32, name="d_oa")
    gw_branch_a = mm(sv["o_a"], dya, mode="tn", out_dtype=BF16, name="gw_branch_a")
    do_b = mm(dyb, w["branch_b"], mode="nt", out_dtype=BF16, name="d_ob")
    gw_branch_b = mm(sv["o_b"], dyb, mode="tn", out_dtype=BF16, name="gw_branch_b")
    dq_b, dk_pad, dv_pad, dbias = attn_bwd(sv["qkv_pad"], sv["bias"], do_b)
    dqkvb = jnp.concatenate([dq_b, dk_pad[PAD_ROWS:].astype(BF16), dv_pad[PAD_ROWS:].astype(BF16)], axis=1)
    drel_bias = bias_reduce(jnp.transpose(dbias, (1, 0, 2)))
    do_pre, dz, dnorm_a = gate_a_bwd(do_a, sv["o_pre"], sv["z"], sm["norm_a"])
    dq, dk, dv, dbeta, dg = delta_bwd(sv["q"], sv["k"], sv["v"], sv["beta"], sv["g"], sv["sprev"], sv["tinv"], do_pre)
    dqkv_raw, dba16, dconv_a, da_log, ddt_bias = prep_a_bwd(
        sv["qkv_raw"], sv["ba"], sm["conv_a"], sm["a_log"], sm["dt_bias"], dq, dk, dv, dbeta, dg)
    dba = jnp.pad(dba16, ((0, 0), (0, BA_PAD - 2 * A_HEADS))).astype(BF16)
    pieces = dict(qkv=dqkv_raw, z=dz, ba=dba, qkvb=dqkvb, g=dgates)
    dh1 = None
    gw_in = {}
    for key, dpiece in pieces.items():
        dh1 = mm(dpiece, w[key], mode="nt", out_dtype=F32, name="d_h1_" + key, acc_in=dh1)
        gw_in[key] = mm(sv["h1"], dpiece, mode="tn", out_dtype=BF16, name="gw_in_" + key)
    grad_x, dsc_t, dsh_t = grad_x_final(dh1, x, dxpre1, mod)
    dmod = jnp.concatenate([dsh_t, dsc_t, dgate_t, dsh_f, dsc_f, fin["gate_f"]], axis=0)
    gw = dict(w_in=join_w_in(gw_in), w_branch_a=gw_branch_a, w_branch_b=gw_branch_b, w_o=gw_o, w_up=gw_up, w_down=gw_down)
    gs = dict(b_gate=db_gate, conv_a=dconv_a, a_log=da_log, dt_bias=ddt_bias, norm_a=dnorm_a, rel_bias=drel_bias,
              ln1_g=dln1_g, ln1_b=dln1_b, conv_ffn=dconv_ffn, b_conv_ffn=db_conv_ffn, ln2_g=fin["ln2_g"], ln2_b=fin["ln2_b"])
    return grad_x, dmod, gw, gs


MESH = pl.DeviceIdType.MESH
ANY = pl.BlockSpec(memory_space=pl.ANY)
WHOLE_VMEM = pl.BlockSpec(memory_space=pltpu.VMEM)


def _place():
    return lax.axis_index("x"), lax.axis_index("y"), lax.axis_index("c")


def allgather8(blk, name):
    m_per, n = blk.shape

    def body(x_ref, out_ref, send_sems, recv_sems, local_sem):
        x, y, c = _place()
        me, sibling = (x, y, c), (x, y, 1 - c)
        chips = [(1 - x, y), (x, 1 - y), (1 - x, 1 - y)]

        def rows(px, py, pc):
            return out_ref.at[pl.ds((4 * px + 2 * py + pc) * m_per, m_per), :]

        def copy(k, block, to, src=None):
            return pltpu.make_async_remote_copy(
                src_ref=rows(*block) if src is None else src, dst_ref=rows(*block),
                send_sem=send_sems.at[k], recv_sem=recv_sems.at[k], device_id=to, device_id_type=MESH)

        mine = pltpu.make_async_copy(x_ref, rows(*me), local_sem)
        mine.start()
        first = [copy(0, me, sibling, src=x_ref)]
        first += [copy(1 + j, me, (*chip, c), src=x_ref) for j, chip in enumerate(chips)]
        for cp in first:
            cp.start()
        passed = [copy(4 + j, (*chip, c), sibling) for j, chip in enumerate(chips)]
        for j, chip in enumerate(chips):
            copy(1 + j, (*chip, c), me).wait_recv()
            passed[j].start()
        copy(0, sibling, me).wait_recv()
        for j, chip in enumerate(chips):
            copy(4 + j, (*chip, 1 - c), me).wait_recv()
        for cp in first + passed:
            cp.wait_send()
        mine.wait()

    return pl.pallas_call(
        body, name=name, out_shape=jax.ShapeDtypeStruct((N_DEV * m_per, n), blk.dtype),
        in_specs=[WHOLE_VMEM], out_specs=WHOLE_VMEM,
        scratch_shapes=[pltpu.SemaphoreType.DMA((7,)), pltpu.SemaphoreType.DMA((7,)), pltpu.SemaphoreType.DMA],
    )(blk)


def _chip_peers(x, y):
    return [(1 - x, y), (x, 1 - y), (1 - x, 1 - y)]


def chip_exchange(arrs, name, scatter):
    n = len(arrs)

    def body(*refs):
        ins, outs = refs[:n], refs[n:2 * n]
        send_sems, recv_sems, local_sems = refs[2 * n:]
        x, y, c = _place()
        me = 2 * x + y
        sibling = (x, y, 1 - c)
        peers = _chip_peers(x, y)

        def half(ref, which):
            r2 = ref.shape[0] // 2
            return ref.at[pl.ds(which * r2, r2), :]

        def outgoing(a, chip):
            return ins[a].at[chip] if scatter else ins[a]

        def copy(k, src, dst, to):
            return pltpu.make_async_remote_copy(src_ref=src, dst_ref=dst, send_sem=send_sems.at[k],
                                                recv_sem=recv_sems.at[k], device_id=to, device_id_type=MESH)

        started, local = [], []
        for a in range(n):
            lc = pltpu.make_async_copy(outgoing(a, me), outs[a].at[me], local_sems.at[a])
            lc.start()
            local.append(lc)
            for j, (px, py) in enumerate(peers):
                cp = copy(6 * a + j, half(outgoing(a, 2 * px + py), c), half(outs[a].at[me], c), (px, py, c))
                cp.start()
                started.append(cp)
        for a in range(n):
            for j, (px, py) in enumerate(peers):
                landed = half(outs[a].at[2 * px + py], c)
                copy(6 * a + j, landed, landed, (px, py, c)).wait_recv()
                relay = copy(6 * a + 3 + j, landed, landed, sibling)
                relay.start()
                started.append(relay)
        for a in range(n):
            for j, (px, py) in enumerate(peers):
                other = half(outs[a].at[2 * px + py], 1 - c)
                copy(6 * a + 3 + j, other, other, sibling).wait_recv()
        for cp in started:
            cp.wait_send()
        for lc in local:
            lc.wait()

    out_shape = [jax.ShapeDtypeStruct(a.shape if scatter else (N_CHIPS,) + a.shape, a.dtype) for a in arrs]
    return pl.pallas_call(
        body, name=name, out_shape=out_shape, in_specs=[ANY] * n, out_specs=[ANY] * n,
        scratch_shapes=[pltpu.SemaphoreType.DMA((6 * n,)), pltpu.SemaphoreType.DMA((6 * n,)), pltpu.SemaphoreType.DMA((n,))],
    )(*arrs)


def sibling_exchange(arrs, name):
    n = len(arrs)

    def body(*refs):
        ins, outs = refs[:n], refs[n:2 * n]
        send_sems, recv_sems = refs[2 * n:]
        x, y, c = _place()
        cps = [pltpu.make_async_remote_copy(src_ref=ins[a], dst_ref=outs[a], send_sem=send_sems.at[a],
                                            recv_sem=recv_sems.at[a], device_id=(x, y, 1 - c), device_id_type=MESH)
               for a in range(n)]
        for cp in cps:
            cp.start()
        for cp in cps:
            cp.wait()

    return pl.pallas_call(
        body, name=name, out_shape=[jax.ShapeDtypeStruct(a.shape, a.dtype) for a in arrs],
        in_specs=[ANY] * n, out_specs=[ANY] * n,
        scratch_shapes=[pltpu.SemaphoreType.DMA((n,)), pltpu.SemaphoreType.DMA((n,))],
    )(*arrs)


TILE_BYTES = 2 * 1024 * 1024


def _row_tile(rows, row_bytes):
    if rows * row_bytes <= TILE_BYTES or rows % 8:
        return rows
    best = 8
    for t in range(8, rows + 1, 8):
        if rows % t == 0 and t * row_bytes <= TILE_BYTES:
            best = t
    return best


def pair_add(a, b, name):
    R, C = a.shape
    tr = _row_tile(R, C * 4)

    def body(a_ref, b_ref, o_ref):
        o_ref[...] = (a_ref[...].astype(F32) + b_ref[...].astype(F32)).astype(BF16)

    spec = pl.BlockSpec((tr, C), lambda i: (i, 0))
    return pl.pallas_call(body, name=name, grid=(R // tr,), in_specs=[spec, spec], out_specs=spec,
                          out_shape=jax.ShapeDtypeStruct((R, C), BF16), compiler_params=_cparams(("parallel",)))(a, b)


def sum_lead(parts, name):
    K, R, C = parts.shape
    tr = _row_tile(R, C * 4)

    def body(p_ref, o_ref):
        acc = p_ref[0].astype(F32)
        for j in range(1, K):
            acc = acc + p_ref[j].astype(F32)
        o_ref[...] = acc

    return pl.pallas_call(
        body, name=name, grid=(R // tr,), in_specs=[pl.BlockSpec((K, tr, C), lambda i: (0, i, 0))],
        out_specs=pl.BlockSpec((tr, C), lambda i: (i, 0)), out_shape=jax.ShapeDtypeStruct((R, C), F32),
        compiler_params=_cparams(("parallel",)))(parts)


def adamw(w, g, m, v, name):
    R, C = w.shape
    tr = _row_tile(R, C * 4)

    def body(w_ref, g_ref, m_ref, v_ref, d_ref, mo_ref, vo_ref):
        gv = g_ref[...]
        m2 = ADAM_B1 * m_ref[...] + (1.0 - ADAM_B1) * gv
        v2 = ADAM_B2 * v_ref[...] + (1.0 - ADAM_B2) * (gv * gv)
        m_hat = m2 / (1.0 - ADAM_B1 ** ADAM_STEP)
        v_hat = v2 / (1.0 - ADAM_B2 ** ADAM_STEP)
        d_ref[...] = -ADAM_LR * (m_hat / (jnp.sqrt(v_hat) + ADAM_EPS) + ADAM_WD * w_ref[...])
        mo_ref[...] = m2
        vo_ref[...] = v2

    spec = pl.BlockSpec((tr, C), lambda i: (i, 0))
    return pl.pallas_call(body, name=name, grid=(R // tr,), in_specs=[spec] * 4, out_specs=[spec] * 3,
                          out_shape=[jax.ShapeDtypeStruct((R, C), F32)] * 3, compiler_params=_cparams(("parallel",)))(w, g, m, v)


LANES = 1024


def _pack(arrs, rows):
    out, offs, r = [], [], 0
    for a in arrs:
        flat = a.reshape(-1)
        nr = -(-flat.shape[0] // LANES)
        out.append(jnp.pad(flat, (0, nr * LANES - flat.shape[0])))
        offs.append(r)
        r += nr
    assert r <= rows, (r, rows)
    out.append(jnp.zeros(((rows - r) * LANES,), F32))
    return jnp.concatenate(out).reshape(rows, LANES), offs


def _unpack(packed, offs, shapes):
    flat = packed.reshape(-1)
    return [flat[o * LANES:o * LANES + math.prod(s)].reshape(s) for o, s in zip(offs, shapes)]


WEIGHTS = ["w_ada", "b_ada", "w_in", "b_gate", "conv_a", "a_log", "dt_bias", "norm_a", "rel_bias", "w_branch_a",
           "w_branch_b", "w_o", "ln1_g", "ln1_b", "w_up", "conv_ffn", "b_conv_ffn", "w_down", "ln2_g", "ln2_b"]
BIG = ["w_in", "w_branch_a", "w_branch_b", "w_o", "w_up", "w_down"]
COL_SHARDED = {"w_in", "w_up"}
SMALL_SHARDED = {"conv_a": 3 * A_W // N_CHIPS, "rel_bias": B_REL // N_CHIPS, "conv_ffn": 2 * D_FF // N_CHIPS}
SMALL = [n for n in WEIGHTS if n not in BIG and n != "w_ada"]


def _to_full(g4, name):
    if name in COL_SHARDED:
        return jnp.transpose(g4, (1, 0, 2)).reshape(g4.shape[1], -1)
    return g4.reshape(-1, g4.shape[2])


def _to_shards(full, name):
    if name in COL_SHARDED:
        return jnp.transpose(full.reshape(full.shape[0], N_CHIPS, -1), (1, 0, 2))
    return full.reshape(N_CHIPS, -1, full.shape[1])


def kernel(x, c, w_ada, b_ada, w_in, b_gate, conv_a, a_log, dt_bias, norm_a, rel_bias, w_branch_a, w_branch_b, w_o, ln1_g, ln1_b, w_up, conv_ffn, b_conv_ffn, w_down, ln2_g, ln2_b, loss_target, m_w_ada, m_b_ada, m_w_in, m_b_gate, m_conv_a, m_a_log, m_dt_bias, m_norm_a, m_rel_bias, m_w_branch_a, m_w_branch_b, m_w_o, m_ln1_g, m_ln1_b, m_w_up, m_conv_ffn, m_b_conv_ffn, m_w_down, m_ln2_g, m_ln2_b, v_w_ada, v_b_ada, v_w_in, v_b_gate, v_conv_a, v_a_log, v_dt_bias, v_norm_a, v_rel_bias, v_w_branch_a, v_w_branch_b, v_w_o, v_ln1_g, v_ln1_b, v_w_up, v_conv_ffn, v_b_conv_ffn, v_w_down, v_ln2_g, v_ln2_b):
    args = dict(locals())
    wts = {n: args[n] for n in WEIGHTS}
    moms = {n: args["m_" + n] for n in WEIGHTS}
    vars_ = {n: args["v_" + n] for n in WEIGHTS}
    xi, yi, ci = _place()
    chip = 2 * xi + yi
    dev = 4 * xi + 2 * yi + ci
    ada_cols = w_ada.shape[2]

    c_all = allgather8(jnp.pad(c, ((0, 7), (0, 0))), "gather_c").reshape(N_DEV, 8, D_MODEL)[:, 0]
    b_ada_sh = lax.dynamic_slice(b_ada, (0, chip * ada_cols), (1, ada_cols))
    mod_sh = ada_fwd(c_all, w_ada[0], b_ada_sh)
    mod_g = allgather8(mod_sh, "gather_mod").reshape(N_CHIPS, 2, N_DEV, ada_cols)[:, 0]
    mod = lax.dynamic_slice(mod_g, (0, dev, 0), (N_CHIPS, 1, ada_cols)).reshape(6, D_MODEL)

    big_full = chip_exchange([wts[n][0].astype(BF16) for n in BIG], "gather_weights", scatter=False)
    w = {n: _to_full(g4, n) for n, g4 in zip(BIG, big_full)}
    wd = dict(split_w_in(w["w_in"]), branch_a=w["w_branch_a"], branch_b=w["w_branch_b"], o=w["w_o"], up=w["w_up"], down=w["w_down"])
    sshapes = [wts[n].shape[1:] for n in SMALL_SHARDED]
    spack, soffs = _pack([wts[n][0] for n in SMALL_SHARDED], 16)
    sg = allgather8(spack, "gather_small_w").reshape(N_CHIPS, 2, 16, LANES)[:, 0]
    sparts = [_unpack(sg[j], soffs, sshapes) for j in range(N_CHIPS)]
    sm = {n: wts[n] for n in SMALL if n not in SMALL_SHARDED and n != "b_ada"}
    for i, n in enumerate(SMALL_SHARDED):
        sm[n] = jnp.concatenate([sparts[j][i] for j in range(N_CHIPS)], axis=-1)

    loss, dxpre2, dffn, fin, sv = forward_local(x[0], loss_target[0], mod, wd, sm)
    grad_x, dmod, gw, gs = backward_local(x[0], mod, wd, sm, dxpre2, dffn, fin, sv)

    gnames = [n for n in SMALL if n != "b_ada"]
    vec, voffs = _pack([dmod] + [gs[n] for n in gnames] + [loss], 56)
    gathered = allgather8(vec, "gather_small_g").reshape(N_DEV, 56, LANES)
    summed = sum_lead(gathered, "sum_small_g")
    full_shapes = [(6, D_MODEL)] + [gs[n].shape for n in gnames] + [(1, 1)]
    parts = _unpack(summed, voffs, full_shapes)
    grads = {"b_ada": parts[0].reshape(1, -1)}
    for n, p in zip(gnames, parts[1:-1]):
        if n in SMALL_SHARDED:
            p = lax.dynamic_slice_in_dim(p, chip * SMALL_SHARDED[n], SMALL_SHARDED[n], axis=1)
        grads[n] = p.reshape(wts[n].shape)
    loss_total = parts[-1].reshape(())
    dmod_all = gathered[:, 0:6, :].reshape(N_DEV, 6 * D_MODEL)
    grads["w_ada"] = ada_bwd(c_all, lax.dynamic_slice(dmod_all, (0, chip * ada_cols), (N_DEV, ada_cols)))[None]

    mine = [gw[n] for n in BIG]
    theirs = sibling_exchange(mine, "grad_sibling")
    chip_sums = [_to_shards(pair_add(a, b, "grad_pair_" + n), n) for n, a, b in zip(BIG, mine, theirs)]
    received = chip_exchange(chip_sums, "grad_scatter", scatter=True)
    for n, r in zip(BIG, received):
        grads[n] = sum_lead(r, "grad_sum_" + n)[None]

    delta, new_m, new_v = {}, {}, {}
    for n in ["w_ada"] + BIG:
        d, m2, v2 = adamw(wts[n][0], grads[n][0], moms[n][0], vars_[n][0], "adamw_" + n)
        delta[n], new_m[n], new_v[n] = d[None], m2[None], v2[None]
    shapes = [wts[n].shape for n in SMALL]
    packs = [_pack([t[n] for n in SMALL], 32) for t in (wts, grads, moms, vars_)]
    outs = adamw(*[p[0] for p in packs], "adamw_small")
    for res, o in zip((delta, new_m, new_v), outs):
        for n, a in zip(SMALL, _unpack(o, packs[0][1], shapes)):
            res[n] = a
    return (loss_total, grad_x[None], *[grads[n] for n in WEIGHTS], *[delta[n] for n in WEIGHTS],
            *[new_m[n] for n in WEIGHTS], *[new_v[n] for n in WEIGHTS])
```

```python
import functools
import math

import jax
import jax.numpy as jnp
from jax import lax
from jax.experimental import pallas as pl
from jax.experimental.pallas import tpu as pltpu

F32 = jnp.float32
BF16 = jnp.bfloat16

D_MODEL = 1024
CHUNK = 64
A_HEADS = 8
A_DK = 128
A_W = A_HEADS * A_DK
A_CONV = 4
B_HEADS = 16
B_DH = 64
B_W = B_HEADS * B_DH
B_PREV = 8
B_BAND = (B_PREV + 1) * CHUNK
B_MAX_REL = 256
B_REL = CHUNK - 1 + B_MAX_REL + 1
D_FF = 2816
FFN_CONV = 3
IN_COLS = 4 * A_W + 2 * A_HEADS + 3 * B_W + 2 * D_MODEL
ALPHA = 2.0 ** 0.25
LN_EPS = 1e-5
RMS_EPS = 1e-6
L2_EPS = 1e-6
NEG_INF = -1e30
ADAM_LR, ADAM_B1, ADAM_B2, ADAM_EPS, ADAM_WD, ADAM_STEP = 0.001, 0.9, 0.999, 1e-08, 0.01, 10
N_CHIPS = 4
N_DEV = 8
VMEM_LIMIT = 56 * 1024 * 1024


def _cparams(sem=None):
    return pltpu.CompilerParams(dimension_semantics=sem, vmem_limit_bytes=VMEM_LIMIT)


_DIMS = {"nn": (((1,), (0,)), ((), ())), "nt": (((1,), (1,)), ((), ())), "tn": (((0,), (0,)), ((), ()))}


MM_TILE_CAP = 1408


def _mm_tile(n):
    return max(t for t in range(128, min(n, MM_TILE_CAP) + 1, 128) if n % t == 0)


def mm(a, b, *, mode, out_dtype, name, acc_in=None):
    if mode == "nn":
        (M, K), (K2, N) = a.shape, b.shape
    elif mode == "nt":
        (M, K), (N, K2) = a.shape, b.shape
    else:
        (K, M), (K2, N) = a.shape, b.shape
    assert K == K2, (a.shape, b.shape, mode)
    tm, tn, tk = _mm_tile(M), _mm_tile(N), _mm_tile(K)
    nk = K // tk

    def body(*refs):
        if acc_in is None:
            a_ref, b_ref, o_ref, acc_ref = refs
        else:
            a_ref, b_ref, c_ref, o_ref, acc_ref = refs
        k = pl.program_id(2)

        @pl.when(k == 0)
        def _():
            if acc_in is None:
                acc_ref[...] = jnp.zeros_like(acc_ref)
            else:
                acc_ref[...] = c_ref[...]

        acc_ref[...] += lax.dot_general(a_ref[...].astype(BF16), b_ref[...].astype(BF16), _DIMS[mode],
                                        preferred_element_type=F32)

        @pl.when(k == nk - 1)
        def _():
            o_ref[...] = acc_ref[...].astype(out_dtype)

    a_spec = pl.BlockSpec((tk, tm), lambda i, j, k: (k, i)) if mode == "tn" else pl.BlockSpec((tm, tk), lambda i, j, k: (i, k))
    b_spec = pl.BlockSpec((tn, tk), lambda i, j, k: (j, k)) if mode == "nt" else pl.BlockSpec((tk, tn), lambda i, j, k: (k, j))
    o_spec = pl.BlockSpec((tm, tn), lambda i, j, k: (i, j))
    ins, in_specs, aliases = [a, b], [a_spec, b_spec], {}
    if acc_in is not None:
        assert acc_in.shape == (M, N) and acc_in.dtype == F32 and out_dtype == F32
        ins.append(acc_in)
        in_specs.append(o_spec)
        aliases = {2: 0}
    return pl.pallas_call(
        body, name=name, grid=(M // tm, N // tn, nk), in_specs=in_specs, out_specs=o_spec,
        out_shape=jax.ShapeDtypeStruct((M, N), out_dtype), scratch_shapes=[pltpu.VMEM((tm, tn), F32)],
        input_output_aliases=aliases, compiler_params=_cparams(("parallel", "parallel", "arbitrary")),
    )(*ins)


def rowcall(body, *, name, S, ts, ins, outs):
    assert S % ts == 0 and ts % 16 == 0
    nsteps = S // ts
    in_specs, arrays = [], []
    for arr, kind in ins:
        arrays.append(arr)
        if kind == "row":
            in_specs.append(pl.BlockSpec((ts, arr.shape[1]), lambda i: (i, 0)))
        elif kind in ("prev", "next"):
            hr = 8 * (4 // arr.dtype.itemsize)
            per, last = ts // hr, S // hr - 1
            if kind == "prev":
                in_specs.append(pl.BlockSpec((hr, arr.shape[1]), lambda i, per=per: (jnp.maximum(i * per - 1, 0), 0)))
            else:
                in_specs.append(pl.BlockSpec((hr, arr.shape[1]), lambda i, per=per, last=last: (jnp.minimum((i + 1) * per, last), 0)))
        else:
            nd = arr.ndim
            in_specs.append(pl.BlockSpec(arr.shape, lambda i, nd=nd: (0,) * nd))
    out_specs, out_shapes, acc_idx = [], [], []
    for n, (shape, dtype, kind) in enumerate(outs):
        out_shapes.append(jax.ShapeDtypeStruct(shape, dtype))
        if kind == "row":
            out_specs.append(pl.BlockSpec((ts, shape[1]), lambda i: (i, 0)))
        else:
            nd = len(shape)
            out_specs.append(pl.BlockSpec(shape, lambda i, nd=nd: (0,) * nd))
            acc_idx.append(n)
    n_in = len(arrays)

    def wrapped(*refs):
        @pl.when(pl.program_id(0) == 0)
        def _():
            for n in acc_idx:
                refs[n_in + n][...] = jnp.zeros_like(refs[n_in + n])

        body(*refs)

    res = pl.pallas_call(
        wrapped, name=name, grid=(nsteps,), in_specs=in_specs, out_specs=out_specs, out_shape=out_shapes,
        compiler_params=_cparams(("arbitrary",) if acc_idx else ("parallel",)),
    )(*arrays)
    return res


def _halo_prev(ref):
    v = ref[...].astype(F32)
    return v[v.shape[0] - 8:]


def _halo_next(ref):
    return ref[...].astype(F32)[:8]


def _shift_down(cur, prev8, k):
    if k == 0:
        return cur
    rolled = pltpu.roll(cur, k, axis=0)
    fix = pltpu.roll(prev8, k, axis=0)
    row = lax.broadcasted_iota(jnp.int32, (8, 1), 0)
    top = jnp.where(row < k, fix, rolled[0:8])
    if cur.shape[0] == 8:
        return top
    return jnp.concatenate([top, rolled[8:]], axis=0)


def _shift_up(cur, next8, k):
    if k == 0:
        return cur
    n = cur.shape[0]
    rolled = pltpu.roll(cur, n - k, axis=0)
    fix = pltpu.roll(next8, 8 - k, axis=0)
    row = lax.broadcasted_iota(jnp.int32, (8, 1), 0)
    bot = jnp.where(row >= 8 - k, fix, rolled[n - 8:n])
    return jnp.concatenate([rolled[:n - 8], bot], axis=0)


def _sigmoid(x):
    return 1.0 / (1.0 + jnp.exp(-x))


def _silu(x):
    return x * _sigmoid(x)


def _silu_and_grad(x):
    s = _sigmoid(x)
    return x * s, s * (1.0 + x * (1.0 - s))


def _softplus(x):
    return jnp.maximum(x, 0.0) + jnp.log1p(jnp.exp(-jnp.abs(x)))


def _split2(x):
    hi = x.astype(BF16)
    return hi, (x - hi.astype(F32)).astype(BF16)


def _dot1(a, b, mode):
    return lax.dot_general(a.astype(BF16), b.astype(BF16), _DIMS[mode], preferred_element_type=F32)


def _dot3(a, b, mode):
    ah, al = _split2(a)
    bh, bl = _split2(b)
    d = lambda p, q: lax.dot_general(p, q, _DIMS[mode], preferred_element_type=F32)
    return d(ah, bh) + (d(ah, bl) + d(al, bh))


def ada_fwd(c_all, w_sh, b_sh):
    n = w_sh.shape[1]
    tn = 512

    def body(c_ref, w_ref, b_ref, o_ref):
        o_ref[...] = _dot1(_silu(c_ref[...]), w_ref[...], "nn") + b_ref[...]

    return pl.pallas_call(
        body, name="ada_fwd", grid=(n // tn,),
        in_specs=[pl.BlockSpec((N_DEV, D_MODEL), lambda j: (0, 0)), pl.BlockSpec((D_MODEL, tn), lambda j: (0, j)),
                  pl.BlockSpec((1, tn), lambda j: (0, j))],
        out_specs=pl.BlockSpec((N_DEV, tn), lambda j: (0, j)), out_shape=jax.ShapeDtypeStruct((N_DEV, n), F32),
        compiler_params=_cparams(("parallel",)),
    )(c_all, w_sh, b_sh)


def ada_bwd(c_all, dmod_sh):
    n = dmod_sh.shape[1]
    tn = 512

    def body(c_ref, d_ref, o_ref):
        o_ref[...] = _dot1(_silu(c_ref[...]), d_ref[...], "tn")

    return pl.pallas_call(
        body, name="ada_bwd", grid=(n // tn,),
        in_specs=[pl.BlockSpec((N_DEV, D_MODEL), lambda j: (0, 0)), pl.BlockSpec((N_DEV, tn), lambda j: (0, j))],
        out_specs=pl.BlockSpec((D_MODEL, tn), lambda j: (0, j)), out_shape=jax.ShapeDtypeStruct((D_MODEL, n), F32),
        compiler_params=_cparams(("parallel",)),
    )(c_all, dmod_sh)


SHIFT_T, SCALE_T, GATE_T, SHIFT_F, SCALE_F, GATE_F = range(6)


def modulate(x, mod, shift_row, scale_row, name):
    S = x.shape[0]

    def body(x_ref, m_ref, o_ref):
        m = m_ref[...]
        o_ref[...] = (x_ref[...] * (1.0 + m[scale_row:scale_row + 1]) + m[shift_row:shift_row + 1]).astype(BF16)

    return rowcall(body, name=name, S=S, ts=512, ins=[(x, "row"), (mod, "vec")], outs=[((S, D_MODEL), BF16, "row")])[0]


def _conv_fwd(cur, prev, w, width):
    y = cur * w[width - 1:width]
    for j in range(width - 1):
        y = y + _shift_down(cur, prev, width - 1 - j) * w[j:j + 1]
    return y


def _prep_a_core(cur, prev, w):
    return _silu_and_grad(_conv_fwd(cur, prev, w, A_CONV))


def prep_a_fwd(qkv_raw, ba, conv_a, a_log, dt_bias):
    S = qkv_raw.shape[0]

    def body(x_ref, xp_ref, ba_ref, w_ref, al_ref, dt_ref, q_ref, k_ref, v_ref, beta_ref, g_ref):
        first = (pl.program_id(0) > 0).astype(F32)
        y, _ = _prep_a_core(x_ref[...].astype(F32), _halo_prev(xp_ref) * first, w_ref[...])
        for h in range(A_HEADS):
            sl = slice(h * A_DK, (h + 1) * A_DK)
            qh = y[:, sl]
            kh = y[:, A_W + h * A_DK:A_W + (h + 1) * A_DK]
            q_ref[:, sl] = qh * (lax.rsqrt(jnp.sum(qh * qh, axis=-1, keepdims=True) + L2_EPS) * (A_DK ** -0.5))
            k_ref[:, sl] = kh * lax.rsqrt(jnp.sum(kh * kh, axis=-1, keepdims=True) + L2_EPS)
        v_ref[...] = y[:, 2 * A_W:3 * A_W]
        bav = ba_ref[...]
        beta_ref[...] = _sigmoid(bav[:, 0:A_HEADS])
        g_ref[...] = -jnp.exp(al_ref[...]) * _softplus(bav[:, A_HEADS:2 * A_HEADS] + dt_ref[...])

    return rowcall(
        body, name="prep_a_fwd", S=S, ts=256,
        ins=[(qkv_raw, "row"), (qkv_raw, "prev"), (ba, "row"), (conv_a, "vec"), (a_log, "vec"), (dt_bias, "vec")],
        outs=[((S, A_W), F32, "row")] * 3 + [((S, A_HEADS), F32, "row")] * 2)


HEAD_GROUP = 4
GROUP_ROWS = HEAD_GROUP * CHUNK
N_HEAD_GROUPS = A_HEADS // HEAD_GROUP
LOG_CHUNK = int(math.log2(CHUNK))


def _tri_masks():
    rb = lax.broadcasted_iota(jnp.int32, (GROUP_ROWS, GROUP_ROWS), 0)
    cb = lax.broadcasted_iota(jnp.int32, (GROUP_ROWS, GROUP_ROWS), 1)
    same = (rb >> LOG_CHUNK) == (cb >> LOG_CHUNK)
    return dict(causal=same & (rb >= cb), strict=same & (rb > cb), eye=rb == cb, upper=same & (cb >= rb),
                last=cb == (rb | (CHUNK - 1)), rb=rb, cb=cb)


def _col_to_row(colv, eye):
    return jnp.sum(jnp.where(eye, colv, 0.0), axis=0, keepdims=True)


def _row_to_col(rowv, eye):
    return jnp.sum(jnp.where(eye, rowv, 0.0), axis=1, keepdims=True)


def _tri_inv(a_list, mk):
    rb, cb = mk["rb"], mk["cb"]
    ts = [jnp.where(mk["eye"], 1.0, 0.0) - jnp.where((rb >> 1) == (cb >> 1), a, 0.0) for a in a_list]
    for lvl in range(1, LOG_CHUNK):
        rs, cs = rb >> lvl, cb >> lvl
        sel = ((rs & 1) == 1) & (cs == rs - 1)
        inner = [_dot3(t, jnp.where(sel, a, 0.0), "nn") for t, a in zip(ts, a_list)]
        ts = [t - _dot3(i, t, "nn") for i, t in zip(inner, ts)]
    return ts


def _stack_heads(ref, grp):
    return jnp.concatenate([ref[:, (grp * HEAD_GROUP + j) * A_DK:(grp * HEAD_GROUP + j + 1) * A_DK]
                            for j in range(HEAD_GROUP)], axis=0)


def _stack_cols(tile, grp):
    return jnp.concatenate([tile[:, grp * HEAD_GROUP + j:grp * HEAD_GROUP + j + 1] for j in range(HEAD_GROUP)], axis=0)


def _delta_local(q, k, v, beta, g, mk):
    causal, strict, eye = mk["causal"], mk["strict"], mk["eye"]
    g_row = _col_to_row(g, eye)
    gc = jnp.sum(jnp.where(causal, g_row, 0.0), axis=1, keepdims=True)
    gc_row = _col_to_row(gc, eye)
    decay = jnp.where(causal, jnp.exp(jnp.where(causal, gc - gc_row, 0.0)), 0.0)
    gam = jnp.exp(gc)
    kb = k * beta
    vb = v * beta
    y = kb * gam
    a = jnp.where(strict, _dot1(kb, k, "nt") * decay, 0.0)
    p = jnp.where(causal, _dot1(q, k, "nt") * decay, 0.0)
    gl = jnp.sum(jnp.where(mk["last"], gc_row, 0.0), axis=1, keepdims=True)
    kd = k * jnp.exp(gl - gc)
    return dict(gc=gc, decay=decay, gam=gam, kb=kb, vb=vb, y=y, a=a, p=p, gl=gl, kd=kd)


def _head_rows(x, j):
    return x[j * CHUNK:(j + 1) * CHUNK]


def delta_fwd(q, k, v, beta, g):
    S = q.shape[0]
    n_chunks = S // CHUNK

    def body(q_ref, k_ref, v_ref, beta_ref, g_ref, o_ref, sprev_ref, t_ref, state_ref):
        @pl.when(pl.program_id(0) == 0)
        def _():
            state_ref[...] = jnp.zeros_like(state_ref)

        mk = _tri_masks()
        betav, gv = beta_ref[...], g_ref[...]
        groups = range(N_HEAD_GROUPS)
        q_all = [_stack_heads(q_ref, grp) for grp in groups]
        locs = [_delta_local(q_all[grp], _stack_heads(k_ref, grp), _stack_heads(v_ref, grp),
                             _stack_cols(betav, grp), _stack_cols(gv, grp), mk) for grp in groups]
        tinvs = _tri_inv([loc["a"] for loc in locs], mk)
        uws = [_dot3(tinvs[grp], jnp.concatenate([locs[grp]["vb"], locs[grp]["y"]], axis=1), "nn") for grp in groups]
        for grp in groups:
            loc, uw = locs[grp], uws[grp]
            t_ref[0, grp] = tinvs[grp]
            qg = q_all[grp] * loc["gam"]
            egl = jnp.exp(loc["gl"])
            vns, o_state = [], []
            for j in range(HEAD_GROUP):
                h = grp * HEAD_GROUP + j
                s0 = state_ref[h]
                sprev_ref[0, h] = s0
                uw_h = _head_rows(uw, j)
                vn = uw_h[:, :A_DK] - _dot1(uw_h[:, A_DK:], s0, "nn")
                vns.append(vn)
                o_state.append(_dot1(_head_rows(qg, j), s0, "nn"))
                state_ref[h] = s0 * egl[(j + 1) * CHUNK - 1:(j + 1) * CHUNK] + _dot1(_head_rows(loc["kd"], j), vn, "tn")
            o_local = _dot1(loc["p"], jnp.concatenate(vns, axis=0), "nn")
            for j in range(HEAD_GROUP):
                h = grp * HEAD_GROUP + j
                o_ref[:, h * A_DK:(h + 1) * A_DK] = o_state[j] + _head_rows(o_local, j)

    tile = pl.BlockSpec((CHUNK, A_W), lambda n: (n, 0))
    small = pl.BlockSpec((CHUNK, A_HEADS), lambda n: (n, 0))
    return pl.pallas_call(
        body, name="delta_fwd", grid=(n_chunks,), in_specs=[tile, tile, tile, small, small],
        out_specs=[tile, pl.BlockSpec((1, A_HEADS, A_DK, A_DK), lambda n: (n, 0, 0, 0)),
                   pl.BlockSpec((1, N_HEAD_GROUPS, GROUP_ROWS, GROUP_ROWS), lambda n: (n, 0, 0, 0))],
        out_shape=[jax.ShapeDtypeStruct((S, A_W), F32), jax.ShapeDtypeStruct((n_chunks, A_HEADS, A_DK, A_DK), F32),
                   jax.ShapeDtypeStruct((n_chunks, N_HEAD_GROUPS, GROUP_ROWS, GROUP_ROWS), F32)],
        scratch_shapes=[pltpu.VMEM((A_HEADS, A_DK, A_DK), F32)],
        compiler_params=_cparams(("arbitrary",)),
    )(q, k, v, beta, g)


def gate_a_fwd(o_pre, z, norm_w):
    S = o_pre.shape[0]

    def body(o_ref, z_ref, nw_ref, out_ref):
        nw = nw_ref[...]
        for h in range(A_HEADS):
            sl = slice(h * A_DK, (h + 1) * A_DK)
            oh = o_ref[:, sl]
            r = lax.rsqrt(jnp.mean(oh * oh, axis=-1, keepdims=True) + RMS_EPS)
            out_ref[:, sl] = (oh * r * nw * _silu(z_ref[:, sl])).astype(BF16)

    return rowcall(body, name="gate_a_fwd", S=S, ts=512, ins=[(o_pre, "row"), (z, "row"), (norm_w, "vec")],
                   outs=[((S, A_W), BF16, "row")])[0]


HEADS_PER_GROUP = 2
GROUP_W = HEADS_PER_GROUP * B_DH
N_GROUPS = B_HEADS // HEADS_PER_GROUP
PAD_ROWS = B_PREV * CHUNK


Q_TILE = 256
Q_CHUNKS = Q_TILE // CHUNK
KEY_WIN = (B_PREV + Q_CHUNKS) * CHUNK


def bias_tiles(bias):
    rows = [jnp.pad(bias, ((0, 0), (0, 0), (qc * CHUNK, (Q_CHUNKS - 1 - qc) * CHUNK)), constant_values=NEG_INF)
            for qc in range(Q_CHUNKS)]
    return jnp.concatenate(rows, axis=1)


def _band_probs(qh, kh, bias, valid):
    s = _dot1(qh, kh, "nt") * (B_DH ** -0.5) + bias
    s = jnp.where(valid, s, NEG_INF)
    e = jnp.exp(s - jnp.max(s, axis=-1, keepdims=True))
    return e / jnp.sum(e, axis=-1, keepdims=True)


def _attn_specs(S):
    n_cb = B_W // GROUP_W
    return [pl.BlockSpec((Q_TILE, GROUP_W), lambda g, n: (n + PAD_ROWS // Q_TILE, g)),
            pl.BlockSpec((PAD_ROWS + S, GROUP_W), lambda g, n: (0, n_cb + g)),
            pl.BlockSpec((PAD_ROWS + S, GROUP_W), lambda g, n: (0, 2 * n_cb + g)),
            pl.BlockSpec((HEADS_PER_GROUP, Q_TILE, KEY_WIN), lambda g, n: (g, 0, 0))]


def _key_valid(n):
    return lax.broadcasted_iota(jnp.int32, (Q_TILE, KEY_WIN), 1) >= PAD_ROWS - n * Q_TILE


def attn_fwd(qkv_pad, btile):
    S = qkv_pad.shape[0] - PAD_ROWS

    def body(q_ref, k_ref, v_ref, b_ref, o_ref):
        n = pl.program_id(1)
        start = pl.multiple_of(n * Q_TILE, Q_TILE)
        kb = k_ref[pl.ds(start, KEY_WIN), :]
        vb = v_ref[pl.ds(start, KEY_WIN), :]
        qv = q_ref[...]
        valid = _key_valid(n)
        outs = []
        for hh in range(HEADS_PER_GROUP):
            sl = slice(hh * B_DH, (hh + 1) * B_DH)
            p = _band_probs(qv[:, sl], kb[:, sl], b_ref[hh], valid)
            outs.append(_dot1(p, vb[:, sl], "nn"))
        o_ref[...] = jnp.concatenate(outs, axis=1).astype(BF16)

    return pl.pallas_call(
        body, name="attn_fwd", grid=(N_GROUPS, S // Q_TILE), in_specs=_attn_specs(S),
        out_specs=pl.BlockSpec((Q_TILE, GROUP_W), lambda g, n: (n, g)),
        out_shape=jax.ShapeDtypeStruct((S, B_W), BF16),
        compiler_params=_cparams(("parallel", "arbitrary")),
    )(qkv_pad, qkv_pad, qkv_pad, btile)


def _rel_onehot(i):
    kj = lax.broadcasted_iota(jnp.int32, (B_BAND, B_REL), 0)
    r = lax.broadcasted_iota(jnp.int32, (B_BAND, B_REL), 1)
    idx = jnp.clip(PAD_ROWS + i - kj, -(CHUNK - 1), B_MAX_REL) + (CHUNK - 1)
    return jnp.where(idx == r, 1.0, 0.0)


def bias_expand(rel_bias):
    def body(rb_ref, o_ref):
        i = pl.program_id(0)
        o_ref[0] = _dot3(rb_ref[...], _rel_onehot(i), "nt")

    return pl.pallas_call(
        body, name="bias_expand", grid=(CHUNK,),
        in_specs=[pl.BlockSpec((B_HEADS, B_REL), lambda i: (0, 0))],
        out_specs=pl.BlockSpec((1, B_HEADS, B_BAND), lambda i: (i, 0, 0)),
        out_shape=jax.ShapeDtypeStruct((CHUNK, B_HEADS, B_BAND), F32),
        compiler_params=_cparams(("parallel",)),
    )(rel_bias)


def bias_reduce(dbias):
    def body(d_ref, o_ref):
        i = pl.program_id(0)

        @pl.when(i == 0)
        def _():
            o_ref[...] = jnp.zeros_like(o_ref)

        o_ref[...] += _dot3(d_ref[0], _rel_onehot(i), "nn")

    return pl.pallas_call(
        body, name="bias_reduce", grid=(CHUNK,),
        in_specs=[pl.BlockSpec((1, B_HEADS, B_BAND), lambda i: (i, 0, 0))],
        out_specs=pl.BlockSpec((B_HEADS, B_REL), lambda i: (0, 0)),
        out_shape=jax.ShapeDtypeStruct((B_HEADS, B_REL), F32),
        compiler_params=_cparams(("arbitrary",)),
    )(dbias)


def merge_fwd(gates_raw, b_gate, ya, yb):
    S = ya.shape[0]

    def body(g_ref, b_ref, ya_ref, yb_ref, o_ref):
        gt = _sigmoid(g_ref[...] + b_ref[...])
        o_ref[...] = (gt[:, :D_MODEL] * ya_ref[...] + gt[:, D_MODEL:] * yb_ref[...]).astype(BF16)

    return rowcall(body, name="merge_fwd", S=S, ts=512,
                   ins=[(gates_raw, "row"), (b_gate, "vec"), (ya, "row"), (yb, "row")],
                   outs=[((S, D_MODEL), BF16, "row")])[0]


def _ln_stats(xpre):
    mu = jnp.mean(xpre, axis=-1, keepdims=True)
    xc = xpre - mu
    rstd = lax.rsqrt(jnp.mean(xc * xc, axis=-1, keepdims=True) + LN_EPS)
    return xc * rstd, rstd


def ln1_fwd(x, mix, mod, ln_g, ln_b):
    S = x.shape[0]

    def body(x_ref, mix_ref, m_ref, g_ref, b_ref, xpre_ref, x1_ref, h2_ref):
        m = m_ref[...]
        xpre = ALPHA * x_ref[...] + m[GATE_T:GATE_T + 1] * mix_ref[...]
        xhat, _ = _ln_stats(xpre)
        x1 = xhat * g_ref[...] + b_ref[...]
        xpre_ref[...] = xpre
        x1_ref[...] = x1
        h2_ref[...] = (x1 * (1.0 + m[SCALE_F:SCALE_F + 1]) + m[SHIFT_F:SHIFT_F + 1]).astype(BF16)

    return rowcall(body, name="ln1_fwd", S=S, ts=512,
                   ins=[(x, "row"), (mix, "row"), (mod, "vec"), (ln_g, "vec"), (ln_b, "vec")],
                   outs=[((S, D_MODEL), F32, "row"), ((S, D_MODEL), F32, "row"), ((S, D_MODEL), BF16, "row")])


def ffn_act_fwd(up, conv_w, conv_b):
    S = up.shape[0]

    def body(u_ref, up_ref, w_ref, b_ref, o_ref):
        first = (pl.program_id(0) > 0).astype(F32)
        uc = _conv_fwd(u_ref[...].astype(F32), _halo_prev(up_ref) * first, w_ref[...], FFN_CONV) + b_ref[...]
        o_ref[...] = (_silu(uc[:, :D_FF]) * uc[:, D_FF:]).astype(BF16)

    return rowcall(body, name="ffn_act_fwd", S=S, ts=128,
                   ins=[(up, "row"), (up, "prev"), (conv_w, "vec"), (conv_b, "vec")],
                   outs=[((S, D_FF), BF16, "row")])[0]


def final_fwd_bwd(x1, ffn, target, mod, ln_g, ln_b):
    S = x1.shape[0]

    def body(x1_ref, f_ref, t_ref, m_ref, g_ref, b_ref, dxpre_ref, dffn_ref, loss_ref, dgate_ref, dg_ref, db_ref):
        gate = m_ref[...][GATE_F:GATE_F + 1]
        ffn_v = f_ref[...]
        xpre = ALPHA * x1_ref[...] + gate * ffn_v
        xhat, rstd = _ln_stats(xpre)
        err = xhat * g_ref[...] + b_ref[...] - t_ref[...]
        loss_ref[...] += 0.5 * jnp.sum(jnp.mean(err * err, axis=-1, keepdims=True), axis=0, keepdims=True)
        dy = err * (1.0 / D_MODEL)
        dg_ref[...] += jnp.sum(dy * xhat, axis=0, keepdims=True)
        db_ref[...] += jnp.sum(dy, axis=0, keepdims=True)
        dyg = dy * g_ref[...]
        dxpre = rstd * (dyg - jnp.mean(dyg, axis=-1, keepdims=True) - xhat * jnp.mean(dyg * xhat, axis=-1, keepdims=True))
        dxpre_ref[...] = dxpre
        dffn_ref[...] = (gate * dxpre).astype(BF16)
        dgate_ref[...] += jnp.sum(dxpre * ffn_v, axis=0, keepdims=True)

    vec = ((1, D_MODEL), F32, "acc")
    return rowcall(body, name="final_fwd_bwd", S=S, ts=512,
                   ins=[(x1, "row"), (ffn, "row"), (target, "row"), (mod, "vec"), (ln_g, "vec"), (ln_b, "vec")],
                   outs=[((S, D_MODEL), F32, "row"), ((S, D_MODEL), BF16, "row"), ((1, 1), F32, "acc"), vec, vec, vec])


def _ffn_duc(dact, uc):
    ug, uv = uc[:, :D_FF], uc[:, D_FF:]
    sg, dsg = _silu_and_grad(ug)
    return jnp.concatenate([dact * uv * dsg, dact * sg], axis=1)


def ffn_act_bwd(dact, up, conv_w, conv_b):
    S = up.shape[0]
    ts = 128

    def body(d_ref, dn_ref, u_ref, up_ref, un_ref, w_ref, b_ref, dup_ref, dw_ref, db_ref):
        i = pl.program_id(0)
        first = (i > 0).astype(F32)
        last = (i < pl.num_programs(0) - 1).astype(F32)
        w, b = w_ref[...], b_ref[...]
        cur, prev = u_ref[...].astype(F32), _halo_prev(up_ref) * first
        shifted = [_shift_down(cur, prev, FFN_CONV - 1 - j) for j in range(FFN_CONV)]
        uc = b + sum(shifted[j] * w[j:j + 1] for j in range(FFN_CONV))
        duc = _ffn_duc(d_ref[...].astype(F32), uc)
        uc_n = _conv_fwd(_halo_next(un_ref), cur[ts - 8:ts], w, FFN_CONV) + b
        duc_n = _ffn_duc(_halo_next(dn_ref), uc_n) * last
        db_ref[...] += jnp.sum(duc, axis=0, keepdims=True)
        for j in range(FFN_CONV):
            dw_ref[j:j + 1, :] += jnp.sum(duc * shifted[j], axis=0, keepdims=True)
        dup = duc * w[FFN_CONV - 1:FFN_CONV]
        for j in range(FFN_CONV - 1):
            dup = dup + _shift_up(duc, duc_n, FFN_CONV - 1 - j) * w[j:j + 1]
        dup_ref[...] = dup.astype(BF16)

    return rowcall(body, name="ffn_act_bwd", S=S, ts=ts,
                   ins=[(dact, "row"), (dact, "next"), (up, "row"), (up, "prev"), (up, "next"), (conv_w, "vec"), (conv_b, "vec")],
                   outs=[((S, 2 * D_FF), BF16, "row"), ((FFN_CONV, 2 * D_FF), F32, "acc"), ((1, 2 * D_FF), F32, "acc")])


def ln1_bwd(dxpre2, dh2, xpre1, mix, mod, ln_g, ln_b):
    S = xpre1.shape[0]

    def body(d2_ref, dh_ref, xp_ref, mix_ref, m_ref, g_ref, b_ref, dxpre_ref, dmix_ref,
             dscale_ref, dshift_ref, dgate_ref, dg_ref, db_ref):
        m = m_ref[...]
        xhat, rstd = _ln_stats(xp_ref[...])
        x1 = xhat * g_ref[...] + b_ref[...]
        dh = dh_ref[...]
        dx1 = ALPHA * d2_ref[...] + dh * (1.0 + m[SCALE_F:SCALE_F + 1])
        dscale_ref[...] += jnp.sum(dh * x1, axis=0, keepdims=True)
        dshift_ref[...] += jnp.sum(dh, axis=0, keepdims=True)
        dg_ref[...] += jnp.sum(dx1 * xhat, axis=0, keepdims=True)
        db_ref[...] += jnp.sum(dx1, axis=0, keepdims=True)
        dyg = dx1 * g_ref[...]
        dxpre = rstd * (dyg - jnp.mean(dyg, axis=-1, keepdims=True) - xhat * jnp.mean(dyg * xhat, axis=-1, keepdims=True))
        dxpre_ref[...] = dxpre
        dmix_ref[...] = (m[GATE_T:GATE_T + 1] * dxpre).astype(BF16)
        dgate_ref[...] += jnp.sum(dxpre * mix_ref[...], axis=0, keepdims=True)

    vec = ((1, D_MODEL), F32, "acc")
    return rowcall(body, name="ln1_bwd", S=S, ts=512,
                   ins=[(dxpre2, "row"), (dh2, "row"), (xpre1, "row"), (mix, "row"), (mod, "vec"), (ln_g, "vec"), (ln_b, "vec")],
                   outs=[((S, D_MODEL), F32, "row"), ((S, D_MODEL), BF16, "row"), vec, vec, vec, vec, vec])


def merge_bwd(dmerged, gates_raw, b_gate, ya, yb):
    S = ya.shape[0]

    def body(d_ref, g_ref, b_ref, ya_ref, yb_ref, dya_ref, dyb_ref, dg_ref, dbg_ref):
        gt = _sigmoid(g_ref[...] + b_ref[...])
        d = d_ref[...]
        ga, gb = gt[:, :D_MODEL], gt[:, D_MODEL:]
        dya_ref[...] = (d * ga).astype(BF16)
        dyb_ref[...] = (d * gb).astype(BF16)
        dgr = jnp.concatenate([d * ya_ref[...] * ga * (1.0 - ga), d * yb_ref[...] * gb * (1.0 - gb)], axis=1)
        dg_ref[...] = dgr.astype(BF16)
        dbg_ref[...] += jnp.sum(dgr, axis=0, keepdims=True)

    return rowcall(body, name="merge_bwd", S=S, ts=512,
                   ins=[(dmerged, "row"), (gates_raw, "row"), (b_gate, "vec"), (ya, "row"), (yb, "row")],
                   outs=[((S, D_MODEL), BF16, "row"), ((S, D_MODEL), BF16, "row"), ((S, 2 * D_MODEL), BF16, "row"),
                         ((1, 2 * D_MODEL), F32, "acc")])


def attn_bwd(qkv_pad, btile, do_b):
    S = qkv_pad.shape[0] - PAD_ROWS

    def body(q_ref, k_ref, v_ref, b_ref, do_ref, dq_ref, dk_ref, dv_ref, db_ref):
        n = pl.program_id(1)

        @pl.when(n == 0)
        def _():
            dk_ref[...] = jnp.zeros_like(dk_ref)
            dv_ref[...] = jnp.zeros_like(dv_ref)
            db_ref[...] = jnp.zeros_like(db_ref)

        start = pl.multiple_of(n * Q_TILE, Q_TILE)
        kb = k_ref[pl.ds(start, KEY_WIN), :]
        vb = v_ref[pl.ds(start, KEY_WIN), :]
        qv, dov = q_ref[...], do_ref[...]
        valid = _key_valid(n)
        dqs, dks, dvs = [], [], []
        for hh in range(HEADS_PER_GROUP):
            sl = slice(hh * B_DH, (hh + 1) * B_DH)
            p = _band_probs(qv[:, sl], kb[:, sl], b_ref[hh], valid)
            dp = _dot1(dov[:, sl], vb[:, sl], "nt")
            ds = p * (dp - jnp.sum(dp * p, axis=-1, keepdims=True))
            dbh = ds[0:CHUNK, 0:B_BAND]
            for qc in range(1, Q_CHUNKS):
                dbh = dbh + ds[qc * CHUNK:(qc + 1) * CHUNK, qc * CHUNK:qc * CHUNK + B_BAND]
            db_ref[hh] += dbh
            dsq = ds * (B_DH ** -0.5)
            dqs.append(_dot1(dsq, kb[:, sl], "nn"))
            dks.append(_dot1(dsq, qv[:, sl], "tn"))
            dvs.append(_dot1(p, dov[:, sl], "tn"))
        dq_ref[...] = jnp.concatenate(dqs, axis=1).astype(BF16)
        dk_ref[pl.ds(start, KEY_WIN), :] += jnp.concatenate(dks, axis=1)
        dv_ref[pl.ds(start, KEY_WIN), :] += jnp.concatenate(dvs, axis=1)

    col = pl.BlockSpec((PAD_ROWS + S, GROUP_W), lambda g, n: (0, g))
    tile = pl.BlockSpec((Q_TILE, GROUP_W), lambda g, n: (n, g))
    return pl.pallas_call(
        body, name="attn_bwd", grid=(N_GROUPS, S // Q_TILE), in_specs=_attn_specs(S) + [tile],
        out_specs=[tile, col, col, pl.BlockSpec((HEADS_PER_GROUP, CHUNK, B_BAND), lambda g, n: (g, 0, 0))],
        out_shape=[jax.ShapeDtypeStruct((S, B_W), BF16), jax.ShapeDtypeStruct((PAD_ROWS + S, B_W), F32),
                   jax.ShapeDtypeStruct((PAD_ROWS + S, B_W), F32), jax.ShapeDtypeStruct((B_HEADS, CHUNK, B_BAND), F32)],
        compiler_params=_cparams(("parallel", "arbitrary")),
    )(qkv_pad, qkv_pad, qkv_pad, btile, do_b)


def gate_a_bwd(do_a, o_pre, z, norm_w):
    S = o_pre.shape[0]

    def body(d_ref, o_ref, z_ref, nw_ref, dop_ref, dz_ref, dnw_ref):
        nw = nw_ref[...]
        acc = jnp.zeros((1, A_DK), F32)
        for h in range(A_HEADS):
            sl = slice(h * A_DK, (h + 1) * A_DK)
            oh, zh, dh = o_ref[:, sl], z_ref[:, sl], d_ref[:, sl]
            r = lax.rsqrt(jnp.mean(oh * oh, axis=-1, keepdims=True) + RMS_EPS)
            sz, dsz = _silu_and_grad(zh)
            dz_ref[:, sl] = (dh * oh * r * nw * dsz).astype(BF16)
            acc = acc + jnp.sum(dh * oh * r * sz, axis=0, keepdims=True)
            t = dh * nw * sz
            dop_ref[:, sl] = r * t - oh * (r * r * r) * jnp.mean(t * oh, axis=-1, keepdims=True)
        dnw_ref[...] += acc

    return rowcall(body, name="gate_a_bwd", S=S, ts=512,
                   ins=[(do_a, "row"), (o_pre, "row"), (z, "row"), (norm_w, "vec")],
                   outs=[((S, A_W), F32, "row"), ((S, A_W), BF16, "row"), ((1, A_DK), F32, "acc")])


def delta_bwd(q, k, v, beta, g, sprev, tinv, do):
    S = q.shape[0]
    n_chunks = S // CHUNK

    def body(q_ref, k_ref, v_ref, beta_ref, g_ref, sprev_ref, t_ref, do_ref,
             dq_ref, dk_ref, dv_ref, dbeta_ref, dg_ref, dstate_ref):
        @pl.when(pl.program_id(0) == 0)
        def _():
            dstate_ref[...] = jnp.zeros_like(dstate_ref)

        mk = _tri_masks()
        causal, strict, eye = mk["causal"], mk["strict"], mk["eye"]
        blk_end = (lax.broadcasted_iota(jnp.int32, (GROUP_ROWS, 1), 0) & (CHUNK - 1)) == CHUNK - 1
        lane = lax.broadcasted_iota(jnp.int32, (CHUNK, A_HEADS), 1)
        betav, gv = beta_ref[...], g_ref[...]
        dbeta_t = jnp.zeros((CHUNK, A_HEADS), F32)
        dg_t = jnp.zeros((CHUNK, A_HEADS), F32)
        groups, heads = range(N_HEAD_GROUPS), range(HEAD_GROUP)
        st = [dict() for _ in groups]

        def local_part(grp, s):
            s["qs"], s["ks"], s["vs"] = _stack_heads(q_ref, grp), _stack_heads(k_ref, grp), _stack_heads(v_ref, grp)
            s["dos"] = _stack_heads(do_ref, grp)
            s["bs"] = _stack_cols(betav, grp)
            s["loc"] = loc = _delta_local(s["qs"], s["ks"], s["vs"], s["bs"], _stack_cols(gv, grp), mk)
            s["tinv"] = t_ref[0, grp]
            s["rhs"] = jnp.concatenate([loc["vb"], loc["y"]], axis=1)
            s["uw"] = _dot3(s["tinv"], s["rhs"], "nn")

        def state_part(grp, s):
            loc, uw, dos, qs = s["loc"], s["uw"], s["dos"], s["qs"]
            gam, kd, gl, gc = loc["gam"], loc["kd"], loc["gl"], loc["gc"]
            qg = qs * gam
            egl = jnp.exp(gl)
            hid = [grp * HEAD_GROUP + j for j in heads]
            s0 = [sprev_ref[0, h] for h in hid]
            ds1 = [dstate_ref[h] for h in hid]
            w = [_head_rows(uw, j)[:, A_DK:] for j in heads]
            vn = [_head_rows(uw, j)[:, :A_DK] - _dot1(w[j], s0[j], "nn") for j in heads]
            vns = jnp.concatenate(vn, axis=0)
            dvn_local = _dot1(loc["p"], dos, "tn")
            dvn = [_head_rows(dvn_local, j) + _dot1(_head_rows(kd, j), ds1[j], "nn") for j in heads]
            dvns = jnp.concatenate(dvn, axis=0)
            s["dp"] = jnp.where(causal, _dot1(dos, vns, "nt"), 0.0)
            dqg = jnp.concatenate([_dot1(_head_rows(dos, j), s0[j], "nt") for j in heads], axis=0)
            s["dq"] = dqg * gam
            dgc = jnp.sum(dqg * qg, axis=-1, keepdims=True)
            for j in heads:
                dstate_ref[hid[j]] = (_dot1(_head_rows(qg, j), _head_rows(dos, j), "tn")
                                      + egl[(j + 1) * CHUNK - 1:(j + 1) * CHUNK] * ds1[j] - _dot1(w[j], dvn[j], "tn"))
            dkd = jnp.concatenate([_dot1(vn[j], ds1[j], "nt") for j in heads], axis=0)
            s["dk"] = dkd * jnp.exp(gl - gc)
            t1 = jnp.sum(dkd * kd, axis=-1, keepdims=True)
            dgl = jnp.concatenate(
                [jnp.broadcast_to(jnp.sum(_head_rows(t1, j), axis=0, keepdims=True)
                                  + jnp.sum(jnp.sum(ds1[j] * s0[j], axis=-1, keepdims=True), axis=0, keepdims=True)
                                  * egl[(j + 1) * CHUNK - 1:(j + 1) * CHUNK], (CHUNK, 1)) for j in heads], axis=0)
            s["dgc"] = dgc - t1 + jnp.where(blk_end, dgl, 0.0)
            s["duw"] = jnp.concatenate(
                [dvns, jnp.concatenate([-_dot1(dvn[j], s0[j], "nt") for j in heads], axis=0)], axis=1)

        def solve_part(grp, s):
            s["dvby"] = _dot3(s["tinv"], s["duw"], "tn")
            s["dt"] = _dot3(s["duw"], s["rhs"], "nt")

        def inverse_part_a(grp, s):
            s["tdt"] = _dot3(s["tinv"], s["dt"], "tn")

        def inverse_part_b(grp, s):
            s["da"] = jnp.where(strict, -_dot3(s["tdt"], s["tinv"], "nt"), 0.0)

        def finish(grp, s):
            loc, qs, ks, vs, bs, da, dp, dvby = s["loc"], s["qs"], s["ks"], s["vs"], s["bs"], s["da"], s["dp"], s["dvby"]
            gam, decay = loc["gam"], loc["decay"]
            dm = da * decay
            dn = dp * decay
            e = da * loc["a"] + dp * loc["p"]
            dgc = s["dgc"] + jnp.sum(e, axis=1, keepdims=True) - _row_to_col(jnp.sum(e, axis=0, keepdims=True), eye)
            dy = dvby[:, A_DK:]
            dvb = dvby[:, :A_DK]
            dkb = _dot1(dm, ks, "nn") + dy * gam
            dk = s["dk"] + _dot1(dm, loc["kb"], "tn") + _dot1(dn, qs, "tn") + dkb * bs
            dq = s["dq"] + _dot1(dn, ks, "nn")
            dgc = dgc + jnp.sum(dy * loc["y"], axis=-1, keepdims=True)
            dbeta = jnp.sum(dkb * ks, axis=-1, keepdims=True) + jnp.sum(dvb * vs, axis=-1, keepdims=True)
            dv = dvb * bs
            dgs = jnp.sum(jnp.where(mk["upper"], _col_to_row(dgc, eye), 0.0), axis=1, keepdims=True)
            for j in heads:
                h = grp * HEAD_GROUP + j
                sl = slice(h * A_DK, (h + 1) * A_DK)
                dq_ref[:, sl] = _head_rows(dq, j)
                dk_ref[:, sl] = _head_rows(dk, j)
                dv_ref[:, sl] = _head_rows(dv, j)
            s["dbeta"], s["dgs"] = dbeta, dgs

        for stage in (local_part, state_part, solve_part, inverse_part_a, inverse_part_b, finish):
            for grp in groups:
                stage(grp, st[grp])
        for grp in groups:
            for j in heads:
                h = grp * HEAD_GROUP + j
                dbeta_t = dbeta_t + jnp.where(lane == h, _head_rows(st[grp]["dbeta"], j), 0.0)
                dg_t = dg_t + jnp.where(lane == h, _head_rows(st[grp]["dgs"], j), 0.0)
        dbeta_ref[...] = dbeta_t
        dg_ref[...] = dg_t

    rev = lambda n: (n_chunks - 1 - n, 0)
    rev4 = lambda n: (n_chunks - 1 - n, 0, 0, 0)
    tile = pl.BlockSpec((CHUNK, A_W), rev)
    small = pl.BlockSpec((CHUNK, A_HEADS), rev)
    return pl.pallas_call(
        body, name="delta_bwd", grid=(n_chunks,),
        in_specs=[tile, tile, tile, small, small, pl.BlockSpec((1, A_HEADS, A_DK, A_DK), rev4),
                  pl.BlockSpec((1, N_HEAD_GROUPS, GROUP_ROWS, GROUP_ROWS), rev4), tile],
        out_specs=[tile, tile, tile, small, small],
        out_shape=[jax.ShapeDtypeStruct((S, A_W), F32)] * 3 + [jax.ShapeDtypeStruct((S, A_HEADS), F32)] * 2,
        scratch_shapes=[pltpu.VMEM((A_HEADS, A_DK, A_DK), F32)],
        compiler_params=_cparams(("arbitrary",)),
    )(q, k, v, beta, g, sprev, tinv, do)


def _prep_a_dpre(raw, raw_prev, w, dq, dk, dv):
    y, dy_dpre = _prep_a_core(raw, raw_prev, w)
    parts = []
    for h in range(A_HEADS):
        yq = y[:, h * A_DK:(h + 1) * A_DK]
        dqh = dq[:, h * A_DK:(h + 1) * A_DK]
        rq = lax.rsqrt(jnp.sum(yq * yq, axis=-1, keepdims=True) + L2_EPS)
        parts.append((A_DK ** -0.5) * (rq * dqh - yq * (rq * rq * rq) * jnp.sum(dqh * yq, axis=-1, keepdims=True)))
    for h in range(A_HEADS):
        yk = y[:, A_W + h * A_DK:A_W + (h + 1) * A_DK]
        dkh = dk[:, h * A_DK:(h + 1) * A_DK]
        rk = lax.rsqrt(jnp.sum(yk * yk, axis=-1, keepdims=True) + L2_EPS)
        parts.append(rk * dkh - yk * (rk * rk * rk) * jnp.sum(dkh * yk, axis=-1, keepdims=True))
    parts.append(dv)
    return jnp.concatenate(parts, axis=1) * dy_dpre


def prep_a_bwd(qkv_raw, ba, conv_a, a_log, dt_bias, dq, dk, dv, dbeta, dg):
    S = qkv_raw.shape[0]
    ts = 256

    def body(x_ref, xp_ref, xn_ref, ba_ref, w_ref, al_ref, dt_ref, dq_ref, dqn_ref, dk_ref, dkn_ref, dv_ref, dvn_ref,
             dbeta_ref, dg_ref, draw_ref, dba_ref, dw_ref, dal_ref, ddt_ref):
        i = pl.program_id(0)
        first = (i > 0).astype(F32)
        last = (i < pl.num_programs(0) - 1).astype(F32)
        w = w_ref[...]
        cur, prev = x_ref[...].astype(F32), _halo_prev(xp_ref) * first
        dpre = _prep_a_dpre(cur, prev, w, dq_ref[...], dk_ref[...], dv_ref[...])
        dpre_n = _prep_a_dpre(_halo_next(xn_ref), cur[ts - 8:ts], w, _halo_next(dqn_ref), _halo_next(dkn_ref),
                              _halo_next(dvn_ref)) * last
        for j in range(A_CONV):
            dw_ref[j:j + 1, :] += jnp.sum(dpre * _shift_down(cur, prev, A_CONV - 1 - j), axis=0, keepdims=True)
        draw = dpre * w[A_CONV - 1:A_CONV]
        for j in range(A_CONV - 1):
            draw = draw + _shift_up(dpre, dpre_n, A_CONV - 1 - j) * w[j:j + 1]
        draw_ref[...] = draw.astype(BF16)
        bav = ba_ref[...]
        beta = _sigmoid(bav[:, 0:A_HEADS])
        xa = bav[:, A_HEADS:2 * A_HEADS] + dt_ref[...]
        nexp = -jnp.exp(al_ref[...])
        dgv = dg_ref[...]
        da = dgv * nexp * _sigmoid(xa)
        dba_ref[:, 0:A_HEADS] = dbeta_ref[...] * beta * (1.0 - beta)
        dba_ref[:, A_HEADS:2 * A_HEADS] = da
        dal_ref[...] += jnp.sum(dgv * nexp * _softplus(xa), axis=0, keepdims=True)
        ddt_ref[...] += jnp.sum(da, axis=0, keepdims=True)

    return rowcall(
        body, name="prep_a_bwd", S=S, ts=ts,
        ins=[(qkv_raw, "row"), (qkv_raw, "prev"), (qkv_raw, "next"), (ba, "row"), (conv_a, "vec"), (a_log, "vec"),
             (dt_bias, "vec"), (dq, "row"), (dq, "next"), (dk, "row"), (dk, "next"), (dv, "row"), (dv, "next"),
             (dbeta, "row"), (dg, "row")],
        outs=[((S, 3 * A_W), BF16, "row"), ((S, 2 * A_HEADS), F32, "row"), ((A_CONV, 3 * A_W), F32, "acc"),
              ((1, A_HEADS), F32, "acc"), ((1, A_HEADS), F32, "acc")])


def grad_x_final(dh1, x, dxpre1, mod):
    S = x.shape[0]

    def body(dh_ref, x_ref, dx_ref, m_ref, gx_ref, dscale_ref, dshift_ref):
        dh = dh_ref[...]
        gx_ref[...] = ALPHA * dx_ref[...] + dh * (1.0 + m_ref[...][SCALE_T:SCALE_T + 1])
        dscale_ref[...] += jnp.sum(dh * x_ref[...], axis=0, keepdims=True)
        dshift_ref[...] += jnp.sum(dh, axis=0, keepdims=True)

    vec = ((1, D_MODEL), F32, "acc")
    return rowcall(body, name="grad_x_final", S=S, ts=512, ins=[(dh1, "row"), (x, "row"), (dxpre1, "row"), (mod, "vec")],
                   outs=[((S, D_MODEL), F32, "row"), vec, vec])


_C_QKV, _C_Z, _C_BA, _C_QKVB, _C_G = 0, 3 * A_W, 4 * A_W, 4 * A_W + 2 * A_HEADS, 4 * A_W + 2 * A_HEADS + 3 * B_W
BA_PAD = 128


def split_w_in(w_in):
    ba = jnp.pad(w_in[:, _C_BA:_C_QKVB], ((0, 0), (0, BA_PAD - 2 * A_HEADS)))
    return dict(qkv=w_in[:, _C_QKV:_C_Z], z=w_in[:, _C_Z:_C_BA], ba=ba, qkvb=w_in[:, _C_QKVB:_C_G], g=w_in[:, _C_G:])


def join_w_in(p):
    return jnp.concatenate([p["qkv"], p["z"], p["ba"][:, :2 * A_HEADS], p["qkvb"], p["g"]], axis=1)


def forward_local(x, target, mod, w, sm):
    h1 = modulate(x, mod, SHIFT_T, SCALE_T, "mod_t")
    qkv_raw = mm(h1, w["qkv"], mode="nn", out_dtype=BF16, name="proj_qkv")
    z = mm(h1, w["z"], mode="nn", out_dtype=F32, name="proj_z")
    ba = mm(h1, w["ba"], mode="nn", out_dtype=F32, name="proj_ba")
    qkvb = mm(h1, w["qkvb"], mode="nn", out_dtype=BF16, name="proj_qkvb")
    gates_raw = mm(h1, w["g"], mode="nn", out_dtype=F32, name="proj_g")
    q, k, v, beta, g = prep_a_fwd(qkv_raw, ba, sm["conv_a"], sm["a_log"], sm["dt_bias"])
    o_pre, sprev, tinv = delta_fwd(q, k, v, beta, g)
    o_a = gate_a_fwd(o_pre, z, sm["norm_a"])
    qkv_pad = jnp.pad(qkvb, ((PAD_ROWS, 0), (0, 0)))
    bias = bias_tiles(jnp.transpose(bias_expand(sm["rel_bias"]), (1, 0, 2)))
    o_b = attn_fwd(qkv_pad, bias)
    ya = mm(o_a, w["branch_a"], mode="nn", out_dtype=F32, name="branch_a")
    yb = mm(o_b, w["branch_b"], mode="nn", out_dtype=F32, name="branch_b")
    merged = merge_fwd(gates_raw, sm["b_gate"], ya, yb)
    mix = mm(merged, w["o"], mode="nn", out_dtype=F32, name="mix")
    xpre1, x1, h2 = ln1_fwd(x, mix, mod, sm["ln1_g"], sm["ln1_b"])
    up = mm(h2, w["up"], mode="nn", out_dtype=BF16, name="ffn_up")
    act = ffn_act_fwd(up, sm["conv_ffn"], sm["b_conv_ffn"])
    ffn = mm(act, w["down"], mode="nn", out_dtype=F32, name="ffn_down")
    dxpre2, dffn, loss, dgate_f, dln2_g, dln2_b = final_fwd_bwd(x1, ffn, target, mod, sm["ln2_g"], sm["ln2_b"])
    saved = dict(h1=h1, qkv_raw=qkv_raw, z=z, ba=ba, gates_raw=gates_raw, q=q, k=k, v=v, beta=beta, g=g,
                 o_pre=o_pre, sprev=sprev, tinv=tinv, o_a=o_a, qkv_pad=qkv_pad, bias=bias, o_b=o_b, ya=ya, yb=yb,
                 merged=merged, mix=mix, xpre1=xpre1, x1=x1, h2=h2, up=up, act=act, ffn=ffn)
    return loss, dxpre2, dffn, dict(gate_f=dgate_f, ln2_g=dln2_g, ln2_b=dln2_b), saved


def backward_local(x, mod, w, sm, dxpre2, dffn, fin, sv):
    dact = mm(dffn, w["down"], mode="nt", out_dtype=BF16, name="d_act")
    gw_down = mm(sv["act"], dffn, mode="tn", out_dtype=BF16, name="gw_down")
    dup, dconv_ffn, db_conv_ffn = ffn_act_bwd(dact, sv["up"], sm["conv_ffn"], sm["b_conv_ffn"])
    dh2 = mm(dup, w["up"], mode="nt", out_dtype=F32, name="d_h2")
    gw_up = mm(sv["h2"], dup, mode="tn", out_dtype=BF16, name="gw_up")
    dxpre1, dmix, dsc_f, dsh_f, dgate_t, dln1_g, dln1_b = ln1_bwd(
        dxpre2, dh2, sv["xpre1"], sv["mix"], mod, sm["ln1_g"], sm["ln1_b"])
    dmerged = mm(dmix, w["o"], mode="nt", out_dtype=F32, name="d_merged")
    gw_o = mm(sv["merged"], dmix, mode="tn", out_dtype=BF16, name="gw_o")
    dya, dyb, dgates, db_gate = merge_bwd(dmerged, sv["gates_raw"], sm["b_gate"], sv["ya"], sv["yb"])
    do_a = mm(dya, w["branch_a"], mode="nt", out_dtype=F32, name="d_oa")
    gw_branch_a = mm(sv["o_a"], dya, mode="tn", out_dtype=BF16, name="gw_branch_a")
    do_b = mm(dyb, w["branch_b"], mode="nt", out_dtype=BF16, name="d_ob")
    gw_branch_b = mm(sv["o_b"], dyb, mode="tn", out_dtype=BF16, name="gw_branch_b")
    dq_b, dk_pad, dv_pad, dbias = attn_bwd(sv["qkv_pad"], sv["bias"], do_b)
    dqkvb = jnp.concatenate([dq_b, dk_pad[PAD_ROWS:].astype(BF16), dv_pad[PAD_ROWS:].astype(BF16)], axis=1)
    drel_bias = bias_reduce(jnp.transpose(dbias, (1, 0, 2)))
    do_pre, dz, dnorm_a = gate_a_bwd(do_a, sv["o_pre"], sv["z"], sm["norm_a"])
    dq, dk, dv, dbeta, dg = delta_bwd(sv["q"], sv["k"], sv["v"], sv["beta"], sv["g"], sv["sprev"], sv["tinv"], do_pre)
    dqkv_raw, dba16, dconv_a, da_log, ddt_bias = prep_a_bwd(
        sv["qkv_raw"], sv["ba"], sm["conv_a"], sm["a_log"], sm["dt_bias"], dq, dk, dv, dbeta, dg)
    dba = jnp.pad(dba16, ((0, 0), (0, BA_PAD - 2 * A_HEADS))).astype(BF16)
    pieces = dict(qkv=dqkv_raw, z=dz, ba=dba, qkvb=dqkvb, g=dgates)
    dh1 = None
    gw_in = {}
    for key, dpiece in pieces.items():
        dh1 = mm(dpiece, w[key], mode="nt", out_dtype=F32, name="d_h1_" + key, acc_in=dh1)
        gw_in[key] = mm(sv["h1"], dpiece, mode="tn", out_dtype=BF16, name="gw_in_" + key)
    grad_x, dsc_t, dsh_t = grad_x_final(dh1, x, dxpre1, mod)
    dmod = jnp.concatenate([dsh_t, dsc_t, dgate_t, dsh_f, dsc_f, fin["gate_f"]], axis=0)
    gw = dict(w_in=join_w_in(gw_in), w_branch_a=gw_branch_a, w_branch_b=gw_branch_b, w_o=gw_o, w_up=gw_up, w_down=gw_down)
    gs = dict(b_gate=db_gate, conv_a=dconv_a, a_log=da_log, dt_bias=ddt_bias, norm_a=dnorm_a, rel_bias=drel_bias,
              ln1_g=dln1_g, ln1_b=dln1_b, conv_ffn=dconv_ffn, b_conv_ffn=db_conv_ffn, ln2_g=fin["ln2_g"], ln2_b=fin["ln2_b"])
    return grad_x, dmod, gw, gs


MESH = pl.DeviceIdType.MESH
ANY = pl.BlockSpec(memory_space=pl.ANY)
WHOLE_VMEM = pl.BlockSpec(memory_space=pltpu.VMEM)


def _place():
    return lax.axis_index("x"), lax.axis_index("y"), lax.axis_index("c")


def allgather8(blk, name):
    m_per, n = blk.shape

    def body(x_ref, out_ref, send_sems, recv_sems, local_sem):
        x, y, c = _place()
        me, sibling = (x, y, c), (x, y, 1 - c)
        chips = [(1 - x, y), (x, 1 - y), (1 - x, 1 - y)]

        def rows(px, py, pc):
            return out_ref.at[pl.ds((4 * px + 2 * py + pc) * m_per, m_per), :]

        def copy(k, block, to, src=None):
            return pltpu.make_async_remote_copy(
                src_ref=rows(*block) if src is None else src, dst_ref=rows(*block),
                send_sem=send_sems.at[k], recv_sem=recv_sems.at[k], device_id=to, device_id_type=MESH)

        mine = pltpu.make_async_copy(x_ref, rows(*me), local_sem)
        mine.start()
        first = [copy(0, me, sibling, src=x_ref)]
        first += [copy(1 + j, me, (*chip, c), src=x_ref) for j, chip in enumerate(chips)]
        for cp in first:
            cp.start()
        passed = [copy(4 + j, (*chip, c), sibling) for j, chip in enumerate(chips)]
        for j, chip in enumerate(chips):
            copy(1 + j, (*chip, c), me).wait_recv()
            passed[j].start()
        copy(0, sibling, me).wait_recv()
        for j, chip in enumerate(chips):
            copy(4 + j, (*chip, 1 - c), me).wait_recv()
        for cp in first + passed:
            cp.wait_send()
        mine.wait()

    return pl.pallas_call(
        body, name=name, out_shape=jax.ShapeDtypeStruct((N_DEV * m_per, n), blk.dtype),
        in_specs=[WHOLE_VMEM], out_specs=WHOLE_VMEM,
        scratch_shapes=[pltpu.SemaphoreType.DMA((7,)), pltpu.SemaphoreType.DMA((7,)), pltpu.SemaphoreType.DMA],
    )(blk)


def _chip_peers(x, y):
    return [(1 - x, y), (x, 1 - y), (1 - x, 1 - y)]


def chip_exchange(arrs, name, scatter):
    n = len(arrs)

    def body(*refs):
        ins, outs = refs[:n], refs[n:2 * n]
        send_sems, recv_sems, local_sems = refs[2 * n:]
        x, y, c = _place()
        me = 2 * x + y
        sibling = (x, y, 1 - c)
        peers = _chip_peers(x, y)

        def half(ref, which):
            r2 = ref.shape[0] // 2
            return ref.at[pl.ds(which * r2, r2), :]

        def outgoing(a, chip):
            return ins[a].at[chip] if scatter else ins[a]

        def copy(k, src, dst, to):
            return pltpu.make_async_remote_copy(src_ref=src, dst_ref=dst, send_sem=send_sems.at[k],
                                                recv_sem=recv_sems.at[k], device_id=to, device_id_type=MESH)

        started, local = [], []
        for a in range(n):
            lc = pltpu.make_async_copy(outgoing(a, me), outs[a].at[me], local_sems.at[a])
            lc.start()
            local.append(lc)
            for j, (px, py) in enumerate(peers):
                cp = copy(6 * a + j, half(outgoing(a, 2 * px + py), c), half(outs[a].at[me], c), (px, py, c))
                cp.start()
                started.append(cp)
        for a in range(n):
            for j, (px, py) in enumerate(peers):
                landed = half(outs[a].at[2 * px + py], c)
                copy(6 * a + j, landed, landed, (px, py, c)).wait_recv()
                relay = copy(6 * a + 3 + j, landed, landed, sibling)
                relay.start()
                started.append(relay)
        for a in range(n):
            for j, (px, py) in enumerate(peers):
                other = half(outs[a].at[2 * px + py], 1 - c)
                copy(6 * a + 3 + j, other, other, sibling).wait_recv()
        for cp in started:
            cp.wait_send()
        for lc in local:
            lc.wait()

    out_shape = [jax.ShapeDtypeStruct(a.shape if scatter else (N_CHIPS,) + a.shape, a.dtype) for a in arrs]
    return pl.pallas_call(
        body, name=name, out_shape=out_shape, in_specs=[ANY] * n, out_specs=[ANY] * n,
        scratch_shapes=[pltpu.SemaphoreType.DMA((6 * n,)), pltpu.SemaphoreType.DMA((6 * n,)), pltpu.SemaphoreType.DMA((n,))],
    )(*arrs)


def sibling_exchange(arrs, name):
    n = len(arrs)

    def body(*refs):
        ins, outs = refs[:n], refs[n:2 * n]
        send_sems, recv_sems = refs[2 * n:]
        x, y, c = _place()
        cps = [pltpu.make_async_remote_copy(src_ref=ins[a], dst_ref=outs[a], send_sem=send_sems.at[a],
                                            recv_sem=recv_sems.at[a], device_id=(x, y, 1 - c), device_id_type=MESH)
               for a in range(n)]
        for cp in cps:
            cp.start()
        for cp in cps:
            cp.wait()

    return pl.pallas_call(
        body, name=name, out_shape=[jax.ShapeDtypeStruct(a.shape, a.dtype) for a in arrs],
        in_specs=[ANY] * n, out_specs=[ANY] * n,
        scratch_shapes=[pltpu.SemaphoreType.DMA((n,)), pltpu.SemaphoreType.DMA((n,))],
    )(*arrs)


TILE_BYTES = 2 * 1024 * 1024


def _row_tile(rows, row_bytes):
    if rows * row_bytes <= TILE_BYTES or rows % 8:
        return rows
    best = 8
    for t in range(8, rows + 1, 8):
        if rows % t == 0 and t * row_bytes <= TILE_BYTES:
            best = t
    return best


def pair_add(a, b, name):
    R, C = a.shape
    tr = _row_tile(R, C * 4)

    def body(a_ref, b_ref, o_ref):
        o_ref[...] = (a_ref[...].astype(F32) + b_ref[...].astype(F32)).astype(BF16)

    spec = pl.BlockSpec((tr, C), lambda i: (i, 0))
    return pl.pallas_call(body, name=name, grid=(R // tr,), in_specs=[spec, spec], out_specs=spec,
                          out_shape=jax.ShapeDtypeStruct((R, C), BF16), compiler_params=_cparams(("parallel",)))(a, b)


def sum_lead(parts, name):
    K, R, C = parts.shape
    tr = _row_tile(R, C * 4)

    def body(p_ref, o_ref):
        acc = p_ref[0].astype(F32)
        for j in range(1, K):
            acc = acc + p_ref[j].astype(F32)
        o_ref[...] = acc

    return pl.pallas_call(
        body, name=name, grid=(R // tr,), in_specs=[pl.BlockSpec((K, tr, C), lambda i: (0, i, 0))],
        out_specs=pl.BlockSpec((tr, C), lambda i: (i, 0)), out_shape=jax.ShapeDtypeStruct((R, C), F32),
        compiler_params=_cparams(("parallel",)))(parts)


def adamw(w, g, m, v, name):
    R, C = w.shape
    tr = _row_tile(R, C * 4)

    def body(w_ref, g_ref, m_ref, v_ref, d_ref, mo_ref, vo_ref):
        gv = g_ref[...]
        m2 = ADAM_B1 * m_ref[...] + (1.0 - ADAM_B1) * gv
        v2 = ADAM_B2 * v_ref[...] + (1.0 - ADAM_B2) * (gv * gv)
        m_hat = m2 / (1.0 - ADAM_B1 ** ADAM_STEP)
        v_hat = v2 / (1.0 - ADAM_B2 ** ADAM_STEP)
        d_ref[...] = -ADAM_LR * (m_hat / (jnp.sqrt(v_hat) + ADAM_EPS) + ADAM_WD * w_ref[...])
        mo_ref[...] = m2
        vo_ref[...] = v2

    spec = pl.BlockSpec((tr, C), lambda i: (i, 0))
    return pl.pallas_call(body, name=name, grid=(R // tr,), in_specs=[spec] * 4, out_specs=[spec] * 3,
                          out_shape=[jax.ShapeDtypeStruct((R, C), F32)] * 3, compiler_params=_cparams(("parallel",)))(w, g, m, v)


LANES = 1024


def _pack(arrs, rows):
    out, offs, r = [], [], 0
    for a in arrs:
        flat = a.reshape(-1)
        nr = -(-flat.shape[0] // LANES)
        out.append(jnp.pad(flat, (0, nr * LANES - flat.shape[0])))
        offs.append(r)
        r += nr
    assert r <= rows, (r, rows)
    out.append(jnp.zeros(((rows - r) * LANES,), F32))
    return jnp.concatenate(out).reshape(rows, LANES), offs


def _unpack(packed, offs, shapes):
    flat = packed.reshape(-1)
    return [flat[o * LANES:o * LANES + math.prod(s)].reshape(s) for o, s in zip(offs, shapes)]


WEIGHTS = ["w_ada", "b_ada", "w_in", "b_gate", "conv_a", "a_log", "dt_bias", "norm_a", "rel_bias", "w_branch_a",
           "w_branch_b", "w_o", "ln1_g", "ln1_b", "w_up", "conv_ffn", "b_conv_ffn", "w_down", "ln2_g", "ln2_b"]
BIG = ["w_in", "w_branch_a", "w_branch_b", "w_o", "w_up", "w_down"]
COL_SHARDED = {"w_in", "w_up"}
SMALL_SHARDED = {"conv_a": 3 * A_W // N_CHIPS, "rel_bias": B_REL // N_CHIPS, "conv_ffn": 2 * D_FF // N_CHIPS}
SMALL = [n for n in WEIGHTS if n not in BIG and n != "w_ada"]


def _to_full(g4, name):
    if name in COL_SHARDED:
        return jnp.transpose(g4, (1, 0, 2)).reshape(g4.shape[1], -1)
    return g4.reshape(-1, g4.shape[2])


def _to_shards(full, name):
    if name in COL_SHARDED:
        return jnp.transpose(full.reshape(full.shape[0], N_CHIPS, -1), (1, 0, 2))
    return full.reshape(N_CHIPS, -1, full.shape[1])


def kernel(x, c, w_ada, b_ada, w_in, b_gate, conv_a, a_log, dt_bias, norm_a, rel_bias, w_branch_a, w_branch_b, w_o, ln1_g, ln1_b, w_up, conv_ffn, b_conv_ffn, w_down, ln2_g, ln2_b, loss_target, m_w_ada, m_b_ada, m_w_in, m_b_gate, m_conv_a, m_a_log, m_dt_bias, m_norm_a, m_rel_bias, m_w_branch_a, m_w_branch_b, m_w_o, m_ln1_g, m_ln1_b, m_w_up, m_conv_ffn, m_b_conv_ffn, m_w_down, m_ln2_g, m_ln2_b, v_w_ada, v_b_ada, v_w_in, v_b_gate, v_conv_a, v_a_log, v_dt_bias, v_norm_a, v_rel_bias, v_w_branch_a, v_w_branch_b, v_w_o, v_ln1_g, v_ln1_b, v_w_up, v_conv_ffn, v_b_conv_ffn, v_w_down, v_ln2_g, v_ln2_b):
    args = dict(locals())
    wts = {n: args[n] for n in WEIGHTS}
    moms = {n: args["m_" + n] for n in WEIGHTS}
    vars_ = {n: args["v_" + n] for n in WEIGHTS}
    xi, yi, ci = _place()
    chip = 2 * xi + yi
    dev = 4 * xi + 2 * yi + ci
    ada_cols = w_ada.shape[2]

    c_all = allgather8(jnp.pad(c, ((0, 7), (0, 0))), "gather_c").reshape(N_DEV, 8, D_MODEL)[:, 0]
    b_ada_sh = lax.dynamic_slice(b_ada, (0, chip * ada_cols), (1, ada_cols))
    mod_sh = ada_fwd(c_all, w_ada[0], b_ada_sh)
    mod_g = allgather8(mod_sh, "gather_mod").reshape(N_CHIPS, 2, N_DEV, ada_cols)[:, 0]
    mod = lax.dynamic_slice(mod_g, (0, dev, 0), (N_CHIPS, 1, ada_cols)).reshape(6, D_MODEL)

    big_full = chip_exchange([wts[n][0].astype(BF16) for n in BIG], "gather_weights", scatter=False)
    w = {n: _to_full(g4, n) for n, g4 in zip(BIG, big_full)}
    wd = dict(split_w_in(w["w_in"]), branch_a=w["w_branch_a"], branch_b=w["w_branch_b"], o=w["w_o"], up=w["w_up"], down=w["w_down"])
    sshapes = [wts[n].shape[1:] for n in SMALL_SHARDED]
    spack, soffs = _pack([wts[n][0] for n in SMALL_SHARDED], 16)
    sg = allgather8(spack, "gather_small_w").reshape(N_CHIPS, 2, 16, LANES)[:, 0]
    sparts = [_unpack(sg[j], soffs, sshapes) for j in range(N_CHIPS)]
    sm = {n: wts[n] for n in SMALL if n not in SMALL_SHARDED and n != "b_ada"}
    for i, n in enumerate(SMALL_SHARDED):
        sm[n] = jnp.concatenate([sparts[j][i] for j in range(N_CHIPS)], axis=-1)

    loss, dxpre2, dffn, fin, sv = forward_local(x[0], loss_target[0], mod, wd, sm)
    grad_x, dmod, gw, gs = backward_local(x[0], mod, wd, sm, dxpre2, dffn, fin, sv)

    gnames = [n for n in SMALL if n != "b_ada"]
    vec, voffs = _pack([dmod] + [gs[n] for n in gnames] + [loss], 56)
    gathered = allgather8(vec, "gather_small_g").reshape(N_DEV, 56, LANES)
    summed = sum_lead(gathered, "sum_small_g")
    full_shapes = [(6, D_MODEL)] + [gs[n].shape for n in gnames] + [(1, 1)]
    parts = _unpack(summed, voffs, full_shapes)
    grads = {"b_ada": parts[0].reshape(1, -1)}
    for n, p in zip(gnames, parts[1:-1]):
        if n in SMALL_SHARDED:
            p = lax.dynamic_slice_in_dim(p, chip * SMALL_SHARDED[n], SMALL_SHARDED[n], axis=1)
        grads[n] = p.reshape(wts[n].shape)
    loss_total = parts[-1].reshape(())
    dmod_all = gathered[:, 0:6, :].reshape(N_DEV, 6 * D_MODEL)
    grads["w_ada"] = ada_bwd(c_all, lax.dynamic_slice(dmod_all, (0, chip * ada_cols), (N_DEV, ada_cols)))[None]

    mine = [gw[n] for n in BIG]
    theirs = sibling_exchange(mine, "grad_sibling")
    chip_sums = [_to_shards(pair_add(a, b, "grad_pair_" + n), n) for n, a, b in zip(BIG, mine, theirs)]
    received = chip_exchange(chip_sums, "grad_scatter", scatter=True)
    for n, r in zip(BIG, received):
        grads[n] = sum_lead(r, "grad_sum_" + n)[None]

    delta, new_m, new_v = {}, {}, {}
    for n in ["w_ada"] + BIG:
        d, m2, v2 = adamw(wts[n][0], grads[n][0], moms[n][0], vars_[n][0], "adamw_" + n)
        delta[n], new_m[n], new_v[n] = d[None], m2[None], v2[None]
    shapes = [wts[n].shape for n in SMALL]
    packs = [_pack([t[n] for n in SMALL], 32) for t in (wts, grads, moms, vars_)]
    outs = adamw(*[p[0] for p in packs], "adamw_small")
    for res, o in zip((delta, new_m, new_v), outs):
        for n, a in zip(SMALL, _unpack(o, packs[0][1], shapes)):
            res[n] = a
    return (loss_total, grad_x[None], *[grads[n] for n in WEIGHTS], *[delta[n] for n in WEIGHTS],
            *[new_m[n] for n in WEIGHTS], *[new_v[n] for n in WEIGHTS])
```

```python
import functools
import math

import jax
import jax.numpy as jnp
from jax import lax
from jax.experimental import pallas as pl
from jax.experimental.pallas import tpu as pltpu

F32 = jnp.float32
BF16 = jnp.bfloat16

D_MODEL = 1024
CHUNK = 64
A_HEADS = 8
A_DK = 128
A_W = A_HEADS * A_DK
A_CONV = 4
B_HEADS = 16
B_DH = 64
B_W = B_HEADS * B_DH
B_PREV = 8
B_BAND = (B_PREV + 1) * CHUNK
B_MAX_REL = 256
B_REL = CHUNK - 1 + B_MAX_REL + 1
D_FF = 2816
FFN_CONV = 3
IN_COLS = 4 * A_W + 2 * A_HEADS + 3 * B_W + 2 * D_MODEL
ALPHA = 2.0 ** 0.25
LN_EPS = 1e-5
RMS_EPS = 1e-6
L2_EPS = 1e-6
NEG_INF = -1e30
ADAM_LR, ADAM_B1, ADAM_B2, ADAM_EPS, ADAM_WD, ADAM_STEP = 0.001, 0.9, 0.999, 1e-08, 0.01, 10
N_CHIPS = 4
N_DEV = 8
VMEM_LIMIT = 56 * 1024 * 1024


def _cparams(sem=None):
    return pltpu.CompilerParams(dimension_semantics=sem, vmem_limit_bytes=VMEM_LIMIT)


_DIMS = {"nn": (((1,), (0,)), ((), ())), "nt": (((1,), (1,)), ((), ())), "tn": (((0,), (0,)), ((), ()))}


MM_TILE_CAP = 1408


def _mm_tile(n):
    return max(t for t in range(128, min(n, MM_TILE_CAP) + 1, 128) if n % t == 0)


def mm(a, b, *, mode, out_dtype, name, acc_in=None):
    if mode == "nn":
        (M, K), (K2, N) = a.shape, b.shape
    elif mode == "nt":
        (M, K), (N, K2) = a.shape, b.shape
    else:
        (K, M), (K2, N) = a.shape, b.shape
    assert K == K2, (a.shape, b.shape, mode)
    tm, tn, tk = _mm_tile(M), _mm_tile(N), _mm_tile(K)
    nk = K // tk

    def body(*refs):
        if acc_in is None:
            a_ref, b_ref, o_ref, acc_ref = refs
        else:
            a_ref, b_ref, c_ref, o_ref, acc_ref = refs
        k = pl.program_id(2)

        @pl.when(k == 0)
        def _():
            if acc_in is None:
                acc_ref[...] = jnp.zeros_like(acc_ref)
            else:
                acc_ref[...] = c_ref[...]

        acc_ref[...] += lax.dot_general(a_ref[...].astype(BF16), b_ref[...].astype(BF16), _DIMS[mode],
                                        preferred_element_type=F32)

        @pl.when(k == nk - 1)
        def _():
            o_ref[...] = acc_ref[...].astype(out_dtype)

    a_spec = pl.BlockSpec((tk, tm), lambda i, j, k: (k, i)) if mode == "tn" else pl.BlockSpec((tm, tk), lambda i, j, k: (i, k))
    b_spec = pl.BlockSpec((tn, tk), lambda i, j, k: (j, k)) if mode == "nt" else pl.BlockSpec((tk, tn), lambda i, j, k: (k, j))
    o_spec = pl.BlockSpec((tm, tn), lambda i, j, k: (i, j))
    ins, in_specs, aliases = [a, b], [a_spec, b_spec], {}
    if acc_in is not None:
        assert acc_in.shape == (M, N) and acc_in.dtype == F32 and out_dtype == F32
        ins.append(acc_in)
        in_specs.append(o_spec)
        aliases = {2: 0}
    return pl.pallas_call(
        body, name=name, grid=(M // tm, N // tn, nk), in_specs=in_specs, out_specs=o_spec,
        out_shape=jax.ShapeDtypeStruct((M, N), out_dtype), scratch_shapes=[pltpu.VMEM((tm, tn), F32)],
        input_output_aliases=aliases, compiler_params=_cparams(("parallel", "parallel", "arbitrary")),
    )(*ins)


def rowcall(body, *, name, S, ts, ins, outs):
    assert S % ts == 0 and ts % 16 == 0
    nsteps = S // ts
    in_specs, arrays = [], []
    for arr, kind in ins:
        arrays.append(arr)
        if kind == "row":
            in_specs.append(pl.BlockSpec((ts, arr.shape[1]), lambda i: (i, 0)))
        elif kind in ("prev", "next"):
            hr = 8 * (4 // arr.dtype.itemsize)
            per, last = ts // hr, S // hr - 1
            if kind == "prev":
                in_specs.append(pl.BlockSpec((hr, arr.shape[1]), lambda i, per=per: (jnp.maximum(i * per - 1, 0), 0)))
            else:
                in_specs.append(pl.BlockSpec((hr, arr.shape[1]), lambda i, per=per, last=last: (jnp.minimum((i + 1) * per, last), 0)))
        else:
            nd = arr.ndim
            in_specs.append(pl.BlockSpec(arr.shape, lambda i, nd=nd: (0,) * nd))
    out_specs, out_shapes, acc_idx = [], [], []
    for n, (shape, dtype, kind) in enumerate(outs):
        out_shapes.append(jax.ShapeDtypeStruct(shape, dtype))
        if kind == "row":
            out_specs.append(pl.BlockSpec((ts, shape[1]), lambda i: (i, 0)))
        else:
            nd = len(shape)
            out_specs.append(pl.BlockSpec(shape, lambda i, nd=nd: (0,) * nd))
            acc_idx.append(n)
    n_in = len(arrays)

    def wrapped(*refs):
        @pl.when(pl.program_id(0) == 0)
        def _():
            for n in acc_idx:
                refs[n_in + n][...] = jnp.zeros_like(refs[n_in + n])

        body(*refs)

    res = pl.pallas_call(
        wrapped, name=name, grid=(nsteps,), in_specs=in_specs, out_specs=out_specs, out_shape=out_shapes,
        compiler_params=_cparams(("arbitrary",) if acc_idx else ("parallel",)),
    )(*arrays)
    return res


def _halo_prev(ref):
    v = ref[...].astype(F32)
    return v[v.shape[0] - 8:]


def _halo_next(ref):
    return ref[...].astype(F32)[:8]


def _shift_down(cur, prev8, k):
    if k == 0:
        return cur
    rolled = pltpu.roll(cur, k, axis=0)
    fix = pltpu.roll(prev8, k, axis=0)
    row = lax.broadcasted_iota(jnp.int32, (8, 1), 0)
    top = jnp.where(row < k, fix, rolled[0:8])
    if cur.shape[0] == 8:
        return top
    return jnp.concatenate([top, rolled[8:]], axis=0)


def _shift_up(cur, next8, k):
    if k == 0:
        return cur
    n = cur.shape[0]
    rolled = pltpu.roll(cur, n - k, axis=0)
    fix = pltpu.roll(next8, 8 - k, axis=0)
    row = lax.broadcasted_iota(jnp.int32, (8, 1), 0)
    bot = jnp.where(row >= 8 - k, fix, rolled[n - 8:n])
    return jnp.concatenate([rolled[:n - 8], bot], axis=0)


def _sigmoid(x):
    return 1.0 / (1.0 + jnp.exp(-x))


def _silu(x):
    return x * _sigmoid(x)


def _silu_and_grad(x):
    s = _sigmoid(x)
    return x * s, s * (1.0 + x * (1.0 - s))


def _softplus(x):
    return jnp.maximum(x, 0.0) + jnp.log1p(jnp.exp(-jnp.abs(x)))


def _split2(x):
    hi = x.astype(BF16)
    return hi, (x - hi.astype(F32)).astype(BF16)


def _dot1(a, b, mode):
    return lax.dot_general(a.astype(BF16), b.astype(BF16), _DIMS[mode], preferred_element_type=F32)


def _dot3(a, b, mode):
    ah, al = _split2(a)
    bh, bl = _split2(b)
    d = lambda p, q: lax.dot_general(p, q, _DIMS[mode], preferred_element_type=F32)
    return d(ah, bh) + (d(ah, bl) + d(al, bh))


def ada_fwd(c_all, w_sh, b_sh):
    n = w_sh.shape[1]
    tn = 512

    def body(c_ref, w_ref, b_ref, o_ref):
        o_ref[...] = _dot1(_silu(c_ref[...]), w_ref[...], "nn") + b_ref[...]

    return pl.pallas_call(
        body, name="ada_fwd", grid=(n // tn,),
        in_specs=[pl.BlockSpec((N_DEV, D_MODEL), lambda j: (0, 0)), pl.BlockSpec((D_MODEL, tn), lambda j: (0, j)),
                  pl.BlockSpec((1, tn), lambda j: (0, j))],
        out_specs=pl.BlockSpec((N_DEV, tn), lambda j: (0, j)), out_shape=jax.ShapeDtypeStruct((N_DEV, n), F32),
        compiler_params=_cparams(("parallel",)),
    )(c_all, w_sh, b_sh)


def ada_bwd(c_all, dmod_sh):
    n = dmod_sh.shape[1]
    tn = 512

    def body(c_ref, d_ref, o_ref):
        o_ref[...] = _dot1(_silu(c_ref[...]), d_ref[...], "tn")

    return pl.pallas_call(
        body, name="ada_bwd", grid=(n // tn,),
        in_specs=[pl.BlockSpec((N_DEV, D_MODEL), lambda j: (0, 0)), pl.BlockSpec((N_DEV, tn), lambda j: (0, j))],
        out_specs=pl.BlockSpec((D_MODEL, tn), lambda j: (0, j)), out_shape=jax.ShapeDtypeStruct((D_MODEL, n), F32),
        compiler_params=_cparams(("parallel",)),
    )(c_all, dmod_sh)


SHIFT_T, SCALE_T, GATE_T, SHIFT_F, SCALE_F, GATE_F = range(6)


def modulate(x, mod, shift_row, scale_row, name):
    S = x.shape[0]

    def body(x_ref, m_ref, o_ref):
        m = m_ref[...]
        o_ref[...] = (x_ref[...] * (1.0 + m[scale_row:scale_row + 1]) + m[shift_row:shift_row + 1]).astype(BF16)

    return rowcall(body, name=name, S=S, ts=512, ins=[(x, "row"), (mod, "vec")], outs=[((S, D_MODEL), BF16, "row")])[0]


def _conv_fwd(cur, prev, w, width):
    y = cur * w[width - 1:width]
    for j in range(width - 1):
        y = y + _shift_down(cur, prev, width - 1 - j) * w[j:j + 1]
    return y


def _prep_a_core(cur, prev, w):
    return _silu_and_grad(_conv_fwd(cur, prev, w, A_CONV))


def prep_a_fwd(qkv_raw, ba, conv_a, a_log, dt_bias):
    S = qkv_raw.shape[0]

    def body(x_ref, xp_ref, ba_ref, w_ref, al_ref, dt_ref, q_ref, k_ref, v_ref, beta_ref, g_ref):
        first = (pl.program_id(0) > 0).astype(F32)
        y, _ = _prep_a_core(x_ref[...].astype(F32), _halo_prev(xp_ref) * first, w_ref[...])
        for h in range(A_HEADS):
            sl = slice(h * A_DK, (h + 1) * A_DK)
            qh = y[:, sl]
            kh = y[:, A_W + h * A_DK:A_W + (h + 1) * A_DK]
            q_ref[:, sl] = qh * (lax.rsqrt(jnp.sum(qh * qh, axis=-1, keepdims=True) + L2_EPS) * (A_DK ** -0.5))
            k_ref[:, sl] = kh * lax.rsqrt(jnp.sum(kh * kh, axis=-1, keepdims=True) + L2_EPS)
        v_ref[...] = y[:, 2 * A_W:3 * A_W]
        bav = ba_ref[...]
        beta_ref[...] = _sigmoid(bav[:, 0:A_HEADS])
        g_ref[...] = -jnp.exp(al_ref[...]) * _softplus(bav[:, A_HEADS:2 * A_HEADS] + dt_ref[...])

    return rowcall(
        body, name="prep_a_fwd", S=S, ts=256,
        ins=[(qkv_raw, "row"), (qkv_raw, "prev"), (ba, "row"), (conv_a, "vec"), (a_log, "vec"), (dt_bias, "vec")],
        outs=[((S, A_W), F32, "row")] * 3 + [((S, A_HEADS), F32, "row")] * 2)


HEAD_GROUP = 4
GROUP_ROWS = HEAD_GROUP * CHUNK
N_HEAD_GROUPS = A_HEADS // HEAD_GROUP
LOG_CHUNK = int(math.log2(CHUNK))


def _tri_masks():
    rb = lax.broadcasted_iota(jnp.int32, (GROUP_ROWS, GROUP_ROWS), 0)
    cb = lax.broadcasted_iota(jnp.int32, (GROUP_ROWS, GROUP_ROWS), 1)
    same = (rb >> LOG_CHUNK) == (cb >> LOG_CHUNK)
    return dict(causal=same & (rb >= cb), strict=same & (rb > cb), eye=rb == cb, upper=same & (cb >= rb),
                last=cb == (rb | (CHUNK - 1)), rb=rb, cb=cb)


def _col_to_row(colv, eye):
    return jnp.sum(jnp.where(eye, colv, 0.0), axis=0, keepdims=True)


def _row_to_col(rowv, eye):
    return jnp.sum(jnp.where(eye, rowv, 0.0), axis=1, keepdims=True)


def _tri_inv(a_list, mk):
    rb, cb = mk["rb"], mk["cb"]
    ts = [jnp.where(mk["eye"], 1.0, 0.0) - jnp.where((rb >> 1) == (cb >> 1), a, 0.0) for a in a_list]
    for lvl in range(1, LOG_CHUNK):
        rs, cs = rb >> lvl, cb >> lvl
        sel = ((rs & 1) == 1) & (cs == rs - 1)
        inner = [_dot3(t, jnp.where(sel, a, 0.0), "nn") for t, a in zip(ts, a_list)]
        ts = [t - _dot3(i, t, "nn") for i, t in zip(inner, ts)]
    return ts


def _stack_heads(ref, grp):
    return jnp.concatenate([ref[:, (grp * HEAD_GROUP + j) * A_DK:(grp * HEAD_GROUP + j + 1) * A_DK]
                            for j in range(HEAD_GROUP)], axis=0)


def _stack_cols(tile, grp):
    return jnp.concatenate([tile[:, grp * HEAD_GROUP + j:grp * HEAD_GROUP + j + 1] for j in range(HEAD_GROUP)], axis=0)


def _delta_local(q, k, v, beta, g, mk):
    causal, strict, eye = mk["causal"], mk["strict"], mk["eye"]
    g_row = _col_to_row(g, eye)
    gc = jnp.sum(jnp.where(causal, g_row, 0.0), axis=1, keepdims=True)
    gc_row = _col_to_row(gc, eye)
    decay = jnp.where(causal, jnp.exp(jnp.where(causal, gc - gc_row, 0.0)), 0.0)
    gam = jnp.exp(gc)
    kb = k * beta
    vb = v * beta
    y = kb * gam
    a = jnp.where(strict, _dot1(kb, k, "nt") * decay, 0.0)
    p = _dot1(q, k, "nt") * decay
    gl = jnp.sum(jnp.where(mk["last"], gc_row, 0.0), axis=1, keepdims=True)
    kd = k * jnp.exp(gl - gc)
    return dict(gc=gc, decay=decay, gam=gam, kb=kb, vb=vb, y=y, a=a, p=p, gl=gl, kd=kd)


def _head_rows(x, j):
    return x[j * CHUNK:(j + 1) * CHUNK]


def delta_fwd(q, k, v, beta, g):
    S = q.shape[0]
    n_chunks = S // CHUNK

    def body(q_ref, k_ref, v_ref, beta_ref, g_ref, o_ref, sprev_ref, t_ref, state_ref):
        @pl.when(pl.program_id(0) == 0)
        def _():
            state_ref[...] = jnp.zeros_like(state_ref)

        mk = _tri_masks()
        betav, gv = beta_ref[...], g_ref[...]
        groups = range(N_HEAD_GROUPS)
        q_all = [_stack_heads(q_ref, grp) for grp in groups]
        locs = [_delta_local(q_all[grp], _stack_heads(k_ref, grp), _stack_heads(v_ref, grp),
                             _stack_cols(betav, grp), _stack_cols(gv, grp), mk) for grp in groups]
        tinvs = _tri_inv([loc["a"] for loc in locs], mk)
        uws = [_dot3(tinvs[grp], jnp.concatenate([locs[grp]["vb"], locs[grp]["y"]], axis=1), "nn") for grp in groups]
        for grp in groups:
            loc, uw = locs[grp], uws[grp]
            t_ref[0, grp] = tinvs[grp]
            qg = q_all[grp] * loc["gam"]
            egl = jnp.exp(loc["gl"])
            vns, o_state = [], []
            for j in range(HEAD_GROUP):
                h = grp * HEAD_GROUP + j
                s0 = state_ref[h]
                sprev_ref[0, h] = s0
                uw_h = _head_rows(uw, j)
                vn = uw_h[:, :A_DK] - _dot1(uw_h[:, A_DK:], s0, "nn")
                vns.append(vn)
                o_state.append(_dot1(_head_rows(qg, j), s0, "nn"))
                state_ref[h] = s0 * egl[(j + 1) * CHUNK - 1:(j + 1) * CHUNK] + _dot1(_head_rows(loc["kd"], j), vn, "tn")
            o_local = _dot1(loc["p"], jnp.concatenate(vns, axis=0), "nn")
            for j in range(HEAD_GROUP):
                h = grp * HEAD_GROUP + j
                o_ref[:, h * A_DK:(h + 1) * A_DK] = o_state[j] + _head_rows(o_local, j)

    tile = pl.BlockSpec((CHUNK, A_W), lambda n: (n, 0))
    small = pl.BlockSpec((CHUNK, A_HEADS), lambda n: (n, 0))
    return pl.pallas_call(
        body, name="delta_fwd", grid=(n_chunks,), in_specs=[tile, tile, tile, small, small],
        out_specs=[tile, pl.BlockSpec((1, A_HEADS, A_DK, A_DK), lambda n: (n, 0, 0, 0)),
                   pl.BlockSpec((1, N_HEAD_GROUPS, GROUP_ROWS, GROUP_ROWS), lambda n: (n, 0, 0, 0))],
        out_shape=[jax.ShapeDtypeStruct((S, A_W), F32), jax.ShapeDtypeStruct((n_chunks, A_HEADS, A_DK, A_DK), F32),
                   jax.ShapeDtypeStruct((n_chunks, N_HEAD_GROUPS, GROUP_ROWS, GROUP_ROWS), F32)],
        scratch_shapes=[pltpu.VMEM((A_HEADS, A_DK, A_DK), F32)],
        compiler_params=_cparams(("arbitrary",)),
    )(q, k, v, beta, g)


def gate_a_fwd(o_pre, z, norm_w):
    S = o_pre.shape[0]

    def body(o_ref, z_ref, nw_ref, out_ref):
        nw = nw_ref[...]
        for h in range(A_HEADS):
            sl = slice(h * A_DK, (h + 1) * A_DK)
            oh = o_ref[:, sl]
            r = lax.rsqrt(jnp.mean(oh * oh, axis=-1, keepdims=True) + RMS_EPS)
            out_ref[:, sl] = (oh * r * nw * _silu(z_ref[:, sl])).astype(BF16)

    return rowcall(body, name="gate_a_fwd", S=S, ts=512, ins=[(o_pre, "row"), (z, "row"), (norm_w, "vec")],
                   outs=[((S, A_W), BF16, "row")])[0]


HEADS_PER_GROUP = 2
GROUP_W = HEADS_PER_GROUP * B_DH
N_GROUPS = B_HEADS // HEADS_PER_GROUP
PAD_ROWS = B_PREV * CHUNK


Q_TILE = 256
Q_CHUNKS = Q_TILE // CHUNK
KEY_WIN = (B_PREV + Q_CHUNKS) * CHUNK


def _band_probs(qh, kh, bias, valid):
    s = _dot1(qh, kh, "nt") * (B_DH ** -0.5) + bias
    s = jnp.where(valid, s, NEG_INF)
    e = jnp.exp(s - jnp.max(s, axis=-1, keepdims=True))
    return e * (1.0 / jnp.sum(e, axis=-1, keepdims=True))


def bias_tiles(bias):
    rows = [jnp.pad(bias, ((0, 0), (0, 0), (qc * CHUNK, (Q_CHUNKS - 1 - qc) * CHUNK)), constant_values=NEG_INF)
            for qc in range(Q_CHUNKS)]
    return jnp.concatenate(rows, axis=1)


def _attn_specs(S, bias_rows, bias_cols):
    n_cb = B_W // GROUP_W
    return [pl.BlockSpec((Q_TILE, GROUP_W), lambda g, n: (n + PAD_ROWS // Q_TILE, g)),
            pl.BlockSpec((PAD_ROWS + S, GROUP_W), lambda g, n: (0, n_cb + g)),
            pl.BlockSpec((PAD_ROWS + S, GROUP_W), lambda g, n: (0, 2 * n_cb + g)),
            pl.BlockSpec((HEADS_PER_GROUP, bias_rows, bias_cols), lambda g, n: (g, 0, 0))]


def _band_valid(n, qc):
    return lax.broadcasted_iota(jnp.int32, (CHUNK, B_BAND), 1) >= PAD_ROWS - n * Q_TILE - qc * CHUNK


def _chunk_rows(x, qc, rows=CHUNK):
    return x[qc * CHUNK:qc * CHUNK + rows]


def attn_fwd(qkv_pad, bias):
    S = qkv_pad.shape[0] - PAD_ROWS

    def body(q_ref, k_ref, v_ref, b_ref, o_ref):
        n = pl.program_id(1)
        start = pl.multiple_of(n * Q_TILE, Q_TILE)
        kwin = k_ref[pl.ds(start, KEY_WIN), :]
        vwin = v_ref[pl.ds(start, KEY_WIN), :]
        qv = q_ref[...]
        pairs = [(qc, hh) for qc in range(Q_CHUNKS) for hh in range(HEADS_PER_GROUP)]
        sl = lambda hh: slice(hh * B_DH, (hh + 1) * B_DH)
        s = [_dot1(_chunk_rows(qv, qc)[:, sl(hh)], _chunk_rows(kwin, qc, B_BAND)[:, sl(hh)], "nt") for qc, hh in pairs]
        s = [jnp.where(_band_valid(n, qc), x * (B_DH ** -0.5) + b_ref[hh], NEG_INF) for x, (qc, hh) in zip(s, pairs)]
        e = [jnp.exp(x - jnp.max(x, axis=-1, keepdims=True)) for x in s]
        p = [x * (1.0 / jnp.sum(x, axis=-1, keepdims=True)) for x in e]
        o = [_dot1(x, _chunk_rows(vwin, qc, B_BAND)[:, sl(hh)], "nn") for x, (qc, hh) in zip(p, pairs)]
        rows = [jnp.concatenate(o[qc * HEADS_PER_GROUP:(qc + 1) * HEADS_PER_GROUP], axis=1) for qc in range(Q_CHUNKS)]
        o_ref[...] = jnp.concatenate(rows, axis=0).astype(BF16)

    return pl.pallas_call(
        body, name="attn_fwd", grid=(N_GROUPS, S // Q_TILE), in_specs=_attn_specs(S, CHUNK, B_BAND),
        out_specs=pl.BlockSpec((Q_TILE, GROUP_W), lambda g, n: (n, g)),
        out_shape=jax.ShapeDtypeStruct((S, B_W), BF16),
        compiler_params=_cparams(("parallel", "arbitrary")),
    )(qkv_pad, qkv_pad, qkv_pad, bias)


def _rel_onehot(i):
    kj = lax.broadcasted_iota(jnp.int32, (B_BAND, B_REL), 0)
    r = lax.broadcasted_iota(jnp.int32, (B_BAND, B_REL), 1)
    idx = jnp.clip(PAD_ROWS + i - kj, -(CHUNK - 1), B_MAX_REL) + (CHUNK - 1)
    return jnp.where(idx == r, 1.0, 0.0)


def bias_expand(rel_bias):
    def body(rb_ref, o_ref):
        i = pl.program_id(0)
        o_ref[0] = _dot3(rb_ref[...], _rel_onehot(i), "nt")

    return pl.pallas_call(
        body, name="bias_expand", grid=(CHUNK,),
        in_specs=[pl.BlockSpec((B_HEADS, B_REL), lambda i: (0, 0))],
        out_specs=pl.BlockSpec((1, B_HEADS, B_BAND), lambda i: (i, 0, 0)),
        out_shape=jax.ShapeDtypeStruct((CHUNK, B_HEADS, B_BAND), F32),
        compiler_params=_cparams(("parallel",)),
    )(rel_bias)


def bias_reduce(dbias):
    def body(d_ref, o_ref):
        i = pl.program_id(0)

        @pl.when(i == 0)
        def _():
            o_ref[...] = jnp.zeros_like(o_ref)

        o_ref[...] += _dot3(d_ref[0], _rel_onehot(i), "nn")

    return pl.pallas_call(
        body, name="bias_reduce", grid=(CHUNK,),
        in_specs=[pl.BlockSpec((1, B_HEADS, B_BAND), lambda i: (i, 0, 0))],
        out_specs=pl.BlockSpec((B_HEADS, B_REL), lambda i: (0, 0)),
        out_shape=jax.ShapeDtypeStruct((B_HEADS, B_REL), F32),
        compiler_params=_cparams(("arbitrary",)),
    )(dbias)


def merge_fwd(gates_raw, b_gate, ya, yb):
    S = ya.shape[0]

    def body(g_ref, b_ref, ya_ref, yb_ref, o_ref):
        gt = _sigmoid(g_ref[...] + b_ref[...])
        o_ref[...] = (gt[:, :D_MODEL] * ya_ref[...] + gt[:, D_MODEL:] * yb_ref[...]).astype(BF16)

    return rowcall(body, name="merge_fwd", S=S, ts=512,
                   ins=[(gates_raw, "row"), (b_gate, "vec"), (ya, "row"), (yb, "row")],
                   outs=[((S, D_MODEL), BF16, "row")])[0]


def _ln_stats(xpre):
    mu = jnp.mean(xpre, axis=-1, keepdims=True)
    xc = xpre - mu
    rstd = lax.rsqrt(jnp.mean(xc * xc, axis=-1, keepdims=True) + LN_EPS)
    return xc * rstd, rstd


def ln1_fwd(x, mix, mod, ln_g, ln_b):
    S = x.shape[0]

    def body(x_ref, mix_ref, m_ref, g_ref, b_ref, xpre_ref, x1_ref, h2_ref):
        m = m_ref[...]
        xpre = ALPHA * x_ref[...] + m[GATE_T:GATE_T + 1] * mix_ref[...]
        xhat, _ = _ln_stats(xpre)
        x1 = xhat * g_ref[...] + b_ref[...]
        xpre_ref[...] = xpre
        x1_ref[...] = x1
        h2_ref[...] = (x1 * (1.0 + m[SCALE_F:SCALE_F + 1]) + m[SHIFT_F:SHIFT_F + 1]).astype(BF16)

    return rowcall(body, name="ln1_fwd", S=S, ts=512,
                   ins=[(x, "row"), (mix, "row"), (mod, "vec"), (ln_g, "vec"), (ln_b, "vec")],
                   outs=[((S, D_MODEL), F32, "row"), ((S, D_MODEL), F32, "row"), ((S, D_MODEL), BF16, "row")])


def ffn_act_fwd(up, conv_w, conv_b):
    S = up.shape[0]

    def body(u_ref, up_ref, w_ref, b_ref, o_ref):
        first = (pl.program_id(0) > 0).astype(F32)
        uc = _conv_fwd(u_ref[...].astype(F32), _halo_prev(up_ref) * first, w_ref[...], FFN_CONV) + b_ref[...]
        o_ref[...] = (_silu(uc[:, :D_FF]) * uc[:, D_FF:]).astype(BF16)

    return rowcall(body, name="ffn_act_fwd", S=S, ts=128,
                   ins=[(up, "row"), (up, "prev"), (conv_w, "vec"), (conv_b, "vec")],
                   outs=[((S, D_FF), BF16, "row")])[0]


def final_fwd_bwd(x1, ffn, target, mod, ln_g, ln_b):
    S = x1.shape[0]

    def body(x1_ref, f_ref, t_ref, m_ref, g_ref, b_ref, dxpre_ref, dffn_ref, loss_ref, dgate_ref, dg_ref, db_ref):
        gate = m_ref[...][GATE_F:GATE_F + 1]
        ffn_v = f_ref[...]
        xpre = ALPHA * x1_ref[...] + gate * ffn_v
        xhat, rstd = _ln_stats(xpre)
        err = xhat * g_ref[...] + b_ref[...] - t_ref[...]
        loss_ref[...] += 0.5 * jnp.sum(jnp.mean(err * err, axis=-1, keepdims=True), axis=0, keepdims=True)
        dy = err * (1.0 / D_MODEL)
        dg_ref[...] += jnp.sum(dy * xhat, axis=0, keepdims=True)
        db_ref[...] += jnp.sum(dy, axis=0, keepdims=True)
        dyg = dy * g_ref[...]
        dxpre = rstd * (dyg - jnp.mean(dyg, axis=-1, keepdims=True) - xhat * jnp.mean(dyg * xhat, axis=-1, keepdims=True))
        dxpre_ref[...] = dxpre
        dffn_ref[...] = (gate * dxpre).astype(BF16)
        dgate_ref[...] += jnp.sum(dxpre * ffn_v, axis=0, keepdims=True)

    vec = ((1, D_MODEL), F32, "acc")
    return rowcall(body, name="final_fwd_bwd", S=S, ts=512,
                   ins=[(x1, "row"), (ffn, "row"), (target, "row"), (mod, "vec"), (ln_g, "vec"), (ln_b, "vec")],
                   outs=[((S, D_MODEL), F32, "row"), ((S, D_MODEL), BF16, "row"), ((1, 1), F32, "acc"), vec, vec, vec])


def _ffn_duc(dact, uc):
    ug, uv = uc[:, :D_FF], uc[:, D_FF:]
    sg, dsg = _silu_and_grad(ug)
    return jnp.concatenate([dact * uv * dsg, dact * sg], axis=1)


def ffn_act_bwd(dact, up, conv_w, conv_b):
    S = up.shape[0]
    ts = 128

    def body(d_ref, dn_ref, u_ref, up_ref, un_ref, w_ref, b_ref, dup_ref, dw_ref, db_ref):
        i = pl.program_id(0)
        first = (i > 0).astype(F32)
        last = (i < pl.num_programs(0) - 1).astype(F32)
        w, b = w_ref[...], b_ref[...]
        cur, prev = u_ref[...].astype(F32), _halo_prev(up_ref) * first
        shifted = [_shift_down(cur, prev, FFN_CONV - 1 - j) for j in range(FFN_CONV)]
        uc = b + sum(shifted[j] * w[j:j + 1] for j in range(FFN_CONV))
        duc = _ffn_duc(d_ref[...].astype(F32), uc)
        uc_n = _conv_fwd(_halo_next(un_ref), cur[ts - 8:ts], w, FFN_CONV) + b
        duc_n = _ffn_duc(_halo_next(dn_ref), uc_n) * last
        db_ref[...] += jnp.sum(duc, axis=0, keepdims=True)
        for j in range(FFN_CONV):
            dw_ref[j:j + 1, :] += jnp.sum(duc * shifted[j], axis=0, keepdims=True)
        dup = duc * w[FFN_CONV - 1:FFN_CONV]
        for j in range(FFN_CONV - 1):
            dup = dup + _shift_up(duc, duc_n, FFN_CONV - 1 - j) * w[j:j + 1]
        dup_ref[...] = dup.astype(BF16)

    return rowcall(body, name="ffn_act_bwd", S=S, ts=ts,
                   ins=[(dact, "row"), (dact, "next"), (up, "row"), (up, "prev"), (up, "next"), (conv_w, "vec"), (conv_b, "vec")],
                   outs=[((S, 2 * D_FF), BF16, "row"), ((FFN_CONV, 2 * D_FF), F32, "acc"), ((1, 2 * D_FF), F32, "acc")])


def ln1_bwd(dxpre2, dh2, xpre1, mix, mod, ln_g, ln_b):
    S = xpre1.shape[0]

    def body(d2_ref, dh_ref, xp_ref, mix_ref, m_ref, g_ref, b_ref, dxpre_ref, dmix_ref,
             dscale_ref, dshift_ref, dgate_ref, dg_ref, db_ref):
        m = m_ref[...]
        xhat, rstd = _ln_stats(xp_ref[...])
        x1 = xhat * g_ref[...] + b_ref[...]
        dh = dh_ref[...]
        dx1 = ALPHA * d2_ref[...] + dh * (1.0 + m[SCALE_F:SCALE_F + 1])
        dscale_ref[...] += jnp.sum(dh * x1, axis=0, keepdims=True)
        dshift_ref[...] += jnp.sum(dh, axis=0, keepdims=True)
        dg_ref[...] += jnp.sum(dx1 * xhat, axis=0, keepdims=True)
        db_ref[...] += jnp.sum(dx1, axis=0, keepdims=True)
        dyg = dx1 * g_ref[...]
        dxpre = rstd * (dyg - jnp.mean(dyg, axis=-1, keepdims=True) - xhat * jnp.mean(dyg * xhat, axis=-1, keepdims=True))
        dxpre_ref[...] = dxpre
        dmix_ref[...] = (m[GATE_T:GATE_T + 1] * dxpre).astype(BF16)
        dgate_ref[...] += jnp.sum(dxpre * mix_ref[...], axis=0, keepdims=True)

    vec = ((1, D_MODEL), F32, "acc")
    return rowcall(body, name="ln1_bwd", S=S, ts=512,
                   ins=[(dxpre2, "row"), (dh2, "row"), (xpre1, "row"), (mix, "row"), (mod, "vec"), (ln_g, "vec"), (ln_b, "vec")],
                   outs=[((S, D_MODEL), F32, "row"), ((S, D_MODEL), BF16, "row"), vec, vec, vec, vec, vec])


def merge_bwd(dmerged, gates_raw, b_gate, ya, yb):
    S = ya.shape[0]

    def body(d_ref, g_ref, b_ref, ya_ref, yb_ref, dya_ref, dyb_ref, dg_ref, dbg_ref):
        gt = _sigmoid(g_ref[...] + b_ref[...])
        d = d_ref[...]
        ga, gb = gt[:, :D_MODEL], gt[:, D_MODEL:]
        dya_ref[...] = (d * ga).astype(BF16)
        dyb_ref[...] = (d * gb).astype(BF16)
        dgr = jnp.concatenate([d * ya_ref[...] * ga * (1.0 - ga), d * yb_ref[...] * gb * (1.0 - gb)], axis=1)
        dg_ref[...] = dgr.astype(BF16)
        dbg_ref[...] += jnp.sum(dgr, axis=0, keepdims=True)

    return rowcall(body, name="merge_bwd", S=S, ts=512,
                   ins=[(dmerged, "row"), (gates_raw, "row"), (b_gate, "vec"), (ya, "row"), (yb, "row")],
                   outs=[((S, D_MODEL), BF16, "row"), ((S, D_MODEL), BF16, "row"), ((S, 2 * D_MODEL), BF16, "row"),
                         ((1, 2 * D_MODEL), F32, "acc")])


def attn_bwd(qkv_pad, btile, do_b):
    S = qkv_pad.shape[0] - PAD_ROWS

    def body(q_ref, k_ref, v_ref, b_ref, do_ref, dq_ref, dk_ref, dv_ref, db_ref):
        n = pl.program_id(1)

        @pl.when(n == 0)
        def _():
            dk_ref[...] = jnp.zeros_like(dk_ref)
            dv_ref[...] = jnp.zeros_like(dv_ref)
            db_ref[...] = jnp.zeros_like(db_ref)

        start = pl.multiple_of(n * Q_TILE, Q_TILE)
        kwin = k_ref[pl.ds(start, KEY_WIN), :]
        vwin = v_ref[pl.ds(start, KEY_WIN), :]
        qv, dov = q_ref[...], do_ref[...]
        valid = lax.broadcasted_iota(jnp.int32, (Q_TILE, KEY_WIN), 1) >= PAD_ROWS - n * Q_TILE
        dqs, dks, dvs = [], [], []
        for hh in range(HEADS_PER_GROUP):
            sl = slice(hh * B_DH, (hh + 1) * B_DH)
            p = _band_probs(qv[:, sl], kwin[:, sl], b_ref[hh], valid)
            dp = _dot1(dov[:, sl], vwin[:, sl], "nt")
            ds = p * (dp - jnp.sum(dp * p, axis=-1, keepdims=True))
            dbh = ds[0:CHUNK, 0:B_BAND]
            for qc in range(1, Q_CHUNKS):
                dbh = dbh + ds[qc * CHUNK:(qc + 1) * CHUNK, qc * CHUNK:qc * CHUNK + B_BAND]
            db_ref[hh] += dbh
            dsq = ds * (B_DH ** -0.5)
            dqs.append(_dot1(dsq, kwin[:, sl], "nn"))
            dks.append(_dot1(dsq, qv[:, sl], "tn"))
            dvs.append(_dot1(p, dov[:, sl], "tn"))
        dq_ref[...] = jnp.concatenate(dqs, axis=1).astype(BF16)
        dk_ref[pl.ds(start, KEY_WIN), :] += jnp.concatenate(dks, axis=1)
        dv_ref[pl.ds(start, KEY_WIN), :] += jnp.concatenate(dvs, axis=1)

    col = pl.BlockSpec((PAD_ROWS + S, GROUP_W), lambda g, n: (0, g))
    tile = pl.BlockSpec((Q_TILE, GROUP_W), lambda g, n: (n, g))
    return pl.pallas_call(
        body, name="attn_bwd", grid=(N_GROUPS, S // Q_TILE), in_specs=_attn_specs(S, Q_TILE, KEY_WIN) + [tile],
        out_specs=[tile, col, col, pl.BlockSpec((HEADS_PER_GROUP, CHUNK, B_BAND), lambda g, n: (g, 0, 0))],
        out_shape=[jax.ShapeDtypeStruct((S, B_W), BF16), jax.ShapeDtypeStruct((PAD_ROWS + S, B_W), F32),
                   jax.ShapeDtypeStruct((PAD_ROWS + S, B_W), F32), jax.ShapeDtypeStruct((B_HEADS, CHUNK, B_BAND), F32)],
        compiler_params=_cparams(("parallel", "arbitrary")),
    )(qkv_pad, qkv_pad, qkv_pad, btile, do_b)


def gate_a_bwd(do_a, o_pre, z, norm_w):
    S = o_pre.shape[0]

    def body(d_ref, o_ref, z_ref, nw_ref, dop_ref, dz_ref, dnw_ref):
        nw = nw_ref[...]
        acc = jnp.zeros((1, A_DK), F32)
        for h in range(A_HEADS):
            sl = slice(h * A_DK, (h + 1) * A_DK)
            oh, zh, dh = o_ref[:, sl], z_ref[:, sl], d_ref[:, sl]
            r = lax.rsqrt(jnp.mean(oh * oh, axis=-1, keepdims=True) + RMS_EPS)
            sz, dsz = _silu_and_grad(zh)
            dz_ref[:, sl] = (dh * oh * r * nw * dsz).astype(BF16)
            acc = acc + jnp.sum(dh * oh * r * sz, axis=0, keepdims=True)
            t = dh * nw * sz
            dop_ref[:, sl] = r * t - oh * (r * r * r) * jnp.mean(t * oh, axis=-1, keepdims=True)
        dnw_ref[...] += acc

    return rowcall(body, name="gate_a_bwd", S=S, ts=512,
                   ins=[(do_a, "row"), (o_pre, "row"), (z, "row"), (norm_w, "vec")],
                   outs=[((S, A_W), F32, "row"), ((S, A_W), BF16, "row"), ((1, A_DK), F32, "acc")])


def delta_bwd(q, k, v, beta, g, sprev, tinv, do):
    S = q.shape[0]
    n_chunks = S // CHUNK

    def body(q_ref, k_ref, v_ref, beta_ref, g_ref, sprev_ref, t_ref, do_ref,
             dq_ref, dk_ref, dv_ref, dbeta_ref, dg_ref, dstate_ref):
        @pl.when(pl.program_id(0) == 0)
        def _():
            dstate_ref[...] = jnp.zeros_like(dstate_ref)

        mk = _tri_masks()
        causal, strict, eye = mk["causal"], mk["strict"], mk["eye"]
        blk_end = (lax.broadcasted_iota(jnp.int32, (GROUP_ROWS, 1), 0) & (CHUNK - 1)) == CHUNK - 1
        lane = lax.broadcasted_iota(jnp.int32, (CHUNK, A_HEADS), 1)
        betav, gv = beta_ref[...], g_ref[...]
        dbeta_t = jnp.zeros((CHUNK, A_HEADS), F32)
        dg_t = jnp.zeros((CHUNK, A_HEADS), F32)
        groups, heads = range(N_HEAD_GROUPS), range(HEAD_GROUP)
        st = [dict() for _ in groups]

        def local_part(grp, s):
            s["qs"], s["ks"], s["vs"] = _stack_heads(q_ref, grp), _stack_heads(k_ref, grp), _stack_heads(v_ref, grp)
            s["dos"] = _stack_heads(do_ref, grp)
            s["bs"] = _stack_cols(betav, grp)
            s["loc"] = loc = _delta_local(s["qs"], s["ks"], s["vs"], s["bs"], _stack_cols(gv, grp), mk)
            s["tinv"] = t_ref[0, grp]
            s["rhs"] = jnp.concatenate([loc["vb"], loc["y"]], axis=1)
            s["uw"] = _dot3(s["tinv"], s["rhs"], "nn")

        def state_part(grp, s):
            loc, uw, dos, qs = s["loc"], s["uw"], s["dos"], s["qs"]
            gam, kd, gl, gc = loc["gam"], loc["kd"], loc["gl"], loc["gc"]
            qg = qs * gam
            egl = jnp.exp(gl)
            hid = [grp * HEAD_GROUP + j for j in heads]
            s0 = [sprev_ref[0, h] for h in hid]
            ds1 = [dstate_ref[h] for h in hid]
            w = [_head_rows(uw, j)[:, A_DK:] for j in heads]
            vn = [_head_rows(uw, j)[:, :A_DK] - _dot1(w[j], s0[j], "nn") for j in heads]
            vns = jnp.concatenate(vn, axis=0)
            dvn_local = _dot1(loc["p"], dos, "tn")
            dvn = [_head_rows(dvn_local, j) + _dot1(_head_rows(kd, j), ds1[j], "nn") for j in heads]
            dvns = jnp.concatenate(dvn, axis=0)
            s["dp"] = jnp.where(causal, _dot1(dos, vns, "nt"), 0.0)
            dqg = jnp.concatenate([_dot1(_head_rows(dos, j), s0[j], "nt") for j in heads], axis=0)
            s["dq"] = dqg * gam
            dgc = jnp.sum(dqg * qg, axis=-1, keepdims=True)
            for j in heads:
                dstate_ref[hid[j]] = (_dot1(_head_rows(qg, j), _head_rows(dos, j), "tn")
                                      + egl[(j + 1) * CHUNK - 1:(j + 1) * CHUNK] * ds1[j] - _dot1(w[j], dvn[j], "tn"))
            dkd = jnp.concatenate([_dot1(vn[j], ds1[j], "nt") for j in heads], axis=0)
            s["dk"] = dkd * jnp.exp(gl - gc)
            t1 = jnp.sum(dkd * kd, axis=-1, keepdims=True)
            dgl = jnp.concatenate(
                [jnp.broadcast_to(jnp.sum(_head_rows(t1, j), axis=0, keepdims=True)
                                  + jnp.sum(jnp.sum(ds1[j] * s0[j], axis=-1, keepdims=True), axis=0, keepdims=True)
                                  * egl[(j + 1) * CHUNK - 1:(j + 1) * CHUNK], (CHUNK, 1)) for j in heads], axis=0)
            s["dgc"] = dgc - t1 + jnp.where(blk_end, dgl, 0.0)
            s["duw"] = jnp.concatenate(
                [dvns, jnp.concatenate([-_dot1(dvn[j], s0[j], "nt") for j in heads], axis=0)], axis=1)

        def solve_part(grp, s):
            s["dvby"] = _dot3(s["tinv"], s["duw"], "tn")
            s["dt"] = _dot3(s["duw"], s["rhs"], "nt")

        def inverse_part_a(grp, s):
            s["tdt"] = _dot3(s["tinv"], s["dt"], "tn")

        def inverse_part_b(grp, s):
            s["da"] = jnp.where(strict, -_dot3(s["tdt"], s["tinv"], "nt"), 0.0)

        def finish(grp, s):
            loc, qs, ks, vs, bs, da, dp, dvby = s["loc"], s["qs"], s["ks"], s["vs"], s["bs"], s["da"], s["dp"], s["dvby"]
            gam, decay = loc["gam"], loc["decay"]
            dm = da * decay
            dn = dp * decay
            e = da * loc["a"] + dp * loc["p"]
            dgc = s["dgc"] + jnp.sum(e, axis=1, keepdims=True) - _row_to_col(jnp.sum(e, axis=0, keepdims=True), eye)
            dy = dvby[:, A_DK:]
            dvb = dvby[:, :A_DK]
            dkb = _dot1(dm, ks, "nn") + dy * gam
            dk = s["dk"] + _dot1(dm, loc["kb"], "tn") + _dot1(dn, qs, "tn") + dkb * bs
            dq = s["dq"] + _dot1(dn, ks, "nn")
            dgc = dgc + jnp.sum(dy * loc["y"], axis=-1, keepdims=True)
            dbeta = jnp.sum(dkb * ks, axis=-1, keepdims=True) + jnp.sum(dvb * vs, axis=-1, keepdims=True)
            dv = dvb * bs
            dgs = jnp.sum(jnp.where(mk["upper"], _col_to_row(dgc, eye), 0.0), axis=1, keepdims=True)
            for j in heads:
                h = grp * HEAD_GROUP + j
                sl = slice(h * A_DK, (h + 1) * A_DK)
                dq_ref[:, sl] = _head_rows(dq, j)
                dk_ref[:, sl] = _head_rows(dk, j)
                dv_ref[:, sl] = _head_rows(dv, j)
            s["dbeta"], s["dgs"] = dbeta, dgs

        for stage in (local_part, state_part, solve_part, inverse_part_a, inverse_part_b, finish):
            for grp in groups:
                stage(grp, st[grp])
        for grp in groups:
            for j in heads:
                h = grp * HEAD_GROUP + j
                dbeta_t = dbeta_t + jnp.where(lane == h, _head_rows(st[grp]["dbeta"], j), 0.0)
                dg_t = dg_t + jnp.where(lane == h, _head_rows(st[grp]["dgs"], j), 0.0)
        dbeta_ref[...] = dbeta_t
        dg_ref[...] = dg_t

    rev = lambda n: (n_chunks - 1 - n, 0)
    rev4 = lambda n: (n_chunks - 1 - n, 0, 0, 0)
    tile = pl.BlockSpec((CHUNK, A_W), rev)
    small = pl.BlockSpec((CHUNK, A_HEADS), rev)
    return pl.pallas_call(
        body, name="delta_bwd", grid=(n_chunks,),
        in_specs=[tile, tile, tile, small, small, pl.BlockSpec((1, A_HEADS, A_DK, A_DK), rev4),
                  pl.BlockSpec((1, N_HEAD_GROUPS, GROUP_ROWS, GROUP_ROWS), rev4), tile],
        out_specs=[tile, tile, tile, small, small],
        out_shape=[jax.ShapeDtypeStruct((S, A_W), F32)] * 3 + [jax.ShapeDtypeStruct((S, A_HEADS), F32)] * 2,
        scratch_shapes=[pltpu.VMEM((A_HEADS, A_DK, A_DK), F32)],
        compiler_params=_cparams(("arbitrary",)),
    )(q, k, v, beta, g, sprev, tinv, do)


def _prep_a_dpre(raw, raw_prev, w, dq, dk, dv):
    y, dy_dpre = _prep_a_core(raw, raw_prev, w)
    parts = []
    for h in range(A_HEADS):
        yq = y[:, h * A_DK:(h + 1) * A_DK]
        dqh = dq[:, h * A_DK:(h + 1) * A_DK]
        rq = lax.rsqrt(jnp.sum(yq * yq, axis=-1, keepdims=True) + L2_EPS)
        parts.append((A_DK ** -0.5) * (rq * dqh - yq * (rq * rq * rq) * jnp.sum(dqh * yq, axis=-1, keepdims=True)))
    for h in range(A_HEADS):
        yk = y[:, A_W + h * A_DK:A_W + (h + 1) * A_DK]
        dkh = dk[:, h * A_DK:(h + 1) * A_DK]
        rk = lax.rsqrt(jnp.sum(yk * yk, axis=-1, keepdims=True) + L2_EPS)
        parts.append(rk * dkh - yk * (rk * rk * rk) * jnp.sum(dkh * yk, axis=-1, keepdims=True))
    parts.append(dv)
    return jnp.concatenate(parts, axis=1) * dy_dpre


def prep_a_bwd(qkv_raw, ba, conv_a, a_log, dt_bias, dq, dk, dv, dbeta, dg):
    S = qkv_raw.shape[0]
    ts = 256

    def body(x_ref, xp_ref, xn_ref, ba_ref, w_ref, al_ref, dt_ref, dq_ref, dqn_ref, dk_ref, dkn_ref, dv_ref, dvn_ref,
             dbeta_ref, dg_ref, draw_ref, dba_ref, dw_ref, dal_ref, ddt_ref):
        i = pl.program_id(0)
        first = (i > 0).astype(F32)
        last = (i < pl.num_programs(0) - 1).astype(F32)
        w = w_ref[...]
        cur, prev = x_ref[...].astype(F32), _halo_prev(xp_ref) * first
        dpre = _prep_a_dpre(cur, prev, w, dq_ref[...], dk_ref[...], dv_ref[...])
        dpre_n = _prep_a_dpre(_halo_next(xn_ref), cur[ts - 8:ts], w, _halo_next(dqn_ref), _halo_next(dkn_ref),
                              _halo_next(dvn_ref)) * last
        for j in range(A_CONV):
            dw_ref[j:j + 1, :] += jnp.sum(dpre * _shift_down(cur, prev, A_CONV - 1 - j), axis=0, keepdims=True)
        draw = dpre * w[A_CONV - 1:A_CONV]
        for j in range(A_CONV - 1):
            draw = draw + _shift_up(dpre, dpre_n, A_CONV - 1 - j) * w[j:j + 1]
        draw_ref[...] = draw.astype(BF16)
        bav = ba_ref[...]
        beta = _sigmoid(bav[:, 0:A_HEADS])
        xa = bav[:, A_HEADS:2 * A_HEADS] + dt_ref[...]
        nexp = -jnp.exp(al_ref[...])
        dgv = dg_ref[...]
        da = dgv * nexp * _sigmoid(xa)
        dba_ref[:, 0:A_HEADS] = dbeta_ref[...] * beta * (1.0 - beta)
        dba_ref[:, A_HEADS:2 * A_HEADS] = da
        dal_ref[...] += jnp.sum(dgv * nexp * _softplus(xa), axis=0, keepdims=True)
        ddt_ref[...] += jnp.sum(da, axis=0, keepdims=True)

    return rowcall(
        body, name="prep_a_bwd", S=S, ts=ts,
        ins=[(qkv_raw, "row"), (qkv_raw, "prev"), (qkv_raw, "next"), (ba, "row"), (conv_a, "vec"), (a_log, "vec"),
             (dt_bias, "vec"), (dq, "row"), (dq, "next"), (dk, "row"), (dk, "next"), (dv, "row"), (dv, "next"),
             (dbeta, "row"), (dg, "row")],
        outs=[((S, 3 * A_W), BF16, "row"), ((S, 2 * A_HEADS), F32, "row"), ((A_CONV, 3 * A_W), F32, "acc"),
              ((1, A_HEADS), F32, "acc"), ((1, A_HEADS), F32, "acc")])


def grad_x_final(dh1, x, dxpre1, mod):
    S = x.shape[0]

    def body(dh_ref, x_ref, dx_ref, m_ref, gx_ref, dscale_ref, dshift_ref):
        dh = dh_ref[...]
        gx_ref[...] = ALPHA * dx_ref[...] + dh * (1.0 + m_ref[...][SCALE_T:SCALE_T + 1])
        dscale_ref[...] += jnp.sum(dh * x_ref[...], axis=0, keepdims=True)
        dshift_ref[...] += jnp.sum(dh, axis=0, keepdims=True)

    vec = ((1, D_MODEL), F32, "acc")
    return rowcall(body, name="grad_x_final", S=S, ts=512, ins=[(dh1, "row"), (x, "row"), (dxpre1, "row"), (mod, "vec")],
                   outs=[((S, D_MODEL), F32, "row"), vec, vec])


_C_QKV, _C_Z, _C_BA, _C_QKVB, _C_G = 0, 3 * A_W, 4 * A_W, 4 * A_W + 2 * A_HEADS, 4 * A_W + 2 * A_HEADS + 3 * B_W
BA_PAD = 128


def split_w_in(w_in):
    ba = jnp.pad(w_in[:, _C_BA:_C_QKVB], ((0, 0), (0, BA_PAD - 2 * A_HEADS)))
    return dict(qkv=w_in[:, _C_QKV:_C_Z], z=w_in[:, _C_Z:_C_BA], ba=ba, qkvb=w_in[:, _C_QKVB:_C_G], g=w_in[:, _C_G:])


def join_w_in(p):
    return jnp.concatenate([p["qkv"], p["z"], p["ba"][:, :2 * A_HEADS], p["qkvb"], p["g"]], axis=1)


def forward_local(x, target, mod, w, sm, late_weights=None):
    h1 = modulate(x, mod, SHIFT_T, SCALE_T, "mod_t")
    qkv_raw = mm(h1, w["qkv"], mode="nn", out_dtype=BF16, name="proj_qkv")
    z = mm(h1, w["z"], mode="nn", out_dtype=F32, name="proj_z")
    ba = mm(h1, w["ba"], mode="nn", out_dtype=F32, name="proj_ba")
    qkvb = mm(h1, w["qkvb"], mode="nn", out_dtype=BF16, name="proj_qkvb")
    gates_raw = mm(h1, w["g"], mode="nn", out_dtype=F32, name="proj_g")
    q, k, v, beta, g = prep_a_fwd(qkv_raw, ba, sm["conv_a"], sm["a_log"], sm["dt_bias"])
    o_pre, sprev, tinv = delta_fwd(q, k, v, beta, g)
    o_a = gate_a_fwd(o_pre, z, sm["norm_a"])
    qkv_pad = jnp.pad(qkvb, ((PAD_ROWS, 0), (0, 0)))
    bias = jnp.transpose(bias_expand(sm["rel_bias"]), (1, 0, 2))
    o_b = attn_fwd(qkv_pad, bias)
    if late_weights is not None:
        w = dict(w, **late_weights(o_b))
    ya = mm(o_a, w["branch_a"], mode="nn", out_dtype=F32, name="branch_a")
    yb = mm(o_b, w["branch_b"], mode="nn", out_dtype=F32, name="branch_b")
    merged = merge_fwd(gates_raw, sm["b_gate"], ya, yb)
    mix = mm(merged, w["o"], mode="nn", out_dtype=F32, name="mix")
    xpre1, x1, h2 = ln1_fwd(x, mix, mod, sm["ln1_g"], sm["ln1_b"])
    up = mm(h2, w["up"], mode="nn", out_dtype=BF16, name="ffn_up")
    act = ffn_act_fwd(up, sm["conv_ffn"], sm["b_conv_ffn"])
    ffn = mm(act, w["down"], mode="nn", out_dtype=F32, name="ffn_down")
    dxpre2, dffn, loss, dgate_f, dln2_g, dln2_b = final_fwd_bwd(x1, ffn, target, mod, sm["ln2_g"], sm["ln2_b"])
    saved = dict(h1=h1, qkv_raw=qkv_raw, z=z, ba=ba, gates_raw=gates_raw, q=q, k=k, v=v, beta=beta, g=g,
                 o_pre=o_pre, sprev=sprev, tinv=tinv, o_a=o_a, qkv_pad=qkv_pad, bias=bias, o_b=o_b, ya=ya, yb=yb,
                 merged=merged, mix=mix, xpre1=xpre1, x1=x1, h2=h2, up=up, act=act, ffn=ffn, w=w)
    return loss, dxpre2, dffn, dict(gate_f=dgate_f, ln2_g=dln2_g, ln2_b=dln2_b), saved


def backward_local(x, mod, sm, dxpre2, dffn, fin, sv, early_grads=None):
    w = sv["w"]
    dact = mm(dffn, w["down"], mode="nt", out_dtype=BF16, name="d_act")
    gw_down = mm(sv["act"], dffn, mode="tn", out_dtype=BF16, name="gw_down")
    dup, dconv_ffn, db_conv_ffn = ffn_act_bwd(dact, sv["up"], sm["conv_ffn"], sm["b_conv_ffn"])
    dh2 = mm(dup, w["up"], mode="nt", out_dtype=F32, name="d_h2")
    gw_up = mm(sv["h2"], dup, mode="tn", out_dtype=BF16, name="gw_up")
    dxpre1, dmix, dsc_f, dsh_f, dgate_t, dln1_g, dln1_b = ln1_bwd(
        dxpre2, dh2, sv["xpre1"], sv["mix"], mod, sm["ln1_g"], sm["ln1_b"])
    dmerged = mm(dmix, w["o"], mode="nt", out_dtype=F32, name="d_merged")
    gw_o = mm(sv["merged"], dmix, mode="tn", out_dtype=BF16, name="gw_o")
    dya, dyb, dgates, db_gate = merge_bwd(dmerged, sv["gates_raw"], sm["b_gate"], sv["ya"], sv["yb"])
    do_a = mm(dya, w["branch_a"], mode="nt", out_dtype=F32, name="d_oa")
    gw_branch_a = mm(sv["o_a"], dya, mode="tn", out_dtype=BF16, name="gw_branch_a")
    do_b = mm(dyb, w["branch_b"], mode="nt", out_dtype=BF16, name="d_ob")
    gw_branch_b = mm(sv["o_b"], dyb, mode="tn", out_dtype=BF16, name="gw_branch_b")
    btile = bias_tiles(sv["bias"])
    if early_grads is not None:
        btile = btile + early_grads(dict(w_branch_a=gw_branch_a, w_branch_b=gw_branch_b, w_o=gw_o, w_up=gw_up,
                                         w_down=gw_down))[0, 0]
    dq_b, dk_pad, dv_pad, dbias = attn_bwd(sv["qkv_pad"], btile, do_b)
    dqkvb = jnp.concatenate([dq_b, dk_pad[PAD_ROWS:].astype(BF16), dv_pad[PAD_ROWS:].astype(BF16)], axis=1)
    drel_bias = bias_reduce(jnp.transpose(dbias, (1, 0, 2)))
    do_pre, dz, dnorm_a = gate_a_bwd(do_a, sv["o_pre"], sv["z"], sm["norm_a"])
    dq, dk, dv, dbeta, dg = delta_bwd(sv["q"], sv["k"], sv["v"], sv["beta"], sv["g"], sv["sprev"], sv["tinv"], do_pre)
    dqkv_raw, dba16, dconv_a, da_log, ddt_bias = prep_a_bwd(
        sv["qkv_raw"], sv["ba"], sm["conv_a"], sm["a_log"], sm["dt_bias"], dq, dk, dv, dbeta, dg)
    dba = jnp.pad(dba16, ((0, 0), (0, BA_PAD - 2 * A_HEADS))).astype(BF16)
    pieces = dict(qkv=dqkv_raw, z=dz, ba=dba, qkvb=dqkvb, g=dgates)
    dh1 = None
    gw_in = {}
    for key, dpiece in pieces.items():
        dh1 = mm(dpiece, w[key], mode="nt", out_dtype=F32, name="d_h1_" + key, acc_in=dh1)
        gw_in[key] = mm(sv["h1"], dpiece, mode="tn", out_dtype=BF16, name="gw_in_" + key)
    grad_x, dsc_t, dsh_t = grad_x_final(dh1, x, dxpre1, mod)
    dmod = jnp.concatenate([dsh_t, dsc_t, dgate_t, dsh_f, dsc_f, fin["gate_f"]], axis=0)
    gw = dict(w_in=join_w_in(gw_in), w_branch_a=gw_branch_a, w_branch_b=gw_branch_b, w_o=gw_o, w_up=gw_up, w_down=gw_down)
    gs = dict(b_gate=db_gate, conv_a=dconv_a, a_log=da_log, dt_bias=ddt_bias, norm_a=dnorm_a, rel_bias=drel_bias,
              ln1_g=dln1_g, ln1_b=dln1_b, conv_ffn=dconv_ffn, b_conv_ffn=db_conv_ffn, ln2_g=fin["ln2_g"], ln2_b=fin["ln2_b"])
    return grad_x, dmod, gw, gs


MESH = pl.DeviceIdType.MESH
ANY = pl.BlockSpec(memory_space=pl.ANY)
WHOLE_VMEM = pl.BlockSpec(memory_space=pltpu.VMEM)


def _place():
    return lax.axis_index("x"), lax.axis_index("y"), lax.axis_index("c")


def allgather8(blk, name):
    m_per, n = blk.shape

    def body(x_ref, out_ref, send_sems, recv_sems, local_sem):
        x, y, c = _place()
        me, sibling = (x, y, c), (x, y, 1 - c)
        chips = [(1 - x, y), (x, 1 - y), (1 - x, 1 - y)]

        def rows(px, py, pc):
            return out_ref.at[pl.ds((4 * px + 2 * py + pc) * m_per, m_per), :]

        def copy(k, block, to, src=None):
            return pltpu.make_async_remote_copy(
                src_ref=rows(*block) if src is None else src, dst_ref=rows(*block),
                send_sem=send_sems.at[k], recv_sem=recv_sems.at[k], device_id=to, device_id_type=MESH)

        mine = pltpu.make_async_copy(x_ref, rows(*me), local_sem)
        mine.start()
        first = [copy(0, me, sibling, src=x_ref)]
        first += [copy(1 + j, me, (*chip, c), src=x_ref) for j, chip in enumerate(chips)]
        for cp in first:
            cp.start()
        passed = [copy(4 + j, (*chip, c), sibling) for j, chip in enumerate(chips)]
        for j, chip in enumerate(chips):
            copy(1 + j, (*chip, c), me).wait_recv()
            passed[j].start()
        copy(0, sibling, me).wait_recv()
        for j, chip in enumerate(chips):
            copy(4 + j, (*chip, 1 - c), me).wait_recv()
        for cp in first + passed:
            cp.wait_send()
        mine.wait()

    return pl.pallas_call(
        body, name=name, out_shape=jax.ShapeDtypeStruct((N_DEV * m_per, n), blk.dtype),
        in_specs=[WHOLE_VMEM], out_specs=WHOLE_VMEM,
        scratch_shapes=[pltpu.SemaphoreType.DMA((7,)), pltpu.SemaphoreType.DMA((7,)), pltpu.SemaphoreType.DMA],
    )(blk)


def _chip_peers(x, y):
    return [(1 - x, y), (x, 1 - y), (1 - x, 1 - y)]


def chip_exchange(arrs, name, scatter):
    n = len(arrs)

    def body(*refs):
        ins, outs = refs[:n], refs[n:2 * n]
        send_sems, recv_sems, local_sems = refs[2 * n:]
        x, y, c = _place()
        me = 2 * x + y
        sibling = (x, y, 1 - c)
        peers = _chip_peers(x, y)

        def half(ref, which):
            r2 = ref.shape[0] // 2
            return ref.at[pl.ds(which * r2, r2), :]

        def outgoing(a, chip):
            return ins[a].at[chip] if scatter else ins[a]

        def copy(k, src, dst, to):
            return pltpu.make_async_remote_copy(src_ref=src, dst_ref=dst, send_sem=send_sems.at[k],
                                                recv_sem=recv_sems.at[k], device_id=to, device_id_type=MESH)

        started, local = [], []
        for a in range(n):
            lc = pltpu.make_async_copy(outgoing(a, me), outs[a].at[me], local_sems.at[a])
            lc.start()
            local.append(lc)
            for j, (px, py) in enumerate(peers):
                cp = copy(6 * a + j, half(outgoing(a, 2 * px + py), c), half(outs[a].at[me], c), (px, py, c))
                cp.start()
                started.append(cp)
        for a in range(n):
            for j, (px, py) in enumerate(peers):
                landed = half(outs[a].at[2 * px + py], c)
                copy(6 * a + j, landed, landed, (px, py, c)).wait_recv()
                relay = copy(6 * a + 3 + j, landed, landed, sibling)
                relay.start()
                started.append(relay)
        for a in range(n):
            for j, (px, py) in enumerate(peers):
                other = half(outs[a].at[2 * px + py], 1 - c)
                copy(6 * a + 3 + j, other, other, sibling).wait_recv()
        for cp in started:
            cp.wait_send()
        for lc in local:
            lc.wait()

    out_shape = [jax.ShapeDtypeStruct(a.shape if scatter else (N_CHIPS,) + a.shape, a.dtype) for a in arrs]
    return pl.pallas_call(
        body, name=name, out_shape=out_shape, in_specs=[ANY] * n, out_specs=[ANY] * n,
        scratch_shapes=[pltpu.SemaphoreType.DMA((6 * n,)), pltpu.SemaphoreType.DMA((6 * n,)), pltpu.SemaphoreType.DMA((n,))],
    )(*arrs)


HBM_SPEC = pl.BlockSpec(memory_space=pltpu.HBM)
SEM_SPEC = pl.BlockSpec(memory_space=pltpu.SEMAPHORE)
SIDE_EFFECT = pltpu.SideEffectType.DATAFLOW_SIDE_EFFECTING


def _in_hbm(a):
    return pltpu.with_memory_space_constraint(a, pltpu.HBM)


def exchange_start(arrs, name, scatter, after):
    n = len(arrs)
    lands = [lax.empty(a.shape if scatter else (N_CHIPS,) + a.shape, a.dtype) for a in arrs]

    def body(*refs):
        ins, zones = refs[:n], refs[n:2 * n]
        send_sems, recv_sems, token = refs[2 * n + 1], refs[2 * n + 2], refs[-1]
        x, y, c = _place()
        me = 2 * x + y
        for a in range(n):
            for j, (px, py) in enumerate(_chip_peers(x, y)):
                pltpu.make_async_remote_copy(
                    src_ref=ins[a].at[2 * px + py] if scatter else ins[a], dst_ref=zones[a].at[me],
                    send_sem=send_sems.at[3 * a + j], recv_sem=recv_sems.at[3 * a + j],
                    device_id=(px, py, c), device_id_type=MESH).start()
        token[...] = jnp.zeros_like(token)

    res = pl.pallas_call(
        body, name=name,
        out_shape=[pltpu.SemaphoreType.DMA((3 * n,)), pltpu.SemaphoreType.DMA((3 * n,))]
        + [pltpu.HBM(a.shape, a.dtype) for a in arrs] + [pltpu.HBM(z.shape, z.dtype) for z in lands]
        + [jax.ShapeDtypeStruct((8, 128), F32)],
        in_specs=[HBM_SPEC] * (2 * n) + [ANY], out_specs=[SEM_SPEC, SEM_SPEC] + [HBM_SPEC] * (2 * n) + [WHOLE_VMEM],
        input_output_aliases={i: 2 + i for i in range(2 * n)},
        compiler_params=pltpu.CompilerParams(has_side_effects=SIDE_EFFECT),
    )(*[_in_hbm(a) for a in arrs], *[_in_hbm(z) for z in lands], after)
    return dict(send=res[0], recv=res[1], src=res[2:2 + n], zones=res[2 + n:2 + 2 * n], token=res[-1], scatter=scatter)


def exchange_wait(handle, name, after):
    srcs, zones, scatter = handle["src"], handle["zones"], handle["scatter"]
    n = len(srcs)

    def body(*refs):
        ins, lands = refs[:n], refs[n:2 * n]
        send_sems, recv_sems = refs[2 * n], refs[2 * n + 1]
        x, y, c = _place()
        me = 2 * x + y
        for a in range(n):
            for j, (px, py) in enumerate(_chip_peers(x, y)):
                cp = pltpu.make_async_remote_copy(
                    src_ref=ins[a].at[me] if scatter else ins[a], dst_ref=lands[a].at[2 * px + py],
                    send_sem=send_sems.at[3 * a + j], recv_sem=recv_sems.at[3 * a + j],
                    device_id=(px, py, c), device_id_type=MESH)
                cp.wait_send()
                cp.wait_recv()

    res = pl.pallas_call(
        body, name=name, out_shape=[pltpu.HBM(a.shape, a.dtype) for a in list(srcs) + list(zones)],
        in_specs=[HBM_SPEC] * (2 * n) + [SEM_SPEC, SEM_SPEC, ANY], out_specs=[HBM_SPEC] * (2 * n),
        input_output_aliases={i: i for i in range(2 * n)},
        compiler_params=pltpu.CompilerParams(has_side_effects=SIDE_EFFECT),
    )(*srcs, *zones, handle["send"], handle["recv"], after)
    return res[n:]


def sibling_exchange(arrs, name):
    n = len(arrs)

    def body(*refs):
        ins, outs = refs[:n], refs[n:2 * n]
        send_sems, recv_sems = refs[2 * n:]
        x, y, c = _place()
        cps = [pltpu.make_async_remote_copy(src_ref=ins[a], dst_ref=outs[a], send_sem=send_sems.at[a],
                                            recv_sem=recv_sems.at[a], device_id=(x, y, 1 - c), device_id_type=MESH)
               for a in range(n)]
        for cp in cps:
            cp.start()
        for cp in cps:
            cp.wait()

    return pl.pallas_call(
        body, name=name, out_shape=[jax.ShapeDtypeStruct(a.shape, a.dtype) for a in arrs],
        in_specs=[ANY] * n, out_specs=[ANY] * n,
        scratch_shapes=[pltpu.SemaphoreType.DMA((n,)), pltpu.SemaphoreType.DMA((n,))],
    )(*arrs)


TILE_BYTES = 2 * 1024 * 1024


def _row_tile(rows, row_bytes):
    if rows * row_bytes <= TILE_BYTES or rows % 8:
        return rows
    best = 8
    for t in range(8, rows + 1, 8):
        if rows % t == 0 and t * row_bytes <= TILE_BYTES:
            best = t
    return best


def pair_add(a, b, name):
    R, C = a.shape
    tr = _row_tile(R, C * 4)

    def body(a_ref, b_ref, o_ref):
        o_ref[...] = (a_ref[...].astype(F32) + b_ref[...].astype(F32)).astype(BF16)

    spec = pl.BlockSpec((tr, C), lambda i: (i, 0))
    return pl.pallas_call(body, name=name, grid=(R // tr,), in_specs=[spec, spec], out_specs=spec,
                          out_shape=jax.ShapeDtypeStruct((R, C), BF16), compiler_params=_cparams(("parallel",)))(a, b)


def sum_lead(parts, name):
    K, R, C = parts.shape
    tr = _row_tile(R, C * 4)

    def body(p_ref, o_ref):
        acc = p_ref[0].astype(F32)
        for j in range(1, K):
            acc = acc + p_ref[j].astype(F32)
        o_ref[...] = acc

    return pl.pallas_call(
        body, name=name, grid=(R // tr,), in_specs=[pl.BlockSpec((K, tr, C), lambda i: (0, i, 0))],
        out_specs=pl.BlockSpec((tr, C), lambda i: (i, 0)), out_shape=jax.ShapeDtypeStruct((R, C), F32),
        compiler_params=_cparams(("parallel",)))(parts)


def adamw(w, g, m, v, name):
    R, C = w.shape
    tr = _row_tile(R, C * 4)

    def body(w_ref, g_ref, m_ref, v_ref, d_ref, mo_ref, vo_ref):
        gv = g_ref[...]
        m2 = ADAM_B1 * m_ref[...] + (1.0 - ADAM_B1) * gv
        v2 = ADAM_B2 * v_ref[...] + (1.0 - ADAM_B2) * (gv * gv)
        m_hat = m2 / (1.0 - ADAM_B1 ** ADAM_STEP)
        v_hat = v2 / (1.0 - ADAM_B2 ** ADAM_STEP)
        d_ref[...] = -ADAM_LR * (m_hat / (jnp.sqrt(v_hat) + ADAM_EPS) + ADAM_WD * w_ref[...])
        mo_ref[...] = m2
        vo_ref[...] = v2

    spec = pl.BlockSpec((tr, C), lambda i: (i, 0))
    return pl.pallas_call(body, name=name, grid=(R // tr,), in_specs=[spec] * 4, out_specs=[spec] * 3,
                          out_shape=[jax.ShapeDtypeStruct((R, C), F32)] * 3, compiler_params=_cparams(("parallel",)))(w, g, m, v)


LANES = 1024


def _pack(arrs, rows):
    out, offs, r = [], [], 0
    for a in arrs:
        flat = a.reshape(-1)
        nr = -(-flat.shape[0] // LANES)
        out.append(jnp.pad(flat, (0, nr * LANES - flat.shape[0])))
        offs.append(r)
        r += nr
    assert r <= rows, (r, rows)
    out.append(jnp.zeros(((rows - r) * LANES,), F32))
    return jnp.concatenate(out).reshape(rows, LANES), offs


def _unpack(packed, offs, shapes):
    flat = packed.reshape(-1)
    return [flat[o * LANES:o * LANES + math.prod(s)].reshape(s) for o, s in zip(offs, shapes)]


WEIGHTS = ["w_ada", "b_ada", "w_in", "b_gate", "conv_a", "a_log", "dt_bias", "norm_a", "rel_bias", "w_branch_a",
           "w_branch_b", "w_o", "ln1_g", "ln1_b", "w_up", "conv_ffn", "b_conv_ffn", "w_down", "ln2_g", "ln2_b"]
BIG = ["w_in", "w_branch_a", "w_branch_b", "w_o", "w_up", "w_down"]
LATE = [n for n in BIG if n != "w_in"]
COL_SHARDED = {"w_in", "w_up"}
SMALL_SHARDED = {"conv_a": 3 * A_W // N_CHIPS, "rel_bias": B_REL // N_CHIPS, "conv_ffn": 2 * D_FF // N_CHIPS}
SMALL = [n for n in WEIGHTS if n not in BIG and n != "w_ada"]


def _to_full(g4, name):
    if name in COL_SHARDED:
        return jnp.transpose(g4, (1, 0, 2)).reshape(g4.shape[1], -1)
    return g4.reshape(-1, g4.shape[2])


def _to_shards(full, name):
    if name in COL_SHARDED:
        return jnp.transpose(full.reshape(full.shape[0], N_CHIPS, -1), (1, 0, 2))
    return full.reshape(N_CHIPS, -1, full.shape[1])


def kernel(x, c, w_ada, b_ada, w_in, b_gate, conv_a, a_log, dt_bias, norm_a, rel_bias, w_branch_a, w_branch_b, w_o, ln1_g, ln1_b, w_up, conv_ffn, b_conv_ffn, w_down, ln2_g, ln2_b, loss_target, m_w_ada, m_b_ada, m_w_in, m_b_gate, m_conv_a, m_a_log, m_dt_bias, m_norm_a, m_rel_bias, m_w_branch_a, m_w_branch_b, m_w_o, m_ln1_g, m_ln1_b, m_w_up, m_conv_ffn, m_b_conv_ffn, m_w_down, m_ln2_g, m_ln2_b, v_w_ada, v_b_ada, v_w_in, v_b_gate, v_conv_a, v_a_log, v_dt_bias, v_norm_a, v_rel_bias, v_w_branch_a, v_w_branch_b, v_w_o, v_ln1_g, v_ln1_b, v_w_up, v_conv_ffn, v_b_conv_ffn, v_w_down, v_ln2_g, v_ln2_b):
    args = dict(locals())
    wts = {n: args[n] for n in WEIGHTS}
    moms = {n: args["m_" + n] for n in WEIGHTS}
    vars_ = {n: args["v_" + n] for n in WEIGHTS}
    xi, yi, ci = _place()
    chip = 2 * xi + yi
    dev = 4 * xi + 2 * yi + ci
    ada_cols = w_ada.shape[2]

    c_all = allgather8(jnp.pad(c, ((0, 7), (0, 0))), "gather_c").reshape(N_DEV, 8, D_MODEL)[:, 0]
    b_ada_sh = lax.dynamic_slice(b_ada, (0, chip * ada_cols), (1, ada_cols))
    mod_sh = ada_fwd(c_all, w_ada[0], b_ada_sh)
    mod_g = allgather8(mod_sh, "gather_mod").reshape(N_CHIPS, 2, N_DEV, ada_cols)[:, 0]
    mod = lax.dynamic_slice(mod_g, (0, dev, 0), (N_CHIPS, 1, ada_cols)).reshape(6, D_MODEL)

    (w_in_g4,) = chip_exchange([wts["w_in"][0].astype(BF16)], "gather_w_in", scatter=False)
    wd = split_w_in(_to_full(w_in_g4, "w_in"))
    late_shards = [wts[n][0].astype(BF16) for n in LATE]
    late_gather = exchange_start(late_shards, "gather_late_start", scatter=False, after=w_in_g4)
    mod = mod + late_gather["token"][0, 0]

    def late_weights(after):
        zones = exchange_wait(late_gather, "gather_late_wait", after)
        full = [_to_full(lax.dynamic_update_slice(z, s[None], (chip, 0, 0)), n) for n, z, s in zip(LATE, zones, late_shards)]
        return {n[2:]: f for n, f in zip(LATE, full)}

    sshapes = [wts[n].shape[1:] for n in SMALL_SHARDED]
    spack, soffs = _pack([wts[n][0] for n in SMALL_SHARDED], 16)
    sg = allgather8(spack, "gather_small_w").reshape(N_CHIPS, 2, 16, LANES)[:, 0]
    sparts = [_unpack(sg[j], soffs, sshapes) for j in range(N_CHIPS)]
    sm = {n: wts[n] for n in SMALL if n not in SMALL_SHARDED and n != "b_ada"}
    for i, n in enumerate(SMALL_SHARDED):
        sm[n] = jnp.concatenate([sparts[j][i] for j in range(N_CHIPS)], axis=-1)

    early = {}

    def early_grads(g):
        mine = [g[n] for n in LATE]
        theirs = sibling_exchange(mine, "grad_sibling_late")
        early["sums"] = [_to_shards(pair_add(a, b, "grad_pair_" + n), n) for n, a, b in zip(LATE, mine, theirs)]
        early["scatter"] = exchange_start(early["sums"], "grad_scatter_start", scatter=True, after=theirs[0])
        return early["scatter"]["token"]

    loss, dxpre2, dffn, fin, sv = forward_local(x[0], loss_target[0], mod, wd, sm, late_weights)
    grad_x, dmod, gw, gs = backward_local(x[0], mod, sm, dxpre2, dffn, fin, sv, early_grads)

    gnames = [n for n in SMALL if n != "b_ada"]
    vec, voffs = _pack([dmod] + [gs[n] for n in gnames] + [loss], 56)
    gathered = allgather8(vec, "gather_small_g").reshape(N_DEV, 56, LANES)
    summed = sum_lead(gathered, "sum_small_g")
    full_shapes = [(6, D_MODEL)] + [gs[n].shape for n in gnames] + [(1, 1)]
    parts = _unpack(summed, voffs, full_shapes)
    grads = {"b_ada": parts[0].reshape(1, -1)}
    for n, p in zip(gnames, parts[1:-1]):
        if n in SMALL_SHARDED:
            p = lax.dynamic_slice_in_dim(p, chip * SMALL_SHARDED[n], SMALL_SHARDED[n], axis=1)
        grads[n] = p.reshape(wts[n].shape)
    loss_total = parts[-1].reshape(())
    dmod_all = gathered[:, 0:6, :].reshape(N_DEV, 6 * D_MODEL)
    grads["w_ada"] = ada_bwd(c_all, lax.dynamic_slice(dmod_all, (0, chip * ada_cols), (N_DEV, ada_cols)))[None]

    (theirs_in,) = sibling_exchange([gw["w_in"]], "grad_sibling_w_in")
    sum_in = _to_shards(pair_add(gw["w_in"], theirs_in, "grad_pair_w_in"), "w_in")
    (rec_in,) = chip_exchange([sum_in], "grad_scatter_w_in", scatter=True)
    grads["w_in"] = sum_lead(rec_in, "grad_sum_w_in")[None]
    zones = exchange_wait(early["scatter"], "grad_scatter_wait", rec_in)
    for n, z, s in zip(LATE, zones, early["sums"]):
        own = lax.dynamic_slice_in_dim(s, chip, 1, axis=0)
        grads[n] = sum_lead(lax.dynamic_update_slice(z, own, (chip, 0, 0)), "grad_sum_" + n)[None]

    delta, new_m, new_v = {}, {}, {}
    for n in ["w_ada"] + BIG:
        d, m2, v2 = adamw(wts[n][0], grads[n][0], moms[n][0], vars_[n][0], "adamw_" + n)
        delta[n], new_m[n], new_v[n] = d[None], m2[None], v2[None]
    shapes = [wts[n].shape for n in SMALL]
    packs = [_pack([t[n] for n in SMALL], 32) for t in (wts, grads, moms, vars_)]
    outs = adamw(*[p[0] for p in packs], "adamw_small")
    for res, o in zip((delta, new_m, new_v), outs):
        for n, a in zip(SMALL, _unpack(o, packs[0][1], shapes)):
            res[n] = a
    return (loss_total, grad_x[None], *[grads[n] for n in WEIGHTS], *[delta[n] for n in WEIGHTS],
            *[new_m[n] for n in WEIGHTS], *[new_v[n] for n in WEIGHTS])
```

```python
import functools
import math

import jax
import jax.numpy as jnp
from jax import lax
from jax.experimental import pallas as pl
from jax.experimental.pallas import tpu as pltpu

F32 = jnp.float32
BF16 = jnp.bfloat16

D_MODEL = 1024
CHUNK = 64
A_HEADS = 8
A_DK = 128
A_W = A_HEADS * A_DK
A_CONV = 4
B_HEADS = 16
B_DH = 64
B_W = B_HEADS * B_DH
B_PREV = 8
B_BAND = (B_PREV + 1) * CHUNK
B_MAX_REL = 256
B_REL = CHUNK - 1 + B_MAX_REL + 1
D_FF = 2816
FFN_CONV = 3
IN_COLS = 4 * A_W + 2 * A_HEADS + 3 * B_W + 2 * D_MODEL
ALPHA = 2.0 ** 0.25
LN_EPS = 1e-5
RMS_EPS = 1e-6
L2_EPS = 1e-6
NEG_INF = -1e30
ADAM_LR, ADAM_B1, ADAM_B2, ADAM_EPS, ADAM_WD, ADAM_STEP = 0.001, 0.9, 0.999, 1e-08, 0.01, 10
N_CHIPS = 4
N_DEV = 8
VMEM_LIMIT = 56 * 1024 * 1024


def _cparams(sem=None):
    return pltpu.CompilerParams(dimension_semantics=sem, vmem_limit_bytes=VMEM_LIMIT)


_DIMS = {"nn": (((1,), (0,)), ((), ())), "nt": (((1,), (1,)), ((), ())), "tn": (((0,), (0,)), ((), ()))}


MM_TILE_CAP = 1408


def _mm_tile(n):
    return max(t for t in range(128, min(n, MM_TILE_CAP) + 1, 128) if n % t == 0)


def mm(a, b, *, mode, out_dtype, name, acc_in=None):
    if mode == "nn":
        (M, K), (K2, N) = a.shape, b.shape
    elif mode == "nt":
        (M, K), (N, K2) = a.shape, b.shape
    else:
        (K, M), (K2, N) = a.shape, b.shape
    assert K == K2, (a.shape, b.shape, mode)
    tm, tn, tk = _mm_tile(M), _mm_tile(N), _mm_tile(K)
    nk = K // tk

    def body(*refs):
        if acc_in is None:
            a_ref, b_ref, o_ref, acc_ref = refs
        else:
            a_ref, b_ref, c_ref, o_ref, acc_ref = refs
        k = pl.program_id(2)

        @pl.when(k == 0)
        def _():
            if acc_in is None:
                acc_ref[...] = jnp.zeros_like(acc_ref)
            else:
                acc_ref[...] = c_ref[...]

        acc_ref[...] += lax.dot_general(a_ref[...].astype(BF16), b_ref[...].astype(BF16), _DIMS[mode],
                                        preferred_element_type=F32)

        @pl.when(k == nk - 1)
        def _():
            o_ref[...] = acc_ref[...].astype(out_dtype)

    a_spec = pl.BlockSpec((tk, tm), lambda i, j, k: (k, i)) if mode == "tn" else pl.BlockSpec((tm, tk), lambda i, j, k: (i, k))
    b_spec = pl.BlockSpec((tn, tk), lambda i, j, k: (j, k)) if mode == "nt" else pl.BlockSpec((tk, tn), lambda i, j, k: (k, j))
    o_spec = pl.BlockSpec((tm, tn), lambda i, j, k: (i, j))
    ins, in_specs, aliases = [a, b], [a_spec, b_spec], {}
    if acc_in is not None:
        assert acc_in.shape == (M, N) and acc_in.dtype == F32 and out_dtype == F32
        ins.append(acc_in)
        in_specs.append(o_spec)
        aliases = {2: 0}
    return pl.pallas_call(
        body, name=name, grid=(M // tm, N // tn, nk), in_specs=in_specs, out_specs=o_spec,
        out_shape=jax.ShapeDtypeStruct((M, N), out_dtype), scratch_shapes=[pltpu.VMEM((tm, tn), F32)],
        input_output_aliases=aliases, compiler_params=_cparams(("parallel", "parallel", "arbitrary")),
    )(*ins)


def rowcall(body, *, name, S, ts, ins, outs):
    assert S % ts == 0 and ts % 16 == 0
    nsteps = S // ts
    in_specs, arrays = [], []
    for arr, kind in ins:
        arrays.append(arr)
        if kind == "row":
            in_specs.append(pl.BlockSpec((ts, arr.shape[1]), lambda i: (i, 0)))
        elif kind in ("prev", "next"):
            hr = 8 * (4 // arr.dtype.itemsize)
            per, last = ts // hr, S // hr - 1
            if kind == "prev":
                in_specs.append(pl.BlockSpec((hr, arr.shape[1]), lambda i, per=per: (jnp.maximum(i * per - 1, 0), 0)))
            else:
                in_specs.append(pl.BlockSpec((hr, arr.shape[1]), lambda i, per=per, last=last: (jnp.minimum((i + 1) * per, last), 0)))
        else:
            nd = arr.ndim
            in_specs.append(pl.BlockSpec(arr.shape, lambda i, nd=nd: (0,) * nd))
    out_specs, out_shapes, acc_idx = [], [], []
    for n, (shape, dtype, kind) in enumerate(outs):
        out_shapes.append(jax.ShapeDtypeStruct(shape, dtype))
        if kind == "row":
            out_specs.append(pl.BlockSpec((ts, shape[1]), lambda i: (i, 0)))
        else:
            nd = len(shape)
            out_specs.append(pl.BlockSpec(shape, lambda i, nd=nd: (0,) * nd))
            acc_idx.append(n)
    n_in = len(arrays)

    def wrapped(*refs):
        @pl.when(pl.program_id(0) == 0)
        def _():
            for n in acc_idx:
                refs[n_in + n][...] = jnp.zeros_like(refs[n_in + n])

        body(*refs)

    res = pl.pallas_call(
        wrapped, name=name, grid=(nsteps,), in_specs=in_specs, out_specs=out_specs, out_shape=out_shapes,
        compiler_params=_cparams(("arbitrary",) if acc_idx else ("parallel",)),
    )(*arrays)
    return res


def _halo_prev(ref):
    v = ref[...].astype(F32)
    return v[v.shape[0] - 8:]


def _halo_next(ref):
    return ref[...].astype(F32)[:8]


def _shift_down(cur, prev8, k):
    if k == 0:
        return cur
    rolled = pltpu.roll(cur, k, axis=0)
    fix = pltpu.roll(prev8, k, axis=0)
    row = lax.broadcasted_iota(jnp.int32, (8, 1), 0)
    top = jnp.where(row < k, fix, rolled[0:8])
    if cur.shape[0] == 8:
        return top
    return jnp.concatenate([top, rolled[8:]], axis=0)


def _shift_up(cur, next8, k):
    if k == 0:
        return cur
    n = cur.shape[0]
    rolled = pltpu.roll(cur, n - k, axis=0)
    fix = pltpu.roll(next8, 8 - k, axis=0)
    row = lax.broadcasted_iota(jnp.int32, (8, 1), 0)
    bot = jnp.where(row >= 8 - k, fix, rolled[n - 8:n])
    return jnp.concatenate([rolled[:n - 8], bot], axis=0)


def _sigmoid(x):
    return 1.0 / (1.0 + jnp.exp(-x))


def _silu(x):
    return x * _sigmoid(x)


def _silu_and_grad(x):
    s = _sigmoid(x)
    return x * s, s * (1.0 + x * (1.0 - s))


def _softplus(x):
    return jnp.maximum(x, 0.0) + jnp.log1p(jnp.exp(-jnp.abs(x)))


def _split2(x):
    hi = x.astype(BF16)
    return hi, (x - hi.astype(F32)).astype(BF16)


def _dot1(a, b, mode):
    return lax.dot_general(a.astype(BF16), b.astype(BF16), _DIMS[mode], preferred_element_type=F32)


def _dot3(a, b, mode):
    ah, al = _split2(a)
    bh, bl = _split2(b)
    d = lambda p, q: lax.dot_general(p, q, _DIMS[mode], preferred_element_type=F32)
    return d(ah, bh) + (d(ah, bl) + d(al, bh))


def ada_fwd(c_all, w_sh, b_sh):
    n = w_sh.shape[1]
    tn = 512

    def body(c_ref, w_ref, b_ref, o_ref):
        o_ref[...] = _dot1(_silu(c_ref[...]), w_ref[...], "nn") + b_ref[...]

    return pl.pallas_call(
        body, name="ada_fwd", grid=(n // tn,),
        in_specs=[pl.BlockSpec((N_DEV, D_MODEL), lambda j: (0, 0)), pl.BlockSpec((D_MODEL, tn), lambda j: (0, j)),
                  pl.BlockSpec((1, tn), lambda j: (0, j))],
        out_specs=pl.BlockSpec((N_DEV, tn), lambda j: (0, j)), out_shape=jax.ShapeDtypeStruct((N_DEV, n), F32),
        compiler_params=_cparams(("parallel",)),
    )(c_all, w_sh, b_sh)


def ada_bwd(c_all, dmod_sh):
    n = dmod_sh.shape[1]
    tn = 512

    def body(c_ref, d_ref, o_ref):
        o_ref[...] = _dot1(_silu(c_ref[...]), d_ref[...], "tn")

    return pl.pallas_call(
        body, name="ada_bwd", grid=(n // tn,),
        in_specs=[pl.BlockSpec((N_DEV, D_MODEL), lambda j: (0, 0)), pl.BlockSpec((N_DEV, tn), lambda j: (0, j))],
        out_specs=pl.BlockSpec((D_MODEL, tn), lambda j: (0, j)), out_shape=jax.ShapeDtypeStruct((D_MODEL, n), F32),
        compiler_params=_cparams(("parallel",)),
    )(c_all, dmod_sh)


SHIFT_T, SCALE_T, GATE_T, SHIFT_F, SCALE_F, GATE_F = range(6)


def modulate(x, mod, shift_row, scale_row, name):
    S = x.shape[0]

    def body(x_ref, m_ref, o_ref):
        m = m_ref[...]
        o_ref[...] = (x_ref[...] * (1.0 + m[scale_row:scale_row + 1]) + m[shift_row:shift_row + 1]).astype(BF16)

    return rowcall(body, name=name, S=S, ts=512, ins=[(x, "row"), (mod, "vec")], outs=[((S, D_MODEL), BF16, "row")])[0]


def _conv_fwd(cur, prev, w, width):
    y = cur * w[width - 1:width]
    for j in range(width - 1):
        y = y + _shift_down(cur, prev, width - 1 - j) * w[j:j + 1]
    return y


def _prep_a_core(cur, prev, w):
    return _silu_and_grad(_conv_fwd(cur, prev, w, A_CONV))


def prep_a_fwd(qkv_raw, ba, conv_a, a_log, dt_bias):
    S = qkv_raw.shape[0]

    def body(x_ref, xp_ref, ba_ref, w_ref, al_ref, dt_ref, q_ref, k_ref, v_ref, beta_ref, g_ref):
        first = (pl.program_id(0) > 0).astype(F32)
        y, _ = _prep_a_core(x_ref[...].astype(F32), _halo_prev(xp_ref) * first, w_ref[...])
        for h in range(A_HEADS):
            sl = slice(h * A_DK, (h + 1) * A_DK)
            qh = y[:, sl]
            kh = y[:, A_W + h * A_DK:A_W + (h + 1) * A_DK]
            q_ref[:, sl] = qh * (lax.rsqrt(jnp.sum(qh * qh, axis=-1, keepdims=True) + L2_EPS) * (A_DK ** -0.5))
            k_ref[:, sl] = kh * lax.rsqrt(jnp.sum(kh * kh, axis=-1, keepdims=True) + L2_EPS)
        v_ref[...] = y[:, 2 * A_W:3 * A_W]
        bav = ba_ref[...]
        beta_ref[...] = _sigmoid(bav[:, 0:A_HEADS])
        g_ref[...] = -jnp.exp(al_ref[...]) * _softplus(bav[:, A_HEADS:2 * A_HEADS] + dt_ref[...])

    return rowcall(
        body, name="prep_a_fwd", S=S, ts=256,
        ins=[(qkv_raw, "row"), (qkv_raw, "prev"), (ba, "row"), (conv_a, "vec"), (a_log, "vec"), (dt_bias, "vec")],
        outs=[((S, A_W), F32, "row")] * 3 + [((S, A_HEADS), F32, "row")] * 2)


HEAD_GROUP = 4
GROUP_ROWS = HEAD_GROUP * CHUNK
N_HEAD_GROUPS = A_HEADS // HEAD_GROUP
LOG_CHUNK = int(math.log2(CHUNK))


def _tri_masks():
    rb = lax.broadcasted_iota(jnp.int32, (GROUP_ROWS, GROUP_ROWS), 0)
    cb = lax.broadcasted_iota(jnp.int32, (GROUP_ROWS, GROUP_ROWS), 1)
    same = (rb >> LOG_CHUNK) == (cb >> LOG_CHUNK)
    return dict(causal=same & (rb >= cb), strict=same & (rb > cb), eye=rb == cb, upper=same & (cb >= rb),
                last=cb == (rb | (CHUNK - 1)), rb=rb, cb=cb)


def _col_to_row(colv, eye):
    return jnp.sum(jnp.where(eye, colv, 0.0), axis=0, keepdims=True)


def _row_to_col(rowv, eye):
    return jnp.sum(jnp.where(eye, rowv, 0.0), axis=1, keepdims=True)


def _tri_inv(a_list, mk):
    rb, cb = mk["rb"], mk["cb"]
    ts = [jnp.where(mk["eye"], 1.0, 0.0) - jnp.where((rb >> 1) == (cb >> 1), a, 0.0) for a in a_list]
    for lvl in range(1, LOG_CHUNK):
        rs, cs = rb >> lvl, cb >> lvl
        sel = ((rs & 1) == 1) & (cs == rs - 1)
        inner = [_dot3(t, jnp.where(sel, a, 0.0), "nn") for t, a in zip(ts, a_list)]
        ts = [t - _dot3(i, t, "nn") for i, t in zip(inner, ts)]
    return ts


def _stack_heads(ref, grp):
    return jnp.concatenate([ref[:, (grp * HEAD_GROUP + j) * A_DK:(grp * HEAD_GROUP + j + 1) * A_DK]
                            for j in range(HEAD_GROUP)], axis=0)


def _stack_cols(tile, grp):
    return jnp.concatenate([tile[:, grp * HEAD_GROUP + j:grp * HEAD_GROUP + j + 1] for j in range(HEAD_GROUP)], axis=0)


def _delta_local(q, k, v, beta, g, mk):
    causal, strict, eye = mk["causal"], mk["strict"], mk["eye"]
    g_row = _col_to_row(g, eye)
    gc = jnp.sum(jnp.where(causal, g_row, 0.0), axis=1, keepdims=True)
    gc_row = _col_to_row(gc, eye)
    decay = jnp.where(causal, jnp.exp(jnp.where(causal, gc - gc_row, 0.0)), 0.0)
    gam = jnp.exp(gc)
    kb = k * beta
    vb = v * beta
    y = kb * gam
    a = jnp.where(strict, _dot1(kb, k, "nt") * decay, 0.0)
    p = _dot1(q, k, "nt") * decay
    gl = jnp.sum(jnp.where(mk["last"], gc_row, 0.0), axis=1, keepdims=True)
    kd = k * jnp.exp(gl - gc)
    return dict(gc=gc, decay=decay, gam=gam, kb=kb, vb=vb, y=y, a=a, p=p, gl=gl, kd=kd)


def _head_rows(x, j):
    return x[j * CHUNK:(j + 1) * CHUNK]


def delta_fwd(q, k, v, beta, g):
    S = q.shape[0]
    n_chunks = S // CHUNK

    def body(q_ref, k_ref, v_ref, beta_ref, g_ref, o_ref, sprev_ref, t_ref, state_ref):
        @pl.when(pl.program_id(0) == 0)
        def _():
            state_ref[...] = jnp.zeros_like(state_ref)

        mk = _tri_masks()
        betav, gv = beta_ref[...], g_ref[...]
        groups = range(N_HEAD_GROUPS)
        q_all = [_stack_heads(q_ref, grp) for grp in groups]
        locs = [_delta_local(q_all[grp], _stack_heads(k_ref, grp), _stack_heads(v_ref, grp),
                             _stack_cols(betav, grp), _stack_cols(gv, grp), mk) for grp in groups]
        tinvs = _tri_inv([loc["a"] for loc in locs], mk)
        uws = [_dot3(tinvs[grp], jnp.concatenate([locs[grp]["vb"], locs[grp]["y"]], axis=1), "nn") for grp in groups]
        for grp in groups:
            loc, uw = locs[grp], uws[grp]
            t_ref[0, grp] = tinvs[grp]
            qg = q_all[grp] * loc["gam"]
            egl = jnp.exp(loc["gl"])
            vns, o_state = [], []
            for j in range(HEAD_GROUP):
                h = grp * HEAD_GROUP + j
                s0 = state_ref[h]
                sprev_ref[0, h] = s0
                uw_h = _head_rows(uw, j)
                vn = uw_h[:, :A_DK] - _dot1(uw_h[:, A_DK:], s0, "nn")
                vns.append(vn)
                o_state.append(_dot1(_head_rows(qg, j), s0, "nn"))
                state_ref[h] = s0 * egl[(j + 1) * CHUNK - 1:(j + 1) * CHUNK] + _dot1(_head_rows(loc["kd"], j), vn, "tn")
            o_local = _dot1(loc["p"], jnp.concatenate(vns, axis=0), "nn")
            for j in range(HEAD_GROUP):
                h = grp * HEAD_GROUP + j
                o_ref[:, h * A_DK:(h + 1) * A_DK] = o_state[j] + _head_rows(o_local, j)

    tile = pl.BlockSpec((CHUNK, A_W), lambda n: (n, 0))
    small = pl.BlockSpec((CHUNK, A_HEADS), lambda n: (n, 0))
    return pl.pallas_call(
        body, name="delta_fwd", grid=(n_chunks,), in_specs=[tile, tile, tile, small, small],
        out_specs=[tile, pl.BlockSpec((1, A_HEADS, A_DK, A_DK), lambda n: (n, 0, 0, 0)),
                   pl.BlockSpec((1, N_HEAD_GROUPS, GROUP_ROWS, GROUP_ROWS), lambda n: (n, 0, 0, 0))],
        out_shape=[jax.ShapeDtypeStruct((S, A_W), F32), jax.ShapeDtypeStruct((n_chunks, A_HEADS, A_DK, A_DK), F32),
                   jax.ShapeDtypeStruct((n_chunks, N_HEAD_GROUPS, GROUP_ROWS, GROUP_ROWS), F32)],
        scratch_shapes=[pltpu.VMEM((A_HEADS, A_DK, A_DK), F32)],
        compiler_params=_cparams(("arbitrary",)),
    )(q, k, v, beta, g)


def gate_a_fwd(o_pre, z, norm_w):
    S = o_pre.shape[0]

    def body(o_ref, z_ref, nw_ref, out_ref):
        nw = nw_ref[...]
        for h in range(A_HEADS):
            sl = slice(h * A_DK, (h + 1) * A_DK)
            oh = o_ref[:, sl]
            r = lax.rsqrt(jnp.mean(oh * oh, axis=-1, keepdims=True) + RMS_EPS)
            out_ref[:, sl] = (oh * r * nw * _silu(z_ref[:, sl])).astype(BF16)

    return rowcall(body, name="gate_a_fwd", S=S, ts=512, ins=[(o_pre, "row"), (z, "row"), (norm_w, "vec")],
                   outs=[((S, A_W), BF16, "row")])[0]


HEADS_PER_GROUP = 2
GROUP_W = HEADS_PER_GROUP * B_DH
N_GROUPS = B_HEADS // HEADS_PER_GROUP
PAD_ROWS = B_PREV * CHUNK


Q_TILE = 256
Q_CHUNKS = Q_TILE // CHUNK
KEY_WIN = (B_PREV + Q_CHUNKS) * CHUNK


def _band_probs(qh, kh, bias, valid):
    s = _dot1(qh, kh, "nt") * (B_DH ** -0.5) + bias
    s = jnp.where(valid, s, NEG_INF)
    e = jnp.exp(s - jnp.max(s, axis=-1, keepdims=True))
    return e * (1.0 / jnp.sum(e, axis=-1, keepdims=True))


def _attn_specs(S, tile_rows):
    n_cb = B_W // GROUP_W
    return [pl.BlockSpec((tile_rows, GROUP_W), lambda g, n: (n + PAD_ROWS // tile_rows, g)),
            pl.BlockSpec((PAD_ROWS + S, GROUP_W), lambda g, n: (0, n_cb + g)),
            pl.BlockSpec((PAD_ROWS + S, GROUP_W), lambda g, n: (0, 2 * n_cb + g)),
            pl.BlockSpec((HEADS_PER_GROUP, CHUNK, B_BAND), lambda g, n: (g, 0, 0))]


def _band_valid(first_chunk):
    return lax.broadcasted_iota(jnp.int32, (CHUNK, B_BAND), 1) >= PAD_ROWS - first_chunk * CHUNK


def _chunk_rows(x, qc, rows=CHUNK):
    return x[qc * CHUNK:qc * CHUNK + rows]


FWD_TILE = 512
FWD_CHUNKS = FWD_TILE // CHUNK
FWD_WIN = (B_PREV + FWD_CHUNKS) * CHUNK


def attn_fwd(qkv_pad, bias):
    S = qkv_pad.shape[0] - PAD_ROWS

    def body(q_ref, k_ref, v_ref, b_ref, o_ref):
        n = pl.program_id(1)
        start = pl.multiple_of(n * FWD_TILE, FWD_TILE)
        kwin = k_ref[pl.ds(start, FWD_WIN), :]
        vwin = v_ref[pl.ds(start, FWD_WIN), :]
        qv = q_ref[...]
        pairs = [(qc, hh) for qc in range(FWD_CHUNKS) for hh in range(HEADS_PER_GROUP)]
        sl = lambda hh: slice(hh * B_DH, (hh + 1) * B_DH)
        s = [_dot1(_chunk_rows(qv, qc)[:, sl(hh)], _chunk_rows(kwin, qc, B_BAND)[:, sl(hh)], "nt") for qc, hh in pairs]
        s = [jnp.where(_band_valid(n * FWD_CHUNKS + qc), x * (B_DH ** -0.5) + b_ref[hh], NEG_INF)
             for x, (qc, hh) in zip(s, pairs)]
        e = [jnp.exp(x - jnp.max(x, axis=-1, keepdims=True)) for x in s]
        p = [x * (1.0 / jnp.sum(x, axis=-1, keepdims=True)) for x in e]
        o = [_dot1(x, _chunk_rows(vwin, qc, B_BAND)[:, sl(hh)], "nn") for x, (qc, hh) in zip(p, pairs)]
        rows = [jnp.concatenate(o[qc * HEADS_PER_GROUP:(qc + 1) * HEADS_PER_GROUP], axis=1) for qc in range(FWD_CHUNKS)]
        o_ref[...] = jnp.concatenate(rows, axis=0).astype(BF16)

    return pl.pallas_call(
        body, name="attn_fwd", grid=(N_GROUPS, S // FWD_TILE), in_specs=_attn_specs(S, FWD_TILE),
        out_specs=pl.BlockSpec((FWD_TILE, GROUP_W), lambda g, n: (n, g)),
        out_shape=jax.ShapeDtypeStruct((S, B_W), BF16),
        compiler_params=_cparams(("parallel", "arbitrary")),
    )(qkv_pad, qkv_pad, qkv_pad, bias)


def _rel_onehot(i):
    kj = lax.broadcasted_iota(jnp.int32, (B_BAND, B_REL), 0)
    r = lax.broadcasted_iota(jnp.int32, (B_BAND, B_REL), 1)
    idx = jnp.clip(PAD_ROWS + i - kj, -(CHUNK - 1), B_MAX_REL) + (CHUNK - 1)
    return jnp.where(idx == r, 1.0, 0.0)


def bias_expand(rel_bias):
    def body(rb_ref, o_ref):
        i = pl.program_id(0)
        o_ref[0] = _dot3(rb_ref[...], _rel_onehot(i), "nt")

    return pl.pallas_call(
        body, name="bias_expand", grid=(CHUNK,),
        in_specs=[pl.BlockSpec((B_HEADS, B_REL), lambda i: (0, 0))],
        out_specs=pl.BlockSpec((1, B_HEADS, B_BAND), lambda i: (i, 0, 0)),
        out_shape=jax.ShapeDtypeStruct((CHUNK, B_HEADS, B_BAND), F32),
        compiler_params=_cparams(("parallel",)),
    )(rel_bias)


def bias_reduce(dbias):
    def body(d_ref, o_ref):
        i = pl.program_id(0)

        @pl.when(i == 0)
        def _():
            o_ref[...] = jnp.zeros_like(o_ref)

        o_ref[...] += _dot3(d_ref[0], _rel_onehot(i), "nn")

    return pl.pallas_call(
        body, name="bias_reduce", grid=(CHUNK,),
        in_specs=[pl.BlockSpec((1, B_HEADS, B_BAND), lambda i: (i, 0, 0))],
        out_specs=pl.BlockSpec((B_HEADS, B_REL), lambda i: (0, 0)),
        out_shape=jax.ShapeDtypeStruct((B_HEADS, B_REL), F32),
        compiler_params=_cparams(("arbitrary",)),
    )(dbias)


def merge_fwd(gates_raw, b_gate, ya, yb):
    S = ya.shape[0]

    def body(g_ref, b_ref, ya_ref, yb_ref, o_ref):
        gt = _sigmoid(g_ref[...] + b_ref[...])
        o_ref[...] = (gt[:, :D_MODEL] * ya_ref[...] + gt[:, D_MODEL:] * yb_ref[...]).astype(BF16)

    return rowcall(body, name="merge_fwd", S=S, ts=512,
                   ins=[(gates_raw, "row"), (b_gate, "vec"), (ya, "row"), (yb, "row")],
                   outs=[((S, D_MODEL), BF16, "row")])[0]


def _ln_stats(xpre):
    mu = jnp.mean(xpre, axis=-1, keepdims=True)
    xc = xpre - mu
    rstd = lax.rsqrt(jnp.mean(xc * xc, axis=-1, keepdims=True) + LN_EPS)
    return xc * rstd, rstd


def ln1_fwd(x, mix, mod, ln_g, ln_b):
    S = x.shape[0]

    def body(x_ref, mix_ref, m_ref, g_ref, b_ref, xpre_ref, x1_ref, h2_ref):
        m = m_ref[...]
        xpre = ALPHA * x_ref[...] + m[GATE_T:GATE_T + 1] * mix_ref[...]
        xhat, _ = _ln_stats(xpre)
        x1 = xhat * g_ref[...] + b_ref[...]
        xpre_ref[...] = xpre
        x1_ref[...] = x1
        h2_ref[...] = (x1 * (1.0 + m[SCALE_F:SCALE_F + 1]) + m[SHIFT_F:SHIFT_F + 1]).astype(BF16)

    return rowcall(body, name="ln1_fwd", S=S, ts=512,
                   ins=[(x, "row"), (mix, "row"), (mod, "vec"), (ln_g, "vec"), (ln_b, "vec")],
                   outs=[((S, D_MODEL), F32, "row"), ((S, D_MODEL), F32, "row"), ((S, D_MODEL), BF16, "row")])


def ffn_act_fwd(up, conv_w, conv_b):
    S = up.shape[0]

    def body(u_ref, up_ref, w_ref, b_ref, o_ref):
        first = (pl.program_id(0) > 0).astype(F32)
        uc = _conv_fwd(u_ref[...].astype(F32), _halo_prev(up_ref) * first, w_ref[...], FFN_CONV) + b_ref[...]
        o_ref[...] = (_silu(uc[:, :D_FF]) * uc[:, D_FF:]).astype(BF16)

    return rowcall(body, name="ffn_act_fwd", S=S, ts=128,
                   ins=[(up, "row"), (up, "prev"), (conv_w, "vec"), (conv_b, "vec")],
                   outs=[((S, D_FF), BF16, "row")])[0]


def final_fwd_bwd(x1, ffn, target, mod, ln_g, ln_b):
    S = x1.shape[0]

    def body(x1_ref, f_ref, t_ref, m_ref, g_ref, b_ref, dxpre_ref, dffn_ref, loss_ref, dgate_ref, dg_ref, db_ref):
        gate = m_ref[...][GATE_F:GATE_F + 1]
        ffn_v = f_ref[...]
        xpre = ALPHA * x1_ref[...] + gate * ffn_v
        xhat, rstd = _ln_stats(xpre)
        err = xhat * g_ref[...] + b_ref[...] - t_ref[...]
        loss_ref[...] += 0.5 * jnp.sum(jnp.mean(err * err, axis=-1, keepdims=True), axis=0, keepdims=True)
        dy = err * (1.0 / D_MODEL)
        dg_ref[...] += jnp.sum(dy * xhat, axis=0, keepdims=True)
        db_ref[...] += jnp.sum(dy, axis=0, keepdims=True)
        dyg = dy * g_ref[...]
        dxpre = rstd * (dyg - jnp.mean(dyg, axis=-1, keepdims=True) - xhat * jnp.mean(dyg * xhat, axis=-1, keepdims=True))
        dxpre_ref[...] = dxpre
        dffn_ref[...] = (gate * dxpre).astype(BF16)
        dgate_ref[...] += jnp.sum(dxpre * ffn_v, axis=0, keepdims=True)

    vec = ((1, D_MODEL), F32, "acc")
    return rowcall(body, name="final_fwd_bwd", S=S, ts=512,
                   ins=[(x1, "row"), (ffn, "row"), (target, "row"), (mod, "vec"), (ln_g, "vec"), (ln_b, "vec")],
                   outs=[((S, D_MODEL), F32, "row"), ((S, D_MODEL), BF16, "row"), ((1, 1), F32, "acc"), vec, vec, vec])


def _ffn_duc(dact, uc):
    ug, uv = uc[:, :D_FF], uc[:, D_FF:]
    sg, dsg = _silu_and_grad(ug)
    return jnp.concatenate([dact * uv * dsg, dact * sg], axis=1)


def ffn_act_bwd(dact, up, conv_w, conv_b):
    S = up.shape[0]
    ts = 128

    def body(d_ref, dn_ref, u_ref, up_ref, un_ref, w_ref, b_ref, dup_ref, dw_ref, db_ref):
        i = pl.program_id(0)
        first = (i > 0).astype(F32)
        last = (i < pl.num_programs(0) - 1).astype(F32)
        w, b = w_ref[...], b_ref[...]
        cur, prev = u_ref[...].astype(F32), _halo_prev(up_ref) * first
        shifted = [_shift_down(cur, prev, FFN_CONV - 1 - j) for j in range(FFN_CONV)]
        uc = b + sum(shifted[j] * w[j:j + 1] for j in range(FFN_CONV))
        duc = _ffn_duc(d_ref[...].astype(F32), uc)
        uc_n = _conv_fwd(_halo_next(un_ref), cur[ts - 8:ts], w, FFN_CONV) + b
        duc_n = _ffn_duc(_halo_next(dn_ref), uc_n) * last
        db_ref[...] += jnp.sum(duc, axis=0, keepdims=True)
        for j in range(FFN_CONV):
            dw_ref[j:j + 1, :] += jnp.sum(duc * shifted[j], axis=0, keepdims=True)
        dup = duc * w[FFN_CONV - 1:FFN_CONV]
        for j in range(FFN_CONV - 1):
            dup = dup + _shift_up(duc, duc_n, FFN_CONV - 1 - j) * w[j:j + 1]
        dup_ref[...] = dup.astype(BF16)

    return rowcall(body, name="ffn_act_bwd", S=S, ts=ts,
                   ins=[(dact, "row"), (dact, "next"), (up, "row"), (up, "prev"), (up, "next"), (conv_w, "vec"), (conv_b, "vec")],
                   outs=[((S, 2 * D_FF), BF16, "row"), ((FFN_CONV, 2 * D_FF), F32, "acc"), ((1, 2 * D_FF), F32, "acc")])


def ln1_bwd(dxpre2, dh2, xpre1, mix, mod, ln_g, ln_b):
    S = xpre1.shape[0]

    def body(d2_ref, dh_ref, xp_ref, mix_ref, m_ref, g_ref, b_ref, dxpre_ref, dmix_ref,
             dscale_ref, dshift_ref, dgate_ref, dg_ref, db_ref):
        m = m_ref[...]
        xhat, rstd = _ln_stats(xp_ref[...])
        x1 = xhat * g_ref[...] + b_ref[...]
        dh = dh_ref[...]
        dx1 = ALPHA * d2_ref[...] + dh * (1.0 + m[SCALE_F:SCALE_F + 1])
        dscale_ref[...] += jnp.sum(dh * x1, axis=0, keepdims=True)
        dshift_ref[...] += jnp.sum(dh, axis=0, keepdims=True)
        dg_ref[...] += jnp.sum(dx1 * xhat, axis=0, keepdims=True)
        db_ref[...] += jnp.sum(dx1, axis=0, keepdims=True)
        dyg = dx1 * g_ref[...]
        dxpre = rstd * (dyg - jnp.mean(dyg, axis=-1, keepdims=True) - xhat * jnp.mean(dyg * xhat, axis=-1, keepdims=True))
        dxpre_ref[...] = dxpre
        dmix_ref[...] = (m[GATE_T:GATE_T + 1] * dxpre).astype(BF16)
        dgate_ref[...] += jnp.sum(dxpre * mix_ref[...], axis=0, keepdims=True)

    vec = ((1, D_MODEL), F32, "acc")
    return rowcall(body, name="ln1_bwd", S=S, ts=512,
                   ins=[(dxpre2, "row"), (dh2, "row"), (xpre1, "row"), (mix, "row"), (mod, "vec"), (ln_g, "vec"), (ln_b, "vec")],
                   outs=[((S, D_MODEL), F32, "row"), ((S, D_MODEL), BF16, "row"), vec, vec, vec, vec, vec])


def merge_bwd(dmerged, gates_raw, b_gate, ya, yb):
    S = ya.shape[0]

    def body(d_ref, g_ref, b_ref, ya_ref, yb_ref, dya_ref, dyb_ref, dg_ref, dbg_ref):
        gt = _sigmoid(g_ref[...] + b_ref[...])
        d = d_ref[...]
        ga, gb = gt[:, :D_MODEL], gt[:, D_MODEL:]
        dya_ref[...] = (d * ga).astype(BF16)
        dyb_ref[...] = (d * gb).astype(BF16)
        dgr = jnp.concatenate([d * ya_ref[...] * ga * (1.0 - ga), d * yb_ref[...] * gb * (1.0 - gb)], axis=1)
        dg_ref[...] = dgr.astype(BF16)
        dbg_ref[...] += jnp.sum(dgr, axis=0, keepdims=True)

    return rowcall(body, name="merge_bwd", S=S, ts=512,
                   ins=[(dmerged, "row"), (gates_raw, "row"), (b_gate, "vec"), (ya, "row"), (yb, "row")],
                   outs=[((S, D_MODEL), BF16, "row"), ((S, D_MODEL), BF16, "row"), ((S, 2 * D_MODEL), BF16, "row"),
                         ((1, 2 * D_MODEL), F32, "acc")])


def attn_bwd(qkv_pad, bias, do_b):
    S = qkv_pad.shape[0] - PAD_ROWS

    def body(q_ref, k_ref, v_ref, bias_ref, do_ref, dq_ref, dk_ref, dv_ref, db_ref, b_ref):
        n = pl.program_id(1)

        @pl.when(n == 0)
        def _():
            dk_ref[...] = jnp.zeros_like(dk_ref)
            dv_ref[...] = jnp.zeros_like(dv_ref)
            db_ref[...] = jnp.zeros_like(db_ref)
            b_ref[...] = jnp.full(b_ref.shape, NEG_INF, F32)
            for hh in range(HEADS_PER_GROUP):
                for qc in range(Q_CHUNKS):
                    b_ref[hh, qc * CHUNK:(qc + 1) * CHUNK, qc * CHUNK:qc * CHUNK + B_BAND] = bias_ref[hh]

        start = pl.multiple_of(n * Q_TILE, Q_TILE)
        kwin = k_ref[pl.ds(start, KEY_WIN), :]
        vwin = v_ref[pl.ds(start, KEY_WIN), :]
        qv, dov = q_ref[...], do_ref[...]
        valid = lax.broadcasted_iota(jnp.int32, (Q_TILE, KEY_WIN), 1) >= PAD_ROWS - n * Q_TILE
        dqs, dks, dvs = [], [], []
        for hh in range(HEADS_PER_GROUP):
            sl = slice(hh * B_DH, (hh + 1) * B_DH)
            p = _band_probs(qv[:, sl], kwin[:, sl], b_ref[hh], valid)
            dp = _dot1(dov[:, sl], vwin[:, sl], "nt")
            ds = p * (dp - jnp.sum(dp * p, axis=-1, keepdims=True))
            dbh = ds[0:CHUNK, 0:B_BAND]
            for qc in range(1, Q_CHUNKS):
                dbh = dbh + ds[qc * CHUNK:(qc + 1) * CHUNK, qc * CHUNK:qc * CHUNK + B_BAND]
            db_ref[hh] += dbh
            dsq = ds * (B_DH ** -0.5)
            dqs.append(_dot1(dsq, kwin[:, sl], "nn"))
            dks.append(_dot1(dsq, qv[:, sl], "tn"))
            dvs.append(_dot1(p, dov[:, sl], "tn"))
        dq_ref[...] = jnp.concatenate(dqs, axis=1).astype(BF16)
        dk_ref[pl.ds(start, KEY_WIN), :] += jnp.concatenate(dks, axis=1)
        dv_ref[pl.ds(start, KEY_WIN), :] += jnp.concatenate(dvs, axis=1)

    col = pl.BlockSpec((PAD_ROWS + S, GROUP_W), lambda g, n: (0, g))
    tile = pl.BlockSpec((Q_TILE, GROUP_W), lambda g, n: (n, g))
    return pl.pallas_call(
        body, name="attn_bwd", grid=(N_GROUPS, S // Q_TILE), in_specs=_attn_specs(S, Q_TILE) + [tile],
        out_specs=[tile, col, col, pl.BlockSpec((HEADS_PER_GROUP, CHUNK, B_BAND), lambda g, n: (g, 0, 0))],
        out_shape=[jax.ShapeDtypeStruct((S, B_W), BF16), jax.ShapeDtypeStruct((PAD_ROWS + S, B_W), F32),
                   jax.ShapeDtypeStruct((PAD_ROWS + S, B_W), F32), jax.ShapeDtypeStruct((B_HEADS, CHUNK, B_BAND), F32)],
        scratch_shapes=[pltpu.VMEM((HEADS_PER_GROUP, Q_TILE, KEY_WIN), F32)],
        compiler_params=_cparams(("parallel", "arbitrary")),
    )(qkv_pad, qkv_pad, qkv_pad, bias, do_b)


def gate_a_bwd(do_a, o_pre, z, norm_w):
    S = o_pre.shape[0]

    def body(d_ref, o_ref, z_ref, nw_ref, dop_ref, dz_ref, dnw_ref):
        nw = nw_ref[...]
        acc = jnp.zeros((1, A_DK), F32)
        for h in range(A_HEADS):
            sl = slice(h * A_DK, (h + 1) * A_DK)
            oh, zh, dh = o_ref[:, sl], z_ref[:, sl], d_ref[:, sl]
            r = lax.rsqrt(jnp.mean(oh * oh, axis=-1, keepdims=True) + RMS_EPS)
            sz, dsz = _silu_and_grad(zh)
            dz_ref[:, sl] = (dh * oh * r * nw * dsz).astype(BF16)
            acc = acc + jnp.sum(dh * oh * r * sz, axis=0, keepdims=True)
            t = dh * nw * sz
            dop_ref[:, sl] = r * t - oh * (r * r * r) * jnp.mean(t * oh, axis=-1, keepdims=True)
        dnw_ref[...] += acc

    return rowcall(body, name="gate_a_bwd", S=S, ts=512,
                   ins=[(do_a, "row"), (o_pre, "row"), (z, "row"), (norm_w, "vec")],
                   outs=[((S, A_W), F32, "row"), ((S, A_W), BF16, "row"), ((1, A_DK), F32, "acc")])


def delta_bwd(q, k, v, beta, g, sprev, tinv, do):
    S = q.shape[0]
    n_chunks = S // CHUNK

    def body(q_ref, k_ref, v_ref, beta_ref, g_ref, sprev_ref, t_ref, do_ref,
             dq_ref, dk_ref, dv_ref, dbeta_ref, dg_ref, dstate_ref):
        @pl.when(pl.program_id(0) == 0)
        def _():
            dstate_ref[...] = jnp.zeros_like(dstate_ref)

        mk = _tri_masks()
        causal, strict, eye = mk["causal"], mk["strict"], mk["eye"]
        blk_end = (lax.broadcasted_iota(jnp.int32, (GROUP_ROWS, 1), 0) & (CHUNK - 1)) == CHUNK - 1
        lane = lax.broadcasted_iota(jnp.int32, (CHUNK, A_HEADS), 1)
        betav, gv = beta_ref[...], g_ref[...]
        dbeta_t = jnp.zeros((CHUNK, A_HEADS), F32)
        dg_t = jnp.zeros((CHUNK, A_HEADS), F32)
        groups, heads = range(N_HEAD_GROUPS), range(HEAD_GROUP)
        st = [dict() for _ in groups]

        def local_part(grp, s):
            s["qs"], s["ks"], s["vs"] = _stack_heads(q_ref, grp), _stack_heads(k_ref, grp), _stack_heads(v_ref, grp)
            s["dos"] = _stack_heads(do_ref, grp)
            s["bs"] = _stack_cols(betav, grp)
            s["loc"] = loc = _delta_local(s["qs"], s["ks"], s["vs"], s["bs"], _stack_cols(gv, grp), mk)
            s["tinv"] = t_ref[0, grp]
            s["rhs"] = jnp.concatenate([loc["vb"], loc["y"]], axis=1)
            s["uw"] = _dot3(s["tinv"], s["rhs"], "nn")

        def state_part(grp, s):
            loc, uw, dos, qs = s["loc"], s["uw"], s["dos"], s["qs"]
            gam, kd, gl, gc = loc["gam"], loc["kd"], loc["gl"], loc["gc"]
            qg = qs * gam
            egl = jnp.exp(gl)
            hid = [grp * HEAD_GROUP + j for j in heads]
            s0 = [sprev_ref[0, h] for h in hid]
            ds1 = [dstate_ref[h] for h in hid]
            w = [_head_rows(uw, j)[:, A_DK:] for j in heads]
            vn = [_head_rows(uw, j)[:, :A_DK] - _dot1(w[j], s0[j], "nn") for j in heads]
            vns = jnp.concatenate(vn, axis=0)
            dvn_local = _dot1(loc["p"], dos, "tn")
            dvn = [_head_rows(dvn_local, j) + _dot1(_head_rows(kd, j), ds1[j], "nn") for j in heads]
            dvns = jnp.concatenate(dvn, axis=0)
            s["dp"] = jnp.where(causal, _dot1(dos, vns, "nt"), 0.0)
            dqg = jnp.concatenate([_dot1(_head_rows(dos, j), s0[j], "nt") for j in heads], axis=0)
            s["dq"] = dqg * gam
            dgc = jnp.sum(dqg * qg, axis=-1, keepdims=True)
            for j in heads:
                dstate_ref[hid[j]] = (_dot1(_head_rows(qg, j), _head_rows(dos, j), "tn")
                                      + egl[(j + 1) * CHUNK - 1:(j + 1) * CHUNK] * ds1[j] - _dot1(w[j], dvn[j], "tn"))
            dkd = jnp.concatenate([_dot1(vn[j], ds1[j], "nt") for j in heads], axis=0)
            s["dk"] = dkd * jnp.exp(gl - gc)
            t1 = jnp.sum(dkd * kd, axis=-1, keepdims=True)
            dgl = jnp.concatenate(
                [jnp.broadcast_to(jnp.sum(_head_rows(t1, j), axis=0, keepdims=True)
                                  + jnp.sum(jnp.sum(ds1[j] * s0[j], axis=-1, keepdims=True), axis=0, keepdims=True)
                                  * egl[(j + 1) * CHUNK - 1:(j + 1) * CHUNK], (CHUNK, 1)) for j in heads], axis=0)
            s["dgc"] = dgc - t1 + jnp.where(blk_end, dgl, 0.0)
            s["duw"] = jnp.concatenate(
                [dvns, jnp.concatenate([-_dot1(dvn[j], s0[j], "nt") for j in heads], axis=0)], axis=1)

        def solve_part(grp, s):
            s["dvby"] = _dot3(s["tinv"], s["duw"], "tn")
            s["dt"] = _dot3(s["duw"], s["rhs"], "nt")

        def inverse_part_a(grp, s):
            s["tdt"] = _dot3(s["tinv"], s["dt"], "tn")

        def inverse_part_b(grp, s):
            s["da"] = jnp.where(strict, -_dot3(s["tdt"], s["tinv"], "nt"), 0.0)

        def finish(grp, s):
            loc, qs, ks, vs, bs, da, dp, dvby = s["loc"], s["qs"], s["ks"], s["vs"], s["bs"], s["da"], s["dp"], s["dvby"]
            gam, decay = loc["gam"], loc["decay"]
            dm = da * decay
            dn = dp * decay
            e = da * loc["a"] + dp * loc["p"]
            dgc = s["dgc"] + jnp.sum(e, axis=1, keepdims=True) - _row_to_col(jnp.sum(e, axis=0, keepdims=True), eye)
            dy = dvby[:, A_DK:]
            dvb = dvby[:, :A_DK]
            dkb = _dot1(dm, ks, "nn") + dy * gam
            dk = s["dk"] + _dot1(dm, loc["kb"], "tn") + _dot1(dn, qs, "tn") + dkb * bs
            dq = s["dq"] + _dot1(dn, ks, "nn")
            dgc = dgc + jnp.sum(dy * loc["y"], axis=-1, keepdims=True)
            dbeta = jnp.sum(dkb * ks, axis=-1, keepdims=True) + jnp.sum(dvb * vs, axis=-1, keepdims=True)
            dv = dvb * bs
            dgs = jnp.sum(jnp.where(mk["upper"], _col_to_row(dgc, eye), 0.0), axis=1, keepdims=True)
            for j in heads:
                h = grp * HEAD_GROUP + j
                sl = slice(h * A_DK, (h + 1) * A_DK)
                dq_ref[:, sl] = _head_rows(dq, j)
                dk_ref[:, sl] = _head_rows(dk, j)
                dv_ref[:, sl] = _head_rows(dv, j)
            s["dbeta"], s["dgs"] = dbeta, dgs

        for stage in (local_part, state_part, solve_part, inverse_part_a, inverse_part_b, finish):
            for grp in groups:
                stage(grp, st[grp])
        for grp in groups:
            for j in heads:
                h = grp * HEAD_GROUP + j
                dbeta_t = dbeta_t + jnp.where(lane == h, _head_rows(st[grp]["dbeta"], j), 0.0)
                dg_t = dg_t + jnp.where(lane == h, _head_rows(st[grp]["dgs"], j), 0.0)
        dbeta_ref[...] = dbeta_t
        dg_ref[...] = dg_t

    rev = lambda n: (n_chunks - 1 - n, 0)
    rev4 = lambda n: (n_chunks - 1 - n, 0, 0, 0)
    tile = pl.BlockSpec((CHUNK, A_W), rev)
    small = pl.BlockSpec((CHUNK, A_HEADS), rev)
    return pl.pallas_call(
        body, name="delta_bwd", grid=(n_chunks,),
        in_specs=[tile, tile, tile, small, small, pl.BlockSpec((1, A_HEADS, A_DK, A_DK), rev4),
                  pl.BlockSpec((1, N_HEAD_GROUPS, GROUP_ROWS, GROUP_ROWS), rev4), tile],
        out_specs=[tile, tile, tile, small, small],
        out_shape=[jax.ShapeDtypeStruct((S, A_W), F32)] * 3 + [jax.ShapeDtypeStruct((S, A_HEADS), F32)] * 2,
        scratch_shapes=[pltpu.VMEM((A_HEADS, A_DK, A_DK), F32)],
        compiler_params=_cparams(("arbitrary",)),
    )(q, k, v, beta, g, sprev, tinv, do)


def _prep_a_dpre(raw, raw_prev, w, dq, dk, dv):
    y, dy_dpre = _prep_a_core(raw, raw_prev, w)
    parts = []
    for h in range(A_HEADS):
        yq = y[:, h * A_DK:(h + 1) * A_DK]
        dqh = dq[:, h * A_DK:(h + 1) * A_DK]
        rq = lax.rsqrt(jnp.sum(yq * yq, axis=-1, keepdims=True) + L2_EPS)
        parts.append((A_DK ** -0.5) * (rq * dqh - yq * (rq * rq * rq) * jnp.sum(dqh * yq, axis=-1, keepdims=True)))
    for h in range(A_HEADS):
        yk = y[:, A_W + h * A_DK:A_W + (h + 1) * A_DK]
        dkh = dk[:, h * A_DK:(h + 1) * A_DK]
        rk = lax.rsqrt(jnp.sum(yk * yk, axis=-1, keepdims=True) + L2_EPS)
        parts.append(rk * dkh - yk * (rk * rk * rk) * jnp.sum(dkh * yk, axis=-1, keepdims=True))
    parts.append(dv)
    return jnp.concatenate(parts, axis=1) * dy_dpre


def prep_a_bwd(qkv_raw, ba, conv_a, a_log, dt_bias, dq, dk, dv, dbeta, dg):
    S = qkv_raw.shape[0]
    ts = 256

    def body(x_ref, xp_ref, xn_ref, ba_ref, w_ref, al_ref, dt_ref, dq_ref, dqn_ref, dk_ref, dkn_ref, dv_ref, dvn_ref,
             dbeta_ref, dg_ref, draw_ref, dba_ref, dw_ref, dal_ref, ddt_ref):
        i = pl.program_id(0)
        first = (i > 0).astype(F32)
        last = (i < pl.num_programs(0) - 1).astype(F32)
        w = w_ref[...]
        cur, prev = x_ref[...].astype(F32), _halo_prev(xp_ref) * first
        dpre = _prep_a_dpre(cur, prev, w, dq_ref[...], dk_ref[...], dv_ref[...])
        dpre_n = _prep_a_dpre(_halo_next(xn_ref), cur[ts - 8:ts], w, _halo_next(dqn_ref), _halo_next(dkn_ref),
                              _halo_next(dvn_ref)) * last
        for j in range(A_CONV):
            dw_ref[j:j + 1, :] += jnp.sum(dpre * _shift_down(cur, prev, A_CONV - 1 - j), axis=0, keepdims=True)
        draw = dpre * w[A_CONV - 1:A_CONV]
        for j in range(A_CONV - 1):
            draw = draw + _shift_up(dpre, dpre_n, A_CONV - 1 - j) * w[j:j + 1]
        draw_ref[...] = draw.astype(BF16)
        bav = ba_ref[...]
        beta = _sigmoid(bav[:, 0:A_HEADS])
        xa = bav[:, A_HEADS:2 * A_HEADS] + dt_ref[...]
        nexp = -jnp.exp(al_ref[...])
        dgv = dg_ref[...]
        da = dgv * nexp * _sigmoid(xa)
        dba_ref[:, 0:A_HEADS] = dbeta_ref[...] * beta * (1.0 - beta)
        dba_ref[:, A_HEADS:2 * A_HEADS] = da
        dal_ref[...] += jnp.sum(dgv * nexp * _softplus(xa), axis=0, keepdims=True)
        ddt_ref[...] += jnp.sum(da, axis=0, keepdims=True)

    return rowcall(
        body, name="prep_a_bwd", S=S, ts=ts,
        ins=[(qkv_raw, "row"), (qkv_raw, "prev"), (qkv_raw, "next"), (ba, "row"), (conv_a, "vec"), (a_log, "vec"),
             (dt_bias, "vec"), (dq, "row"), (dq, "next"), (dk, "row"), (dk, "next"), (dv, "row"), (dv, "next"),
             (dbeta, "row"), (dg, "row")],
        outs=[((S, 3 * A_W), BF16, "row"), ((S, 2 * A_HEADS), F32, "row"), ((A_CONV, 3 * A_W), F32, "acc"),
              ((1, A_HEADS), F32, "acc"), ((1, A_HEADS), F32, "acc")])


def grad_x_final(dh1, x, dxpre1, mod):
    S = x.shape[0]

    def body(dh_ref, x_ref, dx_ref, m_ref, gx_ref, dscale_ref, dshift_ref):
        dh = dh_ref[...]
        gx_ref[...] = ALPHA * dx_ref[...] + dh * (1.0 + m_ref[...][SCALE_T:SCALE_T + 1])
        dscale_ref[...] += jnp.sum(dh * x_ref[...], axis=0, keepdims=True)
        dshift_ref[...] += jnp.sum(dh, axis=0, keepdims=True)

    vec = ((1, D_MODEL), F32, "acc")
    return rowcall(body, name="grad_x_final", S=S, ts=512, ins=[(dh1, "row"), (x, "row"), (dxpre1, "row"), (mod, "vec")],
                   outs=[((S, D_MODEL), F32, "row"), vec, vec])


_C_QKV, _C_Z, _C_BA, _C_QKVB, _C_G = 0, 3 * A_W, 4 * A_W, 4 * A_W + 2 * A_HEADS, 4 * A_W + 2 * A_HEADS + 3 * B_W
BA_PAD = 128


def split_w_in(w_in):
    ba = jnp.pad(w_in[:, _C_BA:_C_QKVB], ((0, 0), (0, BA_PAD - 2 * A_HEADS)))
    return dict(qkv=w_in[:, _C_QKV:_C_Z], z=w_in[:, _C_Z:_C_BA], ba=ba, qkvb=w_in[:, _C_QKVB:_C_G], g=w_in[:, _C_G:])


def join_w_in(p):
    return jnp.concatenate([p["qkv"], p["z"], p["ba"][:, :2 * A_HEADS], p["qkvb"], p["g"]], axis=1)


def forward_local(x, target, mod, w, sm, late_weights=None):
    h1 = modulate(x, mod, SHIFT_T, SCALE_T, "mod_t")
    qkv_raw = mm(h1, w["qkv"], mode="nn", out_dtype=BF16, name="proj_qkv")
    z = mm(h1, w["z"], mode="nn", out_dtype=F32, name="proj_z")
    ba = mm(h1, w["ba"], mode="nn", out_dtype=F32, name="proj_ba")
    qkvb = mm(h1, w["qkvb"], mode="nn", out_dtype=BF16, name="proj_qkvb")
    gates_raw = mm(h1, w["g"], mode="nn", out_dtype=F32, name="proj_g")
    q, k, v, beta, g = prep_a_fwd(qkv_raw, ba, sm["conv_a"], sm["a_log"], sm["dt_bias"])
    o_pre, sprev, tinv = delta_fwd(q, k, v, beta, g)
    o_a = gate_a_fwd(o_pre, z, sm["norm_a"])
    qkv_pad = jnp.pad(qkvb, ((PAD_ROWS, 0), (0, 0)))
    bias = jnp.transpose(bias_expand(sm["rel_bias"]), (1, 0, 2))
    o_b = attn_fwd(qkv_pad, bias)
    if late_weights is not None:
        w = dict(w, **late_weights(o_b))
    ya = mm(o_a, w["branch_a"], mode="nn", out_dtype=F32, name="branch_a")
    yb = mm(o_b, w["branch_b"], mode="nn", out_dtype=F32, name="branch_b")
    merged = merge_fwd(gates_raw, sm["b_gate"], ya, yb)
    mix = mm(merged, w["o"], mode="nn", out_dtype=F32, name="mix")
    xpre1, x1, h2 = ln1_fwd(x, mix, mod, sm["ln1_g"], sm["ln1_b"])
    up = mm(h2, w["up"], mode="nn", out_dtype=BF16, name="ffn_up")
    act = ffn_act_fwd(up, sm["conv_ffn"], sm["b_conv_ffn"])
    ffn = mm(act, w["down"], mode="nn", out_dtype=F32, name="ffn_down")
    dxpre2, dffn, loss, dgate_f, dln2_g, dln2_b = final_fwd_bwd(x1, ffn, target, mod, sm["ln2_g"], sm["ln2_b"])
    saved = dict(h1=h1, qkv_raw=qkv_raw, z=z, ba=ba, gates_raw=gates_raw, q=q, k=k, v=v, beta=beta, g=g,
                 o_pre=o_pre, sprev=sprev, tinv=tinv, o_a=o_a, qkv_pad=qkv_pad, bias=bias, o_b=o_b, ya=ya, yb=yb,
                 merged=merged, mix=mix, xpre1=xpre1, x1=x1, h2=h2, up=up, act=act, ffn=ffn, w=w)
    return loss, dxpre2, dffn, dict(gate_f=dgate_f, ln2_g=dln2_g, ln2_b=dln2_b), saved


def backward_local(x, mod, sm, dxpre2, dffn, fin, sv, early_grads=None):
    w = sv["w"]
    dact = mm(dffn, w["down"], mode="nt", out_dtype=BF16, name="d_act")
    gw_down = mm(sv["act"], dffn, mode="tn", out_dtype=BF16, name="gw_down")
    dup, dconv_ffn, db_conv_ffn = ffn_act_bwd(dact, sv["up"], sm["conv_ffn"], sm["b_conv_ffn"])
    dh2 = mm(dup, w["up"], mode="nt", out_dtype=F32, name="d_h2")
    gw_up = mm(sv["h2"], dup, mode="tn", out_dtype=BF16, name="gw_up")
    dxpre1, dmix, dsc_f, dsh_f, dgate_t, dln1_g, dln1_b = ln1_bwd(
        dxpre2, dh2, sv["xpre1"], sv["mix"], mod, sm["ln1_g"], sm["ln1_b"])
    dmerged = mm(dmix, w["o"], mode="nt", out_dtype=F32, name="d_merged")
    gw_o = mm(sv["merged"], dmix, mode="tn", out_dtype=BF16, name="gw_o")
    dya, dyb, dgates, db_gate = merge_bwd(dmerged, sv["gates_raw"], sm["b_gate"], sv["ya"], sv["yb"])
    do_a = mm(dya, w["branch_a"], mode="nt", out_dtype=F32, name="d_oa")
    gw_branch_a = mm(sv["o_a"], dya, mode="tn", out_dtype=BF16, name="gw_branch_a")
    do_b = mm(dyb, w["branch_b"], mode="nt", out_dtype=BF16, name="d_ob")
    gw_branch_b = mm(sv["o_b"], dyb, mode="tn", out_dtype=BF16, name="gw_branch_b")
    bias = sv["bias"]
    if early_grads is not None:
        bias = bias + early_grads(dict(w_branch_a=gw_branch_a, w_branch_b=gw_branch_b, w_o=gw_o, w_up=gw_up,
                                       w_down=gw_down))[0, 0]
    dq_b, dk_pad, dv_pad, dbias = attn_bwd(sv["qkv_pad"], bias, do_b)
    dqkvb = jnp.concatenate([dq_b, dk_pad[PAD_ROWS:].astype(BF16), dv_pad[PAD_ROWS:].astype(BF16)], axis=1)
    drel_bias = bias_reduce(jnp.transpose(dbias, (1, 0, 2)))
    do_pre, dz, dnorm_a = gate_a_bwd(do_a, sv["o_pre"], sv["z"], sm["norm_a"])
    dq, dk, dv, dbeta, dg = delta_bwd(sv["q"], sv["k"], sv["v"], sv["beta"], sv["g"], sv["sprev"], sv["tinv"], do_pre)
    dqkv_raw, dba16, dconv_a, da_log, ddt_bias = prep_a_bwd(
        sv["qkv_raw"], sv["ba"], sm["conv_a"], sm["a_log"], sm["dt_bias"], dq, dk, dv, dbeta, dg)
    dba = jnp.pad(dba16, ((0, 0), (0, BA_PAD - 2 * A_HEADS))).astype(BF16)
    pieces = dict(qkv=dqkv_raw, z=dz, ba=dba, qkvb=dqkvb, g=dgates)
    dh1 = None
    gw_in = {}
    for key, dpiece in pieces.items():
        dh1 = mm(dpiece, w[key], mode="nt", out_dtype=F32, name="d_h1_" + key, acc_in=dh1)
        gw_in[key] = mm(sv["h1"], dpiece, mode="tn", out_dtype=BF16, name="gw_in_" + key)
    grad_x, dsc_t, dsh_t = grad_x_final(dh1, x, dxpre1, mod)
    dmod = jnp.concatenate([dsh_t, dsc_t, dgate_t, dsh_f, dsc_f, fin["gate_f"]], axis=0)
    gw = dict(w_in=join_w_in(gw_in), w_branch_a=gw_branch_a, w_branch_b=gw_branch_b, w_o=gw_o, w_up=gw_up, w_down=gw_down)
    gs = dict(b_gate=db_gate, conv_a=dconv_a, a_log=da_log, dt_bias=ddt_bias, norm_a=dnorm_a, rel_bias=drel_bias,
              ln1_g=dln1_g, ln1_b=dln1_b, conv_ffn=dconv_ffn, b_conv_ffn=db_conv_ffn, ln2_g=fin["ln2_g"], ln2_b=fin["ln2_b"])
    return grad_x, dmod, gw, gs


MESH = pl.DeviceIdType.MESH
ANY = pl.BlockSpec(memory_space=pl.ANY)
WHOLE_VMEM = pl.BlockSpec(memory_space=pltpu.VMEM)


def _place():
    return lax.axis_index("x"), lax.axis_index("y"), lax.axis_index("c")


def allgather8(blk, name):
    m_per, n = blk.shape

    def body(x_ref, out_ref, send_sems, recv_sems, local_sem):
        x, y, c = _place()
        me, sibling = (x, y, c), (x, y, 1 - c)
        chips = [(1 - x, y), (x, 1 - y), (1 - x, 1 - y)]

        def rows(px, py, pc):
            return out_ref.at[pl.ds((4 * px + 2 * py + pc) * m_per, m_per), :]

        def copy(k, block, to, src=None):
            return pltpu.make_async_remote_copy(
                src_ref=rows(*block) if src is None else src, dst_ref=rows(*block),
                send_sem=send_sems.at[k], recv_sem=recv_sems.at[k], device_id=to, device_id_type=MESH)

        mine = pltpu.make_async_copy(x_ref, rows(*me), local_sem)
        mine.start()
        first = [copy(0, me, sibling, src=x_ref)]
        first += [copy(1 + j, me, (*chip, c), src=x_ref) for j, chip in enumerate(chips)]
        for cp in first:
            cp.start()
        passed = [copy(4 + j, (*chip, c), sibling) for j, chip in enumerate(chips)]
        for j, chip in enumerate(chips):
            copy(1 + j, (*chip, c), me).wait_recv()
            passed[j].start()
        copy(0, sibling, me).wait_recv()
        for j, chip in enumerate(chips):
            copy(4 + j, (*chip, 1 - c), me).wait_recv()
        for cp in first + passed:
            cp.wait_send()
        mine.wait()

    return pl.pallas_call(
        body, name=name, out_shape=jax.ShapeDtypeStruct((N_DEV * m_per, n), blk.dtype),
        in_specs=[WHOLE_VMEM], out_specs=WHOLE_VMEM,
        scratch_shapes=[pltpu.SemaphoreType.DMA((7,)), pltpu.SemaphoreType.DMA((7,)), pltpu.SemaphoreType.DMA],
    )(blk)


def _chip_peers(x, y):
    return [(1 - x, y), (x, 1 - y), (1 - x, 1 - y)]


def chip_exchange(arrs, name, scatter):
    n = len(arrs)

    def body(*refs):
        ins, outs = refs[:n], refs[n:2 * n]
        send_sems, recv_sems, local_sems = refs[2 * n:]
        x, y, c = _place()
        me = 2 * x + y
        sibling = (x, y, 1 - c)
        peers = _chip_peers(x, y)

        def half(ref, which):
            r2 = ref.shape[0] // 2
            return ref.at[pl.ds(which * r2, r2), :]

        def outgoing(a, chip):
            return ins[a].at[chip] if scatter else ins[a]

        def copy(k, src, dst, to):
            return pltpu.make_async_remote_copy(src_ref=src, dst_ref=dst, send_sem=send_sems.at[k],
                                                recv_sem=recv_sems.at[k], device_id=to, device_id_type=MESH)

        started, local = [], []
        for a in range(n):
            lc = pltpu.make_async_copy(outgoing(a, me), outs[a].at[me], local_sems.at[a])
            lc.start()
            local.append(lc)
            for j, (px, py) in enumerate(peers):
                cp = copy(6 * a + j, half(outgoing(a, 2 * px + py), c), half(outs[a].at[me], c), (px, py, c))
                cp.start()
                started.append(cp)
        for a in range(n):
            for j, (px, py) in enumerate(peers):
                landed = half(outs[a].at[2 * px + py], c)
                copy(6 * a + j, landed, landed, (px, py, c)).wait_recv()
                relay = copy(6 * a + 3 + j, landed, landed, sibling)
                relay.start()
                started.append(relay)
        for a in range(n):
            for j, (px, py) in enumerate(peers):
                other = half(outs[a].at[2 * px + py], 1 - c)
                copy(6 * a + 3 + j, other, other, sibling).wait_recv()
        for cp in started:
            cp.wait_send()
        for lc in local:
            lc.wait()

    out_shape = [jax.ShapeDtypeStruct(a.shape if scatter else (N_CHIPS,) + a.shape, a.dtype) for a in arrs]
    return pl.pallas_call(
        body, name=name, out_shape=out_shape, in_specs=[ANY] * n, out_specs=[ANY] * n,
        scratch_shapes=[pltpu.SemaphoreType.DMA((6 * n,)), pltpu.SemaphoreType.DMA((6 * n,)), pltpu.SemaphoreType.DMA((n,))],
    )(*arrs)


HBM_SPEC = pl.BlockSpec(memory_space=pltpu.HBM)
SEM_SPEC = pl.BlockSpec(memory_space=pltpu.SEMAPHORE)
SIDE_EFFECT = pltpu.SideEffectType.DATAFLOW_SIDE_EFFECTING


def _in_hbm(a):
    return pltpu.with_memory_space_constraint(a, pltpu.HBM)


def exchange_start(arrs, name, scatter, after):
    n = len(arrs)
    lands = [lax.empty(a.shape if scatter else (N_CHIPS,) + a.shape, a.dtype) for a in arrs]

    def body(*refs):
        ins, zones = refs[:n], refs[n:2 * n]
        send_sems, recv_sems, token = refs[2 * n + 1], refs[2 * n + 2], refs[-1]
        x, y, c = _place()
        me = 2 * x + y
        for a in range(n):
            for j, (px, py) in enumerate(_chip_peers(x, y)):
                pltpu.make_async_remote_copy(
                    src_ref=ins[a].at[2 * px + py] if scatter else ins[a], dst_ref=zones[a].at[me],
                    send_sem=send_sems.at[3 * a + j], recv_sem=recv_sems.at[3 * a + j],
                    device_id=(px, py, c), device_id_type=MESH).start()
        token[...] = jnp.zeros_like(token)

    res = pl.pallas_call(
        body, name=name,
        out_shape=[pltpu.SemaphoreType.DMA((3 * n,)), pltpu.SemaphoreType.DMA((3 * n,))]
        + [pltpu.HBM(a.shape, a.dtype) for a in arrs] + [pltpu.HBM(z.shape, z.dtype) for z in lands]
        + [jax.ShapeDtypeStruct((8, 128), F32)],
        in_specs=[HBM_SPEC] * (2 * n) + [ANY], out_specs=[SEM_SPEC, SEM_SPEC] + [HBM_SPEC] * (2 * n) + [WHOLE_VMEM],
        input_output_aliases={i: 2 + i for i in range(2 * n)},
        compiler_params=pltpu.CompilerParams(has_side_effects=SIDE_EFFECT),
    )(*[_in_hbm(a) for a in arrs], *[_in_hbm(z) for z in lands], after)
    return dict(send=res[0], recv=res[1], src=res[2:2 + n], zones=res[2 + n:2 + 2 * n], token=res[-1], scatter=scatter)


def exchange_wait(handle, name, after):
    srcs, zones, scatter = handle["src"], handle["zones"], handle["scatter"]
    n = len(srcs)

    def body(*refs):
        ins, lands = refs[:n], refs[n:2 * n]
        send_sems, recv_sems = refs[2 * n], refs[2 * n + 1]
        x, y, c = _place()
        me = 2 * x + y
        for a in range(n):
            for j, (px, py) in enumerate(_chip_peers(x, y)):
                cp = pltpu.make_async_remote_copy(
                    src_ref=ins[a].at[me] if scatter else ins[a], dst_ref=lands[a].at[2 * px + py],
                    send_sem=send_sems.at[3 * a + j], recv_sem=recv_sems.at[3 * a + j],
                    device_id=(px, py, c), device_id_type=MESH)
                cp.wait_send()
                cp.wait_recv()

    res = pl.pallas_call(
        body, name=name, out_shape=[pltpu.HBM(a.shape, a.dtype) for a in list(srcs) + list(zones)],
        in_specs=[HBM_SPEC] * (2 * n) + [SEM_SPEC, SEM_SPEC, ANY], out_specs=[HBM_SPEC] * (2 * n),
        input_output_aliases={i: i for i in range(2 * n)},
        compiler_params=pltpu.CompilerParams(has_side_effects=SIDE_EFFECT),
    )(*srcs, *zones, handle["send"], handle["recv"], after)
    return res[n:]


def sibling_exchange(arrs, name):
    n = len(arrs)

    def body(*refs):
        ins, outs = refs[:n], refs[n:2 * n]
        send_sems, recv_sems = refs[2 * n:]
        x, y, c = _place()
        cps = [pltpu.make_async_remote_copy(src_ref=ins[a], dst_ref=outs[a], send_sem=send_sems.at[a],
                                            recv_sem=recv_sems.at[a], device_id=(x, y, 1 - c), device_id_type=MESH)
               for a in range(n)]
        for cp in cps:
            cp.start()
        for cp in cps:
            cp.wait()

    return pl.pallas_call(
        body, name=name, out_shape=[jax.ShapeDtypeStruct(a.shape, a.dtype) for a in arrs],
        in_specs=[ANY] * n, out_specs=[ANY] * n,
        scratch_shapes=[pltpu.SemaphoreType.DMA((n,)), pltpu.SemaphoreType.DMA((n,))],
    )(*arrs)


TILE_BYTES = 2 * 1024 * 1024


def _row_tile(rows, row_bytes):
    if rows * row_bytes <= TILE_BYTES or rows % 8:
        return rows
    best = 8
    for t in range(8, rows + 1, 8):
        if rows % t == 0 and t * row_bytes <= TILE_BYTES:
            best = t
    return best


def pair_add(a, b, name):
    R, C = a.shape
    tr = _row_tile(R, C * 4)

    def body(a_ref, b_ref, o_ref):
        o_ref[...] = (a_ref[...].astype(F32) + b_ref[...].astype(F32)).astype(BF16)

    spec = pl.BlockSpec((tr, C), lambda i: (i, 0))
    return pl.pallas_call(body, name=name, grid=(R // tr,), in_specs=[spec, spec], out_specs=spec,
                          out_shape=jax.ShapeDtypeStruct((R, C), BF16), compiler_params=_cparams(("parallel",)))(a, b)


def sum_lead(parts, name):
    K, R, C = parts.shape
    tr = _row_tile(R, C * 4)

    def body(p_ref, o_ref):
        acc = p_ref[0].astype(F32)
        for j in range(1, K):
            acc = acc + p_ref[j].astype(F32)
        o_ref[...] = acc

    return pl.pallas_call(
        body, name=name, grid=(R // tr,), in_specs=[pl.BlockSpec((K, tr, C), lambda i: (0, i, 0))],
        out_specs=pl.BlockSpec((tr, C), lambda i: (i, 0)), out_shape=jax.ShapeDtypeStruct((R, C), F32),
        compiler_params=_cparams(("parallel",)))(parts)


def adamw(w, g, m, v, name):
    R, C = w.shape
    tr = _row_tile(R, C * 4)

    def body(w_ref, g_ref, m_ref, v_ref, d_ref, mo_ref, vo_ref):
        gv = g_ref[...]
        m2 = ADAM_B1 * m_ref[...] + (1.0 - ADAM_B1) * gv
        v2 = ADAM_B2 * v_ref[...] + (1.0 - ADAM_B2) * (gv * gv)
        m_hat = m2 / (1.0 - ADAM_B1 ** ADAM_STEP)
        v_hat = v2 / (1.0 - ADAM_B2 ** ADAM_STEP)
        d_ref[...] = -ADAM_LR * (m_hat / (jnp.sqrt(v_hat) + ADAM_EPS) + ADAM_WD * w_ref[...])
        mo_ref[...] = m2
        vo_ref[...] = v2

    spec = pl.BlockSpec((tr, C), lambda i: (i, 0))
    return pl.pallas_call(body, name=name, grid=(R // tr,), in_specs=[spec] * 4, out_specs=[spec] * 3,
                          out_shape=[jax.ShapeDtypeStruct((R, C), F32)] * 3, compiler_params=_cparams(("parallel",)))(w, g, m, v)


LANES = 1024


def _pack(arrs, rows):
    out, offs, r = [], [], 0
    for a in arrs:
        flat = a.reshape(-1)
        nr = -(-flat.shape[0] // LANES)
        out.append(jnp.pad(flat, (0, nr * LANES - flat.shape[0])))
        offs.append(r)
        r += nr
    assert r <= rows, (r, rows)
    out.append(jnp.zeros(((rows - r) * LANES,), F32))
    return jnp.concatenate(out).reshape(rows, LANES), offs


def _unpack(packed, offs, shapes):
    flat = packed.reshape(-1)
    return [flat[o * LANES:o * LANES + math.prod(s)].reshape(s) for o, s in zip(offs, shapes)]


WEIGHTS = ["w_ada", "b_ada", "w_in", "b_gate", "conv_a", "a_log", "dt_bias", "norm_a", "rel_bias", "w_branch_a",
           "w_branch_b", "w_o", "ln1_g", "ln1_b", "w_up", "conv_ffn", "b_conv_ffn", "w_down", "ln2_g", "ln2_b"]
BIG = ["w_in", "w_branch_a", "w_branch_b", "w_o", "w_up", "w_down"]
LATE = [n for n in BIG if n != "w_in"]
COL_SHARDED = {"w_in", "w_up"}
SMALL_SHARDED = {"conv_a": 3 * A_W // N_CHIPS, "rel_bias": B_REL // N_CHIPS, "conv_ffn": 2 * D_FF // N_CHIPS}
SMALL = [n for n in WEIGHTS if n not in BIG and n != "w_ada"]


def _to_full(g4, name):
    if name in COL_SHARDED:
        return jnp.transpose(g4, (1, 0, 2)).reshape(g4.shape[1], -1)
    return g4.reshape(-1, g4.shape[2])


def _to_shards(full, name):
    if name in COL_SHARDED:
        return jnp.transpose(full.reshape(full.shape[0], N_CHIPS, -1), (1, 0, 2))
    return full.reshape(N_CHIPS, -1, full.shape[1])


def kernel(x, c, w_ada, b_ada, w_in, b_gate, conv_a, a_log, dt_bias, norm_a, rel_bias, w_branch_a, w_branch_b, w_o, ln1_g, ln1_b, w_up, conv_ffn, b_conv_ffn, w_down, ln2_g, ln2_b, loss_target, m_w_ada, m_b_ada, m_w_in, m_b_gate, m_conv_a, m_a_log, m_dt_bias, m_norm_a, m_rel_bias, m_w_branch_a, m_w_branch_b, m_w_o, m_ln1_g, m_ln1_b, m_w_up, m_conv_ffn, m_b_conv_ffn, m_w_down, m_ln2_g, m_ln2_b, v_w_ada, v_b_ada, v_w_in, v_b_gate, v_conv_a, v_a_log, v_dt_bias, v_norm_a, v_rel_bias, v_w_branch_a, v_w_branch_b, v_w_o, v_ln1_g, v_ln1_b, v_w_up, v_conv_ffn, v_b_conv_ffn, v_w_down, v_ln2_g, v_ln2_b):
    args = dict(locals())
    wts = {n: args[n] for n in WEIGHTS}
    moms = {n: args["m_" + n] for n in WEIGHTS}
    vars_ = {n: args["v_" + n] for n in WEIGHTS}
    xi, yi, ci = _place()
    chip = 2 * xi + yi
    dev = 4 * xi + 2 * yi + ci
    ada_cols = w_ada.shape[2]

    c_all = allgather8(jnp.pad(c, ((0, 7), (0, 0))), "gather_c").reshape(N_DEV, 8, D_MODEL)[:, 0]
    b_ada_sh = lax.dynamic_slice(b_ada, (0, chip * ada_cols), (1, ada_cols))
    mod_sh = ada_fwd(c_all, w_ada[0], b_ada_sh)
    mod_g = allgather8(mod_sh, "gather_mod").reshape(N_CHIPS, 2, N_DEV, ada_cols)[:, 0]
    mod = lax.dynamic_slice(mod_g, (0, dev, 0), (N_CHIPS, 1, ada_cols)).reshape(6, D_MODEL)

    (w_in_g4,) = chip_exchange([wts["w_in"][0].astype(BF16)], "gather_w_in", scatter=False)
    wd = split_w_in(_to_full(w_in_g4, "w_in"))
    late_shards = [wts[n][0].astype(BF16) for n in LATE]
    late_gather = exchange_start(late_shards, "gather_late_start", scatter=False, after=w_in_g4)
    mod = mod + late_gather["token"][0, 0]

    def late_weights(after):
        zones = exchange_wait(late_gather, "gather_late_wait", after)
        full = [_to_full(lax.dynamic_update_slice(z, s[None], (chip, 0, 0)), n) for n, z, s in zip(LATE, zones, late_shards)]
        return {n[2:]: f for n, f in zip(LATE, full)}

    sshapes = [wts[n].shape[1:] for n in SMALL_SHARDED]
    spack, soffs = _pack([wts[n][0] for n in SMALL_SHARDED], 16)
    sg = allgather8(spack, "gather_small_w").reshape(N_CHIPS, 2, 16, LANES)[:, 0]
    sparts = [_unpack(sg[j], soffs, sshapes) for j in range(N_CHIPS)]
    sm = {n: wts[n] for n in SMALL if n not in SMALL_SHARDED and n != "b_ada"}
    for i, n in enumerate(SMALL_SHARDED):
        sm[n] = jnp.concatenate([sparts[j][i] for j in range(N_CHIPS)], axis=-1)

    early = {}

    def early_grads(g):
        mine = [g[n] for n in LATE]
        theirs = sibling_exchange(mine, "grad_sibling_late")
        early["sums"] = [_to_shards(pair_add(a, b, "grad_pair_" + n), n) for n, a, b in zip(LATE, mine, theirs)]
        early["scatter"] = exchange_start(early["sums"], "grad_scatter_start", scatter=True, after=theirs[0])
        return early["scatter"]["token"]

    loss, dxpre2, dffn, fin, sv = forward_local(x[0], loss_target[0], mod, wd, sm, late_weights)
    grad_x, dmod, gw, gs = backward_local(x[0], mod, sm, dxpre2, dffn, fin, sv, early_grads)

    (theirs_in,) = sibling_exchange([gw["w_in"]], "grad_sibling_w_in")
    sum_in = _to_shards(pair_add(gw["w_in"], theirs_in, "grad_pair_w_in"), "w_in")
    in_scatter = exchange_start([sum_in], "grad_scatter_w_in_start", scatter=True, after=theirs_in)

    gnames = [n for n in SMALL if n != "b_ada"]
    vec, voffs = _pack([dmod + in_scatter["token"][0, 0]] + [gs[n] for n in gnames] + [loss], 56)
    gathered = allgather8(vec, "gather_small_g").reshape(N_DEV, 56, LANES)
    summed = sum_lead(gathered, "sum_small_g")
    full_shapes = [(6, D_MODEL)] + [gs[n].shape for n in gnames] + [(1, 1)]
    parts = _unpack(summed, voffs, full_shapes)
    grads = {"b_ada": parts[0].reshape(1, -1)}
    for n, p in zip(gnames, parts[1:-1]):
        if n in SMALL_SHARDED:
            p = lax.dynamic_slice_in_dim(p, chip * SMALL_SHARDED[n], SMALL_SHARDED[n], axis=1)
        grads[n] = p.reshape(wts[n].shape)
    loss_total = parts[-1].reshape(())
    dmod_all = gathered[:, 0:6, :].reshape(N_DEV, 6 * D_MODEL)
    grads["w_ada"] = ada_bwd(c_all, lax.dynamic_slice(dmod_all, (0, chip * ada_cols), (N_DEV, ada_cols)))[None]

    def own_slot(zone, sums):
        return lax.dynamic_update_slice(zone, lax.dynamic_slice_in_dim(sums, chip, 1, axis=0), (chip, 0, 0))

    zones = exchange_wait(early["scatter"], "grad_scatter_wait", summed)
    for n, z, s in zip(LATE, zones, early["sums"]):
        grads[n] = sum_lead(own_slot(z, s), "grad_sum_" + n)[None]

    delta, new_m, new_v = {}, {}, {}

    def update(n):
        d, m2, v2 = adamw(wts[n][0], grads[n][0], moms[n][0], vars_[n][0], "adamw_" + n)
        delta[n], new_m[n], new_v[n] = d[None], m2[None], v2[None]

    for n in ["w_ada"] + LATE:
        update(n)
    shapes = [wts[n].shape for n in SMALL]
    packs = [_pack([t[n] for n in SMALL], 32) for t in (wts, grads, moms, vars_)]
    outs = adamw(*[p[0] for p in packs], "adamw_small")
    for res, o in zip((delta, new_m, new_v), outs):
        for n, a in zip(SMALL, _unpack(o, packs[0][1], shapes)):
            res[n] = a
    (zone_in,) = exchange_wait(in_scatter, "grad_scatter_w_in_wait", outs[0])
    grads["w_in"] = sum_lead(own_slot(zone_in, sum_in), "grad_sum_w_in")[None]
    update("w_in")
    return (loss_total, grad_x[None], *[grads[n] for n in WEIGHTS], *[delta[n] for n in WEIGHTS],
            *[new_m[n] for n in WEIGHTS], *[new_v[n] for n in WEIGHTS])
```

```python
import functools
import math

import jax
import jax.numpy as jnp
from jax import lax
from jax.experimental import pallas as pl
from jax.experimental.pallas import tpu as pltpu

F32 = jnp.float32
BF16 = jnp.bfloat16

D_MODEL = 1024
CHUNK = 64
A_HEADS = 8
A_DK = 128
A_W = A_HEADS * A_DK
A_CONV = 4
B_HEADS = 16
B_DH = 64
B_W = B_HEADS * B_DH
B_PREV = 8
B_BAND = (B_PREV + 1) * CHUNK
B_MAX_REL = 256
B_REL = CHUNK - 1 + B_MAX_REL + 1
D_FF = 2816
FFN_CONV = 3
IN_COLS = 4 * A_W + 2 * A_HEADS + 3 * B_W + 2 * D_MODEL
ALPHA = 2.0 ** 0.25
LN_EPS = 1e-5
RMS_EPS = 1e-6
L2_EPS = 1e-6
NEG_INF = -1e30
ADAM_LR, ADAM_B1, ADAM_B2, ADAM_EPS, ADAM_WD, ADAM_STEP = 0.001, 0.9, 0.999, 1e-08, 0.01, 10
N_CHIPS = 4
N_DEV = 8
VMEM_LIMIT = 56 * 1024 * 1024


def _cparams(sem=None):
    return pltpu.CompilerParams(dimension_semantics=sem, vmem_limit_bytes=VMEM_LIMIT)


_DIMS = {"nn": (((1,), (0,)), ((), ())), "nt": (((1,), (1,)), ((), ())), "tn": (((0,), (0,)), ((), ()))}


MM_TILE_CAP = 1408


def _mm_tile(n):
    return max(t for t in range(128, min(n, MM_TILE_CAP) + 1, 128) if n % t == 0)


def mm(a, b, *, mode, out_dtype, name, acc_in=None):
    if mode == "nn":
        (M, K), (K2, N) = a.shape, b.shape
    elif mode == "nt":
        (M, K), (N, K2) = a.shape, b.shape
    else:
        (K, M), (K2, N) = a.shape, b.shape
    assert K == K2, (a.shape, b.shape, mode)
    tm, tn, tk = _mm_tile(M), _mm_tile(N), _mm_tile(K)
    nk = K // tk

    def body(*refs):
        if acc_in is None:
            a_ref, b_ref, o_ref, acc_ref = refs
        else:
            a_ref, b_ref, c_ref, o_ref, acc_ref = refs
        k = pl.program_id(2)

        @pl.when(k == 0)
        def _():
            if acc_in is None:
                acc_ref[...] = jnp.zeros_like(acc_ref)
            else:
                acc_ref[...] = c_ref[...]

        acc_ref[...] += lax.dot_general(a_ref[...].astype(BF16), b_ref[...].astype(BF16), _DIMS[mode],
                                        preferred_element_type=F32)

        @pl.when(k == nk - 1)
        def _():
            o_ref[...] = acc_ref[...].astype(out_dtype)

    a_spec = pl.BlockSpec((tk, tm), lambda i, j, k: (k, i)) if mode == "tn" else pl.BlockSpec((tm, tk), lambda i, j, k: (i, k))
    b_spec = pl.BlockSpec((tn, tk), lambda i, j, k: (j, k)) if mode == "nt" else pl.BlockSpec((tk, tn), lambda i, j, k: (k, j))
    o_spec = pl.BlockSpec((tm, tn), lambda i, j, k: (i, j))
    ins, in_specs, aliases = [a, b], [a_spec, b_spec], {}
    if acc_in is not None:
        assert acc_in.shape == (M, N) and acc_in.dtype == F32 and out_dtype == F32
        ins.append(acc_in)
        in_specs.append(o_spec)
        aliases = {2: 0}
    return pl.pallas_call(
        body, name=name, grid=(M // tm, N // tn, nk), in_specs=in_specs, out_specs=o_spec,
        out_shape=jax.ShapeDtypeStruct((M, N), out_dtype), scratch_shapes=[pltpu.VMEM((tm, tn), F32)],
        input_output_aliases=aliases, compiler_params=_cparams(("parallel", "parallel", "arbitrary")),
    )(*ins)


def rowcall(body, *, name, S, ts, ins, outs):
    assert S % ts == 0 and ts % 16 == 0
    nsteps = S // ts
    in_specs, arrays = [], []
    for arr, kind in ins:
        arrays.append(arr)
        if kind == "row":
            in_specs.append(pl.BlockSpec((ts, arr.shape[1]), lambda i: (i, 0)))
        elif kind in ("prev", "next"):
            hr = 8 * (4 // arr.dtype.itemsize)
            per, last = ts // hr, S // hr - 1
            if kind == "prev":
                in_specs.append(pl.BlockSpec((hr, arr.shape[1]), lambda i, per=per: (jnp.maximum(i * per - 1, 0), 0)))
            else:
                in_specs.append(pl.BlockSpec((hr, arr.shape[1]), lambda i, per=per, last=last: (jnp.minimum((i + 1) * per, last), 0)))
        else:
            nd = arr.ndim
            in_specs.append(pl.BlockSpec(arr.shape, lambda i, nd=nd: (0,) * nd))
    out_specs, out_shapes, acc_idx = [], [], []
    for n, (shape, dtype, kind) in enumerate(outs):
        out_shapes.append(jax.ShapeDtypeStruct(shape, dtype))
        if kind == "row":
            out_specs.append(pl.BlockSpec((ts, shape[1]), lambda i: (i, 0)))
        else:
            nd = len(shape)
            out_specs.append(pl.BlockSpec(shape, lambda i, nd=nd: (0,) * nd))
            acc_idx.append(n)
    n_in = len(arrays)

    def wrapped(*refs):
        @pl.when(pl.program_id(0) == 0)
        def _():
            for n in acc_idx:
                refs[n_in + n][...] = jnp.zeros_like(refs[n_in + n])

        body(*refs)

    res = pl.pallas_call(
        wrapped, name=name, grid=(nsteps,), in_specs=in_specs, out_specs=out_specs, out_shape=out_shapes,
        compiler_params=_cparams(("arbitrary",) if acc_idx else ("parallel",)),
    )(*arrays)
    return res


def _halo_prev(ref):
    v = ref[...].astype(F32)
    return v[v.shape[0] - 8:]


def _halo_next(ref):
    return ref[...].astype(F32)[:8]


def _shift_down(cur, prev8, k):
    if k == 0:
        return cur
    rolled = pltpu.roll(cur, k, axis=0)
    fix = pltpu.roll(prev8, k, axis=0)
    row = lax.broadcasted_iota(jnp.int32, (8, 1), 0)
    top = jnp.where(row < k, fix, rolled[0:8])
    if cur.shape[0] == 8:
        return top
    return jnp.concatenate([top, rolled[8:]], axis=0)


def _shift_up(cur, next8, k):
    if k == 0:
        return cur
    n = cur.shape[0]
    rolled = pltpu.roll(cur, n - k, axis=0)
    fix = pltpu.roll(next8, 8 - k, axis=0)
    row = lax.broadcasted_iota(jnp.int32, (8, 1), 0)
    bot = jnp.where(row >= 8 - k, fix, rolled[n - 8:n])
    return jnp.concatenate([rolled[:n - 8], bot], axis=0)


def _sigmoid(x):
    return 1.0 / (1.0 + jnp.exp(-x))


def _silu(x):
    return x * _sigmoid(x)


def _silu_and_grad(x):
    s = _sigmoid(x)
    return x * s, s * (1.0 + x * (1.0 - s))


def _softplus(x):
    return jnp.maximum(x, 0.0) + jnp.log1p(jnp.exp(-jnp.abs(x)))


def _split2(x):
    hi = x.astype(BF16)
    return hi, (x - hi.astype(F32)).astype(BF16)


def _dot1(a, b, mode):
    return lax.dot_general(a.astype(BF16), b.astype(BF16), _DIMS[mode], preferred_element_type=F32)


def _dot3(a, b, mode):
    ah, al = _split2(a)
    bh, bl = _split2(b)
    d = lambda p, q: lax.dot_general(p, q, _DIMS[mode], preferred_element_type=F32)
    return d(ah, bh) + (d(ah, bl) + d(al, bh))


def ada_fwd(c_all, w_sh, b_sh):
    n = w_sh.shape[1]
    tn = 512

    def body(c_ref, w_ref, b_ref, o_ref):
        o_ref[...] = _dot1(_silu(c_ref[...]), w_ref[...], "nn") + b_ref[...]

    return pl.pallas_call(
        body, name="ada_fwd", grid=(n // tn,),
        in_specs=[pl.BlockSpec((N_DEV, D_MODEL), lambda j: (0, 0)), pl.BlockSpec((D_MODEL, tn), lambda j: (0, j)),
                  pl.BlockSpec((1, tn), lambda j: (0, j))],
        out_specs=pl.BlockSpec((N_DEV, tn), lambda j: (0, j)), out_shape=jax.ShapeDtypeStruct((N_DEV, n), F32),
        compiler_params=_cparams(("parallel",)),
    )(c_all, w_sh, b_sh)


def ada_bwd(c_all, dmod_sh):
    n = dmod_sh.shape[1]
    tn = 512

    def body(c_ref, d_ref, o_ref):
        o_ref[...] = _dot1(_silu(c_ref[...]), d_ref[...], "tn")

    return pl.pallas_call(
        body, name="ada_bwd", grid=(n // tn,),
        in_specs=[pl.BlockSpec((N_DEV, D_MODEL), lambda j: (0, 0)), pl.BlockSpec((N_DEV, tn), lambda j: (0, j))],
        out_specs=pl.BlockSpec((D_MODEL, tn), lambda j: (0, j)), out_shape=jax.ShapeDtypeStruct((D_MODEL, n), F32),
        compiler_params=_cparams(("parallel",)),
    )(c_all, dmod_sh)


SHIFT_T, SCALE_T, GATE_T, SHIFT_F, SCALE_F, GATE_F = range(6)


def modulate(x, mod, shift_row, scale_row, name):
    S = x.shape[0]

    def body(x_ref, m_ref, o_ref):
        m = m_ref[...]
        o_ref[...] = (x_ref[...] * (1.0 + m[scale_row:scale_row + 1]) + m[shift_row:shift_row + 1]).astype(BF16)

    return rowcall(body, name=name, S=S, ts=512, ins=[(x, "row"), (mod, "vec")], outs=[((S, D_MODEL), BF16, "row")])[0]


def _conv_fwd(cur, prev, w, width):
    y = cur * w[width - 1:width]
    for j in range(width - 1):
        y = y + _shift_down(cur, prev, width - 1 - j) * w[j:j + 1]
    return y


def _prep_a_core(cur, prev, w):
    return _silu_and_grad(_conv_fwd(cur, prev, w, A_CONV))


def prep_a_fwd(qkv_raw, ba, conv_a, a_log, dt_bias):
    S = qkv_raw.shape[0]

    def body(x_ref, xp_ref, ba_ref, w_ref, al_ref, dt_ref, q_ref, k_ref, v_ref, beta_ref, g_ref):
        first = (pl.program_id(0) > 0).astype(F32)
        y, _ = _prep_a_core(x_ref[...].astype(F32), _halo_prev(xp_ref) * first, w_ref[...])
        for h in range(A_HEADS):
            sl = slice(h * A_DK, (h + 1) * A_DK)
            qh = y[:, sl]
            kh = y[:, A_W + h * A_DK:A_W + (h + 1) * A_DK]
            q_ref[:, sl] = qh * (lax.rsqrt(jnp.sum(qh * qh, axis=-1, keepdims=True) + L2_EPS) * (A_DK ** -0.5))
            k_ref[:, sl] = kh * lax.rsqrt(jnp.sum(kh * kh, axis=-1, keepdims=True) + L2_EPS)
        v_ref[...] = y[:, 2 * A_W:3 * A_W]
        bav = ba_ref[...]
        beta_ref[...] = _sigmoid(bav[:, 0:A_HEADS])
        g_ref[...] = -jnp.exp(al_ref[...]) * _softplus(bav[:, A_HEADS:2 * A_HEADS] + dt_ref[...])

    return rowcall(
        body, name="prep_a_fwd", S=S, ts=256,
        ins=[(qkv_raw, "row"), (qkv_raw, "prev"), (ba, "row"), (conv_a, "vec"), (a_log, "vec"), (dt_bias, "vec")],
        outs=[((S, A_W), F32, "row")] * 3 + [((S, A_HEADS), F32, "row")] * 2)


HEAD_GROUP = 4
GROUP_ROWS = HEAD_GROUP * CHUNK
N_HEAD_GROUPS = A_HEADS // HEAD_GROUP
LOG_CHUNK = int(math.log2(CHUNK))


def _tri_masks():
    rb = lax.broadcasted_iota(jnp.int32, (GROUP_ROWS, GROUP_ROWS), 0)
    cb = lax.broadcasted_iota(jnp.int32, (GROUP_ROWS, GROUP_ROWS), 1)
    same = (rb >> LOG_CHUNK) == (cb >> LOG_CHUNK)
    return dict(causal=same & (rb >= cb), strict=same & (rb > cb), eye=rb == cb, upper=same & (cb >= rb),
                last=cb == (rb | (CHUNK - 1)), rb=rb, cb=cb)


def _col_to_row(colv, eye):
    return jnp.sum(jnp.where(eye, colv, 0.0), axis=0, keepdims=True)


def _row_to_col(rowv, eye):
    return jnp.sum(jnp.where(eye, rowv, 0.0), axis=1, keepdims=True)


def _tri_inv(a_list, mk):
    rb, cb = mk["rb"], mk["cb"]
    ts = [jnp.where(mk["eye"], 1.0, 0.0) - jnp.where((rb >> 1) == (cb >> 1), a, 0.0) for a in a_list]
    for lvl in range(1, LOG_CHUNK):
        rs, cs = rb >> lvl, cb >> lvl
        sel = ((rs & 1) == 1) & (cs == rs - 1)
        inner = [_dot3(t, jnp.where(sel, a, 0.0), "nn") for t, a in zip(ts, a_list)]
        ts = [t - _dot3(i, t, "nn") for i, t in zip(inner, ts)]
    return ts


def _stack_heads(ref, grp):
    return jnp.concatenate([ref[:, (grp * HEAD_GROUP + j) * A_DK:(grp * HEAD_GROUP + j + 1) * A_DK]
                            for j in range(HEAD_GROUP)], axis=0)


def _stack_cols(tile, grp):
    return jnp.concatenate([tile[:, grp * HEAD_GROUP + j:grp * HEAD_GROUP + j + 1] for j in range(HEAD_GROUP)], axis=0)


def _delta_local(q, k, v, beta, g, mk):
    causal, strict, eye = mk["causal"], mk["strict"], mk["eye"]
    g_row = _col_to_row(g, eye)
    gc = jnp.sum(jnp.where(causal, g_row, 0.0), axis=1, keepdims=True)
    gc_row = _col_to_row(gc, eye)
    decay = jnp.where(causal, jnp.exp(jnp.where(causal, gc - gc_row, 0.0)), 0.0)
    gam = jnp.exp(gc)
    kb = k * beta
    vb = v * beta
    y = kb * gam
    a = jnp.where(strict, _dot1(kb, k, "nt") * decay, 0.0)
    p = _dot1(q, k, "nt") * decay
    gl = jnp.sum(jnp.where(mk["last"], gc_row, 0.0), axis=1, keepdims=True)
    kd = k * jnp.exp(gl - gc)
    return dict(gc=gc, decay=decay, gam=gam, kb=kb, vb=vb, y=y, a=a, p=p, gl=gl, kd=kd)


def _head_rows(x, j):
    return x[j * CHUNK:(j + 1) * CHUNK]


def delta_fwd(q, k, v, beta, g):
    S = q.shape[0]
    n_chunks = S // CHUNK

    def body(q_ref, k_ref, v_ref, beta_ref, g_ref, o_ref, sprev_ref, t_ref, state_ref):
        @pl.when(pl.program_id(0) == 0)
        def _():
            state_ref[...] = jnp.zeros_like(state_ref)

        mk = _tri_masks()
        betav, gv = beta_ref[...], g_ref[...]
        groups = range(N_HEAD_GROUPS)
        q_all = [_stack_heads(q_ref, grp) for grp in groups]
        locs = [_delta_local(q_all[grp], _stack_heads(k_ref, grp), _stack_heads(v_ref, grp),
                             _stack_cols(betav, grp), _stack_cols(gv, grp), mk) for grp in groups]
        tinvs = _tri_inv([loc["a"] for loc in locs], mk)
        uws = [_dot3(tinvs[grp], jnp.concatenate([locs[grp]["vb"], locs[grp]["y"]], axis=1), "nn") for grp in groups]
        for grp in groups:
            loc, uw = locs[grp], uws[grp]
            t_ref[0, grp] = tinvs[grp]
            qg = q_all[grp] * loc["gam"]
            egl = jnp.exp(loc["gl"])
            vns, o_state = [], []
            for j in range(HEAD_GROUP):
                h = grp * HEAD_GROUP + j
                s0 = state_ref[h]
                sprev_ref[0, h] = s0
                uw_h = _head_rows(uw, j)
                vn = uw_h[:, :A_DK] - _dot1(uw_h[:, A_DK:], s0, "nn")
                vns.append(vn)
                o_state.append(_dot1(_head_rows(qg, j), s0, "nn"))
                state_ref[h] = s0 * egl[(j + 1) * CHUNK - 1:(j + 1) * CHUNK] + _dot1(_head_rows(loc["kd"], j), vn, "tn")
            o_local = _dot1(loc["p"], jnp.concatenate(vns, axis=0), "nn")
            for j in range(HEAD_GROUP):
                h = grp * HEAD_GROUP + j
                o_ref[:, h * A_DK:(h + 1) * A_DK] = o_state[j] + _head_rows(o_local, j)

    tile = pl.BlockSpec((CHUNK, A_W), lambda n: (n, 0))
    small = pl.BlockSpec((CHUNK, A_HEADS), lambda n: (n, 0))
    return pl.pallas_call(
        body, name="delta_fwd", grid=(n_chunks,), in_specs=[tile, tile, tile, small, small],
        out_specs=[tile, pl.BlockSpec((1, A_HEADS, A_DK, A_DK), lambda n: (n, 0, 0, 0)),
                   pl.BlockSpec((1, N_HEAD_GROUPS, GROUP_ROWS, GROUP_ROWS), lambda n: (n, 0, 0, 0))],
        out_shape=[jax.ShapeDtypeStruct((S, A_W), F32), jax.ShapeDtypeStruct((n_chunks, A_HEADS, A_DK, A_DK), F32),
                   jax.ShapeDtypeStruct((n_chunks, N_HEAD_GROUPS, GROUP_ROWS, GROUP_ROWS), F32)],
        scratch_shapes=[pltpu.VMEM((A_HEADS, A_DK, A_DK), F32)],
        compiler_params=_cparams(("arbitrary",)),
    )(q, k, v, beta, g)


def gate_a_fwd(o_pre, z, norm_w):
    S = o_pre.shape[0]

    def body(o_ref, z_ref, nw_ref, out_ref):
        nw = nw_ref[...]
        for h in range(A_HEADS):
            sl = slice(h * A_DK, (h + 1) * A_DK)
            oh = o_ref[:, sl]
            r = lax.rsqrt(jnp.mean(oh * oh, axis=-1, keepdims=True) + RMS_EPS)
            out_ref[:, sl] = (oh * r * nw * _silu(z_ref[:, sl])).astype(BF16)

    return rowcall(body, name="gate_a_fwd", S=S, ts=512, ins=[(o_pre, "row"), (z, "row"), (norm_w, "vec")],
                   outs=[((S, A_W), BF16, "row")])[0]


HEADS_PER_GROUP = 2
GROUP_W = HEADS_PER_GROUP * B_DH
N_GROUPS = B_HEADS // HEADS_PER_GROUP
PAD_ROWS = B_PREV * CHUNK


Q_TILE = 256
Q_CHUNKS = Q_TILE // CHUNK
KEY_WIN = (B_PREV + Q_CHUNKS) * CHUNK


def _band_probs(qh, kh, bias, valid):
    s = _dot1(qh, kh, "nt") * (B_DH ** -0.5) + bias
    s = jnp.where(valid, s, NEG_INF)
    e = jnp.exp(s - jnp.max(s, axis=-1, keepdims=True))
    return e * (1.0 / jnp.sum(e, axis=-1, keepdims=True))


def _attn_specs(S, tile_rows):
    n_cb = B_W // GROUP_W
    return [pl.BlockSpec((tile_rows, GROUP_W), lambda g, n: (n + PAD_ROWS // tile_rows, g)),
            pl.BlockSpec((PAD_ROWS + S, GROUP_W), lambda g, n: (0, n_cb + g)),
            pl.BlockSpec((PAD_ROWS + S, GROUP_W), lambda g, n: (0, 2 * n_cb + g)),
            pl.BlockSpec((HEADS_PER_GROUP, CHUNK, B_BAND), lambda g, n: (g, 0, 0))]


def _band_valid(first_chunk):
    return lax.broadcasted_iota(jnp.int32, (CHUNK, B_BAND), 1) >= PAD_ROWS - first_chunk * CHUNK


def _chunk_rows(x, qc, rows=CHUNK):
    return x[qc * CHUNK:qc * CHUNK + rows]


FWD_TILE = 512
FWD_CHUNKS = FWD_TILE // CHUNK
FWD_WIN = (B_PREV + FWD_CHUNKS) * CHUNK


def attn_fwd(qkv_pad, bias):
    S = qkv_pad.shape[0] - PAD_ROWS

    def body(q_ref, k_ref, v_ref, b_ref, o_ref):
        n = pl.program_id(1)
        start = pl.multiple_of(n * FWD_TILE, FWD_TILE)
        kwin = k_ref[pl.ds(start, FWD_WIN), :]
        vwin = v_ref[pl.ds(start, FWD_WIN), :]
        qv = q_ref[...]
        pairs = [(qc, hh) for qc in range(FWD_CHUNKS) for hh in range(HEADS_PER_GROUP)]
        sl = lambda hh: slice(hh * B_DH, (hh + 1) * B_DH)
        s = [_dot1(_chunk_rows(qv, qc)[:, sl(hh)], _chunk_rows(kwin, qc, B_BAND)[:, sl(hh)], "nt") for qc, hh in pairs]
        s = [jnp.where(_band_valid(n * FWD_CHUNKS + qc), x * (B_DH ** -0.5) + b_ref[hh], NEG_INF)
             for x, (qc, hh) in zip(s, pairs)]
        e = [jnp.exp(x - jnp.max(x, axis=-1, keepdims=True)) for x in s]
        p = [x * (1.0 / jnp.sum(x, axis=-1, keepdims=True)) for x in e]
        o = [_dot1(x, _chunk_rows(vwin, qc, B_BAND)[:, sl(hh)], "nn") for x, (qc, hh) in zip(p, pairs)]
        rows = [jnp.concatenate(o[qc * HEADS_PER_GROUP:(qc + 1) * HEADS_PER_GROUP], axis=1) for qc in range(FWD_CHUNKS)]
        o_ref[...] = jnp.concatenate(rows, axis=0).astype(BF16)

    return pl.pallas_call(
        body, name="attn_fwd", grid=(N_GROUPS, S // FWD_TILE), in_specs=_attn_specs(S, FWD_TILE),
        out_specs=pl.BlockSpec((FWD_TILE, GROUP_W), lambda g, n: (n, g)),
        out_shape=jax.ShapeDtypeStruct((S, B_W), BF16),
        compiler_params=_cparams(("parallel", "arbitrary")),
    )(qkv_pad, qkv_pad, qkv_pad, bias)


def _rel_onehot(i):
    kj = lax.broadcasted_iota(jnp.int32, (B_BAND, B_REL), 0)
    r = lax.broadcasted_iota(jnp.int32, (B_BAND, B_REL), 1)
    idx = jnp.clip(PAD_ROWS + i - kj, -(CHUNK - 1), B_MAX_REL) + (CHUNK - 1)
    return jnp.where(idx == r, 1.0, 0.0)


def bias_expand(rel_bias):
    def body(rb_ref, o_ref):
        i = pl.program_id(0)
        o_ref[0] = _dot3(rb_ref[...], _rel_onehot(i), "nt")

    return pl.pallas_call(
        body, name="bias_expand", grid=(CHUNK,),
        in_specs=[pl.BlockSpec((B_HEADS, B_REL), lambda i: (0, 0))],
        out_specs=pl.BlockSpec((1, B_HEADS, B_BAND), lambda i: (i, 0, 0)),
        out_shape=jax.ShapeDtypeStruct((CHUNK, B_HEADS, B_BAND), F32),
        compiler_params=_cparams(("parallel",)),
    )(rel_bias)


def bias_reduce(dbias):
    def body(d_ref, o_ref):
        i = pl.program_id(0)

        @pl.when(i == 0)
        def _():
            o_ref[...] = jnp.zeros_like(o_ref)

        o_ref[...] += _dot3(d_ref[0], _rel_onehot(i), "nn")

    return pl.pallas_call(
        body, name="bias_reduce", grid=(CHUNK,),
        in_specs=[pl.BlockSpec((1, B_HEADS, B_BAND), lambda i: (i, 0, 0))],
        out_specs=pl.BlockSpec((B_HEADS, B_REL), lambda i: (0, 0)),
        out_shape=jax.ShapeDtypeStruct((B_HEADS, B_REL), F32),
        compiler_params=_cparams(("arbitrary",)),
    )(dbias)


def merge_fwd(gates_raw, b_gate, ya, yb):
    S = ya.shape[0]

    def body(g_ref, b_ref, ya_ref, yb_ref, o_ref):
        gt = _sigmoid(g_ref[...] + b_ref[...])
        o_ref[...] = (gt[:, :D_MODEL] * ya_ref[...] + gt[:, D_MODEL:] * yb_ref[...]).astype(BF16)

    return rowcall(body, name="merge_fwd", S=S, ts=512,
                   ins=[(gates_raw, "row"), (b_gate, "vec"), (ya, "row"), (yb, "row")],
                   outs=[((S, D_MODEL), BF16, "row")])[0]


def _ln_stats(xpre):
    mu = jnp.mean(xpre, axis=-1, keepdims=True)
    xc = xpre - mu
    rstd = lax.rsqrt(jnp.mean(xc * xc, axis=-1, keepdims=True) + LN_EPS)
    return xc * rstd, rstd


def ln1_fwd(x, mix, mod, ln_g, ln_b):
    S = x.shape[0]

    def body(x_ref, mix_ref, m_ref, g_ref, b_ref, xpre_ref, x1_ref, h2_ref):
        m = m_ref[...]
        xpre = ALPHA * x_ref[...] + m[GATE_T:GATE_T + 1] * mix_ref[...]
        xhat, _ = _ln_stats(xpre)
        x1 = xhat * g_ref[...] + b_ref[...]
        xpre_ref[...] = xpre
        x1_ref[...] = x1
        h2_ref[...] = (x1 * (1.0 + m[SCALE_F:SCALE_F + 1]) + m[SHIFT_F:SHIFT_F + 1]).astype(BF16)

    return rowcall(body, name="ln1_fwd", S=S, ts=512,
                   ins=[(x, "row"), (mix, "row"), (mod, "vec"), (ln_g, "vec"), (ln_b, "vec")],
                   outs=[((S, D_MODEL), F32, "row"), ((S, D_MODEL), F32, "row"), ((S, D_MODEL), BF16, "row")])


def ffn_act_fwd(up, conv_w, conv_b):
    S = up.shape[0]

    def body(u_ref, up_ref, w_ref, b_ref, o_ref):
        first = (pl.program_id(0) > 0).astype(F32)
        uc = _conv_fwd(u_ref[...].astype(F32), _halo_prev(up_ref) * first, w_ref[...], FFN_CONV) + b_ref[...]
        o_ref[...] = (_silu(uc[:, :D_FF]) * uc[:, D_FF:]).astype(BF16)

    return rowcall(body, name="ffn_act_fwd", S=S, ts=128,
                   ins=[(up, "row"), (up, "prev"), (conv_w, "vec"), (conv_b, "vec")],
                   outs=[((S, D_FF), BF16, "row")])[0]


def final_fwd_bwd(x1, ffn, target, mod, ln_g, ln_b):
    S = x1.shape[0]

    def body(x1_ref, f_ref, t_ref, m_ref, g_ref, b_ref, dxpre_ref, dffn_ref, loss_ref, dgate_ref, dg_ref, db_ref):
        gate = m_ref[...][GATE_F:GATE_F + 1]
        ffn_v = f_ref[...]
        xpre = ALPHA * x1_ref[...] + gate * ffn_v
        xhat, rstd = _ln_stats(xpre)
        err = xhat * g_ref[...] + b_ref[...] - t_ref[...]
        loss_ref[...] += 0.5 * jnp.sum(jnp.mean(err * err, axis=-1, keepdims=True), axis=0, keepdims=True)
        dy = err * (1.0 / D_MODEL)
        dg_ref[...] += jnp.sum(dy * xhat, axis=0, keepdims=True)
        db_ref[...] += jnp.sum(dy, axis=0, keepdims=True)
        dyg = dy * g_ref[...]
        dxpre = rstd * (dyg - jnp.mean(dyg, axis=-1, keepdims=True) - xhat * jnp.mean(dyg * xhat, axis=-1, keepdims=True))
        dxpre_ref[...] = dxpre
        dffn_ref[...] = (gate * dxpre).astype(BF16)
        dgate_ref[...] += jnp.sum(dxpre * ffn_v, axis=0, keepdims=True)

    vec = ((1, D_MODEL), F32, "acc")
    return rowcall(body, name="final_fwd_bwd", S=S, ts=512,
                   ins=[(x1, "row"), (ffn, "row"), (target, "row"), (mod, "vec"), (ln_g, "vec"), (ln_b, "vec")],
                   outs=[((S, D_MODEL), F32, "row"), ((S, D_MODEL), BF16, "row"), ((1, 1), F32, "acc"), vec, vec, vec])


def _ffn_duc(dact, uc):
    ug, uv = uc[:, :D_FF], uc[:, D_FF:]
    sg, dsg = _silu_and_grad(ug)
    return jnp.concatenate([dact * uv * dsg, dact * sg], axis=1)


def ffn_act_bwd(dact, up, conv_w, conv_b):
    S = up.shape[0]
    ts = 128

    def body(d_ref, dn_ref, u_ref, up_ref, un_ref, w_ref, b_ref, dup_ref, dw_ref, db_ref):
        i = pl.program_id(0)
        first = (i > 0).astype(F32)
        last = (i < pl.num_programs(0) - 1).astype(F32)
        w, b = w_ref[...], b_ref[...]
        cur, prev = u_ref[...].astype(F32), _halo_prev(up_ref) * first
        shifted = [_shift_down(cur, prev, FFN_CONV - 1 - j) for j in range(FFN_CONV)]
        uc = b + sum(shifted[j] * w[j:j + 1] for j in range(FFN_CONV))
        duc = _ffn_duc(d_ref[...].astype(F32), uc)
        uc_n = _conv_fwd(_halo_next(un_ref), cur[ts - 8:ts], w, FFN_CONV) + b
        duc_n = _ffn_duc(_halo_next(dn_ref), uc_n) * last
        db_ref[...] += jnp.sum(duc, axis=0, keepdims=True)
        for j in range(FFN_CONV):
            dw_ref[j:j + 1, :] += jnp.sum(duc * shifted[j], axis=0, keepdims=True)
        dup = duc * w[FFN_CONV - 1:FFN_CONV]
        for j in range(FFN_CONV - 1):
            dup = dup + _shift_up(duc, duc_n, FFN_CONV - 1 - j) * w[j:j + 1]
        dup_ref[...] = dup.astype(BF16)

    return rowcall(body, name="ffn_act_bwd", S=S, ts=ts,
                   ins=[(dact, "row"), (dact, "next"), (up, "row"), (up, "prev"), (up, "next"), (conv_w, "vec"), (conv_b, "vec")],
                   outs=[((S, 2 * D_FF), BF16, "row"), ((FFN_CONV, 2 * D_FF), F32, "acc"), ((1, 2 * D_FF), F32, "acc")])


def ln1_bwd(dxpre2, dh2, xpre1, mix, mod, ln_g, ln_b):
    S = xpre1.shape[0]

    def body(d2_ref, dh_ref, xp_ref, mix_ref, m_ref, g_ref, b_ref, dxpre_ref, dmix_ref,
             dscale_ref, dshift_ref, dgate_ref, dg_ref, db_ref):
        m = m_ref[...]
        xhat, rstd = _ln_stats(xp_ref[...])
        x1 = xhat * g_ref[...] + b_ref[...]
        dh = dh_ref[...]
        dx1 = ALPHA * d2_ref[...] + dh * (1.0 + m[SCALE_F:SCALE_F + 1])
        dscale_ref[...] += jnp.sum(dh * x1, axis=0, keepdims=True)
        dshift_ref[...] += jnp.sum(dh, axis=0, keepdims=True)
        dg_ref[...] += jnp.sum(dx1 * xhat, axis=0, keepdims=True)
        db_ref[...] += jnp.sum(dx1, axis=0, keepdims=True)
        dyg = dx1 * g_ref[...]
        dxpre = rstd * (dyg - jnp.mean(dyg, axis=-1, keepdims=True) - xhat * jnp.mean(dyg * xhat, axis=-1, keepdims=True))
        dxpre_ref[...] = dxpre
        dmix_ref[...] = (m[GATE_T:GATE_T + 1] * dxpre).astype(BF16)
        dgate_ref[...] += jnp.sum(dxpre * mix_ref[...], axis=0, keepdims=True)

    vec = ((1, D_MODEL), F32, "acc")
    return rowcall(body, name="ln1_bwd", S=S, ts=512,
                   ins=[(dxpre2, "row"), (dh2, "row"), (xpre1, "row"), (mix, "row"), (mod, "vec"), (ln_g, "vec"), (ln_b, "vec")],
                   outs=[((S, D_MODEL), F32, "row"), ((S, D_MODEL), BF16, "row"), vec, vec, vec, vec, vec])


def merge_bwd(dmerged, gates_raw, b_gate, ya, yb):
    S = ya.shape[0]

    def body(d_ref, g_ref, b_ref, ya_ref, yb_ref, dya_ref, dyb_ref, dg_ref, dbg_ref):
        gt = _sigmoid(g_ref[...] + b_ref[...])
        d = d_ref[...]
        ga, gb = gt[:, :D_MODEL], gt[:, D_MODEL:]
        dya_ref[...] = (d * ga).astype(BF16)
        dyb_ref[...] = (d * gb).astype(BF16)
        dgr = jnp.concatenate([d * ya_ref[...] * ga * (1.0 - ga), d * yb_ref[...] * gb * (1.0 - gb)], axis=1)
        dg_ref[...] = dgr.astype(BF16)
        dbg_ref[...] += jnp.sum(dgr, axis=0, keepdims=True)

    return rowcall(body, name="merge_bwd", S=S, ts=512,
                   ins=[(dmerged, "row"), (gates_raw, "row"), (b_gate, "vec"), (ya, "row"), (yb, "row")],
                   outs=[((S, D_MODEL), BF16, "row"), ((S, D_MODEL), BF16, "row"), ((S, 2 * D_MODEL), BF16, "row"),
                         ((1, 2 * D_MODEL), F32, "acc")])


def attn_bwd(qkv_pad, bias, do_b):
    S = qkv_pad.shape[0] - PAD_ROWS

    def body(q_ref, k_ref, v_ref, bias_ref, do_ref, dq_ref, dk_ref, dv_ref, db_ref, b_ref):
        n = pl.program_id(1)

        @pl.when(n == 0)
        def _():
            dk_ref[...] = jnp.zeros_like(dk_ref)
            dv_ref[...] = jnp.zeros_like(dv_ref)
            db_ref[...] = jnp.zeros_like(db_ref)
            b_ref[...] = jnp.full(b_ref.shape, NEG_INF, F32)
            for hh in range(HEADS_PER_GROUP):
                for qc in range(Q_CHUNKS):
                    b_ref[hh, qc * CHUNK:(qc + 1) * CHUNK, qc * CHUNK:qc * CHUNK + B_BAND] = bias_ref[hh]

        start = pl.multiple_of(n * Q_TILE, Q_TILE)
        kwin = k_ref[pl.ds(start, KEY_WIN), :]
        vwin = v_ref[pl.ds(start, KEY_WIN), :]
        qv, dov = q_ref[...], do_ref[...]
        valid = lax.broadcasted_iota(jnp.int32, (Q_TILE, KEY_WIN), 1) >= PAD_ROWS - n * Q_TILE
        dqs, dks, dvs = [], [], []
        for hh in range(HEADS_PER_GROUP):
            sl = slice(hh * B_DH, (hh + 1) * B_DH)
            p = _band_probs(qv[:, sl], kwin[:, sl], b_ref[hh], valid)
            dp = _dot1(dov[:, sl], vwin[:, sl], "nt")
            ds = p * (dp - jnp.sum(dp * p, axis=-1, keepdims=True))
            dbh = ds[0:CHUNK, 0:B_BAND]
            for qc in range(1, Q_CHUNKS):
                dbh = dbh + ds[qc * CHUNK:(qc + 1) * CHUNK, qc * CHUNK:qc * CHUNK + B_BAND]
            db_ref[hh] += dbh
            dsq = ds * (B_DH ** -0.5)
            dqs.append(_dot1(dsq, kwin[:, sl], "nn"))
            dks.append(_dot1(dsq, qv[:, sl], "tn"))
            dvs.append(_dot1(p, dov[:, sl], "tn"))
        dq_ref[...] = jnp.concatenate(dqs, axis=1).astype(BF16)
        dk_ref[pl.ds(start, KEY_WIN), :] += jnp.concatenate(dks, axis=1)
        dv_ref[pl.ds(start, KEY_WIN), :] += jnp.concatenate(dvs, axis=1)

    col = pl.BlockSpec((PAD_ROWS + S, GROUP_W), lambda g, n: (0, g))
    tile = pl.BlockSpec((Q_TILE, GROUP_W), lambda g, n: (n, g))
    return pl.pallas_call(
        body, name="attn_bwd", grid=(N_GROUPS, S // Q_TILE), in_specs=_attn_specs(S, Q_TILE) + [tile],
        out_specs=[tile, col, col, pl.BlockSpec((HEADS_PER_GROUP, CHUNK, B_BAND), lambda g, n: (g, 0, 0))],
        out_shape=[jax.ShapeDtypeStruct((S, B_W), BF16), jax.ShapeDtypeStruct((PAD_ROWS + S, B_W), F32),
                   jax.ShapeDtypeStruct((PAD_ROWS + S, B_W), F32), jax.ShapeDtypeStruct((B_HEADS, CHUNK, B_BAND), F32)],
        scratch_shapes=[pltpu.VMEM((HEADS_PER_GROUP, Q_TILE, KEY_WIN), F32)],
        compiler_params=_cparams(("parallel", "arbitrary")),
    )(qkv_pad, qkv_pad, qkv_pad, bias, do_b)


def gate_a_bwd(do_a, o_pre, z, norm_w):
    S = o_pre.shape[0]

    def body(d_ref, o_ref, z_ref, nw_ref, dop_ref, dz_ref, dnw_ref):
        nw = nw_ref[...]
        acc = jnp.zeros((1, A_DK), F32)
        for h in range(A_HEADS):
            sl = slice(h * A_DK, (h + 1) * A_DK)
            oh, zh, dh = o_ref[:, sl], z_ref[:, sl], d_ref[:, sl]
            r = lax.rsqrt(jnp.mean(oh * oh, axis=-1, keepdims=True) + RMS_EPS)
            sz, dsz = _silu_and_grad(zh)
            dz_ref[:, sl] = (dh * oh * r * nw * dsz).astype(BF16)
            acc = acc + jnp.sum(dh * oh * r * sz, axis=0, keepdims=True)
            t = dh * nw * sz
            dop_ref[:, sl] = r * t - oh * (r * r * r) * jnp.mean(t * oh, axis=-1, keepdims=True)
        dnw_ref[...] += acc

    return rowcall(body, name="gate_a_bwd", S=S, ts=512,
                   ins=[(do_a, "row"), (o_pre, "row"), (z, "row"), (norm_w, "vec")],
                   outs=[((S, A_W), F32, "row"), ((S, A_W), BF16, "row"), ((1, A_DK), F32, "acc")])


def delta_bwd(q, k, v, beta, g, sprev, tinv, do):
    S = q.shape[0]
    n_chunks = S // CHUNK

    def body(q_ref, k_ref, v_ref, beta_ref, g_ref, sprev_ref, t_ref, do_ref,
             dq_ref, dk_ref, dv_ref, dbeta_ref, dg_ref, dstate_ref):
        @pl.when(pl.program_id(0) == 0)
        def _():
            dstate_ref[...] = jnp.zeros_like(dstate_ref)

        mk = _tri_masks()
        causal, strict, eye = mk["causal"], mk["strict"], mk["eye"]
        blk_end = (lax.broadcasted_iota(jnp.int32, (GROUP_ROWS, 1), 0) & (CHUNK - 1)) == CHUNK - 1
        lane = lax.broadcasted_iota(jnp.int32, (CHUNK, A_HEADS), 1)
        betav, gv = beta_ref[...], g_ref[...]
        dbeta_t = jnp.zeros((CHUNK, A_HEADS), F32)
        dg_t = jnp.zeros((CHUNK, A_HEADS), F32)
        groups, heads = range(N_HEAD_GROUPS), range(HEAD_GROUP)
        st = [dict() for _ in groups]

        def local_part(grp, s):
            s["qs"], s["ks"], s["vs"] = _stack_heads(q_ref, grp), _stack_heads(k_ref, grp), _stack_heads(v_ref, grp)
            s["dos"] = _stack_heads(do_ref, grp)
            s["bs"] = _stack_cols(betav, grp)
            s["loc"] = loc = _delta_local(s["qs"], s["ks"], s["vs"], s["bs"], _stack_cols(gv, grp), mk)
            s["tinv"] = t_ref[0, grp]
            s["rhs"] = jnp.concatenate([loc["vb"], loc["y"]], axis=1)
            s["uw"] = _dot3(s["tinv"], s["rhs"], "nn")

        def state_part(grp, s):
            loc, uw, dos, qs = s["loc"], s["uw"], s["dos"], s["qs"]
            gam, kd, gl, gc = loc["gam"], loc["kd"], loc["gl"], loc["gc"]
            qg = qs * gam
            egl = jnp.exp(gl)
            hid = [grp * HEAD_GROUP + j for j in heads]
            s0 = [sprev_ref[0, h] for h in hid]
            ds1 = [dstate_ref[h] for h in hid]
            w = [_head_rows(uw, j)[:, A_DK:] for j in heads]
            vn = [_head_rows(uw, j)[:, :A_DK] - _dot1(w[j], s0[j], "nn") for j in heads]
            vns = jnp.concatenate(vn, axis=0)
            dvn_local = _dot1(loc["p"], dos, "tn")
            dvn = [_head_rows(dvn_local, j) + _dot1(_head_rows(kd, j), ds1[j], "nn") for j in heads]
            dvns = jnp.concatenate(dvn, axis=0)
            s["dp"] = jnp.where(causal, _dot1(dos, vns, "nt"), 0.0)
            dqg = jnp.concatenate([_dot1(_head_rows(dos, j), s0[j], "nt") for j in heads], axis=0)
            s["dq"] = dqg * gam
            dgc = jnp.sum(dqg * qg, axis=-1, keepdims=True)
            for j in heads:
                dstate_ref[hid[j]] = (_dot1(_head_rows(qg, j), _head_rows(dos, j), "tn")
                                      + egl[(j + 1) * CHUNK - 1:(j + 1) * CHUNK] * ds1[j] - _dot1(w[j], dvn[j], "tn"))
            dkd = jnp.concatenate([_dot1(vn[j], ds1[j], "nt") for j in heads], axis=0)
            s["dk"] = dkd * jnp.exp(gl - gc)
            t1 = jnp.sum(dkd * kd, axis=-1, keepdims=True)
            dgl = jnp.concatenate(
                [jnp.broadcast_to(jnp.sum(_head_rows(t1, j), axis=0, keepdims=True)
                                  + jnp.sum(jnp.sum(ds1[j] * s0[j], axis=-1, keepdims=True), axis=0, keepdims=True)
                                  * egl[(j + 1) * CHUNK - 1:(j + 1) * CHUNK], (CHUNK, 1)) for j in heads], axis=0)
            s["dgc"] = dgc - t1 + jnp.where(blk_end, dgl, 0.0)
            s["duw"] = jnp.concatenate(
                [dvns, jnp.concatenate([-_dot1(dvn[j], s0[j], "nt") for j in heads], axis=0)], axis=1)

        def solve_part(grp, s):
            s["dvby"] = _dot3(s["tinv"], s["duw"], "tn")
            s["dt"] = _dot3(s["duw"], s["rhs"], "nt")

        def inverse_part_a(grp, s):
            s["tdt"] = _dot3(s["tinv"], s["dt"], "tn")

        def inverse_part_b(grp, s):
            s["da"] = jnp.where(strict, -_dot3(s["tdt"], s["tinv"], "nt"), 0.0)

        def finish(grp, s):
            loc, qs, ks, vs, bs, da, dp, dvby = s["loc"], s["qs"], s["ks"], s["vs"], s["bs"], s["da"], s["dp"], s["dvby"]
            gam, decay = loc["gam"], loc["decay"]
            dm = da * decay
            dn = dp * decay
            e = da * loc["a"] + dp * loc["p"]
            dgc = s["dgc"] + jnp.sum(e, axis=1, keepdims=True) - _row_to_col(jnp.sum(e, axis=0, keepdims=True), eye)
            dy = dvby[:, A_DK:]
            dvb = dvby[:, :A_DK]
            dkb = _dot1(dm, ks, "nn") + dy * gam
            dk = s["dk"] + _dot1(dm, loc["kb"], "tn") + _dot1(dn, qs, "tn") + dkb * bs
            dq = s["dq"] + _dot1(dn, ks, "nn")
            dgc = dgc + jnp.sum(dy * loc["y"], axis=-1, keepdims=True)
            dbeta = jnp.sum(dkb * ks, axis=-1, keepdims=True) + jnp.sum(dvb * vs, axis=-1, keepdims=True)
            dv = dvb * bs
            dgs = jnp.sum(jnp.where(mk["upper"], _col_to_row(dgc, eye), 0.0), axis=1, keepdims=True)
            for j in heads:
                h = grp * HEAD_GROUP + j
                sl = slice(h * A_DK, (h + 1) * A_DK)
                dq_ref[:, sl] = _head_rows(dq, j)
                dk_ref[:, sl] = _head_rows(dk, j)
                dv_ref[:, sl] = _head_rows(dv, j)
            s["dbeta"], s["dgs"] = dbeta, dgs

        for stage in (local_part, state_part, solve_part, inverse_part_a, inverse_part_b, finish):
            for grp in groups:
                stage(grp, st[grp])
        for grp in groups:
            for j in heads:
                h = grp * HEAD_GROUP + j
                dbeta_t = dbeta_t + jnp.where(lane == h, _head_rows(st[grp]["dbeta"], j), 0.0)
                dg_t = dg_t + jnp.where(lane == h, _head_rows(st[grp]["dgs"], j), 0.0)
        dbeta_ref[...] = dbeta_t
        dg_ref[...] = dg_t

    rev = lambda n: (n_chunks - 1 - n, 0)
    rev4 = lambda n: (n_chunks - 1 - n, 0, 0, 0)
    tile = pl.BlockSpec((CHUNK, A_W), rev)
    small = pl.BlockSpec((CHUNK, A_HEADS), rev)
    return pl.pallas_call(
        body, name="delta_bwd", grid=(n_chunks,),
        in_specs=[tile, tile, tile, small, small, pl.BlockSpec((1, A_HEADS, A_DK, A_DK), rev4),
                  pl.BlockSpec((1, N_HEAD_GROUPS, GROUP_ROWS, GROUP_ROWS), rev4), tile],
        out_specs=[tile, tile, tile, small, small],
        out_shape=[jax.ShapeDtypeStruct((S, A_W), F32)] * 3 + [jax.ShapeDtypeStruct((S, A_HEADS), F32)] * 2,
        scratch_shapes=[pltpu.VMEM((A_HEADS, A_DK, A_DK), F32)],
        compiler_params=_cparams(("arbitrary",)),
    )(q, k, v, beta, g, sprev, tinv, do)


def _prep_a_dpre(raw, raw_prev, w, dq, dk, dv):
    y, dy_dpre = _prep_a_core(raw, raw_prev, w)
    parts = []
    for h in range(A_HEADS):
        yq = y[:, h * A_DK:(h + 1) * A_DK]
        dqh = dq[:, h * A_DK:(h + 1) * A_DK]
        rq = lax.rsqrt(jnp.sum(yq * yq, axis=-1, keepdims=True) + L2_EPS)
        parts.append((A_DK ** -0.5) * (rq * dqh - yq * (rq * rq * rq) * jnp.sum(dqh * yq, axis=-1, keepdims=True)))
    for h in range(A_HEADS):
        yk = y[:, A_W + h * A_DK:A_W + (h + 1) * A_DK]
        dkh = dk[:, h * A_DK:(h + 1) * A_DK]
        rk = lax.rsqrt(jnp.sum(yk * yk, axis=-1, keepdims=True) + L2_EPS)
        parts.append(rk * dkh - yk * (rk * rk * rk) * jnp.sum(dkh * yk, axis=-1, keepdims=True))
    parts.append(dv)
    return jnp.concatenate(parts, axis=1) * dy_dpre


def prep_a_bwd(qkv_raw, ba, conv_a, a_log, dt_bias, dq, dk, dv, dbeta, dg):
    S = qkv_raw.shape[0]
    ts = 256

    def body(x_ref, xp_ref, xn_ref, ba_ref, w_ref, al_ref, dt_ref, dq_ref, dqn_ref, dk_ref, dkn_ref, dv_ref, dvn_ref,
             dbeta_ref, dg_ref, draw_ref, dba_ref, dw_ref, dal_ref, ddt_ref):
        i = pl.program_id(0)
        first = (i > 0).astype(F32)
        last = (i < pl.num_programs(0) - 1).astype(F32)
        w = w_ref[...]
        cur, prev = x_ref[...].astype(F32), _halo_prev(xp_ref) * first
        dpre = _prep_a_dpre(cur, prev, w, dq_ref[...], dk_ref[...], dv_ref[...])
        dpre_n = _prep_a_dpre(_halo_next(xn_ref), cur[ts - 8:ts], w, _halo_next(dqn_ref), _halo_next(dkn_ref),
                              _halo_next(dvn_ref)) * last
        for j in range(A_CONV):
            dw_ref[j:j + 1, :] += jnp.sum(dpre * _shift_down(cur, prev, A_CONV - 1 - j), axis=0, keepdims=True)
        draw = dpre * w[A_CONV - 1:A_CONV]
        for j in range(A_CONV - 1):
            draw = draw + _shift_up(dpre, dpre_n, A_CONV - 1 - j) * w[j:j + 1]
        draw_ref[...] = draw.astype(BF16)
        bav = ba_ref[...]
        beta = _sigmoid(bav[:, 0:A_HEADS])
        xa = bav[:, A_HEADS:2 * A_HEADS] + dt_ref[...]
        nexp = -jnp.exp(al_ref[...])
        dgv = dg_ref[...]
        da = dgv * nexp * _sigmoid(xa)
        dba_ref[:, 0:A_HEADS] = dbeta_ref[...] * beta * (1.0 - beta)
        dba_ref[:, A_HEADS:2 * A_HEADS] = da
        dal_ref[...] += jnp.sum(dgv * nexp * _softplus(xa), axis=0, keepdims=True)
        ddt_ref[...] += jnp.sum(da, axis=0, keepdims=True)

    return rowcall(
        body, name="prep_a_bwd", S=S, ts=ts,
        ins=[(qkv_raw, "row"), (qkv_raw, "prev"), (qkv_raw, "next"), (ba, "row"), (conv_a, "vec"), (a_log, "vec"),
             (dt_bias, "vec"), (dq, "row"), (dq, "next"), (dk, "row"), (dk, "next"), (dv, "row"), (dv, "next"),
             (dbeta, "row"), (dg, "row")],
        outs=[((S, 3 * A_W), BF16, "row"), ((S, 2 * A_HEADS), F32, "row"), ((A_CONV, 3 * A_W), F32, "acc"),
              ((1, A_HEADS), F32, "acc"), ((1, A_HEADS), F32, "acc")])


def grad_x_final(dh1, x, dxpre1, mod):
    S = x.shape[0]

    def body(dh_ref, x_ref, dx_ref, m_ref, gx_ref, dscale_ref, dshift_ref):
        dh = dh_ref[...]
        gx_ref[...] = ALPHA * dx_ref[...] + dh * (1.0 + m_ref[...][SCALE_T:SCALE_T + 1])
        dscale_ref[...] += jnp.sum(dh * x_ref[...], axis=0, keepdims=True)
        dshift_ref[...] += jnp.sum(dh, axis=0, keepdims=True)

    vec = ((1, D_MODEL), F32, "acc")
    return rowcall(body, name="grad_x_final", S=S, ts=512, ins=[(dh1, "row"), (x, "row"), (dxpre1, "row"), (mod, "vec")],
                   outs=[((S, D_MODEL), F32, "row"), vec, vec])


_C_QKV, _C_Z, _C_BA, _C_QKVB, _C_G = 0, 3 * A_W, 4 * A_W, 4 * A_W + 2 * A_HEADS, 4 * A_W + 2 * A_HEADS + 3 * B_W
BA_PAD = 128


def split_w_in(w_in):
    ba = jnp.pad(w_in[:, _C_BA:_C_QKVB], ((0, 0), (0, BA_PAD - 2 * A_HEADS)))
    return dict(qkv=w_in[:, _C_QKV:_C_Z], z=w_in[:, _C_Z:_C_BA], ba=ba, qkvb=w_in[:, _C_QKVB:_C_G], g=w_in[:, _C_G:])


def join_w_in(p):
    return jnp.concatenate([p["qkv"], p["z"], p["ba"][:, :2 * A_HEADS], p["qkvb"], p["g"]], axis=1)


def forward_local(x, target, mod, w, sm, late_weights=None):
    h1 = modulate(x, mod, SHIFT_T, SCALE_T, "mod_t")
    qkv_raw = mm(h1, w["qkv"], mode="nn", out_dtype=BF16, name="proj_qkv")
    z = mm(h1, w["z"], mode="nn", out_dtype=F32, name="proj_z")
    ba = mm(h1, w["ba"], mode="nn", out_dtype=F32, name="proj_ba")
    qkvb = mm(h1, w["qkvb"], mode="nn", out_dtype=BF16, name="proj_qkvb")
    gates_raw = mm(h1, w["g"], mode="nn", out_dtype=F32, name="proj_g")
    q, k, v, beta, g = prep_a_fwd(qkv_raw, ba, sm["conv_a"], sm["a_log"], sm["dt_bias"])
    o_pre, sprev, tinv = delta_fwd(q, k, v, beta, g)
    o_a = gate_a_fwd(o_pre, z, sm["norm_a"])
    qkv_pad = jnp.pad(qkvb, ((PAD_ROWS, 0), (0, 0)))
    bias = jnp.transpose(bias_expand(sm["rel_bias"]), (1, 0, 2))
    o_b = attn_fwd(qkv_pad, bias)
    if late_weights is not None:
        w = dict(w, **late_weights(o_b))
    ya = mm(o_a, w["branch_a"], mode="nn", out_dtype=F32, name="branch_a")
    yb = mm(o_b, w["branch_b"], mode="nn", out_dtype=F32, name="branch_b")
    merged = merge_fwd(gates_raw, sm["b_gate"], ya, yb)
    mix = mm(merged, w["o"], mode="nn", out_dtype=F32, name="mix")
    xpre1, x1, h2 = ln1_fwd(x, mix, mod, sm["ln1_g"], sm["ln1_b"])
    up = mm(h2, w["up"], mode="nn", out_dtype=BF16, name="ffn_up")
    act = ffn_act_fwd(up, sm["conv_ffn"], sm["b_conv_ffn"])
    ffn = mm(act, w["down"], mode="nn", out_dtype=F32, name="ffn_down")
    dxpre2, dffn, loss, dgate_f, dln2_g, dln2_b = final_fwd_bwd(x1, ffn, target, mod, sm["ln2_g"], sm["ln2_b"])
    saved = dict(h1=h1, qkv_raw=qkv_raw, z=z, ba=ba, gates_raw=gates_raw, q=q, k=k, v=v, beta=beta, g=g,
                 o_pre=o_pre, sprev=sprev, tinv=tinv, o_a=o_a, qkv_pad=qkv_pad, bias=bias, o_b=o_b, ya=ya, yb=yb,
                 merged=merged, mix=mix, xpre1=xpre1, x1=x1, h2=h2, up=up, act=act, ffn=ffn, w=w)
    return loss, dxpre2, dffn, dict(gate_f=dgate_f, ln2_g=dln2_g, ln2_b=dln2_b), saved


def backward_local(x, mod, sm, dxpre2, dffn, fin, sv, early_grads=None, early_w_in=None):
    w = sv["w"]
    dact = mm(dffn, w["down"], mode="nt", out_dtype=BF16, name="d_act")
    gw_down = mm(sv["act"], dffn, mode="tn", out_dtype=BF16, name="gw_down")
    dup, dconv_ffn, db_conv_ffn = ffn_act_bwd(dact, sv["up"], sm["conv_ffn"], sm["b_conv_ffn"])
    dh2 = mm(dup, w["up"], mode="nt", out_dtype=F32, name="d_h2")
    gw_up = mm(sv["h2"], dup, mode="tn", out_dtype=BF16, name="gw_up")
    dxpre1, dmix, dsc_f, dsh_f, dgate_t, dln1_g, dln1_b = ln1_bwd(
        dxpre2, dh2, sv["xpre1"], sv["mix"], mod, sm["ln1_g"], sm["ln1_b"])
    dmerged = mm(dmix, w["o"], mode="nt", out_dtype=F32, name="d_merged")
    gw_o = mm(sv["merged"], dmix, mode="tn", out_dtype=BF16, name="gw_o")
    dya, dyb, dgates, db_gate = merge_bwd(dmerged, sv["gates_raw"], sm["b_gate"], sv["ya"], sv["yb"])
    do_a = mm(dya, w["branch_a"], mode="nt", out_dtype=F32, name="d_oa")
    gw_branch_a = mm(sv["o_a"], dya, mode="tn", out_dtype=BF16, name="gw_branch_a")
    do_b = mm(dyb, w["branch_b"], mode="nt", out_dtype=BF16, name="d_ob")
    gw_branch_b = mm(sv["o_b"], dyb, mode="tn", out_dtype=BF16, name="gw_branch_b")
    bias = sv["bias"]
    if early_grads is not None:
        bias = bias + early_grads(dict(w_branch_a=gw_branch_a, w_branch_b=gw_branch_b, w_o=gw_o, w_up=gw_up,
                                       w_down=gw_down))[0, 0]
    dq_b, dk_pad, dv_pad, dbias = attn_bwd(sv["qkv_pad"], bias, do_b)
    dqkvb = jnp.concatenate([dq_b, dk_pad[PAD_ROWS:].astype(BF16), dv_pad[PAD_ROWS:].astype(BF16)], axis=1)
    drel_bias = bias_reduce(jnp.transpose(dbias, (1, 0, 2)))
    do_pre, dz, dnorm_a = gate_a_bwd(do_a, sv["o_pre"], sv["z"], sm["norm_a"])
    dq, dk, dv, dbeta, dg = delta_bwd(sv["q"], sv["k"], sv["v"], sv["beta"], sv["g"], sv["sprev"], sv["tinv"], do_pre)
    dqkv_raw, dba16, dconv_a, da_log, ddt_bias = prep_a_bwd(
        sv["qkv_raw"], sv["ba"], sm["conv_a"], sm["a_log"], sm["dt_bias"], dq, dk, dv, dbeta, dg)
    dba = jnp.pad(dba16, ((0, 0), (0, BA_PAD - 2 * A_HEADS))).astype(BF16)
    pieces = dict(qkv=dqkv_raw, z=dz, ba=dba, qkvb=dqkvb, g=dgates)
    gw_in = join_w_in({key: mm(sv["h1"], dpiece, mode="tn", out_dtype=BF16, name="gw_in_" + key)
                       for key, dpiece in pieces.items()})
    w_ba = w["ba"]
    if early_w_in is not None:
        w_ba = w_ba + early_w_in(gw_in)[0, 0].astype(BF16)
    dh1 = mm(pieces["ba"], w_ba, mode="nt", out_dtype=F32, name="d_h1_ba")
    for key in ("qkv", "z", "qkvb", "g"):
        dh1 = mm(pieces[key], w[key], mode="nt", out_dtype=F32, name="d_h1_" + key, acc_in=dh1)
    grad_x, dsc_t, dsh_t = grad_x_final(dh1, x, dxpre1, mod)
    dmod = jnp.concatenate([dsh_t, dsc_t, dgate_t, dsh_f, dsc_f, fin["gate_f"]], axis=0)
    gw = dict(w_in=gw_in, w_branch_a=gw_branch_a, w_branch_b=gw_branch_b, w_o=gw_o, w_up=gw_up, w_down=gw_down)
    gs = dict(b_gate=db_gate, conv_a=dconv_a, a_log=da_log, dt_bias=ddt_bias, norm_a=dnorm_a, rel_bias=drel_bias,
              ln1_g=dln1_g, ln1_b=dln1_b, conv_ffn=dconv_ffn, b_conv_ffn=db_conv_ffn, ln2_g=fin["ln2_g"], ln2_b=fin["ln2_b"])
    return grad_x, dmod, gw, gs


MESH = pl.DeviceIdType.MESH
ANY = pl.BlockSpec(memory_space=pl.ANY)
WHOLE_VMEM = pl.BlockSpec(memory_space=pltpu.VMEM)


def _place():
    return lax.axis_index("x"), lax.axis_index("y"), lax.axis_index("c")


def allgather8(blk, name):
    m_per, n = blk.shape

    def body(x_ref, out_ref, send_sems, recv_sems, local_sem):
        x, y, c = _place()
        me, sibling = (x, y, c), (x, y, 1 - c)
        chips = [(1 - x, y), (x, 1 - y), (1 - x, 1 - y)]

        def rows(px, py, pc):
            return out_ref.at[pl.ds((4 * px + 2 * py + pc) * m_per, m_per), :]

        def copy(k, block, to, src=None):
            return pltpu.make_async_remote_copy(
                src_ref=rows(*block) if src is None else src, dst_ref=rows(*block),
                send_sem=send_sems.at[k], recv_sem=recv_sems.at[k], device_id=to, device_id_type=MESH)

        mine = pltpu.make_async_copy(x_ref, rows(*me), local_sem)
        mine.start()
        first = [copy(0, me, sibling, src=x_ref)]
        first += [copy(1 + j, me, (*chip, c), src=x_ref) for j, chip in enumerate(chips)]
        for cp in first:
            cp.start()
        passed = [copy(4 + j, (*chip, c), sibling) for j, chip in enumerate(chips)]
        for j, chip in enumerate(chips):
            copy(1 + j, (*chip, c), me).wait_recv()
            passed[j].start()
        copy(0, sibling, me).wait_recv()
        for j, chip in enumerate(chips):
            copy(4 + j, (*chip, 1 - c), me).wait_recv()
        for cp in first + passed:
            cp.wait_send()
        mine.wait()

    return pl.pallas_call(
        body, name=name, out_shape=jax.ShapeDtypeStruct((N_DEV * m_per, n), blk.dtype),
        in_specs=[WHOLE_VMEM], out_specs=WHOLE_VMEM,
        scratch_shapes=[pltpu.SemaphoreType.DMA((7,)), pltpu.SemaphoreType.DMA((7,)), pltpu.SemaphoreType.DMA],
    )(blk)


def _chip_peers(x, y):
    return [(1 - x, y), (x, 1 - y), (1 - x, 1 - y)]


def chip_exchange(arrs, name, scatter):
    n = len(arrs)

    def body(*refs):
        ins, outs = refs[:n], refs[n:2 * n]
        send_sems, recv_sems, local_sems = refs[2 * n:]
        x, y, c = _place()
        me = 2 * x + y
        sibling = (x, y, 1 - c)
        peers = _chip_peers(x, y)

        def half(ref, which):
            r2 = ref.shape[0] // 2
            return ref.at[pl.ds(which * r2, r2), :]

        def outgoing(a, chip):
            return ins[a].at[chip] if scatter else ins[a]

        def copy(k, src, dst, to):
            return pltpu.make_async_remote_copy(src_ref=src, dst_ref=dst, send_sem=send_sems.at[k],
                                                recv_sem=recv_sems.at[k], device_id=to, device_id_type=MESH)

        started, local = [], []
        for a in range(n):
            lc = pltpu.make_async_copy(outgoing(a, me), outs[a].at[me], local_sems.at[a])
            lc.start()
            local.append(lc)
            for j, (px, py) in enumerate(peers):
                cp = copy(6 * a + j, half(outgoing(a, 2 * px + py), c), half(outs[a].at[me], c), (px, py, c))
                cp.start()
                started.append(cp)
        for a in range(n):
            for j, (px, py) in enumerate(peers):
                landed = half(outs[a].at[2 * px + py], c)
                copy(6 * a + j, landed, landed, (px, py, c)).wait_recv()
                relay = copy(6 * a + 3 + j, landed, landed, sibling)
                relay.start()
                started.append(relay)
        for a in range(n):
            for j, (px, py) in enumerate(peers):
                other = half(outs[a].at[2 * px + py], 1 - c)
                copy(6 * a + 3 + j, other, other, sibling).wait_recv()
        for cp in started:
            cp.wait_send()
        for lc in local:
            lc.wait()

    out_shape = [jax.ShapeDtypeStruct(a.shape if scatter else (N_CHIPS,) + a.shape, a.dtype) for a in arrs]
    return pl.pallas_call(
        body, name=name, out_shape=out_shape, in_specs=[ANY] * n, out_specs=[ANY] * n,
        scratch_shapes=[pltpu.SemaphoreType.DMA((6 * n,)), pltpu.SemaphoreType.DMA((6 * n,)), pltpu.SemaphoreType.DMA((n,))],
    )(*arrs)


HBM_SPEC = pl.BlockSpec(memory_space=pltpu.HBM)
SEM_SPEC = pl.BlockSpec(memory_space=pltpu.SEMAPHORE)
SIDE_EFFECT = pltpu.SideEffectType.DATAFLOW_SIDE_EFFECTING


def _in_hbm(a):
    return pltpu.with_memory_space_constraint(a, pltpu.HBM)


def exchange_start(arrs, name, scatter, after):
    n = len(arrs)
    lands = [lax.empty(a.shape if scatter else (N_CHIPS,) + a.shape, a.dtype) for a in arrs]

    def body(*refs):
        ins, zones = refs[:n], refs[n:2 * n]
        send_sems, recv_sems, token = refs[2 * n + 1], refs[2 * n + 2], refs[-1]
        x, y, c = _place()
        me = 2 * x + y
        for a in range(n):
            for j, (px, py) in enumerate(_chip_peers(x, y)):
                pltpu.make_async_remote_copy(
                    src_ref=ins[a].at[2 * px + py] if scatter else ins[a], dst_ref=zones[a].at[me],
                    send_sem=send_sems.at[3 * a + j], recv_sem=recv_sems.at[3 * a + j],
                    device_id=(px, py, c), device_id_type=MESH).start()
        token[...] = jnp.zeros_like(token)

    res = pl.pallas_call(
        body, name=name,
        out_shape=[pltpu.SemaphoreType.DMA((3 * n,)), pltpu.SemaphoreType.DMA((3 * n,))]
        + [pltpu.HBM(a.shape, a.dtype) for a in arrs] + [pltpu.HBM(z.shape, z.dtype) for z in lands]
        + [jax.ShapeDtypeStruct((8, 128), F32)],
        in_specs=[HBM_SPEC] * (2 * n) + [ANY], out_specs=[SEM_SPEC, SEM_SPEC] + [HBM_SPEC] * (2 * n) + [WHOLE_VMEM],
        input_output_aliases={i: 2 + i for i in range(2 * n)},
        compiler_params=pltpu.CompilerParams(has_side_effects=SIDE_EFFECT),
    )(*[_in_hbm(a) for a in arrs], *[_in_hbm(z) for z in lands], after)
    return dict(send=res[0], recv=res[1], src=res[2:2 + n], zones=res[2 + n:2 + 2 * n], token=res[-1], scatter=scatter)


def exchange_wait(handle, name, after):
    srcs, zones, scatter = handle["src"], handle["zones"], handle["scatter"]
    n = len(srcs)

    def body(*refs):
        ins, lands = refs[:n], refs[n:2 * n]
        send_sems, recv_sems = refs[2 * n], refs[2 * n + 1]
        x, y, c = _place()
        me = 2 * x + y
        for a in range(n):
            for j, (px, py) in enumerate(_chip_peers(x, y)):
                cp = pltpu.make_async_remote_copy(
                    src_ref=ins[a].at[me] if scatter else ins[a], dst_ref=lands[a].at[2 * px + py],
                    send_sem=send_sems.at[3 * a + j], recv_sem=recv_sems.at[3 * a + j],
                    device_id=(px, py, c), device_id_type=MESH)
                cp.wait_send()
                cp.wait_recv()

    res = pl.pallas_call(
        body, name=name, out_shape=[pltpu.HBM(a.shape, a.dtype) for a in list(srcs) + list(zones)],
        in_specs=[HBM_SPEC] * (2 * n) + [SEM_SPEC, SEM_SPEC, ANY], out_specs=[HBM_SPEC] * (2 * n),
        input_output_aliases={i: i for i in range(2 * n)},
        compiler_params=pltpu.CompilerParams(has_side_effects=SIDE_EFFECT),
    )(*srcs, *zones, handle["send"], handle["recv"], after)
    return res[n:]


def sibling_exchange(arrs, name):
    n = len(arrs)

    def body(*refs):
        ins, outs = refs[:n], refs[n:2 * n]
        send_sems, recv_sems = refs[2 * n:]
        x, y, c = _place()
        cps = [pltpu.make_async_remote_copy(src_ref=ins[a], dst_ref=outs[a], send_sem=send_sems.at[a],
                                            recv_sem=recv_sems.at[a], device_id=(x, y, 1 - c), device_id_type=MESH)
               for a in range(n)]
        for cp in cps:
            cp.start()
        for cp in cps:
            cp.wait()

    return pl.pallas_call(
        body, name=name, out_shape=[jax.ShapeDtypeStruct(a.shape, a.dtype) for a in arrs],
        in_specs=[ANY] * n, out_specs=[ANY] * n,
        scratch_shapes=[pltpu.SemaphoreType.DMA((n,)), pltpu.SemaphoreType.DMA((n,))],
    )(*arrs)


TILE_BYTES = 2 * 1024 * 1024


def _row_tile(rows, row_bytes):
    if rows * row_bytes <= TILE_BYTES or rows % 8:
        return rows
    best = 8
    for t in range(8, rows + 1, 8):
        if rows % t == 0 and t * row_bytes <= TILE_BYTES:
            best = t
    return best


def pair_add(a, b, name):
    R, C = a.shape
    tr = _row_tile(R, C * 4)

    def body(a_ref, b_ref, o_ref):
        o_ref[...] = (a_ref[...].astype(F32) + b_ref[...].astype(F32)).astype(BF16)

    spec = pl.BlockSpec((tr, C), lambda i: (i, 0))
    return pl.pallas_call(body, name=name, grid=(R // tr,), in_specs=[spec, spec], out_specs=spec,
                          out_shape=jax.ShapeDtypeStruct((R, C), BF16), compiler_params=_cparams(("parallel",)))(a, b)


def sum_lead(parts, name):
    K, R, C = parts.shape
    tr = _row_tile(R, C * 4)

    def body(p_ref, o_ref):
        acc = p_ref[0].astype(F32)
        for j in range(1, K):
            acc = acc + p_ref[j].astype(F32)
        o_ref[...] = acc

    return pl.pallas_call(
        body, name=name, grid=(R // tr,), in_specs=[pl.BlockSpec((K, tr, C), lambda i: (0, i, 0))],
        out_specs=pl.BlockSpec((tr, C), lambda i: (i, 0)), out_shape=jax.ShapeDtypeStruct((R, C), F32),
        compiler_params=_cparams(("parallel",)))(parts)


def adamw(w, g, m, v, name):
    R, C = w.shape
    tr = _row_tile(R, C * 4)

    def body(w_ref, g_ref, m_ref, v_ref, d_ref, mo_ref, vo_ref):
        gv = g_ref[...]
        m2 = ADAM_B1 * m_ref[...] + (1.0 - ADAM_B1) * gv
        v2 = ADAM_B2 * v_ref[...] + (1.0 - ADAM_B2) * (gv * gv)
        m_hat = m2 / (1.0 - ADAM_B1 ** ADAM_STEP)
        v_hat = v2 / (1.0 - ADAM_B2 ** ADAM_STEP)
        d_ref[...] = -ADAM_LR * (m_hat / (jnp.sqrt(v_hat) + ADAM_EPS) + ADAM_WD * w_ref[...])
        mo_ref[...] = m2
        vo_ref[...] = v2

    spec = pl.BlockSpec((tr, C), lambda i: (i, 0))
    return pl.pallas_call(body, name=name, grid=(R // tr,), in_specs=[spec] * 4, out_specs=[spec] * 3,
                          out_shape=[jax.ShapeDtypeStruct((R, C), F32)] * 3, compiler_params=_cparams(("parallel",)))(w, g, m, v)


LANES = 1024


def _pack(arrs, rows):
    out, offs, r = [], [], 0
    for a in arrs:
        flat = a.reshape(-1)
        nr = -(-flat.shape[0] // LANES)
        out.append(jnp.pad(flat, (0, nr * LANES - flat.shape[0])))
        offs.append(r)
        r += nr
    assert r <= rows, (r, rows)
    out.append(jnp.zeros(((rows - r) * LANES,), F32))
    return jnp.concatenate(out).reshape(rows, LANES), offs


def _unpack(packed, offs, shapes):
    flat = packed.reshape(-1)
    return [flat[o * LANES:o * LANES + math.prod(s)].reshape(s) for o, s in zip(offs, shapes)]


WEIGHTS = ["w_ada", "b_ada", "w_in", "b_gate", "conv_a", "a_log", "dt_bias", "norm_a", "rel_bias", "w_branch_a",
           "w_branch_b", "w_o", "ln1_g", "ln1_b", "w_up", "conv_ffn", "b_conv_ffn", "w_down", "ln2_g", "ln2_b"]
BIG = ["w_in", "w_branch_a", "w_branch_b", "w_o", "w_up", "w_down"]
LATE = [n for n in BIG if n != "w_in"]
COL_SHARDED = {"w_in", "w_up"}
SMALL_SHARDED = {"conv_a": 3 * A_W // N_CHIPS, "rel_bias": B_REL // N_CHIPS, "conv_ffn": 2 * D_FF // N_CHIPS}
SMALL = [n for n in WEIGHTS if n not in BIG and n != "w_ada"]


def _to_full(g4, name):
    if name in COL_SHARDED:
        return jnp.transpose(g4, (1, 0, 2)).reshape(g4.shape[1], -1)
    return g4.reshape(-1, g4.shape[2])


def _to_shards(full, name):
    if name in COL_SHARDED:
        return jnp.transpose(full.reshape(full.shape[0], N_CHIPS, -1), (1, 0, 2))
    return full.reshape(N_CHIPS, -1, full.shape[1])


def kernel(x, c, w_ada, b_ada, w_in, b_gate, conv_a, a_log, dt_bias, norm_a, rel_bias, w_branch_a, w_branch_b, w_o, ln1_g, ln1_b, w_up, conv_ffn, b_conv_ffn, w_down, ln2_g, ln2_b, loss_target, m_w_ada, m_b_ada, m_w_in, m_b_gate, m_conv_a, m_a_log, m_dt_bias, m_norm_a, m_rel_bias, m_w_branch_a, m_w_branch_b, m_w_o, m_ln1_g, m_ln1_b, m_w_up, m_conv_ffn, m_b_conv_ffn, m_w_down, m_ln2_g, m_ln2_b, v_w_ada, v_b_ada, v_w_in, v_b_gate, v_conv_a, v_a_log, v_dt_bias, v_norm_a, v_rel_bias, v_w_branch_a, v_w_branch_b, v_w_o, v_ln1_g, v_ln1_b, v_w_up, v_conv_ffn, v_b_conv_ffn, v_w_down, v_ln2_g, v_ln2_b):
    args = dict(locals())
    wts = {n: args[n] for n in WEIGHTS}
    moms = {n: args["m_" + n] for n in WEIGHTS}
    vars_ = {n: args["v_" + n] for n in WEIGHTS}
    xi, yi, ci = _place()
    chip = 2 * xi + yi
    dev = 4 * xi + 2 * yi + ci
    ada_cols = w_ada.shape[2]

    c_all = allgather8(jnp.pad(c, ((0, 7), (0, 0))), "gather_c").reshape(N_DEV, 8, D_MODEL)[:, 0]
    b_ada_sh = lax.dynamic_slice(b_ada, (0, chip * ada_cols), (1, ada_cols))
    mod_sh = ada_fwd(c_all, w_ada[0], b_ada_sh)
    mod_g = allgather8(mod_sh, "gather_mod").reshape(N_CHIPS, 2, N_DEV, ada_cols)[:, 0]
    mod = lax.dynamic_slice(mod_g, (0, dev, 0), (N_CHIPS, 1, ada_cols)).reshape(6, D_MODEL)

    (w_in_g4,) = chip_exchange([wts["w_in"][0].astype(BF16)], "gather_w_in", scatter=False)
    wd = split_w_in(_to_full(w_in_g4, "w_in"))
    late_shards = [wts[n][0].astype(BF16) for n in LATE]
    late_gather = exchange_start(late_shards, "gather_late_start", scatter=False, after=w_in_g4)
    mod = mod + late_gather["token"][0, 0]

    def late_weights(after):
        zones = exchange_wait(late_gather, "gather_late_wait", after)
        full = [_to_full(lax.dynamic_update_slice(z, s[None], (chip, 0, 0)), n) for n, z, s in zip(LATE, zones, late_shards)]
        return {n[2:]: f for n, f in zip(LATE, full)}

    sshapes = [wts[n].shape[1:] for n in SMALL_SHARDED]
    spack, soffs = _pack([wts[n][0] for n in SMALL_SHARDED], 16)
    sg = allgather8(spack, "gather_small_w").reshape(N_CHIPS, 2, 16, LANES)[:, 0]
    sparts = [_unpack(sg[j], soffs, sshapes) for j in range(N_CHIPS)]
    sm = {n: wts[n] for n in SMALL if n not in SMALL_SHARDED and n != "b_ada"}
    for i, n in enumerate(SMALL_SHARDED):
        sm[n] = jnp.concatenate([sparts[j][i] for j in range(N_CHIPS)], axis=-1)

    early = {}

    def early_grads(g):
        mine = [g[n] for n in LATE]
        theirs = sibling_exchange(mine, "grad_sibling_late")
        early["sums"] = [_to_shards(pair_add(a, b, "grad_pair_" + n), n) for n, a, b in zip(LATE, mine, theirs)]
        early["scatter"] = exchange_start(early["sums"], "grad_scatter_start", scatter=True, after=theirs[0])
        return early["scatter"]["token"]

    def early_w_in(g):
        (theirs,) = sibling_exchange([g], "grad_sibling_w_in")
        early["sum_in"] = _to_shards(pair_add(g, theirs, "grad_pair_w_in"), "w_in")
        early["scatter_in"] = exchange_start([early["sum_in"]], "grad_scatter_w_in_start", scatter=True, after=theirs)
        return early["scatter_in"]["token"]

    loss, dxpre2, dffn, fin, sv = forward_local(x[0], loss_target[0], mod, wd, sm, late_weights)
    grad_x, dmod, gw, gs = backward_local(x[0], mod, sm, dxpre2, dffn, fin, sv, early_grads, early_w_in)

    gnames = [n for n in SMALL if n != "b_ada"]
    vec, voffs = _pack([dmod] + [gs[n] for n in gnames] + [loss], 56)
    gathered = allgather8(vec, "gather_small_g").reshape(N_DEV, 56, LANES)
    summed = sum_lead(gathered, "sum_small_g")
    full_shapes = [(6, D_MODEL)] + [gs[n].shape for n in gnames] + [(1, 1)]
    parts = _unpack(summed, voffs, full_shapes)
    grads = {"b_ada": parts[0].reshape(1, -1)}
    for n, p in zip(gnames, parts[1:-1]):
        if n in SMALL_SHARDED:
            p = lax.dynamic_slice_in_dim(p, chip * SMALL_SHARDED[n], SMALL_SHARDED[n], axis=1)
        grads[n] = p.reshape(wts[n].shape)
    loss_total = parts[-1].reshape(())
    dmod_all = gathered[:, 0:6, :].reshape(N_DEV, 6 * D_MODEL)
    grads["w_ada"] = ada_bwd(c_all, lax.dynamic_slice(dmod_all, (0, chip * ada_cols), (N_DEV, ada_cols)))[None]

    def own_slot(zone, sums):
        return lax.dynamic_update_slice(zone, lax.dynamic_slice_in_dim(sums, chip, 1, axis=0), (chip, 0, 0))

    zones = exchange_wait(early["scatter"], "grad_scatter_wait", summed)
    for n, z, s in zip(LATE, zones, early["sums"]):
        grads[n] = sum_lead(own_slot(z, s), "grad_sum_" + n)[None]

    delta, new_m, new_v = {}, {}, {}

    def update(n):
        d, m2, v2 = adamw(wts[n][0], grads[n][0], moms[n][0], vars_[n][0], "adamw_" + n)
        delta[n], new_m[n], new_v[n] = d[None], m2[None], v2[None]

    for n in ["w_ada"] + LATE:
        update(n)
    shapes = [wts[n].shape for n in SMALL]
    packs = [_pack([t[n] for n in SMALL], 32) for t in (wts, grads, moms, vars_)]
    outs = adamw(*[p[0] for p in packs], "adamw_small")
    for res, o in zip((delta, new_m, new_v), outs):
        for n, a in zip(SMALL, _unpack(o, packs[0][1], shapes)):
            res[n] = a
    (zone_in,) = exchange_wait(early["scatter_in"], "grad_scatter_w_in_wait", outs[0])
    grads["w_in"] = sum_lead(own_slot(zone_in, early["sum_in"]), "grad_sum_w_in")[None]
    update("w_in")
    return (loss_total, grad_x[None], *[grads[n] for n in WEIGHTS], *[delta[n] for n in WEIGHTS],
            *[new_m[n] for n in WEIGHTS], *[new_v[n] for n in WEIGHTS])
```

```python
import functools
import math

import jax
import jax.numpy as jnp
from jax import lax
from jax.experimental import pallas as pl
from jax.experimental.pallas import tpu as pltpu

F32 = jnp.float32
BF16 = jnp.bfloat16

D_MODEL = 1024
CHUNK = 64
A_HEADS = 8
A_DK = 128
A_W = A_HEADS * A_DK
A_CONV = 4
B_HEADS = 16
B_DH = 64
B_W = B_HEADS * B_DH
B_PREV = 8
B_BAND = (B_PREV + 1) * CHUNK
B_MAX_REL = 256
B_REL = CHUNK - 1 + B_MAX_REL + 1
D_FF = 2816
FFN_CONV = 3
IN_COLS = 4 * A_W + 2 * A_HEADS + 3 * B_W + 2 * D_MODEL
ALPHA = 2.0 ** 0.25
LN_EPS = 1e-5
RMS_EPS = 1e-6
L2_EPS = 1e-6
NEG_INF = -1e30
ADAM_LR, ADAM_B1, ADAM_B2, ADAM_EPS, ADAM_WD, ADAM_STEP = 0.001, 0.9, 0.999, 1e-08, 0.01, 10
N_CHIPS = 4
N_DEV = 8
VMEM_LIMIT = 56 * 1024 * 1024


def _cparams(sem=None):
    return pltpu.CompilerParams(dimension_semantics=sem, vmem_limit_bytes=VMEM_LIMIT)


_DIMS = {"nn": (((1,), (0,)), ((), ())), "nt": (((1,), (1,)), ((), ())), "tn": (((0,), (0,)), ((), ()))}


MM_TILE_CAP = 1408


def _mm_tile(n):
    return max(t for t in range(128, min(n, MM_TILE_CAP) + 1, 128) if n % t == 0)


def mm(a, b, *, mode, out_dtype, name, acc_in=None, b_shards=False, out_shards=0):
    b_rows, b_cols = (b.shape[1], b.shape[0] * b.shape[2]) if b_shards else b.shape
    if mode == "nn":
        (M, K), (K2, N) = a.shape, (b_rows, b_cols)
    elif mode == "nt":
        (M, K), (N, K2) = a.shape, (b_rows, b_cols)
    else:
        (K, M), (K2, N) = a.shape, (b_rows, b_cols)
    assert K == K2, (a.shape, b.shape, mode)
    tm, tn, tk = _mm_tile(M), _mm_tile(N), _mm_tile(K)
    nk = K // tk

    def body(*refs):
        if acc_in is None:
            a_ref, b_ref, o_ref, acc_ref = refs
        else:
            a_ref, b_ref, c_ref, o_ref, acc_ref = refs
        k = pl.program_id(2)

        @pl.when(k == 0)
        def _():
            if acc_in is None:
                acc_ref[...] = jnp.zeros_like(acc_ref)
            else:
                acc_ref[...] = c_ref[...]

        acc_ref[...] += lax.dot_general(a_ref[...].astype(BF16), b_ref[...].astype(BF16), _DIMS[mode],
                                        preferred_element_type=F32)

        @pl.when(k == nk - 1)
        def _():
            o_ref[...] = acc_ref[...].astype(out_dtype)

    a_spec = pl.BlockSpec((tk, tm), lambda i, j, k: (k, i)) if mode == "tn" else pl.BlockSpec((tm, tk), lambda i, j, k: (i, k))
    if b_shards:
        assert (tk if mode == "nt" else tn) == b.shape[2] and mode != "tn", (b.shape, tn, tk, mode)
        b_spec = (pl.BlockSpec((None, tn, tk), lambda i, j, k: (k, j, 0)) if mode == "nt"
                  else pl.BlockSpec((None, tk, tn), lambda i, j, k: (j, k, 0)))
    else:
        b_spec = pl.BlockSpec((tn, tk), lambda i, j, k: (j, k)) if mode == "nt" else pl.BlockSpec((tk, tn), lambda i, j, k: (k, j))
    o_spec = pl.BlockSpec((tm, tn), lambda i, j, k: (i, j))
    out_shape = jax.ShapeDtypeStruct((M, N), out_dtype)
    if out_shards:
        assert N == out_shards * tn and acc_in is None, (N, tn, out_shards)
        o_spec = pl.BlockSpec((None, tm, tn), lambda i, j, k: (j, i, 0))
        out_shape = jax.ShapeDtypeStruct((out_shards, M, tn), out_dtype)
    ins, in_specs, aliases = [a, b], [a_spec, b_spec], {}
    if acc_in is not None:
        assert acc_in.shape == (M, N) and acc_in.dtype == F32 and out_dtype == F32
        ins.append(acc_in)
        in_specs.append(o_spec)
        aliases = {2: 0}
    return pl.pallas_call(
        body, name=name, grid=(M // tm, N // tn, nk), in_specs=in_specs, out_specs=o_spec,
        out_shape=out_shape, scratch_shapes=[pltpu.VMEM((tm, tn), F32)],
        input_output_aliases=aliases, compiler_params=_cparams(("parallel", "parallel", "arbitrary")),
    )(*ins)


def rowcall(body, *, name, S, ts, ins, outs, scratch=()):
    assert S % ts == 0 and ts % 16 == 0
    nsteps = S // ts
    in_specs, arrays = [], []
    for arr, kind in ins:
        arrays.append(arr)
        if kind == "row":
            in_specs.append(pl.BlockSpec((ts, arr.shape[1]), lambda i: (i, 0)))
        elif kind in ("prev", "next"):
            hr = 8 * (4 // arr.dtype.itemsize)
            per, last = ts // hr, S // hr - 1
            if kind == "prev":
                in_specs.append(pl.BlockSpec((hr, arr.shape[1]), lambda i, per=per: (jnp.maximum(i * per - 1, 0), 0)))
            else:
                in_specs.append(pl.BlockSpec((hr, arr.shape[1]), lambda i, per=per, last=last: (jnp.minimum((i + 1) * per, last), 0)))
        else:
            nd = arr.ndim
            in_specs.append(pl.BlockSpec(arr.shape, lambda i, nd=nd: (0,) * nd))
    out_specs, out_shapes, acc_idx = [], [], []
    for n, (shape, dtype, kind) in enumerate(outs):
        out_shapes.append(jax.ShapeDtypeStruct(shape, dtype))
        if kind == "row":
            out_specs.append(pl.BlockSpec((ts, shape[1]), lambda i: (i, 0)))
        else:
            nd = len(shape)
            out_specs.append(pl.BlockSpec(shape, lambda i, nd=nd: (0,) * nd))
            acc_idx.append(n)
    n_in = len(arrays)

    def wrapped(*refs):
        @pl.when(pl.program_id(0) == 0)
        def _():
            for n in acc_idx:
                refs[n_in + n][...] = jnp.zeros_like(refs[n_in + n])

        body(*refs)

    res = pl.pallas_call(
        wrapped, name=name, grid=(nsteps,), in_specs=in_specs, out_specs=out_specs, out_shape=out_shapes,
        scratch_shapes=list(scratch), compiler_params=_cparams(("arbitrary",) if acc_idx else ("parallel",)),
    )(*arrays)
    return res


def _halo_prev(ref):
    v = ref[...].astype(F32)
    return v[v.shape[0] - 8:]


def _halo_next(ref):
    return ref[...].astype(F32)[:8]


def _shift_down(cur, prev8, k):
    if k == 0:
        return cur
    rolled = pltpu.roll(cur, k, axis=0)
    fix = pltpu.roll(prev8, k, axis=0)
    row = lax.broadcasted_iota(jnp.int32, (8, 1), 0)
    top = jnp.where(row < k, fix, rolled[0:8])
    if cur.shape[0] == 8:
        return top
    return jnp.concatenate([top, rolled[8:]], axis=0)


def _shift_up(cur, next8, k):
    if k == 0:
        return cur
    n = cur.shape[0]
    rolled = pltpu.roll(cur, n - k, axis=0)
    fix = pltpu.roll(next8, 8 - k, axis=0)
    row = lax.broadcasted_iota(jnp.int32, (8, 1), 0)
    bot = jnp.where(row >= 8 - k, fix, rolled[n - 8:n])
    return jnp.concatenate([rolled[:n - 8], bot], axis=0)


def _sigmoid(x):
    return 1.0 / (1.0 + jnp.exp(-x))


def _silu(x):
    return x * _sigmoid(x)


def _silu_and_grad(x):
    s = _sigmoid(x)
    return x * s, s * (1.0 + x * (1.0 - s))


def _softplus(x):
    return jnp.maximum(x, 0.0) + jnp.log1p(jnp.exp(-jnp.abs(x)))


def _split2(x):
    hi = x.astype(BF16)
    return hi, (x - hi.astype(F32)).astype(BF16)


def _dot1(a, b, mode):
    return lax.dot_general(a.astype(BF16), b.astype(BF16), _DIMS[mode], preferred_element_type=F32)


def _dot3(a, b, mode):
    ah, al = _split2(a)
    bh, bl = _split2(b)
    d = lambda p, q: lax.dot_general(p, q, _DIMS[mode], preferred_element_type=F32)
    return d(ah, bh) + (d(ah, bl) + d(al, bh))


def ada_fwd(c_all, w_sh, b_sh):
    n = w_sh.shape[1]
    tn = 512

    def body(c_ref, w_ref, b_ref, o_ref):
        o_ref[...] = _dot1(_silu(c_ref[...]), w_ref[...], "nn") + b_ref[...]

    return pl.pallas_call(
        body, name="ada_fwd", grid=(n // tn,),
        in_specs=[pl.BlockSpec((N_DEV, D_MODEL), lambda j: (0, 0)), pl.BlockSpec((D_MODEL, tn), lambda j: (0, j)),
                  pl.BlockSpec((1, tn), lambda j: (0, j))],
        out_specs=pl.BlockSpec((N_DEV, tn), lambda j: (0, j)), out_shape=jax.ShapeDtypeStruct((N_DEV, n), F32),
        compiler_params=_cparams(("parallel",)),
    )(c_all, w_sh, b_sh)


def ada_bwd(c_all, dmod_sh):
    n = dmod_sh.shape[1]
    tn = 512

    def body(c_ref, d_ref, o_ref):
        o_ref[...] = _dot1(_silu(c_ref[...]), d_ref[...], "tn")

    return pl.pallas_call(
        body, name="ada_bwd", grid=(n // tn,),
        in_specs=[pl.BlockSpec((N_DEV, D_MODEL), lambda j: (0, 0)), pl.BlockSpec((N_DEV, tn), lambda j: (0, j))],
        out_specs=pl.BlockSpec((D_MODEL, tn), lambda j: (0, j)), out_shape=jax.ShapeDtypeStruct((D_MODEL, n), F32),
        compiler_params=_cparams(("parallel",)),
    )(c_all, dmod_sh)


SHIFT_T, SCALE_T, GATE_T, SHIFT_F, SCALE_F, GATE_F = range(6)


def modulate(x, mod, shift_row, scale_row, name):
    S = x.shape[0]

    def body(x_ref, m_ref, o_ref):
        m = m_ref[...]
        o_ref[...] = (x_ref[...] * (1.0 + m[scale_row:scale_row + 1]) + m[shift_row:shift_row + 1]).astype(BF16)

    return rowcall(body, name=name, S=S, ts=512, ins=[(x, "row"), (mod, "vec")], outs=[((S, D_MODEL), BF16, "row")])[0]


def _conv_fwd(cur, prev, w, width):
    y = cur * w[width - 1:width]
    for j in range(width - 1):
        y = y + _shift_down(cur, prev, width - 1 - j) * w[j:j + 1]
    return y


def _prep_a_core(cur, prev, w):
    return _silu_and_grad(_conv_fwd(cur, prev, w, A_CONV))


def prep_a_fwd(qkv_raw, ba, conv_a, a_log, dt_bias):
    S = qkv_raw.shape[0]

    def body(x_ref, xp_ref, ba_ref, w_ref, al_ref, dt_ref, q_ref, k_ref, v_ref, beta_ref, g_ref):
        first = (pl.program_id(0) > 0).astype(F32)
        y, _ = _prep_a_core(x_ref[...].astype(F32), _halo_prev(xp_ref) * first, w_ref[...])
        for h in range(A_HEADS):
            sl = slice(h * A_DK, (h + 1) * A_DK)
            qh = y[:, sl]
            kh = y[:, A_W + h * A_DK:A_W + (h + 1) * A_DK]
            q_ref[:, sl] = qh * (lax.rsqrt(jnp.sum(qh * qh, axis=-1, keepdims=True) + L2_EPS) * (A_DK ** -0.5))
            k_ref[:, sl] = kh * lax.rsqrt(jnp.sum(kh * kh, axis=-1, keepdims=True) + L2_EPS)
        v_ref[...] = y[:, 2 * A_W:3 * A_W]
        bav = ba_ref[...]
        beta_ref[...] = _sigmoid(bav[:, 0:A_HEADS])
        g_ref[...] = -jnp.exp(al_ref[...]) * _softplus(bav[:, A_HEADS:2 * A_HEADS] + dt_ref[...])

    return rowcall(
        body, name="prep_a_fwd", S=S, ts=256,
        ins=[(qkv_raw, "row"), (qkv_raw, "prev"), (ba, "row"), (conv_a, "vec"), (a_log, "vec"), (dt_bias, "vec")],
        outs=[((S, A_W), F32, "row")] * 3 + [((S, A_HEADS), F32, "row")] * 2)


HEAD_GROUP = 4
GROUP_ROWS = HEAD_GROUP * CHUNK
N_HEAD_GROUPS = A_HEADS // HEAD_GROUP
LOG_CHUNK = int(math.log2(CHUNK))


def _tri_masks():
    rb = lax.broadcasted_iota(jnp.int32, (GROUP_ROWS, GROUP_ROWS), 0)
    cb = lax.broadcasted_iota(jnp.int32, (GROUP_ROWS, GROUP_ROWS), 1)
    same = (rb >> LOG_CHUNK) == (cb >> LOG_CHUNK)
    return dict(causal=same & (rb >= cb), strict=same & (rb > cb), eye=rb == cb, upper=same & (cb >= rb),
                last=cb == (rb | (CHUNK - 1)), rb=rb, cb=cb)


def _col_to_row(colv, eye):
    return jnp.sum(jnp.where(eye, colv, 0.0), axis=0, keepdims=True)


def _row_to_col(rowv, eye):
    return jnp.sum(jnp.where(eye, rowv, 0.0), axis=1, keepdims=True)


def _tri_inv(a_list, mk):
    rb, cb = mk["rb"], mk["cb"]
    ts = [jnp.where(mk["eye"], 1.0, 0.0) - jnp.where((rb >> 1) == (cb >> 1), a, 0.0) for a in a_list]
    for lvl in range(1, LOG_CHUNK):
        rs, cs = rb >> lvl, cb >> lvl
        sel = ((rs & 1) == 1) & (cs == rs - 1)
        inner = [_dot3(t, jnp.where(sel, a, 0.0), "nn") for t, a in zip(ts, a_list)]
        ts = [t - _dot3(i, t, "nn") for i, t in zip(inner, ts)]
    return ts


def _stack_heads(ref, grp):
    return jnp.concatenate([ref[:, (grp * HEAD_GROUP + j) * A_DK:(grp * HEAD_GROUP + j + 1) * A_DK]
                            for j in range(HEAD_GROUP)], axis=0)


def _stack_cols(tile, grp):
    return jnp.concatenate([tile[:, grp * HEAD_GROUP + j:grp * HEAD_GROUP + j + 1] for j in range(HEAD_GROUP)], axis=0)


def _delta_local(q, k, v, beta, g, mk):
    causal, strict, eye = mk["causal"], mk["strict"], mk["eye"]
    g_row = _col_to_row(g, eye)
    gc = jnp.sum(jnp.where(causal, g_row, 0.0), axis=1, keepdims=True)
    gc_row = _col_to_row(gc, eye)
    decay = jnp.where(causal, jnp.exp(jnp.where(causal, gc - gc_row, 0.0)), 0.0)
    gam = jnp.exp(gc)
    kb = k * beta
    vb = v * beta
    y = kb * gam
    a = jnp.where(strict, _dot1(kb, k, "nt") * decay, 0.0)
    p = _dot1(q, k, "nt") * decay
    gl = jnp.sum(jnp.where(mk["last"], gc_row, 0.0), axis=1, keepdims=True)
    kd = k * jnp.exp(gl - gc)
    return dict(gc=gc, decay=decay, gam=gam, kb=kb, vb=vb, y=y, a=a, p=p, gl=gl, kd=kd)


def _head_rows(x, j):
    return x[j * CHUNK:(j + 1) * CHUNK]


def delta_fwd(q, k, v, beta, g):
    S = q.shape[0]
    n_chunks = S // CHUNK

    def body(q_ref, k_ref, v_ref, beta_ref, g_ref, o_ref, sprev_ref, t_ref, state_ref):
        @pl.when(pl.program_id(0) == 0)
        def _():
            state_ref[...] = jnp.zeros_like(state_ref)

        mk = _tri_masks()
        betav, gv = beta_ref[...], g_ref[...]
        groups = range(N_HEAD_GROUPS)
        q_all = [_stack_heads(q_ref, grp) for grp in groups]
        locs = [_delta_local(q_all[grp], _stack_heads(k_ref, grp), _stack_heads(v_ref, grp),
                             _stack_cols(betav, grp), _stack_cols(gv, grp), mk) for grp in groups]
        tinvs = _tri_inv([loc["a"] for loc in locs], mk)
        uws = [_dot3(tinvs[grp], jnp.concatenate([locs[grp]["vb"], locs[grp]["y"]], axis=1), "nn") for grp in groups]
        for grp in groups:
            loc, uw = locs[grp], uws[grp]
            t_ref[0, grp] = tinvs[grp]
            qg = q_all[grp] * loc["gam"]
            egl = jnp.exp(loc["gl"])
            vns, o_state = [], []
            for j in range(HEAD_GROUP):
                h = grp * HEAD_GROUP + j
                s0 = state_ref[h]
                sprev_ref[0, h] = s0
                uw_h = _head_rows(uw, j)
                vn = uw_h[:, :A_DK] - _dot1(uw_h[:, A_DK:], s0, "nn")
                vns.append(vn)
                o_state.append(_dot1(_head_rows(qg, j), s0, "nn"))
                state_ref[h] = s0 * egl[(j + 1) * CHUNK - 1:(j + 1) * CHUNK] + _dot1(_head_rows(loc["kd"], j), vn, "tn")
            o_local = _dot1(loc["p"], jnp.concatenate(vns, axis=0), "nn")
            for j in range(HEAD_GROUP):
                h = grp * HEAD_GROUP + j
                o_ref[:, h * A_DK:(h + 1) * A_DK] = o_state[j] + _head_rows(o_local, j)

    tile = pl.BlockSpec((CHUNK, A_W), lambda n: (n, 0))
    small = pl.BlockSpec((CHUNK, A_HEADS), lambda n: (n, 0))
    return pl.pallas_call(
        body, name="delta_fwd", grid=(n_chunks,), in_specs=[tile, tile, tile, small, small],
        out_specs=[tile, pl.BlockSpec((1, A_HEADS, A_DK, A_DK), lambda n: (n, 0, 0, 0)),
                   pl.BlockSpec((1, N_HEAD_GROUPS, GROUP_ROWS, GROUP_ROWS), lambda n: (n, 0, 0, 0))],
        out_shape=[jax.ShapeDtypeStruct((S, A_W), F32), jax.ShapeDtypeStruct((n_chunks, A_HEADS, A_DK, A_DK), F32),
                   jax.ShapeDtypeStruct((n_chunks, N_HEAD_GROUPS, GROUP_ROWS, GROUP_ROWS), F32)],
        scratch_shapes=[pltpu.VMEM((A_HEADS, A_DK, A_DK), F32)],
        compiler_params=_cparams(("arbitrary",)),
    )(q, k, v, beta, g)


def gate_a_fwd(o_pre, z, norm_w):
    S = o_pre.shape[0]

    def body(o_ref, z_ref, nw_ref, out_ref):
        nw = nw_ref[...]
        for h in range(A_HEADS):
            sl = slice(h * A_DK, (h + 1) * A_DK)
            oh = o_ref[:, sl]
            r = lax.rsqrt(jnp.mean(oh * oh, axis=-1, keepdims=True) + RMS_EPS)
            out_ref[:, sl] = (oh * r * nw * _silu(z_ref[:, sl])).astype(BF16)

    return rowcall(body, name="gate_a_fwd", S=S, ts=512, ins=[(o_pre, "row"), (z, "row"), (norm_w, "vec")],
                   outs=[((S, A_W), BF16, "row")])[0]


HEADS_PER_GROUP = 2
GROUP_W = HEADS_PER_GROUP * B_DH
N_GROUPS = B_HEADS // HEADS_PER_GROUP
PAD_ROWS = B_PREV * CHUNK


Q_TILE = 256
Q_CHUNKS = Q_TILE // CHUNK
KEY_WIN = (B_PREV + Q_CHUNKS) * CHUNK


def _band_probs(qh, kh, bias, valid):
    s = _dot1(qh, kh, "nt") * (B_DH ** -0.5) + bias
    s = jnp.where(valid, s, NEG_INF)
    e = jnp.exp(s - jnp.max(s, axis=-1, keepdims=True))
    return e * (1.0 / jnp.sum(e, axis=-1, keepdims=True))


def _attn_specs(S, tile_rows):
    n_cb = B_W // GROUP_W
    return [pl.BlockSpec((tile_rows, GROUP_W), lambda g, n: (n + PAD_ROWS // tile_rows, g)),
            pl.BlockSpec((PAD_ROWS + S, GROUP_W), lambda g, n: (0, n_cb + g)),
            pl.BlockSpec((PAD_ROWS + S, GROUP_W), lambda g, n: (0, 2 * n_cb + g)),
            pl.BlockSpec((HEADS_PER_GROUP, CHUNK, B_BAND), lambda g, n: (g, 0, 0))]


def _band_valid(first_chunk):
    return lax.broadcasted_iota(jnp.int32, (CHUNK, B_BAND), 1) >= PAD_ROWS - first_chunk * CHUNK


def _chunk_rows(x, qc, rows=CHUNK):
    return x[qc * CHUNK:qc * CHUNK + rows]


FWD_TILE = 512
FWD_CHUNKS = FWD_TILE // CHUNK
FWD_WIN = (B_PREV + FWD_CHUNKS) * CHUNK


def attn_fwd(qkv_pad, bias):
    S = qkv_pad.shape[0] - PAD_ROWS

    def body(q_ref, k_ref, v_ref, b_ref, o_ref):
        n = pl.program_id(1)
        start = pl.multiple_of(n * FWD_TILE, FWD_TILE)
        kwin = k_ref[pl.ds(start, FWD_WIN), :]
        vwin = v_ref[pl.ds(start, FWD_WIN), :]
        qv = q_ref[...]
        pairs = [(qc, hh) for qc in range(FWD_CHUNKS) for hh in range(HEADS_PER_GROUP)]
        sl = lambda hh: slice(hh * B_DH, (hh + 1) * B_DH)
        s = [_dot1(_chunk_rows(qv, qc)[:, sl(hh)], _chunk_rows(kwin, qc, B_BAND)[:, sl(hh)], "nt") for qc, hh in pairs]
        s = [jnp.where(_band_valid(n * FWD_CHUNKS + qc), x * (B_DH ** -0.5) + b_ref[hh], NEG_INF)
             for x, (qc, hh) in zip(s, pairs)]
        e = [jnp.exp(x - jnp.max(x, axis=-1, keepdims=True)) for x in s]
        p = [x * (1.0 / jnp.sum(x, axis=-1, keepdims=True)) for x in e]
        o = [_dot1(x, _chunk_rows(vwin, qc, B_BAND)[:, sl(hh)], "nn") for x, (qc, hh) in zip(p, pairs)]
        rows = [jnp.concatenate(o[qc * HEADS_PER_GROUP:(qc + 1) * HEADS_PER_GROUP], axis=1) for qc in range(FWD_CHUNKS)]
        o_ref[...] = jnp.concatenate(rows, axis=0).astype(BF16)

    return pl.pallas_call(
        body, name="attn_fwd", grid=(N_GROUPS, S // FWD_TILE), in_specs=_attn_specs(S, FWD_TILE),
        out_specs=pl.BlockSpec((FWD_TILE, GROUP_W), lambda g, n: (n, g)),
        out_shape=jax.ShapeDtypeStruct((S, B_W), BF16),
        compiler_params=_cparams(("parallel", "arbitrary")),
    )(qkv_pad, qkv_pad, qkv_pad, bias)


def _rel_onehot(i):
    kj = lax.broadcasted_iota(jnp.int32, (B_BAND, B_REL), 0)
    r = lax.broadcasted_iota(jnp.int32, (B_BAND, B_REL), 1)
    idx = jnp.clip(PAD_ROWS + i - kj, -(CHUNK - 1), B_MAX_REL) + (CHUNK - 1)
    return jnp.where(idx == r, 1.0, 0.0)


def bias_expand(rel_bias):
    def body(rb_ref, o_ref):
        i = pl.program_id(0)
        o_ref[0] = _dot3(rb_ref[...], _rel_onehot(i), "nt")

    return pl.pallas_call(
        body, name="bias_expand", grid=(CHUNK,),
        in_specs=[pl.BlockSpec((B_HEADS, B_REL), lambda i: (0, 0))],
        out_specs=pl.BlockSpec((1, B_HEADS, B_BAND), lambda i: (i, 0, 0)),
        out_shape=jax.ShapeDtypeStruct((CHUNK, B_HEADS, B_BAND), F32),
        compiler_params=_cparams(("parallel",)),
    )(rel_bias)


def bias_reduce(dbias):
    def body(d_ref, o_ref):
        i = pl.program_id(0)

        @pl.when(i == 0)
        def _():
            o_ref[...] = jnp.zeros_like(o_ref)

        o_ref[...] += _dot3(d_ref[0], _rel_onehot(i), "nn")

    return pl.pallas_call(
        body, name="bias_reduce", grid=(CHUNK,),
        in_specs=[pl.BlockSpec((1, B_HEADS, B_BAND), lambda i: (i, 0, 0))],
        out_specs=pl.BlockSpec((B_HEADS, B_REL), lambda i: (0, 0)),
        out_shape=jax.ShapeDtypeStruct((B_HEADS, B_REL), F32),
        compiler_params=_cparams(("arbitrary",)),
    )(dbias)


def merge_fwd(gates_raw, b_gate, ya, yb):
    S = ya.shape[0]

    def body(g_ref, b_ref, ya_ref, yb_ref, o_ref):
        gt = _sigmoid(g_ref[...] + b_ref[...])
        o_ref[...] = (gt[:, :D_MODEL] * ya_ref[...] + gt[:, D_MODEL:] * yb_ref[...]).astype(BF16)

    return rowcall(body, name="merge_fwd", S=S, ts=512,
                   ins=[(gates_raw, "row"), (b_gate, "vec"), (ya, "row"), (yb, "row")],
                   outs=[((S, D_MODEL), BF16, "row")])[0]


def _ln_stats(xpre):
    mu = jnp.mean(xpre, axis=-1, keepdims=True)
    xc = xpre - mu
    rstd = lax.rsqrt(jnp.mean(xc * xc, axis=-1, keepdims=True) + LN_EPS)
    return xc * rstd, rstd


def ln1_fwd(x, mix, mod, ln_g, ln_b):
    S = x.shape[0]

    def body(x_ref, mix_ref, m_ref, g_ref, b_ref, xpre_ref, x1_ref, h2_ref):
        m = m_ref[...]
        xpre = ALPHA * x_ref[...] + m[GATE_T:GATE_T + 1] * mix_ref[...]
        xhat, _ = _ln_stats(xpre)
        x1 = xhat * g_ref[...] + b_ref[...]
        xpre_ref[...] = xpre
        x1_ref[...] = x1
        h2_ref[...] = (x1 * (1.0 + m[SCALE_F:SCALE_F + 1]) + m[SHIFT_F:SHIFT_F + 1]).astype(BF16)

    return rowcall(body, name="ln1_fwd", S=S, ts=512,
                   ins=[(x, "row"), (mix, "row"), (mod, "vec"), (ln_g, "vec"), (ln_b, "vec")],
                   outs=[((S, D_MODEL), F32, "row"), ((S, D_MODEL), F32, "row"), ((S, D_MODEL), BF16, "row")])


STRIP_ROWS = 32
STRIP_COLS = 256


def ffn_act_fwd(up, conv_w, conv_b):
    S = up.shape[0]
    ts = 256

    def body(u_ref, up_ref, w_ref, b_ref, o_ref, ubuf):
        ubuf[0:8] = _halo_prev(up_ref) * (pl.program_id(0) > 0).astype(F32)
        ubuf[8:8 + ts] = u_ref[...].astype(F32)

        def col_block(j, carry):
            gate = pl.ds(pl.multiple_of(j * STRIP_COLS, STRIP_COLS), STRIP_COLS)
            halves = [gate, pl.ds(pl.multiple_of(D_FF + j * STRIP_COLS, STRIP_COLS), STRIP_COLS)]
            w = [w_ref[:, c] for c in halves]
            bias = [b_ref[:, c] for c in halves]
            for r0 in range(0, ts, STRIP_ROWS):
                uc = []
                for h in range(2):
                    x = ubuf[r0:r0 + STRIP_ROWS + 8, halves[h]]
                    uc.append(bias[h] + sum(
                        w[h][t:t + 1] * (x if t == FFN_CONV - 1 else pltpu.roll(x, FFN_CONV - 1 - t, axis=0))[8:]
                        for t in range(FFN_CONV)))
                o_ref[r0:r0 + STRIP_ROWS, gate] = (_silu(uc[0]) * uc[1]).astype(BF16)
            return carry

        lax.fori_loop(0, D_FF // STRIP_COLS, col_block, 0)

    return rowcall(body, name="ffn_act_fwd", S=S, ts=ts,
                   ins=[(up, "row"), (up, "prev"), (conv_w, "vec"), (conv_b, "vec")],
                   outs=[((S, D_FF), BF16, "row")], scratch=[pltpu.VMEM((ts + 8, 2 * D_FF), F32)])[0]


def final_fwd_bwd(x1, ffn, target, mod, ln_g, ln_b):
    S = x1.shape[0]

    def body(x1_ref, f_ref, t_ref, m_ref, g_ref, b_ref, dxpre_ref, dffn_ref, loss_ref, dgate_ref, dg_ref, db_ref):
        gate = m_ref[...][GATE_F:GATE_F + 1]
        ffn_v = f_ref[...]
        xpre = ALPHA * x1_ref[...] + gate * ffn_v
        xhat, rstd = _ln_stats(xpre)
        err = xhat * g_ref[...] + b_ref[...] - t_ref[...]
        loss_ref[...] += 0.5 * jnp.sum(jnp.mean(err * err, axis=-1, keepdims=True), axis=0, keepdims=True)
        dy = err * (1.0 / D_MODEL)
        dg_ref[...] += jnp.sum(dy * xhat, axis=0, keepdims=True)
        db_ref[...] += jnp.sum(dy, axis=0, keepdims=True)
        dyg = dy * g_ref[...]
        dxpre = rstd * (dyg - jnp.mean(dyg, axis=-1, keepdims=True) - xhat * jnp.mean(dyg * xhat, axis=-1, keepdims=True))
        dxpre_ref[...] = dxpre
        dffn_ref[...] = (gate * dxpre).astype(BF16)
        dgate_ref[...] += jnp.sum(dxpre * ffn_v, axis=0, keepdims=True)

    vec = ((1, D_MODEL), F32, "acc")
    return rowcall(body, name="final_fwd_bwd", S=S, ts=512,
                   ins=[(x1, "row"), (ffn, "row"), (target, "row"), (mod, "vec"), (ln_g, "vec"), (ln_b, "vec")],
                   outs=[((S, D_MODEL), F32, "row"), ((S, D_MODEL), BF16, "row"), ((1, 1), F32, "acc"), vec, vec, vec])


def ffn_act_bwd(dact, up, conv_w, conv_b):
    S = up.shape[0]
    ts = 256
    win_u, win_d = STRIP_ROWS + 16, STRIP_ROWS + 8

    def body(d_ref, dn_ref, u_ref, up_ref, un_ref, w_ref, b_ref, dup_ref, dw_ref, db_ref, ubuf, dbuf):
        i = pl.program_id(0)
        ubuf[0:8] = _halo_prev(up_ref) * (i > 0).astype(F32)
        ubuf[8:8 + ts] = u_ref[...].astype(F32)
        ubuf[8 + ts:16 + ts] = _halo_next(un_ref)
        dbuf[0:ts] = d_ref[...].astype(F32)
        dbuf[ts:ts + 8] = _halo_next(dn_ref) * (i < pl.num_programs(0) - 1).astype(F32)

        def col_block(j, carry):
            halves = [pl.ds(pl.multiple_of(j * STRIP_COLS, STRIP_COLS), STRIP_COLS),
                      pl.ds(pl.multiple_of(D_FF + j * STRIP_COLS, STRIP_COLS), STRIP_COLS)]
            w = [w_ref[:, c] for c in halves]
            bias = [b_ref[:, c] for c in halves]
            dw_acc = [[jnp.zeros((1, STRIP_COLS), F32) for _ in range(FFN_CONV)] for _ in halves]
            db_acc = [jnp.zeros((1, STRIP_COLS), F32) for _ in halves]
            for r0 in range(0, ts, STRIP_ROWS):
                shifted = [[x if k == 0 else pltpu.roll(x, k, axis=0) for k in range(FFN_CONV)]
                           for x in (ubuf[r0:r0 + win_u, c] for c in halves)]
                uc = [bias[h] + sum(w[h][t:t + 1] * shifted[h][FFN_CONV - 1 - t][8:8 + win_d] for t in range(FFN_CONV))
                      for h in range(2)]
                dact_w = dbuf[r0:r0 + win_d, halves[0]]
                sg, dsg = _silu_and_grad(uc[0])
                duc = [dact_w * uc[1] * dsg, dact_w * sg]
                for h in range(2):
                    dup = duc[h] * w[h][FFN_CONV - 1:FFN_CONV]
                    for t in range(FFN_CONV - 1):
                        dup = dup + pltpu.roll(duc[h], win_d - (FFN_CONV - 1 - t), axis=0) * w[h][t:t + 1]
                    dup_ref[r0:r0 + STRIP_ROWS, halves[h]] = dup[:STRIP_ROWS].astype(BF16)
                    mine = duc[h][:STRIP_ROWS]
                    db_acc[h] = db_acc[h] + jnp.sum(mine, axis=0, keepdims=True)
                    for t in range(FFN_CONV):
                        dw_acc[h][t] = dw_acc[h][t] + jnp.sum(
                            mine * shifted[h][FFN_CONV - 1 - t][8:8 + STRIP_ROWS], axis=0, keepdims=True)
            for h in range(2):
                dw_ref[:, halves[h]] += jnp.concatenate(dw_acc[h], axis=0)
                db_ref[:, halves[h]] += db_acc[h]
            return carry

        lax.fori_loop(0, D_FF // STRIP_COLS, col_block, 0)

    return rowcall(body, name="ffn_act_bwd", S=S, ts=ts,
                   ins=[(dact, "row"), (dact, "next"), (up, "row"), (up, "prev"), (up, "next"), (conv_w, "vec"), (conv_b, "vec")],
                   outs=[((S, 2 * D_FF), BF16, "row"), ((FFN_CONV, 2 * D_FF), F32, "acc"), ((1, 2 * D_FF), F32, "acc")],
                   scratch=[pltpu.VMEM((ts + 16, 2 * D_FF), F32), pltpu.VMEM((ts + 8, D_FF), F32)])


def ln1_bwd(dxpre2, dh2, xpre1, mix, mod, ln_g, ln_b):
    S = xpre1.shape[0]

    def body(d2_ref, dh_ref, xp_ref, mix_ref, m_ref, g_ref, b_ref, dxpre_ref, dmix_ref,
             dscale_ref, dshift_ref, dgate_ref, dg_ref, db_ref):
        m = m_ref[...]
        xhat, rstd = _ln_stats(xp_ref[...])
        x1 = xhat * g_ref[...] + b_ref[...]
        dh = dh_ref[...]
        dx1 = ALPHA * d2_ref[...] + dh * (1.0 + m[SCALE_F:SCALE_F + 1])
        dscale_ref[...] += jnp.sum(dh * x1, axis=0, keepdims=True)
        dshift_ref[...] += jnp.sum(dh, axis=0, keepdims=True)
        dg_ref[...] += jnp.sum(dx1 * xhat, axis=0, keepdims=True)
        db_ref[...] += jnp.sum(dx1, axis=0, keepdims=True)
        dyg = dx1 * g_ref[...]
        dxpre = rstd * (dyg - jnp.mean(dyg, axis=-1, keepdims=True) - xhat * jnp.mean(dyg * xhat, axis=-1, keepdims=True))
        dxpre_ref[...] = dxpre
        dmix_ref[...] = (m[GATE_T:GATE_T + 1] * dxpre).astype(BF16)
        dgate_ref[...] += jnp.sum(dxpre * mix_ref[...], axis=0, keepdims=True)

    vec = ((1, D_MODEL), F32, "acc")
    return rowcall(body, name="ln1_bwd", S=S, ts=512,
                   ins=[(dxpre2, "row"), (dh2, "row"), (xpre1, "row"), (mix, "row"), (mod, "vec"), (ln_g, "vec"), (ln_b, "vec")],
                   outs=[((S, D_MODEL), F32, "row"), ((S, D_MODEL), BF16, "row"), vec, vec, vec, vec, vec])


def merge_bwd(dmerged, gates_raw, b_gate, ya, yb):
    S = ya.shape[0]

    def body(d_ref, g_ref, b_ref, ya_ref, yb_ref, dya_ref, dyb_ref, dg_ref, dbg_ref):
        gt = _sigmoid(g_ref[...] + b_ref[...])
        d = d_ref[...]
        ga, gb = gt[:, :D_MODEL], gt[:, D_MODEL:]
        dya_ref[...] = (d * ga).astype(BF16)
        dyb_ref[...] = (d * gb).astype(BF16)
        dgr = jnp.concatenate([d * ya_ref[...] * ga * (1.0 - ga), d * yb_ref[...] * gb * (1.0 - gb)], axis=1)
        dg_ref[...] = dgr.astype(BF16)
        dbg_ref[...] += jnp.sum(dgr, axis=0, keepdims=True)

    return rowcall(body, name="merge_bwd", S=S, ts=512,
                   ins=[(dmerged, "row"), (gates_raw, "row"), (b_gate, "vec"), (ya, "row"), (yb, "row")],
                   outs=[((S, D_MODEL), BF16, "row"), ((S, D_MODEL), BF16, "row"), ((S, 2 * D_MODEL), BF16, "row"),
                         ((1, 2 * D_MODEL), F32, "acc")])


def attn_bwd(qkv_pad, bias, do_b):
    S = qkv_pad.shape[0] - PAD_ROWS

    def body(q_ref, k_ref, v_ref, bias_ref, do_ref, dq_ref, dk_ref, dv_ref, db_ref, b_ref):
        n = pl.program_id(1)

        @pl.when(n == 0)
        def _():
            dk_ref[...] = jnp.zeros_like(dk_ref)
            dv_ref[...] = jnp.zeros_like(dv_ref)
            db_ref[...] = jnp.zeros_like(db_ref)
            b_ref[...] = jnp.full(b_ref.shape, NEG_INF, F32)
            for hh in range(HEADS_PER_GROUP):
                for qc in range(Q_CHUNKS):
                    b_ref[hh, qc * CHUNK:(qc + 1) * CHUNK, qc * CHUNK:qc * CHUNK + B_BAND] = bias_ref[hh]

        start = pl.multiple_of(n * Q_TILE, Q_TILE)
        kwin = k_ref[pl.ds(start, KEY_WIN), :]
        vwin = v_ref[pl.ds(start, KEY_WIN), :]
        qv, dov = q_ref[...], do_ref[...]
        valid = lax.broadcasted_iota(jnp.int32, (Q_TILE, KEY_WIN), 1) >= PAD_ROWS - n * Q_TILE
        dqs, dks, dvs = [], [], []
        for hh in range(HEADS_PER_GROUP):
            sl = slice(hh * B_DH, (hh + 1) * B_DH)
            p = _band_probs(qv[:, sl], kwin[:, sl], b_ref[hh], valid)
            dp = _dot1(dov[:, sl], vwin[:, sl], "nt")
            ds = p * (dp - jnp.sum(dp * p, axis=-1, keepdims=True))
            dbh = ds[0:CHUNK, 0:B_BAND]
            for qc in range(1, Q_CHUNKS):
                dbh = dbh + ds[qc * CHUNK:(qc + 1) * CHUNK, qc * CHUNK:qc * CHUNK + B_BAND]
            db_ref[hh] += dbh
            dsq = ds * (B_DH ** -0.5)
            dqs.append(_dot1(dsq, kwin[:, sl], "nn"))
            dks.append(_dot1(dsq, qv[:, sl], "tn"))
            dvs.append(_dot1(p, dov[:, sl], "tn"))
        dq_ref[...] = jnp.concatenate(dqs, axis=1).astype(BF16)
        dk_ref[pl.ds(start, KEY_WIN), :] += jnp.concatenate(dks, axis=1)
        dv_ref[pl.ds(start, KEY_WIN), :] += jnp.concatenate(dvs, axis=1)

    col = pl.BlockSpec((PAD_ROWS + S, GROUP_W), lambda g, n: (0, g))
    tile = pl.BlockSpec((Q_TILE, GROUP_W), lambda g, n: (n, g))
    return pl.pallas_call(
        body, name="attn_bwd", grid=(N_GROUPS, S // Q_TILE), in_specs=_attn_specs(S, Q_TILE) + [tile],
        out_specs=[tile, col, col, pl.BlockSpec((HEADS_PER_GROUP, CHUNK, B_BAND), lambda g, n: (g, 0, 0))],
        out_shape=[jax.ShapeDtypeStruct((S, B_W), BF16), jax.ShapeDtypeStruct((PAD_ROWS + S, B_W), F32),
                   jax.ShapeDtypeStruct((PAD_ROWS + S, B_W), F32), jax.ShapeDtypeStruct((B_HEADS, CHUNK, B_BAND), F32)],
        scratch_shapes=[pltpu.VMEM((HEADS_PER_GROUP, Q_TILE, KEY_WIN), F32)],
        compiler_params=_cparams(("parallel", "arbitrary")),
    )(qkv_pad, qkv_pad, qkv_pad, bias, do_b)


def gate_a_bwd(do_a, o_pre, z, norm_w):
    S = o_pre.shape[0]

    def body(d_ref, o_ref, z_ref, nw_ref, dop_ref, dz_ref, dnw_ref):
        nw = nw_ref[...]
        acc = jnp.zeros((1, A_DK), F32)
        for h in range(A_HEADS):
            sl = slice(h * A_DK, (h + 1) * A_DK)
            oh, zh, dh = o_ref[:, sl], z_ref[:, sl], d_ref[:, sl]
            r = lax.rsqrt(jnp.mean(oh * oh, axis=-1, keepdims=True) + RMS_EPS)
            sz, dsz = _silu_and_grad(zh)
            dz_ref[:, sl] = (dh * oh * r * nw * dsz).astype(BF16)
            acc = acc + jnp.sum(dh * oh * r * sz, axis=0, keepdims=True)
            t = dh * nw * sz
            dop_ref[:, sl] = r * t - oh * (r * r * r) * jnp.mean(t * oh, axis=-1, keepdims=True)
        dnw_ref[...] += acc

    return rowcall(body, name="gate_a_bwd", S=S, ts=512,
                   ins=[(do_a, "row"), (o_pre, "row"), (z, "row"), (norm_w, "vec")],
                   outs=[((S, A_W), F32, "row"), ((S, A_W), BF16, "row"), ((1, A_DK), F32, "acc")])


def delta_bwd(q, k, v, beta, g, sprev, tinv, do):
    S = q.shape[0]
    n_chunks = S // CHUNK

    def body(q_ref, k_ref, v_ref, beta_ref, g_ref, sprev_ref, t_ref, do_ref,
             dq_ref, dk_ref, dv_ref, dbeta_ref, dg_ref, dstate_ref):
        @pl.when(pl.program_id(0) == 0)
        def _():
            dstate_ref[...] = jnp.zeros_like(dstate_ref)

        mk = _tri_masks()
        causal, strict, eye = mk["causal"], mk["strict"], mk["eye"]
        blk_end = (lax.broadcasted_iota(jnp.int32, (GROUP_ROWS, 1), 0) & (CHUNK - 1)) == CHUNK - 1
        lane = lax.broadcasted_iota(jnp.int32, (CHUNK, A_HEADS), 1)
        betav, gv = beta_ref[...], g_ref[...]
        dbeta_t = jnp.zeros((CHUNK, A_HEADS), F32)
        dg_t = jnp.zeros((CHUNK, A_HEADS), F32)
        groups, heads = range(N_HEAD_GROUPS), range(HEAD_GROUP)
        st = [dict() for _ in groups]

        def local_part(grp, s):
            s["qs"], s["ks"], s["vs"] = _stack_heads(q_ref, grp), _stack_heads(k_ref, grp), _stack_heads(v_ref, grp)
            s["dos"] = _stack_heads(do_ref, grp)
            s["bs"] = _stack_cols(betav, grp)
            s["loc"] = loc = _delta_local(s["qs"], s["ks"], s["vs"], s["bs"], _stack_cols(gv, grp), mk)
            s["tinv"] = t_ref[0, grp]
            s["rhs"] = jnp.concatenate([loc["vb"], loc["y"]], axis=1)
            s["uw"] = _dot3(s["tinv"], s["rhs"], "nn")

        def state_part(grp, s):
            loc, uw, dos, qs = s["loc"], s["uw"], s["dos"], s["qs"]
            gam, kd, gl, gc = loc["gam"], loc["kd"], loc["gl"], loc["gc"]
            qg = qs * gam
            egl = jnp.exp(gl)
            hid = [grp * HEAD_GROUP + j for j in heads]
            s0 = [sprev_ref[0, h] for h in hid]
            ds1 = [dstate_ref[h] for h in hid]
            w = [_head_rows(uw, j)[:, A_DK:] for j in heads]
            vn = [_head_rows(uw, j)[:, :A_DK] - _dot1(w[j], s0[j], "nn") for j in heads]
            vns = jnp.concatenate(vn, axis=0)
            dvn_local = _dot1(loc["p"], dos, "tn")
            dvn = [_head_rows(dvn_local, j) + _dot1(_head_rows(kd, j), ds1[j], "nn") for j in heads]
            dvns = jnp.concatenate(dvn, axis=0)
            s["dp"] = jnp.where(causal, _dot1(dos, vns, "nt"), 0.0)
            dqg = jnp.concatenate([_dot1(_head_rows(dos, j), s0[j], "nt") for j in heads], axis=0)
            s["dq"] = dqg * gam
            dgc = jnp.sum(dqg * qg, axis=-1, keepdims=True)
            for j in heads:
                dstate_ref[hid[j]] = (_dot1(_head_rows(qg, j), _head_rows(dos, j), "tn")
                                      + egl[(j + 1) * CHUNK - 1:(j + 1) * CHUNK] * ds1[j] - _dot1(w[j], dvn[j], "tn"))
            dkd = jnp.concatenate([_dot1(vn[j], ds1[j], "nt") for j in heads], axis=0)
            s["dk"] = dkd * jnp.exp(gl - gc)
            t1 = jnp.sum(dkd * kd, axis=-1, keepdims=True)
            dgl = jnp.concatenate(
                [jnp.broadcast_to(jnp.sum(_head_rows(t1, j), axis=0, keepdims=True)
                                  + jnp.sum(jnp.sum(ds1[j] * s0[j], axis=-1, keepdims=True), axis=0, keepdims=True)
                                  * egl[(j + 1) * CHUNK - 1:(j + 1) * CHUNK], (CHUNK, 1)) for j in heads], axis=0)
            s["dgc"] = dgc - t1 + jnp.where(blk_end, dgl, 0.0)
            s["duw"] = jnp.concatenate(
                [dvns, jnp.concatenate([-_dot1(dvn[j], s0[j], "nt") for j in heads], axis=0)], axis=1)

        def solve_part(grp, s):
            s["dvby"] = _dot3(s["tinv"], s["duw"], "tn")
            s["dt"] = _dot3(s["duw"], s["rhs"], "nt")

        def inverse_part_a(grp, s):
            s["tdt"] = _dot3(s["tinv"], s["dt"], "tn")

        def inverse_part_b(grp, s):
            s["da"] = jnp.where(strict, -_dot3(s["tdt"], s["tinv"], "nt"), 0.0)

        def finish(grp, s):
            loc, qs, ks, vs, bs, da, dp, dvby = s["loc"], s["qs"], s["ks"], s["vs"], s["bs"], s["da"], s["dp"], s["dvby"]
            gam, decay = loc["gam"], loc["decay"]
            dm = da * decay
            dn = dp * decay
            e = da * loc["a"] + dp * loc["p"]
            dgc = s["dgc"] + jnp.sum(e, axis=1, keepdims=True) - _row_to_col(jnp.sum(e, axis=0, keepdims=True), eye)
            dy = dvby[:, A_DK:]
            dvb = dvby[:, :A_DK]
            dkb = _dot1(dm, ks, "nn") + dy * gam
            dk = s["dk"] + _dot1(dm, loc["kb"], "tn") + _dot1(dn, qs, "tn") + dkb * bs
            dq = s["dq"] + _dot1(dn, ks, "nn")
            dgc = dgc + jnp.sum(dy * loc["y"], axis=-1, keepdims=True)
            dbeta = jnp.sum(dkb * ks, axis=-1, keepdims=True) + jnp.sum(dvb * vs, axis=-1, keepdims=True)
            dv = dvb * bs
            dgs = jnp.sum(jnp.where(mk["upper"], _col_to_row(dgc, eye), 0.0), axis=1, keepdims=True)
            for j in heads:
                h = grp * HEAD_GROUP + j
                sl = slice(h * A_DK, (h + 1) * A_DK)
                dq_ref[:, sl] = _head_rows(dq, j)
                dk_ref[:, sl] = _head_rows(dk, j)
                dv_ref[:, sl] = _head_rows(dv, j)
            s["dbeta"], s["dgs"] = dbeta, dgs

        for stage in (local_part, state_part, solve_part, inverse_part_a, inverse_part_b, finish):
            for grp in groups:
                stage(grp, st[grp])
        for grp in groups:
            for j in heads:
                h = grp * HEAD_GROUP + j
                dbeta_t = dbeta_t + jnp.where(lane == h, _head_rows(st[grp]["dbeta"], j), 0.0)
                dg_t = dg_t + jnp.where(lane == h, _head_rows(st[grp]["dgs"], j), 0.0)
        dbeta_ref[...] = dbeta_t
        dg_ref[...] = dg_t

    rev = lambda n: (n_chunks - 1 - n, 0)
    rev4 = lambda n: (n_chunks - 1 - n, 0, 0, 0)
    tile = pl.BlockSpec((CHUNK, A_W), rev)
    small = pl.BlockSpec((CHUNK, A_HEADS), rev)
    return pl.pallas_call(
        body, name="delta_bwd", grid=(n_chunks,),
        in_specs=[tile, tile, tile, small, small, pl.BlockSpec((1, A_HEADS, A_DK, A_DK), rev4),
                  pl.BlockSpec((1, N_HEAD_GROUPS, GROUP_ROWS, GROUP_ROWS), rev4), tile],
        out_specs=[tile, tile, tile, small, small],
        out_shape=[jax.ShapeDtypeStruct((S, A_W), F32)] * 3 + [jax.ShapeDtypeStruct((S, A_HEADS), F32)] * 2,
        scratch_shapes=[pltpu.VMEM((A_HEADS, A_DK, A_DK), F32)],
        compiler_params=_cparams(("arbitrary",)),
    )(q, k, v, beta, g, sprev, tinv, do)


def _prep_a_dpre(raw, raw_prev, w, dq, dk, dv):
    y, dy_dpre = _prep_a_core(raw, raw_prev, w)
    parts = []
    for h in range(A_HEADS):
        yq = y[:, h * A_DK:(h + 1) * A_DK]
        dqh = dq[:, h * A_DK:(h + 1) * A_DK]
        rq = lax.rsqrt(jnp.sum(yq * yq, axis=-1, keepdims=True) + L2_EPS)
        parts.append((A_DK ** -0.5) * (rq * dqh - yq * (rq * rq * rq) * jnp.sum(dqh * yq, axis=-1, keepdims=True)))
    for h in range(A_HEADS):
        yk = y[:, A_W + h * A_DK:A_W + (h + 1) * A_DK]
        dkh = dk[:, h * A_DK:(h + 1) * A_DK]
        rk = lax.rsqrt(jnp.sum(yk * yk, axis=-1, keepdims=True) + L2_EPS)
        parts.append(rk * dkh - yk * (rk * rk * rk) * jnp.sum(dkh * yk, axis=-1, keepdims=True))
    parts.append(dv)
    return jnp.concatenate(parts, axis=1) * dy_dpre


def prep_a_bwd(qkv_raw, ba, conv_a, a_log, dt_bias, dq, dk, dv, dbeta, dg):
    S = qkv_raw.shape[0]
    ts = 256

    def body(x_ref, xp_ref, xn_ref, ba_ref, w_ref, al_ref, dt_ref, dq_ref, dqn_ref, dk_ref, dkn_ref, dv_ref, dvn_ref,
             dbeta_ref, dg_ref, draw_ref, dba_ref, dw_ref, dal_ref, ddt_ref):
        i = pl.program_id(0)
        first = (i > 0).astype(F32)
        last = (i < pl.num_programs(0) - 1).astype(F32)
        w = w_ref[...]
        cur, prev = x_ref[...].astype(F32), _halo_prev(xp_ref) * first
        dpre = _prep_a_dpre(cur, prev, w, dq_ref[...], dk_ref[...], dv_ref[...])
        dpre_n = _prep_a_dpre(_halo_next(xn_ref), cur[ts - 8:ts], w, _halo_next(dqn_ref), _halo_next(dkn_ref),
                              _halo_next(dvn_ref)) * last
        for j in range(A_CONV):
            dw_ref[j:j + 1, :] += jnp.sum(dpre * _shift_down(cur, prev, A_CONV - 1 - j), axis=0, keepdims=True)
        draw = dpre * w[A_CONV - 1:A_CONV]
        for j in range(A_CONV - 1):
            draw = draw + _shift_up(dpre, dpre_n, A_CONV - 1 - j) * w[j:j + 1]
        draw_ref[...] = draw.astype(BF16)
        bav = ba_ref[...]
        beta = _sigmoid(bav[:, 0:A_HEADS])
        xa = bav[:, A_HEADS:2 * A_HEADS] + dt_ref[...]
        nexp = -jnp.exp(al_ref[...])
        dgv = dg_ref[...]
        da = dgv * nexp * _sigmoid(xa)
        dba_ref[:, 0:A_HEADS] = dbeta_ref[...] * beta * (1.0 - beta)
        dba_ref[:, A_HEADS:2 * A_HEADS] = da
        dal_ref[...] += jnp.sum(dgv * nexp * _softplus(xa), axis=0, keepdims=True)
        ddt_ref[...] += jnp.sum(da, axis=0, keepdims=True)

    return rowcall(
        body, name="prep_a_bwd", S=S, ts=ts,
        ins=[(qkv_raw, "row"), (qkv_raw, "prev"), (qkv_raw, "next"), (ba, "row"), (conv_a, "vec"), (a_log, "vec"),
             (dt_bias, "vec"), (dq, "row"), (dq, "next"), (dk, "row"), (dk, "next"), (dv, "row"), (dv, "next"),
             (dbeta, "row"), (dg, "row")],
        outs=[((S, 3 * A_W), BF16, "row"), ((S, 2 * A_HEADS), F32, "row"), ((A_CONV, 3 * A_W), F32, "acc"),
              ((1, A_HEADS), F32, "acc"), ((1, A_HEADS), F32, "acc")])


def grad_x_final(dh1, x, dxpre1, mod):
    S = x.shape[0]

    def body(dh_ref, x_ref, dx_ref, m_ref, gx_ref, dscale_ref, dshift_ref):
        dh = dh_ref[...]
        gx_ref[...] = ALPHA * dx_ref[...] + dh * (1.0 + m_ref[...][SCALE_T:SCALE_T + 1])
        dscale_ref[...] += jnp.sum(dh * x_ref[...], axis=0, keepdims=True)
        dshift_ref[...] += jnp.sum(dh, axis=0, keepdims=True)

    vec = ((1, D_MODEL), F32, "acc")
    return rowcall(body, name="grad_x_final", S=S, ts=512, ins=[(dh1, "row"), (x, "row"), (dxpre1, "row"), (mod, "vec")],
                   outs=[((S, D_MODEL), F32, "row"), vec, vec])


_C_QKV, _C_Z, _C_BA, _C_QKVB, _C_G = 0, 3 * A_W, 4 * A_W, 4 * A_W + 2 * A_HEADS, 4 * A_W + 2 * A_HEADS + 3 * B_W
BA_PAD = 128


def split_w_in(w_in):
    ba = jnp.pad(w_in[:, _C_BA:_C_QKVB], ((0, 0), (0, BA_PAD - 2 * A_HEADS)))
    return dict(qkv=w_in[:, _C_QKV:_C_Z], z=w_in[:, _C_Z:_C_BA], ba=ba, qkvb=w_in[:, _C_QKVB:_C_G], g=w_in[:, _C_G:])


def join_w_in(p):
    return jnp.concatenate([p["qkv"], p["z"], p["ba"][:, :2 * A_HEADS], p["qkvb"], p["g"]], axis=1)


def forward_local(x, target, mod, w, sm, late_weights=None):
    h1 = modulate(x, mod, SHIFT_T, SCALE_T, "mod_t")
    qkv_raw = mm(h1, w["qkv"], mode="nn", out_dtype=BF16, name="proj_qkv")
    z = mm(h1, w["z"], mode="nn", out_dtype=F32, name="proj_z")
    ba = mm(h1, w["ba"], mode="nn", out_dtype=F32, name="proj_ba")
    qkvb = mm(h1, w["qkvb"], mode="nn", out_dtype=BF16, name="proj_qkvb")
    gates_raw = mm(h1, w["g"], mode="nn", out_dtype=F32, name="proj_g")
    q, k, v, beta, g = prep_a_fwd(qkv_raw, ba, sm["conv_a"], sm["a_log"], sm["dt_bias"])
    o_pre, sprev, tinv = delta_fwd(q, k, v, beta, g)
    o_a = gate_a_fwd(o_pre, z, sm["norm_a"])
    qkv_pad = jnp.pad(qkvb, ((PAD_ROWS, 0), (0, 0)))
    bias = jnp.transpose(bias_expand(sm["rel_bias"]), (1, 0, 2))
    o_b = attn_fwd(qkv_pad, bias)
    if late_weights is not None:
        w = dict(w, **late_weights(o_b))
    ya = mm(o_a, w["branch_a"], mode="nn", out_dtype=F32, name="branch_a")
    yb = mm(o_b, w["branch_b"], mode="nn", out_dtype=F32, name="branch_b")
    merged = merge_fwd(gates_raw, sm["b_gate"], ya, yb)
    mix = mm(merged, w["o"], mode="nn", out_dtype=F32, name="mix")
    xpre1, x1, h2 = ln1_fwd(x, mix, mod, sm["ln1_g"], sm["ln1_b"])
    up = mm(h2, w["up"], mode="nn", out_dtype=BF16, name="ffn_up", b_shards=True)
    act = ffn_act_fwd(up, sm["conv_ffn"], sm["b_conv_ffn"])
    ffn = mm(act, w["down"], mode="nn", out_dtype=F32, name="ffn_down")
    dxpre2, dffn, loss, dgate_f, dln2_g, dln2_b = final_fwd_bwd(x1, ffn, target, mod, sm["ln2_g"], sm["ln2_b"])
    saved = dict(h1=h1, qkv_raw=qkv_raw, z=z, ba=ba, gates_raw=gates_raw, q=q, k=k, v=v, beta=beta, g=g,
                 o_pre=o_pre, sprev=sprev, tinv=tinv, o_a=o_a, qkv_pad=qkv_pad, bias=bias, o_b=o_b, ya=ya, yb=yb,
                 merged=merged, mix=mix, xpre1=xpre1, x1=x1, h2=h2, up=up, act=act, ffn=ffn, w=w)
    return loss, dxpre2, dffn, dict(gate_f=dgate_f, ln2_g=dln2_g, ln2_b=dln2_b), saved


def backward_local(x, mod, sm, dxpre2, dffn, fin, sv, early_grads=None, early_w_in=None):
    w = sv["w"]
    dact = mm(dffn, w["down"], mode="nt", out_dtype=BF16, name="d_act")
    gw_down = mm(sv["act"], dffn, mode="tn", out_dtype=BF16, name="gw_down")
    dup, dconv_ffn, db_conv_ffn = ffn_act_bwd(dact, sv["up"], sm["conv_ffn"], sm["b_conv_ffn"])
    dh2 = mm(dup, w["up"], mode="nt", out_dtype=F32, name="d_h2", b_shards=True)
    gw_up = mm(sv["h2"], dup, mode="tn", out_dtype=BF16, name="gw_up", out_shards=N_CHIPS)
    dxpre1, dmix, dsc_f, dsh_f, dgate_t, dln1_g, dln1_b = ln1_bwd(
        dxpre2, dh2, sv["xpre1"], sv["mix"], mod, sm["ln1_g"], sm["ln1_b"])
    dmerged = mm(dmix, w["o"], mode="nt", out_dtype=F32, name="d_merged")
    gw_o = mm(sv["merged"], dmix, mode="tn", out_dtype=BF16, name="gw_o")
    dya, dyb, dgates, db_gate = merge_bwd(dmerged, sv["gates_raw"], sm["b_gate"], sv["ya"], sv["yb"])
    do_a = mm(dya, w["branch_a"], mode="nt", out_dtype=F32, name="d_oa")
    gw_branch_a = mm(sv["o_a"], dya, mode="tn", out_dtype=BF16, name="gw_branch_a")
    do_b = mm(dyb, w["branch_b"], mode="nt", out_dtype=BF16, name="d_ob")
    gw_branch_b = mm(sv["o_b"], dyb, mode="tn", out_dtype=BF16, name="gw_branch_b")
    bias = sv["bias"]
    if early_grads is not None:
        bias = bias + early_grads(dict(w_branch_a=gw_branch_a, w_branch_b=gw_branch_b, w_o=gw_o, w_up=gw_up,
                                       w_down=gw_down))[0, 0]
    dq_b, dk_pad, dv_pad, dbias = attn_bwd(sv["qkv_pad"], bias, do_b)
    dqkvb = jnp.concatenate([dq_b, dk_pad[PAD_ROWS:].astype(BF16), dv_pad[PAD_ROWS:].astype(BF16)], axis=1)
    drel_bias = bias_reduce(jnp.transpose(dbias, (1, 0, 2)))
    do_pre, dz, dnorm_a = gate_a_bwd(do_a, sv["o_pre"], sv["z"], sm["norm_a"])
    dq, dk, dv, dbeta, dg = delta_bwd(sv["q"], sv["k"], sv["v"], sv["beta"], sv["g"], sv["sprev"], sv["tinv"], do_pre)
    dqkv_raw, dba16, dconv_a, da_log, ddt_bias = prep_a_bwd(
        sv["qkv_raw"], sv["ba"], sm["conv_a"], sm["a_log"], sm["dt_bias"], dq, dk, dv, dbeta, dg)
    dba = jnp.pad(dba16, ((0, 0), (0, BA_PAD - 2 * A_HEADS))).astype(BF16)
    pieces = dict(qkv=dqkv_raw, z=dz, ba=dba, qkvb=dqkvb, g=dgates)
    gw_in = join_w_in({key: mm(sv["h1"], dpiece, mode="tn", out_dtype=BF16, name="gw_in_" + key)
                       for key, dpiece in pieces.items()})
    w_ba = w["ba"]
    if early_w_in is not None:
        w_ba = w_ba + early_w_in(gw_in)[0, 0].astype(BF16)
    dh1 = mm(pieces["ba"], w_ba, mode="nt", out_dtype=F32, name="d_h1_ba")
    for key in ("qkv", "z", "qkvb", "g"):
        dh1 = mm(pieces[key], w[key], mode="nt", out_dtype=F32, name="d_h1_" + key, acc_in=dh1)
    grad_x, dsc_t, dsh_t = grad_x_final(dh1, x, dxpre1, mod)
    dmod = jnp.concatenate([dsh_t, dsc_t, dgate_t, dsh_f, dsc_f, fin["gate_f"]], axis=0)
    gw = dict(w_in=gw_in, w_branch_a=gw_branch_a, w_branch_b=gw_branch_b, w_o=gw_o, w_up=gw_up, w_down=gw_down)
    gs = dict(b_gate=db_gate, conv_a=dconv_a, a_log=da_log, dt_bias=ddt_bias, norm_a=dnorm_a, rel_bias=drel_bias,
              ln1_g=dln1_g, ln1_b=dln1_b, conv_ffn=dconv_ffn, b_conv_ffn=db_conv_ffn, ln2_g=fin["ln2_g"], ln2_b=fin["ln2_b"])
    return grad_x, dmod, gw, gs


MESH = pl.DeviceIdType.MESH
ANY = pl.BlockSpec(memory_space=pl.ANY)
WHOLE_VMEM = pl.BlockSpec(memory_space=pltpu.VMEM)


def _place():
    return lax.axis_index("x"), lax.axis_index("y"), lax.axis_index("c")


def allgather8(blk, name):
    m_per, n = blk.shape

    def body(x_ref, out_ref, send_sems, recv_sems, local_sem):
        x, y, c = _place()
        me, sibling = (x, y, c), (x, y, 1 - c)
        chips = [(1 - x, y), (x, 1 - y), (1 - x, 1 - y)]

        def rows(px, py, pc):
            return out_ref.at[pl.ds((4 * px + 2 * py + pc) * m_per, m_per), :]

        def copy(k, block, to, src=None):
            return pltpu.make_async_remote_copy(
                src_ref=rows(*block) if src is None else src, dst_ref=rows(*block),
                send_sem=send_sems.at[k], recv_sem=recv_sems.at[k], device_id=to, device_id_type=MESH)

        mine = pltpu.make_async_copy(x_ref, rows(*me), local_sem)
        mine.start()
        first = [copy(0, me, sibling, src=x_ref)]
        first += [copy(1 + j, me, (*chip, c), src=x_ref) for j, chip in enumerate(chips)]
        for cp in first:
            cp.start()
        passed = [copy(4 + j, (*chip, c), sibling) for j, chip in enumerate(chips)]
        for j, chip in enumerate(chips):
            copy(1 + j, (*chip, c), me).wait_recv()
            passed[j].start()
        copy(0, sibling, me).wait_recv()
        for j, chip in enumerate(chips):
            copy(4 + j, (*chip, 1 - c), me).wait_recv()
        for cp in first + passed:
            cp.wait_send()
        mine.wait()

    return pl.pallas_call(
        body, name=name, out_shape=jax.ShapeDtypeStruct((N_DEV * m_per, n), blk.dtype),
        in_specs=[WHOLE_VMEM], out_specs=WHOLE_VMEM,
        scratch_shapes=[pltpu.SemaphoreType.DMA((7,)), pltpu.SemaphoreType.DMA((7,)), pltpu.SemaphoreType.DMA],
    )(blk)


def _chip_peers(x, y):
    return [(1 - x, y), (x, 1 - y), (1 - x, 1 - y)]


def chip_exchange(arrs, name, scatter):
    n = len(arrs)

    def body(*refs):
        ins, outs = refs[:n], refs[n:2 * n]
        send_sems, recv_sems, local_sems = refs[2 * n:]
        x, y, c = _place()
        me = 2 * x + y
        sibling = (x, y, 1 - c)
        peers = _chip_peers(x, y)

        def half(ref, which):
            r2 = ref.shape[0] // 2
            return ref.at[pl.ds(which * r2, r2), :]

        def outgoing(a, chip):
            return ins[a].at[chip] if scatter else ins[a]

        def copy(k, src, dst, to):
            return pltpu.make_async_remote_copy(src_ref=src, dst_ref=dst, send_sem=send_sems.at[k],
                                                recv_sem=recv_sems.at[k], device_id=to, device_id_type=MESH)

        started, local = [], []
        for a in range(n):
            lc = pltpu.make_async_copy(outgoing(a, me), outs[a].at[me], local_sems.at[a])
            lc.start()
            local.append(lc)
            for j, (px, py) in enumerate(peers):
                cp = copy(6 * a + j, half(outgoing(a, 2 * px + py), c), half(outs[a].at[me], c), (px, py, c))
                cp.start()
                started.append(cp)
        for a in range(n):
            for j, (px, py) in enumerate(peers):
                landed = half(outs[a].at[2 * px + py], c)
                copy(6 * a + j, landed, landed, (px, py, c)).wait_recv()
                relay = copy(6 * a + 3 + j, landed, landed, sibling)
                relay.start()
                started.append(relay)
        for a in range(n):
            for j, (px, py) in enumerate(peers):
                other = half(outs[a].at[2 * px + py], 1 - c)
                copy(6 * a + 3 + j, other, other, sibling).wait_recv()
        for cp in started:
            cp.wait_send()
        for lc in local:
            lc.wait()

    out_shape = [jax.ShapeDtypeStruct(a.shape if scatter else (N_CHIPS,) + a.shape, a.dtype) for a in arrs]
    return pl.pallas_call(
        body, name=name, out_shape=out_shape, in_specs=[ANY] * n, out_specs=[ANY] * n,
        scratch_shapes=[pltpu.SemaphoreType.DMA((6 * n,)), pltpu.SemaphoreType.DMA((6 * n,)), pltpu.SemaphoreType.DMA((n,))],
    )(*arrs)


HBM_SPEC = pl.BlockSpec(memory_space=pltpu.HBM)
SEM_SPEC = pl.BlockSpec(memory_space=pltpu.SEMAPHORE)
SIDE_EFFECT = pltpu.SideEffectType.DATAFLOW_SIDE_EFFECTING


def _in_hbm(a):
    return pltpu.with_memory_space_constraint(a, pltpu.HBM)


def exchange_start(arrs, name, scatter, after):
    n = len(arrs)
    lands = [lax.empty(a.shape if scatter else (N_CHIPS,) + a.shape, a.dtype) for a in arrs]

    def body(*refs):
        ins, zones = refs[:n], refs[n:2 * n]
        send_sems, recv_sems, token = refs[2 * n + 1], refs[2 * n + 2], refs[-1]
        x, y, c = _place()
        me = 2 * x + y
        for a in range(n):
            for j, (px, py) in enumerate(_chip_peers(x, y)):
                pltpu.make_async_remote_copy(
                    src_ref=ins[a].at[2 * px + py] if scatter else ins[a], dst_ref=zones[a].at[me],
                    send_sem=send_sems.at[3 * a + j], recv_sem=recv_sems.at[3 * a + j],
                    device_id=(px, py, c), device_id_type=MESH).start()
        token[...] = jnp.zeros_like(token)

    res = pl.pallas_call(
        body, name=name,
        out_shape=[pltpu.SemaphoreType.DMA((3 * n,)), pltpu.SemaphoreType.DMA((3 * n,))]
        + [pltpu.HBM(a.shape, a.dtype) for a in arrs] + [pltpu.HBM(z.shape, z.dtype) for z in lands]
        + [jax.ShapeDtypeStruct((8, 128), F32)],
        in_specs=[HBM_SPEC] * (2 * n) + [ANY], out_specs=[SEM_SPEC, SEM_SPEC] + [HBM_SPEC] * (2 * n) + [WHOLE_VMEM],
        input_output_aliases={i: 2 + i for i in range(2 * n)},
        compiler_params=pltpu.CompilerParams(has_side_effects=SIDE_EFFECT),
    )(*[_in_hbm(a) for a in arrs], *[_in_hbm(z) for z in lands], after)
    return dict(send=res[0], recv=res[1], src=res[2:2 + n], zones=res[2 + n:2 + 2 * n], token=res[-1], scatter=scatter)


def exchange_wait(handle, name, after):
    srcs, zones, scatter = handle["src"], handle["zones"], handle["scatter"]
    n = len(srcs)

    def body(*refs):
        ins, lands = refs[:n], refs[n:2 * n]
        send_sems, recv_sems = refs[2 * n], refs[2 * n + 1]
        x, y, c = _place()
        me = 2 * x + y
        for a in range(n):
            for j, (px, py) in enumerate(_chip_peers(x, y)):
                cp = pltpu.make_async_remote_copy(
                    src_ref=ins[a].at[me] if scatter else ins[a], dst_ref=lands[a].at[2 * px + py],
                    send_sem=send_sems.at[3 * a + j], recv_sem=recv_sems.at[3 * a + j],
                    device_id=(px, py, c), device_id_type=MESH)
                cp.wait_send()
                cp.wait_recv()

    res = pl.pallas_call(
        body, name=name, out_shape=[pltpu.HBM(a.shape, a.dtype) for a in list(srcs) + list(zones)],
        in_specs=[HBM_SPEC] * (2 * n) + [SEM_SPEC, SEM_SPEC, ANY], out_specs=[HBM_SPEC] * (2 * n),
        input_output_aliases={i: i for i in range(2 * n)},
        compiler_params=pltpu.CompilerParams(has_side_effects=SIDE_EFFECT),
    )(*srcs, *zones, handle["send"], handle["recv"], after)
    return res[n:]


def sibling_exchange(arrs, name):
    n = len(arrs)

    def body(*refs):
        ins, outs = refs[:n], refs[n:2 * n]
        send_sems, recv_sems = refs[2 * n:]
        x, y, c = _place()
        cps = [pltpu.make_async_remote_copy(src_ref=ins[a], dst_ref=outs[a], send_sem=send_sems.at[a],
                                            recv_sem=recv_sems.at[a], device_id=(x, y, 1 - c), device_id_type=MESH)
               for a in range(n)]
        for cp in cps:
            cp.start()
        for cp in cps:
            cp.wait()

    return pl.pallas_call(
        body, name=name, out_shape=[jax.ShapeDtypeStruct(a.shape, a.dtype) for a in arrs],
        in_specs=[ANY] * n, out_specs=[ANY] * n,
        scratch_shapes=[pltpu.SemaphoreType.DMA((n,)), pltpu.SemaphoreType.DMA((n,))],
    )(*arrs)


TILE_BYTES = 2 * 1024 * 1024


def _row_tile(rows, row_bytes):
    if rows * row_bytes <= TILE_BYTES or rows % 8:
        return rows
    best = 8
    for t in range(8, rows + 1, 8):
        if rows % t == 0 and t * row_bytes <= TILE_BYTES:
            best = t
    return best


def pair_add(a, b, name):
    shape = a.shape
    a, b = a.reshape(-1, shape[-1]), b.reshape(-1, shape[-1])
    R, C = a.shape
    tr = _row_tile(R, C * 4)

    def body(a_ref, b_ref, o_ref):
        o_ref[...] = (a_ref[...].astype(F32) + b_ref[...].astype(F32)).astype(BF16)

    spec = pl.BlockSpec((tr, C), lambda i: (i, 0))
    return pl.pallas_call(body, name=name, grid=(R // tr,), in_specs=[spec, spec], out_specs=spec,
                          out_shape=jax.ShapeDtypeStruct((R, C), BF16), compiler_params=_cparams(("parallel",)))(a, b).reshape(shape)


def sum_lead(parts, name):
    K, R, C = parts.shape
    tr = _row_tile(R, C * 4)

    def body(p_ref, o_ref):
        acc = p_ref[0].astype(F32)
        for j in range(1, K):
            acc = acc + p_ref[j].astype(F32)
        o_ref[...] = acc

    return pl.pallas_call(
        body, name=name, grid=(R // tr,), in_specs=[pl.BlockSpec((K, tr, C), lambda i: (0, i, 0))],
        out_specs=pl.BlockSpec((tr, C), lambda i: (i, 0)), out_shape=jax.ShapeDtypeStruct((R, C), F32),
        compiler_params=_cparams(("parallel",)))(parts)


def adamw(w, g, m, v, name):
    R, C = w.shape
    tr = _row_tile(R, C * 4)

    def body(w_ref, g_ref, m_ref, v_ref, d_ref, mo_ref, vo_ref):
        gv = g_ref[...]
        m2 = ADAM_B1 * m_ref[...] + (1.0 - ADAM_B1) * gv
        v2 = ADAM_B2 * v_ref[...] + (1.0 - ADAM_B2) * (gv * gv)
        m_hat = m2 / (1.0 - ADAM_B1 ** ADAM_STEP)
        v_hat = v2 / (1.0 - ADAM_B2 ** ADAM_STEP)
        d_ref[...] = -ADAM_LR * (m_hat / (jnp.sqrt(v_hat) + ADAM_EPS) + ADAM_WD * w_ref[...])
        mo_ref[...] = m2
        vo_ref[...] = v2

    spec = pl.BlockSpec((tr, C), lambda i: (i, 0))
    return pl.pallas_call(body, name=name, grid=(R // tr,), in_specs=[spec] * 4, out_specs=[spec] * 3,
                          out_shape=[jax.ShapeDtypeStruct((R, C), F32)] * 3, compiler_params=_cparams(("parallel",)))(w, g, m, v)


LANES = 1024


def _pack(arrs, rows):
    out, offs, r = [], [], 0
    for a in arrs:
        flat = a.reshape(-1)
        nr = -(-flat.shape[0] // LANES)
        out.append(jnp.pad(flat, (0, nr * LANES - flat.shape[0])))
        offs.append(r)
        r += nr
    assert r <= rows, (r, rows)
    out.append(jnp.zeros(((rows - r) * LANES,), F32))
    return jnp.concatenate(out).reshape(rows, LANES), offs


def _unpack(packed, offs, shapes):
    flat = packed.reshape(-1)
    return [flat[o * LANES:o * LANES + math.prod(s)].reshape(s) for o, s in zip(offs, shapes)]


WEIGHTS = ["w_ada", "b_ada", "w_in", "b_gate", "conv_a", "a_log", "dt_bias", "norm_a", "rel_bias", "w_branch_a",
           "w_branch_b", "w_o", "ln1_g", "ln1_b", "w_up", "conv_ffn", "b_conv_ffn", "w_down", "ln2_g", "ln2_b"]
BIG = ["w_in", "w_branch_a", "w_branch_b", "w_o", "w_up", "w_down"]
LATE = [n for n in BIG if n != "w_in"]
KEPT_SHARDED = {"w_up"}
COL_SHARDED = {"w_in", "w_up"}
SMALL_SHARDED = {"conv_a": 3 * A_W // N_CHIPS, "rel_bias": B_REL // N_CHIPS, "conv_ffn": 2 * D_FF // N_CHIPS}
SMALL = [n for n in WEIGHTS if n not in BIG and n != "w_ada"]


def _to_full(g4, name):
    if name in KEPT_SHARDED:
        return g4
    if name in COL_SHARDED:
        return jnp.transpose(g4, (1, 0, 2)).reshape(g4.shape[1], -1)
    return g4.reshape(-1, g4.shape[2])


def _to_shards(full, name):
    if name in KEPT_SHARDED:
        return full
    if name in COL_SHARDED:
        return jnp.transpose(full.reshape(full.shape[0], N_CHIPS, -1), (1, 0, 2))
    return full.reshape(N_CHIPS, -1, full.shape[1])


def kernel(x, c, w_ada, b_ada, w_in, b_gate, conv_a, a_log, dt_bias, norm_a, rel_bias, w_branch_a, w_branch_b, w_o, ln1_g, ln1_b, w_up, conv_ffn, b_conv_ffn, w_down, ln2_g, ln2_b, loss_target, m_w_ada, m_b_ada, m_w_in, m_b_gate, m_conv_a, m_a_log, m_dt_bias, m_norm_a, m_rel_bias, m_w_branch_a, m_w_branch_b, m_w_o, m_ln1_g, m_ln1_b, m_w_up, m_conv_ffn, m_b_conv_ffn, m_w_down, m_ln2_g, m_ln2_b, v_w_ada, v_b_ada, v_w_in, v_b_gate, v_conv_a, v_a_log, v_dt_bias, v_norm_a, v_rel_bias, v_w_branch_a, v_w_branch_b, v_w_o, v_ln1_g, v_ln1_b, v_w_up, v_conv_ffn, v_b_conv_ffn, v_w_down, v_ln2_g, v_ln2_b):
    args = dict(locals())
    wts = {n: args[n] for n in WEIGHTS}
    moms = {n: args["m_" + n] for n in WEIGHTS}
    vars_ = {n: args["v_" + n] for n in WEIGHTS}
    xi, yi, ci = _place()
    chip = 2 * xi + yi
    dev = 4 * xi + 2 * yi + ci
    ada_cols = w_ada.shape[2]

    c_all = allgather8(jnp.pad(c, ((0, 7), (0, 0))), "gather_c").reshape(N_DEV, 8, D_MODEL)[:, 0]
    b_ada_sh = lax.dynamic_slice(b_ada, (0, chip * ada_cols), (1, ada_cols))
    mod_sh = ada_fwd(c_all, w_ada[0], b_ada_sh)
    mod_g = allgather8(mod_sh, "gather_mod").reshape(N_CHIPS, 2, N_DEV, ada_cols)[:, 0]
    mod = lax.dynamic_slice(mod_g, (0, dev, 0), (N_CHIPS, 1, ada_cols)).reshape(6, D_MODEL)

    (w_in_g4,) = chip_exchange([wts["w_in"][0].astype(BF16)], "gather_w_in", scatter=False)
    wd = split_w_in(_to_full(w_in_g4, "w_in"))
    late_shards = [wts[n][0].astype(BF16) for n in LATE]
    late_gather = exchange_start(late_shards, "gather_late_start", scatter=False, after=w_in_g4)
    mod = mod + late_gather["token"][0, 0]

    def late_weights(after):
        zones = exchange_wait(late_gather, "gather_late_wait", after)
        full = [_to_full(lax.dynamic_update_slice(z, s[None], (chip, 0, 0)), n) for n, z, s in zip(LATE, zones, late_shards)]
        return {n[2:]: f for n, f in zip(LATE, full)}

    sshapes = [wts[n].shape[1:] for n in SMALL_SHARDED]
    spack, soffs = _pack([wts[n][0] for n in SMALL_SHARDED], 16)
    sg = allgather8(spack, "gather_small_w").reshape(N_CHIPS, 2, 16, LANES)[:, 0]
    sparts = [_unpack(sg[j], soffs, sshapes) for j in range(N_CHIPS)]
    sm = {n: wts[n] for n in SMALL if n not in SMALL_SHARDED and n != "b_ada"}
    for i, n in enumerate(SMALL_SHARDED):
        sm[n] = jnp.concatenate([sparts[j][i] for j in range(N_CHIPS)], axis=-1)

    early = {}

    def early_grads(g):
        mine = [g[n] for n in LATE]
        theirs = sibling_exchange(mine, "grad_sibling_late")
        early["sums"] = [_to_shards(pair_add(a, b, "grad_pair_" + n), n) for n, a, b in zip(LATE, mine, theirs)]
        early["scatter"] = exchange_start(early["sums"], "grad_scatter_start", scatter=True, after=theirs[0])
        return early["scatter"]["token"]

    def early_w_in(g):
        (theirs,) = sibling_exchange([g], "grad_sibling_w_in")
        early["sum_in"] = _to_shards(pair_add(g, theirs, "grad_pair_w_in"), "w_in")
        early["scatter_in"] = exchange_start([early["sum_in"]], "grad_scatter_w_in_start", scatter=True, after=theirs)
        return early["scatter_in"]["token"]

    loss, dxpre2, dffn, fin, sv = forward_local(x[0], loss_target[0], mod, wd, sm, late_weights)
    grad_x, dmod, gw, gs = backward_local(x[0], mod, sm, dxpre2, dffn, fin, sv, early_grads, early_w_in)

    gnames = [n for n in SMALL if n != "b_ada"]
    vec, voffs = _pack([dmod] + [gs[n] for n in gnames] + [loss], 56)
    gathered = allgather8(vec, "gather_small_g").reshape(N_DEV, 56, LANES)
    summed = sum_lead(gathered, "sum_small_g")
    full_shapes = [(6, D_MODEL)] + [gs[n].shape for n in gnames] + [(1, 1)]
    parts = _unpack(summed, voffs, full_shapes)
    grads = {"b_ada": parts[0].reshape(1, -1)}
    for n, p in zip(gnames, parts[1:-1]):
        if n in SMALL_SHARDED:
            p = lax.dynamic_slice_in_dim(p, chip * SMALL_SHARDED[n], SMALL_SHARDED[n], axis=1)
        grads[n] = p.reshape(wts[n].shape)
    loss_total = parts[-1].reshape(())
    dmod_all = gathered[:, 0:6, :].reshape(N_DEV, 6 * D_MODEL)
    grads["w_ada"] = ada_bwd(c_all, lax.dynamic_slice(dmod_all, (0, chip * ada_cols), (N_DEV, ada_cols)))[None]

    def own_slot(zone, sums):
        return lax.dynamic_update_slice(zone, lax.dynamic_slice_in_dim(sums, chip, 1, axis=0), (chip, 0, 0))

    zones = exchange_wait(early["scatter"], "grad_scatter_wait", summed)
    for n, z, s in zip(LATE, zones, early["sums"]):
        grads[n] = sum_lead(own_slot(z, s), "grad_sum_" + n)[None]

    delta, new_m, new_v = {}, {}, {}

    def update(n):
        d, m2, v2 = adamw(wts[n][0], grads[n][0], moms[n][0], vars_[n][0], "adamw_" + n)
        delta[n], new_m[n], new_v[n] = d[None], m2[None], v2[None]

    for n in ["w_ada"] + LATE:
        update(n)
    shapes = [wts[n].shape for n in SMALL]
    packs = [_pack([t[n] for n in SMALL], 32) for t in (wts, grads, moms, vars_)]
    outs = adamw(*[p[0] for p in packs], "adamw_small")
    for res, o in zip((delta, new_m, new_v), outs):
        for n, a in zip(SMALL, _unpack(o, packs[0][1], shapes)):
            res[n] = a
    (zone_in,) = exchange_wait(early["scatter_in"], "grad_scatter_w_in_wait", outs[0])
    grads["w_in"] = sum_lead(own_slot(zone_in, early["sum_in"]), "grad_sum_w_in")[None]
    update("w_in")
    return (loss_total, grad_x[None], *[grads[n] for n in WEIGHTS], *[delta[n] for n in WEIGHTS],
            *[new_m[n] for n in WEIGHTS], *[new_v[n] for n in WEIGHTS])
```

```python
import functools
import math

import jax
import jax.numpy as jnp
from jax import lax
from jax.experimental import pallas as pl
from jax.experimental.pallas import tpu as pltpu

F32 = jnp.float32
BF16 = jnp.bfloat16

D_MODEL = 1024
CHUNK = 64
A_HEADS = 8
A_DK = 128
A_W = A_HEADS * A_DK
A_CONV = 4
B_HEADS = 16
B_DH = 64
B_W = B_HEADS * B_DH
B_PREV = 8
B_BAND = (B_PREV + 1) * CHUNK
B_MAX_REL = 256
B_REL = CHUNK - 1 + B_MAX_REL + 1
D_FF = 2816
FFN_CONV = 3
IN_COLS = 4 * A_W + 2 * A_HEADS + 3 * B_W + 2 * D_MODEL
ALPHA = 2.0 ** 0.25
LN_EPS = 1e-5
RMS_EPS = 1e-6
L2_EPS = 1e-6
NEG_INF = -1e30
ADAM_LR, ADAM_B1, ADAM_B2, ADAM_EPS, ADAM_WD, ADAM_STEP = 0.001, 0.9, 0.999, 1e-08, 0.01, 10
N_CHIPS = 4
N_DEV = 8
VMEM_LIMIT = 56 * 1024 * 1024


def _cparams(sem=None):
    return pltpu.CompilerParams(dimension_semantics=sem, vmem_limit_bytes=VMEM_LIMIT)


_DIMS = {"nn": (((1,), (0,)), ((), ())), "nt": (((1,), (1,)), ((), ())), "tn": (((0,), (0,)), ((), ()))}


MM_TILE_CAP = 1408


def _mm_tile(n):
    return max(t for t in range(128, min(n, MM_TILE_CAP) + 1, 128) if n % t == 0)


def mm(a, b, *, mode, out_dtype, name, acc_in=None, b_shards=False, out_shards=0):
    b_rows, b_cols = (b.shape[1], b.shape[0] * b.shape[2]) if b_shards else b.shape
    if mode == "nn":
        (M, K), (K2, N) = a.shape, (b_rows, b_cols)
    elif mode == "nt":
        (M, K), (N, K2) = a.shape, (b_rows, b_cols)
    else:
        (K, M), (K2, N) = a.shape, (b_rows, b_cols)
    assert K == K2, (a.shape, b.shape, mode)
    tm, tn, tk = _mm_tile(M), _mm_tile(N), _mm_tile(K)
    nk = K // tk

    def body(*refs):
        if acc_in is None:
            a_ref, b_ref, o_ref, acc_ref = refs
        else:
            a_ref, b_ref, c_ref, o_ref, acc_ref = refs
        k = pl.program_id(2)

        @pl.when(k == 0)
        def _():
            if acc_in is None:
                acc_ref[...] = jnp.zeros_like(acc_ref)
            else:
                acc_ref[...] = c_ref[...]

        acc_ref[...] += lax.dot_general(a_ref[...].astype(BF16), b_ref[...].astype(BF16), _DIMS[mode],
                                        preferred_element_type=F32)

        @pl.when(k == nk - 1)
        def _():
            o_ref[...] = acc_ref[...].astype(out_dtype)

    a_spec = pl.BlockSpec((tk, tm), lambda i, j, k: (k, i)) if mode == "tn" else pl.BlockSpec((tm, tk), lambda i, j, k: (i, k))
    if b_shards:
        assert (tk if mode == "nt" else tn) == b.shape[2] and mode != "tn", (b.shape, tn, tk, mode)
        b_spec = (pl.BlockSpec((None, tn, tk), lambda i, j, k: (k, j, 0)) if mode == "nt"
                  else pl.BlockSpec((None, tk, tn), lambda i, j, k: (j, k, 0)))
    else:
        b_spec = pl.BlockSpec((tn, tk), lambda i, j, k: (j, k)) if mode == "nt" else pl.BlockSpec((tk, tn), lambda i, j, k: (k, j))
    o_spec = pl.BlockSpec((tm, tn), lambda i, j, k: (i, j))
    out_shape = jax.ShapeDtypeStruct((M, N), out_dtype)
    if out_shards:
        assert N == out_shards * tn and acc_in is None, (N, tn, out_shards)
        o_spec = pl.BlockSpec((None, tm, tn), lambda i, j, k: (j, i, 0))
        out_shape = jax.ShapeDtypeStruct((out_shards, M, tn), out_dtype)
    ins, in_specs, aliases = [a, b], [a_spec, b_spec], {}
    if acc_in is not None:
        assert acc_in.shape == (M, N) and acc_in.dtype == F32 and out_dtype == F32
        ins.append(acc_in)
        in_specs.append(o_spec)
        aliases = {2: 0}
    return pl.pallas_call(
        body, name=name, grid=(M // tm, N // tn, nk), in_specs=in_specs, out_specs=o_spec,
        out_shape=out_shape, scratch_shapes=[pltpu.VMEM((tm, tn), F32)],
        input_output_aliases=aliases, compiler_params=_cparams(("parallel", "parallel", "arbitrary")),
    )(*ins)


def rowcall(body, *, name, S, ts, ins, outs, scratch=()):
    assert S % ts == 0 and ts % 16 == 0
    nsteps = S // ts
    in_specs, arrays = [], []
    for arr, kind in ins:
        arrays.append(arr)
        if kind == "row":
            in_specs.append(pl.BlockSpec((ts, arr.shape[1]), lambda i: (i, 0)))
        elif kind in ("prev", "next"):
            hr = 8 * (4 // arr.dtype.itemsize)
            per, last = ts // hr, S // hr - 1
            if kind == "prev":
                in_specs.append(pl.BlockSpec((hr, arr.shape[1]), lambda i, per=per: (jnp.maximum(i * per - 1, 0), 0)))
            else:
                in_specs.append(pl.BlockSpec((hr, arr.shape[1]), lambda i, per=per, last=last: (jnp.minimum((i + 1) * per, last), 0)))
        else:
            nd = arr.ndim
            in_specs.append(pl.BlockSpec(arr.shape, lambda i, nd=nd: (0,) * nd))
    out_specs, out_shapes, acc_idx = [], [], []
    for n, (shape, dtype, kind) in enumerate(outs):
        out_shapes.append(jax.ShapeDtypeStruct(shape, dtype))
        if kind == "row":
            out_specs.append(pl.BlockSpec((ts, shape[1]), lambda i: (i, 0)))
        else:
            nd = len(shape)
            out_specs.append(pl.BlockSpec(shape, lambda i, nd=nd: (0,) * nd))
            acc_idx.append(n)
    n_in = len(arrays)

    def wrapped(*refs):
        @pl.when(pl.program_id(0) == 0)
        def _():
            for n in acc_idx:
                refs[n_in + n][...] = jnp.zeros_like(refs[n_in + n])

        body(*refs)

    res = pl.pallas_call(
        wrapped, name=name, grid=(nsteps,), in_specs=in_specs, out_specs=out_specs, out_shape=out_shapes,
        scratch_shapes=list(scratch), compiler_params=_cparams(("arbitrary",) if acc_idx else ("parallel",)),
    )(*arrays)
    return res


def _halo_prev(ref):
    v = ref[...].astype(F32)
    return v[v.shape[0] - 8:]


def _halo_next(ref):
    return ref[...].astype(F32)[:8]


def _shift_down(cur, prev8, k):
    if k == 0:
        return cur
    rolled = pltpu.roll(cur, k, axis=0)
    fix = pltpu.roll(prev8, k, axis=0)
    row = lax.broadcasted_iota(jnp.int32, (8, 1), 0)
    top = jnp.where(row < k, fix, rolled[0:8])
    if cur.shape[0] == 8:
        return top
    return jnp.concatenate([top, rolled[8:]], axis=0)


def _shift_up(cur, next8, k):
    if k == 0:
        return cur
    n = cur.shape[0]
    rolled = pltpu.roll(cur, n - k, axis=0)
    fix = pltpu.roll(next8, 8 - k, axis=0)
    row = lax.broadcasted_iota(jnp.int32, (8, 1), 0)
    bot = jnp.where(row >= 8 - k, fix, rolled[n - 8:n])
    return jnp.concatenate([rolled[:n - 8], bot], axis=0)


def _sigmoid(x):
    return 1.0 / (1.0 + jnp.exp(-x))


def _silu(x):
    return x * _sigmoid(x)


def _silu_and_grad(x):
    s = _sigmoid(x)
    return x * s, s * (1.0 + x * (1.0 - s))


def _softplus(x):
    return jnp.maximum(x, 0.0) + jnp.log1p(jnp.exp(-jnp.abs(x)))


def _split2(x):
    hi = x.astype(BF16)
    return hi, (x - hi.astype(F32)).astype(BF16)


def _dot1(a, b, mode):
    return lax.dot_general(a.astype(BF16), b.astype(BF16), _DIMS[mode], preferred_element_type=F32)


def _dot3(a, b, mode):
    ah, al = _split2(a)
    bh, bl = _split2(b)
    d = lambda p, q: lax.dot_general(p, q, _DIMS[mode], preferred_element_type=F32)
    return d(ah, bh) + (d(ah, bl) + d(al, bh))


def ada_fwd(c_all, w_sh, b_sh):
    n = w_sh.shape[1]
    tn = 512

    def body(c_ref, w_ref, b_ref, o_ref):
        o_ref[...] = _dot1(_silu(c_ref[...]), w_ref[...], "nn") + b_ref[...]

    return pl.pallas_call(
        body, name="ada_fwd", grid=(n // tn,),
        in_specs=[pl.BlockSpec((N_DEV, D_MODEL), lambda j: (0, 0)), pl.BlockSpec((D_MODEL, tn), lambda j: (0, j)),
                  pl.BlockSpec((1, tn), lambda j: (0, j))],
        out_specs=pl.BlockSpec((N_DEV, tn), lambda j: (0, j)), out_shape=jax.ShapeDtypeStruct((N_DEV, n), F32),
        compiler_params=_cparams(("parallel",)),
    )(c_all, w_sh, b_sh)


def ada_bwd(c_all, dmod_sh):
    n = dmod_sh.shape[1]
    tn = 512

    def body(c_ref, d_ref, o_ref):
        o_ref[...] = _dot1(_silu(c_ref[...]), d_ref[...], "tn")

    return pl.pallas_call(
        body, name="ada_bwd", grid=(n // tn,),
        in_specs=[pl.BlockSpec((N_DEV, D_MODEL), lambda j: (0, 0)), pl.BlockSpec((N_DEV, tn), lambda j: (0, j))],
        out_specs=pl.BlockSpec((D_MODEL, tn), lambda j: (0, j)), out_shape=jax.ShapeDtypeStruct((D_MODEL, n), F32),
        compiler_params=_cparams(("parallel",)),
    )(c_all, dmod_sh)


SHIFT_T, SCALE_T, GATE_T, SHIFT_F, SCALE_F, GATE_F = range(6)


def modulate(x, mod, shift_row, scale_row, name):
    S = x.shape[0]

    def body(x_ref, m_ref, o_ref):
        m = m_ref[...]
        o_ref[...] = (x_ref[...] * (1.0 + m[scale_row:scale_row + 1]) + m[shift_row:shift_row + 1]).astype(BF16)

    return rowcall(body, name=name, S=S, ts=512, ins=[(x, "row"), (mod, "vec")], outs=[((S, D_MODEL), BF16, "row")])[0]


def _conv_fwd(cur, prev, w, width):
    y = cur * w[width - 1:width]
    for j in range(width - 1):
        y = y + _shift_down(cur, prev, width - 1 - j) * w[j:j + 1]
    return y


def _prep_a_core(cur, prev, w):
    return _silu_and_grad(_conv_fwd(cur, prev, w, A_CONV))


def prep_a_fwd(qkv_raw, ba, conv_a, a_log, dt_bias):
    S = qkv_raw.shape[0]

    def body(x_ref, xp_ref, ba_ref, w_ref, al_ref, dt_ref, q_ref, k_ref, v_ref, beta_ref, g_ref):
        first = (pl.program_id(0) > 0).astype(F32)
        y, _ = _prep_a_core(x_ref[...].astype(F32), _halo_prev(xp_ref) * first, w_ref[...])
        for h in range(A_HEADS):
            sl = slice(h * A_DK, (h + 1) * A_DK)
            qh = y[:, sl]
            kh = y[:, A_W + h * A_DK:A_W + (h + 1) * A_DK]
            q_ref[:, sl] = qh * (lax.rsqrt(jnp.sum(qh * qh, axis=-1, keepdims=True) + L2_EPS) * (A_DK ** -0.5))
            k_ref[:, sl] = kh * lax.rsqrt(jnp.sum(kh * kh, axis=-1, keepdims=True) + L2_EPS)
        v_ref[...] = y[:, 2 * A_W:3 * A_W]
        bav = ba_ref[...]
        beta_ref[...] = _sigmoid(bav[:, 0:A_HEADS])
        g_ref[...] = -jnp.exp(al_ref[...]) * _softplus(bav[:, A_HEADS:2 * A_HEADS] + dt_ref[...])

    return rowcall(
        body, name="prep_a_fwd", S=S, ts=256,
        ins=[(qkv_raw, "row"), (qkv_raw, "prev"), (ba, "row"), (conv_a, "vec"), (a_log, "vec"), (dt_bias, "vec")],
        outs=[((S, A_W), F32, "row")] * 3 + [((S, A_HEADS), F32, "row")] * 2)


HEAD_GROUP = 4
GROUP_ROWS = HEAD_GROUP * CHUNK
N_HEAD_GROUPS = A_HEADS // HEAD_GROUP
LOG_CHUNK = int(math.log2(CHUNK))


def _tri_masks():
    rb = lax.broadcasted_iota(jnp.int32, (GROUP_ROWS, GROUP_ROWS), 0)
    cb = lax.broadcasted_iota(jnp.int32, (GROUP_ROWS, GROUP_ROWS), 1)
    same = (rb >> LOG_CHUNK) == (cb >> LOG_CHUNK)
    return dict(causal=same & (rb >= cb), strict=same & (rb > cb), eye=rb == cb, upper=same & (cb >= rb),
                last=cb == (rb | (CHUNK - 1)), rb=rb, cb=cb)


def _col_to_row(colv, eye):
    return jnp.sum(jnp.where(eye, colv, 0.0), axis=0, keepdims=True)


def _row_to_col(rowv, eye):
    return jnp.sum(jnp.where(eye, rowv, 0.0), axis=1, keepdims=True)


def _tri_inv(a_list, mk):
    rb, cb = mk["rb"], mk["cb"]
    ts = [jnp.where(mk["eye"], 1.0, 0.0) - jnp.where((rb >> 1) == (cb >> 1), a, 0.0) for a in a_list]
    for lvl in range(1, LOG_CHUNK):
        rs, cs = rb >> lvl, cb >> lvl
        sel = ((rs & 1) == 1) & (cs == rs - 1)
        inner = [_dot3(t, jnp.where(sel, a, 0.0), "nn") for t, a in zip(ts, a_list)]
        ts = [t - _dot3(i, t, "nn") for i, t in zip(inner, ts)]
    return ts


def _stack_heads(ref, grp):
    return jnp.concatenate([ref[:, (grp * HEAD_GROUP + j) * A_DK:(grp * HEAD_GROUP + j + 1) * A_DK]
                            for j in range(HEAD_GROUP)], axis=0)


def _stack_cols(tile, grp):
    return jnp.concatenate([tile[:, grp * HEAD_GROUP + j:grp * HEAD_GROUP + j + 1] for j in range(HEAD_GROUP)], axis=0)


def _delta_local(q, k, v, beta, g, mk):
    causal, strict, eye = mk["causal"], mk["strict"], mk["eye"]
    g_row = _col_to_row(g, eye)
    gc = jnp.sum(jnp.where(causal, g_row, 0.0), axis=1, keepdims=True)
    gc_row = _col_to_row(gc, eye)
    decay = jnp.where(causal, jnp.exp(jnp.where(causal, gc - gc_row, 0.0)), 0.0)
    gam = jnp.exp(gc)
    kb = k * beta
    vb = v * beta
    y = kb * gam
    a = jnp.where(strict, _dot1(kb, k, "nt") * decay, 0.0)
    p = _dot1(q, k, "nt") * decay
    gl = jnp.sum(jnp.where(mk["last"], gc_row, 0.0), axis=1, keepdims=True)
    kd = k * jnp.exp(gl - gc)
    return dict(gc=gc, decay=decay, gam=gam, kb=kb, vb=vb, y=y, a=a, p=p, gl=gl, kd=kd)


def _head_rows(x, j):
    return x[j * CHUNK:(j + 1) * CHUNK]


def delta_fwd(q, k, v, beta, g):
    S = q.shape[0]
    n_chunks = S // CHUNK

    def body(q_ref, k_ref, v_ref, beta_ref, g_ref, o_ref, sprev_ref, t_ref, state_ref):
        @pl.when(pl.program_id(0) == 0)
        def _():
            state_ref[...] = jnp.zeros_like(state_ref)

        mk = _tri_masks()
        betav, gv = beta_ref[...], g_ref[...]
        groups = range(N_HEAD_GROUPS)
        q_all = [_stack_heads(q_ref, grp) for grp in groups]
        locs = [_delta_local(q_all[grp], _stack_heads(k_ref, grp), _stack_heads(v_ref, grp),
                             _stack_cols(betav, grp), _stack_cols(gv, grp), mk) for grp in groups]
        tinvs = _tri_inv([loc["a"] for loc in locs], mk)
        uws = [_dot3(tinvs[grp], jnp.concatenate([locs[grp]["vb"], locs[grp]["y"]], axis=1), "nn") for grp in groups]
        for grp in groups:
            loc, uw = locs[grp], uws[grp]
            t_ref[0, grp] = tinvs[grp]
            qg = q_all[grp] * loc["gam"]
            egl = jnp.exp(loc["gl"])
            vns, o_state = [], []
            for j in range(HEAD_GROUP):
                h = grp * HEAD_GROUP + j
                s0 = state_ref[h]
                sprev_ref[0, h] = s0
                uw_h = _head_rows(uw, j)
                vn = uw_h[:, :A_DK] - _dot1(uw_h[:, A_DK:], s0, "nn")
                vns.append(vn)
                o_state.append(_dot1(_head_rows(qg, j), s0, "nn"))
                state_ref[h] = s0 * egl[(j + 1) * CHUNK - 1:(j + 1) * CHUNK] + _dot1(_head_rows(loc["kd"], j), vn, "tn")
            o_local = _dot1(loc["p"], jnp.concatenate(vns, axis=0), "nn")
            for j in range(HEAD_GROUP):
                h = grp * HEAD_GROUP + j
                o_ref[:, h * A_DK:(h + 1) * A_DK] = o_state[j] + _head_rows(o_local, j)

    tile = pl.BlockSpec((CHUNK, A_W), lambda n: (n, 0))
    small = pl.BlockSpec((CHUNK, A_HEADS), lambda n: (n, 0))
    return pl.pallas_call(
        body, name="delta_fwd", grid=(n_chunks,), in_specs=[tile, tile, tile, small, small],
        out_specs=[tile, pl.BlockSpec((1, A_HEADS, A_DK, A_DK), lambda n: (n, 0, 0, 0)),
                   pl.BlockSpec((1, N_HEAD_GROUPS, GROUP_ROWS, GROUP_ROWS), lambda n: (n, 0, 0, 0))],
        out_shape=[jax.ShapeDtypeStruct((S, A_W), F32), jax.ShapeDtypeStruct((n_chunks, A_HEADS, A_DK, A_DK), F32),
                   jax.ShapeDtypeStruct((n_chunks, N_HEAD_GROUPS, GROUP_ROWS, GROUP_ROWS), F32)],
        scratch_shapes=[pltpu.VMEM((A_HEADS, A_DK, A_DK), F32)],
        compiler_params=_cparams(("arbitrary",)),
    )(q, k, v, beta, g)


def gate_a_fwd(o_pre, z, norm_w):
    S = o_pre.shape[0]

    def body(o_ref, z_ref, nw_ref, out_ref):
        nw = nw_ref[...]
        for h in range(A_HEADS):
            sl = slice(h * A_DK, (h + 1) * A_DK)
            oh = o_ref[:, sl]
            r = lax.rsqrt(jnp.mean(oh * oh, axis=-1, keepdims=True) + RMS_EPS)
            out_ref[:, sl] = (oh * r * nw * _silu(z_ref[:, sl].astype(F32))).astype(BF16)

    return rowcall(body, name="gate_a_fwd", S=S, ts=512, ins=[(o_pre, "row"), (z, "row"), (norm_w, "vec")],
                   outs=[((S, A_W), BF16, "row")])[0]


HEADS_PER_GROUP = 2
GROUP_W = HEADS_PER_GROUP * B_DH
N_GROUPS = B_HEADS // HEADS_PER_GROUP
PAD_ROWS = B_PREV * CHUNK


Q_TILE = 256
Q_CHUNKS = Q_TILE // CHUNK
KEY_WIN = (B_PREV + Q_CHUNKS) * CHUNK


def _band_probs(qh, kh, bias, valid):
    s = _dot1(qh, kh, "nt") * (B_DH ** -0.5) + bias
    s = jnp.where(valid, s, NEG_INF)
    e = jnp.exp(s - jnp.max(s, axis=-1, keepdims=True))
    return e * (1.0 / jnp.sum(e, axis=-1, keepdims=True))


def _attn_specs(S, tile_rows):
    n_cb = B_W // GROUP_W
    return [pl.BlockSpec((tile_rows, GROUP_W), lambda g, n: (n + PAD_ROWS // tile_rows, g)),
            pl.BlockSpec((PAD_ROWS + S, GROUP_W), lambda g, n: (0, n_cb + g)),
            pl.BlockSpec((PAD_ROWS + S, GROUP_W), lambda g, n: (0, 2 * n_cb + g)),
            pl.BlockSpec((HEADS_PER_GROUP, CHUNK, B_BAND), lambda g, n: (g, 0, 0))]


def _band_valid(first_chunk):
    return lax.broadcasted_iota(jnp.int32, (CHUNK, B_BAND), 1) >= PAD_ROWS - first_chunk * CHUNK


def _chunk_rows(x, qc, rows=CHUNK):
    return x[qc * CHUNK:qc * CHUNK + rows]


FWD_TILE = 512
FWD_CHUNKS = FWD_TILE // CHUNK
FWD_WIN = (B_PREV + FWD_CHUNKS) * CHUNK


def attn_fwd(qkv_pad, bias):
    S = qkv_pad.shape[0] - PAD_ROWS

    def body(q_ref, k_ref, v_ref, b_ref, o_ref):
        n = pl.program_id(1)
        start = pl.multiple_of(n * FWD_TILE, FWD_TILE)
        kwin = k_ref[pl.ds(start, FWD_WIN), :]
        vwin = v_ref[pl.ds(start, FWD_WIN), :]
        qv = q_ref[...]
        pairs = [(qc, hh) for qc in range(FWD_CHUNKS) for hh in range(HEADS_PER_GROUP)]
        sl = lambda hh: slice(hh * B_DH, (hh + 1) * B_DH)
        s = [_dot1(_chunk_rows(qv, qc)[:, sl(hh)], _chunk_rows(kwin, qc, B_BAND)[:, sl(hh)], "nt") for qc, hh in pairs]
        s = [jnp.where(_band_valid(n * FWD_CHUNKS + qc), x * (B_DH ** -0.5) + b_ref[hh], NEG_INF)
             for x, (qc, hh) in zip(s, pairs)]
        e = [jnp.exp(x - jnp.max(x, axis=-1, keepdims=True)) for x in s]
        p = [x * (1.0 / jnp.sum(x, axis=-1, keepdims=True)) for x in e]
        o = [_dot1(x, _chunk_rows(vwin, qc, B_BAND)[:, sl(hh)], "nn") for x, (qc, hh) in zip(p, pairs)]
        rows = [jnp.concatenate(o[qc * HEADS_PER_GROUP:(qc + 1) * HEADS_PER_GROUP], axis=1) for qc in range(FWD_CHUNKS)]
        o_ref[...] = jnp.concatenate(rows, axis=0).astype(BF16)

    return pl.pallas_call(
        body, name="attn_fwd", grid=(N_GROUPS, S // FWD_TILE), in_specs=_attn_specs(S, FWD_TILE),
        out_specs=pl.BlockSpec((FWD_TILE, GROUP_W), lambda g, n: (n, g)),
        out_shape=jax.ShapeDtypeStruct((S, B_W), BF16),
        compiler_params=_cparams(("parallel", "arbitrary")),
    )(qkv_pad, qkv_pad, qkv_pad, bias)


def _rel_onehot(i):
    kj = lax.broadcasted_iota(jnp.int32, (B_BAND, B_REL), 0)
    r = lax.broadcasted_iota(jnp.int32, (B_BAND, B_REL), 1)
    idx = jnp.clip(PAD_ROWS + i - kj, -(CHUNK - 1), B_MAX_REL) + (CHUNK - 1)
    return jnp.where(idx == r, 1.0, 0.0)


def bias_expand(rel_bias):
    def body(rb_ref, o_ref):
        i = pl.program_id(0)
        o_ref[0] = _dot3(rb_ref[...], _rel_onehot(i), "nt")

    return pl.pallas_call(
        body, name="bias_expand", grid=(CHUNK,),
        in_specs=[pl.BlockSpec((B_HEADS, B_REL), lambda i: (0, 0))],
        out_specs=pl.BlockSpec((1, B_HEADS, B_BAND), lambda i: (i, 0, 0)),
        out_shape=jax.ShapeDtypeStruct((CHUNK, B_HEADS, B_BAND), F32),
        compiler_params=_cparams(("parallel",)),
    )(rel_bias)


def bias_reduce(dbias):
    def body(d_ref, o_ref):
        i = pl.program_id(0)

        @pl.when(i == 0)
        def _():
            o_ref[...] = jnp.zeros_like(o_ref)

        o_ref[...] += _dot3(d_ref[0], _rel_onehot(i), "nn")

    return pl.pallas_call(
        body, name="bias_reduce", grid=(CHUNK,),
        in_specs=[pl.BlockSpec((1, B_HEADS, B_BAND), lambda i: (i, 0, 0))],
        out_specs=pl.BlockSpec((B_HEADS, B_REL), lambda i: (0, 0)),
        out_shape=jax.ShapeDtypeStruct((B_HEADS, B_REL), F32),
        compiler_params=_cparams(("arbitrary",)),
    )(dbias)


def merge_fwd(gates_raw, b_gate, ya, yb):
    S = ya.shape[0]

    def body(g_ref, b_ref, ya_ref, yb_ref, o_ref):
        gt = _sigmoid(g_ref[...].astype(F32) + b_ref[...])
        o_ref[...] = (gt[:, :D_MODEL] * ya_ref[...].astype(F32) + gt[:, D_MODEL:] * yb_ref[...].astype(F32)).astype(BF16)

    return rowcall(body, name="merge_fwd", S=S, ts=512,
                   ins=[(gates_raw, "row"), (b_gate, "vec"), (ya, "row"), (yb, "row")],
                   outs=[((S, D_MODEL), BF16, "row")])[0]


def _ln_stats(xpre):
    mu = jnp.mean(xpre, axis=-1, keepdims=True)
    xc = xpre - mu
    rstd = lax.rsqrt(jnp.mean(xc * xc, axis=-1, keepdims=True) + LN_EPS)
    return xc * rstd, rstd


def ln1_fwd(x, mix, mod, ln_g, ln_b):
    S = x.shape[0]

    def body(x_ref, mix_ref, m_ref, g_ref, b_ref, xpre_ref, x1_ref, h2_ref):
        m = m_ref[...]
        xpre = ALPHA * x_ref[...] + m[GATE_T:GATE_T + 1] * mix_ref[...]
        xhat, _ = _ln_stats(xpre)
        x1 = xhat * g_ref[...] + b_ref[...]
        xpre_ref[...] = xpre
        x1_ref[...] = x1
        h2_ref[...] = (x1 * (1.0 + m[SCALE_F:SCALE_F + 1]) + m[SHIFT_F:SHIFT_F + 1]).astype(BF16)

    return rowcall(body, name="ln1_fwd", S=S, ts=512,
                   ins=[(x, "row"), (mix, "row"), (mod, "vec"), (ln_g, "vec"), (ln_b, "vec")],
                   outs=[((S, D_MODEL), F32, "row"), ((S, D_MODEL), F32, "row"), ((S, D_MODEL), BF16, "row")])


STRIP_ROWS = 32
STRIP_COLS = 256


def ffn_act_fwd(up, conv_w, conv_b):
    S = up.shape[0]
    ts = 256

    def body(u_ref, up_ref, w_ref, b_ref, o_ref, ubuf):
        ubuf[0:8] = _halo_prev(up_ref) * (pl.program_id(0) > 0).astype(F32)
        ubuf[8:8 + ts] = u_ref[...].astype(F32)

        def col_block(j, carry):
            gate = pl.ds(pl.multiple_of(j * STRIP_COLS, STRIP_COLS), STRIP_COLS)
            halves = [gate, pl.ds(pl.multiple_of(D_FF + j * STRIP_COLS, STRIP_COLS), STRIP_COLS)]
            w = [w_ref[:, c] for c in halves]
            bias = [b_ref[:, c] for c in halves]
            for r0 in range(0, ts, STRIP_ROWS):
                uc = []
                for h in range(2):
                    x = ubuf[r0:r0 + STRIP_ROWS + 8, halves[h]]
                    uc.append(bias[h] + sum(
                        w[h][t:t + 1] * (x if t == FFN_CONV - 1 else pltpu.roll(x, FFN_CONV - 1 - t, axis=0))[8:]
                        for t in range(FFN_CONV)))
                o_ref[r0:r0 + STRIP_ROWS, gate] = (_silu(uc[0]) * uc[1]).astype(BF16)
            return carry

        lax.fori_loop(0, D_FF // STRIP_COLS, col_block, 0)

    return rowcall(body, name="ffn_act_fwd", S=S, ts=ts,
                   ins=[(up, "row"), (up, "prev"), (conv_w, "vec"), (conv_b, "vec")],
                   outs=[((S, D_FF), BF16, "row")], scratch=[pltpu.VMEM((ts + 8, 2 * D_FF), F32)])[0]


def final_fwd_bwd(x1, ffn, target, mod, ln_g, ln_b):
    S = x1.shape[0]

    def body(x1_ref, f_ref, t_ref, m_ref, g_ref, b_ref, dxpre_ref, dffn_ref, loss_ref, dgate_ref, dg_ref, db_ref):
        gate = m_ref[...][GATE_F:GATE_F + 1]
        ffn_v = f_ref[...]
        xpre = ALPHA * x1_ref[...] + gate * ffn_v
        xhat, rstd = _ln_stats(xpre)
        err = xhat * g_ref[...] + b_ref[...] - t_ref[...]
        loss_ref[...] += 0.5 * jnp.sum(jnp.mean(err * err, axis=-1, keepdims=True), axis=0, keepdims=True)
        dy = err * (1.0 / D_MODEL)
        dg_ref[...] += jnp.sum(dy * xhat, axis=0, keepdims=True)
        db_ref[...] += jnp.sum(dy, axis=0, keepdims=True)
        dyg = dy * g_ref[...]
        dxpre = rstd * (dyg - jnp.mean(dyg, axis=-1, keepdims=True) - xhat * jnp.mean(dyg * xhat, axis=-1, keepdims=True))
        dxpre_ref[...] = dxpre
        dffn_ref[...] = (gate * dxpre).astype(BF16)
        dgate_ref[...] += jnp.sum(dxpre * ffn_v, axis=0, keepdims=True)

    vec = ((1, D_MODEL), F32, "acc")
    return rowcall(body, name="final_fwd_bwd", S=S, ts=512,
                   ins=[(x1, "row"), (ffn, "row"), (target, "row"), (mod, "vec"), (ln_g, "vec"), (ln_b, "vec")],
                   outs=[((S, D_MODEL), F32, "row"), ((S, D_MODEL), BF16, "row"), ((1, 1), F32, "acc"), vec, vec, vec])


def ffn_act_bwd(dact, up, conv_w, conv_b):
    S = up.shape[0]
    ts = 256
    win_u, win_d = STRIP_ROWS + 16, STRIP_ROWS + 8

    def body(d_ref, dn_ref, u_ref, up_ref, un_ref, w_ref, b_ref, dup_ref, dw_ref, db_ref, ubuf, dbuf):
        i = pl.program_id(0)
        ubuf[0:8] = _halo_prev(up_ref) * (i > 0).astype(F32)
        ubuf[8:8 + ts] = u_ref[...].astype(F32)
        ubuf[8 + ts:16 + ts] = _halo_next(un_ref)
        dbuf[0:ts] = d_ref[...].astype(F32)
        dbuf[ts:ts + 8] = _halo_next(dn_ref) * (i < pl.num_programs(0) - 1).astype(F32)

        def col_block(j, carry):
            halves = [pl.ds(pl.multiple_of(j * STRIP_COLS, STRIP_COLS), STRIP_COLS),
                      pl.ds(pl.multiple_of(D_FF + j * STRIP_COLS, STRIP_COLS), STRIP_COLS)]
            w = [w_ref[:, c] for c in halves]
            bias = [b_ref[:, c] for c in halves]
            dw_acc = [[jnp.zeros((1, STRIP_COLS), F32) for _ in range(FFN_CONV)] for _ in halves]
            db_acc = [jnp.zeros((1, STRIP_COLS), F32) for _ in halves]
            for r0 in range(0, ts, STRIP_ROWS):
                shifted = [[x if k == 0 else pltpu.roll(x, k, axis=0) for k in range(FFN_CONV)]
                           for x in (ubuf[r0:r0 + win_u, c] for c in halves)]
                uc = [bias[h] + sum(w[h][t:t + 1] * shifted[h][FFN_CONV - 1 - t][8:8 + win_d] for t in range(FFN_CONV))
                      for h in range(2)]
                dact_w = dbuf[r0:r0 + win_d, halves[0]]
                sg, dsg = _silu_and_grad(uc[0])
                duc = [dact_w * uc[1] * dsg, dact_w * sg]
                for h in range(2):
                    dup = duc[h] * w[h][FFN_CONV - 1:FFN_CONV]
                    for t in range(FFN_CONV - 1):
                        dup = dup + pltpu.roll(duc[h], win_d - (FFN_CONV - 1 - t), axis=0) * w[h][t:t + 1]
                    dup_ref[r0:r0 + STRIP_ROWS, halves[h]] = dup[:STRIP_ROWS].astype(BF16)
                    mine = duc[h][:STRIP_ROWS]
                    db_acc[h] = db_acc[h] + jnp.sum(mine, axis=0, keepdims=True)
                    for t in range(FFN_CONV):
                        dw_acc[h][t] = dw_acc[h][t] + jnp.sum(
                            mine * shifted[h][FFN_CONV - 1 - t][8:8 + STRIP_ROWS], axis=0, keepdims=True)
            for h in range(2):
                dw_ref[:, halves[h]] += jnp.concatenate(dw_acc[h], axis=0)
                db_ref[:, halves[h]] += db_acc[h]
            return carry

        lax.fori_loop(0, D_FF // STRIP_COLS, col_block, 0)

    return rowcall(body, name="ffn_act_bwd", S=S, ts=ts,
                   ins=[(dact, "row"), (dact, "next"), (up, "row"), (up, "prev"), (up, "next"), (conv_w, "vec"), (conv_b, "vec")],
                   outs=[((S, 2 * D_FF), BF16, "row"), ((FFN_CONV, 2 * D_FF), F32, "acc"), ((1, 2 * D_FF), F32, "acc")],
                   scratch=[pltpu.VMEM((ts + 16, 2 * D_FF), F32), pltpu.VMEM((ts + 8, D_FF), F32)])


def ln1_bwd(dxpre2, dh2, xpre1, mix, mod, ln_g, ln_b):
    S = xpre1.shape[0]

    def body(d2_ref, dh_ref, xp_ref, mix_ref, m_ref, g_ref, b_ref, dxpre_ref, dmix_ref,
             dscale_ref, dshift_ref, dgate_ref, dg_ref, db_ref):
        m = m_ref[...]
        xhat, rstd = _ln_stats(xp_ref[...])
        x1 = xhat * g_ref[...] + b_ref[...]
        dh = dh_ref[...]
        dx1 = ALPHA * d2_ref[...] + dh * (1.0 + m[SCALE_F:SCALE_F + 1])
        dscale_ref[...] += jnp.sum(dh * x1, axis=0, keepdims=True)
        dshift_ref[...] += jnp.sum(dh, axis=0, keepdims=True)
        dg_ref[...] += jnp.sum(dx1 * xhat, axis=0, keepdims=True)
        db_ref[...] += jnp.sum(dx1, axis=0, keepdims=True)
        dyg = dx1 * g_ref[...]
        dxpre = rstd * (dyg - jnp.mean(dyg, axis=-1, keepdims=True) - xhat * jnp.mean(dyg * xhat, axis=-1, keepdims=True))
        dxpre_ref[...] = dxpre
        dmix_ref[...] = (m[GATE_T:GATE_T + 1] * dxpre).astype(BF16)
        dgate_ref[...] += jnp.sum(dxpre * mix_ref[...], axis=0, keepdims=True)

    vec = ((1, D_MODEL), F32, "acc")
    return rowcall(body, name="ln1_bwd", S=S, ts=512,
                   ins=[(dxpre2, "row"), (dh2, "row"), (xpre1, "row"), (mix, "row"), (mod, "vec"), (ln_g, "vec"), (ln_b, "vec")],
                   outs=[((S, D_MODEL), F32, "row"), ((S, D_MODEL), BF16, "row"), vec, vec, vec, vec, vec])


def merge_bwd(dmerged, gates_raw, b_gate, ya, yb):
    S = ya.shape[0]

    def body(d_ref, g_ref, b_ref, ya_ref, yb_ref, dya_ref, dyb_ref, dg_ref, dbg_ref):
        gt = _sigmoid(g_ref[...].astype(F32) + b_ref[...])
        d = d_ref[...].astype(F32)
        ga, gb = gt[:, :D_MODEL], gt[:, D_MODEL:]
        dya_ref[...] = (d * ga).astype(BF16)
        dyb_ref[...] = (d * gb).astype(BF16)
        dgr = jnp.concatenate([d * ya_ref[...].astype(F32) * ga * (1.0 - ga),
                               d * yb_ref[...].astype(F32) * gb * (1.0 - gb)], axis=1)
        dg_ref[...] = dgr.astype(BF16)
        dbg_ref[...] += jnp.sum(dgr, axis=0, keepdims=True)

    return rowcall(body, name="merge_bwd", S=S, ts=512,
                   ins=[(dmerged, "row"), (gates_raw, "row"), (b_gate, "vec"), (ya, "row"), (yb, "row")],
                   outs=[((S, D_MODEL), BF16, "row"), ((S, D_MODEL), BF16, "row"), ((S, 2 * D_MODEL), BF16, "row"),
                         ((1, 2 * D_MODEL), F32, "acc")])


def attn_bwd(qkv_pad, bias, do_b):
    S = qkv_pad.shape[0] - PAD_ROWS

    def body(q_ref, k_ref, v_ref, bias_ref, do_ref, dq_ref, dk_ref, dv_ref, db_ref, b_ref):
        n = pl.program_id(1)

        @pl.when(n == 0)
        def _():
            dk_ref[...] = jnp.zeros_like(dk_ref)
            dv_ref[...] = jnp.zeros_like(dv_ref)
            db_ref[...] = jnp.zeros_like(db_ref)
            b_ref[...] = jnp.full(b_ref.shape, NEG_INF, F32)
            for hh in range(HEADS_PER_GROUP):
                for qc in range(Q_CHUNKS):
                    b_ref[hh, qc * CHUNK:(qc + 1) * CHUNK, qc * CHUNK:qc * CHUNK + B_BAND] = bias_ref[hh]

        start = pl.multiple_of(n * Q_TILE, Q_TILE)
        kwin = k_ref[pl.ds(start, KEY_WIN), :]
        vwin = v_ref[pl.ds(start, KEY_WIN), :]
        qv, dov = q_ref[...], do_ref[...]
        valid = lax.broadcasted_iota(jnp.int32, (Q_TILE, KEY_WIN), 1) >= PAD_ROWS - n * Q_TILE
        dqs, dks, dvs = [], [], []
        for hh in range(HEADS_PER_GROUP):
            sl = slice(hh * B_DH, (hh + 1) * B_DH)
            p = _band_probs(qv[:, sl], kwin[:, sl], b_ref[hh], valid)
            dp = _dot1(dov[:, sl], vwin[:, sl], "nt")
            ds = p * (dp - jnp.sum(dp * p, axis=-1, keepdims=True))
            dbh = ds[0:CHUNK, 0:B_BAND]
            for qc in range(1, Q_CHUNKS):
                dbh = dbh + ds[qc * CHUNK:(qc + 1) * CHUNK, qc * CHUNK:qc * CHUNK + B_BAND]
            db_ref[hh] += dbh
            dsq = ds * (B_DH ** -0.5)
            dqs.append(_dot1(dsq, kwin[:, sl], "nn"))
            dks.append(_dot1(dsq, qv[:, sl], "tn"))
            dvs.append(_dot1(p, dov[:, sl], "tn"))
        dq_ref[...] = jnp.concatenate(dqs, axis=1).astype(BF16)
        dk_ref[pl.ds(start, KEY_WIN), :] += jnp.concatenate(dks, axis=1)
        dv_ref[pl.ds(start, KEY_WIN), :] += jnp.concatenate(dvs, axis=1)

    col = pl.BlockSpec((PAD_ROWS + S, GROUP_W), lambda g, n: (0, g))
    tile = pl.BlockSpec((Q_TILE, GROUP_W), lambda g, n: (n, g))
    return pl.pallas_call(
        body, name="attn_bwd", grid=(N_GROUPS, S // Q_TILE), in_specs=_attn_specs(S, Q_TILE) + [tile],
        out_specs=[tile, col, col, pl.BlockSpec((HEADS_PER_GROUP, CHUNK, B_BAND), lambda g, n: (g, 0, 0))],
        out_shape=[jax.ShapeDtypeStruct((S, B_W), BF16), jax.ShapeDtypeStruct((PAD_ROWS + S, B_W), F32),
                   jax.ShapeDtypeStruct((PAD_ROWS + S, B_W), F32), jax.ShapeDtypeStruct((B_HEADS, CHUNK, B_BAND), F32)],
        scratch_shapes=[pltpu.VMEM((HEADS_PER_GROUP, Q_TILE, KEY_WIN), F32)],
        compiler_params=_cparams(("parallel", "arbitrary")),
    )(qkv_pad, qkv_pad, qkv_pad, bias, do_b)


def gate_a_bwd(do_a, o_pre, z, norm_w):
    S = o_pre.shape[0]

    def body(d_ref, o_ref, z_ref, nw_ref, dop_ref, dz_ref, dnw_ref):
        nw = nw_ref[...]
        acc = jnp.zeros((1, A_DK), F32)
        for h in range(A_HEADS):
            sl = slice(h * A_DK, (h + 1) * A_DK)
            oh, zh, dh = o_ref[:, sl], z_ref[:, sl].astype(F32), d_ref[:, sl].astype(F32)
            r = lax.rsqrt(jnp.mean(oh * oh, axis=-1, keepdims=True) + RMS_EPS)
            sz, dsz = _silu_and_grad(zh)
            dz_ref[:, sl] = (dh * oh * r * nw * dsz).astype(BF16)
            acc = acc + jnp.sum(dh * oh * r * sz, axis=0, keepdims=True)
            t = dh * nw * sz
            dop_ref[:, sl] = r * t - oh * (r * r * r) * jnp.mean(t * oh, axis=-1, keepdims=True)
        dnw_ref[...] += acc

    return rowcall(body, name="gate_a_bwd", S=S, ts=512,
                   ins=[(do_a, "row"), (o_pre, "row"), (z, "row"), (norm_w, "vec")],
                   outs=[((S, A_W), F32, "row"), ((S, A_W), BF16, "row"), ((1, A_DK), F32, "acc")])


def delta_bwd(q, k, v, beta, g, sprev, tinv, do):
    S = q.shape[0]
    n_chunks = S // CHUNK

    def body(q_ref, k_ref, v_ref, beta_ref, g_ref, sprev_ref, t_ref, do_ref,
             dq_ref, dk_ref, dv_ref, dbeta_ref, dg_ref, dstate_ref):
        @pl.when(pl.program_id(0) == 0)
        def _():
            dstate_ref[...] = jnp.zeros_like(dstate_ref)

        mk = _tri_masks()
        causal, strict, eye = mk["causal"], mk["strict"], mk["eye"]
        blk_end = (lax.broadcasted_iota(jnp.int32, (GROUP_ROWS, 1), 0) & (CHUNK - 1)) == CHUNK - 1
        lane = lax.broadcasted_iota(jnp.int32, (CHUNK, A_HEADS), 1)
        betav, gv = beta_ref[...], g_ref[...]
        dbeta_t = jnp.zeros((CHUNK, A_HEADS), F32)
        dg_t = jnp.zeros((CHUNK, A_HEADS), F32)
        groups, heads = range(N_HEAD_GROUPS), range(HEAD_GROUP)
        st = [dict() for _ in groups]

        def local_part(grp, s):
            s["qs"], s["ks"], s["vs"] = _stack_heads(q_ref, grp), _stack_heads(k_ref, grp), _stack_heads(v_ref, grp)
            s["dos"] = _stack_heads(do_ref, grp)
            s["bs"] = _stack_cols(betav, grp)
            s["loc"] = loc = _delta_local(s["qs"], s["ks"], s["vs"], s["bs"], _stack_cols(gv, grp), mk)
            s["tinv"] = t_ref[0, grp]
            s["rhs"] = jnp.concatenate([loc["vb"], loc["y"]], axis=1)
            s["uw"] = _dot3(s["tinv"], s["rhs"], "nn")

        def state_part(grp, s):
            loc, uw, dos, qs = s["loc"], s["uw"], s["dos"], s["qs"]
            gam, kd, gl, gc = loc["gam"], loc["kd"], loc["gl"], loc["gc"]
            qg = qs * gam
            egl = jnp.exp(gl)
            hid = [grp * HEAD_GROUP + j for j in heads]
            s0 = [sprev_ref[0, h] for h in hid]
            ds1 = [dstate_ref[h] for h in hid]
            w = [_head_rows(uw, j)[:, A_DK:] for j in heads]
            vn = [_head_rows(uw, j)[:, :A_DK] - _dot1(w[j], s0[j], "nn") for j in heads]
            vns = jnp.concatenate(vn, axis=0)
            dvn_local = _dot1(loc["p"], dos, "tn")
            dvn = [_head_rows(dvn_local, j) + _dot1(_head_rows(kd, j), ds1[j], "nn") for j in heads]
            dvns = jnp.concatenate(dvn, axis=0)
            s["dp"] = jnp.where(causal, _dot1(dos, vns, "nt"), 0.0)
            dqg = jnp.concatenate([_dot1(_head_rows(dos, j), s0[j], "nt") for j in heads], axis=0)
            s["dq"] = dqg * gam
            dgc = jnp.sum(dqg * qg, axis=-1, keepdims=True)
            for j in heads:
                dstate_ref[hid[j]] = (_dot1(_head_rows(qg, j), _head_rows(dos, j), "tn")
                                      + egl[(j + 1) * CHUNK - 1:(j + 1) * CHUNK] * ds1[j] - _dot1(w[j], dvn[j], "tn"))
            dkd = jnp.concatenate([_dot1(vn[j], ds1[j], "nt") for j in heads], axis=0)
            s["dk"] = dkd * jnp.exp(gl - gc)
            t1 = jnp.sum(dkd * kd, axis=-1, keepdims=True)
            dgl = jnp.concatenate(
                [jnp.broadcast_to(jnp.sum(_head_rows(t1, j), axis=0, keepdims=True)
                                  + jnp.sum(jnp.sum(ds1[j] * s0[j], axis=-1, keepdims=True), axis=0, keepdims=True)
                                  * egl[(j + 1) * CHUNK - 1:(j + 1) * CHUNK], (CHUNK, 1)) for j in heads], axis=0)
            s["dgc"] = dgc - t1 + jnp.where(blk_end, dgl, 0.0)
            s["duw"] = jnp.concatenate(
                [dvns, jnp.concatenate([-_dot1(dvn[j], s0[j], "nt") for j in heads], axis=0)], axis=1)

        def solve_part(grp, s):
            s["dvby"] = _dot3(s["tinv"], s["duw"], "tn")
            s["dt"] = _dot3(s["duw"], s["rhs"], "nt")

        def inverse_part_a(grp, s):
            s["tdt"] = _dot3(s["tinv"], s["dt"], "tn")

        def inverse_part_b(grp, s):
            s["da"] = jnp.where(strict, -_dot3(s["tdt"], s["tinv"], "nt"), 0.0)

        def finish(grp, s):
            loc, qs, ks, vs, bs, da, dp, dvby = s["loc"], s["qs"], s["ks"], s["vs"], s["bs"], s["da"], s["dp"], s["dvby"]
            gam, decay = loc["gam"], loc["decay"]
            dm = da * decay
            dn = dp * decay
            e = da * loc["a"] + dp * loc["p"]
            dgc = s["dgc"] + jnp.sum(e, axis=1, keepdims=True) - _row_to_col(jnp.sum(e, axis=0, keepdims=True), eye)
            dy = dvby[:, A_DK:]
            dvb = dvby[:, :A_DK]
            dkb = _dot1(dm, ks, "nn") + dy * gam
            dk = s["dk"] + _dot1(dm, loc["kb"], "tn") + _dot1(dn, qs, "tn") + dkb * bs
            dq = s["dq"] + _dot1(dn, ks, "nn")
            dgc = dgc + jnp.sum(dy * loc["y"], axis=-1, keepdims=True)
            dbeta = jnp.sum(dkb * ks, axis=-1, keepdims=True) + jnp.sum(dvb * vs, axis=-1, keepdims=True)
            dv = dvb * bs
            dgs = jnp.sum(jnp.where(mk["upper"], _col_to_row(dgc, eye), 0.0), axis=1, keepdims=True)
            for j in heads:
                h = grp * HEAD_GROUP + j
                sl = slice(h * A_DK, (h + 1) * A_DK)
                dq_ref[:, sl] = _head_rows(dq, j)
                dk_ref[:, sl] = _head_rows(dk, j)
                dv_ref[:, sl] = _head_rows(dv, j)
            s["dbeta"], s["dgs"] = dbeta, dgs

        for stage in (local_part, state_part, solve_part, inverse_part_a, inverse_part_b, finish):
            for grp in groups:
                stage(grp, st[grp])
        for grp in groups:
            for j in heads:
                h = grp * HEAD_GROUP + j
                dbeta_t = dbeta_t + jnp.where(lane == h, _head_rows(st[grp]["dbeta"], j), 0.0)
                dg_t = dg_t + jnp.where(lane == h, _head_rows(st[grp]["dgs"], j), 0.0)
        dbeta_ref[...] = dbeta_t
        dg_ref[...] = dg_t

    rev = lambda n: (n_chunks - 1 - n, 0)
    rev4 = lambda n: (n_chunks - 1 - n, 0, 0, 0)
    tile = pl.BlockSpec((CHUNK, A_W), rev)
    small = pl.BlockSpec((CHUNK, A_HEADS), rev)
    return pl.pallas_call(
        body, name="delta_bwd", grid=(n_chunks,),
        in_specs=[tile, tile, tile, small, small, pl.BlockSpec((1, A_HEADS, A_DK, A_DK), rev4),
                  pl.BlockSpec((1, N_HEAD_GROUPS, GROUP_ROWS, GROUP_ROWS), rev4), tile],
        out_specs=[tile, tile, tile, small, small],
        out_shape=[jax.ShapeDtypeStruct((S, A_W), F32)] * 3 + [jax.ShapeDtypeStruct((S, A_HEADS), F32)] * 2,
        scratch_shapes=[pltpu.VMEM((A_HEADS, A_DK, A_DK), F32)],
        compiler_params=_cparams(("arbitrary",)),
    )(q, k, v, beta, g, sprev, tinv, do)


def _prep_a_dpre(raw, raw_prev, w, dq, dk, dv):
    y, dy_dpre = _prep_a_core(raw, raw_prev, w)
    parts = []
    for h in range(A_HEADS):
        yq = y[:, h * A_DK:(h + 1) * A_DK]
        dqh = dq[:, h * A_DK:(h + 1) * A_DK]
        rq = lax.rsqrt(jnp.sum(yq * yq, axis=-1, keepdims=True) + L2_EPS)
        parts.append((A_DK ** -0.5) * (rq * dqh - yq * (rq * rq * rq) * jnp.sum(dqh * yq, axis=-1, keepdims=True)))
    for h in range(A_HEADS):
        yk = y[:, A_W + h * A_DK:A_W + (h + 1) * A_DK]
        dkh = dk[:, h * A_DK:(h + 1) * A_DK]
        rk = lax.rsqrt(jnp.sum(yk * yk, axis=-1, keepdims=True) + L2_EPS)
        parts.append(rk * dkh - yk * (rk * rk * rk) * jnp.sum(dkh * yk, axis=-1, keepdims=True))
    parts.append(dv)
    return jnp.concatenate(parts, axis=1) * dy_dpre


def prep_a_bwd(qkv_raw, ba, conv_a, a_log, dt_bias, dq, dk, dv, dbeta, dg):
    S = qkv_raw.shape[0]
    ts = 256

    def body(x_ref, xp_ref, xn_ref, ba_ref, w_ref, al_ref, dt_ref, dq_ref, dqn_ref, dk_ref, dkn_ref, dv_ref, dvn_ref,
             dbeta_ref, dg_ref, draw_ref, dba_ref, dw_ref, dal_ref, ddt_ref):
        i = pl.program_id(0)
        first = (i > 0).astype(F32)
        last = (i < pl.num_programs(0) - 1).astype(F32)
        w = w_ref[...]
        cur, prev = x_ref[...].astype(F32), _halo_prev(xp_ref) * first
        dpre = _prep_a_dpre(cur, prev, w, dq_ref[...], dk_ref[...], dv_ref[...])
        dpre_n = _prep_a_dpre(_halo_next(xn_ref), cur[ts - 8:ts], w, _halo_next(dqn_ref), _halo_next(dkn_ref),
                              _halo_next(dvn_ref)) * last
        for j in range(A_CONV):
            dw_ref[j:j + 1, :] += jnp.sum(dpre * _shift_down(cur, prev, A_CONV - 1 - j), axis=0, keepdims=True)
        draw = dpre * w[A_CONV - 1:A_CONV]
        for j in range(A_CONV - 1):
            draw = draw + _shift_up(dpre, dpre_n, A_CONV - 1 - j) * w[j:j + 1]
        draw_ref[...] = draw.astype(BF16)
        bav = ba_ref[...]
        beta = _sigmoid(bav[:, 0:A_HEADS])
        xa = bav[:, A_HEADS:2 * A_HEADS] + dt_ref[...]
        nexp = -jnp.exp(al_ref[...])
        dgv = dg_ref[...]
        da = dgv * nexp * _sigmoid(xa)
        dba_ref[:, 0:A_HEADS] = dbeta_ref[...] * beta * (1.0 - beta)
        dba_ref[:, A_HEADS:2 * A_HEADS] = da
        dal_ref[...] += jnp.sum(dgv * nexp * _softplus(xa), axis=0, keepdims=True)
        ddt_ref[...] += jnp.sum(da, axis=0, keepdims=True)

    return rowcall(
        body, name="prep_a_bwd", S=S, ts=ts,
        ins=[(qkv_raw, "row"), (qkv_raw, "prev"), (qkv_raw, "next"), (ba, "row"), (conv_a, "vec"), (a_log, "vec"),
             (dt_bias, "vec"), (dq, "row"), (dq, "next"), (dk, "row"), (dk, "next"), (dv, "row"), (dv, "next"),
             (dbeta, "row"), (dg, "row")],
        outs=[((S, 3 * A_W), BF16, "row"), ((S, 2 * A_HEADS), F32, "row"), ((A_CONV, 3 * A_W), F32, "acc"),
              ((1, A_HEADS), F32, "acc"), ((1, A_HEADS), F32, "acc")])


def grad_x_final(dh1, x, dxpre1, mod):
    S = x.shape[0]

    def body(dh_ref, x_ref, dx_ref, m_ref, gx_ref, dscale_ref, dshift_ref):
        dh = dh_ref[...]
        gx_ref[...] = ALPHA * dx_ref[...] + dh * (1.0 + m_ref[...][SCALE_T:SCALE_T + 1])
        dscale_ref[...] += jnp.sum(dh * x_ref[...], axis=0, keepdims=True)
        dshift_ref[...] += jnp.sum(dh, axis=0, keepdims=True)

    vec = ((1, D_MODEL), F32, "acc")
    return rowcall(body, name="grad_x_final", S=S, ts=512, ins=[(dh1, "row"), (x, "row"), (dxpre1, "row"), (mod, "vec")],
                   outs=[((S, D_MODEL), F32, "row"), vec, vec])


_C_QKV, _C_Z, _C_BA, _C_QKVB, _C_G = 0, 3 * A_W, 4 * A_W, 4 * A_W + 2 * A_HEADS, 4 * A_W + 2 * A_HEADS + 3 * B_W
BA_PAD = 128


def split_w_in(w_in):
    ba = jnp.pad(w_in[:, _C_BA:_C_QKVB], ((0, 0), (0, BA_PAD - 2 * A_HEADS)))
    return dict(qkv=w_in[:, _C_QKV:_C_Z], z=w_in[:, _C_Z:_C_BA], ba=ba, qkvb=w_in[:, _C_QKVB:_C_G], g=w_in[:, _C_G:])


def join_w_in(p):
    return jnp.concatenate([p["qkv"], p["z"], p["ba"][:, :2 * A_HEADS], p["qkvb"], p["g"]], axis=1)


def forward_local(x, target, mod, w, sm, late_weights=None):
    h1 = modulate(x, mod, SHIFT_T, SCALE_T, "mod_t")
    qkv_raw = mm(h1, w["qkv"], mode="nn", out_dtype=BF16, name="proj_qkv")
    z = mm(h1, w["z"], mode="nn", out_dtype=BF16, name="proj_z")
    ba = mm(h1, w["ba"], mode="nn", out_dtype=F32, name="proj_ba")
    qkvb = mm(h1, w["qkvb"], mode="nn", out_dtype=BF16, name="proj_qkvb")
    gates_raw = mm(h1, w["g"], mode="nn", out_dtype=BF16, name="proj_g")
    q, k, v, beta, g = prep_a_fwd(qkv_raw, ba, sm["conv_a"], sm["a_log"], sm["dt_bias"])
    o_pre, sprev, tinv = delta_fwd(q, k, v, beta, g)
    o_a = gate_a_fwd(o_pre, z, sm["norm_a"])
    qkv_pad = jnp.pad(qkvb, ((PAD_ROWS, 0), (0, 0)))
    bias = jnp.transpose(bias_expand(sm["rel_bias"]), (1, 0, 2))
    o_b = attn_fwd(qkv_pad, bias)
    if late_weights is not None:
        w = dict(w, **late_weights(o_b))
    ya = mm(o_a, w["branch_a"], mode="nn", out_dtype=BF16, name="branch_a")
    yb = mm(o_b, w["branch_b"], mode="nn", out_dtype=BF16, name="branch_b")
    merged = merge_fwd(gates_raw, sm["b_gate"], ya, yb)
    mix = mm(merged, w["o"], mode="nn", out_dtype=F32, name="mix")
    xpre1, x1, h2 = ln1_fwd(x, mix, mod, sm["ln1_g"], sm["ln1_b"])
    up = mm(h2, w["up"], mode="nn", out_dtype=BF16, name="ffn_up", b_shards=True)
    act = ffn_act_fwd(up, sm["conv_ffn"], sm["b_conv_ffn"])
    ffn = mm(act, w["down"], mode="nn", out_dtype=F32, name="ffn_down")
    dxpre2, dffn, loss, dgate_f, dln2_g, dln2_b = final_fwd_bwd(x1, ffn, target, mod, sm["ln2_g"], sm["ln2_b"])
    saved = dict(h1=h1, qkv_raw=qkv_raw, z=z, ba=ba, gates_raw=gates_raw, q=q, k=k, v=v, beta=beta, g=g,
                 o_pre=o_pre, sprev=sprev, tinv=tinv, o_a=o_a, qkv_pad=qkv_pad, bias=bias, o_b=o_b, ya=ya, yb=yb,
                 merged=merged, mix=mix, xpre1=xpre1, x1=x1, h2=h2, up=up, act=act, ffn=ffn, w=w)
    return loss, dxpre2, dffn, dict(gate_f=dgate_f, ln2_g=dln2_g, ln2_b=dln2_b), saved


def backward_local(x, mod, sm, dxpre2, dffn, fin, sv, early_grads=None, early_w_in=None):
    w = sv["w"]
    dact = mm(dffn, w["down"], mode="nt", out_dtype=BF16, name="d_act")
    gw_down = mm(sv["act"], dffn, mode="tn", out_dtype=BF16, name="gw_down")
    dup, dconv_ffn, db_conv_ffn = ffn_act_bwd(dact, sv["up"], sm["conv_ffn"], sm["b_conv_ffn"])
    dh2 = mm(dup, w["up"], mode="nt", out_dtype=F32, name="d_h2", b_shards=True)
    gw_up = mm(sv["h2"], dup, mode="tn", out_dtype=BF16, name="gw_up", out_shards=N_CHIPS)
    dxpre1, dmix, dsc_f, dsh_f, dgate_t, dln1_g, dln1_b = ln1_bwd(
        dxpre2, dh2, sv["xpre1"], sv["mix"], mod, sm["ln1_g"], sm["ln1_b"])
    dmerged = mm(dmix, w["o"], mode="nt", out_dtype=BF16, name="d_merged")
    gw_o = mm(sv["merged"], dmix, mode="tn", out_dtype=BF16, name="gw_o")
    dya, dyb, dgates, db_gate = merge_bwd(dmerged, sv["gates_raw"], sm["b_gate"], sv["ya"], sv["yb"])
    do_a = mm(dya, w["branch_a"], mode="nt", out_dtype=BF16, name="d_oa")
    gw_branch_a = mm(sv["o_a"], dya, mode="tn", out_dtype=BF16, name="gw_branch_a")
    do_b = mm(dyb, w["branch_b"], mode="nt", out_dtype=BF16, name="d_ob")
    gw_branch_b = mm(sv["o_b"], dyb, mode="tn", out_dtype=BF16, name="gw_branch_b")
    bias = sv["bias"]
    if early_grads is not None:
        bias = bias + early_grads(dict(w_branch_a=gw_branch_a, w_branch_b=gw_branch_b, w_o=gw_o, w_up=gw_up,
                                       w_down=gw_down))[0, 0]
    dq_b, dk_pad, dv_pad, dbias = attn_bwd(sv["qkv_pad"], bias, do_b)
    dqkvb = jnp.concatenate([dq_b, dk_pad[PAD_ROWS:].astype(BF16), dv_pad[PAD_ROWS:].astype(BF16)], axis=1)
    drel_bias = bias_reduce(jnp.transpose(dbias, (1, 0, 2)))
    do_pre, dz, dnorm_a = gate_a_bwd(do_a, sv["o_pre"], sv["z"], sm["norm_a"])
    dq, dk, dv, dbeta, dg = delta_bwd(sv["q"], sv["k"], sv["v"], sv["beta"], sv["g"], sv["sprev"], sv["tinv"], do_pre)
    dqkv_raw, dba16, dconv_a, da_log, ddt_bias = prep_a_bwd(
        sv["qkv_raw"], sv["ba"], sm["conv_a"], sm["a_log"], sm["dt_bias"], dq, dk, dv, dbeta, dg)
    dba = jnp.pad(dba16, ((0, 0), (0, BA_PAD - 2 * A_HEADS))).astype(BF16)
    pieces = dict(qkv=dqkv_raw, z=dz, ba=dba, qkvb=dqkvb, g=dgates)
    gw_in = join_w_in({key: mm(sv["h1"], dpiece, mode="tn", out_dtype=BF16, name="gw_in_" + key)
                       for key, dpiece in pieces.items()})
    w_ba = w["ba"]
    if early_w_in is not None:
        w_ba = w_ba + early_w_in(gw_in)[0, 0].astype(BF16)
    dh1 = mm(pieces["ba"], w_ba, mode="nt", out_dtype=F32, name="d_h1_ba")
    for key in ("qkv", "z", "qkvb", "g"):
        dh1 = mm(pieces[key], w[key], mode="nt", out_dtype=F32, name="d_h1_" + key, acc_in=dh1)
    grad_x, dsc_t, dsh_t = grad_x_final(dh1, x, dxpre1, mod)
    dmod = jnp.concatenate([dsh_t, dsc_t, dgate_t, dsh_f, dsc_f, fin["gate_f"]], axis=0)
    gw = dict(w_in=gw_in, w_branch_a=gw_branch_a, w_branch_b=gw_branch_b, w_o=gw_o, w_up=gw_up, w_down=gw_down)
    gs = dict(b_gate=db_gate, conv_a=dconv_a, a_log=da_log, dt_bias=ddt_bias, norm_a=dnorm_a, rel_bias=drel_bias,
              ln1_g=dln1_g, ln1_b=dln1_b, conv_ffn=dconv_ffn, b_conv_ffn=db_conv_ffn, ln2_g=fin["ln2_g"], ln2_b=fin["ln2_b"])
    return grad_x, dmod, gw, gs


MESH = pl.DeviceIdType.MESH
ANY = pl.BlockSpec(memory_space=pl.ANY)
WHOLE_VMEM = pl.BlockSpec(memory_space=pltpu.VMEM)


def _place():
    return lax.axis_index("x"), lax.axis_index("y"), lax.axis_index("c")


def allgather8(blk, name):
    m_per, n = blk.shape

    def body(x_ref, out_ref, send_sems, recv_sems, local_sem):
        x, y, c = _place()
        me, sibling = (x, y, c), (x, y, 1 - c)
        chips = [(1 - x, y), (x, 1 - y), (1 - x, 1 - y)]

        def rows(px, py, pc):
            return out_ref.at[pl.ds((4 * px + 2 * py + pc) * m_per, m_per), :]

        def copy(k, block, to, src=None):
            return pltpu.make_async_remote_copy(
                src_ref=rows(*block) if src is None else src, dst_ref=rows(*block),
                send_sem=send_sems.at[k], recv_sem=recv_sems.at[k], device_id=to, device_id_type=MESH)

        mine = pltpu.make_async_copy(x_ref, rows(*me), local_sem)
        mine.start()
        first = [copy(0, me, sibling, src=x_ref)]
        first += [copy(1 + j, me, (*chip, c), src=x_ref) for j, chip in enumerate(chips)]
        for cp in first:
            cp.start()
        passed = [copy(4 + j, (*chip, c), sibling) for j, chip in enumerate(chips)]
        for j, chip in enumerate(chips):
            copy(1 + j, (*chip, c), me).wait_recv()
            passed[j].start()
        copy(0, sibling, me).wait_recv()
        for j, chip in enumerate(chips):
            copy(4 + j, (*chip, 1 - c), me).wait_recv()
        for cp in first + passed:
            cp.wait_send()
        mine.wait()

    return pl.pallas_call(
        body, name=name, out_shape=jax.ShapeDtypeStruct((N_DEV * m_per, n), blk.dtype),
        in_specs=[WHOLE_VMEM], out_specs=WHOLE_VMEM,
        scratch_shapes=[pltpu.SemaphoreType.DMA((7,)), pltpu.SemaphoreType.DMA((7,)), pltpu.SemaphoreType.DMA],
    )(blk)


def _chip_peers(x, y):
    return [(1 - x, y), (x, 1 - y), (1 - x, 1 - y)]


def chip_exchange(arrs, name, scatter):
    n = len(arrs)

    def body(*refs):
        ins, outs = refs[:n], refs[n:2 * n]
        send_sems, recv_sems, local_sems = refs[2 * n:]
        x, y, c = _place()
        me = 2 * x + y
        sibling = (x, y, 1 - c)
        peers = _chip_peers(x, y)

        def half(ref, which):
            r2 = ref.shape[0] // 2
            return ref.at[pl.ds(which * r2, r2), :]

        def outgoing(a, chip):
            return ins[a].at[chip] if scatter else ins[a]

        def copy(k, src, dst, to):
            return pltpu.make_async_remote_copy(src_ref=src, dst_ref=dst, send_sem=send_sems.at[k],
                                                recv_sem=recv_sems.at[k], device_id=to, device_id_type=MESH)

        started, local = [], []
        for a in range(n):
            lc = pltpu.make_async_copy(outgoing(a, me), outs[a].at[me], local_sems.at[a])
            lc.start()
            local.append(lc)
            for j, (px, py) in enumerate(peers):
                cp = copy(6 * a + j, half(outgoing(a, 2 * px + py), c), half(outs[a].at[me], c), (px, py, c))
                cp.start()
                started.append(cp)
        for a in range(n):
            for j, (px, py) in enumerate(peers):
                landed = half(outs[a].at[2 * px + py], c)
                copy(6 * a + j, landed, landed, (px, py, c)).wait_recv()
                relay = copy(6 * a + 3 + j, landed, landed, sibling)
                relay.start()
                started.append(relay)
        for a in range(n):
            for j, (px, py) in enumerate(peers):
                other = half(outs[a].at[2 * px + py], 1 - c)
                copy(6 * a + 3 + j, other, other, sibling).wait_recv()
        for cp in started:
            cp.wait_send()
        for lc in local:
            lc.wait()

    out_shape = [jax.ShapeDtypeStruct(a.shape if scatter else (N_CHIPS,) + a.shape, a.dtype) for a in arrs]
    return pl.pallas_call(
        body, name=name, out_shape=out_shape, in_specs=[ANY] * n, out_specs=[ANY] * n,
        scratch_shapes=[pltpu.SemaphoreType.DMA((6 * n,)), pltpu.SemaphoreType.DMA((6 * n,)), pltpu.SemaphoreType.DMA((n,))],
    )(*arrs)


HBM_SPEC = pl.BlockSpec(memory_space=pltpu.HBM)
SEM_SPEC = pl.BlockSpec(memory_space=pltpu.SEMAPHORE)
SIDE_EFFECT = pltpu.SideEffectType.DATAFLOW_SIDE_EFFECTING


def _in_hbm(a):
    return pltpu.with_memory_space_constraint(a, pltpu.HBM)


def exchange_start(arrs, name, scatter, after):
    n = len(arrs)
    lands = [lax.empty(a.shape if scatter else (N_CHIPS,) + a.shape, a.dtype) for a in arrs]

    def body(*refs):
        ins, zones = refs[:n], refs[n:2 * n]
        send_sems, recv_sems, token = refs[2 * n + 1], refs[2 * n + 2], refs[-1]
        x, y, c = _place()
        me = 2 * x + y
        for a in range(n):
            for j, (px, py) in enumerate(_chip_peers(x, y)):
                pltpu.make_async_remote_copy(
                    src_ref=ins[a].at[2 * px + py] if scatter else ins[a], dst_ref=zones[a].at[me],
                    send_sem=send_sems.at[3 * a + j], recv_sem=recv_sems.at[3 * a + j],
                    device_id=(px, py, c), device_id_type=MESH).start()
        token[...] = jnp.zeros_like(token)

    res = pl.pallas_call(
        body, name=name,
        out_shape=[pltpu.SemaphoreType.DMA((3 * n,)), pltpu.SemaphoreType.DMA((3 * n,))]
        + [pltpu.HBM(a.shape, a.dtype) for a in arrs] + [pltpu.HBM(z.shape, z.dtype) for z in lands]
        + [jax.ShapeDtypeStruct((8, 128), F32)],
        in_specs=[HBM_SPEC] * (2 * n) + [ANY], out_specs=[SEM_SPEC, SEM_SPEC] + [HBM_SPEC] * (2 * n) + [WHOLE_VMEM],
        input_output_aliases={i: 2 + i for i in range(2 * n)},
        compiler_params=pltpu.CompilerParams(has_side_effects=SIDE_EFFECT),
    )(*[_in_hbm(a) for a in arrs], *[_in_hbm(z) for z in lands], after)
    return dict(send=res[0], recv=res[1], src=res[2:2 + n], zones=res[2 + n:2 + 2 * n], token=res[-1], scatter=scatter)


def exchange_wait(handle, name, after):
    srcs, zones, scatter = handle["src"], handle["zones"], handle["scatter"]
    n = len(srcs)

    def body(*refs):
        ins, lands = refs[:n], refs[n:2 * n]
        send_sems, recv_sems = refs[2 * n], refs[2 * n + 1]
        x, y, c = _place()
        me = 2 * x + y
        for a in range(n):
            for j, (px, py) in enumerate(_chip_peers(x, y)):
                cp = pltpu.make_async_remote_copy(
                    src_ref=ins[a].at[me] if scatter else ins[a], dst_ref=lands[a].at[2 * px + py],
                    send_sem=send_sems.at[3 * a + j], recv_sem=recv_sems.at[3 * a + j],
                    device_id=(px, py, c), device_id_type=MESH)
                cp.wait_send()
                cp.wait_recv()

    res = pl.pallas_call(
        body, name=name, out_shape=[pltpu.HBM(a.shape, a.dtype) for a in list(srcs) + list(zones)],
        in_specs=[HBM_SPEC] * (2 * n) + [SEM_SPEC, SEM_SPEC, ANY], out_specs=[HBM_SPEC] * (2 * n),
        input_output_aliases={i: i for i in range(2 * n)},
        compiler_params=pltpu.CompilerParams(has_side_effects=SIDE_EFFECT),
    )(*srcs, *zones, handle["send"], handle["recv"], after)
    return res[n:]


def sibling_exchange(arrs, name):
    n = len(arrs)

    def body(*refs):
        ins, outs = refs[:n], refs[n:2 * n]
        send_sems, recv_sems = refs[2 * n:]
        x, y, c = _place()
        cps = [pltpu.make_async_remote_copy(src_ref=ins[a], dst_ref=outs[a], send_sem=send_sems.at[a],
                                            recv_sem=recv_sems.at[a], device_id=(x, y, 1 - c), device_id_type=MESH)
               for a in range(n)]
        for cp in cps:
            cp.start()
        for cp in cps:
            cp.wait()

    return pl.pallas_call(
        body, name=name, out_shape=[jax.ShapeDtypeStruct(a.shape, a.dtype) for a in arrs],
        in_specs=[ANY] * n, out_specs=[ANY] * n,
        scratch_shapes=[pltpu.SemaphoreType.DMA((n,)), pltpu.SemaphoreType.DMA((n,))],
    )(*arrs)


TILE_BYTES = 2 * 1024 * 1024


def _row_tile(rows, row_bytes):
    if rows * row_bytes <= TILE_BYTES or rows % 8:
        return rows
    best = 8
    for t in range(8, rows + 1, 8):
        if rows % t == 0 and t * row_bytes <= TILE_BYTES:
            best = t
    return best


def pair_add(a, b, name):
    shape = a.shape
    a, b = a.reshape(-1, shape[-1]), b.reshape(-1, shape[-1])
    R, C = a.shape
    tr = _row_tile(R, C * 4)

    def body(a_ref, b_ref, o_ref):
        o_ref[...] = (a_ref[...].astype(F32) + b_ref[...].astype(F32)).astype(BF16)

    spec = pl.BlockSpec((tr, C), lambda i: (i, 0))
    return pl.pallas_call(body, name=name, grid=(R // tr,), in_specs=[spec, spec], out_specs=spec,
                          out_shape=jax.ShapeDtypeStruct((R, C), BF16), compiler_params=_cparams(("parallel",)))(a, b).reshape(shape)


def sum_lead(parts, name):
    K, R, C = parts.shape
    tr = _row_tile(R, C * 4)

    def body(p_ref, o_ref):
        acc = p_ref[0].astype(F32)
        for j in range(1, K):
            acc = acc + p_ref[j].astype(F32)
        o_ref[...] = acc

    return pl.pallas_call(
        body, name=name, grid=(R // tr,), in_specs=[pl.BlockSpec((K, tr, C), lambda i: (0, i, 0))],
        out_specs=pl.BlockSpec((tr, C), lambda i: (i, 0)), out_shape=jax.ShapeDtypeStruct((R, C), F32),
        compiler_params=_cparams(("parallel",)))(parts)


def adamw(w, g, m, v, name):
    R, C = w.shape
    tr = _row_tile(R, C * 4)

    def body(w_ref, g_ref, m_ref, v_ref, d_ref, mo_ref, vo_ref):
        gv = g_ref[...]
        m2 = ADAM_B1 * m_ref[...] + (1.0 - ADAM_B1) * gv
        v2 = ADAM_B2 * v_ref[...] + (1.0 - ADAM_B2) * (gv * gv)
        m_hat = m2 / (1.0 - ADAM_B1 ** ADAM_STEP)
        v_hat = v2 / (1.0 - ADAM_B2 ** ADAM_STEP)
        d_ref[...] = -ADAM_LR * (m_hat / (jnp.sqrt(v_hat) + ADAM_EPS) + ADAM_WD * w_ref[...])
        mo_ref[...] = m2
        vo_ref[...] = v2

    spec = pl.BlockSpec((tr, C), lambda i: (i, 0))
    return pl.pallas_call(body, name=name, grid=(R // tr,), in_specs=[spec] * 4, out_specs=[spec] * 3,
                          out_shape=[jax.ShapeDtypeStruct((R, C), F32)] * 3, compiler_params=_cparams(("parallel",)))(w, g, m, v)


LANES = 1024


def _pack(arrs, rows):
    out, offs, r = [], [], 0
    for a in arrs:
        flat = a.reshape(-1)
        nr = -(-flat.shape[0] // LANES)
        out.append(jnp.pad(flat, (0, nr * LANES - flat.shape[0])))
        offs.append(r)
        r += nr
    assert r <= rows, (r, rows)
    out.append(jnp.zeros(((rows - r) * LANES,), F32))
    return jnp.concatenate(out).reshape(rows, LANES), offs


def _unpack(packed, offs, shapes):
    flat = packed.reshape(-1)
    return [flat[o * LANES:o * LANES + math.prod(s)].reshape(s) for o, s in zip(offs, shapes)]


WEIGHTS = ["w_ada", "b_ada", "w_in", "b_gate", "conv_a", "a_log", "dt_bias", "norm_a", "rel_bias", "w_branch_a",
           "w_branch_b", "w_o", "ln1_g", "ln1_b", "w_up", "conv_ffn", "b_conv_ffn", "w_down", "ln2_g", "ln2_b"]
BIG = ["w_in", "w_branch_a", "w_branch_b", "w_o", "w_up", "w_down"]
LATE = [n for n in BIG if n != "w_in"]
KEPT_SHARDED = {"w_up"}
COL_SHARDED = {"w_in", "w_up"}
SMALL_SHARDED = {"conv_a": 3 * A_W // N_CHIPS, "rel_bias": B_REL // N_CHIPS, "conv_ffn": 2 * D_FF // N_CHIPS}
SMALL = [n for n in WEIGHTS if n not in BIG and n != "w_ada"]


def _to_full(g4, name):
    if name in KEPT_SHARDED:
        return g4
    if name in COL_SHARDED:
        return jnp.transpose(g4, (1, 0, 2)).reshape(g4.shape[1], -1)
    return g4.reshape(-1, g4.shape[2])


def _to_shards(full, name):
    if name in KEPT_SHARDED:
        return full
    if name in COL_SHARDED:
        return jnp.transpose(full.reshape(full.shape[0], N_CHIPS, -1), (1, 0, 2))
    return full.reshape(N_CHIPS, -1, full.shape[1])


def kernel(x, c, w_ada, b_ada, w_in, b_gate, conv_a, a_log, dt_bias, norm_a, rel_bias, w_branch_a, w_branch_b, w_o, ln1_g, ln1_b, w_up, conv_ffn, b_conv_ffn, w_down, ln2_g, ln2_b, loss_target, m_w_ada, m_b_ada, m_w_in, m_b_gate, m_conv_a, m_a_log, m_dt_bias, m_norm_a, m_rel_bias, m_w_branch_a, m_w_branch_b, m_w_o, m_ln1_g, m_ln1_b, m_w_up, m_conv_ffn, m_b_conv_ffn, m_w_down, m_ln2_g, m_ln2_b, v_w_ada, v_b_ada, v_w_in, v_b_gate, v_conv_a, v_a_log, v_dt_bias, v_norm_a, v_rel_bias, v_w_branch_a, v_w_branch_b, v_w_o, v_ln1_g, v_ln1_b, v_w_up, v_conv_ffn, v_b_conv_ffn, v_w_down, v_ln2_g, v_ln2_b):
    args = dict(locals())
    wts = {n: args[n] for n in WEIGHTS}
    moms = {n: args["m_" + n] for n in WEIGHTS}
    vars_ = {n: args["v_" + n] for n in WEIGHTS}
    xi, yi, ci = _place()
    chip = 2 * xi + yi
    dev = 4 * xi + 2 * yi + ci
    ada_cols = w_ada.shape[2]

    c_all = allgather8(jnp.pad(c, ((0, 7), (0, 0))), "gather_c").reshape(N_DEV, 8, D_MODEL)[:, 0]
    b_ada_sh = lax.dynamic_slice(b_ada, (0, chip * ada_cols), (1, ada_cols))
    mod_sh = ada_fwd(c_all, w_ada[0], b_ada_sh)
    mod_g = allgather8(mod_sh, "gather_mod").reshape(N_CHIPS, 2, N_DEV, ada_cols)[:, 0]
    mod = lax.dynamic_slice(mod_g, (0, dev, 0), (N_CHIPS, 1, ada_cols)).reshape(6, D_MODEL)

    (w_in_g4,) = chip_exchange([wts["w_in"][0].astype(BF16)], "gather_w_in", scatter=False)
    wd = split_w_in(_to_full(w_in_g4, "w_in"))
    late_shards = [wts[n][0].astype(BF16) for n in LATE]
    late_gather = exchange_start(late_shards, "gather_late_start", scatter=False, after=w_in_g4)
    mod = mod + late_gather["token"][0, 0]

    def late_weights(after):
        zones = exchange_wait(late_gather, "gather_late_wait", after)
        full = [_to_full(lax.dynamic_update_slice(z, s[None], (chip, 0, 0)), n) for n, z, s in zip(LATE, zones, late_shards)]
        return {n[2:]: f for n, f in zip(LATE, full)}

    sshapes = [wts[n].shape[1:] for n in SMALL_SHARDED]
    spack, soffs = _pack([wts[n][0] for n in SMALL_SHARDED], 16)
    sg = allgather8(spack, "gather_small_w").reshape(N_CHIPS, 2, 16, LANES)[:, 0]
    sparts = [_unpack(sg[j], soffs, sshapes) for j in range(N_CHIPS)]
    sm = {n: wts[n] for n in SMALL if n not in SMALL_SHARDED and n != "b_ada"}
    for i, n in enumerate(SMALL_SHARDED):
        sm[n] = jnp.concatenate([sparts[j][i] for j in range(N_CHIPS)], axis=-1)

    early = {}

    def early_grads(g):
        mine = [g[n] for n in LATE]
        theirs = sibling_exchange(mine, "grad_sibling_late")
        early["sums"] = [_to_shards(pair_add(a, b, "grad_pair_" + n), n) for n, a, b in zip(LATE, mine, theirs)]
        early["scatter"] = exchange_start(early["sums"], "grad_scatter_start", scatter=True, after=theirs[0])
        return early["scatter"]["token"]

    def early_w_in(g):
        (theirs,) = sibling_exchange([g], "grad_sibling_w_in")
        early["sum_in"] = _to_shards(pair_add(g, theirs, "grad_pair_w_in"), "w_in")
        early["scatter_in"] = exchange_start([early["sum_in"]], "grad_scatter_w_in_start", scatter=True, after=theirs)
        return early["scatter_in"]["token"]

    loss, dxpre2, dffn, fin, sv = forward_local(x[0], loss_target[0], mod, wd, sm, late_weights)
    grad_x, dmod, gw, gs = backward_local(x[0], mod, sm, dxpre2, dffn, fin, sv, early_grads, early_w_in)

    gnames = [n for n in SMALL if n != "b_ada"]
    vec, voffs = _pack([dmod] + [gs[n] for n in gnames] + [loss], 56)
    gathered = allgather8(vec, "gather_small_g").reshape(N_DEV, 56, LANES)
    summed = sum_lead(gathered, "sum_small_g")
    full_shapes = [(6, D_MODEL)] + [gs[n].shape for n in gnames] + [(1, 1)]
    parts = _unpack(summed, voffs, full_shapes)
    grads = {"b_ada": parts[0].reshape(1, -1)}
    for n, p in zip(gnames, parts[1:-1]):
        if n in SMALL_SHARDED:
            p = lax.dynamic_slice_in_dim(p, chip * SMALL_SHARDED[n], SMALL_SHARDED[n], axis=1)
        grads[n] = p.reshape(wts[n].shape)
    loss_total = parts[-1].reshape(())
    dmod_all = gathered[:, 0:6, :].reshape(N_DEV, 6 * D_MODEL)
    grads["w_ada"] = ada_bwd(c_all, lax.dynamic_slice(dmod_all, (0, chip * ada_cols), (N_DEV, ada_cols)))[None]

    def own_slot(zone, sums):
        return lax.dynamic_update_slice(zone, lax.dynamic_slice_in_dim(sums, chip, 1, axis=0), (chip, 0, 0))

    zones = exchange_wait(early["scatter"], "grad_scatter_wait", summed)
    for n, z, s in zip(LATE, zones, early["sums"]):
        grads[n] = sum_lead(own_slot(z, s), "grad_sum_" + n)[None]

    delta, new_m, new_v = {}, {}, {}

    def update(n):
        d, m2, v2 = adamw(wts[n][0], grads[n][0], moms[n][0], vars_[n][0], "adamw_" + n)
        delta[n], new_m[n], new_v[n] = d[None], m2[None], v2[None]

    for n in ["w_ada"] + LATE:
        update(n)
    shapes = [wts[n].shape for n in SMALL]
    packs = [_pack([t[n] for n in SMALL], 32) for t in (wts, grads, moms, vars_)]
    outs = adamw(*[p[0] for p in packs], "adamw_small")
    for res, o in zip((delta, new_m, new_v), outs):
        for n, a in zip(SMALL, _unpack(o, packs[0][1], shapes)):
            res[n] = a
    (zone_in,) = exchange_wait(early["scatter_in"], "grad_scatter_w_in_wait", outs[0])
    grads["w_in"] = sum_lead(own_slot(zone_in, early["sum_in"]), "grad_sum_w_in")[None]
    update("w_in")
    return (loss_total, grad_x[None], *[grads[n] for n in WEIGHTS], *[delta[n] for n in WEIGHTS],
            *[new_m[n] for n in WEIGHTS], *[new_v[n] for n in WEIGHTS])
```

```python
import functools
import math

import jax
import jax.numpy as jnp
from jax import lax
from jax.experimental import pallas as pl
from jax.experimental.pallas import tpu as pltpu

F32 = jnp.float32
BF16 = jnp.bfloat16

D_MODEL = 1024
CHUNK = 64
A_HEADS = 8
A_DK = 128
A_W = A_HEADS * A_DK
A_CONV = 4
B_HEADS = 16
B_DH = 64
B_W = B_HEADS * B_DH
B_PREV = 8
B_BAND = (B_PREV + 1) * CHUNK
B_MAX_REL = 256
B_REL = CHUNK - 1 + B_MAX_REL + 1
D_FF = 2816
FFN_CONV = 3
IN_COLS = 4 * A_W + 2 * A_HEADS + 3 * B_W + 2 * D_MODEL
ALPHA = 2.0 ** 0.25
LN_EPS = 1e-5
RMS_EPS = 1e-6
L2_EPS = 1e-6
NEG_INF = -1e30
ADAM_LR, ADAM_B1, ADAM_B2, ADAM_EPS, ADAM_WD, ADAM_STEP = 0.001, 0.9, 0.999, 1e-08, 0.01, 10
N_CHIPS = 4
N_DEV = 8
VMEM_LIMIT = 56 * 1024 * 1024


def _cparams(sem=None):
    return pltpu.CompilerParams(dimension_semantics=sem, vmem_limit_bytes=VMEM_LIMIT)


_DIMS = {"nn": (((1,), (0,)), ((), ())), "nt": (((1,), (1,)), ((), ())), "tn": (((0,), (0,)), ((), ()))}


MM_TILE_CAP = 1408


def _mm_tile(n):
    return max(t for t in range(128, min(n, MM_TILE_CAP) + 1, 128) if n % t == 0)


def mm(a, b, *, mode, out_dtype, name, acc_in=None, b_shards=False, out_shards=0):
    b_rows, b_cols = (b.shape[1], b.shape[0] * b.shape[2]) if b_shards else b.shape
    if mode == "nn":
        (M, K), (K2, N) = a.shape, (b_rows, b_cols)
    elif mode == "nt":
        (M, K), (N, K2) = a.shape, (b_rows, b_cols)
    else:
        (K, M), (K2, N) = a.shape, (b_rows, b_cols)
    assert K == K2, (a.shape, b.shape, mode)
    tm, tn, tk = _mm_tile(M), _mm_tile(N), _mm_tile(K)
    nk = K // tk

    def body(*refs):
        if acc_in is None:
            a_ref, b_ref, o_ref, acc_ref = refs
        else:
            a_ref, b_ref, c_ref, o_ref, acc_ref = refs
        k = pl.program_id(2)

        @pl.when(k == 0)
        def _():
            if acc_in is None:
                acc_ref[...] = jnp.zeros_like(acc_ref)
            else:
                acc_ref[...] = c_ref[...]

        acc_ref[...] += lax.dot_general(a_ref[...].astype(BF16), b_ref[...].astype(BF16), _DIMS[mode],
                                        preferred_element_type=F32)

        @pl.when(k == nk - 1)
        def _():
            o_ref[...] = acc_ref[...].astype(out_dtype)

    a_spec = pl.BlockSpec((tk, tm), lambda i, j, k: (k, i)) if mode == "tn" else pl.BlockSpec((tm, tk), lambda i, j, k: (i, k))
    if b_shards:
        assert (tk if mode == "nt" else tn) == b.shape[2] and mode != "tn", (b.shape, tn, tk, mode)
        b_spec = (pl.BlockSpec((None, tn, tk), lambda i, j, k: (k, j, 0)) if mode == "nt"
                  else pl.BlockSpec((None, tk, tn), lambda i, j, k: (j, k, 0)))
    else:
        b_spec = pl.BlockSpec((tn, tk), lambda i, j, k: (j, k)) if mode == "nt" else pl.BlockSpec((tk, tn), lambda i, j, k: (k, j))
    o_spec = pl.BlockSpec((tm, tn), lambda i, j, k: (i, j))
    out_shape = jax.ShapeDtypeStruct((M, N), out_dtype)
    if out_shards:
        assert N == out_shards * tn and acc_in is None, (N, tn, out_shards)
        o_spec = pl.BlockSpec((None, tm, tn), lambda i, j, k: (j, i, 0))
        out_shape = jax.ShapeDtypeStruct((out_shards, M, tn), out_dtype)
    ins, in_specs, aliases = [a, b], [a_spec, b_spec], {}
    if acc_in is not None:
        assert acc_in.shape == (M, N) and acc_in.dtype == F32 and out_dtype == F32
        ins.append(acc_in)
        in_specs.append(o_spec)
        aliases = {2: 0}
    return pl.pallas_call(
        body, name=name, grid=(M // tm, N // tn, nk), in_specs=in_specs, out_specs=o_spec,
        out_shape=out_shape, scratch_shapes=[pltpu.VMEM((tm, tn), F32)],
        input_output_aliases=aliases, compiler_params=_cparams(("parallel", "parallel", "arbitrary")),
    )(*ins)


def rowcall(body, *, name, S, ts, ins, outs, scratch=()):
    assert S % ts == 0 and ts % 16 == 0
    nsteps = S // ts
    in_specs, arrays = [], []
    for arr, kind in ins:
        arrays.append(arr)
        if kind == "row":
            in_specs.append(pl.BlockSpec((ts, arr.shape[1]), lambda i: (i, 0)))
        elif kind in ("prev", "next"):
            hr = 8 * (4 // arr.dtype.itemsize)
            per, last = ts // hr, S // hr - 1
            if kind == "prev":
                in_specs.append(pl.BlockSpec((hr, arr.shape[1]), lambda i, per=per: (jnp.maximum(i * per - 1, 0), 0)))
            else:
                in_specs.append(pl.BlockSpec((hr, arr.shape[1]), lambda i, per=per, last=last: (jnp.minimum((i + 1) * per, last), 0)))
        else:
            nd = arr.ndim
            in_specs.append(pl.BlockSpec(arr.shape, lambda i, nd=nd: (0,) * nd))
    out_specs, out_shapes, acc_idx = [], [], []
    for n, (shape, dtype, kind) in enumerate(outs):
        out_shapes.append(jax.ShapeDtypeStruct(shape, dtype))
        if kind == "row":
            out_specs.append(pl.BlockSpec((ts, shape[1]), lambda i: (i, 0)))
        else:
            nd = len(shape)
            out_specs.append(pl.BlockSpec(shape, lambda i, nd=nd: (0,) * nd))
            acc_idx.append(n)
    n_in = len(arrays)

    def wrapped(*refs):
        @pl.when(pl.program_id(0) == 0)
        def _():
            for n in acc_idx:
                refs[n_in + n][...] = jnp.zeros_like(refs[n_in + n])

        body(*refs)

    res = pl.pallas_call(
        wrapped, name=name, grid=(nsteps,), in_specs=in_specs, out_specs=out_specs, out_shape=out_shapes,
        scratch_shapes=list(scratch), compiler_params=_cparams(("arbitrary",) if acc_idx else ("parallel",)),
    )(*arrays)
    return res


def _halo_prev(ref):
    v = ref[...].astype(F32)
    return v[v.shape[0] - 8:]


def _halo_next(ref):
    return ref[...].astype(F32)[:8]


def _shift_down(cur, prev8, k):
    if k == 0:
        return cur
    rolled = pltpu.roll(cur, k, axis=0)
    fix = pltpu.roll(prev8, k, axis=0)
    row = lax.broadcasted_iota(jnp.int32, (8, 1), 0)
    top = jnp.where(row < k, fix, rolled[0:8])
    if cur.shape[0] == 8:
        return top
    return jnp.concatenate([top, rolled[8:]], axis=0)


def _shift_up(cur, next8, k):
    if k == 0:
        return cur
    n = cur.shape[0]
    rolled = pltpu.roll(cur, n - k, axis=0)
    fix = pltpu.roll(next8, 8 - k, axis=0)
    row = lax.broadcasted_iota(jnp.int32, (8, 1), 0)
    bot = jnp.where(row >= 8 - k, fix, rolled[n - 8:n])
    return jnp.concatenate([rolled[:n - 8], bot], axis=0)


def _sigmoid(x):
    return 1.0 / (1.0 + jnp.exp(-x))


def _silu(x):
    return x * _sigmoid(x)


def _silu_and_grad(x):
    s = _sigmoid(x)
    return x * s, s * (1.0 + x * (1.0 - s))


def _softplus(x):
    return jnp.maximum(x, 0.0) + jnp.log1p(jnp.exp(-jnp.abs(x)))


def _split2(x):
    hi = x.astype(BF16)
    return hi, (x - hi.astype(F32)).astype(BF16)


def _dot1(a, b, mode):
    return lax.dot_general(a.astype(BF16), b.astype(BF16), _DIMS[mode], preferred_element_type=F32)


def _dot3(a, b, mode):
    ah, al = _split2(a)
    bh, bl = _split2(b)
    d = lambda p, q: lax.dot_general(p, q, _DIMS[mode], preferred_element_type=F32)
    return d(ah, bh) + (d(ah, bl) + d(al, bh))


def ada_fwd(c_all, w_sh, b_sh):
    n = w_sh.shape[1]
    tn = 512

    def body(c_ref, w_ref, b_ref, o_ref):
        o_ref[...] = _dot1(_silu(c_ref[...]), w_ref[...], "nn") + b_ref[...]

    return pl.pallas_call(
        body, name="ada_fwd", grid=(n // tn,),
        in_specs=[pl.BlockSpec((N_DEV, D_MODEL), lambda j: (0, 0)), pl.BlockSpec((D_MODEL, tn), lambda j: (0, j)),
                  pl.BlockSpec((1, tn), lambda j: (0, j))],
        out_specs=pl.BlockSpec((N_DEV, tn), lambda j: (0, j)), out_shape=jax.ShapeDtypeStruct((N_DEV, n), F32),
        compiler_params=_cparams(("parallel",)),
    )(c_all, w_sh, b_sh)


def ada_bwd(c_all, dmod_sh):
    n = dmod_sh.shape[1]
    tn = 512

    def body(c_ref, d_ref, o_ref):
        o_ref[...] = _dot1(_silu(c_ref[...]), d_ref[...], "tn")

    return pl.pallas_call(
        body, name="ada_bwd", grid=(n // tn,),
        in_specs=[pl.BlockSpec((N_DEV, D_MODEL), lambda j: (0, 0)), pl.BlockSpec((N_DEV, tn), lambda j: (0, j))],
        out_specs=pl.BlockSpec((D_MODEL, tn), lambda j: (0, j)), out_shape=jax.ShapeDtypeStruct((D_MODEL, n), F32),
        compiler_params=_cparams(("parallel",)),
    )(c_all, dmod_sh)


SHIFT_T, SCALE_T, GATE_T, SHIFT_F, SCALE_F, GATE_F = range(6)


def modulate(x, mod, shift_row, scale_row, name):
    S = x.shape[0]

    def body(x_ref, m_ref, o_ref):
        m = m_ref[...]
        o_ref[...] = (x_ref[...] * (1.0 + m[scale_row:scale_row + 1]) + m[shift_row:shift_row + 1]).astype(BF16)

    return rowcall(body, name=name, S=S, ts=512, ins=[(x, "row"), (mod, "vec")], outs=[((S, D_MODEL), BF16, "row")])[0]


def _conv_fwd(cur, prev, w, width):
    y = cur * w[width - 1:width]
    for j in range(width - 1):
        y = y + _shift_down(cur, prev, width - 1 - j) * w[j:j + 1]
    return y


def _prep_a_core(cur, prev, w):
    return _silu_and_grad(_conv_fwd(cur, prev, w, A_CONV))


def prep_a_fwd(qkv_raw, ba, conv_a, a_log, dt_bias):
    S = qkv_raw.shape[0]

    def body(x_ref, xp_ref, ba_ref, w_ref, al_ref, dt_ref, q_ref, k_ref, v_ref, beta_ref, g_ref):
        first = (pl.program_id(0) > 0).astype(F32)
        y, _ = _prep_a_core(x_ref[...].astype(F32), _halo_prev(xp_ref) * first, w_ref[...])
        for h in range(A_HEADS):
            sl = slice(h * A_DK, (h + 1) * A_DK)
            qh = y[:, sl]
            kh = y[:, A_W + h * A_DK:A_W + (h + 1) * A_DK]
            q_ref[:, sl] = qh * (lax.rsqrt(jnp.sum(qh * qh, axis=-1, keepdims=True) + L2_EPS) * (A_DK ** -0.5))
            k_ref[:, sl] = kh * lax.rsqrt(jnp.sum(kh * kh, axis=-1, keepdims=True) + L2_EPS)
        v_ref[...] = y[:, 2 * A_W:3 * A_W]
        bav = ba_ref[...]
        beta_ref[...] = _sigmoid(bav[:, 0:A_HEADS])
        g_ref[...] = -jnp.exp(al_ref[...]) * _softplus(bav[:, A_HEADS:2 * A_HEADS] + dt_ref[...])

    return rowcall(
        body, name="prep_a_fwd", S=S, ts=256,
        ins=[(qkv_raw, "row"), (qkv_raw, "prev"), (ba, "row"), (conv_a, "vec"), (a_log, "vec"), (dt_bias, "vec")],
        outs=[((S, A_W), F32, "row")] * 3 + [((S, A_HEADS), F32, "row")] * 2)


HEAD_GROUP = 4
GROUP_ROWS = HEAD_GROUP * CHUNK
N_HEAD_GROUPS = A_HEADS // HEAD_GROUP
LOG_CHUNK = int(math.log2(CHUNK))


def _tri_masks():
    rb = lax.broadcasted_iota(jnp.int32, (GROUP_ROWS, GROUP_ROWS), 0)
    cb = lax.broadcasted_iota(jnp.int32, (GROUP_ROWS, GROUP_ROWS), 1)
    same = (rb >> LOG_CHUNK) == (cb >> LOG_CHUNK)
    return dict(causal=same & (rb >= cb), strict=same & (rb > cb), eye=rb == cb, upper=same & (cb >= rb),
                last=cb == (rb | (CHUNK - 1)), rb=rb, cb=cb)


def _col_to_row(colv, eye):
    return jnp.sum(jnp.where(eye, colv, 0.0), axis=0, keepdims=True)


def _row_to_col(rowv, eye):
    return jnp.sum(jnp.where(eye, rowv, 0.0), axis=1, keepdims=True)


def _tri_inv(a_list, mk):
    rb, cb = mk["rb"], mk["cb"]
    ts = [jnp.where(mk["eye"], 1.0, 0.0) - jnp.where((rb >> 1) == (cb >> 1), a, 0.0) for a in a_list]
    for lvl in range(1, LOG_CHUNK):
        rs, cs = rb >> lvl, cb >> lvl
        sel = ((rs & 1) == 1) & (cs == rs - 1)
        inner = [_dot3(t, jnp.where(sel, a, 0.0), "nn") for t, a in zip(ts, a_list)]
        ts = [t - _dot3(i, t, "nn") for i, t in zip(inner, ts)]
    return ts


def _stack_heads(ref, grp):
    return jnp.concatenate([ref[:, (grp * HEAD_GROUP + j) * A_DK:(grp * HEAD_GROUP + j + 1) * A_DK]
                            for j in range(HEAD_GROUP)], axis=0)


def _stack_cols(tile, grp):
    return jnp.concatenate([tile[:, grp * HEAD_GROUP + j:grp * HEAD_GROUP + j + 1] for j in range(HEAD_GROUP)], axis=0)


def _delta_local(q, k, v, beta, g, mk):
    causal, strict, eye = mk["causal"], mk["strict"], mk["eye"]
    g_row = _col_to_row(g, eye)
    gc = jnp.sum(jnp.where(causal, g_row, 0.0), axis=1, keepdims=True)
    gc_row = _col_to_row(gc, eye)
    decay = jnp.where(causal, jnp.exp(jnp.where(causal, gc - gc_row, 0.0)), 0.0)
    gam = jnp.exp(gc)
    kb = k * beta
    vb = v * beta
    y = kb * gam
    a = jnp.where(strict, _dot1(kb, k, "nt") * decay, 0.0)
    p = _dot1(q, k, "nt") * decay
    gl = jnp.sum(jnp.where(mk["last"], gc_row, 0.0), axis=1, keepdims=True)
    kd = k * jnp.exp(gl - gc)
    return dict(gc=gc, decay=decay, gam=gam, kb=kb, vb=vb, y=y, a=a, p=p, gl=gl, kd=kd)


def _head_rows(x, j):
    return x[j * CHUNK:(j + 1) * CHUNK]


def delta_fwd(q, k, v, beta, g):
    S = q.shape[0]
    n_chunks = S // CHUNK

    def body(q_ref, k_ref, v_ref, beta_ref, g_ref, o_ref, sprev_ref, t_ref, state_ref):
        @pl.when(pl.program_id(0) == 0)
        def _():
            state_ref[...] = jnp.zeros_like(state_ref)

        mk = _tri_masks()
        betav, gv = beta_ref[...], g_ref[...]
        groups = range(N_HEAD_GROUPS)
        q_all = [_stack_heads(q_ref, grp) for grp in groups]
        locs = [_delta_local(q_all[grp], _stack_heads(k_ref, grp), _stack_heads(v_ref, grp),
                             _stack_cols(betav, grp), _stack_cols(gv, grp), mk) for grp in groups]
        tinvs = _tri_inv([loc["a"] for loc in locs], mk)
        uws = [_dot3(tinvs[grp], jnp.concatenate([locs[grp]["vb"], locs[grp]["y"]], axis=1), "nn") for grp in groups]
        for grp in groups:
            loc, uw = locs[grp], uws[grp]
            t_ref[0, grp] = tinvs[grp]
            qg = q_all[grp] * loc["gam"]
            egl = jnp.exp(loc["gl"])
            vns, o_state = [], []
            for j in range(HEAD_GROUP):
                h = grp * HEAD_GROUP + j
                s0 = state_ref[h]
                sprev_ref[0, h] = s0
                uw_h = _head_rows(uw, j)
                vn = uw_h[:, :A_DK] - _dot1(uw_h[:, A_DK:], s0, "nn")
                vns.append(vn)
                o_state.append(_dot1(_head_rows(qg, j), s0, "nn"))
                state_ref[h] = s0 * egl[(j + 1) * CHUNK - 1:(j + 1) * CHUNK] + _dot1(_head_rows(loc["kd"], j), vn, "tn")
            o_local = _dot1(loc["p"], jnp.concatenate(vns, axis=0), "nn")
            for j in range(HEAD_GROUP):
                h = grp * HEAD_GROUP + j
                o_ref[:, h * A_DK:(h + 1) * A_DK] = o_state[j] + _head_rows(o_local, j)

    tile = pl.BlockSpec((CHUNK, A_W), lambda n: (n, 0))
    small = pl.BlockSpec((CHUNK, A_HEADS), lambda n: (n, 0))
    return pl.pallas_call(
        body, name="delta_fwd", grid=(n_chunks,), in_specs=[tile, tile, tile, small, small],
        out_specs=[tile, pl.BlockSpec((1, A_HEADS, A_DK, A_DK), lambda n: (n, 0, 0, 0)),
                   pl.BlockSpec((1, N_HEAD_GROUPS, GROUP_ROWS, GROUP_ROWS), lambda n: (n, 0, 0, 0))],
        out_shape=[jax.ShapeDtypeStruct((S, A_W), F32), jax.ShapeDtypeStruct((n_chunks, A_HEADS, A_DK, A_DK), F32),
                   jax.ShapeDtypeStruct((n_chunks, N_HEAD_GROUPS, GROUP_ROWS, GROUP_ROWS), F32)],
        scratch_shapes=[pltpu.VMEM((A_HEADS, A_DK, A_DK), F32)],
        compiler_params=_cparams(("arbitrary",)),
    )(q, k, v, beta, g)


def gate_a_fwd(o_pre, z, norm_w):
    S = o_pre.shape[0]

    def body(o_ref, z_ref, nw_ref, out_ref):
        nw = nw_ref[...]
        for h in range(A_HEADS):
            sl = slice(h * A_DK, (h + 1) * A_DK)
            oh = o_ref[:, sl]
            r = lax.rsqrt(jnp.mean(oh * oh, axis=-1, keepdims=True) + RMS_EPS)
            out_ref[:, sl] = (oh * r * nw * _silu(z_ref[:, sl].astype(F32))).astype(BF16)

    return rowcall(body, name="gate_a_fwd", S=S, ts=512, ins=[(o_pre, "row"), (z, "row"), (norm_w, "vec")],
                   outs=[((S, A_W), BF16, "row")])[0]


HEADS_PER_GROUP = 2
GROUP_W = HEADS_PER_GROUP * B_DH
N_GROUPS = B_HEADS // HEADS_PER_GROUP
PAD_ROWS = B_PREV * CHUNK


Q_TILE = 256
Q_CHUNKS = Q_TILE // CHUNK
KEY_WIN = (B_PREV + Q_CHUNKS) * CHUNK


def _band_probs(qh, kh, bias, valid):
    s = _dot1(qh, kh, "nt") * (B_DH ** -0.5) + bias
    s = jnp.where(valid, s, NEG_INF)
    e = jnp.exp(s - jnp.max(s, axis=-1, keepdims=True))
    return e * (1.0 / jnp.sum(e, axis=-1, keepdims=True))


def _attn_specs(S, tile_rows):
    n_cb = B_W // GROUP_W
    return [pl.BlockSpec((tile_rows, GROUP_W), lambda g, n: (n + PAD_ROWS // tile_rows, g)),
            pl.BlockSpec((PAD_ROWS + S, GROUP_W), lambda g, n: (0, n_cb + g)),
            pl.BlockSpec((PAD_ROWS + S, GROUP_W), lambda g, n: (0, 2 * n_cb + g)),
            pl.BlockSpec((HEADS_PER_GROUP, CHUNK, B_BAND), lambda g, n: (g, 0, 0))]


def _band_valid(first_chunk):
    return lax.broadcasted_iota(jnp.int32, (CHUNK, B_BAND), 1) >= PAD_ROWS - first_chunk * CHUNK


def _chunk_rows(x, qc, rows=CHUNK):
    return x[qc * CHUNK:qc * CHUNK + rows]


FWD_TILE = 512
FWD_CHUNKS = FWD_TILE // CHUNK
FWD_WIN = (B_PREV + FWD_CHUNKS) * CHUNK


def attn_fwd(qkv_pad, bias):
    S = qkv_pad.shape[0] - PAD_ROWS

    def body(q_ref, k_ref, v_ref, b_ref, o_ref):
        n = pl.program_id(1)
        start = pl.multiple_of(n * FWD_TILE, FWD_TILE)
        kwin = k_ref[pl.ds(start, FWD_WIN), :]
        vwin = v_ref[pl.ds(start, FWD_WIN), :]
        qv = q_ref[...]
        pairs = [(qc, hh) for qc in range(FWD_CHUNKS) for hh in range(HEADS_PER_GROUP)]
        sl = lambda hh: slice(hh * B_DH, (hh + 1) * B_DH)
        s = [_dot1(_chunk_rows(qv, qc)[:, sl(hh)], _chunk_rows(kwin, qc, B_BAND)[:, sl(hh)], "nt") for qc, hh in pairs]
        s = [jnp.where(_band_valid(n * FWD_CHUNKS + qc), x * (B_DH ** -0.5) + b_ref[hh], NEG_INF)
             for x, (qc, hh) in zip(s, pairs)]
        e = [jnp.exp(x - jnp.max(x, axis=-1, keepdims=True)) for x in s]
        p = [x * (1.0 / jnp.sum(x, axis=-1, keepdims=True)) for x in e]
        o = [_dot1(x, _chunk_rows(vwin, qc, B_BAND)[:, sl(hh)], "nn") for x, (qc, hh) in zip(p, pairs)]
        rows = [jnp.concatenate(o[qc * HEADS_PER_GROUP:(qc + 1) * HEADS_PER_GROUP], axis=1) for qc in range(FWD_CHUNKS)]
        o_ref[...] = jnp.concatenate(rows, axis=0).astype(BF16)

    return pl.pallas_call(
        body, name="attn_fwd", grid=(N_GROUPS, S // FWD_TILE), in_specs=_attn_specs(S, FWD_TILE),
        out_specs=pl.BlockSpec((FWD_TILE, GROUP_W), lambda g, n: (n, g)),
        out_shape=jax.ShapeDtypeStruct((S, B_W), BF16),
        compiler_params=_cparams(("parallel", "arbitrary")),
    )(qkv_pad, qkv_pad, qkv_pad, bias)


def _rel_onehot(i):
    kj = lax.broadcasted_iota(jnp.int32, (B_BAND, B_REL), 0)
    r = lax.broadcasted_iota(jnp.int32, (B_BAND, B_REL), 1)
    idx = jnp.clip(PAD_ROWS + i - kj, -(CHUNK - 1), B_MAX_REL) + (CHUNK - 1)
    return jnp.where(idx == r, 1.0, 0.0)


def bias_expand(rel_bias):
    def body(rb_ref, o_ref):
        i = pl.program_id(0)
        o_ref[0] = _dot3(rb_ref[...], _rel_onehot(i), "nt")

    return pl.pallas_call(
        body, name="bias_expand", grid=(CHUNK,),
        in_specs=[pl.BlockSpec((B_HEADS, B_REL), lambda i: (0, 0))],
        out_specs=pl.BlockSpec((1, B_HEADS, B_BAND), lambda i: (i, 0, 0)),
        out_shape=jax.ShapeDtypeStruct((CHUNK, B_HEADS, B_BAND), F32),
        compiler_params=_cparams(("parallel",)),
    )(rel_bias)


def bias_reduce(dbias):
    def body(d_ref, o_ref):
        i = pl.program_id(0)

        @pl.when(i == 0)
        def _():
            o_ref[...] = jnp.zeros_like(o_ref)

        o_ref[...] += _dot3(d_ref[0], _rel_onehot(i), "nn")

    return pl.pallas_call(
        body, name="bias_reduce", grid=(CHUNK,),
        in_specs=[pl.BlockSpec((1, B_HEADS, B_BAND), lambda i: (i, 0, 0))],
        out_specs=pl.BlockSpec((B_HEADS, B_REL), lambda i: (0, 0)),
        out_shape=jax.ShapeDtypeStruct((B_HEADS, B_REL), F32),
        compiler_params=_cparams(("arbitrary",)),
    )(dbias)


def merge_fwd(gates_raw, b_gate, ya, yb):
    S = ya.shape[0]

    def body(g_ref, b_ref, ya_ref, yb_ref, o_ref):
        gt = _sigmoid(g_ref[...].astype(F32) + b_ref[...])
        o_ref[...] = (gt[:, :D_MODEL] * ya_ref[...].astype(F32) + gt[:, D_MODEL:] * yb_ref[...].astype(F32)).astype(BF16)

    return rowcall(body, name="merge_fwd", S=S, ts=512,
                   ins=[(gates_raw, "row"), (b_gate, "vec"), (ya, "row"), (yb, "row")],
                   outs=[((S, D_MODEL), BF16, "row")])[0]


def _ln_stats(xpre):
    mu = jnp.mean(xpre, axis=-1, keepdims=True)
    xc = xpre - mu
    rstd = lax.rsqrt(jnp.mean(xc * xc, axis=-1, keepdims=True) + LN_EPS)
    return xc * rstd, rstd


def ln1_fwd(x, mix, mod, ln_g, ln_b):
    S = x.shape[0]

    def body(x_ref, mix_ref, m_ref, g_ref, b_ref, xpre_ref, x1_ref, h2_ref):
        m = m_ref[...]
        xpre = ALPHA * x_ref[...] + m[GATE_T:GATE_T + 1] * mix_ref[...]
        xhat, _ = _ln_stats(xpre)
        x1 = xhat * g_ref[...] + b_ref[...]
        xpre_ref[...] = xpre
        x1_ref[...] = x1
        h2_ref[...] = (x1 * (1.0 + m[SCALE_F:SCALE_F + 1]) + m[SHIFT_F:SHIFT_F + 1]).astype(BF16)

    return rowcall(body, name="ln1_fwd", S=S, ts=512,
                   ins=[(x, "row"), (mix, "row"), (mod, "vec"), (ln_g, "vec"), (ln_b, "vec")],
                   outs=[((S, D_MODEL), F32, "row"), ((S, D_MODEL), F32, "row"), ((S, D_MODEL), BF16, "row")])


STRIP_ROWS = 32
STRIP_COLS = 256


def ffn_act_fwd(up, conv_w, conv_b):
    S = up.shape[0]
    ts = 256

    def body(u_ref, up_ref, w_ref, b_ref, o_ref, ubuf):
        ubuf[0:8] = _halo_prev(up_ref) * (pl.program_id(0) > 0).astype(F32)
        ubuf[8:8 + ts] = u_ref[...].astype(F32)

        def col_block(j, carry):
            gate = pl.ds(pl.multiple_of(j * STRIP_COLS, STRIP_COLS), STRIP_COLS)
            halves = [gate, pl.ds(pl.multiple_of(D_FF + j * STRIP_COLS, STRIP_COLS), STRIP_COLS)]
            w = [w_ref[:, c] for c in halves]
            bias = [b_ref[:, c] for c in halves]
            for r0 in range(0, ts, STRIP_ROWS):
                uc = []
                for h in range(2):
                    x = ubuf[r0:r0 + STRIP_ROWS + 8, halves[h]]
                    uc.append(bias[h] + sum(
                        w[h][t:t + 1] * (x if t == FFN_CONV - 1 else pltpu.roll(x, FFN_CONV - 1 - t, axis=0))[8:]
                        for t in range(FFN_CONV)))
                o_ref[r0:r0 + STRIP_ROWS, gate] = (_silu(uc[0]) * uc[1]).astype(BF16)
            return carry

        lax.fori_loop(0, D_FF // STRIP_COLS, col_block, 0)

    return rowcall(body, name="ffn_act_fwd", S=S, ts=ts,
                   ins=[(up, "row"), (up, "prev"), (conv_w, "vec"), (conv_b, "vec")],
                   outs=[((S, D_FF), BF16, "row")], scratch=[pltpu.VMEM((ts + 8, 2 * D_FF), F32)])[0]


def final_fwd_bwd(x1, ffn, target, mod, ln_g, ln_b):
    S = x1.shape[0]

    def body(x1_ref, f_ref, t_ref, m_ref, g_ref, b_ref, dxpre_ref, dffn_ref, loss_ref, dgate_ref, dg_ref, db_ref):
        gate = m_ref[...][GATE_F:GATE_F + 1]
        ffn_v = f_ref[...]
        xpre = ALPHA * x1_ref[...] + gate * ffn_v
        xhat, rstd = _ln_stats(xpre)
        err = xhat * g_ref[...] + b_ref[...] - t_ref[...]
        loss_ref[...] += 0.5 * jnp.sum(jnp.mean(err * err, axis=-1, keepdims=True), axis=0, keepdims=True)
        dy = err * (1.0 / D_MODEL)
        dg_ref[...] += jnp.sum(dy * xhat, axis=0, keepdims=True)
        db_ref[...] += jnp.sum(dy, axis=0, keepdims=True)
        dyg = dy * g_ref[...]
        dxpre = rstd * (dyg - jnp.mean(dyg, axis=-1, keepdims=True) - xhat * jnp.mean(dyg * xhat, axis=-1, keepdims=True))
        dxpre_ref[...] = dxpre
        dffn_ref[...] = (gate * dxpre).astype(BF16)
        dgate_ref[...] += jnp.sum(dxpre * ffn_v, axis=0, keepdims=True)

    vec = ((1, D_MODEL), F32, "acc")
    return rowcall(body, name="final_fwd_bwd", S=S, ts=512,
                   ins=[(x1, "row"), (ffn, "row"), (target, "row"), (mod, "vec"), (ln_g, "vec"), (ln_b, "vec")],
                   outs=[((S, D_MODEL), F32, "row"), ((S, D_MODEL), BF16, "row"), ((1, 1), F32, "acc"), vec, vec, vec])


def ffn_act_bwd(dact, up, conv_w, conv_b):
    S = up.shape[0]
    ts = 256
    win_u, win_d = STRIP_ROWS + 16, STRIP_ROWS + 8

    def body(d_ref, dn_ref, u_ref, up_ref, un_ref, w_ref, b_ref, dup_ref, dw_ref, db_ref, ubuf, dbuf):
        i = pl.program_id(0)
        ubuf[0:8] = _halo_prev(up_ref) * (i > 0).astype(F32)
        ubuf[8:8 + ts] = u_ref[...].astype(F32)
        ubuf[8 + ts:16 + ts] = _halo_next(un_ref)
        dbuf[0:ts] = d_ref[...].astype(F32)
        dbuf[ts:ts + 8] = _halo_next(dn_ref) * (i < pl.num_programs(0) - 1).astype(F32)

        def col_block(j, carry):
            halves = [pl.ds(pl.multiple_of(j * STRIP_COLS, STRIP_COLS), STRIP_COLS),
                      pl.ds(pl.multiple_of(D_FF + j * STRIP_COLS, STRIP_COLS), STRIP_COLS)]
            w = [w_ref[:, c] for c in halves]
            bias = [b_ref[:, c] for c in halves]
            dw_acc = [[jnp.zeros((1, STRIP_COLS), F32) for _ in range(FFN_CONV)] for _ in halves]
            db_acc = [jnp.zeros((1, STRIP_COLS), F32) for _ in halves]
            for r0 in range(0, ts, STRIP_ROWS):
                shifted = [[x if k == 0 else pltpu.roll(x, k, axis=0) for k in range(FFN_CONV)]
                           for x in (ubuf[r0:r0 + win_u, c] for c in halves)]
                uc = [bias[h] + sum(w[h][t:t + 1] * shifted[h][FFN_CONV - 1 - t][8:8 + win_d] for t in range(FFN_CONV))
                      for h in range(2)]
                dact_w = dbuf[r0:r0 + win_d, halves[0]]
                sg, dsg = _silu_and_grad(uc[0])
                duc = [dact_w * uc[1] * dsg, dact_w * sg]
                for h in range(2):
                    dup = duc[h] * w[h][FFN_CONV - 1:FFN_CONV]
                    for t in range(FFN_CONV - 1):
                        dup = dup + pltpu.roll(duc[h], win_d - (FFN_CONV - 1 - t), axis=0) * w[h][t:t + 1]
                    dup_ref[r0:r0 + STRIP_ROWS, halves[h]] = dup[:STRIP_ROWS].astype(BF16)
                    mine = duc[h][:STRIP_ROWS]
                    db_acc[h] = db_acc[h] + jnp.sum(mine, axis=0, keepdims=True)
                    for t in range(FFN_CONV):
                        dw_acc[h][t] = dw_acc[h][t] + jnp.sum(
                            mine * shifted[h][FFN_CONV - 1 - t][8:8 + STRIP_ROWS], axis=0, keepdims=True)
            for h in range(2):
                dw_ref[:, halves[h]] += jnp.concatenate(dw_acc[h], axis=0)
                db_ref[:, halves[h]] += db_acc[h]
            return carry

        lax.fori_loop(0, D_FF // STRIP_COLS, col_block, 0)

    return rowcall(body, name="ffn_act_bwd", S=S, ts=ts,
                   ins=[(dact, "row"), (dact, "next"), (up, "row"), (up, "prev"), (up, "next"), (conv_w, "vec"), (conv_b, "vec")],
                   outs=[((S, 2 * D_FF), BF16, "row"), ((FFN_CONV, 2 * D_FF), F32, "acc"), ((1, 2 * D_FF), F32, "acc")],
                   scratch=[pltpu.VMEM((ts + 16, 2 * D_FF), F32), pltpu.VMEM((ts + 8, D_FF), F32)])


def ln1_bwd(dxpre2, dh2, xpre1, mix, mod, ln_g, ln_b):
    S = xpre1.shape[0]

    def body(d2_ref, dh_ref, xp_ref, mix_ref, m_ref, g_ref, b_ref, dxpre_ref, dmix_ref,
             dscale_ref, dshift_ref, dgate_ref, dg_ref, db_ref):
        m = m_ref[...]
        xhat, rstd = _ln_stats(xp_ref[...])
        x1 = xhat * g_ref[...] + b_ref[...]
        dh = dh_ref[...]
        dx1 = ALPHA * d2_ref[...] + dh * (1.0 + m[SCALE_F:SCALE_F + 1])
        dscale_ref[...] += jnp.sum(dh * x1, axis=0, keepdims=True)
        dshift_ref[...] += jnp.sum(dh, axis=0, keepdims=True)
        dg_ref[...] += jnp.sum(dx1 * xhat, axis=0, keepdims=True)
        db_ref[...] += jnp.sum(dx1, axis=0, keepdims=True)
        dyg = dx1 * g_ref[...]
        dxpre = rstd * (dyg - jnp.mean(dyg, axis=-1, keepdims=True) - xhat * jnp.mean(dyg * xhat, axis=-1, keepdims=True))
        dxpre_ref[...] = dxpre
        dmix_ref[...] = (m[GATE_T:GATE_T + 1] * dxpre).astype(BF16)
        dgate_ref[...] += jnp.sum(dxpre * mix_ref[...], axis=0, keepdims=True)

    vec = ((1, D_MODEL), F32, "acc")
    return rowcall(body, name="ln1_bwd", S=S, ts=512,
                   ins=[(dxpre2, "row"), (dh2, "row"), (xpre1, "row"), (mix, "row"), (mod, "vec"), (ln_g, "vec"), (ln_b, "vec")],
                   outs=[((S, D_MODEL), F32, "row"), ((S, D_MODEL), BF16, "row"), vec, vec, vec, vec, vec])


def merge_bwd(dmerged, gates_raw, b_gate, ya, yb):
    S = ya.shape[0]

    def body(d_ref, g_ref, b_ref, ya_ref, yb_ref, dya_ref, dyb_ref, dg_ref, dbg_ref):
        gt = _sigmoid(g_ref[...].astype(F32) + b_ref[...])
        d = d_ref[...].astype(F32)
        ga, gb = gt[:, :D_MODEL], gt[:, D_MODEL:]
        dya_ref[...] = (d * ga).astype(BF16)
        dyb_ref[...] = (d * gb).astype(BF16)
        dgr = jnp.concatenate([d * ya_ref[...].astype(F32) * ga * (1.0 - ga),
                               d * yb_ref[...].astype(F32) * gb * (1.0 - gb)], axis=1)
        dg_ref[...] = dgr.astype(BF16)
        dbg_ref[...] += jnp.sum(dgr, axis=0, keepdims=True)

    return rowcall(body, name="merge_bwd", S=S, ts=512,
                   ins=[(dmerged, "row"), (gates_raw, "row"), (b_gate, "vec"), (ya, "row"), (yb, "row")],
                   outs=[((S, D_MODEL), BF16, "row"), ((S, D_MODEL), BF16, "row"), ((S, 2 * D_MODEL), BF16, "row"),
                         ((1, 2 * D_MODEL), F32, "acc")])


def attn_bwd(qkv_pad, bias, do_b):
    S = qkv_pad.shape[0] - PAD_ROWS

    def body(q_ref, k_ref, v_ref, bias_ref, do_ref, dq_ref, dk_ref, dv_ref, db_ref, b_ref):
        n = pl.program_id(1)

        @pl.when(n == 0)
        def _():
            dk_ref[...] = jnp.zeros_like(dk_ref)
            dv_ref[...] = jnp.zeros_like(dv_ref)
            db_ref[...] = jnp.zeros_like(db_ref)
            b_ref[...] = jnp.full(b_ref.shape, NEG_INF, F32)
            for hh in range(HEADS_PER_GROUP):
                for qc in range(Q_CHUNKS):
                    b_ref[hh, qc * CHUNK:(qc + 1) * CHUNK, qc * CHUNK:qc * CHUNK + B_BAND] = bias_ref[hh]

        start = pl.multiple_of(n * Q_TILE, Q_TILE)
        kwin = k_ref[pl.ds(start, KEY_WIN), :]
        vwin = v_ref[pl.ds(start, KEY_WIN), :]
        qv, dov = q_ref[...], do_ref[...]
        valid = lax.broadcasted_iota(jnp.int32, (Q_TILE, KEY_WIN), 1) >= PAD_ROWS - n * Q_TILE
        dqs, dks, dvs = [], [], []
        for hh in range(HEADS_PER_GROUP):
            sl = slice(hh * B_DH, (hh + 1) * B_DH)
            p = _band_probs(qv[:, sl], kwin[:, sl], b_ref[hh], valid)
            dp = _dot1(dov[:, sl], vwin[:, sl], "nt")
            ds = p * (dp - jnp.sum(dp * p, axis=-1, keepdims=True))
            dbh = ds[0:CHUNK, 0:B_BAND]
            for qc in range(1, Q_CHUNKS):
                dbh = dbh + ds[qc * CHUNK:(qc + 1) * CHUNK, qc * CHUNK:qc * CHUNK + B_BAND]
            db_ref[hh] += dbh
            dsq = ds * (B_DH ** -0.5)
            dqs.append(_dot1(dsq, kwin[:, sl], "nn"))
            dks.append(_dot1(dsq, qv[:, sl], "tn"))
            dvs.append(_dot1(p, dov[:, sl], "tn"))
        dq_ref[...] = jnp.concatenate(dqs, axis=1).astype(BF16)
        dk_ref[pl.ds(start, KEY_WIN), :] += jnp.concatenate(dks, axis=1)
        dv_ref[pl.ds(start, KEY_WIN), :] += jnp.concatenate(dvs, axis=1)

    col = pl.BlockSpec((PAD_ROWS + S, GROUP_W), lambda g, n: (0, g))
    tile = pl.BlockSpec((Q_TILE, GROUP_W), lambda g, n: (n, g))
    return pl.pallas_call(
        body, name="attn_bwd", grid=(N_GROUPS, S // Q_TILE), in_specs=_attn_specs(S, Q_TILE) + [tile],
        out_specs=[tile, col, col, pl.BlockSpec((HEADS_PER_GROUP, CHUNK, B_BAND), lambda g, n: (g, 0, 0))],
        out_shape=[jax.ShapeDtypeStruct((S, B_W), BF16), jax.ShapeDtypeStruct((PAD_ROWS + S, B_W), F32),
                   jax.ShapeDtypeStruct((PAD_ROWS + S, B_W), F32), jax.ShapeDtypeStruct((B_HEADS, CHUNK, B_BAND), F32)],
        scratch_shapes=[pltpu.VMEM((HEADS_PER_GROUP, Q_TILE, KEY_WIN), F32)],
        compiler_params=_cparams(("parallel", "arbitrary")),
    )(qkv_pad, qkv_pad, qkv_pad, bias, do_b)


def gate_a_bwd(do_a, o_pre, z, norm_w):
    S = o_pre.shape[0]

    def body(d_ref, o_ref, z_ref, nw_ref, dop_ref, dz_ref, dnw_ref):
        nw = nw_ref[...]
        acc = jnp.zeros((1, A_DK), F32)
        for h in range(A_HEADS):
            sl = slice(h * A_DK, (h + 1) * A_DK)
            oh, zh, dh = o_ref[:, sl], z_ref[:, sl].astype(F32), d_ref[:, sl].astype(F32)
            r = lax.rsqrt(jnp.mean(oh * oh, axis=-1, keepdims=True) + RMS_EPS)
            sz, dsz = _silu_and_grad(zh)
            dz_ref[:, sl] = (dh * oh * r * nw * dsz).astype(BF16)
            acc = acc + jnp.sum(dh * oh * r * sz, axis=0, keepdims=True)
            t = dh * nw * sz
            dop_ref[:, sl] = r * t - oh * (r * r * r) * jnp.mean(t * oh, axis=-1, keepdims=True)
        dnw_ref[...] += acc

    return rowcall(body, name="gate_a_bwd", S=S, ts=512,
                   ins=[(do_a, "row"), (o_pre, "row"), (z, "row"), (norm_w, "vec")],
                   outs=[((S, A_W), F32, "row"), ((S, A_W), BF16, "row"), ((1, A_DK), F32, "acc")])


def delta_bwd(q, k, v, beta, g, sprev, tinv, do):
    S = q.shape[0]
    n_chunks = S // CHUNK

    def body(q_ref, k_ref, v_ref, beta_ref, g_ref, sprev_ref, t_ref, do_ref,
             dq_ref, dk_ref, dv_ref, dbeta_ref, dg_ref, dstate_ref):
        @pl.when(pl.program_id(0) == 0)
        def _():
            dstate_ref[...] = jnp.zeros_like(dstate_ref)

        mk = _tri_masks()
        causal, strict, eye = mk["causal"], mk["strict"], mk["eye"]
        blk_end = (lax.broadcasted_iota(jnp.int32, (GROUP_ROWS, 1), 0) & (CHUNK - 1)) == CHUNK - 1
        lane = lax.broadcasted_iota(jnp.int32, (CHUNK, A_HEADS), 1)
        betav, gv = beta_ref[...], g_ref[...]
        dbeta_t = jnp.zeros((CHUNK, A_HEADS), F32)
        dg_t = jnp.zeros((CHUNK, A_HEADS), F32)
        groups, heads = range(N_HEAD_GROUPS), range(HEAD_GROUP)
        st = [dict() for _ in groups]

        def local_part(grp, s):
            s["qs"], s["ks"], s["vs"] = _stack_heads(q_ref, grp), _stack_heads(k_ref, grp), _stack_heads(v_ref, grp)
            s["dos"] = _stack_heads(do_ref, grp)
            s["bs"] = _stack_cols(betav, grp)
            s["loc"] = loc = _delta_local(s["qs"], s["ks"], s["vs"], s["bs"], _stack_cols(gv, grp), mk)
            s["tinv"] = t_ref[0, grp]
            s["rhs"] = jnp.concatenate([loc["vb"], loc["y"]], axis=1)
            s["uw"] = _dot3(s["tinv"], s["rhs"], "nn")

        def state_part(grp, s):
            loc, uw, dos, qs = s["loc"], s["uw"], s["dos"], s["qs"]
            gam, kd, gl, gc = loc["gam"], loc["kd"], loc["gl"], loc["gc"]
            qg = qs * gam
            egl = jnp.exp(gl)
            hid = [grp * HEAD_GROUP + j for j in heads]
            s0 = [sprev_ref[0, h] for h in hid]
            ds1 = [dstate_ref[h] for h in hid]
            w = [_head_rows(uw, j)[:, A_DK:] for j in heads]
            vn = [_head_rows(uw, j)[:, :A_DK] - _dot1(w[j], s0[j], "nn") for j in heads]
            vns = jnp.concatenate(vn, axis=0)
            dvn_local = _dot1(loc["p"], dos, "tn")
            dvn = [_head_rows(dvn_local, j) + _dot1(_head_rows(kd, j), ds1[j], "nn") for j in heads]
            dvns = jnp.concatenate(dvn, axis=0)
            s["dp"] = jnp.where(causal, _dot1(dos, vns, "nt"), 0.0)
            dqg = jnp.concatenate([_dot1(_head_rows(dos, j), s0[j], "nt") for j in heads], axis=0)
            s["dq"] = dqg * gam
            dgc = jnp.sum(dqg * qg, axis=-1, keepdims=True)
            for j in heads:
                dstate_ref[hid[j]] = (_dot1(_head_rows(qg, j), _head_rows(dos, j), "tn")
                                      + egl[(j + 1) * CHUNK - 1:(j + 1) * CHUNK] * ds1[j] - _dot1(w[j], dvn[j], "tn"))
            dkd = jnp.concatenate([_dot1(vn[j], ds1[j], "nt") for j in heads], axis=0)
            s["dk"] = dkd * jnp.exp(gl - gc)
            t1 = jnp.sum(dkd * kd, axis=-1, keepdims=True)
            dgl = jnp.concatenate(
                [jnp.broadcast_to(jnp.sum(_head_rows(t1, j), axis=0, keepdims=True)
                                  + jnp.sum(jnp.sum(ds1[j] * s0[j], axis=-1, keepdims=True), axis=0, keepdims=True)
                                  * egl[(j + 1) * CHUNK - 1:(j + 1) * CHUNK], (CHUNK, 1)) for j in heads], axis=0)
            s["dgc"] = dgc - t1 + jnp.where(blk_end, dgl, 0.0)
            s["duw"] = jnp.concatenate(
                [dvns, jnp.concatenate([-_dot1(dvn[j], s0[j], "nt") for j in heads], axis=0)], axis=1)

        def solve_part(grp, s):
            s["dvby"] = _dot3(s["tinv"], s["duw"], "tn")
            s["dt"] = _dot3(s["duw"], s["rhs"], "nt")

        def inverse_part_a(grp, s):
            s["tdt"] = _dot3(s["tinv"], s["dt"], "tn")

        def inverse_part_b(grp, s):
            s["da"] = jnp.where(strict, -_dot3(s["tdt"], s["tinv"], "nt"), 0.0)

        def finish(grp, s):
            loc, qs, ks, vs, bs, da, dp, dvby = s["loc"], s["qs"], s["ks"], s["vs"], s["bs"], s["da"], s["dp"], s["dvby"]
            gam, decay = loc["gam"], loc["decay"]
            dm = da * decay
            dn = dp * decay
            e = da * loc["a"] + dp * loc["p"]
            dgc = s["dgc"] + jnp.sum(e, axis=1, keepdims=True) - _row_to_col(jnp.sum(e, axis=0, keepdims=True), eye)
            dy = dvby[:, A_DK:]
            dvb = dvby[:, :A_DK]
            dkb = _dot1(dm, ks, "nn") + dy * gam
            dk = s["dk"] + _dot1(dm, loc["kb"], "tn") + _dot1(dn, qs, "tn") + dkb * bs
            dq = s["dq"] + _dot1(dn, ks, "nn")
            dgc = dgc + jnp.sum(dy * loc["y"], axis=-1, keepdims=True)
            dbeta = jnp.sum(dkb * ks, axis=-1, keepdims=True) + jnp.sum(dvb * vs, axis=-1, keepdims=True)
            dv = dvb * bs
            dgs = jnp.sum(jnp.where(mk["upper"], _col_to_row(dgc, eye), 0.0), axis=1, keepdims=True)
            for j in heads:
                h = grp * HEAD_GROUP + j
                sl = slice(h * A_DK, (h + 1) * A_DK)
                dq_ref[:, sl] = _head_rows(dq, j)
                dk_ref[:, sl] = _head_rows(dk, j)
                dv_ref[:, sl] = _head_rows(dv, j)
            s["dbeta"], s["dgs"] = dbeta, dgs

        for stage in (local_part, state_part, solve_part, inverse_part_a, inverse_part_b, finish):
            for grp in groups:
                stage(grp, st[grp])
        for grp in groups:
            for j in heads:
                h = grp * HEAD_GROUP + j
                dbeta_t = dbeta_t + jnp.where(lane == h, _head_rows(st[grp]["dbeta"], j), 0.0)
                dg_t = dg_t + jnp.where(lane == h, _head_rows(st[grp]["dgs"], j), 0.0)
        dbeta_ref[...] = dbeta_t
        dg_ref[...] = dg_t

    rev = lambda n: (n_chunks - 1 - n, 0)
    rev4 = lambda n: (n_chunks - 1 - n, 0, 0, 0)
    tile = pl.BlockSpec((CHUNK, A_W), rev)
    small = pl.BlockSpec((CHUNK, A_HEADS), rev)
    return pl.pallas_call(
        body, name="delta_bwd", grid=(n_chunks,),
        in_specs=[tile, tile, tile, small, small, pl.BlockSpec((1, A_HEADS, A_DK, A_DK), rev4),
                  pl.BlockSpec((1, N_HEAD_GROUPS, GROUP_ROWS, GROUP_ROWS), rev4), tile],
        out_specs=[tile, tile, tile, small, small],
        out_shape=[jax.ShapeDtypeStruct((S, A_W), F32)] * 3 + [jax.ShapeDtypeStruct((S, A_HEADS), F32)] * 2,
        scratch_shapes=[pltpu.VMEM((A_HEADS, A_DK, A_DK), F32)],
        compiler_params=_cparams(("arbitrary",)),
    )(q, k, v, beta, g, sprev, tinv, do)


def _prep_a_dpre(raw, raw_prev, w, dq, dk, dv):
    y, dy_dpre = _prep_a_core(raw, raw_prev, w)
    parts = []
    for h in range(A_HEADS):
        yq = y[:, h * A_DK:(h + 1) * A_DK]
        dqh = dq[:, h * A_DK:(h + 1) * A_DK]
        rq = lax.rsqrt(jnp.sum(yq * yq, axis=-1, keepdims=True) + L2_EPS)
        parts.append((A_DK ** -0.5) * (rq * dqh - yq * (rq * rq * rq) * jnp.sum(dqh * yq, axis=-1, keepdims=True)))
    for h in range(A_HEADS):
        yk = y[:, A_W + h * A_DK:A_W + (h + 1) * A_DK]
        dkh = dk[:, h * A_DK:(h + 1) * A_DK]
        rk = lax.rsqrt(jnp.sum(yk * yk, axis=-1, keepdims=True) + L2_EPS)
        parts.append(rk * dkh - yk * (rk * rk * rk) * jnp.sum(dkh * yk, axis=-1, keepdims=True))
    parts.append(dv)
    return jnp.concatenate(parts, axis=1) * dy_dpre


def prep_a_bwd(qkv_raw, ba, conv_a, a_log, dt_bias, dq, dk, dv, dbeta, dg):
    S = qkv_raw.shape[0]
    ts = 256

    def body(x_ref, xp_ref, xn_ref, ba_ref, w_ref, al_ref, dt_ref, dq_ref, dqn_ref, dk_ref, dkn_ref, dv_ref, dvn_ref,
             dbeta_ref, dg_ref, draw_ref, dba_ref, dw_ref, dal_ref, ddt_ref):
        i = pl.program_id(0)
        first = (i > 0).astype(F32)
        last = (i < pl.num_programs(0) - 1).astype(F32)
        w = w_ref[...]
        cur, prev = x_ref[...].astype(F32), _halo_prev(xp_ref) * first
        dpre = _prep_a_dpre(cur, prev, w, dq_ref[...], dk_ref[...], dv_ref[...])
        dpre_n = _prep_a_dpre(_halo_next(xn_ref), cur[ts - 8:ts], w, _halo_next(dqn_ref), _halo_next(dkn_ref),
                              _halo_next(dvn_ref)) * last
        for j in range(A_CONV):
            dw_ref[j:j + 1, :] += jnp.sum(dpre * _shift_down(cur, prev, A_CONV - 1 - j), axis=0, keepdims=True)
        draw = dpre * w[A_CONV - 1:A_CONV]
        for j in range(A_CONV - 1):
            draw = draw + _shift_up(dpre, dpre_n, A_CONV - 1 - j) * w[j:j + 1]
        draw_ref[...] = draw.astype(BF16)
        bav = ba_ref[...]
        beta = _sigmoid(bav[:, 0:A_HEADS])
        xa = bav[:, A_HEADS:2 * A_HEADS] + dt_ref[...]
        nexp = -jnp.exp(al_ref[...])
        dgv = dg_ref[...]
        da = dgv * nexp * _sigmoid(xa)
        dba_ref[:, 0:A_HEADS] = dbeta_ref[...] * beta * (1.0 - beta)
        dba_ref[:, A_HEADS:2 * A_HEADS] = da
        dal_ref[...] += jnp.sum(dgv * nexp * _softplus(xa), axis=0, keepdims=True)
        ddt_ref[...] += jnp.sum(da, axis=0, keepdims=True)

    return rowcall(
        body, name="prep_a_bwd", S=S, ts=ts,
        ins=[(qkv_raw, "row"), (qkv_raw, "prev"), (qkv_raw, "next"), (ba, "row"), (conv_a, "vec"), (a_log, "vec"),
             (dt_bias, "vec"), (dq, "row"), (dq, "next"), (dk, "row"), (dk, "next"), (dv, "row"), (dv, "next"),
             (dbeta, "row"), (dg, "row")],
        outs=[((S, 3 * A_W), BF16, "row"), ((S, 2 * A_HEADS), F32, "row"), ((A_CONV, 3 * A_W), F32, "acc"),
              ((1, A_HEADS), F32, "acc"), ((1, A_HEADS), F32, "acc")])


def grad_x_final(dh1, x, dxpre1, mod):
    S = x.shape[0]

    def body(dh_ref, x_ref, dx_ref, m_ref, gx_ref, dscale_ref, dshift_ref):
        dh = dh_ref[...]
        gx_ref[...] = ALPHA * dx_ref[...] + dh * (1.0 + m_ref[...][SCALE_T:SCALE_T + 1])
        dscale_ref[...] += jnp.sum(dh * x_ref[...], axis=0, keepdims=True)
        dshift_ref[...] += jnp.sum(dh, axis=0, keepdims=True)

    vec = ((1, D_MODEL), F32, "acc")
    return rowcall(body, name="grad_x_final", S=S, ts=512, ins=[(dh1, "row"), (x, "row"), (dxpre1, "row"), (mod, "vec")],
                   outs=[((S, D_MODEL), F32, "row"), vec, vec])


_C_QKV, _C_Z, _C_BA, _C_QKVB, _C_G = 0, 3 * A_W, 4 * A_W, 4 * A_W + 2 * A_HEADS, 4 * A_W + 2 * A_HEADS + 3 * B_W
BA_PAD = 128


def split_w_in(w_in):
    ba = jnp.pad(w_in[:, _C_BA:_C_QKVB], ((0, 0), (0, BA_PAD - 2 * A_HEADS)))
    return dict(qkv=w_in[:, _C_QKV:_C_Z], z=w_in[:, _C_Z:_C_BA], ba=ba, qkvb=w_in[:, _C_QKVB:_C_G], g=w_in[:, _C_G:])


def join_w_in(p):
    return jnp.concatenate([p["qkv"], p["z"], p["ba"][:, :2 * A_HEADS], p["qkvb"], p["g"]], axis=1)


def forward_local(x, target, mod, w, sm, late_weights=None):
    h1 = modulate(x, mod, SHIFT_T, SCALE_T, "mod_t")
    qkv_raw = mm(h1, w["qkv"], mode="nn", out_dtype=BF16, name="proj_qkv")
    z = mm(h1, w["z"], mode="nn", out_dtype=BF16, name="proj_z")
    ba = mm(h1, w["ba"], mode="nn", out_dtype=F32, name="proj_ba")
    qkvb = mm(h1, w["qkvb"], mode="nn", out_dtype=BF16, name="proj_qkvb")
    gates_raw = mm(h1, w["g"], mode="nn", out_dtype=BF16, name="proj_g")
    q, k, v, beta, g = prep_a_fwd(qkv_raw, ba, sm["conv_a"], sm["a_log"], sm["dt_bias"])
    o_pre, sprev, tinv = delta_fwd(q, k, v, beta, g)
    o_a = gate_a_fwd(o_pre, z, sm["norm_a"])
    qkv_pad = jnp.pad(qkvb, ((PAD_ROWS, 0), (0, 0)))
    bias = jnp.transpose(bias_expand(sm["rel_bias"]), (1, 0, 2))
    o_b = attn_fwd(qkv_pad, bias)
    if late_weights is not None:
        w = dict(w, **late_weights(o_b))
    ya = mm(o_a, w["branch_a"], mode="nn", out_dtype=BF16, name="branch_a")
    yb = mm(o_b, w["branch_b"], mode="nn", out_dtype=BF16, name="branch_b")
    merged = merge_fwd(gates_raw, sm["b_gate"], ya, yb)
    mix = mm(merged, w["o"], mode="nn", out_dtype=F32, name="mix")
    xpre1, x1, h2 = ln1_fwd(x, mix, mod, sm["ln1_g"], sm["ln1_b"])
    up = mm(h2, w["up"], mode="nn", out_dtype=BF16, name="ffn_up", b_shards=True)
    act = ffn_act_fwd(up, sm["conv_ffn"], sm["b_conv_ffn"])
    ffn = mm(act, w["down"], mode="nn", out_dtype=F32, name="ffn_down")
    dxpre2, dffn, loss, dgate_f, dln2_g, dln2_b = final_fwd_bwd(x1, ffn, target, mod, sm["ln2_g"], sm["ln2_b"])
    saved = dict(h1=h1, qkv_raw=qkv_raw, z=z, ba=ba, gates_raw=gates_raw, q=q, k=k, v=v, beta=beta, g=g,
                 o_pre=o_pre, sprev=sprev, tinv=tinv, o_a=o_a, qkv_pad=qkv_pad, bias=bias, o_b=o_b, ya=ya, yb=yb,
                 merged=merged, mix=mix, xpre1=xpre1, x1=x1, h2=h2, up=up, act=act, ffn=ffn, w=w)
    return loss, dxpre2, dffn, dict(gate_f=dgate_f, ln2_g=dln2_g, ln2_b=dln2_b), saved


def backward_local(x, mod, sm, dxpre2, dffn, fin, sv, hooks=None):
    w = sv["w"]
    dact = mm(dffn, w["down"], mode="nt", out_dtype=BF16, name="d_act")
    gw_down = mm(sv["act"], dffn, mode="tn", out_dtype=BF16, name="gw_down")
    dup, dconv_ffn, db_conv_ffn = ffn_act_bwd(dact, sv["up"], sm["conv_ffn"], sm["b_conv_ffn"])
    dh2 = mm(dup, w["up"], mode="nt", out_dtype=F32, name="d_h2", b_shards=True)
    gw_up = mm(sv["h2"], dup, mode="tn", out_dtype=BF16, name="gw_up", out_shards=N_CHIPS)
    dxpre1, dmix, dsc_f, dsh_f, dgate_t, dln1_g, dln1_b = ln1_bwd(
        dxpre2, dh2, sv["xpre1"], sv["mix"], mod, sm["ln1_g"], sm["ln1_b"])
    dmerged = mm(dmix, w["o"], mode="nt", out_dtype=BF16, name="d_merged")
    gw_o = mm(sv["merged"], dmix, mode="tn", out_dtype=BF16, name="gw_o")
    dya, dyb, dgates, db_gate = merge_bwd(dmerged, sv["gates_raw"], sm["b_gate"], sv["ya"], sv["yb"])
    do_a = mm(dya, w["branch_a"], mode="nt", out_dtype=BF16, name="d_oa")
    gw_branch_a = mm(sv["o_a"], dya, mode="tn", out_dtype=BF16, name="gw_branch_a")
    do_b = mm(dyb, w["branch_b"], mode="nt", out_dtype=BF16, name="d_ob")
    gw_branch_b = mm(sv["o_b"], dyb, mode="tn", out_dtype=BF16, name="gw_branch_b")
    bias = sv["bias"]
    if hooks is not None:
        bias = bias + hooks["late_start"](dict(w_branch_a=gw_branch_a, w_branch_b=gw_branch_b, w_o=gw_o, w_up=gw_up,
                                               w_down=gw_down))[0, 0]
    dq_b, dk_pad, dv_pad, dbias = attn_bwd(sv["qkv_pad"], bias, do_b)
    if hooks is not None:
        dbias = dbias + hooks["late_finish"](dq_b)[0, 0]
    dqkvb = jnp.concatenate([dq_b, dk_pad[PAD_ROWS:].astype(BF16), dv_pad[PAD_ROWS:].astype(BF16)], axis=1)
    drel_bias = bias_reduce(jnp.transpose(dbias, (1, 0, 2)))
    do_pre, dz, dnorm_a = gate_a_bwd(do_a, sv["o_pre"], sv["z"], sm["norm_a"])
    dq, dk, dv, dbeta, dg = delta_bwd(sv["q"], sv["k"], sv["v"], sv["beta"], sv["g"], sv["sprev"], sv["tinv"], do_pre)
    dqkv_raw, dba16, dconv_a, da_log, ddt_bias = prep_a_bwd(
        sv["qkv_raw"], sv["ba"], sm["conv_a"], sm["a_log"], sm["dt_bias"], dq, dk, dv, dbeta, dg)
    dba = jnp.pad(dba16, ((0, 0), (0, BA_PAD - 2 * A_HEADS))).astype(BF16)
    pieces = dict(qkv=dqkv_raw, z=dz, ba=dba, qkvb=dqkvb, g=dgates)
    gw_in = join_w_in({key: mm(sv["h1"], dpiece, mode="tn", out_dtype=BF16, name="gw_in_" + key)
                       for key, dpiece in pieces.items()})
    w_ba = w["ba"]
    w_z = w["z"]
    if hooks is not None:
        w_ba = w_ba + hooks["w_in_start"](gw_in)[0, 0].astype(BF16)
    dh1 = mm(pieces["ba"], w_ba, mode="nt", out_dtype=F32, name="d_h1_ba")
    dh1 = mm(pieces["qkv"], w["qkv"], mode="nt", out_dtype=F32, name="d_h1_qkv", acc_in=dh1)
    if hooks is not None:
        w_z = w_z + hooks["w_in_finish"](dh1)[0, 0].astype(BF16)
    dh1 = mm(pieces["z"], w_z, mode="nt", out_dtype=F32, name="d_h1_z", acc_in=dh1)
    for key in ("qkvb", "g"):
        dh1 = mm(pieces[key], w[key], mode="nt", out_dtype=F32, name="d_h1_" + key, acc_in=dh1)
    grad_x, dsc_t, dsh_t = grad_x_final(dh1, x, dxpre1, mod)
    dmod = jnp.concatenate([dsh_t, dsc_t, dgate_t, dsh_f, dsc_f, fin["gate_f"]], axis=0)
    gw = dict(w_in=gw_in, w_branch_a=gw_branch_a, w_branch_b=gw_branch_b, w_o=gw_o, w_up=gw_up, w_down=gw_down)
    gs = dict(b_gate=db_gate, conv_a=dconv_a, a_log=da_log, dt_bias=ddt_bias, norm_a=dnorm_a, rel_bias=drel_bias,
              ln1_g=dln1_g, ln1_b=dln1_b, conv_ffn=dconv_ffn, b_conv_ffn=db_conv_ffn, ln2_g=fin["ln2_g"], ln2_b=fin["ln2_b"])
    return grad_x, dmod, gw, gs


MESH = pl.DeviceIdType.MESH
ANY = pl.BlockSpec(memory_space=pl.ANY)
WHOLE_VMEM = pl.BlockSpec(memory_space=pltpu.VMEM)


def _place():
    return lax.axis_index("x"), lax.axis_index("y"), lax.axis_index("c")


def allgather8(blk, name):
    m_per, n = blk.shape

    def body(x_ref, out_ref, send_sems, recv_sems, local_sem):
        x, y, c = _place()
        me, sibling = (x, y, c), (x, y, 1 - c)
        chips = [(1 - x, y), (x, 1 - y), (1 - x, 1 - y)]

        def rows(px, py, pc):
            return out_ref.at[pl.ds((4 * px + 2 * py + pc) * m_per, m_per), :]

        def copy(k, block, to, src=None):
            return pltpu.make_async_remote_copy(
                src_ref=rows(*block) if src is None else src, dst_ref=rows(*block),
                send_sem=send_sems.at[k], recv_sem=recv_sems.at[k], device_id=to, device_id_type=MESH)

        mine = pltpu.make_async_copy(x_ref, rows(*me), local_sem)
        mine.start()
        first = [copy(0, me, sibling, src=x_ref)]
        first += [copy(1 + j, me, (*chip, c), src=x_ref) for j, chip in enumerate(chips)]
        for cp in first:
            cp.start()
        passed = [copy(4 + j, (*chip, c), sibling) for j, chip in enumerate(chips)]
        for j, chip in enumerate(chips):
            copy(1 + j, (*chip, c), me).wait_recv()
            passed[j].start()
        copy(0, sibling, me).wait_recv()
        for j, chip in enumerate(chips):
            copy(4 + j, (*chip, 1 - c), me).wait_recv()
        for cp in first + passed:
            cp.wait_send()
        mine.wait()

    return pl.pallas_call(
        body, name=name, out_shape=jax.ShapeDtypeStruct((N_DEV * m_per, n), blk.dtype),
        in_specs=[WHOLE_VMEM], out_specs=WHOLE_VMEM,
        scratch_shapes=[pltpu.SemaphoreType.DMA((7,)), pltpu.SemaphoreType.DMA((7,)), pltpu.SemaphoreType.DMA],
    )(blk)


def _chip_peers(x, y):
    return [(1 - x, y), (x, 1 - y), (1 - x, 1 - y)]


def chip_exchange(arrs, name, scatter):
    n = len(arrs)

    def body(*refs):
        ins, outs = refs[:n], refs[n:2 * n]
        send_sems, recv_sems, local_sems = refs[2 * n:]
        x, y, c = _place()
        me = 2 * x + y
        sibling = (x, y, 1 - c)
        peers = _chip_peers(x, y)

        def half(ref, which):
            r2 = ref.shape[0] // 2
            return ref.at[pl.ds(which * r2, r2), :]

        def outgoing(a, chip):
            return ins[a].at[chip] if scatter else ins[a]

        def copy(k, src, dst, to):
            return pltpu.make_async_remote_copy(src_ref=src, dst_ref=dst, send_sem=send_sems.at[k],
                                                recv_sem=recv_sems.at[k], device_id=to, device_id_type=MESH)

        started, local = [], []
        for a in range(n):
            lc = pltpu.make_async_copy(outgoing(a, me), outs[a].at[me], local_sems.at[a])
            lc.start()
            local.append(lc)
            for j, (px, py) in enumerate(peers):
                cp = copy(6 * a + j, half(outgoing(a, 2 * px + py), c), half(outs[a].at[me], c), (px, py, c))
                cp.start()
                started.append(cp)
        for a in range(n):
            for j, (px, py) in enumerate(peers):
                landed = half(outs[a].at[2 * px + py], c)
                copy(6 * a + j, landed, landed, (px, py, c)).wait_recv()
                relay = copy(6 * a + 3 + j, landed, landed, sibling)
                relay.start()
                started.append(relay)
        for a in range(n):
            for j, (px, py) in enumerate(peers):
                other = half(outs[a].at[2 * px + py], 1 - c)
                copy(6 * a + 3 + j, other, other, sibling).wait_recv()
        for cp in started:
            cp.wait_send()
        for lc in local:
            lc.wait()

    out_shape = [jax.ShapeDtypeStruct(a.shape if scatter else (N_CHIPS,) + a.shape, a.dtype) for a in arrs]
    return pl.pallas_call(
        body, name=name, out_shape=out_shape, in_specs=[ANY] * n, out_specs=[ANY] * n,
        scratch_shapes=[pltpu.SemaphoreType.DMA((6 * n,)), pltpu.SemaphoreType.DMA((6 * n,)), pltpu.SemaphoreType.DMA((n,))],
    )(*arrs)


HBM_SPEC = pl.BlockSpec(memory_space=pltpu.HBM)
SEM_SPEC = pl.BlockSpec(memory_space=pltpu.SEMAPHORE)
SIDE_EFFECT = pltpu.SideEffectType.DATAFLOW_SIDE_EFFECTING


def _in_hbm(a):
    return pltpu.with_memory_space_constraint(a, pltpu.HBM)


def exchange_start(arrs, name, scatter, after):
    n = len(arrs)
    lands = [lax.empty(a.shape if scatter else (N_CHIPS,) + a.shape, a.dtype) for a in arrs]

    def body(*refs):
        ins, zones = refs[:n], refs[n:2 * n]
        send_sems, recv_sems, token = refs[2 * n + 1], refs[2 * n + 2], refs[-1]
        x, y, c = _place()
        me = 2 * x + y
        for a in range(n):
            for j, (px, py) in enumerate(_chip_peers(x, y)):
                pltpu.make_async_remote_copy(
                    src_ref=ins[a].at[2 * px + py] if scatter else ins[a], dst_ref=zones[a].at[me],
                    send_sem=send_sems.at[3 * a + j], recv_sem=recv_sems.at[3 * a + j],
                    device_id=(px, py, c), device_id_type=MESH).start()
        token[...] = jnp.zeros_like(token)

    res = pl.pallas_call(
        body, name=name,
        out_shape=[pltpu.SemaphoreType.DMA((3 * n,)), pltpu.SemaphoreType.DMA((3 * n,))]
        + [pltpu.HBM(a.shape, a.dtype) for a in arrs] + [pltpu.HBM(z.shape, z.dtype) for z in lands]
        + [jax.ShapeDtypeStruct((8, 128), F32)],
        in_specs=[HBM_SPEC] * (2 * n) + [ANY], out_specs=[SEM_SPEC, SEM_SPEC] + [HBM_SPEC] * (2 * n) + [WHOLE_VMEM],
        input_output_aliases={i: 2 + i for i in range(2 * n)},
        compiler_params=pltpu.CompilerParams(has_side_effects=SIDE_EFFECT),
    )(*[_in_hbm(a) for a in arrs], *[_in_hbm(z) for z in lands], after)
    return dict(send=res[0], recv=res[1], src=res[2:2 + n], zones=res[2 + n:2 + 2 * n], token=res[-1], scatter=scatter)


def exchange_wait(handle, name, after):
    srcs, zones, scatter = handle["src"], handle["zones"], handle["scatter"]
    n = len(srcs)

    def body(*refs):
        ins, lands = refs[:n], refs[n:2 * n]
        send_sems, recv_sems = refs[2 * n], refs[2 * n + 1]
        x, y, c = _place()
        me = 2 * x + y
        for a in range(n):
            for j, (px, py) in enumerate(_chip_peers(x, y)):
                cp = pltpu.make_async_remote_copy(
                    src_ref=ins[a].at[me] if scatter else ins[a], dst_ref=lands[a].at[2 * px + py],
                    send_sem=send_sems.at[3 * a + j], recv_sem=recv_sems.at[3 * a + j],
                    device_id=(px, py, c), device_id_type=MESH)
                cp.wait_send()
                cp.wait_recv()

    res = pl.pallas_call(
        body, name=name, out_shape=[pltpu.HBM(a.shape, a.dtype) for a in list(srcs) + list(zones)],
        in_specs=[HBM_SPEC] * (2 * n) + [SEM_SPEC, SEM_SPEC, ANY], out_specs=[HBM_SPEC] * (2 * n),
        input_output_aliases={i: i for i in range(2 * n)},
        compiler_params=pltpu.CompilerParams(has_side_effects=SIDE_EFFECT),
    )(*srcs, *zones, handle["send"], handle["recv"], after)
    return res[n:]


def swap_start(arrs, name, after):
    n = len(arrs)
    lands = [lax.empty(a.shape, a.dtype) for a in arrs]

    def body(*refs):
        ins, zones = refs[:n], refs[n:2 * n]
        send_sems, recv_sems, token = refs[2 * n + 1], refs[2 * n + 2], refs[-1]
        x, y, c = _place()
        for a in range(n):
            pltpu.make_async_remote_copy(src_ref=ins[a], dst_ref=zones[a], send_sem=send_sems.at[a], recv_sem=recv_sems.at[a],
                                         device_id=(x, y, 1 - c), device_id_type=MESH).start()
        token[...] = jnp.zeros_like(token)

    res = pl.pallas_call(
        body, name=name,
        out_shape=[pltpu.SemaphoreType.DMA((n,)), pltpu.SemaphoreType.DMA((n,))]
        + [pltpu.HBM(a.shape, a.dtype) for a in arrs] * 2 + [jax.ShapeDtypeStruct((8, 128), F32)],
        in_specs=[HBM_SPEC] * (2 * n) + [ANY], out_specs=[SEM_SPEC, SEM_SPEC] + [HBM_SPEC] * (2 * n) + [WHOLE_VMEM],
        input_output_aliases={i: 2 + i for i in range(2 * n)},
        compiler_params=pltpu.CompilerParams(has_side_effects=SIDE_EFFECT),
    )(*[_in_hbm(a) for a in arrs], *[_in_hbm(z) for z in lands], after)
    return dict(send=res[0], recv=res[1], src=res[2:2 + n], zones=res[2 + n:2 + 2 * n], token=res[-1])


def swap_wait(handle, name, after):
    srcs, zones = handle["src"], handle["zones"]
    n = len(srcs)

    def body(*refs):
        ins, lands = refs[:n], refs[n:2 * n]
        send_sems, recv_sems = refs[2 * n], refs[2 * n + 1]
        x, y, c = _place()
        for a in range(n):
            cp = pltpu.make_async_remote_copy(src_ref=ins[a], dst_ref=lands[a], send_sem=send_sems.at[a],
                                              recv_sem=recv_sems.at[a], device_id=(x, y, 1 - c), device_id_type=MESH)
            cp.wait_send()
            cp.wait_recv()

    res = pl.pallas_call(
        body, name=name, out_shape=[pltpu.HBM(a.shape, a.dtype) for a in list(srcs) + list(zones)],
        in_specs=[HBM_SPEC] * (2 * n) + [SEM_SPEC, SEM_SPEC, ANY], out_specs=[HBM_SPEC] * (2 * n),
        input_output_aliases={i: i for i in range(2 * n)},
        compiler_params=pltpu.CompilerParams(has_side_effects=SIDE_EFFECT),
    )(*srcs, *zones, handle["send"], handle["recv"], after)
    return res[:n], res[n:]


TILE_BYTES = 2 * 1024 * 1024


def _row_tile(rows, row_bytes):
    if rows * row_bytes <= TILE_BYTES or rows % 8:
        return rows
    best = 8
    for t in range(8, rows + 1, 8):
        if rows % t == 0 and t * row_bytes <= TILE_BYTES:
            best = t
    return best


def pair_add(a, b, name):
    shape = a.shape
    a, b = a.reshape(-1, shape[-1]), b.reshape(-1, shape[-1])
    R, C = a.shape
    tr = _row_tile(R, C * 4)

    def body(a_ref, b_ref, o_ref):
        o_ref[...] = (a_ref[...].astype(F32) + b_ref[...].astype(F32)).astype(BF16)

    spec = pl.BlockSpec((tr, C), lambda i: (i, 0))
    return pl.pallas_call(body, name=name, grid=(R // tr,), in_specs=[spec, spec], out_specs=spec,
                          out_shape=jax.ShapeDtypeStruct((R, C), BF16), compiler_params=_cparams(("parallel",)))(a, b).reshape(shape)


def sum_lead(parts, name):
    K, R, C = parts.shape
    tr = _row_tile(R, C * 4)

    def body(p_ref, o_ref):
        acc = p_ref[0].astype(F32)
        for j in range(1, K):
            acc = acc + p_ref[j].astype(F32)
        o_ref[...] = acc

    return pl.pallas_call(
        body, name=name, grid=(R // tr,), in_specs=[pl.BlockSpec((K, tr, C), lambda i: (0, i, 0))],
        out_specs=pl.BlockSpec((tr, C), lambda i: (i, 0)), out_shape=jax.ShapeDtypeStruct((R, C), F32),
        compiler_params=_cparams(("parallel",)))(parts)


def adamw(w, g, m, v, name):
    R, C = w.shape
    tr = _row_tile(R, C * 4)

    def body(w_ref, g_ref, m_ref, v_ref, d_ref, mo_ref, vo_ref):
        gv = g_ref[...]
        m2 = ADAM_B1 * m_ref[...] + (1.0 - ADAM_B1) * gv
        v2 = ADAM_B2 * v_ref[...] + (1.0 - ADAM_B2) * (gv * gv)
        m_hat = m2 / (1.0 - ADAM_B1 ** ADAM_STEP)
        v_hat = v2 / (1.0 - ADAM_B2 ** ADAM_STEP)
        d_ref[...] = -ADAM_LR * (m_hat / (jnp.sqrt(v_hat) + ADAM_EPS) + ADAM_WD * w_ref[...])
        mo_ref[...] = m2
        vo_ref[...] = v2

    spec = pl.BlockSpec((tr, C), lambda i: (i, 0))
    return pl.pallas_call(body, name=name, grid=(R // tr,), in_specs=[spec] * 4, out_specs=[spec] * 3,
                          out_shape=[jax.ShapeDtypeStruct((R, C), F32)] * 3, compiler_params=_cparams(("parallel",)))(w, g, m, v)


LANES = 1024


def _pack(arrs, rows):
    out, offs, r = [], [], 0
    for a in arrs:
        flat = a.reshape(-1)
        nr = -(-flat.shape[0] // LANES)
        out.append(jnp.pad(flat, (0, nr * LANES - flat.shape[0])))
        offs.append(r)
        r += nr
    assert r <= rows, (r, rows)
    out.append(jnp.zeros(((rows - r) * LANES,), F32))
    return jnp.concatenate(out).reshape(rows, LANES), offs


def _unpack(packed, offs, shapes):
    flat = packed.reshape(-1)
    return [flat[o * LANES:o * LANES + math.prod(s)].reshape(s) for o, s in zip(offs, shapes)]


WEIGHTS = ["w_ada", "b_ada", "w_in", "b_gate", "conv_a", "a_log", "dt_bias", "norm_a", "rel_bias", "w_branch_a",
           "w_branch_b", "w_o", "ln1_g", "ln1_b", "w_up", "conv_ffn", "b_conv_ffn", "w_down", "ln2_g", "ln2_b"]
BIG = ["w_in", "w_branch_a", "w_branch_b", "w_o", "w_up", "w_down"]
LATE = [n for n in BIG if n != "w_in"]
KEPT_SHARDED = {"w_up"}
COL_SHARDED = {"w_in", "w_up"}
SMALL_SHARDED = {"conv_a": 3 * A_W // N_CHIPS, "rel_bias": B_REL // N_CHIPS, "conv_ffn": 2 * D_FF // N_CHIPS}
SMALL = [n for n in WEIGHTS if n not in BIG and n != "w_ada"]


def _to_full(g4, name):
    if name in KEPT_SHARDED:
        return g4
    if name in COL_SHARDED:
        return jnp.transpose(g4, (1, 0, 2)).reshape(g4.shape[1], -1)
    return g4.reshape(-1, g4.shape[2])


def _to_shards(full, name):
    if name in KEPT_SHARDED:
        return full
    if name in COL_SHARDED:
        return jnp.transpose(full.reshape(full.shape[0], N_CHIPS, -1), (1, 0, 2))
    return full.reshape(N_CHIPS, -1, full.shape[1])


def kernel(x, c, w_ada, b_ada, w_in, b_gate, conv_a, a_log, dt_bias, norm_a, rel_bias, w_branch_a, w_branch_b, w_o, ln1_g, ln1_b, w_up, conv_ffn, b_conv_ffn, w_down, ln2_g, ln2_b, loss_target, m_w_ada, m_b_ada, m_w_in, m_b_gate, m_conv_a, m_a_log, m_dt_bias, m_norm_a, m_rel_bias, m_w_branch_a, m_w_branch_b, m_w_o, m_ln1_g, m_ln1_b, m_w_up, m_conv_ffn, m_b_conv_ffn, m_w_down, m_ln2_g, m_ln2_b, v_w_ada, v_b_ada, v_w_in, v_b_gate, v_conv_a, v_a_log, v_dt_bias, v_norm_a, v_rel_bias, v_w_branch_a, v_w_branch_b, v_w_o, v_ln1_g, v_ln1_b, v_w_up, v_conv_ffn, v_b_conv_ffn, v_w_down, v_ln2_g, v_ln2_b):
    args = dict(locals())
    wts = {n: args[n] for n in WEIGHTS}
    moms = {n: args["m_" + n] for n in WEIGHTS}
    vars_ = {n: args["v_" + n] for n in WEIGHTS}
    xi, yi, ci = _place()
    chip = 2 * xi + yi
    dev = 4 * xi + 2 * yi + ci
    ada_cols = w_ada.shape[2]

    c_all = allgather8(jnp.pad(c, ((0, 7), (0, 0))), "gather_c").reshape(N_DEV, 8, D_MODEL)[:, 0]
    b_ada_sh = lax.dynamic_slice(b_ada, (0, chip * ada_cols), (1, ada_cols))
    mod_sh = ada_fwd(c_all, w_ada[0], b_ada_sh)
    mod_g = allgather8(mod_sh, "gather_mod").reshape(N_CHIPS, 2, N_DEV, ada_cols)[:, 0]
    mod = lax.dynamic_slice(mod_g, (0, dev, 0), (N_CHIPS, 1, ada_cols)).reshape(6, D_MODEL)

    (w_in_g4,) = chip_exchange([wts["w_in"][0].astype(BF16)], "gather_w_in", scatter=False)
    wd = split_w_in(_to_full(w_in_g4, "w_in"))
    late_shards = [wts[n][0].astype(BF16) for n in LATE]
    late_gather = exchange_start(late_shards, "gather_late_start", scatter=False, after=w_in_g4)
    mod = mod + late_gather["token"][0, 0]

    def late_weights(after):
        zones = exchange_wait(late_gather, "gather_late_wait", after)
        full = [_to_full(lax.dynamic_update_slice(z, s[None], (chip, 0, 0)), n) for n, z, s in zip(LATE, zones, late_shards)]
        return {n[2:]: f for n, f in zip(LATE, full)}

    sshapes = [wts[n].shape[1:] for n in SMALL_SHARDED]
    spack, soffs = _pack([wts[n][0] for n in SMALL_SHARDED], 16)
    sg = allgather8(spack, "gather_small_w").reshape(N_CHIPS, 2, 16, LANES)[:, 0]
    sparts = [_unpack(sg[j], soffs, sshapes) for j in range(N_CHIPS)]
    sm = {n: wts[n] for n in SMALL if n not in SMALL_SHARDED and n != "b_ada"}
    for i, n in enumerate(SMALL_SHARDED):
        sm[n] = jnp.concatenate([sparts[j][i] for j in range(N_CHIPS)], axis=-1)

    early = {}

    def late_start(g):
        early["swap"] = swap_start([g[n] for n in LATE], "grad_swap_late_start", g[LATE[0]])
        return early["swap"]["token"]

    def late_finish(after):
        mine, theirs = swap_wait(early["swap"], "grad_swap_late_wait", after)
        early["sums"] = [_to_shards(pair_add(a, b, "grad_pair_" + n), n) for n, a, b in zip(LATE, mine, theirs)]
        early["scatter"] = exchange_start(early["sums"], "grad_scatter_start", scatter=True, after=theirs[0])
        return early["scatter"]["token"]

    def w_in_start(g):
        early["swap_in"] = swap_start([g], "grad_swap_w_in_start", g)
        return early["swap_in"]["token"]

    def w_in_finish(after):
        (mine,), (theirs,) = swap_wait(early["swap_in"], "grad_swap_w_in_wait", after)
        early["sum_in"] = _to_shards(pair_add(mine, theirs, "grad_pair_w_in"), "w_in")
        early["scatter_in"] = exchange_start([early["sum_in"]], "grad_scatter_w_in_start", scatter=True, after=theirs)
        return early["scatter_in"]["token"]

    hooks = dict(late_start=late_start, late_finish=late_finish, w_in_start=w_in_start, w_in_finish=w_in_finish)
    loss, dxpre2, dffn, fin, sv = forward_local(x[0], loss_target[0], mod, wd, sm, late_weights)
    grad_x, dmod, gw, gs = backward_local(x[0], mod, sm, dxpre2, dffn, fin, sv, hooks)

    gnames = [n for n in SMALL if n != "b_ada"]
    vec, voffs = _pack([dmod] + [gs[n] for n in gnames] + [loss], 56)
    gathered = allgather8(vec, "gather_small_g").reshape(N_DEV, 56, LANES)
    summed = sum_lead(gathered, "sum_small_g")
    full_shapes = [(6, D_MODEL)] + [gs[n].shape for n in gnames] + [(1, 1)]
    parts = _unpack(summed, voffs, full_shapes)
    grads = {"b_ada": parts[0].reshape(1, -1)}
    for n, p in zip(gnames, parts[1:-1]):
        if n in SMALL_SHARDED:
            p = lax.dynamic_slice_in_dim(p, chip * SMALL_SHARDED[n], SMALL_SHARDED[n], axis=1)
        grads[n] = p.reshape(wts[n].shape)
    loss_total = parts[-1].reshape(())
    dmod_all = gathered[:, 0:6, :].reshape(N_DEV, 6 * D_MODEL)
    grads["w_ada"] = ada_bwd(c_all, lax.dynamic_slice(dmod_all, (0, chip * ada_cols), (N_DEV, ada_cols)))[None]

    def own_slot(zone, sums):
        return lax.dynamic_update_slice(zone, lax.dynamic_slice_in_dim(sums, chip, 1, axis=0), (chip, 0, 0))

    zones = exchange_wait(early["scatter"], "grad_scatter_wait", summed)
    for n, z, s in zip(LATE, zones, early["sums"]):
        grads[n] = sum_lead(own_slot(z, s), "grad_sum_" + n)[None]

    delta, new_m, new_v = {}, {}, {}

    def update(n):
        d, m2, v2 = adamw(wts[n][0], grads[n][0], moms[n][0], vars_[n][0], "adamw_" + n)
        delta[n], new_m[n], new_v[n] = d[None], m2[None], v2[None]

    for n in ["w_ada"] + LATE:
        update(n)
    shapes = [wts[n].shape for n in SMALL]
    packs = [_pack([t[n] for n in SMALL], 32) for t in (wts, grads, moms, vars_)]
    outs = adamw(*[p[0] for p in packs], "adamw_small")
    for res, o in zip((delta, new_m, new_v), outs):
        for n, a in zip(SMALL, _unpack(o, packs[0][1], shapes)):
            res[n] = a
    (zone_in,) = exchange_wait(early["scatter_in"], "grad_scatter_w_in_wait", outs[0])
    grads["w_in"] = sum_lead(own_slot(zone_in, early["sum_in"]), "grad_sum_w_in")[None]
    update("w_in")
    return (loss_total, grad_x[None], *[grads[n] for n in WEIGHTS], *[delta[n] for n in WEIGHTS],
            *[new_m[n] for n in WEIGHTS], *[new_v[n] for n in WEIGHTS])
```

```python
import functools
import math

import jax
import jax.numpy as jnp
from jax import lax
from jax.experimental import pallas as pl
from jax.experimental.pallas import tpu as pltpu

F32 = jnp.float32
BF16 = jnp.bfloat16

D_MODEL = 1024
CHUNK = 64
A_HEADS = 8
A_DK = 128
A_W = A_HEADS * A_DK
A_CONV = 4
B_HEADS = 16
B_DH = 64
B_W = B_HEADS * B_DH
B_PREV = 8
B_BAND = (B_PREV + 1) * CHUNK
B_MAX_REL = 256
B_REL = CHUNK - 1 + B_MAX_REL + 1
D_FF = 2816
FFN_CONV = 3
IN_COLS = 4 * A_W + 2 * A_HEADS + 3 * B_W + 2 * D_MODEL
ALPHA = 2.0 ** 0.25
LN_EPS = 1e-5
RMS_EPS = 1e-6
L2_EPS = 1e-6
NEG_INF = -1e30
ADAM_LR, ADAM_B1, ADAM_B2, ADAM_EPS, ADAM_WD, ADAM_STEP = 0.001, 0.9, 0.999, 1e-08, 0.01, 10
N_CHIPS = 4
N_DEV = 8
VMEM_LIMIT = 56 * 1024 * 1024


def _cparams(sem=None):
    return pltpu.CompilerParams(dimension_semantics=sem, vmem_limit_bytes=VMEM_LIMIT)


_DIMS = {"nn": (((1,), (0,)), ((), ())), "nt": (((1,), (1,)), ((), ())), "tn": (((0,), (0,)), ((), ()))}


MM_TILE_CAP = 1408


def _mm_tile(n):
    return max(t for t in range(128, min(n, MM_TILE_CAP) + 1, 128) if n % t == 0)


def mm(a, b, *, mode, out_dtype, name, acc_in=None, b_shards=False, out_shards=0):
    b_rows, b_cols = (b.shape[1], b.shape[0] * b.shape[2]) if b_shards else b.shape
    if mode == "nn":
        (M, K), (K2, N) = a.shape, (b_rows, b_cols)
    elif mode == "nt":
        (M, K), (N, K2) = a.shape, (b_rows, b_cols)
    else:
        (K, M), (K2, N) = a.shape, (b_rows, b_cols)
    assert K == K2, (a.shape, b.shape, mode)
    tm, tn, tk = _mm_tile(M), _mm_tile(N), _mm_tile(K)
    nk = K // tk

    def body(*refs):
        if acc_in is None:
            a_ref, b_ref, o_ref, acc_ref = refs
        else:
            a_ref, b_ref, c_ref, o_ref, acc_ref = refs
        k = pl.program_id(2)

        @pl.when(k == 0)
        def _():
            if acc_in is None:
                acc_ref[...] = jnp.zeros_like(acc_ref)
            else:
                acc_ref[...] = c_ref[...]

        acc_ref[...] += lax.dot_general(a_ref[...].astype(BF16), b_ref[...].astype(BF16), _DIMS[mode],
                                        preferred_element_type=F32)

        @pl.when(k == nk - 1)
        def _():
            o_ref[...] = acc_ref[...].astype(out_dtype)

    a_spec = pl.BlockSpec((tk, tm), lambda i, j, k: (k, i)) if mode == "tn" else pl.BlockSpec((tm, tk), lambda i, j, k: (i, k))
    if b_shards:
        assert (tk if mode == "nt" else tn) == b.shape[2] and mode != "tn", (b.shape, tn, tk, mode)
        b_spec = (pl.BlockSpec((None, tn, tk), lambda i, j, k: (k, j, 0)) if mode == "nt"
                  else pl.BlockSpec((None, tk, tn), lambda i, j, k: (j, k, 0)))
    else:
        b_spec = pl.BlockSpec((tn, tk), lambda i, j, k: (j, k)) if mode == "nt" else pl.BlockSpec((tk, tn), lambda i, j, k: (k, j))
    o_spec = pl.BlockSpec((tm, tn), lambda i, j, k: (i, j))
    out_shape = jax.ShapeDtypeStruct((M, N), out_dtype)
    if out_shards:
        assert N == out_shards * tn and acc_in is None, (N, tn, out_shards)
        o_spec = pl.BlockSpec((None, tm, tn), lambda i, j, k: (j, i, 0))
        out_shape = jax.ShapeDtypeStruct((out_shards, M, tn), out_dtype)
    ins, in_specs, aliases = [a, b], [a_spec, b_spec], {}
    if acc_in is not None:
        assert acc_in.shape == (M, N) and acc_in.dtype == F32 and out_dtype == F32
        ins.append(acc_in)
        in_specs.append(o_spec)
        aliases = {2: 0}
    return pl.pallas_call(
        body, name=name, grid=(M // tm, N // tn, nk), in_specs=in_specs, out_specs=o_spec,
        out_shape=out_shape, scratch_shapes=[pltpu.VMEM((tm, tn), F32)],
        input_output_aliases=aliases, compiler_params=_cparams(("parallel", "parallel", "arbitrary")),
    )(*ins)


def rowcall(body, *, name, S, ts, ins, outs, scratch=()):
    assert S % ts == 0 and ts % 16 == 0
    nsteps = S // ts
    in_specs, arrays = [], []
    for arr, kind in ins:
        arrays.append(arr)
        if kind == "row":
            in_specs.append(pl.BlockSpec((ts, arr.shape[1]), lambda i: (i, 0)))
        elif kind in ("prev", "next"):
            hr = 8 * (4 // arr.dtype.itemsize)
            per, last = ts // hr, S // hr - 1
            if kind == "prev":
                in_specs.append(pl.BlockSpec((hr, arr.shape[1]), lambda i, per=per: (jnp.maximum(i * per - 1, 0), 0)))
            else:
                in_specs.append(pl.BlockSpec((hr, arr.shape[1]), lambda i, per=per, last=last: (jnp.minimum((i + 1) * per, last), 0)))
        else:
            nd = arr.ndim
            in_specs.append(pl.BlockSpec(arr.shape, lambda i, nd=nd: (0,) * nd))
    out_specs, out_shapes, acc_idx = [], [], []
    for n, (shape, dtype, kind) in enumerate(outs):
        out_shapes.append(jax.ShapeDtypeStruct(shape, dtype))
        if kind == "row":
            out_specs.append(pl.BlockSpec((ts, shape[1]), lambda i: (i, 0)))
        else:
            nd = len(shape)
            out_specs.append(pl.BlockSpec(shape, lambda i, nd=nd: (0,) * nd))
            acc_idx.append(n)
    n_in = len(arrays)

    def wrapped(*refs):
        @pl.when(pl.program_id(0) == 0)
        def _():
            for n in acc_idx:
                refs[n_in + n][...] = jnp.zeros_like(refs[n_in + n])

        body(*refs)

    res = pl.pallas_call(
        wrapped, name=name, grid=(nsteps,), in_specs=in_specs, out_specs=out_specs, out_shape=out_shapes,
        scratch_shapes=list(scratch), compiler_params=_cparams(("arbitrary",) if acc_idx else ("parallel",)),
    )(*arrays)
    return res


def _halo_prev(ref):
    v = ref[...].astype(F32)
    return v[v.shape[0] - 8:]


def _halo_next(ref):
    return ref[...].astype(F32)[:8]


def _shift_down(cur, prev8, k):
    if k == 0:
        return cur
    rolled = pltpu.roll(cur, k, axis=0)
    fix = pltpu.roll(prev8, k, axis=0)
    row = lax.broadcasted_iota(jnp.int32, (8, 1), 0)
    top = jnp.where(row < k, fix, rolled[0:8])
    if cur.shape[0] == 8:
        return top
    return jnp.concatenate([top, rolled[8:]], axis=0)


def _shift_up(cur, next8, k):
    if k == 0:
        return cur
    n = cur.shape[0]
    rolled = pltpu.roll(cur, n - k, axis=0)
    fix = pltpu.roll(next8, 8 - k, axis=0)
    row = lax.broadcasted_iota(jnp.int32, (8, 1), 0)
    bot = jnp.where(row >= 8 - k, fix, rolled[n - 8:n])
    return jnp.concatenate([rolled[:n - 8], bot], axis=0)


def _sigmoid(x):
    return 1.0 / (1.0 + jnp.exp(-x))


def _silu(x):
    return x * _sigmoid(x)


def _silu_and_grad(x):
    s = _sigmoid(x)
    return x * s, s * (1.0 + x * (1.0 - s))


def _softplus(x):
    return jnp.maximum(x, 0.0) + jnp.log1p(jnp.exp(-jnp.abs(x)))


def _split2(x):
    hi = x.astype(BF16)
    return hi, (x - hi.astype(F32)).astype(BF16)


def _dot1(a, b, mode):
    return lax.dot_general(a.astype(BF16), b.astype(BF16), _DIMS[mode], preferred_element_type=F32)


def _dot3(a, b, mode):
    ah, al = _split2(a)
    bh, bl = _split2(b)
    d = lambda p, q: lax.dot_general(p, q, _DIMS[mode], preferred_element_type=F32)
    return d(ah, bh) + (d(ah, bl) + d(al, bh))


def ada_fwd(c_all, w_sh, b_sh):
    n = w_sh.shape[1]
    tn = 512

    def body(c_ref, w_ref, b_ref, o_ref):
        o_ref[...] = _dot1(_silu(c_ref[...]), w_ref[...], "nn") + b_ref[...]

    return pl.pallas_call(
        body, name="ada_fwd", grid=(n // tn,),
        in_specs=[pl.BlockSpec((N_DEV, D_MODEL), lambda j: (0, 0)), pl.BlockSpec((D_MODEL, tn), lambda j: (0, j)),
                  pl.BlockSpec((1, tn), lambda j: (0, j))],
        out_specs=pl.BlockSpec((N_DEV, tn), lambda j: (0, j)), out_shape=jax.ShapeDtypeStruct((N_DEV, n), F32),
        compiler_params=_cparams(("parallel",)),
    )(c_all, w_sh, b_sh)


def ada_bwd(c_all, dmod_sh):
    n = dmod_sh.shape[1]
    tn = 512

    def body(c_ref, d_ref, o_ref):
        o_ref[...] = _dot1(_silu(c_ref[...]), d_ref[...], "tn")

    return pl.pallas_call(
        body, name="ada_bwd", grid=(n // tn,),
        in_specs=[pl.BlockSpec((N_DEV, D_MODEL), lambda j: (0, 0)), pl.BlockSpec((N_DEV, tn), lambda j: (0, j))],
        out_specs=pl.BlockSpec((D_MODEL, tn), lambda j: (0, j)), out_shape=jax.ShapeDtypeStruct((D_MODEL, n), F32),
        compiler_params=_cparams(("parallel",)),
    )(c_all, dmod_sh)


SHIFT_T, SCALE_T, GATE_T, SHIFT_F, SCALE_F, GATE_F = range(6)


def modulate(x, mod, shift_row, scale_row, name):
    S = x.shape[0]

    def body(x_ref, m_ref, o_ref):
        m = m_ref[...]
        o_ref[...] = (x_ref[...] * (1.0 + m[scale_row:scale_row + 1]) + m[shift_row:shift_row + 1]).astype(BF16)

    return rowcall(body, name=name, S=S, ts=512, ins=[(x, "row"), (mod, "vec")], outs=[((S, D_MODEL), BF16, "row")])[0]


def _conv_fwd(cur, prev, w, width):
    y = cur * w[width - 1:width]
    for j in range(width - 1):
        y = y + _shift_down(cur, prev, width - 1 - j) * w[j:j + 1]
    return y


def _prep_a_core(cur, prev, w):
    return _silu_and_grad(_conv_fwd(cur, prev, w, A_CONV))


def prep_a_fwd(qkv_raw, ba, conv_a, a_log, dt_bias):
    S = qkv_raw.shape[0]

    def body(x_ref, xp_ref, ba_ref, w_ref, al_ref, dt_ref, q_ref, k_ref, v_ref, beta_ref, g_ref):
        first = (pl.program_id(0) > 0).astype(F32)
        y, _ = _prep_a_core(x_ref[...].astype(F32), _halo_prev(xp_ref) * first, w_ref[...])
        for h in range(A_HEADS):
            sl = slice(h * A_DK, (h + 1) * A_DK)
            qh = y[:, sl]
            kh = y[:, A_W + h * A_DK:A_W + (h + 1) * A_DK]
            q_ref[:, sl] = qh * (lax.rsqrt(jnp.sum(qh * qh, axis=-1, keepdims=True) + L2_EPS) * (A_DK ** -0.5))
            k_ref[:, sl] = kh * lax.rsqrt(jnp.sum(kh * kh, axis=-1, keepdims=True) + L2_EPS)
        v_ref[...] = y[:, 2 * A_W:3 * A_W]
        bav = ba_ref[...]
        beta_ref[...] = _sigmoid(bav[:, 0:A_HEADS])
        g_ref[...] = -jnp.exp(al_ref[...]) * _softplus(bav[:, A_HEADS:2 * A_HEADS] + dt_ref[...])

    return rowcall(
        body, name="prep_a_fwd", S=S, ts=256,
        ins=[(qkv_raw, "row"), (qkv_raw, "prev"), (ba, "row"), (conv_a, "vec"), (a_log, "vec"), (dt_bias, "vec")],
        outs=[((S, A_W), F32, "row")] * 3 + [((S, A_HEADS), F32, "row")] * 2)


HEAD_GROUP = 4
GROUP_ROWS = HEAD_GROUP * CHUNK
N_HEAD_GROUPS = A_HEADS // HEAD_GROUP
LOG_CHUNK = int(math.log2(CHUNK))


def _tri_masks():
    rb = lax.broadcasted_iota(jnp.int32, (GROUP_ROWS, GROUP_ROWS), 0)
    cb = lax.broadcasted_iota(jnp.int32, (GROUP_ROWS, GROUP_ROWS), 1)
    same = (rb >> LOG_CHUNK) == (cb >> LOG_CHUNK)
    return dict(causal=same & (rb >= cb), strict=same & (rb > cb), eye=rb == cb, upper=same & (cb >= rb),
                last=cb == (rb | (CHUNK - 1)), rb=rb, cb=cb)


def _col_to_row(colv, eye):
    return jnp.sum(jnp.where(eye, colv, 0.0), axis=0, keepdims=True)


def _row_to_col(rowv, eye):
    return jnp.sum(jnp.where(eye, rowv, 0.0), axis=1, keepdims=True)


def _tri_inv(a_list, mk):
    rb, cb = mk["rb"], mk["cb"]
    ts = [jnp.where(mk["eye"], 1.0, 0.0) - jnp.where((rb >> 1) == (cb >> 1), a, 0.0) for a in a_list]
    for lvl in range(1, LOG_CHUNK):
        rs, cs = rb >> lvl, cb >> lvl
        sel = ((rs & 1) == 1) & (cs == rs - 1)
        inner = [_dot3(t, jnp.where(sel, a, 0.0), "nn") for t, a in zip(ts, a_list)]
        ts = [t - _dot3(i, t, "nn") for i, t in zip(inner, ts)]
    return ts


def _stack_heads(ref, grp):
    return jnp.concatenate([ref[:, (grp * HEAD_GROUP + j) * A_DK:(grp * HEAD_GROUP + j + 1) * A_DK]
                            for j in range(HEAD_GROUP)], axis=0)


def _stack_cols(tile, grp):
    return jnp.concatenate([tile[:, grp * HEAD_GROUP + j:grp * HEAD_GROUP + j + 1] for j in range(HEAD_GROUP)], axis=0)


def _delta_local(q, k, v, beta, g, mk):
    causal, strict, eye = mk["causal"], mk["strict"], mk["eye"]
    g_row = _col_to_row(g, eye)
    gc = jnp.sum(jnp.where(causal, g_row, 0.0), axis=1, keepdims=True)
    gc_row = _col_to_row(gc, eye)
    decay = jnp.where(causal, jnp.exp(jnp.where(causal, gc - gc_row, 0.0)), 0.0)
    gam = jnp.exp(gc)
    kb = k * beta
    vb = v * beta
    y = kb * gam
    a = jnp.where(strict, _dot1(kb, k, "nt") * decay, 0.0)
    p = _dot1(q, k, "nt") * decay
    gl = jnp.sum(jnp.where(mk["last"], gc_row, 0.0), axis=1, keepdims=True)
    kd = k * jnp.exp(gl - gc)
    return dict(gc=gc, decay=decay, gam=gam, kb=kb, vb=vb, y=y, a=a, p=p, gl=gl, kd=kd)


def _head_rows(x, j):
    return x[j * CHUNK:(j + 1) * CHUNK]


def delta_fwd(q, k, v, beta, g):
    S = q.shape[0]
    n_chunks = S // CHUNK

    def body(q_ref, k_ref, v_ref, beta_ref, g_ref, o_ref, sprev_ref, t_ref, state_ref):
        @pl.when(pl.program_id(0) == 0)
        def _():
            state_ref[...] = jnp.zeros_like(state_ref)

        mk = _tri_masks()
        betav, gv = beta_ref[...], g_ref[...]
        groups = range(N_HEAD_GROUPS)
        q_all = [_stack_heads(q_ref, grp) for grp in groups]
        locs = [_delta_local(q_all[grp], _stack_heads(k_ref, grp), _stack_heads(v_ref, grp),
                             _stack_cols(betav, grp), _stack_cols(gv, grp), mk) for grp in groups]
        tinvs = _tri_inv([loc["a"] for loc in locs], mk)
        uws = [_dot3(tinvs[grp], jnp.concatenate([locs[grp]["vb"], locs[grp]["y"]], axis=1), "nn") for grp in groups]
        for grp in groups:
            loc, uw = locs[grp], uws[grp]
            t_ref[0, grp] = tinvs[grp]
            qg = q_all[grp] * loc["gam"]
            egl = jnp.exp(loc["gl"])
            vns, o_state = [], []
            for j in range(HEAD_GROUP):
                h = grp * HEAD_GROUP + j
                s0 = state_ref[h]
                sprev_ref[0, h] = s0
                uw_h = _head_rows(uw, j)
                vn = uw_h[:, :A_DK] - _dot1(uw_h[:, A_DK:], s0, "nn")
                vns.append(vn)
                o_state.append(_dot1(_head_rows(qg, j), s0, "nn"))
                state_ref[h] = s0 * egl[(j + 1) * CHUNK - 1:(j + 1) * CHUNK] + _dot1(_head_rows(loc["kd"], j), vn, "tn")
            o_local = _dot1(loc["p"], jnp.concatenate(vns, axis=0), "nn")
            for j in range(HEAD_GROUP):
                h = grp * HEAD_GROUP + j
                o_ref[:, h * A_DK:(h + 1) * A_DK] = o_state[j] + _head_rows(o_local, j)

    tile = pl.BlockSpec((CHUNK, A_W), lambda n: (n, 0))
    small = pl.BlockSpec((CHUNK, A_HEADS), lambda n: (n, 0))
    return pl.pallas_call(
        body, name="delta_fwd", grid=(n_chunks,), in_specs=[tile, tile, tile, small, small],
        out_specs=[tile, pl.BlockSpec((1, A_HEADS, A_DK, A_DK), lambda n: (n, 0, 0, 0)),
                   pl.BlockSpec((1, N_HEAD_GROUPS, GROUP_ROWS, GROUP_ROWS), lambda n: (n, 0, 0, 0))],
        out_shape=[jax.ShapeDtypeStruct((S, A_W), F32), jax.ShapeDtypeStruct((n_chunks, A_HEADS, A_DK, A_DK), F32),
                   jax.ShapeDtypeStruct((n_chunks, N_HEAD_GROUPS, GROUP_ROWS, GROUP_ROWS), F32)],
        scratch_shapes=[pltpu.VMEM((A_HEADS, A_DK, A_DK), F32)],
        compiler_params=_cparams(("arbitrary",)),
    )(q, k, v, beta, g)


def gate_a_fwd(o_pre, z, norm_w):
    S = o_pre.shape[0]

    def body(o_ref, z_ref, nw_ref, out_ref):
        nw = nw_ref[...]
        for h in range(A_HEADS):
            sl = slice(h * A_DK, (h + 1) * A_DK)
            oh = o_ref[:, sl]
            r = lax.rsqrt(jnp.mean(oh * oh, axis=-1, keepdims=True) + RMS_EPS)
            out_ref[:, sl] = (oh * r * nw * _silu(z_ref[:, sl].astype(F32))).astype(BF16)

    return rowcall(body, name="gate_a_fwd", S=S, ts=512, ins=[(o_pre, "row"), (z, "row"), (norm_w, "vec")],
                   outs=[((S, A_W), BF16, "row")])[0]


HEADS_PER_GROUP = 2
GROUP_W = HEADS_PER_GROUP * B_DH
N_GROUPS = B_HEADS // HEADS_PER_GROUP
PAD_ROWS = B_PREV * CHUNK


Q_TILE = 256
Q_CHUNKS = Q_TILE // CHUNK
KEY_WIN = (B_PREV + Q_CHUNKS) * CHUNK


def _band_probs(qh, kh, bias, valid):
    s = _dot1(qh, kh, "nt") * (B_DH ** -0.5) + bias
    s = jnp.where(valid, s, NEG_INF)
    e = jnp.exp(s - jnp.max(s, axis=-1, keepdims=True))
    return e * (1.0 / jnp.sum(e, axis=-1, keepdims=True))


def _attn_specs(S, tile_rows):
    n_cb = B_W // GROUP_W
    return [pl.BlockSpec((tile_rows, GROUP_W), lambda g, n: (n + PAD_ROWS // tile_rows, g)),
            pl.BlockSpec((PAD_ROWS + S, GROUP_W), lambda g, n: (0, n_cb + g)),
            pl.BlockSpec((PAD_ROWS + S, GROUP_W), lambda g, n: (0, 2 * n_cb + g)),
            pl.BlockSpec((HEADS_PER_GROUP, CHUNK, B_BAND), lambda g, n: (g, 0, 0))]


def _band_valid(first_chunk):
    return lax.broadcasted_iota(jnp.int32, (CHUNK, B_BAND), 1) >= PAD_ROWS - first_chunk * CHUNK


def _chunk_rows(x, qc, rows=CHUNK):
    return x[qc * CHUNK:qc * CHUNK + rows]


FWD_TILE = 512
FWD_CHUNKS = FWD_TILE // CHUNK
FWD_WIN = (B_PREV + FWD_CHUNKS) * CHUNK


def attn_fwd(qkv_pad, bias):
    S = qkv_pad.shape[0] - PAD_ROWS

    def body(q_ref, k_ref, v_ref, b_ref, o_ref):
        n = pl.program_id(1)
        start = pl.multiple_of(n * FWD_TILE, FWD_TILE)
        kwin = k_ref[pl.ds(start, FWD_WIN), :]
        vwin = v_ref[pl.ds(start, FWD_WIN), :]
        qv = q_ref[...]
        pairs = [(qc, hh) for qc in range(FWD_CHUNKS) for hh in range(HEADS_PER_GROUP)]
        sl = lambda hh: slice(hh * B_DH, (hh + 1) * B_DH)
        s = [_dot1(_chunk_rows(qv, qc)[:, sl(hh)], _chunk_rows(kwin, qc, B_BAND)[:, sl(hh)], "nt") for qc, hh in pairs]
        s = [jnp.where(_band_valid(n * FWD_CHUNKS + qc), x * (B_DH ** -0.5) + b_ref[hh], NEG_INF)
             for x, (qc, hh) in zip(s, pairs)]
        e = [jnp.exp(x - jnp.max(x, axis=-1, keepdims=True)) for x in s]
        p = [x * (1.0 / jnp.sum(x, axis=-1, keepdims=True)) for x in e]
        o = [_dot1(x, _chunk_rows(vwin, qc, B_BAND)[:, sl(hh)], "nn") for x, (qc, hh) in zip(p, pairs)]
        rows = [jnp.concatenate(o[qc * HEADS_PER_GROUP:(qc + 1) * HEADS_PER_GROUP], axis=1) for qc in range(FWD_CHUNKS)]
        o_ref[...] = jnp.concatenate(rows, axis=0).astype(BF16)

    return pl.pallas_call(
        body, name="attn_fwd", grid=(N_GROUPS, S // FWD_TILE), in_specs=_attn_specs(S, FWD_TILE),
        out_specs=pl.BlockSpec((FWD_TILE, GROUP_W), lambda g, n: (n, g)),
        out_shape=jax.ShapeDtypeStruct((S, B_W), BF16),
        compiler_params=_cparams(("parallel", "arbitrary")),
    )(qkv_pad, qkv_pad, qkv_pad, bias)


EXT = B_BAND + CHUNK


def bias_expand(rel_bias):
    def body(rev_ref, o_ref):
        rev = rev_ref[...]
        erev = jnp.concatenate([jnp.broadcast_to(rev[:, 0:1], (B_HEADS, EXT - B_REL)), rev], axis=1)
        for i in range(CHUNK):
            o_ref[i] = erev[:, CHUNK - i:CHUNK - i + B_BAND]

    return pl.pallas_call(
        body, name="bias_expand", in_specs=[WHOLE_VMEM], out_specs=WHOLE_VMEM,
        out_shape=jax.ShapeDtypeStruct((CHUNK, B_HEADS, B_BAND), F32),
    )(jnp.flip(rel_bias, axis=1))


def bias_reduce(dbias):
    def body(d_ref, o_ref):
        acc = jnp.zeros((B_HEADS, EXT), F32)
        for i in range(CHUNK):
            acc = acc + jnp.pad(d_ref[i], ((0, 0), (CHUNK - i, i)))
        tail = acc[:, EXT - B_REL:]
        clipped = jnp.sum(acc[:, :EXT - B_REL], axis=1, keepdims=True)
        lane = lax.broadcasted_iota(jnp.int32, (B_HEADS, B_REL), 1)
        o_ref[...] = jnp.where(lane == 0, tail + clipped, tail)

    rev = pl.pallas_call(body, name="bias_reduce", in_specs=[WHOLE_VMEM], out_specs=WHOLE_VMEM,
                         out_shape=jax.ShapeDtypeStruct((B_HEADS, B_REL), F32))(dbias)
    return jnp.flip(rev, axis=1)


def merge_fwd(gates_raw, b_gate, ya, yb):
    S = ya.shape[0]

    def body(g_ref, b_ref, ya_ref, yb_ref, o_ref):
        gt = _sigmoid(g_ref[...].astype(F32) + b_ref[...])
        o_ref[...] = (gt[:, :D_MODEL] * ya_ref[...].astype(F32) + gt[:, D_MODEL:] * yb_ref[...].astype(F32)).astype(BF16)

    return rowcall(body, name="merge_fwd", S=S, ts=512,
                   ins=[(gates_raw, "row"), (b_gate, "vec"), (ya, "row"), (yb, "row")],
                   outs=[((S, D_MODEL), BF16, "row")])[0]


def _ln_stats(xpre):
    mu = jnp.mean(xpre, axis=-1, keepdims=True)
    xc = xpre - mu
    rstd = lax.rsqrt(jnp.mean(xc * xc, axis=-1, keepdims=True) + LN_EPS)
    return xc * rstd, rstd


def ln1_fwd(x, mix, mod, ln_g, ln_b):
    S = x.shape[0]

    def body(x_ref, mix_ref, m_ref, g_ref, b_ref, xpre_ref, x1_ref, h2_ref):
        m = m_ref[...]
        xpre = ALPHA * x_ref[...] + m[GATE_T:GATE_T + 1] * mix_ref[...]
        xhat, _ = _ln_stats(xpre)
        x1 = xhat * g_ref[...] + b_ref[...]
        xpre_ref[...] = xpre
        x1_ref[...] = x1
        h2_ref[...] = (x1 * (1.0 + m[SCALE_F:SCALE_F + 1]) + m[SHIFT_F:SHIFT_F + 1]).astype(BF16)

    return rowcall(body, name="ln1_fwd", S=S, ts=512,
                   ins=[(x, "row"), (mix, "row"), (mod, "vec"), (ln_g, "vec"), (ln_b, "vec")],
                   outs=[((S, D_MODEL), F32, "row"), ((S, D_MODEL), F32, "row"), ((S, D_MODEL), BF16, "row")])


STRIP_ROWS = 32
STRIP_COLS = 256


def ffn_act_fwd(up, conv_w, conv_b):
    S = up.shape[0]
    ts = 256

    def body(u_ref, up_ref, w_ref, b_ref, o_ref, ubuf):
        ubuf[0:8] = _halo_prev(up_ref) * (pl.program_id(0) > 0).astype(F32)
        ubuf[8:8 + ts] = u_ref[...].astype(F32)

        def col_block(j, carry):
            gate = pl.ds(pl.multiple_of(j * STRIP_COLS, STRIP_COLS), STRIP_COLS)
            halves = [gate, pl.ds(pl.multiple_of(D_FF + j * STRIP_COLS, STRIP_COLS), STRIP_COLS)]
            w = [w_ref[:, c] for c in halves]
            bias = [b_ref[:, c] for c in halves]
            for r0 in range(0, ts, STRIP_ROWS):
                uc = []
                for h in range(2):
                    x = ubuf[r0:r0 + STRIP_ROWS + 8, halves[h]]
                    uc.append(bias[h] + sum(
                        w[h][t:t + 1] * (x if t == FFN_CONV - 1 else pltpu.roll(x, FFN_CONV - 1 - t, axis=0))[8:]
                        for t in range(FFN_CONV)))
                o_ref[r0:r0 + STRIP_ROWS, gate] = (_silu(uc[0]) * uc[1]).astype(BF16)
            return carry

        lax.fori_loop(0, D_FF // STRIP_COLS, col_block, 0)

    return rowcall(body, name="ffn_act_fwd", S=S, ts=ts,
                   ins=[(up, "row"), (up, "prev"), (conv_w, "vec"), (conv_b, "vec")],
                   outs=[((S, D_FF), BF16, "row")], scratch=[pltpu.VMEM((ts + 8, 2 * D_FF), F32)])[0]


def final_fwd_bwd(x1, ffn, target, mod, ln_g, ln_b):
    S = x1.shape[0]

    def body(x1_ref, f_ref, t_ref, m_ref, g_ref, b_ref, dxpre_ref, dffn_ref, loss_ref, dgate_ref, dg_ref, db_ref):
        gate = m_ref[...][GATE_F:GATE_F + 1]
        ffn_v = f_ref[...]
        xpre = ALPHA * x1_ref[...] + gate * ffn_v
        xhat, rstd = _ln_stats(xpre)
        err = xhat * g_ref[...] + b_ref[...] - t_ref[...]
        loss_ref[...] += 0.5 * jnp.sum(jnp.mean(err * err, axis=-1, keepdims=True), axis=0, keepdims=True)
        dy = err * (1.0 / D_MODEL)
        dg_ref[...] += jnp.sum(dy * xhat, axis=0, keepdims=True)
        db_ref[...] += jnp.sum(dy, axis=0, keepdims=True)
        dyg = dy * g_ref[...]
        dxpre = rstd * (dyg - jnp.mean(dyg, axis=-1, keepdims=True) - xhat * jnp.mean(dyg * xhat, axis=-1, keepdims=True))
        dxpre_ref[...] = dxpre
        dffn_ref[...] = (gate * dxpre).astype(BF16)
        dgate_ref[...] += jnp.sum(dxpre * ffn_v, axis=0, keepdims=True)

    vec = ((1, D_MODEL), F32, "acc")
    return rowcall(body, name="final_fwd_bwd", S=S, ts=512,
                   ins=[(x1, "row"), (ffn, "row"), (target, "row"), (mod, "vec"), (ln_g, "vec"), (ln_b, "vec")],
                   outs=[((S, D_MODEL), F32, "row"), ((S, D_MODEL), BF16, "row"), ((1, 1), F32, "acc"), vec, vec, vec])


def ffn_act_bwd(dact, up, conv_w, conv_b):
    S = up.shape[0]
    ts = 256
    win_u, win_d = STRIP_ROWS + 16, STRIP_ROWS + 8

    def body(d_ref, dn_ref, u_ref, up_ref, un_ref, w_ref, b_ref, dup_ref, dw_ref, db_ref, ubuf, dbuf):
        i = pl.program_id(0)
        ubuf[0:8] = _halo_prev(up_ref) * (i > 0).astype(F32)
        ubuf[8:8 + ts] = u_ref[...].astype(F32)
        ubuf[8 + ts:16 + ts] = _halo_next(un_ref)
        dbuf[0:ts] = d_ref[...].astype(F32)
        dbuf[ts:ts + 8] = _halo_next(dn_ref) * (i < pl.num_programs(0) - 1).astype(F32)

        def col_block(j, carry):
            halves = [pl.ds(pl.multiple_of(j * STRIP_COLS, STRIP_COLS), STRIP_COLS),
                      pl.ds(pl.multiple_of(D_FF + j * STRIP_COLS, STRIP_COLS), STRIP_COLS)]
            w = [w_ref[:, c] for c in halves]
            bias = [b_ref[:, c] for c in halves]
            dw_acc = [[jnp.zeros((1, STRIP_COLS), F32) for _ in range(FFN_CONV)] for _ in halves]
            db_acc = [jnp.zeros((1, STRIP_COLS), F32) for _ in halves]
            for r0 in range(0, ts, STRIP_ROWS):
                shifted = [[x if k == 0 else pltpu.roll(x, k, axis=0) for k in range(FFN_CONV)]
                           for x in (ubuf[r0:r0 + win_u, c] for c in halves)]
                uc = [bias[h] + sum(w[h][t:t + 1] * shifted[h][FFN_CONV - 1 - t][8:8 + win_d] for t in range(FFN_CONV))
                      for h in range(2)]
                dact_w = dbuf[r0:r0 + win_d, halves[0]]
                sg, dsg = _silu_and_grad(uc[0])
                duc = [dact_w * uc[1] * dsg, dact_w * sg]
                for h in range(2):
                    dup = duc[h] * w[h][FFN_CONV - 1:FFN_CONV]
                    for t in range(FFN_CONV - 1):
                        dup = dup + pltpu.roll(duc[h], win_d - (FFN_CONV - 1 - t), axis=0) * w[h][t:t + 1]
                    dup_ref[r0:r0 + STRIP_ROWS, halves[h]] = dup[:STRIP_ROWS].astype(BF16)
                    mine = duc[h][:STRIP_ROWS]
                    db_acc[h] = db_acc[h] + jnp.sum(mine, axis=0, keepdims=True)
                    for t in range(FFN_CONV):
                        dw_acc[h][t] = dw_acc[h][t] + jnp.sum(
                            mine * shifted[h][FFN_CONV - 1 - t][8:8 + STRIP_ROWS], axis=0, keepdims=True)
            for h in range(2):
                dw_ref[:, halves[h]] += jnp.concatenate(dw_acc[h], axis=0)
                db_ref[:, halves[h]] += db_acc[h]
            return carry

        lax.fori_loop(0, D_FF // STRIP_COLS, col_block, 0)

    return rowcall(body, name="ffn_act_bwd", S=S, ts=ts,
                   ins=[(dact, "row"), (dact, "next"), (up, "row"), (up, "prev"), (up, "next"), (conv_w, "vec"), (conv_b, "vec")],
                   outs=[((S, 2 * D_FF), BF16, "row"), ((FFN_CONV, 2 * D_FF), F32, "acc"), ((1, 2 * D_FF), F32, "acc")],
                   scratch=[pltpu.VMEM((ts + 16, 2 * D_FF), F32), pltpu.VMEM((ts + 8, D_FF), F32)])


def ln1_bwd(dxpre2, dh2, xpre1, mix, mod, ln_g, ln_b):
    S = xpre1.shape[0]

    def body(d2_ref, dh_ref, xp_ref, mix_ref, m_ref, g_ref, b_ref, dxpre_ref, dmix_ref,
             dscale_ref, dshift_ref, dgate_ref, dg_ref, db_ref):
        m = m_ref[...]
        xhat, rstd = _ln_stats(xp_ref[...])
        x1 = xhat * g_ref[...] + b_ref[...]
        dh = dh_ref[...]
        dx1 = ALPHA * d2_ref[...] + dh * (1.0 + m[SCALE_F:SCALE_F + 1])
        dscale_ref[...] += jnp.sum(dh * x1, axis=0, keepdims=True)
        dshift_ref[...] += jnp.sum(dh, axis=0, keepdims=True)
        dg_ref[...] += jnp.sum(dx1 * xhat, axis=0, keepdims=True)
        db_ref[...] += jnp.sum(dx1, axis=0, keepdims=True)
        dyg = dx1 * g_ref[...]
        dxpre = rstd * (dyg - jnp.mean(dyg, axis=-1, keepdims=True) - xhat * jnp.mean(dyg * xhat, axis=-1, keepdims=True))
        dxpre_ref[...] = dxpre
        dmix_ref[...] = (m[GATE_T:GATE_T + 1] * dxpre).astype(BF16)
        dgate_ref[...] += jnp.sum(dxpre * mix_ref[...], axis=0, keepdims=True)

    vec = ((1, D_MODEL), F32, "acc")
    return rowcall(body, name="ln1_bwd", S=S, ts=512,
                   ins=[(dxpre2, "row"), (dh2, "row"), (xpre1, "row"), (mix, "row"), (mod, "vec"), (ln_g, "vec"), (ln_b, "vec")],
                   outs=[((S, D_MODEL), F32, "row"), ((S, D_MODEL), BF16, "row"), vec, vec, vec, vec, vec])


def merge_bwd(dmerged, gates_raw, b_gate, ya, yb):
    S = ya.shape[0]

    def body(d_ref, g_ref, b_ref, ya_ref, yb_ref, dya_ref, dyb_ref, dg_ref, dbg_ref):
        gt = _sigmoid(g_ref[...].astype(F32) + b_ref[...])
        d = d_ref[...].astype(F32)
        ga, gb = gt[:, :D_MODEL], gt[:, D_MODEL:]
        dya_ref[...] = (d * ga).astype(BF16)
        dyb_ref[...] = (d * gb).astype(BF16)
        dgr = jnp.concatenate([d * ya_ref[...].astype(F32) * ga * (1.0 - ga),
                               d * yb_ref[...].astype(F32) * gb * (1.0 - gb)], axis=1)
        dg_ref[...] = dgr.astype(BF16)
        dbg_ref[...] += jnp.sum(dgr, axis=0, keepdims=True)

    return rowcall(body, name="merge_bwd", S=S, ts=512,
                   ins=[(dmerged, "row"), (gates_raw, "row"), (b_gate, "vec"), (ya, "row"), (yb, "row")],
                   outs=[((S, D_MODEL), BF16, "row"), ((S, D_MODEL), BF16, "row"), ((S, 2 * D_MODEL), BF16, "row"),
                         ((1, 2 * D_MODEL), F32, "acc")])


def attn_bwd(qkv_pad, bias, do_b):
    S = qkv_pad.shape[0] - PAD_ROWS

    def body(q_ref, k_ref, v_ref, bias_ref, do_ref, dq_ref, dk_ref, dv_ref, db_ref, b_ref):
        n = pl.program_id(1)

        @pl.when(n == 0)
        def _():
            dk_ref[...] = jnp.zeros_like(dk_ref)
            dv_ref[...] = jnp.zeros_like(dv_ref)
            db_ref[...] = jnp.zeros_like(db_ref)
            b_ref[...] = jnp.full(b_ref.shape, NEG_INF, F32)
            for hh in range(HEADS_PER_GROUP):
                for qc in range(Q_CHUNKS):
                    b_ref[hh, qc * CHUNK:(qc + 1) * CHUNK, qc * CHUNK:qc * CHUNK + B_BAND] = bias_ref[hh]

        start = pl.multiple_of(n * Q_TILE, Q_TILE)
        kwin = k_ref[pl.ds(start, KEY_WIN), :]
        vwin = v_ref[pl.ds(start, KEY_WIN), :]
        qv, dov = q_ref[...], do_ref[...]
        valid = lax.broadcasted_iota(jnp.int32, (Q_TILE, KEY_WIN), 1) >= PAD_ROWS - n * Q_TILE
        dqs, dks, dvs = [], [], []
        for hh in range(HEADS_PER_GROUP):
            sl = slice(hh * B_DH, (hh + 1) * B_DH)
            p = _band_probs(qv[:, sl], kwin[:, sl], b_ref[hh], valid)
            dp = _dot1(dov[:, sl], vwin[:, sl], "nt")
            ds = p * (dp - jnp.sum(dp * p, axis=-1, keepdims=True))
            dbh = ds[0:CHUNK, 0:B_BAND]
            for qc in range(1, Q_CHUNKS):
                dbh = dbh + ds[qc * CHUNK:(qc + 1) * CHUNK, qc * CHUNK:qc * CHUNK + B_BAND]
            db_ref[hh] += dbh
            dsq = ds * (B_DH ** -0.5)
            dqs.append(_dot1(dsq, kwin[:, sl], "nn"))
            dks.append(_dot1(dsq, qv[:, sl], "tn"))
            dvs.append(_dot1(p, dov[:, sl], "tn"))
        dq_ref[...] = jnp.concatenate(dqs, axis=1).astype(BF16)
        dk_ref[pl.ds(start, KEY_WIN), :] += jnp.concatenate(dks, axis=1)
        dv_ref[pl.ds(start, KEY_WIN), :] += jnp.concatenate(dvs, axis=1)

    col = pl.BlockSpec((PAD_ROWS + S, GROUP_W), lambda g, n: (0, g))
    tile = pl.BlockSpec((Q_TILE, GROUP_W), lambda g, n: (n, g))
    return pl.pallas_call(
        body, name="attn_bwd", grid=(N_GROUPS, S // Q_TILE), in_specs=_attn_specs(S, Q_TILE) + [tile],
        out_specs=[tile, col, col, pl.BlockSpec((HEADS_PER_GROUP, CHUNK, B_BAND), lambda g, n: (g, 0, 0))],
        out_shape=[jax.ShapeDtypeStruct((S, B_W), BF16), jax.ShapeDtypeStruct((PAD_ROWS + S, B_W), F32),
                   jax.ShapeDtypeStruct((PAD_ROWS + S, B_W), F32), jax.ShapeDtypeStruct((B_HEADS, CHUNK, B_BAND), F32)],
        scratch_shapes=[pltpu.VMEM((HEADS_PER_GROUP, Q_TILE, KEY_WIN), F32)],
        compiler_params=_cparams(("parallel", "arbitrary")),
    )(qkv_pad, qkv_pad, qkv_pad, bias, do_b)


def gate_a_bwd(do_a, o_pre, z, norm_w):
    S = o_pre.shape[0]

    def body(d_ref, o_ref, z_ref, nw_ref, dop_ref, dz_ref, dnw_ref):
        nw = nw_ref[...]
        acc = jnp.zeros((1, A_DK), F32)
        for h in range(A_HEADS):
            sl = slice(h * A_DK, (h + 1) * A_DK)
            oh, zh, dh = o_ref[:, sl], z_ref[:, sl].astype(F32), d_ref[:, sl].astype(F32)
            r = lax.rsqrt(jnp.mean(oh * oh, axis=-1, keepdims=True) + RMS_EPS)
            sz, dsz = _silu_and_grad(zh)
            dz_ref[:, sl] = (dh * oh * r * nw * dsz).astype(BF16)
            acc = acc + jnp.sum(dh * oh * r * sz, axis=0, keepdims=True)
            t = dh * nw * sz
            dop_ref[:, sl] = r * t - oh * (r * r * r) * jnp.mean(t * oh, axis=-1, keepdims=True)
        dnw_ref[...] += acc

    return rowcall(body, name="gate_a_bwd", S=S, ts=512,
                   ins=[(do_a, "row"), (o_pre, "row"), (z, "row"), (norm_w, "vec")],
                   outs=[((S, A_W), F32, "row"), ((S, A_W), BF16, "row"), ((1, A_DK), F32, "acc")])


def delta_bwd(q, k, v, beta, g, sprev, tinv, do):
    S = q.shape[0]
    n_chunks = S // CHUNK

    def body(q_ref, k_ref, v_ref, beta_ref, g_ref, sprev_ref, t_ref, do_ref,
             dq_ref, dk_ref, dv_ref, dbeta_ref, dg_ref, dstate_ref):
        @pl.when(pl.program_id(0) == 0)
        def _():
            dstate_ref[...] = jnp.zeros_like(dstate_ref)

        mk = _tri_masks()
        causal, strict, eye = mk["causal"], mk["strict"], mk["eye"]
        blk_end = (lax.broadcasted_iota(jnp.int32, (GROUP_ROWS, 1), 0) & (CHUNK - 1)) == CHUNK - 1
        lane = lax.broadcasted_iota(jnp.int32, (CHUNK, A_HEADS), 1)
        betav, gv = beta_ref[...], g_ref[...]
        dbeta_t = jnp.zeros((CHUNK, A_HEADS), F32)
        dg_t = jnp.zeros((CHUNK, A_HEADS), F32)
        groups, heads = range(N_HEAD_GROUPS), range(HEAD_GROUP)
        st = [dict() for _ in groups]

        def local_part(grp, s):
            s["qs"], s["ks"], s["vs"] = _stack_heads(q_ref, grp), _stack_heads(k_ref, grp), _stack_heads(v_ref, grp)
            s["dos"] = _stack_heads(do_ref, grp)
            s["bs"] = _stack_cols(betav, grp)
            s["loc"] = loc = _delta_local(s["qs"], s["ks"], s["vs"], s["bs"], _stack_cols(gv, grp), mk)
            s["tinv"] = t_ref[0, grp]
            s["rhs"] = jnp.concatenate([loc["vb"], loc["y"]], axis=1)
            s["uw"] = _dot3(s["tinv"], s["rhs"], "nn")

        def state_part(grp, s):
            loc, uw, dos, qs = s["loc"], s["uw"], s["dos"], s["qs"]
            gam, kd, gl, gc = loc["gam"], loc["kd"], loc["gl"], loc["gc"]
            qg = qs * gam
            egl = jnp.exp(gl)
            hid = [grp * HEAD_GROUP + j for j in heads]
            s0 = [sprev_ref[0, h] for h in hid]
            ds1 = [dstate_ref[h] for h in hid]
            w = [_head_rows(uw, j)[:, A_DK:] for j in heads]
            vn = [_head_rows(uw, j)[:, :A_DK] - _dot1(w[j], s0[j], "nn") for j in heads]
            vns = jnp.concatenate(vn, axis=0)
            dvn_local = _dot1(loc["p"], dos, "tn")
            dvn = [_head_rows(dvn_local, j) + _dot1(_head_rows(kd, j), ds1[j], "nn") for j in heads]
            dvns = jnp.concatenate(dvn, axis=0)
            s["dp"] = jnp.where(causal, _dot1(dos, vns, "nt"), 0.0)
            dqg = jnp.concatenate([_dot1(_head_rows(dos, j), s0[j], "nt") for j in heads], axis=0)
            s["dq"] = dqg * gam
            dgc = jnp.sum(dqg * qg, axis=-1, keepdims=True)
            for j in heads:
                dstate_ref[hid[j]] = (_dot1(_head_rows(qg, j), _head_rows(dos, j), "tn")
                                      + egl[(j + 1) * CHUNK - 1:(j + 1) * CHUNK] * ds1[j] - _dot1(w[j], dvn[j], "tn"))
            dkd = jnp.concatenate([_dot1(vn[j], ds1[j], "nt") for j in heads], axis=0)
            s["dk"] = dkd * jnp.exp(gl - gc)
            t1 = jnp.sum(dkd * kd, axis=-1, keepdims=True)
            dgl = jnp.concatenate(
                [jnp.broadcast_to(jnp.sum(_head_rows(t1, j), axis=0, keepdims=True)
                                  + jnp.sum(jnp.sum(ds1[j] * s0[j], axis=-1, keepdims=True), axis=0, keepdims=True)
                                  * egl[(j + 1) * CHUNK - 1:(j + 1) * CHUNK], (CHUNK, 1)) for j in heads], axis=0)
            s["dgc"] = dgc - t1 + jnp.where(blk_end, dgl, 0.0)
            s["duw"] = jnp.concatenate(
                [dvns, jnp.concatenate([-_dot1(dvn[j], s0[j], "nt") for j in heads], axis=0)], axis=1)

        def solve_part(grp, s):
            s["dvby"] = _dot3(s["tinv"], s["duw"], "tn")
            s["dt"] = _dot3(s["duw"], s["rhs"], "nt")

        def inverse_part_a(grp, s):
            s["tdt"] = _dot3(s["tinv"], s["dt"], "tn")

        def inverse_part_b(grp, s):
            s["da"] = jnp.where(strict, -_dot3(s["tdt"], s["tinv"], "nt"), 0.0)

        def finish(grp, s):
            loc, qs, ks, vs, bs, da, dp, dvby = s["loc"], s["qs"], s["ks"], s["vs"], s["bs"], s["da"], s["dp"], s["dvby"]
            gam, decay = loc["gam"], loc["decay"]
            dm = da * decay
            dn = dp * decay
            e = da * loc["a"] + dp * loc["p"]
            dgc = s["dgc"] + jnp.sum(e, axis=1, keepdims=True) - _row_to_col(jnp.sum(e, axis=0, keepdims=True), eye)
            dy = dvby[:, A_DK:]
            dvb = dvby[:, :A_DK]
            dkb = _dot1(dm, ks, "nn") + dy * gam
            dk = s["dk"] + _dot1(dm, loc["kb"], "tn") + _dot1(dn, qs, "tn") + dkb * bs
            dq = s["dq"] + _dot1(dn, ks, "nn")
            dgc = dgc + jnp.sum(dy * loc["y"], axis=-1, keepdims=True)
            dbeta = jnp.sum(dkb * ks, axis=-1, keepdims=True) + jnp.sum(dvb * vs, axis=-1, keepdims=True)
            dv = dvb * bs
            dgs = jnp.sum(jnp.where(mk["upper"], _col_to_row(dgc, eye), 0.0), axis=1, keepdims=True)
            for j in heads:
                h = grp * HEAD_GROUP + j
                sl = slice(h * A_DK, (h + 1) * A_DK)
                dq_ref[:, sl] = _head_rows(dq, j)
                dk_ref[:, sl] = _head_rows(dk, j)
                dv_ref[:, sl] = _head_rows(dv, j)
            s["dbeta"], s["dgs"] = dbeta, dgs

        for stage in (local_part, state_part, solve_part, inverse_part_a, inverse_part_b, finish):
            for grp in groups:
                stage(grp, st[grp])
        for grp in groups:
            for j in heads:
                h = grp * HEAD_GROUP + j
                dbeta_t = dbeta_t + jnp.where(lane == h, _head_rows(st[grp]["dbeta"], j), 0.0)
                dg_t = dg_t + jnp.where(lane == h, _head_rows(st[grp]["dgs"], j), 0.0)
        dbeta_ref[...] = dbeta_t
        dg_ref[...] = dg_t

    rev = lambda n: (n_chunks - 1 - n, 0)
    rev4 = lambda n: (n_chunks - 1 - n, 0, 0, 0)
    tile = pl.BlockSpec((CHUNK, A_W), rev)
    small = pl.BlockSpec((CHUNK, A_HEADS), rev)
    return pl.pallas_call(
        body, name="delta_bwd", grid=(n_chunks,),
        in_specs=[tile, tile, tile, small, small, pl.BlockSpec((1, A_HEADS, A_DK, A_DK), rev4),
                  pl.BlockSpec((1, N_HEAD_GROUPS, GROUP_ROWS, GROUP_ROWS), rev4), tile],
        out_specs=[tile, tile, tile, small, small],
        out_shape=[jax.ShapeDtypeStruct((S, A_W), F32)] * 3 + [jax.ShapeDtypeStruct((S, A_HEADS), F32)] * 2,
        scratch_shapes=[pltpu.VMEM((A_HEADS, A_DK, A_DK), F32)],
        compiler_params=_cparams(("arbitrary",)),
    )(q, k, v, beta, g, sprev, tinv, do)


def _prep_a_dpre(raw, raw_prev, w, dq, dk, dv):
    y, dy_dpre = _prep_a_core(raw, raw_prev, w)
    parts = []
    for h in range(A_HEADS):
        yq = y[:, h * A_DK:(h + 1) * A_DK]
        dqh = dq[:, h * A_DK:(h + 1) * A_DK]
        rq = lax.rsqrt(jnp.sum(yq * yq, axis=-1, keepdims=True) + L2_EPS)
        parts.append((A_DK ** -0.5) * (rq * dqh - yq * (rq * rq * rq) * jnp.sum(dqh * yq, axis=-1, keepdims=True)))
    for h in range(A_HEADS):
        yk = y[:, A_W + h * A_DK:A_W + (h + 1) * A_DK]
        dkh = dk[:, h * A_DK:(h + 1) * A_DK]
        rk = lax.rsqrt(jnp.sum(yk * yk, axis=-1, keepdims=True) + L2_EPS)
        parts.append(rk * dkh - yk * (rk * rk * rk) * jnp.sum(dkh * yk, axis=-1, keepdims=True))
    parts.append(dv)
    return jnp.concatenate(parts, axis=1) * dy_dpre


def prep_a_bwd(qkv_raw, ba, conv_a, a_log, dt_bias, dq, dk, dv, dbeta, dg):
    S = qkv_raw.shape[0]
    ts = 256

    def body(x_ref, xp_ref, xn_ref, ba_ref, w_ref, al_ref, dt_ref, dq_ref, dqn_ref, dk_ref, dkn_ref, dv_ref, dvn_ref,
             dbeta_ref, dg_ref, draw_ref, dba_ref, dw_ref, dal_ref, ddt_ref):
        i = pl.program_id(0)
        first = (i > 0).astype(F32)
        last = (i < pl.num_programs(0) - 1).astype(F32)
        w = w_ref[...]
        cur, prev = x_ref[...].astype(F32), _halo_prev(xp_ref) * first
        dpre = _prep_a_dpre(cur, prev, w, dq_ref[...], dk_ref[...], dv_ref[...])
        dpre_n = _prep_a_dpre(_halo_next(xn_ref), cur[ts - 8:ts], w, _halo_next(dqn_ref), _halo_next(dkn_ref),
                              _halo_next(dvn_ref)) * last
        for j in range(A_CONV):
            dw_ref[j:j + 1, :] += jnp.sum(dpre * _shift_down(cur, prev, A_CONV - 1 - j), axis=0, keepdims=True)
        draw = dpre * w[A_CONV - 1:A_CONV]
        for j in range(A_CONV - 1):
            draw = draw + _shift_up(dpre, dpre_n, A_CONV - 1 - j) * w[j:j + 1]
        draw_ref[...] = draw.astype(BF16)
        bav = ba_ref[...]
        beta = _sigmoid(bav[:, 0:A_HEADS])
        xa = bav[:, A_HEADS:2 * A_HEADS] + dt_ref[...]
        nexp = -jnp.exp(al_ref[...])
        dgv = dg_ref[...]
        da = dgv * nexp * _sigmoid(xa)
        dba_ref[:, 0:A_HEADS] = dbeta_ref[...] * beta * (1.0 - beta)
        dba_ref[:, A_HEADS:2 * A_HEADS] = da
        dal_ref[...] += jnp.sum(dgv * nexp * _softplus(xa), axis=0, keepdims=True)
        ddt_ref[...] += jnp.sum(da, axis=0, keepdims=True)

    return rowcall(
        body, name="prep_a_bwd", S=S, ts=ts,
        ins=[(qkv_raw, "row"), (qkv_raw, "prev"), (qkv_raw, "next"), (ba, "row"), (conv_a, "vec"), (a_log, "vec"),
             (dt_bias, "vec"), (dq, "row"), (dq, "next"), (dk, "row"), (dk, "next"), (dv, "row"), (dv, "next"),
             (dbeta, "row"), (dg, "row")],
        outs=[((S, 3 * A_W), BF16, "row"), ((S, 2 * A_HEADS), F32, "row"), ((A_CONV, 3 * A_W), F32, "acc"),
              ((1, A_HEADS), F32, "acc"), ((1, A_HEADS), F32, "acc")])


def grad_x_final(dh1, x, dxpre1, mod):
    S = x.shape[0]

    def body(dh_ref, x_ref, dx_ref, m_ref, gx_ref, dscale_ref, dshift_ref):
        dh = dh_ref[...]
        gx_ref[...] = ALPHA * dx_ref[...] + dh * (1.0 + m_ref[...][SCALE_T:SCALE_T + 1])
        dscale_ref[...] += jnp.sum(dh * x_ref[...], axis=0, keepdims=True)
        dshift_ref[...] += jnp.sum(dh, axis=0, keepdims=True)

    vec = ((1, D_MODEL), F32, "acc")
    return rowcall(body, name="grad_x_final", S=S, ts=512, ins=[(dh1, "row"), (x, "row"), (dxpre1, "row"), (mod, "vec")],
                   outs=[((S, D_MODEL), F32, "row"), vec, vec])


_C_QKV, _C_Z, _C_BA, _C_QKVB, _C_G = 0, 3 * A_W, 4 * A_W, 4 * A_W + 2 * A_HEADS, 4 * A_W + 2 * A_HEADS + 3 * B_W
BA_PAD = 128


def split_w_in(w_in):
    ba = jnp.pad(w_in[:, _C_BA:_C_QKVB], ((0, 0), (0, BA_PAD - 2 * A_HEADS)))
    return dict(qkv=w_in[:, _C_QKV:_C_Z], z=w_in[:, _C_Z:_C_BA], ba=ba, qkvb=w_in[:, _C_QKVB:_C_G], g=w_in[:, _C_G:])


def join_w_in(p):
    return jnp.concatenate([p["qkv"], p["z"], p["ba"][:, :2 * A_HEADS], p["qkvb"], p["g"]], axis=1)


def forward_local(x, target, mod, w, sm, late_weights=None):
    h1 = modulate(x, mod, SHIFT_T, SCALE_T, "mod_t")
    qkv_raw = mm(h1, w["qkv"], mode="nn", out_dtype=BF16, name="proj_qkv")
    z = mm(h1, w["z"], mode="nn", out_dtype=BF16, name="proj_z")
    ba = mm(h1, w["ba"], mode="nn", out_dtype=F32, name="proj_ba")
    qkvb = mm(h1, w["qkvb"], mode="nn", out_dtype=BF16, name="proj_qkvb")
    gates_raw = mm(h1, w["g"], mode="nn", out_dtype=BF16, name="proj_g")
    q, k, v, beta, g = prep_a_fwd(qkv_raw, ba, sm["conv_a"], sm["a_log"], sm["dt_bias"])
    o_pre, sprev, tinv = delta_fwd(q, k, v, beta, g)
    o_a = gate_a_fwd(o_pre, z, sm["norm_a"])
    qkv_pad = jnp.pad(qkvb, ((PAD_ROWS, 0), (0, 0)))
    bias = jnp.transpose(bias_expand(sm["rel_bias"]), (1, 0, 2))
    o_b = attn_fwd(qkv_pad, bias)
    if late_weights is not None:
        w = dict(w, **late_weights(o_b))
    ya = mm(o_a, w["branch_a"], mode="nn", out_dtype=BF16, name="branch_a")
    yb = mm(o_b, w["branch_b"], mode="nn", out_dtype=BF16, name="branch_b")
    merged = merge_fwd(gates_raw, sm["b_gate"], ya, yb)
    mix = mm(merged, w["o"], mode="nn", out_dtype=F32, name="mix")
    xpre1, x1, h2 = ln1_fwd(x, mix, mod, sm["ln1_g"], sm["ln1_b"])
    up = mm(h2, w["up"], mode="nn", out_dtype=BF16, name="ffn_up", b_shards=True)
    act = ffn_act_fwd(up, sm["conv_ffn"], sm["b_conv_ffn"])
    ffn = mm(act, w["down"], mode="nn", out_dtype=F32, name="ffn_down")
    dxpre2, dffn, loss, dgate_f, dln2_g, dln2_b = final_fwd_bwd(x1, ffn, target, mod, sm["ln2_g"], sm["ln2_b"])
    saved = dict(h1=h1, qkv_raw=qkv_raw, z=z, ba=ba, gates_raw=gates_raw, q=q, k=k, v=v, beta=beta, g=g,
                 o_pre=o_pre, sprev=sprev, tinv=tinv, o_a=o_a, qkv_pad=qkv_pad, bias=bias, o_b=o_b, ya=ya, yb=yb,
                 merged=merged, mix=mix, xpre1=xpre1, x1=x1, h2=h2, up=up, act=act, ffn=ffn, w=w)
    return loss, dxpre2, dffn, dict(gate_f=dgate_f, ln2_g=dln2_g, ln2_b=dln2_b), saved


def backward_local(x, mod, sm, dxpre2, dffn, fin, sv, hooks=None):
    w = sv["w"]
    dact = mm(dffn, w["down"], mode="nt", out_dtype=BF16, name="d_act")
    gw_down = mm(sv["act"], dffn, mode="tn", out_dtype=BF16, name="gw_down")
    dup, dconv_ffn, db_conv_ffn = ffn_act_bwd(dact, sv["up"], sm["conv_ffn"], sm["b_conv_ffn"])
    dh2 = mm(dup, w["up"], mode="nt", out_dtype=F32, name="d_h2", b_shards=True)
    gw_up = mm(sv["h2"], dup, mode="tn", out_dtype=BF16, name="gw_up", out_shards=N_CHIPS)
    dxpre1, dmix, dsc_f, dsh_f, dgate_t, dln1_g, dln1_b = ln1_bwd(
        dxpre2, dh2, sv["xpre1"], sv["mix"], mod, sm["ln1_g"], sm["ln1_b"])
    dmerged = mm(dmix, w["o"], mode="nt", out_dtype=BF16, name="d_merged")
    gw_o = mm(sv["merged"], dmix, mode="tn", out_dtype=BF16, name="gw_o")
    dya, dyb, dgates, db_gate = merge_bwd(dmerged, sv["gates_raw"], sm["b_gate"], sv["ya"], sv["yb"])
    do_a = mm(dya, w["branch_a"], mode="nt", out_dtype=BF16, name="d_oa")
    gw_branch_a = mm(sv["o_a"], dya, mode="tn", out_dtype=BF16, name="gw_branch_a")
    do_b = mm(dyb, w["branch_b"], mode="nt", out_dtype=BF16, name="d_ob")
    gw_branch_b = mm(sv["o_b"], dyb, mode="tn", out_dtype=BF16, name="gw_branch_b")
    bias = sv["bias"]
    if hooks is not None:
        bias = bias + hooks["late_start"](dict(w_branch_a=gw_branch_a, w_branch_b=gw_branch_b, w_o=gw_o, w_up=gw_up,
                                               w_down=gw_down))[0, 0]
    dq_b, dk_pad, dv_pad, dbias = attn_bwd(sv["qkv_pad"], bias, do_b)
    if hooks is not None:
        dbias = dbias + hooks["late_finish"](dq_b)[0, 0]
    dqkvb = jnp.concatenate([dq_b, dk_pad[PAD_ROWS:].astype(BF16), dv_pad[PAD_ROWS:].astype(BF16)], axis=1)
    drel_bias = bias_reduce(jnp.transpose(dbias, (1, 0, 2)))
    do_pre, dz, dnorm_a = gate_a_bwd(do_a, sv["o_pre"], sv["z"], sm["norm_a"])
    dq, dk, dv, dbeta, dg = delta_bwd(sv["q"], sv["k"], sv["v"], sv["beta"], sv["g"], sv["sprev"], sv["tinv"], do_pre)
    dqkv_raw, dba16, dconv_a, da_log, ddt_bias = prep_a_bwd(
        sv["qkv_raw"], sv["ba"], sm["conv_a"], sm["a_log"], sm["dt_bias"], dq, dk, dv, dbeta, dg)
    dba = jnp.pad(dba16, ((0, 0), (0, BA_PAD - 2 * A_HEADS))).astype(BF16)
    pieces = dict(qkv=dqkv_raw, z=dz, ba=dba, qkvb=dqkvb, g=dgates)
    gw_in = join_w_in({key: mm(sv["h1"], dpiece, mode="tn", out_dtype=BF16, name="gw_in_" + key)
                       for key, dpiece in pieces.items()})
    w_ba = w["ba"]
    w_z = w["z"]
    if hooks is not None:
        w_ba = w_ba + hooks["w_in_start"](gw_in)[0, 0].astype(BF16)
    dh1 = mm(pieces["ba"], w_ba, mode="nt", out_dtype=F32, name="d_h1_ba")
    dh1 = mm(pieces["qkv"], w["qkv"], mode="nt", out_dtype=F32, name="d_h1_qkv", acc_in=dh1)
    if hooks is not None:
        w_z = w_z + hooks["w_in_finish"](dh1)[0, 0].astype(BF16)
    dh1 = mm(pieces["z"], w_z, mode="nt", out_dtype=F32, name="d_h1_z", acc_in=dh1)
    for key in ("qkvb", "g"):
        dh1 = mm(pieces[key], w[key], mode="nt", out_dtype=F32, name="d_h1_" + key, acc_in=dh1)
    grad_x, dsc_t, dsh_t = grad_x_final(dh1, x, dxpre1, mod)
    dmod = jnp.concatenate([dsh_t, dsc_t, dgate_t, dsh_f, dsc_f, fin["gate_f"]], axis=0)
    gw = dict(w_in=gw_in, w_branch_a=gw_branch_a, w_branch_b=gw_branch_b, w_o=gw_o, w_up=gw_up, w_down=gw_down)
    gs = dict(b_gate=db_gate, conv_a=dconv_a, a_log=da_log, dt_bias=ddt_bias, norm_a=dnorm_a, rel_bias=drel_bias,
              ln1_g=dln1_g, ln1_b=dln1_b, conv_ffn=dconv_ffn, b_conv_ffn=db_conv_ffn, ln2_g=fin["ln2_g"], ln2_b=fin["ln2_b"])
    return grad_x, dmod, gw, gs


MESH = pl.DeviceIdType.MESH
ANY = pl.BlockSpec(memory_space=pl.ANY)
WHOLE_VMEM = pl.BlockSpec(memory_space=pltpu.VMEM)


def _place():
    return lax.axis_index("x"), lax.axis_index("y"), lax.axis_index("c")


def allgather8(blk, name):
    m_per, n = blk.shape

    def body(x_ref, out_ref, send_sems, recv_sems, local_sem):
        x, y, c = _place()
        me, sibling = (x, y, c), (x, y, 1 - c)
        chips = [(1 - x, y), (x, 1 - y), (1 - x, 1 - y)]

        def rows(px, py, pc):
            return out_ref.at[pl.ds((4 * px + 2 * py + pc) * m_per, m_per), :]

        def copy(k, block, to, src=None):
            return pltpu.make_async_remote_copy(
                src_ref=rows(*block) if src is None else src, dst_ref=rows(*block),
                send_sem=send_sems.at[k], recv_sem=recv_sems.at[k], device_id=to, device_id_type=MESH)

        mine = pltpu.make_async_copy(x_ref, rows(*me), local_sem)
        mine.start()
        first = [copy(0, me, sibling, src=x_ref)]
        first += [copy(1 + j, me, (*chip, c), src=x_ref) for j, chip in enumerate(chips)]
        for cp in first:
            cp.start()
        passed = [copy(4 + j, (*chip, c), sibling) for j, chip in enumerate(chips)]
        for j, chip in enumerate(chips):
            copy(1 + j, (*chip, c), me).wait_recv()
            passed[j].start()
        copy(0, sibling, me).wait_recv()
        for j, chip in enumerate(chips):
            copy(4 + j, (*chip, 1 - c), me).wait_recv()
        for cp in first + passed:
            cp.wait_send()
        mine.wait()

    return pl.pallas_call(
        body, name=name, out_shape=jax.ShapeDtypeStruct((N_DEV * m_per, n), blk.dtype),
        in_specs=[WHOLE_VMEM], out_specs=WHOLE_VMEM,
        scratch_shapes=[pltpu.SemaphoreType.DMA((7,)), pltpu.SemaphoreType.DMA((7,)), pltpu.SemaphoreType.DMA],
    )(blk)


def _chip_peers(x, y):
    return [(1 - x, y), (x, 1 - y), (1 - x, 1 - y)]


def chip_exchange(arrs, name, scatter):
    n = len(arrs)

    def body(*refs):
        ins, outs = refs[:n], refs[n:2 * n]
        send_sems, recv_sems, local_sems = refs[2 * n:]
        x, y, c = _place()
        me = 2 * x + y
        sibling = (x, y, 1 - c)
        peers = _chip_peers(x, y)

        def half(ref, which):
            r2 = ref.shape[0] // 2
            return ref.at[pl.ds(which * r2, r2), :]

        def outgoing(a, chip):
            return ins[a].at[chip] if scatter else ins[a]

        def copy(k, src, dst, to):
            return pltpu.make_async_remote_copy(src_ref=src, dst_ref=dst, send_sem=send_sems.at[k],
                                                recv_sem=recv_sems.at[k], device_id=to, device_id_type=MESH)

        started, local = [], []
        for a in range(n):
            lc = pltpu.make_async_copy(outgoing(a, me), outs[a].at[me], local_sems.at[a])
            lc.start()
            local.append(lc)
            for j, (px, py) in enumerate(peers):
                cp = copy(6 * a + j, half(outgoing(a, 2 * px + py), c), half(outs[a].at[me], c), (px, py, c))
                cp.start()
                started.append(cp)
        for a in range(n):
            for j, (px, py) in enumerate(peers):
                landed = half(outs[a].at[2 * px + py], c)
                copy(6 * a + j, landed, landed, (px, py, c)).wait_recv()
                relay = copy(6 * a + 3 + j, landed, landed, sibling)
                relay.start()
                started.append(relay)
        for a in range(n):
            for j, (px, py) in enumerate(peers):
                other = half(outs[a].at[2 * px + py], 1 - c)
                copy(6 * a + 3 + j, other, other, sibling).wait_recv()
        for cp in started:
            cp.wait_send()
        for lc in local:
            lc.wait()

    out_shape = [jax.ShapeDtypeStruct(a.shape if scatter else (N_CHIPS,) + a.shape, a.dtype) for a in arrs]
    return pl.pallas_call(
        body, name=name, out_shape=out_shape, in_specs=[ANY] * n, out_specs=[ANY] * n,
        scratch_shapes=[pltpu.SemaphoreType.DMA((6 * n,)), pltpu.SemaphoreType.DMA((6 * n,)), pltpu.SemaphoreType.DMA((n,))],
    )(*arrs)


HBM_SPEC = pl.BlockSpec(memory_space=pltpu.HBM)
SEM_SPEC = pl.BlockSpec(memory_space=pltpu.SEMAPHORE)
SIDE_EFFECT = pltpu.SideEffectType.DATAFLOW_SIDE_EFFECTING


def _in_hbm(a):
    return pltpu.with_memory_space_constraint(a, pltpu.HBM)


def exchange_start(arrs, name, scatter, after):
    n = len(arrs)
    lands = [lax.empty(a.shape if scatter else (N_CHIPS,) + a.shape, a.dtype) for a in arrs]

    def body(*refs):
        ins, zones = refs[:n], refs[n:2 * n]
        send_sems, recv_sems, token = refs[2 * n + 1], refs[2 * n + 2], refs[-1]
        x, y, c = _place()
        me = 2 * x + y
        for a in range(n):
            for j, (px, py) in enumerate(_chip_peers(x, y)):
                pltpu.make_async_remote_copy(
                    src_ref=ins[a].at[2 * px + py] if scatter else ins[a], dst_ref=zones[a].at[me],
                    send_sem=send_sems.at[3 * a + j], recv_sem=recv_sems.at[3 * a + j],
                    device_id=(px, py, c), device_id_type=MESH).start()
        token[...] = jnp.zeros_like(token)

    res = pl.pallas_call(
        body, name=name,
        out_shape=[pltpu.SemaphoreType.DMA((3 * n,)), pltpu.SemaphoreType.DMA((3 * n,))]
        + [pltpu.HBM(a.shape, a.dtype) for a in arrs] + [pltpu.HBM(z.shape, z.dtype) for z in lands]
        + [jax.ShapeDtypeStruct((8, 128), F32)],
        in_specs=[HBM_SPEC] * (2 * n) + [ANY], out_specs=[SEM_SPEC, SEM_SPEC] + [HBM_SPEC] * (2 * n) + [WHOLE_VMEM],
        input_output_aliases={i: 2 + i for i in range(2 * n)},
        compiler_params=pltpu.CompilerParams(has_side_effects=SIDE_EFFECT),
    )(*[_in_hbm(a) for a in arrs], *[_in_hbm(z) for z in lands], after)
    return dict(send=res[0], recv=res[1], src=res[2:2 + n], zones=res[2 + n:2 + 2 * n], token=res[-1], scatter=scatter)


def exchange_wait(handle, name, after):
    srcs, zones, scatter = handle["src"], handle["zones"], handle["scatter"]
    n = len(srcs)

    def body(*refs):
        ins, lands = refs[:n], refs[n:2 * n]
        send_sems, recv_sems = refs[2 * n], refs[2 * n + 1]
        x, y, c = _place()
        me = 2 * x + y
        for a in range(n):
            for j, (px, py) in enumerate(_chip_peers(x, y)):
                cp = pltpu.make_async_remote_copy(
                    src_ref=ins[a].at[me] if scatter else ins[a], dst_ref=lands[a].at[2 * px + py],
                    send_sem=send_sems.at[3 * a + j], recv_sem=recv_sems.at[3 * a + j],
                    device_id=(px, py, c), device_id_type=MESH)
                cp.wait_send()
                cp.wait_recv()

    res = pl.pallas_call(
        body, name=name, out_shape=[pltpu.HBM(a.shape, a.dtype) for a in list(srcs) + list(zones)],
        in_specs=[HBM_SPEC] * (2 * n) + [SEM_SPEC, SEM_SPEC, ANY], out_specs=[HBM_SPEC] * (2 * n),
        input_output_aliases={i: i for i in range(2 * n)},
        compiler_params=pltpu.CompilerParams(has_side_effects=SIDE_EFFECT),
    )(*srcs, *zones, handle["send"], handle["recv"], after)
    return res[n:]


def swap_start(arrs, name, after):
    n = len(arrs)
    lands = [lax.empty(a.shape, a.dtype) for a in arrs]

    def body(*refs):
        ins, zones = refs[:n], refs[n:2 * n]
        send_sems, recv_sems, token = refs[2 * n + 1], refs[2 * n + 2], refs[-1]
        x, y, c = _place()
        for a in range(n):
            pltpu.make_async_remote_copy(src_ref=ins[a], dst_ref=zones[a], send_sem=send_sems.at[a], recv_sem=recv_sems.at[a],
                                         device_id=(x, y, 1 - c), device_id_type=MESH).start()
        token[...] = jnp.zeros_like(token)

    res = pl.pallas_call(
        body, name=name,
        out_shape=[pltpu.SemaphoreType.DMA((n,)), pltpu.SemaphoreType.DMA((n,))]
        + [pltpu.HBM(a.shape, a.dtype) for a in arrs] * 2 + [jax.ShapeDtypeStruct((8, 128), F32)],
        in_specs=[HBM_SPEC] * (2 * n) + [ANY], out_specs=[SEM_SPEC, SEM_SPEC] + [HBM_SPEC] * (2 * n) + [WHOLE_VMEM],
        input_output_aliases={i: 2 + i for i in range(2 * n)},
        compiler_params=pltpu.CompilerParams(has_side_effects=SIDE_EFFECT),
    )(*[_in_hbm(a) for a in arrs], *[_in_hbm(z) for z in lands], after)
    return dict(send=res[0], recv=res[1], src=res[2:2 + n], zones=res[2 + n:2 + 2 * n], token=res[-1])


def swap_wait(handle, name, after):
    srcs, zones = handle["src"], handle["zones"]
    n = len(srcs)

    def body(*refs):
        ins, lands = refs[:n], refs[n:2 * n]
        send_sems, recv_sems = refs[2 * n], refs[2 * n + 1]
        x, y, c = _place()
        for a in range(n):
            cp = pltpu.make_async_remote_copy(src_ref=ins[a], dst_ref=lands[a], send_sem=send_sems.at[a],
                                              recv_sem=recv_sems.at[a], device_id=(x, y, 1 - c), device_id_type=MESH)
            cp.wait_send()
            cp.wait_recv()

    res = pl.pallas_call(
        body, name=name, out_shape=[pltpu.HBM(a.shape, a.dtype) for a in list(srcs) + list(zones)],
        in_specs=[HBM_SPEC] * (2 * n) + [SEM_SPEC, SEM_SPEC, ANY], out_specs=[HBM_SPEC] * (2 * n),
        input_output_aliases={i: i for i in range(2 * n)},
        compiler_params=pltpu.CompilerParams(has_side_effects=SIDE_EFFECT),
    )(*srcs, *zones, handle["send"], handle["recv"], after)
    return res[:n], res[n:]


TILE_BYTES = 2 * 1024 * 1024


def _row_tile(rows, row_bytes):
    if rows * row_bytes <= TILE_BYTES or rows % 8:
        return rows
    best = 8
    for t in range(8, rows + 1, 8):
        if rows % t == 0 and t * row_bytes <= TILE_BYTES:
            best = t
    return best


def pair_add(a, b, name):
    shape = a.shape
    a, b = a.reshape(-1, shape[-1]), b.reshape(-1, shape[-1])
    R, C = a.shape
    tr = _row_tile(R, C * 4)

    def body(a_ref, b_ref, o_ref):
        o_ref[...] = (a_ref[...].astype(F32) + b_ref[...].astype(F32)).astype(BF16)

    spec = pl.BlockSpec((tr, C), lambda i: (i, 0))
    return pl.pallas_call(body, name=name, grid=(R // tr,), in_specs=[spec, spec], out_specs=spec,
                          out_shape=jax.ShapeDtypeStruct((R, C), BF16), compiler_params=_cparams(("parallel",)))(a, b).reshape(shape)


def sum_lead(parts, name):
    K, R, C = parts.shape
    tr = _row_tile(R, C * 4)

    def body(p_ref, o_ref):
        acc = p_ref[0].astype(F32)
        for j in range(1, K):
            acc = acc + p_ref[j].astype(F32)
        o_ref[...] = acc

    return pl.pallas_call(
        body, name=name, grid=(R // tr,), in_specs=[pl.BlockSpec((K, tr, C), lambda i: (0, i, 0))],
        out_specs=pl.BlockSpec((tr, C), lambda i: (i, 0)), out_shape=jax.ShapeDtypeStruct((R, C), F32),
        compiler_params=_cparams(("parallel",)))(parts)


def adamw(w, g, m, v, name):
    R, C = w.shape
    tr = _row_tile(R, C * 4)

    def body(w_ref, g_ref, m_ref, v_ref, d_ref, mo_ref, vo_ref):
        gv = g_ref[...]
        m2 = ADAM_B1 * m_ref[...] + (1.0 - ADAM_B1) * gv
        v2 = ADAM_B2 * v_ref[...] + (1.0 - ADAM_B2) * (gv * gv)
        m_hat = m2 / (1.0 - ADAM_B1 ** ADAM_STEP)
        v_hat = v2 / (1.0 - ADAM_B2 ** ADAM_STEP)
        d_ref[...] = -ADAM_LR * (m_hat / (jnp.sqrt(v_hat) + ADAM_EPS) + ADAM_WD * w_ref[...])
        mo_ref[...] = m2
        vo_ref[...] = v2

    spec = pl.BlockSpec((tr, C), lambda i: (i, 0))
    return pl.pallas_call(body, name=name, grid=(R // tr,), in_specs=[spec] * 4, out_specs=[spec] * 3,
                          out_shape=[jax.ShapeDtypeStruct((R, C), F32)] * 3, compiler_params=_cparams(("parallel",)))(w, g, m, v)


LANES = 1024


def _pack(arrs, rows):
    out, offs, r = [], [], 0
    for a in arrs:
        flat = a.reshape(-1)
        nr = -(-flat.shape[0] // LANES)
        out.append(jnp.pad(flat, (0, nr * LANES - flat.shape[0])))
        offs.append(r)
        r += nr
    assert r <= rows, (r, rows)
    out.append(jnp.zeros(((rows - r) * LANES,), F32))
    return jnp.concatenate(out).reshape(rows, LANES), offs


def _unpack(packed, offs, shapes):
    flat = packed.reshape(-1)
    return [flat[o * LANES:o * LANES + math.prod(s)].reshape(s) for o, s in zip(offs, shapes)]


WEIGHTS = ["w_ada", "b_ada", "w_in", "b_gate", "conv_a", "a_log", "dt_bias", "norm_a", "rel_bias", "w_branch_a",
           "w_branch_b", "w_o", "ln1_g", "ln1_b", "w_up", "conv_ffn", "b_conv_ffn", "w_down", "ln2_g", "ln2_b"]
BIG = ["w_in", "w_branch_a", "w_branch_b", "w_o", "w_up", "w_down"]
LATE = [n for n in BIG if n != "w_in"]
KEPT_SHARDED = {"w_up"}
COL_SHARDED = {"w_in", "w_up"}
SMALL_SHARDED = {"conv_a": 3 * A_W // N_CHIPS, "rel_bias": B_REL // N_CHIPS, "conv_ffn": 2 * D_FF // N_CHIPS}
SMALL = [n for n in WEIGHTS if n not in BIG and n != "w_ada"]


def _to_full(g4, name):
    if name in KEPT_SHARDED:
        return g4
    if name in COL_SHARDED:
        return jnp.transpose(g4, (1, 0, 2)).reshape(g4.shape[1], -1)
    return g4.reshape(-1, g4.shape[2])


def _to_shards(full, name):
    if name in KEPT_SHARDED:
        return full
    if name in COL_SHARDED:
        return jnp.transpose(full.reshape(full.shape[0], N_CHIPS, -1), (1, 0, 2))
    return full.reshape(N_CHIPS, -1, full.shape[1])


def kernel(x, c, w_ada, b_ada, w_in, b_gate, conv_a, a_log, dt_bias, norm_a, rel_bias, w_branch_a, w_branch_b, w_o, ln1_g, ln1_b, w_up, conv_ffn, b_conv_ffn, w_down, ln2_g, ln2_b, loss_target, m_w_ada, m_b_ada, m_w_in, m_b_gate, m_conv_a, m_a_log, m_dt_bias, m_norm_a, m_rel_bias, m_w_branch_a, m_w_branch_b, m_w_o, m_ln1_g, m_ln1_b, m_w_up, m_conv_ffn, m_b_conv_ffn, m_w_down, m_ln2_g, m_ln2_b, v_w_ada, v_b_ada, v_w_in, v_b_gate, v_conv_a, v_a_log, v_dt_bias, v_norm_a, v_rel_bias, v_w_branch_a, v_w_branch_b, v_w_o, v_ln1_g, v_ln1_b, v_w_up, v_conv_ffn, v_b_conv_ffn, v_w_down, v_ln2_g, v_ln2_b):
    args = dict(locals())
    wts = {n: args[n] for n in WEIGHTS}
    moms = {n: args["m_" + n] for n in WEIGHTS}
    vars_ = {n: args["v_" + n] for n in WEIGHTS}
    xi, yi, ci = _place()
    chip = 2 * xi + yi
    dev = 4 * xi + 2 * yi + ci
    ada_cols = w_ada.shape[2]

    sshapes = [wts[n].shape[1:] for n in SMALL_SHARDED]
    spack, soffs = _pack([wts[n][0] for n in SMALL_SHARDED], 16)
    first = allgather8(jnp.concatenate([jnp.pad(c, ((0, 7), (0, 0))), spack]), "gather_c_small_w").reshape(N_DEV, 24, LANES)
    c_all = first[:, 0]
    b_ada_sh = lax.dynamic_slice(b_ada, (0, chip * ada_cols), (1, ada_cols))
    mod_sh = ada_fwd(c_all, w_ada[0], b_ada_sh)
    mod_g = allgather8(mod_sh, "gather_mod").reshape(N_CHIPS, 2, N_DEV, ada_cols)[:, 0]
    mod = lax.dynamic_slice(mod_g, (0, dev, 0), (N_CHIPS, 1, ada_cols)).reshape(6, D_MODEL)

    (w_in_g4,) = chip_exchange([wts["w_in"][0].astype(BF16)], "gather_w_in", scatter=False)
    wd = split_w_in(_to_full(w_in_g4, "w_in"))
    late_shards = [wts[n][0].astype(BF16) for n in LATE]
    late_gather = exchange_start(late_shards, "gather_late_start", scatter=False, after=w_in_g4)
    mod = mod + late_gather["token"][0, 0]

    def late_weights(after):
        zones = exchange_wait(late_gather, "gather_late_wait", after)
        full = [_to_full(lax.dynamic_update_slice(z, s[None], (chip, 0, 0)), n) for n, z, s in zip(LATE, zones, late_shards)]
        return {n[2:]: f for n, f in zip(LATE, full)}

    sg = first[::2, 8:]
    sparts = [_unpack(sg[j], soffs, sshapes) for j in range(N_CHIPS)]
    sm = {n: wts[n] for n in SMALL if n not in SMALL_SHARDED and n != "b_ada"}
    for i, n in enumerate(SMALL_SHARDED):
        sm[n] = jnp.concatenate([sparts[j][i] for j in range(N_CHIPS)], axis=-1)

    early = {}

    def late_start(g):
        early["swap"] = swap_start([g[n] for n in LATE], "grad_swap_late_start", g[LATE[0]])
        return early["swap"]["token"]

    def late_finish(after):
        mine, theirs = swap_wait(early["swap"], "grad_swap_late_wait", after)
        early["sums"] = [_to_shards(pair_add(a, b, "grad_pair_" + n), n) for n, a, b in zip(LATE, mine, theirs)]
        early["scatter"] = exchange_start(early["sums"], "grad_scatter_start", scatter=True, after=theirs[0])
        return early["scatter"]["token"]

    def w_in_start(g):
        early["swap_in"] = swap_start([g], "grad_swap_w_in_start", g)
        return early["swap_in"]["token"]

    def w_in_finish(after):
        (mine,), (theirs,) = swap_wait(early["swap_in"], "grad_swap_w_in_wait", after)
        early["sum_in"] = _to_shards(pair_add(mine, theirs, "grad_pair_w_in"), "w_in")
        early["scatter_in"] = exchange_start([early["sum_in"]], "grad_scatter_w_in_start", scatter=True, after=theirs)
        return early["scatter_in"]["token"]

    hooks = dict(late_start=late_start, late_finish=late_finish, w_in_start=w_in_start, w_in_finish=w_in_finish)
    loss, dxpre2, dffn, fin, sv = forward_local(x[0], loss_target[0], mod, wd, sm, late_weights)
    grad_x, dmod, gw, gs = backward_local(x[0], mod, sm, dxpre2, dffn, fin, sv, hooks)

    gnames = [n for n in SMALL if n != "b_ada"]
    vec, voffs = _pack([dmod] + [gs[n] for n in gnames] + [loss], 56)
    gathered = allgather8(vec, "gather_small_g").reshape(N_DEV, 56, LANES)
    summed = sum_lead(gathered, "sum_small_g")
    full_shapes = [(6, D_MODEL)] + [gs[n].shape for n in gnames] + [(1, 1)]
    parts = _unpack(summed, voffs, full_shapes)
    grads = {"b_ada": parts[0].reshape(1, -1)}
    for n, p in zip(gnames, parts[1:-1]):
        if n in SMALL_SHARDED:
            p = lax.dynamic_slice_in_dim(p, chip * SMALL_SHARDED[n], SMALL_SHARDED[n], axis=1)
        grads[n] = p.reshape(wts[n].shape)
    loss_total = parts[-1].reshape(())
    dmod_all = gathered[:, 0:6, :].reshape(N_DEV, 6 * D_MODEL)
    grads["w_ada"] = ada_bwd(c_all, lax.dynamic_slice(dmod_all, (0, chip * ada_cols), (N_DEV, ada_cols)))[None]

    def own_slot(zone, sums):
        return lax.dynamic_update_slice(zone, lax.dynamic_slice_in_dim(sums, chip, 1, axis=0), (chip, 0, 0))

    zones = exchange_wait(early["scatter"], "grad_scatter_wait", summed)
    for n, z, s in zip(LATE, zones, early["sums"]):
        grads[n] = sum_lead(own_slot(z, s), "grad_sum_" + n)[None]

    delta, new_m, new_v = {}, {}, {}

    def update(n):
        d, m2, v2 = adamw(wts[n][0], grads[n][0], moms[n][0], vars_[n][0], "adamw_" + n)
        delta[n], new_m[n], new_v[n] = d[None], m2[None], v2[None]

    for n in ["w_ada"] + LATE:
        update(n)
    shapes = [wts[n].shape for n in SMALL]
    packs = [_pack([t[n] for n in SMALL], 32) for t in (wts, grads, moms, vars_)]
    outs = adamw(*[p[0] for p in packs], "adamw_small")
    for res, o in zip((delta, new_m, new_v), outs):
        for n, a in zip(SMALL, _unpack(o, packs[0][1], shapes)):
            res[n] = a
    (zone_in,) = exchange_wait(early["scatter_in"], "grad_scatter_w_in_wait", outs[0])
    grads["w_in"] = sum_lead(own_slot(zone_in, early["sum_in"]), "grad_sum_w_in")[None]
    update("w_in")
    return (loss_total, grad_x[None], *[grads[n] for n in WEIGHTS], *[delta[n] for n in WEIGHTS],
            *[new_m[n] for n in WEIGHTS], *[new_v[n] for n in WEIGHTS])
```

```python
import functools
import math

import jax
import jax.numpy as jnp
from jax import lax
from jax.experimental import pallas as pl
from jax.experimental.pallas import tpu as pltpu

F32 = jnp.float32
BF16 = jnp.bfloat16

D_MODEL = 1024
CHUNK = 64
A_HEADS = 8
A_DK = 128
A_W = A_HEADS * A_DK
A_CONV = 4
B_HEADS = 16
B_DH = 64
B_W = B_HEADS * B_DH
B_PREV = 8
B_BAND = (B_PREV + 1) * CHUNK
B_MAX_REL = 256
B_REL = CHUNK - 1 + B_MAX_REL + 1
D_FF = 2816
FFN_CONV = 3
IN_COLS = 4 * A_W + 2 * A_HEADS + 3 * B_W + 2 * D_MODEL
ALPHA = 2.0 ** 0.25
LN_EPS = 1e-5
RMS_EPS = 1e-6
L2_EPS = 1e-6
NEG_INF = -1e30
ADAM_LR, ADAM_B1, ADAM_B2, ADAM_EPS, ADAM_WD, ADAM_STEP = 0.001, 0.9, 0.999, 1e-08, 0.01, 10
N_CHIPS = 4
N_DEV = 8
VMEM_LIMIT = 56 * 1024 * 1024


def _cparams(sem=None):
    return pltpu.CompilerParams(dimension_semantics=sem, vmem_limit_bytes=VMEM_LIMIT)


_DIMS = {"nn": (((1,), (0,)), ((), ())), "nt": (((1,), (1,)), ((), ())), "tn": (((0,), (0,)), ((), ()))}


MM_TILE_CAP = 1408


def _mm_tile(n):
    return max(t for t in range(128, min(n, MM_TILE_CAP) + 1, 128) if n % t == 0)


def mm(a, b, *, mode, out_dtype, name, acc_in=None, b_shards=False, out_shards=0):
    b_rows, b_cols = (b.shape[1], b.shape[0] * b.shape[2]) if b_shards else b.shape
    if mode == "nn":
        (M, K), (K2, N) = a.shape, (b_rows, b_cols)
    elif mode == "nt":
        (M, K), (N, K2) = a.shape, (b_rows, b_cols)
    else:
        (K, M), (K2, N) = a.shape, (b_rows, b_cols)
    assert K == K2, (a.shape, b.shape, mode)
    tm, tn, tk = _mm_tile(M), _mm_tile(N), _mm_tile(K)
    nk = K // tk

    def body(*refs):
        if acc_in is None:
            a_ref, b_ref, o_ref, acc_ref = refs
        else:
            a_ref, b_ref, c_ref, o_ref, acc_ref = refs
        k = pl.program_id(2)

        @pl.when(k == 0)
        def _():
            if acc_in is None:
                acc_ref[...] = jnp.zeros_like(acc_ref)
            else:
                acc_ref[...] = c_ref[...]

        acc_ref[...] += lax.dot_general(a_ref[...].astype(BF16), b_ref[...].astype(BF16), _DIMS[mode],
                                        preferred_element_type=F32)

        @pl.when(k == nk - 1)
        def _():
            o_ref[...] = acc_ref[...].astype(out_dtype)

    a_spec = pl.BlockSpec((tk, tm), lambda i, j, k: (k, i)) if mode == "tn" else pl.BlockSpec((tm, tk), lambda i, j, k: (i, k))
    if b_shards:
        assert (tk if mode == "nt" else tn) == b.shape[2] and mode != "tn", (b.shape, tn, tk, mode)
        b_spec = (pl.BlockSpec((None, tn, tk), lambda i, j, k: (k, j, 0)) if mode == "nt"
                  else pl.BlockSpec((None, tk, tn), lambda i, j, k: (j, k, 0)))
    else:
        b_spec = pl.BlockSpec((tn, tk), lambda i, j, k: (j, k)) if mode == "nt" else pl.BlockSpec((tk, tn), lambda i, j, k: (k, j))
    o_spec = pl.BlockSpec((tm, tn), lambda i, j, k: (i, j))
    out_shape = jax.ShapeDtypeStruct((M, N), out_dtype)
    if out_shards:
        assert N == out_shards * tn and acc_in is None, (N, tn, out_shards)
        o_spec = pl.BlockSpec((None, tm, tn), lambda i, j, k: (j, i, 0))
        out_shape = jax.ShapeDtypeStruct((out_shards, M, tn), out_dtype)
    ins, in_specs, aliases = [a, b], [a_spec, b_spec], {}
    if acc_in is not None:
        assert acc_in.shape == (M, N) and acc_in.dtype == F32 and out_dtype == F32
        ins.append(acc_in)
        in_specs.append(o_spec)
        aliases = {2: 0}
    return pl.pallas_call(
        body, name=name, grid=(M // tm, N // tn, nk), in_specs=in_specs, out_specs=o_spec,
        out_shape=out_shape, scratch_shapes=[pltpu.VMEM((tm, tn), F32)],
        input_output_aliases=aliases, compiler_params=_cparams(("parallel", "parallel", "arbitrary")),
    )(*ins)


def rowcall(body, *, name, S, ts, ins, outs, scratch=()):
    assert S % ts == 0 and ts % 16 == 0
    nsteps = S // ts
    in_specs, arrays = [], []
    for arr, kind in ins:
        arrays.append(arr)
        if kind == "row":
            in_specs.append(pl.BlockSpec((ts, arr.shape[1]), lambda i: (i, 0)))
        elif kind in ("prev", "next"):
            hr = 8 * (4 // arr.dtype.itemsize)
            per, last = ts // hr, S // hr - 1
            if kind == "prev":
                in_specs.append(pl.BlockSpec((hr, arr.shape[1]), lambda i, per=per: (jnp.maximum(i * per - 1, 0), 0)))
            else:
                in_specs.append(pl.BlockSpec((hr, arr.shape[1]), lambda i, per=per, last=last: (jnp.minimum((i + 1) * per, last), 0)))
        else:
            nd = arr.ndim
            in_specs.append(pl.BlockSpec(arr.shape, lambda i, nd=nd: (0,) * nd))
    out_specs, out_shapes, acc_idx = [], [], []
    for n, (shape, dtype, kind) in enumerate(outs):
        out_shapes.append(jax.ShapeDtypeStruct(shape, dtype))
        if kind == "row":
            out_specs.append(pl.BlockSpec((ts, shape[1]), lambda i: (i, 0)))
        else:
            nd = len(shape)
            out_specs.append(pl.BlockSpec(shape, lambda i, nd=nd: (0,) * nd))
            acc_idx.append(n)
    n_in = len(arrays)

    def wrapped(*refs):
        @pl.when(pl.program_id(0) == 0)
        def _():
            for n in acc_idx:
                refs[n_in + n][...] = jnp.zeros_like(refs[n_in + n])

        body(*refs)

    res = pl.pallas_call(
        wrapped, name=name, grid=(nsteps,), in_specs=in_specs, out_specs=out_specs, out_shape=out_shapes,
        scratch_shapes=list(scratch), compiler_params=_cparams(("arbitrary",) if acc_idx else ("parallel",)),
    )(*arrays)
    return res


def _halo_prev(ref):
    v = ref[...].astype(F32)
    return v[v.shape[0] - 8:]


def _halo_next(ref):
    return ref[...].astype(F32)[:8]


def _shift_down(cur, prev8, k):
    if k == 0:
        return cur
    rolled = pltpu.roll(cur, k, axis=0)
    fix = pltpu.roll(prev8, k, axis=0)
    row = lax.broadcasted_iota(jnp.int32, (8, 1), 0)
    top = jnp.where(row < k, fix, rolled[0:8])
    if cur.shape[0] == 8:
        return top
    return jnp.concatenate([top, rolled[8:]], axis=0)


def _shift_up(cur, next8, k):
    if k == 0:
        return cur
    n = cur.shape[0]
    rolled = pltpu.roll(cur, n - k, axis=0)
    fix = pltpu.roll(next8, 8 - k, axis=0)
    row = lax.broadcasted_iota(jnp.int32, (8, 1), 0)
    bot = jnp.where(row >= 8 - k, fix, rolled[n - 8:n])
    return jnp.concatenate([rolled[:n - 8], bot], axis=0)


def _sigmoid(x):
    return 1.0 / (1.0 + jnp.exp(-x))


def _silu(x):
    return x * _sigmoid(x)


def _silu_and_grad(x):
    s = _sigmoid(x)
    return x * s, s * (1.0 + x * (1.0 - s))


def _softplus(x):
    return jnp.maximum(x, 0.0) + jnp.log1p(jnp.exp(-jnp.abs(x)))


def _split2(x):
    hi = x.astype(BF16)
    return hi, (x - hi.astype(F32)).astype(BF16)


def _dot1(a, b, mode):
    return lax.dot_general(a.astype(BF16), b.astype(BF16), _DIMS[mode], preferred_element_type=F32)


def _dot3(a, b, mode):
    ah, al = _split2(a)
    bh, bl = _split2(b)
    d = lambda p, q: lax.dot_general(p, q, _DIMS[mode], preferred_element_type=F32)
    return d(ah, bh) + (d(ah, bl) + d(al, bh))


def ada_fwd(c_all, w_sh, b_sh):
    n = w_sh.shape[1]
    tn = 512

    def body(c_ref, w_ref, b_ref, o_ref):
        o_ref[...] = _dot1(_silu(c_ref[...]), w_ref[...], "nn") + b_ref[...]

    return pl.pallas_call(
        body, name="ada_fwd", grid=(n // tn,),
        in_specs=[pl.BlockSpec((N_DEV, D_MODEL), lambda j: (0, 0)), pl.BlockSpec((D_MODEL, tn), lambda j: (0, j)),
                  pl.BlockSpec((1, tn), lambda j: (0, j))],
        out_specs=pl.BlockSpec((N_DEV, tn), lambda j: (0, j)), out_shape=jax.ShapeDtypeStruct((N_DEV, n), F32),
        compiler_params=_cparams(("parallel",)),
    )(c_all, w_sh, b_sh)


def ada_bwd(c_all, dmod_sh):
    n = dmod_sh.shape[1]
    tn = 512

    def body(c_ref, d_ref, o_ref):
        o_ref[...] = _dot1(_silu(c_ref[...]), d_ref[...], "tn")

    return pl.pallas_call(
        body, name="ada_bwd", grid=(n // tn,),
        in_specs=[pl.BlockSpec((N_DEV, D_MODEL), lambda j: (0, 0)), pl.BlockSpec((N_DEV, tn), lambda j: (0, j))],
        out_specs=pl.BlockSpec((D_MODEL, tn), lambda j: (0, j)), out_shape=jax.ShapeDtypeStruct((D_MODEL, n), F32),
        compiler_params=_cparams(("parallel",)),
    )(c_all, dmod_sh)


SHIFT_T, SCALE_T, GATE_T, SHIFT_F, SCALE_F, GATE_F = range(6)


def modulate(x, mod, shift_row, scale_row, name):
    S = x.shape[0]

    def body(x_ref, m_ref, o_ref):
        m = m_ref[...]
        o_ref[...] = (x_ref[...] * (1.0 + m[scale_row:scale_row + 1]) + m[shift_row:shift_row + 1]).astype(BF16)

    return rowcall(body, name=name, S=S, ts=512, ins=[(x, "row"), (mod, "vec")], outs=[((S, D_MODEL), BF16, "row")])[0]


def _conv_fwd(cur, prev, w, width):
    y = cur * w[width - 1:width]
    for j in range(width - 1):
        y = y + _shift_down(cur, prev, width - 1 - j) * w[j:j + 1]
    return y


def _prep_a_core(cur, prev, w):
    return _silu_and_grad(_conv_fwd(cur, prev, w, A_CONV))


def prep_a_fwd(qkvba, conv_a, a_log, dt_bias):
    S = qkvba.shape[0]

    def body(x_ref, xp_ref, w_ref, al_ref, dt_ref, q_ref, k_ref, v_ref, beta_ref, g_ref):
        first = (pl.program_id(0) > 0).astype(F32)
        y, _ = _prep_a_core(x_ref[:, :3 * A_W].astype(F32), _halo_prev(xp_ref)[:, :3 * A_W] * first, w_ref[...])
        for h in range(A_HEADS):
            sl = slice(h * A_DK, (h + 1) * A_DK)
            qh = y[:, sl]
            kh = y[:, A_W + h * A_DK:A_W + (h + 1) * A_DK]
            q_ref[:, sl] = qh * (lax.rsqrt(jnp.sum(qh * qh, axis=-1, keepdims=True) + L2_EPS) * (A_DK ** -0.5))
            k_ref[:, sl] = kh * lax.rsqrt(jnp.sum(kh * kh, axis=-1, keepdims=True) + L2_EPS)
        v_ref[...] = y[:, 2 * A_W:3 * A_W]
        bav = x_ref[:, 3 * A_W:].astype(F32)
        beta_ref[...] = _sigmoid(bav[:, 0:A_HEADS])
        g_ref[...] = -jnp.exp(al_ref[...]) * _softplus(bav[:, A_HEADS:2 * A_HEADS] + dt_ref[...])

    return rowcall(
        body, name="prep_a_fwd", S=S, ts=256,
        ins=[(qkvba, "row"), (qkvba, "prev"), (conv_a, "vec"), (a_log, "vec"), (dt_bias, "vec")],
        outs=[((S, A_W), F32, "row")] * 3 + [((S, A_HEADS), F32, "row")] * 2)


HEAD_GROUP = 4
GROUP_ROWS = HEAD_GROUP * CHUNK
N_HEAD_GROUPS = A_HEADS // HEAD_GROUP
LOG_CHUNK = int(math.log2(CHUNK))


def _tri_masks():
    rb = lax.broadcasted_iota(jnp.int32, (GROUP_ROWS, GROUP_ROWS), 0)
    cb = lax.broadcasted_iota(jnp.int32, (GROUP_ROWS, GROUP_ROWS), 1)
    same = (rb >> LOG_CHUNK) == (cb >> LOG_CHUNK)
    return dict(causal=same & (rb >= cb), strict=same & (rb > cb), eye=rb == cb, upper=same & (cb >= rb),
                last=cb == (rb | (CHUNK - 1)), rb=rb, cb=cb)


def _col_to_row(colv, eye):
    return jnp.sum(jnp.where(eye, colv, 0.0), axis=0, keepdims=True)


def _row_to_col(rowv, eye):
    return jnp.sum(jnp.where(eye, rowv, 0.0), axis=1, keepdims=True)


def _tri_inv(a_list, mk):
    rb, cb = mk["rb"], mk["cb"]
    ts = [jnp.where(mk["eye"], 1.0, 0.0) - jnp.where((rb >> 1) == (cb >> 1), a, 0.0) for a in a_list]
    for lvl in range(1, LOG_CHUNK):
        rs, cs = rb >> lvl, cb >> lvl
        sel = ((rs & 1) == 1) & (cs == rs - 1)
        inner = [_dot3(t, jnp.where(sel, a, 0.0), "nn") for t, a in zip(ts, a_list)]
        ts = [t - _dot3(i, t, "nn") for i, t in zip(inner, ts)]
    return ts


def _stack_heads(ref, grp):
    return jnp.concatenate([ref[:, (grp * HEAD_GROUP + j) * A_DK:(grp * HEAD_GROUP + j + 1) * A_DK]
                            for j in range(HEAD_GROUP)], axis=0)


def _stack_cols(tile, grp):
    return jnp.concatenate([tile[:, grp * HEAD_GROUP + j:grp * HEAD_GROUP + j + 1] for j in range(HEAD_GROUP)], axis=0)


def _delta_local(q, k, v, beta, g, mk):
    causal, strict, eye = mk["causal"], mk["strict"], mk["eye"]
    g_row = _col_to_row(g, eye)
    gc = jnp.sum(jnp.where(causal, g_row, 0.0), axis=1, keepdims=True)
    gc_row = _col_to_row(gc, eye)
    decay = jnp.where(causal, jnp.exp(jnp.where(causal, gc - gc_row, 0.0)), 0.0)
    gam = jnp.exp(gc)
    kb = k * beta
    vb = v * beta
    y = kb * gam
    a = jnp.where(strict, _dot1(kb, k, "nt") * decay, 0.0)
    p = _dot1(q, k, "nt") * decay
    gl = jnp.sum(jnp.where(mk["last"], gc_row, 0.0), axis=1, keepdims=True)
    kd = k * jnp.exp(gl - gc)
    return dict(gc=gc, decay=decay, gam=gam, kb=kb, vb=vb, y=y, a=a, p=p, gl=gl, kd=kd)


def _head_rows(x, j):
    return x[j * CHUNK:(j + 1) * CHUNK]


def delta_fwd(q, k, v, beta, g):
    S = q.shape[0]
    n_chunks = S // CHUNK

    def body(q_ref, k_ref, v_ref, beta_ref, g_ref, o_ref, sprev_ref, t_ref, state_ref):
        @pl.when(pl.program_id(0) == 0)
        def _():
            state_ref[...] = jnp.zeros_like(state_ref)

        mk = _tri_masks()
        betav, gv = beta_ref[...], g_ref[...]
        groups = range(N_HEAD_GROUPS)
        q_all = [_stack_heads(q_ref, grp) for grp in groups]
        locs = [_delta_local(q_all[grp], _stack_heads(k_ref, grp), _stack_heads(v_ref, grp),
                             _stack_cols(betav, grp), _stack_cols(gv, grp), mk) for grp in groups]
        tinvs = _tri_inv([loc["a"] for loc in locs], mk)
        uws = [_dot3(tinvs[grp], jnp.concatenate([locs[grp]["vb"], locs[grp]["y"]], axis=1), "nn") for grp in groups]
        for grp in groups:
            loc, uw = locs[grp], uws[grp]
            t_ref[0, grp] = tinvs[grp]
            qg = q_all[grp] * loc["gam"]
            egl = jnp.exp(loc["gl"])
            vns, o_state = [], []
            for j in range(HEAD_GROUP):
                h = grp * HEAD_GROUP + j
                s0 = state_ref[h]
                sprev_ref[0, h] = s0
                uw_h = _head_rows(uw, j)
                vn = uw_h[:, :A_DK] - _dot1(uw_h[:, A_DK:], s0, "nn")
                vns.append(vn)
                o_state.append(_dot1(_head_rows(qg, j), s0, "nn"))
                state_ref[h] = s0 * egl[(j + 1) * CHUNK - 1:(j + 1) * CHUNK] + _dot1(_head_rows(loc["kd"], j), vn, "tn")
            o_local = _dot1(loc["p"], jnp.concatenate(vns, axis=0), "nn")
            for j in range(HEAD_GROUP):
                h = grp * HEAD_GROUP + j
                o_ref[:, h * A_DK:(h + 1) * A_DK] = o_state[j] + _head_rows(o_local, j)

    tile = pl.BlockSpec((CHUNK, A_W), lambda n: (n, 0))
    small = pl.BlockSpec((CHUNK, A_HEADS), lambda n: (n, 0))
    return pl.pallas_call(
        body, name="delta_fwd", grid=(n_chunks,), in_specs=[tile, tile, tile, small, small],
        out_specs=[tile, pl.BlockSpec((1, A_HEADS, A_DK, A_DK), lambda n: (n, 0, 0, 0)),
                   pl.BlockSpec((1, N_HEAD_GROUPS, GROUP_ROWS, GROUP_ROWS), lambda n: (n, 0, 0, 0))],
        out_shape=[jax.ShapeDtypeStruct((S, A_W), F32), jax.ShapeDtypeStruct((n_chunks, A_HEADS, A_DK, A_DK), F32),
                   jax.ShapeDtypeStruct((n_chunks, N_HEAD_GROUPS, GROUP_ROWS, GROUP_ROWS), F32)],
        scratch_shapes=[pltpu.VMEM((A_HEADS, A_DK, A_DK), F32)],
        compiler_params=_cparams(("arbitrary",)),
    )(q, k, v, beta, g)


def gate_a_fwd(o_pre, z, norm_w):
    S = o_pre.shape[0]

    def body(o_ref, z_ref, nw_ref, out_ref):
        nw = nw_ref[...]
        for h in range(A_HEADS):
            sl = slice(h * A_DK, (h + 1) * A_DK)
            oh = o_ref[:, sl]
            r = lax.rsqrt(jnp.mean(oh * oh, axis=-1, keepdims=True) + RMS_EPS)
            out_ref[:, sl] = (oh * r * nw * _silu(z_ref[:, sl].astype(F32))).astype(BF16)

    return rowcall(body, name="gate_a_fwd", S=S, ts=512, ins=[(o_pre, "row"), (z, "row"), (norm_w, "vec")],
                   outs=[((S, A_W), BF16, "row")])[0]


HEADS_PER_GROUP = 2
GROUP_W = HEADS_PER_GROUP * B_DH
N_GROUPS = B_HEADS // HEADS_PER_GROUP
PAD_ROWS = B_PREV * CHUNK


Q_TILE = 256
Q_CHUNKS = Q_TILE // CHUNK
KEY_WIN = (B_PREV + Q_CHUNKS) * CHUNK


def _band_probs(qh, kh, bias, valid):
    s = _dot1(qh, kh, "nt") * (B_DH ** -0.5) + bias
    s = jnp.where(valid, s, NEG_INF)
    e = jnp.exp(s - jnp.max(s, axis=-1, keepdims=True))
    return e * (1.0 / jnp.sum(e, axis=-1, keepdims=True))


def _attn_specs(S, tile_rows):
    n_cb = B_W // GROUP_W
    return [pl.BlockSpec((tile_rows, GROUP_W), lambda g, n: (n + PAD_ROWS // tile_rows, g)),
            pl.BlockSpec((PAD_ROWS + S, GROUP_W), lambda g, n: (0, n_cb + g)),
            pl.BlockSpec((PAD_ROWS + S, GROUP_W), lambda g, n: (0, 2 * n_cb + g)),
            pl.BlockSpec((HEADS_PER_GROUP, CHUNK, B_BAND), lambda g, n: (g, 0, 0))]


def _band_valid(first_chunk):
    return lax.broadcasted_iota(jnp.int32, (CHUNK, B_BAND), 1) >= PAD_ROWS - first_chunk * CHUNK


def _chunk_rows(x, qc, rows=CHUNK):
    return x[qc * CHUNK:qc * CHUNK + rows]


FWD_TILE = 512
FWD_CHUNKS = FWD_TILE // CHUNK
FWD_WIN = (B_PREV + FWD_CHUNKS) * CHUNK


def attn_fwd(qkv_pad, bias):
    S = qkv_pad.shape[0] - PAD_ROWS

    def body(q_ref, k_ref, v_ref, b_ref, o_ref):
        n = pl.program_id(1)
        start = pl.multiple_of(n * FWD_TILE, FWD_TILE)
        kwin = k_ref[pl.ds(start, FWD_WIN), :]
        vwin = v_ref[pl.ds(start, FWD_WIN), :]
        qv = q_ref[...]
        pairs = [(qc, hh) for qc in range(FWD_CHUNKS) for hh in range(HEADS_PER_GROUP)]
        sl = lambda hh: slice(hh * B_DH, (hh + 1) * B_DH)
        s = [_dot1(_chunk_rows(qv, qc)[:, sl(hh)], _chunk_rows(kwin, qc, B_BAND)[:, sl(hh)], "nt") for qc, hh in pairs]
        s = [jnp.where(_band_valid(n * FWD_CHUNKS + qc), x * (B_DH ** -0.5) + b_ref[hh], NEG_INF)
             for x, (qc, hh) in zip(s, pairs)]
        e = [jnp.exp(x - jnp.max(x, axis=-1, keepdims=True)) for x in s]
        p = [x * (1.0 / jnp.sum(x, axis=-1, keepdims=True)) for x in e]
        o = [_dot1(x, _chunk_rows(vwin, qc, B_BAND)[:, sl(hh)], "nn") for x, (qc, hh) in zip(p, pairs)]
        rows = [jnp.concatenate(o[qc * HEADS_PER_GROUP:(qc + 1) * HEADS_PER_GROUP], axis=1) for qc in range(FWD_CHUNKS)]
        o_ref[...] = jnp.concatenate(rows, axis=0).astype(BF16)

    return pl.pallas_call(
        body, name="attn_fwd", grid=(N_GROUPS, S // FWD_TILE), in_specs=_attn_specs(S, FWD_TILE),
        out_specs=pl.BlockSpec((FWD_TILE, GROUP_W), lambda g, n: (n, g)),
        out_shape=jax.ShapeDtypeStruct((S, B_W), BF16),
        compiler_params=_cparams(("parallel", "arbitrary")),
    )(qkv_pad, qkv_pad, qkv_pad, bias)


EXT = B_BAND + CHUNK


def bias_expand(rel_bias):
    def body(rev_ref, o_ref):
        rev = rev_ref[...]
        erev = jnp.concatenate([jnp.broadcast_to(rev[:, 0:1], (B_HEADS, EXT - B_REL)), rev], axis=1)
        for i in range(CHUNK):
            o_ref[i] = erev[:, CHUNK - i:CHUNK - i + B_BAND]

    return pl.pallas_call(
        body, name="bias_expand", in_specs=[WHOLE_VMEM], out_specs=WHOLE_VMEM,
        out_shape=jax.ShapeDtypeStruct((CHUNK, B_HEADS, B_BAND), F32),
    )(jnp.flip(rel_bias, axis=1))


def bias_reduce(dbias):
    def body(d_ref, o_ref):
        acc = jnp.zeros((B_HEADS, EXT), F32)
        for i in range(CHUNK):
            acc = acc + jnp.pad(d_ref[i], ((0, 0), (CHUNK - i, i)))
        tail = acc[:, EXT - B_REL:]
        clipped = jnp.sum(acc[:, :EXT - B_REL], axis=1, keepdims=True)
        lane = lax.broadcasted_iota(jnp.int32, (B_HEADS, B_REL), 1)
        o_ref[...] = jnp.where(lane == 0, tail + clipped, tail)

    rev = pl.pallas_call(body, name="bias_reduce", in_specs=[WHOLE_VMEM], out_specs=WHOLE_VMEM,
                         out_shape=jax.ShapeDtypeStruct((B_HEADS, B_REL), F32))(dbias)
    return jnp.flip(rev, axis=1)


def merge_fwd(gates_raw, b_gate, ya, yb):
    S = ya.shape[0]

    def body(g_ref, b_ref, ya_ref, yb_ref, o_ref):
        gt = _sigmoid(g_ref[...].astype(F32) + b_ref[...])
        o_ref[...] = (gt[:, :D_MODEL] * ya_ref[...].astype(F32) + gt[:, D_MODEL:] * yb_ref[...].astype(F32)).astype(BF16)

    return rowcall(body, name="merge_fwd", S=S, ts=512,
                   ins=[(gates_raw, "row"), (b_gate, "vec"), (ya, "row"), (yb, "row")],
                   outs=[((S, D_MODEL), BF16, "row")])[0]


def _ln_stats(xpre):
    mu = jnp.mean(xpre, axis=-1, keepdims=True)
    xc = xpre - mu
    rstd = lax.rsqrt(jnp.mean(xc * xc, axis=-1, keepdims=True) + LN_EPS)
    return xc * rstd, rstd


def ln1_fwd(x, mix, mod, ln_g, ln_b):
    S = x.shape[0]

    def body(x_ref, mix_ref, m_ref, g_ref, b_ref, xpre_ref, x1_ref, h2_ref):
        m = m_ref[...]
        xpre = ALPHA * x_ref[...] + m[GATE_T:GATE_T + 1] * mix_ref[...]
        xhat, _ = _ln_stats(xpre)
        x1 = xhat * g_ref[...] + b_ref[...]
        xpre_ref[...] = xpre
        x1_ref[...] = x1
        h2_ref[...] = (x1 * (1.0 + m[SCALE_F:SCALE_F + 1]) + m[SHIFT_F:SHIFT_F + 1]).astype(BF16)

    return rowcall(body, name="ln1_fwd", S=S, ts=512,
                   ins=[(x, "row"), (mix, "row"), (mod, "vec"), (ln_g, "vec"), (ln_b, "vec")],
                   outs=[((S, D_MODEL), F32, "row"), ((S, D_MODEL), F32, "row"), ((S, D_MODEL), BF16, "row")])


STRIP_ROWS = 32
STRIP_COLS = 256


def ffn_act_fwd(up, conv_w, conv_b):
    S = up.shape[0]
    ts = 256

    def body(u_ref, up_ref, w_ref, b_ref, o_ref, ubuf):
        ubuf[0:8] = _halo_prev(up_ref) * (pl.program_id(0) > 0).astype(F32)
        ubuf[8:8 + ts] = u_ref[...].astype(F32)

        def col_block(j, carry):
            gate = pl.ds(pl.multiple_of(j * STRIP_COLS, STRIP_COLS), STRIP_COLS)
            halves = [gate, pl.ds(pl.multiple_of(D_FF + j * STRIP_COLS, STRIP_COLS), STRIP_COLS)]
            w = [w_ref[:, c] for c in halves]
            bias = [b_ref[:, c] for c in halves]
            for r0 in range(0, ts, STRIP_ROWS):
                uc = []
                for h in range(2):
                    x = ubuf[r0:r0 + STRIP_ROWS + 8, halves[h]]
                    uc.append(bias[h] + sum(
                        w[h][t:t + 1] * (x if t == FFN_CONV - 1 else pltpu.roll(x, FFN_CONV - 1 - t, axis=0))[8:]
                        for t in range(FFN_CONV)))
                o_ref[r0:r0 + STRIP_ROWS, gate] = (_silu(uc[0]) * uc[1]).astype(BF16)
            return carry

        lax.fori_loop(0, D_FF // STRIP_COLS, col_block, 0)

    return rowcall(body, name="ffn_act_fwd", S=S, ts=ts,
                   ins=[(up, "row"), (up, "prev"), (conv_w, "vec"), (conv_b, "vec")],
                   outs=[((S, D_FF), BF16, "row")], scratch=[pltpu.VMEM((ts + 8, 2 * D_FF), F32)])[0]


def final_fwd_bwd(x1, ffn, target, mod, ln_g, ln_b):
    S = x1.shape[0]

    def body(x1_ref, f_ref, t_ref, m_ref, g_ref, b_ref, dxpre_ref, dffn_ref, loss_ref, dgate_ref, dg_ref, db_ref):
        gate = m_ref[...][GATE_F:GATE_F + 1]
        ffn_v = f_ref[...]
        xpre = ALPHA * x1_ref[...] + gate * ffn_v
        xhat, rstd = _ln_stats(xpre)
        err = xhat * g_ref[...] + b_ref[...] - t_ref[...]
        loss_ref[...] += 0.5 * jnp.sum(jnp.mean(err * err, axis=-1, keepdims=True), axis=0, keepdims=True)
        dy = err * (1.0 / D_MODEL)
        dg_ref[...] += jnp.sum(dy * xhat, axis=0, keepdims=True)
        db_ref[...] += jnp.sum(dy, axis=0, keepdims=True)
        dyg = dy * g_ref[...]
        dxpre = rstd * (dyg - jnp.mean(dyg, axis=-1, keepdims=True) - xhat * jnp.mean(dyg * xhat, axis=-1, keepdims=True))
        dxpre_ref[...] = dxpre
        dffn_ref[...] = (gate * dxpre).astype(BF16)
        dgate_ref[...] += jnp.sum(dxpre * ffn_v, axis=0, keepdims=True)

    vec = ((1, D_MODEL), F32, "acc")
    return rowcall(body, name="final_fwd_bwd", S=S, ts=512,
                   ins=[(x1, "row"), (ffn, "row"), (target, "row"), (mod, "vec"), (ln_g, "vec"), (ln_b, "vec")],
                   outs=[((S, D_MODEL), F32, "row"), ((S, D_MODEL), BF16, "row"), ((1, 1), F32, "acc"), vec, vec, vec])


def ffn_act_bwd(dact, up, conv_w, conv_b):
    S = up.shape[0]
    ts = 256
    win_u, win_d = STRIP_ROWS + 16, STRIP_ROWS + 8

    def body(d_ref, dn_ref, u_ref, up_ref, un_ref, w_ref, b_ref, dup_ref, dw_ref, db_ref, ubuf, dbuf):
        i = pl.program_id(0)
        ubuf[0:8] = _halo_prev(up_ref) * (i > 0).astype(F32)
        ubuf[8:8 + ts] = u_ref[...].astype(F32)
        ubuf[8 + ts:16 + ts] = _halo_next(un_ref)
        dbuf[0:ts] = d_ref[...].astype(F32)
        dbuf[ts:ts + 8] = _halo_next(dn_ref) * (i < pl.num_programs(0) - 1).astype(F32)

        def col_block(j, carry):
            halves = [pl.ds(pl.multiple_of(j * STRIP_COLS, STRIP_COLS), STRIP_COLS),
                      pl.ds(pl.multiple_of(D_FF + j * STRIP_COLS, STRIP_COLS), STRIP_COLS)]
            w = [w_ref[:, c] for c in halves]
            bias = [b_ref[:, c] for c in halves]
            dw_acc = [[jnp.zeros((1, STRIP_COLS), F32) for _ in range(FFN_CONV)] for _ in halves]
            db_acc = [jnp.zeros((1, STRIP_COLS), F32) for _ in halves]
            for r0 in range(0, ts, STRIP_ROWS):
                shifted = [[x if k == 0 else pltpu.roll(x, k, axis=0) for k in range(FFN_CONV)]
                           for x in (ubuf[r0:r0 + win_u, c] for c in halves)]
                uc = [bias[h] + sum(w[h][t:t + 1] * shifted[h][FFN_CONV - 1 - t][8:8 + win_d] for t in range(FFN_CONV))
                      for h in range(2)]
                dact_w = dbuf[r0:r0 + win_d, halves[0]]
                sg, dsg = _silu_and_grad(uc[0])
                duc = [dact_w * uc[1] * dsg, dact_w * sg]
                for h in range(2):
                    dup = duc[h] * w[h][FFN_CONV - 1:FFN_CONV]
                    for t in range(FFN_CONV - 1):
                        dup = dup + pltpu.roll(duc[h], win_d - (FFN_CONV - 1 - t), axis=0) * w[h][t:t + 1]
                    dup_ref[r0:r0 + STRIP_ROWS, halves[h]] = dup[:STRIP_ROWS].astype(BF16)
                    mine = duc[h][:STRIP_ROWS]
                    db_acc[h] = db_acc[h] + jnp.sum(mine, axis=0, keepdims=True)
                    for t in range(FFN_CONV):
                        dw_acc[h][t] = dw_acc[h][t] + jnp.sum(
                            mine * shifted[h][FFN_CONV - 1 - t][8:8 + STRIP_ROWS], axis=0, keepdims=True)
            for h in range(2):
                dw_ref[:, halves[h]] += jnp.concatenate(dw_acc[h], axis=0)
                db_ref[:, halves[h]] += db_acc[h]
            return carry

        lax.fori_loop(0, D_FF // STRIP_COLS, col_block, 0)

    return rowcall(body, name="ffn_act_bwd", S=S, ts=ts,
                   ins=[(dact, "row"), (dact, "next"), (up, "row"), (up, "prev"), (up, "next"), (conv_w, "vec"), (conv_b, "vec")],
                   outs=[((S, 2 * D_FF), BF16, "row"), ((FFN_CONV, 2 * D_FF), F32, "acc"), ((1, 2 * D_FF), F32, "acc")],
                   scratch=[pltpu.VMEM((ts + 16, 2 * D_FF), F32), pltpu.VMEM((ts + 8, D_FF), F32)])


def ln1_bwd(dxpre2, dh2, xpre1, mix, mod, ln_g, ln_b):
    S = xpre1.shape[0]

    def body(d2_ref, dh_ref, xp_ref, mix_ref, m_ref, g_ref, b_ref, dxpre_ref, dmix_ref,
             dscale_ref, dshift_ref, dgate_ref, dg_ref, db_ref):
        m = m_ref[...]
        xhat, rstd = _ln_stats(xp_ref[...])
        x1 = xhat * g_ref[...] + b_ref[...]
        dh = dh_ref[...]
        dx1 = ALPHA * d2_ref[...] + dh * (1.0 + m[SCALE_F:SCALE_F + 1])
        dscale_ref[...] += jnp.sum(dh * x1, axis=0, keepdims=True)
        dshift_ref[...] += jnp.sum(dh, axis=0, keepdims=True)
        dg_ref[...] += jnp.sum(dx1 * xhat, axis=0, keepdims=True)
        db_ref[...] += jnp.sum(dx1, axis=0, keepdims=True)
        dyg = dx1 * g_ref[...]
        dxpre = rstd * (dyg - jnp.mean(dyg, axis=-1, keepdims=True) - xhat * jnp.mean(dyg * xhat, axis=-1, keepdims=True))
        dxpre_ref[...] = dxpre
        dmix_ref[...] = (m[GATE_T:GATE_T + 1] * dxpre).astype(BF16)
        dgate_ref[...] += jnp.sum(dxpre * mix_ref[...], axis=0, keepdims=True)

    vec = ((1, D_MODEL), F32, "acc")
    return rowcall(body, name="ln1_bwd", S=S, ts=512,
                   ins=[(dxpre2, "row"), (dh2, "row"), (xpre1, "row"), (mix, "row"), (mod, "vec"), (ln_g, "vec"), (ln_b, "vec")],
                   outs=[((S, D_MODEL), F32, "row"), ((S, D_MODEL), BF16, "row"), vec, vec, vec, vec, vec])


def merge_bwd(dmerged, gates_raw, b_gate, ya, yb):
    S = ya.shape[0]

    def body(d_ref, g_ref, b_ref, ya_ref, yb_ref, dya_ref, dyb_ref, dg_ref, dbg_ref):
        gt = _sigmoid(g_ref[...].astype(F32) + b_ref[...])
        d = d_ref[...].astype(F32)
        ga, gb = gt[:, :D_MODEL], gt[:, D_MODEL:]
        dya_ref[...] = (d * ga).astype(BF16)
        dyb_ref[...] = (d * gb).astype(BF16)
        dgr = jnp.concatenate([d * ya_ref[...].astype(F32) * ga * (1.0 - ga),
                               d * yb_ref[...].astype(F32) * gb * (1.0 - gb)], axis=1)
        dg_ref[...] = dgr.astype(BF16)
        dbg_ref[...] += jnp.sum(dgr, axis=0, keepdims=True)

    return rowcall(body, name="merge_bwd", S=S, ts=512,
                   ins=[(dmerged, "row"), (gates_raw, "row"), (b_gate, "vec"), (ya, "row"), (yb, "row")],
                   outs=[((S, D_MODEL), BF16, "row"), ((S, D_MODEL), BF16, "row"), ((S, 2 * D_MODEL), BF16, "row"),
                         ((1, 2 * D_MODEL), F32, "acc")])


def attn_bwd(qkv_pad, bias, do_b):
    S = qkv_pad.shape[0] - PAD_ROWS

    def body(q_ref, k_ref, v_ref, bias_ref, do_ref, dq_ref, dk_ref, dv_ref, db_ref, b_ref):
        n = pl.program_id(1)

        @pl.when(n == 0)
        def _():
            dk_ref[...] = jnp.zeros_like(dk_ref)
            dv_ref[...] = jnp.zeros_like(dv_ref)
            db_ref[...] = jnp.zeros_like(db_ref)
            b_ref[...] = jnp.full(b_ref.shape, NEG_INF, F32)
            for hh in range(HEADS_PER_GROUP):
                for qc in range(Q_CHUNKS):
                    b_ref[hh, qc * CHUNK:(qc + 1) * CHUNK, qc * CHUNK:qc * CHUNK + B_BAND] = bias_ref[hh]

        start = pl.multiple_of(n * Q_TILE, Q_TILE)
        kwin = k_ref[pl.ds(start, KEY_WIN), :]
        vwin = v_ref[pl.ds(start, KEY_WIN), :]
        qv, dov = q_ref[...], do_ref[...]
        valid = lax.broadcasted_iota(jnp.int32, (Q_TILE, KEY_WIN), 1) >= PAD_ROWS - n * Q_TILE
        dqs, dks, dvs = [], [], []
        for hh in range(HEADS_PER_GROUP):
            sl = slice(hh * B_DH, (hh + 1) * B_DH)
            p = _band_probs(qv[:, sl], kwin[:, sl], b_ref[hh], valid)
            dp = _dot1(dov[:, sl], vwin[:, sl], "nt")
            ds = p * (dp - jnp.sum(dp * p, axis=-1, keepdims=True))
            dbh = ds[0:CHUNK, 0:B_BAND]
            for qc in range(1, Q_CHUNKS):
                dbh = dbh + ds[qc * CHUNK:(qc + 1) * CHUNK, qc * CHUNK:qc * CHUNK + B_BAND]
            db_ref[hh] += dbh
            dsq = ds * (B_DH ** -0.5)
            dqs.append(_dot1(dsq, kwin[:, sl], "nn"))
            dks.append(_dot1(dsq, qv[:, sl], "tn"))
            dvs.append(_dot1(p, dov[:, sl], "tn"))
        dq_ref[...] = jnp.concatenate(dqs, axis=1).astype(BF16)
        dk_ref[pl.ds(start, KEY_WIN), :] += jnp.concatenate(dks, axis=1)
        dv_ref[pl.ds(start, KEY_WIN), :] += jnp.concatenate(dvs, axis=1)

    col = pl.BlockSpec((PAD_ROWS + S, GROUP_W), lambda g, n: (0, g))
    tile = pl.BlockSpec((Q_TILE, GROUP_W), lambda g, n: (n, g))
    return pl.pallas_call(
        body, name="attn_bwd", grid=(N_GROUPS, S // Q_TILE), in_specs=_attn_specs(S, Q_TILE) + [tile],
        out_specs=[tile, col, col, pl.BlockSpec((HEADS_PER_GROUP, CHUNK, B_BAND), lambda g, n: (g, 0, 0))],
        out_shape=[jax.ShapeDtypeStruct((S, B_W), BF16), jax.ShapeDtypeStruct((PAD_ROWS + S, B_W), F32),
                   jax.ShapeDtypeStruct((PAD_ROWS + S, B_W), F32), jax.ShapeDtypeStruct((B_HEADS, CHUNK, B_BAND), F32)],
        scratch_shapes=[pltpu.VMEM((HEADS_PER_GROUP, Q_TILE, KEY_WIN), F32)],
        compiler_params=_cparams(("parallel", "arbitrary")),
    )(qkv_pad, qkv_pad, qkv_pad, bias, do_b)


def gate_a_bwd(do_a, o_pre, z, norm_w):
    S = o_pre.shape[0]

    def body(d_ref, o_ref, z_ref, nw_ref, dop_ref, dz_ref, dnw_ref):
        nw = nw_ref[...]
        acc = jnp.zeros((1, A_DK), F32)
        for h in range(A_HEADS):
            sl = slice(h * A_DK, (h + 1) * A_DK)
            oh, zh, dh = o_ref[:, sl], z_ref[:, sl].astype(F32), d_ref[:, sl].astype(F32)
            r = lax.rsqrt(jnp.mean(oh * oh, axis=-1, keepdims=True) + RMS_EPS)
            sz, dsz = _silu_and_grad(zh)
            dz_ref[:, sl] = (dh * oh * r * nw * dsz).astype(BF16)
            acc = acc + jnp.sum(dh * oh * r * sz, axis=0, keepdims=True)
            t = dh * nw * sz
            dop_ref[:, sl] = r * t - oh * (r * r * r) * jnp.mean(t * oh, axis=-1, keepdims=True)
        dnw_ref[...] += acc

    return rowcall(body, name="gate_a_bwd", S=S, ts=512,
                   ins=[(do_a, "row"), (o_pre, "row"), (z, "row"), (norm_w, "vec")],
                   outs=[((S, A_W), F32, "row"), ((S, A_W), BF16, "row"), ((1, A_DK), F32, "acc")])


def delta_bwd(q, k, v, beta, g, sprev, tinv, do):
    S = q.shape[0]
    n_chunks = S // CHUNK

    def body(q_ref, k_ref, v_ref, beta_ref, g_ref, sprev_ref, t_ref, do_ref,
             dq_ref, dk_ref, dv_ref, dbeta_ref, dg_ref, dstate_ref):
        @pl.when(pl.program_id(0) == 0)
        def _():
            dstate_ref[...] = jnp.zeros_like(dstate_ref)

        mk = _tri_masks()
        causal, strict, eye = mk["causal"], mk["strict"], mk["eye"]
        blk_end = (lax.broadcasted_iota(jnp.int32, (GROUP_ROWS, 1), 0) & (CHUNK - 1)) == CHUNK - 1
        lane = lax.broadcasted_iota(jnp.int32, (CHUNK, A_HEADS), 1)
        betav, gv = beta_ref[...], g_ref[...]
        dbeta_t = jnp.zeros((CHUNK, A_HEADS), F32)
        dg_t = jnp.zeros((CHUNK, A_HEADS), F32)
        groups, heads = range(N_HEAD_GROUPS), range(HEAD_GROUP)
        st = [dict() for _ in groups]

        def local_part(grp, s):
            s["qs"], s["ks"], s["vs"] = _stack_heads(q_ref, grp), _stack_heads(k_ref, grp), _stack_heads(v_ref, grp)
            s["dos"] = _stack_heads(do_ref, grp)
            s["bs"] = _stack_cols(betav, grp)
            s["loc"] = loc = _delta_local(s["qs"], s["ks"], s["vs"], s["bs"], _stack_cols(gv, grp), mk)
            s["tinv"] = t_ref[0, grp]
            s["rhs"] = jnp.concatenate([loc["vb"], loc["y"]], axis=1)
            s["uw"] = _dot3(s["tinv"], s["rhs"], "nn")

        def state_part(grp, s):
            loc, uw, dos, qs = s["loc"], s["uw"], s["dos"], s["qs"]
            gam, kd, gl, gc = loc["gam"], loc["kd"], loc["gl"], loc["gc"]
            qg = qs * gam
            egl = jnp.exp(gl)
            hid = [grp * HEAD_GROUP + j for j in heads]
            s0 = [sprev_ref[0, h] for h in hid]
            ds1 = [dstate_ref[h] for h in hid]
            w = [_head_rows(uw, j)[:, A_DK:] for j in heads]
            vn = [_head_rows(uw, j)[:, :A_DK] - _dot1(w[j], s0[j], "nn") for j in heads]
            vns = jnp.concatenate(vn, axis=0)
            dvn_local = _dot1(loc["p"], dos, "tn")
            dvn = [_head_rows(dvn_local, j) + _dot1(_head_rows(kd, j), ds1[j], "nn") for j in heads]
            dvns = jnp.concatenate(dvn, axis=0)
            s["dp"] = jnp.where(causal, _dot1(dos, vns, "nt"), 0.0)
            dqg = jnp.concatenate([_dot1(_head_rows(dos, j), s0[j], "nt") for j in heads], axis=0)
            s["dq"] = dqg * gam
            dgc = jnp.sum(dqg * qg, axis=-1, keepdims=True)
            for j in heads:
                dstate_ref[hid[j]] = (_dot1(_head_rows(qg, j), _head_rows(dos, j), "tn")
                                      + egl[(j + 1) * CHUNK - 1:(j + 1) * CHUNK] * ds1[j] - _dot1(w[j], dvn[j], "tn"))
            dkd = jnp.concatenate([_dot1(vn[j], ds1[j], "nt") for j in heads], axis=0)
            s["dk"] = dkd * jnp.exp(gl - gc)
            t1 = jnp.sum(dkd * kd, axis=-1, keepdims=True)
            dgl = jnp.concatenate(
                [jnp.broadcast_to(jnp.sum(_head_rows(t1, j), axis=0, keepdims=True)
                                  + jnp.sum(jnp.sum(ds1[j] * s0[j], axis=-1, keepdims=True), axis=0, keepdims=True)
                                  * egl[(j + 1) * CHUNK - 1:(j + 1) * CHUNK], (CHUNK, 1)) for j in heads], axis=0)
            s["dgc"] = dgc - t1 + jnp.where(blk_end, dgl, 0.0)
            s["duw"] = jnp.concatenate(
                [dvns, jnp.concatenate([-_dot1(dvn[j], s0[j], "nt") for j in heads], axis=0)], axis=1)

        def solve_part(grp, s):
            s["dvby"] = _dot3(s["tinv"], s["duw"], "tn")
            s["dt"] = _dot3(s["duw"], s["rhs"], "nt")

        def inverse_part_a(grp, s):
            s["tdt"] = _dot3(s["tinv"], s["dt"], "tn")

        def inverse_part_b(grp, s):
            s["da"] = jnp.where(strict, -_dot3(s["tdt"], s["tinv"], "nt"), 0.0)

        def finish(grp, s):
            loc, qs, ks, vs, bs, da, dp, dvby = s["loc"], s["qs"], s["ks"], s["vs"], s["bs"], s["da"], s["dp"], s["dvby"]
            gam, decay = loc["gam"], loc["decay"]
            dm = da * decay
            dn = dp * decay
            e = da * loc["a"] + dp * loc["p"]
            dgc = s["dgc"] + jnp.sum(e, axis=1, keepdims=True) - _row_to_col(jnp.sum(e, axis=0, keepdims=True), eye)
            dy = dvby[:, A_DK:]
            dvb = dvby[:, :A_DK]
            dkb = _dot1(dm, ks, "nn") + dy * gam
            dk = s["dk"] + _dot1(dm, loc["kb"], "tn") + _dot1(dn, qs, "tn") + dkb * bs
            dq = s["dq"] + _dot1(dn, ks, "nn")
            dgc = dgc + jnp.sum(dy * loc["y"], axis=-1, keepdims=True)
            dbeta = jnp.sum(dkb * ks, axis=-1, keepdims=True) + jnp.sum(dvb * vs, axis=-1, keepdims=True)
            dv = dvb * bs
            dgs = jnp.sum(jnp.where(mk["upper"], _col_to_row(dgc, eye), 0.0), axis=1, keepdims=True)
            for j in heads:
                h = grp * HEAD_GROUP + j
                sl = slice(h * A_DK, (h + 1) * A_DK)
                dq_ref[:, sl] = _head_rows(dq, j)
                dk_ref[:, sl] = _head_rows(dk, j)
                dv_ref[:, sl] = _head_rows(dv, j)
            s["dbeta"], s["dgs"] = dbeta, dgs

        for stage in (local_part, state_part, solve_part, inverse_part_a, inverse_part_b, finish):
            for grp in groups:
                stage(grp, st[grp])
        for grp in groups:
            for j in heads:
                h = grp * HEAD_GROUP + j
                dbeta_t = dbeta_t + jnp.where(lane == h, _head_rows(st[grp]["dbeta"], j), 0.0)
                dg_t = dg_t + jnp.where(lane == h, _head_rows(st[grp]["dgs"], j), 0.0)
        dbeta_ref[...] = dbeta_t
        dg_ref[...] = dg_t

    rev = lambda n: (n_chunks - 1 - n, 0)
    rev4 = lambda n: (n_chunks - 1 - n, 0, 0, 0)
    tile = pl.BlockSpec((CHUNK, A_W), rev)
    small = pl.BlockSpec((CHUNK, A_HEADS), rev)
    return pl.pallas_call(
        body, name="delta_bwd", grid=(n_chunks,),
        in_specs=[tile, tile, tile, small, small, pl.BlockSpec((1, A_HEADS, A_DK, A_DK), rev4),
                  pl.BlockSpec((1, N_HEAD_GROUPS, GROUP_ROWS, GROUP_ROWS), rev4), tile],
        out_specs=[tile, tile, tile, small, small],
        out_shape=[jax.ShapeDtypeStruct((S, A_W), F32)] * 3 + [jax.ShapeDtypeStruct((S, A_HEADS), F32)] * 2,
        scratch_shapes=[pltpu.VMEM((A_HEADS, A_DK, A_DK), F32)],
        compiler_params=_cparams(("arbitrary",)),
    )(q, k, v, beta, g, sprev, tinv, do)


def _prep_a_dpre(raw, raw_prev, w, dq, dk, dv):
    y, dy_dpre = _prep_a_core(raw, raw_prev, w)
    parts = []
    for h in range(A_HEADS):
        yq = y[:, h * A_DK:(h + 1) * A_DK]
        dqh = dq[:, h * A_DK:(h + 1) * A_DK]
        rq = lax.rsqrt(jnp.sum(yq * yq, axis=-1, keepdims=True) + L2_EPS)
        parts.append((A_DK ** -0.5) * (rq * dqh - yq * (rq * rq * rq) * jnp.sum(dqh * yq, axis=-1, keepdims=True)))
    for h in range(A_HEADS):
        yk = y[:, A_W + h * A_DK:A_W + (h + 1) * A_DK]
        dkh = dk[:, h * A_DK:(h + 1) * A_DK]
        rk = lax.rsqrt(jnp.sum(yk * yk, axis=-1, keepdims=True) + L2_EPS)
        parts.append(rk * dkh - yk * (rk * rk * rk) * jnp.sum(dkh * yk, axis=-1, keepdims=True))
    parts.append(dv)
    return jnp.concatenate(parts, axis=1) * dy_dpre


def prep_a_bwd(qkvba, conv_a, a_log, dt_bias, dq, dk, dv, dbeta, dg):
    S = qkvba.shape[0]
    ts = 256

    def body(x_ref, xp_ref, xn_ref, w_ref, al_ref, dt_ref, dq_ref, dqn_ref, dk_ref, dkn_ref, dv_ref, dvn_ref,
             dbeta_ref, dg_ref, draw_ref, dw_ref, dal_ref, ddt_ref):
        i = pl.program_id(0)
        first = (i > 0).astype(F32)
        last = (i < pl.num_programs(0) - 1).astype(F32)
        w = w_ref[...]
        cur, prev = x_ref[:, :3 * A_W].astype(F32), _halo_prev(xp_ref)[:, :3 * A_W] * first
        dpre = _prep_a_dpre(cur, prev, w, dq_ref[...], dk_ref[...], dv_ref[...])
        dpre_n = _prep_a_dpre(_halo_next(xn_ref)[:, :3 * A_W], cur[ts - 8:ts], w, _halo_next(dqn_ref), _halo_next(dkn_ref),
                              _halo_next(dvn_ref)) * last
        for j in range(A_CONV):
            dw_ref[j:j + 1, :] += jnp.sum(dpre * _shift_down(cur, prev, A_CONV - 1 - j), axis=0, keepdims=True)
        draw = dpre * w[A_CONV - 1:A_CONV]
        for j in range(A_CONV - 1):
            draw = draw + _shift_up(dpre, dpre_n, A_CONV - 1 - j) * w[j:j + 1]
        draw_ref[:, :3 * A_W] = draw.astype(BF16)
        bav = x_ref[:, 3 * A_W:].astype(F32)
        beta = _sigmoid(bav[:, 0:A_HEADS])
        xa = bav[:, A_HEADS:2 * A_HEADS] + dt_ref[...]
        nexp = -jnp.exp(al_ref[...])
        dgv = dg_ref[...]
        da = dgv * nexp * _sigmoid(xa)
        draw_ref[:, 3 * A_W:] = jnp.concatenate(
            [dbeta_ref[...] * beta * (1.0 - beta), da, jnp.zeros((ts, BA_PAD - 2 * A_HEADS), F32)], axis=1).astype(BF16)
        dal_ref[...] += jnp.sum(dgv * nexp * _softplus(xa), axis=0, keepdims=True)
        ddt_ref[...] += jnp.sum(da, axis=0, keepdims=True)

    return rowcall(
        body, name="prep_a_bwd", S=S, ts=ts,
        ins=[(qkvba, "row"), (qkvba, "prev"), (qkvba, "next"), (conv_a, "vec"), (a_log, "vec"),
             (dt_bias, "vec"), (dq, "row"), (dq, "next"), (dk, "row"), (dk, "next"), (dv, "row"), (dv, "next"),
             (dbeta, "row"), (dg, "row")],
        outs=[((S, QKVBA_W), BF16, "row"), ((A_CONV, 3 * A_W), F32, "acc"),
              ((1, A_HEADS), F32, "acc"), ((1, A_HEADS), F32, "acc")])


def grad_x_final(dh1, x, dxpre1, mod):
    S = x.shape[0]

    def body(dh_ref, x_ref, dx_ref, m_ref, gx_ref, dscale_ref, dshift_ref):
        dh = dh_ref[...]
        gx_ref[...] = ALPHA * dx_ref[...] + dh * (1.0 + m_ref[...][SCALE_T:SCALE_T + 1])
        dscale_ref[...] += jnp.sum(dh * x_ref[...], axis=0, keepdims=True)
        dshift_ref[...] += jnp.sum(dh, axis=0, keepdims=True)

    vec = ((1, D_MODEL), F32, "acc")
    return rowcall(body, name="grad_x_final", S=S, ts=512, ins=[(dh1, "row"), (x, "row"), (dxpre1, "row"), (mod, "vec")],
                   outs=[((S, D_MODEL), F32, "row"), vec, vec])


_C_QKV, _C_Z, _C_BA, _C_QKVB, _C_G = 0, 3 * A_W, 4 * A_W, 4 * A_W + 2 * A_HEADS, 4 * A_W + 2 * A_HEADS + 3 * B_W
BA_PAD = 128
QKVBA_W = 3 * A_W + BA_PAD


def split_w_in(w_in):
    ba = jnp.pad(w_in[:, _C_BA:_C_QKVB], ((0, 0), (0, BA_PAD - 2 * A_HEADS)))
    return dict(qkvba=jnp.concatenate([w_in[:, _C_QKV:_C_Z], ba], axis=1), z=w_in[:, _C_Z:_C_BA],
                qkvb=w_in[:, _C_QKVB:_C_G], g=w_in[:, _C_G:])


def join_w_in(p):
    qkv, ba = p["qkvba"][:, :3 * A_W], p["qkvba"][:, 3 * A_W:3 * A_W + 2 * A_HEADS]
    return jnp.concatenate([qkv, p["z"], ba, p["qkvb"], p["g"]], axis=1)


def forward_local(x, target, mod, w, sm, late_weights=None):
    h1 = modulate(x, mod, SHIFT_T, SCALE_T, "mod_t")
    qkvba = mm(h1, w["qkvba"], mode="nn", out_dtype=BF16, name="proj_qkvba")
    z = mm(h1, w["z"], mode="nn", out_dtype=BF16, name="proj_z")
    qkvb = mm(h1, w["qkvb"], mode="nn", out_dtype=BF16, name="proj_qkvb")
    gates_raw = mm(h1, w["g"], mode="nn", out_dtype=BF16, name="proj_g")
    q, k, v, beta, g = prep_a_fwd(qkvba, sm["conv_a"], sm["a_log"], sm["dt_bias"])
    o_pre, sprev, tinv = delta_fwd(q, k, v, beta, g)
    o_a = gate_a_fwd(o_pre, z, sm["norm_a"])
    qkv_pad = jnp.pad(qkvb, ((PAD_ROWS, 0), (0, 0)))
    bias = jnp.transpose(bias_expand(sm["rel_bias"]), (1, 0, 2))
    o_b = attn_fwd(qkv_pad, bias)
    if late_weights is not None:
        w = dict(w, **late_weights(o_b))
    ya = mm(o_a, w["branch_a"], mode="nn", out_dtype=BF16, name="branch_a")
    yb = mm(o_b, w["branch_b"], mode="nn", out_dtype=BF16, name="branch_b")
    merged = merge_fwd(gates_raw, sm["b_gate"], ya, yb)
    mix = mm(merged, w["o"], mode="nn", out_dtype=F32, name="mix")
    xpre1, x1, h2 = ln1_fwd(x, mix, mod, sm["ln1_g"], sm["ln1_b"])
    up = mm(h2, w["up"], mode="nn", out_dtype=BF16, name="ffn_up", b_shards=True)
    act = ffn_act_fwd(up, sm["conv_ffn"], sm["b_conv_ffn"])
    ffn = mm(act, w["down"], mode="nn", out_dtype=F32, name="ffn_down")
    dxpre2, dffn, loss, dgate_f, dln2_g, dln2_b = final_fwd_bwd(x1, ffn, target, mod, sm["ln2_g"], sm["ln2_b"])
    saved = dict(h1=h1, qkvba=qkvba, z=z, gates_raw=gates_raw, q=q, k=k, v=v, beta=beta, g=g,
                 o_pre=o_pre, sprev=sprev, tinv=tinv, o_a=o_a, qkv_pad=qkv_pad, bias=bias, o_b=o_b, ya=ya, yb=yb,
                 merged=merged, mix=mix, xpre1=xpre1, x1=x1, h2=h2, up=up, act=act, ffn=ffn, w=w)
    return loss, dxpre2, dffn, dict(gate_f=dgate_f, ln2_g=dln2_g, ln2_b=dln2_b), saved


def backward_local(x, mod, sm, dxpre2, dffn, fin, sv, hooks=None):
    w = sv["w"]
    dact = mm(dffn, w["down"], mode="nt", out_dtype=BF16, name="d_act")
    gw_down = mm(sv["act"], dffn, mode="tn", out_dtype=BF16, name="gw_down")
    dup, dconv_ffn, db_conv_ffn = ffn_act_bwd(dact, sv["up"], sm["conv_ffn"], sm["b_conv_ffn"])
    dh2 = mm(dup, w["up"], mode="nt", out_dtype=F32, name="d_h2", b_shards=True)
    gw_up = mm(sv["h2"], dup, mode="tn", out_dtype=BF16, name="gw_up", out_shards=N_CHIPS)
    dxpre1, dmix, dsc_f, dsh_f, dgate_t, dln1_g, dln1_b = ln1_bwd(
        dxpre2, dh2, sv["xpre1"], sv["mix"], mod, sm["ln1_g"], sm["ln1_b"])
    dmerged = mm(dmix, w["o"], mode="nt", out_dtype=BF16, name="d_merged")
    gw_o = mm(sv["merged"], dmix, mode="tn", out_dtype=BF16, name="gw_o")
    dya, dyb, dgates, db_gate = merge_bwd(dmerged, sv["gates_raw"], sm["b_gate"], sv["ya"], sv["yb"])
    do_a = mm(dya, w["branch_a"], mode="nt", out_dtype=BF16, name="d_oa")
    gw_branch_a = mm(sv["o_a"], dya, mode="tn", out_dtype=BF16, name="gw_branch_a")
    do_b = mm(dyb, w["branch_b"], mode="nt", out_dtype=BF16, name="d_ob")
    gw_branch_b = mm(sv["o_b"], dyb, mode="tn", out_dtype=BF16, name="gw_branch_b")
    bias = sv["bias"]
    if hooks is not None:
        bias = bias + hooks["late_start"](dict(w_branch_a=gw_branch_a, w_branch_b=gw_branch_b, w_o=gw_o, w_up=gw_up,
                                               w_down=gw_down))[0, 0]
    dq_b, dk_pad, dv_pad, dbias = attn_bwd(sv["qkv_pad"], bias, do_b)
    if hooks is not None:
        dbias = dbias + hooks["late_finish"](dq_b)[0, 0]
    dqkvb = jnp.concatenate([dq_b, dk_pad[PAD_ROWS:].astype(BF16), dv_pad[PAD_ROWS:].astype(BF16)], axis=1)
    drel_bias = bias_reduce(jnp.transpose(dbias, (1, 0, 2)))
    do_pre, dz, dnorm_a = gate_a_bwd(do_a, sv["o_pre"], sv["z"], sm["norm_a"])
    dq, dk, dv, dbeta, dg = delta_bwd(sv["q"], sv["k"], sv["v"], sv["beta"], sv["g"], sv["sprev"], sv["tinv"], do_pre)
    dqkvba, dconv_a, da_log, ddt_bias = prep_a_bwd(
        sv["qkvba"], sm["conv_a"], sm["a_log"], sm["dt_bias"], dq, dk, dv, dbeta, dg)
    pieces = dict(qkvba=dqkvba, z=dz, qkvb=dqkvb, g=dgates)
    gw_in = join_w_in({key: mm(sv["h1"], dpiece, mode="tn", out_dtype=BF16, name="gw_in_" + key)
                       for key, dpiece in pieces.items()})
    w_z, w_g = w["z"], w["g"]
    if hooks is not None:
        w_z = w_z + hooks["w_in_start"](gw_in)[0, 0].astype(BF16)
    dh1 = mm(pieces["z"], w_z, mode="nt", out_dtype=F32, name="d_h1_z")
    dh1 = mm(pieces["qkvba"], w["qkvba"], mode="nt", out_dtype=F32, name="d_h1_qkvba", acc_in=dh1)
    if hooks is not None:
        w_g = w_g + hooks["w_in_finish"](dh1)[0, 0].astype(BF16)
    dh1 = mm(pieces["g"], w_g, mode="nt", out_dtype=F32, name="d_h1_g", acc_in=dh1)
    dh1 = mm(pieces["qkvb"], w["qkvb"], mode="nt", out_dtype=F32, name="d_h1_qkvb", acc_in=dh1)
    grad_x, dsc_t, dsh_t = grad_x_final(dh1, x, dxpre1, mod)
    dmod = jnp.concatenate([dsh_t, dsc_t, dgate_t, dsh_f, dsc_f, fin["gate_f"]], axis=0)
    gw = dict(w_in=gw_in, w_branch_a=gw_branch_a, w_branch_b=gw_branch_b, w_o=gw_o, w_up=gw_up, w_down=gw_down)
    gs = dict(b_gate=db_gate, conv_a=dconv_a, a_log=da_log, dt_bias=ddt_bias, norm_a=dnorm_a, rel_bias=drel_bias,
              ln1_g=dln1_g, ln1_b=dln1_b, conv_ffn=dconv_ffn, b_conv_ffn=db_conv_ffn, ln2_g=fin["ln2_g"], ln2_b=fin["ln2_b"])
    return grad_x, dmod, gw, gs


MESH = pl.DeviceIdType.MESH
ANY = pl.BlockSpec(memory_space=pl.ANY)
WHOLE_VMEM = pl.BlockSpec(memory_space=pltpu.VMEM)


def _place():
    return lax.axis_index("x"), lax.axis_index("y"), lax.axis_index("c")


def allgather8(blk, name):
    m_per, n = blk.shape

    def body(x_ref, out_ref, send_sems, recv_sems, local_sem):
        x, y, c = _place()
        me, sibling = (x, y, c), (x, y, 1 - c)
        chips = [(1 - x, y), (x, 1 - y), (1 - x, 1 - y)]

        def rows(px, py, pc):
            return out_ref.at[pl.ds((4 * px + 2 * py + pc) * m_per, m_per), :]

        def copy(k, block, to, src=None):
            return pltpu.make_async_remote_copy(
                src_ref=rows(*block) if src is None else src, dst_ref=rows(*block),
                send_sem=send_sems.at[k], recv_sem=recv_sems.at[k], device_id=to, device_id_type=MESH)

        mine = pltpu.make_async_copy(x_ref, rows(*me), local_sem)
        mine.start()
        first = [copy(0, me, sibling, src=x_ref)]
        first += [copy(1 + j, me, (*chip, c), src=x_ref) for j, chip in enumerate(chips)]
        for cp in first:
            cp.start()
        passed = [copy(4 + j, (*chip, c), sibling) for j, chip in enumerate(chips)]
        for j, chip in enumerate(chips):
            copy(1 + j, (*chip, c), me).wait_recv()
            passed[j].start()
        copy(0, sibling, me).wait_recv()
        for j, chip in enumerate(chips):
            copy(4 + j, (*chip, 1 - c), me).wait_recv()
        for cp in first + passed:
            cp.wait_send()
        mine.wait()

    return pl.pallas_call(
        body, name=name, out_shape=jax.ShapeDtypeStruct((N_DEV * m_per, n), blk.dtype),
        in_specs=[WHOLE_VMEM], out_specs=WHOLE_VMEM,
        scratch_shapes=[pltpu.SemaphoreType.DMA((7,)), pltpu.SemaphoreType.DMA((7,)), pltpu.SemaphoreType.DMA],
    )(blk)


def _chip_peers(x, y):
    return [(1 - x, y), (x, 1 - y), (1 - x, 1 - y)]


def chip_exchange(arrs, name, scatter):
    n = len(arrs)

    def body(*refs):
        ins, outs = refs[:n], refs[n:2 * n]
        send_sems, recv_sems, local_sems = refs[2 * n:]
        x, y, c = _place()
        me = 2 * x + y
        sibling = (x, y, 1 - c)
        peers = _chip_peers(x, y)

        def half(ref, which):
            r2 = ref.shape[0] // 2
            return ref.at[pl.ds(which * r2, r2), :]

        def outgoing(a, chip):
            return ins[a].at[chip] if scatter else ins[a]

        def copy(k, src, dst, to):
            return pltpu.make_async_remote_copy(src_ref=src, dst_ref=dst, send_sem=send_sems.at[k],
                                                recv_sem=recv_sems.at[k], device_id=to, device_id_type=MESH)

        started, local = [], []
        for a in range(n):
            lc = pltpu.make_async_copy(outgoing(a, me), outs[a].at[me], local_sems.at[a])
            lc.start()
            local.append(lc)
            for j, (px, py) in enumerate(peers):
                cp = copy(6 * a + j, half(outgoing(a, 2 * px + py), c), half(outs[a].at[me], c), (px, py, c))
                cp.start()
                started.append(cp)
        for a in range(n):
            for j, (px, py) in enumerate(peers):
                landed = half(outs[a].at[2 * px + py], c)
                copy(6 * a + j, landed, landed, (px, py, c)).wait_recv()
                relay = copy(6 * a + 3 + j, landed, landed, sibling)
                relay.start()
                started.append(relay)
        for a in range(n):
            for j, (px, py) in enumerate(peers):
                other = half(outs[a].at[2 * px + py], 1 - c)
                copy(6 * a + 3 + j, other, other, sibling).wait_recv()
        for cp in started:
            cp.wait_send()
        for lc in local:
            lc.wait()

    out_shape = [jax.ShapeDtypeStruct(a.shape if scatter else (N_CHIPS,) + a.shape, a.dtype) for a in arrs]
    return pl.pallas_call(
        body, name=name, out_shape=out_shape, in_specs=[ANY] * n, out_specs=[ANY] * n,
        scratch_shapes=[pltpu.SemaphoreType.DMA((6 * n,)), pltpu.SemaphoreType.DMA((6 * n,)), pltpu.SemaphoreType.DMA((n,))],
    )(*arrs)


HBM_SPEC = pl.BlockSpec(memory_space=pltpu.HBM)
SEM_SPEC = pl.BlockSpec(memory_space=pltpu.SEMAPHORE)
SIDE_EFFECT = pltpu.SideEffectType.DATAFLOW_SIDE_EFFECTING


def _in_hbm(a):
    return pltpu.with_memory_space_constraint(a, pltpu.HBM)


def exchange_start(arrs, name, scatter, after):
    n = len(arrs)
    lands = [lax.empty(a.shape if scatter else (N_CHIPS,) + a.shape, a.dtype) for a in arrs]

    def body(*refs):
        ins, zones = refs[:n], refs[n:2 * n]
        send_sems, recv_sems, token = refs[2 * n + 1], refs[2 * n + 2], refs[-1]
        x, y, c = _place()
        me = 2 * x + y
        for a in range(n):
            for j, (px, py) in enumerate(_chip_peers(x, y)):
                pltpu.make_async_remote_copy(
                    src_ref=ins[a].at[2 * px + py] if scatter else ins[a], dst_ref=zones[a].at[me],
                    send_sem=send_sems.at[3 * a + j], recv_sem=recv_sems.at[3 * a + j],
                    device_id=(px, py, c), device_id_type=MESH).start()
        token[...] = jnp.zeros_like(token)

    res = pl.pallas_call(
        body, name=name,
        out_shape=[pltpu.SemaphoreType.DMA((3 * n,)), pltpu.SemaphoreType.DMA((3 * n,))]
        + [pltpu.HBM(a.shape, a.dtype) for a in arrs] + [pltpu.HBM(z.shape, z.dtype) for z in lands]
        + [jax.ShapeDtypeStruct((8, 128), F32)],
        in_specs=[HBM_SPEC] * (2 * n) + [ANY], out_specs=[SEM_SPEC, SEM_SPEC] + [HBM_SPEC] * (2 * n) + [WHOLE_VMEM],
        input_output_aliases={i: 2 + i for i in range(2 * n)},
        compiler_params=pltpu.CompilerParams(has_side_effects=SIDE_EFFECT),
    )(*[_in_hbm(a) for a in arrs], *[_in_hbm(z) for z in lands], after)
    return dict(send=res[0], recv=res[1], src=res[2:2 + n], zones=res[2 + n:2 + 2 * n], token=res[-1], scatter=scatter)


def exchange_wait(handle, name, after):
    srcs, zones, scatter = handle["src"], handle["zones"], handle["scatter"]
    n = len(srcs)

    def body(*refs):
        ins, lands = refs[:n], refs[n:2 * n]
        send_sems, recv_sems = refs[2 * n], refs[2 * n + 1]
        x, y, c = _place()
        me = 2 * x + y
        for a in range(n):
            for j, (px, py) in enumerate(_chip_peers(x, y)):
                cp = pltpu.make_async_remote_copy(
                    src_ref=ins[a].at[me] if scatter else ins[a], dst_ref=lands[a].at[2 * px + py],
                    send_sem=send_sems.at[3 * a + j], recv_sem=recv_sems.at[3 * a + j],
                    device_id=(px, py, c), device_id_type=MESH)
                cp.wait_send()
                cp.wait_recv()

    res = pl.pallas_call(
        body, name=name, out_shape=[pltpu.HBM(a.shape, a.dtype) for a in list(srcs) + list(zones)],
        in_specs=[HBM_SPEC] * (2 * n) + [SEM_SPEC, SEM_SPEC, ANY], out_specs=[HBM_SPEC] * (2 * n),
        input_output_aliases={i: i for i in range(2 * n)},
        compiler_params=pltpu.CompilerParams(has_side_effects=SIDE_EFFECT),
    )(*srcs, *zones, handle["send"], handle["recv"], after)
    return res[n:]


def swap_start(arrs, name, after):
    n = len(arrs)
    lands = [lax.empty(a.shape, a.dtype) for a in arrs]

    def body(*refs):
        ins, zones = refs[:n], refs[n:2 * n]
        send_sems, recv_sems, token = refs[2 * n + 1], refs[2 * n + 2], refs[-1]
        x, y, c = _place()
        for a in range(n):
            pltpu.make_async_remote_copy(src_ref=ins[a], dst_ref=zones[a], send_sem=send_sems.at[a], recv_sem=recv_sems.at[a],
                                         device_id=(x, y, 1 - c), device_id_type=MESH).start()
        token[...] = jnp.zeros_like(token)

    res = pl.pallas_call(
        body, name=name,
        out_shape=[pltpu.SemaphoreType.DMA((n,)), pltpu.SemaphoreType.DMA((n,))]
        + [pltpu.HBM(a.shape, a.dtype) for a in arrs] * 2 + [jax.ShapeDtypeStruct((8, 128), F32)],
        in_specs=[HBM_SPEC] * (2 * n) + [ANY], out_specs=[SEM_SPEC, SEM_SPEC] + [HBM_SPEC] * (2 * n) + [WHOLE_VMEM],
        input_output_aliases={i: 2 + i for i in range(2 * n)},
        compiler_params=pltpu.CompilerParams(has_side_effects=SIDE_EFFECT),
    )(*[_in_hbm(a) for a in arrs], *[_in_hbm(z) for z in lands], after)
    return dict(send=res[0], recv=res[1], src=res[2:2 + n], zones=res[2 + n:2 + 2 * n], token=res[-1])


def swap_wait(handle, name, after):
    srcs, zones = handle["src"], handle["zones"]
    n = len(srcs)

    def body(*refs):
        ins, lands = refs[:n], refs[n:2 * n]
        send_sems, recv_sems = refs[2 * n], refs[2 * n + 1]
        x, y, c = _place()
        for a in range(n):
            cp = pltpu.make_async_remote_copy(src_ref=ins[a], dst_ref=lands[a], send_sem=send_sems.at[a],
                                              recv_sem=recv_sems.at[a], device_id=(x, y, 1 - c), device_id_type=MESH)
            cp.wait_send()
            cp.wait_recv()

    res = pl.pallas_call(
        body, name=name, out_shape=[pltpu.HBM(a.shape, a.dtype) for a in list(srcs) + list(zones)],
        in_specs=[HBM_SPEC] * (2 * n) + [SEM_SPEC, SEM_SPEC, ANY], out_specs=[HBM_SPEC] * (2 * n),
        input_output_aliases={i: i for i in range(2 * n)},
        compiler_params=pltpu.CompilerParams(has_side_effects=SIDE_EFFECT),
    )(*srcs, *zones, handle["send"], handle["recv"], after)
    return res[:n], res[n:]


TILE_BYTES = 2 * 1024 * 1024


def _row_tile(rows, row_bytes):
    if rows * row_bytes <= TILE_BYTES or rows % 8:
        return rows
    best = 8
    for t in range(8, rows + 1, 8):
        if rows % t == 0 and t * row_bytes <= TILE_BYTES:
            best = t
    return best


def pair_add(a, b, name):
    shape = a.shape
    a, b = a.reshape(-1, shape[-1]), b.reshape(-1, shape[-1])
    R, C = a.shape
    tr = _row_tile(R, C * 4)

    def body(a_ref, b_ref, o_ref):
        o_ref[...] = (a_ref[...].astype(F32) + b_ref[...].astype(F32)).astype(BF16)

    spec = pl.BlockSpec((tr, C), lambda i: (i, 0))
    return pl.pallas_call(body, name=name, grid=(R // tr,), in_specs=[spec, spec], out_specs=spec,
                          out_shape=jax.ShapeDtypeStruct((R, C), BF16), compiler_params=_cparams(("parallel",)))(a, b).reshape(shape)


def sum_lead(parts, name):
    K, R, C = parts.shape
    tr = _row_tile(R, C * 4)

    def body(p_ref, o_ref):
        acc = p_ref[0].astype(F32)
        for j in range(1, K):
            acc = acc + p_ref[j].astype(F32)
        o_ref[...] = acc

    return pl.pallas_call(
        body, name=name, grid=(R // tr,), in_specs=[pl.BlockSpec((K, tr, C), lambda i: (0, i, 0))],
        out_specs=pl.BlockSpec((tr, C), lambda i: (i, 0)), out_shape=jax.ShapeDtypeStruct((R, C), F32),
        compiler_params=_cparams(("parallel",)))(parts)


def adamw(w, g, m, v, name):
    R, C = w.shape
    tr = _row_tile(R, C * 4)

    def body(w_ref, g_ref, m_ref, v_ref, d_ref, mo_ref, vo_ref):
        gv = g_ref[...]
        m2 = ADAM_B1 * m_ref[...] + (1.0 - ADAM_B1) * gv
        v2 = ADAM_B2 * v_ref[...] + (1.0 - ADAM_B2) * (gv * gv)
        m_hat = m2 / (1.0 - ADAM_B1 ** ADAM_STEP)
        v_hat = v2 / (1.0 - ADAM_B2 ** ADAM_STEP)
        d_ref[...] = -ADAM_LR * (m_hat / (jnp.sqrt(v_hat) + ADAM_EPS) + ADAM_WD * w_ref[...])
        mo_ref[...] = m2
        vo_ref[...] = v2

    spec = pl.BlockSpec((tr, C), lambda i: (i, 0))
    return pl.pallas_call(body, name=name, grid=(R // tr,), in_specs=[spec] * 4, out_specs=[spec] * 3,
                          out_shape=[jax.ShapeDtypeStruct((R, C), F32)] * 3, compiler_params=_cparams(("parallel",)))(w, g, m, v)


LANES = 1024


def _pack(arrs, rows):
    out, offs, r = [], [], 0
    for a in arrs:
        flat = a.reshape(-1)
        nr = -(-flat.shape[0] // LANES)
        out.append(jnp.pad(flat, (0, nr * LANES - flat.shape[0])))
        offs.append(r)
        r += nr
    assert r <= rows, (r, rows)
    out.append(jnp.zeros(((rows - r) * LANES,), F32))
    return jnp.concatenate(out).reshape(rows, LANES), offs


def _unpack(packed, offs, shapes):
    flat = packed.reshape(-1)
    return [flat[o * LANES:o * LANES + math.prod(s)].reshape(s) for o, s in zip(offs, shapes)]


WEIGHTS = ["w_ada", "b_ada", "w_in", "b_gate", "conv_a", "a_log", "dt_bias", "norm_a", "rel_bias", "w_branch_a",
           "w_branch_b", "w_o", "ln1_g", "ln1_b", "w_up", "conv_ffn", "b_conv_ffn", "w_down", "ln2_g", "ln2_b"]
BIG = ["w_in", "w_branch_a", "w_branch_b", "w_o", "w_up", "w_down"]
LATE = [n for n in BIG if n != "w_in"]
KEPT_SHARDED = {"w_up"}
COL_SHARDED = {"w_in", "w_up"}
SMALL_SHARDED = {"conv_a": 3 * A_W // N_CHIPS, "rel_bias": B_REL // N_CHIPS, "conv_ffn": 2 * D_FF // N_CHIPS}
SMALL = [n for n in WEIGHTS if n not in BIG and n != "w_ada"]


def _to_full(g4, name):
    if name in KEPT_SHARDED:
        return g4
    if name in COL_SHARDED:
        return jnp.transpose(g4, (1, 0, 2)).reshape(g4.shape[1], -1)
    return g4.reshape(-1, g4.shape[2])


def _to_shards(full, name):
    if name in KEPT_SHARDED:
        return full
    if name in COL_SHARDED:
        return jnp.transpose(full.reshape(full.shape[0], N_CHIPS, -1), (1, 0, 2))
    return full.reshape(N_CHIPS, -1, full.shape[1])


def kernel(x, c, w_ada, b_ada, w_in, b_gate, conv_a, a_log, dt_bias, norm_a, rel_bias, w_branch_a, w_branch_b, w_o, ln1_g, ln1_b, w_up, conv_ffn, b_conv_ffn, w_down, ln2_g, ln2_b, loss_target, m_w_ada, m_b_ada, m_w_in, m_b_gate, m_conv_a, m_a_log, m_dt_bias, m_norm_a, m_rel_bias, m_w_branch_a, m_w_branch_b, m_w_o, m_ln1_g, m_ln1_b, m_w_up, m_conv_ffn, m_b_conv_ffn, m_w_down, m_ln2_g, m_ln2_b, v_w_ada, v_b_ada, v_w_in, v_b_gate, v_conv_a, v_a_log, v_dt_bias, v_norm_a, v_rel_bias, v_w_branch_a, v_w_branch_b, v_w_o, v_ln1_g, v_ln1_b, v_w_up, v_conv_ffn, v_b_conv_ffn, v_w_down, v_ln2_g, v_ln2_b):
    args = dict(locals())
    wts = {n: args[n] for n in WEIGHTS}
    moms = {n: args["m_" + n] for n in WEIGHTS}
    vars_ = {n: args["v_" + n] for n in WEIGHTS}
    xi, yi, ci = _place()
    chip = 2 * xi + yi
    dev = 4 * xi + 2 * yi + ci
    ada_cols = w_ada.shape[2]

    sshapes = [wts[n].shape[1:] for n in SMALL_SHARDED]
    spack, soffs = _pack([wts[n][0] for n in SMALL_SHARDED], 16)
    first = allgather8(jnp.concatenate([jnp.pad(c, ((0, 7), (0, 0))), spack]), "gather_c_small_w").reshape(N_DEV, 24, LANES)
    c_all = first[:, 0]
    b_ada_sh = lax.dynamic_slice(b_ada, (0, chip * ada_cols), (1, ada_cols))
    mod_sh = ada_fwd(c_all, w_ada[0], b_ada_sh)
    mod_g = allgather8(mod_sh, "gather_mod").reshape(N_CHIPS, 2, N_DEV, ada_cols)[:, 0]
    mod = lax.dynamic_slice(mod_g, (0, dev, 0), (N_CHIPS, 1, ada_cols)).reshape(6, D_MODEL)

    (w_in_g4,) = chip_exchange([wts["w_in"][0].astype(BF16)], "gather_w_in", scatter=False)
    wd = split_w_in(_to_full(w_in_g4, "w_in"))
    late_shards = [wts[n][0].astype(BF16) for n in LATE]
    late_gather = exchange_start(late_shards, "gather_late_start", scatter=False, after=w_in_g4)
    mod = mod + late_gather["token"][0, 0]

    def late_weights(after):
        zones = exchange_wait(late_gather, "gather_late_wait", after)
        full = [_to_full(lax.dynamic_update_slice(z, s[None], (chip, 0, 0)), n) for n, z, s in zip(LATE, zones, late_shards)]
        return {n[2:]: f for n, f in zip(LATE, full)}

    sg = first[::2, 8:]
    sparts = [_unpack(sg[j], soffs, sshapes) for j in range(N_CHIPS)]
    sm = {n: wts[n] for n in SMALL if n not in SMALL_SHARDED and n != "b_ada"}
    for i, n in enumerate(SMALL_SHARDED):
        sm[n] = jnp.concatenate([sparts[j][i] for j in range(N_CHIPS)], axis=-1)

    early = {}

    def late_start(g):
        early["swap"] = swap_start([g[n] for n in LATE], "grad_swap_late_start", g[LATE[0]])
        return early["swap"]["token"]

    def late_finish(after):
        mine, theirs = swap_wait(early["swap"], "grad_swap_late_wait", after)
        early["sums"] = [_to_shards(pair_add(a, b, "grad_pair_" + n), n) for n, a, b in zip(LATE, mine, theirs)]
        early["scatter"] = exchange_start(early["sums"], "grad_scatter_start", scatter=True, after=theirs[0])
        return early["scatter"]["token"]

    def w_in_start(g):
        early["swap_in"] = swap_start([g], "grad_swap_w_in_start", g)
        return early["swap_in"]["token"]

    def w_in_finish(after):
        (mine,), (theirs,) = swap_wait(early["swap_in"], "grad_swap_w_in_wait", after)
        early["sum_in"] = _to_shards(pair_add(mine, theirs, "grad_pair_w_in"), "w_in")
        early["scatter_in"] = exchange_start([early["sum_in"]], "grad_scatter_w_in_start", scatter=True, after=theirs)
        return early["scatter_in"]["token"]

    hooks = dict(late_start=late_start, late_finish=late_finish, w_in_start=w_in_start, w_in_finish=w_in_finish)
    loss, dxpre2, dffn, fin, sv = forward_local(x[0], loss_target[0], mod, wd, sm, late_weights)
    grad_x, dmod, gw, gs = backward_local(x[0], mod, sm, dxpre2, dffn, fin, sv, hooks)

    gnames = [n for n in SMALL if n != "b_ada"]
    vec, voffs = _pack([dmod] + [gs[n] for n in gnames] + [loss], 56)
    gathered = allgather8(vec, "gather_small_g").reshape(N_DEV, 56, LANES)
    summed = sum_lead(gathered, "sum_small_g")
    full_shapes = [(6, D_MODEL)] + [gs[n].shape for n in gnames] + [(1, 1)]
    parts = _unpack(summed, voffs, full_shapes)
    grads = {"b_ada": parts[0].reshape(1, -1)}
    for n, p in zip(gnames, parts[1:-1]):
        if n in SMALL_SHARDED:
            p = lax.dynamic_slice_in_dim(p, chip * SMALL_SHARDED[n], SMALL_SHARDED[n], axis=1)
        grads[n] = p.reshape(wts[n].shape)
    loss_total = parts[-1].reshape(())
    dmod_all = gathered[:, 0:6, :].reshape(N_DEV, 6 * D_MODEL)
    grads["w_ada"] = ada_bwd(c_all, lax.dynamic_slice(dmod_all, (0, chip * ada_cols), (N_DEV, ada_cols)))[None]

    def own_slot(zone, sums):
        return lax.dynamic_update_slice(zone, lax.dynamic_slice_in_dim(sums, chip, 1, axis=0), (chip, 0, 0))

    zones = exchange_wait(early["scatter"], "grad_scatter_wait", summed)
    for n, z, s in zip(LATE, zones, early["sums"]):
        grads[n] = sum_lead(own_slot(z, s), "grad_sum_" + n)[None]

    delta, new_m, new_v = {}, {}, {}

    def update(n):
        d, m2, v2 = adamw(wts[n][0], grads[n][0], moms[n][0], vars_[n][0], "adamw_" + n)
        delta[n], new_m[n], new_v[n] = d[None], m2[None], v2[None]

    for n in ["w_ada"] + LATE:
        update(n)
    shapes = [wts[n].shape for n in SMALL]
    packs = [_pack([t[n] for n in SMALL], 32) for t in (wts, grads, moms, vars_)]
    outs = adamw(*[p[0] for p in packs], "adamw_small")
    for res, o in zip((delta, new_m, new_v), outs):
        for n, a in zip(SMALL, _unpack(o, packs[0][1], shapes)):
            res[n] = a
    (zone_in,) = exchange_wait(early["scatter_in"], "grad_scatter_w_in_wait", outs[0])
    grads["w_in"] = sum_lead(own_slot(zone_in, early["sum_in"]), "grad_sum_w_in")[None]
    update("w_in")
    return (loss_total, grad_x[None], *[grads[n] for n in WEIGHTS], *[delta[n] for n in WEIGHTS],
            *[new_m[n] for n in WEIGHTS], *[new_v[n] for n in WEIGHTS])
```

```python
import functools
import math

import jax
import jax.numpy as jnp
from jax import lax
from jax.experimental import pallas as pl
from jax.experimental.pallas import tpu as pltpu

F32 = jnp.float32
BF16 = jnp.bfloat16

D_MODEL = 1024
CHUNK = 64
A_HEADS = 8
A_DK = 128
A_W = A_HEADS * A_DK
A_CONV = 4
B_HEADS = 16
B_DH = 64
B_W = B_HEADS * B_DH
B_PREV = 8
B_BAND = (B_PREV + 1) * CHUNK
B_MAX_REL = 256
B_REL = CHUNK - 1 + B_MAX_REL + 1
D_FF = 2816
FFN_CONV = 3
IN_COLS = 4 * A_W + 2 * A_HEADS + 3 * B_W + 2 * D_MODEL
ALPHA = 2.0 ** 0.25
LN_EPS = 1e-5
RMS_EPS = 1e-6
L2_EPS = 1e-6
NEG_INF = -1e30
ADAM_LR, ADAM_B1, ADAM_B2, ADAM_EPS, ADAM_WD, ADAM_STEP = 0.001, 0.9, 0.999, 1e-08, 0.01, 10
N_CHIPS = 4
N_DEV = 8
VMEM_LIMIT = 56 * 1024 * 1024


def _cparams(sem=None):
    return pltpu.CompilerParams(dimension_semantics=sem, vmem_limit_bytes=VMEM_LIMIT)


_DIMS = {"nn": (((1,), (0,)), ((), ())), "nt": (((1,), (1,)), ((), ())), "tn": (((0,), (0,)), ((), ()))}


MM_TILE_CAP = 1408


def _mm_tile(n):
    return max(t for t in range(128, min(n, MM_TILE_CAP) + 1, 128) if n % t == 0)


def mm(a, b, *, mode, out_dtype, name, acc_in=None, b_shards=False, out_shards=0):
    b_rows, b_cols = (b.shape[1], b.shape[0] * b.shape[2]) if b_shards else b.shape
    if mode == "nn":
        (M, K), (K2, N) = a.shape, (b_rows, b_cols)
    elif mode == "nt":
        (M, K), (N, K2) = a.shape, (b_rows, b_cols)
    else:
        (K, M), (K2, N) = a.shape, (b_rows, b_cols)
    assert K == K2, (a.shape, b.shape, mode)
    tm, tn, tk = _mm_tile(M), _mm_tile(N), _mm_tile(K)
    nk = K // tk

    def body(*refs):
        if acc_in is None:
            a_ref, b_ref, o_ref, acc_ref = refs
        else:
            a_ref, b_ref, c_ref, o_ref, acc_ref = refs
        k = pl.program_id(2)

        @pl.when(k == 0)
        def _():
            if acc_in is None:
                acc_ref[...] = jnp.zeros_like(acc_ref)
            else:
                acc_ref[...] = c_ref[...]

        acc_ref[...] += lax.dot_general(a_ref[...].astype(BF16), b_ref[...].astype(BF16), _DIMS[mode],
                                        preferred_element_type=F32)

        @pl.when(k == nk - 1)
        def _():
            o_ref[...] = acc_ref[...].astype(out_dtype)

    a_spec = pl.BlockSpec((tk, tm), lambda i, j, k: (k, i)) if mode == "tn" else pl.BlockSpec((tm, tk), lambda i, j, k: (i, k))
    if b_shards:
        assert (tk if mode == "nt" else tn) == b.shape[2] and mode != "tn", (b.shape, tn, tk, mode)
        b_spec = (pl.BlockSpec((None, tn, tk), lambda i, j, k: (k, j, 0)) if mode == "nt"
                  else pl.BlockSpec((None, tk, tn), lambda i, j, k: (j, k, 0)))
    else:
        b_spec = pl.BlockSpec((tn, tk), lambda i, j, k: (j, k)) if mode == "nt" else pl.BlockSpec((tk, tn), lambda i, j, k: (k, j))
    o_spec = pl.BlockSpec((tm, tn), lambda i, j, k: (i, j))
    out_shape = jax.ShapeDtypeStruct((M, N), out_dtype)
    if out_shards:
        assert N == out_shards * tn and acc_in is None, (N, tn, out_shards)
        o_spec = pl.BlockSpec((None, tm, tn), lambda i, j, k: (j, i, 0))
        out_shape = jax.ShapeDtypeStruct((out_shards, M, tn), out_dtype)
    ins, in_specs, aliases = [a, b], [a_spec, b_spec], {}
    if acc_in is not None:
        assert acc_in.shape == (M, N) and acc_in.dtype == F32 and out_dtype == F32
        ins.append(acc_in)
        in_specs.append(o_spec)
        aliases = {2: 0}
    return pl.pallas_call(
        body, name=name, grid=(M // tm, N // tn, nk), in_specs=in_specs, out_specs=o_spec,
        out_shape=out_shape, scratch_shapes=[pltpu.VMEM((tm, tn), F32)],
        input_output_aliases=aliases, compiler_params=_cparams(("parallel", "parallel", "arbitrary")),
    )(*ins)


def rowcall(body, *, name, S, ts, ins, outs, scratch=()):
    assert S % ts == 0 and ts % 16 == 0
    nsteps = S // ts
    in_specs, arrays = [], []
    for arr, kind in ins:
        arrays.append(arr)
        if kind == "row":
            in_specs.append(pl.BlockSpec((ts, arr.shape[1]), lambda i: (i, 0)))
        elif kind in ("prev", "next"):
            hr = 8 * (4 // arr.dtype.itemsize)
            per, last = ts // hr, S // hr - 1
            if kind == "prev":
                in_specs.append(pl.BlockSpec((hr, arr.shape[1]), lambda i, per=per: (jnp.maximum(i * per - 1, 0), 0)))
            else:
                in_specs.append(pl.BlockSpec((hr, arr.shape[1]), lambda i, per=per, last=last: (jnp.minimum((i + 1) * per, last), 0)))
        else:
            nd = arr.ndim
            in_specs.append(pl.BlockSpec(arr.shape, lambda i, nd=nd: (0,) * nd))
    out_specs, out_shapes, acc_idx = [], [], []
    for n, (shape, dtype, kind) in enumerate(outs):
        out_shapes.append(jax.ShapeDtypeStruct(shape, dtype))
        if kind == "row":
            out_specs.append(pl.BlockSpec((ts, shape[1]), lambda i: (i, 0)))
        else:
            nd = len(shape)
            out_specs.append(pl.BlockSpec(shape, lambda i, nd=nd: (0,) * nd))
            acc_idx.append(n)
    n_in = len(arrays)

    def wrapped(*refs):
        @pl.when(pl.program_id(0) == 0)
        def _():
            for n in acc_idx:
                refs[n_in + n][...] = jnp.zeros_like(refs[n_in + n])

        body(*refs)

    res = pl.pallas_call(
        wrapped, name=name, grid=(nsteps,), in_specs=in_specs, out_specs=out_specs, out_shape=out_shapes,
        scratch_shapes=list(scratch), compiler_params=_cparams(("arbitrary",) if acc_idx else ("parallel",)),
    )(*arrays)
    return res


def _halo_prev(ref):
    v = ref[...].astype(F32)
    return v[v.shape[0] - 8:]


def _halo_next(ref):
    return ref[...].astype(F32)[:8]


def _shift_down(cur, prev8, k):
    if k == 0:
        return cur
    rolled = pltpu.roll(cur, k, axis=0)
    fix = pltpu.roll(prev8, k, axis=0)
    row = lax.broadcasted_iota(jnp.int32, (8, 1), 0)
    top = jnp.where(row < k, fix, rolled[0:8])
    if cur.shape[0] == 8:
        return top
    return jnp.concatenate([top, rolled[8:]], axis=0)


def _shift_up(cur, next8, k):
    if k == 0:
        return cur
    n = cur.shape[0]
    rolled = pltpu.roll(cur, n - k, axis=0)
    fix = pltpu.roll(next8, 8 - k, axis=0)
    row = lax.broadcasted_iota(jnp.int32, (8, 1), 0)
    bot = jnp.where(row >= 8 - k, fix, rolled[n - 8:n])
    return jnp.concatenate([rolled[:n - 8], bot], axis=0)


def _sigmoid(x):
    return 1.0 / (1.0 + jnp.exp(-x))


def _silu(x):
    return x * _sigmoid(x)


def _silu_and_grad(x):
    s = _sigmoid(x)
    return x * s, s * (1.0 + x * (1.0 - s))


def _softplus(x):
    return jnp.maximum(x, 0.0) + jnp.log1p(jnp.exp(-jnp.abs(x)))


def _split2(x):
    hi = x.astype(BF16)
    return hi, (x - hi.astype(F32)).astype(BF16)


def _dot1(a, b, mode):
    return lax.dot_general(a.astype(BF16), b.astype(BF16), _DIMS[mode], preferred_element_type=F32)


def _dot3(a, b, mode):
    ah, al = _split2(a)
    bh, bl = _split2(b)
    d = lambda p, q: lax.dot_general(p, q, _DIMS[mode], preferred_element_type=F32)
    return d(ah, bh) + (d(ah, bl) + d(al, bh))


def ada_fwd(c_all, w_sh, b_sh):
    n = w_sh.shape[1]
    tn = 512

    def body(c_ref, w_ref, b_ref, o_ref):
        o_ref[...] = _dot1(_silu(c_ref[...]), w_ref[...], "nn") + b_ref[...]

    return pl.pallas_call(
        body, name="ada_fwd", grid=(n // tn,),
        in_specs=[pl.BlockSpec((N_DEV, D_MODEL), lambda j: (0, 0)), pl.BlockSpec((D_MODEL, tn), lambda j: (0, j)),
                  pl.BlockSpec((1, tn), lambda j: (0, j))],
        out_specs=pl.BlockSpec((N_DEV, tn), lambda j: (0, j)), out_shape=jax.ShapeDtypeStruct((N_DEV, n), F32),
        compiler_params=_cparams(("parallel",)),
    )(c_all, w_sh, b_sh)


def ada_bwd(c_all, dmod_sh):
    n = dmod_sh.shape[1]
    tn = 512

    def body(c_ref, d_ref, o_ref):
        o_ref[...] = _dot1(_silu(c_ref[...]), d_ref[...], "tn")

    return pl.pallas_call(
        body, name="ada_bwd", grid=(n // tn,),
        in_specs=[pl.BlockSpec((N_DEV, D_MODEL), lambda j: (0, 0)), pl.BlockSpec((N_DEV, tn), lambda j: (0, j))],
        out_specs=pl.BlockSpec((D_MODEL, tn), lambda j: (0, j)), out_shape=jax.ShapeDtypeStruct((D_MODEL, n), F32),
        compiler_params=_cparams(("parallel",)),
    )(c_all, dmod_sh)


SHIFT_T, SCALE_T, GATE_T, SHIFT_F, SCALE_F, GATE_F = range(6)


def modulate(x, mod, shift_row, scale_row, name):
    S = x.shape[0]

    def body(x_ref, m_ref, o_ref):
        m = m_ref[...]
        o_ref[...] = (x_ref[...] * (1.0 + m[scale_row:scale_row + 1]) + m[shift_row:shift_row + 1]).astype(BF16)

    return rowcall(body, name=name, S=S, ts=512, ins=[(x, "row"), (mod, "vec")], outs=[((S, D_MODEL), BF16, "row")])[0]


def _conv_fwd(cur, prev, w, width):
    y = cur * w[width - 1:width]
    for j in range(width - 1):
        y = y + _shift_down(cur, prev, width - 1 - j) * w[j:j + 1]
    return y


def _prep_a_core(cur, prev, w):
    return _silu_and_grad(_conv_fwd(cur, prev, w, A_CONV))


def prep_a_fwd(qkv_raw, ba, conv_a, a_log, dt_bias):
    S = qkv_raw.shape[0]

    def body(x_ref, xp_ref, ba_ref, w_ref, al_ref, dt_ref, q_ref, k_ref, v_ref, beta_ref, g_ref):
        first = (pl.program_id(0) > 0).astype(F32)
        y, _ = _prep_a_core(x_ref[...].astype(F32), _halo_prev(xp_ref) * first, w_ref[...])
        for h in range(A_HEADS):
            sl = slice(h * A_DK, (h + 1) * A_DK)
            qh = y[:, sl]
            kh = y[:, A_W + h * A_DK:A_W + (h + 1) * A_DK]
            q_ref[:, sl] = qh * (lax.rsqrt(jnp.sum(qh * qh, axis=-1, keepdims=True) + L2_EPS) * (A_DK ** -0.5))
            k_ref[:, sl] = kh * lax.rsqrt(jnp.sum(kh * kh, axis=-1, keepdims=True) + L2_EPS)
        v_ref[...] = y[:, 2 * A_W:3 * A_W]
        bav = ba_ref[...]
        beta_ref[...] = _sigmoid(bav[:, 0:A_HEADS])
        g_ref[...] = -jnp.exp(al_ref[...]) * _softplus(bav[:, A_HEADS:2 * A_HEADS] + dt_ref[...])

    return rowcall(
        body, name="prep_a_fwd", S=S, ts=256,
        ins=[(qkv_raw, "row"), (qkv_raw, "prev"), (ba, "row"), (conv_a, "vec"), (a_log, "vec"), (dt_bias, "vec")],
        outs=[((S, A_W), F32, "row")] * 3 + [((S, A_HEADS), F32, "row")] * 2)


HEAD_GROUP = 2
GROUP_ROWS = HEAD_GROUP * CHUNK
N_HEAD_GROUPS = A_HEADS // HEAD_GROUP
LOG_CHUNK = int(math.log2(CHUNK))


def _tri_masks():
    rb = lax.broadcasted_iota(jnp.int32, (GROUP_ROWS, GROUP_ROWS), 0)
    cb = lax.broadcasted_iota(jnp.int32, (GROUP_ROWS, GROUP_ROWS), 1)
    same = (rb >> LOG_CHUNK) == (cb >> LOG_CHUNK)
    return dict(causal=same & (rb >= cb), strict=same & (rb > cb), eye=rb == cb, upper=same & (cb >= rb),
                last=cb == (rb | (CHUNK - 1)), rb=rb, cb=cb)


def _col_to_row(colv, eye):
    return jnp.sum(jnp.where(eye, colv, 0.0), axis=0, keepdims=True)


def _row_to_col(rowv, eye):
    return jnp.sum(jnp.where(eye, rowv, 0.0), axis=1, keepdims=True)


def _tri_inv(a_list, mk):
    rb, cb = mk["rb"], mk["cb"]
    ts = [jnp.where(mk["eye"], 1.0, 0.0) - jnp.where((rb >> 1) == (cb >> 1), a, 0.0) for a in a_list]
    for lvl in range(1, LOG_CHUNK):
        rs, cs = rb >> lvl, cb >> lvl
        sel = ((rs & 1) == 1) & (cs == rs - 1)
        inner = [_dot3(t, jnp.where(sel, a, 0.0), "nn") for t, a in zip(ts, a_list)]
        ts = [t - _dot3(i, t, "nn") for i, t in zip(inner, ts)]
    return ts


def _stack_heads(ref, grp):
    return jnp.concatenate([ref[:, (grp * HEAD_GROUP + j) * A_DK:(grp * HEAD_GROUP + j + 1) * A_DK]
                            for j in range(HEAD_GROUP)], axis=0)


def _stack_cols(tile, grp):
    return jnp.concatenate([tile[:, grp * HEAD_GROUP + j:grp * HEAD_GROUP + j + 1] for j in range(HEAD_GROUP)], axis=0)


def _delta_local(q, k, v, beta, g, mk):
    causal, strict, eye = mk["causal"], mk["strict"], mk["eye"]
    g_row = _col_to_row(g, eye)
    gc = jnp.sum(jnp.where(causal, g_row, 0.0), axis=1, keepdims=True)
    gc_row = _col_to_row(gc, eye)
    decay = jnp.where(causal, jnp.exp(jnp.where(causal, gc - gc_row, 0.0)), 0.0)
    gam = jnp.exp(gc)
    kb = k * beta
    vb = v * beta
    y = kb * gam
    a = jnp.where(strict, _dot1(kb, k, "nt") * decay, 0.0)
    p = _dot1(q, k, "nt") * decay
    gl = jnp.sum(jnp.where(mk["last"], gc_row, 0.0), axis=1, keepdims=True)
    kd = k * jnp.exp(gl - gc)
    return dict(gc=gc, decay=decay, gam=gam, kb=kb, vb=vb, y=y, a=a, p=p, gl=gl, kd=kd)


def _head_rows(x, j):
    return x[j * CHUNK:(j + 1) * CHUNK]


def delta_fwd(q, k, v, beta, g):
    S = q.shape[0]
    n_chunks = S // CHUNK

    def body(q_ref, k_ref, v_ref, beta_ref, g_ref, o_ref, sprev_ref, t_ref, state_ref):
        @pl.when(pl.program_id(0) == 0)
        def _():
            state_ref[...] = jnp.zeros_like(state_ref)

        mk = _tri_masks()
        betav, gv = beta_ref[...], g_ref[...]
        groups = range(N_HEAD_GROUPS)
        q_all = [_stack_heads(q_ref, grp) for grp in groups]
        locs = [_delta_local(q_all[grp], _stack_heads(k_ref, grp), _stack_heads(v_ref, grp),
                             _stack_cols(betav, grp), _stack_cols(gv, grp), mk) for grp in groups]
        tinvs = _tri_inv([loc["a"] for loc in locs], mk)
        uws = [_dot3(tinvs[grp], jnp.concatenate([locs[grp]["vb"], locs[grp]["y"]], axis=1), "nn") for grp in groups]
        for grp in groups:
            loc, uw = locs[grp], uws[grp]
            t_ref[0, grp] = tinvs[grp]
            qg = q_all[grp] * loc["gam"]
            egl = jnp.exp(loc["gl"])
            vns, o_state = [], []
            for j in range(HEAD_GROUP):
                h = grp * HEAD_GROUP + j
                s0 = state_ref[h]
                sprev_ref[0, h] = s0
                uw_h = _head_rows(uw, j)
                vn = uw_h[:, :A_DK] - _dot1(uw_h[:, A_DK:], s0, "nn")
                vns.append(vn)
                o_state.append(_dot1(_head_rows(qg, j), s0, "nn"))
                state_ref[h] = s0 * egl[(j + 1) * CHUNK - 1:(j + 1) * CHUNK] + _dot1(_head_rows(loc["kd"], j), vn, "tn")
            o_local = _dot1(loc["p"], jnp.concatenate(vns, axis=0), "nn")
            for j in range(HEAD_GROUP):
                h = grp * HEAD_GROUP + j
                o_ref[:, h * A_DK:(h + 1) * A_DK] = o_state[j] + _head_rows(o_local, j)

    tile = pl.BlockSpec((CHUNK, A_W), lambda n: (n, 0))
    small = pl.BlockSpec((CHUNK, A_HEADS), lambda n: (n, 0))
    return pl.pallas_call(
        body, name="delta_fwd", grid=(n_chunks,), in_specs=[tile, tile, tile, small, small],
        out_specs=[tile, pl.BlockSpec((1, A_HEADS, A_DK, A_DK), lambda n: (n, 0, 0, 0)),
                   pl.BlockSpec((1, N_HEAD_GROUPS, GROUP_ROWS, GROUP_ROWS), lambda n: (n, 0, 0, 0))],
        out_shape=[jax.ShapeDtypeStruct((S, A_W), F32), jax.ShapeDtypeStruct((n_chunks, A_HEADS, A_DK, A_DK), F32),
                   jax.ShapeDtypeStruct((n_chunks, N_HEAD_GROUPS, GROUP_ROWS, GROUP_ROWS), F32)],
        scratch_shapes=[pltpu.VMEM((A_HEADS, A_DK, A_DK), F32)],
        compiler_params=_cparams(("arbitrary",)),
    )(q, k, v, beta, g)


def gate_a_fwd(o_pre, z, norm_w):
    S = o_pre.shape[0]

    def body(o_ref, z_ref, nw_ref, out_ref):
        nw = nw_ref[...]
        for h in range(A_HEADS):
            sl = slice(h * A_DK, (h + 1) * A_DK)
            oh = o_ref[:, sl]
            r = lax.rsqrt(jnp.mean(oh * oh, axis=-1, keepdims=True) + RMS_EPS)
            out_ref[:, sl] = (oh * r * nw * _silu(z_ref[:, sl].astype(F32))).astype(BF16)

    return rowcall(body, name="gate_a_fwd", S=S, ts=512, ins=[(o_pre, "row"), (z, "row"), (norm_w, "vec")],
                   outs=[((S, A_W), BF16, "row")])[0]


HEADS_PER_GROUP = 2
GROUP_W = HEADS_PER_GROUP * B_DH
N_GROUPS = B_HEADS // HEADS_PER_GROUP
PAD_ROWS = B_PREV * CHUNK


Q_TILE = 256
Q_CHUNKS = Q_TILE // CHUNK
KEY_WIN = (B_PREV + Q_CHUNKS) * CHUNK


def _band_probs(qh, kh, bias, valid):
    s = _dot1(qh, kh, "nt") * (B_DH ** -0.5) + bias
    s = jnp.where(valid, s, NEG_INF)
    e = jnp.exp(s - jnp.max(s, axis=-1, keepdims=True))
    return e * (1.0 / jnp.sum(e, axis=-1, keepdims=True))


def _attn_specs(S, tile_rows):
    n_cb = B_W // GROUP_W
    return [pl.BlockSpec((tile_rows, GROUP_W), lambda g, n: (n + PAD_ROWS // tile_rows, g)),
            pl.BlockSpec((PAD_ROWS + S, GROUP_W), lambda g, n: (0, n_cb + g)),
            pl.BlockSpec((PAD_ROWS + S, GROUP_W), lambda g, n: (0, 2 * n_cb + g)),
            pl.BlockSpec((HEADS_PER_GROUP, CHUNK, B_BAND), lambda g, n: (g, 0, 0))]


def _band_valid(first_chunk):
    return lax.broadcasted_iota(jnp.int32, (CHUNK, B_BAND), 1) >= PAD_ROWS - first_chunk * CHUNK


def _chunk_rows(x, qc, rows=CHUNK):
    return x[qc * CHUNK:qc * CHUNK + rows]


FWD_TILE = 512
FWD_CHUNKS = FWD_TILE // CHUNK
FWD_WIN = (B_PREV + FWD_CHUNKS) * CHUNK


def attn_fwd(qkv_pad, bias):
    S = qkv_pad.shape[0] - PAD_ROWS

    def body(q_ref, k_ref, v_ref, b_ref, o_ref):
        n = pl.program_id(1)
        start = pl.multiple_of(n * FWD_TILE, FWD_TILE)
        kwin = k_ref[pl.ds(start, FWD_WIN), :]
        vwin = v_ref[pl.ds(start, FWD_WIN), :]
        qv = q_ref[...]
        pairs = [(qc, hh) for qc in range(FWD_CHUNKS) for hh in range(HEADS_PER_GROUP)]
        sl = lambda hh: slice(hh * B_DH, (hh + 1) * B_DH)
        s = [_dot1(_chunk_rows(qv, qc)[:, sl(hh)], _chunk_rows(kwin, qc, B_BAND)[:, sl(hh)], "nt") for qc, hh in pairs]
        s = [jnp.where(_band_valid(n * FWD_CHUNKS + qc), x * (B_DH ** -0.5) + b_ref[hh], NEG_INF)
             for x, (qc, hh) in zip(s, pairs)]
        e = [jnp.exp(x - jnp.max(x, axis=-1, keepdims=True)) for x in s]
        p = [x * (1.0 / jnp.sum(x, axis=-1, keepdims=True)) for x in e]
        o = [_dot1(x, _chunk_rows(vwin, qc, B_BAND)[:, sl(hh)], "nn") for x, (qc, hh) in zip(p, pairs)]
        rows = [jnp.concatenate(o[qc * HEADS_PER_GROUP:(qc + 1) * HEADS_PER_GROUP], axis=1) for qc in range(FWD_CHUNKS)]
        o_ref[...] = jnp.concatenate(rows, axis=0).astype(BF16)

    return pl.pallas_call(
        body, name="attn_fwd", grid=(N_GROUPS, S // FWD_TILE), in_specs=_attn_specs(S, FWD_TILE),
        out_specs=pl.BlockSpec((FWD_TILE, GROUP_W), lambda g, n: (n, g)),
        out_shape=jax.ShapeDtypeStruct((S, B_W), BF16),
        compiler_params=_cparams(("parallel", "arbitrary")),
    )(qkv_pad, qkv_pad, qkv_pad, bias)


EXT = B_BAND + CHUNK


def bias_expand(rel_bias):
    def body(rev_ref, o_ref):
        rev = rev_ref[...]
        erev = jnp.concatenate([jnp.broadcast_to(rev[:, 0:1], (B_HEADS, EXT - B_REL)), rev], axis=1)
        for i in range(CHUNK):
            o_ref[i] = erev[:, CHUNK - i:CHUNK - i + B_BAND]

    return pl.pallas_call(
        body, name="bias_expand", in_specs=[WHOLE_VMEM], out_specs=WHOLE_VMEM,
        out_shape=jax.ShapeDtypeStruct((CHUNK, B_HEADS, B_BAND), F32),
    )(jnp.flip(rel_bias, axis=1))


def bias_reduce(dbias):
    def body(d_ref, o_ref):
        acc = jnp.zeros((B_HEADS, EXT), F32)
        for i in range(CHUNK):
            acc = acc + jnp.pad(d_ref[i], ((0, 0), (CHUNK - i, i)))
        tail = acc[:, EXT - B_REL:]
        clipped = jnp.sum(acc[:, :EXT - B_REL], axis=1, keepdims=True)
        lane = lax.broadcasted_iota(jnp.int32, (B_HEADS, B_REL), 1)
        o_ref[...] = jnp.where(lane == 0, tail + clipped, tail)

    rev = pl.pallas_call(body, name="bias_reduce", in_specs=[WHOLE_VMEM], out_specs=WHOLE_VMEM,
                         out_shape=jax.ShapeDtypeStruct((B_HEADS, B_REL), F32))(dbias)
    return jnp.flip(rev, axis=1)


def merge_fwd(gates_raw, b_gate, ya, yb):
    S = ya.shape[0]

    def body(g_ref, b_ref, ya_ref, yb_ref, o_ref):
        gt = _sigmoid(g_ref[...].astype(F32) + b_ref[...])
        o_ref[...] = (gt[:, :D_MODEL] * ya_ref[...].astype(F32) + gt[:, D_MODEL:] * yb_ref[...].astype(F32)).astype(BF16)

    return rowcall(body, name="merge_fwd", S=S, ts=512,
                   ins=[(gates_raw, "row"), (b_gate, "vec"), (ya, "row"), (yb, "row")],
                   outs=[((S, D_MODEL), BF16, "row")])[0]


def _ln_stats(xpre):
    mu = jnp.mean(xpre, axis=-1, keepdims=True)
    xc = xpre - mu
    rstd = lax.rsqrt(jnp.mean(xc * xc, axis=-1, keepdims=True) + LN_EPS)
    return xc * rstd, rstd


def ln1_fwd(x, mix, mod, ln_g, ln_b):
    S = x.shape[0]

    def body(x_ref, mix_ref, m_ref, g_ref, b_ref, xpre_ref, x1_ref, h2_ref):
        m = m_ref[...]
        xpre = ALPHA * x_ref[...] + m[GATE_T:GATE_T + 1] * mix_ref[...]
        xhat, _ = _ln_stats(xpre)
        x1 = xhat * g_ref[...] + b_ref[...]
        xpre_ref[...] = xpre
        x1_ref[...] = x1
        h2_ref[...] = (x1 * (1.0 + m[SCALE_F:SCALE_F + 1]) + m[SHIFT_F:SHIFT_F + 1]).astype(BF16)

    return rowcall(body, name="ln1_fwd", S=S, ts=512,
                   ins=[(x, "row"), (mix, "row"), (mod, "vec"), (ln_g, "vec"), (ln_b, "vec")],
                   outs=[((S, D_MODEL), F32, "row"), ((S, D_MODEL), F32, "row"), ((S, D_MODEL), BF16, "row")])


STRIP_ROWS = 32
STRIP_COLS = 256


def ffn_act_fwd(up, conv_w, conv_b):
    S = up.shape[0]
    ts = 256

    def body(u_ref, up_ref, w_ref, b_ref, o_ref, ubuf):
        ubuf[0:8] = _halo_prev(up_ref) * (pl.program_id(0) > 0).astype(F32)
        ubuf[8:8 + ts] = u_ref[...].astype(F32)

        def col_block(j, carry):
            gate = pl.ds(pl.multiple_of(j * STRIP_COLS, STRIP_COLS), STRIP_COLS)
            halves = [gate, pl.ds(pl.multiple_of(D_FF + j * STRIP_COLS, STRIP_COLS), STRIP_COLS)]
            w = [w_ref[:, c] for c in halves]
            bias = [b_ref[:, c] for c in halves]
            for r0 in range(0, ts, STRIP_ROWS):
                uc = []
                for h in range(2):
                    x = ubuf[r0:r0 + STRIP_ROWS + 8, halves[h]]
                    uc.append(bias[h] + sum(
                        w[h][t:t + 1] * (x if t == FFN_CONV - 1 else pltpu.roll(x, FFN_CONV - 1 - t, axis=0))[8:]
                        for t in range(FFN_CONV)))
                o_ref[r0:r0 + STRIP_ROWS, gate] = (_silu(uc[0]) * uc[1]).astype(BF16)
            return carry

        lax.fori_loop(0, D_FF // STRIP_COLS, col_block, 0)

    return rowcall(body, name="ffn_act_fwd", S=S, ts=ts,
                   ins=[(up, "row"), (up, "prev"), (conv_w, "vec"), (conv_b, "vec")],
                   outs=[((S, D_FF), BF16, "row")], scratch=[pltpu.VMEM((ts + 8, 2 * D_FF), F32)])[0]


def final_fwd_bwd(x1, ffn, target, mod, ln_g, ln_b):
    S = x1.shape[0]

    def body(x1_ref, f_ref, t_ref, m_ref, g_ref, b_ref, dxpre_ref, dffn_ref, loss_ref, dgate_ref, dg_ref, db_ref):
        gate = m_ref[...][GATE_F:GATE_F + 1]
        ffn_v = f_ref[...]
        xpre = ALPHA * x1_ref[...] + gate * ffn_v
        xhat, rstd = _ln_stats(xpre)
        err = xhat * g_ref[...] + b_ref[...] - t_ref[...]
        loss_ref[...] += 0.5 * jnp.sum(jnp.mean(err * err, axis=-1, keepdims=True), axis=0, keepdims=True)
        dy = err * (1.0 / D_MODEL)
        dg_ref[...] += jnp.sum(dy * xhat, axis=0, keepdims=True)
        db_ref[...] += jnp.sum(dy, axis=0, keepdims=True)
        dyg = dy * g_ref[...]
        dxpre = rstd * (dyg - jnp.mean(dyg, axis=-1, keepdims=True) - xhat * jnp.mean(dyg * xhat, axis=-1, keepdims=True))
        dxpre_ref[...] = dxpre
        dffn_ref[...] = (gate * dxpre).astype(BF16)
        dgate_ref[...] += jnp.sum(dxpre * ffn_v, axis=0, keepdims=True)

    vec = ((1, D_MODEL), F32, "acc")
    return rowcall(body, name="final_fwd_bwd", S=S, ts=512,
                   ins=[(x1, "row"), (ffn, "row"), (target, "row"), (mod, "vec"), (ln_g, "vec"), (ln_b, "vec")],
                   outs=[((S, D_MODEL), F32, "row"), ((S, D_MODEL), BF16, "row"), ((1, 1), F32, "acc"), vec, vec, vec])


def ffn_act_bwd(dact, up, conv_w, conv_b):
    S = up.shape[0]
    ts = 256
    win_u, win_d = STRIP_ROWS + 16, STRIP_ROWS + 8

    def body(d_ref, dn_ref, u_ref, up_ref, un_ref, w_ref, b_ref, dup_ref, dw_ref, db_ref, ubuf, dbuf):
        i = pl.program_id(0)
        ubuf[0:8] = _halo_prev(up_ref) * (i > 0).astype(F32)
        ubuf[8:8 + ts] = u_ref[...].astype(F32)
        ubuf[8 + ts:16 + ts] = _halo_next(un_ref)
        dbuf[0:ts] = d_ref[...].astype(F32)
        dbuf[ts:ts + 8] = _halo_next(dn_ref) * (i < pl.num_programs(0) - 1).astype(F32)

        def col_block(j, carry):
            halves = [pl.ds(pl.multiple_of(j * STRIP_COLS, STRIP_COLS), STRIP_COLS),
                      pl.ds(pl.multiple_of(D_FF + j * STRIP_COLS, STRIP_COLS), STRIP_COLS)]
            w = [w_ref[:, c] for c in halves]
            bias = [b_ref[:, c] for c in halves]
            dw_acc = [[jnp.zeros((1, STRIP_COLS), F32) for _ in range(FFN_CONV)] for _ in halves]
            db_acc = [jnp.zeros((1, STRIP_COLS), F32) for _ in halves]
            for r0 in range(0, ts, STRIP_ROWS):
                shifted = [[x if k == 0 else pltpu.roll(x, k, axis=0) for k in range(FFN_CONV)]
                           for x in (ubuf[r0:r0 + win_u, c] for c in halves)]
                uc = [bias[h] + sum(w[h][t:t + 1] * shifted[h][FFN_CONV - 1 - t][8:8 + win_d] for t in range(FFN_CONV))
                      for h in range(2)]
                dact_w = dbuf[r0:r0 + win_d, halves[0]]
                sg, dsg = _silu_and_grad(uc[0])
                duc = [dact_w * uc[1] * dsg, dact_w * sg]
                for h in range(2):
                    dup = duc[h] * w[h][FFN_CONV - 1:FFN_CONV]
                    for t in range(FFN_CONV - 1):
                        dup = dup + pltpu.roll(duc[h], win_d - (FFN_CONV - 1 - t), axis=0) * w[h][t:t + 1]
                    dup_ref[r0:r0 + STRIP_ROWS, halves[h]] = dup[:STRIP_ROWS].astype(BF16)
                    mine = duc[h][:STRIP_ROWS]
                    db_acc[h] = db_acc[h] + jnp.sum(mine, axis=0, keepdims=True)
                    for t in range(FFN_CONV):
                        dw_acc[h][t] = dw_acc[h][t] + jnp.sum(
                            mine * shifted[h][FFN_CONV - 1 - t][8:8 + STRIP_ROWS], axis=0, keepdims=True)
            for h in range(2):
                dw_ref[:, halves[h]] += jnp.concatenate(dw_acc[h], axis=0)
                db_ref[:, halves[h]] += db_acc[h]
            return carry

        lax.fori_loop(0, D_FF // STRIP_COLS, col_block, 0)

    return rowcall(body, name="ffn_act_bwd", S=S, ts=ts,
                   ins=[(dact, "row"), (dact, "next"), (up, "row"), (up, "prev"), (up, "next"), (conv_w, "vec"), (conv_b, "vec")],
                   outs=[((S, 2 * D_FF), BF16, "row"), ((FFN_CONV, 2 * D_FF), F32, "acc"), ((1, 2 * D_FF), F32, "acc")],
                   scratch=[pltpu.VMEM((ts + 16, 2 * D_FF), F32), pltpu.VMEM((ts + 8, D_FF), F32)])


def ln1_bwd(dxpre2, dh2, xpre1, mix, mod, ln_g, ln_b):
    S = xpre1.shape[0]

    def body(d2_ref, dh_ref, xp_ref, mix_ref, m_ref, g_ref, b_ref, dxpre_ref, dmix_ref,
             dscale_ref, dshift_ref, dgate_ref, dg_ref, db_ref):
        m = m_ref[...]
        xhat, rstd = _ln_stats(xp_ref[...])
        x1 = xhat * g_ref[...] + b_ref[...]
        dh = dh_ref[...]
        dx1 = ALPHA * d2_ref[...] + dh * (1.0 + m[SCALE_F:SCALE_F + 1])
        dscale_ref[...] += jnp.sum(dh * x1, axis=0, keepdims=True)
        dshift_ref[...] += jnp.sum(dh, axis=0, keepdims=True)
        dg_ref[...] += jnp.sum(dx1 * xhat, axis=0, keepdims=True)
        db_ref[...] += jnp.sum(dx1, axis=0, keepdims=True)
        dyg = dx1 * g_ref[...]
        dxpre = rstd * (dyg - jnp.mean(dyg, axis=-1, keepdims=True) - xhat * jnp.mean(dyg * xhat, axis=-1, keepdims=True))
        dxpre_ref[...] = dxpre
        dmix_ref[...] = (m[GATE_T:GATE_T + 1] * dxpre).astype(BF16)
        dgate_ref[...] += jnp.sum(dxpre * mix_ref[...], axis=0, keepdims=True)

    vec = ((1, D_MODEL), F32, "acc")
    return rowcall(body, name="ln1_bwd", S=S, ts=512,
                   ins=[(dxpre2, "row"), (dh2, "row"), (xpre1, "row"), (mix, "row"), (mod, "vec"), (ln_g, "vec"), (ln_b, "vec")],
                   outs=[((S, D_MODEL), F32, "row"), ((S, D_MODEL), BF16, "row"), vec, vec, vec, vec, vec])


def merge_bwd(dmerged, gates_raw, b_gate, ya, yb):
    S = ya.shape[0]

    def body(d_ref, g_ref, b_ref, ya_ref, yb_ref, dya_ref, dyb_ref, dg_ref, dbg_ref):
        gt = _sigmoid(g_ref[...].astype(F32) + b_ref[...])
        d = d_ref[...].astype(F32)
        ga, gb = gt[:, :D_MODEL], gt[:, D_MODEL:]
        dya_ref[...] = (d * ga).astype(BF16)
        dyb_ref[...] = (d * gb).astype(BF16)
        dgr = jnp.concatenate([d * ya_ref[...].astype(F32) * ga * (1.0 - ga),
                               d * yb_ref[...].astype(F32) * gb * (1.0 - gb)], axis=1)
        dg_ref[...] = dgr.astype(BF16)
        dbg_ref[...] += jnp.sum(dgr, axis=0, keepdims=True)

    return rowcall(body, name="merge_bwd", S=S, ts=512,
                   ins=[(dmerged, "row"), (gates_raw, "row"), (b_gate, "vec"), (ya, "row"), (yb, "row")],
                   outs=[((S, D_MODEL), BF16, "row"), ((S, D_MODEL), BF16, "row"), ((S, 2 * D_MODEL), BF16, "row"),
                         ((1, 2 * D_MODEL), F32, "acc")])


def attn_bwd(qkv_pad, bias, do_b):
    S = qkv_pad.shape[0] - PAD_ROWS

    def body(q_ref, k_ref, v_ref, bias_ref, do_ref, dq_ref, dk_ref, dv_ref, db_ref, b_ref):
        n = pl.program_id(1)

        @pl.when(n == 0)
        def _():
            dk_ref[...] = jnp.zeros_like(dk_ref)
            dv_ref[...] = jnp.zeros_like(dv_ref)
            db_ref[...] = jnp.zeros_like(db_ref)
            b_ref[...] = jnp.full(b_ref.shape, NEG_INF, F32)
            for hh in range(HEADS_PER_GROUP):
                for qc in range(Q_CHUNKS):
                    b_ref[hh, qc * CHUNK:(qc + 1) * CHUNK, qc * CHUNK:qc * CHUNK + B_BAND] = bias_ref[hh]

        start = pl.multiple_of(n * Q_TILE, Q_TILE)
        kwin = k_ref[pl.ds(start, KEY_WIN), :]
        vwin = v_ref[pl.ds(start, KEY_WIN), :]
        qv, dov = q_ref[...], do_ref[...]
        valid = lax.broadcasted_iota(jnp.int32, (Q_TILE, KEY_WIN), 1) >= PAD_ROWS - n * Q_TILE
        dqs, dks, dvs = [], [], []
        for hh in range(HEADS_PER_GROUP):
            sl = slice(hh * B_DH, (hh + 1) * B_DH)
            p = _band_probs(qv[:, sl], kwin[:, sl], b_ref[hh], valid)
            dp = _dot1(dov[:, sl], vwin[:, sl], "nt")
            ds = p * (dp - jnp.sum(dp * p, axis=-1, keepdims=True))
            dbh = ds[0:CHUNK, 0:B_BAND]
            for qc in range(1, Q_CHUNKS):
                dbh = dbh + ds[qc * CHUNK:(qc + 1) * CHUNK, qc * CHUNK:qc * CHUNK + B_BAND]
            db_ref[hh] += dbh
            dsq = ds * (B_DH ** -0.5)
            dqs.append(_dot1(dsq, kwin[:, sl], "nn"))
            dks.append(_dot1(dsq, qv[:, sl], "tn"))
            dvs.append(_dot1(p, dov[:, sl], "tn"))
        dq_ref[...] = jnp.concatenate(dqs, axis=1).astype(BF16)
        dk_ref[pl.ds(start, KEY_WIN), :] += jnp.concatenate(dks, axis=1)
        dv_ref[pl.ds(start, KEY_WIN), :] += jnp.concatenate(dvs, axis=1)

    col = pl.BlockSpec((PAD_ROWS + S, GROUP_W), lambda g, n: (0, g))
    tile = pl.BlockSpec((Q_TILE, GROUP_W), lambda g, n: (n, g))
    return pl.pallas_call(
        body, name="attn_bwd", grid=(N_GROUPS, S // Q_TILE), in_specs=_attn_specs(S, Q_TILE) + [tile],
        out_specs=[tile, col, col, pl.BlockSpec((HEADS_PER_GROUP, CHUNK, B_BAND), lambda g, n: (g, 0, 0))],
        out_shape=[jax.ShapeDtypeStruct((S, B_W), BF16), jax.ShapeDtypeStruct((PAD_ROWS + S, B_W), F32),
                   jax.ShapeDtypeStruct((PAD_ROWS + S, B_W), F32), jax.ShapeDtypeStruct((B_HEADS, CHUNK, B_BAND), F32)],
        scratch_shapes=[pltpu.VMEM((HEADS_PER_GROUP, Q_TILE, KEY_WIN), F32)],
        compiler_params=_cparams(("parallel", "arbitrary")),
    )(qkv_pad, qkv_pad, qkv_pad, bias, do_b)


def gate_a_bwd(do_a, o_pre, z, norm_w):
    S = o_pre.shape[0]

    def body(d_ref, o_ref, z_ref, nw_ref, dop_ref, dz_ref, dnw_ref):
        nw = nw_ref[...]
        acc = jnp.zeros((1, A_DK), F32)
        for h in range(A_HEADS):
            sl = slice(h * A_DK, (h + 1) * A_DK)
            oh, zh, dh = o_ref[:, sl], z_ref[:, sl].astype(F32), d_ref[:, sl].astype(F32)
            r = lax.rsqrt(jnp.mean(oh * oh, axis=-1, keepdims=True) + RMS_EPS)
            sz, dsz = _silu_and_grad(zh)
            dz_ref[:, sl] = (dh * oh * r * nw * dsz).astype(BF16)
            acc = acc + jnp.sum(dh * oh * r * sz, axis=0, keepdims=True)
            t = dh * nw * sz
            dop_ref[:, sl] = r * t - oh * (r * r * r) * jnp.mean(t * oh, axis=-1, keepdims=True)
        dnw_ref[...] += acc

    return rowcall(body, name="gate_a_bwd", S=S, ts=512,
                   ins=[(do_a, "row"), (o_pre, "row"), (z, "row"), (norm_w, "vec")],
                   outs=[((S, A_W), F32, "row"), ((S, A_W), BF16, "row"), ((1, A_DK), F32, "acc")])


def delta_bwd(q, k, v, beta, g, sprev, tinv, do):
    S = q.shape[0]
    n_chunks = S // CHUNK

    def body(q_ref, k_ref, v_ref, beta_ref, g_ref, sprev_ref, t_ref, do_ref,
             dq_ref, dk_ref, dv_ref, dbeta_ref, dg_ref, dstate_ref):
        @pl.when(pl.program_id(0) == 0)
        def _():
            dstate_ref[...] = jnp.zeros_like(dstate_ref)

        mk = _tri_masks()
        causal, strict, eye = mk["causal"], mk["strict"], mk["eye"]
        blk_end = (lax.broadcasted_iota(jnp.int32, (GROUP_ROWS, 1), 0) & (CHUNK - 1)) == CHUNK - 1
        lane = lax.broadcasted_iota(jnp.int32, (CHUNK, A_HEADS), 1)
        betav, gv = beta_ref[...], g_ref[...]
        dbeta_t = jnp.zeros((CHUNK, A_HEADS), F32)
        dg_t = jnp.zeros((CHUNK, A_HEADS), F32)
        groups, heads = range(N_HEAD_GROUPS), range(HEAD_GROUP)
        st = [dict() for _ in groups]

        def local_part(grp, s):
            s["qs"], s["ks"], s["vs"] = _stack_heads(q_ref, grp), _stack_heads(k_ref, grp), _stack_heads(v_ref, grp)
            s["dos"] = _stack_heads(do_ref, grp)
            s["bs"] = _stack_cols(betav, grp)
            s["loc"] = loc = _delta_local(s["qs"], s["ks"], s["vs"], s["bs"], _stack_cols(gv, grp), mk)
            s["tinv"] = t_ref[0, grp]
            s["rhs"] = jnp.concatenate([loc["vb"], loc["y"]], axis=1)
            s["uw"] = _dot3(s["tinv"], s["rhs"], "nn")

        def state_part(grp, s):
            loc, uw, dos, qs = s["loc"], s["uw"], s["dos"], s["qs"]
            gam, kd, gl, gc = loc["gam"], loc["kd"], loc["gl"], loc["gc"]
            qg = qs * gam
            egl = jnp.exp(gl)
            hid = [grp * HEAD_GROUP + j for j in heads]
            s0 = [sprev_ref[0, h] for h in hid]
            ds1 = [dstate_ref[h] for h in hid]
            w = [_head_rows(uw, j)[:, A_DK:] for j in heads]
            vn = [_head_rows(uw, j)[:, :A_DK] - _dot1(w[j], s0[j], "nn") for j in heads]
            vns = jnp.concatenate(vn, axis=0)
            dvn_local = _dot1(loc["p"], dos, "tn")
            dvn = [_head_rows(dvn_local, j) + _dot1(_head_rows(kd, j), ds1[j], "nn") for j in heads]
            dvns = jnp.concatenate(dvn, axis=0)
            s["dp"] = jnp.where(causal, _dot1(dos, vns, "nt"), 0.0)
            dqg = jnp.concatenate([_dot1(_head_rows(dos, j), s0[j], "nt") for j in heads], axis=0)
            s["dq"] = dqg * gam
            dgc = jnp.sum(dqg * qg, axis=-1, keepdims=True)
            for j in heads:
                dstate_ref[hid[j]] = (_dot1(_head_rows(qg, j), _head_rows(dos, j), "tn")
                                      + egl[(j + 1) * CHUNK - 1:(j + 1) * CHUNK] * ds1[j] - _dot1(w[j], dvn[j], "tn"))
            dkd = jnp.concatenate([_dot1(vn[j], ds1[j], "nt") for j in heads], axis=0)
            s["dk"] = dkd * jnp.exp(gl - gc)
            t1 = jnp.sum(dkd * kd, axis=-1, keepdims=True)
            dgl = jnp.concatenate(
                [jnp.broadcast_to(jnp.sum(_head_rows(t1, j), axis=0, keepdims=True)
                                  + jnp.sum(jnp.sum(ds1[j] * s0[j], axis=-1, keepdims=True), axis=0, keepdims=True)
                                  * egl[(j + 1) * CHUNK - 1:(j + 1) * CHUNK], (CHUNK, 1)) for j in heads], axis=0)
            s["dgc"] = dgc - t1 + jnp.where(blk_end, dgl, 0.0)
            s["duw"] = jnp.concatenate(
                [dvns, jnp.concatenate([-_dot1(dvn[j], s0[j], "nt") for j in heads], axis=0)], axis=1)

        def solve_part(grp, s):
            s["dvby"] = _dot3(s["tinv"], s["duw"], "tn")
            s["dt"] = _dot3(s["duw"], s["rhs"], "nt")

        def inverse_part_a(grp, s):
            s["tdt"] = _dot3(s["tinv"], s["dt"], "tn")

        def inverse_part_b(grp, s):
            s["da"] = jnp.where(strict, -_dot3(s["tdt"], s["tinv"], "nt"), 0.0)

        def finish(grp, s):
            loc, qs, ks, vs, bs, da, dp, dvby = s["loc"], s["qs"], s["ks"], s["vs"], s["bs"], s["da"], s["dp"], s["dvby"]
            gam, decay = loc["gam"], loc["decay"]
            dm = da * decay
            dn = dp * decay
            e = da * loc["a"] + dp * loc["p"]
            dgc = s["dgc"] + jnp.sum(e, axis=1, keepdims=True) - _row_to_col(jnp.sum(e, axis=0, keepdims=True), eye)
            dy = dvby[:, A_DK:]
            dvb = dvby[:, :A_DK]
            dkb = _dot1(dm, ks, "nn") + dy * gam
            dk = s["dk"] + _dot1(dm, loc["kb"], "tn") + _dot1(dn, qs, "tn") + dkb * bs
            dq = s["dq"] + _dot1(dn, ks, "nn")
            dgc = dgc + jnp.sum(dy * loc["y"], axis=-1, keepdims=True)
            dbeta = jnp.sum(dkb * ks, axis=-1, keepdims=True) + jnp.sum(dvb * vs, axis=-1, keepdims=True)
            dv = dvb * bs
            dgs = jnp.sum(jnp.where(mk["upper"], _col_to_row(dgc, eye), 0.0), axis=1, keepdims=True)
            for j in heads:
                h = grp * HEAD_GROUP + j
                sl = slice(h * A_DK, (h + 1) * A_DK)
                dq_ref[:, sl] = _head_rows(dq, j)
                dk_ref[:, sl] = _head_rows(dk, j)
                dv_ref[:, sl] = _head_rows(dv, j)
            s["dbeta"], s["dgs"] = dbeta, dgs

        for stage in (local_part, state_part, solve_part, inverse_part_a, inverse_part_b, finish):
            for grp in groups:
                stage(grp, st[grp])
        for grp in groups:
            for j in heads:
                h = grp * HEAD_GROUP + j
                dbeta_t = dbeta_t + jnp.where(lane == h, _head_rows(st[grp]["dbeta"], j), 0.0)
                dg_t = dg_t + jnp.where(lane == h, _head_rows(st[grp]["dgs"], j), 0.0)
        dbeta_ref[...] = dbeta_t
        dg_ref[...] = dg_t

    rev = lambda n: (n_chunks - 1 - n, 0)
    rev4 = lambda n: (n_chunks - 1 - n, 0, 0, 0)
    tile = pl.BlockSpec((CHUNK, A_W), rev)
    small = pl.BlockSpec((CHUNK, A_HEADS), rev)
    return pl.pallas_call(
        body, name="delta_bwd", grid=(n_chunks,),
        in_specs=[tile, tile, tile, small, small, pl.BlockSpec((1, A_HEADS, A_DK, A_DK), rev4),
                  pl.BlockSpec((1, N_HEAD_GROUPS, GROUP_ROWS, GROUP_ROWS), rev4), tile],
        out_specs=[tile, tile, tile, small, small],
        out_shape=[jax.ShapeDtypeStruct((S, A_W), F32)] * 3 + [jax.ShapeDtypeStruct((S, A_HEADS), F32)] * 2,
        scratch_shapes=[pltpu.VMEM((A_HEADS, A_DK, A_DK), F32)],
        compiler_params=_cparams(("arbitrary",)),
    )(q, k, v, beta, g, sprev, tinv, do)


def _prep_a_dpre(raw, raw_prev, w, dq, dk, dv):
    y, dy_dpre = _prep_a_core(raw, raw_prev, w)
    parts = []
    for h in range(A_HEADS):
        yq = y[:, h * A_DK:(h + 1) * A_DK]
        dqh = dq[:, h * A_DK:(h + 1) * A_DK]
        rq = lax.rsqrt(jnp.sum(yq * yq, axis=-1, keepdims=True) + L2_EPS)
        parts.append((A_DK ** -0.5) * (rq * dqh - yq * (rq * rq * rq) * jnp.sum(dqh * yq, axis=-1, keepdims=True)))
    for h in range(A_HEADS):
        yk = y[:, A_W + h * A_DK:A_W + (h + 1) * A_DK]
        dkh = dk[:, h * A_DK:(h + 1) * A_DK]
        rk = lax.rsqrt(jnp.sum(yk * yk, axis=-1, keepdims=True) + L2_EPS)
        parts.append(rk * dkh - yk * (rk * rk * rk) * jnp.sum(dkh * yk, axis=-1, keepdims=True))
    parts.append(dv)
    return jnp.concatenate(parts, axis=1) * dy_dpre


def prep_a_bwd(qkv_raw, ba, conv_a, a_log, dt_bias, dq, dk, dv, dbeta, dg):
    S = qkv_raw.shape[0]
    ts = 256

    def body(x_ref, xp_ref, xn_ref, ba_ref, w_ref, al_ref, dt_ref, dq_ref, dqn_ref, dk_ref, dkn_ref, dv_ref, dvn_ref,
             dbeta_ref, dg_ref, draw_ref, dba_ref, dw_ref, dal_ref, ddt_ref):
        i = pl.program_id(0)
        first = (i > 0).astype(F32)
        last = (i < pl.num_programs(0) - 1).astype(F32)
        w = w_ref[...]
        cur, prev = x_ref[...].astype(F32), _halo_prev(xp_ref) * first
        dpre = _prep_a_dpre(cur, prev, w, dq_ref[...], dk_ref[...], dv_ref[...])
        dpre_n = _prep_a_dpre(_halo_next(xn_ref), cur[ts - 8:ts], w, _halo_next(dqn_ref), _halo_next(dkn_ref),
                              _halo_next(dvn_ref)) * last
        for j in range(A_CONV):
            dw_ref[j:j + 1, :] += jnp.sum(dpre * _shift_down(cur, prev, A_CONV - 1 - j), axis=0, keepdims=True)
        draw = dpre * w[A_CONV - 1:A_CONV]
        for j in range(A_CONV - 1):
            draw = draw + _shift_up(dpre, dpre_n, A_CONV - 1 - j) * w[j:j + 1]
        draw_ref[...] = draw.astype(BF16)
        bav = ba_ref[...]
        beta = _sigmoid(bav[:, 0:A_HEADS])
        xa = bav[:, A_HEADS:2 * A_HEADS] + dt_ref[...]
        nexp = -jnp.exp(al_ref[...])
        dgv = dg_ref[...]
        da = dgv * nexp * _sigmoid(xa)
        dba_ref[:, 0:A_HEADS] = dbeta_ref[...] * beta * (1.0 - beta)
        dba_ref[:, A_HEADS:2 * A_HEADS] = da
        dal_ref[...] += jnp.sum(dgv * nexp * _softplus(xa), axis=0, keepdims=True)
        ddt_ref[...] += jnp.sum(da, axis=0, keepdims=True)

    return rowcall(
        body, name="prep_a_bwd", S=S, ts=ts,
        ins=[(qkv_raw, "row"), (qkv_raw, "prev"), (qkv_raw, "next"), (ba, "row"), (conv_a, "vec"), (a_log, "vec"),
             (dt_bias, "vec"), (dq, "row"), (dq, "next"), (dk, "row"), (dk, "next"), (dv, "row"), (dv, "next"),
             (dbeta, "row"), (dg, "row")],
        outs=[((S, 3 * A_W), BF16, "row"), ((S, 2 * A_HEADS), F32, "row"), ((A_CONV, 3 * A_W), F32, "acc"),
              ((1, A_HEADS), F32, "acc"), ((1, A_HEADS), F32, "acc")])


def grad_x_final(dh1, x, dxpre1, mod):
    S = x.shape[0]

    def body(dh_ref, x_ref, dx_ref, m_ref, gx_ref, dscale_ref, dshift_ref):
        dh = dh_ref[...]
        gx_ref[...] = ALPHA * dx_ref[...] + dh * (1.0 + m_ref[...][SCALE_T:SCALE_T + 1])
        dscale_ref[...] += jnp.sum(dh * x_ref[...], axis=0, keepdims=True)
        dshift_ref[...] += jnp.sum(dh, axis=0, keepdims=True)

    vec = ((1, D_MODEL), F32, "acc")
    return rowcall(body, name="grad_x_final", S=S, ts=512, ins=[(dh1, "row"), (x, "row"), (dxpre1, "row"), (mod, "vec")],
                   outs=[((S, D_MODEL), F32, "row"), vec, vec])


_C_QKV, _C_Z, _C_BA, _C_QKVB, _C_G = 0, 3 * A_W, 4 * A_W, 4 * A_W + 2 * A_HEADS, 4 * A_W + 2 * A_HEADS + 3 * B_W
BA_PAD = 128


def split_w_in(w_in):
    ba = jnp.pad(w_in[:, _C_BA:_C_QKVB], ((0, 0), (0, BA_PAD - 2 * A_HEADS)))
    return dict(qkv=w_in[:, _C_QKV:_C_Z], z=w_in[:, _C_Z:_C_BA], ba=ba, qkvb=w_in[:, _C_QKVB:_C_G], g=w_in[:, _C_G:])


def join_w_in(p):
    return jnp.concatenate([p["qkv"], p["z"], p["ba"][:, :2 * A_HEADS], p["qkvb"], p["g"]], axis=1)


def forward_local(x, target, mod, w, sm, late_weights=None):
    h1 = modulate(x, mod, SHIFT_T, SCALE_T, "mod_t")
    qkv_raw = mm(h1, w["qkv"], mode="nn", out_dtype=BF16, name="proj_qkv")
    z = mm(h1, w["z"], mode="nn", out_dtype=BF16, name="proj_z")
    ba = mm(h1, w["ba"], mode="nn", out_dtype=F32, name="proj_ba")
    qkvb = mm(h1, w["qkvb"], mode="nn", out_dtype=BF16, name="proj_qkvb")
    gates_raw = mm(h1, w["g"], mode="nn", out_dtype=BF16, name="proj_g")
    q, k, v, beta, g = prep_a_fwd(qkv_raw, ba, sm["conv_a"], sm["a_log"], sm["dt_bias"])
    o_pre, sprev, tinv = delta_fwd(q, k, v, beta, g)
    o_a = gate_a_fwd(o_pre, z, sm["norm_a"])
    qkv_pad = jnp.pad(qkvb, ((PAD_ROWS, 0), (0, 0)))
    bias = jnp.transpose(bias_expand(sm["rel_bias"]), (1, 0, 2))
    o_b = attn_fwd(qkv_pad, bias)
    if late_weights is not None:
        w = dict(w, **late_weights(o_b))
    ya = mm(o_a, w["branch_a"], mode="nn", out_dtype=BF16, name="branch_a")
    yb = mm(o_b, w["branch_b"], mode="nn", out_dtype=BF16, name="branch_b")
    merged = merge_fwd(gates_raw, sm["b_gate"], ya, yb)
    mix = mm(merged, w["o"], mode="nn", out_dtype=F32, name="mix")
    xpre1, x1, h2 = ln1_fwd(x, mix, mod, sm["ln1_g"], sm["ln1_b"])
    up = mm(h2, w["up"], mode="nn", out_dtype=BF16, name="ffn_up", b_shards=True)
    act = ffn_act_fwd(up, sm["conv_ffn"], sm["b_conv_ffn"])
    ffn = mm(act, w["down"], mode="nn", out_dtype=F32, name="ffn_down")
    dxpre2, dffn, loss, dgate_f, dln2_g, dln2_b = final_fwd_bwd(x1, ffn, target, mod, sm["ln2_g"], sm["ln2_b"])
    saved = dict(h1=h1, qkv_raw=qkv_raw, z=z, ba=ba, gates_raw=gates_raw, q=q, k=k, v=v, beta=beta, g=g,
                 o_pre=o_pre, sprev=sprev, tinv=tinv, o_a=o_a, qkv_pad=qkv_pad, bias=bias, o_b=o_b, ya=ya, yb=yb,
                 merged=merged, mix=mix, xpre1=xpre1, x1=x1, h2=h2, up=up, act=act, ffn=ffn, w=w)
    return loss, dxpre2, dffn, dict(gate_f=dgate_f, ln2_g=dln2_g, ln2_b=dln2_b), saved


def backward_local(x, mod, sm, dxpre2, dffn, fin, sv, hooks=None):
    w = sv["w"]
    dact = mm(dffn, w["down"], mode="nt", out_dtype=BF16, name="d_act")
    gw_down = mm(sv["act"], dffn, mode="tn", out_dtype=BF16, name="gw_down")
    dup, dconv_ffn, db_conv_ffn = ffn_act_bwd(dact, sv["up"], sm["conv_ffn"], sm["b_conv_ffn"])
    dh2 = mm(dup, w["up"], mode="nt", out_dtype=F32, name="d_h2", b_shards=True)
    gw_up = mm(sv["h2"], dup, mode="tn", out_dtype=BF16, name="gw_up", out_shards=N_CHIPS)
    dxpre1, dmix, dsc_f, dsh_f, dgate_t, dln1_g, dln1_b = ln1_bwd(
        dxpre2, dh2, sv["xpre1"], sv["mix"], mod, sm["ln1_g"], sm["ln1_b"])
    dmerged = mm(dmix, w["o"], mode="nt", out_dtype=BF16, name="d_merged")
    gw_o = mm(sv["merged"], dmix, mode="tn", out_dtype=BF16, name="gw_o")
    dya, dyb, dgates, db_gate = merge_bwd(dmerged, sv["gates_raw"], sm["b_gate"], sv["ya"], sv["yb"])
    do_a = mm(dya, w["branch_a"], mode="nt", out_dtype=BF16, name="d_oa")
    gw_branch_a = mm(sv["o_a"], dya, mode="tn", out_dtype=BF16, name="gw_branch_a")
    do_b = mm(dyb, w["branch_b"], mode="nt", out_dtype=BF16, name="d_ob")
    gw_branch_b = mm(sv["o_b"], dyb, mode="tn", out_dtype=BF16, name="gw_branch_b")
    bias = sv["bias"]
    if hooks is not None:
        bias = bias + hooks["late_start"](dict(w_branch_a=gw_branch_a, w_branch_b=gw_branch_b, w_o=gw_o, w_up=gw_up,
                                               w_down=gw_down))[0, 0]
    dq_b, dk_pad, dv_pad, dbias = attn_bwd(sv["qkv_pad"], bias, do_b)
    if hooks is not None:
        dbias = dbias + hooks["late_finish"](dq_b)[0, 0]
    dqkvb = jnp.concatenate([dq_b, dk_pad[PAD_ROWS:].astype(BF16), dv_pad[PAD_ROWS:].astype(BF16)], axis=1)
    drel_bias = bias_reduce(jnp.transpose(dbias, (1, 0, 2)))
    do_pre, dz, dnorm_a = gate_a_bwd(do_a, sv["o_pre"], sv["z"], sm["norm_a"])
    dq, dk, dv, dbeta, dg = delta_bwd(sv["q"], sv["k"], sv["v"], sv["beta"], sv["g"], sv["sprev"], sv["tinv"], do_pre)
    dqkv_raw, dba16, dconv_a, da_log, ddt_bias = prep_a_bwd(
        sv["qkv_raw"], sv["ba"], sm["conv_a"], sm["a_log"], sm["dt_bias"], dq, dk, dv, dbeta, dg)
    dba = jnp.pad(dba16, ((0, 0), (0, BA_PAD - 2 * A_HEADS))).astype(BF16)
    pieces = dict(qkv=dqkv_raw, z=dz, ba=dba, qkvb=dqkvb, g=dgates)
    gw_in = join_w_in({key: mm(sv["h1"], dpiece, mode="tn", out_dtype=BF16, name="gw_in_" + key)
                       for key, dpiece in pieces.items()})
    w_ba = w["ba"]
    w_z = w["z"]
    if hooks is not None:
        w_ba = w_ba + hooks["w_in_start"](gw_in)[0, 0].astype(BF16)
    dh1 = mm(pieces["ba"], w_ba, mode="nt", out_dtype=F32, name="d_h1_ba")
    dh1 = mm(pieces["qkv"], w["qkv"], mode="nt", out_dtype=F32, name="d_h1_qkv", acc_in=dh1)
    if hooks is not None:
        w_z = w_z + hooks["w_in_finish"](dh1)[0, 0].astype(BF16)
    dh1 = mm(pieces["z"], w_z, mode="nt", out_dtype=F32, name="d_h1_z", acc_in=dh1)
    for key in ("qkvb", "g"):
        dh1 = mm(pieces[key], w[key], mode="nt", out_dtype=F32, name="d_h1_" + key, acc_in=dh1)
    grad_x, dsc_t, dsh_t = grad_x_final(dh1, x, dxpre1, mod)
    dmod = jnp.concatenate([dsh_t, dsc_t, dgate_t, dsh_f, dsc_f, fin["gate_f"]], axis=0)
    gw = dict(w_in=gw_in, w_branch_a=gw_branch_a, w_branch_b=gw_branch_b, w_o=gw_o, w_up=gw_up, w_down=gw_down)
    gs = dict(b_gate=db_gate, conv_a=dconv_a, a_log=da_log, dt_bias=ddt_bias, norm_a=dnorm_a, rel_bias=drel_bias,
              ln1_g=dln1_g, ln1_b=dln1_b, conv_ffn=dconv_ffn, b_conv_ffn=db_conv_ffn, ln2_g=fin["ln2_g"], ln2_b=fin["ln2_b"])
    return grad_x, dmod, gw, gs


MESH = pl.DeviceIdType.MESH
ANY = pl.BlockSpec(memory_space=pl.ANY)
WHOLE_VMEM = pl.BlockSpec(memory_space=pltpu.VMEM)


def _place():
    return lax.axis_index("x"), lax.axis_index("y"), lax.axis_index("c")


def allgather8(blk, name):
    m_per, n = blk.shape

    def body(x_ref, out_ref, send_sems, recv_sems, local_sem):
        x, y, c = _place()
        me, sibling = (x, y, c), (x, y, 1 - c)
        chips = [(1 - x, y), (x, 1 - y), (1 - x, 1 - y)]

        def rows(px, py, pc):
            return out_ref.at[pl.ds((4 * px + 2 * py + pc) * m_per, m_per), :]

        def copy(k, block, to, src=None):
            return pltpu.make_async_remote_copy(
                src_ref=rows(*block) if src is None else src, dst_ref=rows(*block),
                send_sem=send_sems.at[k], recv_sem=recv_sems.at[k], device_id=to, device_id_type=MESH)

        mine = pltpu.make_async_copy(x_ref, rows(*me), local_sem)
        mine.start()
        first = [copy(0, me, sibling, src=x_ref)]
        first += [copy(1 + j, me, (*chip, c), src=x_ref) for j, chip in enumerate(chips)]
        for cp in first:
            cp.start()
        passed = [copy(4 + j, (*chip, c), sibling) for j, chip in enumerate(chips)]
        for j, chip in enumerate(chips):
            copy(1 + j, (*chip, c), me).wait_recv()
            passed[j].start()
        copy(0, sibling, me).wait_recv()
        for j, chip in enumerate(chips):
            copy(4 + j, (*chip, 1 - c), me).wait_recv()
        for cp in first + passed:
            cp.wait_send()
        mine.wait()

    return pl.pallas_call(
        body, name=name, out_shape=jax.ShapeDtypeStruct((N_DEV * m_per, n), blk.dtype),
        in_specs=[WHOLE_VMEM], out_specs=WHOLE_VMEM,
        scratch_shapes=[pltpu.SemaphoreType.DMA((7,)), pltpu.SemaphoreType.DMA((7,)), pltpu.SemaphoreType.DMA],
    )(blk)


def _chip_peers(x, y):
    return [(1 - x, y), (x, 1 - y), (1 - x, 1 - y)]


def chip_exchange(arrs, name, scatter):
    n = len(arrs)

    def body(*refs):
        ins, outs = refs[:n], refs[n:2 * n]
        send_sems, recv_sems, local_sems = refs[2 * n:]
        x, y, c = _place()
        me = 2 * x + y
        sibling = (x, y, 1 - c)
        peers = _chip_peers(x, y)

        def half(ref, which):
            r2 = ref.shape[0] // 2
            return ref.at[pl.ds(which * r2, r2), :]

        def outgoing(a, chip):
            return ins[a].at[chip] if scatter else ins[a]

        def copy(k, src, dst, to):
            return pltpu.make_async_remote_copy(src_ref=src, dst_ref=dst, send_sem=send_sems.at[k],
                                                recv_sem=recv_sems.at[k], device_id=to, device_id_type=MESH)

        started, local = [], []
        for a in range(n):
            lc = pltpu.make_async_copy(outgoing(a, me), outs[a].at[me], local_sems.at[a])
            lc.start()
            local.append(lc)
            for j, (px, py) in enumerate(peers):
                cp = copy(6 * a + j, half(outgoing(a, 2 * px + py), c), half(outs[a].at[me], c), (px, py, c))
                cp.start()
                started.append(cp)
        for a in range(n):
            for j, (px, py) in enumerate(peers):
                landed = half(outs[a].at[2 * px + py], c)
                copy(6 * a + j, landed, landed, (px, py, c)).wait_recv()
                relay = copy(6 * a + 3 + j, landed, landed, sibling)
                relay.start()
                started.append(relay)
        for a in range(n):
            for j, (px, py) in enumerate(peers):
                other = half(outs[a].at[2 * px + py], 1 - c)
                copy(6 * a + 3 + j, other, other, sibling).wait_recv()
        for cp in started:
            cp.wait_send()
        for lc in local:
            lc.wait()

    out_shape = [jax.ShapeDtypeStruct(a.shape if scatter else (N_CHIPS,) + a.shape, a.dtype) for a in arrs]
    return pl.pallas_call(
        body, name=name, out_shape=out_shape, in_specs=[ANY] * n, out_specs=[ANY] * n,
        scratch_shapes=[pltpu.SemaphoreType.DMA((6 * n,)), pltpu.SemaphoreType.DMA((6 * n,)), pltpu.SemaphoreType.DMA((n,))],
    )(*arrs)


HBM_SPEC = pl.BlockSpec(memory_space=pltpu.HBM)
SEM_SPEC = pl.BlockSpec(memory_space=pltpu.SEMAPHORE)
SIDE_EFFECT = pltpu.SideEffectType.DATAFLOW_SIDE_EFFECTING


def _in_hbm(a):
    return pltpu.with_memory_space_constraint(a, pltpu.HBM)


def exchange_start(arrs, name, scatter, after):
    n = len(arrs)
    lands = [lax.empty(a.shape if scatter else (N_CHIPS,) + a.shape, a.dtype) for a in arrs]

    def body(*refs):
        ins, zones = refs[:n], refs[n:2 * n]
        send_sems, recv_sems, token = refs[2 * n + 1], refs[2 * n + 2], refs[-1]
        x, y, c = _place()
        me = 2 * x + y
        for a in range(n):
            for j, (px, py) in enumerate(_chip_peers(x, y)):
                pltpu.make_async_remote_copy(
                    src_ref=ins[a].at[2 * px + py] if scatter else ins[a], dst_ref=zones[a].at[me],
                    send_sem=send_sems.at[3 * a + j], recv_sem=recv_sems.at[3 * a + j],
                    device_id=(px, py, c), device_id_type=MESH).start()
        token[...] = jnp.zeros_like(token)

    res = pl.pallas_call(
        body, name=name,
        out_shape=[pltpu.SemaphoreType.DMA((3 * n,)), pltpu.SemaphoreType.DMA((3 * n,))]
        + [pltpu.HBM(a.shape, a.dtype) for a in arrs] + [pltpu.HBM(z.shape, z.dtype) for z in lands]
        + [jax.ShapeDtypeStruct((8, 128), F32)],
        in_specs=[HBM_SPEC] * (2 * n) + [ANY], out_specs=[SEM_SPEC, SEM_SPEC] + [HBM_SPEC] * (2 * n) + [WHOLE_VMEM],
        input_output_aliases={i: 2 + i for i in range(2 * n)},
        compiler_params=pltpu.CompilerParams(has_side_effects=SIDE_EFFECT),
    )(*[_in_hbm(a) for a in arrs], *[_in_hbm(z) for z in lands], after)
    return dict(send=res[0], recv=res[1], src=res[2:2 + n], zones=res[2 + n:2 + 2 * n], token=res[-1], scatter=scatter)


def exchange_wait(handle, name, after):
    srcs, zones, scatter = handle["src"], handle["zones"], handle["scatter"]
    n = len(srcs)

    def body(*refs):
        ins, lands = refs[:n], refs[n:2 * n]
        send_sems, recv_sems = refs[2 * n], refs[2 * n + 1]
        x, y, c = _place()
        me = 2 * x + y
        for a in range(n):
            for j, (px, py) in enumerate(_chip_peers(x, y)):
                cp = pltpu.make_async_remote_copy(
                    src_ref=ins[a].at[me] if scatter else ins[a], dst_ref=lands[a].at[2 * px + py],
                    send_sem=send_sems.at[3 * a + j], recv_sem=recv_sems.at[3 * a + j],
                    device_id=(px, py, c), device_id_type=MESH)
                cp.wait_send()
                cp.wait_recv()

    res = pl.pallas_call(
        body, name=name, out_shape=[pltpu.HBM(a.shape, a.dtype) for a in list(srcs) + list(zones)],
        in_specs=[HBM_SPEC] * (2 * n) + [SEM_SPEC, SEM_SPEC, ANY], out_specs=[HBM_SPEC] * (2 * n),
        input_output_aliases={i: i for i in range(2 * n)},
        compiler_params=pltpu.CompilerParams(has_side_effects=SIDE_EFFECT),
    )(*srcs, *zones, handle["send"], handle["recv"], after)
    return res[n:]


def swap_start(arrs, name, after):
    n = len(arrs)
    lands = [lax.empty(a.shape, a.dtype) for a in arrs]

    def body(*refs):
        ins, zones = refs[:n], refs[n:2 * n]
        send_sems, recv_sems, token = refs[2 * n + 1], refs[2 * n + 2], refs[-1]
        x, y, c = _place()
        for a in range(n):
            pltpu.make_async_remote_copy(src_ref=ins[a], dst_ref=zones[a], send_sem=send_sems.at[a], recv_sem=recv_sems.at[a],
                                         device_id=(x, y, 1 - c), device_id_type=MESH).start()
        token[...] = jnp.zeros_like(token)

    res = pl.pallas_call(
        body, name=name,
        out_shape=[pltpu.SemaphoreType.DMA((n,)), pltpu.SemaphoreType.DMA((n,))]
        + [pltpu.HBM(a.shape, a.dtype) for a in arrs] * 2 + [jax.ShapeDtypeStruct((8, 128), F32)],
        in_specs=[HBM_SPEC] * (2 * n) + [ANY], out_specs=[SEM_SPEC, SEM_SPEC] + [HBM_SPEC] * (2 * n) + [WHOLE_VMEM],
        input_output_aliases={i: 2 + i for i in range(2 * n)},
        compiler_params=pltpu.CompilerParams(has_side_effects=SIDE_EFFECT),
    )(*[_in_hbm(a) for a in arrs], *[_in_hbm(z) for z in lands], after)
    return dict(send=res[0], recv=res[1], src=res[2:2 + n], zones=res[2 + n:2 + 2 * n], token=res[-1])


def swap_wait(handle, name, after):
    srcs, zones = handle["src"], handle["zones"]
    n = len(srcs)

    def body(*refs):
        ins, lands = refs[:n], refs[n:2 * n]
        send_sems, recv_sems = refs[2 * n], refs[2 * n + 1]
        x, y, c = _place()
        for a in range(n):
            cp = pltpu.make_async_remote_copy(src_ref=ins[a], dst_ref=lands[a], send_sem=send_sems.at[a],
                                              recv_sem=recv_sems.at[a], device_id=(x, y, 1 - c), device_id_type=MESH)
            cp.wait_send()
            cp.wait_recv()

    res = pl.pallas_call(
        body, name=name, out_shape=[pltpu.HBM(a.shape, a.dtype) for a in list(srcs) + list(zones)],
        in_specs=[HBM_SPEC] * (2 * n) + [SEM_SPEC, SEM_SPEC, ANY], out_specs=[HBM_SPEC] * (2 * n),
        input_output_aliases={i: i for i in range(2 * n)},
        compiler_params=pltpu.CompilerParams(has_side_effects=SIDE_EFFECT),
    )(*srcs, *zones, handle["send"], handle["recv"], after)
    return res[:n], res[n:]


TILE_BYTES = 2 * 1024 * 1024


def _row_tile(rows, row_bytes):
    if rows * row_bytes <= TILE_BYTES or rows % 8:
        return rows
    best = 8
    for t in range(8, rows + 1, 8):
        if rows % t == 0 and t * row_bytes <= TILE_BYTES:
            best = t
    return best


def pair_add(a, b, name):
    shape = a.shape
    a, b = a.reshape(-1, shape[-1]), b.reshape(-1, shape[-1])
    R, C = a.shape
    tr = _row_tile(R, C * 4)

    def body(a_ref, b_ref, o_ref):
        o_ref[...] = (a_ref[...].astype(F32) + b_ref[...].astype(F32)).astype(BF16)

    spec = pl.BlockSpec((tr, C), lambda i: (i, 0))
    return pl.pallas_call(body, name=name, grid=(R // tr,), in_specs=[spec, spec], out_specs=spec,
                          out_shape=jax.ShapeDtypeStruct((R, C), BF16), compiler_params=_cparams(("parallel",)))(a, b).reshape(shape)


def sum_lead(parts, name):
    K, R, C = parts.shape
    tr = _row_tile(R, C * 4)

    def body(p_ref, o_ref):
        acc = p_ref[0].astype(F32)
        for j in range(1, K):
            acc = acc + p_ref[j].astype(F32)
        o_ref[...] = acc

    return pl.pallas_call(
        body, name=name, grid=(R // tr,), in_specs=[pl.BlockSpec((K, tr, C), lambda i: (0, i, 0))],
        out_specs=pl.BlockSpec((tr, C), lambda i: (i, 0)), out_shape=jax.ShapeDtypeStruct((R, C), F32),
        compiler_params=_cparams(("parallel",)))(parts)


def adamw(w, g, m, v, name):
    R, C = w.shape
    tr = _row_tile(R, C * 4)

    def body(w_ref, g_ref, m_ref, v_ref, d_ref, mo_ref, vo_ref):
        gv = g_ref[...]
        m2 = ADAM_B1 * m_ref[...] + (1.0 - ADAM_B1) * gv
        v2 = ADAM_B2 * v_ref[...] + (1.0 - ADAM_B2) * (gv * gv)
        m_hat = m2 / (1.0 - ADAM_B1 ** ADAM_STEP)
        v_hat = v2 / (1.0 - ADAM_B2 ** ADAM_STEP)
        d_ref[...] = -ADAM_LR * (m_hat / (jnp.sqrt(v_hat) + ADAM_EPS) + ADAM_WD * w_ref[...])
        mo_ref[...] = m2
        vo_ref[...] = v2

    spec = pl.BlockSpec((tr, C), lambda i: (i, 0))
    return pl.pallas_call(body, name=name, grid=(R // tr,), in_specs=[spec] * 4, out_specs=[spec] * 3,
                          out_shape=[jax.ShapeDtypeStruct((R, C), F32)] * 3, compiler_params=_cparams(("parallel",)))(w, g, m, v)


LANES = 1024


def _pack(arrs, rows):
    out, offs, r = [], [], 0
    for a in arrs:
        flat = a.reshape(-1)
        nr = -(-flat.shape[0] // LANES)
        out.append(jnp.pad(flat, (0, nr * LANES - flat.shape[0])))
        offs.append(r)
        r += nr
    assert r <= rows, (r, rows)
    out.append(jnp.zeros(((rows - r) * LANES,), F32))
    return jnp.concatenate(out).reshape(rows, LANES), offs


def _unpack(packed, offs, shapes):
    flat = packed.reshape(-1)
    return [flat[o * LANES:o * LANES + math.prod(s)].reshape(s) for o, s in zip(offs, shapes)]


WEIGHTS = ["w_ada", "b_ada", "w_in", "b_gate", "conv_a", "a_log", "dt_bias", "norm_a", "rel_bias", "w_branch_a",
           "w_branch_b", "w_o", "ln1_g", "ln1_b", "w_up", "conv_ffn", "b_conv_ffn", "w_down", "ln2_g", "ln2_b"]
BIG = ["w_in", "w_branch_a", "w_branch_b", "w_o", "w_up", "w_down"]
LATE = [n for n in BIG if n != "w_in"]
KEPT_SHARDED = {"w_up"}
COL_SHARDED = {"w_in", "w_up"}
SMALL_SHARDED = {"conv_a": 3 * A_W // N_CHIPS, "rel_bias": B_REL // N_CHIPS, "conv_ffn": 2 * D_FF // N_CHIPS}
SMALL = [n for n in WEIGHTS if n not in BIG and n != "w_ada"]


def _to_full(g4, name):
    if name in KEPT_SHARDED:
        return g4
    if name in COL_SHARDED:
        return jnp.transpose(g4, (1, 0, 2)).reshape(g4.shape[1], -1)
    return g4.reshape(-1, g4.shape[2])


def _to_shards(full, name):
    if name in KEPT_SHARDED:
        return full
    if name in COL_SHARDED:
        return jnp.transpose(full.reshape(full.shape[0], N_CHIPS, -1), (1, 0, 2))
    return full.reshape(N_CHIPS, -1, full.shape[1])


def kernel(x, c, w_ada, b_ada, w_in, b_gate, conv_a, a_log, dt_bias, norm_a, rel_bias, w_branch_a, w_branch_b, w_o, ln1_g, ln1_b, w_up, conv_ffn, b_conv_ffn, w_down, ln2_g, ln2_b, loss_target, m_w_ada, m_b_ada, m_w_in, m_b_gate, m_conv_a, m_a_log, m_dt_bias, m_norm_a, m_rel_bias, m_w_branch_a, m_w_branch_b, m_w_o, m_ln1_g, m_ln1_b, m_w_up, m_conv_ffn, m_b_conv_ffn, m_w_down, m_ln2_g, m_ln2_b, v_w_ada, v_b_ada, v_w_in, v_b_gate, v_conv_a, v_a_log, v_dt_bias, v_norm_a, v_rel_bias, v_w_branch_a, v_w_branch_b, v_w_o, v_ln1_g, v_ln1_b, v_w_up, v_conv_ffn, v_b_conv_ffn, v_w_down, v_ln2_g, v_ln2_b):
    args = dict(locals())
    wts = {n: args[n] for n in WEIGHTS}
    moms = {n: args["m_" + n] for n in WEIGHTS}
    vars_ = {n: args["v_" + n] for n in WEIGHTS}
    xi, yi, ci = _place()
    chip = 2 * xi + yi
    dev = 4 * xi + 2 * yi + ci
    ada_cols = w_ada.shape[2]

    sshapes = [wts[n].shape[1:] for n in SMALL_SHARDED]
    spack, soffs = _pack([wts[n][0] for n in SMALL_SHARDED], 16)
    first = allgather8(jnp.concatenate([jnp.pad(c, ((0, 7), (0, 0))), spack]), "gather_c_small_w").reshape(N_DEV, 24, LANES)
    c_all = first[:, 0]
    b_ada_sh = lax.dynamic_slice(b_ada, (0, chip * ada_cols), (1, ada_cols))
    mod_sh = ada_fwd(c_all, w_ada[0], b_ada_sh)
    mod_g = allgather8(mod_sh, "gather_mod").reshape(N_CHIPS, 2, N_DEV, ada_cols)[:, 0]
    mod = lax.dynamic_slice(mod_g, (0, dev, 0), (N_CHIPS, 1, ada_cols)).reshape(6, D_MODEL)

    (w_in_g4,) = chip_exchange([wts["w_in"][0].astype(BF16)], "gather_w_in", scatter=False)
    wd = split_w_in(_to_full(w_in_g4, "w_in"))
    late_shards = [wts[n][0].astype(BF16) for n in LATE]
    late_gather = exchange_start(late_shards, "gather_late_start", scatter=False, after=w_in_g4)
    mod = mod + late_gather["token"][0, 0]

    def late_weights(after):
        zones = exchange_wait(late_gather, "gather_late_wait", after)
        full = [_to_full(lax.dynamic_update_slice(z, s[None], (chip, 0, 0)), n) for n, z, s in zip(LATE, zones, late_shards)]
        return {n[2:]: f for n, f in zip(LATE, full)}

    sg = first[::2, 8:]
    sparts = [_unpack(sg[j], soffs, sshapes) for j in range(N_CHIPS)]
    sm = {n: wts[n] for n in SMALL if n not in SMALL_SHARDED and n != "b_ada"}
    for i, n in enumerate(SMALL_SHARDED):
        sm[n] = jnp.concatenate([sparts[j][i] for j in range(N_CHIPS)], axis=-1)

    early = {}

    def late_start(g):
        early["swap"] = swap_start([g[n] for n in LATE], "grad_swap_late_start", g[LATE[0]])
        return early["swap"]["token"]

    def late_finish(after):
        mine, theirs = swap_wait(early["swap"], "grad_swap_late_wait", after)
        early["sums"] = [_to_shards(pair_add(a, b, "grad_pair_" + n), n) for n, a, b in zip(LATE, mine, theirs)]
        early["scatter"] = exchange_start(early["sums"], "grad_scatter_start", scatter=True, after=theirs[0])
        return early["scatter"]["token"]

    def w_in_start(g):
        early["swap_in"] = swap_start([g], "grad_swap_w_in_start", g)
        return early["swap_in"]["token"]

    def w_in_finish(after):
        (mine,), (theirs,) = swap_wait(early["swap_in"], "grad_swap_w_in_wait", after)
        early["sum_in"] = _to_shards(pair_add(mine, theirs, "grad_pair_w_in"), "w_in")
        early["scatter_in"] = exchange_start([early["sum_in"]], "grad_scatter_w_in_start", scatter=True, after=theirs)
        return early["scatter_in"]["token"]

    hooks = dict(late_start=late_start, late_finish=late_finish, w_in_start=w_in_start, w_in_finish=w_in_finish)
    loss, dxpre2, dffn, fin, sv = forward_local(x[0], loss_target[0], mod, wd, sm, late_weights)
    grad_x, dmod, gw, gs = backward_local(x[0], mod, sm, dxpre2, dffn, fin, sv, hooks)

    gnames = [n for n in SMALL if n != "b_ada"]
    vec, voffs = _pack([dmod] + [gs[n] for n in gnames] + [loss], 56)
    gathered = allgather8(vec, "gather_small_g").reshape(N_DEV, 56, LANES)
    summed = sum_lead(gathered, "sum_small_g")
    full_shapes = [(6, D_MODEL)] + [gs[n].shape for n in gnames] + [(1, 1)]
    parts = _unpack(summed, voffs, full_shapes)
    grads = {"b_ada": parts[0].reshape(1, -1)}
    for n, p in zip(gnames, parts[1:-1]):
        if n in SMALL_SHARDED:
            p = lax.dynamic_slice_in_dim(p, chip * SMALL_SHARDED[n], SMALL_SHARDED[n], axis=1)
        grads[n] = p.reshape(wts[n].shape)
    loss_total = parts[-1].reshape(())
    dmod_all = gathered[:, 0:6, :].reshape(N_DEV, 6 * D_MODEL)
    grads["w_ada"] = ada_bwd(c_all, lax.dynamic_slice(dmod_all, (0, chip * ada_cols), (N_DEV, ada_cols)))[None]

    def own_slot(zone, sums):
        return lax.dynamic_update_slice(zone, lax.dynamic_slice_in_dim(sums, chip, 1, axis=0), (chip, 0, 0))

    zones = exchange_wait(early["scatter"], "grad_scatter_wait", summed)
    for n, z, s in zip(LATE, zones, early["sums"]):
        grads[n] = sum_lead(own_slot(z, s), "grad_sum_" + n)[None]

    delta, new_m, new_v = {}, {}, {}

    def update(n):
        d, m2, v2 = adamw(wts[n][0], grads[n][0], moms[n][0], vars_[n][0], "adamw_" + n)
        delta[n], new_m[n], new_v[n] = d[None], m2[None], v2[None]

    for n in ["w_ada"] + LATE:
        update(n)
    shapes = [wts[n].shape for n in SMALL]
    packs = [_pack([t[n] for n in SMALL], 32) for t in (wts, grads, moms, vars_)]
    outs = adamw(*[p[0] for p in packs], "adamw_small")
    for res, o in zip((delta, new_m, new_v), outs):
        for n, a in zip(SMALL, _unpack(o, packs[0][1], shapes)):
            res[n] = a
    (zone_in,) = exchange_wait(early["scatter_in"], "grad_scatter_w_in_wait", outs[0])
    grads["w_in"] = sum_lead(own_slot(zone_in, early["sum_in"]), "grad_sum_w_in")[None]
    update("w_in")
    return (loss_total, grad_x[None], *[grads[n] for n in WEIGHTS], *[delta[n] for n in WEIGHTS],
            *[new_m[n] for n in WEIGHTS], *[new_v[n] for n in WEIGHTS])
```

```python
import functools
import math

import jax
import jax.numpy as jnp
from jax import lax
from jax.experimental import pallas as pl
from jax.experimental.pallas import tpu as pltpu

F32 = jnp.float32
BF16 = jnp.bfloat16

D_MODEL = 1024
CHUNK = 64
A_HEADS = 8
A_DK = 128
A_W = A_HEADS * A_DK
A_CONV = 4
B_HEADS = 16
B_DH = 64
B_W = B_HEADS * B_DH
B_PREV = 8
B_BAND = (B_PREV + 1) * CHUNK
B_MAX_REL = 256
B_REL = CHUNK - 1 + B_MAX_REL + 1
D_FF = 2816
FFN_CONV = 3
IN_COLS = 4 * A_W + 2 * A_HEADS + 3 * B_W + 2 * D_MODEL
ALPHA = 2.0 ** 0.25
LN_EPS = 1e-5
RMS_EPS = 1e-6
L2_EPS = 1e-6
NEG_INF = -1e30
ADAM_LR, ADAM_B1, ADAM_B2, ADAM_EPS, ADAM_WD, ADAM_STEP = 0.001, 0.9, 0.999, 1e-08, 0.01, 10
N_CHIPS = 4
N_DEV = 8
VMEM_LIMIT = 56 * 1024 * 1024


def _cparams(sem=None):
    return pltpu.CompilerParams(dimension_semantics=sem, vmem_limit_bytes=VMEM_LIMIT)


_DIMS = {"nn": (((1,), (0,)), ((), ())), "nt": (((1,), (1,)), ((), ())), "tn": (((0,), (0,)), ((), ()))}


MM_TILE_CAP = 1408


def _mm_tile(n):
    return max(t for t in range(128, min(n, MM_TILE_CAP) + 1, 128) if n % t == 0)


def mm(a, b, *, mode, out_dtype, name, acc_in=None, b_shards=False, out_shards=0):
    b_rows, b_cols = (b.shape[1], b.shape[0] * b.shape[2]) if b_shards else b.shape
    if mode == "nn":
        (M, K), (K2, N) = a.shape, (b_rows, b_cols)
    elif mode == "nt":
        (M, K), (N, K2) = a.shape, (b_rows, b_cols)
    else:
        (K, M), (K2, N) = a.shape, (b_rows, b_cols)
    assert K == K2, (a.shape, b.shape, mode)
    tm, tn, tk = _mm_tile(M), _mm_tile(N), _mm_tile(K)
    nk = K // tk

    def body(*refs):
        if acc_in is None:
            a_ref, b_ref, o_ref, acc_ref = refs
        else:
            a_ref, b_ref, c_ref, o_ref, acc_ref = refs
        k = pl.program_id(2)

        @pl.when(k == 0)
        def _():
            if acc_in is None:
                acc_ref[...] = jnp.zeros_like(acc_ref)
            else:
                acc_ref[...] = c_ref[...]

        acc_ref[...] += lax.dot_general(a_ref[...].astype(BF16), b_ref[...].astype(BF16), _DIMS[mode],
                                        preferred_element_type=F32)

        @pl.when(k == nk - 1)
        def _():
            o_ref[...] = acc_ref[...].astype(out_dtype)

    a_spec = pl.BlockSpec((tk, tm), lambda i, j, k: (k, i)) if mode == "tn" else pl.BlockSpec((tm, tk), lambda i, j, k: (i, k))
    if b_shards:
        assert (tk if mode == "nt" else tn) == b.shape[2] and mode != "tn", (b.shape, tn, tk, mode)
        b_spec = (pl.BlockSpec((None, tn, tk), lambda i, j, k: (k, j, 0)) if mode == "nt"
                  else pl.BlockSpec((None, tk, tn), lambda i, j, k: (j, k, 0)))
    else:
        b_spec = pl.BlockSpec((tn, tk), lambda i, j, k: (j, k)) if mode == "nt" else pl.BlockSpec((tk, tn), lambda i, j, k: (k, j))
    o_spec = pl.BlockSpec((tm, tn), lambda i, j, k: (i, j))
    out_shape = jax.ShapeDtypeStruct((M, N), out_dtype)
    if out_shards:
        assert N == out_shards * tn and acc_in is None, (N, tn, out_shards)
        o_spec = pl.BlockSpec((None, tm, tn), lambda i, j, k: (j, i, 0))
        out_shape = jax.ShapeDtypeStruct((out_shards, M, tn), out_dtype)
    ins, in_specs, aliases = [a, b], [a_spec, b_spec], {}
    if acc_in is not None:
        assert acc_in.shape == (M, N) and acc_in.dtype == F32 and out_dtype == F32
        ins.append(acc_in)
        in_specs.append(o_spec)
        aliases = {2: 0}
    return pl.pallas_call(
        body, name=name, grid=(M // tm, N // tn, nk), in_specs=in_specs, out_specs=o_spec,
        out_shape=out_shape, scratch_shapes=[pltpu.VMEM((tm, tn), F32)],
        input_output_aliases=aliases, compiler_params=_cparams(("parallel", "parallel", "arbitrary")),
    )(*ins)


def rowcall(body, *, name, S, ts, ins, outs, scratch=()):
    assert S % ts == 0 and ts % 16 == 0
    nsteps = S // ts
    in_specs, arrays = [], []
    for arr, kind in ins:
        arrays.append(arr)
        if kind == "row":
            in_specs.append(pl.BlockSpec((ts, arr.shape[1]), lambda i: (i, 0)))
        elif kind in ("prev", "next"):
            hr = 8 * (4 // arr.dtype.itemsize)
            per, last = ts // hr, S // hr - 1
            if kind == "prev":
                in_specs.append(pl.BlockSpec((hr, arr.shape[1]), lambda i, per=per: (jnp.maximum(i * per - 1, 0), 0)))
            else:
                in_specs.append(pl.BlockSpec((hr, arr.shape[1]), lambda i, per=per, last=last: (jnp.minimum((i + 1) * per, last), 0)))
        else:
            nd = arr.ndim
            in_specs.append(pl.BlockSpec(arr.shape, lambda i, nd=nd: (0,) * nd))
    out_specs, out_shapes, acc_idx = [], [], []
    for n, (shape, dtype, kind) in enumerate(outs):
        out_shapes.append(jax.ShapeDtypeStruct(shape, dtype))
        if kind == "row":
            out_specs.append(pl.BlockSpec((ts, shape[1]), lambda i: (i, 0)))
        else:
            nd = len(shape)
            out_specs.append(pl.BlockSpec(shape, lambda i, nd=nd: (0,) * nd))
            acc_idx.append(n)
    n_in = len(arrays)

    def wrapped(*refs):
        @pl.when(pl.program_id(0) == 0)
        def _():
            for n in acc_idx:
                refs[n_in + n][...] = jnp.zeros_like(refs[n_in + n])

        body(*refs)

    res = pl.pallas_call(
        wrapped, name=name, grid=(nsteps,), in_specs=in_specs, out_specs=out_specs, out_shape=out_shapes,
        scratch_shapes=list(scratch), compiler_params=_cparams(("arbitrary",) if acc_idx else ("parallel",)),
    )(*arrays)
    return res


def _halo_prev(ref):
    v = ref[...].astype(F32)
    return v[v.shape[0] - 8:]


def _halo_next(ref):
    return ref[...].astype(F32)[:8]


def _shift_down(cur, prev8, k):
    if k == 0:
        return cur
    rolled = pltpu.roll(cur, k, axis=0)
    fix = pltpu.roll(prev8, k, axis=0)
    row = lax.broadcasted_iota(jnp.int32, (8, 1), 0)
    top = jnp.where(row < k, fix, rolled[0:8])
    if cur.shape[0] == 8:
        return top
    return jnp.concatenate([top, rolled[8:]], axis=0)


def _shift_up(cur, next8, k):
    if k == 0:
        return cur
    n = cur.shape[0]
    rolled = pltpu.roll(cur, n - k, axis=0)
    fix = pltpu.roll(next8, 8 - k, axis=0)
    row = lax.broadcasted_iota(jnp.int32, (8, 1), 0)
    bot = jnp.where(row >= 8 - k, fix, rolled[n - 8:n])
    return jnp.concatenate([rolled[:n - 8], bot], axis=0)


def _sigmoid(x):
    return 1.0 / (1.0 + jnp.exp(-x))


def _silu(x):
    return x * _sigmoid(x)


def _silu_and_grad(x):
    s = _sigmoid(x)
    return x * s, s * (1.0 + x * (1.0 - s))


def _softplus(x):
    return jnp.maximum(x, 0.0) + jnp.log1p(jnp.exp(-jnp.abs(x)))


def _split2(x):
    hi = x.astype(BF16)
    return hi, (x - hi.astype(F32)).astype(BF16)


def _dot1(a, b, mode):
    return lax.dot_general(a.astype(BF16), b.astype(BF16), _DIMS[mode], preferred_element_type=F32)


def _dot3(a, b, mode):
    ah, al = _split2(a)
    bh, bl = _split2(b)
    d = lambda p, q: lax.dot_general(p, q, _DIMS[mode], preferred_element_type=F32)
    return d(ah, bh) + (d(ah, bl) + d(al, bh))


def ada_fwd(c_all, w_sh, b_sh):
    n = w_sh.shape[1]
    tn = 512

    def body(c_ref, w_ref, b_ref, o_ref):
        o_ref[...] = _dot1(_silu(c_ref[...]), w_ref[...], "nn") + b_ref[...]

    return pl.pallas_call(
        body, name="ada_fwd", grid=(n // tn,),
        in_specs=[pl.BlockSpec((N_DEV, D_MODEL), lambda j: (0, 0)), pl.BlockSpec((D_MODEL, tn), lambda j: (0, j)),
                  pl.BlockSpec((1, tn), lambda j: (0, j))],
        out_specs=pl.BlockSpec((N_DEV, tn), lambda j: (0, j)), out_shape=jax.ShapeDtypeStruct((N_DEV, n), F32),
        compiler_params=_cparams(("parallel",)),
    )(c_all, w_sh, b_sh)


def ada_bwd(c_all, dmod_sh):
    n = dmod_sh.shape[1]
    tn = 512

    def body(c_ref, d_ref, o_ref):
        o_ref[...] = _dot1(_silu(c_ref[...]), d_ref[...], "tn")

    return pl.pallas_call(
        body, name="ada_bwd", grid=(n // tn,),
        in_specs=[pl.BlockSpec((N_DEV, D_MODEL), lambda j: (0, 0)), pl.BlockSpec((N_DEV, tn), lambda j: (0, j))],
        out_specs=pl.BlockSpec((D_MODEL, tn), lambda j: (0, j)), out_shape=jax.ShapeDtypeStruct((D_MODEL, n), F32),
        compiler_params=_cparams(("parallel",)),
    )(c_all, dmod_sh)


SHIFT_T, SCALE_T, GATE_T, SHIFT_F, SCALE_F, GATE_F = range(6)


def modulate(x, mod, shift_row, scale_row, name):
    S = x.shape[0]

    def body(x_ref, m_ref, o_ref):
        m = m_ref[...]
        o_ref[...] = (x_ref[...] * (1.0 + m[scale_row:scale_row + 1]) + m[shift_row:shift_row + 1]).astype(BF16)

    return rowcall(body, name=name, S=S, ts=512, ins=[(x, "row"), (mod, "vec")], outs=[((S, D_MODEL), BF16, "row")])[0]


def _conv_fwd(cur, prev, w, width):
    y = cur * w[width - 1:width]
    for j in range(width - 1):
        y = y + _shift_down(cur, prev, width - 1 - j) * w[j:j + 1]
    return y


def _prep_a_core(cur, prev, w):
    return _silu_and_grad(_conv_fwd(cur, prev, w, A_CONV))


def prep_a_fwd(qkv_raw, ba, conv_a, a_log, dt_bias):
    S = qkv_raw.shape[0]

    def body(x_ref, xp_ref, ba_ref, w_ref, al_ref, dt_ref, q_ref, k_ref, v_ref, beta_ref, g_ref):
        first = (pl.program_id(0) > 0).astype(F32)
        y, _ = _prep_a_core(x_ref[...].astype(F32), _halo_prev(xp_ref) * first, w_ref[...])
        for h in range(A_HEADS):
            sl = slice(h * A_DK, (h + 1) * A_DK)
            qh = y[:, sl]
            kh = y[:, A_W + h * A_DK:A_W + (h + 1) * A_DK]
            q_ref[:, sl] = qh * (lax.rsqrt(jnp.sum(qh * qh, axis=-1, keepdims=True) + L2_EPS) * (A_DK ** -0.5))
            k_ref[:, sl] = kh * lax.rsqrt(jnp.sum(kh * kh, axis=-1, keepdims=True) + L2_EPS)
        v_ref[...] = y[:, 2 * A_W:3 * A_W]
        bav = ba_ref[...]
        beta_ref[...] = _sigmoid(bav[:, 0:A_HEADS])
        g_ref[...] = -jnp.exp(al_ref[...]) * _softplus(bav[:, A_HEADS:2 * A_HEADS] + dt_ref[...])

    return rowcall(
        body, name="prep_a_fwd", S=S, ts=256,
        ins=[(qkv_raw, "row"), (qkv_raw, "prev"), (ba, "row"), (conv_a, "vec"), (a_log, "vec"), (dt_bias, "vec")],
        outs=[((S, A_W), F32, "row")] * 3 + [((S, A_HEADS), F32, "row")] * 2)


HEAD_GROUP = 2
GROUP_ROWS = HEAD_GROUP * CHUNK
N_HEAD_GROUPS = A_HEADS // HEAD_GROUP
LOG_CHUNK = int(math.log2(CHUNK))


def _tri_masks():
    rb = lax.broadcasted_iota(jnp.int32, (GROUP_ROWS, GROUP_ROWS), 0)
    cb = lax.broadcasted_iota(jnp.int32, (GROUP_ROWS, GROUP_ROWS), 1)
    same = (rb >> LOG_CHUNK) == (cb >> LOG_CHUNK)
    return dict(causal=same & (rb >= cb), strict=same & (rb > cb), eye=rb == cb, upper=same & (cb >= rb),
                last=cb == (rb | (CHUNK - 1)), rb=rb, cb=cb)


def _col_to_row(colv, eye):
    return jnp.sum(jnp.where(eye, colv, 0.0), axis=0, keepdims=True)


def _row_to_col(rowv, eye):
    return jnp.sum(jnp.where(eye, rowv, 0.0), axis=1, keepdims=True)


def _tri_inv(a_list, mk):
    rb, cb = mk["rb"], mk["cb"]
    ts = [jnp.where(mk["eye"], 1.0, 0.0) - jnp.where((rb >> 1) == (cb >> 1), a, 0.0) for a in a_list]
    for lvl in range(1, LOG_CHUNK):
        rs, cs = rb >> lvl, cb >> lvl
        sel = ((rs & 1) == 1) & (cs == rs - 1)
        inner = [_dot3(t, jnp.where(sel, a, 0.0), "nn") for t, a in zip(ts, a_list)]
        ts = [t - _dot3(i, t, "nn") for i, t in zip(inner, ts)]
    return ts


def _stack_heads(ref, grp):
    return jnp.concatenate([ref[:, (grp * HEAD_GROUP + j) * A_DK:(grp * HEAD_GROUP + j + 1) * A_DK]
                            for j in range(HEAD_GROUP)], axis=0)


def _stack_cols(tile, grp):
    return jnp.concatenate([tile[:, grp * HEAD_GROUP + j:grp * HEAD_GROUP + j + 1] for j in range(HEAD_GROUP)], axis=0)


def _delta_local(q, k, v, beta, g, mk):
    causal, strict, eye = mk["causal"], mk["strict"], mk["eye"]
    g_row = _col_to_row(g, eye)
    gc = jnp.sum(jnp.where(causal, g_row, 0.0), axis=1, keepdims=True)
    gc_row = _col_to_row(gc, eye)
    decay = jnp.where(causal, jnp.exp(jnp.where(causal, gc - gc_row, 0.0)), 0.0)
    gam = jnp.exp(gc)
    kb = k * beta
    vb = v * beta
    y = kb * gam
    a = jnp.where(strict, _dot1(kb, k, "nt") * decay, 0.0)
    p = _dot1(q, k, "nt") * decay
    gl = jnp.sum(jnp.where(mk["last"], gc_row, 0.0), axis=1, keepdims=True)
    kd = k * jnp.exp(gl - gc)
    return dict(gc=gc, decay=decay, gam=gam, kb=kb, vb=vb, y=y, a=a, p=p, gl=gl, kd=kd)


def _head_rows(x, j):
    return x[j * CHUNK:(j + 1) * CHUNK]


def delta_fwd(q, k, v, beta, g):
    S = q.shape[0]
    n_chunks = S // CHUNK

    def body(q_ref, k_ref, v_ref, beta_ref, g_ref, o_ref, sprev_ref, t_ref, state_ref):
        @pl.when(pl.program_id(0) == 0)
        def _():
            state_ref[...] = jnp.zeros_like(state_ref)

        mk = _tri_masks()
        betav, gv = beta_ref[...], g_ref[...]
        groups = range(N_HEAD_GROUPS)
        q_all = [_stack_heads(q_ref, grp) for grp in groups]
        locs = [_delta_local(q_all[grp], _stack_heads(k_ref, grp), _stack_heads(v_ref, grp),
                             _stack_cols(betav, grp), _stack_cols(gv, grp), mk) for grp in groups]
        tinvs = _tri_inv([loc["a"] for loc in locs], mk)
        uws = [_dot3(tinvs[grp], jnp.concatenate([locs[grp]["vb"], locs[grp]["y"]], axis=1), "nn") for grp in groups]
        for grp in groups:
            loc, uw = locs[grp], uws[grp]
            t_ref[0, grp] = tinvs[grp]
            qg = q_all[grp] * loc["gam"]
            egl = jnp.exp(loc["gl"])
            vns, o_state = [], []
            for j in range(HEAD_GROUP):
                h = grp * HEAD_GROUP + j
                s0 = state_ref[h]
                sprev_ref[0, h] = s0
                uw_h = _head_rows(uw, j)
                vn = uw_h[:, :A_DK] - _dot1(uw_h[:, A_DK:], s0, "nn")
                vns.append(vn)
                o_state.append(_dot1(_head_rows(qg, j), s0, "nn"))
                state_ref[h] = s0 * egl[(j + 1) * CHUNK - 1:(j + 1) * CHUNK] + _dot1(_head_rows(loc["kd"], j), vn, "tn")
            o_local = _dot1(loc["p"], jnp.concatenate(vns, axis=0), "nn")
            for j in range(HEAD_GROUP):
                h = grp * HEAD_GROUP + j
                o_ref[:, h * A_DK:(h + 1) * A_DK] = o_state[j] + _head_rows(o_local, j)

    tile = pl.BlockSpec((CHUNK, A_W), lambda n: (n, 0))
    small = pl.BlockSpec((CHUNK, A_HEADS), lambda n: (n, 0))
    return pl.pallas_call(
        body, name="delta_fwd", grid=(n_chunks,), in_specs=[tile, tile, tile, small, small],
        out_specs=[tile, pl.BlockSpec((1, A_HEADS, A_DK, A_DK), lambda n: (n, 0, 0, 0)),
                   pl.BlockSpec((1, N_HEAD_GROUPS, GROUP_ROWS, GROUP_ROWS), lambda n: (n, 0, 0, 0))],
        out_shape=[jax.ShapeDtypeStruct((S, A_W), F32), jax.ShapeDtypeStruct((n_chunks, A_HEADS, A_DK, A_DK), F32),
                   jax.ShapeDtypeStruct((n_chunks, N_HEAD_GROUPS, GROUP_ROWS, GROUP_ROWS), F32)],
        scratch_shapes=[pltpu.VMEM((A_HEADS, A_DK, A_DK), F32)],
        compiler_params=_cparams(("arbitrary",)),
    )(q, k, v, beta, g)


def gate_a_fwd(o_pre, z, norm_w):
    S = o_pre.shape[0]

    def body(o_ref, z_ref, nw_ref, out_ref):
        nw = nw_ref[...]
        for h in range(A_HEADS):
            sl = slice(h * A_DK, (h + 1) * A_DK)
            oh = o_ref[:, sl]
            r = lax.rsqrt(jnp.mean(oh * oh, axis=-1, keepdims=True) + RMS_EPS)
            out_ref[:, sl] = (oh * r * nw * _silu(z_ref[:, sl].astype(F32))).astype(BF16)

    return rowcall(body, name="gate_a_fwd", S=S, ts=512, ins=[(o_pre, "row"), (z, "row"), (norm_w, "vec")],
                   outs=[((S, A_W), BF16, "row")])[0]


HEADS_PER_GROUP = 2
GROUP_W = HEADS_PER_GROUP * B_DH
N_GROUPS = B_HEADS // HEADS_PER_GROUP
PAD_ROWS = B_PREV * CHUNK


Q_TILE = 256
Q_CHUNKS = Q_TILE // CHUNK
KEY_WIN = (B_PREV + Q_CHUNKS) * CHUNK


def _band_probs(qh, kh, bias, valid):
    s = _dot1(qh, kh, "nt") * (B_DH ** -0.5) + bias
    s = jnp.where(valid, s, NEG_INF)
    e = jnp.exp(s - jnp.max(s, axis=-1, keepdims=True))
    return e * (1.0 / jnp.sum(e, axis=-1, keepdims=True))


def _attn_specs(S, tile_rows):
    assert PAD_ROWS % tile_rows == 0 and S % tile_rows == 0, (PAD_ROWS, S, tile_rows)
    n_cb = B_W // GROUP_W
    return [pl.BlockSpec((tile_rows, GROUP_W), lambda g, n: (n + PAD_ROWS // tile_rows, g)),
            pl.BlockSpec((PAD_ROWS + S, GROUP_W), lambda g, n: (0, n_cb + g)),
            pl.BlockSpec((PAD_ROWS + S, GROUP_W), lambda g, n: (0, 2 * n_cb + g)),
            pl.BlockSpec((HEADS_PER_GROUP, CHUNK, B_BAND), lambda g, n: (g, 0, 0))]


def _band_valid(first_chunk):
    return lax.broadcasted_iota(jnp.int32, (CHUNK, B_BAND), 1) >= PAD_ROWS - first_chunk * CHUNK


def _chunk_rows(x, qc, rows=CHUNK):
    return x[qc * CHUNK:qc * CHUNK + rows]


FWD_TILE = 512
FWD_CHUNKS = FWD_TILE // CHUNK
FWD_WIN = (B_PREV + FWD_CHUNKS) * CHUNK


def attn_fwd(qkv_pad, bias):
    S = qkv_pad.shape[0] - PAD_ROWS

    def body(q_ref, k_ref, v_ref, b_ref, o_ref):
        n = pl.program_id(1)
        start = pl.multiple_of(n * FWD_TILE, FWD_TILE)
        kwin = k_ref[pl.ds(start, FWD_WIN), :]
        vwin = v_ref[pl.ds(start, FWD_WIN), :]
        qv = q_ref[...]
        pairs = [(qc, hh) for qc in range(FWD_CHUNKS) for hh in range(HEADS_PER_GROUP)]
        sl = lambda hh: slice(hh * B_DH, (hh + 1) * B_DH)
        s = [_dot1(_chunk_rows(qv, qc)[:, sl(hh)], _chunk_rows(kwin, qc, B_BAND)[:, sl(hh)], "nt") for qc, hh in pairs]
        s = [jnp.where(_band_valid(n * FWD_CHUNKS + qc), x * (B_DH ** -0.5) + b_ref[hh], NEG_INF)
             for x, (qc, hh) in zip(s, pairs)]
        e = [jnp.exp(x - jnp.max(x, axis=-1, keepdims=True)) for x in s]
        p = [x * (1.0 / jnp.sum(x, axis=-1, keepdims=True)) for x in e]
        o = [_dot1(x, _chunk_rows(vwin, qc, B_BAND)[:, sl(hh)], "nn") for x, (qc, hh) in zip(p, pairs)]
        rows = [jnp.concatenate(o[qc * HEADS_PER_GROUP:(qc + 1) * HEADS_PER_GROUP], axis=1) for qc in range(FWD_CHUNKS)]
        o_ref[...] = jnp.concatenate(rows, axis=0).astype(BF16)

    return pl.pallas_call(
        body, name="attn_fwd", grid=(N_GROUPS, S // FWD_TILE), in_specs=_attn_specs(S, FWD_TILE),
        out_specs=pl.BlockSpec((FWD_TILE, GROUP_W), lambda g, n: (n, g)),
        out_shape=jax.ShapeDtypeStruct((S, B_W), BF16),
        compiler_params=_cparams(("parallel", "arbitrary")),
    )(qkv_pad, qkv_pad, qkv_pad, bias)


EXT = B_BAND + CHUNK


def bias_expand(rel_bias):
    def body(rev_ref, o_ref):
        rev = rev_ref[...]
        erev = jnp.concatenate([jnp.broadcast_to(rev[:, 0:1], (B_HEADS, EXT - B_REL)), rev], axis=1)
        for i in range(CHUNK):
            o_ref[i] = erev[:, CHUNK - i:CHUNK - i + B_BAND]

    return pl.pallas_call(
        body, name="bias_expand", in_specs=[WHOLE_VMEM], out_specs=WHOLE_VMEM,
        out_shape=jax.ShapeDtypeStruct((CHUNK, B_HEADS, B_BAND), F32),
    )(jnp.flip(rel_bias, axis=1))


def bias_reduce(dbias):
    def body(d_ref, o_ref):
        acc = jnp.zeros((B_HEADS, EXT), F32)
        for i in range(CHUNK):
            acc = acc + jnp.pad(d_ref[i], ((0, 0), (CHUNK - i, i)))
        tail = acc[:, EXT - B_REL:]
        clipped = jnp.sum(acc[:, :EXT - B_REL], axis=1, keepdims=True)
        lane = lax.broadcasted_iota(jnp.int32, (B_HEADS, B_REL), 1)
        o_ref[...] = jnp.where(lane == 0, tail + clipped, tail)

    rev = pl.pallas_call(body, name="bias_reduce", in_specs=[WHOLE_VMEM], out_specs=WHOLE_VMEM,
                         out_shape=jax.ShapeDtypeStruct((B_HEADS, B_REL), F32))(dbias)
    return jnp.flip(rev, axis=1)


def merge_fwd(gates_raw, b_gate, ya, yb):
    S = ya.shape[0]

    def body(g_ref, b_ref, ya_ref, yb_ref, o_ref):
        gt = _sigmoid(g_ref[...].astype(F32) + b_ref[...])
        o_ref[...] = (gt[:, :D_MODEL] * ya_ref[...].astype(F32) + gt[:, D_MODEL:] * yb_ref[...].astype(F32)).astype(BF16)

    return rowcall(body, name="merge_fwd", S=S, ts=512,
                   ins=[(gates_raw, "row"), (b_gate, "vec"), (ya, "row"), (yb, "row")],
                   outs=[((S, D_MODEL), BF16, "row")])[0]


def _ln_stats(xpre):
    mu = jnp.mean(xpre, axis=-1, keepdims=True)
    xc = xpre - mu
    rstd = lax.rsqrt(jnp.mean(xc * xc, axis=-1, keepdims=True) + LN_EPS)
    return xc * rstd, rstd


def ln1_fwd(x, mix, mod, ln_g, ln_b):
    S = x.shape[0]

    def body(x_ref, mix_ref, m_ref, g_ref, b_ref, xpre_ref, x1_ref, h2_ref):
        m = m_ref[...]
        xpre = ALPHA * x_ref[...] + m[GATE_T:GATE_T + 1] * mix_ref[...]
        xhat, _ = _ln_stats(xpre)
        x1 = xhat * g_ref[...] + b_ref[...]
        xpre_ref[...] = xpre
        x1_ref[...] = x1
        h2_ref[...] = (x1 * (1.0 + m[SCALE_F:SCALE_F + 1]) + m[SHIFT_F:SHIFT_F + 1]).astype(BF16)

    return rowcall(body, name="ln1_fwd", S=S, ts=512,
                   ins=[(x, "row"), (mix, "row"), (mod, "vec"), (ln_g, "vec"), (ln_b, "vec")],
                   outs=[((S, D_MODEL), F32, "row"), ((S, D_MODEL), F32, "row"), ((S, D_MODEL), BF16, "row")])


STRIP_FWD = (32, 256)
STRIP_BWD = (64, 128)


def ffn_act_fwd(up, conv_w, conv_b):
    S = up.shape[0]
    ts = 256
    STRIP_ROWS, STRIP_COLS = STRIP_FWD

    def body(u_ref, up_ref, w_ref, b_ref, o_ref, ubuf):
        ubuf[0:8] = _halo_prev(up_ref) * (pl.program_id(0) > 0).astype(F32)
        ubuf[8:8 + ts] = u_ref[...].astype(F32)

        def col_block(j, carry):
            gate = pl.ds(pl.multiple_of(j * STRIP_COLS, STRIP_COLS), STRIP_COLS)
            halves = [gate, pl.ds(pl.multiple_of(D_FF + j * STRIP_COLS, STRIP_COLS), STRIP_COLS)]
            w = [w_ref[:, c] for c in halves]
            bias = [b_ref[:, c] for c in halves]
            for r0 in range(0, ts, STRIP_ROWS):
                uc = []
                for h in range(2):
                    x = ubuf[r0:r0 + STRIP_ROWS + 8, halves[h]]
                    uc.append(bias[h] + sum(
                        w[h][t:t + 1] * (x if t == FFN_CONV - 1 else pltpu.roll(x, FFN_CONV - 1 - t, axis=0))[8:]
                        for t in range(FFN_CONV)))
                o_ref[r0:r0 + STRIP_ROWS, gate] = (_silu(uc[0]) * uc[1]).astype(BF16)
            return carry

        lax.fori_loop(0, D_FF // STRIP_COLS, col_block, 0)

    return rowcall(body, name="ffn_act_fwd", S=S, ts=ts,
                   ins=[(up, "row"), (up, "prev"), (conv_w, "vec"), (conv_b, "vec")],
                   outs=[((S, D_FF), BF16, "row")], scratch=[pltpu.VMEM((ts + 8, 2 * D_FF), F32)])[0]


def final_fwd_bwd(x1, ffn, target, mod, ln_g, ln_b):
    S = x1.shape[0]

    def body(x1_ref, f_ref, t_ref, m_ref, g_ref, b_ref, dxpre_ref, dffn_ref, loss_ref, dgate_ref, dg_ref, db_ref):
        gate = m_ref[...][GATE_F:GATE_F + 1]
        ffn_v = f_ref[...]
        xpre = ALPHA * x1_ref[...] + gate * ffn_v
        xhat, rstd = _ln_stats(xpre)
        err = xhat * g_ref[...] + b_ref[...] - t_ref[...]
        loss_ref[...] += 0.5 * jnp.sum(jnp.mean(err * err, axis=-1, keepdims=True), axis=0, keepdims=True)
        dy = err * (1.0 / D_MODEL)
        dg_ref[...] += jnp.sum(dy * xhat, axis=0, keepdims=True)
        db_ref[...] += jnp.sum(dy, axis=0, keepdims=True)
        dyg = dy * g_ref[...]
        dxpre = rstd * (dyg - jnp.mean(dyg, axis=-1, keepdims=True) - xhat * jnp.mean(dyg * xhat, axis=-1, keepdims=True))
        dxpre_ref[...] = dxpre
        dffn_ref[...] = (gate * dxpre).astype(BF16)
        dgate_ref[...] += jnp.sum(dxpre * ffn_v, axis=0, keepdims=True)

    vec = ((1, D_MODEL), F32, "acc")
    return rowcall(body, name="final_fwd_bwd", S=S, ts=512,
                   ins=[(x1, "row"), (ffn, "row"), (target, "row"), (mod, "vec"), (ln_g, "vec"), (ln_b, "vec")],
                   outs=[((S, D_MODEL), F32, "row"), ((S, D_MODEL), BF16, "row"), ((1, 1), F32, "acc"), vec, vec, vec])


def ffn_act_bwd(dact, up, conv_w, conv_b):
    S = up.shape[0]
    ts = 256
    STRIP_ROWS, STRIP_COLS = STRIP_BWD
    win_u, win_d = STRIP_ROWS + 16, STRIP_ROWS + 8

    def body(d_ref, dn_ref, u_ref, up_ref, un_ref, w_ref, b_ref, dup_ref, dw_ref, db_ref, ubuf, dbuf):
        i = pl.program_id(0)
        ubuf[0:8] = _halo_prev(up_ref) * (i > 0).astype(F32)
        ubuf[8:8 + ts] = u_ref[...].astype(F32)
        ubuf[8 + ts:16 + ts] = _halo_next(un_ref)
        dbuf[0:ts] = d_ref[...].astype(F32)
        dbuf[ts:ts + 8] = _halo_next(dn_ref) * (i < pl.num_programs(0) - 1).astype(F32)

        def col_block(j, carry):
            halves = [pl.ds(pl.multiple_of(j * STRIP_COLS, STRIP_COLS), STRIP_COLS),
                      pl.ds(pl.multiple_of(D_FF + j * STRIP_COLS, STRIP_COLS), STRIP_COLS)]
            w = [w_ref[:, c] for c in halves]
            bias = [b_ref[:, c] for c in halves]
            dw_acc = [[jnp.zeros((1, STRIP_COLS), F32) for _ in range(FFN_CONV)] for _ in halves]
            db_acc = [jnp.zeros((1, STRIP_COLS), F32) for _ in halves]
            for r0 in range(0, ts, STRIP_ROWS):
                shifted = [[x if k == 0 else pltpu.roll(x, k, axis=0) for k in range(FFN_CONV)]
                           for x in (ubuf[r0:r0 + win_u, c] for c in halves)]
                uc = [bias[h] + sum(w[h][t:t + 1] * shifted[h][FFN_CONV - 1 - t][8:8 + win_d] for t in range(FFN_CONV))
                      for h in range(2)]
                dact_w = dbuf[r0:r0 + win_d, halves[0]]
                sg, dsg = _silu_and_grad(uc[0])
                duc = [dact_w * uc[1] * dsg, dact_w * sg]
                for h in range(2):
                    dup = duc[h] * w[h][FFN_CONV - 1:FFN_CONV]
                    for t in range(FFN_CONV - 1):
                        dup = dup + pltpu.roll(duc[h], win_d - (FFN_CONV - 1 - t), axis=0) * w[h][t:t + 1]
                    dup_ref[r0:r0 + STRIP_ROWS, halves[h]] = dup[:STRIP_ROWS].astype(BF16)
                    mine = duc[h][:STRIP_ROWS]
                    db_acc[h] = db_acc[h] + jnp.sum(mine, axis=0, keepdims=True)
                    for t in range(FFN_CONV):
                        dw_acc[h][t] = dw_acc[h][t] + jnp.sum(
                            mine * shifted[h][FFN_CONV - 1 - t][8:8 + STRIP_ROWS], axis=0, keepdims=True)
            for h in range(2):
                dw_ref[:, halves[h]] += jnp.concatenate(dw_acc[h], axis=0)
                db_ref[:, halves[h]] += db_acc[h]
            return carry

        lax.fori_loop(0, D_FF // STRIP_COLS, col_block, 0)

    return rowcall(body, name="ffn_act_bwd", S=S, ts=ts,
                   ins=[(dact, "row"), (dact, "next"), (up, "row"), (up, "prev"), (up, "next"), (conv_w, "vec"), (conv_b, "vec")],
                   outs=[((S, 2 * D_FF), BF16, "row"), ((FFN_CONV, 2 * D_FF), F32, "acc"), ((1, 2 * D_FF), F32, "acc")],
                   scratch=[pltpu.VMEM((ts + 16, 2 * D_FF), F32), pltpu.VMEM((ts + 8, D_FF), F32)])


def ln1_bwd(dxpre2, dh2, xpre1, mix, mod, ln_g, ln_b):
    S = xpre1.shape[0]

    def body(d2_ref, dh_ref, xp_ref, mix_ref, m_ref, g_ref, b_ref, dxpre_ref, dmix_ref,
             dscale_ref, dshift_ref, dgate_ref, dg_ref, db_ref):
        m = m_ref[...]
        xhat, rstd = _ln_stats(xp_ref[...])
        x1 = xhat * g_ref[...] + b_ref[...]
        dh = dh_ref[...]
        dx1 = ALPHA * d2_ref[...] + dh * (1.0 + m[SCALE_F:SCALE_F + 1])
        dscale_ref[...] += jnp.sum(dh * x1, axis=0, keepdims=True)
        dshift_ref[...] += jnp.sum(dh, axis=0, keepdims=True)
        dg_ref[...] += jnp.sum(dx1 * xhat, axis=0, keepdims=True)
        db_ref[...] += jnp.sum(dx1, axis=0, keepdims=True)
        dyg = dx1 * g_ref[...]
        dxpre = rstd * (dyg - jnp.mean(dyg, axis=-1, keepdims=True) - xhat * jnp.mean(dyg * xhat, axis=-1, keepdims=True))
        dxpre_ref[...] = dxpre
        dmix_ref[...] = (m[GATE_T:GATE_T + 1] * dxpre).astype(BF16)
        dgate_ref[...] += jnp.sum(dxpre * mix_ref[...], axis=0, keepdims=True)

    vec = ((1, D_MODEL), F32, "acc")
    return rowcall(body, name="ln1_bwd", S=S, ts=512,
                   ins=[(dxpre2, "row"), (dh2, "row"), (xpre1, "row"), (mix, "row"), (mod, "vec"), (ln_g, "vec"), (ln_b, "vec")],
                   outs=[((S, D_MODEL), F32, "row"), ((S, D_MODEL), BF16, "row"), vec, vec, vec, vec, vec])


def merge_bwd(dmerged, gates_raw, b_gate, ya, yb):
    S = ya.shape[0]

    def body(d_ref, g_ref, b_ref, ya_ref, yb_ref, dya_ref, dyb_ref, dg_ref, dbg_ref):
        gt = _sigmoid(g_ref[...].astype(F32) + b_ref[...])
        d = d_ref[...].astype(F32)
        ga, gb = gt[:, :D_MODEL], gt[:, D_MODEL:]
        dya_ref[...] = (d * ga).astype(BF16)
        dyb_ref[...] = (d * gb).astype(BF16)
        dgr = jnp.concatenate([d * ya_ref[...].astype(F32) * ga * (1.0 - ga),
                               d * yb_ref[...].astype(F32) * gb * (1.0 - gb)], axis=1)
        dg_ref[...] = dgr.astype(BF16)
        dbg_ref[...] += jnp.sum(dgr, axis=0, keepdims=True)

    return rowcall(body, name="merge_bwd", S=S, ts=512,
                   ins=[(dmerged, "row"), (gates_raw, "row"), (b_gate, "vec"), (ya, "row"), (yb, "row")],
                   outs=[((S, D_MODEL), BF16, "row"), ((S, D_MODEL), BF16, "row"), ((S, 2 * D_MODEL), BF16, "row"),
                         ((1, 2 * D_MODEL), F32, "acc")])


def attn_bwd(qkv_pad, bias, do_b):
    S = qkv_pad.shape[0] - PAD_ROWS

    def body(q_ref, k_ref, v_ref, bias_ref, do_ref, dq_ref, dk_ref, dv_ref, db_ref, b_ref):
        n = pl.program_id(1)

        @pl.when(n == 0)
        def _():
            dk_ref[...] = jnp.zeros_like(dk_ref)
            dv_ref[...] = jnp.zeros_like(dv_ref)
            db_ref[...] = jnp.zeros_like(db_ref)
            b_ref[...] = jnp.full(b_ref.shape, NEG_INF, F32)
            for hh in range(HEADS_PER_GROUP):
                for qc in range(Q_CHUNKS):
                    b_ref[hh, qc * CHUNK:(qc + 1) * CHUNK, qc * CHUNK:qc * CHUNK + B_BAND] = bias_ref[hh]

        start = pl.multiple_of(n * Q_TILE, Q_TILE)
        kwin = k_ref[pl.ds(start, KEY_WIN), :]
        vwin = v_ref[pl.ds(start, KEY_WIN), :]
        qv, dov = q_ref[...], do_ref[...]
        valid = lax.broadcasted_iota(jnp.int32, (Q_TILE, KEY_WIN), 1) >= PAD_ROWS - n * Q_TILE
        dqs, dks, dvs = [], [], []
        for hh in range(HEADS_PER_GROUP):
            sl = slice(hh * B_DH, (hh + 1) * B_DH)
            p = _band_probs(qv[:, sl], kwin[:, sl], b_ref[hh], valid)
            dp = _dot1(dov[:, sl], vwin[:, sl], "nt")
            ds = p * (dp - jnp.sum(dp * p, axis=-1, keepdims=True))
            dbh = ds[0:CHUNK, 0:B_BAND]
            for qc in range(1, Q_CHUNKS):
                dbh = dbh + ds[qc * CHUNK:(qc + 1) * CHUNK, qc * CHUNK:qc * CHUNK + B_BAND]
            db_ref[hh] += dbh
            dsq = ds * (B_DH ** -0.5)
            dqs.append(_dot1(dsq, kwin[:, sl], "nn"))
            dks.append(_dot1(dsq, qv[:, sl], "tn"))
            dvs.append(_dot1(p, dov[:, sl], "tn"))
        dq_ref[...] = jnp.concatenate(dqs, axis=1).astype(BF16)
        dk_ref[pl.ds(start, KEY_WIN), :] += jnp.concatenate(dks, axis=1)
        dv_ref[pl.ds(start, KEY_WIN), :] += jnp.concatenate(dvs, axis=1)

    col = pl.BlockSpec((PAD_ROWS + S, GROUP_W), lambda g, n: (0, g))
    tile = pl.BlockSpec((Q_TILE, GROUP_W), lambda g, n: (n, g))
    return pl.pallas_call(
        body, name="attn_bwd", grid=(N_GROUPS, S // Q_TILE), in_specs=_attn_specs(S, Q_TILE) + [tile],
        out_specs=[tile, col, col, pl.BlockSpec((HEADS_PER_GROUP, CHUNK, B_BAND), lambda g, n: (g, 0, 0))],
        out_shape=[jax.ShapeDtypeStruct((S, B_W), BF16), jax.ShapeDtypeStruct((PAD_ROWS + S, B_W), F32),
                   jax.ShapeDtypeStruct((PAD_ROWS + S, B_W), F32), jax.ShapeDtypeStruct((B_HEADS, CHUNK, B_BAND), F32)],
        scratch_shapes=[pltpu.VMEM((HEADS_PER_GROUP, Q_TILE, KEY_WIN), F32)],
        compiler_params=_cparams(("parallel", "arbitrary")),
    )(qkv_pad, qkv_pad, qkv_pad, bias, do_b)


def gate_a_bwd(do_a, o_pre, z, norm_w):
    S = o_pre.shape[0]

    def body(d_ref, o_ref, z_ref, nw_ref, dop_ref, dz_ref, dnw_ref):
        nw = nw_ref[...]
        acc = jnp.zeros((1, A_DK), F32)
        for h in range(A_HEADS):
            sl = slice(h * A_DK, (h + 1) * A_DK)
            oh, zh, dh = o_ref[:, sl], z_ref[:, sl].astype(F32), d_ref[:, sl].astype(F32)
            r = lax.rsqrt(jnp.mean(oh * oh, axis=-1, keepdims=True) + RMS_EPS)
            sz, dsz = _silu_and_grad(zh)
            dz_ref[:, sl] = (dh * oh * r * nw * dsz).astype(BF16)
            acc = acc + jnp.sum(dh * oh * r * sz, axis=0, keepdims=True)
            t = dh * nw * sz
            dop_ref[:, sl] = r * t - oh * (r * r * r) * jnp.mean(t * oh, axis=-1, keepdims=True)
        dnw_ref[...] += acc

    return rowcall(body, name="gate_a_bwd", S=S, ts=512,
                   ins=[(do_a, "row"), (o_pre, "row"), (z, "row"), (norm_w, "vec")],
                   outs=[((S, A_W), F32, "row"), ((S, A_W), BF16, "row"), ((1, A_DK), F32, "acc")])


def delta_bwd(q, k, v, beta, g, sprev, tinv, do):
    S = q.shape[0]
    n_chunks = S // CHUNK

    def body(q_ref, k_ref, v_ref, beta_ref, g_ref, sprev_ref, t_ref, do_ref,
             dq_ref, dk_ref, dv_ref, dbeta_ref, dg_ref, dstate_ref):
        @pl.when(pl.program_id(0) == 0)
        def _():
            dstate_ref[...] = jnp.zeros_like(dstate_ref)

        mk = _tri_masks()
        causal, strict, eye = mk["causal"], mk["strict"], mk["eye"]
        blk_end = (lax.broadcasted_iota(jnp.int32, (GROUP_ROWS, 1), 0) & (CHUNK - 1)) == CHUNK - 1
        lane = lax.broadcasted_iota(jnp.int32, (CHUNK, A_HEADS), 1)
        betav, gv = beta_ref[...], g_ref[...]
        dbeta_t = jnp.zeros((CHUNK, A_HEADS), F32)
        dg_t = jnp.zeros((CHUNK, A_HEADS), F32)
        groups, heads = range(N_HEAD_GROUPS), range(HEAD_GROUP)
        st = [dict() for _ in groups]

        def local_part(grp, s):
            s["qs"], s["ks"], s["vs"] = _stack_heads(q_ref, grp), _stack_heads(k_ref, grp), _stack_heads(v_ref, grp)
            s["dos"] = _stack_heads(do_ref, grp)
            s["bs"] = _stack_cols(betav, grp)
            s["loc"] = loc = _delta_local(s["qs"], s["ks"], s["vs"], s["bs"], _stack_cols(gv, grp), mk)
            s["tinv"] = t_ref[0, grp]
            s["rhs"] = jnp.concatenate([loc["vb"], loc["y"]], axis=1)
            s["uw"] = _dot3(s["tinv"], s["rhs"], "nn")

        def state_part(grp, s):
            loc, uw, dos, qs = s["loc"], s["uw"], s["dos"], s["qs"]
            gam, kd, gl, gc = loc["gam"], loc["kd"], loc["gl"], loc["gc"]
            qg = qs * gam
            egl = jnp.exp(gl)
            hid = [grp * HEAD_GROUP + j for j in heads]
            s0 = [sprev_ref[0, h] for h in hid]
            ds1 = [dstate_ref[h] for h in hid]
            w = [_head_rows(uw, j)[:, A_DK:] for j in heads]
            vn = [_head_rows(uw, j)[:, :A_DK] - _dot1(w[j], s0[j], "nn") for j in heads]
            vns = jnp.concatenate(vn, axis=0)
            dvn_local = _dot1(loc["p"], dos, "tn")
            dvn = [_head_rows(dvn_local, j) + _dot1(_head_rows(kd, j), ds1[j], "nn") for j in heads]
            dvns = jnp.concatenate(dvn, axis=0)
            s["dp"] = jnp.where(causal, _dot1(dos, vns, "nt"), 0.0)
            dqg = jnp.concatenate([_dot1(_head_rows(dos, j), s0[j], "nt") for j in heads], axis=0)
            s["dq"] = dqg * gam
            dgc = jnp.sum(dqg * qg, axis=-1, keepdims=True)
            for j in heads:
                dstate_ref[hid[j]] = (_dot1(_head_rows(qg, j), _head_rows(dos, j), "tn")
                                      + egl[(j + 1) * CHUNK - 1:(j + 1) * CHUNK] * ds1[j] - _dot1(w[j], dvn[j], "tn"))
            dkd = jnp.concatenate([_dot1(vn[j], ds1[j], "nt") for j in heads], axis=0)
            s["dk"] = dkd * jnp.exp(gl - gc)
            t1 = jnp.sum(dkd * kd, axis=-1, keepdims=True)
            dgl = jnp.concatenate(
                [jnp.broadcast_to(jnp.sum(_head_rows(t1, j), axis=0, keepdims=True)
                                  + jnp.sum(jnp.sum(ds1[j] * s0[j], axis=-1, keepdims=True), axis=0, keepdims=True)
                                  * egl[(j + 1) * CHUNK - 1:(j + 1) * CHUNK], (CHUNK, 1)) for j in heads], axis=0)
            s["dgc"] = dgc - t1 + jnp.where(blk_end, dgl, 0.0)
            s["duw"] = jnp.concatenate(
                [dvns, jnp.concatenate([-_dot1(dvn[j], s0[j], "nt") for j in heads], axis=0)], axis=1)

        def solve_part(grp, s):
            s["dvby"] = _dot3(s["tinv"], s["duw"], "tn")
            s["dt"] = _dot3(s["duw"], s["rhs"], "nt")

        def inverse_part_a(grp, s):
            s["tdt"] = _dot3(s["tinv"], s["dt"], "tn")

        def inverse_part_b(grp, s):
            s["da"] = jnp.where(strict, -_dot3(s["tdt"], s["tinv"], "nt"), 0.0)

        def finish(grp, s):
            loc, qs, ks, vs, bs, da, dp, dvby = s["loc"], s["qs"], s["ks"], s["vs"], s["bs"], s["da"], s["dp"], s["dvby"]
            gam, decay = loc["gam"], loc["decay"]
            dm = da * decay
            dn = dp * decay
            e = da * loc["a"] + dp * loc["p"]
            dgc = s["dgc"] + jnp.sum(e, axis=1, keepdims=True) - _row_to_col(jnp.sum(e, axis=0, keepdims=True), eye)
            dy = dvby[:, A_DK:]
            dvb = dvby[:, :A_DK]
            dkb = _dot1(dm, ks, "nn") + dy * gam
            dk = s["dk"] + _dot1(dm, loc["kb"], "tn") + _dot1(dn, qs, "tn") + dkb * bs
            dq = s["dq"] + _dot1(dn, ks, "nn")
            dgc = dgc + jnp.sum(dy * loc["y"], axis=-1, keepdims=True)
            dbeta = jnp.sum(dkb * ks, axis=-1, keepdims=True) + jnp.sum(dvb * vs, axis=-1, keepdims=True)
            dv = dvb * bs
            dgs = jnp.sum(jnp.where(mk["upper"], _col_to_row(dgc, eye), 0.0), axis=1, keepdims=True)
            for j in heads:
                h = grp * HEAD_GROUP + j
                sl = slice(h * A_DK, (h + 1) * A_DK)
                dq_ref[:, sl] = _head_rows(dq, j)
                dk_ref[:, sl] = _head_rows(dk, j)
                dv_ref[:, sl] = _head_rows(dv, j)
            s["dbeta"], s["dgs"] = dbeta, dgs

        for stage in (local_part, state_part, solve_part, inverse_part_a, inverse_part_b, finish):
            for grp in groups:
                stage(grp, st[grp])
        for grp in groups:
            for j in heads:
                h = grp * HEAD_GROUP + j
                dbeta_t = dbeta_t + jnp.where(lane == h, _head_rows(st[grp]["dbeta"], j), 0.0)
                dg_t = dg_t + jnp.where(lane == h, _head_rows(st[grp]["dgs"], j), 0.0)
        dbeta_ref[...] = dbeta_t
        dg_ref[...] = dg_t

    rev = lambda n: (n_chunks - 1 - n, 0)
    rev4 = lambda n: (n_chunks - 1 - n, 0, 0, 0)
    tile = pl.BlockSpec((CHUNK, A_W), rev)
    small = pl.BlockSpec((CHUNK, A_HEADS), rev)
    return pl.pallas_call(
        body, name="delta_bwd", grid=(n_chunks,),
        in_specs=[tile, tile, tile, small, small, pl.BlockSpec((1, A_HEADS, A_DK, A_DK), rev4),
                  pl.BlockSpec((1, N_HEAD_GROUPS, GROUP_ROWS, GROUP_ROWS), rev4), tile],
        out_specs=[tile, tile, tile, small, small],
        out_shape=[jax.ShapeDtypeStruct((S, A_W), F32)] * 3 + [jax.ShapeDtypeStruct((S, A_HEADS), F32)] * 2,
        scratch_shapes=[pltpu.VMEM((A_HEADS, A_DK, A_DK), F32)],
        compiler_params=_cparams(("arbitrary",)),
    )(q, k, v, beta, g, sprev, tinv, do)


def _prep_a_dpre(raw, raw_prev, w, dq, dk, dv):
    y, dy_dpre = _prep_a_core(raw, raw_prev, w)
    parts = []
    for h in range(A_HEADS):
        yq = y[:, h * A_DK:(h + 1) * A_DK]
        dqh = dq[:, h * A_DK:(h + 1) * A_DK]
        rq = lax.rsqrt(jnp.sum(yq * yq, axis=-1, keepdims=True) + L2_EPS)
        parts.append((A_DK ** -0.5) * (rq * dqh - yq * (rq * rq * rq) * jnp.sum(dqh * yq, axis=-1, keepdims=True)))
    for h in range(A_HEADS):
        yk = y[:, A_W + h * A_DK:A_W + (h + 1) * A_DK]
        dkh = dk[:, h * A_DK:(h + 1) * A_DK]
        rk = lax.rsqrt(jnp.sum(yk * yk, axis=-1, keepdims=True) + L2_EPS)
        parts.append(rk * dkh - yk * (rk * rk * rk) * jnp.sum(dkh * yk, axis=-1, keepdims=True))
    parts.append(dv)
    return jnp.concatenate(parts, axis=1) * dy_dpre


def prep_a_bwd(qkv_raw, ba, conv_a, a_log, dt_bias, dq, dk, dv, dbeta, dg):
    S = qkv_raw.shape[0]
    ts = 256

    def body(x_ref, xp_ref, xn_ref, ba_ref, w_ref, al_ref, dt_ref, dq_ref, dqn_ref, dk_ref, dkn_ref, dv_ref, dvn_ref,
             dbeta_ref, dg_ref, draw_ref, dba_ref, dw_ref, dal_ref, ddt_ref):
        i = pl.program_id(0)
        first = (i > 0).astype(F32)
        last = (i < pl.num_programs(0) - 1).astype(F32)
        w = w_ref[...]
        cur, prev = x_ref[...].astype(F32), _halo_prev(xp_ref) * first
        dpre = _prep_a_dpre(cur, prev, w, dq_ref[...], dk_ref[...], dv_ref[...])
        dpre_n = _prep_a_dpre(_halo_next(xn_ref), cur[ts - 8:ts], w, _halo_next(dqn_ref), _halo_next(dkn_ref),
                              _halo_next(dvn_ref)) * last
        for j in range(A_CONV):
            dw_ref[j:j + 1, :] += jnp.sum(dpre * _shift_down(cur, prev, A_CONV - 1 - j), axis=0, keepdims=True)
        draw = dpre * w[A_CONV - 1:A_CONV]
        for j in range(A_CONV - 1):
            draw = draw + _shift_up(dpre, dpre_n, A_CONV - 1 - j) * w[j:j + 1]
        draw_ref[...] = draw.astype(BF16)
        bav = ba_ref[...]
        beta = _sigmoid(bav[:, 0:A_HEADS])
        xa = bav[:, A_HEADS:2 * A_HEADS] + dt_ref[...]
        nexp = -jnp.exp(al_ref[...])
        dgv = dg_ref[...]
        da = dgv * nexp * _sigmoid(xa)
        dba_ref[:, 0:A_HEADS] = dbeta_ref[...] * beta * (1.0 - beta)
        dba_ref[:, A_HEADS:2 * A_HEADS] = da
        dal_ref[...] += jnp.sum(dgv * nexp * _softplus(xa), axis=0, keepdims=True)
        ddt_ref[...] += jnp.sum(da, axis=0, keepdims=True)

    return rowcall(
        body, name="prep_a_bwd", S=S, ts=ts,
        ins=[(qkv_raw, "row"), (qkv_raw, "prev"), (qkv_raw, "next"), (ba, "row"), (conv_a, "vec"), (a_log, "vec"),
             (dt_bias, "vec"), (dq, "row"), (dq, "next"), (dk, "row"), (dk, "next"), (dv, "row"), (dv, "next"),
             (dbeta, "row"), (dg, "row")],
        outs=[((S, 3 * A_W), BF16, "row"), ((S, 2 * A_HEADS), F32, "row"), ((A_CONV, 3 * A_W), F32, "acc"),
              ((1, A_HEADS), F32, "acc"), ((1, A_HEADS), F32, "acc")])


def grad_x_final(dh1, x, dxpre1, mod):
    S = x.shape[0]

    def body(dh_ref, x_ref, dx_ref, m_ref, gx_ref, dscale_ref, dshift_ref):
        dh = dh_ref[...]
        gx_ref[...] = ALPHA * dx_ref[...] + dh * (1.0 + m_ref[...][SCALE_T:SCALE_T + 1])
        dscale_ref[...] += jnp.sum(dh * x_ref[...], axis=0, keepdims=True)
        dshift_ref[...] += jnp.sum(dh, axis=0, keepdims=True)

    vec = ((1, D_MODEL), F32, "acc")
    return rowcall(body, name="grad_x_final", S=S, ts=512, ins=[(dh1, "row"), (x, "row"), (dxpre1, "row"), (mod, "vec")],
                   outs=[((S, D_MODEL), F32, "row"), vec, vec])


_C_QKV, _C_Z, _C_BA, _C_QKVB, _C_G = 0, 3 * A_W, 4 * A_W, 4 * A_W + 2 * A_HEADS, 4 * A_W + 2 * A_HEADS + 3 * B_W
BA_PAD = 128


def split_w_in(w_in):
    ba = jnp.pad(w_in[:, _C_BA:_C_QKVB], ((0, 0), (0, BA_PAD - 2 * A_HEADS)))
    return dict(qkv=w_in[:, _C_QKV:_C_Z], z=w_in[:, _C_Z:_C_BA], ba=ba, qkvb=w_in[:, _C_QKVB:_C_G], g=w_in[:, _C_G:])


def join_w_in(p):
    return jnp.concatenate([p["qkv"], p["z"], p["ba"][:, :2 * A_HEADS], p["qkvb"], p["g"]], axis=1)


def forward_local(x, target, mod, w, sm, late_weights=None):
    h1 = modulate(x, mod, SHIFT_T, SCALE_T, "mod_t")
    qkv_raw = mm(h1, w["qkv"], mode="nn", out_dtype=BF16, name="proj_qkv")
    z = mm(h1, w["z"], mode="nn", out_dtype=BF16, name="proj_z")
    ba = mm(h1, w["ba"], mode="nn", out_dtype=F32, name="proj_ba")
    qkvb = mm(h1, w["qkvb"], mode="nn", out_dtype=BF16, name="proj_qkvb")
    gates_raw = mm(h1, w["g"], mode="nn", out_dtype=BF16, name="proj_g")
    q, k, v, beta, g = prep_a_fwd(qkv_raw, ba, sm["conv_a"], sm["a_log"], sm["dt_bias"])
    o_pre, sprev, tinv = delta_fwd(q, k, v, beta, g)
    o_a = gate_a_fwd(o_pre, z, sm["norm_a"])
    qkv_pad = jnp.pad(qkvb, ((PAD_ROWS, 0), (0, 0)))
    bias = jnp.transpose(bias_expand(sm["rel_bias"]), (1, 0, 2))
    o_b = attn_fwd(qkv_pad, bias)
    if late_weights is not None:
        w = dict(w, **late_weights(o_b))
    ya = mm(o_a, w["branch_a"], mode="nn", out_dtype=BF16, name="branch_a")
    yb = mm(o_b, w["branch_b"], mode="nn", out_dtype=BF16, name="branch_b")
    merged = merge_fwd(gates_raw, sm["b_gate"], ya, yb)
    mix = mm(merged, w["o"], mode="nn", out_dtype=F32, name="mix")
    xpre1, x1, h2 = ln1_fwd(x, mix, mod, sm["ln1_g"], sm["ln1_b"])
    up = mm(h2, w["up"], mode="nn", out_dtype=BF16, name="ffn_up", b_shards=True)
    act = ffn_act_fwd(up, sm["conv_ffn"], sm["b_conv_ffn"])
    ffn = mm(act, w["down"], mode="nn", out_dtype=F32, name="ffn_down")
    dxpre2, dffn, loss, dgate_f, dln2_g, dln2_b = final_fwd_bwd(x1, ffn, target, mod, sm["ln2_g"], sm["ln2_b"])
    saved = dict(h1=h1, qkv_raw=qkv_raw, z=z, ba=ba, gates_raw=gates_raw, q=q, k=k, v=v, beta=beta, g=g,
                 o_pre=o_pre, sprev=sprev, tinv=tinv, o_a=o_a, qkv_pad=qkv_pad, bias=bias, o_b=o_b, ya=ya, yb=yb,
                 merged=merged, mix=mix, xpre1=xpre1, x1=x1, h2=h2, up=up, act=act, ffn=ffn, w=w)
    return loss, dxpre2, dffn, dict(gate_f=dgate_f, ln2_g=dln2_g, ln2_b=dln2_b), saved


def backward_local(x, mod, sm, dxpre2, dffn, fin, sv, hooks=None):
    w = sv["w"]
    dact = mm(dffn, w["down"], mode="nt", out_dtype=BF16, name="d_act")
    gw_down = mm(sv["act"], dffn, mode="tn", out_dtype=BF16, name="gw_down")
    dup, dconv_ffn, db_conv_ffn = ffn_act_bwd(dact, sv["up"], sm["conv_ffn"], sm["b_conv_ffn"])
    dh2 = mm(dup, w["up"], mode="nt", out_dtype=F32, name="d_h2", b_shards=True)
    gw_up = mm(sv["h2"], dup, mode="tn", out_dtype=BF16, name="gw_up", out_shards=N_CHIPS)
    dxpre1, dmix, dsc_f, dsh_f, dgate_t, dln1_g, dln1_b = ln1_bwd(
        dxpre2, dh2, sv["xpre1"], sv["mix"], mod, sm["ln1_g"], sm["ln1_b"])
    dmerged = mm(dmix, w["o"], mode="nt", out_dtype=BF16, name="d_merged")
    gw_o = mm(sv["merged"], dmix, mode="tn", out_dtype=BF16, name="gw_o")
    dya, dyb, dgates, db_gate = merge_bwd(dmerged, sv["gates_raw"], sm["b_gate"], sv["ya"], sv["yb"])
    do_a = mm(dya, w["branch_a"], mode="nt", out_dtype=BF16, name="d_oa")
    gw_branch_a = mm(sv["o_a"], dya, mode="tn", out_dtype=BF16, name="gw_branch_a")
    do_b = mm(dyb, w["branch_b"], mode="nt", out_dtype=BF16, name="d_ob")
    gw_branch_b = mm(sv["o_b"], dyb, mode="tn", out_dtype=BF16, name="gw_branch_b")
    bias = sv["bias"]
    if hooks is not None:
        bias = bias + hooks["late_start"](dict(w_branch_a=gw_branch_a, w_branch_b=gw_branch_b, w_o=gw_o, w_up=gw_up,
                                               w_down=gw_down))[0, 0]
    dq_b, dk_pad, dv_pad, dbias = attn_bwd(sv["qkv_pad"], bias, do_b)
    if hooks is not None:
        dbias = dbias + hooks["late_finish"](dq_b)[0, 0]
    dqkvb = jnp.concatenate([dq_b, dk_pad[PAD_ROWS:].astype(BF16), dv_pad[PAD_ROWS:].astype(BF16)], axis=1)
    drel_bias = bias_reduce(jnp.transpose(dbias, (1, 0, 2)))
    do_pre, dz, dnorm_a = gate_a_bwd(do_a, sv["o_pre"], sv["z"], sm["norm_a"])
    dq, dk, dv, dbeta, dg = delta_bwd(sv["q"], sv["k"], sv["v"], sv["beta"], sv["g"], sv["sprev"], sv["tinv"], do_pre)
    dqkv_raw, dba16, dconv_a, da_log, ddt_bias = prep_a_bwd(
        sv["qkv_raw"], sv["ba"], sm["conv_a"], sm["a_log"], sm["dt_bias"], dq, dk, dv, dbeta, dg)
    dba = jnp.pad(dba16, ((0, 0), (0, BA_PAD - 2 * A_HEADS))).astype(BF16)
    pieces = dict(qkv=dqkv_raw, z=dz, ba=dba, qkvb=dqkvb, g=dgates)
    gw_in = join_w_in({key: mm(sv["h1"], dpiece, mode="tn", out_dtype=BF16, name="gw_in_" + key)
                       for key, dpiece in pieces.items()})
    w_ba = w["ba"]
    w_z = w["z"]
    if hooks is not None:
        w_ba = w_ba + hooks["w_in_start"](gw_in)[0, 0].astype(BF16)
    dh1 = mm(pieces["ba"], w_ba, mode="nt", out_dtype=F32, name="d_h1_ba")
    dh1 = mm(pieces["qkv"], w["qkv"], mode="nt", out_dtype=F32, name="d_h1_qkv", acc_in=dh1)
    if hooks is not None:
        w_z = w_z + hooks["w_in_finish"](dh1)[0, 0].astype(BF16)
    dh1 = mm(pieces["z"], w_z, mode="nt", out_dtype=F32, name="d_h1_z", acc_in=dh1)
    for key in ("qkvb", "g"):
        dh1 = mm(pieces[key], w[key], mode="nt", out_dtype=F32, name="d_h1_" + key, acc_in=dh1)
    grad_x, dsc_t, dsh_t = grad_x_final(dh1, x, dxpre1, mod)
    dmod = jnp.concatenate([dsh_t, dsc_t, dgate_t, dsh_f, dsc_f, fin["gate_f"]], axis=0)
    gw = dict(w_in=gw_in, w_branch_a=gw_branch_a, w_branch_b=gw_branch_b, w_o=gw_o, w_up=gw_up, w_down=gw_down)
    gs = dict(b_gate=db_gate, conv_a=dconv_a, a_log=da_log, dt_bias=ddt_bias, norm_a=dnorm_a, rel_bias=drel_bias,
              ln1_g=dln1_g, ln1_b=dln1_b, conv_ffn=dconv_ffn, b_conv_ffn=db_conv_ffn, ln2_g=fin["ln2_g"], ln2_b=fin["ln2_b"])
    return grad_x, dmod, gw, gs


MESH = pl.DeviceIdType.MESH
ANY = pl.BlockSpec(memory_space=pl.ANY)
WHOLE_VMEM = pl.BlockSpec(memory_space=pltpu.VMEM)


def _place():
    return lax.axis_index("x"), lax.axis_index("y"), lax.axis_index("c")


def allgather8(blk, name):
    m_per, n = blk.shape

    def body(x_ref, out_ref, send_sems, recv_sems, local_sem):
        x, y, c = _place()
        me, sibling = (x, y, c), (x, y, 1 - c)
        chips = [(1 - x, y), (x, 1 - y), (1 - x, 1 - y)]

        def rows(px, py, pc):
            return out_ref.at[pl.ds((4 * px + 2 * py + pc) * m_per, m_per), :]

        def copy(k, block, to, src=None):
            return pltpu.make_async_remote_copy(
                src_ref=rows(*block) if src is None else src, dst_ref=rows(*block),
                send_sem=send_sems.at[k], recv_sem=recv_sems.at[k], device_id=to, device_id_type=MESH)

        mine = pltpu.make_async_copy(x_ref, rows(*me), local_sem)
        mine.start()
        first = [copy(0, me, sibling, src=x_ref)]
        first += [copy(1 + j, me, (*chip, c), src=x_ref) for j, chip in enumerate(chips)]
        for cp in first:
            cp.start()
        passed = [copy(4 + j, (*chip, c), sibling) for j, chip in enumerate(chips)]
        for j, chip in enumerate(chips):
            copy(1 + j, (*chip, c), me).wait_recv()
            passed[j].start()
        copy(0, sibling, me).wait_recv()
        for j, chip in enumerate(chips):
            copy(4 + j, (*chip, 1 - c), me).wait_recv()
        for cp in first + passed:
            cp.wait_send()
        mine.wait()

    return pl.pallas_call(
        body, name=name, out_shape=jax.ShapeDtypeStruct((N_DEV * m_per, n), blk.dtype),
        in_specs=[WHOLE_VMEM], out_specs=WHOLE_VMEM,
        scratch_shapes=[pltpu.SemaphoreType.DMA((7,)), pltpu.SemaphoreType.DMA((7,)), pltpu.SemaphoreType.DMA],
    )(blk)


def _chip_peers(x, y):
    return [(1 - x, y), (x, 1 - y), (1 - x, 1 - y)]


def chip_exchange(arrs, name, scatter):
    n = len(arrs)

    def body(*refs):
        ins, outs = refs[:n], refs[n:2 * n]
        send_sems, recv_sems, local_sems = refs[2 * n:]
        x, y, c = _place()
        me = 2 * x + y
        sibling = (x, y, 1 - c)
        peers = _chip_peers(x, y)

        def half(ref, which):
            r2 = ref.shape[0] // 2
            return ref.at[pl.ds(which * r2, r2), :]

        def outgoing(a, chip):
            return ins[a].at[chip] if scatter else ins[a]

        def copy(k, src, dst, to):
            return pltpu.make_async_remote_copy(src_ref=src, dst_ref=dst, send_sem=send_sems.at[k],
                                                recv_sem=recv_sems.at[k], device_id=to, device_id_type=MESH)

        started, local = [], []
        for a in range(n):
            lc = pltpu.make_async_copy(outgoing(a, me), outs[a].at[me], local_sems.at[a])
            lc.start()
            local.append(lc)
            for j, (px, py) in enumerate(peers):
                cp = copy(6 * a + j, half(outgoing(a, 2 * px + py), c), half(outs[a].at[me], c), (px, py, c))
                cp.start()
                started.append(cp)
        for a in range(n):
            for j, (px, py) in enumerate(peers):
                landed = half(outs[a].at[2 * px + py], c)
                copy(6 * a + j, landed, landed, (px, py, c)).wait_recv()
                relay = copy(6 * a + 3 + j, landed, landed, sibling)
                relay.start()
                started.append(relay)
        for a in range(n):
            for j, (px, py) in enumerate(peers):
                other = half(outs[a].at[2 * px + py], 1 - c)
                copy(6 * a + 3 + j, other, other, sibling).wait_recv()
        for cp in started:
            cp.wait_send()
        for lc in local:
            lc.wait()

    out_shape = [jax.ShapeDtypeStruct(a.shape if scatter else (N_CHIPS,) + a.shape, a.dtype) for a in arrs]
    return pl.pallas_call(
        body, name=name, out_shape=out_shape, in_specs=[ANY] * n, out_specs=[ANY] * n,
        scratch_shapes=[pltpu.SemaphoreType.DMA((6 * n,)), pltpu.SemaphoreType.DMA((6 * n,)), pltpu.SemaphoreType.DMA((n,))],
    )(*arrs)


HBM_SPEC = pl.BlockSpec(memory_space=pltpu.HBM)
SEM_SPEC = pl.BlockSpec(memory_space=pltpu.SEMAPHORE)
SIDE_EFFECT = pltpu.SideEffectType.DATAFLOW_SIDE_EFFECTING


def _in_hbm(a):
    return pltpu.with_memory_space_constraint(a, pltpu.HBM)


def exchange_start(arrs, name, scatter, after):
    n = len(arrs)
    lands = [lax.empty(a.shape if scatter else (N_CHIPS,) + a.shape, a.dtype) for a in arrs]

    def body(*refs):
        ins, zones = refs[:n], refs[n:2 * n]
        send_sems, recv_sems, token = refs[2 * n + 1], refs[2 * n + 2], refs[-1]
        x, y, c = _place()
        me = 2 * x + y
        for a in range(n):
            for j, (px, py) in enumerate(_chip_peers(x, y)):
                pltpu.make_async_remote_copy(
                    src_ref=ins[a].at[2 * px + py] if scatter else ins[a], dst_ref=zones[a].at[me],
                    send_sem=send_sems.at[3 * a + j], recv_sem=recv_sems.at[3 * a + j],
                    device_id=(px, py, c), device_id_type=MESH).start()
        token[...] = jnp.zeros_like(token)

    res = pl.pallas_call(
        body, name=name,
        out_shape=[pltpu.SemaphoreType.DMA((3 * n,)), pltpu.SemaphoreType.DMA((3 * n,))]
        + [pltpu.HBM(a.shape, a.dtype) for a in arrs] + [pltpu.HBM(z.shape, z.dtype) for z in lands]
        + [jax.ShapeDtypeStruct((8, 128), F32)],
        in_specs=[HBM_SPEC] * (2 * n) + [ANY], out_specs=[SEM_SPEC, SEM_SPEC] + [HBM_SPEC] * (2 * n) + [WHOLE_VMEM],
        input_output_aliases={i: 2 + i for i in range(2 * n)},
        compiler_params=pltpu.CompilerParams(has_side_effects=SIDE_EFFECT),
    )(*[_in_hbm(a) for a in arrs], *[_in_hbm(z) for z in lands], after)
    return dict(send=res[0], recv=res[1], src=res[2:2 + n], zones=res[2 + n:2 + 2 * n], token=res[-1], scatter=scatter)


def exchange_wait(handle, name, after):
    srcs, zones, scatter = handle["src"], handle["zones"], handle["scatter"]
    n = len(srcs)

    def body(*refs):
        ins, lands = refs[:n], refs[n:2 * n]
        send_sems, recv_sems = refs[2 * n], refs[2 * n + 1]
        x, y, c = _place()
        me = 2 * x + y
        for a in range(n):
            for j, (px, py) in enumerate(_chip_peers(x, y)):
                cp = pltpu.make_async_remote_copy(
                    src_ref=ins[a].at[me] if scatter else ins[a], dst_ref=lands[a].at[2 * px + py],
                    send_sem=send_sems.at[3 * a + j], recv_sem=recv_sems.at[3 * a + j],
                    device_id=(px, py, c), device_id_type=MESH)
                cp.wait_send()
                cp.wait_recv()

    res = pl.pallas_call(
        body, name=name, out_shape=[pltpu.HBM(a.shape, a.dtype) for a in list(srcs) + list(zones)],
        in_specs=[HBM_SPEC] * (2 * n) + [SEM_SPEC, SEM_SPEC, ANY], out_specs=[HBM_SPEC] * (2 * n),
        input_output_aliases={i: i for i in range(2 * n)},
        compiler_params=pltpu.CompilerParams(has_side_effects=SIDE_EFFECT),
    )(*srcs, *zones, handle["send"], handle["recv"], after)
    return res[n:]


def swap_start(arrs, name, after):
    n = len(arrs)
    lands = [lax.empty(a.shape, a.dtype) for a in arrs]

    def body(*refs):
        ins, zones = refs[:n], refs[n:2 * n]
        send_sems, recv_sems, token = refs[2 * n + 1], refs[2 * n + 2], refs[-1]
        x, y, c = _place()
        for a in range(n):
            pltpu.make_async_remote_copy(src_ref=ins[a], dst_ref=zones[a], send_sem=send_sems.at[a], recv_sem=recv_sems.at[a],
                                         device_id=(x, y, 1 - c), device_id_type=MESH).start()
        token[...] = jnp.zeros_like(token)

    res = pl.pallas_call(
        body, name=name,
        out_shape=[pltpu.SemaphoreType.DMA((n,)), pltpu.SemaphoreType.DMA((n,))]
        + [pltpu.HBM(a.shape, a.dtype) for a in arrs] * 2 + [jax.ShapeDtypeStruct((8, 128), F32)],
        in_specs=[HBM_SPEC] * (2 * n) + [ANY], out_specs=[SEM_SPEC, SEM_SPEC] + [HBM_SPEC] * (2 * n) + [WHOLE_VMEM],
        input_output_aliases={i: 2 + i for i in range(2 * n)},
        compiler_params=pltpu.CompilerParams(has_side_effects=SIDE_EFFECT),
    )(*[_in_hbm(a) for a in arrs], *[_in_hbm(z) for z in lands], after)
    return dict(send=res[0], recv=res[1], src=res[2:2 + n], zones=res[2 + n:2 + 2 * n], token=res[-1])


def swap_wait(handle, name, after):
    srcs, zones = handle["src"], handle["zones"]
    n = len(srcs)

    def body(*refs):
        ins, lands = refs[:n], refs[n:2 * n]
        send_sems, recv_sems = refs[2 * n], refs[2 * n + 1]
        x, y, c = _place()
        for a in range(n):
            cp = pltpu.make_async_remote_copy(src_ref=ins[a], dst_ref=lands[a], send_sem=send_sems.at[a],
                                              recv_sem=recv_sems.at[a], device_id=(x, y, 1 - c), device_id_type=MESH)
            cp.wait_send()
            cp.wait_recv()

    res = pl.pallas_call(
        body, name=name, out_shape=[pltpu.HBM(a.shape, a.dtype) for a in list(srcs) + list(zones)],
        in_specs=[HBM_SPEC] * (2 * n) + [SEM_SPEC, SEM_SPEC, ANY], out_specs=[HBM_SPEC] * (2 * n),
        input_output_aliases={i: i for i in range(2 * n)},
        compiler_params=pltpu.CompilerParams(has_side_effects=SIDE_EFFECT),
    )(*srcs, *zones, handle["send"], handle["recv"], after)
    return res[:n], res[n:]


TILE_BYTES = 2 * 1024 * 1024


def _row_tile(rows, row_bytes):
    if rows * row_bytes <= TILE_BYTES or rows % 8:
        return rows
    best = 8
    for t in range(8, rows + 1, 8):
        if rows % t == 0 and t * row_bytes <= TILE_BYTES:
            best = t
    return best


def pair_add(a, b, name):
    shape = a.shape
    a, b = a.reshape(-1, shape[-1]), b.reshape(-1, shape[-1])
    R, C = a.shape
    tr = _row_tile(R, C * 4)

    def body(a_ref, b_ref, o_ref):
        o_ref[...] = (a_ref[...].astype(F32) + b_ref[...].astype(F32)).astype(BF16)

    spec = pl.BlockSpec((tr, C), lambda i: (i, 0))
    return pl.pallas_call(body, name=name, grid=(R // tr,), in_specs=[spec, spec], out_specs=spec,
                          out_shape=jax.ShapeDtypeStruct((R, C), BF16), compiler_params=_cparams(("parallel",)))(a, b).reshape(shape)


def sum_lead(parts, name):
    K, R, C = parts.shape
    tr = _row_tile(R, C * 4)

    def body(p_ref, o_ref):
        acc = p_ref[0].astype(F32)
        for j in range(1, K):
            acc = acc + p_ref[j].astype(F32)
        o_ref[...] = acc

    return pl.pallas_call(
        body, name=name, grid=(R // tr,), in_specs=[pl.BlockSpec((K, tr, C), lambda i: (0, i, 0))],
        out_specs=pl.BlockSpec((tr, C), lambda i: (i, 0)), out_shape=jax.ShapeDtypeStruct((R, C), F32),
        compiler_params=_cparams(("parallel",)))(parts)


def adamw(w, g, m, v, name):
    R, C = w.shape
    tr = _row_tile(R, C * 4)

    def body(w_ref, g_ref, m_ref, v_ref, d_ref, mo_ref, vo_ref):
        gv = g_ref[...]
        m2 = ADAM_B1 * m_ref[...] + (1.0 - ADAM_B1) * gv
        v2 = ADAM_B2 * v_ref[...] + (1.0 - ADAM_B2) * (gv * gv)
        m_hat = m2 / (1.0 - ADAM_B1 ** ADAM_STEP)
        v_hat = v2 / (1.0 - ADAM_B2 ** ADAM_STEP)
        d_ref[...] = -ADAM_LR * (m_hat / (jnp.sqrt(v_hat) + ADAM_EPS) + ADAM_WD * w_ref[...])
        mo_ref[...] = m2
        vo_ref[...] = v2

    spec = pl.BlockSpec((tr, C), lambda i: (i, 0))
    return pl.pallas_call(body, name=name, grid=(R // tr,), in_specs=[spec] * 4, out_specs=[spec] * 3,
                          out_shape=[jax.ShapeDtypeStruct((R, C), F32)] * 3, compiler_params=_cparams(("parallel",)))(w, g, m, v)


LANES = 1024


def _pack(arrs, rows):
    out, offs, r = [], [], 0
    for a in arrs:
        flat = a.reshape(-1)
        nr = -(-flat.shape[0] // LANES)
        out.append(jnp.pad(flat, (0, nr * LANES - flat.shape[0])))
        offs.append(r)
        r += nr
    assert r <= rows, (r, rows)
    out.append(jnp.zeros(((rows - r) * LANES,), F32))
    return jnp.concatenate(out).reshape(rows, LANES), offs


def _unpack(packed, offs, shapes):
    flat = packed.reshape(-1)
    return [flat[o * LANES:o * LANES + math.prod(s)].reshape(s) for o, s in zip(offs, shapes)]


WEIGHTS = ["w_ada", "b_ada", "w_in", "b_gate", "conv_a", "a_log", "dt_bias", "norm_a", "rel_bias", "w_branch_a",
           "w_branch_b", "w_o", "ln1_g", "ln1_b", "w_up", "conv_ffn", "b_conv_ffn", "w_down", "ln2_g", "ln2_b"]
BIG = ["w_in", "w_branch_a", "w_branch_b", "w_o", "w_up", "w_down"]
LATE = [n for n in BIG if n != "w_in"]
KEPT_SHARDED = {"w_up"}
COL_SHARDED = {"w_in", "w_up"}
SMALL_SHARDED = {"conv_a": 3 * A_W // N_CHIPS, "rel_bias": B_REL // N_CHIPS, "conv_ffn": 2 * D_FF // N_CHIPS}
SMALL = [n for n in WEIGHTS if n not in BIG and n != "w_ada"]


def _to_full(g4, name):
    if name in KEPT_SHARDED:
        return g4
    if name in COL_SHARDED:
        return jnp.transpose(g4, (1, 0, 2)).reshape(g4.shape[1], -1)
    return g4.reshape(-1, g4.shape[2])


def _to_shards(full, name):
    if name in KEPT_SHARDED:
        return full
    if name in COL_SHARDED:
        return jnp.transpose(full.reshape(full.shape[0], N_CHIPS, -1), (1, 0, 2))
    return full.reshape(N_CHIPS, -1, full.shape[1])


def kernel(x, c, w_ada, b_ada, w_in, b_gate, conv_a, a_log, dt_bias, norm_a, rel_bias, w_branch_a, w_branch_b, w_o, ln1_g, ln1_b, w_up, conv_ffn, b_conv_ffn, w_down, ln2_g, ln2_b, loss_target, m_w_ada, m_b_ada, m_w_in, m_b_gate, m_conv_a, m_a_log, m_dt_bias, m_norm_a, m_rel_bias, m_w_branch_a, m_w_branch_b, m_w_o, m_ln1_g, m_ln1_b, m_w_up, m_conv_ffn, m_b_conv_ffn, m_w_down, m_ln2_g, m_ln2_b, v_w_ada, v_b_ada, v_w_in, v_b_gate, v_conv_a, v_a_log, v_dt_bias, v_norm_a, v_rel_bias, v_w_branch_a, v_w_branch_b, v_w_o, v_ln1_g, v_ln1_b, v_w_up, v_conv_ffn, v_b_conv_ffn, v_w_down, v_ln2_g, v_ln2_b):
    args = dict(locals())
    wts = {n: args[n] for n in WEIGHTS}
    moms = {n: args["m_" + n] for n in WEIGHTS}
    vars_ = {n: args["v_" + n] for n in WEIGHTS}
    xi, yi, ci = _place()
    chip = 2 * xi + yi
    dev = 4 * xi + 2 * yi + ci
    ada_cols = w_ada.shape[2]

    sshapes = [wts[n].shape[1:] for n in SMALL_SHARDED]
    spack, soffs = _pack([wts[n][0] for n in SMALL_SHARDED], 16)
    first = allgather8(jnp.concatenate([jnp.pad(c, ((0, 7), (0, 0))), spack]), "gather_c_small_w").reshape(N_DEV, 24, LANES)
    c_all = first[:, 0]
    b_ada_sh = lax.dynamic_slice(b_ada, (0, chip * ada_cols), (1, ada_cols))
    mod_sh = ada_fwd(c_all, w_ada[0], b_ada_sh)
    mod_g = allgather8(mod_sh, "gather_mod").reshape(N_CHIPS, 2, N_DEV, ada_cols)[:, 0]
    mod = lax.dynamic_slice(mod_g, (0, dev, 0), (N_CHIPS, 1, ada_cols)).reshape(6, D_MODEL)

    (w_in_g4,) = chip_exchange([wts["w_in"][0].astype(BF16)], "gather_w_in", scatter=False)
    wd = split_w_in(_to_full(w_in_g4, "w_in"))
    late_shards = [wts[n][0].astype(BF16) for n in LATE]
    late_gather = exchange_start(late_shards, "gather_late_start", scatter=False, after=w_in_g4)
    mod = mod + late_gather["token"][0, 0]

    def late_weights(after):
        zones = exchange_wait(late_gather, "gather_late_wait", after)
        full = [_to_full(lax.dynamic_update_slice(z, s[None], (chip, 0, 0)), n) for n, z, s in zip(LATE, zones, late_shards)]
        return {n[2:]: f for n, f in zip(LATE, full)}

    sg = first[::2, 8:]
    sparts = [_unpack(sg[j], soffs, sshapes) for j in range(N_CHIPS)]
    sm = {n: wts[n] for n in SMALL if n not in SMALL_SHARDED and n != "b_ada"}
    for i, n in enumerate(SMALL_SHARDED):
        sm[n] = jnp.concatenate([sparts[j][i] for j in range(N_CHIPS)], axis=-1)

    early = {}

    def late_start(g):
        early["swap"] = swap_start([g[n] for n in LATE], "grad_swap_late_start", g[LATE[0]])
        return early["swap"]["token"]

    def late_finish(after):
        mine, theirs = swap_wait(early["swap"], "grad_swap_late_wait", after)
        early["sums"] = [_to_shards(pair_add(a, b, "grad_pair_" + n), n) for n, a, b in zip(LATE, mine, theirs)]
        early["scatter"] = exchange_start(early["sums"], "grad_scatter_start", scatter=True, after=theirs[0])
        return early["scatter"]["token"]

    def w_in_start(g):
        early["swap_in"] = swap_start([g], "grad_swap_w_in_start", g)
        return early["swap_in"]["token"]

    def w_in_finish(after):
        (mine,), (theirs,) = swap_wait(early["swap_in"], "grad_swap_w_in_wait", after)
        early["sum_in"] = _to_shards(pair_add(mine, theirs, "grad_pair_w_in"), "w_in")
        early["scatter_in"] = exchange_start([early["sum_in"]], "grad_scatter_w_in_start", scatter=True, after=theirs)
        return early["scatter_in"]["token"]

    hooks = dict(late_start=late_start, late_finish=late_finish, w_in_start=w_in_start, w_in_finish=w_in_finish)
    loss, dxpre2, dffn, fin, sv = forward_local(x[0], loss_target[0], mod, wd, sm, late_weights)
    grad_x, dmod, gw, gs = backward_local(x[0], mod, sm, dxpre2, dffn, fin, sv, hooks)

    gnames = [n for n in SMALL if n != "b_ada"]
    vec, voffs = _pack([dmod] + [gs[n] for n in gnames] + [loss], 56)
    gathered = allgather8(vec, "gather_small_g").reshape(N_DEV, 56, LANES)
    summed = sum_lead(gathered, "sum_small_g")
    full_shapes = [(6, D_MODEL)] + [gs[n].shape for n in gnames] + [(1, 1)]
    parts = _unpack(summed, voffs, full_shapes)
    grads = {"b_ada": parts[0].reshape(1, -1)}
    for n, p in zip(gnames, parts[1:-1]):
        if n in SMALL_SHARDED:
            p = lax.dynamic_slice_in_dim(p, chip * SMALL_SHARDED[n], SMALL_SHARDED[n], axis=1)
        grads[n] = p.reshape(wts[n].shape)
    loss_total = parts[-1].reshape(())
    dmod_all = gathered[:, 0:6, :].reshape(N_DEV, 6 * D_MODEL)
    grads["w_ada"] = ada_bwd(c_all, lax.dynamic_slice(dmod_all, (0, chip * ada_cols), (N_DEV, ada_cols)))[None]

    def own_slot(zone, sums):
        return lax.dynamic_update_slice(zone, lax.dynamic_slice_in_dim(sums, chip, 1, axis=0), (chip, 0, 0))

    zones = exchange_wait(early["scatter"], "grad_scatter_wait", summed)
    for n, z, s in zip(LATE, zones, early["sums"]):
        grads[n] = sum_lead(own_slot(z, s), "grad_sum_" + n)[None]

    delta, new_m, new_v = {}, {}, {}

    def update(n):
        d, m2, v2 = adamw(wts[n][0], grads[n][0], moms[n][0], vars_[n][0], "adamw_" + n)
        delta[n], new_m[n], new_v[n] = d[None], m2[None], v2[None]

    for n in ["w_ada"] + LATE:
        update(n)
    shapes = [wts[n].shape for n in SMALL]
    packs = [_pack([t[n] for n in SMALL], 32) for t in (wts, grads, moms, vars_)]
    outs = adamw(*[p[0] for p in packs], "adamw_small")
    for res, o in zip((delta, new_m, new_v), outs):
        for n, a in zip(SMALL, _unpack(o, packs[0][1], shapes)):
            res[n] = a
    (zone_in,) = exchange_wait(early["scatter_in"], "grad_scatter_w_in_wait", outs[0])
    grads["w_in"] = sum_lead(own_slot(zone_in, early["sum_in"]), "grad_sum_w_in")[None]
    update("w_in")
    return (loss_total, grad_x[None], *[grads[n] for n in WEIGHTS], *[delta[n] for n in WEIGHTS],
            *[new_m[n] for n in WEIGHTS], *[new_v[n] for n in WEIGHTS])
```

```python
import functools
import math

import jax
import jax.numpy as jnp
from jax import lax
from jax.experimental import pallas as pl
from jax.experimental.pallas import tpu as pltpu

F32 = jnp.float32
BF16 = jnp.bfloat16

D_MODEL = 1024
CHUNK = 64
A_HEADS = 8
A_DK = 128
A_W = A_HEADS * A_DK
A_CONV = 4
B_HEADS = 16
B_DH = 64
B_W = B_HEADS * B_DH
B_PREV = 8
B_BAND = (B_PREV + 1) * CHUNK
B_MAX_REL = 256
B_REL = CHUNK - 1 + B_MAX_REL + 1
D_FF = 2816
FFN_CONV = 3
IN_COLS = 4 * A_W + 2 * A_HEADS + 3 * B_W + 2 * D_MODEL
ALPHA = 2.0 ** 0.25
LN_EPS = 1e-5
RMS_EPS = 1e-6
L2_EPS = 1e-6
NEG_INF = -1e30
ADAM_LR, ADAM_B1, ADAM_B2, ADAM_EPS, ADAM_WD, ADAM_STEP = 0.001, 0.9, 0.999, 1e-08, 0.01, 10
N_CHIPS = 4
N_DEV = 8
VMEM_LIMIT = 56 * 1024 * 1024


def _cparams(sem=None):
    return pltpu.CompilerParams(dimension_semantics=sem, vmem_limit_bytes=VMEM_LIMIT)


_DIMS = {"nn": (((1,), (0,)), ((), ())), "nt": (((1,), (1,)), ((), ())), "tn": (((0,), (0,)), ((), ()))}


MM_TILE_CAP = 1536


def _mm_tile(n):
    return max(t for t in range(128, min(n, MM_TILE_CAP) + 1, 128) if n % t == 0)


def mm(a, b, *, mode, out_dtype, name, acc_in=None, b_shards=False, out_shards=0):
    b_rows, b_cols = (b.shape[1], b.shape[0] * b.shape[2]) if b_shards else b.shape
    if mode == "nn":
        (M, K), (K2, N) = a.shape, (b_rows, b_cols)
    elif mode == "nt":
        (M, K), (N, K2) = a.shape, (b_rows, b_cols)
    else:
        (K, M), (K2, N) = a.shape, (b_rows, b_cols)
    assert K == K2, (a.shape, b.shape, mode)
    tm, tn, tk = _mm_tile(M), _mm_tile(N), _mm_tile(K)
    nk = K // tk

    def body(*refs):
        if acc_in is None:
            a_ref, b_ref, o_ref, acc_ref = refs
        else:
            a_ref, b_ref, c_ref, o_ref, acc_ref = refs
        k = pl.program_id(2)

        @pl.when(k == 0)
        def _():
            if acc_in is None:
                acc_ref[...] = jnp.zeros_like(acc_ref)
            else:
                acc_ref[...] = c_ref[...]

        acc_ref[...] += lax.dot_general(a_ref[...].astype(BF16), b_ref[...].astype(BF16), _DIMS[mode],
                                        preferred_element_type=F32)

        @pl.when(k == nk - 1)
        def _():
            o_ref[...] = acc_ref[...].astype(out_dtype)

    a_spec = pl.BlockSpec((tk, tm), lambda i, j, k: (k, i)) if mode == "tn" else pl.BlockSpec((tm, tk), lambda i, j, k: (i, k))
    if b_shards:
        assert (tk if mode == "nt" else tn) == b.shape[2] and mode != "tn", (b.shape, tn, tk, mode)
        b_spec = (pl.BlockSpec((None, tn, tk), lambda i, j, k: (k, j, 0)) if mode == "nt"
                  else pl.BlockSpec((None, tk, tn), lambda i, j, k: (j, k, 0)))
    else:
        b_spec = pl.BlockSpec((tn, tk), lambda i, j, k: (j, k)) if mode == "nt" else pl.BlockSpec((tk, tn), lambda i, j, k: (k, j))
    o_spec = pl.BlockSpec((tm, tn), lambda i, j, k: (i, j))
    out_shape = jax.ShapeDtypeStruct((M, N), out_dtype)
    if out_shards:
        assert N == out_shards * tn and acc_in is None, (N, tn, out_shards)
        o_spec = pl.BlockSpec((None, tm, tn), lambda i, j, k: (j, i, 0))
        out_shape = jax.ShapeDtypeStruct((out_shards, M, tn), out_dtype)
    ins, in_specs, aliases = [a, b], [a_spec, b_spec], {}
    if acc_in is not None:
        assert acc_in.shape == (M, N) and acc_in.dtype == F32 and out_dtype == F32
        ins.append(acc_in)
        in_specs.append(o_spec)
        aliases = {2: 0}
    return pl.pallas_call(
        body, name=name, grid=(M // tm, N // tn, nk), in_specs=in_specs, out_specs=o_spec,
        out_shape=out_shape, scratch_shapes=[pltpu.VMEM((tm, tn), F32)],
        input_output_aliases=aliases, compiler_params=_cparams(("parallel", "parallel", "arbitrary")),
    )(*ins)


def rowcall(body, *, name, S, ts, ins, outs, scratch=()):
    assert S % ts == 0 and ts % 16 == 0
    nsteps = S // ts
    in_specs, arrays = [], []
    for arr, kind in ins:
        arrays.append(arr)
        if kind == "row":
            in_specs.append(pl.BlockSpec((ts, arr.shape[1]), lambda i: (i, 0)))
        elif kind in ("prev", "next"):
            hr = 8 * (4 // arr.dtype.itemsize)
            per, last = ts // hr, S // hr - 1
            if kind == "prev":
                in_specs.append(pl.BlockSpec((hr, arr.shape[1]), lambda i, per=per: (jnp.maximum(i * per - 1, 0), 0)))
            else:
                in_specs.append(pl.BlockSpec((hr, arr.shape[1]), lambda i, per=per, last=last: (jnp.minimum((i + 1) * per, last), 0)))
        else:
            nd = arr.ndim
            in_specs.append(pl.BlockSpec(arr.shape, lambda i, nd=nd: (0,) * nd))
    out_specs, out_shapes, acc_idx = [], [], []
    for n, (shape, dtype, kind) in enumerate(outs):
        out_shapes.append(jax.ShapeDtypeStruct(shape, dtype))
        if kind == "row":
            out_specs.append(pl.BlockSpec((ts, shape[1]), lambda i: (i, 0)))
        else:
            nd = len(shape)
            out_specs.append(pl.BlockSpec(shape, lambda i, nd=nd: (0,) * nd))
            acc_idx.append(n)
    n_in = len(arrays)

    def wrapped(*refs):
        @pl.when(pl.program_id(0) == 0)
        def _():
            for n in acc_idx:
                refs[n_in + n][...] = jnp.zeros_like(refs[n_in + n])

        body(*refs)

    res = pl.pallas_call(
        wrapped, name=name, grid=(nsteps,), in_specs=in_specs, out_specs=out_specs, out_shape=out_shapes,
        scratch_shapes=list(scratch), compiler_params=_cparams(("arbitrary",) if acc_idx else ("parallel",)),
    )(*arrays)
    return res


def _halo_prev(ref):
    v = ref[...].astype(F32)
    return v[v.shape[0] - 8:]


def _halo_next(ref):
    return ref[...].astype(F32)[:8]


def _shift_down(cur, prev8, k):
    if k == 0:
        return cur
    rolled = pltpu.roll(cur, k, axis=0)
    fix = pltpu.roll(prev8, k, axis=0)
    row = lax.broadcasted_iota(jnp.int32, (8, 1), 0)
    top = jnp.where(row < k, fix, rolled[0:8])
    if cur.shape[0] == 8:
        return top
    return jnp.concatenate([top, rolled[8:]], axis=0)


def _shift_up(cur, next8, k):
    if k == 0:
        return cur
    n = cur.shape[0]
    rolled = pltpu.roll(cur, n - k, axis=0)
    fix = pltpu.roll(next8, 8 - k, axis=0)
    row = lax.broadcasted_iota(jnp.int32, (8, 1), 0)
    bot = jnp.where(row >= 8 - k, fix, rolled[n - 8:n])
    return jnp.concatenate([rolled[:n - 8], bot], axis=0)


def _sigmoid(x):
    return 1.0 / (1.0 + jnp.exp(-x))


def _silu(x):
    return x * _sigmoid(x)


def _silu_and_grad(x):
    s = _sigmoid(x)
    return x * s, s * (1.0 + x * (1.0 - s))


def _softplus(x):
    return jnp.maximum(x, 0.0) + jnp.log1p(jnp.exp(-jnp.abs(x)))


def _split2(x):
    hi = x.astype(BF16)
    return hi, (x - hi.astype(F32)).astype(BF16)


def _dot1(a, b, mode):
    return lax.dot_general(a.astype(BF16), b.astype(BF16), _DIMS[mode], preferred_element_type=F32)


def _dot3(a, b, mode):
    ah, al = _split2(a)
    bh, bl = _split2(b)
    d = lambda p, q: lax.dot_general(p, q, _DIMS[mode], preferred_element_type=F32)
    return d(ah, bh) + (d(ah, bl) + d(al, bh))


def ada_fwd(c_all, w_sh, b_sh):
    n = w_sh.shape[1]
    tn = 512

    def body(c_ref, w_ref, b_ref, o_ref):
        o_ref[...] = _dot1(_silu(c_ref[...]), w_ref[...], "nn") + b_ref[...]

    return pl.pallas_call(
        body, name="ada_fwd", grid=(n // tn,),
        in_specs=[pl.BlockSpec((N_DEV, D_MODEL), lambda j: (0, 0)), pl.BlockSpec((D_MODEL, tn), lambda j: (0, j)),
                  pl.BlockSpec((1, tn), lambda j: (0, j))],
        out_specs=pl.BlockSpec((N_DEV, tn), lambda j: (0, j)), out_shape=jax.ShapeDtypeStruct((N_DEV, n), F32),
        compiler_params=_cparams(("parallel",)),
    )(c_all, w_sh, b_sh)


def ada_bwd(c_all, dmod_sh):
    n = dmod_sh.shape[1]
    tn = 512

    def body(c_ref, d_ref, o_ref):
        o_ref[...] = _dot1(_silu(c_ref[...]), d_ref[...], "tn")

    return pl.pallas_call(
        body, name="ada_bwd", grid=(n // tn,),
        in_specs=[pl.BlockSpec((N_DEV, D_MODEL), lambda j: (0, 0)), pl.BlockSpec((N_DEV, tn), lambda j: (0, j))],
        out_specs=pl.BlockSpec((D_MODEL, tn), lambda j: (0, j)), out_shape=jax.ShapeDtypeStruct((D_MODEL, n), F32),
        compiler_params=_cparams(("parallel",)),
    )(c_all, dmod_sh)


SHIFT_T, SCALE_T, GATE_T, SHIFT_F, SCALE_F, GATE_F = range(6)


def modulate(x, mod, shift_row, scale_row, name):
    S = x.shape[0]

    def body(x_ref, m_ref, o_ref):
        m = m_ref[...]
        o_ref[...] = (x_ref[...] * (1.0 + m[scale_row:scale_row + 1]) + m[shift_row:shift_row + 1]).astype(BF16)

    return rowcall(body, name=name, S=S, ts=512, ins=[(x, "row"), (mod, "vec")], outs=[((S, D_MODEL), BF16, "row")])[0]


def _conv_fwd(cur, prev, w, width):
    y = cur * w[width - 1:width]
    for j in range(width - 1):
        y = y + _shift_down(cur, prev, width - 1 - j) * w[j:j + 1]
    return y


def _prep_a_core(cur, prev, w):
    return _silu_and_grad(_conv_fwd(cur, prev, w, A_CONV))


def prep_a_fwd(qkv_raw, ba, conv_a, a_log, dt_bias):
    S = qkv_raw.shape[0]

    def body(x_ref, xp_ref, ba_ref, w_ref, al_ref, dt_ref, q_ref, k_ref, v_ref, beta_ref, g_ref):
        first = (pl.program_id(0) > 0).astype(F32)
        y, _ = _prep_a_core(x_ref[...].astype(F32), _halo_prev(xp_ref) * first, w_ref[...])
        for h in range(A_HEADS):
            sl = slice(h * A_DK, (h + 1) * A_DK)
            qh = y[:, sl]
            kh = y[:, A_W + h * A_DK:A_W + (h + 1) * A_DK]
            q_ref[:, sl] = qh * (lax.rsqrt(jnp.sum(qh * qh, axis=-1, keepdims=True) + L2_EPS) * (A_DK ** -0.5))
            k_ref[:, sl] = kh * lax.rsqrt(jnp.sum(kh * kh, axis=-1, keepdims=True) + L2_EPS)
        v_ref[...] = y[:, 2 * A_W:3 * A_W]
        bav = ba_ref[...]
        beta_ref[...] = _sigmoid(bav[:, 0:A_HEADS])
        g_ref[...] = -jnp.exp(al_ref[...]) * _softplus(bav[:, A_HEADS:2 * A_HEADS] + dt_ref[...])

    return rowcall(
        body, name="prep_a_fwd", S=S, ts=256,
        ins=[(qkv_raw, "row"), (qkv_raw, "prev"), (ba, "row"), (conv_a, "vec"), (a_log, "vec"), (dt_bias, "vec")],
        outs=[((S, A_W), F32, "row")] * 3 + [((S, A_HEADS), F32, "row")] * 2)


HEAD_GROUP = 2
GROUP_ROWS = HEAD_GROUP * CHUNK
N_HEAD_GROUPS = A_HEADS // HEAD_GROUP
LOG_CHUNK = int(math.log2(CHUNK))


def _tri_masks():
    rb = lax.broadcasted_iota(jnp.int32, (GROUP_ROWS, GROUP_ROWS), 0)
    cb = lax.broadcasted_iota(jnp.int32, (GROUP_ROWS, GROUP_ROWS), 1)
    same = (rb >> LOG_CHUNK) == (cb >> LOG_CHUNK)
    return dict(causal=same & (rb >= cb), strict=same & (rb > cb), eye=rb == cb, upper=same & (cb >= rb),
                last=cb == (rb | (CHUNK - 1)), rb=rb, cb=cb)


def _col_to_row(colv, eye):
    return jnp.sum(jnp.where(eye, colv, 0.0), axis=0, keepdims=True)


def _row_to_col(rowv, eye):
    return jnp.sum(jnp.where(eye, rowv, 0.0), axis=1, keepdims=True)


def _tri_inv(a_list, mk):
    rb, cb = mk["rb"], mk["cb"]
    ts = [jnp.where(mk["eye"], 1.0, 0.0) - jnp.where((rb >> 1) == (cb >> 1), a, 0.0) for a in a_list]
    for lvl in range(1, LOG_CHUNK):
        rs, cs = rb >> lvl, cb >> lvl
        sel = ((rs & 1) == 1) & (cs == rs - 1)
        inner = [_dot3(t, jnp.where(sel, a, 0.0), "nn") for t, a in zip(ts, a_list)]
        ts = [t - _dot3(i, t, "nn") for i, t in zip(inner, ts)]
    return ts


def _stack_heads(ref, grp):
    return jnp.concatenate([ref[:, (grp * HEAD_GROUP + j) * A_DK:(grp * HEAD_GROUP + j + 1) * A_DK]
                            for j in range(HEAD_GROUP)], axis=0)


def _stack_cols(tile, grp):
    return jnp.concatenate([tile[:, grp * HEAD_GROUP + j:grp * HEAD_GROUP + j + 1] for j in range(HEAD_GROUP)], axis=0)


def _delta_local(q, k, v, beta, g, mk):
    causal, strict, eye = mk["causal"], mk["strict"], mk["eye"]
    g_row = _col_to_row(g, eye)
    gc = jnp.sum(jnp.where(causal, g_row, 0.0), axis=1, keepdims=True)
    gc_row = _col_to_row(gc, eye)
    decay = jnp.where(causal, jnp.exp(jnp.where(causal, gc - gc_row, 0.0)), 0.0)
    gam = jnp.exp(gc)
    kb = k * beta
    vb = v * beta
    y = kb * gam
    a = jnp.where(strict, _dot1(kb, k, "nt") * decay, 0.0)
    p = _dot1(q, k, "nt") * decay
    gl = jnp.sum(jnp.where(mk["last"], gc_row, 0.0), axis=1, keepdims=True)
    kd = k * jnp.exp(gl - gc)
    return dict(gc=gc, decay=decay, gam=gam, kb=kb, vb=vb, y=y, a=a, p=p, gl=gl, kd=kd)


def _head_rows(x, j):
    return x[j * CHUNK:(j + 1) * CHUNK]


def delta_fwd(q, k, v, beta, g):
    S = q.shape[0]
    n_chunks = S // CHUNK

    def body(q_ref, k_ref, v_ref, beta_ref, g_ref, o_ref, sprev_ref, t_ref, state_ref):
        @pl.when(pl.program_id(0) == 0)
        def _():
            state_ref[...] = jnp.zeros_like(state_ref)

        mk = _tri_masks()
        betav, gv = beta_ref[...], g_ref[...]
        groups = range(N_HEAD_GROUPS)
        q_all = [_stack_heads(q_ref, grp) for grp in groups]
        locs = [_delta_local(q_all[grp], _stack_heads(k_ref, grp), _stack_heads(v_ref, grp),
                             _stack_cols(betav, grp), _stack_cols(gv, grp), mk) for grp in groups]
        tinvs = _tri_inv([loc["a"] for loc in locs], mk)
        uws = [_dot3(tinvs[grp], jnp.concatenate([locs[grp]["vb"], locs[grp]["y"]], axis=1), "nn") for grp in groups]
        for grp in groups:
            loc, uw = locs[grp], uws[grp]
            t_ref[0, grp] = tinvs[grp]
            qg = q_all[grp] * loc["gam"]
            egl = jnp.exp(loc["gl"])
            vns, o_state = [], []
            for j in range(HEAD_GROUP):
                h = grp * HEAD_GROUP + j
                s0 = state_ref[h]
                sprev_ref[0, h] = s0
                uw_h = _head_rows(uw, j)
                vn = uw_h[:, :A_DK] - _dot1(uw_h[:, A_DK:], s0, "nn")
                vns.append(vn)
                o_state.append(_dot1(_head_rows(qg, j), s0, "nn"))
                state_ref[h] = s0 * egl[(j + 1) * CHUNK - 1:(j + 1) * CHUNK] + _dot1(_head_rows(loc["kd"], j), vn, "tn")
            o_local = _dot1(loc["p"], jnp.concatenate(vns, axis=0), "nn")
            for j in range(HEAD_GROUP):
                h = grp * HEAD_GROUP + j
                o_ref[:, h * A_DK:(h + 1) * A_DK] = o_state[j] + _head_rows(o_local, j)

    tile = pl.BlockSpec((CHUNK, A_W), lambda n: (n, 0))
    small = pl.BlockSpec((CHUNK, A_HEADS), lambda n: (n, 0))
    return pl.pallas_call(
        body, name="delta_fwd", grid=(n_chunks,), in_specs=[tile, tile, tile, small, small],
        out_specs=[tile, pl.BlockSpec((1, A_HEADS, A_DK, A_DK), lambda n: (n, 0, 0, 0)),
                   pl.BlockSpec((1, N_HEAD_GROUPS, GROUP_ROWS, GROUP_ROWS), lambda n: (n, 0, 0, 0))],
        out_shape=[jax.ShapeDtypeStruct((S, A_W), F32), jax.ShapeDtypeStruct((n_chunks, A_HEADS, A_DK, A_DK), F32),
                   jax.ShapeDtypeStruct((n_chunks, N_HEAD_GROUPS, GROUP_ROWS, GROUP_ROWS), F32)],
        scratch_shapes=[pltpu.VMEM((A_HEADS, A_DK, A_DK), F32)],
        compiler_params=_cparams(("arbitrary",)),
    )(q, k, v, beta, g)


def gate_a_fwd(o_pre, z, norm_w):
    S = o_pre.shape[0]

    def body(o_ref, z_ref, nw_ref, out_ref):
        nw = nw_ref[...]
        for h in range(A_HEADS):
            sl = slice(h * A_DK, (h + 1) * A_DK)
            oh = o_ref[:, sl]
            r = lax.rsqrt(jnp.mean(oh * oh, axis=-1, keepdims=True) + RMS_EPS)
            out_ref[:, sl] = (oh * r * nw * _silu(z_ref[:, sl].astype(F32))).astype(BF16)

    return rowcall(body, name="gate_a_fwd", S=S, ts=512, ins=[(o_pre, "row"), (z, "row"), (norm_w, "vec")],
                   outs=[((S, A_W), BF16, "row")])[0]


HEADS_PER_GROUP = 2
GROUP_W = HEADS_PER_GROUP * B_DH
N_GROUPS = B_HEADS // HEADS_PER_GROUP
PAD_ROWS = B_PREV * CHUNK


Q_TILE = 256
Q_CHUNKS = Q_TILE // CHUNK
KEY_WIN = (B_PREV + Q_CHUNKS) * CHUNK


def _band_probs(qh, kh, bias, valid):
    s = _dot1(qh, kh, "nt") * (B_DH ** -0.5) + bias
    s = jnp.where(valid, s, NEG_INF)
    e = jnp.exp(s - jnp.max(s, axis=-1, keepdims=True))
    return e * (1.0 / jnp.sum(e, axis=-1, keepdims=True))


def _attn_specs(S, tile_rows):
    assert PAD_ROWS % tile_rows == 0 and S % tile_rows == 0, (PAD_ROWS, S, tile_rows)
    n_cb = B_W // GROUP_W
    return [pl.BlockSpec((tile_rows, GROUP_W), lambda g, n: (n + PAD_ROWS // tile_rows, g)),
            pl.BlockSpec((PAD_ROWS + S, GROUP_W), lambda g, n: (0, n_cb + g)),
            pl.BlockSpec((PAD_ROWS + S, GROUP_W), lambda g, n: (0, 2 * n_cb + g)),
            pl.BlockSpec((HEADS_PER_GROUP, CHUNK, B_BAND), lambda g, n: (g, 0, 0))]


def _band_valid(first_chunk):
    return lax.broadcasted_iota(jnp.int32, (CHUNK, B_BAND), 1) >= PAD_ROWS - first_chunk * CHUNK


def _chunk_rows(x, qc, rows=CHUNK):
    return x[qc * CHUNK:qc * CHUNK + rows]


FWD_TILE = 512
FWD_CHUNKS = FWD_TILE // CHUNK
FWD_WIN = (B_PREV + FWD_CHUNKS) * CHUNK


def attn_fwd(qkv_pad, bias):
    S = qkv_pad.shape[0] - PAD_ROWS

    def body(q_ref, k_ref, v_ref, b_ref, o_ref):
        n = pl.program_id(1)
        start = pl.multiple_of(n * FWD_TILE, FWD_TILE)
        kwin = k_ref[pl.ds(start, FWD_WIN), :]
        vwin = v_ref[pl.ds(start, FWD_WIN), :]
        qv = q_ref[...]
        pairs = [(qc, hh) for qc in range(FWD_CHUNKS) for hh in range(HEADS_PER_GROUP)]
        sl = lambda hh: slice(hh * B_DH, (hh + 1) * B_DH)
        s = [_dot1(_chunk_rows(qv, qc)[:, sl(hh)], _chunk_rows(kwin, qc, B_BAND)[:, sl(hh)], "nt") for qc, hh in pairs]
        s = [jnp.where(_band_valid(n * FWD_CHUNKS + qc), x * (B_DH ** -0.5) + b_ref[hh], NEG_INF)
             for x, (qc, hh) in zip(s, pairs)]
        e = [jnp.exp(x - jnp.max(x, axis=-1, keepdims=True)) for x in s]
        p = [x * (1.0 / jnp.sum(x, axis=-1, keepdims=True)) for x in e]
        o = [_dot1(x, _chunk_rows(vwin, qc, B_BAND)[:, sl(hh)], "nn") for x, (qc, hh) in zip(p, pairs)]
        rows = [jnp.concatenate(o[qc * HEADS_PER_GROUP:(qc + 1) * HEADS_PER_GROUP], axis=1) for qc in range(FWD_CHUNKS)]
        o_ref[...] = jnp.concatenate(rows, axis=0).astype(BF16)

    return pl.pallas_call(
        body, name="attn_fwd", grid=(N_GROUPS, S // FWD_TILE), in_specs=_attn_specs(S, FWD_TILE),
        out_specs=pl.BlockSpec((FWD_TILE, GROUP_W), lambda g, n: (n, g)),
        out_shape=jax.ShapeDtypeStruct((S, B_W), BF16),
        compiler_params=_cparams(("parallel", "arbitrary")),
    )(qkv_pad, qkv_pad, qkv_pad, bias)


EXT = B_BAND + CHUNK


def bias_expand(rel_bias):
    def body(rev_ref, o_ref):
        rev = rev_ref[...]
        erev = jnp.concatenate([jnp.broadcast_to(rev[:, 0:1], (B_HEADS, EXT - B_REL)), rev], axis=1)
        for i in range(CHUNK):
            o_ref[i] = erev[:, CHUNK - i:CHUNK - i + B_BAND]

    return pl.pallas_call(
        body, name="bias_expand", in_specs=[WHOLE_VMEM], out_specs=WHOLE_VMEM,
        out_shape=jax.ShapeDtypeStruct((CHUNK, B_HEADS, B_BAND), F32),
    )(jnp.flip(rel_bias, axis=1))


def bias_reduce(dbias):
    def body(d_ref, o_ref):
        acc = jnp.zeros((B_HEADS, EXT), F32)
        for i in range(CHUNK):
            acc = acc + jnp.pad(d_ref[i], ((0, 0), (CHUNK - i, i)))
        tail = acc[:, EXT - B_REL:]
        clipped = jnp.sum(acc[:, :EXT - B_REL], axis=1, keepdims=True)
        lane = lax.broadcasted_iota(jnp.int32, (B_HEADS, B_REL), 1)
        o_ref[...] = jnp.where(lane == 0, tail + clipped, tail)

    rev = pl.pallas_call(body, name="bias_reduce", in_specs=[WHOLE_VMEM], out_specs=WHOLE_VMEM,
                         out_shape=jax.ShapeDtypeStruct((B_HEADS, B_REL), F32))(dbias)
    return jnp.flip(rev, axis=1)


def merge_fwd(gates_raw, b_gate, ya, yb):
    S = ya.shape[0]

    def body(g_ref, b_ref, ya_ref, yb_ref, o_ref):
        gt = _sigmoid(g_ref[...].astype(F32) + b_ref[...])
        o_ref[...] = (gt[:, :D_MODEL] * ya_ref[...].astype(F32) + gt[:, D_MODEL:] * yb_ref[...].astype(F32)).astype(BF16)

    return rowcall(body, name="merge_fwd", S=S, ts=512,
                   ins=[(gates_raw, "row"), (b_gate, "vec"), (ya, "row"), (yb, "row")],
                   outs=[((S, D_MODEL), BF16, "row")])[0]


def _ln_stats(xpre):
    mu = jnp.mean(xpre, axis=-1, keepdims=True)
    xc = xpre - mu
    rstd = lax.rsqrt(jnp.mean(xc * xc, axis=-1, keepdims=True) + LN_EPS)
    return xc * rstd, rstd


def ln1_fwd(x, mix, mod, ln_g, ln_b):
    S = x.shape[0]

    def body(x_ref, mix_ref, m_ref, g_ref, b_ref, xpre_ref, x1_ref, h2_ref):
        m = m_ref[...]
        xpre = ALPHA * x_ref[...] + m[GATE_T:GATE_T + 1] * mix_ref[...]
        xhat, _ = _ln_stats(xpre)
        x1 = xhat * g_ref[...] + b_ref[...]
        xpre_ref[...] = xpre
        x1_ref[...] = x1
        h2_ref[...] = (x1 * (1.0 + m[SCALE_F:SCALE_F + 1]) + m[SHIFT_F:SHIFT_F + 1]).astype(BF16)

    return rowcall(body, name="ln1_fwd", S=S, ts=512,
                   ins=[(x, "row"), (mix, "row"), (mod, "vec"), (ln_g, "vec"), (ln_b, "vec")],
                   outs=[((S, D_MODEL), F32, "row"), ((S, D_MODEL), F32, "row"), ((S, D_MODEL), BF16, "row")])


STRIP_FWD = (64, 256)
STRIP_BWD = (128, 128)


def ffn_act_fwd(up, conv_w, conv_b):
    S = up.shape[0]
    ts = 256
    STRIP_ROWS, STRIP_COLS = STRIP_FWD

    def body(u_ref, up_ref, w_ref, b_ref, o_ref, ubuf):
        ubuf[0:8] = _halo_prev(up_ref) * (pl.program_id(0) > 0).astype(F32)
        ubuf[8:8 + ts] = u_ref[...].astype(F32)

        def col_block(j, carry):
            gate = pl.ds(pl.multiple_of(j * STRIP_COLS, STRIP_COLS), STRIP_COLS)
            halves = [gate, pl.ds(pl.multiple_of(D_FF + j * STRIP_COLS, STRIP_COLS), STRIP_COLS)]
            w = [w_ref[:, c] for c in halves]
            bias = [b_ref[:, c] for c in halves]
            for r0 in range(0, ts, STRIP_ROWS):
                uc = []
                for h in range(2):
                    x = ubuf[r0:r0 + STRIP_ROWS + 8, halves[h]]
                    uc.append(bias[h] + sum(
                        w[h][t:t + 1] * (x if t == FFN_CONV - 1 else pltpu.roll(x, FFN_CONV - 1 - t, axis=0))[8:]
                        for t in range(FFN_CONV)))
                o_ref[r0:r0 + STRIP_ROWS, gate] = (_silu(uc[0]) * uc[1]).astype(BF16)
            return carry

        lax.fori_loop(0, D_FF // STRIP_COLS, col_block, 0)

    return rowcall(body, name="ffn_act_fwd", S=S, ts=ts,
                   ins=[(up, "row"), (up, "prev"), (conv_w, "vec"), (conv_b, "vec")],
                   outs=[((S, D_FF), BF16, "row")], scratch=[pltpu.VMEM((ts + 8, 2 * D_FF), F32)])[0]


def final_fwd_bwd(x1, ffn, target, mod, ln_g, ln_b):
    S = x1.shape[0]

    def body(x1_ref, f_ref, t_ref, m_ref, g_ref, b_ref, dxpre_ref, dffn_ref, loss_ref, dgate_ref, dg_ref, db_ref):
        gate = m_ref[...][GATE_F:GATE_F + 1]
        ffn_v = f_ref[...]
        xpre = ALPHA * x1_ref[...] + gate * ffn_v
        xhat, rstd = _ln_stats(xpre)
        err = xhat * g_ref[...] + b_ref[...] - t_ref[...]
        loss_ref[...] += 0.5 * jnp.sum(jnp.mean(err * err, axis=-1, keepdims=True), axis=0, keepdims=True)
        dy = err * (1.0 / D_MODEL)
        dg_ref[...] += jnp.sum(dy * xhat, axis=0, keepdims=True)
        db_ref[...] += jnp.sum(dy, axis=0, keepdims=True)
        dyg = dy * g_ref[...]
        dxpre = rstd * (dyg - jnp.mean(dyg, axis=-1, keepdims=True) - xhat * jnp.mean(dyg * xhat, axis=-1, keepdims=True))
        dxpre_ref[...] = dxpre
        dffn_ref[...] = (gate * dxpre).astype(BF16)
        dgate_ref[...] += jnp.sum(dxpre * ffn_v, axis=0, keepdims=True)

    vec = ((1, D_MODEL), F32, "acc")
    return rowcall(body, name="final_fwd_bwd", S=S, ts=512,
                   ins=[(x1, "row"), (ffn, "row"), (target, "row"), (mod, "vec"), (ln_g, "vec"), (ln_b, "vec")],
                   outs=[((S, D_MODEL), F32, "row"), ((S, D_MODEL), BF16, "row"), ((1, 1), F32, "acc"), vec, vec, vec])


def ffn_act_bwd(dact, up, conv_w, conv_b):
    S = up.shape[0]
    ts = 256
    STRIP_ROWS, STRIP_COLS = STRIP_BWD
    win_u, win_d = STRIP_ROWS + 16, STRIP_ROWS + 8

    def body(d_ref, dn_ref, u_ref, up_ref, un_ref, w_ref, b_ref, dup_ref, dw_ref, db_ref, ubuf, dbuf):
        i = pl.program_id(0)
        ubuf[0:8] = _halo_prev(up_ref) * (i > 0).astype(F32)
        ubuf[8:8 + ts] = u_ref[...].astype(F32)
        ubuf[8 + ts:16 + ts] = _halo_next(un_ref)
        dbuf[0:ts] = d_ref[...].astype(F32)
        dbuf[ts:ts + 8] = _halo_next(dn_ref) * (i < pl.num_programs(0) - 1).astype(F32)

        def col_block(j, carry):
            halves = [pl.ds(pl.multiple_of(j * STRIP_COLS, STRIP_COLS), STRIP_COLS),
                      pl.ds(pl.multiple_of(D_FF + j * STRIP_COLS, STRIP_COLS), STRIP_COLS)]
            w = [w_ref[:, c] for c in halves]
            bias = [b_ref[:, c] for c in halves]
            dw_acc = [[jnp.zeros((1, STRIP_COLS), F32) for _ in range(FFN_CONV)] for _ in halves]
            db_acc = [jnp.zeros((1, STRIP_COLS), F32) for _ in halves]
            for r0 in range(0, ts, STRIP_ROWS):
                shifted = [[x if k == 0 else pltpu.roll(x, k, axis=0) for k in range(FFN_CONV)]
                           for x in (ubuf[r0:r0 + win_u, c] for c in halves)]
                uc = [bias[h] + sum(w[h][t:t + 1] * shifted[h][FFN_CONV - 1 - t][8:8 + win_d] for t in range(FFN_CONV))
                      for h in range(2)]
                dact_w = dbuf[r0:r0 + win_d, halves[0]]
                sg, dsg = _silu_and_grad(uc[0])
                duc = [dact_w * uc[1] * dsg, dact_w * sg]
                for h in range(2):
                    dup = duc[h] * w[h][FFN_CONV - 1:FFN_CONV]
                    for t in range(FFN_CONV - 1):
                        dup = dup + pltpu.roll(duc[h], win_d - (FFN_CONV - 1 - t), axis=0) * w[h][t:t + 1]
                    dup_ref[r0:r0 + STRIP_ROWS, halves[h]] = dup[:STRIP_ROWS].astype(BF16)
                    mine = duc[h][:STRIP_ROWS]
                    db_acc[h] = db_acc[h] + jnp.sum(mine, axis=0, keepdims=True)
                    for t in range(FFN_CONV):
                        dw_acc[h][t] = dw_acc[h][t] + jnp.sum(
                            mine * shifted[h][FFN_CONV - 1 - t][8:8 + STRIP_ROWS], axis=0, keepdims=True)
            for h in range(2):
                dw_ref[:, halves[h]] += jnp.concatenate(dw_acc[h], axis=0)
                db_ref[:, halves[h]] += db_acc[h]
            return carry

        lax.fori_loop(0, D_FF // STRIP_COLS, col_block, 0)

    return rowcall(body, name="ffn_act_bwd", S=S, ts=ts,
                   ins=[(dact, "row"), (dact, "next"), (up, "row"), (up, "prev"), (up, "next"), (conv_w, "vec"), (conv_b, "vec")],
                   outs=[((S, 2 * D_FF), BF16, "row"), ((FFN_CONV, 2 * D_FF), F32, "acc"), ((1, 2 * D_FF), F32, "acc")],
                   scratch=[pltpu.VMEM((ts + 16, 2 * D_FF), F32), pltpu.VMEM((ts + 8, D_FF), F32)])


def ln1_bwd(dxpre2, dh2, xpre1, mix, mod, ln_g, ln_b):
    S = xpre1.shape[0]

    def body(d2_ref, dh_ref, xp_ref, mix_ref, m_ref, g_ref, b_ref, dxpre_ref, dmix_ref,
             dscale_ref, dshift_ref, dgate_ref, dg_ref, db_ref):
        m = m_ref[...]
        xhat, rstd = _ln_stats(xp_ref[...])
        x1 = xhat * g_ref[...] + b_ref[...]
        dh = dh_ref[...]
        dx1 = ALPHA * d2_ref[...] + dh * (1.0 + m[SCALE_F:SCALE_F + 1])
        dscale_ref[...] += jnp.sum(dh * x1, axis=0, keepdims=True)
        dshift_ref[...] += jnp.sum(dh, axis=0, keepdims=True)
        dg_ref[...] += jnp.sum(dx1 * xhat, axis=0, keepdims=True)
        db_ref[...] += jnp.sum(dx1, axis=0, keepdims=True)
        dyg = dx1 * g_ref[...]
        dxpre = rstd * (dyg - jnp.mean(dyg, axis=-1, keepdims=True) - xhat * jnp.mean(dyg * xhat, axis=-1, keepdims=True))
        dxpre_ref[...] = dxpre
        dmix_ref[...] = (m[GATE_T:GATE_T + 1] * dxpre).astype(BF16)
        dgate_ref[...] += jnp.sum(dxpre * mix_ref[...], axis=0, keepdims=True)

    vec = ((1, D_MODEL), F32, "acc")
    return rowcall(body, name="ln1_bwd", S=S, ts=512,
                   ins=[(dxpre2, "row"), (dh2, "row"), (xpre1, "row"), (mix, "row"), (mod, "vec"), (ln_g, "vec"), (ln_b, "vec")],
                   outs=[((S, D_MODEL), F32, "row"), ((S, D_MODEL), BF16, "row"), vec, vec, vec, vec, vec])


def merge_bwd(dmerged, gates_raw, b_gate, ya, yb):
    S = ya.shape[0]

    def body(d_ref, g_ref, b_ref, ya_ref, yb_ref, dya_ref, dyb_ref, dg_ref, dbg_ref):
        gt = _sigmoid(g_ref[...].astype(F32) + b_ref[...])
        d = d_ref[...].astype(F32)
        ga, gb = gt[:, :D_MODEL], gt[:, D_MODEL:]
        dya_ref[...] = (d * ga).astype(BF16)
        dyb_ref[...] = (d * gb).astype(BF16)
        dgr = jnp.concatenate([d * ya_ref[...].astype(F32) * ga * (1.0 - ga),
                               d * yb_ref[...].astype(F32) * gb * (1.0 - gb)], axis=1)
        dg_ref[...] = dgr.astype(BF16)
        dbg_ref[...] += jnp.sum(dgr, axis=0, keepdims=True)

    return rowcall(body, name="merge_bwd", S=S, ts=512,
                   ins=[(dmerged, "row"), (gates_raw, "row"), (b_gate, "vec"), (ya, "row"), (yb, "row")],
                   outs=[((S, D_MODEL), BF16, "row"), ((S, D_MODEL), BF16, "row"), ((S, 2 * D_MODEL), BF16, "row"),
                         ((1, 2 * D_MODEL), F32, "acc")])


def attn_bwd(qkv_pad, bias, do_b):
    S = qkv_pad.shape[0] - PAD_ROWS

    def body(q_ref, k_ref, v_ref, bias_ref, do_ref, dq_ref, dk_ref, dv_ref, db_ref, b_ref):
        n = pl.program_id(1)

        @pl.when(n == 0)
        def _():
            dk_ref[...] = jnp.zeros_like(dk_ref)
            dv_ref[...] = jnp.zeros_like(dv_ref)
            db_ref[...] = jnp.zeros_like(db_ref)
            b_ref[...] = jnp.full(b_ref.shape, NEG_INF, F32)
            for hh in range(HEADS_PER_GROUP):
                for qc in range(Q_CHUNKS):
                    b_ref[hh, qc * CHUNK:(qc + 1) * CHUNK, qc * CHUNK:qc * CHUNK + B_BAND] = bias_ref[hh]

        start = pl.multiple_of(n * Q_TILE, Q_TILE)
        kwin = k_ref[pl.ds(start, KEY_WIN), :]
        vwin = v_ref[pl.ds(start, KEY_WIN), :]
        qv, dov = q_ref[...], do_ref[...]
        valid = lax.broadcasted_iota(jnp.int32, (Q_TILE, KEY_WIN), 1) >= PAD_ROWS - n * Q_TILE
        dqs, dks, dvs = [], [], []
        for hh in range(HEADS_PER_GROUP):
            sl = slice(hh * B_DH, (hh + 1) * B_DH)
            p = _band_probs(qv[:, sl], kwin[:, sl], b_ref[hh], valid)
            dp = _dot1(dov[:, sl], vwin[:, sl], "nt")
            ds = p * (dp - jnp.sum(dp * p, axis=-1, keepdims=True))
            dbh = ds[0:CHUNK, 0:B_BAND]
            for qc in range(1, Q_CHUNKS):
                dbh = dbh + ds[qc * CHUNK:(qc + 1) * CHUNK, qc * CHUNK:qc * CHUNK + B_BAND]
            db_ref[hh] += dbh
            dsq = ds * (B_DH ** -0.5)
            dqs.append(_dot1(dsq, kwin[:, sl], "nn"))
            dks.append(_dot1(dsq, qv[:, sl], "tn"))
            dvs.append(_dot1(p, dov[:, sl], "tn"))
        dq_ref[...] = jnp.concatenate(dqs, axis=1).astype(BF16)
        dk_ref[pl.ds(start, KEY_WIN), :] += jnp.concatenate(dks, axis=1)
        dv_ref[pl.ds(start, KEY_WIN), :] += jnp.concatenate(dvs, axis=1)

    col = pl.BlockSpec((PAD_ROWS + S, GROUP_W), lambda g, n: (0, g))
    tile = pl.BlockSpec((Q_TILE, GROUP_W), lambda g, n: (n, g))
    return pl.pallas_call(
        body, name="attn_bwd", grid=(N_GROUPS, S // Q_TILE), in_specs=_attn_specs(S, Q_TILE) + [tile],
        out_specs=[tile, col, col, pl.BlockSpec((HEADS_PER_GROUP, CHUNK, B_BAND), lambda g, n: (g, 0, 0))],
        out_shape=[jax.ShapeDtypeStruct((S, B_W), BF16), jax.ShapeDtypeStruct((PAD_ROWS + S, B_W), F32),
                   jax.ShapeDtypeStruct((PAD_ROWS + S, B_W), F32), jax.ShapeDtypeStruct((B_HEADS, CHUNK, B_BAND), F32)],
        scratch_shapes=[pltpu.VMEM((HEADS_PER_GROUP, Q_TILE, KEY_WIN), F32)],
        compiler_params=_cparams(("parallel", "arbitrary")),
    )(qkv_pad, qkv_pad, qkv_pad, bias, do_b)


def gate_a_bwd(do_a, o_pre, z, norm_w):
    S = o_pre.shape[0]

    def body(d_ref, o_ref, z_ref, nw_ref, dop_ref, dz_ref, dnw_ref):
        nw = nw_ref[...]
        acc = jnp.zeros((1, A_DK), F32)
        for h in range(A_HEADS):
            sl = slice(h * A_DK, (h + 1) * A_DK)
            oh, zh, dh = o_ref[:, sl], z_ref[:, sl].astype(F32), d_ref[:, sl].astype(F32)
            r = lax.rsqrt(jnp.mean(oh * oh, axis=-1, keepdims=True) + RMS_EPS)
            sz, dsz = _silu_and_grad(zh)
            dz_ref[:, sl] = (dh * oh * r * nw * dsz).astype(BF16)
            acc = acc + jnp.sum(dh * oh * r * sz, axis=0, keepdims=True)
            t = dh * nw * sz
            dop_ref[:, sl] = r * t - oh * (r * r * r) * jnp.mean(t * oh, axis=-1, keepdims=True)
        dnw_ref[...] += acc

    return rowcall(body, name="gate_a_bwd", S=S, ts=512,
                   ins=[(do_a, "row"), (o_pre, "row"), (z, "row"), (norm_w, "vec")],
                   outs=[((S, A_W), F32, "row"), ((S, A_W), BF16, "row"), ((1, A_DK), F32, "acc")])


def delta_bwd(q, k, v, beta, g, sprev, tinv, do):
    S = q.shape[0]
    n_chunks = S // CHUNK

    def body(q_ref, k_ref, v_ref, beta_ref, g_ref, sprev_ref, t_ref, do_ref,
             dq_ref, dk_ref, dv_ref, dbeta_ref, dg_ref, dstate_ref):
        @pl.when(pl.program_id(0) == 0)
        def _():
            dstate_ref[...] = jnp.zeros_like(dstate_ref)

        mk = _tri_masks()
        causal, strict, eye = mk["causal"], mk["strict"], mk["eye"]
        blk_end = (lax.broadcasted_iota(jnp.int32, (GROUP_ROWS, 1), 0) & (CHUNK - 1)) == CHUNK - 1
        lane = lax.broadcasted_iota(jnp.int32, (CHUNK, A_HEADS), 1)
        betav, gv = beta_ref[...], g_ref[...]
        dbeta_t = jnp.zeros((CHUNK, A_HEADS), F32)
        dg_t = jnp.zeros((CHUNK, A_HEADS), F32)
        groups, heads = range(N_HEAD_GROUPS), range(HEAD_GROUP)
        st = [dict() for _ in groups]

        def local_part(grp, s):
            s["qs"], s["ks"], s["vs"] = _stack_heads(q_ref, grp), _stack_heads(k_ref, grp), _stack_heads(v_ref, grp)
            s["dos"] = _stack_heads(do_ref, grp)
            s["bs"] = _stack_cols(betav, grp)
            s["loc"] = loc = _delta_local(s["qs"], s["ks"], s["vs"], s["bs"], _stack_cols(gv, grp), mk)
            s["tinv"] = t_ref[0, grp]
            s["rhs"] = jnp.concatenate([loc["vb"], loc["y"]], axis=1)
            s["uw"] = _dot3(s["tinv"], s["rhs"], "nn")

        def state_part(grp, s):
            loc, uw, dos, qs = s["loc"], s["uw"], s["dos"], s["qs"]
            gam, kd, gl, gc = loc["gam"], loc["kd"], loc["gl"], loc["gc"]
            qg = qs * gam
            egl = jnp.exp(gl)
            hid = [grp * HEAD_GROUP + j for j in heads]
            s0 = [sprev_ref[0, h] for h in hid]
            ds1 = [dstate_ref[h] for h in hid]
            w = [_head_rows(uw, j)[:, A_DK:] for j in heads]
            vn = [_head_rows(uw, j)[:, :A_DK] - _dot1(w[j], s0[j], "nn") for j in heads]
            vns = jnp.concatenate(vn, axis=0)
            dvn_local = _dot1(loc["p"], dos, "tn")
            dvn = [_head_rows(dvn_local, j) + _dot1(_head_rows(kd, j), ds1[j], "nn") for j in heads]
            dvns = jnp.concatenate(dvn, axis=0)
            s["dp"] = jnp.where(causal, _dot1(dos, vns, "nt"), 0.0)
            dqg = jnp.concatenate([_dot1(_head_rows(dos, j), s0[j], "nt") for j in heads], axis=0)
            s["dq"] = dqg * gam
            dgc = jnp.sum(dqg * qg, axis=-1, keepdims=True)
            for j in heads:
                dstate_ref[hid[j]] = (_dot1(_head_rows(qg, j), _head_rows(dos, j), "tn")
                                      + egl[(j + 1) * CHUNK - 1:(j + 1) * CHUNK] * ds1[j] - _dot1(w[j], dvn[j], "tn"))
            dkd = jnp.concatenate([_dot1(vn[j], ds1[j], "nt") for j in heads], axis=0)
            s["dk"] = dkd * jnp.exp(gl - gc)
            t1 = jnp.sum(dkd * kd, axis=-1, keepdims=True)
            dgl = jnp.concatenate(
                [jnp.broadcast_to(jnp.sum(_head_rows(t1, j), axis=0, keepdims=True)
                                  + jnp.sum(jnp.sum(ds1[j] * s0[j], axis=-1, keepdims=True), axis=0, keepdims=True)
                                  * egl[(j + 1) * CHUNK - 1:(j + 1) * CHUNK], (CHUNK, 1)) for j in heads], axis=0)
            s["dgc"] = dgc - t1 + jnp.where(blk_end, dgl, 0.0)
            s["duw"] = jnp.concatenate(
                [dvns, jnp.concatenate([-_dot1(dvn[j], s0[j], "nt") for j in heads], axis=0)], axis=1)

        def solve_part(grp, s):
            s["dvby"] = _dot3(s["tinv"], s["duw"], "tn")
            s["dt"] = _dot3(s["duw"], s["rhs"], "nt")

        def inverse_part_a(grp, s):
            s["tdt"] = _dot3(s["tinv"], s["dt"], "tn")

        def inverse_part_b(grp, s):
            s["da"] = jnp.where(strict, -_dot3(s["tdt"], s["tinv"], "nt"), 0.0)

        def finish(grp, s):
            loc, qs, ks, vs, bs, da, dp, dvby = s["loc"], s["qs"], s["ks"], s["vs"], s["bs"], s["da"], s["dp"], s["dvby"]
            gam, decay = loc["gam"], loc["decay"]
            dm = da * decay
            dn = dp * decay
            e = da * loc["a"] + dp * loc["p"]
            dgc = s["dgc"] + jnp.sum(e, axis=1, keepdims=True) - _row_to_col(jnp.sum(e, axis=0, keepdims=True), eye)
            dy = dvby[:, A_DK:]
            dvb = dvby[:, :A_DK]
            dkb = _dot1(dm, ks, "nn") + dy * gam
            dk = s["dk"] + _dot1(dm, loc["kb"], "tn") + _dot1(dn, qs, "tn") + dkb * bs
            dq = s["dq"] + _dot1(dn, ks, "nn")
            dgc = dgc + jnp.sum(dy * loc["y"], axis=-1, keepdims=True)
            dbeta = jnp.sum(dkb * ks, axis=-1, keepdims=True) + jnp.sum(dvb * vs, axis=-1, keepdims=True)
            dv = dvb * bs
            dgs = jnp.sum(jnp.where(mk["upper"], _col_to_row(dgc, eye), 0.0), axis=1, keepdims=True)
            for j in heads:
                h = grp * HEAD_GROUP + j
                sl = slice(h * A_DK, (h + 1) * A_DK)
                dq_ref[:, sl] = _head_rows(dq, j)
                dk_ref[:, sl] = _head_rows(dk, j)
                dv_ref[:, sl] = _head_rows(dv, j)
            s["dbeta"], s["dgs"] = dbeta, dgs

        for stage in (local_part, state_part, solve_part, inverse_part_a, inverse_part_b, finish):
            for grp in groups:
                stage(grp, st[grp])
        for grp in groups:
            for j in heads:
                h = grp * HEAD_GROUP + j
                dbeta_t = dbeta_t + jnp.where(lane == h, _head_rows(st[grp]["dbeta"], j), 0.0)
                dg_t = dg_t + jnp.where(lane == h, _head_rows(st[grp]["dgs"], j), 0.0)
        dbeta_ref[...] = dbeta_t
        dg_ref[...] = dg_t

    rev = lambda n: (n_chunks - 1 - n, 0)
    rev4 = lambda n: (n_chunks - 1 - n, 0, 0, 0)
    tile = pl.BlockSpec((CHUNK, A_W), rev)
    small = pl.BlockSpec((CHUNK, A_HEADS), rev)
    return pl.pallas_call(
        body, name="delta_bwd", grid=(n_chunks,),
        in_specs=[tile, tile, tile, small, small, pl.BlockSpec((1, A_HEADS, A_DK, A_DK), rev4),
                  pl.BlockSpec((1, N_HEAD_GROUPS, GROUP_ROWS, GROUP_ROWS), rev4), tile],
        out_specs=[tile, tile, tile, small, small],
        out_shape=[jax.ShapeDtypeStruct((S, A_W), F32)] * 3 + [jax.ShapeDtypeStruct((S, A_HEADS), F32)] * 2,
        scratch_shapes=[pltpu.VMEM((A_HEADS, A_DK, A_DK), F32)],
        compiler_params=_cparams(("arbitrary",)),
    )(q, k, v, beta, g, sprev, tinv, do)


def _prep_a_dpre(raw, raw_prev, w, dq, dk, dv):
    y, dy_dpre = _prep_a_core(raw, raw_prev, w)
    parts = []
    for h in range(A_HEADS):
        yq = y[:, h * A_DK:(h + 1) * A_DK]
        dqh = dq[:, h * A_DK:(h + 1) * A_DK]
        rq = lax.rsqrt(jnp.sum(yq * yq, axis=-1, keepdims=True) + L2_EPS)
        parts.append((A_DK ** -0.5) * (rq * dqh - yq * (rq * rq * rq) * jnp.sum(dqh * yq, axis=-1, keepdims=True)))
    for h in range(A_HEADS):
        yk = y[:, A_W + h * A_DK:A_W + (h + 1) * A_DK]
        dkh = dk[:, h * A_DK:(h + 1) * A_DK]
        rk = lax.rsqrt(jnp.sum(yk * yk, axis=-1, keepdims=True) + L2_EPS)
        parts.append(rk * dkh - yk * (rk * rk * rk) * jnp.sum(dkh * yk, axis=-1, keepdims=True))
    parts.append(dv)
    return jnp.concatenate(parts, axis=1) * dy_dpre


def prep_a_bwd(qkv_raw, ba, conv_a, a_log, dt_bias, dq, dk, dv, dbeta, dg):
    S = qkv_raw.shape[0]
    ts = 256

    def body(x_ref, xp_ref, xn_ref, ba_ref, w_ref, al_ref, dt_ref, dq_ref, dqn_ref, dk_ref, dkn_ref, dv_ref, dvn_ref,
             dbeta_ref, dg_ref, draw_ref, dba_ref, dw_ref, dal_ref, ddt_ref):
        i = pl.program_id(0)
        first = (i > 0).astype(F32)
        last = (i < pl.num_programs(0) - 1).astype(F32)
        w = w_ref[...]
        cur, prev = x_ref[...].astype(F32), _halo_prev(xp_ref) * first
        dpre = _prep_a_dpre(cur, prev, w, dq_ref[...], dk_ref[...], dv_ref[...])
        dpre_n = _prep_a_dpre(_halo_next(xn_ref), cur[ts - 8:ts], w, _halo_next(dqn_ref), _halo_next(dkn_ref),
                              _halo_next(dvn_ref)) * last
        for j in range(A_CONV):
            dw_ref[j:j + 1, :] += jnp.sum(dpre * _shift_down(cur, prev, A_CONV - 1 - j), axis=0, keepdims=True)
        draw = dpre * w[A_CONV - 1:A_CONV]
        for j in range(A_CONV - 1):
            draw = draw + _shift_up(dpre, dpre_n, A_CONV - 1 - j) * w[j:j + 1]
        draw_ref[...] = draw.astype(BF16)
        bav = ba_ref[...]
        beta = _sigmoid(bav[:, 0:A_HEADS])
        xa = bav[:, A_HEADS:2 * A_HEADS] + dt_ref[...]
        nexp = -jnp.exp(al_ref[...])
        dgv = dg_ref[...]
        da = dgv * nexp * _sigmoid(xa)
        dba_ref[:, 0:A_HEADS] = dbeta_ref[...] * beta * (1.0 - beta)
        dba_ref[:, A_HEADS:2 * A_HEADS] = da
        dal_ref[...] += jnp.sum(dgv * nexp * _softplus(xa), axis=0, keepdims=True)
        ddt_ref[...] += jnp.sum(da, axis=0, keepdims=True)

    return rowcall(
        body, name="prep_a_bwd", S=S, ts=ts,
        ins=[(qkv_raw, "row"), (qkv_raw, "prev"), (qkv_raw, "next"), (ba, "row"), (conv_a, "vec"), (a_log, "vec"),
             (dt_bias, "vec"), (dq, "row"), (dq, "next"), (dk, "row"), (dk, "next"), (dv, "row"), (dv, "next"),
             (dbeta, "row"), (dg, "row")],
        outs=[((S, 3 * A_W), BF16, "row"), ((S, 2 * A_HEADS), F32, "row"), ((A_CONV, 3 * A_W), F32, "acc"),
              ((1, A_HEADS), F32, "acc"), ((1, A_HEADS), F32, "acc")])


def grad_x_final(dh1, x, dxpre1, mod):
    S = x.shape[0]

    def body(dh_ref, x_ref, dx_ref, m_ref, gx_ref, dscale_ref, dshift_ref):
        dh = dh_ref[...]
        gx_ref[...] = ALPHA * dx_ref[...] + dh * (1.0 + m_ref[...][SCALE_T:SCALE_T + 1])
        dscale_ref[...] += jnp.sum(dh * x_ref[...], axis=0, keepdims=True)
        dshift_ref[...] += jnp.sum(dh, axis=0, keepdims=True)

    vec = ((1, D_MODEL), F32, "acc")
    return rowcall(body, name="grad_x_final", S=S, ts=512, ins=[(dh1, "row"), (x, "row"), (dxpre1, "row"), (mod, "vec")],
                   outs=[((S, D_MODEL), F32, "row"), vec, vec])


_C_QKV, _C_Z, _C_BA, _C_QKVB, _C_G = 0, 3 * A_W, 4 * A_W, 4 * A_W + 2 * A_HEADS, 4 * A_W + 2 * A_HEADS + 3 * B_W
BA_PAD = 128


def split_w_in(w_in):
    ba = jnp.pad(w_in[:, _C_BA:_C_QKVB], ((0, 0), (0, BA_PAD - 2 * A_HEADS)))
    return dict(qkv=w_in[:, _C_QKV:_C_Z], z=w_in[:, _C_Z:_C_BA], ba=ba, qkvb=w_in[:, _C_QKVB:_C_G], g=w_in[:, _C_G:])


def join_w_in(p):
    return jnp.concatenate([p["qkv"], p["z"], p["ba"][:, :2 * A_HEADS], p["qkvb"], p["g"]], axis=1)


def forward_local(x, target, mod, w, sm, late_weights=None):
    h1 = modulate(x, mod, SHIFT_T, SCALE_T, "mod_t")
    qkv_raw = mm(h1, w["qkv"], mode="nn", out_dtype=BF16, name="proj_qkv")
    z = mm(h1, w["z"], mode="nn", out_dtype=BF16, name="proj_z")
    ba = mm(h1, w["ba"], mode="nn", out_dtype=F32, name="proj_ba")
    qkvb = mm(h1, w["qkvb"], mode="nn", out_dtype=BF16, name="proj_qkvb")
    gates_raw = mm(h1, w["g"], mode="nn", out_dtype=BF16, name="proj_g")
    q, k, v, beta, g = prep_a_fwd(qkv_raw, ba, sm["conv_a"], sm["a_log"], sm["dt_bias"])
    o_pre, sprev, tinv = delta_fwd(q, k, v, beta, g)
    o_a = gate_a_fwd(o_pre, z, sm["norm_a"])
    qkv_pad = jnp.pad(qkvb, ((PAD_ROWS, 0), (0, 0)))
    bias = jnp.transpose(bias_expand(sm["rel_bias"]), (1, 0, 2))
    o_b = attn_fwd(qkv_pad, bias)
    if late_weights is not None:
        w = dict(w, **late_weights(o_b))
    ya = mm(o_a, w["branch_a"], mode="nn", out_dtype=BF16, name="branch_a")
    yb = mm(o_b, w["branch_b"], mode="nn", out_dtype=BF16, name="branch_b")
    merged = merge_fwd(gates_raw, sm["b_gate"], ya, yb)
    mix = mm(merged, w["o"], mode="nn", out_dtype=F32, name="mix")
    xpre1, x1, h2 = ln1_fwd(x, mix, mod, sm["ln1_g"], sm["ln1_b"])
    up = mm(h2, w["up"], mode="nn", out_dtype=BF16, name="ffn_up", b_shards=True)
    act = ffn_act_fwd(up, sm["conv_ffn"], sm["b_conv_ffn"])
    ffn = mm(act, w["down"], mode="nn", out_dtype=F32, name="ffn_down")
    dxpre2, dffn, loss, dgate_f, dln2_g, dln2_b = final_fwd_bwd(x1, ffn, target, mod, sm["ln2_g"], sm["ln2_b"])
    saved = dict(h1=h1, qkv_raw=qkv_raw, z=z, ba=ba, gates_raw=gates_raw, q=q, k=k, v=v, beta=beta, g=g,
                 o_pre=o_pre, sprev=sprev, tinv=tinv, o_a=o_a, qkv_pad=qkv_pad, bias=bias, o_b=o_b, ya=ya, yb=yb,
                 merged=merged, mix=mix, xpre1=xpre1, x1=x1, h2=h2, up=up, act=act, ffn=ffn, w=w)
    return loss, dxpre2, dffn, dict(gate_f=dgate_f, ln2_g=dln2_g, ln2_b=dln2_b), saved


def backward_local(x, mod, sm, dxpre2, dffn, fin, sv, hooks=None):
    w = sv["w"]
    dact = mm(dffn, w["down"], mode="nt", out_dtype=BF16, name="d_act")
    gw_down = mm(sv["act"], dffn, mode="tn", out_dtype=BF16, name="gw_down")
    dup, dconv_ffn, db_conv_ffn = ffn_act_bwd(dact, sv["up"], sm["conv_ffn"], sm["b_conv_ffn"])
    dh2 = mm(dup, w["up"], mode="nt", out_dtype=F32, name="d_h2", b_shards=True)
    gw_up = mm(sv["h2"], dup, mode="tn", out_dtype=BF16, name="gw_up", out_shards=N_CHIPS)
    dxpre1, dmix, dsc_f, dsh_f, dgate_t, dln1_g, dln1_b = ln1_bwd(
        dxpre2, dh2, sv["xpre1"], sv["mix"], mod, sm["ln1_g"], sm["ln1_b"])
    dmerged = mm(dmix, w["o"], mode="nt", out_dtype=BF16, name="d_merged")
    gw_o = mm(sv["merged"], dmix, mode="tn", out_dtype=BF16, name="gw_o")
    dya, dyb, dgates, db_gate = merge_bwd(dmerged, sv["gates_raw"], sm["b_gate"], sv["ya"], sv["yb"])
    do_a = mm(dya, w["branch_a"], mode="nt", out_dtype=BF16, name="d_oa")
    gw_branch_a = mm(sv["o_a"], dya, mode="tn", out_dtype=BF16, name="gw_branch_a")
    do_b = mm(dyb, w["branch_b"], mode="nt", out_dtype=BF16, name="d_ob")
    gw_branch_b = mm(sv["o_b"], dyb, mode="tn", out_dtype=BF16, name="gw_branch_b")
    bias = sv["bias"]
    if hooks is not None:
        bias = bias + hooks["late_start"](dict(w_branch_a=gw_branch_a, w_branch_b=gw_branch_b, w_o=gw_o, w_up=gw_up,
                                               w_down=gw_down))[0, 0]
    dq_b, dk_pad, dv_pad, dbias = attn_bwd(sv["qkv_pad"], bias, do_b)
    if hooks is not None:
        dbias = dbias + hooks["late_finish"](dq_b)[0, 0]
    dqkvb = jnp.concatenate([dq_b, dk_pad[PAD_ROWS:].astype(BF16), dv_pad[PAD_ROWS:].astype(BF16)], axis=1)
    drel_bias = bias_reduce(jnp.transpose(dbias, (1, 0, 2)))
    do_pre, dz, dnorm_a = gate_a_bwd(do_a, sv["o_pre"], sv["z"], sm["norm_a"])
    dq, dk, dv, dbeta, dg = delta_bwd(sv["q"], sv["k"], sv["v"], sv["beta"], sv["g"], sv["sprev"], sv["tinv"], do_pre)
    dqkv_raw, dba16, dconv_a, da_log, ddt_bias = prep_a_bwd(
        sv["qkv_raw"], sv["ba"], sm["conv_a"], sm["a_log"], sm["dt_bias"], dq, dk, dv, dbeta, dg)
    dba = jnp.pad(dba16, ((0, 0), (0, BA_PAD - 2 * A_HEADS))).astype(BF16)
    pieces = dict(qkv=dqkv_raw, z=dz, ba=dba, qkvb=dqkvb, g=dgates)
    gw_in = join_w_in({key: mm(sv["h1"], dpiece, mode="tn", out_dtype=BF16, name="gw_in_" + key)
                       for key, dpiece in pieces.items()})
    w_ba = w["ba"]
    w_z = w["z"]
    if hooks is not None:
        w_ba = w_ba + hooks["w_in_start"](gw_in)[0, 0].astype(BF16)
    dh1 = mm(pieces["ba"], w_ba, mode="nt", out_dtype=F32, name="d_h1_ba")
    dh1 = mm(pieces["qkv"], w["qkv"], mode="nt", out_dtype=F32, name="d_h1_qkv", acc_in=dh1)
    if hooks is not None:
        w_z = w_z + hooks["w_in_finish"](dh1)[0, 0].astype(BF16)
    dh1 = mm(pieces["z"], w_z, mode="nt", out_dtype=F32, name="d_h1_z", acc_in=dh1)
    for key in ("qkvb", "g"):
        dh1 = mm(pieces[key], w[key], mode="nt", out_dtype=F32, name="d_h1_" + key, acc_in=dh1)
    grad_x, dsc_t, dsh_t = grad_x_final(dh1, x, dxpre1, mod)
    dmod = jnp.concatenate([dsh_t, dsc_t, dgate_t, dsh_f, dsc_f, fin["gate_f"]], axis=0)
    gw = dict(w_in=gw_in, w_branch_a=gw_branch_a, w_branch_b=gw_branch_b, w_o=gw_o, w_up=gw_up, w_down=gw_down)
    gs = dict(b_gate=db_gate, conv_a=dconv_a, a_log=da_log, dt_bias=ddt_bias, norm_a=dnorm_a, rel_bias=drel_bias,
              ln1_g=dln1_g, ln1_b=dln1_b, conv_ffn=dconv_ffn, b_conv_ffn=db_conv_ffn, ln2_g=fin["ln2_g"], ln2_b=fin["ln2_b"])
    return grad_x, dmod, gw, gs


MESH = pl.DeviceIdType.MESH
ANY = pl.BlockSpec(memory_space=pl.ANY)
WHOLE_VMEM = pl.BlockSpec(memory_space=pltpu.VMEM)


def _place():
    return lax.axis_index("x"), lax.axis_index("y"), lax.axis_index("c")


def allgather8(blk, name):
    m_per, n = blk.shape

    def body(x_ref, out_ref, send_sems, recv_sems, local_sem):
        x, y, c = _place()
        me, sibling = (x, y, c), (x, y, 1 - c)
        chips = [(1 - x, y), (x, 1 - y), (1 - x, 1 - y)]

        def rows(px, py, pc):
            return out_ref.at[pl.ds((4 * px + 2 * py + pc) * m_per, m_per), :]

        def copy(k, block, to, src=None):
            return pltpu.make_async_remote_copy(
                src_ref=rows(*block) if src is None else src, dst_ref=rows(*block),
                send_sem=send_sems.at[k], recv_sem=recv_sems.at[k], device_id=to, device_id_type=MESH)

        mine = pltpu.make_async_copy(x_ref, rows(*me), local_sem)
        mine.start()
        first = [copy(0, me, sibling, src=x_ref)]
        first += [copy(1 + j, me, (*chip, c), src=x_ref) for j, chip in enumerate(chips)]
        for cp in first:
            cp.start()
        passed = [copy(4 + j, (*chip, c), sibling) for j, chip in enumerate(chips)]
        for j, chip in enumerate(chips):
            copy(1 + j, (*chip, c), me).wait_recv()
            passed[j].start()
        copy(0, sibling, me).wait_recv()
        for j, chip in enumerate(chips):
            copy(4 + j, (*chip, 1 - c), me).wait_recv()
        for cp in first + passed:
            cp.wait_send()
        mine.wait()

    return pl.pallas_call(
        body, name=name, out_shape=jax.ShapeDtypeStruct((N_DEV * m_per, n), blk.dtype),
        in_specs=[WHOLE_VMEM], out_specs=WHOLE_VMEM,
        scratch_shapes=[pltpu.SemaphoreType.DMA((7,)), pltpu.SemaphoreType.DMA((7,)), pltpu.SemaphoreType.DMA],
    )(blk)


def _chip_peers(x, y):
    return [(1 - x, y), (x, 1 - y), (1 - x, 1 - y)]


def chip_exchange(arrs, name, scatter):
    n = len(arrs)

    def body(*refs):
        ins, outs = refs[:n], refs[n:2 * n]
        send_sems, recv_sems, local_sems = refs[2 * n:]
        x, y, c = _place()
        me = 2 * x + y
        sibling = (x, y, 1 - c)
        peers = _chip_peers(x, y)

        def half(ref, which):
            r2 = ref.shape[0] // 2
            return ref.at[pl.ds(which * r2, r2), :]

        def outgoing(a, chip):
            return ins[a].at[chip] if scatter else ins[a]

        def copy(k, src, dst, to):
            return pltpu.make_async_remote_copy(src_ref=src, dst_ref=dst, send_sem=send_sems.at[k],
                                                recv_sem=recv_sems.at[k], device_id=to, device_id_type=MESH)

        started, local = [], []
        for a in range(n):
            lc = pltpu.make_async_copy(outgoing(a, me), outs[a].at[me], local_sems.at[a])
            lc.start()
            local.append(lc)
            for j, (px, py) in enumerate(peers):
                cp = copy(6 * a + j, half(outgoing(a, 2 * px + py), c), half(outs[a].at[me], c), (px, py, c))
                cp.start()
                started.append(cp)
        for a in range(n):
            for j, (px, py) in enumerate(peers):
                landed = half(outs[a].at[2 * px + py], c)
                copy(6 * a + j, landed, landed, (px, py, c)).wait_recv()
                relay = copy(6 * a + 3 + j, landed, landed, sibling)
                relay.start()
                started.append(relay)
        for a in range(n):
            for j, (px, py) in enumerate(peers):
                other = half(outs[a].at[2 * px + py], 1 - c)
                copy(6 * a + 3 + j, other, other, sibling).wait_recv()
        for cp in started:
            cp.wait_send()
        for lc in local:
            lc.wait()

    out_shape = [jax.ShapeDtypeStruct(a.shape if scatter else (N_CHIPS,) + a.shape, a.dtype) for a in arrs]
    return pl.pallas_call(
        body, name=name, out_shape=out_shape, in_specs=[ANY] * n, out_specs=[ANY] * n,
        scratch_shapes=[pltpu.SemaphoreType.DMA((6 * n,)), pltpu.SemaphoreType.DMA((6 * n,)), pltpu.SemaphoreType.DMA((n,))],
    )(*arrs)


HBM_SPEC = pl.BlockSpec(memory_space=pltpu.HBM)
SEM_SPEC = pl.BlockSpec(memory_space=pltpu.SEMAPHORE)
SIDE_EFFECT = pltpu.SideEffectType.DATAFLOW_SIDE_EFFECTING


def _in_hbm(a):
    return pltpu.with_memory_space_constraint(a, pltpu.HBM)


def exchange_start(arrs, name, scatter, after):
    n = len(arrs)
    lands = [lax.empty(a.shape if scatter else (N_CHIPS,) + a.shape, a.dtype) for a in arrs]

    def body(*refs):
        ins, zones = refs[:n], refs[n:2 * n]
        send_sems, recv_sems, token = refs[2 * n + 1], refs[2 * n + 2], refs[-1]
        x, y, c = _place()
        me = 2 * x + y
        for a in range(n):
            for j, (px, py) in enumerate(_chip_peers(x, y)):
                pltpu.make_async_remote_copy(
                    src_ref=ins[a].at[2 * px + py] if scatter else ins[a], dst_ref=zones[a].at[me],
                    send_sem=send_sems.at[3 * a + j], recv_sem=recv_sems.at[3 * a + j],
                    device_id=(px, py, c), device_id_type=MESH).start()
        token[...] = jnp.zeros_like(token)

    res = pl.pallas_call(
        body, name=name,
        out_shape=[pltpu.SemaphoreType.DMA((3 * n,)), pltpu.SemaphoreType.DMA((3 * n,))]
        + [pltpu.HBM(a.shape, a.dtype) for a in arrs] + [pltpu.HBM(z.shape, z.dtype) for z in lands]
        + [jax.ShapeDtypeStruct((8, 128), F32)],
        in_specs=[HBM_SPEC] * (2 * n) + [ANY], out_specs=[SEM_SPEC, SEM_SPEC] + [HBM_SPEC] * (2 * n) + [WHOLE_VMEM],
        input_output_aliases={i: 2 + i for i in range(2 * n)},
        compiler_params=pltpu.CompilerParams(has_side_effects=SIDE_EFFECT),
    )(*[_in_hbm(a) for a in arrs], *[_in_hbm(z) for z in lands], after)
    return dict(send=res[0], recv=res[1], src=res[2:2 + n], zones=res[2 + n:2 + 2 * n], token=res[-1], scatter=scatter)


def exchange_wait(handle, name, after):
    srcs, zones, scatter = handle["src"], handle["zones"], handle["scatter"]
    n = len(srcs)

    def body(*refs):
        ins, lands = refs[:n], refs[n:2 * n]
        send_sems, recv_sems = refs[2 * n], refs[2 * n + 1]
        x, y, c = _place()
        me = 2 * x + y
        for a in range(n):
            for j, (px, py) in enumerate(_chip_peers(x, y)):
                cp = pltpu.make_async_remote_copy(
                    src_ref=ins[a].at[me] if scatter else ins[a], dst_ref=lands[a].at[2 * px + py],
                    send_sem=send_sems.at[3 * a + j], recv_sem=recv_sems.at[3 * a + j],
                    device_id=(px, py, c), device_id_type=MESH)
                cp.wait_send()
                cp.wait_recv()

    res = pl.pallas_call(
        body, name=name, out_shape=[pltpu.HBM(a.shape, a.dtype) for a in list(srcs) + list(zones)],
        in_specs=[HBM_SPEC] * (2 * n) + [SEM_SPEC, SEM_SPEC, ANY], out_specs=[HBM_SPEC] * (2 * n),
        input_output_aliases={i: i for i in range(2 * n)},
        compiler_params=pltpu.CompilerParams(has_side_effects=SIDE_EFFECT),
    )(*srcs, *zones, handle["send"], handle["recv"], after)
    return res[n:]


def swap_start(arrs, name, after):
    n = len(arrs)
    lands = [lax.empty(a.shape, a.dtype) for a in arrs]

    def body(*refs):
        ins, zones = refs[:n], refs[n:2 * n]
        send_sems, recv_sems, token = refs[2 * n + 1], refs[2 * n + 2], refs[-1]
        x, y, c = _place()
        for a in range(n):
            pltpu.make_async_remote_copy(src_ref=ins[a], dst_ref=zones[a], send_sem=send_sems.at[a], recv_sem=recv_sems.at[a],
                                         device_id=(x, y, 1 - c), device_id_type=MESH).start()
        token[...] = jnp.zeros_like(token)

    res = pl.pallas_call(
        body, name=name,
        out_shape=[pltpu.SemaphoreType.DMA((n,)), pltpu.SemaphoreType.DMA((n,))]
        + [pltpu.HBM(a.shape, a.dtype) for a in arrs] * 2 + [jax.ShapeDtypeStruct((8, 128), F32)],
        in_specs=[HBM_SPEC] * (2 * n) + [ANY], out_specs=[SEM_SPEC, SEM_SPEC] + [HBM_SPEC] * (2 * n) + [WHOLE_VMEM],
        input_output_aliases={i: 2 + i for i in range(2 * n)},
        compiler_params=pltpu.CompilerParams(has_side_effects=SIDE_EFFECT),
    )(*[_in_hbm(a) for a in arrs], *[_in_hbm(z) for z in lands], after)
    return dict(send=res[0], recv=res[1], src=res[2:2 + n], zones=res[2 + n:2 + 2 * n], token=res[-1])


def swap_wait(handle, name, after):
    srcs, zones = handle["src"], handle["zones"]
    n = len(srcs)

    def body(*refs):
        ins, lands = refs[:n], refs[n:2 * n]
        send_sems, recv_sems = refs[2 * n], refs[2 * n + 1]
        x, y, c = _place()
        for a in range(n):
            cp = pltpu.make_async_remote_copy(src_ref=ins[a], dst_ref=lands[a], send_sem=send_sems.at[a],
                                              recv_sem=recv_sems.at[a], device_id=(x, y, 1 - c), device_id_type=MESH)
            cp.wait_send()
            cp.wait_recv()

    res = pl.pallas_call(
        body, name=name, out_shape=[pltpu.HBM(a.shape, a.dtype) for a in list(srcs) + list(zones)],
        in_specs=[HBM_SPEC] * (2 * n) + [SEM_SPEC, SEM_SPEC, ANY], out_specs=[HBM_SPEC] * (2 * n),
        input_output_aliases={i: i for i in range(2 * n)},
        compiler_params=pltpu.CompilerParams(has_side_effects=SIDE_EFFECT),
    )(*srcs, *zones, handle["send"], handle["recv"], after)
    return res[:n], res[n:]


TILE_BYTES = 2 * 1024 * 1024


def _row_tile(rows, row_bytes):
    if rows * row_bytes <= TILE_BYTES or rows % 8:
        return rows
    best = 8
    for t in range(8, rows + 1, 8):
        if rows % t == 0 and t * row_bytes <= TILE_BYTES:
            best = t
    return best


def pair_add(a, b, name):
    shape = a.shape
    a, b = a.reshape(-1, shape[-1]), b.reshape(-1, shape[-1])
    R, C = a.shape
    tr = _row_tile(R, C * 4)

    def body(a_ref, b_ref, o_ref):
        o_ref[...] = (a_ref[...].astype(F32) + b_ref[...].astype(F32)).astype(BF16)

    spec = pl.BlockSpec((tr, C), lambda i: (i, 0))
    return pl.pallas_call(body, name=name, grid=(R // tr,), in_specs=[spec, spec], out_specs=spec,
                          out_shape=jax.ShapeDtypeStruct((R, C), BF16), compiler_params=_cparams(("parallel",)))(a, b).reshape(shape)


def sum_lead(parts, name):
    K, R, C = parts.shape
    tr = _row_tile(R, C * 4)

    def body(p_ref, o_ref):
        acc = p_ref[0].astype(F32)
        for j in range(1, K):
            acc = acc + p_ref[j].astype(F32)
        o_ref[...] = acc

    return pl.pallas_call(
        body, name=name, grid=(R // tr,), in_specs=[pl.BlockSpec((K, tr, C), lambda i: (0, i, 0))],
        out_specs=pl.BlockSpec((tr, C), lambda i: (i, 0)), out_shape=jax.ShapeDtypeStruct((R, C), F32),
        compiler_params=_cparams(("parallel",)))(parts)


def adamw(w, g, m, v, name):
    R, C = w.shape
    tr = _row_tile(R, C * 4)

    def body(w_ref, g_ref, m_ref, v_ref, d_ref, mo_ref, vo_ref):
        gv = g_ref[...]
        m2 = ADAM_B1 * m_ref[...] + (1.0 - ADAM_B1) * gv
        v2 = ADAM_B2 * v_ref[...] + (1.0 - ADAM_B2) * (gv * gv)
        m_hat = m2 / (1.0 - ADAM_B1 ** ADAM_STEP)
        v_hat = v2 / (1.0 - ADAM_B2 ** ADAM_STEP)
        d_ref[...] = -ADAM_LR * (m_hat / (jnp.sqrt(v_hat) + ADAM_EPS) + ADAM_WD * w_ref[...])
        mo_ref[...] = m2
        vo_ref[...] = v2

    spec = pl.BlockSpec((tr, C), lambda i: (i, 0))
    return pl.pallas_call(body, name=name, grid=(R // tr,), in_specs=[spec] * 4, out_specs=[spec] * 3,
                          out_shape=[jax.ShapeDtypeStruct((R, C), F32)] * 3, compiler_params=_cparams(("parallel",)))(w, g, m, v)


LANES = 1024


def _pack(arrs, rows):
    out, offs, r = [], [], 0
    for a in arrs:
        flat = a.reshape(-1)
        nr = -(-flat.shape[0] // LANES)
        out.append(jnp.pad(flat, (0, nr * LANES - flat.shape[0])))
        offs.append(r)
        r += nr
    assert r <= rows, (r, rows)
    out.append(jnp.zeros(((rows - r) * LANES,), F32))
    return jnp.concatenate(out).reshape(rows, LANES), offs


def _unpack(packed, offs, shapes):
    flat = packed.reshape(-1)
    return [flat[o * LANES:o * LANES + math.prod(s)].reshape(s) for o, s in zip(offs, shapes)]


WEIGHTS = ["w_ada", "b_ada", "w_in", "b_gate", "conv_a", "a_log", "dt_bias", "norm_a", "rel_bias", "w_branch_a",
           "w_branch_b", "w_o", "ln1_g", "ln1_b", "w_up", "conv_ffn", "b_conv_ffn", "w_down", "ln2_g", "ln2_b"]
BIG = ["w_in", "w_branch_a", "w_branch_b", "w_o", "w_up", "w_down"]
LATE = [n for n in BIG if n != "w_in"]
KEPT_SHARDED = {"w_up"}
COL_SHARDED = {"w_in", "w_up"}
SMALL_SHARDED = {"conv_a": 3 * A_W // N_CHIPS, "rel_bias": B_REL // N_CHIPS, "conv_ffn": 2 * D_FF // N_CHIPS}
SMALL = [n for n in WEIGHTS if n not in BIG and n != "w_ada"]


def _to_full(g4, name):
    if name in KEPT_SHARDED:
        return g4
    if name in COL_SHARDED:
        return jnp.transpose(g4, (1, 0, 2)).reshape(g4.shape[1], -1)
    return g4.reshape(-1, g4.shape[2])


def _to_shards(full, name):
    if name in KEPT_SHARDED:
        return full
    if name in COL_SHARDED:
        return jnp.transpose(full.reshape(full.shape[0], N_CHIPS, -1), (1, 0, 2))
    return full.reshape(N_CHIPS, -1, full.shape[1])


def kernel(x, c, w_ada, b_ada, w_in, b_gate, conv_a, a_log, dt_bias, norm_a, rel_bias, w_branch_a, w_branch_b, w_o, ln1_g, ln1_b, w_up, conv_ffn, b_conv_ffn, w_down, ln2_g, ln2_b, loss_target, m_w_ada, m_b_ada, m_w_in, m_b_gate, m_conv_a, m_a_log, m_dt_bias, m_norm_a, m_rel_bias, m_w_branch_a, m_w_branch_b, m_w_o, m_ln1_g, m_ln1_b, m_w_up, m_conv_ffn, m_b_conv_ffn, m_w_down, m_ln2_g, m_ln2_b, v_w_ada, v_b_ada, v_w_in, v_b_gate, v_conv_a, v_a_log, v_dt_bias, v_norm_a, v_rel_bias, v_w_branch_a, v_w_branch_b, v_w_o, v_ln1_g, v_ln1_b, v_w_up, v_conv_ffn, v_b_conv_ffn, v_w_down, v_ln2_g, v_ln2_b):
    args = dict(locals())
    wts = {n: args[n] for n in WEIGHTS}
    moms = {n: args["m_" + n] for n in WEIGHTS}
    vars_ = {n: args["v_" + n] for n in WEIGHTS}
    xi, yi, ci = _place()
    chip = 2 * xi + yi
    dev = 4 * xi + 2 * yi + ci
    ada_cols = w_ada.shape[2]

    sshapes = [wts[n].shape[1:] for n in SMALL_SHARDED]
    spack, soffs = _pack([wts[n][0] for n in SMALL_SHARDED], 16)
    first = allgather8(jnp.concatenate([jnp.pad(c, ((0, 7), (0, 0))), spack]), "gather_c_small_w").reshape(N_DEV, 24, LANES)
    c_all = first[:, 0]
    b_ada_sh = lax.dynamic_slice(b_ada, (0, chip * ada_cols), (1, ada_cols))
    mod_sh = ada_fwd(c_all, w_ada[0], b_ada_sh)
    mod_g = allgather8(mod_sh, "gather_mod").reshape(N_CHIPS, 2, N_DEV, ada_cols)[:, 0]
    mod = lax.dynamic_slice(mod_g, (0, dev, 0), (N_CHIPS, 1, ada_cols)).reshape(6, D_MODEL)

    (w_in_g4,) = chip_exchange([wts["w_in"][0].astype(BF16)], "gather_w_in", scatter=False)
    wd = split_w_in(_to_full(w_in_g4, "w_in"))
    late_shards = [wts[n][0].astype(BF16) for n in LATE]
    late_gather = exchange_start(late_shards, "gather_late_start", scatter=False, after=w_in_g4)
    mod = mod + late_gather["token"][0, 0]

    def late_weights(after):
        zones = exchange_wait(late_gather, "gather_late_wait", after)
        full = [_to_full(lax.dynamic_update_slice(z, s[None], (chip, 0, 0)), n) for n, z, s in zip(LATE, zones, late_shards)]
        return {n[2:]: f for n, f in zip(LATE, full)}

    sg = first[::2, 8:]
    sparts = [_unpack(sg[j], soffs, sshapes) for j in range(N_CHIPS)]
    sm = {n: wts[n] for n in SMALL if n not in SMALL_SHARDED and n != "b_ada"}
    for i, n in enumerate(SMALL_SHARDED):
        sm[n] = jnp.concatenate([sparts[j][i] for j in range(N_CHIPS)], axis=-1)

    early = {}

    def late_start(g):
        early["swap"] = swap_start([g[n] for n in LATE], "grad_swap_late_start", g[LATE[0]])
        return early["swap"]["token"]

    def late_finish(after):
        mine, theirs = swap_wait(early["swap"], "grad_swap_late_wait", after)
        early["sums"] = [_to_shards(pair_add(a, b, "grad_pair_" + n), n) for n, a, b in zip(LATE, mine, theirs)]
        early["scatter"] = exchange_start(early["sums"], "grad_scatter_start", scatter=True, after=theirs[0])
        return early["scatter"]["token"]

    def w_in_start(g):
        early["swap_in"] = swap_start([g], "grad_swap_w_in_start", g)
        return early["swap_in"]["token"]

    def w_in_finish(after):
        (mine,), (theirs,) = swap_wait(early["swap_in"], "grad_swap_w_in_wait", after)
        early["sum_in"] = _to_shards(pair_add(mine, theirs, "grad_pair_w_in"), "w_in")
        early["scatter_in"] = exchange_start([early["sum_in"]], "grad_scatter_w_in_start", scatter=True, after=theirs)
        return early["scatter_in"]["token"]

    hooks = dict(late_start=late_start, late_finish=late_finish, w_in_start=w_in_start, w_in_finish=w_in_finish)
    loss, dxpre2, dffn, fin, sv = forward_local(x[0], loss_target[0], mod, wd, sm, late_weights)
    grad_x, dmod, gw, gs = backward_local(x[0], mod, sm, dxpre2, dffn, fin, sv, hooks)

    gnames = [n for n in SMALL if n != "b_ada"]
    vec, voffs = _pack([dmod] + [gs[n] for n in gnames] + [loss], 56)
    gathered = allgather8(vec, "gather_small_g").reshape(N_DEV, 56, LANES)
    summed = sum_lead(gathered, "sum_small_g")
    full_shapes = [(6, D_MODEL)] + [gs[n].shape for n in gnames] + [(1, 1)]
    parts = _unpack(summed, voffs, full_shapes)
    grads = {"b_ada": parts[0].reshape(1, -1)}
    for n, p in zip(gnames, parts[1:-1]):
        if n in SMALL_SHARDED:
            p = lax.dynamic_slice_in_dim(p, chip * SMALL_SHARDED[n], SMALL_SHARDED[n], axis=1)
        grads[n] = p.reshape(wts[n].shape)
    loss_total = parts[-1].reshape(())
    dmod_all = gathered[:, 0:6, :].reshape(N_DEV, 6 * D_MODEL)
    grads["w_ada"] = ada_bwd(c_all, lax.dynamic_slice(dmod_all, (0, chip * ada_cols), (N_DEV, ada_cols)))[None]

    def own_slot(zone, sums):
        return lax.dynamic_update_slice(zone, lax.dynamic_slice_in_dim(sums, chip, 1, axis=0), (chip, 0, 0))

    zones = exchange_wait(early["scatter"], "grad_scatter_wait", summed)
    for n, z, s in zip(LATE, zones, early["sums"]):
        grads[n] = sum_lead(own_slot(z, s), "grad_sum_" + n)[None]

    delta, new_m, new_v = {}, {}, {}

    def update(n):
        d, m2, v2 = adamw(wts[n][0], grads[n][0], moms[n][0], vars_[n][0], "adamw_" + n)
        delta[n], new_m[n], new_v[n] = d[None], m2[None], v2[None]

    for n in ["w_ada"] + LATE:
        update(n)
    shapes = [wts[n].shape for n in SMALL]
    packs = [_pack([t[n] for n in SMALL], 32) for t in (wts, grads, moms, vars_)]
    outs = adamw(*[p[0] for p in packs], "adamw_small")
    for res, o in zip((delta, new_m, new_v), outs):
        for n, a in zip(SMALL, _unpack(o, packs[0][1], shapes)):
            res[n] = a
    (zone_in,) = exchange_wait(early["scatter_in"], "grad_scatter_w_in_wait", outs[0])
    grads["w_in"] = sum_lead(own_slot(zone_in, early["sum_in"]), "grad_sum_w_in")[None]
    update("w_in")
    return (loss_total, grad_x[None], *[grads[n] for n in WEIGHTS], *[delta[n] for n in WEIGHTS],
            *[new_m[n] for n in WEIGHTS], *[new_v[n] for n in WEIGHTS])
```

```python
import functools
import math

import jax
import jax.numpy as jnp
from jax import lax
from jax.experimental import pallas as pl
from jax.experimental.pallas import tpu as pltpu

F32 = jnp.float32
BF16 = jnp.bfloat16

D_MODEL = 1024
CHUNK = 64
A_HEADS = 8
A_DK = 128
A_W = A_HEADS * A_DK
A_CONV = 4
B_HEADS = 16
B_DH = 64
B_W = B_HEADS * B_DH
B_PREV = 8
B_BAND = (B_PREV + 1) * CHUNK
B_MAX_REL = 256
B_REL = CHUNK - 1 + B_MAX_REL + 1
D_FF = 2816
FFN_CONV = 3
IN_COLS = 4 * A_W + 2 * A_HEADS + 3 * B_W + 2 * D_MODEL
ALPHA = 2.0 ** 0.25
LN_EPS = 1e-5
RMS_EPS = 1e-6
L2_EPS = 1e-6
NEG_INF = -1e30
ADAM_LR, ADAM_B1, ADAM_B2, ADAM_EPS, ADAM_WD, ADAM_STEP = 0.001, 0.9, 0.999, 1e-08, 0.01, 10
N_CHIPS = 4
N_DEV = 8
VMEM_LIMIT = 56 * 1024 * 1024


def _cparams(sem=None):
    return pltpu.CompilerParams(dimension_semantics=sem, vmem_limit_bytes=VMEM_LIMIT)


_DIMS = {"nn": (((1,), (0,)), ((), ())), "nt": (((1,), (1,)), ((), ())), "tn": (((0,), (0,)), ((), ()))}


MM_TILE_CAP = 1536


MM_TOKEN_K_CAP = 2048


def _mm_tile(n, cap=MM_TILE_CAP):
    return max(t for t in range(128, min(n, cap) + 1, 128) if n % t == 0)


def mm(a, b, *, mode, out_dtype, name, acc_in=None, b_shards=False, out_shards=0):
    b_rows, b_cols = (b.shape[1], b.shape[0] * b.shape[2]) if b_shards else b.shape
    if mode == "nn":
        (M, K), (K2, N) = a.shape, (b_rows, b_cols)
    elif mode == "nt":
        (M, K), (N, K2) = a.shape, (b_rows, b_cols)
    else:
        (K, M), (K2, N) = a.shape, (b_rows, b_cols)
    assert K == K2, (a.shape, b.shape, mode)
    tm, tn, tk = _mm_tile(M), _mm_tile(N), _mm_tile(K, MM_TOKEN_K_CAP if mode == "tn" else MM_TILE_CAP)
    nk = K // tk

    def body(*refs):
        if acc_in is None:
            a_ref, b_ref, o_ref, acc_ref = refs
        else:
            a_ref, b_ref, c_ref, o_ref, acc_ref = refs
        k = pl.program_id(2)

        @pl.when(k == 0)
        def _():
            if acc_in is None:
                acc_ref[...] = jnp.zeros_like(acc_ref)
            else:
                acc_ref[...] = c_ref[...]

        acc_ref[...] += lax.dot_general(a_ref[...].astype(BF16), b_ref[...].astype(BF16), _DIMS[mode],
                                        preferred_element_type=F32)

        @pl.when(k == nk - 1)
        def _():
            o_ref[...] = acc_ref[...].astype(out_dtype)

    a_spec = pl.BlockSpec((tk, tm), lambda i, j, k: (k, i)) if mode == "tn" else pl.BlockSpec((tm, tk), lambda i, j, k: (i, k))
    if b_shards:
        assert (tk if mode == "nt" else tn) == b.shape[2] and mode != "tn", (b.shape, tn, tk, mode)
        b_spec = (pl.BlockSpec((None, tn, tk), lambda i, j, k: (k, j, 0)) if mode == "nt"
                  else pl.BlockSpec((None, tk, tn), lambda i, j, k: (j, k, 0)))
    else:
        b_spec = pl.BlockSpec((tn, tk), lambda i, j, k: (j, k)) if mode == "nt" else pl.BlockSpec((tk, tn), lambda i, j, k: (k, j))
    o_spec = pl.BlockSpec((tm, tn), lambda i, j, k: (i, j))
    out_shape = jax.ShapeDtypeStruct((M, N), out_dtype)
    if out_shards:
        assert N == out_shards * tn and acc_in is None, (N, tn, out_shards)
        o_spec = pl.BlockSpec((None, tm, tn), lambda i, j, k: (j, i, 0))
        out_shape = jax.ShapeDtypeStruct((out_shards, M, tn), out_dtype)
    ins, in_specs, aliases = [a, b], [a_spec, b_spec], {}
    if acc_in is not None:
        assert acc_in.shape == (M, N) and acc_in.dtype == F32 and out_dtype == F32
        ins.append(acc_in)
        in_specs.append(o_spec)
        aliases = {2: 0}
    return pl.pallas_call(
        body, name=name, grid=(M // tm, N // tn, nk), in_specs=in_specs, out_specs=o_spec,
        out_shape=out_shape, scratch_shapes=[pltpu.VMEM((tm, tn), F32)],
        input_output_aliases=aliases, compiler_params=_cparams(("parallel", "parallel", "arbitrary")),
    )(*ins)


def rowcall(body, *, name, S, ts, ins, outs, scratch=()):
    assert S % ts == 0 and ts % 16 == 0
    nsteps = S // ts
    in_specs, arrays = [], []
    for arr, kind in ins:
        arrays.append(arr)
        if kind == "row":
            in_specs.append(pl.BlockSpec((ts, arr.shape[1]), lambda i: (i, 0)))
        elif kind in ("prev", "next"):
            hr = 8 * (4 // arr.dtype.itemsize)
            per, last = ts // hr, S // hr - 1
            if kind == "prev":
                in_specs.append(pl.BlockSpec((hr, arr.shape[1]), lambda i, per=per: (jnp.maximum(i * per - 1, 0), 0)))
            else:
                in_specs.append(pl.BlockSpec((hr, arr.shape[1]), lambda i, per=per, last=last: (jnp.minimum((i + 1) * per, last), 0)))
        else:
            nd = arr.ndim
            in_specs.append(pl.BlockSpec(arr.shape, lambda i, nd=nd: (0,) * nd))
    out_specs, out_shapes, acc_idx = [], [], []
    for n, (shape, dtype, kind) in enumerate(outs):
        out_shapes.append(jax.ShapeDtypeStruct(shape, dtype))
        if kind == "row":
            out_specs.append(pl.BlockSpec((ts, shape[1]), lambda i: (i, 0)))
        else:
            nd = len(shape)
            out_specs.append(pl.BlockSpec(shape, lambda i, nd=nd: (0,) * nd))
            acc_idx.append(n)
    n_in = len(arrays)

    def wrapped(*refs):
        @pl.when(pl.program_id(0) == 0)
        def _():
            for n in acc_idx:
                refs[n_in + n][...] = jnp.zeros_like(refs[n_in + n])

        body(*refs)

    res = pl.pallas_call(
        wrapped, name=name, grid=(nsteps,), in_specs=in_specs, out_specs=out_specs, out_shape=out_shapes,
        scratch_shapes=list(scratch), compiler_params=_cparams(("arbitrary",) if acc_idx else ("parallel",)),
    )(*arrays)
    return res


def _halo_prev(ref):
    v = ref[...].astype(F32)
    return v[v.shape[0] - 8:]


def _halo_next(ref):
    return ref[...].astype(F32)[:8]


def _shift_down(cur, prev8, k):
    if k == 0:
        return cur
    rolled = pltpu.roll(cur, k, axis=0)
    fix = pltpu.roll(prev8, k, axis=0)
    row = lax.broadcasted_iota(jnp.int32, (8, 1), 0)
    top = jnp.where(row < k, fix, rolled[0:8])
    if cur.shape[0] == 8:
        return top
    return jnp.concatenate([top, rolled[8:]], axis=0)


def _shift_up(cur, next8, k):
    if k == 0:
        return cur
    n = cur.shape[0]
    rolled = pltpu.roll(cur, n - k, axis=0)
    fix = pltpu.roll(next8, 8 - k, axis=0)
    row = lax.broadcasted_iota(jnp.int32, (8, 1), 0)
    bot = jnp.where(row >= 8 - k, fix, rolled[n - 8:n])
    return jnp.concatenate([rolled[:n - 8], bot], axis=0)


def _sigmoid(x):
    return 1.0 / (1.0 + jnp.exp(-x))


def _silu(x):
    return x * _sigmoid(x)


def _silu_and_grad(x):
    s = _sigmoid(x)
    return x * s, s * (1.0 + x * (1.0 - s))


def _softplus(x):
    return jnp.maximum(x, 0.0) + jnp.log1p(jnp.exp(-jnp.abs(x)))


def _split2(x):
    hi = x.astype(BF16)
    return hi, (x - hi.astype(F32)).astype(BF16)


def _dot1(a, b, mode):
    return lax.dot_general(a.astype(BF16), b.astype(BF16), _DIMS[mode], preferred_element_type=F32)


def _dot3(a, b, mode):
    ah, al = _split2(a)
    bh, bl = _split2(b)
    d = lambda p, q: lax.dot_general(p, q, _DIMS[mode], preferred_element_type=F32)
    return d(ah, bh) + (d(ah, bl) + d(al, bh))


def ada_fwd(c_all, w_sh, b_sh):
    n = w_sh.shape[1]
    tn = 512

    def body(c_ref, w_ref, b_ref, o_ref):
        o_ref[...] = _dot1(_silu(c_ref[...]), w_ref[...], "nn") + b_ref[...]

    return pl.pallas_call(
        body, name="ada_fwd", grid=(n // tn,),
        in_specs=[pl.BlockSpec((N_DEV, D_MODEL), lambda j: (0, 0)), pl.BlockSpec((D_MODEL, tn), lambda j: (0, j)),
                  pl.BlockSpec((1, tn), lambda j: (0, j))],
        out_specs=pl.BlockSpec((N_DEV, tn), lambda j: (0, j)), out_shape=jax.ShapeDtypeStruct((N_DEV, n), F32),
        compiler_params=_cparams(("parallel",)),
    )(c_all, w_sh, b_sh)


def ada_bwd(c_all, dmod_sh):
    n = dmod_sh.shape[1]
    tn = 512

    def body(c_ref, d_ref, o_ref):
        o_ref[...] = _dot1(_silu(c_ref[...]), d_ref[...], "tn")

    return pl.pallas_call(
        body, name="ada_bwd", grid=(n // tn,),
        in_specs=[pl.BlockSpec((N_DEV, D_MODEL), lambda j: (0, 0)), pl.BlockSpec((N_DEV, tn), lambda j: (0, j))],
        out_specs=pl.BlockSpec((D_MODEL, tn), lambda j: (0, j)), out_shape=jax.ShapeDtypeStruct((D_MODEL, n), F32),
        compiler_params=_cparams(("parallel",)),
    )(c_all, dmod_sh)


SHIFT_T, SCALE_T, GATE_T, SHIFT_F, SCALE_F, GATE_F = range(6)


def modulate(x, mod, shift_row, scale_row, name):
    S = x.shape[0]

    def body(x_ref, m_ref, o_ref):
        m = m_ref[...]
        o_ref[...] = (x_ref[...] * (1.0 + m[scale_row:scale_row + 1]) + m[shift_row:shift_row + 1]).astype(BF16)

    return rowcall(body, name=name, S=S, ts=512, ins=[(x, "row"), (mod, "vec")], outs=[((S, D_MODEL), BF16, "row")])[0]


def _conv_fwd(cur, prev, w, width):
    y = cur * w[width - 1:width]
    for j in range(width - 1):
        y = y + _shift_down(cur, prev, width - 1 - j) * w[j:j + 1]
    return y


def _prep_a_core(cur, prev, w):
    return _silu_and_grad(_conv_fwd(cur, prev, w, A_CONV))


def prep_a_fwd(qkv_raw, ba, conv_a, a_log, dt_bias):
    S = qkv_raw.shape[0]

    def body(x_ref, xp_ref, ba_ref, w_ref, al_ref, dt_ref, q_ref, k_ref, v_ref, beta_ref, g_ref):
        first = (pl.program_id(0) > 0).astype(F32)
        y, _ = _prep_a_core(x_ref[...].astype(F32), _halo_prev(xp_ref) * first, w_ref[...])
        for h in range(A_HEADS):
            sl = slice(h * A_DK, (h + 1) * A_DK)
            qh = y[:, sl]
            kh = y[:, A_W + h * A_DK:A_W + (h + 1) * A_DK]
            q_ref[:, sl] = qh * (lax.rsqrt(jnp.sum(qh * qh, axis=-1, keepdims=True) + L2_EPS) * (A_DK ** -0.5))
            k_ref[:, sl] = kh * lax.rsqrt(jnp.sum(kh * kh, axis=-1, keepdims=True) + L2_EPS)
        v_ref[...] = y[:, 2 * A_W:3 * A_W]
        bav = ba_ref[...]
        beta_ref[...] = _sigmoid(bav[:, 0:A_HEADS])
        g_ref[...] = -jnp.exp(al_ref[...]) * _softplus(bav[:, A_HEADS:2 * A_HEADS] + dt_ref[...])

    return rowcall(
        body, name="prep_a_fwd", S=S, ts=256,
        ins=[(qkv_raw, "row"), (qkv_raw, "prev"), (ba, "row"), (conv_a, "vec"), (a_log, "vec"), (dt_bias, "vec")],
        outs=[((S, A_W), F32, "row")] * 3 + [((S, A_HEADS), F32, "row")] * 2)


HEAD_GROUP = 2
GROUP_ROWS = HEAD_GROUP * CHUNK
N_HEAD_GROUPS = A_HEADS // HEAD_GROUP
LOG_CHUNK = int(math.log2(CHUNK))


def _tri_masks():
    rb = lax.broadcasted_iota(jnp.int32, (GROUP_ROWS, GROUP_ROWS), 0)
    cb = lax.broadcasted_iota(jnp.int32, (GROUP_ROWS, GROUP_ROWS), 1)
    same = (rb >> LOG_CHUNK) == (cb >> LOG_CHUNK)
    return dict(causal=same & (rb >= cb), strict=same & (rb > cb), eye=rb == cb, upper=same & (cb >= rb),
                last=cb == (rb | (CHUNK - 1)), rb=rb, cb=cb)


def _col_to_row(colv, eye):
    return jnp.sum(jnp.where(eye, colv, 0.0), axis=0, keepdims=True)


def _row_to_col(rowv, eye):
    return jnp.sum(jnp.where(eye, rowv, 0.0), axis=1, keepdims=True)


def _tri_inv(a_list, mk):
    rb, cb = mk["rb"], mk["cb"]
    ts = [jnp.where(mk["eye"], 1.0, 0.0) - jnp.where((rb >> 1) == (cb >> 1), a, 0.0) for a in a_list]
    for lvl in range(1, LOG_CHUNK):
        rs, cs = rb >> lvl, cb >> lvl
        sel = ((rs & 1) == 1) & (cs == rs - 1)
        inner = [_dot3(t, jnp.where(sel, a, 0.0), "nn") for t, a in zip(ts, a_list)]
        ts = [t - _dot3(i, t, "nn") for i, t in zip(inner, ts)]
    return ts


def _stack_heads(ref, grp):
    return jnp.concatenate([ref[:, (grp * HEAD_GROUP + j) * A_DK:(grp * HEAD_GROUP + j + 1) * A_DK]
                            for j in range(HEAD_GROUP)], axis=0)


def _stack_cols(tile, grp):
    return jnp.concatenate([tile[:, grp * HEAD_GROUP + j:grp * HEAD_GROUP + j + 1] for j in range(HEAD_GROUP)], axis=0)


def _delta_local(q, k, v, beta, g, mk):
    causal, strict, eye = mk["causal"], mk["strict"], mk["eye"]
    g_row = _col_to_row(g, eye)
    gc = jnp.sum(jnp.where(causal, g_row, 0.0), axis=1, keepdims=True)
    gc_row = _col_to_row(gc, eye)
    decay = jnp.where(causal, jnp.exp(jnp.where(causal, gc - gc_row, 0.0)), 0.0)
    gam = jnp.exp(gc)
    kb = k * beta
    vb = v * beta
    y = kb * gam
    a = jnp.where(strict, _dot1(kb, k, "nt") * decay, 0.0)
    p = _dot1(q, k, "nt") * decay
    gl = jnp.sum(jnp.where(mk["last"], gc_row, 0.0), axis=1, keepdims=True)
    kd = k * jnp.exp(gl - gc)
    return dict(gc=gc, decay=decay, gam=gam, kb=kb, vb=vb, y=y, a=a, p=p, gl=gl, kd=kd)


def _head_rows(x, j):
    return x[j * CHUNK:(j + 1) * CHUNK]


def delta_fwd(q, k, v, beta, g):
    S = q.shape[0]
    n_chunks = S // CHUNK

    def body(q_ref, k_ref, v_ref, beta_ref, g_ref, o_ref, sprev_ref, t_ref, state_ref):
        @pl.when(pl.program_id(0) == 0)
        def _():
            state_ref[...] = jnp.zeros_like(state_ref)

        mk = _tri_masks()
        betav, gv = beta_ref[...], g_ref[...]
        groups = range(N_HEAD_GROUPS)
        q_all = [_stack_heads(q_ref, grp) for grp in groups]
        locs = [_delta_local(q_all[grp], _stack_heads(k_ref, grp), _stack_heads(v_ref, grp),
                             _stack_cols(betav, grp), _stack_cols(gv, grp), mk) for grp in groups]
        tinvs = _tri_inv([loc["a"] for loc in locs], mk)
        uws = [_dot3(tinvs[grp], jnp.concatenate([locs[grp]["vb"], locs[grp]["y"]], axis=1), "nn") for grp in groups]
        for grp in groups:
            loc, uw = locs[grp], uws[grp]
            t_ref[0, grp] = tinvs[grp]
            qg = q_all[grp] * loc["gam"]
            egl = jnp.exp(loc["gl"])
            vns, o_state = [], []
            for j in range(HEAD_GROUP):
                h = grp * HEAD_GROUP + j
                s0 = state_ref[h]
                sprev_ref[0, h] = s0
                uw_h = _head_rows(uw, j)
                vn = uw_h[:, :A_DK] - _dot1(uw_h[:, A_DK:], s0, "nn")
                vns.append(vn)
                o_state.append(_dot1(_head_rows(qg, j), s0, "nn"))
                state_ref[h] = s0 * egl[(j + 1) * CHUNK - 1:(j + 1) * CHUNK] + _dot1(_head_rows(loc["kd"], j), vn, "tn")
            o_local = _dot1(loc["p"], jnp.concatenate(vns, axis=0), "nn")
            for j in range(HEAD_GROUP):
                h = grp * HEAD_GROUP + j
                o_ref[:, h * A_DK:(h + 1) * A_DK] = o_state[j] + _head_rows(o_local, j)

    tile = pl.BlockSpec((CHUNK, A_W), lambda n: (n, 0))
    small = pl.BlockSpec((CHUNK, A_HEADS), lambda n: (n, 0))
    return pl.pallas_call(
        body, name="delta_fwd", grid=(n_chunks,), in_specs=[tile, tile, tile, small, small],
        out_specs=[tile, pl.BlockSpec((1, A_HEADS, A_DK, A_DK), lambda n: (n, 0, 0, 0)),
                   pl.BlockSpec((1, N_HEAD_GROUPS, GROUP_ROWS, GROUP_ROWS), lambda n: (n, 0, 0, 0))],
        out_shape=[jax.ShapeDtypeStruct((S, A_W), F32), jax.ShapeDtypeStruct((n_chunks, A_HEADS, A_DK, A_DK), F32),
                   jax.ShapeDtypeStruct((n_chunks, N_HEAD_GROUPS, GROUP_ROWS, GROUP_ROWS), F32)],
        scratch_shapes=[pltpu.VMEM((A_HEADS, A_DK, A_DK), F32)],
        compiler_params=_cparams(("arbitrary",)),
    )(q, k, v, beta, g)


def gate_a_fwd(o_pre, z, norm_w):
    S = o_pre.shape[0]

    def body(o_ref, z_ref, nw_ref, out_ref):
        nw = nw_ref[...]
        for h in range(A_HEADS):
            sl = slice(h * A_DK, (h + 1) * A_DK)
            oh = o_ref[:, sl]
            r = lax.rsqrt(jnp.mean(oh * oh, axis=-1, keepdims=True) + RMS_EPS)
            out_ref[:, sl] = (oh * r * nw * _silu(z_ref[:, sl].astype(F32))).astype(BF16)

    return rowcall(body, name="gate_a_fwd", S=S, ts=512, ins=[(o_pre, "row"), (z, "row"), (norm_w, "vec")],
                   outs=[((S, A_W), BF16, "row")])[0]


HEADS_PER_GROUP = 2
GROUP_W = HEADS_PER_GROUP * B_DH
N_GROUPS = B_HEADS // HEADS_PER_GROUP
PAD_ROWS = B_PREV * CHUNK


Q_TILE = 256
Q_CHUNKS = Q_TILE // CHUNK
KEY_WIN = (B_PREV + Q_CHUNKS) * CHUNK


def _band_probs(qh, kh, bias, valid):
    s = _dot1(qh, kh, "nt") * (B_DH ** -0.5) + bias
    s = jnp.where(valid, s, NEG_INF)
    e = jnp.exp(s - jnp.max(s, axis=-1, keepdims=True))
    return e * (1.0 / jnp.sum(e, axis=-1, keepdims=True))


def _attn_specs(S, tile_rows):
    assert PAD_ROWS % tile_rows == 0 and S % tile_rows == 0, (PAD_ROWS, S, tile_rows)
    n_cb = B_W // GROUP_W
    return [pl.BlockSpec((tile_rows, GROUP_W), lambda g, n: (n + PAD_ROWS // tile_rows, g)),
            pl.BlockSpec((PAD_ROWS + S, GROUP_W), lambda g, n: (0, n_cb + g)),
            pl.BlockSpec((PAD_ROWS + S, GROUP_W), lambda g, n: (0, 2 * n_cb + g)),
            pl.BlockSpec((HEADS_PER_GROUP, CHUNK, B_BAND), lambda g, n: (g, 0, 0))]


def _band_valid(first_chunk):
    return lax.broadcasted_iota(jnp.int32, (CHUNK, B_BAND), 1) >= PAD_ROWS - first_chunk * CHUNK


def _chunk_rows(x, qc, rows=CHUNK):
    return x[qc * CHUNK:qc * CHUNK + rows]


FWD_TILE = 512
FWD_CHUNKS = FWD_TILE // CHUNK
FWD_WIN = (B_PREV + FWD_CHUNKS) * CHUNK


def attn_fwd(qkv_pad, bias):
    S = qkv_pad.shape[0] - PAD_ROWS

    def body(q_ref, k_ref, v_ref, b_ref, o_ref):
        n = pl.program_id(1)
        start = pl.multiple_of(n * FWD_TILE, FWD_TILE)
        kwin = k_ref[pl.ds(start, FWD_WIN), :]
        vwin = v_ref[pl.ds(start, FWD_WIN), :]
        qv = q_ref[...]
        pairs = [(qc, hh) for qc in range(FWD_CHUNKS) for hh in range(HEADS_PER_GROUP)]
        sl = lambda hh: slice(hh * B_DH, (hh + 1) * B_DH)
        s = [_dot1(_chunk_rows(qv, qc)[:, sl(hh)], _chunk_rows(kwin, qc, B_BAND)[:, sl(hh)], "nt") for qc, hh in pairs]
        s = [jnp.where(_band_valid(n * FWD_CHUNKS + qc), x * (B_DH ** -0.5) + b_ref[hh], NEG_INF)
             for x, (qc, hh) in zip(s, pairs)]
        e = [jnp.exp(x - jnp.max(x, axis=-1, keepdims=True)) for x in s]
        p = [x * (1.0 / jnp.sum(x, axis=-1, keepdims=True)) for x in e]
        o = [_dot1(x, _chunk_rows(vwin, qc, B_BAND)[:, sl(hh)], "nn") for x, (qc, hh) in zip(p, pairs)]
        rows = [jnp.concatenate(o[qc * HEADS_PER_GROUP:(qc + 1) * HEADS_PER_GROUP], axis=1) for qc in range(FWD_CHUNKS)]
        o_ref[...] = jnp.concatenate(rows, axis=0).astype(BF16)

    return pl.pallas_call(
        body, name="attn_fwd", grid=(N_GROUPS, S // FWD_TILE), in_specs=_attn_specs(S, FWD_TILE),
        out_specs=pl.BlockSpec((FWD_TILE, GROUP_W), lambda g, n: (n, g)),
        out_shape=jax.ShapeDtypeStruct((S, B_W), BF16),
        compiler_params=_cparams(("parallel", "arbitrary")),
    )(qkv_pad, qkv_pad, qkv_pad, bias)


EXT = B_BAND + CHUNK


def bias_expand(rel_bias):
    def body(rev_ref, o_ref):
        rev = rev_ref[...]
        erev = jnp.concatenate([jnp.broadcast_to(rev[:, 0:1], (B_HEADS, EXT - B_REL)), rev], axis=1)
        for i in range(CHUNK):
            o_ref[i] = erev[:, CHUNK - i:CHUNK - i + B_BAND]

    return pl.pallas_call(
        body, name="bias_expand", in_specs=[WHOLE_VMEM], out_specs=WHOLE_VMEM,
        out_shape=jax.ShapeDtypeStruct((CHUNK, B_HEADS, B_BAND), F32),
    )(jnp.flip(rel_bias, axis=1))


def bias_reduce(dbias):
    def body(d_ref, o_ref):
        acc = jnp.zeros((B_HEADS, EXT), F32)
        for i in range(CHUNK):
            acc = acc + jnp.pad(d_ref[i], ((0, 0), (CHUNK - i, i)))
        tail = acc[:, EXT - B_REL:]
        clipped = jnp.sum(acc[:, :EXT - B_REL], axis=1, keepdims=True)
        lane = lax.broadcasted_iota(jnp.int32, (B_HEADS, B_REL), 1)
        o_ref[...] = jnp.where(lane == 0, tail + clipped, tail)

    rev = pl.pallas_call(body, name="bias_reduce", in_specs=[WHOLE_VMEM], out_specs=WHOLE_VMEM,
                         out_shape=jax.ShapeDtypeStruct((B_HEADS, B_REL), F32))(dbias)
    return jnp.flip(rev, axis=1)


def merge_fwd(gates_raw, b_gate, ya, yb):
    S = ya.shape[0]

    def body(g_ref, b_ref, ya_ref, yb_ref, o_ref):
        gt = _sigmoid(g_ref[...].astype(F32) + b_ref[...])
        o_ref[...] = (gt[:, :D_MODEL] * ya_ref[...].astype(F32) + gt[:, D_MODEL:] * yb_ref[...].astype(F32)).astype(BF16)

    return rowcall(body, name="merge_fwd", S=S, ts=512,
                   ins=[(gates_raw, "row"), (b_gate, "vec"), (ya, "row"), (yb, "row")],
                   outs=[((S, D_MODEL), BF16, "row")])[0]


def _ln_stats(xpre):
    mu = jnp.mean(xpre, axis=-1, keepdims=True)
    xc = xpre - mu
    rstd = lax.rsqrt(jnp.mean(xc * xc, axis=-1, keepdims=True) + LN_EPS)
    return xc * rstd, rstd


def ln1_fwd(x, mix, mod, ln_g, ln_b):
    S = x.shape[0]

    def body(x_ref, mix_ref, m_ref, g_ref, b_ref, xpre_ref, x1_ref, h2_ref):
        m = m_ref[...]
        xpre = ALPHA * x_ref[...] + m[GATE_T:GATE_T + 1] * mix_ref[...]
        xhat, _ = _ln_stats(xpre)
        x1 = xhat * g_ref[...] + b_ref[...]
        xpre_ref[...] = xpre
        x1_ref[...] = x1
        h2_ref[...] = (x1 * (1.0 + m[SCALE_F:SCALE_F + 1]) + m[SHIFT_F:SHIFT_F + 1]).astype(BF16)

    return rowcall(body, name="ln1_fwd", S=S, ts=512,
                   ins=[(x, "row"), (mix, "row"), (mod, "vec"), (ln_g, "vec"), (ln_b, "vec")],
                   outs=[((S, D_MODEL), F32, "row"), ((S, D_MODEL), F32, "row"), ((S, D_MODEL), BF16, "row")])


STRIP_FWD = (64, 256)
STRIP_BWD = (128, 128)


def ffn_act_fwd(up, conv_w, conv_b):
    S = up.shape[0]
    ts = 256
    STRIP_ROWS, STRIP_COLS = STRIP_FWD

    def body(u_ref, up_ref, w_ref, b_ref, o_ref, ubuf):
        ubuf[0:8] = _halo_prev(up_ref) * (pl.program_id(0) > 0).astype(F32)
        ubuf[8:8 + ts] = u_ref[...].astype(F32)

        def col_block(j, carry):
            gate = pl.ds(pl.multiple_of(j * STRIP_COLS, STRIP_COLS), STRIP_COLS)
            halves = [gate, pl.ds(pl.multiple_of(D_FF + j * STRIP_COLS, STRIP_COLS), STRIP_COLS)]
            w = [w_ref[:, c] for c in halves]
            bias = [b_ref[:, c] for c in halves]
            for r0 in range(0, ts, STRIP_ROWS):
                uc = []
                for h in range(2):
                    x = ubuf[r0:r0 + STRIP_ROWS + 8, halves[h]]
                    uc.append(bias[h] + sum(
                        w[h][t:t + 1] * (x if t == FFN_CONV - 1 else pltpu.roll(x, FFN_CONV - 1 - t, axis=0))[8:]
                        for t in range(FFN_CONV)))
                o_ref[r0:r0 + STRIP_ROWS, gate] = (_silu(uc[0]) * uc[1]).astype(BF16)
            return carry

        lax.fori_loop(0, D_FF // STRIP_COLS, col_block, 0)

    return rowcall(body, name="ffn_act_fwd", S=S, ts=ts,
                   ins=[(up, "row"), (up, "prev"), (conv_w, "vec"), (conv_b, "vec")],
                   outs=[((S, D_FF), BF16, "row")], scratch=[pltpu.VMEM((ts + 8, 2 * D_FF), F32)])[0]


def final_fwd_bwd(x1, ffn, target, mod, ln_g, ln_b):
    S = x1.shape[0]

    def body(x1_ref, f_ref, t_ref, m_ref, g_ref, b_ref, dxpre_ref, dffn_ref, loss_ref, dgate_ref, dg_ref, db_ref):
        gate = m_ref[...][GATE_F:GATE_F + 1]
        ffn_v = f_ref[...]
        xpre = ALPHA * x1_ref[...] + gate * ffn_v
        xhat, rstd = _ln_stats(xpre)
        err = xhat * g_ref[...] + b_ref[...] - t_ref[...]
        loss_ref[...] += 0.5 * jnp.sum(jnp.mean(err * err, axis=-1, keepdims=True), axis=0, keepdims=True)
        dy = err * (1.0 / D_MODEL)
        dg_ref[...] += jnp.sum(dy * xhat, axis=0, keepdims=True)
        db_ref[...] += jnp.sum(dy, axis=0, keepdims=True)
        dyg = dy * g_ref[...]
        dxpre = rstd * (dyg - jnp.mean(dyg, axis=-1, keepdims=True) - xhat * jnp.mean(dyg * xhat, axis=-1, keepdims=True))
        dxpre_ref[...] = dxpre
        dffn_ref[...] = (gate * dxpre).astype(BF16)
        dgate_ref[...] += jnp.sum(dxpre * ffn_v, axis=0, keepdims=True)

    vec = ((1, D_MODEL), F32, "acc")
    return rowcall(body, name="final_fwd_bwd", S=S, ts=512,
                   ins=[(x1, "row"), (ffn, "row"), (target, "row"), (mod, "vec"), (ln_g, "vec"), (ln_b, "vec")],
                   outs=[((S, D_MODEL), F32, "row"), ((S, D_MODEL), BF16, "row"), ((1, 1), F32, "acc"), vec, vec, vec])


def ffn_act_bwd(dact, up, conv_w, conv_b):
    S = up.shape[0]
    ts = 256
    STRIP_ROWS, STRIP_COLS = STRIP_BWD
    win_u, win_d = STRIP_ROWS + 16, STRIP_ROWS + 8

    def body(d_ref, dn_ref, u_ref, up_ref, un_ref, w_ref, b_ref, dup_ref, dw_ref, db_ref, ubuf, dbuf):
        i = pl.program_id(0)
        ubuf[0:8] = _halo_prev(up_ref) * (i > 0).astype(F32)
        ubuf[8:8 + ts] = u_ref[...].astype(F32)
        ubuf[8 + ts:16 + ts] = _halo_next(un_ref)
        dbuf[0:ts] = d_ref[...].astype(F32)
        dbuf[ts:ts + 8] = _halo_next(dn_ref) * (i < pl.num_programs(0) - 1).astype(F32)

        def col_block(j, carry):
            halves = [pl.ds(pl.multiple_of(j * STRIP_COLS, STRIP_COLS), STRIP_COLS),
                      pl.ds(pl.multiple_of(D_FF + j * STRIP_COLS, STRIP_COLS), STRIP_COLS)]
            w = [w_ref[:, c] for c in halves]
            bias = [b_ref[:, c] for c in halves]
            dw_acc = [[jnp.zeros((1, STRIP_COLS), F32) for _ in range(FFN_CONV)] for _ in halves]
            db_acc = [jnp.zeros((1, STRIP_COLS), F32) for _ in halves]
            for r0 in range(0, ts, STRIP_ROWS):
                shifted = [[x if k == 0 else pltpu.roll(x, k, axis=0) for k in range(FFN_CONV)]
                           for x in (ubuf[r0:r0 + win_u, c] for c in halves)]
                uc = [bias[h] + sum(w[h][t:t + 1] * shifted[h][FFN_CONV - 1 - t][8:8 + win_d] for t in range(FFN_CONV))
                      for h in range(2)]
                dact_w = dbuf[r0:r0 + win_d, halves[0]]
                sg, dsg = _silu_and_grad(uc[0])
                duc = [dact_w * uc[1] * dsg, dact_w * sg]
                for h in range(2):
                    dup = duc[h] * w[h][FFN_CONV - 1:FFN_CONV]
                    for t in range(FFN_CONV - 1):
                        dup = dup + pltpu.roll(duc[h], win_d - (FFN_CONV - 1 - t), axis=0) * w[h][t:t + 1]
                    dup_ref[r0:r0 + STRIP_ROWS, halves[h]] = dup[:STRIP_ROWS].astype(BF16)
                    mine = duc[h][:STRIP_ROWS]
                    db_acc[h] = db_acc[h] + jnp.sum(mine, axis=0, keepdims=True)
                    for t in range(FFN_CONV):
                        dw_acc[h][t] = dw_acc[h][t] + jnp.sum(
                            mine * shifted[h][FFN_CONV - 1 - t][8:8 + STRIP_ROWS], axis=0, keepdims=True)
            for h in range(2):
                dw_ref[:, halves[h]] += jnp.concatenate(dw_acc[h], axis=0)
                db_ref[:, halves[h]] += db_acc[h]
            return carry

        lax.fori_loop(0, D_FF // STRIP_COLS, col_block, 0)

    return rowcall(body, name="ffn_act_bwd", S=S, ts=ts,
                   ins=[(dact, "row"), (dact, "next"), (up, "row"), (up, "prev"), (up, "next"), (conv_w, "vec"), (conv_b, "vec")],
                   outs=[((S, 2 * D_FF), BF16, "row"), ((FFN_CONV, 2 * D_FF), F32, "acc"), ((1, 2 * D_FF), F32, "acc")],
                   scratch=[pltpu.VMEM((ts + 16, 2 * D_FF), F32), pltpu.VMEM((ts + 8, D_FF), F32)])


def ln1_bwd(dxpre2, dh2, xpre1, mix, mod, ln_g, ln_b):
    S = xpre1.shape[0]

    def body(d2_ref, dh_ref, xp_ref, mix_ref, m_ref, g_ref, b_ref, dxpre_ref, dmix_ref,
             dscale_ref, dshift_ref, dgate_ref, dg_ref, db_ref):
        m = m_ref[...]
        xhat, rstd = _ln_stats(xp_ref[...])
        x1 = xhat * g_ref[...] + b_ref[...]
        dh = dh_ref[...]
        dx1 = ALPHA * d2_ref[...] + dh * (1.0 + m[SCALE_F:SCALE_F + 1])
        dscale_ref[...] += jnp.sum(dh * x1, axis=0, keepdims=True)
        dshift_ref[...] += jnp.sum(dh, axis=0, keepdims=True)
        dg_ref[...] += jnp.sum(dx1 * xhat, axis=0, keepdims=True)
        db_ref[...] += jnp.sum(dx1, axis=0, keepdims=True)
        dyg = dx1 * g_ref[...]
        dxpre = rstd * (dyg - jnp.mean(dyg, axis=-1, keepdims=True) - xhat * jnp.mean(dyg * xhat, axis=-1, keepdims=True))
        dxpre_ref[...] = dxpre
        dmix_ref[...] = (m[GATE_T:GATE_T + 1] * dxpre).astype(BF16)
        dgate_ref[...] += jnp.sum(dxpre * mix_ref[...], axis=0, keepdims=True)

    vec = ((1, D_MODEL), F32, "acc")
    return rowcall(body, name="ln1_bwd", S=S, ts=512,
                   ins=[(dxpre2, "row"), (dh2, "row"), (xpre1, "row"), (mix, "row"), (mod, "vec"), (ln_g, "vec"), (ln_b, "vec")],
                   outs=[((S, D_MODEL), F32, "row"), ((S, D_MODEL), BF16, "row"), vec, vec, vec, vec, vec])


def merge_bwd(dmerged, gates_raw, b_gate, ya, yb):
    S = ya.shape[0]

    def body(d_ref, g_ref, b_ref, ya_ref, yb_ref, dya_ref, dyb_ref, dg_ref, dbg_ref):
        gt = _sigmoid(g_ref[...].astype(F32) + b_ref[...])
        d = d_ref[...].astype(F32)
        ga, gb = gt[:, :D_MODEL], gt[:, D_MODEL:]
        dya_ref[...] = (d * ga).astype(BF16)
        dyb_ref[...] = (d * gb).astype(BF16)
        dgr = jnp.concatenate([d * ya_ref[...].astype(F32) * ga * (1.0 - ga),
                               d * yb_ref[...].astype(F32) * gb * (1.0 - gb)], axis=1)
        dg_ref[...] = dgr.astype(BF16)
        dbg_ref[...] += jnp.sum(dgr, axis=0, keepdims=True)

    return rowcall(body, name="merge_bwd", S=S, ts=512,
                   ins=[(dmerged, "row"), (gates_raw, "row"), (b_gate, "vec"), (ya, "row"), (yb, "row")],
                   outs=[((S, D_MODEL), BF16, "row"), ((S, D_MODEL), BF16, "row"), ((S, 2 * D_MODEL), BF16, "row"),
                         ((1, 2 * D_MODEL), F32, "acc")])


def attn_bwd(qkv_pad, bias, do_b):
    S = qkv_pad.shape[0] - PAD_ROWS

    def body(q_ref, k_ref, v_ref, bias_ref, do_ref, dq_ref, dk_ref, dv_ref, db_ref, b_ref):
        n = pl.program_id(1)

        @pl.when(n == 0)
        def _():
            dk_ref[...] = jnp.zeros_like(dk_ref)
            dv_ref[...] = jnp.zeros_like(dv_ref)
            db_ref[...] = jnp.zeros_like(db_ref)
            b_ref[...] = jnp.full(b_ref.shape, NEG_INF, F32)
            for hh in range(HEADS_PER_GROUP):
                for qc in range(Q_CHUNKS):
                    b_ref[hh, qc * CHUNK:(qc + 1) * CHUNK, qc * CHUNK:qc * CHUNK + B_BAND] = bias_ref[hh]

        start = pl.multiple_of(n * Q_TILE, Q_TILE)
        kwin = k_ref[pl.ds(start, KEY_WIN), :]
        vwin = v_ref[pl.ds(start, KEY_WIN), :]
        qv, dov = q_ref[...], do_ref[...]
        valid = lax.broadcasted_iota(jnp.int32, (Q_TILE, KEY_WIN), 1) >= PAD_ROWS - n * Q_TILE
        dqs, dks, dvs = [], [], []
        for hh in range(HEADS_PER_GROUP):
            sl = slice(hh * B_DH, (hh + 1) * B_DH)
            p = _band_probs(qv[:, sl], kwin[:, sl], b_ref[hh], valid)
            dp = _dot1(dov[:, sl], vwin[:, sl], "nt")
            ds = p * (dp - jnp.sum(dp * p, axis=-1, keepdims=True))
            dbh = ds[0:CHUNK, 0:B_BAND]
            for qc in range(1, Q_CHUNKS):
                dbh = dbh + ds[qc * CHUNK:(qc + 1) * CHUNK, qc * CHUNK:qc * CHUNK + B_BAND]
            db_ref[hh] += dbh
            dsq = ds * (B_DH ** -0.5)
            dqs.append(_dot1(dsq, kwin[:, sl], "nn"))
            dks.append(_dot1(dsq, qv[:, sl], "tn"))
            dvs.append(_dot1(p, dov[:, sl], "tn"))
        dq_ref[...] = jnp.concatenate(dqs, axis=1).astype(BF16)
        dk_ref[pl.ds(start, KEY_WIN), :] += jnp.concatenate(dks, axis=1)
        dv_ref[pl.ds(start, KEY_WIN), :] += jnp.concatenate(dvs, axis=1)

    col = pl.BlockSpec((PAD_ROWS + S, GROUP_W), lambda g, n: (0, g))
    tile = pl.BlockSpec((Q_TILE, GROUP_W), lambda g, n: (n, g))
    return pl.pallas_call(
        body, name="attn_bwd", grid=(N_GROUPS, S // Q_TILE), in_specs=_attn_specs(S, Q_TILE) + [tile],
        out_specs=[tile, col, col, pl.BlockSpec((HEADS_PER_GROUP, CHUNK, B_BAND), lambda g, n: (g, 0, 0))],
        out_shape=[jax.ShapeDtypeStruct((S, B_W), BF16), jax.ShapeDtypeStruct((PAD_ROWS + S, B_W), F32),
                   jax.ShapeDtypeStruct((PAD_ROWS + S, B_W), F32), jax.ShapeDtypeStruct((B_HEADS, CHUNK, B_BAND), F32)],
        scratch_shapes=[pltpu.VMEM((HEADS_PER_GROUP, Q_TILE, KEY_WIN), F32)],
        compiler_params=_cparams(("parallel", "arbitrary")),
    )(qkv_pad, qkv_pad, qkv_pad, bias, do_b)


def gate_a_bwd(do_a, o_pre, z, norm_w):
    S = o_pre.shape[0]

    def body(d_ref, o_ref, z_ref, nw_ref, dop_ref, dz_ref, dnw_ref):
        nw = nw_ref[...]
        acc = jnp.zeros((1, A_DK), F32)
        for h in range(A_HEADS):
            sl = slice(h * A_DK, (h + 1) * A_DK)
            oh, zh, dh = o_ref[:, sl], z_ref[:, sl].astype(F32), d_ref[:, sl].astype(F32)
            r = lax.rsqrt(jnp.mean(oh * oh, axis=-1, keepdims=True) + RMS_EPS)
            sz, dsz = _silu_and_grad(zh)
            dz_ref[:, sl] = (dh * oh * r * nw * dsz).astype(BF16)
            acc = acc + jnp.sum(dh * oh * r * sz, axis=0, keepdims=True)
            t = dh * nw * sz
            dop_ref[:, sl] = r * t - oh * (r * r * r) * jnp.mean(t * oh, axis=-1, keepdims=True)
        dnw_ref[...] += acc

    return rowcall(body, name="gate_a_bwd", S=S, ts=512,
                   ins=[(do_a, "row"), (o_pre, "row"), (z, "row"), (norm_w, "vec")],
                   outs=[((S, A_W), F32, "row"), ((S, A_W), BF16, "row"), ((1, A_DK), F32, "acc")])


def delta_bwd(q, k, v, beta, g, sprev, tinv, do):
    S = q.shape[0]
    n_chunks = S // CHUNK

    def body(q_ref, k_ref, v_ref, beta_ref, g_ref, sprev_ref, t_ref, do_ref,
             dq_ref, dk_ref, dv_ref, dbeta_ref, dg_ref, dstate_ref):
        @pl.when(pl.program_id(0) == 0)
        def _():
            dstate_ref[...] = jnp.zeros_like(dstate_ref)

        mk = _tri_masks()
        causal, strict, eye = mk["causal"], mk["strict"], mk["eye"]
        blk_end = (lax.broadcasted_iota(jnp.int32, (GROUP_ROWS, 1), 0) & (CHUNK - 1)) == CHUNK - 1
        lane = lax.broadcasted_iota(jnp.int32, (CHUNK, A_HEADS), 1)
        betav, gv = beta_ref[...], g_ref[...]
        dbeta_t = jnp.zeros((CHUNK, A_HEADS), F32)
        dg_t = jnp.zeros((CHUNK, A_HEADS), F32)
        groups, heads = range(N_HEAD_GROUPS), range(HEAD_GROUP)
        st = [dict() for _ in groups]

        def local_part(grp, s):
            s["qs"], s["ks"], s["vs"] = _stack_heads(q_ref, grp), _stack_heads(k_ref, grp), _stack_heads(v_ref, grp)
            s["dos"] = _stack_heads(do_ref, grp)
            s["bs"] = _stack_cols(betav, grp)
            s["loc"] = loc = _delta_local(s["qs"], s["ks"], s["vs"], s["bs"], _stack_cols(gv, grp), mk)
            s["tinv"] = t_ref[0, grp]
            s["rhs"] = jnp.concatenate([loc["vb"], loc["y"]], axis=1)
            s["uw"] = _dot3(s["tinv"], s["rhs"], "nn")

        def state_part(grp, s):
            loc, uw, dos, qs = s["loc"], s["uw"], s["dos"], s["qs"]
            gam, kd, gl, gc = loc["gam"], loc["kd"], loc["gl"], loc["gc"]
            qg = qs * gam
            egl = jnp.exp(gl)
            hid = [grp * HEAD_GROUP + j for j in heads]
            s0 = [sprev_ref[0, h] for h in hid]
            ds1 = [dstate_ref[h] for h in hid]
            w = [_head_rows(uw, j)[:, A_DK:] for j in heads]
            vn = [_head_rows(uw, j)[:, :A_DK] - _dot1(w[j], s0[j], "nn") for j in heads]
            vns = jnp.concatenate(vn, axis=0)
            dvn_local = _dot1(loc["p"], dos, "tn")
            dvn = [_head_rows(dvn_local, j) + _dot1(_head_rows(kd, j), ds1[j], "nn") for j in heads]
            dvns = jnp.concatenate(dvn, axis=0)
            s["dp"] = jnp.where(causal, _dot1(dos, vns, "nt"), 0.0)
            dqg = jnp.concatenate([_dot1(_head_rows(dos, j), s0[j], "nt") for j in heads], axis=0)
            s["dq"] = dqg * gam
            dgc = jnp.sum(dqg * qg, axis=-1, keepdims=True)
            for j in heads:
                dstate_ref[hid[j]] = (_dot1(_head_rows(qg, j), _head_rows(dos, j), "tn")
                                      + egl[(j + 1) * CHUNK - 1:(j + 1) * CHUNK] * ds1[j] - _dot1(w[j], dvn[j], "tn"))
            dkd = jnp.concatenate([_dot1(vn[j], ds1[j], "nt") for j in heads], axis=0)
            s["dk"] = dkd * jnp.exp(gl - gc)
            t1 = jnp.sum(dkd * kd, axis=-1, keepdims=True)
            dgl = jnp.concatenate(
                [jnp.broadcast_to(jnp.sum(_head_rows(t1, j), axis=0, keepdims=True)
                                  + jnp.sum(jnp.sum(ds1[j] * s0[j], axis=-1, keepdims=True), axis=0, keepdims=True)
                                  * egl[(j + 1) * CHUNK - 1:(j + 1) * CHUNK], (CHUNK, 1)) for j in heads], axis=0)
            s["dgc"] = dgc - t1 + jnp.where(blk_end, dgl, 0.0)
            s["duw"] = jnp.concatenate(
                [dvns, jnp.concatenate([-_dot1(dvn[j], s0[j], "nt") for j in heads], axis=0)], axis=1)

        def solve_part(grp, s):
            s["dvby"] = _dot3(s["tinv"], s["duw"], "tn")
            s["dt"] = _dot3(s["duw"], s["rhs"], "nt")

        def inverse_part_a(grp, s):
            s["tdt"] = _dot3(s["tinv"], s["dt"], "tn")

        def inverse_part_b(grp, s):
            s["da"] = jnp.where(strict, -_dot3(s["tdt"], s["tinv"], "nt"), 0.0)

        def finish(grp, s):
            loc, qs, ks, vs, bs, da, dp, dvby = s["loc"], s["qs"], s["ks"], s["vs"], s["bs"], s["da"], s["dp"], s["dvby"]
            gam, decay = loc["gam"], loc["decay"]
            dm = da * decay
            dn = dp * decay
            e = da * loc["a"] + dp * loc["p"]
            dgc = s["dgc"] + jnp.sum(e, axis=1, keepdims=True) - _row_to_col(jnp.sum(e, axis=0, keepdims=True), eye)
            dy = dvby[:, A_DK:]
            dvb = dvby[:, :A_DK]
            dkb = _dot1(dm, ks, "nn") + dy * gam
            dk = s["dk"] + _dot1(dm, loc["kb"], "tn") + _dot1(dn, qs, "tn") + dkb * bs
            dq = s["dq"] + _dot1(dn, ks, "nn")
            dgc = dgc + jnp.sum(dy * loc["y"], axis=-1, keepdims=True)
            dbeta = jnp.sum(dkb * ks, axis=-1, keepdims=True) + jnp.sum(dvb * vs, axis=-1, keepdims=True)
            dv = dvb * bs
            dgs = jnp.sum(jnp.where(mk["upper"], _col_to_row(dgc, eye), 0.0), axis=1, keepdims=True)
            for j in heads:
                h = grp * HEAD_GROUP + j
                sl = slice(h * A_DK, (h + 1) * A_DK)
                dq_ref[:, sl] = _head_rows(dq, j)
                dk_ref[:, sl] = _head_rows(dk, j)
                dv_ref[:, sl] = _head_rows(dv, j)
            s["dbeta"], s["dgs"] = dbeta, dgs

        for stage in (local_part, state_part, solve_part, inverse_part_a, inverse_part_b, finish):
            for grp in groups:
                stage(grp, st[grp])
        for grp in groups:
            for j in heads:
                h = grp * HEAD_GROUP + j
                dbeta_t = dbeta_t + jnp.where(lane == h, _head_rows(st[grp]["dbeta"], j), 0.0)
                dg_t = dg_t + jnp.where(lane == h, _head_rows(st[grp]["dgs"], j), 0.0)
        dbeta_ref[...] = dbeta_t
        dg_ref[...] = dg_t

    rev = lambda n: (n_chunks - 1 - n, 0)
    rev4 = lambda n: (n_chunks - 1 - n, 0, 0, 0)
    tile = pl.BlockSpec((CHUNK, A_W), rev)
    small = pl.BlockSpec((CHUNK, A_HEADS), rev)
    return pl.pallas_call(
        body, name="delta_bwd", grid=(n_chunks,),
        in_specs=[tile, tile, tile, small, small, pl.BlockSpec((1, A_HEADS, A_DK, A_DK), rev4),
                  pl.BlockSpec((1, N_HEAD_GROUPS, GROUP_ROWS, GROUP_ROWS), rev4), tile],
        out_specs=[tile, tile, tile, small, small],
        out_shape=[jax.ShapeDtypeStruct((S, A_W), F32)] * 3 + [jax.ShapeDtypeStruct((S, A_HEADS), F32)] * 2,
        scratch_shapes=[pltpu.VMEM((A_HEADS, A_DK, A_DK), F32)],
        compiler_params=_cparams(("arbitrary",)),
    )(q, k, v, beta, g, sprev, tinv, do)


def _prep_a_dpre(raw, raw_prev, w, dq, dk, dv):
    y, dy_dpre = _prep_a_core(raw, raw_prev, w)
    parts = []
    for h in range(A_HEADS):
        yq = y[:, h * A_DK:(h + 1) * A_DK]
        dqh = dq[:, h * A_DK:(h + 1) * A_DK]
        rq = lax.rsqrt(jnp.sum(yq * yq, axis=-1, keepdims=True) + L2_EPS)
        parts.append((A_DK ** -0.5) * (rq * dqh - yq * (rq * rq * rq) * jnp.sum(dqh * yq, axis=-1, keepdims=True)))
    for h in range(A_HEADS):
        yk = y[:, A_W + h * A_DK:A_W + (h + 1) * A_DK]
        dkh = dk[:, h * A_DK:(h + 1) * A_DK]
        rk = lax.rsqrt(jnp.sum(yk * yk, axis=-1, keepdims=True) + L2_EPS)
        parts.append(rk * dkh - yk * (rk * rk * rk) * jnp.sum(dkh * yk, axis=-1, keepdims=True))
    parts.append(dv)
    return jnp.concatenate(parts, axis=1) * dy_dpre


def prep_a_bwd(qkv_raw, ba, conv_a, a_log, dt_bias, dq, dk, dv, dbeta, dg):
    S = qkv_raw.shape[0]
    ts = 256

    def body(x_ref, xp_ref, xn_ref, ba_ref, w_ref, al_ref, dt_ref, dq_ref, dqn_ref, dk_ref, dkn_ref, dv_ref, dvn_ref,
             dbeta_ref, dg_ref, draw_ref, dba_ref, dw_ref, dal_ref, ddt_ref):
        i = pl.program_id(0)
        first = (i > 0).astype(F32)
        last = (i < pl.num_programs(0) - 1).astype(F32)
        w = w_ref[...]
        cur, prev = x_ref[...].astype(F32), _halo_prev(xp_ref) * first
        dpre = _prep_a_dpre(cur, prev, w, dq_ref[...], dk_ref[...], dv_ref[...])
        dpre_n = _prep_a_dpre(_halo_next(xn_ref), cur[ts - 8:ts], w, _halo_next(dqn_ref), _halo_next(dkn_ref),
                              _halo_next(dvn_ref)) * last
        for j in range(A_CONV):
            dw_ref[j:j + 1, :] += jnp.sum(dpre * _shift_down(cur, prev, A_CONV - 1 - j), axis=0, keepdims=True)
        draw = dpre * w[A_CONV - 1:A_CONV]
        for j in range(A_CONV - 1):
            draw = draw + _shift_up(dpre, dpre_n, A_CONV - 1 - j) * w[j:j + 1]
        draw_ref[...] = draw.astype(BF16)
        bav = ba_ref[...]
        beta = _sigmoid(bav[:, 0:A_HEADS])
        xa = bav[:, A_HEADS:2 * A_HEADS] + dt_ref[...]
        nexp = -jnp.exp(al_ref[...])
        dgv = dg_ref[...]
        da = dgv * nexp * _sigmoid(xa)
        dba_ref[:, 0:A_HEADS] = dbeta_ref[...] * beta * (1.0 - beta)
        dba_ref[:, A_HEADS:2 * A_HEADS] = da
        dal_ref[...] += jnp.sum(dgv * nexp * _softplus(xa), axis=0, keepdims=True)
        ddt_ref[...] += jnp.sum(da, axis=0, keepdims=True)

    return rowcall(
        body, name="prep_a_bwd", S=S, ts=ts,
        ins=[(qkv_raw, "row"), (qkv_raw, "prev"), (qkv_raw, "next"), (ba, "row"), (conv_a, "vec"), (a_log, "vec"),
             (dt_bias, "vec"), (dq, "row"), (dq, "next"), (dk, "row"), (dk, "next"), (dv, "row"), (dv, "next"),
             (dbeta, "row"), (dg, "row")],
        outs=[((S, 3 * A_W), BF16, "row"), ((S, 2 * A_HEADS), F32, "row"), ((A_CONV, 3 * A_W), F32, "acc"),
              ((1, A_HEADS), F32, "acc"), ((1, A_HEADS), F32, "acc")])


def grad_x_final(dh1, x, dxpre1, mod):
    S = x.shape[0]

    def body(dh_ref, x_ref, dx_ref, m_ref, gx_ref, dscale_ref, dshift_ref):
        dh = dh_ref[...]
        gx_ref[...] = ALPHA * dx_ref[...] + dh * (1.0 + m_ref[...][SCALE_T:SCALE_T + 1])
        dscale_ref[...] += jnp.sum(dh * x_ref[...], axis=0, keepdims=True)
        dshift_ref[...] += jnp.sum(dh, axis=0, keepdims=True)

    vec = ((1, D_MODEL), F32, "acc")
    return rowcall(body, name="grad_x_final", S=S, ts=512, ins=[(dh1, "row"), (x, "row"), (dxpre1, "row"), (mod, "vec")],
                   outs=[((S, D_MODEL), F32, "row"), vec, vec])


_C_QKV, _C_Z, _C_BA, _C_QKVB, _C_G = 0, 3 * A_W, 4 * A_W, 4 * A_W + 2 * A_HEADS, 4 * A_W + 2 * A_HEADS + 3 * B_W
BA_PAD = 128


def split_w_in(w_in):
    ba = jnp.pad(w_in[:, _C_BA:_C_QKVB], ((0, 0), (0, BA_PAD - 2 * A_HEADS)))
    return dict(qkv=w_in[:, _C_QKV:_C_Z], z=w_in[:, _C_Z:_C_BA], ba=ba, qkvb=w_in[:, _C_QKVB:_C_G], g=w_in[:, _C_G:])


def join_w_in(p):
    return jnp.concatenate([p["qkv"], p["z"], p["ba"][:, :2 * A_HEADS], p["qkvb"], p["g"]], axis=1)


def forward_local(x, target, mod, w, sm, late_weights=None):
    h1 = modulate(x, mod, SHIFT_T, SCALE_T, "mod_t")
    qkv_raw = mm(h1, w["qkv"], mode="nn", out_dtype=BF16, name="proj_qkv")
    z = mm(h1, w["z"], mode="nn", out_dtype=BF16, name="proj_z")
    ba = mm(h1, w["ba"], mode="nn", out_dtype=F32, name="proj_ba")
    qkvb = mm(h1, w["qkvb"], mode="nn", out_dtype=BF16, name="proj_qkvb")
    gates_raw = mm(h1, w["g"], mode="nn", out_dtype=BF16, name="proj_g")
    q, k, v, beta, g = prep_a_fwd(qkv_raw, ba, sm["conv_a"], sm["a_log"], sm["dt_bias"])
    o_pre, sprev, tinv = delta_fwd(q, k, v, beta, g)
    o_a = gate_a_fwd(o_pre, z, sm["norm_a"])
    qkv_pad = jnp.pad(qkvb, ((PAD_ROWS, 0), (0, 0)))
    bias = jnp.transpose(bias_expand(sm["rel_bias"]), (1, 0, 2))
    o_b = attn_fwd(qkv_pad, bias)
    if late_weights is not None:
        w = dict(w, **late_weights(o_b))
    ya = mm(o_a, w["branch_a"], mode="nn", out_dtype=BF16, name="branch_a")
    yb = mm(o_b, w["branch_b"], mode="nn", out_dtype=BF16, name="branch_b")
    merged = merge_fwd(gates_raw, sm["b_gate"], ya, yb)
    mix = mm(merged, w["o"], mode="nn", out_dtype=F32, name="mix")
    xpre1, x1, h2 = ln1_fwd(x, mix, mod, sm["ln1_g"], sm["ln1_b"])
    up = mm(h2, w["up"], mode="nn", out_dtype=BF16, name="ffn_up", b_shards=True)
    act = ffn_act_fwd(up, sm["conv_ffn"], sm["b_conv_ffn"])
    ffn = mm(act, w["down"], mode="nn", out_dtype=F32, name="ffn_down")
    dxpre2, dffn, loss, dgate_f, dln2_g, dln2_b = final_fwd_bwd(x1, ffn, target, mod, sm["ln2_g"], sm["ln2_b"])
    saved = dict(h1=h1, qkv_raw=qkv_raw, z=z, ba=ba, gates_raw=gates_raw, q=q, k=k, v=v, beta=beta, g=g,
                 o_pre=o_pre, sprev=sprev, tinv=tinv, o_a=o_a, qkv_pad=qkv_pad, bias=bias, o_b=o_b, ya=ya, yb=yb,
                 merged=merged, mix=mix, xpre1=xpre1, x1=x1, h2=h2, up=up, act=act, ffn=ffn, w=w)
    return loss, dxpre2, dffn, dict(gate_f=dgate_f, ln2_g=dln2_g, ln2_b=dln2_b), saved


def backward_local(x, mod, sm, dxpre2, dffn, fin, sv, hooks=None):
    w = sv["w"]
    dact = mm(dffn, w["down"], mode="nt", out_dtype=BF16, name="d_act")
    gw_down = mm(sv["act"], dffn, mode="tn", out_dtype=BF16, name="gw_down")
    dup, dconv_ffn, db_conv_ffn = ffn_act_bwd(dact, sv["up"], sm["conv_ffn"], sm["b_conv_ffn"])
    dh2 = mm(dup, w["up"], mode="nt", out_dtype=F32, name="d_h2", b_shards=True)
    gw_up = mm(sv["h2"], dup, mode="tn", out_dtype=BF16, name="gw_up", out_shards=N_CHIPS)
    dxpre1, dmix, dsc_f, dsh_f, dgate_t, dln1_g, dln1_b = ln1_bwd(
        dxpre2, dh2, sv["xpre1"], sv["mix"], mod, sm["ln1_g"], sm["ln1_b"])
    dmerged = mm(dmix, w["o"], mode="nt", out_dtype=BF16, name="d_merged")
    gw_o = mm(sv["merged"], dmix, mode="tn", out_dtype=BF16, name="gw_o")
    dya, dyb, dgates, db_gate = merge_bwd(dmerged, sv["gates_raw"], sm["b_gate"], sv["ya"], sv["yb"])
    do_a = mm(dya, w["branch_a"], mode="nt", out_dtype=BF16, name="d_oa")
    gw_branch_a = mm(sv["o_a"], dya, mode="tn", out_dtype=BF16, name="gw_branch_a")
    do_b = mm(dyb, w["branch_b"], mode="nt", out_dtype=BF16, name="d_ob")
    gw_branch_b = mm(sv["o_b"], dyb, mode="tn", out_dtype=BF16, name="gw_branch_b")
    bias = sv["bias"]
    if hooks is not None:
        bias = bias + hooks["late_start"](dict(w_branch_a=gw_branch_a, w_branch_b=gw_branch_b, w_o=gw_o, w_up=gw_up,
                                               w_down=gw_down))[0, 0]
    dq_b, dk_pad, dv_pad, dbias = attn_bwd(sv["qkv_pad"], bias, do_b)
    if hooks is not None:
        dbias = dbias + hooks["late_finish"](dq_b)[0, 0]
    dqkvb = jnp.concatenate([dq_b, dk_pad[PAD_ROWS:].astype(BF16), dv_pad[PAD_ROWS:].astype(BF16)], axis=1)
    drel_bias = bias_reduce(jnp.transpose(dbias, (1, 0, 2)))
    do_pre, dz, dnorm_a = gate_a_bwd(do_a, sv["o_pre"], sv["z"], sm["norm_a"])
    dq, dk, dv, dbeta, dg = delta_bwd(sv["q"], sv["k"], sv["v"], sv["beta"], sv["g"], sv["sprev"], sv["tinv"], do_pre)
    dqkv_raw, dba16, dconv_a, da_log, ddt_bias = prep_a_bwd(
        sv["qkv_raw"], sv["ba"], sm["conv_a"], sm["a_log"], sm["dt_bias"], dq, dk, dv, dbeta, dg)
    dba = jnp.pad(dba16, ((0, 0), (0, BA_PAD - 2 * A_HEADS))).astype(BF16)
    pieces = dict(qkv=dqkv_raw, z=dz, ba=dba, qkvb=dqkvb, g=dgates)
    gw_in = join_w_in({key: mm(sv["h1"], dpiece, mode="tn", out_dtype=BF16, name="gw_in_" + key)
                       for key, dpiece in pieces.items()})
    w_ba = w["ba"]
    w_z = w["z"]
    if hooks is not None:
        w_ba = w_ba + hooks["w_in_start"](gw_in)[0, 0].astype(BF16)
    dh1 = mm(pieces["ba"], w_ba, mode="nt", out_dtype=F32, name="d_h1_ba")
    dh1 = mm(pieces["qkv"], w["qkv"], mode="nt", out_dtype=F32, name="d_h1_qkv", acc_in=dh1)
    if hooks is not None:
        w_z = w_z + hooks["w_in_finish"](dh1)[0, 0].astype(BF16)
    dh1 = mm(pieces["z"], w_z, mode="nt", out_dtype=F32, name="d_h1_z", acc_in=dh1)
    for key in ("qkvb", "g"):
        dh1 = mm(pieces[key], w[key], mode="nt", out_dtype=F32, name="d_h1_" + key, acc_in=dh1)
    grad_x, dsc_t, dsh_t = grad_x_final(dh1, x, dxpre1, mod)
    dmod = jnp.concatenate([dsh_t, dsc_t, dgate_t, dsh_f, dsc_f, fin["gate_f"]], axis=0)
    gw = dict(w_in=gw_in, w_branch_a=gw_branch_a, w_branch_b=gw_branch_b, w_o=gw_o, w_up=gw_up, w_down=gw_down)
    gs = dict(b_gate=db_gate, conv_a=dconv_a, a_log=da_log, dt_bias=ddt_bias, norm_a=dnorm_a, rel_bias=drel_bias,
              ln1_g=dln1_g, ln1_b=dln1_b, conv_ffn=dconv_ffn, b_conv_ffn=db_conv_ffn, ln2_g=fin["ln2_g"], ln2_b=fin["ln2_b"])
    return grad_x, dmod, gw, gs


MESH = pl.DeviceIdType.MESH
ANY = pl.BlockSpec(memory_space=pl.ANY)
WHOLE_VMEM = pl.BlockSpec(memory_space=pltpu.VMEM)


def _place():
    return lax.axis_index("x"), lax.axis_index("y"), lax.axis_index("c")


def allgather8(blk, name):
    m_per, n = blk.shape

    def body(x_ref, out_ref, send_sems, recv_sems, local_sem):
        x, y, c = _place()
        me, sibling = (x, y, c), (x, y, 1 - c)
        chips = [(1 - x, y), (x, 1 - y), (1 - x, 1 - y)]

        def rows(px, py, pc):
            return out_ref.at[pl.ds((4 * px + 2 * py + pc) * m_per, m_per), :]

        def copy(k, block, to, src=None):
            return pltpu.make_async_remote_copy(
                src_ref=rows(*block) if src is None else src, dst_ref=rows(*block),
                send_sem=send_sems.at[k], recv_sem=recv_sems.at[k], device_id=to, device_id_type=MESH)

        mine = pltpu.make_async_copy(x_ref, rows(*me), local_sem)
        mine.start()
        first = [copy(0, me, sibling, src=x_ref)]
        first += [copy(1 + j, me, (*chip, c), src=x_ref) for j, chip in enumerate(chips)]
        for cp in first:
            cp.start()
        passed = [copy(4 + j, (*chip, c), sibling) for j, chip in enumerate(chips)]
        for j, chip in enumerate(chips):
            copy(1 + j, (*chip, c), me).wait_recv()
            passed[j].start()
        copy(0, sibling, me).wait_recv()
        for j, chip in enumerate(chips):
            copy(4 + j, (*chip, 1 - c), me).wait_recv()
        for cp in first + passed:
            cp.wait_send()
        mine.wait()

    return pl.pallas_call(
        body, name=name, out_shape=jax.ShapeDtypeStruct((N_DEV * m_per, n), blk.dtype),
        in_specs=[WHOLE_VMEM], out_specs=WHOLE_VMEM,
        scratch_shapes=[pltpu.SemaphoreType.DMA((7,)), pltpu.SemaphoreType.DMA((7,)), pltpu.SemaphoreType.DMA],
    )(blk)


def _chip_peers(x, y):
    return [(1 - x, y), (x, 1 - y), (1 - x, 1 - y)]


def chip_exchange(arrs, name, scatter):
    n = len(arrs)

    def body(*refs):
        ins, outs = refs[:n], refs[n:2 * n]
        send_sems, recv_sems, local_sems = refs[2 * n:]
        x, y, c = _place()
        me = 2 * x + y
        sibling = (x, y, 1 - c)
        peers = _chip_peers(x, y)

        def half(ref, which):
            r2 = ref.shape[0] // 2
            return ref.at[pl.ds(which * r2, r2), :]

        def outgoing(a, chip):
            return ins[a].at[chip] if scatter else ins[a]

        def copy(k, src, dst, to):
            return pltpu.make_async_remote_copy(src_ref=src, dst_ref=dst, send_sem=send_sems.at[k],
                                                recv_sem=recv_sems.at[k], device_id=to, device_id_type=MESH)

        started, local = [], []
        for a in range(n):
            lc = pltpu.make_async_copy(outgoing(a, me), outs[a].at[me], local_sems.at[a])
            lc.start()
            local.append(lc)
            for j, (px, py) in enumerate(peers):
                cp = copy(6 * a + j, half(outgoing(a, 2 * px + py), c), half(outs[a].at[me], c), (px, py, c))
                cp.start()
                started.append(cp)
        for a in range(n):
            for j, (px, py) in enumerate(peers):
                landed = half(outs[a].at[2 * px + py], c)
                copy(6 * a + j, landed, landed, (px, py, c)).wait_recv()
                relay = copy(6 * a + 3 + j, landed, landed, sibling)
                relay.start()
                started.append(relay)
        for a in range(n):
            for j, (px, py) in enumerate(peers):
                other = half(outs[a].at[2 * px + py], 1 - c)
                copy(6 * a + 3 + j, other, other, sibling).wait_recv()
        for cp in started:
            cp.wait_send()
        for lc in local:
            lc.wait()

    out_shape = [jax.ShapeDtypeStruct(a.shape if scatter else (N_CHIPS,) + a.shape, a.dtype) for a in arrs]
    return pl.pallas_call(
        body, name=name, out_shape=out_shape, in_specs=[ANY] * n, out_specs=[ANY] * n,
        scratch_shapes=[pltpu.SemaphoreType.DMA((6 * n,)), pltpu.SemaphoreType.DMA((6 * n,)), pltpu.SemaphoreType.DMA((n,))],
    )(*arrs)


HBM_SPEC = pl.BlockSpec(memory_space=pltpu.HBM)
SEM_SPEC = pl.BlockSpec(memory_space=pltpu.SEMAPHORE)
SIDE_EFFECT = pltpu.SideEffectType.DATAFLOW_SIDE_EFFECTING


def _in_hbm(a):
    return pltpu.with_memory_space_constraint(a, pltpu.HBM)


def exchange_start(arrs, name, scatter, after):
    n = len(arrs)
    lands = [lax.empty(a.shape if scatter else (N_CHIPS,) + a.shape, a.dtype) for a in arrs]

    def body(*refs):
        ins, zones = refs[:n], refs[n:2 * n]
        send_sems, recv_sems, token = refs[2 * n + 1], refs[2 * n + 2], refs[-1]
        x, y, c = _place()
        me = 2 * x + y
        for a in range(n):
            for j, (px, py) in enumerate(_chip_peers(x, y)):
                pltpu.make_async_remote_copy(
                    src_ref=ins[a].at[2 * px + py] if scatter else ins[a], dst_ref=zones[a].at[me],
                    send_sem=send_sems.at[3 * a + j], recv_sem=recv_sems.at[3 * a + j],
                    device_id=(px, py, c), device_id_type=MESH).start()
        token[...] = jnp.zeros_like(token)

    res = pl.pallas_call(
        body, name=name,
        out_shape=[pltpu.SemaphoreType.DMA((3 * n,)), pltpu.SemaphoreType.DMA((3 * n,))]
        + [pltpu.HBM(a.shape, a.dtype) for a in arrs] + [pltpu.HBM(z.shape, z.dtype) for z in lands]
        + [jax.ShapeDtypeStruct((8, 128), F32)],
        in_specs=[HBM_SPEC] * (2 * n) + [ANY], out_specs=[SEM_SPEC, SEM_SPEC] + [HBM_SPEC] * (2 * n) + [WHOLE_VMEM],
        input_output_aliases={i: 2 + i for i in range(2 * n)},
        compiler_params=pltpu.CompilerParams(has_side_effects=SIDE_EFFECT),
    )(*[_in_hbm(a) for a in arrs], *[_in_hbm(z) for z in lands], after)
    return dict(send=res[0], recv=res[1], src=res[2:2 + n], zones=res[2 + n:2 + 2 * n], token=res[-1], scatter=scatter)


def exchange_wait(handle, name, after):
    srcs, zones, scatter = handle["src"], handle["zones"], handle["scatter"]
    n = len(srcs)

    def body(*refs):
        ins, lands = refs[:n], refs[n:2 * n]
        send_sems, recv_sems = refs[2 * n], refs[2 * n + 1]
        x, y, c = _place()
        me = 2 * x + y
        for a in range(n):
            for j, (px, py) in enumerate(_chip_peers(x, y)):
                cp = pltpu.make_async_remote_copy(
                    src_ref=ins[a].at[me] if scatter else ins[a], dst_ref=lands[a].at[2 * px + py],
                    send_sem=send_sems.at[3 * a + j], recv_sem=recv_sems.at[3 * a + j],
                    device_id=(px, py, c), device_id_type=MESH)
                cp.wait_send()
                cp.wait_recv()

    res = pl.pallas_call(
        body, name=name, out_shape=[pltpu.HBM(a.shape, a.dtype) for a in list(srcs) + list(zones)],
        in_specs=[HBM_SPEC] * (2 * n) + [SEM_SPEC, SEM_SPEC, ANY], out_specs=[HBM_SPEC] * (2 * n),
        input_output_aliases={i: i for i in range(2 * n)},
        compiler_params=pltpu.CompilerParams(has_side_effects=SIDE_EFFECT),
    )(*srcs, *zones, handle["send"], handle["recv"], after)
    return res[n:]


def swap_start(arrs, name, after):
    n = len(arrs)
    lands = [lax.empty(a.shape, a.dtype) for a in arrs]

    def body(*refs):
        ins, zones = refs[:n], refs[n:2 * n]
        send_sems, recv_sems, token = refs[2 * n + 1], refs[2 * n + 2], refs[-1]
        x, y, c = _place()
        for a in range(n):
            pltpu.make_async_remote_copy(src_ref=ins[a], dst_ref=zones[a], send_sem=send_sems.at[a], recv_sem=recv_sems.at[a],
                                         device_id=(x, y, 1 - c), device_id_type=MESH).start()
        token[...] = jnp.zeros_like(token)

    res = pl.pallas_call(
        body, name=name,
        out_shape=[pltpu.SemaphoreType.DMA((n,)), pltpu.SemaphoreType.DMA((n,))]
        + [pltpu.HBM(a.shape, a.dtype) for a in arrs] * 2 + [jax.ShapeDtypeStruct((8, 128), F32)],
        in_specs=[HBM_SPEC] * (2 * n) + [ANY], out_specs=[SEM_SPEC, SEM_SPEC] + [HBM_SPEC] * (2 * n) + [WHOLE_VMEM],
        input_output_aliases={i: 2 + i for i in range(2 * n)},
        compiler_params=pltpu.CompilerParams(has_side_effects=SIDE_EFFECT),
    )(*[_in_hbm(a) for a in arrs], *[_in_hbm(z) for z in lands], after)
    return dict(send=res[0], recv=res[1], src=res[2:2 + n], zones=res[2 + n:2 + 2 * n], token=res[-1])


def swap_wait(handle, name, after):
    srcs, zones = handle["src"], handle["zones"]
    n = len(srcs)

    def body(*refs):
        ins, lands = refs[:n], refs[n:2 * n]
        send_sems, recv_sems = refs[2 * n], refs[2 * n + 1]
        x, y, c = _place()
        for a in range(n):
            cp = pltpu.make_async_remote_copy(src_ref=ins[a], dst_ref=lands[a], send_sem=send_sems.at[a],
                                              recv_sem=recv_sems.at[a], device_id=(x, y, 1 - c), device_id_type=MESH)
            cp.wait_send()
            cp.wait_recv()

    res = pl.pallas_call(
        body, name=name, out_shape=[pltpu.HBM(a.shape, a.dtype) for a in list(srcs) + list(zones)],
        in_specs=[HBM_SPEC] * (2 * n) + [SEM_SPEC, SEM_SPEC, ANY], out_specs=[HBM_SPEC] * (2 * n),
        input_output_aliases={i: i for i in range(2 * n)},
        compiler_params=pltpu.CompilerParams(has_side_effects=SIDE_EFFECT),
    )(*srcs, *zones, handle["send"], handle["recv"], after)
    return res[:n], res[n:]


TILE_BYTES = 2 * 1024 * 1024


def _row_tile(rows, row_bytes):
    if rows * row_bytes <= TILE_BYTES or rows % 8:
        return rows
    best = 8
    for t in range(8, rows + 1, 8):
        if rows % t == 0 and t * row_bytes <= TILE_BYTES:
            best = t
    return best


def pair_add(a, b, name):
    shape = a.shape
    a, b = a.reshape(-1, shape[-1]), b.reshape(-1, shape[-1])
    R, C = a.shape
    tr = _row_tile(R, C * 4)

    def body(a_ref, b_ref, o_ref):
        o_ref[...] = (a_ref[...].astype(F32) + b_ref[...].astype(F32)).astype(BF16)

    spec = pl.BlockSpec((tr, C), lambda i: (i, 0))
    return pl.pallas_call(body, name=name, grid=(R // tr,), in_specs=[spec, spec], out_specs=spec,
                          out_shape=jax.ShapeDtypeStruct((R, C), BF16), compiler_params=_cparams(("parallel",)))(a, b).reshape(shape)


def sum_lead(parts, name):
    K, R, C = parts.shape
    tr = _row_tile(R, C * 4)

    def body(p_ref, o_ref):
        acc = p_ref[0].astype(F32)
        for j in range(1, K):
            acc = acc + p_ref[j].astype(F32)
        o_ref[...] = acc

    return pl.pallas_call(
        body, name=name, grid=(R // tr,), in_specs=[pl.BlockSpec((K, tr, C), lambda i: (0, i, 0))],
        out_specs=pl.BlockSpec((tr, C), lambda i: (i, 0)), out_shape=jax.ShapeDtypeStruct((R, C), F32),
        compiler_params=_cparams(("parallel",)))(parts)


def adamw(w, g, m, v, name):
    R, C = w.shape
    tr = _row_tile(R, C * 4)

    def body(w_ref, g_ref, m_ref, v_ref, d_ref, mo_ref, vo_ref):
        gv = g_ref[...]
        m2 = ADAM_B1 * m_ref[...] + (1.0 - ADAM_B1) * gv
        v2 = ADAM_B2 * v_ref[...] + (1.0 - ADAM_B2) * (gv * gv)
        m_hat = m2 / (1.0 - ADAM_B1 ** ADAM_STEP)
        v_hat = v2 / (1.0 - ADAM_B2 ** ADAM_STEP)
        d_ref[...] = -ADAM_LR * (m_hat / (jnp.sqrt(v_hat) + ADAM_EPS) + ADAM_WD * w_ref[...])
        mo_ref[...] = m2
        vo_ref[...] = v2

    spec = pl.BlockSpec((tr, C), lambda i: (i, 0))
    return pl.pallas_call(body, name=name, grid=(R // tr,), in_specs=[spec] * 4, out_specs=[spec] * 3,
                          out_shape=[jax.ShapeDtypeStruct((R, C), F32)] * 3, compiler_params=_cparams(("parallel",)))(w, g, m, v)


LANES = 1024


def _pack(arrs, rows):
    out, offs, r = [], [], 0
    for a in arrs:
        flat = a.reshape(-1)
        nr = -(-flat.shape[0] // LANES)
        out.append(jnp.pad(flat, (0, nr * LANES - flat.shape[0])))
        offs.append(r)
        r += nr
    assert r <= rows, (r, rows)
    out.append(jnp.zeros(((rows - r) * LANES,), F32))
    return jnp.concatenate(out).reshape(rows, LANES), offs


def _unpack(packed, offs, shapes):
    flat = packed.reshape(-1)
    return [flat[o * LANES:o * LANES + math.prod(s)].reshape(s) for o, s in zip(offs, shapes)]


WEIGHTS = ["w_ada", "b_ada", "w_in", "b_gate", "conv_a", "a_log", "dt_bias", "norm_a", "rel_bias", "w_branch_a",
           "w_branch_b", "w_o", "ln1_g", "ln1_b", "w_up", "conv_ffn", "b_conv_ffn", "w_down", "ln2_g", "ln2_b"]
BIG = ["w_in", "w_branch_a", "w_branch_b", "w_o", "w_up", "w_down"]
LATE = [n for n in BIG if n != "w_in"]
KEPT_SHARDED = {"w_up"}
COL_SHARDED = {"w_in", "w_up"}
SMALL_SHARDED = {"conv_a": 3 * A_W // N_CHIPS, "rel_bias": B_REL // N_CHIPS, "conv_ffn": 2 * D_FF // N_CHIPS}
SMALL = [n for n in WEIGHTS if n not in BIG and n != "w_ada"]


def _to_full(g4, name):
    if name in KEPT_SHARDED:
        return g4
    if name in COL_SHARDED:
        return jnp.transpose(g4, (1, 0, 2)).reshape(g4.shape[1], -1)
    return g4.reshape(-1, g4.shape[2])


def _to_shards(full, name):
    if name in KEPT_SHARDED:
        return full
    if name in COL_SHARDED:
        return jnp.transpose(full.reshape(full.shape[0], N_CHIPS, -1), (1, 0, 2))
    return full.reshape(N_CHIPS, -1, full.shape[1])


def kernel(x, c, w_ada, b_ada, w_in, b_gate, conv_a, a_log, dt_bias, norm_a, rel_bias, w_branch_a, w_branch_b, w_o, ln1_g, ln1_b, w_up, conv_ffn, b_conv_ffn, w_down, ln2_g, ln2_b, loss_target, m_w_ada, m_b_ada, m_w_in, m_b_gate, m_conv_a, m_a_log, m_dt_bias, m_norm_a, m_rel_bias, m_w_branch_a, m_w_branch_b, m_w_o, m_ln1_g, m_ln1_b, m_w_up, m_conv_ffn, m_b_conv_ffn, m_w_down, m_ln2_g, m_ln2_b, v_w_ada, v_b_ada, v_w_in, v_b_gate, v_conv_a, v_a_log, v_dt_bias, v_norm_a, v_rel_bias, v_w_branch_a, v_w_branch_b, v_w_o, v_ln1_g, v_ln1_b, v_w_up, v_conv_ffn, v_b_conv_ffn, v_w_down, v_ln2_g, v_ln2_b):
    args = dict(locals())
    wts = {n: args[n] for n in WEIGHTS}
    moms = {n: args["m_" + n] for n in WEIGHTS}
    vars_ = {n: args["v_" + n] for n in WEIGHTS}
    xi, yi, ci = _place()
    chip = 2 * xi + yi
    dev = 4 * xi + 2 * yi + ci
    ada_cols = w_ada.shape[2]

    sshapes = [wts[n].shape[1:] for n in SMALL_SHARDED]
    spack, soffs = _pack([wts[n][0] for n in SMALL_SHARDED], 16)
    first = allgather8(jnp.concatenate([jnp.pad(c, ((0, 7), (0, 0))), spack]), "gather_c_small_w").reshape(N_DEV, 24, LANES)
    c_all = first[:, 0]
    b_ada_sh = lax.dynamic_slice(b_ada, (0, chip * ada_cols), (1, ada_cols))
    mod_sh = ada_fwd(c_all, w_ada[0], b_ada_sh)
    mod_g = allgather8(mod_sh, "gather_mod").reshape(N_CHIPS, 2, N_DEV, ada_cols)[:, 0]
    mod = lax.dynamic_slice(mod_g, (0, dev, 0), (N_CHIPS, 1, ada_cols)).reshape(6, D_MODEL)

    (w_in_g4,) = chip_exchange([wts["w_in"][0].astype(BF16)], "gather_w_in", scatter=False)
    wd = split_w_in(_to_full(w_in_g4, "w_in"))
    late_shards = [wts[n][0].astype(BF16) for n in LATE]
    late_gather = exchange_start(late_shards, "gather_late_start", scatter=False, after=w_in_g4)
    mod = mod + late_gather["token"][0, 0]

    def late_weights(after):
        zones = exchange_wait(late_gather, "gather_late_wait", after)
        full = [_to_full(lax.dynamic_update_slice(z, s[None], (chip, 0, 0)), n) for n, z, s in zip(LATE, zones, late_shards)]
        return {n[2:]: f for n, f in zip(LATE, full)}

    sg = first[::2, 8:]
    sparts = [_unpack(sg[j], soffs, sshapes) for j in range(N_CHIPS)]
    sm = {n: wts[n] for n in SMALL if n not in SMALL_SHARDED and n != "b_ada"}
    for i, n in enumerate(SMALL_SHARDED):
        sm[n] = jnp.concatenate([sparts[j][i] for j in range(N_CHIPS)], axis=-1)

    early = {}

    def late_start(g):
        early["swap"] = swap_start([g[n] for n in LATE], "grad_swap_late_start", g[LATE[0]])
        return early["swap"]["token"]

    def late_finish(after):
        mine, theirs = swap_wait(early["swap"], "grad_swap_late_wait", after)
        early["sums"] = [_to_shards(pair_add(a, b, "grad_pair_" + n), n) for n, a, b in zip(LATE, mine, theirs)]
        early["scatter"] = exchange_start(early["sums"], "grad_scatter_start", scatter=True, after=theirs[0])
        return early["scatter"]["token"]

    def w_in_start(g):
        early["swap_in"] = swap_start([g], "grad_swap_w_in_start", g)
        return early["swap_in"]["token"]

    def w_in_finish(after):
        (mine,), (theirs,) = swap_wait(early["swap_in"], "grad_swap_w_in_wait", after)
        early["sum_in"] = _to_shards(pair_add(mine, theirs, "grad_pair_w_in"), "w_in")
        early["scatter_in"] = exchange_start([early["sum_in"]], "grad_scatter_w_in_start", scatter=True, after=theirs)
        return early["scatter_in"]["token"]

    hooks = dict(late_start=late_start, late_finish=late_finish, w_in_start=w_in_start, w_in_finish=w_in_finish)
    loss, dxpre2, dffn, fin, sv = forward_local(x[0], loss_target[0], mod, wd, sm, late_weights)
    grad_x, dmod, gw, gs = backward_local(x[0], mod, sm, dxpre2, dffn, fin, sv, hooks)

    gnames = [n for n in SMALL if n != "b_ada"]
    vec, voffs = _pack([dmod] + [gs[n] for n in gnames] + [loss], 56)
    gathered = allgather8(vec, "gather_small_g").reshape(N_DEV, 56, LANES)
    summed = sum_lead(gathered, "sum_small_g")
    full_shapes = [(6, D_MODEL)] + [gs[n].shape for n in gnames] + [(1, 1)]
    parts = _unpack(summed, voffs, full_shapes)
    grads = {"b_ada": parts[0].reshape(1, -1)}
    for n, p in zip(gnames, parts[1:-1]):
        if n in SMALL_SHARDED:
            p = lax.dynamic_slice_in_dim(p, chip * SMALL_SHARDED[n], SMALL_SHARDED[n], axis=1)
        grads[n] = p.reshape(wts[n].shape)
    loss_total = parts[-1].reshape(())
    dmod_all = gathered[:, 0:6, :].reshape(N_DEV, 6 * D_MODEL)
    grads["w_ada"] = ada_bwd(c_all, lax.dynamic_slice(dmod_all, (0, chip * ada_cols), (N_DEV, ada_cols)))[None]

    def own_slot(zone, sums):
        return lax.dynamic_update_slice(zone, lax.dynamic_slice_in_dim(sums, chip, 1, axis=0), (chip, 0, 0))

    zones = exchange_wait(early["scatter"], "grad_scatter_wait", summed)
    for n, z, s in zip(LATE, zones, early["sums"]):
        grads[n] = sum_lead(own_slot(z, s), "grad_sum_" + n)[None]

    delta, new_m, new_v = {}, {}, {}

    def update(n):
        d, m2, v2 = adamw(wts[n][0], grads[n][0], moms[n][0], vars_[n][0], "adamw_" + n)
        delta[n], new_m[n], new_v[n] = d[None], m2[None], v2[None]

    for n in ["w_ada"] + LATE:
        update(n)
    shapes = [wts[n].shape for n in SMALL]
    packs = [_pack([t[n] for n in SMALL], 32) for t in (wts, grads, moms, vars_)]
    outs = adamw(*[p[0] for p in packs], "adamw_small")
    for res, o in zip((delta, new_m, new_v), outs):
        for n, a in zip(SMALL, _unpack(o, packs[0][1], shapes)):
            res[n] = a
    (zone_in,) = exchange_wait(early["scatter_in"], "grad_scatter_w_in_wait", outs[0])
    grads["w_in"] = sum_lead(own_slot(zone_in, early["sum_in"]), "grad_sum_w_in")[None]
    update("w_in")
    return (loss_total, grad_x[None], *[grads[n] for n in WEIGHTS], *[delta[n] for n in WEIGHTS],
            *[new_m[n] for n in WEIGHTS], *[new_v[n] for n in WEIGHTS])
```

```python
import functools
import math

import jax
import jax.numpy as jnp
from jax import lax
from jax.experimental import pallas as pl
from jax.experimental.pallas import tpu as pltpu

F32 = jnp.float32
BF16 = jnp.bfloat16

D_MODEL = 1024
CHUNK = 64
A_HEADS = 8
A_DK = 128
A_W = A_HEADS * A_DK
A_CONV = 4
B_HEADS = 16
B_DH = 64
B_W = B_HEADS * B_DH
B_PREV = 8
B_BAND = (B_PREV + 1) * CHUNK
B_MAX_REL = 256
B_REL = CHUNK - 1 + B_MAX_REL + 1
D_FF = 2816
FFN_CONV = 3
IN_COLS = 4 * A_W + 2 * A_HEADS + 3 * B_W + 2 * D_MODEL
ALPHA = 2.0 ** 0.25
LN_EPS = 1e-5
RMS_EPS = 1e-6
L2_EPS = 1e-6
NEG_INF = -1e30
ADAM_LR, ADAM_B1, ADAM_B2, ADAM_EPS, ADAM_WD, ADAM_STEP = 0.001, 0.9, 0.999, 1e-08, 0.01, 10
N_CHIPS = 4
N_DEV = 8
VMEM_LIMIT = 56 * 1024 * 1024


def _cparams(sem=None):
    return pltpu.CompilerParams(dimension_semantics=sem, vmem_limit_bytes=VMEM_LIMIT)


_DIMS = {"nn": (((1,), (0,)), ((), ())), "nt": (((1,), (1,)), ((), ())), "tn": (((0,), (0,)), ((), ()))}


MM_TILE_CAP = 1536


MM_TOKEN_K_CAP = 2048
MM_K_CAP = 3072


def _mm_tile(n, cap=MM_TILE_CAP):
    return max(t for t in range(128, min(n, cap) + 1, 128) if n % t == 0)


def mm(a, b, *, mode, out_dtype, name, acc_in=None, b_shards=False, out_shards=0):
    b_rows, b_cols = (b.shape[1], b.shape[0] * b.shape[2]) if b_shards else b.shape
    if mode == "nn":
        (M, K), (K2, N) = a.shape, (b_rows, b_cols)
    elif mode == "nt":
        (M, K), (N, K2) = a.shape, (b_rows, b_cols)
    else:
        (K, M), (K2, N) = a.shape, (b_rows, b_cols)
    assert K == K2, (a.shape, b.shape, mode)
    tm, tn, tk = _mm_tile(M), _mm_tile(N), _mm_tile(K, MM_TOKEN_K_CAP if mode == "tn" else MM_K_CAP)
    if b_shards and mode == "nt":
        tk = b.shape[2]
    nk = K // tk

    def body(*refs):
        if acc_in is None:
            a_ref, b_ref, o_ref, acc_ref = refs
        else:
            a_ref, b_ref, c_ref, o_ref, acc_ref = refs
        k = pl.program_id(2)

        @pl.when(k == 0)
        def _():
            if acc_in is None:
                acc_ref[...] = jnp.zeros_like(acc_ref)
            else:
                acc_ref[...] = c_ref[...]

        acc_ref[...] += lax.dot_general(a_ref[...].astype(BF16), b_ref[...].astype(BF16), _DIMS[mode],
                                        preferred_element_type=F32)

        @pl.when(k == nk - 1)
        def _():
            o_ref[...] = acc_ref[...].astype(out_dtype)

    a_spec = pl.BlockSpec((tk, tm), lambda i, j, k: (k, i)) if mode == "tn" else pl.BlockSpec((tm, tk), lambda i, j, k: (i, k))
    if b_shards:
        assert (tk if mode == "nt" else tn) == b.shape[2] and mode != "tn", (b.shape, tn, tk, mode)
        b_spec = (pl.BlockSpec((None, tn, tk), lambda i, j, k: (k, j, 0)) if mode == "nt"
                  else pl.BlockSpec((None, tk, tn), lambda i, j, k: (j, k, 0)))
    else:
        b_spec = pl.BlockSpec((tn, tk), lambda i, j, k: (j, k)) if mode == "nt" else pl.BlockSpec((tk, tn), lambda i, j, k: (k, j))
    o_spec = pl.BlockSpec((tm, tn), lambda i, j, k: (i, j))
    out_shape = jax.ShapeDtypeStruct((M, N), out_dtype)
    if out_shards:
        assert N == out_shards * tn and acc_in is None, (N, tn, out_shards)
        o_spec = pl.BlockSpec((None, tm, tn), lambda i, j, k: (j, i, 0))
        out_shape = jax.ShapeDtypeStruct((out_shards, M, tn), out_dtype)
    ins, in_specs, aliases = [a, b], [a_spec, b_spec], {}
    if acc_in is not None:
        assert acc_in.shape == (M, N) and acc_in.dtype == F32 and out_dtype == F32
        ins.append(acc_in)
        in_specs.append(o_spec)
        aliases = {2: 0}
    return pl.pallas_call(
        body, name=name, grid=(M // tm, N // tn, nk), in_specs=in_specs, out_specs=o_spec,
        out_shape=out_shape, scratch_shapes=[pltpu.VMEM((tm, tn), F32)],
        input_output_aliases=aliases, compiler_params=_cparams(("parallel", "parallel", "arbitrary")),
    )(*ins)


def rowcall(body, *, name, S, ts, ins, outs, scratch=()):
    assert S % ts == 0 and ts % 16 == 0
    nsteps = S // ts
    in_specs, arrays = [], []
    for arr, kind in ins:
        arrays.append(arr)
        if kind == "row":
            in_specs.append(pl.BlockSpec((ts, arr.shape[1]), lambda i: (i, 0)))
        elif kind in ("prev", "next"):
            hr = 8 * (4 // arr.dtype.itemsize)
            per, last = ts // hr, S // hr - 1
            if kind == "prev":
                in_specs.append(pl.BlockSpec((hr, arr.shape[1]), lambda i, per=per: (jnp.maximum(i * per - 1, 0), 0)))
            else:
                in_specs.append(pl.BlockSpec((hr, arr.shape[1]), lambda i, per=per, last=last: (jnp.minimum((i + 1) * per, last), 0)))
        else:
            nd = arr.ndim
            in_specs.append(pl.BlockSpec(arr.shape, lambda i, nd=nd: (0,) * nd))
    out_specs, out_shapes, acc_idx = [], [], []
    for n, (shape, dtype, kind) in enumerate(outs):
        out_shapes.append(jax.ShapeDtypeStruct(shape, dtype))
        if kind == "row":
            out_specs.append(pl.BlockSpec((ts, shape[1]), lambda i: (i, 0)))
        else:
            nd = len(shape)
            out_specs.append(pl.BlockSpec(shape, lambda i, nd=nd: (0,) * nd))
            acc_idx.append(n)
    n_in = len(arrays)

    def wrapped(*refs):
        @pl.when(pl.program_id(0) == 0)
        def _():
            for n in acc_idx:
                refs[n_in + n][...] = jnp.zeros_like(refs[n_in + n])

        body(*refs)

    res = pl.pallas_call(
        wrapped, name=name, grid=(nsteps,), in_specs=in_specs, out_specs=out_specs, out_shape=out_shapes,
        scratch_shapes=list(scratch), compiler_params=_cparams(("arbitrary",) if acc_idx else ("parallel",)),
    )(*arrays)
    return res


def _halo_prev(ref):
    v = ref[...].astype(F32)
    return v[v.shape[0] - 8:]


def _halo_next(ref):
    return ref[...].astype(F32)[:8]


def _shift_down(cur, prev8, k):
    if k == 0:
        return cur
    rolled = pltpu.roll(cur, k, axis=0)
    fix = pltpu.roll(prev8, k, axis=0)
    row = lax.broadcasted_iota(jnp.int32, (8, 1), 0)
    top = jnp.where(row < k, fix, rolled[0:8])
    if cur.shape[0] == 8:
        return top
    return jnp.concatenate([top, rolled[8:]], axis=0)


def _shift_up(cur, next8, k):
    if k == 0:
        return cur
    n = cur.shape[0]
    rolled = pltpu.roll(cur, n - k, axis=0)
    fix = pltpu.roll(next8, 8 - k, axis=0)
    row = lax.broadcasted_iota(jnp.int32, (8, 1), 0)
    bot = jnp.where(row >= 8 - k, fix, rolled[n - 8:n])
    return jnp.concatenate([rolled[:n - 8], bot], axis=0)


def _sigmoid(x):
    return 1.0 / (1.0 + jnp.exp(-x))


def _silu(x):
    return x * _sigmoid(x)


def _silu_and_grad(x):
    s = _sigmoid(x)
    return x * s, s * (1.0 + x * (1.0 - s))


def _softplus(x):
    return jnp.maximum(x, 0.0) + jnp.log1p(jnp.exp(-jnp.abs(x)))


def _split2(x):
    hi = x.astype(BF16)
    return hi, (x - hi.astype(F32)).astype(BF16)


def _dot1(a, b, mode):
    return lax.dot_general(a.astype(BF16), b.astype(BF16), _DIMS[mode], preferred_element_type=F32)


def _dot3(a, b, mode):
    ah, al = _split2(a)
    bh, bl = _split2(b)
    d = lambda p, q: lax.dot_general(p, q, _DIMS[mode], preferred_element_type=F32)
    return d(ah, bh) + (d(ah, bl) + d(al, bh))


def ada_fwd(c_all, w_sh, b_sh):
    n = w_sh.shape[1]
    tn = 512

    def body(c_ref, w_ref, b_ref, o_ref):
        o_ref[...] = _dot1(_silu(c_ref[...]), w_ref[...], "nn") + b_ref[...]

    return pl.pallas_call(
        body, name="ada_fwd", grid=(n // tn,),
        in_specs=[pl.BlockSpec((N_DEV, D_MODEL), lambda j: (0, 0)), pl.BlockSpec((D_MODEL, tn), lambda j: (0, j)),
                  pl.BlockSpec((1, tn), lambda j: (0, j))],
        out_specs=pl.BlockSpec((N_DEV, tn), lambda j: (0, j)), out_shape=jax.ShapeDtypeStruct((N_DEV, n), F32),
        compiler_params=_cparams(("parallel",)),
    )(c_all, w_sh, b_sh)


def ada_bwd(c_all, dmod_sh):
    n = dmod_sh.shape[1]
    tn = 512

    def body(c_ref, d_ref, o_ref):
        o_ref[...] = _dot1(_silu(c_ref[...]), d_ref[...], "tn")

    return pl.pallas_call(
        body, name="ada_bwd", grid=(n // tn,),
        in_specs=[pl.BlockSpec((N_DEV, D_MODEL), lambda j: (0, 0)), pl.BlockSpec((N_DEV, tn), lambda j: (0, j))],
        out_specs=pl.BlockSpec((D_MODEL, tn), lambda j: (0, j)), out_shape=jax.ShapeDtypeStruct((D_MODEL, n), F32),
        compiler_params=_cparams(("parallel",)),
    )(c_all, dmod_sh)


SHIFT_T, SCALE_T, GATE_T, SHIFT_F, SCALE_F, GATE_F = range(6)


def modulate(x, mod, shift_row, scale_row, name):
    S = x.shape[0]

    def body(x_ref, m_ref, o_ref):
        m = m_ref[...]
        o_ref[...] = (x_ref[...] * (1.0 + m[scale_row:scale_row + 1]) + m[shift_row:shift_row + 1]).astype(BF16)

    return rowcall(body, name=name, S=S, ts=512, ins=[(x, "row"), (mod, "vec")], outs=[((S, D_MODEL), BF16, "row")])[0]


def _conv_fwd(cur, prev, w, width):
    y = cur * w[width - 1:width]
    for j in range(width - 1):
        y = y + _shift_down(cur, prev, width - 1 - j) * w[j:j + 1]
    return y


def _prep_a_core(cur, prev, w):
    return _silu_and_grad(_conv_fwd(cur, prev, w, A_CONV))


def prep_a_fwd(qkv_raw, ba, conv_a, a_log, dt_bias):
    S = qkv_raw.shape[0]

    def body(x_ref, xp_ref, ba_ref, w_ref, al_ref, dt_ref, q_ref, k_ref, v_ref, beta_ref, g_ref):
        first = (pl.program_id(0) > 0).astype(F32)
        y, _ = _prep_a_core(x_ref[...].astype(F32), _halo_prev(xp_ref) * first, w_ref[...])
        for h in range(A_HEADS):
            sl = slice(h * A_DK, (h + 1) * A_DK)
            qh = y[:, sl]
            kh = y[:, A_W + h * A_DK:A_W + (h + 1) * A_DK]
            q_ref[:, sl] = qh * (lax.rsqrt(jnp.sum(qh * qh, axis=-1, keepdims=True) + L2_EPS) * (A_DK ** -0.5))
            k_ref[:, sl] = kh * lax.rsqrt(jnp.sum(kh * kh, axis=-1, keepdims=True) + L2_EPS)
        v_ref[...] = y[:, 2 * A_W:3 * A_W]
        bav = ba_ref[...]
        beta_ref[...] = _sigmoid(bav[:, 0:A_HEADS])
        g_ref[...] = -jnp.exp(al_ref[...]) * _softplus(bav[:, A_HEADS:2 * A_HEADS] + dt_ref[...])

    return rowcall(
        body, name="prep_a_fwd", S=S, ts=256,
        ins=[(qkv_raw, "row"), (qkv_raw, "prev"), (ba, "row"), (conv_a, "vec"), (a_log, "vec"), (dt_bias, "vec")],
        outs=[((S, A_W), F32, "row")] * 3 + [((S, A_HEADS), F32, "row")] * 2)


HEAD_GROUP = 2
GROUP_ROWS = HEAD_GROUP * CHUNK
N_HEAD_GROUPS = A_HEADS // HEAD_GROUP
LOG_CHUNK = int(math.log2(CHUNK))


def _tri_masks():
    rb = lax.broadcasted_iota(jnp.int32, (GROUP_ROWS, GROUP_ROWS), 0)
    cb = lax.broadcasted_iota(jnp.int32, (GROUP_ROWS, GROUP_ROWS), 1)
    same = (rb >> LOG_CHUNK) == (cb >> LOG_CHUNK)
    return dict(causal=same & (rb >= cb), strict=same & (rb > cb), eye=rb == cb, upper=same & (cb >= rb),
                last=cb == (rb | (CHUNK - 1)), rb=rb, cb=cb)


def _col_to_row(colv, eye):
    return jnp.sum(jnp.where(eye, colv, 0.0), axis=0, keepdims=True)


def _row_to_col(rowv, eye):
    return jnp.sum(jnp.where(eye, rowv, 0.0), axis=1, keepdims=True)


def _tri_inv(a_list, mk):
    rb, cb = mk["rb"], mk["cb"]
    ts = [jnp.where(mk["eye"], 1.0, 0.0) - jnp.where((rb >> 1) == (cb >> 1), a, 0.0) for a in a_list]
    for lvl in range(1, LOG_CHUNK):
        rs, cs = rb >> lvl, cb >> lvl
        sel = ((rs & 1) == 1) & (cs == rs - 1)
        inner = [_dot3(t, jnp.where(sel, a, 0.0), "nn") for t, a in zip(ts, a_list)]
        ts = [t - _dot3(i, t, "nn") for i, t in zip(inner, ts)]
    return ts


def _stack_heads(ref, grp):
    return jnp.concatenate([ref[:, (grp * HEAD_GROUP + j) * A_DK:(grp * HEAD_GROUP + j + 1) * A_DK]
                            for j in range(HEAD_GROUP)], axis=0)


def _stack_cols(tile, grp):
    return jnp.concatenate([tile[:, grp * HEAD_GROUP + j:grp * HEAD_GROUP + j + 1] for j in range(HEAD_GROUP)], axis=0)


def _delta_local(q, k, v, beta, g, mk):
    causal, strict, eye = mk["causal"], mk["strict"], mk["eye"]
    g_row = _col_to_row(g, eye)
    gc = jnp.sum(jnp.where(causal, g_row, 0.0), axis=1, keepdims=True)
    gc_row = _col_to_row(gc, eye)
    decay = jnp.where(causal, jnp.exp(jnp.where(causal, gc - gc_row, 0.0)), 0.0)
    gam = jnp.exp(gc)
    kb = k * beta
    vb = v * beta
    y = kb * gam
    a = jnp.where(strict, _dot1(kb, k, "nt") * decay, 0.0)
    p = _dot1(q, k, "nt") * decay
    gl = jnp.sum(jnp.where(mk["last"], gc_row, 0.0), axis=1, keepdims=True)
    kd = k * jnp.exp(gl - gc)
    return dict(gc=gc, decay=decay, gam=gam, kb=kb, vb=vb, y=y, a=a, p=p, gl=gl, kd=kd)


def _head_rows(x, j):
    return x[j * CHUNK:(j + 1) * CHUNK]


def delta_fwd(q, k, v, beta, g):
    S = q.shape[0]
    n_chunks = S // CHUNK

    def body(q_ref, k_ref, v_ref, beta_ref, g_ref, o_ref, sprev_ref, t_ref, state_ref):
        @pl.when(pl.program_id(0) == 0)
        def _():
            state_ref[...] = jnp.zeros_like(state_ref)

        mk = _tri_masks()
        betav, gv = beta_ref[...], g_ref[...]
        groups = range(N_HEAD_GROUPS)
        q_all = [_stack_heads(q_ref, grp) for grp in groups]
        locs = [_delta_local(q_all[grp], _stack_heads(k_ref, grp), _stack_heads(v_ref, grp),
                             _stack_cols(betav, grp), _stack_cols(gv, grp), mk) for grp in groups]
        tinvs = _tri_inv([loc["a"] for loc in locs], mk)
        uws = [_dot3(tinvs[grp], jnp.concatenate([locs[grp]["vb"], locs[grp]["y"]], axis=1), "nn") for grp in groups]
        for grp in groups:
            loc, uw = locs[grp], uws[grp]
            t_ref[0, grp] = tinvs[grp]
            qg = q_all[grp] * loc["gam"]
            egl = jnp.exp(loc["gl"])
            vns, o_state = [], []
            for j in range(HEAD_GROUP):
                h = grp * HEAD_GROUP + j
                s0 = state_ref[h]
                sprev_ref[0, h] = s0
                uw_h = _head_rows(uw, j)
                vn = uw_h[:, :A_DK] - _dot1(uw_h[:, A_DK:], s0, "nn")
                vns.append(vn)
                o_state.append(_dot1(_head_rows(qg, j), s0, "nn"))
                state_ref[h] = s0 * egl[(j + 1) * CHUNK - 1:(j + 1) * CHUNK] + _dot1(_head_rows(loc["kd"], j), vn, "tn")
            o_local = _dot1(loc["p"], jnp.concatenate(vns, axis=0), "nn")
            for j in range(HEAD_GROUP):
                h = grp * HEAD_GROUP + j
                o_ref[:, h * A_DK:(h + 1) * A_DK] = o_state[j] + _head_rows(o_local, j)

    tile = pl.BlockSpec((CHUNK, A_W), lambda n: (n, 0))
    small = pl.BlockSpec((CHUNK, A_HEADS), lambda n: (n, 0))
    return pl.pallas_call(
        body, name="delta_fwd", grid=(n_chunks,), in_specs=[tile, tile, tile, small, small],
        out_specs=[tile, pl.BlockSpec((1, A_HEADS, A_DK, A_DK), lambda n: (n, 0, 0, 0)),
                   pl.BlockSpec((1, N_HEAD_GROUPS, GROUP_ROWS, GROUP_ROWS), lambda n: (n, 0, 0, 0))],
        out_shape=[jax.ShapeDtypeStruct((S, A_W), F32), jax.ShapeDtypeStruct((n_chunks, A_HEADS, A_DK, A_DK), F32),
                   jax.ShapeDtypeStruct((n_chunks, N_HEAD_GROUPS, GROUP_ROWS, GROUP_ROWS), F32)],
        scratch_shapes=[pltpu.VMEM((A_HEADS, A_DK, A_DK), F32)],
        compiler_params=_cparams(("arbitrary",)),
    )(q, k, v, beta, g)


def gate_a_fwd(o_pre, z, norm_w):
    S = o_pre.shape[0]

    def body(o_ref, z_ref, nw_ref, out_ref):
        nw = nw_ref[...]
        for h in range(A_HEADS):
            sl = slice(h * A_DK, (h + 1) * A_DK)
            oh = o_ref[:, sl]
            r = lax.rsqrt(jnp.mean(oh * oh, axis=-1, keepdims=True) + RMS_EPS)
            out_ref[:, sl] = (oh * r * nw * _silu(z_ref[:, sl].astype(F32))).astype(BF16)

    return rowcall(body, name="gate_a_fwd", S=S, ts=512, ins=[(o_pre, "row"), (z, "row"), (norm_w, "vec")],
                   outs=[((S, A_W), BF16, "row")])[0]


HEADS_PER_GROUP = 2
GROUP_W = HEADS_PER_GROUP * B_DH
N_GROUPS = B_HEADS // HEADS_PER_GROUP
PAD_ROWS = B_PREV * CHUNK


Q_TILE = 256
Q_CHUNKS = Q_TILE // CHUNK
KEY_WIN = (B_PREV + Q_CHUNKS) * CHUNK


def _band_probs(qh, kh, bias, valid):
    s = _dot1(qh, kh, "nt") * (B_DH ** -0.5) + bias
    s = jnp.where(valid, s, NEG_INF)
    e = jnp.exp(s - jnp.max(s, axis=-1, keepdims=True))
    return e * (1.0 / jnp.sum(e, axis=-1, keepdims=True))


def _attn_specs(S, tile_rows):
    assert PAD_ROWS % tile_rows == 0 and S % tile_rows == 0, (PAD_ROWS, S, tile_rows)
    n_cb = B_W // GROUP_W
    return [pl.BlockSpec((tile_rows, GROUP_W), lambda g, n: (n + PAD_ROWS // tile_rows, g)),
            pl.BlockSpec((PAD_ROWS + S, GROUP_W), lambda g, n: (0, n_cb + g)),
            pl.BlockSpec((PAD_ROWS + S, GROUP_W), lambda g, n: (0, 2 * n_cb + g)),
            pl.BlockSpec((HEADS_PER_GROUP, CHUNK, B_BAND), lambda g, n: (g, 0, 0))]


def _band_valid(first_chunk):
    return lax.broadcasted_iota(jnp.int32, (CHUNK, B_BAND), 1) >= PAD_ROWS - first_chunk * CHUNK


def _chunk_rows(x, qc, rows=CHUNK):
    return x[qc * CHUNK:qc * CHUNK + rows]


FWD_TILE = 512
FWD_CHUNKS = FWD_TILE // CHUNK
FWD_WIN = (B_PREV + FWD_CHUNKS) * CHUNK


def attn_fwd(qkv_pad, bias):
    S = qkv_pad.shape[0] - PAD_ROWS

    def body(q_ref, k_ref, v_ref, b_ref, o_ref):
        n = pl.program_id(1)
        start = pl.multiple_of(n * FWD_TILE, FWD_TILE)
        kwin = k_ref[pl.ds(start, FWD_WIN), :]
        vwin = v_ref[pl.ds(start, FWD_WIN), :]
        qv = q_ref[...]
        pairs = [(qc, hh) for qc in range(FWD_CHUNKS) for hh in range(HEADS_PER_GROUP)]
        sl = lambda hh: slice(hh * B_DH, (hh + 1) * B_DH)
        s = [_dot1(_chunk_rows(qv, qc)[:, sl(hh)], _chunk_rows(kwin, qc, B_BAND)[:, sl(hh)], "nt") for qc, hh in pairs]
        s = [jnp.where(_band_valid(n * FWD_CHUNKS + qc), x * (B_DH ** -0.5) + b_ref[hh], NEG_INF)
             for x, (qc, hh) in zip(s, pairs)]
        e = [jnp.exp(x - jnp.max(x, axis=-1, keepdims=True)) for x in s]
        p = [x * (1.0 / jnp.sum(x, axis=-1, keepdims=True)) for x in e]
        o = [_dot1(x, _chunk_rows(vwin, qc, B_BAND)[:, sl(hh)], "nn") for x, (qc, hh) in zip(p, pairs)]
        rows = [jnp.concatenate(o[qc * HEADS_PER_GROUP:(qc + 1) * HEADS_PER_GROUP], axis=1) for qc in range(FWD_CHUNKS)]
        o_ref[...] = jnp.concatenate(rows, axis=0).astype(BF16)

    return pl.pallas_call(
        body, name="attn_fwd", grid=(N_GROUPS, S // FWD_TILE), in_specs=_attn_specs(S, FWD_TILE),
        out_specs=pl.BlockSpec((FWD_TILE, GROUP_W), lambda g, n: (n, g)),
        out_shape=jax.ShapeDtypeStruct((S, B_W), BF16),
        compiler_params=_cparams(("parallel", "arbitrary")),
    )(qkv_pad, qkv_pad, qkv_pad, bias)


EXT = B_BAND + CHUNK


def bias_expand(rel_bias):
    def body(rev_ref, o_ref):
        rev = rev_ref[...]
        erev = jnp.concatenate([jnp.broadcast_to(rev[:, 0:1], (B_HEADS, EXT - B_REL)), rev], axis=1)
        for i in range(CHUNK):
            o_ref[i] = erev[:, CHUNK - i:CHUNK - i + B_BAND]

    return pl.pallas_call(
        body, name="bias_expand", in_specs=[WHOLE_VMEM], out_specs=WHOLE_VMEM,
        out_shape=jax.ShapeDtypeStruct((CHUNK, B_HEADS, B_BAND), F32),
    )(jnp.flip(rel_bias, axis=1))


def bias_reduce(dbias):
    def body(d_ref, o_ref):
        acc = jnp.zeros((B_HEADS, EXT), F32)
        for i in range(CHUNK):
            acc = acc + jnp.pad(d_ref[i], ((0, 0), (CHUNK - i, i)))
        tail = acc[:, EXT - B_REL:]
        clipped = jnp.sum(acc[:, :EXT - B_REL], axis=1, keepdims=True)
        lane = lax.broadcasted_iota(jnp.int32, (B_HEADS, B_REL), 1)
        o_ref[...] = jnp.where(lane == 0, tail + clipped, tail)

    rev = pl.pallas_call(body, name="bias_reduce", in_specs=[WHOLE_VMEM], out_specs=WHOLE_VMEM,
                         out_shape=jax.ShapeDtypeStruct((B_HEADS, B_REL), F32))(dbias)
    return jnp.flip(rev, axis=1)


def merge_fwd(gates_raw, b_gate, ya, yb):
    S = ya.shape[0]

    def body(g_ref, b_ref, ya_ref, yb_ref, o_ref):
        gt = _sigmoid(g_ref[...].astype(F32) + b_ref[...])
        o_ref[...] = (gt[:, :D_MODEL] * ya_ref[...].astype(F32) + gt[:, D_MODEL:] * yb_ref[...].astype(F32)).astype(BF16)

    return rowcall(body, name="merge_fwd", S=S, ts=512,
                   ins=[(gates_raw, "row"), (b_gate, "vec"), (ya, "row"), (yb, "row")],
                   outs=[((S, D_MODEL), BF16, "row")])[0]


def _ln_stats(xpre):
    mu = jnp.mean(xpre, axis=-1, keepdims=True)
    xc = xpre - mu
    rstd = lax.rsqrt(jnp.mean(xc * xc, axis=-1, keepdims=True) + LN_EPS)
    return xc * rstd, rstd


def ln1_fwd(x, mix, mod, ln_g, ln_b):
    S = x.shape[0]

    def body(x_ref, mix_ref, m_ref, g_ref, b_ref, xpre_ref, x1_ref, h2_ref):
        m = m_ref[...]
        xpre = ALPHA * x_ref[...] + m[GATE_T:GATE_T + 1] * mix_ref[...]
        xhat, _ = _ln_stats(xpre)
        x1 = xhat * g_ref[...] + b_ref[...]
        xpre_ref[...] = xpre
        x1_ref[...] = x1
        h2_ref[...] = (x1 * (1.0 + m[SCALE_F:SCALE_F + 1]) + m[SHIFT_F:SHIFT_F + 1]).astype(BF16)

    return rowcall(body, name="ln1_fwd", S=S, ts=512,
                   ins=[(x, "row"), (mix, "row"), (mod, "vec"), (ln_g, "vec"), (ln_b, "vec")],
                   outs=[((S, D_MODEL), F32, "row"), ((S, D_MODEL), F32, "row"), ((S, D_MODEL), BF16, "row")])


STRIP_FWD = (64, 256)
STRIP_BWD = (128, 128)


def ffn_act_fwd(up, conv_w, conv_b):
    S = up.shape[0]
    ts = 256
    STRIP_ROWS, STRIP_COLS = STRIP_FWD

    def body(u_ref, up_ref, w_ref, b_ref, o_ref, ubuf):
        ubuf[0:8] = _halo_prev(up_ref) * (pl.program_id(0) > 0).astype(F32)
        ubuf[8:8 + ts] = u_ref[...].astype(F32)

        def col_block(j, carry):
            gate = pl.ds(pl.multiple_of(j * STRIP_COLS, STRIP_COLS), STRIP_COLS)
            halves = [gate, pl.ds(pl.multiple_of(D_FF + j * STRIP_COLS, STRIP_COLS), STRIP_COLS)]
            w = [w_ref[:, c] for c in halves]
            bias = [b_ref[:, c] for c in halves]
            for r0 in range(0, ts, STRIP_ROWS):
                uc = []
                for h in range(2):
                    x = ubuf[r0:r0 + STRIP_ROWS + 8, halves[h]]
                    uc.append(bias[h] + sum(
                        w[h][t:t + 1] * (x if t == FFN_CONV - 1 else pltpu.roll(x, FFN_CONV - 1 - t, axis=0))[8:]
                        for t in range(FFN_CONV)))
                o_ref[r0:r0 + STRIP_ROWS, gate] = (_silu(uc[0]) * uc[1]).astype(BF16)
            return carry

        lax.fori_loop(0, D_FF // STRIP_COLS, col_block, 0)

    return rowcall(body, name="ffn_act_fwd", S=S, ts=ts,
                   ins=[(up, "row"), (up, "prev"), (conv_w, "vec"), (conv_b, "vec")],
                   outs=[((S, D_FF), BF16, "row")], scratch=[pltpu.VMEM((ts + 8, 2 * D_FF), F32)])[0]


def final_fwd_bwd(x1, ffn, target, mod, ln_g, ln_b):
    S = x1.shape[0]

    def body(x1_ref, f_ref, t_ref, m_ref, g_ref, b_ref, dxpre_ref, dffn_ref, loss_ref, dgate_ref, dg_ref, db_ref):
        gate = m_ref[...][GATE_F:GATE_F + 1]
        ffn_v = f_ref[...]
        xpre = ALPHA * x1_ref[...] + gate * ffn_v
        xhat, rstd = _ln_stats(xpre)
        err = xhat * g_ref[...] + b_ref[...] - t_ref[...]
        loss_ref[...] += 0.5 * jnp.sum(jnp.mean(err * err, axis=-1, keepdims=True), axis=0, keepdims=True)
        dy = err * (1.0 / D_MODEL)
        dg_ref[...] += jnp.sum(dy * xhat, axis=0, keepdims=True)
        db_ref[...] += jnp.sum(dy, axis=0, keepdims=True)
        dyg = dy * g_ref[...]
        dxpre = rstd * (dyg - jnp.mean(dyg, axis=-1, keepdims=True) - xhat * jnp.mean(dyg * xhat, axis=-1, keepdims=True))
        dxpre_ref[...] = dxpre
        dffn_ref[...] = (gate * dxpre).astype(BF16)
        dgate_ref[...] += jnp.sum(dxpre * ffn_v, axis=0, keepdims=True)

    vec = ((1, D_MODEL), F32, "acc")
    return rowcall(body, name="final_fwd_bwd", S=S, ts=512,
                   ins=[(x1, "row"), (ffn, "row"), (target, "row"), (mod, "vec"), (ln_g, "vec"), (ln_b, "vec")],
                   outs=[((S, D_MODEL), F32, "row"), ((S, D_MODEL), BF16, "row"), ((1, 1), F32, "acc"), vec, vec, vec])


def ffn_act_bwd(dact, up, conv_w, conv_b):
    S = up.shape[0]
    ts = 256
    STRIP_ROWS, STRIP_COLS = STRIP_BWD
    win_u, win_d = STRIP_ROWS + 16, STRIP_ROWS + 8

    def body(d_ref, dn_ref, u_ref, up_ref, un_ref, w_ref, b_ref, dup_ref, dw_ref, db_ref, ubuf, dbuf):
        i = pl.program_id(0)
        ubuf[0:8] = _halo_prev(up_ref) * (i > 0).astype(F32)
        ubuf[8:8 + ts] = u_ref[...].astype(F32)
        ubuf[8 + ts:16 + ts] = _halo_next(un_ref)
        dbuf[0:ts] = d_ref[...].astype(F32)
        dbuf[ts:ts + 8] = _halo_next(dn_ref) * (i < pl.num_programs(0) - 1).astype(F32)

        def col_block(j, carry):
            halves = [pl.ds(pl.multiple_of(j * STRIP_COLS, STRIP_COLS), STRIP_COLS),
                      pl.ds(pl.multiple_of(D_FF + j * STRIP_COLS, STRIP_COLS), STRIP_COLS)]
            w = [w_ref[:, c] for c in halves]
            bias = [b_ref[:, c] for c in halves]
            dw_acc = [[jnp.zeros((1, STRIP_COLS), F32) for _ in range(FFN_CONV)] for _ in halves]
            db_acc = [jnp.zeros((1, STRIP_COLS), F32) for _ in halves]
            for r0 in range(0, ts, STRIP_ROWS):
                shifted = [[x if k == 0 else pltpu.roll(x, k, axis=0) for k in range(FFN_CONV)]
                           for x in (ubuf[r0:r0 + win_u, c] for c in halves)]
                uc = [bias[h] + sum(w[h][t:t + 1] * shifted[h][FFN_CONV - 1 - t][8:8 + win_d] for t in range(FFN_CONV))
                      for h in range(2)]
                dact_w = dbuf[r0:r0 + win_d, halves[0]]
                sg, dsg = _silu_and_grad(uc[0])
                duc = [dact_w * uc[1] * dsg, dact_w * sg]
                for h in range(2):
                    dup = duc[h] * w[h][FFN_CONV - 1:FFN_CONV]
                    for t in range(FFN_CONV - 1):
                        dup = dup + pltpu.roll(duc[h], win_d - (FFN_CONV - 1 - t), axis=0) * w[h][t:t + 1]
                    dup_ref[r0:r0 + STRIP_ROWS, halves[h]] = dup[:STRIP_ROWS].astype(BF16)
                    mine = duc[h][:STRIP_ROWS]
                    db_acc[h] = db_acc[h] + jnp.sum(mine, axis=0, keepdims=True)
                    for t in range(FFN_CONV):
                        dw_acc[h][t] = dw_acc[h][t] + jnp.sum(
                            mine * shifted[h][FFN_CONV - 1 - t][8:8 + STRIP_ROWS], axis=0, keepdims=True)
            for h in range(2):
                dw_ref[:, halves[h]] += jnp.concatenate(dw_acc[h], axis=0)
                db_ref[:, halves[h]] += db_acc[h]
            return carry

        lax.fori_loop(0, D_FF // STRIP_COLS, col_block, 0)

    return rowcall(body, name="ffn_act_bwd", S=S, ts=ts,
                   ins=[(dact, "row"), (dact, "next"), (up, "row"), (up, "prev"), (up, "next"), (conv_w, "vec"), (conv_b, "vec")],
                   outs=[((S, 2 * D_FF), BF16, "row"), ((FFN_CONV, 2 * D_FF), F32, "acc"), ((1, 2 * D_FF), F32, "acc")],
                   scratch=[pltpu.VMEM((ts + 16, 2 * D_FF), F32), pltpu.VMEM((ts + 8, D_FF), F32)])


def ln1_bwd(dxpre2, dh2, xpre1, mix, mod, ln_g, ln_b):
    S = xpre1.shape[0]

    def body(d2_ref, dh_ref, xp_ref, mix_ref, m_ref, g_ref, b_ref, dxpre_ref, dmix_ref,
             dscale_ref, dshift_ref, dgate_ref, dg_ref, db_ref):
        m = m_ref[...]
        xhat, rstd = _ln_stats(xp_ref[...])
        x1 = xhat * g_ref[...] + b_ref[...]
        dh = dh_ref[...]
        dx1 = ALPHA * d2_ref[...] + dh * (1.0 + m[SCALE_F:SCALE_F + 1])
        dscale_ref[...] += jnp.sum(dh * x1, axis=0, keepdims=True)
        dshift_ref[...] += jnp.sum(dh, axis=0, keepdims=True)
        dg_ref[...] += jnp.sum(dx1 * xhat, axis=0, keepdims=True)
        db_ref[...] += jnp.sum(dx1, axis=0, keepdims=True)
        dyg = dx1 * g_ref[...]
        dxpre = rstd * (dyg - jnp.mean(dyg, axis=-1, keepdims=True) - xhat * jnp.mean(dyg * xhat, axis=-1, keepdims=True))
        dxpre_ref[...] = dxpre
        dmix_ref[...] = (m[GATE_T:GATE_T + 1] * dxpre).astype(BF16)
        dgate_ref[...] += jnp.sum(dxpre * mix_ref[...], axis=0, keepdims=True)

    vec = ((1, D_MODEL), F32, "acc")
    return rowcall(body, name="ln1_bwd", S=S, ts=512,
                   ins=[(dxpre2, "row"), (dh2, "row"), (xpre1, "row"), (mix, "row"), (mod, "vec"), (ln_g, "vec"), (ln_b, "vec")],
                   outs=[((S, D_MODEL), F32, "row"), ((S, D_MODEL), BF16, "row"), vec, vec, vec, vec, vec])


def merge_bwd(dmerged, gates_raw, b_gate, ya, yb):
    S = ya.shape[0]

    def body(d_ref, g_ref, b_ref, ya_ref, yb_ref, dya_ref, dyb_ref, dg_ref, dbg_ref):
        gt = _sigmoid(g_ref[...].astype(F32) + b_ref[...])
        d = d_ref[...].astype(F32)
        ga, gb = gt[:, :D_MODEL], gt[:, D_MODEL:]
        dya_ref[...] = (d * ga).astype(BF16)
        dyb_ref[...] = (d * gb).astype(BF16)
        dgr = jnp.concatenate([d * ya_ref[...].astype(F32) * ga * (1.0 - ga),
                               d * yb_ref[...].astype(F32) * gb * (1.0 - gb)], axis=1)
        dg_ref[...] = dgr.astype(BF16)
        dbg_ref[...] += jnp.sum(dgr, axis=0, keepdims=True)

    return rowcall(body, name="merge_bwd", S=S, ts=512,
                   ins=[(dmerged, "row"), (gates_raw, "row"), (b_gate, "vec"), (ya, "row"), (yb, "row")],
                   outs=[((S, D_MODEL), BF16, "row"), ((S, D_MODEL), BF16, "row"), ((S, 2 * D_MODEL), BF16, "row"),
                         ((1, 2 * D_MODEL), F32, "acc")])


def attn_bwd(qkv_pad, bias, do_b):
    S = qkv_pad.shape[0] - PAD_ROWS

    def body(q_ref, k_ref, v_ref, bias_ref, do_ref, dq_ref, dk_ref, dv_ref, db_ref, b_ref):
        n = pl.program_id(1)

        @pl.when(n == 0)
        def _():
            dk_ref[...] = jnp.zeros_like(dk_ref)
            dv_ref[...] = jnp.zeros_like(dv_ref)
            db_ref[...] = jnp.zeros_like(db_ref)
            b_ref[...] = jnp.full(b_ref.shape, NEG_INF, F32)
            for hh in range(HEADS_PER_GROUP):
                for qc in range(Q_CHUNKS):
                    b_ref[hh, qc * CHUNK:(qc + 1) * CHUNK, qc * CHUNK:qc * CHUNK + B_BAND] = bias_ref[hh]

        start = pl.multiple_of(n * Q_TILE, Q_TILE)
        kwin = k_ref[pl.ds(start, KEY_WIN), :]
        vwin = v_ref[pl.ds(start, KEY_WIN), :]
        qv, dov = q_ref[...], do_ref[...]
        valid = lax.broadcasted_iota(jnp.int32, (Q_TILE, KEY_WIN), 1) >= PAD_ROWS - n * Q_TILE
        dqs, dks, dvs = [], [], []
        for hh in range(HEADS_PER_GROUP):
            sl = slice(hh * B_DH, (hh + 1) * B_DH)
            p = _band_probs(qv[:, sl], kwin[:, sl], b_ref[hh], valid)
            dp = _dot1(dov[:, sl], vwin[:, sl], "nt")
            ds = p * (dp - jnp.sum(dp * p, axis=-1, keepdims=True))
            dbh = ds[0:CHUNK, 0:B_BAND]
            for qc in range(1, Q_CHUNKS):
                dbh = dbh + ds[qc * CHUNK:(qc + 1) * CHUNK, qc * CHUNK:qc * CHUNK + B_BAND]
            db_ref[hh] += dbh
            dsq = ds * (B_DH ** -0.5)
            dqs.append(_dot1(dsq, kwin[:, sl], "nn"))
            dks.append(_dot1(dsq, qv[:, sl], "tn"))
            dvs.append(_dot1(p, dov[:, sl], "tn"))
        dq_ref[...] = jnp.concatenate(dqs, axis=1).astype(BF16)
        dk_ref[pl.ds(start, KEY_WIN), :] += jnp.concatenate(dks, axis=1)
        dv_ref[pl.ds(start, KEY_WIN), :] += jnp.concatenate(dvs, axis=1)

    col = pl.BlockSpec((PAD_ROWS + S, GROUP_W), lambda g, n: (0, g))
    tile = pl.BlockSpec((Q_TILE, GROUP_W), lambda g, n: (n, g))
    return pl.pallas_call(
        body, name="attn_bwd", grid=(N_GROUPS, S // Q_TILE), in_specs=_attn_specs(S, Q_TILE) + [tile],
        out_specs=[tile, col, col, pl.BlockSpec((HEADS_PER_GROUP, CHUNK, B_BAND), lambda g, n: (g, 0, 0))],
        out_shape=[jax.ShapeDtypeStruct((S, B_W), BF16), jax.ShapeDtypeStruct((PAD_ROWS + S, B_W), F32),
                   jax.ShapeDtypeStruct((PAD_ROWS + S, B_W), F32), jax.ShapeDtypeStruct((B_HEADS, CHUNK, B_BAND), F32)],
        scratch_shapes=[pltpu.VMEM((HEADS_PER_GROUP, Q_TILE, KEY_WIN), F32)],
        compiler_params=_cparams(("parallel", "arbitrary")),
    )(qkv_pad, qkv_pad, qkv_pad, bias, do_b)


def gate_a_bwd(do_a, o_pre, z, norm_w):
    S = o_pre.shape[0]

    def body(d_ref, o_ref, z_ref, nw_ref, dop_ref, dz_ref, dnw_ref):
        nw = nw_ref[...]
        acc = jnp.zeros((1, A_DK), F32)
        for h in range(A_HEADS):
            sl = slice(h * A_DK, (h + 1) * A_DK)
            oh, zh, dh = o_ref[:, sl], z_ref[:, sl].astype(F32), d_ref[:, sl].astype(F32)
            r = lax.rsqrt(jnp.mean(oh * oh, axis=-1, keepdims=True) + RMS_EPS)
            sz, dsz = _silu_and_grad(zh)
            dz_ref[:, sl] = (dh * oh * r * nw * dsz).astype(BF16)
            acc = acc + jnp.sum(dh * oh * r * sz, axis=0, keepdims=True)
            t = dh * nw * sz
            dop_ref[:, sl] = r * t - oh * (r * r * r) * jnp.mean(t * oh, axis=-1, keepdims=True)
        dnw_ref[...] += acc

    return rowcall(body, name="gate_a_bwd", S=S, ts=512,
                   ins=[(do_a, "row"), (o_pre, "row"), (z, "row"), (norm_w, "vec")],
                   outs=[((S, A_W), F32, "row"), ((S, A_W), BF16, "row"), ((1, A_DK), F32, "acc")])


def delta_bwd(q, k, v, beta, g, sprev, tinv, do):
    S = q.shape[0]
    n_chunks = S // CHUNK

    def body(q_ref, k_ref, v_ref, beta_ref, g_ref, sprev_ref, t_ref, do_ref,
             dq_ref, dk_ref, dv_ref, dbeta_ref, dg_ref, dstate_ref):
        @pl.when(pl.program_id(0) == 0)
        def _():
            dstate_ref[...] = jnp.zeros_like(dstate_ref)

        mk = _tri_masks()
        causal, strict, eye = mk["causal"], mk["strict"], mk["eye"]
        blk_end = (lax.broadcasted_iota(jnp.int32, (GROUP_ROWS, 1), 0) & (CHUNK - 1)) == CHUNK - 1
        lane = lax.broadcasted_iota(jnp.int32, (CHUNK, A_HEADS), 1)
        betav, gv = beta_ref[...], g_ref[...]
        dbeta_t = jnp.zeros((CHUNK, A_HEADS), F32)
        dg_t = jnp.zeros((CHUNK, A_HEADS), F32)
        groups, heads = range(N_HEAD_GROUPS), range(HEAD_GROUP)
        st = [dict() for _ in groups]

        def local_part(grp, s):
            s["qs"], s["ks"], s["vs"] = _stack_heads(q_ref, grp), _stack_heads(k_ref, grp), _stack_heads(v_ref, grp)
            s["dos"] = _stack_heads(do_ref, grp)
            s["bs"] = _stack_cols(betav, grp)
            s["loc"] = loc = _delta_local(s["qs"], s["ks"], s["vs"], s["bs"], _stack_cols(gv, grp), mk)
            s["tinv"] = t_ref[0, grp]
            s["rhs"] = jnp.concatenate([loc["vb"], loc["y"]], axis=1)
            s["uw"] = _dot3(s["tinv"], s["rhs"], "nn")

        def state_part(grp, s):
            loc, uw, dos, qs = s["loc"], s["uw"], s["dos"], s["qs"]
            gam, kd, gl, gc = loc["gam"], loc["kd"], loc["gl"], loc["gc"]
            qg = qs * gam
            egl = jnp.exp(gl)
            hid = [grp * HEAD_GROUP + j for j in heads]
            s0 = [sprev_ref[0, h] for h in hid]
            ds1 = [dstate_ref[h] for h in hid]
            w = [_head_rows(uw, j)[:, A_DK:] for j in heads]
            vn = [_head_rows(uw, j)[:, :A_DK] - _dot1(w[j], s0[j], "nn") for j in heads]
            vns = jnp.concatenate(vn, axis=0)
            dvn_local = _dot1(loc["p"], dos, "tn")
            dvn = [_head_rows(dvn_local, j) + _dot1(_head_rows(kd, j), ds1[j], "nn") for j in heads]
            dvns = jnp.concatenate(dvn, axis=0)
            s["dp"] = jnp.where(causal, _dot1(dos, vns, "nt"), 0.0)
            dqg = jnp.concatenate([_dot1(_head_rows(dos, j), s0[j], "nt") for j in heads], axis=0)
            s["dq"] = dqg * gam
            dgc = jnp.sum(dqg * qg, axis=-1, keepdims=True)
            for j in heads:
                dstate_ref[hid[j]] = (_dot1(_head_rows(qg, j), _head_rows(dos, j), "tn")
                                      + egl[(j + 1) * CHUNK - 1:(j + 1) * CHUNK] * ds1[j] - _dot1(w[j], dvn[j], "tn"))
            dkd = jnp.concatenate([_dot1(vn[j], ds1[j], "nt") for j in heads], axis=0)
            s["dk"] = dkd * jnp.exp(gl - gc)
            t1 = jnp.sum(dkd * kd, axis=-1, keepdims=True)
            dgl = jnp.concatenate(
                [jnp.broadcast_to(jnp.sum(_head_rows(t1, j), axis=0, keepdims=True)
                                  + jnp.sum(jnp.sum(ds1[j] * s0[j], axis=-1, keepdims=True), axis=0, keepdims=True)
                                  * egl[(j + 1) * CHUNK - 1:(j + 1) * CHUNK], (CHUNK, 1)) for j in heads], axis=0)
            s["dgc"] = dgc - t1 + jnp.where(blk_end, dgl, 0.0)
            s["duw"] = jnp.concatenate(
                [dvns, jnp.concatenate([-_dot1(dvn[j], s0[j], "nt") for j in heads], axis=0)], axis=1)

        def solve_part(grp, s):
            s["dvby"] = _dot3(s["tinv"], s["duw"], "tn")
            s["dt"] = _dot3(s["duw"], s["rhs"], "nt")

        def inverse_part_a(grp, s):
            s["tdt"] = _dot3(s["tinv"], s["dt"], "tn")

        def inverse_part_b(grp, s):
            s["da"] = jnp.where(strict, -_dot3(s["tdt"], s["tinv"], "nt"), 0.0)

        def finish(grp, s):
            loc, qs, ks, vs, bs, da, dp, dvby = s["loc"], s["qs"], s["ks"], s["vs"], s["bs"], s["da"], s["dp"], s["dvby"]
            gam, decay = loc["gam"], loc["decay"]
            dm = da * decay
            dn = dp * decay
            e = da * loc["a"] + dp * loc["p"]
            dgc = s["dgc"] + jnp.sum(e, axis=1, keepdims=True) - _row_to_col(jnp.sum(e, axis=0, keepdims=True), eye)
            dy = dvby[:, A_DK:]
            dvb = dvby[:, :A_DK]
            dkb = _dot1(dm, ks, "nn") + dy * gam
            dk = s["dk"] + _dot1(dm, loc["kb"], "tn") + _dot1(dn, qs, "tn") + dkb * bs
            dq = s["dq"] + _dot1(dn, ks, "nn")
            dgc = dgc + jnp.sum(dy * loc["y"], axis=-1, keepdims=True)
            dbeta = jnp.sum(dkb * ks, axis=-1, keepdims=True) + jnp.sum(dvb * vs, axis=-1, keepdims=True)
            dv = dvb * bs
            dgs = jnp.sum(jnp.where(mk["upper"], _col_to_row(dgc, eye), 0.0), axis=1, keepdims=True)
            for j in heads:
                h = grp * HEAD_GROUP + j
                sl = slice(h * A_DK, (h + 1) * A_DK)
                dq_ref[:, sl] = _head_rows(dq, j)
                dk_ref[:, sl] = _head_rows(dk, j)
                dv_ref[:, sl] = _head_rows(dv, j)
            s["dbeta"], s["dgs"] = dbeta, dgs

        for stage in (local_part, state_part, solve_part, inverse_part_a, inverse_part_b, finish):
            for grp in groups:
                stage(grp, st[grp])
        for grp in groups:
            for j in heads:
                h = grp * HEAD_GROUP + j
                dbeta_t = dbeta_t + jnp.where(lane == h, _head_rows(st[grp]["dbeta"], j), 0.0)
                dg_t = dg_t + jnp.where(lane == h, _head_rows(st[grp]["dgs"], j), 0.0)
        dbeta_ref[...] = dbeta_t
        dg_ref[...] = dg_t

    rev = lambda n: (n_chunks - 1 - n, 0)
    rev4 = lambda n: (n_chunks - 1 - n, 0, 0, 0)
    tile = pl.BlockSpec((CHUNK, A_W), rev)
    small = pl.BlockSpec((CHUNK, A_HEADS), rev)
    return pl.pallas_call(
        body, name="delta_bwd", grid=(n_chunks,),
        in_specs=[tile, tile, tile, small, small, pl.BlockSpec((1, A_HEADS, A_DK, A_DK), rev4),
                  pl.BlockSpec((1, N_HEAD_GROUPS, GROUP_ROWS, GROUP_ROWS), rev4), tile],
        out_specs=[tile, tile, tile, small, small],
        out_shape=[jax.ShapeDtypeStruct((S, A_W), F32)] * 3 + [jax.ShapeDtypeStruct((S, A_HEADS), F32)] * 2,
        scratch_shapes=[pltpu.VMEM((A_HEADS, A_DK, A_DK), F32)],
        compiler_params=_cparams(("arbitrary",)),
    )(q, k, v, beta, g, sprev, tinv, do)


def _prep_a_dpre(raw, raw_prev, w, dq, dk, dv):
    y, dy_dpre = _prep_a_core(raw, raw_prev, w)
    parts = []
    for h in range(A_HEADS):
        yq = y[:, h * A_DK:(h + 1) * A_DK]
        dqh = dq[:, h * A_DK:(h + 1) * A_DK]
        rq = lax.rsqrt(jnp.sum(yq * yq, axis=-1, keepdims=True) + L2_EPS)
        parts.append((A_DK ** -0.5) * (rq * dqh - yq * (rq * rq * rq) * jnp.sum(dqh * yq, axis=-1, keepdims=True)))
    for h in range(A_HEADS):
        yk = y[:, A_W + h * A_DK:A_W + (h + 1) * A_DK]
        dkh = dk[:, h * A_DK:(h + 1) * A_DK]
        rk = lax.rsqrt(jnp.sum(yk * yk, axis=-1, keepdims=True) + L2_EPS)
        parts.append(rk * dkh - yk * (rk * rk * rk) * jnp.sum(dkh * yk, axis=-1, keepdims=True))
    parts.append(dv)
    return jnp.concatenate(parts, axis=1) * dy_dpre


def prep_a_bwd(qkv_raw, ba, conv_a, a_log, dt_bias, dq, dk, dv, dbeta, dg):
    S = qkv_raw.shape[0]
    ts = 256

    def body(x_ref, xp_ref, xn_ref, ba_ref, w_ref, al_ref, dt_ref, dq_ref, dqn_ref, dk_ref, dkn_ref, dv_ref, dvn_ref,
             dbeta_ref, dg_ref, draw_ref, dba_ref, dw_ref, dal_ref, ddt_ref):
        i = pl.program_id(0)
        first = (i > 0).astype(F32)
        last = (i < pl.num_programs(0) - 1).astype(F32)
        w = w_ref[...]
        cur, prev = x_ref[...].astype(F32), _halo_prev(xp_ref) * first
        dpre = _prep_a_dpre(cur, prev, w, dq_ref[...], dk_ref[...], dv_ref[...])
        dpre_n = _prep_a_dpre(_halo_next(xn_ref), cur[ts - 8:ts], w, _halo_next(dqn_ref), _halo_next(dkn_ref),
                              _halo_next(dvn_ref)) * last
        for j in range(A_CONV):
            dw_ref[j:j + 1, :] += jnp.sum(dpre * _shift_down(cur, prev, A_CONV - 1 - j), axis=0, keepdims=True)
        draw = dpre * w[A_CONV - 1:A_CONV]
        for j in range(A_CONV - 1):
            draw = draw + _shift_up(dpre, dpre_n, A_CONV - 1 - j) * w[j:j + 1]
        draw_ref[...] = draw.astype(BF16)
        bav = ba_ref[...]
        beta = _sigmoid(bav[:, 0:A_HEADS])
        xa = bav[:, A_HEADS:2 * A_HEADS] + dt_ref[...]
        nexp = -jnp.exp(al_ref[...])
        dgv = dg_ref[...]
        da = dgv * nexp * _sigmoid(xa)
        dba_ref[:, 0:A_HEADS] = dbeta_ref[...] * beta * (1.0 - beta)
        dba_ref[:, A_HEADS:2 * A_HEADS] = da
        dal_ref[...] += jnp.sum(dgv * nexp * _softplus(xa), axis=0, keepdims=True)
        ddt_ref[...] += jnp.sum(da, axis=0, keepdims=True)

    return rowcall(
        body, name="prep_a_bwd", S=S, ts=ts,
        ins=[(qkv_raw, "row"), (qkv_raw, "prev"), (qkv_raw, "next"), (ba, "row"), (conv_a, "vec"), (a_log, "vec"),
             (dt_bias, "vec"), (dq, "row"), (dq, "next"), (dk, "row"), (dk, "next"), (dv, "row"), (dv, "next"),
             (dbeta, "row"), (dg, "row")],
        outs=[((S, 3 * A_W), BF16, "row"), ((S, 2 * A_HEADS), F32, "row"), ((A_CONV, 3 * A_W), F32, "acc"),
              ((1, A_HEADS), F32, "acc"), ((1, A_HEADS), F32, "acc")])


def grad_x_final(dh1, x, dxpre1, mod):
    S = x.shape[0]

    def body(dh_ref, x_ref, dx_ref, m_ref, gx_ref, dscale_ref, dshift_ref):
        dh = dh_ref[...]
        gx_ref[...] = ALPHA * dx_ref[...] + dh * (1.0 + m_ref[...][SCALE_T:SCALE_T + 1])
        dscale_ref[...] += jnp.sum(dh * x_ref[...], axis=0, keepdims=True)
        dshift_ref[...] += jnp.sum(dh, axis=0, keepdims=True)

    vec = ((1, D_MODEL), F32, "acc")
    return rowcall(body, name="grad_x_final", S=S, ts=512, ins=[(dh1, "row"), (x, "row"), (dxpre1, "row"), (mod, "vec")],
                   outs=[((S, D_MODEL), F32, "row"), vec, vec])


_C_QKV, _C_Z, _C_BA, _C_QKVB, _C_G = 0, 3 * A_W, 4 * A_W, 4 * A_W + 2 * A_HEADS, 4 * A_W + 2 * A_HEADS + 3 * B_W
BA_PAD = 128


def split_w_in(w_in):
    ba = jnp.pad(w_in[:, _C_BA:_C_QKVB], ((0, 0), (0, BA_PAD - 2 * A_HEADS)))
    return dict(qkv=w_in[:, _C_QKV:_C_Z], z=w_in[:, _C_Z:_C_BA], ba=ba, qkvb=w_in[:, _C_QKVB:_C_G], g=w_in[:, _C_G:])


def join_w_in(p):
    return jnp.concatenate([p["qkv"], p["z"], p["ba"][:, :2 * A_HEADS], p["qkvb"], p["g"]], axis=1)


def forward_local(x, target, mod, w, sm, late_weights=None):
    h1 = modulate(x, mod, SHIFT_T, SCALE_T, "mod_t")
    qkv_raw = mm(h1, w["qkv"], mode="nn", out_dtype=BF16, name="proj_qkv")
    z = mm(h1, w["z"], mode="nn", out_dtype=BF16, name="proj_z")
    ba = mm(h1, w["ba"], mode="nn", out_dtype=F32, name="proj_ba")
    qkvb = mm(h1, w["qkvb"], mode="nn", out_dtype=BF16, name="proj_qkvb")
    gates_raw = mm(h1, w["g"], mode="nn", out_dtype=BF16, name="proj_g")
    q, k, v, beta, g = prep_a_fwd(qkv_raw, ba, sm["conv_a"], sm["a_log"], sm["dt_bias"])
    o_pre, sprev, tinv = delta_fwd(q, k, v, beta, g)
    o_a = gate_a_fwd(o_pre, z, sm["norm_a"])
    qkv_pad = jnp.pad(qkvb, ((PAD_ROWS, 0), (0, 0)))
    bias = jnp.transpose(bias_expand(sm["rel_bias"]), (1, 0, 2))
    o_b = attn_fwd(qkv_pad, bias)
    if late_weights is not None:
        w = dict(w, **late_weights(o_b))
    ya = mm(o_a, w["branch_a"], mode="nn", out_dtype=BF16, name="branch_a")
    yb = mm(o_b, w["branch_b"], mode="nn", out_dtype=BF16, name="branch_b")
    merged = merge_fwd(gates_raw, sm["b_gate"], ya, yb)
    mix = mm(merged, w["o"], mode="nn", out_dtype=F32, name="mix")
    xpre1, x1, h2 = ln1_fwd(x, mix, mod, sm["ln1_g"], sm["ln1_b"])
    up = mm(h2, w["up"], mode="nn", out_dtype=BF16, name="ffn_up", b_shards=True)
    act = ffn_act_fwd(up, sm["conv_ffn"], sm["b_conv_ffn"])
    ffn = mm(act, w["down"], mode="nn", out_dtype=F32, name="ffn_down")
    dxpre2, dffn, loss, dgate_f, dln2_g, dln2_b = final_fwd_bwd(x1, ffn, target, mod, sm["ln2_g"], sm["ln2_b"])
    saved = dict(h1=h1, qkv_raw=qkv_raw, z=z, ba=ba, gates_raw=gates_raw, q=q, k=k, v=v, beta=beta, g=g,
                 o_pre=o_pre, sprev=sprev, tinv=tinv, o_a=o_a, qkv_pad=qkv_pad, bias=bias, o_b=o_b, ya=ya, yb=yb,
                 merged=merged, mix=mix, xpre1=xpre1, x1=x1, h2=h2, up=up, act=act, ffn=ffn, w=w)
    return loss, dxpre2, dffn, dict(gate_f=dgate_f, ln2_g=dln2_g, ln2_b=dln2_b), saved


def backward_local(x, mod, sm, dxpre2, dffn, fin, sv, hooks=None):
    w = sv["w"]
    dact = mm(dffn, w["down"], mode="nt", out_dtype=BF16, name="d_act")
    gw_down = mm(sv["act"], dffn, mode="tn", out_dtype=BF16, name="gw_down")
    dup, dconv_ffn, db_conv_ffn = ffn_act_bwd(dact, sv["up"], sm["conv_ffn"], sm["b_conv_ffn"])
    dh2 = mm(dup, w["up"], mode="nt", out_dtype=F32, name="d_h2", b_shards=True)
    gw_up = mm(sv["h2"], dup, mode="tn", out_dtype=BF16, name="gw_up", out_shards=N_CHIPS)
    dxpre1, dmix, dsc_f, dsh_f, dgate_t, dln1_g, dln1_b = ln1_bwd(
        dxpre2, dh2, sv["xpre1"], sv["mix"], mod, sm["ln1_g"], sm["ln1_b"])
    dmerged = mm(dmix, w["o"], mode="nt", out_dtype=BF16, name="d_merged")
    gw_o = mm(sv["merged"], dmix, mode="tn", out_dtype=BF16, name="gw_o")
    dya, dyb, dgates, db_gate = merge_bwd(dmerged, sv["gates_raw"], sm["b_gate"], sv["ya"], sv["yb"])
    do_a = mm(dya, w["branch_a"], mode="nt", out_dtype=BF16, name="d_oa")
    gw_branch_a = mm(sv["o_a"], dya, mode="tn", out_dtype=BF16, name="gw_branch_a")
    do_b = mm(dyb, w["branch_b"], mode="nt", out_dtype=BF16, name="d_ob")
    gw_branch_b = mm(sv["o_b"], dyb, mode="tn", out_dtype=BF16, name="gw_branch_b")
    bias = sv["bias"]
    if hooks is not None:
        bias = bias + hooks["late_start"](dict(w_branch_a=gw_branch_a, w_branch_b=gw_branch_b, w_o=gw_o, w_up=gw_up,
                                               w_down=gw_down))[0, 0]
    dq_b, dk_pad, dv_pad, dbias = attn_bwd(sv["qkv_pad"], bias, do_b)
    if hooks is not None:
        dbias = dbias + hooks["late_finish"](dq_b)[0, 0]
    dqkvb = jnp.concatenate([dq_b, dk_pad[PAD_ROWS:].astype(BF16), dv_pad[PAD_ROWS:].astype(BF16)], axis=1)
    drel_bias = bias_reduce(jnp.transpose(dbias, (1, 0, 2)))
    do_pre, dz, dnorm_a = gate_a_bwd(do_a, sv["o_pre"], sv["z"], sm["norm_a"])
    dq, dk, dv, dbeta, dg = delta_bwd(sv["q"], sv["k"], sv["v"], sv["beta"], sv["g"], sv["sprev"], sv["tinv"], do_pre)
    dqkv_raw, dba16, dconv_a, da_log, ddt_bias = prep_a_bwd(
        sv["qkv_raw"], sv["ba"], sm["conv_a"], sm["a_log"], sm["dt_bias"], dq, dk, dv, dbeta, dg)
    dba = jnp.pad(dba16, ((0, 0), (0, BA_PAD - 2 * A_HEADS))).astype(BF16)
    pieces = dict(qkv=dqkv_raw, z=dz, ba=dba, qkvb=dqkvb, g=dgates)
    gw_in = join_w_in({key: mm(sv["h1"], dpiece, mode="tn", out_dtype=BF16, name="gw_in_" + key)
                       for key, dpiece in pieces.items()})
    w_ba = w["ba"]
    w_z = w["z"]
    if hooks is not None:
        w_ba = w_ba + hooks["w_in_start"](gw_in)[0, 0].astype(BF16)
    dh1 = mm(pieces["ba"], w_ba, mode="nt", out_dtype=F32, name="d_h1_ba")
    dh1 = mm(pieces["qkv"], w["qkv"], mode="nt", out_dtype=F32, name="d_h1_qkv", acc_in=dh1)
    if hooks is not None:
        w_z = w_z + hooks["w_in_finish"](dh1)[0, 0].astype(BF16)
    dh1 = mm(pieces["z"], w_z, mode="nt", out_dtype=F32, name="d_h1_z", acc_in=dh1)
    for key in ("qkvb", "g"):
        dh1 = mm(pieces[key], w[key], mode="nt", out_dtype=F32, name="d_h1_" + key, acc_in=dh1)
    grad_x, dsc_t, dsh_t = grad_x_final(dh1, x, dxpre1, mod)
    dmod = jnp.concatenate([dsh_t, dsc_t, dgate_t, dsh_f, dsc_f, fin["gate_f"]], axis=0)
    gw = dict(w_in=gw_in, w_branch_a=gw_branch_a, w_branch_b=gw_branch_b, w_o=gw_o, w_up=gw_up, w_down=gw_down)
    gs = dict(b_gate=db_gate, conv_a=dconv_a, a_log=da_log, dt_bias=ddt_bias, norm_a=dnorm_a, rel_bias=drel_bias,
              ln1_g=dln1_g, ln1_b=dln1_b, conv_ffn=dconv_ffn, b_conv_ffn=db_conv_ffn, ln2_g=fin["ln2_g"], ln2_b=fin["ln2_b"])
    return grad_x, dmod, gw, gs


MESH = pl.DeviceIdType.MESH
ANY = pl.BlockSpec(memory_space=pl.ANY)
WHOLE_VMEM = pl.BlockSpec(memory_space=pltpu.VMEM)


def _place():
    return lax.axis_index("x"), lax.axis_index("y"), lax.axis_index("c")


def allgather8(blk, name):
    m_per, n = blk.shape

    def body(x_ref, out_ref, send_sems, recv_sems, local_sem):
        x, y, c = _place()
        me, sibling = (x, y, c), (x, y, 1 - c)
        chips = [(1 - x, y), (x, 1 - y), (1 - x, 1 - y)]

        def rows(px, py, pc):
            return out_ref.at[pl.ds((4 * px + 2 * py + pc) * m_per, m_per), :]

        def copy(k, block, to, src=None):
            return pltpu.make_async_remote_copy(
                src_ref=rows(*block) if src is None else src, dst_ref=rows(*block),
                send_sem=send_sems.at[k], recv_sem=recv_sems.at[k], device_id=to, device_id_type=MESH)

        mine = pltpu.make_async_copy(x_ref, rows(*me), local_sem)
        mine.start()
        first = [copy(0, me, sibling, src=x_ref)]
        first += [copy(1 + j, me, (*chip, c), src=x_ref) for j, chip in enumerate(chips)]
        for cp in first:
            cp.start()
        passed = [copy(4 + j, (*chip, c), sibling) for j, chip in enumerate(chips)]
        for j, chip in enumerate(chips):
            copy(1 + j, (*chip, c), me).wait_recv()
            passed[j].start()
        copy(0, sibling, me).wait_recv()
        for j, chip in enumerate(chips):
            copy(4 + j, (*chip, 1 - c), me).wait_recv()
        for cp in first + passed:
            cp.wait_send()
        mine.wait()

    return pl.pallas_call(
        body, name=name, out_shape=jax.ShapeDtypeStruct((N_DEV * m_per, n), blk.dtype),
        in_specs=[WHOLE_VMEM], out_specs=WHOLE_VMEM,
        scratch_shapes=[pltpu.SemaphoreType.DMA((7,)), pltpu.SemaphoreType.DMA((7,)), pltpu.SemaphoreType.DMA],
    )(blk)


def _chip_peers(x, y):
    return [(1 - x, y), (x, 1 - y), (1 - x, 1 - y)]


def chip_exchange(arrs, name, scatter):
    n = len(arrs)

    def body(*refs):
        ins, outs = refs[:n], refs[n:2 * n]
        send_sems, recv_sems, local_sems = refs[2 * n:]
        x, y, c = _place()
        me = 2 * x + y
        sibling = (x, y, 1 - c)
        peers = _chip_peers(x, y)

        def half(ref, which):
            r2 = ref.shape[0] // 2
            return ref.at[pl.ds(which * r2, r2), :]

        def outgoing(a, chip):
            return ins[a].at[chip] if scatter else ins[a]

        def copy(k, src, dst, to):
            return pltpu.make_async_remote_copy(src_ref=src, dst_ref=dst, send_sem=send_sems.at[k],
                                                recv_sem=recv_sems.at[k], device_id=to, device_id_type=MESH)

        started, local = [], []
        for a in range(n):
            lc = pltpu.make_async_copy(outgoing(a, me), outs[a].at[me], local_sems.at[a])
            lc.start()
            local.append(lc)
            for j, (px, py) in enumerate(peers):
                cp = copy(6 * a + j, half(outgoing(a, 2 * px + py), c), half(outs[a].at[me], c), (px, py, c))
                cp.start()
                started.append(cp)
        for a in range(n):
            for j, (px, py) in enumerate(peers):
                landed = half(outs[a].at[2 * px + py], c)
                copy(6 * a + j, landed, landed, (px, py, c)).wait_recv()
                relay = copy(6 * a + 3 + j, landed, landed, sibling)
                relay.start()
                started.append(relay)
        for a in range(n):
            for j, (px, py) in enumerate(peers):
                other = half(outs[a].at[2 * px + py], 1 - c)
                copy(6 * a + 3 + j, other, other, sibling).wait_recv()
        for cp in started:
            cp.wait_send()
        for lc in local:
            lc.wait()

    out_shape = [jax.ShapeDtypeStruct(a.shape if scatter else (N_CHIPS,) + a.shape, a.dtype) for a in arrs]
    return pl.pallas_call(
        body, name=name, out_shape=out_shape, in_specs=[ANY] * n, out_specs=[ANY] * n,
        scratch_shapes=[pltpu.SemaphoreType.DMA((6 * n,)), pltpu.SemaphoreType.DMA((6 * n,)), pltpu.SemaphoreType.DMA((n,))],
    )(*arrs)


HBM_SPEC = pl.BlockSpec(memory_space=pltpu.HBM)
SEM_SPEC = pl.BlockSpec(memory_space=pltpu.SEMAPHORE)
SIDE_EFFECT = pltpu.SideEffectType.DATAFLOW_SIDE_EFFECTING


def _in_hbm(a):
    return pltpu.with_memory_space_constraint(a, pltpu.HBM)


def exchange_start(arrs, name, scatter, after):
    n = len(arrs)
    lands = [lax.empty(a.shape if scatter else (N_CHIPS,) + a.shape, a.dtype) for a in arrs]

    def body(*refs):
        ins, zones = refs[:n], refs[n:2 * n]
        send_sems, recv_sems, token = refs[2 * n + 1], refs[2 * n + 2], refs[-1]
        x, y, c = _place()
        me = 2 * x + y
        for a in range(n):
            for j, (px, py) in enumerate(_chip_peers(x, y)):
                pltpu.make_async_remote_copy(
                    src_ref=ins[a].at[2 * px + py] if scatter else ins[a], dst_ref=zones[a].at[me],
                    send_sem=send_sems.at[3 * a + j], recv_sem=recv_sems.at[3 * a + j],
                    device_id=(px, py, c), device_id_type=MESH).start()
        token[...] = jnp.zeros_like(token)

    res = pl.pallas_call(
        body, name=name,
        out_shape=[pltpu.SemaphoreType.DMA((3 * n,)), pltpu.SemaphoreType.DMA((3 * n,))]
        + [pltpu.HBM(a.shape, a.dtype) for a in arrs] + [pltpu.HBM(z.shape, z.dtype) for z in lands]
        + [jax.ShapeDtypeStruct((8, 128), F32)],
        in_specs=[HBM_SPEC] * (2 * n) + [ANY], out_specs=[SEM_SPEC, SEM_SPEC] + [HBM_SPEC] * (2 * n) + [WHOLE_VMEM],
        input_output_aliases={i: 2 + i for i in range(2 * n)},
        compiler_params=pltpu.CompilerParams(has_side_effects=SIDE_EFFECT),
    )(*[_in_hbm(a) for a in arrs], *[_in_hbm(z) for z in lands], after)
    return dict(send=res[0], recv=res[1], src=res[2:2 + n], zones=res[2 + n:2 + 2 * n], token=res[-1], scatter=scatter)


def exchange_wait(handle, name, after):
    srcs, zones, scatter = handle["src"], handle["zones"], handle["scatter"]
    n = len(srcs)

    def body(*refs):
        ins, lands = refs[:n], refs[n:2 * n]
        send_sems, recv_sems = refs[2 * n], refs[2 * n + 1]
        x, y, c = _place()
        me = 2 * x + y
        for a in range(n):
            for j, (px, py) in enumerate(_chip_peers(x, y)):
                cp = pltpu.make_async_remote_copy(
                    src_ref=ins[a].at[me] if scatter else ins[a], dst_ref=lands[a].at[2 * px + py],
                    send_sem=send_sems.at[3 * a + j], recv_sem=recv_sems.at[3 * a + j],
                    device_id=(px, py, c), device_id_type=MESH)
                cp.wait_send()
                cp.wait_recv()

    res = pl.pallas_call(
        body, name=name, out_shape=[pltpu.HBM(a.shape, a.dtype) for a in list(srcs) + list(zones)],
        in_specs=[HBM_SPEC] * (2 * n) + [SEM_SPEC, SEM_SPEC, ANY], out_specs=[HBM_SPEC] * (2 * n),
        input_output_aliases={i: i for i in range(2 * n)},
        compiler_params=pltpu.CompilerParams(has_side_effects=SIDE_EFFECT),
    )(*srcs, *zones, handle["send"], handle["recv"], after)
    return res[n:]


def swap_start(arrs, name, after):
    n = len(arrs)
    lands = [lax.empty(a.shape, a.dtype) for a in arrs]

    def body(*refs):
        ins, zones = refs[:n], refs[n:2 * n]
        send_sems, recv_sems, token = refs[2 * n + 1], refs[2 * n + 2], refs[-1]
        x, y, c = _place()
        for a in range(n):
            pltpu.make_async_remote_copy(src_ref=ins[a], dst_ref=zones[a], send_sem=send_sems.at[a], recv_sem=recv_sems.at[a],
                                         device_id=(x, y, 1 - c), device_id_type=MESH).start()
        token[...] = jnp.zeros_like(token)

    res = pl.pallas_call(
        body, name=name,
        out_shape=[pltpu.SemaphoreType.DMA((n,)), pltpu.SemaphoreType.DMA((n,))]
        + [pltpu.HBM(a.shape, a.dtype) for a in arrs] * 2 + [jax.ShapeDtypeStruct((8, 128), F32)],
        in_specs=[HBM_SPEC] * (2 * n) + [ANY], out_specs=[SEM_SPEC, SEM_SPEC] + [HBM_SPEC] * (2 * n) + [WHOLE_VMEM],
        input_output_aliases={i: 2 + i for i in range(2 * n)},
        compiler_params=pltpu.CompilerParams(has_side_effects=SIDE_EFFECT),
    )(*[_in_hbm(a) for a in arrs], *[_in_hbm(z) for z in lands], after)
    return dict(send=res[0], recv=res[1], src=res[2:2 + n], zones=res[2 + n:2 + 2 * n], token=res[-1])


def swap_wait(handle, name, after):
    srcs, zones = handle["src"], handle["zones"]
    n = len(srcs)

    def body(*refs):
        ins, lands = refs[:n], refs[n:2 * n]
        send_sems, recv_sems = refs[2 * n], refs[2 * n + 1]
        x, y, c = _place()
        for a in range(n):
            cp = pltpu.make_async_remote_copy(src_ref=ins[a], dst_ref=lands[a], send_sem=send_sems.at[a],
                                              recv_sem=recv_sems.at[a], device_id=(x, y, 1 - c), device_id_type=MESH)
            cp.wait_send()
            cp.wait_recv()

    res = pl.pallas_call(
        body, name=name, out_shape=[pltpu.HBM(a.shape, a.dtype) for a in list(srcs) + list(zones)],
        in_specs=[HBM_SPEC] * (2 * n) + [SEM_SPEC, SEM_SPEC, ANY], out_specs=[HBM_SPEC] * (2 * n),
        input_output_aliases={i: i for i in range(2 * n)},
        compiler_params=pltpu.CompilerParams(has_side_effects=SIDE_EFFECT),
    )(*srcs, *zones, handle["send"], handle["recv"], after)
    return res[:n], res[n:]


TILE_BYTES = 2 * 1024 * 1024


def _row_tile(rows, row_bytes):
    if rows * row_bytes <= TILE_BYTES or rows % 8:
        return rows
    best = 8
    for t in range(8, rows + 1, 8):
        if rows % t == 0 and t * row_bytes <= TILE_BYTES:
            best = t
    return best


def pair_add(a, b, name):
    shape = a.shape
    a, b = a.reshape(-1, shape[-1]), b.reshape(-1, shape[-1])
    R, C = a.shape
    tr = _row_tile(R, C * 4)

    def body(a_ref, b_ref, o_ref):
        o_ref[...] = (a_ref[...].astype(F32) + b_ref[...].astype(F32)).astype(BF16)

    spec = pl.BlockSpec((tr, C), lambda i: (i, 0))
    return pl.pallas_call(body, name=name, grid=(R // tr,), in_specs=[spec, spec], out_specs=spec,
                          out_shape=jax.ShapeDtypeStruct((R, C), BF16), compiler_params=_cparams(("parallel",)))(a, b).reshape(shape)


def sum_lead(parts, name):
    K, R, C = parts.shape
    tr = _row_tile(R, C * 4)

    def body(p_ref, o_ref):
        acc = p_ref[0].astype(F32)
        for j in range(1, K):
            acc = acc + p_ref[j].astype(F32)
        o_ref[...] = acc

    return pl.pallas_call(
        body, name=name, grid=(R // tr,), in_specs=[pl.BlockSpec((K, tr, C), lambda i: (0, i, 0))],
        out_specs=pl.BlockSpec((tr, C), lambda i: (i, 0)), out_shape=jax.ShapeDtypeStruct((R, C), F32),
        compiler_params=_cparams(("parallel",)))(parts)


def adamw(w, g, m, v, name):
    R, C = w.shape
    tr = _row_tile(R, C * 4)

    def body(w_ref, g_ref, m_ref, v_ref, d_ref, mo_ref, vo_ref):
        gv = g_ref[...]
        m2 = ADAM_B1 * m_ref[...] + (1.0 - ADAM_B1) * gv
        v2 = ADAM_B2 * v_ref[...] + (1.0 - ADAM_B2) * (gv * gv)
        m_hat = m2 / (1.0 - ADAM_B1 ** ADAM_STEP)
        v_hat = v2 / (1.0 - ADAM_B2 ** ADAM_STEP)
        d_ref[...] = -ADAM_LR * (m_hat / (jnp.sqrt(v_hat) + ADAM_EPS) + ADAM_WD * w_ref[...])
        mo_ref[...] = m2
        vo_ref[...] = v2

    spec = pl.BlockSpec((tr, C), lambda i: (i, 0))
    return pl.pallas_call(body, name=name, grid=(R // tr,), in_specs=[spec] * 4, out_specs=[spec] * 3,
                          out_shape=[jax.ShapeDtypeStruct((R, C), F32)] * 3, compiler_params=_cparams(("parallel",)))(w, g, m, v)


LANES = 1024


def _pack(arrs, rows):
    out, offs, r = [], [], 0
    for a in arrs:
        flat = a.reshape(-1)
        nr = -(-flat.shape[0] // LANES)
        out.append(jnp.pad(flat, (0, nr * LANES - flat.shape[0])))
        offs.append(r)
        r += nr
    assert r <= rows, (r, rows)
    out.append(jnp.zeros(((rows - r) * LANES,), F32))
    return jnp.concatenate(out).reshape(rows, LANES), offs


def _unpack(packed, offs, shapes):
    flat = packed.reshape(-1)
    return [flat[o * LANES:o * LANES + math.prod(s)].reshape(s) for o, s in zip(offs, shapes)]


WEIGHTS = ["w_ada", "b_ada", "w_in", "b_gate", "conv_a", "a_log", "dt_bias", "norm_a", "rel_bias", "w_branch_a",
           "w_branch_b", "w_o", "ln1_g", "ln1_b", "w_up", "conv_ffn", "b_conv_ffn", "w_down", "ln2_g", "ln2_b"]
BIG = ["w_in", "w_branch_a", "w_branch_b", "w_o", "w_up", "w_down"]
LATE = [n for n in BIG if n != "w_in"]
KEPT_SHARDED = {"w_up"}
COL_SHARDED = {"w_in", "w_up"}
SMALL_SHARDED = {"conv_a": 3 * A_W // N_CHIPS, "rel_bias": B_REL // N_CHIPS, "conv_ffn": 2 * D_FF // N_CHIPS}
SMALL = [n for n in WEIGHTS if n not in BIG and n != "w_ada"]


def _to_full(g4, name):
    if name in KEPT_SHARDED:
        return g4
    if name in COL_SHARDED:
        return jnp.transpose(g4, (1, 0, 2)).reshape(g4.shape[1], -1)
    return g4.reshape(-1, g4.shape[2])


def _to_shards(full, name):
    if name in KEPT_SHARDED:
        return full
    if name in COL_SHARDED:
        return jnp.transpose(full.reshape(full.shape[0], N_CHIPS, -1), (1, 0, 2))
    return full.reshape(N_CHIPS, -1, full.shape[1])


def kernel(x, c, w_ada, b_ada, w_in, b_gate, conv_a, a_log, dt_bias, norm_a, rel_bias, w_branch_a, w_branch_b, w_o, ln1_g, ln1_b, w_up, conv_ffn, b_conv_ffn, w_down, ln2_g, ln2_b, loss_target, m_w_ada, m_b_ada, m_w_in, m_b_gate, m_conv_a, m_a_log, m_dt_bias, m_norm_a, m_rel_bias, m_w_branch_a, m_w_branch_b, m_w_o, m_ln1_g, m_ln1_b, m_w_up, m_conv_ffn, m_b_conv_ffn, m_w_down, m_ln2_g, m_ln2_b, v_w_ada, v_b_ada, v_w_in, v_b_gate, v_conv_a, v_a_log, v_dt_bias, v_norm_a, v_rel_bias, v_w_branch_a, v_w_branch_b, v_w_o, v_ln1_g, v_ln1_b, v_w_up, v_conv_ffn, v_b_conv_ffn, v_w_down, v_ln2_g, v_ln2_b):
    args = dict(locals())
    wts = {n: args[n] for n in WEIGHTS}
    moms = {n: args["m_" + n] for n in WEIGHTS}
    vars_ = {n: args["v_" + n] for n in WEIGHTS}
    xi, yi, ci = _place()
    chip = 2 * xi + yi
    dev = 4 * xi + 2 * yi + ci
    ada_cols = w_ada.shape[2]

    sshapes = [wts[n].shape[1:] for n in SMALL_SHARDED]
    spack, soffs = _pack([wts[n][0] for n in SMALL_SHARDED], 16)
    first = allgather8(jnp.concatenate([jnp.pad(c, ((0, 7), (0, 0))), spack]), "gather_c_small_w").reshape(N_DEV, 24, LANES)
    c_all = first[:, 0]
    b_ada_sh = lax.dynamic_slice(b_ada, (0, chip * ada_cols), (1, ada_cols))
    mod_sh = ada_fwd(c_all, w_ada[0], b_ada_sh)
    mod_g = allgather8(mod_sh, "gather_mod").reshape(N_CHIPS, 2, N_DEV, ada_cols)[:, 0]
    mod = lax.dynamic_slice(mod_g, (0, dev, 0), (N_CHIPS, 1, ada_cols)).reshape(6, D_MODEL)

    (w_in_g4,) = chip_exchange([wts["w_in"][0].astype(BF16)], "gather_w_in", scatter=False)
    wd = split_w_in(_to_full(w_in_g4, "w_in"))
    late_shards = [wts[n][0].astype(BF16) for n in LATE]
    late_gather = exchange_start(late_shards, "gather_late_start", scatter=False, after=w_in_g4)
    mod = mod + late_gather["token"][0, 0]

    def late_weights(after):
        zones = exchange_wait(late_gather, "gather_late_wait", after)
        full = [_to_full(lax.dynamic_update_slice(z, s[None], (chip, 0, 0)), n) for n, z, s in zip(LATE, zones, late_shards)]
        return {n[2:]: f for n, f in zip(LATE, full)}

    sg = first[::2, 8:]
    sparts = [_unpack(sg[j], soffs, sshapes) for j in range(N_CHIPS)]
    sm = {n: wts[n] for n in SMALL if n not in SMALL_SHARDED and n != "b_ada"}
    for i, n in enumerate(SMALL_SHARDED):
        sm[n] = jnp.concatenate([sparts[j][i] for j in range(N_CHIPS)], axis=-1)

    early = {}

    def late_start(g):
        early["swap"] = swap_start([g[n] for n in LATE], "grad_swap_late_start", g[LATE[0]])
        return early["swap"]["token"]

    def late_finish(after):
        mine, theirs = swap_wait(early["swap"], "grad_swap_late_wait", after)
        early["sums"] = [_to_shards(pair_add(a, b, "grad_pair_" + n), n) for n, a, b in zip(LATE, mine, theirs)]
        early["scatter"] = exchange_start(early["sums"], "grad_scatter_start", scatter=True, after=theirs[0])
        return early["scatter"]["token"]

    def w_in_start(g):
        early["swap_in"] = swap_start([g], "grad_swap_w_in_start", g)
        return early["swap_in"]["token"]

    def w_in_finish(after):
        (mine,), (theirs,) = swap_wait(early["swap_in"], "grad_swap_w_in_wait", after)
        early["sum_in"] = _to_shards(pair_add(mine, theirs, "grad_pair_w_in"), "w_in")
        early["scatter_in"] = exchange_start([early["sum_in"]], "grad_scatter_w_in_start", scatter=True, after=theirs)
        return early["scatter_in"]["token"]

    hooks = dict(late_start=late_start, late_finish=late_finish, w_in_start=w_in_start, w_in_finish=w_in_finish)
    loss, dxpre2, dffn, fin, sv = forward_local(x[0], loss_target[0], mod, wd, sm, late_weights)
    grad_x, dmod, gw, gs = backward_local(x[0], mod, sm, dxpre2, dffn, fin, sv, hooks)

    gnames = [n for n in SMALL if n != "b_ada"]
    vec, voffs = _pack([dmod] + [gs[n] for n in gnames] + [loss], 56)
    gathered = allgather8(vec, "gather_small_g").reshape(N_DEV, 56, LANES)
    summed = sum_lead(gathered, "sum_small_g")
    full_shapes = [(6, D_MODEL)] + [gs[n].shape for n in gnames] + [(1, 1)]
    parts = _unpack(summed, voffs, full_shapes)
    grads = {"b_ada": parts[0].reshape(1, -1)}
    for n, p in zip(gnames, parts[1:-1]):
        if n in SMALL_SHARDED:
            p = lax.dynamic_slice_in_dim(p, chip * SMALL_SHARDED[n], SMALL_SHARDED[n], axis=1)
        grads[n] = p.reshape(wts[n].shape)
    loss_total = parts[-1].reshape(())
    dmod_all = gathered[:, 0:6, :].reshape(N_DEV, 6 * D_MODEL)
    grads["w_ada"] = ada_bwd(c_all, lax.dynamic_slice(dmod_all, (0, chip * ada_cols), (N_DEV, ada_cols)))[None]

    def own_slot(zone, sums):
        return lax.dynamic_update_slice(zone, lax.dynamic_slice_in_dim(sums, chip, 1, axis=0), (chip, 0, 0))

    zones = exchange_wait(early["scatter"], "grad_scatter_wait", summed)
    for n, z, s in zip(LATE, zones, early["sums"]):
        grads[n] = sum_lead(own_slot(z, s), "grad_sum_" + n)[None]

    delta, new_m, new_v = {}, {}, {}

    def update(n):
        d, m2, v2 = adamw(wts[n][0], grads[n][0], moms[n][0], vars_[n][0], "adamw_" + n)
        delta[n], new_m[n], new_v[n] = d[None], m2[None], v2[None]

    for n in ["w_ada"] + LATE:
        update(n)
    shapes = [wts[n].shape for n in SMALL]
    packs = [_pack([t[n] for n in SMALL], 32) for t in (wts, grads, moms, vars_)]
    outs = adamw(*[p[0] for p in packs], "adamw_small")
    for res, o in zip((delta, new_m, new_v), outs):
        for n, a in zip(SMALL, _unpack(o, packs[0][1], shapes)):
            res[n] = a
    (zone_in,) = exchange_wait(early["scatter_in"], "grad_scatter_w_in_wait", outs[0])
    grads["w_in"] = sum_lead(own_slot(zone_in, early["sum_in"]), "grad_sum_w_in")[None]
    update("w_in")
    return (loss_total, grad_x[None], *[grads[n] for n in WEIGHTS], *[delta[n] for n in WEIGHTS],
            *[new_m[n] for n in WEIGHTS], *[new_v[n] for n in WEIGHTS])
```

```python
import functools
import math

import jax
import jax.numpy as jnp
from jax import lax
from jax.experimental import pallas as pl
from jax.experimental.pallas import tpu as pltpu

F32 = jnp.float32
BF16 = jnp.bfloat16

D_MODEL = 1024
CHUNK = 64
A_HEADS = 8
A_DK = 128
A_W = A_HEADS * A_DK
A_CONV = 4
B_HEADS = 16
B_DH = 64
B_W = B_HEADS * B_DH
B_PREV = 8
B_BAND = (B_PREV + 1) * CHUNK
B_MAX_REL = 256
B_REL = CHUNK - 1 + B_MAX_REL + 1
D_FF = 2816
FFN_CONV = 3
IN_COLS = 4 * A_W + 2 * A_HEADS + 3 * B_W + 2 * D_MODEL
ALPHA = 2.0 ** 0.25
LN_EPS = 1e-5
RMS_EPS = 1e-6
L2_EPS = 1e-6
NEG_INF = -1e30
ADAM_LR, ADAM_B1, ADAM_B2, ADAM_EPS, ADAM_WD, ADAM_STEP = 0.001, 0.9, 0.999, 1e-08, 0.01, 10
N_CHIPS = 4
N_DEV = 8
VMEM_LIMIT = 56 * 1024 * 1024


def _cparams(sem=None):
    return pltpu.CompilerParams(dimension_semantics=sem, vmem_limit_bytes=VMEM_LIMIT)


_DIMS = {"nn": (((1,), (0,)), ((), ())), "nt": (((1,), (1,)), ((), ())), "tn": (((0,), (0,)), ((), ()))}


MM_TILE_CAP = 1536


MM_TOKEN_K_CAP = 2048
MM_BF16_ROWS_CAP = 2048
MM_K_CAP = 3072


def _mm_tile(n, cap=MM_TILE_CAP):
    return max(t for t in range(128, min(n, cap) + 1, 128) if n % t == 0)


def mm(a, b, *, mode, out_dtype, name, acc_in=None, b_shards=False, out_shards=0):
    b_rows, b_cols = (b.shape[1], b.shape[0] * b.shape[2]) if b_shards else b.shape
    if mode == "nn":
        (M, K), (K2, N) = a.shape, (b_rows, b_cols)
    elif mode == "nt":
        (M, K), (N, K2) = a.shape, (b_rows, b_cols)
    else:
        (K, M), (K2, N) = a.shape, (b_rows, b_cols)
    assert K == K2, (a.shape, b.shape, mode)
    tm, tn, tk = _mm_tile(M), _mm_tile(N), _mm_tile(K, MM_TOKEN_K_CAP if mode == "tn" else MM_K_CAP)
    if out_dtype == BF16 and acc_in is None and mode != "tn" and K == tk:
        tm = _mm_tile(M, MM_BF16_ROWS_CAP)
    if b_shards and mode == "nt":
        tk = b.shape[2]
    nk = K // tk

    def body(*refs):
        if acc_in is None:
            a_ref, b_ref, o_ref, acc_ref = refs
        else:
            a_ref, b_ref, c_ref, o_ref, acc_ref = refs
        k = pl.program_id(2)

        @pl.when(k == 0)
        def _():
            if acc_in is None:
                acc_ref[...] = jnp.zeros_like(acc_ref)
            else:
                acc_ref[...] = c_ref[...]

        acc_ref[...] += lax.dot_general(a_ref[...].astype(BF16), b_ref[...].astype(BF16), _DIMS[mode],
                                        preferred_element_type=F32)

        @pl.when(k == nk - 1)
        def _():
            o_ref[...] = acc_ref[...].astype(out_dtype)

    a_spec = pl.BlockSpec((tk, tm), lambda i, j, k: (k, i)) if mode == "tn" else pl.BlockSpec((tm, tk), lambda i, j, k: (i, k))
    if b_shards:
        assert (tk if mode == "nt" else tn) == b.shape[2] and mode != "tn", (b.shape, tn, tk, mode)
        b_spec = (pl.BlockSpec((None, tn, tk), lambda i, j, k: (k, j, 0)) if mode == "nt"
                  else pl.BlockSpec((None, tk, tn), lambda i, j, k: (j, k, 0)))
    else:
        b_spec = pl.BlockSpec((tn, tk), lambda i, j, k: (j, k)) if mode == "nt" else pl.BlockSpec((tk, tn), lambda i, j, k: (k, j))
    o_spec = pl.BlockSpec((tm, tn), lambda i, j, k: (i, j))
    out_shape = jax.ShapeDtypeStruct((M, N), out_dtype)
    if out_shards:
        assert N == out_shards * tn and acc_in is None, (N, tn, out_shards)
        o_spec = pl.BlockSpec((None, tm, tn), lambda i, j, k: (j, i, 0))
        out_shape = jax.ShapeDtypeStruct((out_shards, M, tn), out_dtype)
    ins, in_specs, aliases = [a, b], [a_spec, b_spec], {}
    if acc_in is not None:
        assert acc_in.shape == (M, N) and acc_in.dtype == F32 and out_dtype == F32
        ins.append(acc_in)
        in_specs.append(o_spec)
        aliases = {2: 0}
    return pl.pallas_call(
        body, name=name, grid=(M // tm, N // tn, nk), in_specs=in_specs, out_specs=o_spec,
        out_shape=out_shape, scratch_shapes=[pltpu.VMEM((tm, tn), F32)],
        input_output_aliases=aliases, compiler_params=_cparams(("parallel", "parallel", "arbitrary")),
    )(*ins)


def rowcall(body, *, name, S, ts, ins, outs, scratch=()):
    assert S % ts == 0 and ts % 16 == 0
    nsteps = S // ts
    in_specs, arrays = [], []
    for arr, kind in ins:
        arrays.append(arr)
        if kind == "row":
            in_specs.append(pl.BlockSpec((ts, arr.shape[1]), lambda i: (i, 0)))
        elif kind in ("prev", "next"):
            hr = 8 * (4 // arr.dtype.itemsize)
            per, last = ts // hr, S // hr - 1
            if kind == "prev":
                in_specs.append(pl.BlockSpec((hr, arr.shape[1]), lambda i, per=per: (jnp.maximum(i * per - 1, 0), 0)))
            else:
                in_specs.append(pl.BlockSpec((hr, arr.shape[1]), lambda i, per=per, last=last: (jnp.minimum((i + 1) * per, last), 0)))
        else:
            nd = arr.ndim
            in_specs.append(pl.BlockSpec(arr.shape, lambda i, nd=nd: (0,) * nd))
    out_specs, out_shapes, acc_idx = [], [], []
    for n, (shape, dtype, kind) in enumerate(outs):
        out_shapes.append(jax.ShapeDtypeStruct(shape, dtype))
        if kind == "row":
            out_specs.append(pl.BlockSpec((ts, shape[1]), lambda i: (i, 0)))
        else:
            nd = len(shape)
            out_specs.append(pl.BlockSpec(shape, lambda i, nd=nd: (0,) * nd))
            acc_idx.append(n)
    n_in = len(arrays)

    def wrapped(*refs):
        @pl.when(pl.program_id(0) == 0)
        def _():
            for n in acc_idx:
                refs[n_in + n][...] = jnp.zeros_like(refs[n_in + n])

        body(*refs)

    res = pl.pallas_call(
        wrapped, name=name, grid=(nsteps,), in_specs=in_specs, out_specs=out_specs, out_shape=out_shapes,
        scratch_shapes=list(scratch), compiler_params=_cparams(("arbitrary",) if acc_idx else ("parallel",)),
    )(*arrays)
    return res


def _halo_prev(ref):
    v = ref[...].astype(F32)
    return v[v.shape[0] - 8:]


def _halo_next(ref):
    return ref[...].astype(F32)[:8]


def _shift_down(cur, prev8, k):
    if k == 0:
        return cur
    rolled = pltpu.roll(cur, k, axis=0)
    fix = pltpu.roll(prev8, k, axis=0)
    row = lax.broadcasted_iota(jnp.int32, (8, 1), 0)
    top = jnp.where(row < k, fix, rolled[0:8])
    if cur.shape[0] == 8:
        return top
    return jnp.concatenate([top, rolled[8:]], axis=0)


def _shift_up(cur, next8, k):
    if k == 0:
        return cur
    n = cur.shape[0]
    rolled = pltpu.roll(cur, n - k, axis=0)
    fix = pltpu.roll(next8, 8 - k, axis=0)
    row = lax.broadcasted_iota(jnp.int32, (8, 1), 0)
    bot = jnp.where(row >= 8 - k, fix, rolled[n - 8:n])
    return jnp.concatenate([rolled[:n - 8], bot], axis=0)


def _sigmoid(x):
    return 1.0 / (1.0 + jnp.exp(-x))


def _silu(x):
    return x * _sigmoid(x)


def _silu_and_grad(x):
    s = _sigmoid(x)
    return x * s, s * (1.0 + x * (1.0 - s))


def _softplus(x):
    return jnp.maximum(x, 0.0) + jnp.log1p(jnp.exp(-jnp.abs(x)))


def _split2(x):
    hi = x.astype(BF16)
    return hi, (x - hi.astype(F32)).astype(BF16)


def _dot1(a, b, mode):
    return lax.dot_general(a.astype(BF16), b.astype(BF16), _DIMS[mode], preferred_element_type=F32)


def _dot3(a, b, mode):
    ah, al = _split2(a)
    bh, bl = _split2(b)
    d = lambda p, q: lax.dot_general(p, q, _DIMS[mode], preferred_element_type=F32)
    return d(ah, bh) + (d(ah, bl) + d(al, bh))


def ada_fwd(c_all, w_sh, b_sh):
    n = w_sh.shape[1]
    tn = 512

    def body(c_ref, w_ref, b_ref, o_ref):
        o_ref[...] = _dot1(_silu(c_ref[...]), w_ref[...], "nn") + b_ref[...]

    return pl.pallas_call(
        body, name="ada_fwd", grid=(n // tn,),
        in_specs=[pl.BlockSpec((N_DEV, D_MODEL), lambda j: (0, 0)), pl.BlockSpec((D_MODEL, tn), lambda j: (0, j)),
                  pl.BlockSpec((1, tn), lambda j: (0, j))],
        out_specs=pl.BlockSpec((N_DEV, tn), lambda j: (0, j)), out_shape=jax.ShapeDtypeStruct((N_DEV, n), F32),
        compiler_params=_cparams(("parallel",)),
    )(c_all, w_sh, b_sh)


def ada_bwd(c_all, dmod_sh):
    n = dmod_sh.shape[1]
    tn = 512

    def body(c_ref, d_ref, o_ref):
        o_ref[...] = _dot1(_silu(c_ref[...]), d_ref[...], "tn")

    return pl.pallas_call(
        body, name="ada_bwd", grid=(n // tn,),
        in_specs=[pl.BlockSpec((N_DEV, D_MODEL), lambda j: (0, 0)), pl.BlockSpec((N_DEV, tn), lambda j: (0, j))],
        out_specs=pl.BlockSpec((D_MODEL, tn), lambda j: (0, j)), out_shape=jax.ShapeDtypeStruct((D_MODEL, n), F32),
        compiler_params=_cparams(("parallel",)),
    )(c_all, dmod_sh)


SHIFT_T, SCALE_T, GATE_T, SHIFT_F, SCALE_F, GATE_F = range(6)


def modulate(x, mod, shift_row, scale_row, name):
    S = x.shape[0]

    def body(x_ref, m_ref, o_ref):
        m = m_ref[...]
        o_ref[...] = (x_ref[...] * (1.0 + m[scale_row:scale_row + 1]) + m[shift_row:shift_row + 1]).astype(BF16)

    return rowcall(body, name=name, S=S, ts=512, ins=[(x, "row"), (mod, "vec")], outs=[((S, D_MODEL), BF16, "row")])[0]


def _conv_fwd(cur, prev, w, width):
    y = cur * w[width - 1:width]
    for j in range(width - 1):
        y = y + _shift_down(cur, prev, width - 1 - j) * w[j:j + 1]
    return y


def _prep_a_core(cur, prev, w):
    return _silu_and_grad(_conv_fwd(cur, prev, w, A_CONV))


def prep_a_fwd(qkv_raw, ba, conv_a, a_log, dt_bias):
    S = qkv_raw.shape[0]

    def body(x_ref, xp_ref, ba_ref, w_ref, al_ref, dt_ref, q_ref, k_ref, v_ref, beta_ref, g_ref):
        first = (pl.program_id(0) > 0).astype(F32)
        y, _ = _prep_a_core(x_ref[...].astype(F32), _halo_prev(xp_ref) * first, w_ref[...])
        for h in range(A_HEADS):
            sl = slice(h * A_DK, (h + 1) * A_DK)
            qh = y[:, sl]
            kh = y[:, A_W + h * A_DK:A_W + (h + 1) * A_DK]
            q_ref[:, sl] = qh * (lax.rsqrt(jnp.sum(qh * qh, axis=-1, keepdims=True) + L2_EPS) * (A_DK ** -0.5))
            k_ref[:, sl] = kh * lax.rsqrt(jnp.sum(kh * kh, axis=-1, keepdims=True) + L2_EPS)
        v_ref[...] = y[:, 2 * A_W:3 * A_W]
        bav = ba_ref[...]
        beta_ref[...] = _sigmoid(bav[:, 0:A_HEADS])
        g_ref[...] = -jnp.exp(al_ref[...]) * _softplus(bav[:, A_HEADS:2 * A_HEADS] + dt_ref[...])

    return rowcall(
        body, name="prep_a_fwd", S=S, ts=256,
        ins=[(qkv_raw, "row"), (qkv_raw, "prev"), (ba, "row"), (conv_a, "vec"), (a_log, "vec"), (dt_bias, "vec")],
        outs=[((S, A_W), F32, "row")] * 3 + [((S, A_HEADS), F32, "row")] * 2)


HEAD_GROUP = 2
GROUP_ROWS = HEAD_GROUP * CHUNK
N_HEAD_GROUPS = A_HEADS // HEAD_GROUP
LOG_CHUNK = int(math.log2(CHUNK))


def _tri_masks():
    rb = lax.broadcasted_iota(jnp.int32, (GROUP_ROWS, GROUP_ROWS), 0)
    cb = lax.broadcasted_iota(jnp.int32, (GROUP_ROWS, GROUP_ROWS), 1)
    same = (rb >> LOG_CHUNK) == (cb >> LOG_CHUNK)
    return dict(causal=same & (rb >= cb), strict=same & (rb > cb), eye=rb == cb, upper=same & (cb >= rb),
                last=cb == (rb | (CHUNK - 1)), rb=rb, cb=cb)


def _col_to_row(colv, eye):
    return jnp.sum(jnp.where(eye, colv, 0.0), axis=0, keepdims=True)


def _row_to_col(rowv, eye):
    return jnp.sum(jnp.where(eye, rowv, 0.0), axis=1, keepdims=True)


def _tri_inv(a_list, mk):
    rb, cb = mk["rb"], mk["cb"]
    ts = [jnp.where(mk["eye"], 1.0, 0.0) - jnp.where((rb >> 1) == (cb >> 1), a, 0.0) for a in a_list]
    for lvl in range(1, LOG_CHUNK):
        rs, cs = rb >> lvl, cb >> lvl
        sel = ((rs & 1) == 1) & (cs == rs - 1)
        inner = [_dot3(t, jnp.where(sel, a, 0.0), "nn") for t, a in zip(ts, a_list)]
        ts = [t - _dot3(i, t, "nn") for i, t in zip(inner, ts)]
    return ts


def _stack_heads(ref, grp):
    return jnp.concatenate([ref[:, (grp * HEAD_GROUP + j) * A_DK:(grp * HEAD_GROUP + j + 1) * A_DK]
                            for j in range(HEAD_GROUP)], axis=0)


def _stack_cols(tile, grp):
    return jnp.concatenate([tile[:, grp * HEAD_GROUP + j:grp * HEAD_GROUP + j + 1] for j in range(HEAD_GROUP)], axis=0)


def _delta_local(q, k, v, beta, g, mk):
    causal, strict, eye = mk["causal"], mk["strict"], mk["eye"]
    g_row = _col_to_row(g, eye)
    gc = jnp.sum(jnp.where(causal, g_row, 0.0), axis=1, keepdims=True)
    gc_row = _col_to_row(gc, eye)
    decay = jnp.where(causal, jnp.exp(jnp.where(causal, gc - gc_row, 0.0)), 0.0)
    gam = jnp.exp(gc)
    kb = k * beta
    vb = v * beta
    y = kb * gam
    a = jnp.where(strict, _dot1(kb, k, "nt") * decay, 0.0)
    p = _dot1(q, k, "nt") * decay
    gl = jnp.sum(jnp.where(mk["last"], gc_row, 0.0), axis=1, keepdims=True)
    kd = k * jnp.exp(gl - gc)
    return dict(gc=gc, decay=decay, gam=gam, kb=kb, vb=vb, y=y, a=a, p=p, gl=gl, kd=kd)


def _head_rows(x, j):
    return x[j * CHUNK:(j + 1) * CHUNK]


def delta_fwd(q, k, v, beta, g):
    S = q.shape[0]
    n_chunks = S // CHUNK

    def body(q_ref, k_ref, v_ref, beta_ref, g_ref, o_ref, sprev_ref, t_ref, state_ref):
        @pl.when(pl.program_id(0) == 0)
        def _():
            state_ref[...] = jnp.zeros_like(state_ref)

        mk = _tri_masks()
        betav, gv = beta_ref[...], g_ref[...]
        groups = range(N_HEAD_GROUPS)
        q_all = [_stack_heads(q_ref, grp) for grp in groups]
        locs = [_delta_local(q_all[grp], _stack_heads(k_ref, grp), _stack_heads(v_ref, grp),
                             _stack_cols(betav, grp), _stack_cols(gv, grp), mk) for grp in groups]
        tinvs = _tri_inv([loc["a"] for loc in locs], mk)
        uws = [_dot3(tinvs[grp], jnp.concatenate([locs[grp]["vb"], locs[grp]["y"]], axis=1), "nn") for grp in groups]
        for grp in groups:
            loc, uw = locs[grp], uws[grp]
            t_ref[0, grp] = tinvs[grp]
            qg = q_all[grp] * loc["gam"]
            egl = jnp.exp(loc["gl"])
            vns, o_state = [], []
            for j in range(HEAD_GROUP):
                h = grp * HEAD_GROUP + j
                s0 = state_ref[h]
                sprev_ref[0, h] = s0
                uw_h = _head_rows(uw, j)
                vn = uw_h[:, :A_DK] - _dot1(uw_h[:, A_DK:], s0, "nn")
                vns.append(vn)
                o_state.append(_dot1(_head_rows(qg, j), s0, "nn"))
                state_ref[h] = s0 * egl[(j + 1) * CHUNK - 1:(j + 1) * CHUNK] + _dot1(_head_rows(loc["kd"], j), vn, "tn")
            o_local = _dot1(loc["p"], jnp.concatenate(vns, axis=0), "nn")
            for j in range(HEAD_GROUP):
                h = grp * HEAD_GROUP + j
                o_ref[:, h * A_DK:(h + 1) * A_DK] = o_state[j] + _head_rows(o_local, j)

    tile = pl.BlockSpec((CHUNK, A_W), lambda n: (n, 0))
    small = pl.BlockSpec((CHUNK, A_HEADS), lambda n: (n, 0))
    return pl.pallas_call(
        body, name="delta_fwd", grid=(n_chunks,), in_specs=[tile, tile, tile, small, small],
        out_specs=[tile, pl.BlockSpec((1, A_HEADS, A_DK, A_DK), lambda n: (n, 0, 0, 0)),
                   pl.BlockSpec((1, N_HEAD_GROUPS, GROUP_ROWS, GROUP_ROWS), lambda n: (n, 0, 0, 0))],
        out_shape=[jax.ShapeDtypeStruct((S, A_W), F32), jax.ShapeDtypeStruct((n_chunks, A_HEADS, A_DK, A_DK), F32),
                   jax.ShapeDtypeStruct((n_chunks, N_HEAD_GROUPS, GROUP_ROWS, GROUP_ROWS), F32)],
        scratch_shapes=[pltpu.VMEM((A_HEADS, A_DK, A_DK), F32)],
        compiler_params=_cparams(("arbitrary",)),
    )(q, k, v, beta, g)


def gate_a_fwd(o_pre, z, norm_w):
    S = o_pre.shape[0]

    def body(o_ref, z_ref, nw_ref, out_ref):
        nw = nw_ref[...]
        for h in range(A_HEADS):
            sl = slice(h * A_DK, (h + 1) * A_DK)
            oh = o_ref[:, sl]
            r = lax.rsqrt(jnp.mean(oh * oh, axis=-1, keepdims=True) + RMS_EPS)
            out_ref[:, sl] = (oh * r * nw * _silu(z_ref[:, sl].astype(F32))).astype(BF16)

    return rowcall(body, name="gate_a_fwd", S=S, ts=512, ins=[(o_pre, "row"), (z, "row"), (norm_w, "vec")],
                   outs=[((S, A_W), BF16, "row")])[0]


HEADS_PER_GROUP = 2
GROUP_W = HEADS_PER_GROUP * B_DH
N_GROUPS = B_HEADS // HEADS_PER_GROUP
PAD_ROWS = B_PREV * CHUNK


Q_TILE = 256
Q_CHUNKS = Q_TILE // CHUNK
KEY_WIN = (B_PREV + Q_CHUNKS) * CHUNK


def _band_probs(qh, kh, bias, valid):
    s = _dot1(qh, kh, "nt") * (B_DH ** -0.5) + bias
    s = jnp.where(valid, s, NEG_INF)
    e = jnp.exp(s - jnp.max(s, axis=-1, keepdims=True))
    return e * (1.0 / jnp.sum(e, axis=-1, keepdims=True))


def _attn_specs(S, tile_rows):
    assert PAD_ROWS % tile_rows == 0 and S % tile_rows == 0, (PAD_ROWS, S, tile_rows)
    n_cb = B_W // GROUP_W
    return [pl.BlockSpec((tile_rows, GROUP_W), lambda g, n: (n + PAD_ROWS // tile_rows, g)),
            pl.BlockSpec((PAD_ROWS + S, GROUP_W), lambda g, n: (0, n_cb + g)),
            pl.BlockSpec((PAD_ROWS + S, GROUP_W), lambda g, n: (0, 2 * n_cb + g)),
            pl.BlockSpec((HEADS_PER_GROUP, CHUNK, B_BAND), lambda g, n: (g, 0, 0))]


def _band_valid(first_chunk):
    return lax.broadcasted_iota(jnp.int32, (CHUNK, B_BAND), 1) >= PAD_ROWS - first_chunk * CHUNK


def _chunk_rows(x, qc, rows=CHUNK):
    return x[qc * CHUNK:qc * CHUNK + rows]


FWD_TILE = 512
FWD_CHUNKS = FWD_TILE // CHUNK
FWD_WIN = (B_PREV + FWD_CHUNKS) * CHUNK


def attn_fwd(qkv_pad, bias):
    S = qkv_pad.shape[0] - PAD_ROWS

    def body(q_ref, k_ref, v_ref, b_ref, o_ref):
        n = pl.program_id(1)
        start = pl.multiple_of(n * FWD_TILE, FWD_TILE)
        kwin = k_ref[pl.ds(start, FWD_WIN), :]
        vwin = v_ref[pl.ds(start, FWD_WIN), :]
        qv = q_ref[...]
        pairs = [(qc, hh) for qc in range(FWD_CHUNKS) for hh in range(HEADS_PER_GROUP)]
        sl = lambda hh: slice(hh * B_DH, (hh + 1) * B_DH)
        s = [_dot1(_chunk_rows(qv, qc)[:, sl(hh)], _chunk_rows(kwin, qc, B_BAND)[:, sl(hh)], "nt") for qc, hh in pairs]
        s = [jnp.where(_band_valid(n * FWD_CHUNKS + qc), x * (B_DH ** -0.5) + b_ref[hh], NEG_INF)
             for x, (qc, hh) in zip(s, pairs)]
        e = [jnp.exp(x - jnp.max(x, axis=-1, keepdims=True)) for x in s]
        p = [x * (1.0 / jnp.sum(x, axis=-1, keepdims=True)) for x in e]
        o = [_dot1(x, _chunk_rows(vwin, qc, B_BAND)[:, sl(hh)], "nn") for x, (qc, hh) in zip(p, pairs)]
        rows = [jnp.concatenate(o[qc * HEADS_PER_GROUP:(qc + 1) * HEADS_PER_GROUP], axis=1) for qc in range(FWD_CHUNKS)]
        o_ref[...] = jnp.concatenate(rows, axis=0).astype(BF16)

    return pl.pallas_call(
        body, name="attn_fwd", grid=(N_GROUPS, S // FWD_TILE), in_specs=_attn_specs(S, FWD_TILE),
        out_specs=pl.BlockSpec((FWD_TILE, GROUP_W), lambda g, n: (n, g)),
        out_shape=jax.ShapeDtypeStruct((S, B_W), BF16),
        compiler_params=_cparams(("parallel", "arbitrary")),
    )(qkv_pad, qkv_pad, qkv_pad, bias)


EXT = B_BAND + CHUNK


def bias_expand(rel_bias):
    def body(rev_ref, o_ref):
        rev = rev_ref[...]
        erev = jnp.concatenate([jnp.broadcast_to(rev[:, 0:1], (B_HEADS, EXT - B_REL)), rev], axis=1)
        for i in range(CHUNK):
            o_ref[i] = erev[:, CHUNK - i:CHUNK - i + B_BAND]

    return pl.pallas_call(
        body, name="bias_expand", in_specs=[WHOLE_VMEM], out_specs=WHOLE_VMEM,
        out_shape=jax.ShapeDtypeStruct((CHUNK, B_HEADS, B_BAND), F32),
    )(jnp.flip(rel_bias, axis=1))


def bias_reduce(dbias):
    def body(d_ref, o_ref):
        acc = jnp.zeros((B_HEADS, EXT), F32)
        for i in range(CHUNK):
            acc = acc + jnp.pad(d_ref[i], ((0, 0), (CHUNK - i, i)))
        tail = acc[:, EXT - B_REL:]
        clipped = jnp.sum(acc[:, :EXT - B_REL], axis=1, keepdims=True)
        lane = lax.broadcasted_iota(jnp.int32, (B_HEADS, B_REL), 1)
        o_ref[...] = jnp.where(lane == 0, tail + clipped, tail)

    rev = pl.pallas_call(body, name="bias_reduce", in_specs=[WHOLE_VMEM], out_specs=WHOLE_VMEM,
                         out_shape=jax.ShapeDtypeStruct((B_HEADS, B_REL), F32))(dbias)
    return jnp.flip(rev, axis=1)


def merge_fwd(gates_raw, b_gate, ya, yb):
    S = ya.shape[0]

    def body(g_ref, b_ref, ya_ref, yb_ref, o_ref):
        gt = _sigmoid(g_ref[...].astype(F32) + b_ref[...])
        o_ref[...] = (gt[:, :D_MODEL] * ya_ref[...].astype(F32) + gt[:, D_MODEL:] * yb_ref[...].astype(F32)).astype(BF16)

    return rowcall(body, name="merge_fwd", S=S, ts=512,
                   ins=[(gates_raw, "row"), (b_gate, "vec"), (ya, "row"), (yb, "row")],
                   outs=[((S, D_MODEL), BF16, "row")])[0]


def _ln_stats(xpre):
    mu = jnp.mean(xpre, axis=-1, keepdims=True)
    xc = xpre - mu
    rstd = lax.rsqrt(jnp.mean(xc * xc, axis=-1, keepdims=True) + LN_EPS)
    return xc * rstd, rstd


def ln1_fwd(x, mix, mod, ln_g, ln_b):
    S = x.shape[0]

    def body(x_ref, mix_ref, m_ref, g_ref, b_ref, xpre_ref, x1_ref, h2_ref):
        m = m_ref[...]
        xpre = ALPHA * x_ref[...] + m[GATE_T:GATE_T + 1] * mix_ref[...]
        xhat, _ = _ln_stats(xpre)
        x1 = xhat * g_ref[...] + b_ref[...]
        xpre_ref[...] = xpre
        x1_ref[...] = x1
        h2_ref[...] = (x1 * (1.0 + m[SCALE_F:SCALE_F + 1]) + m[SHIFT_F:SHIFT_F + 1]).astype(BF16)

    return rowcall(body, name="ln1_fwd", S=S, ts=512,
                   ins=[(x, "row"), (mix, "row"), (mod, "vec"), (ln_g, "vec"), (ln_b, "vec")],
                   outs=[((S, D_MODEL), F32, "row"), ((S, D_MODEL), F32, "row"), ((S, D_MODEL), BF16, "row")])


STRIP_FWD = (64, 256)
STRIP_BWD = (128, 128)


def ffn_act_fwd(up, conv_w, conv_b):
    S = up.shape[0]
    ts = 256
    STRIP_ROWS, STRIP_COLS = STRIP_FWD

    def body(u_ref, up_ref, w_ref, b_ref, o_ref, ubuf):
        ubuf[0:8] = _halo_prev(up_ref) * (pl.program_id(0) > 0).astype(F32)
        ubuf[8:8 + ts] = u_ref[...].astype(F32)

        def col_block(j, carry):
            gate = pl.ds(pl.multiple_of(j * STRIP_COLS, STRIP_COLS), STRIP_COLS)
            halves = [gate, pl.ds(pl.multiple_of(D_FF + j * STRIP_COLS, STRIP_COLS), STRIP_COLS)]
            w = [w_ref[:, c] for c in halves]
            bias = [b_ref[:, c] for c in halves]
            for r0 in range(0, ts, STRIP_ROWS):
                uc = []
                for h in range(2):
                    x = ubuf[r0:r0 + STRIP_ROWS + 8, halves[h]]
                    uc.append(bias[h] + sum(
                        w[h][t:t + 1] * (x if t == FFN_CONV - 1 else pltpu.roll(x, FFN_CONV - 1 - t, axis=0))[8:]
                        for t in range(FFN_CONV)))
                o_ref[r0:r0 + STRIP_ROWS, gate] = (_silu(uc[0]) * uc[1]).astype(BF16)
            return carry

        lax.fori_loop(0, D_FF // STRIP_COLS, col_block, 0)

    return rowcall(body, name="ffn_act_fwd", S=S, ts=ts,
                   ins=[(up, "row"), (up, "prev"), (conv_w, "vec"), (conv_b, "vec")],
                   outs=[((S, D_FF), BF16, "row")], scratch=[pltpu.VMEM((ts + 8, 2 * D_FF), F32)])[0]


def final_fwd_bwd(x1, ffn, target, mod, ln_g, ln_b):
    S = x1.shape[0]

    def body(x1_ref, f_ref, t_ref, m_ref, g_ref, b_ref, dxpre_ref, dffn_ref, loss_ref, dgate_ref, dg_ref, db_ref):
        gate = m_ref[...][GATE_F:GATE_F + 1]
        ffn_v = f_ref[...]
        xpre = ALPHA * x1_ref[...] + gate * ffn_v
        xhat, rstd = _ln_stats(xpre)
        err = xhat * g_ref[...] + b_ref[...] - t_ref[...]
        loss_ref[...] += 0.5 * jnp.sum(jnp.mean(err * err, axis=-1, keepdims=True), axis=0, keepdims=True)
        dy = err * (1.0 / D_MODEL)
        dg_ref[...] += jnp.sum(dy * xhat, axis=0, keepdims=True)
        db_ref[...] += jnp.sum(dy, axis=0, keepdims=True)
        dyg = dy * g_ref[...]
        dxpre = rstd * (dyg - jnp.mean(dyg, axis=-1, keepdims=True) - xhat * jnp.mean(dyg * xhat, axis=-1, keepdims=True))
        dxpre_ref[...] = dxpre
        dffn_ref[...] = (gate * dxpre).astype(BF16)
        dgate_ref[...] += jnp.sum(dxpre * ffn_v, axis=0, keepdims=True)

    vec = ((1, D_MODEL), F32, "acc")
    return rowcall(body, name="final_fwd_bwd", S=S, ts=512,
                   ins=[(x1, "row"), (ffn, "row"), (target, "row"), (mod, "vec"), (ln_g, "vec"), (ln_b, "vec")],
                   outs=[((S, D_MODEL), F32, "row"), ((S, D_MODEL), BF16, "row"), ((1, 1), F32, "acc"), vec, vec, vec])


def ffn_act_bwd(dact, up, conv_w, conv_b):
    S = up.shape[0]
    ts = 256
    STRIP_ROWS, STRIP_COLS = STRIP_BWD
    win_u, win_d = STRIP_ROWS + 16, STRIP_ROWS + 8

    def body(d_ref, dn_ref, u_ref, up_ref, un_ref, w_ref, b_ref, dup_ref, dw_ref, db_ref, ubuf, dbuf):
        i = pl.program_id(0)
        ubuf[0:8] = _halo_prev(up_ref) * (i > 0).astype(F32)
        ubuf[8:8 + ts] = u_ref[...].astype(F32)
        ubuf[8 + ts:16 + ts] = _halo_next(un_ref)
        dbuf[0:ts] = d_ref[...].astype(F32)
        dbuf[ts:ts + 8] = _halo_next(dn_ref) * (i < pl.num_programs(0) - 1).astype(F32)

        def col_block(j, carry):
            halves = [pl.ds(pl.multiple_of(j * STRIP_COLS, STRIP_COLS), STRIP_COLS),
                      pl.ds(pl.multiple_of(D_FF + j * STRIP_COLS, STRIP_COLS), STRIP_COLS)]
            w = [w_ref[:, c] for c in halves]
            bias = [b_ref[:, c] for c in halves]
            dw_acc = [[jnp.zeros((1, STRIP_COLS), F32) for _ in range(FFN_CONV)] for _ in halves]
            db_acc = [jnp.zeros((1, STRIP_COLS), F32) for _ in halves]
            for r0 in range(0, ts, STRIP_ROWS):
                shifted = [[x if k == 0 else pltpu.roll(x, k, axis=0) for k in range(FFN_CONV)]
                           for x in (ubuf[r0:r0 + win_u, c] for c in halves)]
                uc = [bias[h] + sum(w[h][t:t + 1] * shifted[h][FFN_CONV - 1 - t][8:8 + win_d] for t in range(FFN_CONV))
                      for h in range(2)]
                dact_w = dbuf[r0:r0 + win_d, halves[0]]
                sg, dsg = _silu_and_grad(uc[0])
                duc = [dact_w * uc[1] * dsg, dact_w * sg]
                for h in range(2):
                    dup = duc[h] * w[h][FFN_CONV - 1:FFN_CONV]
                    for t in range(FFN_CONV - 1):
                        dup = dup + pltpu.roll(duc[h], win_d - (FFN_CONV - 1 - t), axis=0) * w[h][t:t + 1]
                    dup_ref[r0:r0 + STRIP_ROWS, halves[h]] = dup[:STRIP_ROWS].astype(BF16)
                    mine = duc[h][:STRIP_ROWS]
                    db_acc[h] = db_acc[h] + jnp.sum(mine, axis=0, keepdims=True)
                    for t in range(FFN_CONV):
                        dw_acc[h][t] = dw_acc[h][t] + jnp.sum(
                            mine * shifted[h][FFN_CONV - 1 - t][8:8 + STRIP_ROWS], axis=0, keepdims=True)
            for h in range(2):
                dw_ref[:, halves[h]] += jnp.concatenate(dw_acc[h], axis=0)
                db_ref[:, halves[h]] += db_acc[h]
            return carry

        lax.fori_loop(0, D_FF // STRIP_COLS, col_block, 0)

    return rowcall(body, name="ffn_act_bwd", S=S, ts=ts,
                   ins=[(dact, "row"), (dact, "next"), (up, "row"), (up, "prev"), (up, "next"), (conv_w, "vec"), (conv_b, "vec")],
                   outs=[((S, 2 * D_FF), BF16, "row"), ((FFN_CONV, 2 * D_FF), F32, "acc"), ((1, 2 * D_FF), F32, "acc")],
                   scratch=[pltpu.VMEM((ts + 16, 2 * D_FF), F32), pltpu.VMEM((ts + 8, D_FF), F32)])


def ln1_bwd(dxpre2, dh2, xpre1, mix, mod, ln_g, ln_b):
    S = xpre1.shape[0]

    def body(d2_ref, dh_ref, xp_ref, mix_ref, m_ref, g_ref, b_ref, dxpre_ref, dmix_ref,
             dscale_ref, dshift_ref, dgate_ref, dg_ref, db_ref):
        m = m_ref[...]
        xhat, rstd = _ln_stats(xp_ref[...])
        x1 = xhat * g_ref[...] + b_ref[...]
        dh = dh_ref[...]
        dx1 = ALPHA * d2_ref[...] + dh * (1.0 + m[SCALE_F:SCALE_F + 1])
        dscale_ref[...] += jnp.sum(dh * x1, axis=0, keepdims=True)
        dshift_ref[...] += jnp.sum(dh, axis=0, keepdims=True)
        dg_ref[...] += jnp.sum(dx1 * xhat, axis=0, keepdims=True)
        db_ref[...] += jnp.sum(dx1, axis=0, keepdims=True)
        dyg = dx1 * g_ref[...]
        dxpre = rstd * (dyg - jnp.mean(dyg, axis=-1, keepdims=True) - xhat * jnp.mean(dyg * xhat, axis=-1, keepdims=True))
        dxpre_ref[...] = dxpre
        dmix_ref[...] = (m[GATE_T:GATE_T + 1] * dxpre).astype(BF16)
        dgate_ref[...] += jnp.sum(dxpre * mix_ref[...], axis=0, keepdims=True)

    vec = ((1, D_MODEL), F32, "acc")
    return rowcall(body, name="ln1_bwd", S=S, ts=512,
                   ins=[(dxpre2, "row"), (dh2, "row"), (xpre1, "row"), (mix, "row"), (mod, "vec"), (ln_g, "vec"), (ln_b, "vec")],
                   outs=[((S, D_MODEL), F32, "row"), ((S, D_MODEL), BF16, "row"), vec, vec, vec, vec, vec])


def merge_bwd(dmerged, gates_raw, b_gate, ya, yb):
    S = ya.shape[0]

    def body(d_ref, g_ref, b_ref, ya_ref, yb_ref, dya_ref, dyb_ref, dg_ref, dbg_ref):
        gt = _sigmoid(g_ref[...].astype(F32) + b_ref[...])
        d = d_ref[...].astype(F32)
        ga, gb = gt[:, :D_MODEL], gt[:, D_MODEL:]
        dya_ref[...] = (d * ga).astype(BF16)
        dyb_ref[...] = (d * gb).astype(BF16)
        dgr = jnp.concatenate([d * ya_ref[...].astype(F32) * ga * (1.0 - ga),
                               d * yb_ref[...].astype(F32) * gb * (1.0 - gb)], axis=1)
        dg_ref[...] = dgr.astype(BF16)
        dbg_ref[...] += jnp.sum(dgr, axis=0, keepdims=True)

    return rowcall(body, name="merge_bwd", S=S, ts=512,
                   ins=[(dmerged, "row"), (gates_raw, "row"), (b_gate, "vec"), (ya, "row"), (yb, "row")],
                   outs=[((S, D_MODEL), BF16, "row"), ((S, D_MODEL), BF16, "row"), ((S, 2 * D_MODEL), BF16, "row"),
                         ((1, 2 * D_MODEL), F32, "acc")])


def attn_bwd(qkv_pad, bias, do_b):
    S = qkv_pad.shape[0] - PAD_ROWS

    def body(q_ref, k_ref, v_ref, bias_ref, do_ref, dq_ref, dk_ref, dv_ref, db_ref, b_ref):
        n = pl.program_id(1)

        @pl.when(n == 0)
        def _():
            dk_ref[...] = jnp.zeros_like(dk_ref)
            dv_ref[...] = jnp.zeros_like(dv_ref)
            db_ref[...] = jnp.zeros_like(db_ref)
            b_ref[...] = jnp.full(b_ref.shape, NEG_INF, F32)
            for hh in range(HEADS_PER_GROUP):
                for qc in range(Q_CHUNKS):
                    b_ref[hh, qc * CHUNK:(qc + 1) * CHUNK, qc * CHUNK:qc * CHUNK + B_BAND] = bias_ref[hh]

        start = pl.multiple_of(n * Q_TILE, Q_TILE)
        kwin = k_ref[pl.ds(start, KEY_WIN), :]
        vwin = v_ref[pl.ds(start, KEY_WIN), :]
        qv, dov = q_ref[...], do_ref[...]
        valid = lax.broadcasted_iota(jnp.int32, (Q_TILE, KEY_WIN), 1) >= PAD_ROWS - n * Q_TILE
        dqs, dks, dvs = [], [], []
        for hh in range(HEADS_PER_GROUP):
            sl = slice(hh * B_DH, (hh + 1) * B_DH)
            p = _band_probs(qv[:, sl], kwin[:, sl], b_ref[hh], valid)
            dp = _dot1(dov[:, sl], vwin[:, sl], "nt")
            ds = p * (dp - jnp.sum(dp * p, axis=-1, keepdims=True))
            dbh = ds[0:CHUNK, 0:B_BAND]
            for qc in range(1, Q_CHUNKS):
                dbh = dbh + ds[qc * CHUNK:(qc + 1) * CHUNK, qc * CHUNK:qc * CHUNK + B_BAND]
            db_ref[hh] += dbh
            dsq = ds * (B_DH ** -0.5)
            dqs.append(_dot1(dsq, kwin[:, sl], "nn"))
            dks.append(_dot1(dsq, qv[:, sl], "tn"))
            dvs.append(_dot1(p, dov[:, sl], "tn"))
        dq_ref[...] = jnp.concatenate(dqs, axis=1).astype(BF16)
        dk_ref[pl.ds(start, KEY_WIN), :] += jnp.concatenate(dks, axis=1)
        dv_ref[pl.ds(start, KEY_WIN), :] += jnp.concatenate(dvs, axis=1)

    col = pl.BlockSpec((PAD_ROWS + S, GROUP_W), lambda g, n: (0, g))
    tile = pl.BlockSpec((Q_TILE, GROUP_W), lambda g, n: (n, g))
    return pl.pallas_call(
        body, name="attn_bwd", grid=(N_GROUPS, S // Q_TILE), in_specs=_attn_specs(S, Q_TILE) + [tile],
        out_specs=[tile, col, col, pl.BlockSpec((HEADS_PER_GROUP, CHUNK, B_BAND), lambda g, n: (g, 0, 0))],
        out_shape=[jax.ShapeDtypeStruct((S, B_W), BF16), jax.ShapeDtypeStruct((PAD_ROWS + S, B_W), F32),
                   jax.ShapeDtypeStruct((PAD_ROWS + S, B_W), F32), jax.ShapeDtypeStruct((B_HEADS, CHUNK, B_BAND), F32)],
        scratch_shapes=[pltpu.VMEM((HEADS_PER_GROUP, Q_TILE, KEY_WIN), F32)],
        compiler_params=_cparams(("parallel", "arbitrary")),
    )(qkv_pad, qkv_pad, qkv_pad, bias, do_b)


def gate_a_bwd(do_a, o_pre, z, norm_w):
    S = o_pre.shape[0]

    def body(d_ref, o_ref, z_ref, nw_ref, dop_ref, dz_ref, dnw_ref):
        nw = nw_ref[...]
        acc = jnp.zeros((1, A_DK), F32)
        for h in range(A_HEADS):
            sl = slice(h * A_DK, (h + 1) * A_DK)
            oh, zh, dh = o_ref[:, sl], z_ref[:, sl].astype(F32), d_ref[:, sl].astype(F32)
            r = lax.rsqrt(jnp.mean(oh * oh, axis=-1, keepdims=True) + RMS_EPS)
            sz, dsz = _silu_and_grad(zh)
            dz_ref[:, sl] = (dh * oh * r * nw * dsz).astype(BF16)
            acc = acc + jnp.sum(dh * oh * r * sz, axis=0, keepdims=True)
            t = dh * nw * sz
            dop_ref[:, sl] = r * t - oh * (r * r * r) * jnp.mean(t * oh, axis=-1, keepdims=True)
        dnw_ref[...] += acc

    return rowcall(body, name="gate_a_bwd", S=S, ts=512,
                   ins=[(do_a, "row"), (o_pre, "row"), (z, "row"), (norm_w, "vec")],
                   outs=[((S, A_W), F32, "row"), ((S, A_W), BF16, "row"), ((1, A_DK), F32, "acc")])


def delta_bwd(q, k, v, beta, g, sprev, tinv, do):
    S = q.shape[0]
    n_chunks = S // CHUNK

    def body(q_ref, k_ref, v_ref, beta_ref, g_ref, sprev_ref, t_ref, do_ref,
             dq_ref, dk_ref, dv_ref, dbeta_ref, dg_ref, dstate_ref):
        @pl.when(pl.program_id(0) == 0)
        def _():
            dstate_ref[...] = jnp.zeros_like(dstate_ref)

        mk = _tri_masks()
        causal, strict, eye = mk["causal"], mk["strict"], mk["eye"]
        blk_end = (lax.broadcasted_iota(jnp.int32, (GROUP_ROWS, 1), 0) & (CHUNK - 1)) == CHUNK - 1
        lane = lax.broadcasted_iota(jnp.int32, (CHUNK, A_HEADS), 1)
        betav, gv = beta_ref[...], g_ref[...]
        dbeta_t = jnp.zeros((CHUNK, A_HEADS), F32)
        dg_t = jnp.zeros((CHUNK, A_HEADS), F32)
        groups, heads = range(N_HEAD_GROUPS), range(HEAD_GROUP)
        st = [dict() for _ in groups]

        def local_part(grp, s):
            s["qs"], s["ks"], s["vs"] = _stack_heads(q_ref, grp), _stack_heads(k_ref, grp), _stack_heads(v_ref, grp)
            s["dos"] = _stack_heads(do_ref, grp)
            s["bs"] = _stack_cols(betav, grp)
            s["loc"] = loc = _delta_local(s["qs"], s["ks"], s["vs"], s["bs"], _stack_cols(gv, grp), mk)
            s["tinv"] = t_ref[0, grp]
            s["rhs"] = jnp.concatenate([loc["vb"], loc["y"]], axis=1)
            s["uw"] = _dot3(s["tinv"], s["rhs"], "nn")

        def state_part(grp, s):
            loc, uw, dos, qs = s["loc"], s["uw"], s["dos"], s["qs"]
            gam, kd, gl, gc = loc["gam"], loc["kd"], loc["gl"], loc["gc"]
            qg = qs * gam
            egl = jnp.exp(gl)
            hid = [grp * HEAD_GROUP + j for j in heads]
            s0 = [sprev_ref[0, h] for h in hid]
            ds1 = [dstate_ref[h] for h in hid]
            w = [_head_rows(uw, j)[:, A_DK:] for j in heads]
            vn = [_head_rows(uw, j)[:, :A_DK] - _dot1(w[j], s0[j], "nn") for j in heads]
            vns = jnp.concatenate(vn, axis=0)
            dvn_local = _dot1(loc["p"], dos, "tn")
            dvn = [_head_rows(dvn_local, j) + _dot1(_head_rows(kd, j), ds1[j], "nn") for j in heads]
            dvns = jnp.concatenate(dvn, axis=0)
            s["dp"] = jnp.where(causal, _dot1(dos, vns, "nt"), 0.0)
            dqg = jnp.concatenate([_dot1(_head_rows(dos, j), s0[j], "nt") for j in heads], axis=0)
            s["dq"] = dqg * gam
            dgc = jnp.sum(dqg * qg, axis=-1, keepdims=True)
            for j in heads:
                dstate_ref[hid[j]] = (_dot1(_head_rows(qg, j), _head_rows(dos, j), "tn")
                                      + egl[(j + 1) * CHUNK - 1:(j + 1) * CHUNK] * ds1[j] - _dot1(w[j], dvn[j], "tn"))
            dkd = jnp.concatenate([_dot1(vn[j], ds1[j], "nt") for j in heads], axis=0)
            s["dk"] = dkd * jnp.exp(gl - gc)
            t1 = jnp.sum(dkd * kd, axis=-1, keepdims=True)
            dgl = jnp.concatenate(
                [jnp.broadcast_to(jnp.sum(_head_rows(t1, j), axis=0, keepdims=True)
                                  + jnp.sum(jnp.sum(ds1[j] * s0[j], axis=-1, keepdims=True), axis=0, keepdims=True)
                                  * egl[(j + 1) * CHUNK - 1:(j + 1) * CHUNK], (CHUNK, 1)) for j in heads], axis=0)
            s["dgc"] = dgc - t1 + jnp.where(blk_end, dgl, 0.0)
            s["duw"] = jnp.concatenate(
                [dvns, jnp.concatenate([-_dot1(dvn[j], s0[j], "nt") for j in heads], axis=0)], axis=1)

        def solve_part(grp, s):
            s["dvby"] = _dot3(s["tinv"], s["duw"], "tn")
            s["dt"] = _dot3(s["duw"], s["rhs"], "nt")

        def inverse_part_a(grp, s):
            s["tdt"] = _dot3(s["tinv"], s["dt"], "tn")

        def inverse_part_b(grp, s):
            s["da"] = jnp.where(strict, -_dot3(s["tdt"], s["tinv"], "nt"), 0.0)

        def finish(grp, s):
            loc, qs, ks, vs, bs, da, dp, dvby = s["loc"], s["qs"], s["ks"], s["vs"], s["bs"], s["da"], s["dp"], s["dvby"]
            gam, decay = loc["gam"], loc["decay"]
            dm = da * decay
            dn = dp * decay
            e = da * loc["a"] + dp * loc["p"]
            dgc = s["dgc"] + jnp.sum(e, axis=1, keepdims=True) - _row_to_col(jnp.sum(e, axis=0, keepdims=True), eye)
            dy = dvby[:, A_DK:]
            dvb = dvby[:, :A_DK]
            dkb = _dot1(dm, ks, "nn") + dy * gam
            dk = s["dk"] + _dot1(dm, loc["kb"], "tn") + _dot1(dn, qs, "tn") + dkb * bs
            dq = s["dq"] + _dot1(dn, ks, "nn")
            dgc = dgc + jnp.sum(dy * loc["y"], axis=-1, keepdims=True)
            dbeta = jnp.sum(dkb * ks, axis=-1, keepdims=True) + jnp.sum(dvb * vs, axis=-1, keepdims=True)
            dv = dvb * bs
            dgs = jnp.sum(jnp.where(mk["upper"], _col_to_row(dgc, eye), 0.0), axis=1, keepdims=True)
            for j in heads:
                h = grp * HEAD_GROUP + j
                sl = slice(h * A_DK, (h + 1) * A_DK)
                dq_ref[:, sl] = _head_rows(dq, j)
                dk_ref[:, sl] = _head_rows(dk, j)
                dv_ref[:, sl] = _head_rows(dv, j)
            s["dbeta"], s["dgs"] = dbeta, dgs

        for stage in (local_part, state_part, solve_part, inverse_part_a, inverse_part_b, finish):
            for grp in groups:
                stage(grp, st[grp])
        for grp in groups:
            for j in heads:
                h = grp * HEAD_GROUP + j
                dbeta_t = dbeta_t + jnp.where(lane == h, _head_rows(st[grp]["dbeta"], j), 0.0)
                dg_t = dg_t + jnp.where(lane == h, _head_rows(st[grp]["dgs"], j), 0.0)
        dbeta_ref[...] = dbeta_t
        dg_ref[...] = dg_t

    rev = lambda n: (n_chunks - 1 - n, 0)
    rev4 = lambda n: (n_chunks - 1 - n, 0, 0, 0)
    tile = pl.BlockSpec((CHUNK, A_W), rev)
    small = pl.BlockSpec((CHUNK, A_HEADS), rev)
    return pl.pallas_call(
        body, name="delta_bwd", grid=(n_chunks,),
        in_specs=[tile, tile, tile, small, small, pl.BlockSpec((1, A_HEADS, A_DK, A_DK), rev4),
                  pl.BlockSpec((1, N_HEAD_GROUPS, GROUP_ROWS, GROUP_ROWS), rev4), tile],
        out_specs=[tile, tile, tile, small, small],
        out_shape=[jax.ShapeDtypeStruct((S, A_W), F32)] * 3 + [jax.ShapeDtypeStruct((S, A_HEADS), F32)] * 2,
        scratch_shapes=[pltpu.VMEM((A_HEADS, A_DK, A_DK), F32)],
        compiler_params=_cparams(("arbitrary",)),
    )(q, k, v, beta, g, sprev, tinv, do)


def _prep_a_dpre(raw, raw_prev, w, dq, dk, dv):
    y, dy_dpre = _prep_a_core(raw, raw_prev, w)
    parts = []
    for h in range(A_HEADS):
        yq = y[:, h * A_DK:(h + 1) * A_DK]
        dqh = dq[:, h * A_DK:(h + 1) * A_DK]
        rq = lax.rsqrt(jnp.sum(yq * yq, axis=-1, keepdims=True) + L2_EPS)
        parts.append((A_DK ** -0.5) * (rq * dqh - yq * (rq * rq * rq) * jnp.sum(dqh * yq, axis=-1, keepdims=True)))
    for h in range(A_HEADS):
        yk = y[:, A_W + h * A_DK:A_W + (h + 1) * A_DK]
        dkh = dk[:, h * A_DK:(h + 1) * A_DK]
        rk = lax.rsqrt(jnp.sum(yk * yk, axis=-1, keepdims=True) + L2_EPS)
        parts.append(rk * dkh - yk * (rk * rk * rk) * jnp.sum(dkh * yk, axis=-1, keepdims=True))
    parts.append(dv)
    return jnp.concatenate(parts, axis=1) * dy_dpre


def prep_a_bwd(qkv_raw, ba, conv_a, a_log, dt_bias, dq, dk, dv, dbeta, dg):
    S = qkv_raw.shape[0]
    ts = 256

    def body(x_ref, xp_ref, xn_ref, ba_ref, w_ref, al_ref, dt_ref, dq_ref, dqn_ref, dk_ref, dkn_ref, dv_ref, dvn_ref,
             dbeta_ref, dg_ref, draw_ref, dba_ref, dw_ref, dal_ref, ddt_ref):
        i = pl.program_id(0)
        first = (i > 0).astype(F32)
        last = (i < pl.num_programs(0) - 1).astype(F32)
        w = w_ref[...]
        cur, prev = x_ref[...].astype(F32), _halo_prev(xp_ref) * first
        dpre = _prep_a_dpre(cur, prev, w, dq_ref[...], dk_ref[...], dv_ref[...])
        dpre_n = _prep_a_dpre(_halo_next(xn_ref), cur[ts - 8:ts], w, _halo_next(dqn_ref), _halo_next(dkn_ref),
                              _halo_next(dvn_ref)) * last
        for j in range(A_CONV):
            dw_ref[j:j + 1, :] += jnp.sum(dpre * _shift_down(cur, prev, A_CONV - 1 - j), axis=0, keepdims=True)
        draw = dpre * w[A_CONV - 1:A_CONV]
        for j in range(A_CONV - 1):
            draw = draw + _shift_up(dpre, dpre_n, A_CONV - 1 - j) * w[j:j + 1]
        draw_ref[...] = draw.astype(BF16)
        bav = ba_ref[...]
        beta = _sigmoid(bav[:, 0:A_HEADS])
        xa = bav[:, A_HEADS:2 * A_HEADS] + dt_ref[...]
        nexp = -jnp.exp(al_ref[...])
        dgv = dg_ref[...]
        da = dgv * nexp * _sigmoid(xa)
        dba_ref[:, 0:A_HEADS] = dbeta_ref[...] * beta * (1.0 - beta)
        dba_ref[:, A_HEADS:2 * A_HEADS] = da
        dal_ref[...] += jnp.sum(dgv * nexp * _softplus(xa), axis=0, keepdims=True)
        ddt_ref[...] += jnp.sum(da, axis=0, keepdims=True)

    return rowcall(
        body, name="prep_a_bwd", S=S, ts=ts,
        ins=[(qkv_raw, "row"), (qkv_raw, "prev"), (qkv_raw, "next"), (ba, "row"), (conv_a, "vec"), (a_log, "vec"),
             (dt_bias, "vec"), (dq, "row"), (dq, "next"), (dk, "row"), (dk, "next"), (dv, "row"), (dv, "next"),
             (dbeta, "row"), (dg, "row")],
        outs=[((S, 3 * A_W), BF16, "row"), ((S, 2 * A_HEADS), F32, "row"), ((A_CONV, 3 * A_W), F32, "acc"),
              ((1, A_HEADS), F32, "acc"), ((1, A_HEADS), F32, "acc")])


def grad_x_final(dh1, x, dxpre1, mod):
    S = x.shape[0]

    def body(dh_ref, x_ref, dx_ref, m_ref, gx_ref, dscale_ref, dshift_ref):
        dh = dh_ref[...]
        gx_ref[...] = ALPHA * dx_ref[...] + dh * (1.0 + m_ref[...][SCALE_T:SCALE_T + 1])
        dscale_ref[...] += jnp.sum(dh * x_ref[...], axis=0, keepdims=True)
        dshift_ref[...] += jnp.sum(dh, axis=0, keepdims=True)

    vec = ((1, D_MODEL), F32, "acc")
    return rowcall(body, name="grad_x_final", S=S, ts=512, ins=[(dh1, "row"), (x, "row"), (dxpre1, "row"), (mod, "vec")],
                   outs=[((S, D_MODEL), F32, "row"), vec, vec])


_C_QKV, _C_Z, _C_BA, _C_QKVB, _C_G = 0, 3 * A_W, 4 * A_W, 4 * A_W + 2 * A_HEADS, 4 * A_W + 2 * A_HEADS + 3 * B_W
BA_PAD = 128


def split_w_in(w_in):
    ba = jnp.pad(w_in[:, _C_BA:_C_QKVB], ((0, 0), (0, BA_PAD - 2 * A_HEADS)))
    return dict(qkv=w_in[:, _C_QKV:_C_Z], z=w_in[:, _C_Z:_C_BA], ba=ba, qkvb=w_in[:, _C_QKVB:_C_G], g=w_in[:, _C_G:])


def join_w_in(p):
    return jnp.concatenate([p["qkv"], p["z"], p["ba"][:, :2 * A_HEADS], p["qkvb"], p["g"]], axis=1)


def forward_local(x, target, mod, w, sm, late_weights=None):
    h1 = modulate(x, mod, SHIFT_T, SCALE_T, "mod_t")
    qkv_raw = mm(h1, w["qkv"], mode="nn", out_dtype=BF16, name="proj_qkv")
    z = mm(h1, w["z"], mode="nn", out_dtype=BF16, name="proj_z")
    ba = mm(h1, w["ba"], mode="nn", out_dtype=F32, name="proj_ba")
    qkvb = mm(h1, w["qkvb"], mode="nn", out_dtype=BF16, name="proj_qkvb")
    gates_raw = mm(h1, w["g"], mode="nn", out_dtype=BF16, name="proj_g")
    q, k, v, beta, g = prep_a_fwd(qkv_raw, ba, sm["conv_a"], sm["a_log"], sm["dt_bias"])
    o_pre, sprev, tinv = delta_fwd(q, k, v, beta, g)
    o_a = gate_a_fwd(o_pre, z, sm["norm_a"])
    qkv_pad = jnp.pad(qkvb, ((PAD_ROWS, 0), (0, 0)))
    bias = jnp.transpose(bias_expand(sm["rel_bias"]), (1, 0, 2))
    o_b = attn_fwd(qkv_pad, bias)
    if late_weights is not None:
        w = dict(w, **late_weights(o_b))
    ya = mm(o_a, w["branch_a"], mode="nn", out_dtype=BF16, name="branch_a")
    yb = mm(o_b, w["branch_b"], mode="nn", out_dtype=BF16, name="branch_b")
    merged = merge_fwd(gates_raw, sm["b_gate"], ya, yb)
    mix = mm(merged, w["o"], mode="nn", out_dtype=F32, name="mix")
    xpre1, x1, h2 = ln1_fwd(x, mix, mod, sm["ln1_g"], sm["ln1_b"])
    up = mm(h2, w["up"], mode="nn", out_dtype=BF16, name="ffn_up", b_shards=True)
    act = ffn_act_fwd(up, sm["conv_ffn"], sm["b_conv_ffn"])
    ffn = mm(act, w["down"], mode="nn", out_dtype=F32, name="ffn_down")
    dxpre2, dffn, loss, dgate_f, dln2_g, dln2_b = final_fwd_bwd(x1, ffn, target, mod, sm["ln2_g"], sm["ln2_b"])
    saved = dict(h1=h1, qkv_raw=qkv_raw, z=z, ba=ba, gates_raw=gates_raw, q=q, k=k, v=v, beta=beta, g=g,
                 o_pre=o_pre, sprev=sprev, tinv=tinv, o_a=o_a, qkv_pad=qkv_pad, bias=bias, o_b=o_b, ya=ya, yb=yb,
                 merged=merged, mix=mix, xpre1=xpre1, x1=x1, h2=h2, up=up, act=act, ffn=ffn, w=w)
    return loss, dxpre2, dffn, dict(gate_f=dgate_f, ln2_g=dln2_g, ln2_b=dln2_b), saved


def backward_local(x, mod, sm, dxpre2, dffn, fin, sv, hooks=None):
    w = sv["w"]
    dact = mm(dffn, w["down"], mode="nt", out_dtype=BF16, name="d_act")
    gw_down = mm(sv["act"], dffn, mode="tn", out_dtype=BF16, name="gw_down")
    dup, dconv_ffn, db_conv_ffn = ffn_act_bwd(dact, sv["up"], sm["conv_ffn"], sm["b_conv_ffn"])
    dh2 = mm(dup, w["up"], mode="nt", out_dtype=F32, name="d_h2", b_shards=True)
    gw_up = mm(sv["h2"], dup, mode="tn", out_dtype=BF16, name="gw_up", out_shards=N_CHIPS)
    dxpre1, dmix, dsc_f, dsh_f, dgate_t, dln1_g, dln1_b = ln1_bwd(
        dxpre2, dh2, sv["xpre1"], sv["mix"], mod, sm["ln1_g"], sm["ln1_b"])
    dmerged = mm(dmix, w["o"], mode="nt", out_dtype=BF16, name="d_merged")
    gw_o = mm(sv["merged"], dmix, mode="tn", out_dtype=BF16, name="gw_o")
    dya, dyb, dgates, db_gate = merge_bwd(dmerged, sv["gates_raw"], sm["b_gate"], sv["ya"], sv["yb"])
    do_a = mm(dya, w["branch_a"], mode="nt", out_dtype=BF16, name="d_oa")
    gw_branch_a = mm(sv["o_a"], dya, mode="tn", out_dtype=BF16, name="gw_branch_a")
    do_b = mm(dyb, w["branch_b"], mode="nt", out_dtype=BF16, name="d_ob")
    gw_branch_b = mm(sv["o_b"], dyb, mode="tn", out_dtype=BF16, name="gw_branch_b")
    bias = sv["bias"]
    if hooks is not None:
        bias = bias + hooks["late_start"](dict(w_branch_a=gw_branch_a, w_branch_b=gw_branch_b, w_o=gw_o, w_up=gw_up,
                                               w_down=gw_down))[0, 0]
    dq_b, dk_pad, dv_pad, dbias = attn_bwd(sv["qkv_pad"], bias, do_b)
    if hooks is not None:
        dbias = dbias + hooks["late_finish"](dq_b)[0, 0]
    dqkvb = jnp.concatenate([dq_b, dk_pad[PAD_ROWS:].astype(BF16), dv_pad[PAD_ROWS:].astype(BF16)], axis=1)
    drel_bias = bias_reduce(jnp.transpose(dbias, (1, 0, 2)))
    do_pre, dz, dnorm_a = gate_a_bwd(do_a, sv["o_pre"], sv["z"], sm["norm_a"])
    dq, dk, dv, dbeta, dg = delta_bwd(sv["q"], sv["k"], sv["v"], sv["beta"], sv["g"], sv["sprev"], sv["tinv"], do_pre)
    dqkv_raw, dba16, dconv_a, da_log, ddt_bias = prep_a_bwd(
        sv["qkv_raw"], sv["ba"], sm["conv_a"], sm["a_log"], sm["dt_bias"], dq, dk, dv, dbeta, dg)
    dba = jnp.pad(dba16, ((0, 0), (0, BA_PAD - 2 * A_HEADS))).astype(BF16)
    pieces = dict(qkv=dqkv_raw, z=dz, ba=dba, qkvb=dqkvb, g=dgates)
    gw_in = join_w_in({key: mm(sv["h1"], dpiece, mode="tn", out_dtype=BF16, name="gw_in_" + key)
                       for key, dpiece in pieces.items()})
    w_ba = w["ba"]
    w_z = w["z"]
    if hooks is not None:
        w_ba = w_ba + hooks["w_in_start"](gw_in)[0, 0].astype(BF16)
    dh1 = mm(pieces["ba"], w_ba, mode="nt", out_dtype=F32, name="d_h1_ba")
    dh1 = mm(pieces["qkv"], w["qkv"], mode="nt", out_dtype=F32, name="d_h1_qkv", acc_in=dh1)
    if hooks is not None:
        w_z = w_z + hooks["w_in_finish"](dh1)[0, 0].astype(BF16)
    dh1 = mm(pieces["z"], w_z, mode="nt", out_dtype=F32, name="d_h1_z", acc_in=dh1)
    for key in ("qkvb", "g"):
        dh1 = mm(pieces[key], w[key], mode="nt", out_dtype=F32, name="d_h1_" + key, acc_in=dh1)
    grad_x, dsc_t, dsh_t = grad_x_final(dh1, x, dxpre1, mod)
    dmod = jnp.concatenate([dsh_t, dsc_t, dgate_t, dsh_f, dsc_f, fin["gate_f"]], axis=0)
    gw = dict(w_in=gw_in, w_branch_a=gw_branch_a, w_branch_b=gw_branch_b, w_o=gw_o, w_up=gw_up, w_down=gw_down)
    gs = dict(b_gate=db_gate, conv_a=dconv_a, a_log=da_log, dt_bias=ddt_bias, norm_a=dnorm_a, rel_bias=drel_bias,
              ln1_g=dln1_g, ln1_b=dln1_b, conv_ffn=dconv_ffn, b_conv_ffn=db_conv_ffn, ln2_g=fin["ln2_g"], ln2_b=fin["ln2_b"])
    return grad_x, dmod, gw, gs


MESH = pl.DeviceIdType.MESH
ANY = pl.BlockSpec(memory_space=pl.ANY)
WHOLE_VMEM = pl.BlockSpec(memory_space=pltpu.VMEM)


def _place():
    return lax.axis_index("x"), lax.axis_index("y"), lax.axis_index("c")


def allgather8(blk, name):
    m_per, n = blk.shape

    def body(x_ref, out_ref, send_sems, recv_sems, local_sem):
        x, y, c = _place()
        me, sibling = (x, y, c), (x, y, 1 - c)
        chips = [(1 - x, y), (x, 1 - y), (1 - x, 1 - y)]

        def rows(px, py, pc):
            return out_ref.at[pl.ds((4 * px + 2 * py + pc) * m_per, m_per), :]

        def copy(k, block, to, src=None):
            return pltpu.make_async_remote_copy(
                src_ref=rows(*block) if src is None else src, dst_ref=rows(*block),
                send_sem=send_sems.at[k], recv_sem=recv_sems.at[k], device_id=to, device_id_type=MESH)

        mine = pltpu.make_async_copy(x_ref, rows(*me), local_sem)
        mine.start()
        first = [copy(0, me, sibling, src=x_ref)]
        first += [copy(1 + j, me, (*chip, c), src=x_ref) for j, chip in enumerate(chips)]
        for cp in first:
            cp.start()
        passed = [copy(4 + j, (*chip, c), sibling) for j, chip in enumerate(chips)]
        for j, chip in enumerate(chips):
            copy(1 + j, (*chip, c), me).wait_recv()
            passed[j].start()
        copy(0, sibling, me).wait_recv()
        for j, chip in enumerate(chips):
            copy(4 + j, (*chip, 1 - c), me).wait_recv()
        for cp in first + passed:
            cp.wait_send()
        mine.wait()

    return pl.pallas_call(
        body, name=name, out_shape=jax.ShapeDtypeStruct((N_DEV * m_per, n), blk.dtype),
        in_specs=[WHOLE_VMEM], out_specs=WHOLE_VMEM,
        scratch_shapes=[pltpu.SemaphoreType.DMA((7,)), pltpu.SemaphoreType.DMA((7,)), pltpu.SemaphoreType.DMA],
    )(blk)


def _chip_peers(x, y):
    return [(1 - x, y), (x, 1 - y), (1 - x, 1 - y)]


def chip_exchange(arrs, name, scatter):
    n = len(arrs)

    def body(*refs):
        ins, outs = refs[:n], refs[n:2 * n]
        send_sems, recv_sems, local_sems = refs[2 * n:]
        x, y, c = _place()
        me = 2 * x + y
        sibling = (x, y, 1 - c)
        peers = _chip_peers(x, y)

        def half(ref, which):
            r2 = ref.shape[0] // 2
            return ref.at[pl.ds(which * r2, r2), :]

        def outgoing(a, chip):
            return ins[a].at[chip] if scatter else ins[a]

        def copy(k, src, dst, to):
            return pltpu.make_async_remote_copy(src_ref=src, dst_ref=dst, send_sem=send_sems.at[k],
                                                recv_sem=recv_sems.at[k], device_id=to, device_id_type=MESH)

        started, local = [], []
        for a in range(n):
            lc = pltpu.make_async_copy(outgoing(a, me), outs[a].at[me], local_sems.at[a])
            lc.start()
            local.append(lc)
            for j, (px, py) in enumerate(peers):
                cp = copy(6 * a + j, half(outgoing(a, 2 * px + py), c), half(outs[a].at[me], c), (px, py, c))
                cp.start()
                started.append(cp)
        for a in range(n):
            for j, (px, py) in enumerate(peers):
                landed = half(outs[a].at[2 * px + py], c)
                copy(6 * a + j, landed, landed, (px, py, c)).wait_recv()
                relay = copy(6 * a + 3 + j, landed, landed, sibling)
                relay.start()
                started.append(relay)
        for a in range(n):
            for j, (px, py) in enumerate(peers):
                other = half(outs[a].at[2 * px + py], 1 - c)
                copy(6 * a + 3 + j, other, other, sibling).wait_recv()
        for cp in started:
            cp.wait_send()
        for lc in local:
            lc.wait()

    out_shape = [jax.ShapeDtypeStruct(a.shape if scatter else (N_CHIPS,) + a.shape, a.dtype) for a in arrs]
    return pl.pallas_call(
        body, name=name, out_shape=out_shape, in_specs=[ANY] * n, out_specs=[ANY] * n,
        scratch_shapes=[pltpu.SemaphoreType.DMA((6 * n,)), pltpu.SemaphoreType.DMA((6 * n,)), pltpu.SemaphoreType.DMA((n,))],
    )(*arrs)


HBM_SPEC = pl.BlockSpec(memory_space=pltpu.HBM)
SEM_SPEC = pl.BlockSpec(memory_space=pltpu.SEMAPHORE)
SIDE_EFFECT = pltpu.SideEffectType.DATAFLOW_SIDE_EFFECTING


def _in_hbm(a):
    return pltpu.with_memory_space_constraint(a, pltpu.HBM)


def exchange_start(arrs, name, scatter, after):
    n = len(arrs)
    lands = [lax.empty(a.shape if scatter else (N_CHIPS,) + a.shape, a.dtype) for a in arrs]

    def body(*refs):
        ins, zones = refs[:n], refs[n:2 * n]
        send_sems, recv_sems, token = refs[2 * n + 1], refs[2 * n + 2], refs[-1]
        x, y, c = _place()
        me = 2 * x + y
        for a in range(n):
            for j, (px, py) in enumerate(_chip_peers(x, y)):
                pltpu.make_async_remote_copy(
                    src_ref=ins[a].at[2 * px + py] if scatter else ins[a], dst_ref=zones[a].at[me],
                    send_sem=send_sems.at[3 * a + j], recv_sem=recv_sems.at[3 * a + j],
                    device_id=(px, py, c), device_id_type=MESH).start()
        token[...] = jnp.zeros_like(token)

    res = pl.pallas_call(
        body, name=name,
        out_shape=[pltpu.SemaphoreType.DMA((3 * n,)), pltpu.SemaphoreType.DMA((3 * n,))]
        + [pltpu.HBM(a.shape, a.dtype) for a in arrs] + [pltpu.HBM(z.shape, z.dtype) for z in lands]
        + [jax.ShapeDtypeStruct((8, 128), F32)],
        in_specs=[HBM_SPEC] * (2 * n) + [ANY], out_specs=[SEM_SPEC, SEM_SPEC] + [HBM_SPEC] * (2 * n) + [WHOLE_VMEM],
        input_output_aliases={i: 2 + i for i in range(2 * n)},
        compiler_params=pltpu.CompilerParams(has_side_effects=SIDE_EFFECT),
    )(*[_in_hbm(a) for a in arrs], *[_in_hbm(z) for z in lands], after)
    return dict(send=res[0], recv=res[1], src=res[2:2 + n], zones=res[2 + n:2 + 2 * n], token=res[-1], scatter=scatter)


def exchange_wait(handle, name, after):
    srcs, zones, scatter = handle["src"], handle["zones"], handle["scatter"]
    n = len(srcs)

    def body(*refs):
        ins, lands = refs[:n], refs[n:2 * n]
        send_sems, recv_sems = refs[2 * n], refs[2 * n + 1]
        x, y, c = _place()
        me = 2 * x + y
        for a in range(n):
            for j, (px, py) in enumerate(_chip_peers(x, y)):
                cp = pltpu.make_async_remote_copy(
                    src_ref=ins[a].at[me] if scatter else ins[a], dst_ref=lands[a].at[2 * px + py],
                    send_sem=send_sems.at[3 * a + j], recv_sem=recv_sems.at[3 * a + j],
                    device_id=(px, py, c), device_id_type=MESH)
                cp.wait_send()
                cp.wait_recv()

    res = pl.pallas_call(
        body, name=name, out_shape=[pltpu.HBM(a.shape, a.dtype) for a in list(srcs) + list(zones)],
        in_specs=[HBM_SPEC] * (2 * n) + [SEM_SPEC, SEM_SPEC, ANY], out_specs=[HBM_SPEC] * (2 * n),
        input_output_aliases={i: i for i in range(2 * n)},
        compiler_params=pltpu.CompilerParams(has_side_effects=SIDE_EFFECT),
    )(*srcs, *zones, handle["send"], handle["recv"], after)
    return res[n:]


def swap_start(arrs, name, after):
    n = len(arrs)
    lands = [lax.empty(a.shape, a.dtype) for a in arrs]

    def body(*refs):
        ins, zones = refs[:n], refs[n:2 * n]
        send_sems, recv_sems, token = refs[2 * n + 1], refs[2 * n + 2], refs[-1]
        x, y, c = _place()
        for a in range(n):
            pltpu.make_async_remote_copy(src_ref=ins[a], dst_ref=zones[a], send_sem=send_sems.at[a], recv_sem=recv_sems.at[a],
                                         device_id=(x, y, 1 - c), device_id_type=MESH).start()
        token[...] = jnp.zeros_like(token)

    res = pl.pallas_call(
        body, name=name,
        out_shape=[pltpu.SemaphoreType.DMA((n,)), pltpu.SemaphoreType.DMA((n,))]
        + [pltpu.HBM(a.shape, a.dtype) for a in arrs] * 2 + [jax.ShapeDtypeStruct((8, 128), F32)],
        in_specs=[HBM_SPEC] * (2 * n) + [ANY], out_specs=[SEM_SPEC, SEM_SPEC] + [HBM_SPEC] * (2 * n) + [WHOLE_VMEM],
        input_output_aliases={i: 2 + i for i in range(2 * n)},
        compiler_params=pltpu.CompilerParams(has_side_effects=SIDE_EFFECT),
    )(*[_in_hbm(a) for a in arrs], *[_in_hbm(z) for z in lands], after)
    return dict(send=res[0], recv=res[1], src=res[2:2 + n], zones=res[2 + n:2 + 2 * n], token=res[-1])


def swap_wait(handle, name, after):
    srcs, zones = handle["src"], handle["zones"]
    n = len(srcs)

    def body(*refs):
        ins, lands = refs[:n], refs[n:2 * n]
        send_sems, recv_sems = refs[2 * n], refs[2 * n + 1]
        x, y, c = _place()
        for a in range(n):
            cp = pltpu.make_async_remote_copy(src_ref=ins[a], dst_ref=lands[a], send_sem=send_sems.at[a],
                                              recv_sem=recv_sems.at[a], device_id=(x, y, 1 - c), device_id_type=MESH)
            cp.wait_send()
            cp.wait_recv()

    res = pl.pallas_call(
        body, name=name, out_shape=[pltpu.HBM(a.shape, a.dtype) for a in list(srcs) + list(zones)],
        in_specs=[HBM_SPEC] * (2 * n) + [SEM_SPEC, SEM_SPEC, ANY], out_specs=[HBM_SPEC] * (2 * n),
        input_output_aliases={i: i for i in range(2 * n)},
        compiler_params=pltpu.CompilerParams(has_side_effects=SIDE_EFFECT),
    )(*srcs, *zones, handle["send"], handle["recv"], after)
    return res[:n], res[n:]


TILE_BYTES = 2 * 1024 * 1024


def _row_tile(rows, row_bytes):
    if rows * row_bytes <= TILE_BYTES or rows % 8:
        return rows
    best = 8
    for t in range(8, rows + 1, 8):
        if rows % t == 0 and t * row_bytes <= TILE_BYTES:
            best = t
    return best


def pair_add(a, b, name):
    shape = a.shape
    a, b = a.reshape(-1, shape[-1]), b.reshape(-1, shape[-1])
    R, C = a.shape
    tr = _row_tile(R, C * 4)

    def body(a_ref, b_ref, o_ref):
        o_ref[...] = (a_ref[...].astype(F32) + b_ref[...].astype(F32)).astype(BF16)

    spec = pl.BlockSpec((tr, C), lambda i: (i, 0))
    return pl.pallas_call(body, name=name, grid=(R // tr,), in_specs=[spec, spec], out_specs=spec,
                          out_shape=jax.ShapeDtypeStruct((R, C), BF16), compiler_params=_cparams(("parallel",)))(a, b).reshape(shape)


def sum_lead(parts, name):
    K, R, C = parts.shape
    tr = _row_tile(R, C * 4)

    def body(p_ref, o_ref):
        acc = p_ref[0].astype(F32)
        for j in range(1, K):
            acc = acc + p_ref[j].astype(F32)
        o_ref[...] = acc

    return pl.pallas_call(
        body, name=name, grid=(R // tr,), in_specs=[pl.BlockSpec((K, tr, C), lambda i: (0, i, 0))],
        out_specs=pl.BlockSpec((tr, C), lambda i: (i, 0)), out_shape=jax.ShapeDtypeStruct((R, C), F32),
        compiler_params=_cparams(("parallel",)))(parts)


def adamw(w, g, m, v, name):
    R, C = w.shape
    tr = _row_tile(R, C * 4)

    def body(w_ref, g_ref, m_ref, v_ref, d_ref, mo_ref, vo_ref):
        gv = g_ref[...]
        m2 = ADAM_B1 * m_ref[...] + (1.0 - ADAM_B1) * gv
        v2 = ADAM_B2 * v_ref[...] + (1.0 - ADAM_B2) * (gv * gv)
        m_hat = m2 / (1.0 - ADAM_B1 ** ADAM_STEP)
        v_hat = v2 / (1.0 - ADAM_B2 ** ADAM_STEP)
        d_ref[...] = -ADAM_LR * (m_hat / (jnp.sqrt(v_hat) + ADAM_EPS) + ADAM_WD * w_ref[...])
        mo_ref[...] = m2
        vo_ref[...] = v2

    spec = pl.BlockSpec((tr, C), lambda i: (i, 0))
    return pl.pallas_call(body, name=name, grid=(R // tr,), in_specs=[spec] * 4, out_specs=[spec] * 3,
                          out_shape=[jax.ShapeDtypeStruct((R, C), F32)] * 3, compiler_params=_cparams(("parallel",)))(w, g, m, v)


LANES = 1024


def _pack(arrs, rows):
    out, offs, r = [], [], 0
    for a in arrs:
        flat = a.reshape(-1)
        nr = -(-flat.shape[0] // LANES)
        out.append(jnp.pad(flat, (0, nr * LANES - flat.shape[0])))
        offs.append(r)
        r += nr
    assert r <= rows, (r, rows)
    out.append(jnp.zeros(((rows - r) * LANES,), F32))
    return jnp.concatenate(out).reshape(rows, LANES), offs


def _unpack(packed, offs, shapes):
    flat = packed.reshape(-1)
    return [flat[o * LANES:o * LANES + math.prod(s)].reshape(s) for o, s in zip(offs, shapes)]


WEIGHTS = ["w_ada", "b_ada", "w_in", "b_gate", "conv_a", "a_log", "dt_bias", "norm_a", "rel_bias", "w_branch_a",
           "w_branch_b", "w_o", "ln1_g", "ln1_b", "w_up", "conv_ffn", "b_conv_ffn", "w_down", "ln2_g", "ln2_b"]
BIG = ["w_in", "w_branch_a", "w_branch_b", "w_o", "w_up", "w_down"]
LATE = [n for n in BIG if n != "w_in"]
KEPT_SHARDED = {"w_up"}
COL_SHARDED = {"w_in", "w_up"}
SMALL_SHARDED = {"conv_a": 3 * A_W // N_CHIPS, "rel_bias": B_REL // N_CHIPS, "conv_ffn": 2 * D_FF // N_CHIPS}
SMALL = [n for n in WEIGHTS if n not in BIG and n != "w_ada"]


def _to_full(g4, name):
    if name in KEPT_SHARDED:
        return g4
    if name in COL_SHARDED:
        return jnp.transpose(g4, (1, 0, 2)).reshape(g4.shape[1], -1)
    return g4.reshape(-1, g4.shape[2])


def _to_shards(full, name):
    if name in KEPT_SHARDED:
        return full
    if name in COL_SHARDED:
        return jnp.transpose(full.reshape(full.shape[0], N_CHIPS, -1), (1, 0, 2))
    return full.reshape(N_CHIPS, -1, full.shape[1])


def kernel(x, c, w_ada, b_ada, w_in, b_gate, conv_a, a_log, dt_bias, norm_a, rel_bias, w_branch_a, w_branch_b, w_o, ln1_g, ln1_b, w_up, conv_ffn, b_conv_ffn, w_down, ln2_g, ln2_b, loss_target, m_w_ada, m_b_ada, m_w_in, m_b_gate, m_conv_a, m_a_log, m_dt_bias, m_norm_a, m_rel_bias, m_w_branch_a, m_w_branch_b, m_w_o, m_ln1_g, m_ln1_b, m_w_up, m_conv_ffn, m_b_conv_ffn, m_w_down, m_ln2_g, m_ln2_b, v_w_ada, v_b_ada, v_w_in, v_b_gate, v_conv_a, v_a_log, v_dt_bias, v_norm_a, v_rel_bias, v_w_branch_a, v_w_branch_b, v_w_o, v_ln1_g, v_ln1_b, v_w_up, v_conv_ffn, v_b_conv_ffn, v_w_down, v_ln2_g, v_ln2_b):
    args = dict(locals())
    wts = {n: args[n] for n in WEIGHTS}
    moms = {n: args["m_" + n] for n in WEIGHTS}
    vars_ = {n: args["v_" + n] for n in WEIGHTS}
    xi, yi, ci = _place()
    chip = 2 * xi + yi
    dev = 4 * xi + 2 * yi + ci
    ada_cols = w_ada.shape[2]

    sshapes = [wts[n].shape[1:] for n in SMALL_SHARDED]
    spack, soffs = _pack([wts[n][0] for n in SMALL_SHARDED], 16)
    first = allgather8(jnp.concatenate([jnp.pad(c, ((0, 7), (0, 0))), spack]), "gather_c_small_w").reshape(N_DEV, 24, LANES)
    c_all = first[:, 0]
    b_ada_sh = lax.dynamic_slice(b_ada, (0, chip * ada_cols), (1, ada_cols))
    mod_sh = ada_fwd(c_all, w_ada[0], b_ada_sh)
    mod_g = allgather8(mod_sh, "gather_mod").reshape(N_CHIPS, 2, N_DEV, ada_cols)[:, 0]
    mod = lax.dynamic_slice(mod_g, (0, dev, 0), (N_CHIPS, 1, ada_cols)).reshape(6, D_MODEL)

    (w_in_g4,) = chip_exchange([wts["w_in"][0].astype(BF16)], "gather_w_in", scatter=False)
    wd = split_w_in(_to_full(w_in_g4, "w_in"))
    late_shards = [wts[n][0].astype(BF16) for n in LATE]
    late_gather = exchange_start(late_shards, "gather_late_start", scatter=False, after=w_in_g4)
    mod = mod + late_gather["token"][0, 0]

    def late_weights(after):
        zones = exchange_wait(late_gather, "gather_late_wait", after)
        full = [_to_full(lax.dynamic_update_slice(z, s[None], (chip, 0, 0)), n) for n, z, s in zip(LATE, zones, late_shards)]
        return {n[2:]: f for n, f in zip(LATE, full)}

    sg = first[::2, 8:]
    sparts = [_unpack(sg[j], soffs, sshapes) for j in range(N_CHIPS)]
    sm = {n: wts[n] for n in SMALL if n not in SMALL_SHARDED and n != "b_ada"}
    for i, n in enumerate(SMALL_SHARDED):
        sm[n] = jnp.concatenate([sparts[j][i] for j in range(N_CHIPS)], axis=-1)

    early = {}

    def late_start(g):
        early["swap"] = swap_start([g[n] for n in LATE], "grad_swap_late_start", g[LATE[0]])
        return early["swap"]["token"]

    def late_finish(after):
        mine, theirs = swap_wait(early["swap"], "grad_swap_late_wait", after)
        early["sums"] = [_to_shards(pair_add(a, b, "grad_pair_" + n), n) for n, a, b in zip(LATE, mine, theirs)]
        early["scatter"] = exchange_start(early["sums"], "grad_scatter_start", scatter=True, after=theirs[0])
        return early["scatter"]["token"]

    def w_in_start(g):
        early["swap_in"] = swap_start([g], "grad_swap_w_in_start", g)
        return early["swap_in"]["token"]

    def w_in_finish(after):
        (mine,), (theirs,) = swap_wait(early["swap_in"], "grad_swap_w_in_wait", after)
        early["sum_in"] = _to_shards(pair_add(mine, theirs, "grad_pair_w_in"), "w_in")
        early["scatter_in"] = exchange_start([early["sum_in"]], "grad_scatter_w_in_start", scatter=True, after=theirs)
        return early["scatter_in"]["token"]

    hooks = dict(late_start=late_start, late_finish=late_finish, w_in_start=w_in_start, w_in_finish=w_in_finish)
    loss, dxpre2, dffn, fin, sv = forward_local(x[0], loss_target[0], mod, wd, sm, late_weights)
    grad_x, dmod, gw, gs = backward_local(x[0], mod, sm, dxpre2, dffn, fin, sv, hooks)

    gnames = [n for n in SMALL if n != "b_ada"]
    vec, voffs = _pack([dmod] + [gs[n] for n in gnames] + [loss], 56)
    gathered = allgather8(vec, "gather_small_g").reshape(N_DEV, 56, LANES)
    summed = sum_lead(gathered, "sum_small_g")
    full_shapes = [(6, D_MODEL)] + [gs[n].shape for n in gnames] + [(1, 1)]
    parts = _unpack(summed, voffs, full_shapes)
    grads = {"b_ada": parts[0].reshape(1, -1)}
    for n, p in zip(gnames, parts[1:-1]):
        if n in SMALL_SHARDED:
            p = lax.dynamic_slice_in_dim(p, chip * SMALL_SHARDED[n], SMALL_SHARDED[n], axis=1)
        grads[n] = p.reshape(wts[n].shape)
    loss_total = parts[-1].reshape(())
    dmod_all = gathered[:, 0:6, :].reshape(N_DEV, 6 * D_MODEL)
    grads["w_ada"] = ada_bwd(c_all, lax.dynamic_slice(dmod_all, (0, chip * ada_cols), (N_DEV, ada_cols)))[None]

    def own_slot(zone, sums):
        return lax.dynamic_update_slice(zone, lax.dynamic_slice_in_dim(sums, chip, 1, axis=0), (chip, 0, 0))

    zones = exchange_wait(early["scatter"], "grad_scatter_wait", summed)
    for n, z, s in zip(LATE, zones, early["sums"]):
        grads[n] = sum_lead(own_slot(z, s), "grad_sum_" + n)[None]

    delta, new_m, new_v = {}, {}, {}

    def update(n):
        d, m2, v2 = adamw(wts[n][0], grads[n][0], moms[n][0], vars_[n][0], "adamw_" + n)
        delta[n], new_m[n], new_v[n] = d[None], m2[None], v2[None]

    for n in ["w_ada"] + LATE:
        update(n)
    shapes = [wts[n].shape for n in SMALL]
    packs = [_pack([t[n] for n in SMALL], 32) for t in (wts, grads, moms, vars_)]
    outs = adamw(*[p[0] for p in packs], "adamw_small")
    for res, o in zip((delta, new_m, new_v), outs):
        for n, a in zip(SMALL, _unpack(o, packs[0][1], shapes)):
            res[n] = a
    (zone_in,) = exchange_wait(early["scatter_in"], "grad_scatter_w_in_wait", outs[0])
    grads["w_in"] = sum_lead(own_slot(zone_in, early["sum_in"]), "grad_sum_w_in")[None]
    update("w_in")
    return (loss_total, grad_x[None], *[grads[n] for n in WEIGHTS], *[delta[n] for n in WEIGHTS],
            *[new_m[n] for n in WEIGHTS], *[new_v[n] for n in WEIGHTS])
```

```python
import functools
import math

import jax
import jax.numpy as jnp
from jax import lax
from jax.experimental import pallas as pl
from jax.experimental.pallas import tpu as pltpu

F32 = jnp.float32
BF16 = jnp.bfloat16

D_MODEL = 1024
CHUNK = 64
A_HEADS = 8
A_DK = 128
A_W = A_HEADS * A_DK
A_CONV = 4
B_HEADS = 16
B_DH = 64
B_W = B_HEADS * B_DH
B_PREV = 8
B_BAND = (B_PREV + 1) * CHUNK
B_MAX_REL = 256
B_REL = CHUNK - 1 + B_MAX_REL + 1
D_FF = 2816
FFN_CONV = 3
IN_COLS = 4 * A_W + 2 * A_HEADS + 3 * B_W + 2 * D_MODEL
ALPHA = 2.0 ** 0.25
LN_EPS = 1e-5
RMS_EPS = 1e-6
L2_EPS = 1e-6
NEG_INF = -1e30
ADAM_LR, ADAM_B1, ADAM_B2, ADAM_EPS, ADAM_WD, ADAM_STEP = 0.001, 0.9, 0.999, 1e-08, 0.01, 10
N_CHIPS = 4
N_DEV = 8
VMEM_LIMIT = 56 * 1024 * 1024


def _cparams(sem=None):
    return pltpu.CompilerParams(dimension_semantics=sem, vmem_limit_bytes=VMEM_LIMIT)


_DIMS = {"nn": (((1,), (0,)), ((), ())), "nt": (((1,), (1,)), ((), ())), "tn": (((0,), (0,)), ((), ()))}


MM_TILE_CAP = 1536


MM_TOKEN_K_CAP = 2048
MM_K_CAP = 3072


def _mm_tile(n, cap=MM_TILE_CAP):
    return max(t for t in range(128, min(n, cap) + 1, 128) if n % t == 0)


def mm(a, b, *, mode, out_dtype, name, acc_in=None, b_shards=False, out_shards=0):
    b_rows, b_cols = (b.shape[1], b.shape[0] * b.shape[2]) if b_shards else b.shape
    if mode == "nn":
        (M, K), (K2, N) = a.shape, (b_rows, b_cols)
    elif mode == "nt":
        (M, K), (N, K2) = a.shape, (b_rows, b_cols)
    else:
        (K, M), (K2, N) = a.shape, (b_rows, b_cols)
    assert K == K2, (a.shape, b.shape, mode)
    tm, tn, tk = _mm_tile(M), _mm_tile(N), _mm_tile(K, MM_TOKEN_K_CAP if mode == "tn" else MM_K_CAP)
    if b_shards and mode == "nt":
        tk = b.shape[2]
    nk = K // tk

    def body(*refs):
        if acc_in is None:
            a_ref, b_ref, o_ref, acc_ref = refs
        else:
            a_ref, b_ref, c_ref, o_ref, acc_ref = refs
        k = pl.program_id(2)

        @pl.when(k == 0)
        def _():
            if acc_in is None:
                acc_ref[...] = jnp.zeros_like(acc_ref)
            else:
                acc_ref[...] = c_ref[...]

        acc_ref[...] += lax.dot_general(a_ref[...].astype(BF16), b_ref[...].astype(BF16), _DIMS[mode],
                                        preferred_element_type=F32)

        @pl.when(k == nk - 1)
        def _():
            o_ref[...] = acc_ref[...].astype(out_dtype)

    a_spec = pl.BlockSpec((tk, tm), lambda i, j, k: (k, i)) if mode == "tn" else pl.BlockSpec((tm, tk), lambda i, j, k: (i, k))
    if b_shards:
        assert (tk if mode == "nt" else tn) == b.shape[2] and mode != "tn", (b.shape, tn, tk, mode)
        b_spec = (pl.BlockSpec((None, tn, tk), lambda i, j, k: (k, j, 0)) if mode == "nt"
                  else pl.BlockSpec((None, tk, tn), lambda i, j, k: (j, k, 0)))
    else:
        b_spec = pl.BlockSpec((tn, tk), lambda i, j, k: (j, k)) if mode == "nt" else pl.BlockSpec((tk, tn), lambda i, j, k: (k, j))
    o_spec = pl.BlockSpec((tm, tn), lambda i, j, k: (i, j))
    out_shape = jax.ShapeDtypeStruct((M, N), out_dtype)
    if out_shards:
        assert N == out_shards * tn and acc_in is None, (N, tn, out_shards)
        o_spec = pl.BlockSpec((None, tm, tn), lambda i, j, k: (j, i, 0))
        out_shape = jax.ShapeDtypeStruct((out_shards, M, tn), out_dtype)
    ins, in_specs, aliases = [a, b], [a_spec, b_spec], {}
    if acc_in is not None:
        assert acc_in.shape == (M, N) and acc_in.dtype == F32 and out_dtype == F32
        ins.append(acc_in)
        in_specs.append(o_spec)
        aliases = {2: 0}
    return pl.pallas_call(
        body, name=name, grid=(M // tm, N // tn, nk), in_specs=in_specs, out_specs=o_spec,
        out_shape=out_shape, scratch_shapes=[pltpu.VMEM((tm, tn), F32)],
        input_output_aliases=aliases, compiler_params=_cparams(("parallel", "parallel", "arbitrary")),
    )(*ins)


def rowcall(body, *, name, S, ts, ins, outs, scratch=()):
    assert S % ts == 0 and ts % 16 == 0
    nsteps = S // ts
    in_specs, arrays = [], []
    for arr, kind in ins:
        arrays.append(arr)
        if kind == "row":
            in_specs.append(pl.BlockSpec((ts, arr.shape[1]), lambda i: (i, 0)))
        elif kind in ("prev", "next"):
            hr = 8 * (4 // arr.dtype.itemsize)
            per, last = ts // hr, S // hr - 1
            if kind == "prev":
                in_specs.append(pl.BlockSpec((hr, arr.shape[1]), lambda i, per=per: (jnp.maximum(i * per - 1, 0), 0)))
            else:
                in_specs.append(pl.BlockSpec((hr, arr.shape[1]), lambda i, per=per, last=last: (jnp.minimum((i + 1) * per, last), 0)))
        else:
            nd = arr.ndim
            in_specs.append(pl.BlockSpec(arr.shape, lambda i, nd=nd: (0,) * nd))
    out_specs, out_shapes, acc_idx = [], [], []
    for n, (shape, dtype, kind) in enumerate(outs):
        out_shapes.append(jax.ShapeDtypeStruct(shape, dtype))
        if kind == "row":
            out_specs.append(pl.BlockSpec((ts, shape[1]), lambda i: (i, 0)))
        else:
            nd = len(shape)
            out_specs.append(pl.BlockSpec(shape, lambda i, nd=nd: (0,) * nd))
            acc_idx.append(n)
    n_in = len(arrays)

    def wrapped(*refs):
        @pl.when(pl.program_id(0) == 0)
        def _():
            for n in acc_idx:
                refs[n_in + n][...] = jnp.zeros_like(refs[n_in + n])

        body(*refs)

    res = pl.pallas_call(
        wrapped, name=name, grid=(nsteps,), in_specs=in_specs, out_specs=out_specs, out_shape=out_shapes,
        scratch_shapes=list(scratch), compiler_params=_cparams(("arbitrary",) if acc_idx else ("parallel",)),
    )(*arrays)
    return res


def _halo_prev(ref):
    v = ref[...].astype(F32)
    return v[v.shape[0] - 8:]


def _halo_next(ref):
    return ref[...].astype(F32)[:8]


def _shift_down(cur, prev8, k):
    if k == 0:
        return cur
    rolled = pltpu.roll(cur, k, axis=0)
    fix = pltpu.roll(prev8, k, axis=0)
    row = lax.broadcasted_iota(jnp.int32, (8, 1), 0)
    top = jnp.where(row < k, fix, rolled[0:8])
    if cur.shape[0] == 8:
        return top
    return jnp.concatenate([top, rolled[8:]], axis=0)


def _shift_up(cur, next8, k):
    if k == 0:
        return cur
    n = cur.shape[0]
    rolled = pltpu.roll(cur, n - k, axis=0)
    fix = pltpu.roll(next8, 8 - k, axis=0)
    row = lax.broadcasted_iota(jnp.int32, (8, 1), 0)
    bot = jnp.where(row >= 8 - k, fix, rolled[n - 8:n])
    return jnp.concatenate([rolled[:n - 8], bot], axis=0)


def _sigmoid(x):
    return 1.0 / (1.0 + jnp.exp(-x))


def _silu(x):
    return x * _sigmoid(x)


def _silu_and_grad(x):
    s = _sigmoid(x)
    return x * s, s * (1.0 + x * (1.0 - s))


def _softplus(x):
    return jnp.maximum(x, 0.0) + jnp.log1p(jnp.exp(-jnp.abs(x)))


def _split2(x):
    hi = x.astype(BF16)
    return hi, (x - hi.astype(F32)).astype(BF16)


def _dot1(a, b, mode):
    return lax.dot_general(a.astype(BF16), b.astype(BF16), _DIMS[mode], preferred_element_type=F32)


def _dot3(a, b, mode):
    ah, al = _split2(a)
    bh, bl = _split2(b)
    d = lambda p, q: lax.dot_general(p, q, _DIMS[mode], preferred_element_type=F32)
    return d(ah, bh) + (d(ah, bl) + d(al, bh))


def ada_fwd(c_all, w_sh, b_sh):
    n = w_sh.shape[1]
    tn = 512

    def body(c_ref, w_ref, b_ref, o_ref):
        o_ref[...] = _dot1(_silu(c_ref[...]), w_ref[...], "nn") + b_ref[...]

    return pl.pallas_call(
        body, name="ada_fwd", grid=(n // tn,),
        in_specs=[pl.BlockSpec((N_DEV, D_MODEL), lambda j: (0, 0)), pl.BlockSpec((D_MODEL, tn), lambda j: (0, j)),
                  pl.BlockSpec((1, tn), lambda j: (0, j))],
        out_specs=pl.BlockSpec((N_DEV, tn), lambda j: (0, j)), out_shape=jax.ShapeDtypeStruct((N_DEV, n), F32),
        compiler_params=_cparams(("parallel",)),
    )(c_all, w_sh, b_sh)


def ada_bwd(c_all, dmod_sh):
    n = dmod_sh.shape[1]
    tn = 512

    def body(c_ref, d_ref, o_ref):
        o_ref[...] = _dot1(_silu(c_ref[...]), d_ref[...], "tn")

    return pl.pallas_call(
        body, name="ada_bwd", grid=(n // tn,),
        in_specs=[pl.BlockSpec((N_DEV, D_MODEL), lambda j: (0, 0)), pl.BlockSpec((N_DEV, tn), lambda j: (0, j))],
        out_specs=pl.BlockSpec((D_MODEL, tn), lambda j: (0, j)), out_shape=jax.ShapeDtypeStruct((D_MODEL, n), F32),
        compiler_params=_cparams(("parallel",)),
    )(c_all, dmod_sh)


SHIFT_T, SCALE_T, GATE_T, SHIFT_F, SCALE_F, GATE_F = range(6)


def modulate(x, mod, shift_row, scale_row, name):
    S = x.shape[0]

    def body(x_ref, m_ref, o_ref):
        m = m_ref[...]
        o_ref[...] = (x_ref[...] * (1.0 + m[scale_row:scale_row + 1]) + m[shift_row:shift_row + 1]).astype(BF16)

    return rowcall(body, name=name, S=S, ts=min(S, 1024), ins=[(x, "row"), (mod, "vec")], outs=[((S, D_MODEL), BF16, "row")])[0]


def _conv_fwd(cur, prev, w, width):
    y = cur * w[width - 1:width]
    for j in range(width - 1):
        y = y + _shift_down(cur, prev, width - 1 - j) * w[j:j + 1]
    return y


def _prep_a_core(cur, prev, w):
    return _silu_and_grad(_conv_fwd(cur, prev, w, A_CONV))


def prep_a_fwd(qkv_raw, ba, conv_a, a_log, dt_bias):
    S = qkv_raw.shape[0]

    def body(x_ref, xp_ref, ba_ref, w_ref, al_ref, dt_ref, q_ref, k_ref, v_ref, beta_ref, g_ref):
        first = (pl.program_id(0) > 0).astype(F32)
        y, _ = _prep_a_core(x_ref[...].astype(F32), _halo_prev(xp_ref) * first, w_ref[...])
        for h in range(A_HEADS):
            sl = slice(h * A_DK, (h + 1) * A_DK)
            qh = y[:, sl]
            kh = y[:, A_W + h * A_DK:A_W + (h + 1) * A_DK]
            q_ref[:, sl] = qh * (lax.rsqrt(jnp.sum(qh * qh, axis=-1, keepdims=True) + L2_EPS) * (A_DK ** -0.5))
            k_ref[:, sl] = kh * lax.rsqrt(jnp.sum(kh * kh, axis=-1, keepdims=True) + L2_EPS)
        v_ref[...] = y[:, 2 * A_W:3 * A_W]
        bav = ba_ref[...]
        beta_ref[...] = _sigmoid(bav[:, 0:A_HEADS])
        g_ref[...] = -jnp.exp(al_ref[...]) * _softplus(bav[:, A_HEADS:2 * A_HEADS] + dt_ref[...])

    return rowcall(
        body, name="prep_a_fwd", S=S, ts=256,
        ins=[(qkv_raw, "row"), (qkv_raw, "prev"), (ba, "row"), (conv_a, "vec"), (a_log, "vec"), (dt_bias, "vec")],
        outs=[((S, A_W), F32, "row")] * 3 + [((S, A_HEADS), F32, "row")] * 2)


HEAD_GROUP = 2
GROUP_ROWS = HEAD_GROUP * CHUNK
N_HEAD_GROUPS = A_HEADS // HEAD_GROUP
LOG_CHUNK = int(math.log2(CHUNK))


def _tri_masks():
    rb = lax.broadcasted_iota(jnp.int32, (GROUP_ROWS, GROUP_ROWS), 0)
    cb = lax.broadcasted_iota(jnp.int32, (GROUP_ROWS, GROUP_ROWS), 1)
    same = (rb >> LOG_CHUNK) == (cb >> LOG_CHUNK)
    return dict(causal=same & (rb >= cb), strict=same & (rb > cb), eye=rb == cb, upper=same & (cb >= rb),
                last=cb == (rb | (CHUNK - 1)), rb=rb, cb=cb)


def _col_to_row(colv, eye):
    return jnp.sum(jnp.where(eye, colv, 0.0), axis=0, keepdims=True)


def _row_to_col(rowv, eye):
    return jnp.sum(jnp.where(eye, rowv, 0.0), axis=1, keepdims=True)


def _tri_inv(a_list, mk):
    rb, cb = mk["rb"], mk["cb"]
    ts = [jnp.where(mk["eye"], 1.0, 0.0) - jnp.where((rb >> 1) == (cb >> 1), a, 0.0) for a in a_list]
    for lvl in range(1, LOG_CHUNK):
        rs, cs = rb >> lvl, cb >> lvl
        sel = ((rs & 1) == 1) & (cs == rs - 1)
        inner = [_dot3(t, jnp.where(sel, a, 0.0), "nn") for t, a in zip(ts, a_list)]
        ts = [t - _dot3(i, t, "nn") for i, t in zip(inner, ts)]
    return ts


def _stack_heads(ref, grp):
    return jnp.concatenate([ref[:, (grp * HEAD_GROUP + j) * A_DK:(grp * HEAD_GROUP + j + 1) * A_DK]
                            for j in range(HEAD_GROUP)], axis=0)


def _stack_cols(tile, grp):
    return jnp.concatenate([tile[:, grp * HEAD_GROUP + j:grp * HEAD_GROUP + j + 1] for j in range(HEAD_GROUP)], axis=0)


def _delta_local(q, k, v, beta, g, mk):
    causal, strict, eye = mk["causal"], mk["strict"], mk["eye"]
    g_row = _col_to_row(g, eye)
    gc = jnp.sum(jnp.where(causal, g_row, 0.0), axis=1, keepdims=True)
    gc_row = _col_to_row(gc, eye)
    decay = jnp.where(causal, jnp.exp(jnp.where(causal, gc - gc_row, 0.0)), 0.0)
    gam = jnp.exp(gc)
    kb = k * beta
    vb = v * beta
    y = kb * gam
    a = jnp.where(strict, _dot1(kb, k, "nt") * decay, 0.0)
    p = _dot1(q, k, "nt") * decay
    gl = jnp.sum(jnp.where(mk["last"], gc_row, 0.0), axis=1, keepdims=True)
    kd = k * jnp.exp(gl - gc)
    return dict(gc=gc, decay=decay, gam=gam, kb=kb, vb=vb, y=y, a=a, p=p, gl=gl, kd=kd)


def _head_rows(x, j):
    return x[j * CHUNK:(j + 1) * CHUNK]


def delta_fwd(q, k, v, beta, g):
    S = q.shape[0]
    n_chunks = S // CHUNK

    def body(q_ref, k_ref, v_ref, beta_ref, g_ref, o_ref, sprev_ref, t_ref, state_ref):
        @pl.when(pl.program_id(0) == 0)
        def _():
            state_ref[...] = jnp.zeros_like(state_ref)

        mk = _tri_masks()
        betav, gv = beta_ref[...], g_ref[...]
        groups = range(N_HEAD_GROUPS)
        q_all = [_stack_heads(q_ref, grp) for grp in groups]
        locs = [_delta_local(q_all[grp], _stack_heads(k_ref, grp), _stack_heads(v_ref, grp),
                             _stack_cols(betav, grp), _stack_cols(gv, grp), mk) for grp in groups]
        tinvs = _tri_inv([loc["a"] for loc in locs], mk)
        uws = [_dot3(tinvs[grp], jnp.concatenate([locs[grp]["vb"], locs[grp]["y"]], axis=1), "nn") for grp in groups]
        for grp in groups:
            loc, uw = locs[grp], uws[grp]
            t_ref[0, grp] = tinvs[grp]
            qg = q_all[grp] * loc["gam"]
            egl = jnp.exp(loc["gl"])
            vns, o_state = [], []
            for j in range(HEAD_GROUP):
                h = grp * HEAD_GROUP + j
                s0 = state_ref[h]
                sprev_ref[0, h] = s0
                uw_h = _head_rows(uw, j)
                vn = uw_h[:, :A_DK] - _dot1(uw_h[:, A_DK:], s0, "nn")
                vns.append(vn)
                o_state.append(_dot1(_head_rows(qg, j), s0, "nn"))
                state_ref[h] = s0 * egl[(j + 1) * CHUNK - 1:(j + 1) * CHUNK] + _dot1(_head_rows(loc["kd"], j), vn, "tn")
            o_local = _dot1(loc["p"], jnp.concatenate(vns, axis=0), "nn")
            for j in range(HEAD_GROUP):
                h = grp * HEAD_GROUP + j
                o_ref[:, h * A_DK:(h + 1) * A_DK] = o_state[j] + _head_rows(o_local, j)

    tile = pl.BlockSpec((CHUNK, A_W), lambda n: (n, 0))
    small = pl.BlockSpec((CHUNK, A_HEADS), lambda n: (n, 0))
    return pl.pallas_call(
        body, name="delta_fwd", grid=(n_chunks,), in_specs=[tile, tile, tile, small, small],
        out_specs=[tile, pl.BlockSpec((1, A_HEADS, A_DK, A_DK), lambda n: (n, 0, 0, 0)),
                   pl.BlockSpec((1, N_HEAD_GROUPS, GROUP_ROWS, GROUP_ROWS), lambda n: (n, 0, 0, 0))],
        out_shape=[jax.ShapeDtypeStruct((S, A_W), F32), jax.ShapeDtypeStruct((n_chunks, A_HEADS, A_DK, A_DK), F32),
                   jax.ShapeDtypeStruct((n_chunks, N_HEAD_GROUPS, GROUP_ROWS, GROUP_ROWS), F32)],
        scratch_shapes=[pltpu.VMEM((A_HEADS, A_DK, A_DK), F32)],
        compiler_params=_cparams(("arbitrary",)),
    )(q, k, v, beta, g)


def gate_a_fwd(o_pre, z, norm_w):
    S = o_pre.shape[0]

    def body(o_ref, z_ref, nw_ref, out_ref):
        nw = nw_ref[...]
        for h in range(A_HEADS):
            sl = slice(h * A_DK, (h + 1) * A_DK)
            oh = o_ref[:, sl]
            r = lax.rsqrt(jnp.mean(oh * oh, axis=-1, keepdims=True) + RMS_EPS)
            out_ref[:, sl] = (oh * r * nw * _silu(z_ref[:, sl].astype(F32))).astype(BF16)

    return rowcall(body, name="gate_a_fwd", S=S, ts=min(S, 1024), ins=[(o_pre, "row"), (z, "row"), (norm_w, "vec")],
                   outs=[((S, A_W), BF16, "row")])[0]


HEADS_PER_GROUP = 2
GROUP_W = HEADS_PER_GROUP * B_DH
N_GROUPS = B_HEADS // HEADS_PER_GROUP
PAD_ROWS = B_PREV * CHUNK


Q_TILE = 256
Q_CHUNKS = Q_TILE // CHUNK
KEY_WIN = (B_PREV + Q_CHUNKS) * CHUNK


def _band_probs(qh, kh, bias, valid):
    s = _dot1(qh, kh, "nt") * (B_DH ** -0.5) + bias
    s = jnp.where(valid, s, NEG_INF)
    e = jnp.exp(s - jnp.max(s, axis=-1, keepdims=True))
    return e * (1.0 / jnp.sum(e, axis=-1, keepdims=True))


def _attn_specs(S, tile_rows):
    assert PAD_ROWS % tile_rows == 0 and S % tile_rows == 0, (PAD_ROWS, S, tile_rows)
    n_cb = B_W // GROUP_W
    return [pl.BlockSpec((tile_rows, GROUP_W), lambda g, n: (n + PAD_ROWS // tile_rows, g)),
            pl.BlockSpec((PAD_ROWS + S, GROUP_W), lambda g, n: (0, n_cb + g)),
            pl.BlockSpec((PAD_ROWS + S, GROUP_W), lambda g, n: (0, 2 * n_cb + g)),
            pl.BlockSpec((HEADS_PER_GROUP, CHUNK, B_BAND), lambda g, n: (g, 0, 0))]


def _band_valid(first_chunk):
    return lax.broadcasted_iota(jnp.int32, (CHUNK, B_BAND), 1) >= PAD_ROWS - first_chunk * CHUNK


def _chunk_rows(x, qc, rows=CHUNK):
    return x[qc * CHUNK:qc * CHUNK + rows]


FWD_TILE = 512
FWD_CHUNKS = FWD_TILE // CHUNK
FWD_WIN = (B_PREV + FWD_CHUNKS) * CHUNK


def attn_fwd(qkv_pad, bias):
    S = qkv_pad.shape[0] - PAD_ROWS

    def body(q_ref, k_ref, v_ref, b_ref, o_ref):
        n = pl.program_id(1)
        start = pl.multiple_of(n * FWD_TILE, FWD_TILE)
        kwin = k_ref[pl.ds(start, FWD_WIN), :]
        vwin = v_ref[pl.ds(start, FWD_WIN), :]
        qv = q_ref[...]
        pairs = [(qc, hh) for qc in range(FWD_CHUNKS) for hh in range(HEADS_PER_GROUP)]
        sl = lambda hh: slice(hh * B_DH, (hh + 1) * B_DH)
        s = [_dot1(_chunk_rows(qv, qc)[:, sl(hh)], _chunk_rows(kwin, qc, B_BAND)[:, sl(hh)], "nt") for qc, hh in pairs]
        s = [jnp.where(_band_valid(n * FWD_CHUNKS + qc), x * (B_DH ** -0.5) + b_ref[hh], NEG_INF)
             for x, (qc, hh) in zip(s, pairs)]
        e = [jnp.exp(x - jnp.max(x, axis=-1, keepdims=True)) for x in s]
        p = [x * (1.0 / jnp.sum(x, axis=-1, keepdims=True)) for x in e]
        o = [_dot1(x, _chunk_rows(vwin, qc, B_BAND)[:, sl(hh)], "nn") for x, (qc, hh) in zip(p, pairs)]
        rows = [jnp.concatenate(o[qc * HEADS_PER_GROUP:(qc + 1) * HEADS_PER_GROUP], axis=1) for qc in range(FWD_CHUNKS)]
        o_ref[...] = jnp.concatenate(rows, axis=0).astype(BF16)

    return pl.pallas_call(
        body, name="attn_fwd", grid=(N_GROUPS, S // FWD_TILE), in_specs=_attn_specs(S, FWD_TILE),
        out_specs=pl.BlockSpec((FWD_TILE, GROUP_W), lambda g, n: (n, g)),
        out_shape=jax.ShapeDtypeStruct((S, B_W), BF16),
        compiler_params=_cparams(("parallel", "arbitrary")),
    )(qkv_pad, qkv_pad, qkv_pad, bias)


EXT = B_BAND + CHUNK


def bias_expand(rel_bias):
    def body(rev_ref, o_ref):
        rev = rev_ref[...]
        erev = jnp.concatenate([jnp.broadcast_to(rev[:, 0:1], (B_HEADS, EXT - B_REL)), rev], axis=1)
        for i in range(CHUNK):
            o_ref[i] = erev[:, CHUNK - i:CHUNK - i + B_BAND]

    return pl.pallas_call(
        body, name="bias_expand", in_specs=[WHOLE_VMEM], out_specs=WHOLE_VMEM,
        out_shape=jax.ShapeDtypeStruct((CHUNK, B_HEADS, B_BAND), F32),
    )(jnp.flip(rel_bias, axis=1))


def bias_reduce(dbias):
    def body(d_ref, o_ref):
        acc = jnp.zeros((B_HEADS, EXT), F32)
        for i in range(CHUNK):
            acc = acc + jnp.pad(d_ref[i], ((0, 0), (CHUNK - i, i)))
        tail = acc[:, EXT - B_REL:]
        clipped = jnp.sum(acc[:, :EXT - B_REL], axis=1, keepdims=True)
        lane = lax.broadcasted_iota(jnp.int32, (B_HEADS, B_REL), 1)
        o_ref[...] = jnp.where(lane == 0, tail + clipped, tail)

    rev = pl.pallas_call(body, name="bias_reduce", in_specs=[WHOLE_VMEM], out_specs=WHOLE_VMEM,
                         out_shape=jax.ShapeDtypeStruct((B_HEADS, B_REL), F32))(dbias)
    return jnp.flip(rev, axis=1)


def merge_fwd(gates_raw, b_gate, ya, yb):
    S = ya.shape[0]

    def body(g_ref, b_ref, ya_ref, yb_ref, o_ref):
        gt = _sigmoid(g_ref[...].astype(F32) + b_ref[...])
        o_ref[...] = (gt[:, :D_MODEL] * ya_ref[...].astype(F32) + gt[:, D_MODEL:] * yb_ref[...].astype(F32)).astype(BF16)

    return rowcall(body, name="merge_fwd", S=S, ts=512,
                   ins=[(gates_raw, "row"), (b_gate, "vec"), (ya, "row"), (yb, "row")],
                   outs=[((S, D_MODEL), BF16, "row")])[0]


def _ln_stats(xpre):
    mu = jnp.mean(xpre, axis=-1, keepdims=True)
    xc = xpre - mu
    rstd = lax.rsqrt(jnp.mean(xc * xc, axis=-1, keepdims=True) + LN_EPS)
    return xc * rstd, rstd


def ln1_fwd(x, mix, mod, ln_g, ln_b):
    S = x.shape[0]

    def body(x_ref, mix_ref, m_ref, g_ref, b_ref, xpre_ref, x1_ref, h2_ref):
        m = m_ref[...]
        xpre = ALPHA * x_ref[...] + m[GATE_T:GATE_T + 1] * mix_ref[...]
        xhat, _ = _ln_stats(xpre)
        x1 = xhat * g_ref[...] + b_ref[...]
        xpre_ref[...] = xpre
        x1_ref[...] = x1
        h2_ref[...] = (x1 * (1.0 + m[SCALE_F:SCALE_F + 1]) + m[SHIFT_F:SHIFT_F + 1]).astype(BF16)

    return rowcall(body, name="ln1_fwd", S=S, ts=512,
                   ins=[(x, "row"), (mix, "row"), (mod, "vec"), (ln_g, "vec"), (ln_b, "vec")],
                   outs=[((S, D_MODEL), F32, "row"), ((S, D_MODEL), F32, "row"), ((S, D_MODEL), BF16, "row")])


STRIP_FWD = (64, 256)
STRIP_BWD = (128, 128)


def ffn_act_fwd(up, conv_w, conv_b):
    S = up.shape[0]
    ts = 256
    STRIP_ROWS, STRIP_COLS = STRIP_FWD

    def body(u_ref, up_ref, w_ref, b_ref, o_ref, ubuf):
        ubuf[0:8] = _halo_prev(up_ref) * (pl.program_id(0) > 0).astype(F32)
        ubuf[8:8 + ts] = u_ref[...].astype(F32)

        def col_block(j, carry):
            gate = pl.ds(pl.multiple_of(j * STRIP_COLS, STRIP_COLS), STRIP_COLS)
            halves = [gate, pl.ds(pl.multiple_of(D_FF + j * STRIP_COLS, STRIP_COLS), STRIP_COLS)]
            w = [w_ref[:, c] for c in halves]
            bias = [b_ref[:, c] for c in halves]
            for r0 in range(0, ts, STRIP_ROWS):
                uc = []
                for h in range(2):
                    x = ubuf[r0:r0 + STRIP_ROWS + 8, halves[h]]
                    uc.append(bias[h] + sum(
                        w[h][t:t + 1] * (x if t == FFN_CONV - 1 else pltpu.roll(x, FFN_CONV - 1 - t, axis=0))[8:]
                        for t in range(FFN_CONV)))
                o_ref[r0:r0 + STRIP_ROWS, gate] = (_silu(uc[0]) * uc[1]).astype(BF16)
            return carry

        lax.fori_loop(0, D_FF // STRIP_COLS, col_block, 0)

    return rowcall(body, name="ffn_act_fwd", S=S, ts=ts,
                   ins=[(up, "row"), (up, "prev"), (conv_w, "vec"), (conv_b, "vec")],
                   outs=[((S, D_FF), BF16, "row")], scratch=[pltpu.VMEM((ts + 8, 2 * D_FF), F32)])[0]


def final_fwd_bwd(x1, ffn, target, mod, ln_g, ln_b):
    S = x1.shape[0]

    def body(x1_ref, f_ref, t_ref, m_ref, g_ref, b_ref, dxpre_ref, dffn_ref, loss_ref, dgate_ref, dg_ref, db_ref):
        gate = m_ref[...][GATE_F:GATE_F + 1]
        ffn_v = f_ref[...]
        xpre = ALPHA * x1_ref[...] + gate * ffn_v
        xhat, rstd = _ln_stats(xpre)
        err = xhat * g_ref[...] + b_ref[...] - t_ref[...]
        loss_ref[...] += 0.5 * jnp.sum(jnp.mean(err * err, axis=-1, keepdims=True), axis=0, keepdims=True)
        dy = err * (1.0 / D_MODEL)
        dg_ref[...] += jnp.sum(dy * xhat, axis=0, keepdims=True)
        db_ref[...] += jnp.sum(dy, axis=0, keepdims=True)
        dyg = dy * g_ref[...]
        dxpre = rstd * (dyg - jnp.mean(dyg, axis=-1, keepdims=True) - xhat * jnp.mean(dyg * xhat, axis=-1, keepdims=True))
        dxpre_ref[...] = dxpre
        dffn_ref[...] = (gate * dxpre).astype(BF16)
        dgate_ref[...] += jnp.sum(dxpre * ffn_v, axis=0, keepdims=True)

    vec = ((1, D_MODEL), F32, "acc")
    return rowcall(body, name="final_fwd_bwd", S=S, ts=512,
                   ins=[(x1, "row"), (ffn, "row"), (target, "row"), (mod, "vec"), (ln_g, "vec"), (ln_b, "vec")],
                   outs=[((S, D_MODEL), F32, "row"), ((S, D_MODEL), BF16, "row"), ((1, 1), F32, "acc"), vec, vec, vec])


def ffn_act_bwd(dact, up, conv_w, conv_b):
    S = up.shape[0]
    ts = 256
    STRIP_ROWS, STRIP_COLS = STRIP_BWD
    win_u, win_d = STRIP_ROWS + 16, STRIP_ROWS + 8

    def body(d_ref, dn_ref, u_ref, up_ref, un_ref, w_ref, b_ref, dup_ref, dw_ref, db_ref, ubuf, dbuf):
        i = pl.program_id(0)
        ubuf[0:8] = _halo_prev(up_ref) * (i > 0).astype(F32)
        ubuf[8:8 + ts] = u_ref[...].astype(F32)
        ubuf[8 + ts:16 + ts] = _halo_next(un_ref)
        dbuf[0:ts] = d_ref[...].astype(F32)
        dbuf[ts:ts + 8] = _halo_next(dn_ref) * (i < pl.num_programs(0) - 1).astype(F32)

        def col_block(j, carry):
            halves = [pl.ds(pl.multiple_of(j * STRIP_COLS, STRIP_COLS), STRIP_COLS),
                      pl.ds(pl.multiple_of(D_FF + j * STRIP_COLS, STRIP_COLS), STRIP_COLS)]
            w = [w_ref[:, c] for c in halves]
            bias = [b_ref[:, c] for c in halves]
            dw_acc = [[jnp.zeros((1, STRIP_COLS), F32) for _ in range(FFN_CONV)] for _ in halves]
            db_acc = [jnp.zeros((1, STRIP_COLS), F32) for _ in halves]
            for r0 in range(0, ts, STRIP_ROWS):
                shifted = [[x if k == 0 else pltpu.roll(x, k, axis=0) for k in range(FFN_CONV)]
                           for x in (ubuf[r0:r0 + win_u, c] for c in halves)]
                uc = [bias[h] + sum(w[h][t:t + 1] * shifted[h][FFN_CONV - 1 - t][8:8 + win_d] for t in range(FFN_CONV))
                      for h in range(2)]
                dact_w = dbuf[r0:r0 + win_d, halves[0]]
                sg, dsg = _silu_and_grad(uc[0])
                duc = [dact_w * uc[1] * dsg, dact_w * sg]
                for h in range(2):
                    dup = duc[h] * w[h][FFN_CONV - 1:FFN_CONV]
                    for t in range(FFN_CONV - 1):
                        dup = dup + pltpu.roll(duc[h], win_d - (FFN_CONV - 1 - t), axis=0) * w[h][t:t + 1]
                    dup_ref[r0:r0 + STRIP_ROWS, halves[h]] = dup[:STRIP_ROWS].astype(BF16)
                    mine = duc[h][:STRIP_ROWS]
                    db_acc[h] = db_acc[h] + jnp.sum(mine, axis=0, keepdims=True)
                    for t in range(FFN_CONV):
                        dw_acc[h][t] = dw_acc[h][t] + jnp.sum(
                            mine * shifted[h][FFN_CONV - 1 - t][8:8 + STRIP_ROWS], axis=0, keepdims=True)
            for h in range(2):
                dw_ref[:, halves[h]] += jnp.concatenate(dw_acc[h], axis=0)
                db_ref[:, halves[h]] += db_acc[h]
            return carry

        lax.fori_loop(0, D_FF // STRIP_COLS, col_block, 0)

    return rowcall(body, name="ffn_act_bwd", S=S, ts=ts,
                   ins=[(dact, "row"), (dact, "next"), (up, "row"), (up, "prev"), (up, "next"), (conv_w, "vec"), (conv_b, "vec")],
                   outs=[((S, 2 * D_FF), BF16, "row"), ((FFN_CONV, 2 * D_FF), F32, "acc"), ((1, 2 * D_FF), F32, "acc")],
                   scratch=[pltpu.VMEM((ts + 16, 2 * D_FF), F32), pltpu.VMEM((ts + 8, D_FF), F32)])


def ln1_bwd(dxpre2, dh2, xpre1, mix, mod, ln_g, ln_b):
    S = xpre1.shape[0]

    def body(d2_ref, dh_ref, xp_ref, mix_ref, m_ref, g_ref, b_ref, dxpre_ref, dmix_ref,
             dscale_ref, dshift_ref, dgate_ref, dg_ref, db_ref):
        m = m_ref[...]
        xhat, rstd = _ln_stats(xp_ref[...])
        x1 = xhat * g_ref[...] + b_ref[...]
        dh = dh_ref[...]
        dx1 = ALPHA * d2_ref[...] + dh * (1.0 + m[SCALE_F:SCALE_F + 1])
        dscale_ref[...] += jnp.sum(dh * x1, axis=0, keepdims=True)
        dshift_ref[...] += jnp.sum(dh, axis=0, keepdims=True)
        dg_ref[...] += jnp.sum(dx1 * xhat, axis=0, keepdims=True)
        db_ref[...] += jnp.sum(dx1, axis=0, keepdims=True)
        dyg = dx1 * g_ref[...]
        dxpre = rstd * (dyg - jnp.mean(dyg, axis=-1, keepdims=True) - xhat * jnp.mean(dyg * xhat, axis=-1, keepdims=True))
        dxpre_ref[...] = dxpre
        dmix_ref[...] = (m[GATE_T:GATE_T + 1] * dxpre).astype(BF16)
        dgate_ref[...] += jnp.sum(dxpre * mix_ref[...], axis=0, keepdims=True)

    vec = ((1, D_MODEL), F32, "acc")
    return rowcall(body, name="ln1_bwd", S=S, ts=512,
                   ins=[(dxpre2, "row"), (dh2, "row"), (xpre1, "row"), (mix, "row"), (mod, "vec"), (ln_g, "vec"), (ln_b, "vec")],
                   outs=[((S, D_MODEL), F32, "row"), ((S, D_MODEL), BF16, "row"), vec, vec, vec, vec, vec])


def merge_bwd(dmerged, gates_raw, b_gate, ya, yb):
    S = ya.shape[0]

    def body(d_ref, g_ref, b_ref, ya_ref, yb_ref, dya_ref, dyb_ref, dg_ref, dbg_ref):
        gt = _sigmoid(g_ref[...].astype(F32) + b_ref[...])
        d = d_ref[...].astype(F32)
        ga, gb = gt[:, :D_MODEL], gt[:, D_MODEL:]
        dya_ref[...] = (d * ga).astype(BF16)
        dyb_ref[...] = (d * gb).astype(BF16)
        dgr = jnp.concatenate([d * ya_ref[...].astype(F32) * ga * (1.0 - ga),
                               d * yb_ref[...].astype(F32) * gb * (1.0 - gb)], axis=1)
        dg_ref[...] = dgr.astype(BF16)
        dbg_ref[...] += jnp.sum(dgr, axis=0, keepdims=True)

    return rowcall(body, name="merge_bwd", S=S, ts=512,
                   ins=[(dmerged, "row"), (gates_raw, "row"), (b_gate, "vec"), (ya, "row"), (yb, "row")],
                   outs=[((S, D_MODEL), BF16, "row"), ((S, D_MODEL), BF16, "row"), ((S, 2 * D_MODEL), BF16, "row"),
                         ((1, 2 * D_MODEL), F32, "acc")])


def attn_bwd(qkv_pad, bias, do_b):
    S = qkv_pad.shape[0] - PAD_ROWS

    def body(q_ref, k_ref, v_ref, bias_ref, do_ref, dq_ref, dk_ref, dv_ref, db_ref, b_ref):
        n = pl.program_id(1)

        @pl.when(n == 0)
        def _():
            dk_ref[...] = jnp.zeros_like(dk_ref)
            dv_ref[...] = jnp.zeros_like(dv_ref)
            db_ref[...] = jnp.zeros_like(db_ref)
            b_ref[...] = jnp.full(b_ref.shape, NEG_INF, F32)
            for hh in range(HEADS_PER_GROUP):
                for qc in range(Q_CHUNKS):
                    b_ref[hh, qc * CHUNK:(qc + 1) * CHUNK, qc * CHUNK:qc * CHUNK + B_BAND] = bias_ref[hh]

        start = pl.multiple_of(n * Q_TILE, Q_TILE)
        kwin = k_ref[pl.ds(start, KEY_WIN), :]
        vwin = v_ref[pl.ds(start, KEY_WIN), :]
        qv, dov = q_ref[...], do_ref[...]
        valid = lax.broadcasted_iota(jnp.int32, (Q_TILE, KEY_WIN), 1) >= PAD_ROWS - n * Q_TILE
        dqs, dks, dvs = [], [], []
        for hh in range(HEADS_PER_GROUP):
            sl = slice(hh * B_DH, (hh + 1) * B_DH)
            p = _band_probs(qv[:, sl], kwin[:, sl], b_ref[hh], valid)
            dp = _dot1(dov[:, sl], vwin[:, sl], "nt")
            ds = p * (dp - jnp.sum(dp * p, axis=-1, keepdims=True))
            dbh = ds[0:CHUNK, 0:B_BAND]
            for qc in range(1, Q_CHUNKS):
                dbh = dbh + ds[qc * CHUNK:(qc + 1) * CHUNK, qc * CHUNK:qc * CHUNK + B_BAND]
            db_ref[hh] += dbh
            dsq = ds * (B_DH ** -0.5)
            dqs.append(_dot1(dsq, kwin[:, sl], "nn"))
            dks.append(_dot1(dsq, qv[:, sl], "tn"))
            dvs.append(_dot1(p, dov[:, sl], "tn"))
        dq_ref[...] = jnp.concatenate(dqs, axis=1).astype(BF16)
        dk_ref[pl.ds(start, KEY_WIN), :] += jnp.concatenate(dks, axis=1)
        dv_ref[pl.ds(start, KEY_WIN), :] += jnp.concatenate(dvs, axis=1)

    col = pl.BlockSpec((PAD_ROWS + S, GROUP_W), lambda g, n: (0, g))
    tile = pl.BlockSpec((Q_TILE, GROUP_W), lambda g, n: (n, g))
    return pl.pallas_call(
        body, name="attn_bwd", grid=(N_GROUPS, S // Q_TILE), in_specs=_attn_specs(S, Q_TILE) + [tile],
        out_specs=[tile, col, col, pl.BlockSpec((HEADS_PER_GROUP, CHUNK, B_BAND), lambda g, n: (g, 0, 0))],
        out_shape=[jax.ShapeDtypeStruct((S, B_W), BF16), jax.ShapeDtypeStruct((PAD_ROWS + S, B_W), F32),
                   jax.ShapeDtypeStruct((PAD_ROWS + S, B_W), F32), jax.ShapeDtypeStruct((B_HEADS, CHUNK, B_BAND), F32)],
        scratch_shapes=[pltpu.VMEM((HEADS_PER_GROUP, Q_TILE, KEY_WIN), F32)],
        compiler_params=_cparams(("parallel", "arbitrary")),
    )(qkv_pad, qkv_pad, qkv_pad, bias, do_b)


def gate_a_bwd(do_a, o_pre, z, norm_w):
    S = o_pre.shape[0]

    def body(d_ref, o_ref, z_ref, nw_ref, dop_ref, dz_ref, dnw_ref):
        nw = nw_ref[...]
        acc = jnp.zeros((1, A_DK), F32)
        for h in range(A_HEADS):
            sl = slice(h * A_DK, (h + 1) * A_DK)
            oh, zh, dh = o_ref[:, sl], z_ref[:, sl].astype(F32), d_ref[:, sl].astype(F32)
            r = lax.rsqrt(jnp.mean(oh * oh, axis=-1, keepdims=True) + RMS_EPS)
            sz, dsz = _silu_and_grad(zh)
            dz_ref[:, sl] = (dh * oh * r * nw * dsz).astype(BF16)
            acc = acc + jnp.sum(dh * oh * r * sz, axis=0, keepdims=True)
            t = dh * nw * sz
            dop_ref[:, sl] = r * t - oh * (r * r * r) * jnp.mean(t * oh, axis=-1, keepdims=True)
        dnw_ref[...] += acc

    return rowcall(body, name="gate_a_bwd", S=S, ts=512,
                   ins=[(do_a, "row"), (o_pre, "row"), (z, "row"), (norm_w, "vec")],
                   outs=[((S, A_W), F32, "row"), ((S, A_W), BF16, "row"), ((1, A_DK), F32, "acc")])


def delta_bwd(q, k, v, beta, g, sprev, tinv, do):
    S = q.shape[0]
    n_chunks = S // CHUNK

    def body(q_ref, k_ref, v_ref, beta_ref, g_ref, sprev_ref, t_ref, do_ref,
             dq_ref, dk_ref, dv_ref, dbeta_ref, dg_ref, dstate_ref):
        @pl.when(pl.program_id(0) == 0)
        def _():
            dstate_ref[...] = jnp.zeros_like(dstate_ref)

        mk = _tri_masks()
        causal, strict, eye = mk["causal"], mk["strict"], mk["eye"]
        blk_end = (lax.broadcasted_iota(jnp.int32, (GROUP_ROWS, 1), 0) & (CHUNK - 1)) == CHUNK - 1
        lane = lax.broadcasted_iota(jnp.int32, (CHUNK, A_HEADS), 1)
        betav, gv = beta_ref[...], g_ref[...]
        dbeta_t = jnp.zeros((CHUNK, A_HEADS), F32)
        dg_t = jnp.zeros((CHUNK, A_HEADS), F32)
        groups, heads = range(N_HEAD_GROUPS), range(HEAD_GROUP)
        st = [dict() for _ in groups]

        def local_part(grp, s):
            s["qs"], s["ks"], s["vs"] = _stack_heads(q_ref, grp), _stack_heads(k_ref, grp), _stack_heads(v_ref, grp)
            s["dos"] = _stack_heads(do_ref, grp)
            s["bs"] = _stack_cols(betav, grp)
            s["loc"] = loc = _delta_local(s["qs"], s["ks"], s["vs"], s["bs"], _stack_cols(gv, grp), mk)
            s["tinv"] = t_ref[0, grp]
            s["rhs"] = jnp.concatenate([loc["vb"], loc["y"]], axis=1)
            s["uw"] = _dot3(s["tinv"], s["rhs"], "nn")

        def state_part(grp, s):
            loc, uw, dos, qs = s["loc"], s["uw"], s["dos"], s["qs"]
            gam, kd, gl, gc = loc["gam"], loc["kd"], loc["gl"], loc["gc"]
            qg = qs * gam
            egl = jnp.exp(gl)
            hid = [grp * HEAD_GROUP + j for j in heads]
            s0 = [sprev_ref[0, h] for h in hid]
            ds1 = [dstate_ref[h] for h in hid]
            w = [_head_rows(uw, j)[:, A_DK:] for j in heads]
            vn = [_head_rows(uw, j)[:, :A_DK] - _dot1(w[j], s0[j], "nn") for j in heads]
            vns = jnp.concatenate(vn, axis=0)
            dvn_local = _dot1(loc["p"], dos, "tn")
            dvn = [_head_rows(dvn_local, j) + _dot1(_head_rows(kd, j), ds1[j], "nn") for j in heads]
            dvns = jnp.concatenate(dvn, axis=0)
            s["dp"] = jnp.where(causal, _dot1(dos, vns, "nt"), 0.0)
            dqg = jnp.concatenate([_dot1(_head_rows(dos, j), s0[j], "nt") for j in heads], axis=0)
            s["dq"] = dqg * gam
            dgc = jnp.sum(dqg * qg, axis=-1, keepdims=True)
            for j in heads:
                dstate_ref[hid[j]] = (_dot1(_head_rows(qg, j), _head_rows(dos, j), "tn")
                                      + egl[(j + 1) * CHUNK - 1:(j + 1) * CHUNK] * ds1[j] - _dot1(w[j], dvn[j], "tn"))
            dkd = jnp.concatenate([_dot1(vn[j], ds1[j], "nt") for j in heads], axis=0)
            s["dk"] = dkd * jnp.exp(gl - gc)
            t1 = jnp.sum(dkd * kd, axis=-1, keepdims=True)
            dgl = jnp.concatenate(
                [jnp.broadcast_to(jnp.sum(_head_rows(t1, j), axis=0, keepdims=True)
                                  + jnp.sum(jnp.sum(ds1[j] * s0[j], axis=-1, keepdims=True), axis=0, keepdims=True)
                                  * egl[(j + 1) * CHUNK - 1:(j + 1) * CHUNK], (CHUNK, 1)) for j in heads], axis=0)
            s["dgc"] = dgc - t1 + jnp.where(blk_end, dgl, 0.0)
            s["duw"] = jnp.concatenate(
                [dvns, jnp.concatenate([-_dot1(dvn[j], s0[j], "nt") for j in heads], axis=0)], axis=1)

        def solve_part(grp, s):
            s["dvby"] = _dot3(s["tinv"], s["duw"], "tn")
            s["dt"] = _dot3(s["duw"], s["rhs"], "nt")

        def inverse_part_a(grp, s):
            s["tdt"] = _dot3(s["tinv"], s["dt"], "tn")

        def inverse_part_b(grp, s):
            s["da"] = jnp.where(strict, -_dot3(s["tdt"], s["tinv"], "nt"), 0.0)

        def finish(grp, s):
            loc, qs, ks, vs, bs, da, dp, dvby = s["loc"], s["qs"], s["ks"], s["vs"], s["bs"], s["da"], s["dp"], s["dvby"]
            gam, decay = loc["gam"], loc["decay"]
            dm = da * decay
            dn = dp * decay
            e = da * loc["a"] + dp * loc["p"]
            dgc = s["dgc"] + jnp.sum(e, axis=1, keepdims=True) - _row_to_col(jnp.sum(e, axis=0, keepdims=True), eye)
            dy = dvby[:, A_DK:]
            dvb = dvby[:, :A_DK]
            dkb = _dot1(dm, ks, "nn") + dy * gam
            dk = s["dk"] + _dot1(dm, loc["kb"], "tn") + _dot1(dn, qs, "tn") + dkb * bs
            dq = s["dq"] + _dot1(dn, ks, "nn")
            dgc = dgc + jnp.sum(dy * loc["y"], axis=-1, keepdims=True)
            dbeta = jnp.sum(dkb * ks, axis=-1, keepdims=True) + jnp.sum(dvb * vs, axis=-1, keepdims=True)
            dv = dvb * bs
            dgs = jnp.sum(jnp.where(mk["upper"], _col_to_row(dgc, eye), 0.0), axis=1, keepdims=True)
            for j in heads:
                h = grp * HEAD_GROUP + j
                sl = slice(h * A_DK, (h + 1) * A_DK)
                dq_ref[:, sl] = _head_rows(dq, j)
                dk_ref[:, sl] = _head_rows(dk, j)
                dv_ref[:, sl] = _head_rows(dv, j)
            s["dbeta"], s["dgs"] = dbeta, dgs

        for stage in (local_part, state_part, solve_part, inverse_part_a, inverse_part_b, finish):
            for grp in groups:
                stage(grp, st[grp])
        for grp in groups:
            for j in heads:
                h = grp * HEAD_GROUP + j
                dbeta_t = dbeta_t + jnp.where(lane == h, _head_rows(st[grp]["dbeta"], j), 0.0)
                dg_t = dg_t + jnp.where(lane == h, _head_rows(st[grp]["dgs"], j), 0.0)
        dbeta_ref[...] = dbeta_t
        dg_ref[...] = dg_t

    rev = lambda n: (n_chunks - 1 - n, 0)
    rev4 = lambda n: (n_chunks - 1 - n, 0, 0, 0)
    tile = pl.BlockSpec((CHUNK, A_W), rev)
    small = pl.BlockSpec((CHUNK, A_HEADS), rev)
    return pl.pallas_call(
        body, name="delta_bwd", grid=(n_chunks,),
        in_specs=[tile, tile, tile, small, small, pl.BlockSpec((1, A_HEADS, A_DK, A_DK), rev4),
                  pl.BlockSpec((1, N_HEAD_GROUPS, GROUP_ROWS, GROUP_ROWS), rev4), tile],
        out_specs=[tile, tile, tile, small, small],
        out_shape=[jax.ShapeDtypeStruct((S, A_W), F32)] * 3 + [jax.ShapeDtypeStruct((S, A_HEADS), F32)] * 2,
        scratch_shapes=[pltpu.VMEM((A_HEADS, A_DK, A_DK), F32)],
        compiler_params=_cparams(("arbitrary",)),
    )(q, k, v, beta, g, sprev, tinv, do)


def _prep_a_dpre(raw, raw_prev, w, dq, dk, dv):
    y, dy_dpre = _prep_a_core(raw, raw_prev, w)
    parts = []
    for h in range(A_HEADS):
        yq = y[:, h * A_DK:(h + 1) * A_DK]
        dqh = dq[:, h * A_DK:(h + 1) * A_DK]
        rq = lax.rsqrt(jnp.sum(yq * yq, axis=-1, keepdims=True) + L2_EPS)
        parts.append((A_DK ** -0.5) * (rq * dqh - yq * (rq * rq * rq) * jnp.sum(dqh * yq, axis=-1, keepdims=True)))
    for h in range(A_HEADS):
        yk = y[:, A_W + h * A_DK:A_W + (h + 1) * A_DK]
        dkh = dk[:, h * A_DK:(h + 1) * A_DK]
        rk = lax.rsqrt(jnp.sum(yk * yk, axis=-1, keepdims=True) + L2_EPS)
        parts.append(rk * dkh - yk * (rk * rk * rk) * jnp.sum(dkh * yk, axis=-1, keepdims=True))
    parts.append(dv)
    return jnp.concatenate(parts, axis=1) * dy_dpre


def prep_a_bwd(qkv_raw, ba, conv_a, a_log, dt_bias, dq, dk, dv, dbeta, dg):
    S = qkv_raw.shape[0]
    ts = 256

    def body(x_ref, xp_ref, xn_ref, ba_ref, w_ref, al_ref, dt_ref, dq_ref, dqn_ref, dk_ref, dkn_ref, dv_ref, dvn_ref,
             dbeta_ref, dg_ref, draw_ref, dba_ref, dw_ref, dal_ref, ddt_ref):
        i = pl.program_id(0)
        first = (i > 0).astype(F32)
        last = (i < pl.num_programs(0) - 1).astype(F32)
        w = w_ref[...]
        cur, prev = x_ref[...].astype(F32), _halo_prev(xp_ref) * first
        dpre = _prep_a_dpre(cur, prev, w, dq_ref[...], dk_ref[...], dv_ref[...])
        dpre_n = _prep_a_dpre(_halo_next(xn_ref), cur[ts - 8:ts], w, _halo_next(dqn_ref), _halo_next(dkn_ref),
                              _halo_next(dvn_ref)) * last
        for j in range(A_CONV):
            dw_ref[j:j + 1, :] += jnp.sum(dpre * _shift_down(cur, prev, A_CONV - 1 - j), axis=0, keepdims=True)
        draw = dpre * w[A_CONV - 1:A_CONV]
        for j in range(A_CONV - 1):
            draw = draw + _shift_up(dpre, dpre_n, A_CONV - 1 - j) * w[j:j + 1]
        draw_ref[...] = draw.astype(BF16)
        bav = ba_ref[...]
        beta = _sigmoid(bav[:, 0:A_HEADS])
        xa = bav[:, A_HEADS:2 * A_HEADS] + dt_ref[...]
        nexp = -jnp.exp(al_ref[...])
        dgv = dg_ref[...]
        da = dgv * nexp * _sigmoid(xa)
        dba_ref[:, 0:A_HEADS] = dbeta_ref[...] * beta * (1.0 - beta)
        dba_ref[:, A_HEADS:2 * A_HEADS] = da
        dal_ref[...] += jnp.sum(dgv * nexp * _softplus(xa), axis=0, keepdims=True)
        ddt_ref[...] += jnp.sum(da, axis=0, keepdims=True)

    return rowcall(
        body, name="prep_a_bwd", S=S, ts=ts,
        ins=[(qkv_raw, "row"), (qkv_raw, "prev"), (qkv_raw, "next"), (ba, "row"), (conv_a, "vec"), (a_log, "vec"),
             (dt_bias, "vec"), (dq, "row"), (dq, "next"), (dk, "row"), (dk, "next"), (dv, "row"), (dv, "next"),
             (dbeta, "row"), (dg, "row")],
        outs=[((S, 3 * A_W), BF16, "row"), ((S, 2 * A_HEADS), F32, "row"), ((A_CONV, 3 * A_W), F32, "acc"),
              ((1, A_HEADS), F32, "acc"), ((1, A_HEADS), F32, "acc")])


def grad_x_final(dh1, x, dxpre1, mod):
    S = x.shape[0]

    def body(dh_ref, x_ref, dx_ref, m_ref, gx_ref, dscale_ref, dshift_ref):
        dh = dh_ref[...]
        gx_ref[...] = ALPHA * dx_ref[...] + dh * (1.0 + m_ref[...][SCALE_T:SCALE_T + 1])
        dscale_ref[...] += jnp.sum(dh * x_ref[...], axis=0, keepdims=True)
        dshift_ref[...] += jnp.sum(dh, axis=0, keepdims=True)

    vec = ((1, D_MODEL), F32, "acc")
    return rowcall(body, name="grad_x_final", S=S, ts=512, ins=[(dh1, "row"), (x, "row"), (dxpre1, "row"), (mod, "vec")],
                   outs=[((S, D_MODEL), F32, "row"), vec, vec])


_C_QKV, _C_Z, _C_BA, _C_QKVB, _C_G = 0, 3 * A_W, 4 * A_W, 4 * A_W + 2 * A_HEADS, 4 * A_W + 2 * A_HEADS + 3 * B_W
BA_PAD = 128


def split_w_in(w_in):
    ba = jnp.pad(w_in[:, _C_BA:_C_QKVB], ((0, 0), (0, BA_PAD - 2 * A_HEADS)))
    return dict(qkv=w_in[:, _C_QKV:_C_Z], z=w_in[:, _C_Z:_C_BA], ba=ba, qkvb=w_in[:, _C_QKVB:_C_G], g=w_in[:, _C_G:])


def join_w_in(p):
    return jnp.concatenate([p["qkv"], p["z"], p["ba"][:, :2 * A_HEADS], p["qkvb"], p["g"]], axis=1)


def forward_local(x, target, mod, w, sm, late_weights=None):
    h1 = modulate(x, mod, SHIFT_T, SCALE_T, "mod_t")
    qkv_raw = mm(h1, w["qkv"], mode="nn", out_dtype=BF16, name="proj_qkv")
    z = mm(h1, w["z"], mode="nn", out_dtype=BF16, name="proj_z")
    ba = mm(h1, w["ba"], mode="nn", out_dtype=F32, name="proj_ba")
    qkvb = mm(h1, w["qkvb"], mode="nn", out_dtype=BF16, name="proj_qkvb")
    gates_raw = mm(h1, w["g"], mode="nn", out_dtype=BF16, name="proj_g")
    q, k, v, beta, g = prep_a_fwd(qkv_raw, ba, sm["conv_a"], sm["a_log"], sm["dt_bias"])
    o_pre, sprev, tinv = delta_fwd(q, k, v, beta, g)
    o_a = gate_a_fwd(o_pre, z, sm["norm_a"])
    qkv_pad = jnp.pad(qkvb, ((PAD_ROWS, 0), (0, 0)))
    bias = jnp.transpose(bias_expand(sm["rel_bias"]), (1, 0, 2))
    o_b = attn_fwd(qkv_pad, bias)
    if late_weights is not None:
        w = dict(w, **late_weights(o_b))
    ya = mm(o_a, w["branch_a"], mode="nn", out_dtype=BF16, name="branch_a")
    yb = mm(o_b, w["branch_b"], mode="nn", out_dtype=BF16, name="branch_b")
    merged = merge_fwd(gates_raw, sm["b_gate"], ya, yb)
    mix = mm(merged, w["o"], mode="nn", out_dtype=F32, name="mix")
    xpre1, x1, h2 = ln1_fwd(x, mix, mod, sm["ln1_g"], sm["ln1_b"])
    up = mm(h2, w["up"], mode="nn", out_dtype=BF16, name="ffn_up", b_shards=True)
    act = ffn_act_fwd(up, sm["conv_ffn"], sm["b_conv_ffn"])
    ffn = mm(act, w["down"], mode="nn", out_dtype=F32, name="ffn_down")
    dxpre2, dffn, loss, dgate_f, dln2_g, dln2_b = final_fwd_bwd(x1, ffn, target, mod, sm["ln2_g"], sm["ln2_b"])
    saved = dict(h1=h1, qkv_raw=qkv_raw, z=z, ba=ba, gates_raw=gates_raw, q=q, k=k, v=v, beta=beta, g=g,
                 o_pre=o_pre, sprev=sprev, tinv=tinv, o_a=o_a, qkv_pad=qkv_pad, bias=bias, o_b=o_b, ya=ya, yb=yb,
                 merged=merged, mix=mix, xpre1=xpre1, x1=x1, h2=h2, up=up, act=act, ffn=ffn, w=w)
    return loss, dxpre2, dffn, dict(gate_f=dgate_f, ln2_g=dln2_g, ln2_b=dln2_b), saved


def backward_local(x, mod, sm, dxpre2, dffn, fin, sv, hooks=None):
    w = sv["w"]
    dact = mm(dffn, w["down"], mode="nt", out_dtype=BF16, name="d_act")
    gw_down = mm(sv["act"], dffn, mode="tn", out_dtype=BF16, name="gw_down")
    dup, dconv_ffn, db_conv_ffn = ffn_act_bwd(dact, sv["up"], sm["conv_ffn"], sm["b_conv_ffn"])
    dh2 = mm(dup, w["up"], mode="nt", out_dtype=F32, name="d_h2", b_shards=True)
    gw_up = mm(sv["h2"], dup, mode="tn", out_dtype=BF16, name="gw_up", out_shards=N_CHIPS)
    dxpre1, dmix, dsc_f, dsh_f, dgate_t, dln1_g, dln1_b = ln1_bwd(
        dxpre2, dh2, sv["xpre1"], sv["mix"], mod, sm["ln1_g"], sm["ln1_b"])
    dmerged = mm(dmix, w["o"], mode="nt", out_dtype=BF16, name="d_merged")
    gw_o = mm(sv["merged"], dmix, mode="tn", out_dtype=BF16, name="gw_o")
    dya, dyb, dgates, db_gate = merge_bwd(dmerged, sv["gates_raw"], sm["b_gate"], sv["ya"], sv["yb"])
    do_a = mm(dya, w["branch_a"], mode="nt", out_dtype=BF16, name="d_oa")
    gw_branch_a = mm(sv["o_a"], dya, mode="tn", out_dtype=BF16, name="gw_branch_a")
    do_b = mm(dyb, w["branch_b"], mode="nt", out_dtype=BF16, name="d_ob")
    gw_branch_b = mm(sv["o_b"], dyb, mode="tn", out_dtype=BF16, name="gw_branch_b")
    bias = sv["bias"]
    if hooks is not None:
        bias = bias + hooks["late_start"](dict(w_branch_a=gw_branch_a, w_branch_b=gw_branch_b, w_o=gw_o, w_up=gw_up,
                                               w_down=gw_down))[0, 0]
    dq_b, dk_pad, dv_pad, dbias = attn_bwd(sv["qkv_pad"], bias, do_b)
    if hooks is not None:
        dbias = dbias + hooks["late_finish"](dq_b)[0, 0]
    dqkvb = jnp.concatenate([dq_b, dk_pad[PAD_ROWS:].astype(BF16), dv_pad[PAD_ROWS:].astype(BF16)], axis=1)
    drel_bias = bias_reduce(jnp.transpose(dbias, (1, 0, 2)))
    do_pre, dz, dnorm_a = gate_a_bwd(do_a, sv["o_pre"], sv["z"], sm["norm_a"])
    dq, dk, dv, dbeta, dg = delta_bwd(sv["q"], sv["k"], sv["v"], sv["beta"], sv["g"], sv["sprev"], sv["tinv"], do_pre)
    dqkv_raw, dba16, dconv_a, da_log, ddt_bias = prep_a_bwd(
        sv["qkv_raw"], sv["ba"], sm["conv_a"], sm["a_log"], sm["dt_bias"], dq, dk, dv, dbeta, dg)
    dba = jnp.pad(dba16, ((0, 0), (0, BA_PAD - 2 * A_HEADS))).astype(BF16)
    pieces = dict(qkv=dqkv_raw, z=dz, ba=dba, qkvb=dqkvb, g=dgates)
    gw_in = join_w_in({key: mm(sv["h1"], dpiece, mode="tn", out_dtype=BF16, name="gw_in_" + key)
                       for key, dpiece in pieces.items()})
    w_ba = w["ba"]
    w_z = w["z"]
    if hooks is not None:
        w_ba = w_ba + hooks["w_in_start"](gw_in)[0, 0].astype(BF16)
    dh1 = mm(pieces["ba"], w_ba, mode="nt", out_dtype=F32, name="d_h1_ba")
    dh1 = mm(pieces["qkv"], w["qkv"], mode="nt", out_dtype=F32, name="d_h1_qkv", acc_in=dh1)
    if hooks is not None:
        w_z = w_z + hooks["w_in_finish"](dh1)[0, 0].astype(BF16)
    dh1 = mm(pieces["z"], w_z, mode="nt", out_dtype=F32, name="d_h1_z", acc_in=dh1)
    for key in ("qkvb", "g"):
        dh1 = mm(pieces[key], w[key], mode="nt", out_dtype=F32, name="d_h1_" + key, acc_in=dh1)
    grad_x, dsc_t, dsh_t = grad_x_final(dh1, x, dxpre1, mod)
    dmod = jnp.concatenate([dsh_t, dsc_t, dgate_t, dsh_f, dsc_f, fin["gate_f"]], axis=0)
    gw = dict(w_in=gw_in, w_branch_a=gw_branch_a, w_branch_b=gw_branch_b, w_o=gw_o, w_up=gw_up, w_down=gw_down)
    gs = dict(b_gate=db_gate, conv_a=dconv_a, a_log=da_log, dt_bias=ddt_bias, norm_a=dnorm_a, rel_bias=drel_bias,
              ln1_g=dln1_g, ln1_b=dln1_b, conv_ffn=dconv_ffn, b_conv_ffn=db_conv_ffn, ln2_g=fin["ln2_g"], ln2_b=fin["ln2_b"])
    return grad_x, dmod, gw, gs


MESH = pl.DeviceIdType.MESH
ANY = pl.BlockSpec(memory_space=pl.ANY)
WHOLE_VMEM = pl.BlockSpec(memory_space=pltpu.VMEM)


def _place():
    return lax.axis_index("x"), lax.axis_index("y"), lax.axis_index("c")


def allgather8(blk, name):
    m_per, n = blk.shape

    def body(x_ref, out_ref, send_sems, recv_sems, local_sem):
        x, y, c = _place()
        me, sibling = (x, y, c), (x, y, 1 - c)
        chips = [(1 - x, y), (x, 1 - y), (1 - x, 1 - y)]

        def rows(px, py, pc):
            return out_ref.at[pl.ds((4 * px + 2 * py + pc) * m_per, m_per), :]

        def copy(k, block, to, src=None):
            return pltpu.make_async_remote_copy(
                src_ref=rows(*block) if src is None else src, dst_ref=rows(*block),
                send_sem=send_sems.at[k], recv_sem=recv_sems.at[k], device_id=to, device_id_type=MESH)

        mine = pltpu.make_async_copy(x_ref, rows(*me), local_sem)
        mine.start()
        first = [copy(0, me, sibling, src=x_ref)]
        first += [copy(1 + j, me, (*chip, c), src=x_ref) for j, chip in enumerate(chips)]
        for cp in first:
            cp.start()
        passed = [copy(4 + j, (*chip, c), sibling) for j, chip in enumerate(chips)]
        for j, chip in enumerate(chips):
            copy(1 + j, (*chip, c), me).wait_recv()
            passed[j].start()
        copy(0, sibling, me).wait_recv()
        for j, chip in enumerate(chips):
            copy(4 + j, (*chip, 1 - c), me).wait_recv()
        for cp in first + passed:
            cp.wait_send()
        mine.wait()

    return pl.pallas_call(
        body, name=name, out_shape=jax.ShapeDtypeStruct((N_DEV * m_per, n), blk.dtype),
        in_specs=[WHOLE_VMEM], out_specs=WHOLE_VMEM,
        scratch_shapes=[pltpu.SemaphoreType.DMA((7,)), pltpu.SemaphoreType.DMA((7,)), pltpu.SemaphoreType.DMA],
    )(blk)


def _chip_peers(x, y):
    return [(1 - x, y), (x, 1 - y), (1 - x, 1 - y)]


def chip_exchange(arrs, name, scatter):
    n = len(arrs)

    def body(*refs):
        ins, outs = refs[:n], refs[n:2 * n]
        send_sems, recv_sems, local_sems = refs[2 * n:]
        x, y, c = _place()
        me = 2 * x + y
        sibling = (x, y, 1 - c)
        peers = _chip_peers(x, y)

        def half(ref, which):
            r2 = ref.shape[0] // 2
            return ref.at[pl.ds(which * r2, r2), :]

        def outgoing(a, chip):
            return ins[a].at[chip] if scatter else ins[a]

        def copy(k, src, dst, to):
            return pltpu.make_async_remote_copy(src_ref=src, dst_ref=dst, send_sem=send_sems.at[k],
                                                recv_sem=recv_sems.at[k], device_id=to, device_id_type=MESH)

        started, local = [], []
        for a in range(n):
            lc = pltpu.make_async_copy(outgoing(a, me), outs[a].at[me], local_sems.at[a])
            lc.start()
            local.append(lc)
            for j, (px, py) in enumerate(peers):
                cp = copy(6 * a + j, half(outgoing(a, 2 * px + py), c), half(outs[a].at[me], c), (px, py, c))
                cp.start()
                started.append(cp)
        for a in range(n):
            for j, (px, py) in enumerate(peers):
                landed = half(outs[a].at[2 * px + py], c)
                copy(6 * a + j, landed, landed, (px, py, c)).wait_recv()
                relay = copy(6 * a + 3 + j, landed, landed, sibling)
                relay.start()
                started.append(relay)
        for a in range(n):
            for j, (px, py) in enumerate(peers):
                other = half(outs[a].at[2 * px + py], 1 - c)
                copy(6 * a + 3 + j, other, other, sibling).wait_recv()
        for cp in started:
            cp.wait_send()
        for lc in local:
            lc.wait()

    out_shape = [jax.ShapeDtypeStruct(a.shape if scatter else (N_CHIPS,) + a.shape, a.dtype) for a in arrs]
    return pl.pallas_call(
        body, name=name, out_shape=out_shape, in_specs=[ANY] * n, out_specs=[ANY] * n,
        scratch_shapes=[pltpu.SemaphoreType.DMA((6 * n,)), pltpu.SemaphoreType.DMA((6 * n,)), pltpu.SemaphoreType.DMA((n,))],
    )(*arrs)


HBM_SPEC = pl.BlockSpec(memory_space=pltpu.HBM)
SEM_SPEC = pl.BlockSpec(memory_space=pltpu.SEMAPHORE)
SIDE_EFFECT = pltpu.SideEffectType.DATAFLOW_SIDE_EFFECTING


def _in_hbm(a):
    return pltpu.with_memory_space_constraint(a, pltpu.HBM)


def exchange_start(arrs, name, scatter, after):
    n = len(arrs)
    lands = [lax.empty(a.shape if scatter else (N_CHIPS,) + a.shape, a.dtype) for a in arrs]

    def body(*refs):
        ins, zones = refs[:n], refs[n:2 * n]
        send_sems, recv_sems, token = refs[2 * n + 1], refs[2 * n + 2], refs[-1]
        x, y, c = _place()
        me = 2 * x + y
        for a in range(n):
            for j, (px, py) in enumerate(_chip_peers(x, y)):
                pltpu.make_async_remote_copy(
                    src_ref=ins[a].at[2 * px + py] if scatter else ins[a], dst_ref=zones[a].at[me],
                    send_sem=send_sems.at[3 * a + j], recv_sem=recv_sems.at[3 * a + j],
                    device_id=(px, py, c), device_id_type=MESH).start()
        token[...] = jnp.zeros_like(token)

    res = pl.pallas_call(
        body, name=name,
        out_shape=[pltpu.SemaphoreType.DMA((3 * n,)), pltpu.SemaphoreType.DMA((3 * n,))]
        + [pltpu.HBM(a.shape, a.dtype) for a in arrs] + [pltpu.HBM(z.shape, z.dtype) for z in lands]
        + [jax.ShapeDtypeStruct((8, 128), F32)],
        in_specs=[HBM_SPEC] * (2 * n) + [ANY], out_specs=[SEM_SPEC, SEM_SPEC] + [HBM_SPEC] * (2 * n) + [WHOLE_VMEM],
        input_output_aliases={i: 2 + i for i in range(2 * n)},
        compiler_params=pltpu.CompilerParams(has_side_effects=SIDE_EFFECT),
    )(*[_in_hbm(a) for a in arrs], *[_in_hbm(z) for z in lands], after)
    return dict(send=res[0], recv=res[1], src=res[2:2 + n], zones=res[2 + n:2 + 2 * n], token=res[-1], scatter=scatter)


def exchange_wait(handle, name, after):
    srcs, zones, scatter = handle["src"], handle["zones"], handle["scatter"]
    n = len(srcs)

    def body(*refs):
        ins, lands = refs[:n], refs[n:2 * n]
        send_sems, recv_sems = refs[2 * n], refs[2 * n + 1]
        x, y, c = _place()
        me = 2 * x + y
        for a in range(n):
            for j, (px, py) in enumerate(_chip_peers(x, y)):
                cp = pltpu.make_async_remote_copy(
                    src_ref=ins[a].at[me] if scatter else ins[a], dst_ref=lands[a].at[2 * px + py],
                    send_sem=send_sems.at[3 * a + j], recv_sem=recv_sems.at[3 * a + j],
                    device_id=(px, py, c), device_id_type=MESH)
                cp.wait_send()
                cp.wait_recv()

    res = pl.pallas_call(
        body, name=name, out_shape=[pltpu.HBM(a.shape, a.dtype) for a in list(srcs) + list(zones)],
        in_specs=[HBM_SPEC] * (2 * n) + [SEM_SPEC, SEM_SPEC, ANY], out_specs=[HBM_SPEC] * (2 * n),
        input_output_aliases={i: i for i in range(2 * n)},
        compiler_params=pltpu.CompilerParams(has_side_effects=SIDE_EFFECT),
    )(*srcs, *zones, handle["send"], handle["recv"], after)
    return res[n:]


def swap_start(arrs, name, after):
    n = len(arrs)
    lands = [lax.empty(a.shape, a.dtype) for a in arrs]

    def body(*refs):
        ins, zones = refs[:n], refs[n:2 * n]
        send_sems, recv_sems, token = refs[2 * n + 1], refs[2 * n + 2], refs[-1]
        x, y, c = _place()
        for a in range(n):
            pltpu.make_async_remote_copy(src_ref=ins[a], dst_ref=zones[a], send_sem=send_sems.at[a], recv_sem=recv_sems.at[a],
                                         device_id=(x, y, 1 - c), device_id_type=MESH).start()
        token[...] = jnp.zeros_like(token)

    res = pl.pallas_call(
        body, name=name,
        out_shape=[pltpu.SemaphoreType.DMA((n,)), pltpu.SemaphoreType.DMA((n,))]
        + [pltpu.HBM(a.shape, a.dtype) for a in arrs] * 2 + [jax.ShapeDtypeStruct((8, 128), F32)],
        in_specs=[HBM_SPEC] * (2 * n) + [ANY], out_specs=[SEM_SPEC, SEM_SPEC] + [HBM_SPEC] * (2 * n) + [WHOLE_VMEM],
        input_output_aliases={i: 2 + i for i in range(2 * n)},
        compiler_params=pltpu.CompilerParams(has_side_effects=SIDE_EFFECT),
    )(*[_in_hbm(a) for a in arrs], *[_in_hbm(z) for z in lands], after)
    return dict(send=res[0], recv=res[1], src=res[2:2 + n], zones=res[2 + n:2 + 2 * n], token=res[-1])


def swap_wait(handle, name, after):
    srcs, zones = handle["src"], handle["zones"]
    n = len(srcs)

    def body(*refs):
        ins, lands = refs[:n], refs[n:2 * n]
        send_sems, recv_sems = refs[2 * n], refs[2 * n + 1]
        x, y, c = _place()
        for a in range(n):
            cp = pltpu.make_async_remote_copy(src_ref=ins[a], dst_ref=lands[a], send_sem=send_sems.at[a],
                                              recv_sem=recv_sems.at[a], device_id=(x, y, 1 - c), device_id_type=MESH)
            cp.wait_send()
            cp.wait_recv()

    res = pl.pallas_call(
        body, name=name, out_shape=[pltpu.HBM(a.shape, a.dtype) for a in list(srcs) + list(zones)],
        in_specs=[HBM_SPEC] * (2 * n) + [SEM_SPEC, SEM_SPEC, ANY], out_specs=[HBM_SPEC] * (2 * n),
        input_output_aliases={i: i for i in range(2 * n)},
        compiler_params=pltpu.CompilerParams(has_side_effects=SIDE_EFFECT),
    )(*srcs, *zones, handle["send"], handle["recv"], after)
    return res[:n], res[n:]


TILE_BYTES = 2 * 1024 * 1024


def _row_tile(rows, row_bytes):
    if rows * row_bytes <= TILE_BYTES or rows % 8:
        return rows
    best = 8
    for t in range(8, rows + 1, 8):
        if rows % t == 0 and t * row_bytes <= TILE_BYTES:
            best = t
    return best


def pair_add(a, b, name):
    shape = a.shape
    a, b = a.reshape(-1, shape[-1]), b.reshape(-1, shape[-1])
    R, C = a.shape
    tr = _row_tile(R, C * 4)

    def body(a_ref, b_ref, o_ref):
        o_ref[...] = (a_ref[...].astype(F32) + b_ref[...].astype(F32)).astype(BF16)

    spec = pl.BlockSpec((tr, C), lambda i: (i, 0))
    return pl.pallas_call(body, name=name, grid=(R // tr,), in_specs=[spec, spec], out_specs=spec,
                          out_shape=jax.ShapeDtypeStruct((R, C), BF16), compiler_params=_cparams(("parallel",)))(a, b).reshape(shape)


def sum_lead(parts, name):
    K, R, C = parts.shape
    tr = _row_tile(R, C * 4)

    def body(p_ref, o_ref):
        acc = p_ref[0].astype(F32)
        for j in range(1, K):
            acc = acc + p_ref[j].astype(F32)
        o_ref[...] = acc

    return pl.pallas_call(
        body, name=name, grid=(R // tr,), in_specs=[pl.BlockSpec((K, tr, C), lambda i: (0, i, 0))],
        out_specs=pl.BlockSpec((tr, C), lambda i: (i, 0)), out_shape=jax.ShapeDtypeStruct((R, C), F32),
        compiler_params=_cparams(("parallel",)))(parts)


def adamw(w, g, m, v, name):
    R, C = w.shape
    tr = _row_tile(R, C * 4)

    def body(w_ref, g_ref, m_ref, v_ref, d_ref, mo_ref, vo_ref):
        gv = g_ref[...]
        m2 = ADAM_B1 * m_ref[...] + (1.0 - ADAM_B1) * gv
        v2 = ADAM_B2 * v_ref[...] + (1.0 - ADAM_B2) * (gv * gv)
        m_hat = m2 / (1.0 - ADAM_B1 ** ADAM_STEP)
        v_hat = v2 / (1.0 - ADAM_B2 ** ADAM_STEP)
        d_ref[...] = -ADAM_LR * (m_hat / (jnp.sqrt(v_hat) + ADAM_EPS) + ADAM_WD * w_ref[...])
        mo_ref[...] = m2
        vo_ref[...] = v2

    spec = pl.BlockSpec((tr, C), lambda i: (i, 0))
    return pl.pallas_call(body, name=name, grid=(R // tr,), in_specs=[spec] * 4, out_specs=[spec] * 3,
                          out_shape=[jax.ShapeDtypeStruct((R, C), F32)] * 3, compiler_params=_cparams(("parallel",)))(w, g, m, v)


LANES = 1024


def _pack(arrs, rows):
    out, offs, r = [], [], 0
    for a in arrs:
        flat = a.reshape(-1)
        nr = -(-flat.shape[0] // LANES)
        out.append(jnp.pad(flat, (0, nr * LANES - flat.shape[0])))
        offs.append(r)
        r += nr
    assert r <= rows, (r, rows)
    out.append(jnp.zeros(((rows - r) * LANES,), F32))
    return jnp.concatenate(out).reshape(rows, LANES), offs


def _unpack(packed, offs, shapes):
    flat = packed.reshape(-1)
    return [flat[o * LANES:o * LANES + math.prod(s)].reshape(s) for o, s in zip(offs, shapes)]


WEIGHTS = ["w_ada", "b_ada", "w_in", "b_gate", "conv_a", "a_log", "dt_bias", "norm_a", "rel_bias", "w_branch_a",
           "w_branch_b", "w_o", "ln1_g", "ln1_b", "w_up", "conv_ffn", "b_conv_ffn", "w_down", "ln2_g", "ln2_b"]
BIG = ["w_in", "w_branch_a", "w_branch_b", "w_o", "w_up", "w_down"]
LATE = [n for n in BIG if n != "w_in"]
KEPT_SHARDED = {"w_up"}
COL_SHARDED = {"w_in", "w_up"}
SMALL_SHARDED = {"conv_a": 3 * A_W // N_CHIPS, "rel_bias": B_REL // N_CHIPS, "conv_ffn": 2 * D_FF // N_CHIPS}
SMALL = [n for n in WEIGHTS if n not in BIG and n != "w_ada"]


def _to_full(g4, name):
    if name in KEPT_SHARDED:
        return g4
    if name in COL_SHARDED:
        return jnp.transpose(g4, (1, 0, 2)).reshape(g4.shape[1], -1)
    return g4.reshape(-1, g4.shape[2])


def _to_shards(full, name):
    if name in KEPT_SHARDED:
        return full
    if name in COL_SHARDED:
        return jnp.transpose(full.reshape(full.shape[0], N_CHIPS, -1), (1, 0, 2))
    return full.reshape(N_CHIPS, -1, full.shape[1])


def kernel(x, c, w_ada, b_ada, w_in, b_gate, conv_a, a_log, dt_bias, norm_a, rel_bias, w_branch_a, w_branch_b, w_o, ln1_g, ln1_b, w_up, conv_ffn, b_conv_ffn, w_down, ln2_g, ln2_b, loss_target, m_w_ada, m_b_ada, m_w_in, m_b_gate, m_conv_a, m_a_log, m_dt_bias, m_norm_a, m_rel_bias, m_w_branch_a, m_w_branch_b, m_w_o, m_ln1_g, m_ln1_b, m_w_up, m_conv_ffn, m_b_conv_ffn, m_w_down, m_ln2_g, m_ln2_b, v_w_ada, v_b_ada, v_w_in, v_b_gate, v_conv_a, v_a_log, v_dt_bias, v_norm_a, v_rel_bias, v_w_branch_a, v_w_branch_b, v_w_o, v_ln1_g, v_ln1_b, v_w_up, v_conv_ffn, v_b_conv_ffn, v_w_down, v_ln2_g, v_ln2_b):
    args = dict(locals())
    wts = {n: args[n] for n in WEIGHTS}
    moms = {n: args["m_" + n] for n in WEIGHTS}
    vars_ = {n: args["v_" + n] for n in WEIGHTS}
    xi, yi, ci = _place()
    chip = 2 * xi + yi
    dev = 4 * xi + 2 * yi + ci
    ada_cols = w_ada.shape[2]

    sshapes = [wts[n].shape[1:] for n in SMALL_SHARDED]
    spack, soffs = _pack([wts[n][0] for n in SMALL_SHARDED], 16)
    first = allgather8(jnp.concatenate([jnp.pad(c, ((0, 7), (0, 0))), spack]), "gather_c_small_w").reshape(N_DEV, 24, LANES)
    c_all = first[:, 0]
    b_ada_sh = lax.dynamic_slice(b_ada, (0, chip * ada_cols), (1, ada_cols))
    mod_sh = ada_fwd(c_all, w_ada[0], b_ada_sh)
    mod_g = allgather8(mod_sh, "gather_mod").reshape(N_CHIPS, 2, N_DEV, ada_cols)[:, 0]
    mod = lax.dynamic_slice(mod_g, (0, dev, 0), (N_CHIPS, 1, ada_cols)).reshape(6, D_MODEL)

    (w_in_g4,) = chip_exchange([wts["w_in"][0].astype(BF16)], "gather_w_in", scatter=False)
    wd = split_w_in(_to_full(w_in_g4, "w_in"))
    late_shards = [wts[n][0].astype(BF16) for n in LATE]
    late_gather = exchange_start(late_shards, "gather_late_start", scatter=False, after=w_in_g4)
    mod = mod + late_gather["token"][0, 0]

    def late_weights(after):
        zones = exchange_wait(late_gather, "gather_late_wait", after)
        full = [_to_full(lax.dynamic_update_slice(z, s[None], (chip, 0, 0)), n) for n, z, s in zip(LATE, zones, late_shards)]
        return {n[2:]: f for n, f in zip(LATE, full)}

    sg = first[::2, 8:]
    sparts = [_unpack(sg[j], soffs, sshapes) for j in range(N_CHIPS)]
    sm = {n: wts[n] for n in SMALL if n not in SMALL_SHARDED and n != "b_ada"}
    for i, n in enumerate(SMALL_SHARDED):
        sm[n] = jnp.concatenate([sparts[j][i] for j in range(N_CHIPS)], axis=-1)

    early = {}

    def late_start(g):
        early["swap"] = swap_start([g[n] for n in LATE], "grad_swap_late_start", g[LATE[0]])
        return early["swap"]["token"]

    def late_finish(after):
        mine, theirs = swap_wait(early["swap"], "grad_swap_late_wait", after)
        early["sums"] = [_to_shards(pair_add(a, b, "grad_pair_" + n), n) for n, a, b in zip(LATE, mine, theirs)]
        early["scatter"] = exchange_start(early["sums"], "grad_scatter_start", scatter=True, after=theirs[0])
        return early["scatter"]["token"]

    def w_in_start(g):
        early["swap_in"] = swap_start([g], "grad_swap_w_in_start", g)
        return early["swap_in"]["token"]

    def w_in_finish(after):
        (mine,), (theirs,) = swap_wait(early["swap_in"], "grad_swap_w_in_wait", after)
        early["sum_in"] = _to_shards(pair_add(mine, theirs, "grad_pair_w_in"), "w_in")
        early["scatter_in"] = exchange_start([early["sum_in"]], "grad_scatter_w_in_start", scatter=True, after=theirs)
        return early["scatter_in"]["token"]

    hooks = dict(late_start=late_start, late_finish=late_finish, w_in_start=w_in_start, w_in_finish=w_in_finish)
    loss, dxpre2, dffn, fin, sv = forward_local(x[0], loss_target[0], mod, wd, sm, late_weights)
    grad_x, dmod, gw, gs = backward_local(x[0], mod, sm, dxpre2, dffn, fin, sv, hooks)

    gnames = [n for n in SMALL if n != "b_ada"]
    vec, voffs = _pack([dmod] + [gs[n] for n in gnames] + [loss], 56)
    gathered = allgather8(vec, "gather_small_g").reshape(N_DEV, 56, LANES)
    summed = sum_lead(gathered, "sum_small_g")
    full_shapes = [(6, D_MODEL)] + [gs[n].shape for n in gnames] + [(1, 1)]
    parts = _unpack(summed, voffs, full_shapes)
    grads = {"b_ada": parts[0].reshape(1, -1)}
    for n, p in zip(gnames, parts[1:-1]):
        if n in SMALL_SHARDED:
            p = lax.dynamic_slice_in_dim(p, chip * SMALL_SHARDED[n], SMALL_SHARDED[n], axis=1)
        grads[n] = p.reshape(wts[n].shape)
    loss_total = parts[-1].reshape(())
    dmod_all = gathered[:, 0:6, :].reshape(N_DEV, 6 * D_MODEL)
    grads["w_ada"] = ada_bwd(c_all, lax.dynamic_slice(dmod_all, (0, chip * ada_cols), (N_DEV, ada_cols)))[None]

    def own_slot(zone, sums):
        return lax.dynamic_update_slice(zone, lax.dynamic_slice_in_dim(sums, chip, 1, axis=0), (chip, 0, 0))

    zones = exchange_wait(early["scatter"], "grad_scatter_wait", summed)
    for n, z, s in zip(LATE, zones, early["sums"]):
        grads[n] = sum_lead(own_slot(z, s), "grad_sum_" + n)[None]

    delta, new_m, new_v = {}, {}, {}

    def update(n):
        d, m2, v2 = adamw(wts[n][0], grads[n][0], moms[n][0], vars_[n][0], "adamw_" + n)
        delta[n], new_m[n], new_v[n] = d[None], m2[None], v2[None]

    for n in ["w_ada"] + LATE:
        update(n)
    shapes = [wts[n].shape for n in SMALL]
    packs = [_pack([t[n] for n in SMALL], 32) for t in (wts, grads, moms, vars_)]
    outs = adamw(*[p[0] for p in packs], "adamw_small")
    for res, o in zip((delta, new_m, new_v), outs):
        for n, a in zip(SMALL, _unpack(o, packs[0][1], shapes)):
            res[n] = a
    (zone_in,) = exchange_wait(early["scatter_in"], "grad_scatter_w_in_wait", outs[0])
    grads["w_in"] = sum_lead(own_slot(zone_in, early["sum_in"]), "grad_sum_w_in")[None]
    update("w_in")
    return (loss_total, grad_x[None], *[grads[n] for n in WEIGHTS], *[delta[n] for n in WEIGHTS],
            *[new_m[n] for n in WEIGHTS], *[new_v[n] for n in WEIGHTS])
```
